```python
import math
import jax, jax.numpy as jnp
from jax import lax
import numpy as np

D_MODEL = 1024
BATCH = 32
SEQ = 2048
DEPTH = 1

HEAD_DIM = 64
ATTN_HEADS_PER_GROUP = 8
DILATED_GROUPS = ((128, 1), (512, 4), (2048, 16))
N_GROUPS = 3
ATTN_WIDTH = ATTN_HEADS_PER_GROUP * HEAD_DIM
N_ATTN_HEADS = N_GROUPS * ATTN_HEADS_PER_GROUP
QBLOCK = 128
N_BUCKETS = 32
MAX_DISTANCE = 2048
RWKV_HEADS = 8
RWKV_HEAD_SIZE = 64
RWKV_WIDTH = RWKV_HEADS * RWKV_HEAD_SIZE
DECAY_LORA = 64
AAA_LORA = 64
GN_EPS = 64e-5
RMS_EPS = 1e-6

QKV_COLS = 3 * N_GROUPS * ATTN_WIDTH
OFF_ATTN_GATE = QKV_COLS
OFF_RWKV = OFF_ATTN_GATE + ATTN_WIDTH
RWKV_SHIFT_COLS = 3 * RWKV_WIDTH + DECAY_LORA + AAA_LORA
OFF_RWKV_GATE = OFF_RWKV + RWKV_SHIFT_COLS
OFF_MERGE = OFF_RWKV_GATE + RWKV_WIDTH
IN_COLS = OFF_MERGE + 2 * D_MODEL

kernel_name = "hybrid_dilated_attn_rwkv7_gated_merge"


def rms_norm(x, g):
    xf = x.astype(jnp.float32)
    y = xf * lax.rsqrt(jnp.mean(xf * xf, axis=-1, keepdims=True) + RMS_EPS) * g.astype(jnp.float32)
    return y.astype(x.dtype)


def t5_causal_bucket(dist):
    d = np.maximum(np.asarray(dist), 0)
    max_exact = N_BUCKETS // 2
    ratio = np.log(np.maximum(d, 1).astype(np.float32) / max_exact) / np.float32(math.log(MAX_DISTANCE / max_exact))
    large = max_exact + (ratio * (N_BUCKETS - max_exact)).astype(np.int32)
    large = np.minimum(large, N_BUCKETS - 1)
    return np.where(d < max_exact, d, large).astype(np.int32)


def dilated_group_attention(q, k, v, bias_table_g, window, dilation):
    B, S, H, Dh = q.shape
    L = S // dilation
    steps = window // dilation
    nb = -(-L // QBLOCK)
    Lp = nb * QBLOCK

    def to_blocks(t):
        t = t.reshape(B, L, dilation, H, Dh).transpose(0, 2, 3, 1, 4)
        t = jnp.pad(t, ((0, 0), (0, 0), (0, 0), (0, Lp - L), (0, 0)))
        return t.reshape(B, dilation, H, nb, QBLOCK, Dh)

    def with_prev(t):
        prev = jnp.pad(t, ((0, 0), (0, 0), (0, 0), (1, 0), (0, 0), (0, 0)))[:, :, :, :-1]
        return jnp.concatenate([prev, t], axis=4)

    qb = to_blocks(q)
    kb = with_prev(to_blocks(k))
    vb = with_prev(to_blocks(v))

    qi = np.arange(QBLOCK)[:, None] + QBLOCK
    ki = np.arange(2 * QBLOCK)[None, :]
    rel = qi - ki
    band = (rel >= 0) & (rel <= steps)
    has_prev = (np.arange(nb)[:, None, None] > 0) | (ki >= QBLOCK)[None]
    valid = band[None] & has_prev
    bucket = t5_causal_bucket(np.maximum(rel, 0) * dilation)
    bias = bias_table_g.astype(jnp.float32)[bucket].transpose(2, 0, 1)

    scale = 1.0 / math.sqrt(Dh)
    logits = jnp.einsum('bdhnqc,bdhnkc->bdhnqk', qb, kb) * scale + bias[None, None, :, None]
    logits = jnp.where(valid, logits, -jnp.inf)
    m = jnp.max(logits, axis=-1)
    p = jnp.exp(logits - m[..., None])
    s = jnp.sum(p, axis=-1)
    num = jnp.einsum('bdhnqk,bdhnkc->bdhnqc', p, vb)

    def unblock(t):
        C = t.shape[-1]
        t = t.reshape(B, dilation, H, Lp, C)[:, :, :, :L]
        return t.transpose(0, 3, 1, 2, 4).reshape(B, S, H, C)

    return unblock(num), unblock(s[..., None])[..., 0], unblock(m[..., None])[..., 0]


def rwkv7_scan(r, decay, k, v, a_in, b_in):
    B, S, H, N = r.shape
    xs = tuple(t.transpose(1, 0, 2, 3) for t in (r, decay, k, v, a_in, b_in))

    def step(state, inp):
        r_t, w_t, k_t, v_t, a_t, b_t = inp
        sa = jnp.einsum('bhvk,bhk->bhv', state, a_t)
        state = (state * w_t[:, :, None, :] + sa[..., None] * b_t[:, :, None, :]
                 + v_t[..., None] * k_t[:, :, None, :])
        y = jnp.einsum('bhvk,bhk->bhv', state, r_t)
        return state, y

    s0 = jnp.zeros((B, H, N, N), jnp.float32)
    _, ys = lax.scan(step, s0, xs)
    return ys.transpose(1, 0, 2, 3)


def _fwd_setup_inputs(seed: int = 0) -> dict:
    key = jax.random.key(seed)
    ks = jax.random.split(key, 20)
    f32 = jnp.float32
    nrm = lambda k, shape: jax.random.normal(k, shape, f32)
    L = DEPTH
    return {
        "x": nrm(ks[0], (BATCH, SEQ, D_MODEL)),
        "pre_norm_gain": 1.0 + 0.05 * nrm(ks[1], (L, D_MODEL)),
        "w_in": nrm(ks[2], (L, D_MODEL, IN_COLS)) * D_MODEL ** -0.5,
        "rel_bias": 0.5 * nrm(ks[3], (N_BUCKETS, N_ATTN_HEADS)),
        "rwkv_shift_mix": jax.random.uniform(ks[4], (L, RWKV_SHIFT_COLS), f32),
        "rwkv_w0": jax.random.uniform(ks[5], (L, RWKV_WIDTH), f32, -6.0, 1.0),
        "rwkv_w_up": nrm(ks[6], (L, DECAY_LORA, RWKV_WIDTH)) * 0.5 * DECAY_LORA ** -0.5,
        "rwkv_a0": 0.5 * nrm(ks[7], (L, RWKV_WIDTH)),
        "rwkv_a_up": nrm(ks[8], (L, AAA_LORA, RWKV_WIDTH)) * 0.5 * AAA_LORA ** -0.5,
        "rwkv_k_k": 0.85 + 0.05 * nrm(ks[9], (L, RWKV_WIDTH)),
        "rwkv_k_a": 1.0 + 0.05 * nrm(ks[10], (L, RWKV_WIDTH)),
        "rwkv_r_k": 0.1 * nrm(ks[11], (L, RWKV_HEADS, RWKV_HEAD_SIZE)),
        "rwkv_ln_w": 1.0 + 0.05 * nrm(ks[12], (L, RWKV_WIDTH)),
        "rwkv_ln_b": 0.01 * nrm(ks[13], (L, RWKV_WIDTH)),
        "w_up_attn": nrm(ks[14], (L, ATTN_WIDTH, D_MODEL)) * ATTN_WIDTH ** -0.5,
        "w_up_rwkv": nrm(ks[15], (L, RWKV_WIDTH, D_MODEL)) * RWKV_WIDTH ** -0.5,
        "w_out": nrm(ks[16], (L, D_MODEL, D_MODEL)) * D_MODEL ** -0.5,
        "post_norm_gain": 1.0 + 0.05 * nrm(ks[17], (L, D_MODEL)),
    }


def _fwd_reference(x, pre_norm_gain, w_in, rel_bias, rwkv_shift_mix, rwkv_w0, rwkv_w_up,
              rwkv_a0, rwkv_a_up, rwkv_k_k, rwkv_k_a, rwkv_r_k, rwkv_ln_w, rwkv_ln_b,
              w_up_attn, w_up_rwkv, w_out, post_norm_gain):
    B, S, D = x.shape
    f32 = jnp.float32
    for l in range(DEPTH):
        h = rms_norm(x, pre_norm_gain[l])
        proj = h @ w_in[l]

        qkv = proj[..., :QKV_COLS].astype(f32).reshape(
            B, S, 3, N_GROUPS, ATTN_HEADS_PER_GROUP, HEAD_DIM)
        nums, dens, maxes = [], [], []
        for g, (window, dilation) in enumerate(DILATED_GROUPS):
            table_g = rel_bias[:, g * ATTN_HEADS_PER_GROUP:(g + 1) * ATTN_HEADS_PER_GROUP]
            num, den, mx = dilated_group_attention(
                qkv[:, :, 0, g], qkv[:, :, 1, g], qkv[:, :, 2, g], table_g, window, dilation)
            nums.append(num); dens.append(den); maxes.append(mx)
        m_all = jnp.max(jnp.stack(maxes), axis=0)
        wts = [jnp.exp(mx - m_all) for mx in maxes]
        num_tot = sum(w[..., None] * n for w, n in zip(wts, nums))
        den_tot = sum(w * d for w, d in zip(wts, dens))
        o_attn = (num_tot / den_tot[..., None]).reshape(B, S, ATTN_WIDTH).astype(x.dtype)
        z_attn = proj[..., OFF_ATTN_GATE:OFF_RWKV]
        y_attn = (o_attn * jax.nn.silu(z_attn)) @ w_up_attn[l]

        pr = proj[..., OFF_RWKV:OFF_RWKV_GATE].astype(f32)
        pr_prev = jnp.pad(pr, ((0, 0), (1, 0), (0, 0)))[:, :-1]
        pr = pr + (pr_prev - pr) * rwkv_shift_mix[l].astype(f32)
        r = pr[..., :RWKV_WIDTH]
        k = pr[..., RWKV_WIDTH:2 * RWKV_WIDTH]
        v = pr[..., 2 * RWKV_WIDTH:3 * RWKV_WIDTH]
        w_low = pr[..., 3 * RWKV_WIDTH:3 * RWKV_WIDTH + DECAY_LORA]
        a_low = pr[..., 3 * RWKV_WIDTH + DECAY_LORA:]
        w_log = -jax.nn.softplus(-(rwkv_w0[l].astype(f32) + jnp.tanh(w_low) @ rwkv_w_up[l].astype(f32))) - 0.5
        decay = jnp.exp(-jnp.exp(w_log))
        a = jax.nn.sigmoid(rwkv_a0[l].astype(f32) + a_low @ rwkv_a_up[l].astype(f32))
        hs = (B, S, RWKV_HEADS, RWKV_HEAD_SIZE)
        kk = (k * rwkv_k_k[l].astype(f32)).reshape(hs)
        kk = kk / jnp.maximum(jnp.linalg.norm(kk, axis=-1, keepdims=True), 1e-12)
        k = k * (1.0 + (a - 1.0) * rwkv_k_a[l].astype(f32))
        r4, k4, v4 = r.reshape(hs), k.reshape(hs), v.reshape(hs)
        y = rwkv7_scan(r4, decay.reshape(hs), k4, v4, -kk, kk * a.reshape(hs))
        mu = jnp.mean(y, axis=-1, keepdims=True)
        var = jnp.mean(jnp.square(y - mu), axis=-1, keepdims=True)
        y = ((y - mu) * lax.rsqrt(var + GN_EPS)).reshape(B, S, RWKV_WIDTH)
        y = y * rwkv_ln_w[l].astype(f32) + rwkv_ln_b[l].astype(f32)
        bonus = jnp.sum(r4 * k4 * rwkv_r_k[l].astype(f32), axis=-1, keepdims=True) * v4
        o_rwkv = (y + bonus.reshape(B, S, RWKV_WIDTH)).astype(x.dtype)
        z_rwkv = proj[..., OFF_RWKV_GATE:OFF_MERGE]
        y_rwkv = (o_rwkv * jax.nn.silu(z_rwkv)) @ w_up_rwkv[l]

        g_attn = proj[..., OFF_MERGE:OFF_MERGE + D_MODEL]
        g_rwkv = proj[..., OFF_MERGE + D_MODEL:]
        merged = jax.nn.sigmoid(g_attn) * y_attn + jax.nn.sigmoid(g_rwkv) * y_rwkv
        out = merged @ w_out[l]
        x = x + rms_norm(out, post_norm_gain[l])
    return x


import jax as _jax
import jax.numpy as _jnp

TWIN_FORMAT = 'train_step'
FWD_PARAMS = ['x', 'pre_norm_gain', 'w_in', 'rel_bias', 'rwkv_shift_mix', 'rwkv_w0', 'rwkv_w_up', 'rwkv_a0', 'rwkv_a_up', 'rwkv_k_k', 'rwkv_k_a', 'rwkv_r_k', 'rwkv_ln_w', 'rwkv_ln_b', 'w_up_attn', 'w_up_rwkv', 'w_out', 'post_norm_gain']
TWIN_WEIGHTS = ['pre_norm_gain', 'w_in', 'rel_bias', 'rwkv_shift_mix', 'rwkv_w0', 'rwkv_w_up', 'rwkv_a0', 'rwkv_a_up', 'rwkv_k_k', 'rwkv_k_a', 'rwkv_r_k', 'rwkv_ln_w', 'rwkv_ln_b', 'w_up_attn', 'w_up_rwkv', 'w_out', 'post_norm_gain']
TWIN_DIFF_INPUT = 'x'
TWIN_INPUTS = ['x', 'pre_norm_gain', 'w_in', 'rel_bias', 'rwkv_shift_mix', 'rwkv_w0', 'rwkv_w_up', 'rwkv_a0', 'rwkv_a_up', 'rwkv_k_k', 'rwkv_k_a', 'rwkv_r_k', 'rwkv_ln_w', 'rwkv_ln_b', 'w_up_attn', 'w_up_rwkv', 'w_out', 'post_norm_gain', 'loss_target', 'm_pre_norm_gain', 'm_w_in', 'm_rel_bias', 'm_rwkv_shift_mix', 'm_rwkv_w0', 'm_rwkv_w_up', 'm_rwkv_a0', 'm_rwkv_a_up', 'm_rwkv_k_k', 'm_rwkv_k_a', 'm_rwkv_r_k', 'm_rwkv_ln_w', 'm_rwkv_ln_b', 'm_w_up_attn', 'm_w_up_rwkv', 'm_w_out', 'm_post_norm_gain', 'v_pre_norm_gain', 'v_w_in', 'v_rel_bias', 'v_rwkv_shift_mix', 'v_rwkv_w0', 'v_rwkv_w_up', 'v_rwkv_a0', 'v_rwkv_a_up', 'v_rwkv_k_k', 'v_rwkv_k_a', 'v_rwkv_r_k', 'v_rwkv_ln_w', 'v_rwkv_ln_b', 'v_w_up_attn', 'v_w_up_rwkv', 'v_w_out', 'v_post_norm_gain']
TWIN_OUTPUTS = ['loss', 'grad_x', 'grad_pre_norm_gain', 'grad_w_in', 'grad_rel_bias', 'grad_rwkv_shift_mix', 'grad_rwkv_w0', 'grad_rwkv_w_up', 'grad_rwkv_a0', 'grad_rwkv_a_up', 'grad_rwkv_k_k', 'grad_rwkv_k_a', 'grad_rwkv_r_k', 'grad_rwkv_ln_w', 'grad_rwkv_ln_b', 'grad_w_up_attn', 'grad_w_up_rwkv', 'grad_w_out', 'grad_post_norm_gain', 'delta_pre_norm_gain', 'delta_w_in', 'delta_rel_bias', 'delta_rwkv_shift_mix', 'delta_rwkv_w0', 'delta_rwkv_w_up', 'delta_rwkv_a0', 'delta_rwkv_a_up', 'delta_rwkv_k_k', 'delta_rwkv_k_a', 'delta_rwkv_r_k', 'delta_rwkv_ln_w', 'delta_rwkv_ln_b', 'delta_w_up_attn', 'delta_w_up_rwkv', 'delta_w_out', 'delta_post_norm_gain', 'new_m_pre_norm_gain', 'new_m_w_in', 'new_m_rel_bias', 'new_m_rwkv_shift_mix', 'new_m_rwkv_w0', 'new_m_rwkv_w_up', 'new_m_rwkv_a0', 'new_m_rwkv_a_up', 'new_m_rwkv_k_k', 'new_m_rwkv_k_a', 'new_m_rwkv_r_k', 'new_m_rwkv_ln_w', 'new_m_rwkv_ln_b', 'new_m_w_up_attn', 'new_m_w_up_rwkv', 'new_m_w_out', 'new_m_post_norm_gain', 'new_v_pre_norm_gain', 'new_v_w_in', 'new_v_rel_bias', 'new_v_rwkv_shift_mix', 'new_v_rwkv_w0', 'new_v_rwkv_w_up', 'new_v_rwkv_a0', 'new_v_rwkv_a_up', 'new_v_rwkv_k_k', 'new_v_rwkv_k_a', 'new_v_rwkv_r_k', 'new_v_rwkv_ln_w', 'new_v_rwkv_ln_b', 'new_v_w_up_attn', 'new_v_w_up_rwkv', 'new_v_w_out', 'new_v_post_norm_gain']
TWIN_LEAF_KINDS = {'loss': 'loss', 'grad_x': 'grad_x', 'grad_pre_norm_gain': 'grad_w', 'grad_w_in': 'grad_w', 'grad_rel_bias': 'grad_w', 'grad_rwkv_shift_mix': 'grad_w', 'grad_rwkv_w0': 'grad_w', 'grad_rwkv_w_up': 'grad_w', 'grad_rwkv_a0': 'grad_w', 'grad_rwkv_a_up': 'grad_w', 'grad_rwkv_k_k': 'grad_w', 'grad_rwkv_k_a': 'grad_w', 'grad_rwkv_r_k': 'grad_w', 'grad_rwkv_ln_w': 'grad_w', 'grad_rwkv_ln_b': 'grad_w', 'grad_w_up_attn': 'grad_w', 'grad_w_up_rwkv': 'grad_w', 'grad_w_out': 'grad_w', 'grad_post_norm_gain': 'grad_w', 'delta_pre_norm_gain': 'delta_w', 'delta_w_in': 'delta_w', 'delta_rel_bias': 'delta_w', 'delta_rwkv_shift_mix': 'delta_w', 'delta_rwkv_w0': 'delta_w', 'delta_rwkv_w_up': 'delta_w', 'delta_rwkv_a0': 'delta_w', 'delta_rwkv_a_up': 'delta_w', 'delta_rwkv_k_k': 'delta_w', 'delta_rwkv_k_a': 'delta_w', 'delta_rwkv_r_k': 'delta_w', 'delta_rwkv_ln_w': 'delta_w', 'delta_rwkv_ln_b': 'delta_w', 'delta_w_up_attn': 'delta_w', 'delta_w_up_rwkv': 'delta_w', 'delta_w_out': 'delta_w', 'delta_post_norm_gain': 'delta_w', 'new_m_pre_norm_gain': 'new_m', 'new_m_w_in': 'new_m', 'new_m_rel_bias': 'new_m', 'new_m_rwkv_shift_mix': 'new_m', 'new_m_rwkv_w0': 'new_m', 'new_m_rwkv_w_up': 'new_m', 'new_m_rwkv_a0': 'new_m', 'new_m_rwkv_a_up': 'new_m', 'new_m_rwkv_k_k': 'new_m', 'new_m_rwkv_k_a': 'new_m', 'new_m_rwkv_r_k': 'new_m', 'new_m_rwkv_ln_w': 'new_m', 'new_m_rwkv_ln_b': 'new_m', 'new_m_w_up_attn': 'new_m', 'new_m_w_up_rwkv': 'new_m', 'new_m_w_out': 'new_m', 'new_m_post_norm_gain': 'new_m', 'new_v_pre_norm_gain': 'new_v', 'new_v_w_in': 'new_v', 'new_v_rel_bias': 'new_v', 'new_v_rwkv_shift_mix': 'new_v', 'new_v_rwkv_w0': 'new_v', 'new_v_rwkv_w_up': 'new_v', 'new_v_rwkv_a0': 'new_v', 'new_v_rwkv_a_up': 'new_v', 'new_v_rwkv_k_k': 'new_v', 'new_v_rwkv_k_a': 'new_v', 'new_v_rwkv_r_k': 'new_v', 'new_v_rwkv_ln_w': 'new_v', 'new_v_rwkv_ln_b': 'new_v', 'new_v_w_up_attn': 'new_v', 'new_v_w_up_rwkv': 'new_v', 'new_v_w_out': 'new_v', 'new_v_post_norm_gain': 'new_v'}


def _forward(args):
    return _fwd_reference(*[args[k] for k in FWD_PARAMS])


def _output_shape():
    out = _jax.eval_shape(lambda: _forward(_fwd_setup_inputs(0)))
    return out.shape, out.dtype

N_MICROBATCH = 1
ADAM_LR = 0.001
ADAM_B1 = 0.9
ADAM_B2 = 0.999
ADAM_EPS = 1e-08
ADAM_WD = 0.01
ADAM_STEP = 10
PER_EXAMPLE_BATCH_AXIS = {'x': 0, 'loss_target': 0}
SHARED_INPUTS = []
_WEIGHT_DTYPES = {'pre_norm_gain': _jnp.float32, 'w_in': _jnp.float32, 'rel_bias': _jnp.float32, 'rwkv_shift_mix': _jnp.float32, 'rwkv_w0': _jnp.float32, 'rwkv_w_up': _jnp.float32, 'rwkv_a0': _jnp.float32, 'rwkv_a_up': _jnp.float32, 'rwkv_k_k': _jnp.float32, 'rwkv_k_a': _jnp.float32, 'rwkv_r_k': _jnp.float32, 'rwkv_ln_w': _jnp.float32, 'rwkv_ln_b': _jnp.float32, 'w_up_attn': _jnp.float32, 'w_up_rwkv': _jnp.float32, 'w_out': _jnp.float32, 'post_norm_gain': _jnp.float32}
MOMENT_SCALE = {'pre_norm_gain': 8.353287e-01, 'w_in': 2.531186e-01, 'rel_bias': 6.982752e-02, 'rwkv_shift_mix': 8.931555e-01, 'rwkv_w0': 2.499576e-01, 'rwkv_w_up': 2.468943e-02, 'rwkv_a0': 2.247017e-01, 'rwkv_a_up': 1.932990e-01, 'rwkv_k_k': 1.021803e+00, 'rwkv_k_a': 7.621505e-01, 'rwkv_r_k': 1.268261e+00, 'rwkv_ln_w': 5.904669e-01, 'rwkv_ln_b': 1.023918e+00, 'w_up_attn': 6.655746e-02, 'w_up_rwkv': 3.737809e-01, 'w_out': 3.798071e-01, 'post_norm_gain': 6.429354e+01}


def _to_microbatches(a, axis):
    t = _jnp.moveaxis(a, axis, 0)
    t = t.reshape((N_MICROBATCH, t.shape[0] // N_MICROBATCH) + t.shape[1:])
    return _jnp.moveaxis(t, 1, axis + 1)


def setup_inputs(seed: int = 0) -> dict:
    inp = _fwd_setup_inputs(seed)
    key = _jax.random.fold_in(_jax.random.key(seed), 7919)
    shape, _ = _output_shape()
    out = dict(inp)
    out["loss_target"] = _jax.random.normal(_jax.random.fold_in(key, 0), shape, _jnp.float32)
    for i, name in enumerate(TWIN_WEIGHTS):
        w = inp[name].astype(_jnp.float32)
        if MOMENT_SCALE is None:
            s = _jnp.sqrt(_jnp.mean(_jnp.square(w)) + 1e-30)
        else:
            s = MOMENT_SCALE[name]
        km, kv = _jax.random.split(_jax.random.fold_in(key, i + 1))
        out[name] = w
        out["m_" + name] = s * _jax.random.normal(km, w.shape, _jnp.float32)
        out["v_" + name] = (s * s) * _jax.random.uniform(kv, w.shape, _jnp.float32, 0.5, 1.5)
    if N_MICROBATCH > 1:
        for name, axis in PER_EXAMPLE_BATCH_AXIS.items():
            out[name] = _to_microbatches(out[name], axis)
    return {'x': out['x'], 'pre_norm_gain': out['pre_norm_gain'], 'w_in': out['w_in'], 'rel_bias': out['rel_bias'], 'rwkv_shift_mix': out['rwkv_shift_mix'], 'rwkv_w0': out['rwkv_w0'], 'rwkv_w_up': out['rwkv_w_up'], 'rwkv_a0': out['rwkv_a0'], 'rwkv_a_up': out['rwkv_a_up'], 'rwkv_k_k': out['rwkv_k_k'], 'rwkv_k_a': out['rwkv_k_a'], 'rwkv_r_k': out['rwkv_r_k'], 'rwkv_ln_w': out['rwkv_ln_w'], 'rwkv_ln_b': out['rwkv_ln_b'], 'w_up_attn': out['w_up_attn'], 'w_up_rwkv': out['w_up_rwkv'], 'w_out': out['w_out'], 'post_norm_gain': out['post_norm_gain'], 'loss_target': out['loss_target'], 'm_pre_norm_gain': out['m_pre_norm_gain'], 'm_w_in': out['m_w_in'], 'm_rel_bias': out['m_rel_bias'], 'm_rwkv_shift_mix': out['m_rwkv_shift_mix'], 'm_rwkv_w0': out['m_rwkv_w0'], 'm_rwkv_w_up': out['m_rwkv_w_up'], 'm_rwkv_a0': out['m_rwkv_a0'], 'm_rwkv_a_up': out['m_rwkv_a_up'], 'm_rwkv_k_k': out['m_rwkv_k_k'], 'm_rwkv_k_a': out['m_rwkv_k_a'], 'm_rwkv_r_k': out['m_rwkv_r_k'], 'm_rwkv_ln_w': out['m_rwkv_ln_w'], 'm_rwkv_ln_b': out['m_rwkv_ln_b'], 'm_w_up_attn': out['m_w_up_attn'], 'm_w_up_rwkv': out['m_w_up_rwkv'], 'm_w_out': out['m_w_out'], 'm_post_norm_gain': out['m_post_norm_gain'], 'v_pre_norm_gain': out['v_pre_norm_gain'], 'v_w_in': out['v_w_in'], 'v_rel_bias': out['v_rel_bias'], 'v_rwkv_shift_mix': out['v_rwkv_shift_mix'], 'v_rwkv_w0': out['v_rwkv_w0'], 'v_rwkv_w_up': out['v_rwkv_w_up'], 'v_rwkv_a0': out['v_rwkv_a0'], 'v_rwkv_a_up': out['v_rwkv_a_up'], 'v_rwkv_k_k': out['v_rwkv_k_k'], 'v_rwkv_k_a': out['v_rwkv_k_a'], 'v_rwkv_r_k': out['v_rwkv_r_k'], 'v_rwkv_ln_w': out['v_rwkv_ln_w'], 'v_rwkv_ln_b': out['v_rwkv_ln_b'], 'v_w_up_attn': out['v_w_up_attn'], 'v_w_up_rwkv': out['v_w_up_rwkv'], 'v_w_out': out['v_w_out'], 'v_post_norm_gain': out['v_post_norm_gain']}


def _loss(weights, diff, rest, loss_target):
    with _jax.named_scope("forward"):
        args = {**rest, TWIN_DIFF_INPUT: diff, **{k: w.astype(_WEIGHT_DTYPES[k]) for k, w in weights.items()}}
        y = _forward(args)
    with _jax.named_scope("loss_head"):
        err = _jnp.square(y.astype(_jnp.float32) - loss_target)
        return 0.5 * _jnp.sum(_jnp.mean(err, axis=-1)) if err.ndim else 0.5 * err


def _adamw(w, g, m, v):
    m = ADAM_B1 * m + (1.0 - ADAM_B1) * g
    v = ADAM_B2 * v + (1.0 - ADAM_B2) * _jnp.square(g)
    m_hat = m / (1.0 - ADAM_B1 ** ADAM_STEP)
    v_hat = v / (1.0 - ADAM_B2 ** ADAM_STEP)
    delta = -ADAM_LR * (m_hat / (_jnp.sqrt(v_hat) + ADAM_EPS) + ADAM_WD * w)
    return delta, m, v


def reference(x, pre_norm_gain, w_in, rel_bias, rwkv_shift_mix, rwkv_w0, rwkv_w_up, rwkv_a0, rwkv_a_up, rwkv_k_k, rwkv_k_a, rwkv_r_k, rwkv_ln_w, rwkv_ln_b, w_up_attn, w_up_rwkv, w_out, post_norm_gain, loss_target, m_pre_norm_gain, m_w_in, m_rel_bias, m_rwkv_shift_mix, m_rwkv_w0, m_rwkv_w_up, m_rwkv_a0, m_rwkv_a_up, m_rwkv_k_k, m_rwkv_k_a, m_rwkv_r_k, m_rwkv_ln_w, m_rwkv_ln_b, m_w_up_attn, m_w_up_rwkv, m_w_out, m_post_norm_gain, v_pre_norm_gain, v_w_in, v_rel_bias, v_rwkv_shift_mix, v_rwkv_w0, v_rwkv_w_up, v_rwkv_a0, v_rwkv_a_up, v_rwkv_k_k, v_rwkv_k_a, v_rwkv_r_k, v_rwkv_ln_w, v_rwkv_ln_b, v_w_up_attn, v_w_up_rwkv, v_w_out, v_post_norm_gain):
    given = dict(x=x, pre_norm_gain=pre_norm_gain, w_in=w_in, rel_bias=rel_bias, rwkv_shift_mix=rwkv_shift_mix, rwkv_w0=rwkv_w0, rwkv_w_up=rwkv_w_up, rwkv_a0=rwkv_a0, rwkv_a_up=rwkv_a_up, rwkv_k_k=rwkv_k_k, rwkv_k_a=rwkv_k_a, rwkv_r_k=rwkv_r_k, rwkv_ln_w=rwkv_ln_w, rwkv_ln_b=rwkv_ln_b, w_up_attn=w_up_attn, w_up_rwkv=w_up_rwkv, w_out=w_out, post_norm_gain=post_norm_gain, loss_target=loss_target, m_pre_norm_gain=m_pre_norm_gain, m_w_in=m_w_in, m_rel_bias=m_rel_bias, m_rwkv_shift_mix=m_rwkv_shift_mix, m_rwkv_w0=m_rwkv_w0, m_rwkv_w_up=m_rwkv_w_up, m_rwkv_a0=m_rwkv_a0, m_rwkv_a_up=m_rwkv_a_up, m_rwkv_k_k=m_rwkv_k_k, m_rwkv_k_a=m_rwkv_k_a, m_rwkv_r_k=m_rwkv_r_k, m_rwkv_ln_w=m_rwkv_ln_w, m_rwkv_ln_b=m_rwkv_ln_b, m_w_up_attn=m_w_up_attn, m_w_up_rwkv=m_w_up_rwkv, m_w_out=m_w_out, m_post_norm_gain=m_post_norm_gain, v_pre_norm_gain=v_pre_norm_gain, v_w_in=v_w_in, v_rel_bias=v_rel_bias, v_rwkv_shift_mix=v_rwkv_shift_mix, v_rwkv_w0=v_rwkv_w0, v_rwkv_w_up=v_rwkv_w_up, v_rwkv_a0=v_rwkv_a0, v_rwkv_a_up=v_rwkv_a_up, v_rwkv_k_k=v_rwkv_k_k, v_rwkv_k_a=v_rwkv_k_a, v_rwkv_r_k=v_rwkv_r_k, v_rwkv_ln_w=v_rwkv_ln_w, v_rwkv_ln_b=v_rwkv_ln_b, v_w_up_attn=v_w_up_attn, v_w_up_rwkv=v_w_up_rwkv, v_w_out=v_w_out, v_post_norm_gain=v_post_norm_gain)
    weights = {n: given[n] for n in TWIN_WEIGHTS}
    shared = {n: given[n] for n in SHARED_INPUTS}
    per_example = {n: given[n] for n in ['x']}
    grad_fn = _jax.value_and_grad(_loss, argnums=(0, 1))

    def one_microbatch(ex, loss_target):
        ex = dict(ex)
        diff = ex.pop(TWIN_DIFF_INPUT)
        return grad_fn(weights, diff, {**shared, **ex}, loss_target)

    if N_MICROBATCH == 1:
        loss, (grad_w, grad_x) = one_microbatch(per_example, given["loss_target"])
    else:
        def body(carry, xs):
            loss_sum, grad_sum = carry
            l_k, (gw_k, gx_k) = one_microbatch(xs[0], xs[1])
            with _jax.named_scope("update"):
                return (loss_sum + l_k, _jax.tree.map(_jnp.add, grad_sum, gw_k)), gx_k

        init = (_jnp.zeros((), _jnp.float32), _jax.tree.map(_jnp.zeros_like, weights))
        (loss, grad_w), grad_x = _jax.lax.scan(body, init, (per_example, given["loss_target"]))
    with _jax.named_scope("update"):
        delta_w, new_m, new_v = {}, {}, {}
        for n in TWIN_WEIGHTS:
            delta_w[n], new_m[n], new_v[n] = _adamw(weights[n], grad_w[n], given["m_" + n], given["v_" + n])
    return (loss, grad_x, *[grad_w[n] for n in TWIN_WEIGHTS], *[delta_w[n] for n in TWIN_WEIGHTS],
            *[new_m[n] for n in TWIN_WEIGHTS], *[new_v[n] for n in TWIN_WEIGHTS])
```

```python
import functools
import math

import numpy as np
import jax
import jax.numpy as jnp
from jax import lax
from jax.experimental import pallas as pl
from jax.experimental.pallas import tpu as pltpu

F32, BF16 = jnp.float32, jnp.bfloat16
SDS = jax.ShapeDtypeStruct
HI = lax.Precision.HIGHEST
MESH = pl.DeviceIdType.MESH

N_DEV = 8
D_MODEL = 1024
HEAD = 64
N_HEAD = 8
WIDTH = N_HEAD * HEAD
DILATIONS = (1, 4, 16)
QB = 128
N_BUCKET = 32
MAX_DIST = 2048
LORA = 64
QKV_COLS = 9 * WIDTH
PR_COLS = 3 * WIDTH + 2 * LORA
IN_COLS = QKV_COLS + WIDTH + PR_COLS + WIDTH + 2 * D_MODEL
OFF_ZA, OFF_PR, OFF_ZR, OFF_GM = QKV_COLS, QKV_COLS + WIDTH, QKV_COLS + WIDTH + PR_COLS, QKV_COLS + 2 * WIDTH + PR_COLS
RMS_EPS = 1e-6
GN_EPS = 64e-5
SCALE = 1.0 / math.sqrt(HEAD)
CHUNK = 64
NEG = -1e30
LANE = 128

ADAM_LR, ADAM_B1, ADAM_B2, ADAM_EPS, ADAM_WD, ADAM_STEP = 0.001, 0.9, 0.999, 1e-08, 0.01, 10

VMEM_LIMIT = 56 * 1024 * 1024

SMALL = (("pre_norm_gain", 1024), ("rel_bias", 768), ("rwkv_shift_mix", 1664), ("rwkv_w0", 512), ("rwkv_a0", 512),
         ("rwkv_k_k", 512), ("rwkv_k_a", 512), ("rwkv_r_k", 512), ("rwkv_ln_w", 512), ("rwkv_ln_b", 512),
         ("post_norm_gain", 1024))
SMALL_ROWS = 64
SHARD = (("w_in", D_MODEL * IN_COLS // N_DEV), ("w_up_attn", WIDTH * D_MODEL // N_DEV), ("w_up_rwkv", WIDTH * D_MODEL // N_DEV),
         ("w_out", D_MODEL * D_MODEL // N_DEV), ("rwkv_w_up", LORA * WIDTH // N_DEV), ("rwkv_a_up", LORA * WIDTH // N_DEV))
SHARD_ROWS = sum(n for _, n in SHARD) // LANE
PACK_ROWS = SHARD_ROWS + SMALL_ROWS


def _params(sem=None):
    return pltpu.CompilerParams(dimension_semantics=sem, vmem_limit_bytes=VMEM_LIMIT)


def _dot(a, b):
    return jnp.dot(a, b, preferred_element_type=F32)


def _dot_nt(a, b):
    return lax.dot_general(a, b, (((1,), (1,)), ((), ())), preferred_element_type=F32)


def _dot_tn(a, b):
    return lax.dot_general(a, b, (((0,), (0,)), ((), ())), preferred_element_type=F32)


@jax.custom_vjp
def _bdot(a, b):
    return _dot(a.astype(BF16), b.astype(BF16))


def _bdot_fwd(a, b):
    return _bdot(a, b), (a, b)


def _bdot_bwd(res, g):
    a, b = res
    gb = g.astype(BF16)
    return _dot_nt(gb, b.astype(BF16)), _dot_tn(a.astype(BF16), gb)


_bdot.defvjp(_bdot_fwd, _bdot_bwd)


def _silu(z):
    return z * jax.nn.sigmoid(z)


def _dsilu(z):
    s = jax.nn.sigmoid(z)
    return s * (1.0 + z * (1.0 - s))


def _softplus(x):
    return jnp.maximum(x, 0.0) + jnp.log(1.0 + jnp.exp(-jnp.abs(x)))


def _bucket_tables():
    qi = np.arange(QB)[:, None] + QB
    ki = np.arange(2 * QB)[None, :]
    rel = np.maximum(qi - ki, 0)
    out = []
    for d in DILATIONS:
        dist = rel * d
        max_exact = N_BUCKET // 2
        ratio = np.log(np.maximum(dist, 1).astype(np.float32) / max_exact) / np.float32(math.log(MAX_DIST / max_exact))
        large = max_exact + (ratio * (N_BUCKET - max_exact)).astype(np.int32)
        large = np.minimum(large, N_BUCKET - 1)
        out.append(np.where(dist < max_exact, dist, large).astype(np.int32))
    return np.stack(out)


def _prenorm(x2, g):
    n, d = x2.shape
    tm = 1024

    def body(x_ref, g_ref, h_ref, rs_ref):
        x = x_ref[...]
        rs = lax.rsqrt(jnp.mean(x * x, axis=-1, keepdims=True) + RMS_EPS)
        h_ref[...] = (x * rs * g_ref[...]).astype(BF16)
        rs_ref[...] = rs

    return pl.pallas_call(
        body, name="prenorm", grid=(n // tm,),
        in_specs=[pl.BlockSpec((tm, d), lambda i: (i, 0)), pl.BlockSpec((1, d), lambda i: (0, 0))],
        out_specs=[pl.BlockSpec((tm, d), lambda i: (i, 0)), pl.BlockSpec((tm, 1), lambda i: (i, 0))],
        out_shape=[SDS((n, d), BF16), SDS((n, 1), F32)], compiler_params=_params(("parallel",)))(x2, g)


def _mm(a, b, tn, name):
    m, k = a.shape
    n = b.shape[1]
    tm = 1024

    def body(a_ref, b_ref, o_ref):
        o_ref[...] = _dot(a_ref[...], b_ref[...])

    return pl.pallas_call(
        body, name=name, grid=(n // tn, m // tm),
        in_specs=[pl.BlockSpec((tm, k), lambda j, i: (i, 0)), pl.BlockSpec((k, tn), lambda j, i: (0, j))],
        out_specs=pl.BlockSpec((tm, tn), lambda j, i: (i, j)),
        out_shape=SDS((m, n), F32), compiler_params=_params(("parallel", "parallel")))(a, b)


def _mm_nt_acc(a, b, acc, name):
    m, k = a.shape
    d = b.shape[0]
    tm = 512
    have_acc = acc is not None

    def body(*refs):
        if have_acc:
            a_ref, b_ref, c_ref, o_ref = refs
        else:
            a_ref, b_ref, o_ref = refs
        r = _dot_nt(a_ref[...].astype(BF16), b_ref[...])
        o_ref[...] = r + c_ref[...] if have_acc else r

    in_specs = [pl.BlockSpec((tm, k), lambda i: (i, 0)), pl.BlockSpec((d, k), lambda i: (0, 0))]
    args = [a, b]
    if have_acc:
        in_specs.append(pl.BlockSpec((tm, d), lambda i: (i, 0)))
        args.append(acc)
    return pl.pallas_call(
        body, name=name, grid=(m // tm,), in_specs=in_specs, out_specs=pl.BlockSpec((tm, d), lambda i: (i, 0)),
        out_shape=SDS((m, d), F32), compiler_params=_params(("parallel",)))(*args)


def _mm_tn(a, b, tn, name):
    m, k1 = a.shape
    n2 = b.shape[1]
    tm = 1024

    def body(a_ref, b_ref, o_ref):
        r = _dot_tn(a_ref[...], b_ref[...].astype(BF16))

        @pl.when(pl.program_id(1) == 0)
        def _():
            o_ref[...] = r

        @pl.when(pl.program_id(1) != 0)
        def _():
            o_ref[...] += r

    return pl.pallas_call(
        body, name=name, grid=(n2 // tn, m // tm),
        in_specs=[pl.BlockSpec((tm, k1), lambda j, i: (i, 0)), pl.BlockSpec((tm, tn), lambda j, i: (i, j))],
        out_specs=pl.BlockSpec((k1, tn), lambda j, i: (0, j)),
        out_shape=SDS((k1, n2), F32), compiler_params=_params(("parallel", "arbitrary")))(a, b)


def _ds(start, d):
    return pl.ds(start, QB) if d == 1 else pl.ds(start, QB, stride=d)


def _fill_bias(tab_ref, bidx_ref, bias_sc, hp):
    for g in range(3):
        bi = bidx_ref[g]
        for h in range(2):
            acc = jnp.zeros((QB, 2 * QB), F32)
            for j in range(N_BUCKET):
                acc = jnp.where(bi == j, tab_ref[j, g * N_HEAD + hp * 2 + h], acc)
            bias_sc[g * 2 + h] = acc


def _block_starts(it, d, nb):
    rho = it // nb
    n = it % nb
    st = rho + d * QB * n
    stp = rho + d * QB * jnp.maximum(n - 1, 0)
    return st, stp, n > 0


def _logits(q, kc, kp, bias_sc, gh, cur_ok, prev_ok, hasprev):
    sc = _dot_nt(q, kc) * SCALE + bias_sc[gh, :, QB:2 * QB]
    sp = _dot_nt(q, kp) * SCALE + bias_sc[gh, :, 0:QB]
    sc = jnp.where(cur_ok, sc, NEG)
    sp = jnp.where(jnp.logical_and(prev_ok, hasprev), sp, NEG)
    return sc, sp


def _attn_fwd(qkv3, rel_bias, bidx):
    bsz, s, _ = qkv3.shape
    rt = 256

    def body(tab_ref, bidx_ref, *refs):
        q_refs, k_refs, v_refs = refs[0:3], refs[3:6], refs[6:9]
        o_ref, lse_ref = refs[9:11]
        bias_sc, num_sc, den_sc, m_sc = refs[11:]
        hp = pl.program_id(1)
        _fill_bias(tab_ref, bidx_ref, bias_sc, hp)
        ii = lax.broadcasted_iota(jnp.int32, (QB, QB), 0)
        jj = lax.broadcasted_iota(jnp.int32, (QB, QB), 1)
        cur_ok, prev_ok = ii >= jj, jj >= ii
        for g, d in enumerate(DILATIONS):
            nb = s // (QB * d)

            def blk(it, c, g=g, d=d, nb=nb):
                st, stp, hasprev = _block_starts(it, d, nb)
                qf = q_refs[g][0, _ds(st, d), :]
                kcf, kpf = k_refs[g][0, _ds(st, d), :], k_refs[g][0, _ds(stp, d), :]
                vcf, vpf = v_refs[g][0, _ds(st, d), :], v_refs[g][0, _ds(stp, d), :]
                nums, dens, ms = [], [], []
                for h in range(2):
                    sl = slice(HEAD * h, HEAD * h + HEAD)
                    q = qf[:, sl].astype(BF16)
                    sc, sp = _logits(q, kcf[:, sl].astype(BF16), kpf[:, sl].astype(BF16), bias_sc, g * 2 + h, cur_ok, prev_ok, hasprev)
                    m = jnp.maximum(jnp.max(sc, axis=-1, keepdims=True), jnp.max(sp, axis=-1, keepdims=True))
                    pc, pp = jnp.exp(sc - m), jnp.exp(sp - m)
                    den = jnp.sum(pc, axis=-1, keepdims=True) + jnp.sum(pp, axis=-1, keepdims=True)
                    num = _dot(pc.astype(BF16), vcf[:, sl].astype(BF16)) + _dot(pp.astype(BF16), vpf[:, sl].astype(BF16))
                    nums.append(num)
                    dens.append(jnp.broadcast_to(den, (QB, HEAD)))
                    ms.append(jnp.broadcast_to(m, (QB, HEAD)))
                num_sc[g, _ds(st, d), :] = jnp.concatenate(nums, axis=1)
                den_sc[g, _ds(st, d), :] = jnp.concatenate(dens, axis=1)
                m_sc[g, _ds(st, d), :] = jnp.concatenate(ms, axis=1)
                return c

            lax.fori_loop(0, s // QB, blk, 0)

        def merge(i, c):
            rows = pl.ds(pl.multiple_of(i * rt, rt), rt)
            m0, m1, m2 = m_sc[0, rows, :], m_sc[1, rows, :], m_sc[2, rows, :]
            mall = jnp.maximum(jnp.maximum(m0, m1), m2)
            w0, w1, w2 = jnp.exp(m0 - mall), jnp.exp(m1 - mall), jnp.exp(m2 - mall)
            num = w0 * num_sc[0, rows, :] + w1 * num_sc[1, rows, :] + w2 * num_sc[2, rows, :]
            den = w0 * den_sc[0, rows, :] + w1 * den_sc[1, rows, :] + w2 * den_sc[2, rows, :]
            o_ref[0, rows, :] = num / den
            lse_ref[0, rows, :] = mall + jnp.log(den)
            return c

        lax.fori_loop(0, s // rt, merge, 0)

    col = lambda w, g: (lambda b, hp: (b, 0, (w * 3 + g) * 4 + hp))
    in_specs = [pl.BlockSpec(memory_space=pltpu.SMEM), pl.BlockSpec((3, QB, 2 * QB), lambda b, hp: (0, 0, 0))]
    in_specs += [pl.BlockSpec((1, s, LANE), col(w, g)) for w in range(3) for g in range(3)]
    out_spec = pl.BlockSpec((1, s, LANE), lambda b, hp: (b, 0, hp))
    return pl.pallas_call(
        body, name="attn_fwd", grid=(bsz, 4), in_specs=in_specs, out_specs=[out_spec, out_spec],
        out_shape=[SDS((bsz, s, WIDTH), F32), SDS((bsz, s, WIDTH), F32)],
        scratch_shapes=[pltpu.VMEM((6, QB, 2 * QB), F32), pltpu.VMEM((3, s, LANE), F32), pltpu.VMEM((3, s, LANE), F32),
                        pltpu.VMEM((3, s, LANE), F32)],
        compiler_params=_params(("parallel", "parallel")))(rel_bias, bidx, *([qkv3] * 9))


def _attn_bwd(qkv3, o3, lse3, do3, rel_bias, bidx):
    bsz, s, _ = qkv3.shape
    rt = 256

    def body(tab_ref, bidx_ref, *refs):
        q_refs, k_refs, v_refs = refs[0:3], refs[3:6], refs[6:9]
        o_ref, lse_ref, do_ref = refs[9:12]
        dq_refs, dk_refs, dv_refs = refs[12:15], refs[15:18], refs[18:21]
        db_ref = refs[21]
        bias_sc, delta_sc = refs[22:]
        hp, b = pl.program_id(0), pl.program_id(1)
        _fill_bias(tab_ref, bidx_ref, bias_sc, hp)

        @pl.when(b == 0)
        def _():
            db_ref[...] = jnp.zeros_like(db_ref)

        def prep(i, c):
            rows = pl.ds(pl.multiple_of(i * rt, rt), rt)
            prod = do_ref[0, rows, :] * o_ref[0, rows, :]
            d0 = jnp.sum(prod[:, :HEAD], axis=-1, keepdims=True)
            d1 = jnp.sum(prod[:, HEAD:], axis=-1, keepdims=True)
            delta_sc[rows, :] = jnp.concatenate([jnp.broadcast_to(d0, (rt, HEAD)), jnp.broadcast_to(d1, (rt, HEAD))], axis=1)
            z = jnp.zeros((rt, LANE), F32)
            for g in range(3):
                dk_refs[g][0, rows, :] = z
                dv_refs[g][0, rows, :] = z
            return c

        lax.fori_loop(0, s // rt, prep, 0)
        ii = lax.broadcasted_iota(jnp.int32, (QB, QB), 0)
        jj = lax.broadcasted_iota(jnp.int32, (QB, QB), 1)
        cur_ok, prev_ok = ii >= jj, jj >= ii
        for g, d in enumerate(DILATIONS):
            nb = s // (QB * d)

            def blk(it, c, g=g, d=d, nb=nb):
                st, stp, hasprev = _block_starts(it, d, nb)
                qf = q_refs[g][0, _ds(st, d), :]
                kcf, kpf = k_refs[g][0, _ds(st, d), :], k_refs[g][0, _ds(stp, d), :]
                vcf, vpf = v_refs[g][0, _ds(st, d), :], v_refs[g][0, _ds(stp, d), :]
                dof, lsef, delf = do_ref[0, _ds(st, d), :], lse_ref[0, _ds(st, d), :], delta_sc[_ds(st, d), :]
                dqs, dkcs, dkps, dvcs, dvps = [], [], [], [], []
                for h in range(2):
                    sl = slice(HEAD * h, HEAD * h + HEAD)
                    q = qf[:, sl].astype(BF16)
                    kc, kp = kcf[:, sl].astype(BF16), kpf[:, sl].astype(BF16)
                    vc, vp = vcf[:, sl].astype(BF16), vpf[:, sl].astype(BF16)
                    do = dof[:, sl].astype(BF16)
                    lse = lsef[:, HEAD * h:HEAD * h + 1]
                    delta = delf[:, HEAD * h:HEAD * h + 1]
                    sc, sp = _logits(q, kc, kp, bias_sc, g * 2 + h, cur_ok, prev_ok, hasprev)
                    pc, pp = jnp.exp(sc - lse), jnp.exp(sp - lse)
                    dvcs.append(_dot_tn(pc.astype(BF16), do))
                    dvps.append(_dot_tn(pp.astype(BF16), do))
                    dsc = pc * (_dot_nt(do, vc) - delta)
                    dsp = pp * (_dot_nt(do, vp) - delta)
                    db_ref[0, g * 2 + h, :, QB:2 * QB] += dsc
                    db_ref[0, g * 2 + h, :, 0:QB] += dsp
                    dscb, dspb = dsc.astype(BF16), dsp.astype(BF16)
                    dqs.append((_dot(dscb, kc) + _dot(dspb, kp)) * SCALE)
                    dkcs.append(_dot_tn(dscb, q) * SCALE)
                    dkps.append(_dot_tn(dspb, q) * SCALE)
                dq_refs[g][0, _ds(st, d), :] = jnp.concatenate(dqs, axis=1)
                dk_refs[g][0, _ds(st, d), :] += jnp.concatenate(dkcs, axis=1)
                dv_refs[g][0, _ds(st, d), :] += jnp.concatenate(dvcs, axis=1)
                dk_refs[g][0, _ds(stp, d), :] += jnp.concatenate(dkps, axis=1)
                dv_refs[g][0, _ds(stp, d), :] += jnp.concatenate(dvps, axis=1)
                return c

            lax.fori_loop(0, s // QB, blk, 0)

    col = lambda w, g: (lambda hp, b: (b, 0, (w * 3 + g) * 4 + hp))
    blk_spec = pl.BlockSpec((1, s, LANE), lambda hp, b: (b, 0, hp))
    in_specs = [pl.BlockSpec(memory_space=pltpu.SMEM), pl.BlockSpec((3, QB, 2 * QB), lambda hp, b: (0, 0, 0))]
    in_specs += [pl.BlockSpec((1, s, LANE), col(w, g)) for w in range(3) for g in range(3)]
    in_specs += [blk_spec] * 3
    out_specs = [blk_spec] * 9 + [pl.BlockSpec((1, 6, QB, 2 * QB), lambda hp, b: (hp, 0, 0, 0))]
    out_shape = [SDS((bsz, s, WIDTH), F32)] * 9 + [SDS((4, 6, QB, 2 * QB), F32)]
    outs = pl.pallas_call(
        body, name="attn_bwd", grid=(4, bsz), in_specs=in_specs, out_specs=out_specs, out_shape=out_shape,
        scratch_shapes=[pltpu.VMEM((6, QB, 2 * QB), F32), pltpu.VMEM((s, LANE), F32)],
        compiler_params=_params(("parallel", "arbitrary")))(rel_bias, bidx, *([qkv3] * 9), o3, lse3, do3)
    return outs[:9], outs[9]


def _bias_grad(dbias, bidx):
    def body(db_ref, bidx_ref, o_ref):
        lane = lax.broadcasted_iota(jnp.int32, (1, LANE), 1)
        for g in range(3):
            bi = bidx_ref[g]
            for hp in range(4):
                for h in range(2):
                    mat = db_ref[hp, g * 2 + h]
                    row = jnp.zeros((1, LANE), F32)
                    for j in range(N_BUCKET):
                        part = jnp.sum(jnp.where(bi == j, mat, 0.0), axis=0, keepdims=True)
                        row = jnp.where(lane == j, jnp.sum(part, axis=1, keepdims=True), row)
                    hd = g * N_HEAD + hp * 2 + h
                    o_ref[hd:hd + 1, :] = row

    return pl.pallas_call(body, name="bias_grad", out_shape=SDS((3 * N_HEAD, LANE), F32), compiler_params=_params())(dbias, bidx)


def _pre_fn(r, k0, v, wl, al, w0, wup, a0, aup, kk_, ka_):
    u = w0 + _bdot(jnp.tanh(wl), wup)
    lw = -jnp.exp(-_softplus(-u) - 0.5)
    a = jax.nn.sigmoid(a0 + _bdot(al, aup))
    kkraw = k0 * kk_
    k = k0 * (1.0 + (a - 1.0) * ka_)
    return r, lw, k, v, kkraw, a


PRE_SPLIT = (0, WIDTH, 2 * WIDTH, 3 * WIDTH, 3 * WIDTH + LORA, 3 * WIDTH + 2 * LORA)


def _pre_pieces(prs):
    return [prs[:, a:b] for a, b in zip(PRE_SPLIT[:-1], PRE_SPLIT[1:])]


PRE_TT = 512


def _shifted(pr_ref, edge_ref, first, back):
    pr = pr_ref[0]
    tt = pr.shape[0]
    row = lax.broadcasted_iota(jnp.int32, (tt, 1), 0)
    if back:
        edge = jnp.where(first, 0.0, edge_ref[0, 7:8, :])
        return jnp.where(row == 0, edge, pltpu.roll(pr, 1, axis=0))
    edge = jnp.where(first, 0.0, edge_ref[0, 0:1, :])
    return jnp.where(row == tt - 1, edge, pltpu.roll(pr, tt - 1, axis=0))


def _rwkv_pre(pr3, mix, w0, wup, a0, aup, kk_, ka_):
    bsz, s, _ = pr3.shape
    tt = PRE_TT

    def body(pr_ref, edge_ref, mix_ref, w0_ref, wup_ref, a0_ref, aup_ref, kk_ref, ka_ref, *outs):
        pr = pr_ref[0]
        prev = _shifted(pr_ref, edge_ref, pl.program_id(1) == 0, True)
        prs = pr + (prev - pr) * mix_ref[...]
        vals = _pre_fn(*_pre_pieces(prs), w0_ref[...], wup_ref[...].astype(F32), a0_ref[...], aup_ref[...].astype(F32), kk_ref[...],
                       ka_ref[...])
        for o, val in zip(outs, vals):
            o[0] = val

    vec = lambda n: pl.BlockSpec((1, n), lambda b, i: (0, 0))
    mat = pl.BlockSpec((LORA, WIDTH), lambda b, i: (0, 0))
    in_specs = [pl.BlockSpec((1, tt, PR_COLS), lambda b, i: (b, i, 0)),
                pl.BlockSpec((1, 8, PR_COLS), lambda b, i: (b, jnp.maximum(i * (tt // 8) - 1, 0), 0)),
                vec(PR_COLS), vec(WIDTH), mat, vec(WIDTH), mat, vec(WIDTH), vec(WIDTH)]
    out_spec = pl.BlockSpec((1, tt, WIDTH), lambda b, i: (b, i, 0))
    return pl.pallas_call(
        body, name="rwkv_pre", grid=(bsz, s // tt), in_specs=in_specs, out_specs=[out_spec] * 6,
        out_shape=[SDS((bsz, s, WIDTH), F32)] * 6, compiler_params=_params(("parallel", "parallel")))(
            pr3, pr3, mix, w0, wup, a0, aup, kk_, ka_)


def _rwkv_pre_bwd(pr3, cots, mix, w0, wup, a0, aup, kk_, ka_):
    bsz, s, _ = pr3.shape
    tt = PRE_TT

    def body(pr_ref, edge_ref, c0, c1, c2, c3, c4, c5, mix_ref, w0_ref, wup_ref, a0_ref, aup_ref, kk_ref, ka_ref,
             dprs_ref, dmix_ref, dw0_ref, dwup_ref, da0_ref, daup_ref, dkk_ref, dka_ref):
        pr = pr_ref[0]
        prev = _shifted(pr_ref, edge_ref, pl.program_id(1) == 0, True)
        prs = pr + (prev - pr) * mix_ref[...]
        _, vjp = jax.vjp(_pre_fn, *_pre_pieces(prs), w0_ref[...], wup_ref[...].astype(F32), a0_ref[...], aup_ref[...].astype(F32),
                         kk_ref[...], ka_ref[...])
        grads = vjp(tuple(c[0] for c in (c0, c1, c2, c3, c4, c5)))
        for piece, a, b in zip(grads[:5], PRE_SPLIT[:-1], PRE_SPLIT[1:]):
            dprs_ref[0, :, a:b] = piece
        dw0, dwup, da0, daup, dkk, dka = grads[5:]
        dprs = dprs_ref[0]
        grads = (jnp.sum(dprs * (prev - pr), axis=0, keepdims=True), dw0, dwup, da0, daup, dkk, dka)
        refs = (dmix_ref, dw0_ref, dwup_ref, da0_ref, daup_ref, dkk_ref, dka_ref)
        first = jnp.logical_and(pl.program_id(0) == 0, pl.program_id(1) == 0)

        @pl.when(first)
        def _():
            for r_, g_ in zip(refs, grads):
                r_[...] = g_

        @pl.when(jnp.logical_not(first))
        def _():
            for r_, g_ in zip(refs, grads):
                r_[...] += g_

    vec = lambda n: pl.BlockSpec((1, n), lambda b, i: (0, 0))
    mat = pl.BlockSpec((LORA, WIDTH), lambda b, i: (0, 0))
    tile = pl.BlockSpec((1, tt, WIDTH), lambda b, i: (b, i, 0))
    in_specs = [pl.BlockSpec((1, tt, PR_COLS), lambda b, i: (b, i, 0)),
                pl.BlockSpec((1, 8, PR_COLS), lambda b, i: (b, jnp.maximum(i * (tt // 8) - 1, 0), 0))]
    in_specs += [tile] * 6 + [vec(PR_COLS), vec(WIDTH), mat, vec(WIDTH), mat, vec(WIDTH), vec(WIDTH)]
    out_specs = [pl.BlockSpec((1, tt, PR_COLS), lambda b, i: (b, i, 0)), vec(PR_COLS), vec(WIDTH), mat, vec(WIDTH), mat,
                 vec(WIDTH), vec(WIDTH)]
    out_shape = [SDS((bsz, s, PR_COLS), F32), SDS((1, PR_COLS), F32), SDS((1, WIDTH), F32), SDS((LORA, WIDTH), F32),
                 SDS((1, WIDTH), F32), SDS((LORA, WIDTH), F32), SDS((1, WIDTH), F32), SDS((1, WIDTH), F32)]
    return pl.pallas_call(
        body, name="rwkv_pre_bwd", grid=(bsz, s // tt), in_specs=in_specs, out_specs=out_specs, out_shape=out_shape,
        compiler_params=_params(("arbitrary", "arbitrary")))(pr3, pr3, *cots, mix, w0, wup, a0, aup, kk_, ka_)


def _shift_bwd(dprs3, mix):
    bsz, s, _ = dprs3.shape
    tt = PRE_TT
    nt = s // tt

    def body(d_ref, edge_ref, mix_ref, o_ref):
        nxt = _shifted(d_ref, edge_ref, pl.program_id(1) == nt - 1, False)
        m = mix_ref[...]
        o_ref[0] = d_ref[0] * (1.0 - m) + nxt * m

    in_specs = [pl.BlockSpec((1, tt, PR_COLS), lambda b, i: (b, i, 0)),
                pl.BlockSpec((1, 8, PR_COLS), lambda b, i: (b, jnp.minimum((i + 1) * (tt // 8), s // 8 - 1), 0)),
                pl.BlockSpec((1, PR_COLS), lambda b, i: (0, 0))]
    return pl.pallas_call(
        body, name="shift_bwd", grid=(bsz, nt), in_specs=in_specs, out_specs=pl.BlockSpec((1, tt, PR_COLS), lambda b, i: (b, i, 0)),
        out_shape=SDS((bsz, s, PR_COLS), F32), compiler_params=_params(("parallel", "parallel")))(dprs3, dprs3, mix)


def _hdot(a, b):
    return jnp.dot(a, b, precision=HI, preferred_element_type=F32)


def _hdot_nt(a, b):
    return lax.dot_general(a, b, (((1,), (1,)), ((), ())), precision=HI, preferred_element_type=F32)


def _hdot_tn(a, b):
    return lax.dot_general(a, b, (((0,), (0,)), ((), ())), precision=HI, preferred_element_type=F32)


def _chunk_fn(s0t, r, lw, k, v, kkraw, a, rk, lnw, lnb):
    c = r.shape[0]
    nrm = jnp.sqrt(jnp.sum(kkraw * kkraw, axis=-1, keepdims=True))
    kkn = kkraw / jnp.maximum(nrm, 1e-12)
    ain, bin_ = -kkn, kkn * a
    ii = lax.broadcasted_iota(jnp.int32, (c, c), 0)
    jj = lax.broadcasted_iota(jnp.int32, (c, c), 1)
    strict, incl = ii > jj, ii >= jj
    lg = _hdot(incl.astype(F32), lw)
    g, gp, gi = jnp.exp(lg), jnp.exp(lg - lw), jnp.exp(-lg)
    at, rt, bt, kt = ain * gp, r * g, bin_ * gi, k * gi
    aab = jnp.where(strict, _hdot_nt(at, bt), 0.0)
    aak = jnp.where(strict, _hdot_nt(at, kt), 0.0)
    arb = jnp.where(incl, _hdot_nt(rt, bt), 0.0)
    ark = jnp.where(incl, _hdot_nt(rt, kt), 0.0)
    x = _hdot(at, s0t) + _hdot(aak, v)
    tinv = jnp.where(ii == jj, 1.0, 0.0) + aab
    p = aab
    n = 1
    while 2 * n < c:
        p = _hdot(p, p)
        tinv = tinv + _hdot(p, tinv)
        n *= 2
    u = _hdot(tinv, x)
    y = _hdot(rt, s0t) + _hdot(arb, u) + _hdot(ark, v)
    gc = g[c - 1:c, :]
    sct = jnp.transpose(gc) * s0t + _hdot_tn(bt * gc, u) + _hdot_tn(kt * gc, v)
    mu = jnp.mean(y, axis=-1, keepdims=True)
    var = jnp.mean(jnp.square(y - mu), axis=-1, keepdims=True)
    yn = (y - mu) * lax.rsqrt(var + GN_EPS) * lnw + lnb
    bonus = jnp.sum(r * k * rk, axis=-1, keepdims=True) * v
    return yn + bonus, sct


def _rwkv_scan(ins, rk, lnw, lnb):
    bsz, s, _ = ins[0].shape
    nch = s // CHUNK

    def body(r_ref, lw_ref, k_ref, v_ref, kk_ref, a_ref, rk_ref, lnw_ref, lnb_ref, o_ref, st_ref):
        def step(c, carry):
            rows = pl.ds(pl.multiple_of(c * CHUNK, CHUNK), CHUNK)
            tiles = [ref[0, rows, :] for ref in (r_ref, lw_ref, k_ref, v_ref, kk_ref, a_ref)]
            outs, new = [], []
            for h in range(2):
                sl = slice(HEAD * h, HEAD * h + HEAD)
                st_ref[0, 0, c, h] = carry[h]
                o, sn = _chunk_fn(carry[h], *[t[:, sl] for t in tiles], rk_ref[:, sl], lnw_ref[:, sl], lnb_ref[:, sl])
                outs.append(o)
                new.append(sn)
            o_ref[0, rows, :] = jnp.concatenate(outs, axis=1)
            return tuple(new)

        z = jnp.zeros((HEAD, HEAD), F32)
        lax.fori_loop(0, nch, step, (z, z))

    tile = pl.BlockSpec((1, s, LANE), lambda b, hp: (b, 0, hp))
    vec = pl.BlockSpec((1, LANE), lambda b, hp: (0, hp))
    return pl.pallas_call(
        body, name="rwkv_scan", grid=(bsz, 4), in_specs=[tile] * 6 + [vec] * 3,
        out_specs=[tile, pl.BlockSpec((1, 1, nch, 2, HEAD, HEAD), lambda b, hp: (b, hp, 0, 0, 0, 0))],
        out_shape=[SDS((bsz, s, WIDTH), F32), SDS((bsz, 4, nch, 2, HEAD, HEAD), F32)],
        compiler_params=_params(("parallel", "parallel")))(*ins, rk, lnw, lnb)


def _rwkv_scan_bwd(ins, states, do3, rk, lnw, lnb):
    bsz, s, _ = ins[0].shape
    nch = s // CHUNK

    def body(r_ref, lw_ref, k_ref, v_ref, kk_ref, a_ref, st_ref, do_ref, rk_ref, lnw_ref, lnb_ref,
             dr_ref, dlw_ref, dk_ref, dv_ref, dkk_ref, da_ref, drk_ref, dlnw_ref, dlnb_ref):
        def step(i, carry):
            c = nch - 1 - i
            rows = pl.ds(pl.multiple_of(c * CHUNK, CHUNK), CHUNK)
            tiles = [ref[0, rows, :] for ref in (r_ref, lw_ref, k_ref, v_ref, kk_ref, a_ref)]
            do = do_ref[0, rows, :]
            dst, dp = carry
            cots, new_dst, new_dp = [], [], []
            for h in range(2):
                sl = slice(HEAD * h, HEAD * h + HEAD)
                _, vjp = jax.vjp(_chunk_fn, st_ref[0, 0, c, h], *[t[:, sl] for t in tiles], rk_ref[:, sl], lnw_ref[:, sl],
                                 lnb_ref[:, sl])
                grads = vjp((do[:, sl], dst[h]))
                new_dst.append(grads[0])
                cots.append(grads[1:7])
                new_dp.append(tuple(p_ + g_ for p_, g_ in zip(dp[h], grads[7:10])))
            for j, ref in enumerate((dr_ref, dlw_ref, dk_ref, dv_ref, dkk_ref, da_ref)):
                ref[0, rows, :] = jnp.concatenate([cots[0][j], cots[1][j]], axis=1)
            return tuple(new_dst), tuple(new_dp)

        z = jnp.zeros((HEAD, HEAD), F32)
        zp = (jnp.zeros((1, HEAD), F32),) * 3
        _, dp = lax.fori_loop(0, nch, step, ((z, z), (zp, zp)))
        first = pl.program_id(1) == 0
        for j, ref in enumerate((drk_ref, dlnw_ref, dlnb_ref)):
            val = jnp.concatenate([dp[0][j], dp[1][j]], axis=1)

            @pl.when(first)
            def _(ref=ref, val=val):
                ref[...] = val

            @pl.when(jnp.logical_not(first))
            def _(ref=ref, val=val):
                ref[...] += val

    tile = pl.BlockSpec((1, s, LANE), lambda hp, b: (b, 0, hp))
    vec = pl.BlockSpec((1, LANE), lambda hp, b: (0, hp))
    st_spec = pl.BlockSpec((1, 1, nch, 2, HEAD, HEAD), lambda hp, b: (b, hp, 0, 0, 0, 0))
    outs = pl.pallas_call(
        body, name="rwkv_scan_bwd", grid=(4, bsz), in_specs=[tile] * 6 + [st_spec, tile] + [vec] * 3,
        out_specs=[tile] * 6 + [vec] * 3,
        out_shape=[SDS((bsz, s, WIDTH), F32)] * 6 + [SDS((1, WIDTH), F32)] * 3,
        compiler_params=_params(("parallel", "arbitrary")))(*ins, states, do3, rk, lnw, lnb)
    return outs[:6], outs[6:]


def _head(o_attn, o_rwkv, z_attn, z_rwkv, gm, x2, tgt, wua, wur, wout, g2):
    n = x2.shape[0]
    tm = 256
    nt = n // tm
    d = D_MODEL

    def body(oa_ref, or_ref, za_ref, zr_ref, gm_ref, x_ref, t_ref, wua_ref, wur_ref, wout_ref, g2_ref,
             dxo_ref, doa_ref, dor_ref, dza_ref, dzr_ref, dgm_ref, dwua_ref, dwur_ref, dwout_ref, dg2_ref, loss_ref, lacc):
        i = pl.program_id(0)
        oa, orw, za, zr = oa_ref[...], or_ref[...], za_ref[...], zr_ref[...]
        ga, gb = gm_ref[:, 0:d], gm_ref[:, d:2 * d]
        am = (oa * _silu(za)).astype(BF16)
        bm = (orw * _silu(zr)).astype(BF16)
        ya, yb = _dot(am, wua_ref[...]), _dot(bm, wur_ref[...])
        sa, sb = jax.nn.sigmoid(ga), jax.nn.sigmoid(gb)
        merged = (sa * ya + sb * yb).astype(BF16)
        out = _dot(merged, wout_ref[...])
        rs = lax.rsqrt(jnp.mean(out * out, axis=-1, keepdims=True) + RMS_EPS)
        g2 = g2_ref[...]
        err = x_ref[...] + out * rs * g2 - t_ref[...]
        lpart = jnp.sum(err * err, axis=0, keepdims=True)
        dxo = err * (1.0 / d)
        dxo_ref[...] = dxo
        dg2 = jnp.sum(dxo * out * rs, axis=0, keepdims=True)
        gd = dxo * g2
        dout = (rs * (gd - out * (rs * rs) * jnp.mean(gd * out, axis=-1, keepdims=True))).astype(BF16)
        dmerged = _dot_nt(dout, wout_ref[...])
        dwout = _dot_tn(merged, dout)
        dya, dyb = (dmerged * sa).astype(BF16), (dmerged * sb).astype(BF16)
        dgm_ref[:, 0:d] = dmerged * ya * sa * (1.0 - sa)
        dgm_ref[:, d:2 * d] = dmerged * yb * sb * (1.0 - sb)
        dam, dbm = _dot_nt(dya, wua_ref[...]), _dot_nt(dyb, wur_ref[...])
        dwua, dwur = _dot_tn(am, dya), _dot_tn(bm, dyb)
        doa_ref[...] = dam * _silu(za)
        dza_ref[...] = dam * oa * _dsilu(za)
        dor_ref[...] = dbm * _silu(zr)
        dzr_ref[...] = dbm * orw * _dsilu(zr)

        @pl.when(i == 0)
        def _():
            dwua_ref[...], dwur_ref[...], dwout_ref[...], dg2_ref[...], lacc[...] = dwua, dwur, dwout, dg2, lpart

        @pl.when(i != 0)
        def _():
            dwua_ref[...] += dwua
            dwur_ref[...] += dwur
            dwout_ref[...] += dwout
            dg2_ref[...] += dg2
            lacc[...] += lpart

        @pl.when(i == nt - 1)
        def _():
            loss_ref[...] = jnp.sum(lacc[...], axis=1, keepdims=True) * (0.5 / d)

    t512 = pl.BlockSpec((tm, WIDTH), lambda i: (i, 0))
    t1k = pl.BlockSpec((tm, d), lambda i: (i, 0))
    t2k = pl.BlockSpec((tm, 2 * d), lambda i: (i, 0))
    full = lambda r, c: pl.BlockSpec((r, c), lambda i: (0, 0))
    return pl.pallas_call(
        body, name="head_fwd_bwd", grid=(nt,),
        in_specs=[t512, t512, t512, t512, t2k, t1k, t1k, full(WIDTH, d), full(WIDTH, d), full(d, d), full(1, d)],
        out_specs=[t1k, t512, t512, t512, t512, t2k, full(WIDTH, d), full(WIDTH, d), full(d, d), full(1, d), full(1, 1)],
        out_shape=[SDS((n, d), F32)] + [SDS((n, WIDTH), F32)] * 4 + [SDS((n, 2 * d), F32), SDS((WIDTH, d), F32), SDS((WIDTH, d), F32),
                                                                    SDS((d, d), F32), SDS((1, d), F32), SDS((1, 1), F32)],
        scratch_shapes=[pltpu.VMEM((1, d), F32)],
        compiler_params=_params(("arbitrary",)))(o_attn, o_rwkv, z_attn, z_rwkv, gm, x2, tgt, wua, wur, wout, g2)


def _prenorm_bwd(dh, x2, rs, g1, dxo):
    n, d = x2.shape
    tm = 1024

    def body(dh_ref, x_ref, rs_ref, g_ref, dxo_ref, gx_ref, dg_ref):
        x, r = x_ref[...], rs_ref[...]
        gd = dh_ref[...] * g_ref[...]
        gx_ref[...] = dxo_ref[...] + r * (gd - x * (r * r) * jnp.mean(gd * x, axis=-1, keepdims=True))
        dg = jnp.sum(dh_ref[...] * x * r, axis=0, keepdims=True)

        @pl.when(pl.program_id(0) == 0)
        def _():
            dg_ref[...] = dg

        @pl.when(pl.program_id(0) != 0)
        def _():
            dg_ref[...] += dg

    t = pl.BlockSpec((tm, d), lambda i: (i, 0))
    return pl.pallas_call(
        body, name="prenorm_bwd", grid=(n // tm,),
        in_specs=[t, t, pl.BlockSpec((tm, 1), lambda i: (i, 0)), pl.BlockSpec((1, d), lambda i: (0, 0)), t],
        out_specs=[t, pl.BlockSpec((1, d), lambda i: (0, 0))], out_shape=[SDS((n, d), F32), SDS((1, d), F32)],
        compiler_params=_params(("arbitrary",)))(dh, x2, rs, g1, dxo)


def _mesh_pos():
    x, y, c = lax.axis_index("x"), lax.axis_index("y"), lax.axis_index("c")
    return 4 * x + 2 * y + c


def _coords(idx):
    return (idx // 4, (idx // 2) % 2, idx % 2)


def _exchange(src, name, gather):
    rows = src.shape[-2]

    def body(src_ref, dst_ref, send_sems, recv_sems, local_sem):
        me = _mesh_pos()

        def piece(j):
            return src_ref if gather else src_ref.at[j]

        mine = pltpu.make_async_copy(piece(me), dst_ref.at[me], local_sem)
        mine.start()
        sends = []
        for off in range(1, N_DEV):
            to = (me + off) % N_DEV
            cp = pltpu.make_async_remote_copy(src_ref=piece(to), dst_ref=dst_ref.at[me], send_sem=send_sems.at[off - 1],
                                              recv_sem=recv_sems.at[off - 1], device_id=_coords(to), device_id_type=MESH)
            cp.start()
            sends.append(cp)
        for off in range(1, N_DEV):
            frm = (me + N_DEV - off) % N_DEV
            pltpu.make_async_remote_copy(src_ref=piece(me), dst_ref=dst_ref.at[frm], send_sem=send_sems.at[off - 1],
                                         recv_sem=recv_sems.at[off - 1], device_id=_coords(frm), device_id_type=MESH).wait_recv()
        for cp in sends:
            cp.wait_send()
        mine.wait()

    return pl.pallas_call(
        body, name=name, in_specs=[pl.BlockSpec(memory_space=pltpu.HBM)], out_specs=pl.BlockSpec(memory_space=pltpu.HBM),
        out_shape=SDS((N_DEV, rows, LANE), src.dtype),
        scratch_shapes=[pltpu.SemaphoreType.DMA((N_DEV - 1,)), pltpu.SemaphoreType.DMA((N_DEV - 1,)), pltpu.SemaphoreType.DMA],
        compiler_params=pltpu.CompilerParams())(src)


def _adamw(parts, w, m, v):
    rows = w.shape[0]
    tr = 1152
    c1, c2 = 1.0 - ADAM_B1 ** ADAM_STEP, 1.0 - ADAM_B2 ** ADAM_STEP

    def body(p_ref, w_ref, m_ref, v_ref, g_ref, d_ref, nm_ref, nv_ref):
        g = p_ref[0]
        for j in range(1, N_DEV):
            g = g + p_ref[j]
        nm = ADAM_B1 * m_ref[...] + (1.0 - ADAM_B1) * g
        nv = ADAM_B2 * v_ref[...] + (1.0 - ADAM_B2) * jnp.square(g)
        g_ref[...] = g
        nm_ref[...] = nm
        nv_ref[...] = nv
        d_ref[...] = -ADAM_LR * ((nm / c1) / (jnp.sqrt(nv / c2) + ADAM_EPS) + ADAM_WD * w_ref[...])

    t = pl.BlockSpec((tr, LANE), lambda i: (i, 0))
    return pl.pallas_call(
        body, name="adamw", grid=(rows // tr,), in_specs=[pl.BlockSpec((N_DEV, tr, LANE), lambda i: (0, i, 0)), t, t, t],
        out_specs=[t] * 4, out_shape=[SDS((rows, LANE), F32)] * 4, compiler_params=_params(("parallel",)))(parts, w, m, v)


def _pack_local(shard, small):
    flat = [shard[n].reshape(-1) for n, _ in SHARD] + [small[n].reshape(-1) for n, _ in SMALL]
    flat.append(jnp.zeros((SMALL_ROWS * LANE - sum(n for _, n in SMALL),), flat[0].dtype))
    return jnp.concatenate(flat).reshape(PACK_ROWS, LANE)


def _unpack_local(packed, shapes):
    flat = packed.reshape(-1)
    out, off = {}, 0
    for n, cnt in SHARD + SMALL:
        out[n] = flat[off:off + cnt].reshape(shapes[n])
        off += cnt
    return out


def _split_for_peers(full, small):
    def cols(a, per):
        return a.reshape(a.shape[0], N_DEV, per).transpose(1, 0, 2).reshape(N_DEV, -1)
    pieces = [cols(full["w_in"], IN_COLS // N_DEV), cols(full["w_up_attn"], D_MODEL // N_DEV), cols(full["w_up_rwkv"], D_MODEL // N_DEV),
              full["w_out"].reshape(N_DEV, -1), cols(full["rwkv_w_up"], WIDTH // N_DEV), cols(full["rwkv_a_up"], WIDTH // N_DEV)]
    sm = jnp.concatenate([small[n].reshape(-1) for n, _ in SMALL] + [jnp.zeros((SMALL_ROWS * LANE - sum(n for _, n in SMALL),), F32)])
    pieces.append(jnp.broadcast_to(sm[None], (N_DEV, sm.shape[0])))
    return jnp.concatenate(pieces, axis=1).reshape(N_DEV, PACK_ROWS, LANE)


def _gathered_weights(g):
    flat = g.reshape(N_DEV, -1)
    out, off = {}, 0
    for n, cnt in SHARD:
        out[n] = flat[:, off:off + cnt]
        off += cnt
    def cols(a, r, per):
        return a.reshape(N_DEV, r, per).transpose(1, 0, 2).reshape(r, N_DEV * per)
    return {"w_in": cols(out["w_in"], D_MODEL, IN_COLS // N_DEV), "w_up_attn": cols(out["w_up_attn"], WIDTH, D_MODEL // N_DEV),
            "w_up_rwkv": cols(out["w_up_rwkv"], WIDTH, D_MODEL // N_DEV), "w_out": out["w_out"].reshape(D_MODEL, D_MODEL),
            "rwkv_w_up": cols(out["rwkv_w_up"], LORA, WIDTH // N_DEV), "rwkv_a_up": cols(out["rwkv_a_up"], LORA, WIDTH // N_DEV)}


def _local_step(x, loss_target, sm, wts):
    bsz, s, d = x.shape
    n = bsz * s
    x2, tgt = x.reshape(n, d), loss_target.reshape(n, d)
    bidx = jnp.asarray(_bucket_tables())
    w_in = wts["w_in"]
    segs = (("qkv", 0, QKV_COLS, 512), ("za", OFF_ZA, WIDTH, 512), ("pr", OFF_PR, PR_COLS, PR_COLS), ("zr", OFF_ZR, WIDTH, 512),
            ("gm", OFF_GM, 2 * D_MODEL, 512))

    h, rs = _prenorm(x2, sm["pre_norm_gain"])
    proj = {nm: _mm(h, w_in[:, off:off + cnt], tn, "proj_" + nm) for nm, off, cnt, tn in segs}
    qkv3 = proj["qkv"].reshape(bsz, s, QKV_COLS)
    pr3 = proj["pr"].reshape(bsz, s, PR_COLS)

    o_attn, lse = _attn_fwd(qkv3, sm["rel_bias"], bidx)
    rk = sm["rwkv_r_k"].reshape(1, WIDTH)
    pre_args = (sm["rwkv_shift_mix"], sm["rwkv_w0"], wts["rwkv_w_up"], sm["rwkv_a0"], wts["rwkv_a_up"], sm["rwkv_k_k"], sm["rwkv_k_a"])
    scan_in = _rwkv_pre(pr3, *pre_args)
    o_rwkv, states = _rwkv_scan(scan_in, rk, sm["rwkv_ln_w"], sm["rwkv_ln_b"])

    (dxo, do_attn, do_rwkv, dza, dzr, dgm, g_wua, g_wur, g_wout, g_post, loss) = _head(
        o_attn.reshape(n, WIDTH), o_rwkv.reshape(n, WIDTH), proj["za"], proj["zr"], proj["gm"], x2, tgt,
        wts["w_up_attn"], wts["w_up_rwkv"], wts["w_out"], sm["post_norm_gain"])

    dqkv, dbias = _attn_bwd(qkv3, o_attn, lse, do_attn.reshape(bsz, s, WIDTH), sm["rel_bias"], bidx)
    g_bias = _bias_grad(dbias, bidx)[:, :N_BUCKET].T

    scan_cots, (g_rk, g_lnw, g_lnb) = _rwkv_scan_bwd(scan_in, states, do_rwkv.reshape(bsz, s, WIDTH), rk, sm["rwkv_ln_w"], sm["rwkv_ln_b"])
    dprs, g_mix, g_w0, g_wup, g_a0, g_aup, g_kk, g_ka = _rwkv_pre_bwd(pr3, scan_cots, *pre_args)
    dpr = _shift_bwd(dprs, sm["rwkv_shift_mix"]).reshape(n, PR_COLS)

    dsegs = [(t.reshape(n, WIDTH), j * WIDTH) for j, t in enumerate(dqkv)]
    dsegs += [(dza, OFF_ZA), (dpr, OFF_PR), (dzr, OFF_ZR), (dgm[:, :D_MODEL], OFF_GM), (dgm[:, D_MODEL:], OFF_GM + D_MODEL)]
    dh = None
    g_win = []
    for j, (t, off) in enumerate(dsegs):
        cnt = t.shape[1]
        dh = _mm_nt_acc(t, w_in[:, off:off + cnt], dh, "dh_%d" % j)
        g_win.append(_mm_tn(h, t, min(cnt, 1024) if cnt != PR_COLS else PR_COLS, "gw_in_%d" % j))
    grad_x, g_pre = _prenorm_bwd(dh, x2, rs, sm["pre_norm_gain"], dxo)

    full = {"w_in": jnp.concatenate(g_win, axis=1), "w_up_attn": g_wua, "w_up_rwkv": g_wur, "w_out": g_wout,
            "rwkv_w_up": g_wup, "rwkv_a_up": g_aup}
    small = {"pre_norm_gain": g_pre, "rel_bias": g_bias, "rwkv_shift_mix": g_mix, "rwkv_w0": g_w0, "rwkv_a0": g_a0, "rwkv_k_k": g_kk,
             "rwkv_k_a": g_ka, "rwkv_r_k": g_rk, "rwkv_ln_w": g_lnw, "rwkv_ln_b": g_lnb, "post_norm_gain": g_post}
    return loss[0, 0], grad_x.reshape(bsz, s, d), full, small


def kernel(x, pre_norm_gain, w_in, rel_bias, rwkv_shift_mix, rwkv_w0, rwkv_w_up, rwkv_a0, rwkv_a_up, rwkv_k_k, rwkv_k_a, rwkv_r_k, rwkv_ln_w, rwkv_ln_b, w_up_attn, w_up_rwkv, w_out, post_norm_gain, loss_target, m_pre_norm_gain, m_w_in, m_rel_bias, m_rwkv_shift_mix, m_rwkv_w0, m_rwkv_w_up, m_rwkv_a0, m_rwkv_a_up, m_rwkv_k_k, m_rwkv_k_a, m_rwkv_r_k, m_rwkv_ln_w, m_rwkv_ln_b, m_w_up_attn, m_w_up_rwkv, m_w_out, m_post_norm_gain, v_pre_norm_gain, v_w_in, v_rel_bias, v_rwkv_shift_mix, v_rwkv_w0, v_rwkv_w_up, v_rwkv_a0, v_rwkv_a_up, v_rwkv_k_k, v_rwkv_k_a, v_rwkv_r_k, v_rwkv_ln_w, v_rwkv_ln_b, v_w_up_attn, v_w_up_rwkv, v_w_out, v_post_norm_gain):
    names = [n for n, _ in SHARD + SMALL]
    loc = dict(locals())
    w = {n: loc[n] for n in names}
    m = {n: loc["m_" + n] for n in names}
    v = {n: loc["v_" + n] for n in names}
    shapes = {n: w[n].shape for n in names}
    order = ["pre_norm_gain", "w_in", "rel_bias", "rwkv_shift_mix", "rwkv_w0", "rwkv_w_up", "rwkv_a0", "rwkv_a_up", "rwkv_k_k", "rwkv_k_a",
             "rwkv_r_k", "rwkv_ln_w", "rwkv_ln_b", "w_up_attn", "w_up_rwkv", "w_out", "post_norm_gain"]

    w_pack, m_pack, v_pack = (_pack_local(t, t) for t in (w, m, v))
    gathered = _exchange(w_pack[:SHARD_ROWS].astype(BF16), "gather_weights", True)
    wts = _gathered_weights(gathered)

    loss, grad_x, full, small = _local_step(x, loss_target, w, wts)
    parts = _exchange(_split_for_peers(full, small), "exchange_grads", False)
    g_pack, d_pack, nm_pack, nv_pack = _adamw(parts, w_pack, m_pack, v_pack)

    loss = lax.psum(loss, ("x", "y", "c"))
    outs = [_unpack_local(p, shapes) for p in (g_pack, d_pack, nm_pack, nv_pack)]
    return (loss, grad_x, *[o[n] for o in outs for n in order])
```

```python
import functools
import math

import numpy as np
import jax
import jax.numpy as jnp
from jax import lax
from jax.experimental import pallas as pl
from jax.experimental.pallas import tpu as pltpu

F32, BF16 = jnp.float32, jnp.bfloat16
SDS = jax.ShapeDtypeStruct
HI = lax.Precision.HIGHEST
HI3 = lax.Precision.HIGH
MESH = pl.DeviceIdType.MESH

N_DEV = 8
D_MODEL = 1024
HEAD = 64
N_HEAD = 8
WIDTH = N_HEAD * HEAD
DILATIONS = (1, 4, 16)
QB = 128
N_BUCKET = 32
MAX_DIST = 2048
LORA = 64
QKV_COLS = 9 * WIDTH
PR_COLS = 3 * WIDTH + 2 * LORA
IN_COLS = QKV_COLS + WIDTH + PR_COLS + WIDTH + 2 * D_MODEL
OFF_ZA, OFF_PR, OFF_ZR, OFF_GM = QKV_COLS, QKV_COLS + WIDTH, QKV_COLS + WIDTH + PR_COLS, QKV_COLS + 2 * WIDTH + PR_COLS
RMS_EPS = 1e-6
GN_EPS = 64e-5
SCALE = 1.0 / math.sqrt(HEAD)
CHUNK = 64
CHUNK_GROUP = 2
EARLY = 8
NEG = -1e30
LANE = 128

ADAM_LR, ADAM_B1, ADAM_B2, ADAM_EPS, ADAM_WD, ADAM_STEP = 0.001, 0.9, 0.999, 1e-08, 0.01, 10

VMEM_LIMIT = 56 * 1024 * 1024

SMALL = (("pre_norm_gain", 1024), ("rel_bias", 768), ("rwkv_shift_mix", 1664), ("rwkv_w0", 512), ("rwkv_a0", 512),
         ("rwkv_k_k", 512), ("rwkv_k_a", 512), ("rwkv_r_k", 512), ("rwkv_ln_w", 512), ("rwkv_ln_b", 512),
         ("post_norm_gain", 1024))
SMALL_ROWS = 64


def _params(sem=None):
    return pltpu.CompilerParams(dimension_semantics=sem, vmem_limit_bytes=VMEM_LIMIT)


def _dot(a, b):
    return jnp.dot(a, b, preferred_element_type=F32)


def _dot_nt(a, b):
    return lax.dot_general(a, b, (((1,), (1,)), ((), ())), preferred_element_type=F32)


def _dot_tn(a, b):
    return lax.dot_general(a, b, (((0,), (0,)), ((), ())), preferred_element_type=F32)


@jax.custom_vjp
def _bdot(a, b):
    return _dot(a.astype(BF16), b.astype(BF16))


def _bdot_fwd(a, b):
    return _bdot(a, b), (a, b)


def _bdot_bwd(res, g):
    a, b = res
    gb = g.astype(BF16)
    return _dot_nt(gb, b.astype(BF16)), _dot_tn(a.astype(BF16), gb)


_bdot.defvjp(_bdot_fwd, _bdot_bwd)


def _silu(z):
    return z * jax.nn.sigmoid(z)


def _dsilu(z):
    s = jax.nn.sigmoid(z)
    return s * (1.0 + z * (1.0 - s))


def _softplus(x):
    return jnp.maximum(x, 0.0) + jnp.log(1.0 + jnp.exp(-jnp.abs(x)))


def _bucket_tables():
    qi = np.arange(QB)[:, None] + QB
    ki = np.arange(2 * QB)[None, :]
    rel = np.maximum(qi - ki, 0)
    out = []
    for d in DILATIONS:
        dist = rel * d
        max_exact = N_BUCKET // 2
        ratio = np.log(np.maximum(dist, 1).astype(np.float32) / max_exact) / np.float32(math.log(MAX_DIST / max_exact))
        large = max_exact + (ratio * (N_BUCKET - max_exact)).astype(np.int32)
        large = np.minimum(large, N_BUCKET - 1)
        out.append(np.where(dist < max_exact, dist, large).astype(np.int32))
    return np.stack(out)


def _prenorm(x2, g):
    n, d = x2.shape
    tm = 1024

    def body(x_ref, g_ref, h_ref, rs_ref):
        x = x_ref[...]
        rs = lax.rsqrt(jnp.mean(x * x, axis=-1, keepdims=True) + RMS_EPS)
        h_ref[...] = (x * rs * g_ref[...]).astype(BF16)
        rs_ref[...] = rs

    return pl.pallas_call(
        body, name="prenorm", grid=(n // tm,),
        in_specs=[pl.BlockSpec((tm, d), lambda i: (i, 0)), pl.BlockSpec((1, d), lambda i: (0, 0))],
        out_specs=[pl.BlockSpec((tm, d), lambda i: (i, 0)), pl.BlockSpec((tm, 1), lambda i: (i, 0))],
        out_shape=[SDS((n, d), BF16), SDS((n, 1), F32)], compiler_params=_params(("parallel",)))(x2, g)


def _mm(a, b, tn, name):
    m, k = a.shape
    n = b.shape[1]
    tm = 1024

    def body(a_ref, b_ref, o_ref):
        o_ref[...] = _dot(a_ref[...], b_ref[...])

    return pl.pallas_call(
        body, name=name, grid=(n // tn, m // tm),
        in_specs=[pl.BlockSpec((tm, k), lambda j, i: (i, 0)), pl.BlockSpec((k, tn), lambda j, i: (0, j))],
        out_specs=pl.BlockSpec((tm, tn), lambda j, i: (i, j)),
        out_shape=SDS((m, n), F32), compiler_params=_params(("parallel", "parallel")))(a, b)


def _mm_nt_acc(a, b, acc, name):
    m, k = a.shape
    d = b.shape[0]
    tm = 512
    have_acc = acc is not None

    def body(*refs):
        if have_acc:
            a_ref, b_ref, c_ref, o_ref = refs
        else:
            a_ref, b_ref, o_ref = refs
        r = _dot_nt(a_ref[...].astype(BF16), b_ref[...])
        o_ref[...] = r + c_ref[...] if have_acc else r

    in_specs = [pl.BlockSpec((tm, k), lambda i: (i, 0)), pl.BlockSpec((d, k), lambda i: (0, 0))]
    args = [a, b]
    if have_acc:
        in_specs.append(pl.BlockSpec((tm, d), lambda i: (i, 0)))
        args.append(acc)
    return pl.pallas_call(
        body, name=name, grid=(m // tm,), in_specs=in_specs, out_specs=pl.BlockSpec((tm, d), lambda i: (i, 0)),
        out_shape=SDS((m, d), F32), compiler_params=_params(("parallel",)))(*args)


def _mm_tn(a, b, tn, name):
    m, k1 = a.shape
    n2 = b.shape[1]
    tm = 1024

    def body(a_ref, b_ref, o_ref):
        r = _dot_tn(a_ref[...], b_ref[...].astype(BF16))

        @pl.when(pl.program_id(1) == 0)
        def _():
            o_ref[...] = r

        @pl.when(pl.program_id(1) != 0)
        def _():
            o_ref[...] += r

    return pl.pallas_call(
        body, name=name, grid=(n2 // tn, m // tm),
        in_specs=[pl.BlockSpec((tm, k1), lambda j, i: (i, 0)), pl.BlockSpec((tm, tn), lambda j, i: (i, j))],
        out_specs=pl.BlockSpec((k1, tn), lambda j, i: (0, j)),
        out_shape=SDS((k1, n2), F32), compiler_params=_params(("parallel", "arbitrary")))(a, b)


def _ds(start, d):
    return pl.ds(start, QB) if d == 1 else pl.ds(start, QB, stride=d)


def _fill_bias(tab_ref, bidx_ref, bias_sc, hp):
    for g in range(3):
        bi = bidx_ref[g]
        for h in range(2):
            acc = jnp.zeros((QB, 2 * QB), F32)
            for j in range(N_BUCKET):
                acc = jnp.where(bi == j, tab_ref[j, g * N_HEAD + hp * 2 + h], acc)
            bias_sc[g * 2 + h] = acc


def _block_starts(it, d, nb):
    rho = it // nb
    n = it % nb
    st = rho + d * QB * n
    stp = rho + d * QB * jnp.maximum(n - 1, 0)
    return st, stp, n > 0


def _logits(q, kc, kp, bias_sc, gh, cur_ok, prev_ok, hasprev):
    sc = _dot_nt(q, kc) * SCALE + bias_sc[gh, :, QB:2 * QB]
    sp = _dot_nt(q, kp) * SCALE + bias_sc[gh, :, 0:QB]
    sc = jnp.where(cur_ok, sc, NEG)
    sp = jnp.where(jnp.logical_and(prev_ok, hasprev), sp, NEG)
    return sc, sp


def _attn_fwd(qkv3, rel_bias, bidx):
    bsz, s, _ = qkv3.shape
    rt = 256

    def body(tab_ref, bidx_ref, *refs):
        q_refs, k_refs, v_refs = refs[0:3], refs[3:6], refs[6:9]
        o_ref, lse_ref = refs[9:11]
        bias_sc, num_sc, den_sc, m_sc = refs[11:]
        hp = pl.program_id(1)
        _fill_bias(tab_ref, bidx_ref, bias_sc, hp)
        ii = lax.broadcasted_iota(jnp.int32, (QB, QB), 0)
        jj = lax.broadcasted_iota(jnp.int32, (QB, QB), 1)
        cur_ok, prev_ok = ii >= jj, jj >= ii
        for g, d in enumerate(DILATIONS):
            nb = s // (QB * d)

            def blk(it, c, g=g, d=d, nb=nb):
                st, stp, hasprev = _block_starts(it, d, nb)
                qf = q_refs[g][0, _ds(st, d), :]
                kcf, kpf = k_refs[g][0, _ds(st, d), :], k_refs[g][0, _ds(stp, d), :]
                vcf, vpf = v_refs[g][0, _ds(st, d), :], v_refs[g][0, _ds(stp, d), :]
                nums, dens, ms = [], [], []
                for h in range(2):
                    sl = slice(HEAD * h, HEAD * h + HEAD)
                    q = qf[:, sl].astype(BF16)
                    sc, sp = _logits(q, kcf[:, sl].astype(BF16), kpf[:, sl].astype(BF16), bias_sc, g * 2 + h, cur_ok, prev_ok, hasprev)
                    m = jnp.maximum(jnp.max(sc, axis=-1, keepdims=True), jnp.max(sp, axis=-1, keepdims=True))
                    pc, pp = jnp.exp(sc - m), jnp.exp(sp - m)
                    den = jnp.sum(pc, axis=-1, keepdims=True) + jnp.sum(pp, axis=-1, keepdims=True)
                    num = _dot(pc.astype(BF16), vcf[:, sl].astype(BF16)) + _dot(pp.astype(BF16), vpf[:, sl].astype(BF16))
                    nums.append(num)
                    dens.append(jnp.broadcast_to(den, (QB, HEAD)))
                    ms.append(jnp.broadcast_to(m, (QB, HEAD)))
                num_sc[g, _ds(st, d), :] = jnp.concatenate(nums, axis=1)
                den_sc[g, _ds(st, d), :] = jnp.concatenate(dens, axis=1)
                m_sc[g, _ds(st, d), :] = jnp.concatenate(ms, axis=1)
                return c

            lax.fori_loop(0, s // QB, blk, 0)

        def merge(i, c):
            rows = pl.ds(pl.multiple_of(i * rt, rt), rt)
            m0, m1, m2 = m_sc[0, rows, :], m_sc[1, rows, :], m_sc[2, rows, :]
            mall = jnp.maximum(jnp.maximum(m0, m1), m2)
            w0, w1, w2 = jnp.exp(m0 - mall), jnp.exp(m1 - mall), jnp.exp(m2 - mall)
            num = w0 * num_sc[0, rows, :] + w1 * num_sc[1, rows, :] + w2 * num_sc[2, rows, :]
            den = w0 * den_sc[0, rows, :] + w1 * den_sc[1, rows, :] + w2 * den_sc[2, rows, :]
            o_ref[0, rows, :] = num / den
            lse_ref[0, rows, :] = mall + jnp.log(den)
            return c

        lax.fori_loop(0, s // rt, merge, 0)

    col = lambda w, g: (lambda b, hp: (b, 0, (w * 3 + g) * 4 + hp))
    in_specs = [pl.BlockSpec(memory_space=pltpu.SMEM), pl.BlockSpec((3, QB, 2 * QB), lambda b, hp: (0, 0, 0))]
    in_specs += [pl.BlockSpec((1, s, LANE), col(w, g)) for w in range(3) for g in range(3)]
    out_spec = pl.BlockSpec((1, s, LANE), lambda b, hp: (b, 0, hp))
    return pl.pallas_call(
        body, name="attn_fwd", grid=(bsz, 4), in_specs=in_specs, out_specs=[out_spec, out_spec],
        out_shape=[SDS((bsz, s, WIDTH), F32), SDS((bsz, s, WIDTH), F32)],
        scratch_shapes=[pltpu.VMEM((6, QB, 2 * QB), F32), pltpu.VMEM((3, s, LANE), F32), pltpu.VMEM((3, s, LANE), F32),
                        pltpu.VMEM((3, s, LANE), F32)],
        compiler_params=_params(("parallel", "parallel")))(rel_bias, bidx, *([qkv3] * 9))


def _attn_bwd(qkv3, o3, lse3, do3, rel_bias, bidx):
    bsz, s, _ = qkv3.shape
    rt = 256

    def body(tab_ref, bidx_ref, *refs):
        q_refs, k_refs, v_refs = refs[0:3], refs[3:6], refs[6:9]
        o_ref, lse_ref, do_ref = refs[9:12]
        dq_refs, dk_refs, dv_refs = refs[12:15], refs[15:18], refs[18:21]
        db_ref = refs[21]
        bias_sc, delta_sc = refs[22:]
        hp, b = pl.program_id(0), pl.program_id(1)
        _fill_bias(tab_ref, bidx_ref, bias_sc, hp)

        @pl.when(b == 0)
        def _():
            db_ref[...] = jnp.zeros_like(db_ref)

        def prep(i, c):
            rows = pl.ds(pl.multiple_of(i * rt, rt), rt)
            prod = do_ref[0, rows, :] * o_ref[0, rows, :]
            d0 = jnp.sum(prod[:, :HEAD], axis=-1, keepdims=True)
            d1 = jnp.sum(prod[:, HEAD:], axis=-1, keepdims=True)
            delta_sc[rows, :] = jnp.concatenate([jnp.broadcast_to(d0, (rt, HEAD)), jnp.broadcast_to(d1, (rt, HEAD))], axis=1)
            z = jnp.zeros((rt, LANE), F32)
            for g in range(3):
                dk_refs[g][0, rows, :] = z
                dv_refs[g][0, rows, :] = z
            return c

        lax.fori_loop(0, s // rt, prep, 0)
        ii = lax.broadcasted_iota(jnp.int32, (QB, QB), 0)
        jj = lax.broadcasted_iota(jnp.int32, (QB, QB), 1)
        cur_ok, prev_ok = ii >= jj, jj >= ii
        for g, d in enumerate(DILATIONS):
            nb = s // (QB * d)

            def blk(it, c, g=g, d=d, nb=nb):
                st, stp, hasprev = _block_starts(it, d, nb)
                qf = q_refs[g][0, _ds(st, d), :]
                kcf, kpf = k_refs[g][0, _ds(st, d), :], k_refs[g][0, _ds(stp, d), :]
                vcf, vpf = v_refs[g][0, _ds(st, d), :], v_refs[g][0, _ds(stp, d), :]
                dof, lsef, delf = do_ref[0, _ds(st, d), :], lse_ref[0, _ds(st, d), :], delta_sc[_ds(st, d), :]
                dqs, dkcs, dkps, dvcs, dvps = [], [], [], [], []
                for h in range(2):
                    sl = slice(HEAD * h, HEAD * h + HEAD)
                    q = qf[:, sl].astype(BF16)
                    kc, kp = kcf[:, sl].astype(BF16), kpf[:, sl].astype(BF16)
                    vc, vp = vcf[:, sl].astype(BF16), vpf[:, sl].astype(BF16)
                    do = dof[:, sl].astype(BF16)
                    lse = lsef[:, HEAD * h:HEAD * h + 1]
                    delta = delf[:, HEAD * h:HEAD * h + 1]
                    sc, sp = _logits(q, kc, kp, bias_sc, g * 2 + h, cur_ok, prev_ok, hasprev)
                    pc, pp = jnp.exp(sc - lse), jnp.exp(sp - lse)
                    dvcs.append(_dot_tn(pc.astype(BF16), do))
                    dvps.append(_dot_tn(pp.astype(BF16), do))
                    dsc = pc * (_dot_nt(do, vc) - delta)
                    dsp = pp * (_dot_nt(do, vp) - delta)
                    db_ref[0, g * 2 + h, :, QB:2 * QB] += dsc
                    db_ref[0, g * 2 + h, :, 0:QB] += dsp
                    dscb, dspb = dsc.astype(BF16), dsp.astype(BF16)
                    dqs.append((_dot(dscb, kc) + _dot(dspb, kp)) * SCALE)
                    dkcs.append(_dot_tn(dscb, q) * SCALE)
                    dkps.append(_dot_tn(dspb, q) * SCALE)
                dq_refs[g][0, _ds(st, d), :] = jnp.concatenate(dqs, axis=1)
                dk_refs[g][0, _ds(st, d), :] += jnp.concatenate(dkcs, axis=1)
                dv_refs[g][0, _ds(st, d), :] += jnp.concatenate(dvcs, axis=1)
                dk_refs[g][0, _ds(stp, d), :] += jnp.concatenate(dkps, axis=1)
                dv_refs[g][0, _ds(stp, d), :] += jnp.concatenate(dvps, axis=1)
                return c

            lax.fori_loop(0, s // QB, blk, 0)

    col = lambda w, g: (lambda hp, b: (b, 0, (w * 3 + g) * 4 + hp))
    blk_spec = pl.BlockSpec((1, s, LANE), lambda hp, b: (b, 0, hp))
    in_specs = [pl.BlockSpec(memory_space=pltpu.SMEM), pl.BlockSpec((3, QB, 2 * QB), lambda hp, b: (0, 0, 0))]
    in_specs += [pl.BlockSpec((1, s, LANE), col(w, g)) for w in range(3) for g in range(3)]
    in_specs += [blk_spec] * 3
    out_specs = [blk_spec] * 9 + [pl.BlockSpec((1, 6, QB, 2 * QB), lambda hp, b: (hp, 0, 0, 0))]
    out_shape = [SDS((bsz, s, WIDTH), F32)] * 9 + [SDS((4, 6, QB, 2 * QB), F32)]
    outs = pl.pallas_call(
        body, name="attn_bwd", grid=(4, bsz), in_specs=in_specs, out_specs=out_specs, out_shape=out_shape,
        scratch_shapes=[pltpu.VMEM((6, QB, 2 * QB), F32), pltpu.VMEM((s, LANE), F32)],
        compiler_params=_params(("parallel", "arbitrary")))(rel_bias, bidx, *([qkv3] * 9), o3, lse3, do3)
    return outs[:9], outs[9]


def _bias_grad(dbias, bidx):
    def body(db_ref, bidx_ref, o_ref):
        lane = lax.broadcasted_iota(jnp.int32, (1, LANE), 1)
        for g in range(3):
            bi = bidx_ref[g]
            for hp in range(4):
                for h in range(2):
                    mat = db_ref[hp, g * 2 + h]
                    row = jnp.zeros((1, LANE), F32)
                    for j in range(N_BUCKET):
                        part = jnp.sum(jnp.where(bi == j, mat, 0.0), axis=0, keepdims=True)
                        row = jnp.where(lane == j, jnp.sum(part, axis=1, keepdims=True), row)
                    hd = g * N_HEAD + hp * 2 + h
                    o_ref[hd:hd + 1, :] = row

    return pl.pallas_call(body, name="bias_grad", out_shape=SDS((3 * N_HEAD, LANE), F32), compiler_params=_params())(dbias, bidx)


def _pre_fn(r, k0, v, wl, al, w0, wup, a0, aup, kk_, ka_):
    u = w0 + _bdot(jnp.tanh(wl), wup)
    lw = -jnp.exp(-_softplus(-u) - 0.5)
    a = jax.nn.sigmoid(a0 + _bdot(al, aup))
    kkraw = k0 * kk_
    k = k0 * (1.0 + (a - 1.0) * ka_)
    return r, lw, k, v, kkraw, a


PRE_SPLIT = (0, WIDTH, 2 * WIDTH, 3 * WIDTH, 3 * WIDTH + LORA, 3 * WIDTH + 2 * LORA)


def _pre_pieces(prs):
    return [prs[:, a:b] for a, b in zip(PRE_SPLIT[:-1], PRE_SPLIT[1:])]


PRE_TT = 512


def _shifted(pr_ref, edge_ref, first, back):
    pr = pr_ref[0]
    tt = pr.shape[0]
    row = lax.broadcasted_iota(jnp.int32, (tt, 1), 0)
    if back:
        edge = jnp.where(first, 0.0, edge_ref[0, 7:8, :])
        return jnp.where(row == 0, edge, pltpu.roll(pr, 1, axis=0))
    edge = jnp.where(first, 0.0, edge_ref[0, 0:1, :])
    return jnp.where(row == tt - 1, edge, pltpu.roll(pr, tt - 1, axis=0))


def _rwkv_pre(pr3, mix, w0, wup, a0, aup, kk_, ka_):
    bsz, s, _ = pr3.shape
    tt = PRE_TT

    def body(pr_ref, edge_ref, mix_ref, w0_ref, wup_ref, a0_ref, aup_ref, kk_ref, ka_ref, *outs):
        pr = pr_ref[0]
        prev = _shifted(pr_ref, edge_ref, pl.program_id(1) == 0, True)
        prs = pr + (prev - pr) * mix_ref[...]
        vals = _pre_fn(*_pre_pieces(prs), w0_ref[...], wup_ref[...].astype(F32), a0_ref[...], aup_ref[...].astype(F32), kk_ref[...],
                       ka_ref[...])
        for o, val in zip(outs, vals):
            o[0] = val

    vec = lambda n: pl.BlockSpec((1, n), lambda b, i: (0, 0))
    mat = pl.BlockSpec((LORA, WIDTH), lambda b, i: (0, 0))
    in_specs = [pl.BlockSpec((1, tt, PR_COLS), lambda b, i: (b, i, 0)),
                pl.BlockSpec((1, 8, PR_COLS), lambda b, i: (b, jnp.maximum(i * (tt // 8) - 1, 0), 0)),
                vec(PR_COLS), vec(WIDTH), mat, vec(WIDTH), mat, vec(WIDTH), vec(WIDTH)]
    out_spec = pl.BlockSpec((1, tt, WIDTH), lambda b, i: (b, i, 0))
    return pl.pallas_call(
        body, name="rwkv_pre", grid=(bsz, s // tt), in_specs=in_specs, out_specs=[out_spec] * 6,
        out_shape=[SDS((bsz, s, WIDTH), F32)] * 6, compiler_params=_params(("parallel", "parallel")))(
            pr3, pr3, mix, w0, wup, a0, aup, kk_, ka_)


def _rwkv_pre_bwd(pr3, cots, mix, w0, wup, a0, aup, kk_, ka_):
    bsz, s, _ = pr3.shape
    tt = PRE_TT

    def body(pr_ref, edge_ref, c0, c1, c2, c3, c4, c5, mix_ref, w0_ref, wup_ref, a0_ref, aup_ref, kk_ref, ka_ref,
             dprs_ref, dmix_ref, dw0_ref, dwup_ref, da0_ref, daup_ref, dkk_ref, dka_ref):
        pr = pr_ref[0]
        prev = _shifted(pr_ref, edge_ref, pl.program_id(1) == 0, True)
        prs = pr + (prev - pr) * mix_ref[...]
        _, vjp = jax.vjp(_pre_fn, *_pre_pieces(prs), w0_ref[...], wup_ref[...].astype(F32), a0_ref[...], aup_ref[...].astype(F32),
                         kk_ref[...], ka_ref[...])
        grads = vjp(tuple(c[0] for c in (c0, c1, c2, c3, c4, c5)))
        for piece, a, b in zip(grads[:5], PRE_SPLIT[:-1], PRE_SPLIT[1:]):
            dprs_ref[0, :, a:b] = piece
        dw0, dwup, da0, daup, dkk, dka = grads[5:]
        dprs = dprs_ref[0]
        grads = (jnp.sum(dprs * (prev - pr), axis=0, keepdims=True), dw0, dwup, da0, daup, dkk, dka)
        refs = (dmix_ref, dw0_ref, dwup_ref, da0_ref, daup_ref, dkk_ref, dka_ref)
        first = jnp.logical_and(pl.program_id(0) == 0, pl.program_id(1) == 0)

        @pl.when(first)
        def _():
            for r_, g_ in zip(refs, grads):
                r_[...] = g_

        @pl.when(jnp.logical_not(first))
        def _():
            for r_, g_ in zip(refs, grads):
                r_[...] += g_

    vec = lambda n: pl.BlockSpec((1, n), lambda b, i: (0, 0))
    mat = pl.BlockSpec((LORA, WIDTH), lambda b, i: (0, 0))
    tile = pl.BlockSpec((1, tt, WIDTH), lambda b, i: (b, i, 0))
    in_specs = [pl.BlockSpec((1, tt, PR_COLS), lambda b, i: (b, i, 0)),
                pl.BlockSpec((1, 8, PR_COLS), lambda b, i: (b, jnp.maximum(i * (tt // 8) - 1, 0), 0))]
    in_specs += [tile] * 6 + [vec(PR_COLS), vec(WIDTH), mat, vec(WIDTH), mat, vec(WIDTH), vec(WIDTH)]
    out_specs = [pl.BlockSpec((1, tt, PR_COLS), lambda b, i: (b, i, 0)), vec(PR_COLS), vec(WIDTH), mat, vec(WIDTH), mat,
                 vec(WIDTH), vec(WIDTH)]
    out_shape = [SDS((bsz, s, PR_COLS), F32), SDS((1, PR_COLS), F32), SDS((1, WIDTH), F32), SDS((LORA, WIDTH), F32),
                 SDS((1, WIDTH), F32), SDS((LORA, WIDTH), F32), SDS((1, WIDTH), F32), SDS((1, WIDTH), F32)]
    return pl.pallas_call(
        body, name="rwkv_pre_bwd", grid=(bsz, s // tt), in_specs=in_specs, out_specs=out_specs, out_shape=out_shape,
        compiler_params=_params(("arbitrary", "arbitrary")))(pr3, pr3, *cots, mix, w0, wup, a0, aup, kk_, ka_)


def _shift_bwd(dprs3, mix):
    bsz, s, _ = dprs3.shape
    tt = PRE_TT
    nt = s // tt

    def body(d_ref, edge_ref, mix_ref, o_ref):
        nxt = _shifted(d_ref, edge_ref, pl.program_id(1) == nt - 1, False)
        m = mix_ref[...]
        o_ref[0] = d_ref[0] * (1.0 - m) + nxt * m

    in_specs = [pl.BlockSpec((1, tt, PR_COLS), lambda b, i: (b, i, 0)),
                pl.BlockSpec((1, 8, PR_COLS), lambda b, i: (b, jnp.minimum((i + 1) * (tt // 8), s // 8 - 1), 0)),
                pl.BlockSpec((1, PR_COLS), lambda b, i: (0, 0))]
    return pl.pallas_call(
        body, name="shift_bwd", grid=(bsz, nt), in_specs=in_specs, out_specs=pl.BlockSpec((1, tt, PR_COLS), lambda b, i: (b, i, 0)),
        out_shape=SDS((bsz, s, PR_COLS), F32), compiler_params=_params(("parallel", "parallel")))(dprs3, dprs3, mix)


def _hdot(a, b):
    return jnp.dot(a, b, precision=HI3, preferred_element_type=F32)


def _hdot_nt(a, b):
    return lax.dot_general(a, b, (((1,), (1,)), ((), ())), precision=HI3, preferred_element_type=F32)


def _hdot_tn(a, b):
    return lax.dot_general(a, b, (((0,), (0,)), ((), ())), precision=HI3, preferred_element_type=F32)


def _chunk_fn(s0t, r, lw, k, v, kkraw, a, rk, lnw, lnb, first=False):
    c = r.shape[0]
    at, rt, btc, ktc, gc, aab, arb, xv, arkv, ain, bin_ = _chunk_core(r, lw, k, v, kkraw, a)
    rs = _hdot(jnp.concatenate([at, rt], axis=0), s0t)
    u = _solve(aab, rs[:c] + xv)
    y = rs[c:] + _hdot(arb, u) + arkv
    if first:
        y = jnp.concatenate([_early_rows(r, lw, k, v, ain, bin_), y[EARLY:]], axis=0)
    sct = jnp.transpose(gc) * s0t + _hdot_tn(jnp.concatenate([btc, ktc], axis=0), jnp.concatenate([u, v], axis=0))
    return _post(y, r, k, v, rk, lnw, lnb), sct


def _early_rows(r, lw, k, v, ain, bin_):
    wc, bc, kc = jnp.transpose(jnp.exp(lw)), jnp.transpose(bin_), jnp.transpose(k)
    st = jnp.zeros((HEAD, HEAD), F32)
    rows = []
    for t in range(EARLY):
        sa = _bdot(ain[t:t + 1], st)
        st = st * wc[:, t:t + 1] + bc[:, t:t + 1] * sa + kc[:, t:t + 1] * v[t:t + 1]
        rows.append(_bdot(r[t:t + 1], st))
    return jnp.concatenate(rows, axis=0)


def _chunk_rows(c):
    return pl.ds(c * CHUNK, CHUNK) if isinstance(c, int) else pl.ds(pl.multiple_of(c * CHUNK, CHUNK), CHUNK)


def _masks(c):
    ii = lax.broadcasted_iota(jnp.int32, (c, c), 0)
    jj = lax.broadcasted_iota(jnp.int32, (c, c), 1)
    return ii > jj, ii >= jj, ii == jj


def _chunk_core(r, lw, k, v, kkraw, a):
    c = r.shape[0]
    nrm = jnp.sqrt(jnp.sum(kkraw * kkraw, axis=-1, keepdims=True))
    kkn = kkraw / jnp.maximum(nrm, 1e-12)
    ain, bin_ = -kkn, kkn * a
    strict, incl, _ = _masks(c)
    lg = jnp.dot(incl.astype(F32), lw, precision=HI, preferred_element_type=F32)
    g, gp, gi = jnp.exp(lg), jnp.exp(lg - lw), jnp.exp(-lg)
    at, rt, bt, kt = ain * gp, r * g, bin_ * gi, k * gi
    aa = _hdot_nt(jnp.concatenate([at, rt], axis=0), jnp.concatenate([bt, kt], axis=0))
    aab = jnp.where(strict, aa[:c, :c], 0.0)
    aak = jnp.where(strict, aa[:c, c:], 0.0)
    arb = jnp.where(incl, aa[c:, :c], 0.0)
    ark = jnp.where(incl, aa[c:, c:], 0.0)
    akv = _hdot(jnp.concatenate([aak, ark], axis=0), v)
    gc = g[c - 1:c, :]
    return at, rt, bt * gc, kt * gc, gc, aab, arb, akv[:c], akv[c:], ain, bin_


def _solve(aab, z):
    c = aab.shape[0]
    p = aab
    z = z + _hdot(p, z)
    n = 1
    while 2 * n < c:
        p = _hdot(p, p)
        z = z + _hdot(p, z)
        n *= 2
    return z


def _post(y, r, k, v, rk, lnw, lnb):
    mu = jnp.mean(y, axis=-1, keepdims=True)
    var = jnp.mean(jnp.square(y - mu), axis=-1, keepdims=True)
    yn = (y - mu) * lax.rsqrt(var + GN_EPS) * lnw + lnb
    return yn + jnp.sum(r * k * rk, axis=-1, keepdims=True) * v


def _chunk_consts(r, lw, k, v, kkraw, a, first=False):
    at, rt, btc, ktc, gc, aab, arb, xv, arkv, ain, bin_ = _chunk_core(r, lw, k, v, kkraw, a)
    z = _solve(aab, jnp.concatenate([at, xv], axis=1))
    ryv = jnp.concatenate([rt, arkv], axis=1) + _hdot(arb, z)
    if first:
        early = jnp.concatenate([ryv[:EARLY, :HEAD], _early_rows(r, lw, k, v, ain, bin_)], axis=1)
        ryv = jnp.concatenate([early, ryv[EARLY:]], axis=0)
    eye = _masks(HEAD)[2]
    mkv = _hdot_tn(btc, z) + jnp.concatenate([jnp.where(eye, jnp.transpose(gc), 0.0), _hdot_tn(ktc, v)], axis=1)
    return mkv, ryv


def _rwkv_scan(ins, rk, lnw, lnb):
    bsz, s, _ = ins[0].shape
    nch = s // CHUNK

    def consts_body(r_ref, lw_ref, k_ref, v_ref, kk_ref, a_ref, mkv_ref, ry_ref, yv_ref):
        def group(i, carry):
            for j in range(CHUNK_GROUP):
                c = i * CHUNK_GROUP + j
                first = isinstance(c, int) and c == 0
                rows = _chunk_rows(c)
                tiles = [ref[0, rows, :] for ref in (r_ref, lw_ref, k_ref, v_ref, kk_ref, a_ref)]
                ryv = []
                for h in range(2):
                    sl = slice(HEAD * h, HEAD * h + HEAD)
                    mkv, ryv_h = _chunk_consts(*[t[:, sl] for t in tiles], first=first)
                    mkv_ref[0, 0, c, h] = mkv
                    ryv.append(ryv_h)
                ry_ref[0, rows, :] = jnp.concatenate([ryv[0][:, :HEAD], ryv[1][:, :HEAD]], axis=1)
                yv_ref[0, rows, :] = jnp.concatenate([ryv[0][:, HEAD:], ryv[1][:, HEAD:]], axis=1)
            return carry

        group(0, 0)
        lax.fori_loop(1, nch // CHUNK_GROUP, group, 0)

    tile = pl.BlockSpec((1, s, LANE), lambda b, hp: (b, 0, hp))
    vec = pl.BlockSpec((1, LANE), lambda b, hp: (0, hp))
    mkv_spec = pl.BlockSpec((1, 1, nch, 2, HEAD, LANE), lambda b, hp: (b, hp, 0, 0, 0, 0))
    st_spec = pl.BlockSpec((1, 1, nch, 2, HEAD, HEAD), lambda b, hp: (b, hp, 0, 0, 0, 0))
    mkv, ry, yv = pl.pallas_call(
        consts_body, name="rwkv_consts", grid=(bsz, 4), in_specs=[tile] * 6, out_specs=[mkv_spec, tile, tile],
        out_shape=[SDS((bsz, 4, nch, 2, HEAD, LANE), F32), SDS((bsz, s, WIDTH), F32), SDS((bsz, s, WIDTH), F32)],
        compiler_params=_params(("parallel", "parallel")))(*ins)

    def scan_body(mkv_ref, ry_ref, yv_ref, r_ref, k_ref, v_ref, rk_ref, lnw_ref, lnb_ref, o_ref, st_ref, state):
        state[...] = jnp.zeros_like(state)

        def step(c, carry):
            rows = pl.ds(pl.multiple_of(c * CHUNK, CHUNK), CHUNK)
            ry, yv, r, k, v = (ref[0, rows, :] for ref in (ry_ref, yv_ref, r_ref, k_ref, v_ref))
            outs = []
            for h in range(2):
                sl = slice(HEAD * h, HEAD * h + HEAD)
                s0t = state[h]
                st_ref[0, 0, c, h] = s0t
                mkv_h = mkv_ref[0, 0, c, h]
                state[h] = _hdot(mkv_h[:, :HEAD], s0t) + mkv_h[:, HEAD:]
                y = _hdot(ry[:, sl], s0t) + yv[:, sl]
                outs.append(_post(y, r[:, sl], k[:, sl], v[:, sl], rk_ref[:, sl], lnw_ref[:, sl], lnb_ref[:, sl]))
            o_ref[0, rows, :] = jnp.concatenate(outs, axis=1)
            return carry

        lax.fori_loop(0, nch, step, 0)

    o, states = pl.pallas_call(
        scan_body, name="rwkv_scan", grid=(bsz, 4), in_specs=[mkv_spec] + [tile] * 5 + [vec] * 3, out_specs=[tile, st_spec],
        out_shape=[SDS((bsz, s, WIDTH), F32), SDS((bsz, 4, nch, 2, HEAD, HEAD), F32)],
        scratch_shapes=[pltpu.VMEM((2, HEAD, HEAD), F32)],
        compiler_params=_params(("parallel", "parallel")))(mkv, ry, yv, ins[0], ins[2], ins[3], rk, lnw, lnb)
    return o, states, (mkv, ry, yv)


def _rwkv_scan_bwd(ins, states, consts, do3, rk, lnw, lnb):
    bsz, s, _ = ins[0].shape
    nch = s // CHUNK

    mkv, ry, yv = consts
    tile_f = pl.BlockSpec((1, s, LANE), lambda b, hp: (b, 0, hp))
    vec_f = pl.BlockSpec((1, LANE), lambda b, hp: (0, hp))
    mkv_spec = pl.BlockSpec((1, 1, nch, 2, HEAD, LANE), lambda b, hp: (b, hp, 0, 0, 0, 0))
    st_spec_f = pl.BlockSpec((1, 1, nch, 2, HEAD, HEAD), lambda b, hp: (b, hp, 0, 0, 0, 0))

    def dstate_body(mkv_ref, ry_ref, yv_ref, r_ref, k_ref, v_ref, st_ref, do_ref, rk_ref, lnw_ref, lnb_ref, dst_ref, dstate):
        dstate[...] = jnp.zeros_like(dstate)

        def step(i, carry):
            c = nch - 1 - i
            rows = pl.ds(pl.multiple_of(c * CHUNK, CHUNK), CHUNK)
            ry_, yv_, r, k, v, do = (ref[0, rows, :] for ref in (ry_ref, yv_ref, r_ref, k_ref, v_ref, do_ref))
            for h in range(2):
                sl = slice(HEAD * h, HEAD * h + HEAD)
                dsct = dstate[h]
                dst_ref[0, 0, c, h] = dsct
                y = _hdot(ry_[:, sl], st_ref[0, 0, c, h]) + yv_[:, sl]
                _, vjp = jax.vjp(lambda y_: _post(y_, r[:, sl], k[:, sl], v[:, sl], rk_ref[:, sl], lnw_ref[:, sl], lnb_ref[:, sl]), y)
                (dy,) = vjp(do[:, sl])
                dstate[h] = _hdot_tn(mkv_ref[0, 0, c, h][:, :HEAD], dsct) + _hdot_tn(ry_[:, sl], dy)
            return carry

        lax.fori_loop(0, nch, step, 0)

    dstates = pl.pallas_call(
        dstate_body, name="rwkv_dstate", grid=(bsz, 4), in_specs=[mkv_spec] + [tile_f] * 5 + [st_spec_f, tile_f] + [vec_f] * 3,
        out_specs=st_spec_f, out_shape=SDS((bsz, 4, nch, 2, HEAD, HEAD), F32), scratch_shapes=[pltpu.VMEM((2, HEAD, HEAD), F32)],
        compiler_params=_params(("parallel", "parallel")))(mkv, ry, yv, ins[0], ins[2], ins[3], states, do3, rk, lnw, lnb)

    def body(r_ref, lw_ref, k_ref, v_ref, kk_ref, a_ref, st_ref, dst_ref, do_ref, rk_ref, lnw_ref, lnb_ref,
             dr_ref, dlw_ref, dk_ref, dv_ref, dkk_ref, da_ref, drk_ref, dlnw_ref, dlnb_ref):
        def group(i, dp):
            new_dp = list(dp)
            for j in range(CHUNK_GROUP):
                c = i * CHUNK_GROUP + j
                first = isinstance(c, int) and c == 0
                rows = _chunk_rows(c)
                tiles = [ref[0, rows, :] for ref in (r_ref, lw_ref, k_ref, v_ref, kk_ref, a_ref)]
                do = do_ref[0, rows, :]
                cots = []
                for h in range(2):
                    sl = slice(HEAD * h, HEAD * h + HEAD)
                    _, vjp = jax.vjp(functools.partial(_chunk_fn, first=first), st_ref[0, 0, c, h], *[t[:, sl] for t in tiles],
                                     rk_ref[:, sl], lnw_ref[:, sl], lnb_ref[:, sl])
                    grads = vjp((do[:, sl], dst_ref[0, 0, c, h]))
                    cots.append(grads[1:7])
                    new_dp[h] = tuple(p_ + g_ for p_, g_ in zip(new_dp[h], grads[7:10]))
                for j_, ref in enumerate((dr_ref, dlw_ref, dk_ref, dv_ref, dkk_ref, da_ref)):
                    ref[0, rows, :] = jnp.concatenate([cots[0][j_], cots[1][j_]], axis=1)
            return tuple(new_dp)

        zp = (jnp.zeros((1, HEAD), F32),) * 3
        dp = lax.fori_loop(1, nch // CHUNK_GROUP, group, group(0, (zp, zp)))
        first = pl.program_id(1) == 0
        for j, ref in enumerate((drk_ref, dlnw_ref, dlnb_ref)):
            val = jnp.concatenate([dp[0][j], dp[1][j]], axis=1)

            @pl.when(first)
            def _(ref=ref, val=val):
                ref[...] = val

            @pl.when(jnp.logical_not(first))
            def _(ref=ref, val=val):
                ref[...] += val

    tile = pl.BlockSpec((1, s, LANE), lambda hp, b: (b, 0, hp))
    vec = pl.BlockSpec((1, LANE), lambda hp, b: (0, hp))
    st_spec = pl.BlockSpec((1, 1, nch, 2, HEAD, HEAD), lambda hp, b: (b, hp, 0, 0, 0, 0))
    outs = pl.pallas_call(
        body, name="rwkv_scan_bwd", grid=(4, bsz), in_specs=[tile] * 6 + [st_spec, st_spec, tile] + [vec] * 3,
        out_specs=[tile] * 6 + [vec] * 3,
        out_shape=[SDS((bsz, s, WIDTH), F32)] * 6 + [SDS((1, WIDTH), F32)] * 3,
        compiler_params=_params(("parallel", "arbitrary")))(*ins, states, dstates, do3, rk, lnw, lnb)
    return outs[:6], outs[6:]


def _head(o_attn, o_rwkv, z_attn, z_rwkv, gm, x2, tgt, wua, wur, wout, g2):
    n = x2.shape[0]
    tm = 256
    nt = n // tm
    d = D_MODEL

    def body(oa_ref, or_ref, za_ref, zr_ref, gm_ref, x_ref, t_ref, wua_ref, wur_ref, wout_ref, g2_ref,
             dxo_ref, doa_ref, dor_ref, dza_ref, dzr_ref, dgm_ref, dwua_ref, dwur_ref, dwout_ref, dg2_ref, loss_ref, lacc):
        i = pl.program_id(0)
        oa, orw, za, zr = oa_ref[...], or_ref[...], za_ref[...], zr_ref[...]
        ga, gb = gm_ref[:, 0:d], gm_ref[:, d:2 * d]
        am = (oa * _silu(za)).astype(BF16)
        bm = (orw * _silu(zr)).astype(BF16)
        ya, yb = _dot(am, wua_ref[...]), _dot(bm, wur_ref[...])
        sa, sb = jax.nn.sigmoid(ga), jax.nn.sigmoid(gb)
        merged = (sa * ya + sb * yb).astype(BF16)
        out = _dot(merged, wout_ref[...])
        rs = lax.rsqrt(jnp.mean(out * out, axis=-1, keepdims=True) + RMS_EPS)
        g2 = g2_ref[...]
        err = x_ref[...] + out * rs * g2 - t_ref[...]
        lpart = jnp.sum(err * err, axis=0, keepdims=True)
        dxo = err * (1.0 / d)
        dxo_ref[...] = dxo
        dg2 = jnp.sum(dxo * out * rs, axis=0, keepdims=True)
        gd = dxo * g2
        dout = (rs * (gd - out * (rs * rs) * jnp.mean(gd * out, axis=-1, keepdims=True))).astype(BF16)
        dmerged = _dot_nt(dout, wout_ref[...])
        dwout = _dot_tn(merged, dout)
        dya, dyb = (dmerged * sa).astype(BF16), (dmerged * sb).astype(BF16)
        dgm_ref[:, 0:d] = dmerged * ya * sa * (1.0 - sa)
        dgm_ref[:, d:2 * d] = dmerged * yb * sb * (1.0 - sb)
        dam, dbm = _dot_nt(dya, wua_ref[...]), _dot_nt(dyb, wur_ref[...])
        dwua, dwur = _dot_tn(am, dya), _dot_tn(bm, dyb)
        doa_ref[...] = dam * _silu(za)
        dza_ref[...] = dam * oa * _dsilu(za)
        dor_ref[...] = dbm * _silu(zr)
        dzr_ref[...] = dbm * orw * _dsilu(zr)

        @pl.when(i == 0)
        def _():
            dwua_ref[...], dwur_ref[...], dwout_ref[...], dg2_ref[...], lacc[...] = dwua, dwur, dwout, dg2, lpart

        @pl.when(i != 0)
        def _():
            dwua_ref[...] += dwua
            dwur_ref[...] += dwur
            dwout_ref[...] += dwout
            dg2_ref[...] += dg2
            lacc[...] += lpart

        @pl.when(i == nt - 1)
        def _():
            loss_ref[...] = jnp.sum(lacc[...], axis=1, keepdims=True) * (0.5 / d)

    t512 = pl.BlockSpec((tm, WIDTH), lambda i: (i, 0))
    t1k = pl.BlockSpec((tm, d), lambda i: (i, 0))
    t2k = pl.BlockSpec((tm, 2 * d), lambda i: (i, 0))
    full = lambda r, c: pl.BlockSpec((r, c), lambda i: (0, 0))
    return pl.pallas_call(
        body, name="head_fwd_bwd", grid=(nt,),
        in_specs=[t512, t512, t512, t512, t2k, t1k, t1k, full(WIDTH, d), full(WIDTH, d), full(d, d), full(1, d)],
        out_specs=[t1k, t512, t512, t512, t512, t2k, full(WIDTH, d), full(WIDTH, d), full(d, d), full(1, d), full(1, 1)],
        out_shape=[SDS((n, d), F32)] + [SDS((n, WIDTH), F32)] * 4 + [SDS((n, 2 * d), F32), SDS((WIDTH, d), F32), SDS((WIDTH, d), F32),
                                                                    SDS((d, d), F32), SDS((1, d), F32), SDS((1, 1), F32)],
        scratch_shapes=[pltpu.VMEM((1, d), F32)],
        compiler_params=_params(("arbitrary",)))(o_attn, o_rwkv, z_attn, z_rwkv, gm, x2, tgt, wua, wur, wout, g2)


def _prenorm_bwd(dh, x2, rs, g1, dxo):
    n, d = x2.shape
    tm = 1024

    def body(dh_ref, x_ref, rs_ref, g_ref, dxo_ref, gx_ref, dg_ref):
        x, r = x_ref[...], rs_ref[...]
        gd = dh_ref[...] * g_ref[...]
        gx_ref[...] = dxo_ref[...] + r * (gd - x * (r * r) * jnp.mean(gd * x, axis=-1, keepdims=True))
        dg = jnp.sum(dh_ref[...] * x * r, axis=0, keepdims=True)

        @pl.when(pl.program_id(0) == 0)
        def _():
            dg_ref[...] = dg

        @pl.when(pl.program_id(0) != 0)
        def _():
            dg_ref[...] += dg

    t = pl.BlockSpec((tm, d), lambda i: (i, 0))
    return pl.pallas_call(
        body, name="prenorm_bwd", grid=(n // tm,),
        in_specs=[t, t, pl.BlockSpec((tm, 1), lambda i: (i, 0)), pl.BlockSpec((1, d), lambda i: (0, 0)), t],
        out_specs=[t, pl.BlockSpec((1, d), lambda i: (0, 0))], out_shape=[SDS((n, d), F32), SDS((1, d), F32)],
        compiler_params=_params(("arbitrary",)))(dh, x2, rs, g1, dxo)


def _mesh_pos():
    x, y, c = lax.axis_index("x"), lax.axis_index("y"), lax.axis_index("c")
    return 4 * x + 2 * y + c


def _coords(idx):
    return (idx // 4, (idx // 2) % 2, idx % 2)


def _exchange(srcs, to_all, name):
    n = len(srcs)

    def body(*refs):
        src_refs, dst_refs = refs[:n], refs[n:2 * n]
        send_sems, recv_sems, local_sems = refs[2 * n:]
        me = _mesh_pos()

        def piece(i, j):
            return src_refs[i] if to_all[i] else src_refs[i].at[j]

        def remote(i, off, peer, block, slot):
            return pltpu.make_async_remote_copy(src_ref=piece(i, block), dst_ref=dst_refs[i].at[slot],
                                                send_sem=send_sems.at[i, off - 1], recv_sem=recv_sems.at[i, off - 1],
                                                device_id=_coords(peer), device_id_type=MESH)

        local = [pltpu.make_async_copy(piece(i, me), dst_refs[i].at[me], local_sems.at[i]) for i in range(n)]
        for cp in local:
            cp.start()
        sends = []
        for off in range(1, N_DEV):
            to = (me + off) % N_DEV
            for i in range(n):
                sends.append(remote(i, off, to, to, me))
                sends[-1].start()
        for off in range(1, N_DEV):
            frm = (me + N_DEV - off) % N_DEV
            for i in range(n):
                remote(i, off, frm, me, frm).wait_recv()
        for cp in sends:
            cp.wait_send()
        for cp in local:
            cp.wait()

    outs = pl.pallas_call(
        body, name=name, in_specs=[pl.BlockSpec(memory_space=pltpu.HBM)] * n, out_specs=[pl.BlockSpec(memory_space=pltpu.HBM)] * n,
        out_shape=[SDS((N_DEV,) + s.shape[-2:], s.dtype) for s in srcs],
        scratch_shapes=[pltpu.SemaphoreType.DMA((n, N_DEV - 1)), pltpu.SemaphoreType.DMA((n, N_DEV - 1)), pltpu.SemaphoreType.DMA((n,))],
        compiler_params=pltpu.CompilerParams())(*srcs)
    return outs


def _adamw(parts, w, m, v, tr, name):
    rows, cols = w.shape
    c1, c2 = 1.0 - ADAM_B1 ** ADAM_STEP, 1.0 - ADAM_B2 ** ADAM_STEP

    def body(p_ref, w_ref, m_ref, v_ref, g_ref, d_ref, nm_ref, nv_ref):
        g = p_ref[0]
        for j in range(1, N_DEV):
            g = g + p_ref[j]
        nm = ADAM_B1 * m_ref[...] + (1.0 - ADAM_B1) * g
        nv = ADAM_B2 * v_ref[...] + (1.0 - ADAM_B2) * jnp.square(g)
        g_ref[...] = g
        nm_ref[...] = nm
        nv_ref[...] = nv
        d_ref[...] = -ADAM_LR * ((nm / c1) / (jnp.sqrt(nv / c2) + ADAM_EPS) + ADAM_WD * w_ref[...])

    t = pl.BlockSpec((tr, cols), lambda i: (i, 0))
    return pl.pallas_call(
        body, name=name, grid=(rows // tr,), in_specs=[pl.BlockSpec((N_DEV, tr, cols), lambda i: (0, i, 0)), t, t, t],
        out_specs=[t] * 4, out_shape=[SDS((rows, cols), F32)] * 4, compiler_params=_params(("parallel",)))(parts, w, m, v)


SHARDED = (("w_in", D_MODEL, IN_COLS // N_DEV, True, 128), ("w_up_attn", WIDTH, D_MODEL // N_DEV, True, WIDTH),
           ("w_up_rwkv", WIDTH, D_MODEL // N_DEV, True, WIDTH), ("w_out", D_MODEL // N_DEV, D_MODEL, False, D_MODEL // N_DEV),
           ("rwkv_w_up", LORA, WIDTH // N_DEV, True, LORA), ("rwkv_a_up", LORA, WIDTH // N_DEV, True, LORA))
LOSS_SLOT = sum(n for _, n in SMALL)


def _pack_small(small, extra=None):
    flat = [small[n].reshape(-1).astype(F32) for n, _ in SMALL]
    flat.append(jnp.zeros((1,), F32) if extra is None else extra.reshape(1))
    flat.append(jnp.zeros((SMALL_ROWS * LANE - LOSS_SLOT - 1,), F32))
    return jnp.concatenate(flat).reshape(SMALL_ROWS, LANE)


def _unpack_small(packed, shapes):
    flat = packed.reshape(-1)
    out, off = {}, 0
    for n, cnt in SMALL:
        out[n] = flat[off:off + cnt].reshape(shapes[n])
        off += cnt
    return out, flat[LOSS_SLOT]


def _whole(gathered, by_cols):
    if not by_cols:
        return gathered.reshape(-1, gathered.shape[-1])
    return gathered.transpose(1, 0, 2).reshape(gathered.shape[1], -1)


def _per_owner(full, by_cols):
    if not by_cols:
        return full.reshape(N_DEV, -1, full.shape[-1])
    return full.reshape(full.shape[0], N_DEV, -1).transpose(1, 0, 2)


def _local_step(x, loss_target, sm, wts):
    bsz, s, d = x.shape
    n = bsz * s
    x2, tgt = x.reshape(n, d), loss_target.reshape(n, d)
    bidx = jnp.asarray(_bucket_tables())
    w_in = wts["w_in"]
    segs = (("qkv", 0, QKV_COLS, 512), ("za", OFF_ZA, WIDTH, 512), ("pr", OFF_PR, PR_COLS, PR_COLS), ("zr", OFF_ZR, WIDTH, 512),
            ("gm", OFF_GM, 2 * D_MODEL, 512))

    h, rs = _prenorm(x2, sm["pre_norm_gain"])
    proj = {nm: _mm(h, w_in[:, off:off + cnt], tn, "proj_" + nm) for nm, off, cnt, tn in segs}
    qkv3 = proj["qkv"].reshape(bsz, s, QKV_COLS)
    pr3 = proj["pr"].reshape(bsz, s, PR_COLS)

    o_attn, lse = _attn_fwd(qkv3, sm["rel_bias"], bidx)
    rk = sm["rwkv_r_k"].reshape(1, WIDTH)
    pre_args = (sm["rwkv_shift_mix"], sm["rwkv_w0"], wts["rwkv_w_up"], sm["rwkv_a0"], wts["rwkv_a_up"], sm["rwkv_k_k"], sm["rwkv_k_a"])
    scan_in = _rwkv_pre(pr3, *pre_args)
    o_rwkv, states, consts = _rwkv_scan(scan_in, rk, sm["rwkv_ln_w"], sm["rwkv_ln_b"])

    (dxo, do_attn, do_rwkv, dza, dzr, dgm, g_wua, g_wur, g_wout, g_post, loss) = _head(
        o_attn.reshape(n, WIDTH), o_rwkv.reshape(n, WIDTH), proj["za"], proj["zr"], proj["gm"], x2, tgt,
        wts["w_up_attn"], wts["w_up_rwkv"], wts["w_out"], sm["post_norm_gain"])

    dqkv, dbias = _attn_bwd(qkv3, o_attn, lse, do_attn.reshape(bsz, s, WIDTH), sm["rel_bias"], bidx)
    g_bias = _bias_grad(dbias, bidx)[:, :N_BUCKET].T

    scan_cots, (g_rk, g_lnw, g_lnb) = _rwkv_scan_bwd(scan_in, states, consts, do_rwkv.reshape(bsz, s, WIDTH), rk, sm["rwkv_ln_w"],
                                                     sm["rwkv_ln_b"])
    dprs, g_mix, g_w0, g_wup, g_a0, g_aup, g_kk, g_ka = _rwkv_pre_bwd(pr3, scan_cots, *pre_args)
    dpr = _shift_bwd(dprs, sm["rwkv_shift_mix"]).reshape(n, PR_COLS)

    dsegs = [(t.reshape(n, WIDTH), j * WIDTH) for j, t in enumerate(dqkv)]
    dsegs += [(dza, OFF_ZA), (dpr, OFF_PR), (dzr, OFF_ZR), (dgm[:, :D_MODEL], OFF_GM), (dgm[:, D_MODEL:], OFF_GM + D_MODEL)]
    dh = None
    g_win = []
    for j, (t, off) in enumerate(dsegs):
        cnt = t.shape[1]
        dh = _mm_nt_acc(t, w_in[:, off:off + cnt], dh, "dh_%d" % j)
        g_win.append(_mm_tn(h, t, min(cnt, 1024) if cnt != PR_COLS else PR_COLS, "gw_in_%d" % j))
    grad_x, g_pre = _prenorm_bwd(dh, x2, rs, sm["pre_norm_gain"], dxo)

    full = {"w_in": jnp.concatenate(g_win, axis=1), "w_up_attn": g_wua, "w_up_rwkv": g_wur, "w_out": g_wout,
            "rwkv_w_up": g_wup, "rwkv_a_up": g_aup}
    small = {"pre_norm_gain": g_pre, "rel_bias": g_bias, "rwkv_shift_mix": g_mix, "rwkv_w0": g_w0, "rwkv_a0": g_a0, "rwkv_k_k": g_kk,
             "rwkv_k_a": g_ka, "rwkv_r_k": g_rk, "rwkv_ln_w": g_lnw, "rwkv_ln_b": g_lnb, "post_norm_gain": g_post}
    return loss[0, 0], grad_x.reshape(bsz, s, d), full, small


def kernel(x, pre_norm_gain, w_in, rel_bias, rwkv_shift_mix, rwkv_w0, rwkv_w_up, rwkv_a0, rwkv_a_up, rwkv_k_k, rwkv_k_a, rwkv_r_k, rwkv_ln_w, rwkv_ln_b, w_up_attn, w_up_rwkv, w_out, post_norm_gain, loss_target, m_pre_norm_gain, m_w_in, m_rel_bias, m_rwkv_shift_mix, m_rwkv_w0, m_rwkv_w_up, m_rwkv_a0, m_rwkv_a_up, m_rwkv_k_k, m_rwkv_k_a, m_rwkv_r_k, m_rwkv_ln_w, m_rwkv_ln_b, m_w_up_attn, m_w_up_rwkv, m_w_out, m_post_norm_gain, v_pre_norm_gain, v_w_in, v_rel_bias, v_rwkv_shift_mix, v_rwkv_w0, v_rwkv_w_up, v_rwkv_a0, v_rwkv_a_up, v_rwkv_k_k, v_rwkv_k_a, v_rwkv_r_k, v_rwkv_ln_w, v_rwkv_ln_b, v_w_up_attn, v_w_up_rwkv, v_w_out, v_post_norm_gain):
    names = [n for n, *_ in SHARDED] + [n for n, _ in SMALL]
    loc = dict(locals())
    w = {n: loc[n] for n in names}
    m = {n: loc["m_" + n] for n in names}
    v = {n: loc["v_" + n] for n in names}
    shapes = {n: w[n].shape for n in names}
    order = ["pre_norm_gain", "w_in", "rel_bias", "rwkv_shift_mix", "rwkv_w0", "rwkv_w_up", "rwkv_a0", "rwkv_a_up", "rwkv_k_k", "rwkv_k_a",
             "rwkv_r_k", "rwkv_ln_w", "rwkv_ln_b", "w_up_attn", "w_up_rwkv", "w_out", "post_norm_gain"]
    shard2d = lambda t, n, r, c: t[n].reshape(r, c)

    gathered = _exchange([shard2d(w, n, r, c).astype(BF16) for n, r, c, _, _ in SHARDED], [True] * len(SHARDED), "gather_weights")
    wts = {n: _whole(g, by_cols) for (n, _, _, by_cols, _), g in zip(SHARDED, gathered)}

    loss, grad_x, full, small = _local_step(x, loss_target, w, wts)
    parts = _exchange([_per_owner(full[n], by_cols) for n, _, _, by_cols, _ in SHARDED] + [_pack_small(small, loss)],
                      [False] * len(SHARDED) + [True], "exchange_grads")

    outs = [{}, {}, {}, {}]
    for (n, r, c, _, tr), p in zip(SHARDED, parts):
        res = _adamw(p, shard2d(w, n, r, c), shard2d(m, n, r, c), shard2d(v, n, r, c), tr, "adamw_" + n)
        for o, t in zip(outs, res):
            o[n] = t.reshape(shapes[n])
    res = _adamw(parts[-1], _pack_small(w), _pack_small(m), _pack_small(v), SMALL_ROWS, "adamw_small")
    for o, t in zip(outs, res):
        o.update(_unpack_small(t, shapes)[0])
    loss = _unpack_small(res[0], shapes)[1]
    return (loss, grad_x, *[o[n] for o in outs for n in order])
```

```python
import functools
import math

import numpy as np
import jax
import jax.numpy as jnp
from jax import lax
from jax.experimental import pallas as pl
from jax.experimental.pallas import tpu as pltpu

F32, BF16 = jnp.float32, jnp.bfloat16
SDS = jax.ShapeDtypeStruct
HI = lax.Precision.HIGHEST
HI3 = lax.Precision.HIGH
MESH = pl.DeviceIdType.MESH

N_DEV = 8
D_MODEL = 1024
HEAD = 64
N_HEAD = 8
WIDTH = N_HEAD * HEAD
DILATIONS = (1, 4, 16)
QB = 128
N_BUCKET = 32
MAX_DIST = 2048
LORA = 64
QKV_COLS = 9 * WIDTH
PR_COLS = 3 * WIDTH + 2 * LORA
IN_COLS = QKV_COLS + WIDTH + PR_COLS + WIDTH + 2 * D_MODEL
OFF_ZA, OFF_PR, OFF_ZR, OFF_GM = QKV_COLS, QKV_COLS + WIDTH, QKV_COLS + WIDTH + PR_COLS, QKV_COLS + 2 * WIDTH + PR_COLS
RMS_EPS = 1e-6
GN_EPS = 64e-5
SCALE = 1.0 / math.sqrt(HEAD)
CHUNK = 64
CHUNK_GROUP = 4
BWD_GROUP = 2
EARLY = 8
NEG = -1e30
LANE = 128

ADAM_LR, ADAM_B1, ADAM_B2, ADAM_EPS, ADAM_WD, ADAM_STEP = 0.001, 0.9, 0.999, 1e-08, 0.01, 10

VMEM_LIMIT = 56 * 1024 * 1024

SMALL = (("pre_norm_gain", 1024), ("rel_bias", 768), ("rwkv_shift_mix", 1664), ("rwkv_w0", 512), ("rwkv_a0", 512),
         ("rwkv_k_k", 512), ("rwkv_k_a", 512), ("rwkv_r_k", 512), ("rwkv_ln_w", 512), ("rwkv_ln_b", 512),
         ("post_norm_gain", 1024))
SMALL_ROWS = 64


def _params(sem=None):
    return pltpu.CompilerParams(dimension_semantics=sem, vmem_limit_bytes=VMEM_LIMIT)


def _dot(a, b):
    return jnp.dot(a, b, preferred_element_type=F32)


def _dot_nt(a, b):
    return lax.dot_general(a, b, (((1,), (1,)), ((), ())), preferred_element_type=F32)


def _dot_tn(a, b):
    return lax.dot_general(a, b, (((0,), (0,)), ((), ())), preferred_element_type=F32)


@jax.custom_vjp
def _bdot(a, b):
    return _dot(a.astype(BF16), b.astype(BF16))


def _bdot_fwd(a, b):
    return _bdot(a, b), (a, b)


def _bdot_bwd(res, g):
    a, b = res
    gb = g.astype(BF16)
    return _dot_nt(gb, b.astype(BF16)), _dot_tn(a.astype(BF16), gb)


_bdot.defvjp(_bdot_fwd, _bdot_bwd)


def _silu(z):
    return z * jax.nn.sigmoid(z)


def _dsilu(z):
    s = jax.nn.sigmoid(z)
    return s * (1.0 + z * (1.0 - s))


def _softplus(x):
    return jnp.maximum(x, 0.0) + jnp.log(1.0 + jnp.exp(-jnp.abs(x)))


def _bucket_tables():
    qi = np.arange(QB)[:, None] + QB
    ki = np.arange(2 * QB)[None, :]
    rel = np.maximum(qi - ki, 0)
    out = []
    for d in DILATIONS:
        dist = rel * d
        max_exact = N_BUCKET // 2
        ratio = np.log(np.maximum(dist, 1).astype(np.float32) / max_exact) / np.float32(math.log(MAX_DIST / max_exact))
        large = max_exact + (ratio * (N_BUCKET - max_exact)).astype(np.int32)
        large = np.minimum(large, N_BUCKET - 1)
        out.append(np.where(dist < max_exact, dist, large).astype(np.int32))
    return np.stack(out)


def _prenorm(x2, g):
    n, d = x2.shape
    tm = 1024

    def body(x_ref, g_ref, h_ref, rs_ref):
        x = x_ref[...]
        rs = lax.rsqrt(jnp.mean(x * x, axis=-1, keepdims=True) + RMS_EPS)
        h_ref[...] = (x * rs * g_ref[...]).astype(BF16)
        rs_ref[...] = rs

    return pl.pallas_call(
        body, name="prenorm", grid=(n // tm,),
        in_specs=[pl.BlockSpec((tm, d), lambda i: (i, 0)), pl.BlockSpec((1, d), lambda i: (0, 0))],
        out_specs=[pl.BlockSpec((tm, d), lambda i: (i, 0)), pl.BlockSpec((tm, 1), lambda i: (i, 0))],
        out_shape=[SDS((n, d), BF16), SDS((n, 1), F32)], compiler_params=_params(("parallel",)))(x2, g)


def _mm(a, b, tn, name):
    m, k = a.shape
    n = b.shape[1]
    tm = 1024

    def body(a_ref, b_ref, o_ref):
        o_ref[...] = _dot(a_ref[...], b_ref[...])

    return pl.pallas_call(
        body, name=name, grid=(n // tn, m // tm),
        in_specs=[pl.BlockSpec((tm, k), lambda j, i: (i, 0)), pl.BlockSpec((k, tn), lambda j, i: (0, j))],
        out_specs=pl.BlockSpec((tm, tn), lambda j, i: (i, j)),
        out_shape=SDS((m, n), F32), compiler_params=_params(("parallel", "parallel")))(a, b)


def _mm_nt_acc(a, b, acc, name):
    m, k = a.shape
    d = b.shape[0]
    tm = 512
    have_acc = acc is not None

    def body(*refs):
        if have_acc:
            a_ref, b_ref, c_ref, o_ref = refs
        else:
            a_ref, b_ref, o_ref = refs
        r = _dot_nt(a_ref[...].astype(BF16), b_ref[...])
        o_ref[...] = r + c_ref[...] if have_acc else r

    in_specs = [pl.BlockSpec((tm, k), lambda i: (i, 0)), pl.BlockSpec((d, k), lambda i: (0, 0))]
    args = [a, b]
    if have_acc:
        in_specs.append(pl.BlockSpec((tm, d), lambda i: (i, 0)))
        args.append(acc)
    return pl.pallas_call(
        body, name=name, grid=(m // tm,), in_specs=in_specs, out_specs=pl.BlockSpec((tm, d), lambda i: (i, 0)),
        out_shape=SDS((m, d), F32), compiler_params=_params(("parallel",)))(*args)


def _mm_tn(a, b, tn, name):
    m, k1 = a.shape
    n2 = b.shape[1]
    tm = 1024

    def body(a_ref, b_ref, o_ref):
        r = _dot_tn(a_ref[...], b_ref[...].astype(BF16))

        @pl.when(pl.program_id(1) == 0)
        def _():
            o_ref[...] = r

        @pl.when(pl.program_id(1) != 0)
        def _():
            o_ref[...] += r

    return pl.pallas_call(
        body, name=name, grid=(n2 // tn, m // tm),
        in_specs=[pl.BlockSpec((tm, k1), lambda j, i: (i, 0)), pl.BlockSpec((tm, tn), lambda j, i: (i, j))],
        out_specs=pl.BlockSpec((k1, tn), lambda j, i: (0, j)),
        out_shape=SDS((k1, n2), F32), compiler_params=_params(("parallel", "arbitrary")))(a, b)


def _ds(start, d):
    return pl.ds(start, QB) if d == 1 else pl.ds(start, QB, stride=d)


def _fill_bias(tab_ref, bidx_ref, bias_sc, hp):
    for g in range(3):
        bi = bidx_ref[g]
        for h in range(2):
            acc = jnp.zeros((QB, 2 * QB), F32)
            for j in range(N_BUCKET):
                acc = jnp.where(bi == j, tab_ref[j, g * N_HEAD + hp * 2 + h], acc)
            bias_sc[g * 2 + h] = acc


def _block_starts(it, d, nb):
    rho = it // nb
    n = it % nb
    st = rho + d * QB * n
    stp = rho + d * QB * jnp.maximum(n - 1, 0)
    return st, stp, n > 0


def _logits(q, kc, kp, bias_sc, gh, cur_ok, prev_ok, hasprev):
    sc = _dot_nt(q, kc) * SCALE + bias_sc[gh, :, QB:2 * QB]
    sp = _dot_nt(q, kp) * SCALE + bias_sc[gh, :, 0:QB]
    sc = jnp.where(cur_ok, sc, NEG)
    sp = jnp.where(jnp.logical_and(prev_ok, hasprev), sp, NEG)
    return sc, sp


def _attn_fwd(qkv3, rel_bias, bidx):
    bsz, s, _ = qkv3.shape
    rt = 256

    def body(tab_ref, bidx_ref, *refs):
        q_refs, k_refs, v_refs = refs[0:3], refs[3:6], refs[6:9]
        o_ref, lse_ref = refs[9:11]
        bias_sc, num_sc, den_sc, m_sc = refs[11:]
        hp = pl.program_id(1)
        _fill_bias(tab_ref, bidx_ref, bias_sc, hp)
        ii = lax.broadcasted_iota(jnp.int32, (QB, QB), 0)
        jj = lax.broadcasted_iota(jnp.int32, (QB, QB), 1)
        cur_ok, prev_ok = ii >= jj, jj >= ii
        for g, d in enumerate(DILATIONS):
            nb = s // (QB * d)

            def blk(it, c, g=g, d=d, nb=nb):
                st, stp, hasprev = _block_starts(it, d, nb)
                qf = q_refs[g][0, _ds(st, d), :]
                kcf, kpf = k_refs[g][0, _ds(st, d), :], k_refs[g][0, _ds(stp, d), :]
                vcf, vpf = v_refs[g][0, _ds(st, d), :], v_refs[g][0, _ds(stp, d), :]
                nums, dens, ms = [], [], []
                for h in range(2):
                    sl = slice(HEAD * h, HEAD * h + HEAD)
                    q = qf[:, sl].astype(BF16)
                    sc, sp = _logits(q, kcf[:, sl].astype(BF16), kpf[:, sl].astype(BF16), bias_sc, g * 2 + h, cur_ok, prev_ok, hasprev)
                    m = jnp.maximum(jnp.max(sc, axis=-1, keepdims=True), jnp.max(sp, axis=-1, keepdims=True))
                    pc, pp = jnp.exp(sc - m), jnp.exp(sp - m)
                    den = jnp.sum(pc, axis=-1, keepdims=True) + jnp.sum(pp, axis=-1, keepdims=True)
                    num = _dot(pc.astype(BF16), vcf[:, sl].astype(BF16)) + _dot(pp.astype(BF16), vpf[:, sl].astype(BF16))
                    nums.append(num)
                    dens.append(jnp.broadcast_to(den, (QB, HEAD)))
                    ms.append(jnp.broadcast_to(m, (QB, HEAD)))
                num_sc[g, _ds(st, d), :] = jnp.concatenate(nums, axis=1)
                den_sc[g, _ds(st, d), :] = jnp.concatenate(dens, axis=1)
                m_sc[g, _ds(st, d), :] = jnp.concatenate(ms, axis=1)
                return c

            lax.fori_loop(0, s // QB, blk, 0)

        def merge(i, c):
            rows = pl.ds(pl.multiple_of(i * rt, rt), rt)
            m0, m1, m2 = m_sc[0, rows, :], m_sc[1, rows, :], m_sc[2, rows, :]
            mall = jnp.maximum(jnp.maximum(m0, m1), m2)
            w0, w1, w2 = jnp.exp(m0 - mall), jnp.exp(m1 - mall), jnp.exp(m2 - mall)
            num = w0 * num_sc[0, rows, :] + w1 * num_sc[1, rows, :] + w2 * num_sc[2, rows, :]
            den = w0 * den_sc[0, rows, :] + w1 * den_sc[1, rows, :] + w2 * den_sc[2, rows, :]
            o_ref[0, rows, :] = num / den
            lse_ref[0, rows, :] = mall + jnp.log(den)
            return c

        lax.fori_loop(0, s // rt, merge, 0)

    col = lambda w, g: (lambda b, hp: (b, 0, (w * 3 + g) * 4 + hp))
    in_specs = [pl.BlockSpec(memory_space=pltpu.SMEM), pl.BlockSpec((3, QB, 2 * QB), lambda b, hp: (0, 0, 0))]
    in_specs += [pl.BlockSpec((1, s, LANE), col(w, g)) for w in range(3) for g in range(3)]
    out_spec = pl.BlockSpec((1, s, LANE), lambda b, hp: (b, 0, hp))
    return pl.pallas_call(
        body, name="attn_fwd", grid=(bsz, 4), in_specs=in_specs, out_specs=[out_spec, out_spec],
        out_shape=[SDS((bsz, s, WIDTH), F32), SDS((bsz, s, WIDTH), F32)],
        scratch_shapes=[pltpu.VMEM((6, QB, 2 * QB), F32), pltpu.VMEM((3, s, LANE), F32), pltpu.VMEM((3, s, LANE), F32),
                        pltpu.VMEM((3, s, LANE), F32)],
        compiler_params=_params(("parallel", "parallel")))(rel_bias, bidx, *([qkv3] * 9))


def _attn_bwd(qkv3, o3, lse3, do3, rel_bias, bidx):
    bsz, s, _ = qkv3.shape
    rt = 256

    def body(tab_ref, bidx_ref, *refs):
        q_refs, k_refs, v_refs = refs[0:3], refs[3:6], refs[6:9]
        o_ref, lse_ref, do_ref = refs[9:12]
        dq_refs, dk_refs, dv_refs = refs[12:15], refs[15:18], refs[18:21]
        db_ref = refs[21]
        bias_sc, delta_sc = refs[22:]
        hp, b = pl.program_id(0), pl.program_id(1)
        _fill_bias(tab_ref, bidx_ref, bias_sc, hp)

        @pl.when(b == 0)
        def _():
            db_ref[...] = jnp.zeros_like(db_ref)

        def prep(i, c):
            rows = pl.ds(pl.multiple_of(i * rt, rt), rt)
            prod = do_ref[0, rows, :] * o_ref[0, rows, :]
            d0 = jnp.sum(prod[:, :HEAD], axis=-1, keepdims=True)
            d1 = jnp.sum(prod[:, HEAD:], axis=-1, keepdims=True)
            delta_sc[rows, :] = jnp.concatenate([jnp.broadcast_to(d0, (rt, HEAD)), jnp.broadcast_to(d1, (rt, HEAD))], axis=1)
            z = jnp.zeros((rt, LANE), F32)
            for g in range(3):
                dk_refs[g][0, rows, :] = z
                dv_refs[g][0, rows, :] = z
            return c

        lax.fori_loop(0, s // rt, prep, 0)
        ii = lax.broadcasted_iota(jnp.int32, (QB, QB), 0)
        jj = lax.broadcasted_iota(jnp.int32, (QB, QB), 1)
        cur_ok, prev_ok = ii >= jj, jj >= ii
        for g, d in enumerate(DILATIONS):
            nb = s // (QB * d)

            def blk(it, c, g=g, d=d, nb=nb):
                st, stp, hasprev = _block_starts(it, d, nb)
                qf = q_refs[g][0, _ds(st, d), :]
                kcf, kpf = k_refs[g][0, _ds(st, d), :], k_refs[g][0, _ds(stp, d), :]
                vcf, vpf = v_refs[g][0, _ds(st, d), :], v_refs[g][0, _ds(stp, d), :]
                dof, lsef, delf = do_ref[0, _ds(st, d), :], lse_ref[0, _ds(st, d), :], delta_sc[_ds(st, d), :]
                dqs, dkcs, dkps, dvcs, dvps = [], [], [], [], []
                for h in range(2):
                    sl = slice(HEAD * h, HEAD * h + HEAD)
                    q = qf[:, sl].astype(BF16)
                    kc, kp = kcf[:, sl].astype(BF16), kpf[:, sl].astype(BF16)
                    vc, vp = vcf[:, sl].astype(BF16), vpf[:, sl].astype(BF16)
                    do = dof[:, sl].astype(BF16)
                    lse = lsef[:, HEAD * h:HEAD * h + 1]
                    delta = delf[:, HEAD * h:HEAD * h + 1]
                    sc, sp = _logits(q, kc, kp, bias_sc, g * 2 + h, cur_ok, prev_ok, hasprev)
                    pc, pp = jnp.exp(sc - lse), jnp.exp(sp - lse)
                    dvcs.append(_dot_tn(pc.astype(BF16), do))
                    dvps.append(_dot_tn(pp.astype(BF16), do))
                    dsc = pc * (_dot_nt(do, vc) - delta)
                    dsp = pp * (_dot_nt(do, vp) - delta)
                    db_ref[0, g * 2 + h, :, QB:2 * QB] += dsc
                    db_ref[0, g * 2 + h, :, 0:QB] += dsp
                    dscb, dspb = dsc.astype(BF16), dsp.astype(BF16)
                    dqs.append((_dot(dscb, kc) + _dot(dspb, kp)) * SCALE)
                    dkcs.append(_dot_tn(dscb, q) * SCALE)
                    dkps.append(_dot_tn(dspb, q) * SCALE)
                dq_refs[g][0, _ds(st, d), :] = jnp.concatenate(dqs, axis=1)
                dk_refs[g][0, _ds(st, d), :] += jnp.concatenate(dkcs, axis=1)
                dv_refs[g][0, _ds(st, d), :] += jnp.concatenate(dvcs, axis=1)
                dk_refs[g][0, _ds(stp, d), :] += jnp.concatenate(dkps, axis=1)
                dv_refs[g][0, _ds(stp, d), :] += jnp.concatenate(dvps, axis=1)
                return c

            lax.fori_loop(0, s // QB, blk, 0)

    col = lambda w, g: (lambda hp, b: (b, 0, (w * 3 + g) * 4 + hp))
    blk_spec = pl.BlockSpec((1, s, LANE), lambda hp, b: (b, 0, hp))
    in_specs = [pl.BlockSpec(memory_space=pltpu.SMEM), pl.BlockSpec((3, QB, 2 * QB), lambda hp, b: (0, 0, 0))]
    in_specs += [pl.BlockSpec((1, s, LANE), col(w, g)) for w in range(3) for g in range(3)]
    in_specs += [blk_spec] * 3
    out_specs = [blk_spec] * 9 + [pl.BlockSpec((1, 6, QB, 2 * QB), lambda hp, b: (hp, 0, 0, 0))]
    out_shape = [SDS((bsz, s, WIDTH), F32)] * 9 + [SDS((4, 6, QB, 2 * QB), F32)]
    outs = pl.pallas_call(
        body, name="attn_bwd", grid=(4, bsz), in_specs=in_specs, out_specs=out_specs, out_shape=out_shape,
        scratch_shapes=[pltpu.VMEM((6, QB, 2 * QB), F32), pltpu.VMEM((s, LANE), F32)],
        compiler_params=_params(("parallel", "arbitrary")))(rel_bias, bidx, *([qkv3] * 9), o3, lse3, do3)
    return outs[:9], outs[9]


def _bias_grad(dbias, bidx):
    def body(db_ref, bidx_ref, o_ref):
        lane = lax.broadcasted_iota(jnp.int32, (1, LANE), 1)
        for g in range(3):
            bi = bidx_ref[g]
            for hp in range(4):
                for h in range(2):
                    mat = db_ref[hp, g * 2 + h]
                    row = jnp.zeros((1, LANE), F32)
                    for j in range(N_BUCKET):
                        part = jnp.sum(jnp.where(bi == j, mat, 0.0), axis=0, keepdims=True)
                        row = jnp.where(lane == j, jnp.sum(part, axis=1, keepdims=True), row)
                    hd = g * N_HEAD + hp * 2 + h
                    o_ref[hd:hd + 1, :] = row

    return pl.pallas_call(body, name="bias_grad", out_shape=SDS((3 * N_HEAD, LANE), F32), compiler_params=_params())(dbias, bidx)


def _pre_fn(r, k0, v, wl, al, w0, wup, a0, aup, kk_, ka_):
    u = w0 + _bdot(jnp.tanh(wl), wup)
    lw = -jnp.exp(-_softplus(-u) - 0.5)
    a = jax.nn.sigmoid(a0 + _bdot(al, aup))
    kkraw = k0 * kk_
    k = k0 * (1.0 + (a - 1.0) * ka_)
    return r, lw, k, v, kkraw, a


PRE_SPLIT = (0, WIDTH, 2 * WIDTH, 3 * WIDTH, 3 * WIDTH + LORA, 3 * WIDTH + 2 * LORA)


def _pre_pieces(prs):
    return [prs[:, a:b] for a, b in zip(PRE_SPLIT[:-1], PRE_SPLIT[1:])]


PRE_TT = 512


def _shifted(pr_ref, edge_ref, first, back):
    pr = pr_ref[0]
    tt = pr.shape[0]
    row = lax.broadcasted_iota(jnp.int32, (tt, 1), 0)
    if back:
        edge = jnp.where(first, 0.0, edge_ref[0, 7:8, :])
        return jnp.where(row == 0, edge, pltpu.roll(pr, 1, axis=0))
    edge = jnp.where(first, 0.0, edge_ref[0, 0:1, :])
    return jnp.where(row == tt - 1, edge, pltpu.roll(pr, tt - 1, axis=0))


def _rwkv_pre(pr3, mix, w0, wup, a0, aup, kk_, ka_):
    bsz, s, _ = pr3.shape
    tt = PRE_TT

    def body(pr_ref, edge_ref, mix_ref, w0_ref, wup_ref, a0_ref, aup_ref, kk_ref, ka_ref, *outs):
        pr = pr_ref[0]
        prev = _shifted(pr_ref, edge_ref, pl.program_id(1) == 0, True)
        prs = pr + (prev - pr) * mix_ref[...]
        vals = _pre_fn(*_pre_pieces(prs), w0_ref[...], wup_ref[...].astype(F32), a0_ref[...], aup_ref[...].astype(F32), kk_ref[...],
                       ka_ref[...])
        for o, val in zip(outs, vals):
            o[0] = val

    vec = lambda n: pl.BlockSpec((1, n), lambda b, i: (0, 0))
    mat = pl.BlockSpec((LORA, WIDTH), lambda b, i: (0, 0))
    in_specs = [pl.BlockSpec((1, tt, PR_COLS), lambda b, i: (b, i, 0)),
                pl.BlockSpec((1, 8, PR_COLS), lambda b, i: (b, jnp.maximum(i * (tt // 8) - 1, 0), 0)),
                vec(PR_COLS), vec(WIDTH), mat, vec(WIDTH), mat, vec(WIDTH), vec(WIDTH)]
    out_spec = pl.BlockSpec((1, tt, WIDTH), lambda b, i: (b, i, 0))
    return pl.pallas_call(
        body, name="rwkv_pre", grid=(bsz, s // tt), in_specs=in_specs, out_specs=[out_spec] * 6,
        out_shape=[SDS((bsz, s, WIDTH), F32)] * 6, compiler_params=_params(("parallel", "parallel")))(
            pr3, pr3, mix, w0, wup, a0, aup, kk_, ka_)


def _rwkv_pre_bwd(pr3, cots, mix, w0, wup, a0, aup, kk_, ka_):
    bsz, s, _ = pr3.shape
    tt = PRE_TT

    def body(pr_ref, edge_ref, c0, c1, c2, c3, c4, c5, mix_ref, w0_ref, wup_ref, a0_ref, aup_ref, kk_ref, ka_ref,
             dprs_ref, dmix_ref, dw0_ref, dwup_ref, da0_ref, daup_ref, dkk_ref, dka_ref):
        pr = pr_ref[0]
        prev = _shifted(pr_ref, edge_ref, pl.program_id(1) == 0, True)
        prs = pr + (prev - pr) * mix_ref[...]
        _, vjp = jax.vjp(_pre_fn, *_pre_pieces(prs), w0_ref[...], wup_ref[...].astype(F32), a0_ref[...], aup_ref[...].astype(F32),
                         kk_ref[...], ka_ref[...])
        grads = vjp(tuple(c[0] for c in (c0, c1, c2, c3, c4, c5)))
        for piece, a, b in zip(grads[:5], PRE_SPLIT[:-1], PRE_SPLIT[1:]):
            dprs_ref[0, :, a:b] = piece
        dw0, dwup, da0, daup, dkk, dka = grads[5:]
        dprs = dprs_ref[0]
        grads = (jnp.sum(dprs * (prev - pr), axis=0, keepdims=True), dw0, dwup, da0, daup, dkk, dka)
        refs = (dmix_ref, dw0_ref, dwup_ref, da0_ref, daup_ref, dkk_ref, dka_ref)
        first = jnp.logical_and(pl.program_id(0) == 0, pl.program_id(1) == 0)

        @pl.when(first)
        def _():
            for r_, g_ in zip(refs, grads):
                r_[...] = g_

        @pl.when(jnp.logical_not(first))
        def _():
            for r_, g_ in zip(refs, grads):
                r_[...] += g_

    vec = lambda n: pl.BlockSpec((1, n), lambda b, i: (0, 0))
    mat = pl.BlockSpec((LORA, WIDTH), lambda b, i: (0, 0))
    tile = pl.BlockSpec((1, tt, WIDTH), lambda b, i: (b, i, 0))
    in_specs = [pl.BlockSpec((1, tt, PR_COLS), lambda b, i: (b, i, 0)),
                pl.BlockSpec((1, 8, PR_COLS), lambda b, i: (b, jnp.maximum(i * (tt // 8) - 1, 0), 0))]
    in_specs += [tile] * 6 + [vec(PR_COLS), vec(WIDTH), mat, vec(WIDTH), mat, vec(WIDTH), vec(WIDTH)]
    out_specs = [pl.BlockSpec((1, tt, PR_COLS), lambda b, i: (b, i, 0)), vec(PR_COLS), vec(WIDTH), mat, vec(WIDTH), mat,
                 vec(WIDTH), vec(WIDTH)]
    out_shape = [SDS((bsz, s, PR_COLS), F32), SDS((1, PR_COLS), F32), SDS((1, WIDTH), F32), SDS((LORA, WIDTH), F32),
                 SDS((1, WIDTH), F32), SDS((LORA, WIDTH), F32), SDS((1, WIDTH), F32), SDS((1, WIDTH), F32)]
    return pl.pallas_call(
        body, name="rwkv_pre_bwd", grid=(bsz, s // tt), in_specs=in_specs, out_specs=out_specs, out_shape=out_shape,
        compiler_params=_params(("arbitrary", "arbitrary")))(pr3, pr3, *cots, mix, w0, wup, a0, aup, kk_, ka_)


def _shift_bwd(dprs3, mix):
    bsz, s, _ = dprs3.shape
    tt = PRE_TT
    nt = s // tt

    def body(d_ref, edge_ref, mix_ref, o_ref):
        nxt = _shifted(d_ref, edge_ref, pl.program_id(1) == nt - 1, False)
        m = mix_ref[...]
        o_ref[0] = d_ref[0] * (1.0 - m) + nxt * m

    in_specs = [pl.BlockSpec((1, tt, PR_COLS), lambda b, i: (b, i, 0)),
                pl.BlockSpec((1, 8, PR_COLS), lambda b, i: (b, jnp.minimum((i + 1) * (tt // 8), s // 8 - 1), 0)),
                pl.BlockSpec((1, PR_COLS), lambda b, i: (0, 0))]
    return pl.pallas_call(
        body, name="shift_bwd", grid=(bsz, nt), in_specs=in_specs, out_specs=pl.BlockSpec((1, tt, PR_COLS), lambda b, i: (b, i, 0)),
        out_shape=SDS((bsz, s, PR_COLS), F32), compiler_params=_params(("parallel", "parallel")))(dprs3, dprs3, mix)


def _hdot(a, b):
    return jnp.dot(a, b, precision=HI3, preferred_element_type=F32)


def _hdot_nt(a, b):
    return lax.dot_general(a, b, (((1,), (1,)), ((), ())), precision=HI3, preferred_element_type=F32)


def _hdot_tn(a, b):
    return lax.dot_general(a, b, (((0,), (0,)), ((), ())), precision=HI3, preferred_element_type=F32)


def _bmm(a, b):
    return lax.dot_general(a, b, (((2,), (1,)), ((0,), (0,))), precision=HI3, preferred_element_type=F32)


def _bmm_nt(a, b):
    return lax.dot_general(a, b, (((2,), (2,)), ((0,), (0,))), precision=HI3, preferred_element_type=F32)


def _bmm_tn(a, b):
    return lax.dot_general(a, b, (((1,), (1,)), ((0,), (0,))), precision=HI3, preferred_element_type=F32)


def _chunk_fn(s0t, r, lw, k, v, kkraw, a, rk, lnw, lnb, first=False):
    c = r.shape[1]
    at, rt, btc, ktc, gc, aab, arb, xv, arkv, ain, bin_ = _chunk_core(r, lw, k, v, kkraw, a)
    rs = _bmm(jnp.concatenate([at, rt], axis=1), s0t)
    u = _solve(aab, rs[:, :c] + xv)
    y = rs[:, c:] + _bmm(arb, u) + arkv
    if first:
        y = _with_early_rows(y, r, lw, k, v, ain, bin_)
    gcol = jnp.sum(_diag(gc), axis=2, keepdims=True)
    sct = gcol * s0t + _bmm_tn(jnp.concatenate([btc, ktc], axis=1), jnp.concatenate([u, v], axis=1))
    return _post(y, r, k, v, rk, lnw, lnb), sct


def _diag(gc):
    return jnp.where(_masks(HEAD)[2], gc, 0.0)


def _with_early_rows(y, r, lw, k, v, ain, bin_):
    early = jnp.stack([_early_rows(r[h], lw[h], k[h], v[h], ain[h], bin_[h]) for h in range(2)])
    return jnp.concatenate([jnp.concatenate([early, y[:2, EARLY:]], axis=1), y[2:]], axis=0)


def _early_rows(r, lw, k, v, ain, bin_):
    wc, bc, kc = jnp.transpose(jnp.exp(lw)), jnp.transpose(bin_), jnp.transpose(k)
    st = jnp.zeros((HEAD, HEAD), F32)
    rows = []
    for t in range(EARLY):
        sa = _bdot(ain[t:t + 1], st)
        st = st * wc[:, t:t + 1] + bc[:, t:t + 1] * sa + kc[:, t:t + 1] * v[t:t + 1]
        rows.append(_bdot(r[t:t + 1], st))
    return jnp.concatenate(rows, axis=0)


def _chunk_rows(c):
    return pl.ds(c * CHUNK, CHUNK) if isinstance(c, int) else pl.ds(pl.multiple_of(c * CHUNK, CHUNK), CHUNK)


def _pairs(ref, chunks):
    tiles = [ref[0, _chunk_rows(c), :] for c in chunks]
    return jnp.stack([t[:, HEAD * h:HEAD * h + HEAD] for t in tiles for h in range(2)])


def _unpair(vals, j):
    return jnp.concatenate([vals[2 * j], vals[2 * j + 1]], axis=1)


def _masks(c):
    ii = lax.broadcasted_iota(jnp.int32, (c, c), 0)
    jj = lax.broadcasted_iota(jnp.int32, (c, c), 1)
    return ii > jj, ii >= jj, ii == jj


def _chunk_core(r, lw, k, v, kkraw, a):
    g_, c = r.shape[0], r.shape[1]
    nrm = jnp.sqrt(jnp.sum(kkraw * kkraw, axis=-1, keepdims=True))
    kkn = kkraw / jnp.maximum(nrm, 1e-12)
    ain, bin_ = -kkn, kkn * a
    strict, incl, _ = _masks(c)
    lg = lax.dot_general(jnp.broadcast_to(incl.astype(F32), (g_, c, c)), lw, (((2,), (1,)), ((0,), (0,))), precision=HI,
                         preferred_element_type=F32)
    g, gp, gi = jnp.exp(lg), jnp.exp(lg - lw), jnp.exp(-lg)
    at, rt, bt, kt = ain * gp, r * g, bin_ * gi, k * gi
    aa = _bmm_nt(jnp.concatenate([at, rt], axis=1), jnp.concatenate([bt, kt], axis=1))
    aab = jnp.where(strict, aa[:, :c, :c], 0.0)
    aak = jnp.where(strict, aa[:, :c, c:], 0.0)
    arb = jnp.where(incl, aa[:, c:, :c], 0.0)
    ark = jnp.where(incl, aa[:, c:, c:], 0.0)
    akv = _bmm(jnp.concatenate([aak, ark], axis=1), v)
    gc = g[:, c - 1:c, :]
    return at, rt, bt * gc, kt * gc, gc, aab, arb, akv[:, :c], akv[:, c:], ain, bin_


def _solve(aab, z):
    c = aab.shape[1]
    p = aab
    z = z + _bmm(p, z)
    n = 1
    while 2 * n < c:
        p = _bmm(p, p)
        z = z + _bmm(p, z)
        n *= 2
    return z


def _post(y, r, k, v, rk, lnw, lnb):
    mu = jnp.mean(y, axis=-1, keepdims=True)
    var = jnp.mean(jnp.square(y - mu), axis=-1, keepdims=True)
    yn = (y - mu) * lax.rsqrt(var + GN_EPS) * lnw + lnb
    return yn + jnp.sum(r * k * rk, axis=-1, keepdims=True) * v


def _chunk_consts(r, lw, k, v, kkraw, a, first=False):
    at, rt, btc, ktc, gc, aab, arb, xv, arkv, ain, bin_ = _chunk_core(r, lw, k, v, kkraw, a)
    z = _solve(aab, jnp.concatenate([at, xv], axis=2))
    ryv = jnp.concatenate([rt, arkv], axis=2) + _bmm(arb, z)
    if first:
        ryv = jnp.concatenate([ryv[:, :, :HEAD], _with_early_rows(ryv[:, :, HEAD:], r, lw, k, v, ain, bin_)], axis=2)
    mkv = _bmm_tn(btc, z) + jnp.concatenate([_diag(gc), _bmm_tn(ktc, v)], axis=2)
    return mkv, ryv


def _rwkv_scan(ins, rk, lnw, lnb):
    bsz, s, _ = ins[0].shape
    nch = s // CHUNK

    def consts_body(r_ref, lw_ref, k_ref, v_ref, kk_ref, a_ref, mkv_ref, ry_ref, yv_ref):
        def group(i, carry):
            chunks = [i * CHUNK_GROUP + j for j in range(CHUNK_GROUP)]
            mkv, ryv = _chunk_consts(*[_pairs(ref, chunks) for ref in (r_ref, lw_ref, k_ref, v_ref, kk_ref, a_ref)],
                                     first=isinstance(i, int) and i == 0)
            for j, c in enumerate(chunks):
                for h in range(2):
                    mkv_ref[0, 0, c, h] = mkv[2 * j + h]
                ry_ref[0, _chunk_rows(c), :] = jnp.concatenate([ryv[2 * j][:, :HEAD], ryv[2 * j + 1][:, :HEAD]], axis=1)
                yv_ref[0, _chunk_rows(c), :] = jnp.concatenate([ryv[2 * j][:, HEAD:], ryv[2 * j + 1][:, HEAD:]], axis=1)
            return carry

        group(0, 0)
        lax.fori_loop(1, nch // CHUNK_GROUP, group, 0)

    tile = pl.BlockSpec((1, s, LANE), lambda b, hp: (b, 0, hp))
    vec = pl.BlockSpec((1, LANE), lambda b, hp: (0, hp))
    mkv_spec = pl.BlockSpec((1, 1, nch, 2, HEAD, LANE), lambda b, hp: (b, hp, 0, 0, 0, 0))
    st_spec = pl.BlockSpec((1, 1, nch, 2, HEAD, HEAD), lambda b, hp: (b, hp, 0, 0, 0, 0))
    mkv, ry, yv = pl.pallas_call(
        consts_body, name="rwkv_consts", grid=(bsz, 4), in_specs=[tile] * 6, out_specs=[mkv_spec, tile, tile],
        out_shape=[SDS((bsz, 4, nch, 2, HEAD, LANE), F32), SDS((bsz, s, WIDTH), F32), SDS((bsz, s, WIDTH), F32)],
        compiler_params=_params(("parallel", "parallel")))(*ins)

    def scan_body(mkv_ref, ry_ref, yv_ref, r_ref, k_ref, v_ref, rk_ref, lnw_ref, lnb_ref, o_ref, st_ref, state):
        state[...] = jnp.zeros_like(state)

        def step(c, carry):
            rows = pl.ds(pl.multiple_of(c * CHUNK, CHUNK), CHUNK)
            ry, yv, r, k, v = (ref[0, rows, :] for ref in (ry_ref, yv_ref, r_ref, k_ref, v_ref))
            outs = []
            for h in range(2):
                sl = slice(HEAD * h, HEAD * h + HEAD)
                s0t = state[h]
                st_ref[0, 0, c, h] = s0t
                mkv_h = mkv_ref[0, 0, c, h]
                state[h] = _hdot(mkv_h[:, :HEAD], s0t) + mkv_h[:, HEAD:]
                y = _hdot(ry[:, sl], s0t) + yv[:, sl]
                outs.append(_post(y, r[:, sl], k[:, sl], v[:, sl], rk_ref[:, sl], lnw_ref[:, sl], lnb_ref[:, sl]))
            o_ref[0, rows, :] = jnp.concatenate(outs, axis=1)
            return carry

        lax.fori_loop(0, nch, step, 0)

    o, states = pl.pallas_call(
        scan_body, name="rwkv_scan", grid=(bsz, 4), in_specs=[mkv_spec] + [tile] * 5 + [vec] * 3, out_specs=[tile, st_spec],
        out_shape=[SDS((bsz, s, WIDTH), F32), SDS((bsz, 4, nch, 2, HEAD, HEAD), F32)],
        scratch_shapes=[pltpu.VMEM((2, HEAD, HEAD), F32)],
        compiler_params=_params(("parallel", "parallel")))(mkv, ry, yv, ins[0], ins[2], ins[3], rk, lnw, lnb)
    return o, states, (mkv, ry, yv)


def _rwkv_scan_bwd(ins, states, consts, do3, rk, lnw, lnb):
    bsz, s, _ = ins[0].shape
    nch = s // CHUNK

    mkv, ry, yv = consts
    tile_f = pl.BlockSpec((1, s, LANE), lambda b, hp: (b, 0, hp))
    vec_f = pl.BlockSpec((1, LANE), lambda b, hp: (0, hp))
    mkv_spec = pl.BlockSpec((1, 1, nch, 2, HEAD, LANE), lambda b, hp: (b, hp, 0, 0, 0, 0))
    st_spec_f = pl.BlockSpec((1, 1, nch, 2, HEAD, HEAD), lambda b, hp: (b, hp, 0, 0, 0, 0))

    def dstate_body(mkv_ref, ry_ref, yv_ref, r_ref, k_ref, v_ref, st_ref, do_ref, rk_ref, lnw_ref, lnb_ref, dst_ref, dstate):
        dstate[...] = jnp.zeros_like(dstate)

        def step(i, carry):
            c = nch - 1 - i
            rows = pl.ds(pl.multiple_of(c * CHUNK, CHUNK), CHUNK)
            ry_, yv_, r, k, v, do = (ref[0, rows, :] for ref in (ry_ref, yv_ref, r_ref, k_ref, v_ref, do_ref))
            for h in range(2):
                sl = slice(HEAD * h, HEAD * h + HEAD)
                dsct = dstate[h]
                dst_ref[0, 0, c, h] = dsct
                y = _hdot(ry_[:, sl], st_ref[0, 0, c, h]) + yv_[:, sl]
                _, vjp = jax.vjp(lambda y_: _post(y_, r[:, sl], k[:, sl], v[:, sl], rk_ref[:, sl], lnw_ref[:, sl], lnb_ref[:, sl]), y)
                (dy,) = vjp(do[:, sl])
                dstate[h] = _hdot_tn(mkv_ref[0, 0, c, h][:, :HEAD], dsct) + _hdot_tn(ry_[:, sl], dy)
            return carry

        lax.fori_loop(0, nch, step, 0)

    dstates = pl.pallas_call(
        dstate_body, name="rwkv_dstate", grid=(bsz, 4), in_specs=[mkv_spec] + [tile_f] * 5 + [st_spec_f, tile_f] + [vec_f] * 3,
        out_specs=st_spec_f, out_shape=SDS((bsz, 4, nch, 2, HEAD, HEAD), F32), scratch_shapes=[pltpu.VMEM((2, HEAD, HEAD), F32)],
        compiler_params=_params(("parallel", "parallel")))(mkv, ry, yv, ins[0], ins[2], ins[3], states, do3, rk, lnw, lnb)

    def body(r_ref, lw_ref, k_ref, v_ref, kk_ref, a_ref, st_ref, dst_ref, do_ref, rk_ref, lnw_ref, lnb_ref,
             dr_ref, dlw_ref, dk_ref, dv_ref, dkk_ref, da_ref, drk_ref, dlnw_ref, dlnb_ref):
        def group(i, dp):
            chunks = [i * BWD_GROUP + j for j in range(BWD_GROUP)]
            per_pair = lambda ref: jnp.stack([ref[0, 0, c, h] for c in chunks for h in range(2)])
            vecs = [jnp.stack([ref[:, HEAD * h:HEAD * h + HEAD] for _ in chunks for h in range(2)]) for ref in (rk_ref, lnw_ref, lnb_ref)]
            _, vjp = jax.vjp(functools.partial(_chunk_fn, first=isinstance(i, int) and i == 0), per_pair(st_ref),
                             *[_pairs(ref, chunks) for ref in (r_ref, lw_ref, k_ref, v_ref, kk_ref, a_ref)], *vecs)
            grads = vjp((_pairs(do_ref, chunks), per_pair(dst_ref)))
            for ref, cot in zip((dr_ref, dlw_ref, dk_ref, dv_ref, dkk_ref, da_ref), grads[1:7]):
                for j, c in enumerate(chunks):
                    ref[0, _chunk_rows(c), :] = _unpair(cot, j)
            return tuple(tuple(p_ + sum(g_[2 * j + h] for j in range(BWD_GROUP)) for p_, g_ in zip(dp[h], grads[7:10]))
                         for h in range(2))

        zp = (jnp.zeros((1, HEAD), F32),) * 3
        dp = lax.fori_loop(1, nch // BWD_GROUP, group, group(0, (zp, zp)))
        first = pl.program_id(1) == 0
        for j, ref in enumerate((drk_ref, dlnw_ref, dlnb_ref)):
            val = jnp.concatenate([dp[0][j], dp[1][j]], axis=1)

            @pl.when(first)
            def _(ref=ref, val=val):
                ref[...] = val

            @pl.when(jnp.logical_not(first))
            def _(ref=ref, val=val):
                ref[...] += val

    tile = pl.BlockSpec((1, s, LANE), lambda hp, b: (b, 0, hp))
    vec = pl.BlockSpec((1, LANE), lambda hp, b: (0, hp))
    st_spec = pl.BlockSpec((1, 1, nch, 2, HEAD, HEAD), lambda hp, b: (b, hp, 0, 0, 0, 0))
    outs = pl.pallas_call(
        body, name="rwkv_scan_bwd", grid=(4, bsz), in_specs=[tile] * 6 + [st_spec, st_spec, tile] + [vec] * 3,
        out_specs=[tile] * 6 + [vec] * 3,
        out_shape=[SDS((bsz, s, WIDTH), F32)] * 6 + [SDS((1, WIDTH), F32)] * 3,
        compiler_params=_params(("parallel", "arbitrary")))(*ins, states, dstates, do3, rk, lnw, lnb)
    return outs[:6], outs[6:]


def _head(o_attn, o_rwkv, z_attn, z_rwkv, gm, x2, tgt, wua, wur, wout, g2):
    n = x2.shape[0]
    tm = 256
    nt = n // tm
    d = D_MODEL

    def body(oa_ref, or_ref, za_ref, zr_ref, gm_ref, x_ref, t_ref, wua_ref, wur_ref, wout_ref, g2_ref,
             dxo_ref, doa_ref, dor_ref, dza_ref, dzr_ref, dgm_ref, dwua_ref, dwur_ref, dwout_ref, dg2_ref, loss_ref, lacc):
        i = pl.program_id(0)
        oa, orw, za, zr = oa_ref[...], or_ref[...], za_ref[...], zr_ref[...]
        ga, gb = gm_ref[:, 0:d], gm_ref[:, d:2 * d]
        am = (oa * _silu(za)).astype(BF16)
        bm = (orw * _silu(zr)).astype(BF16)
        ya, yb = _dot(am, wua_ref[...]), _dot(bm, wur_ref[...])
        sa, sb = jax.nn.sigmoid(ga), jax.nn.sigmoid(gb)
        merged = (sa * ya + sb * yb).astype(BF16)
        out = _dot(merged, wout_ref[...])
        rs = lax.rsqrt(jnp.mean(out * out, axis=-1, keepdims=True) + RMS_EPS)
        g2 = g2_ref[...]
        err = x_ref[...] + out * rs * g2 - t_ref[...]
        lpart = jnp.sum(err * err, axis=0, keepdims=True)
        dxo = err * (1.0 / d)
        dxo_ref[...] = dxo
        dg2 = jnp.sum(dxo * out * rs, axis=0, keepdims=True)
        gd = dxo * g2
        dout = (rs * (gd - out * (rs * rs) * jnp.mean(gd * out, axis=-1, keepdims=True))).astype(BF16)
        dmerged = _dot_nt(dout, wout_ref[...])
        dwout = _dot_tn(merged, dout)
        dya, dyb = (dmerged * sa).astype(BF16), (dmerged * sb).astype(BF16)
        dgm_ref[:, 0:d] = dmerged * ya * sa * (1.0 - sa)
        dgm_ref[:, d:2 * d] = dmerged * yb * sb * (1.0 - sb)
        dam, dbm = _dot_nt(dya, wua_ref[...]), _dot_nt(dyb, wur_ref[...])
        dwua, dwur = _dot_tn(am, dya), _dot_tn(bm, dyb)
        doa_ref[...] = dam * _silu(za)
        dza_ref[...] = dam * oa * _dsilu(za)
        dor_ref[...] = dbm * _silu(zr)
        dzr_ref[...] = dbm * orw * _dsilu(zr)

        @pl.when(i == 0)
        def _():
            dwua_ref[...], dwur_ref[...], dwout_ref[...], dg2_ref[...], lacc[...] = dwua, dwur, dwout, dg2, lpart

        @pl.when(i != 0)
        def _():
            dwua_ref[...] += dwua
            dwur_ref[...] += dwur
            dwout_ref[...] += dwout
            dg2_ref[...] += dg2
            lacc[...] += lpart

        @pl.when(i == nt - 1)
        def _():
            loss_ref[...] = jnp.sum(lacc[...], axis=1, keepdims=True) * (0.5 / d)

    t512 = pl.BlockSpec((tm, WIDTH), lambda i: (i, 0))
    t1k = pl.BlockSpec((tm, d), lambda i: (i, 0))
    t2k = pl.BlockSpec((tm, 2 * d), lambda i: (i, 0))
    full = lambda r, c: pl.BlockSpec((r, c), lambda i: (0, 0))
    return pl.pallas_call(
        body, name="head_fwd_bwd", grid=(nt,),
        in_specs=[t512, t512, t512, t512, t2k, t1k, t1k, full(WIDTH, d), full(WIDTH, d), full(d, d), full(1, d)],
        out_specs=[t1k, t512, t512, t512, t512, t2k, full(WIDTH, d), full(WIDTH, d), full(d, d), full(1, d), full(1, 1)],
        out_shape=[SDS((n, d), F32)] + [SDS((n, WIDTH), F32)] * 4 + [SDS((n, 2 * d), F32), SDS((WIDTH, d), F32), SDS((WIDTH, d), F32),
                                                                    SDS((d, d), F32), SDS((1, d), F32), SDS((1, 1), F32)],
        scratch_shapes=[pltpu.VMEM((1, d), F32)],
        compiler_params=_params(("arbitrary",)))(o_attn, o_rwkv, z_attn, z_rwkv, gm, x2, tgt, wua, wur, wout, g2)


def _prenorm_bwd(dh, x2, rs, g1, dxo):
    n, d = x2.shape
    tm = 1024

    def body(dh_ref, x_ref, rs_ref, g_ref, dxo_ref, gx_ref, dg_ref):
        x, r = x_ref[...], rs_ref[...]
        gd = dh_ref[...] * g_ref[...]
        gx_ref[...] = dxo_ref[...] + r * (gd - x * (r * r) * jnp.mean(gd * x, axis=-1, keepdims=True))
        dg = jnp.sum(dh_ref[...] * x * r, axis=0, keepdims=True)

        @pl.when(pl.program_id(0) == 0)
        def _():
            dg_ref[...] = dg

        @pl.when(pl.program_id(0) != 0)
        def _():
            dg_ref[...] += dg

    t = pl.BlockSpec((tm, d), lambda i: (i, 0))
    return pl.pallas_call(
        body, name="prenorm_bwd", grid=(n // tm,),
        in_specs=[t, t, pl.BlockSpec((tm, 1), lambda i: (i, 0)), pl.BlockSpec((1, d), lambda i: (0, 0)), t],
        out_specs=[t, pl.BlockSpec((1, d), lambda i: (0, 0))], out_shape=[SDS((n, d), F32), SDS((1, d), F32)],
        compiler_params=_params(("arbitrary",)))(dh, x2, rs, g1, dxo)


def _mesh_pos():
    x, y, c = lax.axis_index("x"), lax.axis_index("y"), lax.axis_index("c")
    return 4 * x + 2 * y + c


def _coords(idx):
    return (idx // 4, (idx // 2) % 2, idx % 2)


def _exchange(srcs, to_all, name):
    n = len(srcs)

    def body(*refs):
        src_refs, dst_refs = refs[:n], refs[n:2 * n]
        send_sems, recv_sems, local_sems = refs[2 * n:]
        me = _mesh_pos()

        def piece(i, j):
            return src_refs[i] if to_all[i] else src_refs[i].at[j]

        def remote(i, off, peer, block, slot):
            return pltpu.make_async_remote_copy(src_ref=piece(i, block), dst_ref=dst_refs[i].at[slot],
                                                send_sem=send_sems.at[i, off - 1], recv_sem=recv_sems.at[i, off - 1],
                                                device_id=_coords(peer), device_id_type=MESH)

        local = [pltpu.make_async_copy(piece(i, me), dst_refs[i].at[me], local_sems.at[i]) for i in range(n)]
        for cp in local:
            cp.start()
        sends = []
        for off in range(1, N_DEV):
            to = (me + off) % N_DEV
            for i in range(n):
                sends.append(remote(i, off, to, to, me))
                sends[-1].start()
        for off in range(1, N_DEV):
            frm = (me + N_DEV - off) % N_DEV
            for i in range(n):
                remote(i, off, frm, me, frm).wait_recv()
        for cp in sends:
            cp.wait_send()
        for cp in local:
            cp.wait()

    outs = pl.pallas_call(
        body, name=name, in_specs=[pl.BlockSpec(memory_space=pltpu.HBM)] * n, out_specs=[pl.BlockSpec(memory_space=pltpu.HBM)] * n,
        out_shape=[SDS((N_DEV,) + s.shape[-2:], s.dtype) for s in srcs],
        scratch_shapes=[pltpu.SemaphoreType.DMA((n, N_DEV - 1)), pltpu.SemaphoreType.DMA((n, N_DEV - 1)), pltpu.SemaphoreType.DMA((n,))],
        compiler_params=pltpu.CompilerParams())(*srcs)
    return outs


def _adamw(parts, w, m, v, tr, name):
    rows, cols = w.shape
    c1, c2 = 1.0 - ADAM_B1 ** ADAM_STEP, 1.0 - ADAM_B2 ** ADAM_STEP

    def body(p_ref, w_ref, m_ref, v_ref, g_ref, d_ref, nm_ref, nv_ref):
        g = p_ref[0]
        for j in range(1, N_DEV):
            g = g + p_ref[j]
        nm = ADAM_B1 * m_ref[...] + (1.0 - ADAM_B1) * g
        nv = ADAM_B2 * v_ref[...] + (1.0 - ADAM_B2) * jnp.square(g)
        g_ref[...] = g
        nm_ref[...] = nm
        nv_ref[...] = nv
        d_ref[...] = -ADAM_LR * ((nm / c1) / (jnp.sqrt(nv / c2) + ADAM_EPS) + ADAM_WD * w_ref[...])

    t = pl.BlockSpec((tr, cols), lambda i: (i, 0))
    return pl.pallas_call(
        body, name=name, grid=(rows // tr,), in_specs=[pl.BlockSpec((N_DEV, tr, cols), lambda i: (0, i, 0)), t, t, t],
        out_specs=[t] * 4, out_shape=[SDS((rows, cols), F32)] * 4, compiler_params=_params(("parallel",)))(parts, w, m, v)


SHARDED = (("w_in", D_MODEL, IN_COLS // N_DEV, True, 128), ("w_up_attn", WIDTH, D_MODEL // N_DEV, True, WIDTH),
           ("w_up_rwkv", WIDTH, D_MODEL // N_DEV, True, WIDTH), ("w_out", D_MODEL // N_DEV, D_MODEL, False, D_MODEL // N_DEV),
           ("rwkv_w_up", LORA, WIDTH // N_DEV, True, LORA), ("rwkv_a_up", LORA, WIDTH // N_DEV, True, LORA))
LOSS_SLOT = sum(n for _, n in SMALL)


def _pack_small(small, extra=None):
    flat = [small[n].reshape(-1).astype(F32) for n, _ in SMALL]
    flat.append(jnp.zeros((1,), F32) if extra is None else extra.reshape(1))
    flat.append(jnp.zeros((SMALL_ROWS * LANE - LOSS_SLOT - 1,), F32))
    return jnp.concatenate(flat).reshape(SMALL_ROWS, LANE)


def _unpack_small(packed, shapes):
    flat = packed.reshape(-1)
    out, off = {}, 0
    for n, cnt in SMALL:
        out[n] = flat[off:off + cnt].reshape(shapes[n])
        off += cnt
    return out, flat[LOSS_SLOT]


def _whole(gathered, by_cols):
    if not by_cols:
        return gathered.reshape(-1, gathered.shape[-1])
    return gathered.transpose(1, 0, 2).reshape(gathered.shape[1], -1)


def _per_owner(full, by_cols):
    if not by_cols:
        return full.reshape(N_DEV, -1, full.shape[-1])
    return full.reshape(full.shape[0], N_DEV, -1).transpose(1, 0, 2)


def _local_step(x, loss_target, sm, wts):
    bsz, s, d = x.shape
    n = bsz * s
    x2, tgt = x.reshape(n, d), loss_target.reshape(n, d)
    bidx = jnp.asarray(_bucket_tables())
    w_in = wts["w_in"]
    segs = (("qkv", 0, QKV_COLS, 512), ("za", OFF_ZA, WIDTH, 512), ("pr", OFF_PR, PR_COLS, PR_COLS), ("zr", OFF_ZR, WIDTH, 512),
            ("gm", OFF_GM, 2 * D_MODEL, 512))

    h, rs = _prenorm(x2, sm["pre_norm_gain"])
    proj = {nm: _mm(h, w_in[:, off:off + cnt], tn, "proj_" + nm) for nm, off, cnt, tn in segs}
    qkv3 = proj["qkv"].reshape(bsz, s, QKV_COLS)
    pr3 = proj["pr"].reshape(bsz, s, PR_COLS)

    o_attn, lse = _attn_fwd(qkv3, sm["rel_bias"], bidx)
    rk = sm["rwkv_r_k"].reshape(1, WIDTH)
    pre_args = (sm["rwkv_shift_mix"], sm["rwkv_w0"], wts["rwkv_w_up"], sm["rwkv_a0"], wts["rwkv_a_up"], sm["rwkv_k_k"], sm["rwkv_k_a"])
    scan_in = _rwkv_pre(pr3, *pre_args)
    o_rwkv, states, consts = _rwkv_scan(scan_in, rk, sm["rwkv_ln_w"], sm["rwkv_ln_b"])

    (dxo, do_attn, do_rwkv, dza, dzr, dgm, g_wua, g_wur, g_wout, g_post, loss) = _head(
        o_attn.reshape(n, WIDTH), o_rwkv.reshape(n, WIDTH), proj["za"], proj["zr"], proj["gm"], x2, tgt,
        wts["w_up_attn"], wts["w_up_rwkv"], wts["w_out"], sm["post_norm_gain"])

    dqkv, dbias = _attn_bwd(qkv3, o_attn, lse, do_attn.reshape(bsz, s, WIDTH), sm["rel_bias"], bidx)
    g_bias = _bias_grad(dbias, bidx)[:, :N_BUCKET].T

    scan_cots, (g_rk, g_lnw, g_lnb) = _rwkv_scan_bwd(scan_in, states, consts, do_rwkv.reshape(bsz, s, WIDTH), rk, sm["rwkv_ln_w"],
                                                     sm["rwkv_ln_b"])
    dprs, g_mix, g_w0, g_wup, g_a0, g_aup, g_kk, g_ka = _rwkv_pre_bwd(pr3, scan_cots, *pre_args)
    dpr = _shift_bwd(dprs, sm["rwkv_shift_mix"]).reshape(n, PR_COLS)

    dsegs = [(t.reshape(n, WIDTH), j * WIDTH) for j, t in enumerate(dqkv)]
    dsegs += [(dza, OFF_ZA), (dpr, OFF_PR), (dzr, OFF_ZR), (dgm[:, :D_MODEL], OFF_GM), (dgm[:, D_MODEL:], OFF_GM + D_MODEL)]
    dh = None
    g_win = []
    for j, (t, off) in enumerate(dsegs):
        cnt = t.shape[1]
        dh = _mm_nt_acc(t, w_in[:, off:off + cnt], dh, "dh_%d" % j)
        g_win.append(_mm_tn(h, t, min(cnt, 1024) if cnt != PR_COLS else PR_COLS, "gw_in_%d" % j))
    grad_x, g_pre = _prenorm_bwd(dh, x2, rs, sm["pre_norm_gain"], dxo)

    full = {"w_in": jnp.concatenate(g_win, axis=1), "w_up_attn": g_wua, "w_up_rwkv": g_wur, "w_out": g_wout,
            "rwkv_w_up": g_wup, "rwkv_a_up": g_aup}
    small = {"pre_norm_gain": g_pre, "rel_bias": g_bias, "rwkv_shift_mix": g_mix, "rwkv_w0": g_w0, "rwkv_a0": g_a0, "rwkv_k_k": g_kk,
             "rwkv_k_a": g_ka, "rwkv_r_k": g_rk, "rwkv_ln_w": g_lnw, "rwkv_ln_b": g_lnb, "post_norm_gain": g_post}
    return loss[0, 0], grad_x.reshape(bsz, s, d), full, small


def kernel(x, pre_norm_gain, w_in, rel_bias, rwkv_shift_mix, rwkv_w0, rwkv_w_up, rwkv_a0, rwkv_a_up, rwkv_k_k, rwkv_k_a, rwkv_r_k, rwkv_ln_w, rwkv_ln_b, w_up_attn, w_up_rwkv, w_out, post_norm_gain, loss_target, m_pre_norm_gain, m_w_in, m_rel_bias, m_rwkv_shift_mix, m_rwkv_w0, m_rwkv_w_up, m_rwkv_a0, m_rwkv_a_up, m_rwkv_k_k, m_rwkv_k_a, m_rwkv_r_k, m_rwkv_ln_w, m_rwkv_ln_b, m_w_up_attn, m_w_up_rwkv, m_w_out, m_post_norm_gain, v_pre_norm_gain, v_w_in, v_rel_bias, v_rwkv_shift_mix, v_rwkv_w0, v_rwkv_w_up, v_rwkv_a0, v_rwkv_a_up, v_rwkv_k_k, v_rwkv_k_a, v_rwkv_r_k, v_rwkv_ln_w, v_rwkv_ln_b, v_w_up_attn, v_w_up_rwkv, v_w_out, v_post_norm_gain):
    names = [n for n, *_ in SHARDED] + [n for n, _ in SMALL]
    loc = dict(locals())
    w = {n: loc[n] for n in names}
    m = {n: loc["m_" + n] for n in names}
    v = {n: loc["v_" + n] for n in names}
    shapes = {n: w[n].shape for n in names}
    order = ["pre_norm_gain", "w_in", "rel_bias", "rwkv_shift_mix", "rwkv_w0", "rwkv_w_up", "rwkv_a0", "rwkv_a_up", "rwkv_k_k", "rwkv_k_a",
             "rwkv_r_k", "rwkv_ln_w", "rwkv_ln_b", "w_up_attn", "w_up_rwkv", "w_out", "post_norm_gain"]
    shard2d = lambda t, n, r, c: t[n].reshape(r, c)

    gathered = _exchange([shard2d(w, n, r, c).astype(BF16) for n, r, c, _, _ in SHARDED], [True] * len(SHARDED), "gather_weights")
    wts = {n: _whole(g, by_cols) for (n, _, _, by_cols, _), g in zip(SHARDED, gathered)}

    loss, grad_x, full, small = _local_step(x, loss_target, w, wts)
    parts = _exchange([_per_owner(full[n], by_cols) for n, _, _, by_cols, _ in SHARDED] + [_pack_small(small, loss)],
                      [False] * len(SHARDED) + [True], "exchange_grads")

    outs = [{}, {}, {}, {}]
    for (n, r, c, _, tr), p in zip(SHARDED, parts):
        res = _adamw(p, shard2d(w, n, r, c), shard2d(m, n, r, c), shard2d(v, n, r, c), tr, "adamw_" + n)
        for o, t in zip(outs, res):
            o[n] = t.reshape(shapes[n])
    res = _adamw(parts[-1], _pack_small(w), _pack_small(m), _pack_small(v), SMALL_ROWS, "adamw_small")
    for o, t in zip(outs, res):
        o.update(_unpack_small(t, shapes)[0])
    loss = _unpack_small(res[0], shapes)[1]
    return (loss, grad_x, *[o[n] for o in outs for n in order])
```

```python
import functools
import math

import numpy as np
import jax
import jax.numpy as jnp
from jax import lax
from jax.experimental import pallas as pl
from jax.experimental.pallas import tpu as pltpu

F32, BF16 = jnp.float32, jnp.bfloat16
SDS = jax.ShapeDtypeStruct
HI = lax.Precision.HIGHEST
HI3 = lax.Precision.HIGH
MESH = pl.DeviceIdType.MESH

N_DEV = 8
D_MODEL = 1024
HEAD = 64
N_HEAD = 8
WIDTH = N_HEAD * HEAD
DILATIONS = (1, 4, 16)
QB = 128
N_BUCKET = 32
MAX_DIST = 2048
LORA = 64
QKV_COLS = 9 * WIDTH
PR_COLS = 3 * WIDTH + 2 * LORA
IN_COLS = QKV_COLS + WIDTH + PR_COLS + WIDTH + 2 * D_MODEL
OFF_ZA, OFF_PR, OFF_ZR, OFF_GM = QKV_COLS, QKV_COLS + WIDTH, QKV_COLS + WIDTH + PR_COLS, QKV_COLS + 2 * WIDTH + PR_COLS
RMS_EPS = 1e-6
GN_EPS = 64e-5
SCALE = 1.0 / math.sqrt(HEAD)
CHUNK = 64
CHUNK_GROUP = 4
BWD_GROUP = 4
EARLY = 8
NEG = -1e30
LANE = 128

ADAM_LR, ADAM_B1, ADAM_B2, ADAM_EPS, ADAM_WD, ADAM_STEP = 0.001, 0.9, 0.999, 1e-08, 0.01, 10

VMEM_LIMIT = 56 * 1024 * 1024

SMALL = (("pre_norm_gain", 1024), ("rel_bias", 768), ("rwkv_shift_mix", 1664), ("rwkv_w0", 512), ("rwkv_a0", 512),
         ("rwkv_k_k", 512), ("rwkv_k_a", 512), ("rwkv_r_k", 512), ("rwkv_ln_w", 512), ("rwkv_ln_b", 512),
         ("post_norm_gain", 1024))
SMALL_ROWS = 64


def _params(sem=None):
    return pltpu.CompilerParams(dimension_semantics=sem, vmem_limit_bytes=VMEM_LIMIT)


def _dot(a, b):
    return jnp.dot(a, b, preferred_element_type=F32)


def _dot_nt(a, b):
    return lax.dot_general(a, b, (((1,), (1,)), ((), ())), preferred_element_type=F32)


def _dot_tn(a, b):
    return lax.dot_general(a, b, (((0,), (0,)), ((), ())), preferred_element_type=F32)


@jax.custom_vjp
def _bdot(a, b):
    return _dot(a.astype(BF16), b.astype(BF16))


def _bdot_fwd(a, b):
    return _bdot(a, b), (a, b)


def _bdot_bwd(res, g):
    a, b = res
    gb = g.astype(BF16)
    return _dot_nt(gb, b.astype(BF16)), _dot_tn(a.astype(BF16), gb)


_bdot.defvjp(_bdot_fwd, _bdot_bwd)


def _silu(z):
    return z * jax.nn.sigmoid(z)


def _dsilu(z):
    s = jax.nn.sigmoid(z)
    return s * (1.0 + z * (1.0 - s))


def _softplus(x):
    return jnp.maximum(x, 0.0) + jnp.log(1.0 + jnp.exp(-jnp.abs(x)))


def _bucket_tables():
    qi = np.arange(QB)[:, None] + QB
    ki = np.arange(2 * QB)[None, :]
    rel = np.maximum(qi - ki, 0)
    out = []
    for d in DILATIONS:
        dist = rel * d
        max_exact = N_BUCKET // 2
        ratio = np.log(np.maximum(dist, 1).astype(np.float32) / max_exact) / np.float32(math.log(MAX_DIST / max_exact))
        large = max_exact + (ratio * (N_BUCKET - max_exact)).astype(np.int32)
        large = np.minimum(large, N_BUCKET - 1)
        out.append(np.where(dist < max_exact, dist, large).astype(np.int32))
    return np.stack(out)


def _prenorm(x2, g):
    n, d = x2.shape
    tm = 1024

    def body(x_ref, g_ref, h_ref, rs_ref):
        x = x_ref[...]
        rs = lax.rsqrt(jnp.mean(x * x, axis=-1, keepdims=True) + RMS_EPS)
        h_ref[...] = (x * rs * g_ref[...]).astype(BF16)
        rs_ref[...] = rs

    return pl.pallas_call(
        body, name="prenorm", grid=(n // tm,),
        in_specs=[pl.BlockSpec((tm, d), lambda i: (i, 0)), pl.BlockSpec((1, d), lambda i: (0, 0))],
        out_specs=[pl.BlockSpec((tm, d), lambda i: (i, 0)), pl.BlockSpec((tm, 1), lambda i: (i, 0))],
        out_shape=[SDS((n, d), BF16), SDS((n, 1), F32)], compiler_params=_params(("parallel",)))(x2, g)


def _mm(a, b, tn, name):
    m, k = a.shape
    n = b.shape[1]
    tm = 1024

    def body(a_ref, b_ref, o_ref):
        o_ref[...] = _dot(a_ref[...], b_ref[...])

    return pl.pallas_call(
        body, name=name, grid=(n // tn, m // tm),
        in_specs=[pl.BlockSpec((tm, k), lambda j, i: (i, 0)), pl.BlockSpec((k, tn), lambda j, i: (0, j))],
        out_specs=pl.BlockSpec((tm, tn), lambda j, i: (i, j)),
        out_shape=SDS((m, n), F32), compiler_params=_params(("parallel", "parallel")))(a, b)


def _mm_nt_acc(a, b, acc, name):
    m, k = a.shape
    d = b.shape[0]
    tm = 512
    tk = k if k <= 2048 else 512
    have_acc = acc is not None

    def body(*refs):
        if have_acc:
            a_ref, b_ref, c_ref, o_ref = refs
        else:
            a_ref, b_ref, o_ref = refs
        r = _dot_nt(a_ref[...].astype(BF16), b_ref[...])

        @pl.when(pl.program_id(1) == 0)
        def _():
            o_ref[...] = r + c_ref[...] if have_acc else r

        @pl.when(pl.program_id(1) != 0)
        def _():
            o_ref[...] += r

    in_specs = [pl.BlockSpec((tm, tk), lambda i, j: (i, j)), pl.BlockSpec((d, tk), lambda i, j: (0, j))]
    args = [a, b]
    if have_acc:
        in_specs.append(pl.BlockSpec((tm, d), lambda i, j: (i, 0)))
        args.append(acc)
    return pl.pallas_call(
        body, name=name, grid=(m // tm, k // tk), in_specs=in_specs, out_specs=pl.BlockSpec((tm, d), lambda i, j: (i, 0)),
        out_shape=SDS((m, d), F32), compiler_params=_params(("parallel", "arbitrary")))(*args)


def _mm_tn(a, b, tn, name):
    m, k1 = a.shape
    n2 = b.shape[1]
    tm = 1024

    def body(a_ref, b_ref, o_ref):
        r = _dot_tn(a_ref[...], b_ref[...].astype(BF16))

        @pl.when(pl.program_id(1) == 0)
        def _():
            o_ref[...] = r

        @pl.when(pl.program_id(1) != 0)
        def _():
            o_ref[...] += r

    return pl.pallas_call(
        body, name=name, grid=(n2 // tn, m // tm),
        in_specs=[pl.BlockSpec((tm, k1), lambda j, i: (i, 0)), pl.BlockSpec((tm, tn), lambda j, i: (i, j))],
        out_specs=pl.BlockSpec((k1, tn), lambda j, i: (0, j)),
        out_shape=SDS((k1, n2), F32), compiler_params=_params(("parallel", "arbitrary")))(a, b)


def _ds(start, d):
    return pl.ds(start, QB) if d == 1 else pl.ds(start, QB, stride=d)


def _fill_bias(tab_ref, bidx_ref, bias_sc, hp):
    for g in range(3):
        bi = bidx_ref[g]
        for h in range(2):
            acc = jnp.zeros((QB, 2 * QB), F32)
            for j in range(N_BUCKET):
                acc = jnp.where(bi == j, tab_ref[j, g * N_HEAD + hp * 2 + h], acc)
            bias_sc[g * 2 + h] = acc


def _block_starts(it, d, nb):
    rho = it // nb
    n = it % nb
    st = rho + d * QB * n
    stp = rho + d * QB * jnp.maximum(n - 1, 0)
    return st, stp, n > 0


def _logits(q, kc, kp, bias_sc, gh, cur_ok, prev_ok, hasprev):
    sc = _dot_nt(q, kc) * SCALE + bias_sc[gh, :, QB:2 * QB]
    sp = _dot_nt(q, kp) * SCALE + bias_sc[gh, :, 0:QB]
    sc = jnp.where(cur_ok, sc, NEG)
    sp = jnp.where(jnp.logical_and(prev_ok, hasprev), sp, NEG)
    return sc, sp


def _attn_fwd(qkv3, rel_bias, bidx):
    bsz, s, _ = qkv3.shape
    rt = 256

    def body(tab_ref, bidx_ref, *refs):
        q_refs, k_refs, v_refs = refs[0:3], refs[3:6], refs[6:9]
        o_ref, lse_ref = refs[9:11]
        bias_sc, num_sc, den_sc, m_sc = refs[11:]
        hp = pl.program_id(1)
        _fill_bias(tab_ref, bidx_ref, bias_sc, hp)
        ii = lax.broadcasted_iota(jnp.int32, (QB, QB), 0)
        jj = lax.broadcasted_iota(jnp.int32, (QB, QB), 1)
        cur_ok, prev_ok = ii >= jj, jj >= ii
        for g, d in enumerate(DILATIONS):
            nb = s // (QB * d)

            def blk(it, c, g=g, d=d, nb=nb):
                st, stp, hasprev = _block_starts(it, d, nb)
                qf = q_refs[g][0, _ds(st, d), :]
                kcf, kpf = k_refs[g][0, _ds(st, d), :], k_refs[g][0, _ds(stp, d), :]
                vcf, vpf = v_refs[g][0, _ds(st, d), :], v_refs[g][0, _ds(stp, d), :]
                nums, dens, ms = [], [], []
                for h in range(2):
                    sl = slice(HEAD * h, HEAD * h + HEAD)
                    q = qf[:, sl].astype(BF16)
                    sc, sp = _logits(q, kcf[:, sl].astype(BF16), kpf[:, sl].astype(BF16), bias_sc, g * 2 + h, cur_ok, prev_ok, hasprev)
                    m = jnp.maximum(jnp.max(sc, axis=-1, keepdims=True), jnp.max(sp, axis=-1, keepdims=True))
                    pc, pp = jnp.exp(sc - m), jnp.exp(sp - m)
                    den = jnp.sum(pc, axis=-1, keepdims=True) + jnp.sum(pp, axis=-1, keepdims=True)
                    num = _dot(pc.astype(BF16), vcf[:, sl].astype(BF16)) + _dot(pp.astype(BF16), vpf[:, sl].astype(BF16))
                    nums.append(num)
                    dens.append(jnp.broadcast_to(den, (QB, HEAD)))
                    ms.append(jnp.broadcast_to(m, (QB, HEAD)))
                num_sc[g, _ds(st, d), :] = jnp.concatenate(nums, axis=1)
                den_sc[g, _ds(st, d), :] = jnp.concatenate(dens, axis=1)
                m_sc[g, _ds(st, d), :] = jnp.concatenate(ms, axis=1)
                return c

            lax.fori_loop(0, s // QB, blk, 0)

        def merge(i, c):
            rows = pl.ds(pl.multiple_of(i * rt, rt), rt)
            m0, m1, m2 = m_sc[0, rows, :], m_sc[1, rows, :], m_sc[2, rows, :]
            mall = jnp.maximum(jnp.maximum(m0, m1), m2)
            w0, w1, w2 = jnp.exp(m0 - mall), jnp.exp(m1 - mall), jnp.exp(m2 - mall)
            num = w0 * num_sc[0, rows, :] + w1 * num_sc[1, rows, :] + w2 * num_sc[2, rows, :]
            den = w0 * den_sc[0, rows, :] + w1 * den_sc[1, rows, :] + w2 * den_sc[2, rows, :]
            o_ref[0, rows, :] = num / den
            lse_ref[0, rows, :] = mall + jnp.log(den)
            return c

        lax.fori_loop(0, s // rt, merge, 0)

    col = lambda w, g: (lambda b, hp: (b, 0, (w * 3 + g) * 4 + hp))
    in_specs = [pl.BlockSpec(memory_space=pltpu.SMEM), pl.BlockSpec((3, QB, 2 * QB), lambda b, hp: (0, 0, 0))]
    in_specs += [pl.BlockSpec((1, s, LANE), col(w, g)) for w in range(3) for g in range(3)]
    out_spec = pl.BlockSpec((1, s, LANE), lambda b, hp: (b, 0, hp))
    return pl.pallas_call(
        body, name="attn_fwd", grid=(bsz, 4), in_specs=in_specs, out_specs=[out_spec, out_spec],
        out_shape=[SDS((bsz, s, WIDTH), F32), SDS((bsz, s, WIDTH), F32)],
        scratch_shapes=[pltpu.VMEM((6, QB, 2 * QB), F32), pltpu.VMEM((3, s, LANE), F32), pltpu.VMEM((3, s, LANE), F32),
                        pltpu.VMEM((3, s, LANE), F32)],
        compiler_params=_params(("parallel", "parallel")))(rel_bias, bidx, *([qkv3] * 9))


def _attn_bwd(qkv3, o3, lse3, do3, rel_bias, bidx):
    bsz, s, _ = qkv3.shape
    rt = 256

    def body(tab_ref, bidx_ref, *refs):
        q_refs, k_refs, v_refs = refs[0:3], refs[3:6], refs[6:9]
        o_ref, lse_ref, do_ref = refs[9:12]
        dq_refs, dk_refs, dv_refs = refs[12:15], refs[15:18], refs[18:21]
        db_ref = refs[21]
        bias_sc, delta_sc = refs[22:]
        hp, b = pl.program_id(0), pl.program_id(1)
        _fill_bias(tab_ref, bidx_ref, bias_sc, hp)

        @pl.when(b == 0)
        def _():
            db_ref[...] = jnp.zeros_like(db_ref)

        def prep(i, c):
            rows = pl.ds(pl.multiple_of(i * rt, rt), rt)
            prod = do_ref[0, rows, :] * o_ref[0, rows, :]
            d0 = jnp.sum(prod[:, :HEAD], axis=-1, keepdims=True)
            d1 = jnp.sum(prod[:, HEAD:], axis=-1, keepdims=True)
            delta_sc[rows, :] = jnp.concatenate([jnp.broadcast_to(d0, (rt, HEAD)), jnp.broadcast_to(d1, (rt, HEAD))], axis=1)
            z = jnp.zeros((rt, LANE), F32)
            for g in range(3):
                dk_refs[g][0, rows, :] = z
                dv_refs[g][0, rows, :] = z
            return c

        lax.fori_loop(0, s // rt, prep, 0)
        ii = lax.broadcasted_iota(jnp.int32, (QB, QB), 0)
        jj = lax.broadcasted_iota(jnp.int32, (QB, QB), 1)
        cur_ok, prev_ok = ii >= jj, jj >= ii
        for g, d in enumerate(DILATIONS):
            nb = s // (QB * d)

            def blk(it, c, g=g, d=d, nb=nb):
                st, stp, hasprev = _block_starts(it, d, nb)
                qf = q_refs[g][0, _ds(st, d), :]
                kcf, kpf = k_refs[g][0, _ds(st, d), :], k_refs[g][0, _ds(stp, d), :]
                vcf, vpf = v_refs[g][0, _ds(st, d), :], v_refs[g][0, _ds(stp, d), :]
                dof, lsef, delf = do_ref[0, _ds(st, d), :], lse_ref[0, _ds(st, d), :], delta_sc[_ds(st, d), :]
                dqs, dkcs, dkps, dvcs, dvps = [], [], [], [], []
                for h in range(2):
                    sl = slice(HEAD * h, HEAD * h + HEAD)
                    q = qf[:, sl].astype(BF16)
                    kc, kp = kcf[:, sl].astype(BF16), kpf[:, sl].astype(BF16)
                    vc, vp = vcf[:, sl].astype(BF16), vpf[:, sl].astype(BF16)
                    do = dof[:, sl].astype(BF16)
                    lse = lsef[:, HEAD * h:HEAD * h + 1]
                    delta = delf[:, HEAD * h:HEAD * h + 1]
                    sc, sp = _logits(q, kc, kp, bias_sc, g * 2 + h, cur_ok, prev_ok, hasprev)
                    pc, pp = jnp.exp(sc - lse), jnp.exp(sp - lse)
                    dvcs.append(_dot_tn(pc.astype(BF16), do))
                    dvps.append(_dot_tn(pp.astype(BF16), do))
                    dsc = pc * (_dot_nt(do, vc) - delta)
                    dsp = pp * (_dot_nt(do, vp) - delta)
                    db_ref[0, g * 2 + h, :, QB:2 * QB] += dsc
                    db_ref[0, g * 2 + h, :, 0:QB] += dsp
                    dscb, dspb = dsc.astype(BF16), dsp.astype(BF16)
                    dqs.append((_dot(dscb, kc) + _dot(dspb, kp)) * SCALE)
                    dkcs.append(_dot_tn(dscb, q) * SCALE)
                    dkps.append(_dot_tn(dspb, q) * SCALE)
                dq_refs[g][0, _ds(st, d), :] = jnp.concatenate(dqs, axis=1)
                dk_refs[g][0, _ds(st, d), :] += jnp.concatenate(dkcs, axis=1)
                dv_refs[g][0, _ds(st, d), :] += jnp.concatenate(dvcs, axis=1)
                dk_refs[g][0, _ds(stp, d), :] += jnp.concatenate(dkps, axis=1)
                dv_refs[g][0, _ds(stp, d), :] += jnp.concatenate(dvps, axis=1)
                return c

            lax.fori_loop(0, s // QB, blk, 0)

    col = lambda w, g: (lambda hp, b: (b, 0, (w * 3 + g) * 4 + hp))
    blk_spec = pl.BlockSpec((1, s, LANE), lambda hp, b: (b, 0, hp))
    in_specs = [pl.BlockSpec(memory_space=pltpu.SMEM), pl.BlockSpec((3, QB, 2 * QB), lambda hp, b: (0, 0, 0))]
    in_specs += [pl.BlockSpec((1, s, LANE), col(w, g)) for w in range(3) for g in range(3)]
    in_specs += [blk_spec] * 3
    out_specs = [blk_spec] * 9 + [pl.BlockSpec((1, 6, QB, 2 * QB), lambda hp, b: (hp, 0, 0, 0))]
    out_shape = [SDS((bsz, s, WIDTH), F32)] * 9 + [SDS((4, 6, QB, 2 * QB), F32)]
    outs = pl.pallas_call(
        body, name="attn_bwd", grid=(4, bsz), in_specs=in_specs, out_specs=out_specs, out_shape=out_shape,
        scratch_shapes=[pltpu.VMEM((6, QB, 2 * QB), F32), pltpu.VMEM((s, LANE), F32)],
        compiler_params=_params(("parallel", "arbitrary")))(rel_bias, bidx, *([qkv3] * 9), o3, lse3, do3)
    return outs[:9], outs[9]


def _bias_grad(dbias, bidx):
    def body(db_ref, bidx_ref, o_ref):
        lane = lax.broadcasted_iota(jnp.int32, (1, LANE), 1)
        for g in range(3):
            bi = bidx_ref[g]
            for hp in range(4):
                for h in range(2):
                    mat = db_ref[hp, g * 2 + h]
                    row = jnp.zeros((1, LANE), F32)
                    for j in range(N_BUCKET):
                        part = jnp.sum(jnp.where(bi == j, mat, 0.0), axis=0, keepdims=True)
                        row = jnp.where(lane == j, jnp.sum(part, axis=1, keepdims=True), row)
                    hd = g * N_HEAD + hp * 2 + h
                    o_ref[hd:hd + 1, :] = row

    return pl.pallas_call(body, name="bias_grad", out_shape=SDS((3 * N_HEAD, LANE), F32), compiler_params=_params())(dbias, bidx)


def _pre_fn(r, k0, v, wl, al, w0, wup, a0, aup, kk_, ka_):
    u = w0 + _bdot(jnp.tanh(wl), wup)
    lw = -jnp.exp(-_softplus(-u) - 0.5)
    a = jax.nn.sigmoid(a0 + _bdot(al, aup))
    kkraw = k0 * kk_
    k = k0 * (1.0 + (a - 1.0) * ka_)
    return r, lw, k, v, kkraw, a


PRE_SPLIT = (0, WIDTH, 2 * WIDTH, 3 * WIDTH, 3 * WIDTH + LORA, 3 * WIDTH + 2 * LORA)


def _pre_pieces(prs):
    return [prs[:, a:b] for a, b in zip(PRE_SPLIT[:-1], PRE_SPLIT[1:])]


PRE_TT = 512


def _shifted(pr_ref, edge_ref, first, back):
    pr = pr_ref[0]
    tt = pr.shape[0]
    row = lax.broadcasted_iota(jnp.int32, (tt, 1), 0)
    if back:
        edge = jnp.where(first, 0.0, edge_ref[0, 7:8, :])
        return jnp.where(row == 0, edge, pltpu.roll(pr, 1, axis=0))
    edge = jnp.where(first, 0.0, edge_ref[0, 0:1, :])
    return jnp.where(row == tt - 1, edge, pltpu.roll(pr, tt - 1, axis=0))


def _rwkv_pre(pr3, mix, w0, wup, a0, aup, kk_, ka_):
    bsz, s, _ = pr3.shape
    tt = PRE_TT

    def body(pr_ref, edge_ref, mix_ref, w0_ref, wup_ref, a0_ref, aup_ref, kk_ref, ka_ref, *outs):
        pr = pr_ref[0]
        prev = _shifted(pr_ref, edge_ref, pl.program_id(1) == 0, True)
        prs = pr + (prev - pr) * mix_ref[...]
        vals = _pre_fn(*_pre_pieces(prs), w0_ref[...], wup_ref[...].astype(F32), a0_ref[...], aup_ref[...].astype(F32), kk_ref[...],
                       ka_ref[...])
        for o, val in zip(outs, vals):
            o[0] = val

    vec = lambda n: pl.BlockSpec((1, n), lambda b, i: (0, 0))
    mat = pl.BlockSpec((LORA, WIDTH), lambda b, i: (0, 0))
    in_specs = [pl.BlockSpec((1, tt, PR_COLS), lambda b, i: (b, i, 0)),
                pl.BlockSpec((1, 8, PR_COLS), lambda b, i: (b, jnp.maximum(i * (tt // 8) - 1, 0), 0)),
                vec(PR_COLS), vec(WIDTH), mat, vec(WIDTH), mat, vec(WIDTH), vec(WIDTH)]
    out_spec = pl.BlockSpec((1, tt, WIDTH), lambda b, i: (b, i, 0))
    return pl.pallas_call(
        body, name="rwkv_pre", grid=(bsz, s // tt), in_specs=in_specs, out_specs=[out_spec] * 6,
        out_shape=[SDS((bsz, s, WIDTH), F32)] * 6, compiler_params=_params(("parallel", "parallel")))(
            pr3, pr3, mix, w0, wup, a0, aup, kk_, ka_)


def _rwkv_pre_bwd(pr3, cots, mix, w0, wup, a0, aup, kk_, ka_):
    bsz, s, _ = pr3.shape
    tt = PRE_TT

    def body(pr_ref, edge_ref, c0, c1, c2, c3, c4, c5, mix_ref, w0_ref, wup_ref, a0_ref, aup_ref, kk_ref, ka_ref,
             dprs_ref, dmix_ref, dw0_ref, dwup_ref, da0_ref, daup_ref, dkk_ref, dka_ref):
        pr = pr_ref[0]
        prev = _shifted(pr_ref, edge_ref, pl.program_id(1) == 0, True)
        prs = pr + (prev - pr) * mix_ref[...]
        _, vjp = jax.vjp(_pre_fn, *_pre_pieces(prs), w0_ref[...], wup_ref[...].astype(F32), a0_ref[...], aup_ref[...].astype(F32),
                         kk_ref[...], ka_ref[...])
        grads = vjp(tuple(c[0] for c in (c0, c1, c2, c3, c4, c5)))
        for piece, a, b in zip(grads[:5], PRE_SPLIT[:-1], PRE_SPLIT[1:]):
            dprs_ref[0, :, a:b] = piece
        dw0, dwup, da0, daup, dkk, dka = grads[5:]
        dprs = dprs_ref[0]
        grads = (jnp.sum(dprs * (prev - pr), axis=0, keepdims=True), dw0, dwup, da0, daup, dkk, dka)
        refs = (dmix_ref, dw0_ref, dwup_ref, da0_ref, daup_ref, dkk_ref, dka_ref)
        first = jnp.logical_and(pl.program_id(0) == 0, pl.program_id(1) == 0)

        @pl.when(first)
        def _():
            for r_, g_ in zip(refs, grads):
                r_[...] = g_

        @pl.when(jnp.logical_not(first))
        def _():
            for r_, g_ in zip(refs, grads):
                r_[...] += g_

    vec = lambda n: pl.BlockSpec((1, n), lambda b, i: (0, 0))
    mat = pl.BlockSpec((LORA, WIDTH), lambda b, i: (0, 0))
    tile = pl.BlockSpec((1, tt, WIDTH), lambda b, i: (b, i, 0))
    in_specs = [pl.BlockSpec((1, tt, PR_COLS), lambda b, i: (b, i, 0)),
                pl.BlockSpec((1, 8, PR_COLS), lambda b, i: (b, jnp.maximum(i * (tt // 8) - 1, 0), 0))]
    in_specs += [tile] * 6 + [vec(PR_COLS), vec(WIDTH), mat, vec(WIDTH), mat, vec(WIDTH), vec(WIDTH)]
    out_specs = [pl.BlockSpec((1, tt, PR_COLS), lambda b, i: (b, i, 0)), vec(PR_COLS), vec(WIDTH), mat, vec(WIDTH), mat,
                 vec(WIDTH), vec(WIDTH)]
    out_shape = [SDS((bsz, s, PR_COLS), F32), SDS((1, PR_COLS), F32), SDS((1, WIDTH), F32), SDS((LORA, WIDTH), F32),
                 SDS((1, WIDTH), F32), SDS((LORA, WIDTH), F32), SDS((1, WIDTH), F32), SDS((1, WIDTH), F32)]
    return pl.pallas_call(
        body, name="rwkv_pre_bwd", grid=(bsz, s // tt), in_specs=in_specs, out_specs=out_specs, out_shape=out_shape,
        compiler_params=_params(("arbitrary", "arbitrary")))(pr3, pr3, *cots, mix, w0, wup, a0, aup, kk_, ka_)


def _shift_bwd(dprs3, mix):
    bsz, s, _ = dprs3.shape
    tt = PRE_TT
    nt = s // tt

    def body(d_ref, edge_ref, mix_ref, o_ref):
        nxt = _shifted(d_ref, edge_ref, pl.program_id(1) == nt - 1, False)
        m = mix_ref[...]
        o_ref[0] = d_ref[0] * (1.0 - m) + nxt * m

    in_specs = [pl.BlockSpec((1, tt, PR_COLS), lambda b, i: (b, i, 0)),
                pl.BlockSpec((1, 8, PR_COLS), lambda b, i: (b, jnp.minimum((i + 1) * (tt // 8), s // 8 - 1), 0)),
                pl.BlockSpec((1, PR_COLS), lambda b, i: (0, 0))]
    return pl.pallas_call(
        body, name="shift_bwd", grid=(bsz, nt), in_specs=in_specs, out_specs=pl.BlockSpec((1, tt, PR_COLS), lambda b, i: (b, i, 0)),
        out_shape=SDS((bsz, s, PR_COLS), F32), compiler_params=_params(("parallel", "parallel")))(dprs3, dprs3, mix)


def _hdot(a, b):
    return jnp.dot(a, b, precision=HI3, preferred_element_type=F32)


def _hdot_nt(a, b):
    return lax.dot_general(a, b, (((1,), (1,)), ((), ())), precision=HI3, preferred_element_type=F32)


def _hdot_tn(a, b):
    return lax.dot_general(a, b, (((0,), (0,)), ((), ())), precision=HI3, preferred_element_type=F32)


def _bmm(a, b):
    return lax.dot_general(a, b, (((2,), (1,)), ((0,), (0,))), precision=HI3, preferred_element_type=F32)


def _bmm_nt(a, b):
    return lax.dot_general(a, b, (((2,), (2,)), ((0,), (0,))), precision=HI3, preferred_element_type=F32)


def _bmm_tn(a, b):
    return lax.dot_general(a, b, (((1,), (1,)), ((0,), (0,))), precision=HI3, preferred_element_type=F32)


def _chunk_fn(s0t, r, lw, k, v, kkraw, a, rk, lnw, lnb, first=False):
    c = r.shape[1]
    at, rt, btc, ktc, gc, aab, arb, xv, arkv, ain, bin_ = _chunk_core(r, lw, k, v, kkraw, a)
    rs = _bmm(jnp.concatenate([at, rt], axis=1), s0t)
    u = _solve(aab, rs[:, :c] + xv)
    y = rs[:, c:] + _bmm(arb, u) + arkv
    if first:
        y = _with_early_rows(y, r, lw, k, v, ain, bin_)
    gcol = jnp.sum(_diag(gc), axis=2, keepdims=True)
    sct = gcol * s0t + _bmm_tn(jnp.concatenate([btc, ktc], axis=1), jnp.concatenate([u, v], axis=1))
    return _post(y, r, k, v, rk, lnw, lnb), sct


def _diag(gc):
    return jnp.where(_masks(HEAD)[2], gc, 0.0)


def _with_early_rows(y, r, lw, k, v, ain, bin_):
    early = _stack([_early_rows(r[h], lw[h], k[h], v[h], ain[h], bin_[h]) for h in range(2)])
    return jnp.concatenate([jnp.concatenate([early, y[:2, EARLY:]], axis=1), y[2:]], axis=0)


def _early_rows(r, lw, k, v, ain, bin_):
    wc, bc, kc = jnp.transpose(jnp.exp(lw)), jnp.transpose(bin_), jnp.transpose(k)
    st = jnp.zeros((HEAD, HEAD), F32)
    rows = []
    for t in range(EARLY):
        sa = _bdot(ain[t:t + 1], st)
        st = st * wc[:, t:t + 1] + bc[:, t:t + 1] * sa + kc[:, t:t + 1] * v[t:t + 1]
        rows.append(_bdot(r[t:t + 1], st))
    return jnp.concatenate(rows, axis=0)


def _chunk_rows(c):
    return pl.ds(c * CHUNK, CHUNK) if isinstance(c, int) else pl.ds(pl.multiple_of(c * CHUNK, CHUNK), CHUNK)


def _stack(xs):
    return jnp.concatenate([x[None] for x in xs], axis=0)


def _pairs(ref, chunks):
    tiles = [ref[0, _chunk_rows(c), :] for c in chunks]
    return _stack([t[:, HEAD * h:HEAD * h + HEAD] for t in tiles for h in range(2)])


def _unpair(vals, j):
    return jnp.concatenate([vals[2 * j], vals[2 * j + 1]], axis=1)


def _masks(c):
    ii = lax.broadcasted_iota(jnp.int32, (c, c), 0)
    jj = lax.broadcasted_iota(jnp.int32, (c, c), 1)
    return ii > jj, ii >= jj, ii == jj


def _chunk_core(r, lw, k, v, kkraw, a):
    g_, c = r.shape[0], r.shape[1]
    nrm = jnp.sqrt(jnp.sum(kkraw * kkraw, axis=-1, keepdims=True))
    kkn = kkraw / jnp.maximum(nrm, 1e-12)
    ain, bin_ = -kkn, kkn * a
    strict, incl, _ = _masks(c)
    lg = lax.dot_general(jnp.broadcast_to(incl.astype(F32), (g_, c, c)), lw, (((2,), (1,)), ((0,), (0,))), precision=HI,
                         preferred_element_type=F32)
    g, gp, gi = jnp.exp(lg), jnp.exp(lg - lw), jnp.exp(-lg)
    at, rt, bt, kt = ain * gp, r * g, bin_ * gi, k * gi
    aa = _bmm_nt(jnp.concatenate([at, rt], axis=1), jnp.concatenate([bt, kt], axis=1))
    aab = jnp.where(strict, aa[:, :c, :c], 0.0)
    aak = jnp.where(strict, aa[:, :c, c:], 0.0)
    arb = jnp.where(incl, aa[:, c:, :c], 0.0)
    ark = jnp.where(incl, aa[:, c:, c:], 0.0)
    akv = _bmm(jnp.concatenate([aak, ark], axis=1), v)
    gc = g[:, c - 1:c, :]
    return at, rt, bt * gc, kt * gc, gc, aab, arb, akv[:, :c], akv[:, c:], ain, bin_


def _solve(aab, z):
    c = aab.shape[1]
    p = aab
    z = z + _bmm(p, z)
    n = 1
    while 2 * n < c:
        p = _bmm(p, p)
        z = z + _bmm(p, z)
        n *= 2
    return z


def _post(y, r, k, v, rk, lnw, lnb):
    mu = jnp.mean(y, axis=-1, keepdims=True)
    var = jnp.mean(jnp.square(y - mu), axis=-1, keepdims=True)
    yn = (y - mu) * lax.rsqrt(var + GN_EPS) * lnw + lnb
    return yn + jnp.sum(r * k * rk, axis=-1, keepdims=True) * v


def _chunk_consts(r, lw, k, v, kkraw, a, first=False):
    at, rt, btc, ktc, gc, aab, arb, xv, arkv, ain, bin_ = _chunk_core(r, lw, k, v, kkraw, a)
    z = _solve(aab, jnp.concatenate([at, xv], axis=2))
    ryv = jnp.concatenate([rt, arkv], axis=2) + _bmm(arb, z)
    if first:
        ryv = jnp.concatenate([ryv[:, :, :HEAD], _with_early_rows(ryv[:, :, HEAD:], r, lw, k, v, ain, bin_)], axis=2)
    mkv = _bmm_tn(btc, z) + jnp.concatenate([_diag(gc), _bmm_tn(ktc, v)], axis=2)
    return mkv, ryv


def _rwkv_scan(ins, rk, lnw, lnb):
    bsz, s, _ = ins[0].shape
    nch = s // CHUNK

    def consts_body(r_ref, lw_ref, k_ref, v_ref, kk_ref, a_ref, mkv_ref, ry_ref, yv_ref):
        def group(i, carry):
            chunks = [i * CHUNK_GROUP + j for j in range(CHUNK_GROUP)]
            mkv, ryv = _chunk_consts(*[_pairs(ref, chunks) for ref in (r_ref, lw_ref, k_ref, v_ref, kk_ref, a_ref)],
                                     first=isinstance(i, int) and i == 0)
            for j, c in enumerate(chunks):
                for h in range(2):
                    mkv_ref[0, 0, c, h] = mkv[2 * j + h]
                ry_ref[0, _chunk_rows(c), :] = jnp.concatenate([ryv[2 * j][:, :HEAD], ryv[2 * j + 1][:, :HEAD]], axis=1)
                yv_ref[0, _chunk_rows(c), :] = jnp.concatenate([ryv[2 * j][:, HEAD:], ryv[2 * j + 1][:, HEAD:]], axis=1)
            return carry

        group(0, 0)
        lax.fori_loop(1, nch // CHUNK_GROUP, group, 0)

    tile = pl.BlockSpec((1, s, LANE), lambda b, hp: (b, 0, hp))
    vec = pl.BlockSpec((1, LANE), lambda b, hp: (0, hp))
    mkv_spec = pl.BlockSpec((1, 1, nch, 2, HEAD, LANE), lambda b, hp: (b, hp, 0, 0, 0, 0))
    st_spec = pl.BlockSpec((1, 1, nch, 2, HEAD, HEAD), lambda b, hp: (b, hp, 0, 0, 0, 0))
    mkv, ry, yv = pl.pallas_call(
        consts_body, name="rwkv_consts", grid=(bsz, 4), in_specs=[tile] * 6, out_specs=[mkv_spec, tile, tile],
        out_shape=[SDS((bsz, 4, nch, 2, HEAD, LANE), F32), SDS((bsz, s, WIDTH), F32), SDS((bsz, s, WIDTH), F32)],
        compiler_params=_params(("parallel", "parallel")))(*ins)

    def scan_body(mkv_ref, ry_ref, yv_ref, r_ref, k_ref, v_ref, rk_ref, lnw_ref, lnb_ref, o_ref, st_ref, state):
        state[...] = jnp.zeros_like(state)

        def step(c, carry):
            rows = pl.ds(pl.multiple_of(c * CHUNK, CHUNK), CHUNK)
            ry, yv, r, k, v = (ref[0, rows, :] for ref in (ry_ref, yv_ref, r_ref, k_ref, v_ref))
            outs = []
            for h in range(2):
                sl = slice(HEAD * h, HEAD * h + HEAD)
                s0t = state[h]
                st_ref[0, 0, c, h] = s0t
                mkv_h = mkv_ref[0, 0, c, h]
                state[h] = _hdot(mkv_h[:, :HEAD], s0t) + mkv_h[:, HEAD:]
                y = _hdot(ry[:, sl], s0t) + yv[:, sl]
                outs.append(_post(y, r[:, sl], k[:, sl], v[:, sl], rk_ref[:, sl], lnw_ref[:, sl], lnb_ref[:, sl]))
            o_ref[0, rows, :] = jnp.concatenate(outs, axis=1)
            return carry

        lax.fori_loop(0, nch, step, 0)

    o, states = pl.pallas_call(
        scan_body, name="rwkv_scan", grid=(bsz, 4), in_specs=[mkv_spec] + [tile] * 5 + [vec] * 3, out_specs=[tile, st_spec],
        out_shape=[SDS((bsz, s, WIDTH), F32), SDS((bsz, 4, nch, 2, HEAD, HEAD), F32)],
        scratch_shapes=[pltpu.VMEM((2, HEAD, HEAD), F32)],
        compiler_params=_params(("parallel", "parallel")))(mkv, ry, yv, ins[0], ins[2], ins[3], rk, lnw, lnb)
    return o, states, (mkv, ry, yv)


def _rwkv_scan_bwd(ins, states, consts, do3, rk, lnw, lnb):
    bsz, s, _ = ins[0].shape
    nch = s // CHUNK

    mkv, ry, yv = consts
    tile_f = pl.BlockSpec((1, s, LANE), lambda b, hp: (b, 0, hp))
    vec_f = pl.BlockSpec((1, LANE), lambda b, hp: (0, hp))
    mkv_spec = pl.BlockSpec((1, 1, nch, 2, HEAD, LANE), lambda b, hp: (b, hp, 0, 0, 0, 0))
    st_spec_f = pl.BlockSpec((1, 1, nch, 2, HEAD, HEAD), lambda b, hp: (b, hp, 0, 0, 0, 0))

    def dstate_body(mkv_ref, ry_ref, yv_ref, r_ref, k_ref, v_ref, st_ref, do_ref, rk_ref, lnw_ref, lnb_ref, dst_ref, dstate):
        dstate[...] = jnp.zeros_like(dstate)

        def step(i, carry):
            c = nch - 1 - i
            rows = pl.ds(pl.multiple_of(c * CHUNK, CHUNK), CHUNK)
            ry_, yv_, r, k, v, do = (ref[0, rows, :] for ref in (ry_ref, yv_ref, r_ref, k_ref, v_ref, do_ref))
            for h in range(2):
                sl = slice(HEAD * h, HEAD * h + HEAD)
                dsct = dstate[h]
                dst_ref[0, 0, c, h] = dsct
                y = _hdot(ry_[:, sl], st_ref[0, 0, c, h]) + yv_[:, sl]
                _, vjp = jax.vjp(lambda y_: _post(y_, r[:, sl], k[:, sl], v[:, sl], rk_ref[:, sl], lnw_ref[:, sl], lnb_ref[:, sl]), y)
                (dy,) = vjp(do[:, sl])
                dstate[h] = _hdot_tn(mkv_ref[0, 0, c, h][:, :HEAD], dsct) + _hdot_tn(ry_[:, sl], dy)
            return carry

        lax.fori_loop(0, nch, step, 0)

    dstates = pl.pallas_call(
        dstate_body, name="rwkv_dstate", grid=(bsz, 4), in_specs=[mkv_spec] + [tile_f] * 5 + [st_spec_f, tile_f] + [vec_f] * 3,
        out_specs=st_spec_f, out_shape=SDS((bsz, 4, nch, 2, HEAD, HEAD), F32), scratch_shapes=[pltpu.VMEM((2, HEAD, HEAD), F32)],
        compiler_params=_params(("parallel", "parallel")))(mkv, ry, yv, ins[0], ins[2], ins[3], states, do3, rk, lnw, lnb)

    def body(r_ref, lw_ref, k_ref, v_ref, kk_ref, a_ref, st_ref, dst_ref, do_ref, rk_ref, lnw_ref, lnb_ref,
             dr_ref, dlw_ref, dk_ref, dv_ref, dkk_ref, da_ref, drk_ref, dlnw_ref, dlnb_ref):
        chunks = list(range(BWD_GROUP))
        par_refs = (drk_ref, dlnw_ref, dlnb_ref)

        @pl.when(jnp.logical_and(pl.program_id(1) == 0, pl.program_id(2) == 0))
        def _():
            for ref in par_refs:
                ref[...] = jnp.zeros_like(ref)

        def group(first):
            per_pair = lambda ref: _stack([ref[0, 0, c, h] for c in chunks for h in range(2)])
            vecs = [_stack([ref[:, HEAD * h:HEAD * h + HEAD] for _ in chunks for h in range(2)]) for ref in (rk_ref, lnw_ref, lnb_ref)]
            _, vjp = jax.vjp(functools.partial(_chunk_fn, first=first), per_pair(st_ref),
                             *[_pairs(ref, chunks) for ref in (r_ref, lw_ref, k_ref, v_ref, kk_ref, a_ref)], *vecs)
            grads = vjp((_pairs(do_ref, chunks), per_pair(dst_ref)))
            for ref, cot in zip((dr_ref, dlw_ref, dk_ref, dv_ref, dkk_ref, da_ref), grads[1:7]):
                for j, c in enumerate(chunks):
                    ref[0, _chunk_rows(c), :] = _unpair(cot, j)
            for ref, g_ in zip(par_refs, grads[7:10]):
                ref[...] += jnp.concatenate([sum(g_[2 * j + h] for j in range(BWD_GROUP)) for h in range(2)], axis=1)

        pl.when(pl.program_id(2) == 0)(functools.partial(group, True))
        pl.when(pl.program_id(2) != 0)(functools.partial(group, False))

    tt = BWD_GROUP * CHUNK
    tile = pl.BlockSpec((1, tt, LANE), lambda hp, b, t: (b, t, hp))
    vec = pl.BlockSpec((1, LANE), lambda hp, b, t: (0, hp))
    st_spec = pl.BlockSpec((1, 1, BWD_GROUP, 2, HEAD, HEAD), lambda hp, b, t: (b, hp, t, 0, 0, 0))
    outs = pl.pallas_call(
        body, name="rwkv_scan_bwd", grid=(4, bsz, s // tt), in_specs=[tile] * 6 + [st_spec, st_spec, tile] + [vec] * 3,
        out_specs=[tile] * 6 + [vec] * 3,
        out_shape=[SDS((bsz, s, WIDTH), F32)] * 6 + [SDS((1, WIDTH), F32)] * 3,
        compiler_params=_params(("parallel", "arbitrary", "arbitrary")))(*ins, states, dstates, do3, rk, lnw, lnb)
    return outs[:6], outs[6:]


def _head(o_attn, o_rwkv, z_attn, z_rwkv, gm, x2, tgt, wua, wur, wout, g2):
    n = x2.shape[0]
    tm = 256
    nt = n // tm
    d = D_MODEL

    def body(oa_ref, or_ref, za_ref, zr_ref, gm_ref, x_ref, t_ref, wua_ref, wur_ref, wout_ref, g2_ref,
             dxo_ref, doa_ref, dor_ref, dza_ref, dzr_ref, dgm_ref, dwua_ref, dwur_ref, dwout_ref, dg2_ref, loss_ref, lacc):
        i = pl.program_id(0)
        oa, orw, za, zr = oa_ref[...], or_ref[...], za_ref[...], zr_ref[...]
        ga, gb = gm_ref[:, 0:d], gm_ref[:, d:2 * d]
        am = (oa * _silu(za)).astype(BF16)
        bm = (orw * _silu(zr)).astype(BF16)
        ya, yb = _dot(am, wua_ref[...]), _dot(bm, wur_ref[...])
        sa, sb = jax.nn.sigmoid(ga), jax.nn.sigmoid(gb)
        merged = (sa * ya + sb * yb).astype(BF16)
        out = _dot(merged, wout_ref[...])
        rs = lax.rsqrt(jnp.mean(out * out, axis=-1, keepdims=True) + RMS_EPS)
        g2 = g2_ref[...]
        err = x_ref[...] + out * rs * g2 - t_ref[...]
        lpart = jnp.sum(err * err, axis=0, keepdims=True)
        dxo = err * (1.0 / d)
        dxo_ref[...] = dxo
        dg2 = jnp.sum(dxo * out * rs, axis=0, keepdims=True)
        gd = dxo * g2
        dout = (rs * (gd - out * (rs * rs) * jnp.mean(gd * out, axis=-1, keepdims=True))).astype(BF16)
        dmerged = _dot_nt(dout, wout_ref[...])
        dwout = _dot_tn(merged, dout)
        dya, dyb = (dmerged * sa).astype(BF16), (dmerged * sb).astype(BF16)
        dgm_ref[:, 0:d] = dmerged * ya * sa * (1.0 - sa)
        dgm_ref[:, d:2 * d] = dmerged * yb * sb * (1.0 - sb)
        dam, dbm = _dot_nt(dya, wua_ref[...]), _dot_nt(dyb, wur_ref[...])
        dwua, dwur = _dot_tn(am, dya), _dot_tn(bm, dyb)
        doa_ref[...] = dam * _silu(za)
        dza_ref[...] = dam * oa * _dsilu(za)
        dor_ref[...] = dbm * _silu(zr)
        dzr_ref[...] = dbm * orw * _dsilu(zr)

        @pl.when(i == 0)
        def _():
            dwua_ref[...], dwur_ref[...], dwout_ref[...], dg2_ref[...], lacc[...] = dwua, dwur, dwout, dg2, lpart

        @pl.when(i != 0)
        def _():
            dwua_ref[...] += dwua
            dwur_ref[...] += dwur
            dwout_ref[...] += dwout
            dg2_ref[...] += dg2
            lacc[...] += lpart

        @pl.when(i == nt - 1)
        def _():
            loss_ref[...] = jnp.sum(lacc[...], axis=1, keepdims=True) * (0.5 / d)

    t512 = pl.BlockSpec((tm, WIDTH), lambda i: (i, 0))
    t1k = pl.BlockSpec((tm, d), lambda i: (i, 0))
    t2k = pl.BlockSpec((tm, 2 * d), lambda i: (i, 0))
    full = lambda r, c: pl.BlockSpec((r, c), lambda i: (0, 0))
    return pl.pallas_call(
        body, name="head_fwd_bwd", grid=(nt,),
        in_specs=[t512, t512, t512, t512, t2k, t1k, t1k, full(WIDTH, d), full(WIDTH, d), full(d, d), full(1, d)],
        out_specs=[t1k, t512, t512, t512, t512, t2k, full(WIDTH, d), full(WIDTH, d), full(d, d), full(1, d), full(1, 1)],
        out_shape=[SDS((n, d), F32)] + [SDS((n, WIDTH), F32)] * 4 + [SDS((n, 2 * d), F32), SDS((WIDTH, d), F32), SDS((WIDTH, d), F32),
                                                                    SDS((d, d), F32), SDS((1, d), F32), SDS((1, 1), F32)],
        scratch_shapes=[pltpu.VMEM((1, d), F32)],
        compiler_params=_params(("arbitrary",)))(o_attn, o_rwkv, z_attn, z_rwkv, gm, x2, tgt, wua, wur, wout, g2)


def _prenorm_bwd(dh, x2, rs, g1, dxo):
    n, d = x2.shape
    tm = 1024

    def body(dh_ref, x_ref, rs_ref, g_ref, dxo_ref, gx_ref, dg_ref):
        x, r = x_ref[...], rs_ref[...]
        gd = dh_ref[...] * g_ref[...]
        gx_ref[...] = dxo_ref[...] + r * (gd - x * (r * r) * jnp.mean(gd * x, axis=-1, keepdims=True))
        dg = jnp.sum(dh_ref[...] * x * r, axis=0, keepdims=True)

        @pl.when(pl.program_id(0) == 0)
        def _():
            dg_ref[...] = dg

        @pl.when(pl.program_id(0) != 0)
        def _():
            dg_ref[...] += dg

    t = pl.BlockSpec((tm, d), lambda i: (i, 0))
    return pl.pallas_call(
        body, name="prenorm_bwd", grid=(n // tm,),
        in_specs=[t, t, pl.BlockSpec((tm, 1), lambda i: (i, 0)), pl.BlockSpec((1, d), lambda i: (0, 0)), t],
        out_specs=[t, pl.BlockSpec((1, d), lambda i: (0, 0))], out_shape=[SDS((n, d), F32), SDS((1, d), F32)],
        compiler_params=_params(("arbitrary",)))(dh, x2, rs, g1, dxo)


def _mesh_pos():
    x, y, c = lax.axis_index("x"), lax.axis_index("y"), lax.axis_index("c")
    return 4 * x + 2 * y + c


def _coords(idx):
    return (idx // 4, (idx // 2) % 2, idx % 2)


def _exchange(srcs, to_all, name):
    n = len(srcs)

    def body(*refs):
        src_refs, dst_refs = refs[:n], refs[n:2 * n]
        send_sems, recv_sems, local_sems = refs[2 * n:]
        me = _mesh_pos()

        def piece(i, j):
            return src_refs[i] if to_all[i] else src_refs[i].at[j]

        def remote(i, off, peer, block, slot):
            return pltpu.make_async_remote_copy(src_ref=piece(i, block), dst_ref=dst_refs[i].at[slot],
                                                send_sem=send_sems.at[i, off - 1], recv_sem=recv_sems.at[i, off - 1],
                                                device_id=_coords(peer), device_id_type=MESH)

        local = [pltpu.make_async_copy(piece(i, me), dst_refs[i].at[me], local_sems.at[i]) for i in range(n)]
        for cp in local:
            cp.start()
        sends = []
        for off in range(1, N_DEV):
            to = (me + off) % N_DEV
            for i in range(n):
                sends.append(remote(i, off, to, to, me))
                sends[-1].start()
        for off in range(1, N_DEV):
            frm = (me + N_DEV - off) % N_DEV
            for i in range(n):
                remote(i, off, frm, me, frm).wait_recv()
        for cp in sends:
            cp.wait_send()
        for cp in local:
            cp.wait()

    outs = pl.pallas_call(
        body, name=name, in_specs=[pl.BlockSpec(memory_space=pltpu.HBM)] * n, out_specs=[pl.BlockSpec(memory_space=pltpu.HBM)] * n,
        out_shape=[SDS((N_DEV,) + s.shape[-2:], s.dtype) for s in srcs],
        scratch_shapes=[pltpu.SemaphoreType.DMA((n, N_DEV - 1)), pltpu.SemaphoreType.DMA((n, N_DEV - 1)), pltpu.SemaphoreType.DMA((n,))],
        compiler_params=pltpu.CompilerParams())(*srcs)
    return outs


def _adamw(parts, w, m, v, tr, name):
    rows, cols = w.shape
    c1, c2 = 1.0 - ADAM_B1 ** ADAM_STEP, 1.0 - ADAM_B2 ** ADAM_STEP

    def body(p_ref, w_ref, m_ref, v_ref, g_ref, d_ref, nm_ref, nv_ref):
        g = p_ref[0].astype(F32)
        for j in range(1, N_DEV):
            g = g + p_ref[j].astype(F32)
        nm = ADAM_B1 * m_ref[...] + (1.0 - ADAM_B1) * g
        nv = ADAM_B2 * v_ref[...] + (1.0 - ADAM_B2) * jnp.square(g)
        g_ref[...] = g
        nm_ref[...] = nm
        nv_ref[...] = nv
        d_ref[...] = -ADAM_LR * ((nm / c1) / (jnp.sqrt(nv / c2) + ADAM_EPS) + ADAM_WD * w_ref[...])

    t = pl.BlockSpec((tr, cols), lambda i: (i, 0))
    return pl.pallas_call(
        body, name=name, grid=(rows // tr,), in_specs=[pl.BlockSpec((N_DEV, tr, cols), lambda i: (0, i, 0)), t, t, t],
        out_specs=[t] * 4, out_shape=[SDS((rows, cols), F32)] * 4, compiler_params=_params(("parallel",)))(parts, w, m, v)


SHARDED = (("w_in", D_MODEL, IN_COLS // N_DEV, True, 128), ("w_up_attn", WIDTH, D_MODEL // N_DEV, True, WIDTH),
           ("w_up_rwkv", WIDTH, D_MODEL // N_DEV, True, WIDTH), ("w_out", D_MODEL // N_DEV, D_MODEL, False, D_MODEL // N_DEV),
           ("rwkv_w_up", LORA, WIDTH // N_DEV, True, LORA), ("rwkv_a_up", LORA, WIDTH // N_DEV, True, LORA))
LOSS_SLOT = sum(n for _, n in SMALL)


def _pack_small(small, extra=None):
    flat = [small[n].reshape(-1).astype(F32) for n, _ in SMALL]
    flat.append(jnp.zeros((1,), F32) if extra is None else extra.reshape(1))
    flat.append(jnp.zeros((SMALL_ROWS * LANE - LOSS_SLOT - 1,), F32))
    return jnp.concatenate(flat).reshape(SMALL_ROWS, LANE)


def _unpack_small(packed, shapes):
    flat = packed.reshape(-1)
    out, off = {}, 0
    for n, cnt in SMALL:
        out[n] = flat[off:off + cnt].reshape(shapes[n])
        off += cnt
    return out, flat[LOSS_SLOT]


def _whole(gathered, by_cols):
    if not by_cols:
        return gathered.reshape(-1, gathered.shape[-1])
    return gathered.transpose(1, 0, 2).reshape(gathered.shape[1], -1)


def _per_owner(full, by_cols):
    if not by_cols:
        return full.reshape(N_DEV, -1, full.shape[-1])
    return full.reshape(full.shape[0], N_DEV, -1).transpose(1, 0, 2)


def _local_step(x, loss_target, sm, wts):
    bsz, s, d = x.shape
    n = bsz * s
    x2, tgt = x.reshape(n, d), loss_target.reshape(n, d)
    bidx = jnp.asarray(_bucket_tables())
    w_in = wts["w_in"]
    segs = (("qkv", 0, QKV_COLS, 512), ("za", OFF_ZA, WIDTH, 512), ("pr", OFF_PR, PR_COLS, PR_COLS), ("zr", OFF_ZR, WIDTH, 512),
            ("gm", OFF_GM, 2 * D_MODEL, 512))

    h, rs = _prenorm(x2, sm["pre_norm_gain"])
    proj = {nm: _mm(h, w_in[:, off:off + cnt], tn, "proj_" + nm) for nm, off, cnt, tn in segs}
    qkv3 = proj["qkv"].reshape(bsz, s, QKV_COLS)
    pr3 = proj["pr"].reshape(bsz, s, PR_COLS)

    o_attn, lse = _attn_fwd(qkv3, sm["rel_bias"], bidx)
    rk = sm["rwkv_r_k"].reshape(1, WIDTH)
    pre_args = (sm["rwkv_shift_mix"], sm["rwkv_w0"], wts["rwkv_w_up"], sm["rwkv_a0"], wts["rwkv_a_up"], sm["rwkv_k_k"], sm["rwkv_k_a"])
    scan_in = _rwkv_pre(pr3, *pre_args)
    o_rwkv, states, consts = _rwkv_scan(scan_in, rk, sm["rwkv_ln_w"], sm["rwkv_ln_b"])

    (dxo, do_attn, do_rwkv, dza, dzr, dgm, g_wua, g_wur, g_wout, g_post, loss) = _head(
        o_attn.reshape(n, WIDTH), o_rwkv.reshape(n, WIDTH), proj["za"], proj["zr"], proj["gm"], x2, tgt,
        wts["w_up_attn"], wts["w_up_rwkv"], wts["w_out"], sm["post_norm_gain"])

    dqkv, dbias = _attn_bwd(qkv3, o_attn, lse, do_attn.reshape(bsz, s, WIDTH), sm["rel_bias"], bidx)
    g_bias = _bias_grad(dbias, bidx)[:, :N_BUCKET].T

    scan_cots, (g_rk, g_lnw, g_lnb) = _rwkv_scan_bwd(scan_in, states, consts, do_rwkv.reshape(bsz, s, WIDTH), rk, sm["rwkv_ln_w"],
                                                     sm["rwkv_ln_b"])
    dprs, g_mix, g_w0, g_wup, g_a0, g_aup, g_kk, g_ka = _rwkv_pre_bwd(pr3, scan_cots, *pre_args)
    dpr = _shift_bwd(dprs, sm["rwkv_shift_mix"]).reshape(n, PR_COLS)

    dsegs = [(jnp.concatenate([t.reshape(n, WIDTH) for t in dqkv], axis=1), 0, 512), (dza, OFF_ZA, WIDTH), (dpr, OFF_PR, PR_COLS),
             (dzr, OFF_ZR, WIDTH), (dgm, OFF_GM, D_MODEL)]
    dh = None
    g_win = []
    for j, (t, off, tn) in enumerate(dsegs):
        dh = _mm_nt_acc(t, w_in[:, off:off + t.shape[1]], dh, "dh_%d" % j)
        g_win.append(_mm_tn(h, t, tn, "gw_in_%d" % j))
    grad_x, g_pre = _prenorm_bwd(dh, x2, rs, sm["pre_norm_gain"], dxo)

    full = {"w_in": jnp.concatenate(g_win, axis=1), "w_up_attn": g_wua, "w_up_rwkv": g_wur, "w_out": g_wout,
            "rwkv_w_up": g_wup, "rwkv_a_up": g_aup}
    small = {"pre_norm_gain": g_pre, "rel_bias": g_bias, "rwkv_shift_mix": g_mix, "rwkv_w0": g_w0, "rwkv_a0": g_a0, "rwkv_k_k": g_kk,
             "rwkv_k_a": g_ka, "rwkv_r_k": g_rk, "rwkv_ln_w": g_lnw, "rwkv_ln_b": g_lnb, "post_norm_gain": g_post}
    return loss[0, 0], grad_x.reshape(bsz, s, d), full, small


def kernel(x, pre_norm_gain, w_in, rel_bias, rwkv_shift_mix, rwkv_w0, rwkv_w_up, rwkv_a0, rwkv_a_up, rwkv_k_k, rwkv_k_a, rwkv_r_k, rwkv_ln_w, rwkv_ln_b, w_up_attn, w_up_rwkv, w_out, post_norm_gain, loss_target, m_pre_norm_gain, m_w_in, m_rel_bias, m_rwkv_shift_mix, m_rwkv_w0, m_rwkv_w_up, m_rwkv_a0, m_rwkv_a_up, m_rwkv_k_k, m_rwkv_k_a, m_rwkv_r_k, m_rwkv_ln_w, m_rwkv_ln_b, m_w_up_attn, m_w_up_rwkv, m_w_out, m_post_norm_gain, v_pre_norm_gain, v_w_in, v_rel_bias, v_rwkv_shift_mix, v_rwkv_w0, v_rwkv_w_up, v_rwkv_a0, v_rwkv_a_up, v_rwkv_k_k, v_rwkv_k_a, v_rwkv_r_k, v_rwkv_ln_w, v_rwkv_ln_b, v_w_up_attn, v_w_up_rwkv, v_w_out, v_post_norm_gain):
    names = [n for n, *_ in SHARDED] + [n for n, _ in SMALL]
    loc = dict(locals())
    w = {n: loc[n] for n in names}
    m = {n: loc["m_" + n] for n in names}
    v = {n: loc["v_" + n] for n in names}
    shapes = {n: w[n].shape for n in names}
    order = ["pre_norm_gain", "w_in", "rel_bias", "rwkv_shift_mix", "rwkv_w0", "rwkv_w_up", "rwkv_a0", "rwkv_a_up", "rwkv_k_k", "rwkv_k_a",
             "rwkv_r_k", "rwkv_ln_w", "rwkv_ln_b", "w_up_attn", "w_up_rwkv", "w_out", "post_norm_gain"]
    shard2d = lambda t, n, r, c: t[n].reshape(r, c)

    gathered = _exchange([shard2d(w, n, r, c).astype(BF16) for n, r, c, _, _ in SHARDED], [True] * len(SHARDED), "gather_weights")
    wts = {n: _whole(g, by_cols) for (n, _, _, by_cols, _), g in zip(SHARDED, gathered)}

    loss, grad_x, full, small = _local_step(x, loss_target, w, wts)
    parts = _exchange([_per_owner(full[n], by_cols).astype(BF16) for n, _, _, by_cols, _ in SHARDED] + [_pack_small(small, loss)],
                      [False] * len(SHARDED) + [True], "exchange_grads")

    outs = [{}, {}, {}, {}]
    for (n, r, c, _, tr), p in zip(SHARDED, parts):
        res = _adamw(p, shard2d(w, n, r, c), shard2d(m, n, r, c), shard2d(v, n, r, c), tr, "adamw_" + n)
        for o, t in zip(outs, res):
            o[n] = t.reshape(shapes[n])
    res = _adamw(parts[-1], _pack_small(w), _pack_small(m), _pack_small(v), SMALL_ROWS, "adamw_small")
    for o, t in zip(outs, res):
        o.update(_unpack_small(t, shapes)[0])
    loss = _unpack_small(res[0], shapes)[1]
    return (loss, grad_x, *[o[n] for o in outs for n in order])
```

```python
import functools
import math

import numpy as np
import jax
import jax.numpy as jnp
from jax import lax
from jax.experimental import pallas as pl
from jax.experimental.pallas import tpu as pltpu

F32, BF16 = jnp.float32, jnp.bfloat16
SDS = jax.ShapeDtypeStruct
HI = lax.Precision.HIGHEST
HI3 = lax.Precision.HIGH
MESH = pl.DeviceIdType.MESH

N_DEV = 8
D_MODEL = 1024
HEAD = 64
N_HEAD = 8
WIDTH = N_HEAD * HEAD
DILATIONS = (1, 4, 16)
QB = 128
N_BUCKET = 32
MAX_DIST = 2048
LORA = 64
QKV_COLS = 9 * WIDTH
PR_COLS = 3 * WIDTH + 2 * LORA
IN_COLS = QKV_COLS + WIDTH + PR_COLS + WIDTH + 2 * D_MODEL
OFF_ZA, OFF_PR, OFF_ZR, OFF_GM = QKV_COLS, QKV_COLS + WIDTH, QKV_COLS + WIDTH + PR_COLS, QKV_COLS + 2 * WIDTH + PR_COLS
RMS_EPS = 1e-6
GN_EPS = 64e-5
SCALE = 1.0 / math.sqrt(HEAD)
CHUNK = 64
CHUNK_GROUP = 4
BWD_GROUP = 4
EARLY = 8
NEG = -1e30
LANE = 128

ADAM_LR, ADAM_B1, ADAM_B2, ADAM_EPS, ADAM_WD, ADAM_STEP = 0.001, 0.9, 0.999, 1e-08, 0.01, 10

VMEM_LIMIT = 56 * 1024 * 1024

SMALL = (("pre_norm_gain", 1024), ("rel_bias", 768), ("rwkv_shift_mix", 1664), ("rwkv_w0", 512), ("rwkv_a0", 512),
         ("rwkv_k_k", 512), ("rwkv_k_a", 512), ("rwkv_r_k", 512), ("rwkv_ln_w", 512), ("rwkv_ln_b", 512),
         ("post_norm_gain", 1024))
SMALL_ROWS = 64


def _params(sem=None):
    return pltpu.CompilerParams(dimension_semantics=sem, vmem_limit_bytes=VMEM_LIMIT)


def _dot(a, b):
    return jnp.dot(a, b, preferred_element_type=F32)


def _dot_nt(a, b):
    return lax.dot_general(a, b, (((1,), (1,)), ((), ())), preferred_element_type=F32)


def _dot_tn(a, b):
    return lax.dot_general(a, b, (((0,), (0,)), ((), ())), preferred_element_type=F32)


@jax.custom_vjp
def _bdot(a, b):
    return _dot(a.astype(BF16), b.astype(BF16))


def _bdot_fwd(a, b):
    return _bdot(a, b), (a, b)


def _bdot_bwd(res, g):
    a, b = res
    gb = g.astype(BF16)
    return _dot_nt(gb, b.astype(BF16)), _dot_tn(a.astype(BF16), gb)


_bdot.defvjp(_bdot_fwd, _bdot_bwd)


def _silu(z):
    return z * jax.nn.sigmoid(z)


def _dsilu(z):
    s = jax.nn.sigmoid(z)
    return s * (1.0 + z * (1.0 - s))


def _softplus(x):
    return jnp.maximum(x, 0.0) + jnp.log(1.0 + jnp.exp(-jnp.abs(x)))


def _bucket_tables():
    qi = np.arange(QB)[:, None] + QB
    ki = np.arange(2 * QB)[None, :]
    rel = np.maximum(qi - ki, 0)
    out = []
    for d in DILATIONS:
        dist = rel * d
        max_exact = N_BUCKET // 2
        ratio = np.log(np.maximum(dist, 1).astype(np.float32) / max_exact) / np.float32(math.log(MAX_DIST / max_exact))
        large = max_exact + (ratio * (N_BUCKET - max_exact)).astype(np.int32)
        large = np.minimum(large, N_BUCKET - 1)
        out.append(np.where(dist < max_exact, dist, large).astype(np.int32))
    return np.stack(out)


def _prenorm(x2, g):
    n, d = x2.shape
    tm = 1024

    def body(x_ref, g_ref, h_ref, rs_ref):
        x = x_ref[...]
        rs = lax.rsqrt(jnp.mean(x * x, axis=-1, keepdims=True) + RMS_EPS)
        h_ref[...] = (x * rs * g_ref[...]).astype(BF16)
        rs_ref[...] = rs

    return pl.pallas_call(
        body, name="prenorm", grid=(n // tm,),
        in_specs=[pl.BlockSpec((tm, d), lambda i: (i, 0)), pl.BlockSpec((1, d), lambda i: (0, 0))],
        out_specs=[pl.BlockSpec((tm, d), lambda i: (i, 0)), pl.BlockSpec((tm, 1), lambda i: (i, 0))],
        out_shape=[SDS((n, d), BF16), SDS((n, 1), F32)], compiler_params=_params(("parallel",)))(x2, g)


def _mm(a, b, tn, name):
    m, k = a.shape
    n = b.shape[1]
    tm = 1024

    def body(a_ref, b_ref, o_ref):
        o_ref[...] = _dot(a_ref[...], b_ref[...])

    return pl.pallas_call(
        body, name=name, grid=(n // tn, m // tm),
        in_specs=[pl.BlockSpec((tm, k), lambda j, i: (i, 0)), pl.BlockSpec((k, tn), lambda j, i: (0, j))],
        out_specs=pl.BlockSpec((tm, tn), lambda j, i: (i, j)),
        out_shape=SDS((m, n), F32), compiler_params=_params(("parallel", "parallel")))(a, b)


def _mm_nt_acc(a, b, acc, name):
    m, k = a.shape
    d = b.shape[0]
    tm = 1024
    tk = k if k <= 2048 else 1536
    have_acc = acc is not None

    def body(*refs):
        if have_acc:
            a_ref, b_ref, c_ref, o_ref = refs
        else:
            a_ref, b_ref, o_ref = refs
        r = _dot_nt(a_ref[...].astype(BF16), b_ref[...])

        @pl.when(pl.program_id(1) == 0)
        def _():
            o_ref[...] = r + c_ref[...] if have_acc else r

        @pl.when(pl.program_id(1) != 0)
        def _():
            o_ref[...] += r

    in_specs = [pl.BlockSpec((tm, tk), lambda i, j: (i, j)), pl.BlockSpec((d, tk), lambda i, j: (0, j))]
    args = [a, b]
    if have_acc:
        in_specs.append(pl.BlockSpec((tm, d), lambda i, j: (i, 0)))
        args.append(acc)
    return pl.pallas_call(
        body, name=name, grid=(m // tm, k // tk), in_specs=in_specs, out_specs=pl.BlockSpec((tm, d), lambda i, j: (i, 0)),
        out_shape=SDS((m, d), F32), compiler_params=_params(("parallel", "arbitrary")))(*args)


def _mm_tn(a, b, tn, name):
    m, k1 = a.shape
    n2 = b.shape[1]
    tm = 1024

    def body(a_ref, b_ref, o_ref):
        r = _dot_tn(a_ref[...], b_ref[...].astype(BF16))

        @pl.when(pl.program_id(1) == 0)
        def _():
            o_ref[...] = r

        @pl.when(pl.program_id(1) != 0)
        def _():
            o_ref[...] += r

    return pl.pallas_call(
        body, name=name, grid=(n2 // tn, m // tm),
        in_specs=[pl.BlockSpec((tm, k1), lambda j, i: (i, 0)), pl.BlockSpec((tm, tn), lambda j, i: (i, j))],
        out_specs=pl.BlockSpec((k1, tn), lambda j, i: (0, j)),
        out_shape=SDS((k1, n2), F32), compiler_params=_params(("parallel", "arbitrary")))(a, b)


def _ds(start, d):
    return pl.ds(start, QB) if d == 1 else pl.ds(start, QB, stride=d)


def _fill_bias(tab_ref, bidx_ref, bias_sc, hp):
    for g in range(3):
        bi = bidx_ref[g]
        for h in range(2):
            acc = jnp.zeros((QB, 2 * QB), F32)
            for j in range(N_BUCKET):
                acc = jnp.where(bi == j, tab_ref[j, g * N_HEAD + hp * 2 + h], acc)
            bias_sc[g * 2 + h] = acc


def _block_starts(it, d, nb):
    rho = it // nb
    n = it % nb
    st = rho + d * QB * n
    stp = rho + d * QB * jnp.maximum(n - 1, 0)
    return st, stp, n > 0


ATTN_BLOCKS = 2


def _bdot3(a, b, dims):
    return lax.dot_general(a, b, (dims, ((0,), (0,))), preferred_element_type=F32)


def _attn_operands(q_ref, k_ref, v_ref, bias_sc, g, d, nb, it0):
    ii = lax.broadcasted_iota(jnp.int32, (QB, 2 * QB), 0)
    cc = lax.broadcasted_iota(jnp.int32, (QB, 2 * QB), 1)
    qs, ks, vs, pens, starts = [], [], [], [], []
    for u in range(ATTN_BLOCKS):
        st, stp, hasprev = _block_starts(it0 + u, d, nb)
        qf = q_ref[0, _ds(st, d), :]
        kf = jnp.concatenate([k_ref[0, _ds(stp, d), :], k_ref[0, _ds(st, d), :]], axis=0)
        vf = jnp.concatenate([v_ref[0, _ds(stp, d), :], v_ref[0, _ds(st, d), :]], axis=0)
        own = jnp.logical_and(cc >= QB, ii >= cc - QB)
        prev = jnp.logical_and(jnp.logical_and(cc < QB, cc >= ii), hasprev)
        pen = jnp.where(jnp.logical_or(own, prev), 0.0, NEG)
        for h in range(2):
            sl = slice(HEAD * h, HEAD * h + HEAD)
            qs.append(qf[:, sl])
            ks.append(kf[:, sl])
            vs.append(vf[:, sl])
            pens.append(pen + bias_sc[g * 2 + h])
        starts.append((st, stp))
    return _stack(qs).astype(BF16), _stack(ks).astype(BF16), _stack(vs).astype(BF16), _stack(pens), starts


def _heads(x, u):
    return jnp.concatenate([x[2 * u], x[2 * u + 1]], axis=1)


def _attn_fwd(qkv3, rel_bias, bidx):
    bsz, s, _ = qkv3.shape
    rt = 256

    def body(tab_ref, bidx_ref, *refs):
        q_refs, k_refs, v_refs = refs[0:3], refs[3:6], refs[6:9]
        o_ref, lse_ref = refs[9:11]
        bias_sc, num_sc, den_sc, m_sc = refs[11:]
        pl.when(pl.program_id(1) == 0)(lambda: _fill_bias(tab_ref, bidx_ref, bias_sc, pl.program_id(0)))
        for g, d in enumerate(DILATIONS):
            nb = s // (QB * d)

            def blk(it, c, g=g, d=d, nb=nb):
                q, k, v, bias, starts = _attn_operands(q_refs[g], k_refs[g], v_refs[g], bias_sc, g, d, nb, it * ATTN_BLOCKS)
                sc = _bdot3(q, k, ((2,), (2,))) * SCALE + bias
                m = jnp.max(sc, axis=-1, keepdims=True)
                p = jnp.exp(sc - m)
                den = jnp.sum(p, axis=-1, keepdims=True)
                num = _bdot3(p.astype(BF16), v, ((2,), (1,)))
                den, m = jnp.broadcast_to(den, num.shape), jnp.broadcast_to(m, num.shape)
                for u, (st, _) in enumerate(starts):
                    num_sc[g, _ds(st, d), :] = _heads(num, u)
                    den_sc[g, _ds(st, d), :] = _heads(den, u)
                    m_sc[g, _ds(st, d), :] = _heads(m, u)
                return c

            lax.fori_loop(0, s // QB // ATTN_BLOCKS, blk, 0)

        def merge(i, c):
            rows = pl.ds(pl.multiple_of(i * rt, rt), rt)
            m0, m1, m2 = m_sc[0, rows, :], m_sc[1, rows, :], m_sc[2, rows, :]
            mall = jnp.maximum(jnp.maximum(m0, m1), m2)
            w0, w1, w2 = jnp.exp(m0 - mall), jnp.exp(m1 - mall), jnp.exp(m2 - mall)
            num = w0 * num_sc[0, rows, :] + w1 * num_sc[1, rows, :] + w2 * num_sc[2, rows, :]
            den = w0 * den_sc[0, rows, :] + w1 * den_sc[1, rows, :] + w2 * den_sc[2, rows, :]
            o_ref[0, rows, :] = num / den
            lse_ref[0, rows, :] = mall + jnp.log(den)
            return c

        lax.fori_loop(0, s // rt, merge, 0)

    col = lambda w, g: (lambda hp, b: (b, 0, (w * 3 + g) * 4 + hp))
    in_specs = [pl.BlockSpec(memory_space=pltpu.SMEM), pl.BlockSpec((3, QB, 2 * QB), lambda hp, b: (0, 0, 0))]
    in_specs += [pl.BlockSpec((1, s, LANE), col(w, g)) for w in range(3) for g in range(3)]
    out_spec = pl.BlockSpec((1, s, LANE), lambda hp, b: (b, 0, hp))
    return pl.pallas_call(
        body, name="attn_fwd", grid=(4, bsz), in_specs=in_specs, out_specs=[out_spec, out_spec],
        out_shape=[SDS((bsz, s, WIDTH), F32), SDS((bsz, s, WIDTH), F32)],
        scratch_shapes=[pltpu.VMEM((6, QB, 2 * QB), F32), pltpu.VMEM((3, s, LANE), F32), pltpu.VMEM((3, s, LANE), F32),
                        pltpu.VMEM((3, s, LANE), F32)],
        compiler_params=_params(("arbitrary", "arbitrary")))(rel_bias, bidx, *([qkv3] * 9))


def _attn_bwd(qkv3, o3, lse3, do3, rel_bias, bidx):
    bsz, s, _ = qkv3.shape
    rt = 256

    def body(tab_ref, bidx_ref, *refs):
        q_refs, k_refs, v_refs = refs[0:3], refs[3:6], refs[6:9]
        o_ref, lse_ref, do_ref = refs[9:12]
        dq_refs, dk_refs, dv_refs = refs[12:15], refs[15:18], refs[18:21]
        db_ref = refs[21]
        bias_sc, delta_sc = refs[22:]
        @pl.when(pl.program_id(1) == 0)
        def _():
            _fill_bias(tab_ref, bidx_ref, bias_sc, pl.program_id(0))
            db_ref[...] = jnp.zeros_like(db_ref)

        def prep(i, c):
            rows = pl.ds(pl.multiple_of(i * rt, rt), rt)
            prod = do_ref[0, rows, :] * o_ref[0, rows, :]
            d0 = jnp.sum(prod[:, :HEAD], axis=-1, keepdims=True)
            d1 = jnp.sum(prod[:, HEAD:], axis=-1, keepdims=True)
            delta_sc[rows, :] = jnp.concatenate([jnp.broadcast_to(d0, (rt, HEAD)), jnp.broadcast_to(d1, (rt, HEAD))], axis=1)
            z = jnp.zeros((rt, LANE), F32)
            for g in range(3):
                dk_refs[g][0, rows, :] = z
                dv_refs[g][0, rows, :] = z
            return c

        lax.fori_loop(0, s // rt, prep, 0)
        for g, d in enumerate(DILATIONS):
            nb = s // (QB * d)

            def blk(it, c, g=g, d=d, nb=nb):
                q, k, v, bias, starts = _attn_operands(q_refs[g], k_refs[g], v_refs[g], bias_sc, g, d, nb, it * ATTN_BLOCKS)
                dos, lses, deltas = [], [], []
                for st, _ in starts:
                    dof, lsef, delf = do_ref[0, _ds(st, d), :], lse_ref[0, _ds(st, d), :], delta_sc[_ds(st, d), :]
                    for h in range(2):
                        dos.append(dof[:, HEAD * h:HEAD * h + HEAD])
                        lses.append(lsef[:, HEAD * h:HEAD * h + 1])
                        deltas.append(delf[:, HEAD * h:HEAD * h + 1])
                do, lse, delta = _stack(dos).astype(BF16), _stack(lses), _stack(deltas)
                p = jnp.exp(_bdot3(q, k, ((2,), (2,))) * SCALE + bias - lse)
                dv = _bdot3(p.astype(BF16), do, ((1,), (1,)))
                ds = p * (_bdot3(do, v, ((2,), (2,))) - delta)
                dsb = ds.astype(BF16)
                dq = _bdot3(dsb, k, ((2,), (1,))) * SCALE
                dk = _bdot3(dsb, q, ((1,), (1,))) * SCALE
                for h in range(2):
                    db_ref[0, g * 2 + h] += sum(ds[2 * u + h] for u in range(ATTN_BLOCKS))
                for u, (st, stp) in enumerate(starts):
                    dq_refs[g][0, _ds(st, d), :] = _heads(dq, u)
                    dk_refs[g][0, _ds(stp, d), :] += _heads(dk[:, :QB], u)
                    dv_refs[g][0, _ds(stp, d), :] += _heads(dv[:, :QB], u)
                    dk_refs[g][0, _ds(st, d), :] += _heads(dk[:, QB:], u)
                    dv_refs[g][0, _ds(st, d), :] += _heads(dv[:, QB:], u)
                return c

            lax.fori_loop(0, s // QB // ATTN_BLOCKS, blk, 0)

    col = lambda w, g: (lambda hp, b: (b, 0, (w * 3 + g) * 4 + hp))
    blk_spec = pl.BlockSpec((1, s, LANE), lambda hp, b: (b, 0, hp))
    in_specs = [pl.BlockSpec(memory_space=pltpu.SMEM), pl.BlockSpec((3, QB, 2 * QB), lambda hp, b: (0, 0, 0))]
    in_specs += [pl.BlockSpec((1, s, LANE), col(w, g)) for w in range(3) for g in range(3)]
    in_specs += [blk_spec] * 3
    out_specs = [blk_spec] * 9 + [pl.BlockSpec((1, 6, QB, 2 * QB), lambda hp, b: (hp, 0, 0, 0))]
    out_shape = [SDS((bsz, s, WIDTH), F32)] * 9 + [SDS((4, 6, QB, 2 * QB), F32)]
    outs = pl.pallas_call(
        body, name="attn_bwd", grid=(4, bsz), in_specs=in_specs, out_specs=out_specs, out_shape=out_shape,
        scratch_shapes=[pltpu.VMEM((6, QB, 2 * QB), F32), pltpu.VMEM((s, LANE), F32)],
        compiler_params=_params(("parallel", "arbitrary")))(rel_bias, bidx, *([qkv3] * 9), o3, lse3, do3)
    return outs[:9], outs[9]


def _bias_grad(dbias, bidx):
    def body(db_ref, bidx_ref, o_ref):
        lane = lax.broadcasted_iota(jnp.int32, (1, LANE), 1)
        for g in range(3):
            bi = bidx_ref[g]
            for hp in range(4):
                for h in range(2):
                    mat = db_ref[hp, g * 2 + h]
                    row = jnp.zeros((1, LANE), F32)
                    for j in range(N_BUCKET):
                        part = jnp.sum(jnp.where(bi == j, mat, 0.0), axis=0, keepdims=True)
                        row = jnp.where(lane == j, jnp.sum(part, axis=1, keepdims=True), row)
                    hd = g * N_HEAD + hp * 2 + h
                    o_ref[hd:hd + 1, :] = row

    return pl.pallas_call(body, name="bias_grad", out_shape=SDS((3 * N_HEAD, LANE), F32), compiler_params=_params())(dbias, bidx)


def _pre_fn(r, k0, v, wl, al, w0, wup, a0, aup, kk_, ka_):
    u = w0 + _bdot(jnp.tanh(wl), wup)
    lw = -jnp.exp(-_softplus(-u) - 0.5)
    a = jax.nn.sigmoid(a0 + _bdot(al, aup))
    kkraw = k0 * kk_
    k = k0 * (1.0 + (a - 1.0) * ka_)
    return r, lw, k, v, kkraw, a


PRE_SPLIT = (0, WIDTH, 2 * WIDTH, 3 * WIDTH, 3 * WIDTH + LORA, 3 * WIDTH + 2 * LORA)


def _pre_pieces(prs):
    return [prs[:, a:b] for a, b in zip(PRE_SPLIT[:-1], PRE_SPLIT[1:])]


PRE_TT = 512


def _shifted(pr_ref, edge_ref, first, back):
    pr = pr_ref[0]
    tt = pr.shape[0]
    row = lax.broadcasted_iota(jnp.int32, (tt, 1), 0)
    if back:
        edge = jnp.where(first, 0.0, edge_ref[0, 7:8, :])
        return jnp.where(row == 0, edge, pltpu.roll(pr, 1, axis=0))
    edge = jnp.where(first, 0.0, edge_ref[0, 0:1, :])
    return jnp.where(row == tt - 1, edge, pltpu.roll(pr, tt - 1, axis=0))


def _rwkv_pre(pr3, mix, w0, wup, a0, aup, kk_, ka_):
    bsz, s, _ = pr3.shape
    tt = PRE_TT

    def body(pr_ref, edge_ref, mix_ref, w0_ref, wup_ref, a0_ref, aup_ref, kk_ref, ka_ref, *outs):
        pr = pr_ref[0]
        prev = _shifted(pr_ref, edge_ref, pl.program_id(1) == 0, True)
        prs = pr + (prev - pr) * mix_ref[...]
        vals = _pre_fn(*_pre_pieces(prs), w0_ref[...], wup_ref[...].astype(F32), a0_ref[...], aup_ref[...].astype(F32), kk_ref[...],
                       ka_ref[...])
        for o, val in zip(outs, vals):
            o[0] = val

    vec = lambda n: pl.BlockSpec((1, n), lambda b, i: (0, 0))
    mat = pl.BlockSpec((LORA, WIDTH), lambda b, i: (0, 0))
    in_specs = [pl.BlockSpec((1, tt, PR_COLS), lambda b, i: (b, i, 0)),
                pl.BlockSpec((1, 8, PR_COLS), lambda b, i: (b, jnp.maximum(i * (tt // 8) - 1, 0), 0)),
                vec(PR_COLS), vec(WIDTH), mat, vec(WIDTH), mat, vec(WIDTH), vec(WIDTH)]
    out_spec = pl.BlockSpec((1, tt, WIDTH), lambda b, i: (b, i, 0))
    return pl.pallas_call(
        body, name="rwkv_pre", grid=(bsz, s // tt), in_specs=in_specs, out_specs=[out_spec] * 6,
        out_shape=[SDS((bsz, s, WIDTH), F32)] * 6, compiler_params=_params(("parallel", "parallel")))(
            pr3, pr3, mix, w0, wup, a0, aup, kk_, ka_)


def _rwkv_pre_bwd(pr3, cots, mix, w0, wup, a0, aup, kk_, ka_):
    bsz, s, _ = pr3.shape
    tt = PRE_TT

    def body(pr_ref, edge_ref, c0, c1, c2, c3, c4, c5, mix_ref, w0_ref, wup_ref, a0_ref, aup_ref, kk_ref, ka_ref,
             dprs_ref, dmix_ref, dw0_ref, dwup_ref, da0_ref, daup_ref, dkk_ref, dka_ref):
        pr = pr_ref[0]
        prev = _shifted(pr_ref, edge_ref, pl.program_id(1) == 0, True)
        prs = pr + (prev - pr) * mix_ref[...]
        _, vjp = jax.vjp(_pre_fn, *_pre_pieces(prs), w0_ref[...], wup_ref[...].astype(F32), a0_ref[...], aup_ref[...].astype(F32),
                         kk_ref[...], ka_ref[...])
        grads = vjp(tuple(c[0] for c in (c0, c1, c2, c3, c4, c5)))
        for piece, a, b in zip(grads[:5], PRE_SPLIT[:-1], PRE_SPLIT[1:]):
            dprs_ref[0, :, a:b] = piece
        dw0, dwup, da0, daup, dkk, dka = grads[5:]
        dprs = dprs_ref[0]
        grads = (jnp.sum(dprs * (prev - pr), axis=0, keepdims=True), dw0, dwup, da0, daup, dkk, dka)
        refs = (dmix_ref, dw0_ref, dwup_ref, da0_ref, daup_ref, dkk_ref, dka_ref)
        first = jnp.logical_and(pl.program_id(0) == 0, pl.program_id(1) == 0)

        @pl.when(first)
        def _():
            for r_, g_ in zip(refs, grads):
                r_[...] = g_

        @pl.when(jnp.logical_not(first))
        def _():
            for r_, g_ in zip(refs, grads):
                r_[...] += g_

    vec = lambda n: pl.BlockSpec((1, n), lambda b, i: (0, 0))
    mat = pl.BlockSpec((LORA, WIDTH), lambda b, i: (0, 0))
    tile = pl.BlockSpec((1, tt, WIDTH), lambda b, i: (b, i, 0))
    in_specs = [pl.BlockSpec((1, tt, PR_COLS), lambda b, i: (b, i, 0)),
                pl.BlockSpec((1, 8, PR_COLS), lambda b, i: (b, jnp.maximum(i * (tt // 8) - 1, 0), 0))]
    in_specs += [tile] * 6 + [vec(PR_COLS), vec(WIDTH), mat, vec(WIDTH), mat, vec(WIDTH), vec(WIDTH)]
    out_specs = [pl.BlockSpec((1, tt, PR_COLS), lambda b, i: (b, i, 0)), vec(PR_COLS), vec(WIDTH), mat, vec(WIDTH), mat,
                 vec(WIDTH), vec(WIDTH)]
    out_shape = [SDS((bsz, s, PR_COLS), F32), SDS((1, PR_COLS), F32), SDS((1, WIDTH), F32), SDS((LORA, WIDTH), F32),
                 SDS((1, WIDTH), F32), SDS((LORA, WIDTH), F32), SDS((1, WIDTH), F32), SDS((1, WIDTH), F32)]
    return pl.pallas_call(
        body, name="rwkv_pre_bwd", grid=(bsz, s // tt), in_specs=in_specs, out_specs=out_specs, out_shape=out_shape,
        compiler_params=_params(("arbitrary", "arbitrary")))(pr3, pr3, *cots, mix, w0, wup, a0, aup, kk_, ka_)


def _shift_bwd(dprs3, mix):
    bsz, s, _ = dprs3.shape
    tt = PRE_TT
    nt = s // tt

    def body(d_ref, edge_ref, mix_ref, o_ref):
        nxt = _shifted(d_ref, edge_ref, pl.program_id(1) == nt - 1, False)
        m = mix_ref[...]
        o_ref[0] = d_ref[0] * (1.0 - m) + nxt * m

    in_specs = [pl.BlockSpec((1, tt, PR_COLS), lambda b, i: (b, i, 0)),
                pl.BlockSpec((1, 8, PR_COLS), lambda b, i: (b, jnp.minimum((i + 1) * (tt // 8), s // 8 - 1), 0)),
                pl.BlockSpec((1, PR_COLS), lambda b, i: (0, 0))]
    return pl.pallas_call(
        body, name="shift_bwd", grid=(bsz, nt), in_specs=in_specs, out_specs=pl.BlockSpec((1, tt, PR_COLS), lambda b, i: (b, i, 0)),
        out_shape=SDS((bsz, s, PR_COLS), F32), compiler_params=_params(("parallel", "parallel")))(dprs3, dprs3, mix)


def _hdot(a, b):
    return jnp.dot(a, b, precision=HI3, preferred_element_type=F32)


def _hdot_nt(a, b):
    return lax.dot_general(a, b, (((1,), (1,)), ((), ())), precision=HI3, preferred_element_type=F32)


def _hdot_tn(a, b):
    return lax.dot_general(a, b, (((0,), (0,)), ((), ())), precision=HI3, preferred_element_type=F32)


def _bmm(a, b):
    return lax.dot_general(a, b, (((2,), (1,)), ((0,), (0,))), precision=HI3, preferred_element_type=F32)


def _bmm_nt(a, b):
    return lax.dot_general(a, b, (((2,), (2,)), ((0,), (0,))), precision=HI3, preferred_element_type=F32)


def _bmm_tn(a, b):
    return lax.dot_general(a, b, (((1,), (1,)), ((0,), (0,))), precision=HI3, preferred_element_type=F32)


def _chunk_fn(s0t, r, lw, k, v, kkraw, a, rk, lnw, lnb, first=False):
    c = r.shape[1]
    at, rt, btc, ktc, gc, aab, arb, xv, arkv, ain, bin_ = _chunk_core(r, lw, k, v, kkraw, a)
    rs = _bmm(jnp.concatenate([at, rt], axis=1), s0t)
    u = _solve(aab, rs[:, :c] + xv)
    y = rs[:, c:] + _bmm(arb, u) + arkv
    if first:
        y = _with_early_rows(y, r, lw, k, v, ain, bin_)
    gcol = jnp.sum(_diag(gc), axis=2, keepdims=True)
    sct = gcol * s0t + _bmm_tn(jnp.concatenate([btc, ktc], axis=1), jnp.concatenate([u, v], axis=1))
    return _post(y, r, k, v, rk, lnw, lnb), sct


def _diag(gc):
    return jnp.where(_masks(HEAD)[2], gc, 0.0)


def _with_early_rows(y, r, lw, k, v, ain, bin_):
    early = _stack([_early_rows(r[h], lw[h], k[h], v[h], ain[h], bin_[h]) for h in range(2)])
    return jnp.concatenate([jnp.concatenate([early, y[:2, EARLY:]], axis=1), y[2:]], axis=0)


def _early_rows(r, lw, k, v, ain, bin_):
    wc, bc, kc = jnp.transpose(jnp.exp(lw)), jnp.transpose(bin_), jnp.transpose(k)
    st = jnp.zeros((HEAD, HEAD), F32)
    rows = []
    for t in range(EARLY):
        sa = _bdot(ain[t:t + 1], st)
        st = st * wc[:, t:t + 1] + bc[:, t:t + 1] * sa + kc[:, t:t + 1] * v[t:t + 1]
        rows.append(_bdot(r[t:t + 1], st))
    return jnp.concatenate(rows, axis=0)


def _chunk_rows(c):
    return pl.ds(c * CHUNK, CHUNK) if isinstance(c, int) else pl.ds(pl.multiple_of(c * CHUNK, CHUNK), CHUNK)


def _stack(xs):
    return jnp.concatenate([x[None] for x in xs], axis=0)


def _pairs(ref, chunks):
    tiles = [ref[0, _chunk_rows(c), :] for c in chunks]
    return _stack([t[:, HEAD * h:HEAD * h + HEAD] for t in tiles for h in range(2)])


def _unpair(vals, j):
    return jnp.concatenate([vals[2 * j], vals[2 * j + 1]], axis=1)


def _masks(c):
    ii = lax.broadcasted_iota(jnp.int32, (c, c), 0)
    jj = lax.broadcasted_iota(jnp.int32, (c, c), 1)
    return ii > jj, ii >= jj, ii == jj


def _chunk_core(r, lw, k, v, kkraw, a):
    g_, c = r.shape[0], r.shape[1]
    nrm = jnp.sqrt(jnp.sum(kkraw * kkraw, axis=-1, keepdims=True))
    kkn = kkraw / jnp.maximum(nrm, 1e-12)
    ain, bin_ = -kkn, kkn * a
    strict, incl, _ = _masks(c)
    lg = lax.dot_general(jnp.broadcast_to(incl.astype(F32), (g_, c, c)), lw, (((2,), (1,)), ((0,), (0,))), precision=HI,
                         preferred_element_type=F32)
    g, gp, gi = jnp.exp(lg), jnp.exp(lg - lw), jnp.exp(-lg)
    at, rt, bt, kt = ain * gp, r * g, bin_ * gi, k * gi
    aa = _bmm_nt(jnp.concatenate([at, rt], axis=1), jnp.concatenate([bt, kt], axis=1))
    aab = jnp.where(strict, aa[:, :c, :c], 0.0)
    aak = jnp.where(strict, aa[:, :c, c:], 0.0)
    arb = jnp.where(incl, aa[:, c:, :c], 0.0)
    ark = jnp.where(incl, aa[:, c:, c:], 0.0)
    akv = _bmm(jnp.concatenate([aak, ark], axis=1), v)
    gc = g[:, c - 1:c, :]
    return at, rt, bt * gc, kt * gc, gc, aab, arb, akv[:, :c], akv[:, c:], ain, bin_


def _solve(aab, z):
    c = aab.shape[1]
    p = aab
    z = z + _bmm(p, z)
    n = 1
    while 2 * n < c:
        p = _bmm(p, p)
        z = z + _bmm(p, z)
        n *= 2
    return z


def _post(y, r, k, v, rk, lnw, lnb):
    mu = jnp.mean(y, axis=-1, keepdims=True)
    var = jnp.mean(jnp.square(y - mu), axis=-1, keepdims=True)
    yn = (y - mu) * lax.rsqrt(var + GN_EPS) * lnw + lnb
    return yn + jnp.sum(r * k * rk, axis=-1, keepdims=True) * v


def _chunk_consts(r, lw, k, v, kkraw, a, first=False):
    at, rt, btc, ktc, gc, aab, arb, xv, arkv, ain, bin_ = _chunk_core(r, lw, k, v, kkraw, a)
    z = _solve(aab, jnp.concatenate([at, xv], axis=2))
    ryv = jnp.concatenate([rt, arkv], axis=2) + _bmm(arb, z)
    if first:
        ryv = jnp.concatenate([ryv[:, :, :HEAD], _with_early_rows(ryv[:, :, HEAD:], r, lw, k, v, ain, bin_)], axis=2)
    mkv = _bmm_tn(btc, z) + jnp.concatenate([_diag(gc), _bmm_tn(ktc, v)], axis=2)
    return mkv, ryv


def _rwkv_scan(ins, rk, lnw, lnb):
    bsz, s, _ = ins[0].shape
    nch = s // CHUNK

    def consts_body(r_ref, lw_ref, k_ref, v_ref, kk_ref, a_ref, mkv_ref, ry_ref, yv_ref):
        def group(i, carry):
            chunks = [i * CHUNK_GROUP + j for j in range(CHUNK_GROUP)]
            mkv, ryv = _chunk_consts(*[_pairs(ref, chunks) for ref in (r_ref, lw_ref, k_ref, v_ref, kk_ref, a_ref)],
                                     first=isinstance(i, int) and i == 0)
            for j, c in enumerate(chunks):
                for h in range(2):
                    mkv_ref[0, 0, c, h] = mkv[2 * j + h]
                ry_ref[0, _chunk_rows(c), :] = jnp.concatenate([ryv[2 * j][:, :HEAD], ryv[2 * j + 1][:, :HEAD]], axis=1)
                yv_ref[0, _chunk_rows(c), :] = jnp.concatenate([ryv[2 * j][:, HEAD:], ryv[2 * j + 1][:, HEAD:]], axis=1)
            return carry

        group(0, 0)
        lax.fori_loop(1, nch // CHUNK_GROUP, group, 0)

    tile = pl.BlockSpec((1, s, LANE), lambda b, hp: (b, 0, hp))
    vec = pl.BlockSpec((1, LANE), lambda b, hp: (0, hp))
    mkv_spec = pl.BlockSpec((1, 1, nch, 2, HEAD, LANE), lambda b, hp: (b, hp, 0, 0, 0, 0))
    st_spec = pl.BlockSpec((1, 1, nch, 2, HEAD, HEAD), lambda b, hp: (b, hp, 0, 0, 0, 0))
    mkv, ry, yv = pl.pallas_call(
        consts_body, name="rwkv_consts", grid=(bsz, 4), in_specs=[tile] * 6, out_specs=[mkv_spec, tile, tile],
        out_shape=[SDS((bsz, 4, nch, 2, HEAD, LANE), F32), SDS((bsz, s, WIDTH), F32), SDS((bsz, s, WIDTH), F32)],
        compiler_params=_params(("parallel", "parallel")))(*ins)

    states = _chunk_recurrence(mkv, None, "rwkv_states")

    def out_body(ry_ref, yv_ref, r_ref, k_ref, v_ref, st_ref, rk_ref, lnw_ref, lnb_ref, o_ref):
        y, r, k, v, rk_, lnw_, lnb_ = _scan_rows(ry_ref, yv_ref, r_ref, k_ref, v_ref, st_ref, rk_ref, lnw_ref, lnb_ref)
        o = _post(y, r, k, v, rk_, lnw_, lnb_)
        for j in range(CHUNK_GROUP):
            o_ref[0, _chunk_rows(j), :] = _unpair(o, j)

    o = pl.pallas_call(
        out_body, name="rwkv_out", grid=(bsz, 4, nch // CHUNK_GROUP), in_specs=_group_specs(5), out_specs=_group_specs(1)[0],
        out_shape=SDS((bsz, s, WIDTH), F32),
        compiler_params=_params(("parallel", "parallel", "parallel")))(ry, yv, ins[0], ins[2], ins[3], states, rk, lnw, lnb)
    return o, states, (mkv, ry, yv)


def _group_specs(n_tiles):
    tile = pl.BlockSpec((1, CHUNK_GROUP * CHUNK, LANE), lambda b, hp, t: (b, t, hp))
    if n_tiles == 1:
        return [tile]
    st = pl.BlockSpec((1, 1, CHUNK_GROUP, 2, HEAD, HEAD), lambda b, hp, t: (b, hp, t, 0, 0, 0))
    vec = pl.BlockSpec((1, LANE), lambda b, hp, t: (0, hp))
    return [tile] * n_tiles + [st] + [vec] * 3


def _scan_rows(ry_ref, yv_ref, r_ref, k_ref, v_ref, st_ref, rk_ref, lnw_ref, lnb_ref):
    chunks = list(range(CHUNK_GROUP))
    ry, yv, r, k, v = (_pairs(ref, chunks) for ref in (ry_ref, yv_ref, r_ref, k_ref, v_ref))
    st = _stack([st_ref[0, 0, c, h] for c in chunks for h in range(2)])
    vecs = [_stack([ref[:, HEAD * h:HEAD * h + HEAD] for _ in chunks for h in range(2)]) for ref in (rk_ref, lnw_ref, lnb_ref)]
    return (_bmm(ry, st) + yv, r, k, v, *vecs)


def _chunk_recurrence(mkv, q, name):
    bsz, _, nch = mkv.shape[:3]
    pairs = [(hp, h) for hp in range(4) for h in range(2)]

    def body(*refs):
        mkv_ref, out_ref, acc = refs[0], refs[-2], refs[-1]
        acc[...] = jnp.zeros_like(acc)

        def step(i, carry):
            c = i if q is None else nch - 1 - i
            cur = acc[...]
            for j, (hp, h) in enumerate(pairs):
                out_ref[0, hp, c, h] = cur[j]
            m = _stack([mkv_ref[0, hp, c, h] for hp, h in pairs])
            if q is None:
                acc[...] = _bmm(m[:, :, :HEAD], cur) + m[:, :, HEAD:]
            else:
                acc[...] = _bmm_tn(m[:, :, :HEAD], cur) + _stack([refs[1][0, hp, c, h] for hp, h in pairs])
            return carry

        lax.fori_loop(0, nch, step, 0)

    spec = lambda w: pl.BlockSpec((1, 4, nch, 2, HEAD, w), lambda b: (b, 0, 0, 0, 0, 0))
    return pl.pallas_call(
        body, name=name, grid=(bsz,), in_specs=[spec(LANE)] + ([] if q is None else [spec(HEAD)]), out_specs=spec(HEAD),
        out_shape=SDS((bsz, 4, nch, 2, HEAD, HEAD), F32), scratch_shapes=[pltpu.VMEM((8, HEAD, HEAD), F32)],
        compiler_params=_params(("parallel",)))(*([mkv] if q is None else [mkv, q]))


def _rwkv_scan_bwd(ins, states, consts, do3, rk, lnw, lnb):
    bsz, s, _ = ins[0].shape
    nch = s // CHUNK

    mkv, ry, yv = consts

    def q_body(do_ref, ry_ref, yv_ref, r_ref, k_ref, v_ref, st_ref, rk_ref, lnw_ref, lnb_ref, q_ref):
        y, r, k, v, rk_, lnw_, lnb_ = _scan_rows(ry_ref, yv_ref, r_ref, k_ref, v_ref, st_ref, rk_ref, lnw_ref, lnb_ref)
        _, vjp = jax.vjp(lambda y_: _post(y_, r, k, v, rk_, lnw_, lnb_), y)
        (dy,) = vjp(_pairs(do_ref, list(range(CHUNK_GROUP))))
        q = _bmm_tn(_pairs(ry_ref, list(range(CHUNK_GROUP))), dy)
        for j in range(CHUNK_GROUP):
            for h in range(2):
                q_ref[0, 0, j, h] = q[2 * j + h]

    specs = _group_specs(6)
    q = pl.pallas_call(
        q_body, name="rwkv_q", grid=(bsz, 4, nch // CHUNK_GROUP), in_specs=specs, out_specs=specs[6],
        out_shape=SDS((bsz, 4, nch, 2, HEAD, HEAD), F32),
        compiler_params=_params(("parallel", "parallel", "parallel")))(do3, ry, yv, ins[0], ins[2], ins[3], states, rk, lnw, lnb)

    dstates = _chunk_recurrence(mkv, q, "rwkv_dstates")

    def body(r_ref, lw_ref, k_ref, v_ref, kk_ref, a_ref, st_ref, dst_ref, do_ref, rk_ref, lnw_ref, lnb_ref,
             dr_ref, dlw_ref, dk_ref, dv_ref, dkk_ref, da_ref, drk_ref, dlnw_ref, dlnb_ref):
        chunks = list(range(BWD_GROUP))
        par_refs = (drk_ref, dlnw_ref, dlnb_ref)

        @pl.when(jnp.logical_and(pl.program_id(1) == 0, pl.program_id(2) == 0))
        def _():
            for ref in par_refs:
                ref[...] = jnp.zeros_like(ref)

        def group(first):
            per_pair = lambda ref: _stack([ref[0, 0, c, h] for c in chunks for h in range(2)])
            vecs = [_stack([ref[:, HEAD * h:HEAD * h + HEAD] for _ in chunks for h in range(2)]) for ref in (rk_ref, lnw_ref, lnb_ref)]
            _, vjp = jax.vjp(functools.partial(_chunk_fn, first=first), per_pair(st_ref),
                             *[_pairs(ref, chunks) for ref in (r_ref, lw_ref, k_ref, v_ref, kk_ref, a_ref)], *vecs)
            grads = vjp((_pairs(do_ref, chunks), per_pair(dst_ref)))
            for ref, cot in zip((dr_ref, dlw_ref, dk_ref, dv_ref, dkk_ref, da_ref), grads[1:7]):
                for j, c in enumerate(chunks):
                    ref[0, _chunk_rows(c), :] = _unpair(cot, j)
            for ref, g_ in zip(par_refs, grads[7:10]):
                ref[...] += jnp.concatenate([sum(g_[2 * j + h] for j in range(BWD_GROUP)) for h in range(2)], axis=1)

        pl.when(pl.program_id(2) == 0)(functools.partial(group, True))
        pl.when(pl.program_id(2) != 0)(functools.partial(group, False))

    tt = BWD_GROUP * CHUNK
    tile = pl.BlockSpec((1, tt, LANE), lambda hp, b, t: (b, t, hp))
    vec = pl.BlockSpec((1, LANE), lambda hp, b, t: (0, hp))
    st_spec = pl.BlockSpec((1, 1, BWD_GROUP, 2, HEAD, HEAD), lambda hp, b, t: (b, hp, t, 0, 0, 0))
    outs = pl.pallas_call(
        body, name="rwkv_scan_bwd", grid=(4, bsz, s // tt), in_specs=[tile] * 6 + [st_spec, st_spec, tile] + [vec] * 3,
        out_specs=[tile] * 6 + [vec] * 3,
        out_shape=[SDS((bsz, s, WIDTH), F32)] * 6 + [SDS((1, WIDTH), F32)] * 3,
        compiler_params=_params(("parallel", "arbitrary", "arbitrary")))(*ins, states, dstates, do3, rk, lnw, lnb)
    return outs[:6], outs[6:]


def _head(o_attn, o_rwkv, z_attn, z_rwkv, gm, x2, tgt, wua, wur, wout, g2):
    n = x2.shape[0]
    tm = 256
    nt = n // tm
    d = D_MODEL

    def body(oa_ref, or_ref, za_ref, zr_ref, gm_ref, x_ref, t_ref, wua_ref, wur_ref, wout_ref, g2_ref,
             dxo_ref, doa_ref, dor_ref, dza_ref, dzr_ref, dgm_ref, dwua_ref, dwur_ref, dwout_ref, dg2_ref, loss_ref, lacc):
        i = pl.program_id(0)
        oa, orw, za, zr = oa_ref[...], or_ref[...], za_ref[...], zr_ref[...]
        ga, gb = gm_ref[:, 0:d], gm_ref[:, d:2 * d]
        am = (oa * _silu(za)).astype(BF16)
        bm = (orw * _silu(zr)).astype(BF16)
        ya, yb = _dot(am, wua_ref[...]), _dot(bm, wur_ref[...])
        sa, sb = jax.nn.sigmoid(ga), jax.nn.sigmoid(gb)
        merged = (sa * ya + sb * yb).astype(BF16)
        out = _dot(merged, wout_ref[...])
        rs = lax.rsqrt(jnp.mean(out * out, axis=-1, keepdims=True) + RMS_EPS)
        g2 = g2_ref[...]
        err = x_ref[...] + out * rs * g2 - t_ref[...]
        lpart = jnp.sum(err * err, axis=0, keepdims=True)
        dxo = err * (1.0 / d)
        dxo_ref[...] = dxo
        dg2 = jnp.sum(dxo * out * rs, axis=0, keepdims=True)
        gd = dxo * g2
        dout = (rs * (gd - out * (rs * rs) * jnp.mean(gd * out, axis=-1, keepdims=True))).astype(BF16)
        dmerged = _dot_nt(dout, wout_ref[...])
        dwout = _dot_tn(merged, dout)
        dya, dyb = (dmerged * sa).astype(BF16), (dmerged * sb).astype(BF16)
        dgm_ref[:, 0:d] = dmerged * ya * sa * (1.0 - sa)
        dgm_ref[:, d:2 * d] = dmerged * yb * sb * (1.0 - sb)
        dam, dbm = _dot_nt(dya, wua_ref[...]), _dot_nt(dyb, wur_ref[...])
        dwua, dwur = _dot_tn(am, dya), _dot_tn(bm, dyb)
        doa_ref[...] = dam * _silu(za)
        dza_ref[...] = dam * oa * _dsilu(za)
        dor_ref[...] = dbm * _silu(zr)
        dzr_ref[...] = dbm * orw * _dsilu(zr)

        @pl.when(i == 0)
        def _():
            dwua_ref[...], dwur_ref[...], dwout_ref[...], dg2_ref[...], lacc[...] = dwua, dwur, dwout, dg2, lpart

        @pl.when(i != 0)
        def _():
            dwua_ref[...] += dwua
            dwur_ref[...] += dwur
            dwout_ref[...] += dwout
            dg2_ref[...] += dg2
            lacc[...] += lpart

        @pl.when(i == nt - 1)
        def _():
            loss_ref[...] = jnp.sum(lacc[...], axis=1, keepdims=True) * (0.5 / d)

    t512 = pl.BlockSpec((tm, WIDTH), lambda i: (i, 0))
    t1k = pl.BlockSpec((tm, d), lambda i: (i, 0))
    t2k = pl.BlockSpec((tm, 2 * d), lambda i: (i, 0))
    full = lambda r, c: pl.BlockSpec((r, c), lambda i: (0, 0))
    return pl.pallas_call(
        body, name="head_fwd_bwd", grid=(nt,),
        in_specs=[t512, t512, t512, t512, t2k, t1k, t1k, full(WIDTH, d), full(WIDTH, d), full(d, d), full(1, d)],
        out_specs=[t1k, t512, t512, t512, t512, t2k, full(WIDTH, d), full(WIDTH, d), full(d, d), full(1, d), full(1, 1)],
        out_shape=[SDS((n, d), F32)] + [SDS((n, WIDTH), F32)] * 4 + [SDS((n, 2 * d), F32), SDS((WIDTH, d), F32), SDS((WIDTH, d), F32),
                                                                    SDS((d, d), F32), SDS((1, d), F32), SDS((1, 1), F32)],
        scratch_shapes=[pltpu.VMEM((1, d), F32)],
        compiler_params=_params(("arbitrary",)))(o_attn, o_rwkv, z_attn, z_rwkv, gm, x2, tgt, wua, wur, wout, g2)


def _prenorm_bwd(dh, x2, rs, g1, dxo):
    n, d = x2.shape
    tm = 1024

    def body(dh_ref, x_ref, rs_ref, g_ref, dxo_ref, gx_ref, dg_ref):
        x, r = x_ref[...], rs_ref[...]
        gd = dh_ref[...] * g_ref[...]
        gx_ref[...] = dxo_ref[...] + r * (gd - x * (r * r) * jnp.mean(gd * x, axis=-1, keepdims=True))
        dg = jnp.sum(dh_ref[...] * x * r, axis=0, keepdims=True)

        @pl.when(pl.program_id(0) == 0)
        def _():
            dg_ref[...] = dg

        @pl.when(pl.program_id(0) != 0)
        def _():
            dg_ref[...] += dg

    t = pl.BlockSpec((tm, d), lambda i: (i, 0))
    return pl.pallas_call(
        body, name="prenorm_bwd", grid=(n // tm,),
        in_specs=[t, t, pl.BlockSpec((tm, 1), lambda i: (i, 0)), pl.BlockSpec((1, d), lambda i: (0, 0)), t],
        out_specs=[t, pl.BlockSpec((1, d), lambda i: (0, 0))], out_shape=[SDS((n, d), F32), SDS((1, d), F32)],
        compiler_params=_params(("arbitrary",)))(dh, x2, rs, g1, dxo)


def _mesh_pos():
    x, y, c = lax.axis_index("x"), lax.axis_index("y"), lax.axis_index("c")
    return 4 * x + 2 * y + c


def _coords(idx):
    return (idx // 4, (idx // 2) % 2, idx % 2)


def _exchange(srcs, to_all, name):
    n = len(srcs)

    def body(*refs):
        src_refs, dst_refs = refs[:n], refs[n:2 * n]
        send_sems, recv_sems, local_sems = refs[2 * n:]
        me = _mesh_pos()

        def piece(i, j):
            return src_refs[i] if to_all[i] else src_refs[i].at[j]

        def remote(i, off, peer, block, slot):
            return pltpu.make_async_remote_copy(src_ref=piece(i, block), dst_ref=dst_refs[i].at[slot],
                                                send_sem=send_sems.at[i, off - 1], recv_sem=recv_sems.at[i, off - 1],
                                                device_id=_coords(peer), device_id_type=MESH)

        local = [pltpu.make_async_copy(piece(i, me), dst_refs[i].at[me], local_sems.at[i]) for i in range(n)]
        for cp in local:
            cp.start()
        sends = []
        for off in range(1, N_DEV):
            to = (me + off) % N_DEV
            for i in range(n):
                sends.append(remote(i, off, to, to, me))
                sends[-1].start()
        for off in range(1, N_DEV):
            frm = (me + N_DEV - off) % N_DEV
            for i in range(n):
                remote(i, off, frm, me, frm).wait_recv()
        for cp in sends:
            cp.wait_send()
        for cp in local:
            cp.wait()

    outs = pl.pallas_call(
        body, name=name, in_specs=[pl.BlockSpec(memory_space=pltpu.HBM)] * n, out_specs=[pl.BlockSpec(memory_space=pltpu.HBM)] * n,
        out_shape=[SDS((N_DEV,) + s.shape[-2:], s.dtype) for s in srcs],
        scratch_shapes=[pltpu.SemaphoreType.DMA((n, N_DEV - 1)), pltpu.SemaphoreType.DMA((n, N_DEV - 1)), pltpu.SemaphoreType.DMA((n,))],
        compiler_params=pltpu.CompilerParams())(*srcs)
    return outs


def _adamw(parts, w, m, v, tr, name):
    rows, cols = w.shape
    c1, c2 = 1.0 - ADAM_B1 ** ADAM_STEP, 1.0 - ADAM_B2 ** ADAM_STEP

    def body(p_ref, w_ref, m_ref, v_ref, g_ref, d_ref, nm_ref, nv_ref):
        g = p_ref[0].astype(F32)
        for j in range(1, N_DEV):
            g = g + p_ref[j].astype(F32)
        nm = ADAM_B1 * m_ref[...] + (1.0 - ADAM_B1) * g
        nv = ADAM_B2 * v_ref[...] + (1.0 - ADAM_B2) * jnp.square(g)
        g_ref[...] = g
        nm_ref[...] = nm
        nv_ref[...] = nv
        d_ref[...] = -ADAM_LR * ((nm / c1) / (jnp.sqrt(nv / c2) + ADAM_EPS) + ADAM_WD * w_ref[...])

    t = pl.BlockSpec((tr, cols), lambda i: (i, 0))
    return pl.pallas_call(
        body, name=name, grid=(rows // tr,), in_specs=[pl.BlockSpec((N_DEV, tr, cols), lambda i: (0, i, 0)), t, t, t],
        out_specs=[t] * 4, out_shape=[SDS((rows, cols), F32)] * 4, compiler_params=_params(("parallel",)))(parts, w, m, v)


SHARDED = (("w_in", D_MODEL, IN_COLS // N_DEV, True, 128), ("w_up_attn", WIDTH, D_MODEL // N_DEV, True, WIDTH),
           ("w_up_rwkv", WIDTH, D_MODEL // N_DEV, True, WIDTH), ("w_out", D_MODEL // N_DEV, D_MODEL, False, D_MODEL // N_DEV),
           ("rwkv_w_up", LORA, WIDTH // N_DEV, True, LORA), ("rwkv_a_up", LORA, WIDTH // N_DEV, True, LORA))
LOSS_SLOT = sum(n for _, n in SMALL)


def _pack_small(small, extra=None):
    flat = [small[n].reshape(-1).astype(F32) for n, _ in SMALL]
    flat.append(jnp.zeros((1,), F32) if extra is None else extra.reshape(1))
    flat.append(jnp.zeros((SMALL_ROWS * LANE - LOSS_SLOT - 1,), F32))
    return jnp.concatenate(flat).reshape(SMALL_ROWS, LANE)


def _unpack_small(packed, shapes):
    flat = packed.reshape(-1)
    out, off = {}, 0
    for n, cnt in SMALL:
        out[n] = flat[off:off + cnt].reshape(shapes[n])
        off += cnt
    return out, flat[LOSS_SLOT]


def _whole(gathered, by_cols):
    if not by_cols:
        return gathered.reshape(-1, gathered.shape[-1])
    return gathered.transpose(1, 0, 2).reshape(gathered.shape[1], -1)


def _per_owner(full, by_cols):
    if not by_cols:
        return full.reshape(N_DEV, -1, full.shape[-1])
    return full.reshape(full.shape[0], N_DEV, -1).transpose(1, 0, 2)


def _local_step(x, loss_target, sm, wts):
    bsz, s, d = x.shape
    n = bsz * s
    x2, tgt = x.reshape(n, d), loss_target.reshape(n, d)
    bidx = jnp.asarray(_bucket_tables())
    w_in = wts["w_in"]
    segs = (("qkv", 0, QKV_COLS, 512), ("za", OFF_ZA, WIDTH, 512), ("pr", OFF_PR, PR_COLS, PR_COLS), ("zr", OFF_ZR, WIDTH, 512),
            ("gm", OFF_GM, 2 * D_MODEL, 512))

    h, rs = _prenorm(x2, sm["pre_norm_gain"])
    proj = {nm: _mm(h, w_in[:, off:off + cnt], tn, "proj_" + nm) for nm, off, cnt, tn in segs}
    qkv3 = proj["qkv"].reshape(bsz, s, QKV_COLS)
    pr3 = proj["pr"].reshape(bsz, s, PR_COLS)

    o_attn, lse = _attn_fwd(qkv3, sm["rel_bias"], bidx)
    rk = sm["rwkv_r_k"].reshape(1, WIDTH)
    pre_args = (sm["rwkv_shift_mix"], sm["rwkv_w0"], wts["rwkv_w_up"], sm["rwkv_a0"], wts["rwkv_a_up"], sm["rwkv_k_k"], sm["rwkv_k_a"])
    scan_in = _rwkv_pre(pr3, *pre_args)
    o_rwkv, states, consts = _rwkv_scan(scan_in, rk, sm["rwkv_ln_w"], sm["rwkv_ln_b"])

    (dxo, do_attn, do_rwkv, dza, dzr, dgm, g_wua, g_wur, g_wout, g_post, loss) = _head(
        o_attn.reshape(n, WIDTH), o_rwkv.reshape(n, WIDTH), proj["za"], proj["zr"], proj["gm"], x2, tgt,
        wts["w_up_attn"], wts["w_up_rwkv"], wts["w_out"], sm["post_norm_gain"])

    dqkv, dbias = _attn_bwd(qkv3, o_attn, lse, do_attn.reshape(bsz, s, WIDTH), sm["rel_bias"], bidx)
    g_bias = _bias_grad(dbias, bidx)[:, :N_BUCKET].T

    scan_cots, (g_rk, g_lnw, g_lnb) = _rwkv_scan_bwd(scan_in, states, consts, do_rwkv.reshape(bsz, s, WIDTH), rk, sm["rwkv_ln_w"],
                                                     sm["rwkv_ln_b"])
    dprs, g_mix, g_w0, g_wup, g_a0, g_aup, g_kk, g_ka = _rwkv_pre_bwd(pr3, scan_cots, *pre_args)
    dpr = _shift_bwd(dprs, sm["rwkv_shift_mix"]).reshape(n, PR_COLS)

    dsegs = [(jnp.concatenate([t.reshape(n, WIDTH) for t in dqkv], axis=1), 0, 512), (dza, OFF_ZA, WIDTH), (dpr, OFF_PR, PR_COLS),
             (dzr, OFF_ZR, WIDTH), (dgm, OFF_GM, D_MODEL)]
    dh = None
    g_win = []
    for j, (t, off, tn) in enumerate(dsegs):
        dh = _mm_nt_acc(t, w_in[:, off:off + t.shape[1]], dh, "dh_%d" % j)
        g_win.append(_mm_tn(h, t, tn, "gw_in_%d" % j))
    grad_x, g_pre = _prenorm_bwd(dh, x2, rs, sm["pre_norm_gain"], dxo)

    full = {"w_in": jnp.concatenate(g_win, axis=1), "w_up_attn": g_wua, "w_up_rwkv": g_wur, "w_out": g_wout,
            "rwkv_w_up": g_wup, "rwkv_a_up": g_aup}
    small = {"pre_norm_gain": g_pre, "rel_bias": g_bias, "rwkv_shift_mix": g_mix, "rwkv_w0": g_w0, "rwkv_a0": g_a0, "rwkv_k_k": g_kk,
             "rwkv_k_a": g_ka, "rwkv_r_k": g_rk, "rwkv_ln_w": g_lnw, "rwkv_ln_b": g_lnb, "post_norm_gain": g_post}
    return loss[0, 0], grad_x.reshape(bsz, s, d), full, small


def kernel(x, pre_norm_gain, w_in, rel_bias, rwkv_shift_mix, rwkv_w0, rwkv_w_up, rwkv_a0, rwkv_a_up, rwkv_k_k, rwkv_k_a, rwkv_r_k, rwkv_ln_w, rwkv_ln_b, w_up_attn, w_up_rwkv, w_out, post_norm_gain, loss_target, m_pre_norm_gain, m_w_in, m_rel_bias, m_rwkv_shift_mix, m_rwkv_w0, m_rwkv_w_up, m_rwkv_a0, m_rwkv_a_up, m_rwkv_k_k, m_rwkv_k_a, m_rwkv_r_k, m_rwkv_ln_w, m_rwkv_ln_b, m_w_up_attn, m_w_up_rwkv, m_w_out, m_post_norm_gain, v_pre_norm_gain, v_w_in, v_rel_bias, v_rwkv_shift_mix, v_rwkv_w0, v_rwkv_w_up, v_rwkv_a0, v_rwkv_a_up, v_rwkv_k_k, v_rwkv_k_a, v_rwkv_r_k, v_rwkv_ln_w, v_rwkv_ln_b, v_w_up_attn, v_w_up_rwkv, v_w_out, v_post_norm_gain):
    names = [n for n, *_ in SHARDED] + [n for n, _ in SMALL]
    loc = dict(locals())
    w = {n: loc[n] for n in names}
    m = {n: loc["m_" + n] for n in names}
    v = {n: loc["v_" + n] for n in names}
    shapes = {n: w[n].shape for n in names}
    order = ["pre_norm_gain", "w_in", "rel_bias", "rwkv_shift_mix", "rwkv_w0", "rwkv_w_up", "rwkv_a0", "rwkv_a_up", "rwkv_k_k", "rwkv_k_a",
             "rwkv_r_k", "rwkv_ln_w", "rwkv_ln_b", "w_up_attn", "w_up_rwkv", "w_out", "post_norm_gain"]
    shard2d = lambda t, n, r, c: t[n].reshape(r, c)

    gathered = _exchange([shard2d(w, n, r, c).astype(BF16) for n, r, c, _, _ in SHARDED], [True] * len(SHARDED), "gather_weights")
    wts = {n: _whole(g, by_cols) for (n, _, _, by_cols, _), g in zip(SHARDED, gathered)}

    loss, grad_x, full, small = _local_step(x, loss_target, w, wts)
    parts = _exchange([_per_owner(full[n], by_cols).astype(BF16) for n, _, _, by_cols, _ in SHARDED] + [_pack_small(small, loss)],
                      [False] * len(SHARDED) + [True], "exchange_grads")

    outs = [{}, {}, {}, {}]
    for (n, r, c, _, tr), p in zip(SHARDED, parts):
        res = _adamw(p, shard2d(w, n, r, c), shard2d(m, n, r, c), shard2d(v, n, r, c), tr, "adamw_" + n)
        for o, t in zip(outs, res):
            o[n] = t.reshape(shapes[n])
    res = _adamw(parts[-1], _pack_small(w), _pack_small(m), _pack_small(v), SMALL_ROWS, "adamw_small")
    for o, t in zip(outs, res):
        o.update(_unpack_small(t, shapes)[0])
    loss = _unpack_small(res[0], shapes)[1]
    return (loss, grad_x, *[o[n] for o in outs for n in order])
```

```python
import functools
import math

import numpy as np
import jax
import jax.numpy as jnp
from jax import lax
from jax.experimental import pallas as pl
from jax.experimental.pallas import tpu as pltpu

F32, BF16 = jnp.float32, jnp.bfloat16
SDS = jax.ShapeDtypeStruct
HI = lax.Precision.HIGHEST
HI3 = lax.Precision.HIGH
MESH = pl.DeviceIdType.MESH

N_DEV = 8
D_MODEL = 1024
HEAD = 64
N_HEAD = 8
WIDTH = N_HEAD * HEAD
DILATIONS = (1, 4, 16)
QB = 128
N_BUCKET = 32
MAX_DIST = 2048
LORA = 64
QKV_COLS = 9 * WIDTH
PR_COLS = 3 * WIDTH + 2 * LORA
IN_COLS = QKV_COLS + WIDTH + PR_COLS + WIDTH + 2 * D_MODEL
OFF_ZA, OFF_PR, OFF_ZR, OFF_GM = QKV_COLS, QKV_COLS + WIDTH, QKV_COLS + WIDTH + PR_COLS, QKV_COLS + 2 * WIDTH + PR_COLS
RMS_EPS = 1e-6
GN_EPS = 64e-5
SCALE = 1.0 / math.sqrt(HEAD)
CHUNK = 64
CHUNK_GROUP = 4
BWD_GROUP = 4
EARLY = 8
NEG = -1e30
LANE = 128

ADAM_LR, ADAM_B1, ADAM_B2, ADAM_EPS, ADAM_WD, ADAM_STEP = 0.001, 0.9, 0.999, 1e-08, 0.01, 10

VMEM_LIMIT = 56 * 1024 * 1024

SMALL = (("pre_norm_gain", 1024), ("rel_bias", 768), ("rwkv_shift_mix", 1664), ("rwkv_w0", 512), ("rwkv_a0", 512),
         ("rwkv_k_k", 512), ("rwkv_k_a", 512), ("rwkv_r_k", 512), ("rwkv_ln_w", 512), ("rwkv_ln_b", 512),
         ("post_norm_gain", 1024))
SMALL_ROWS = 64


def _params(sem=None):
    return pltpu.CompilerParams(dimension_semantics=sem, vmem_limit_bytes=VMEM_LIMIT)


def _dot(a, b):
    return jnp.dot(a, b, preferred_element_type=F32)


def _dot_nt(a, b):
    return lax.dot_general(a, b, (((1,), (1,)), ((), ())), preferred_element_type=F32)


def _dot_tn(a, b):
    return lax.dot_general(a, b, (((0,), (0,)), ((), ())), preferred_element_type=F32)


@jax.custom_vjp
def _bdot(a, b):
    return _dot(a.astype(BF16), b.astype(BF16))


def _bdot_fwd(a, b):
    return _bdot(a, b), (a, b)


def _bdot_bwd(res, g):
    a, b = res
    gb = g.astype(BF16)
    return _dot_nt(gb, b.astype(BF16)), _dot_tn(a.astype(BF16), gb)


_bdot.defvjp(_bdot_fwd, _bdot_bwd)


def _silu(z):
    return z * jax.nn.sigmoid(z)


def _dsilu(z):
    s = jax.nn.sigmoid(z)
    return s * (1.0 + z * (1.0 - s))


def _softplus(x):
    return jnp.maximum(x, 0.0) + jnp.log(1.0 + jnp.exp(-jnp.abs(x)))


def _bucket_tables():
    qi = np.arange(QB)[:, None] + QB
    ki = np.arange(2 * QB)[None, :]
    rel = np.maximum(qi - ki, 0)
    out = []
    for d in DILATIONS:
        dist = rel * d
        max_exact = N_BUCKET // 2
        ratio = np.log(np.maximum(dist, 1).astype(np.float32) / max_exact) / np.float32(math.log(MAX_DIST / max_exact))
        large = max_exact + (ratio * (N_BUCKET - max_exact)).astype(np.int32)
        large = np.minimum(large, N_BUCKET - 1)
        out.append(np.where(dist < max_exact, dist, large).astype(np.int32))
    return np.stack(out)


def _prenorm(x2, g):
    n, d = x2.shape
    tm = 1024

    def body(x_ref, g_ref, h_ref, rs_ref):
        x = x_ref[...]
        rs = lax.rsqrt(jnp.mean(x * x, axis=-1, keepdims=True) + RMS_EPS)
        h_ref[...] = (x * rs * g_ref[...]).astype(BF16)
        rs_ref[...] = rs

    return pl.pallas_call(
        body, name="prenorm", grid=(n // tm,),
        in_specs=[pl.BlockSpec((tm, d), lambda i: (i, 0)), pl.BlockSpec((1, d), lambda i: (0, 0))],
        out_specs=[pl.BlockSpec((tm, d), lambda i: (i, 0)), pl.BlockSpec((tm, 1), lambda i: (i, 0))],
        out_shape=[SDS((n, d), BF16), SDS((n, 1), F32)], compiler_params=_params(("parallel",)))(x2, g)


def _mm(a, b, tn, name):
    m, k = a.shape
    n = b.shape[1]
    tm = 1024

    def body(a_ref, b_ref, o_ref):
        o_ref[...] = _dot(a_ref[...], b_ref[...])

    return pl.pallas_call(
        body, name=name, grid=(n // tn, m // tm),
        in_specs=[pl.BlockSpec((tm, k), lambda j, i: (i, 0)), pl.BlockSpec((k, tn), lambda j, i: (0, j))],
        out_specs=pl.BlockSpec((tm, tn), lambda j, i: (i, j)),
        out_shape=SDS((m, n), F32), compiler_params=_params(("parallel", "parallel")))(a, b)


def _mm_nt_acc(a, b, acc, name):
    m, k = a.shape
    d = b.shape[0]
    tm = 1024
    tk = k if k <= 2048 else 1536
    have_acc = acc is not None

    def body(*refs):
        if have_acc:
            a_ref, b_ref, c_ref, o_ref = refs
        else:
            a_ref, b_ref, o_ref = refs
        r = _dot_nt(a_ref[...].astype(BF16), b_ref[...])

        @pl.when(pl.program_id(1) == 0)
        def _():
            o_ref[...] = r + c_ref[...] if have_acc else r

        @pl.when(pl.program_id(1) != 0)
        def _():
            o_ref[...] += r

    in_specs = [pl.BlockSpec((tm, tk), lambda i, j: (i, j)), pl.BlockSpec((d, tk), lambda i, j: (0, j))]
    args = [a, b]
    if have_acc:
        in_specs.append(pl.BlockSpec((tm, d), lambda i, j: (i, 0)))
        args.append(acc)
    return pl.pallas_call(
        body, name=name, grid=(m // tm, k // tk), in_specs=in_specs, out_specs=pl.BlockSpec((tm, d), lambda i, j: (i, 0)),
        out_shape=SDS((m, d), F32), compiler_params=_params(("parallel", "arbitrary")))(*args)


def _mm_tn(a, b, tn, name):
    m, k1 = a.shape
    n2 = b.shape[1]
    tm = 1024

    def body(a_ref, b_ref, o_ref):
        r = _dot_tn(a_ref[...], b_ref[...].astype(BF16))

        @pl.when(pl.program_id(1) == 0)
        def _():
            o_ref[...] = r

        @pl.when(pl.program_id(1) != 0)
        def _():
            o_ref[...] += r

    return pl.pallas_call(
        body, name=name, grid=(n2 // tn, m // tm),
        in_specs=[pl.BlockSpec((tm, k1), lambda j, i: (i, 0)), pl.BlockSpec((tm, tn), lambda j, i: (i, j))],
        out_specs=pl.BlockSpec((k1, tn), lambda j, i: (0, j)),
        out_shape=SDS((k1, n2), F32), compiler_params=_params(("parallel", "arbitrary")))(a, b)


def _ds(start, d):
    return pl.ds(start, QB) if d == 1 else pl.ds(start, QB, stride=d)


def _fill_bias(tab_ref, bidx_ref, bias_sc, hp):
    for g in range(3):
        bi = bidx_ref[g]
        for h in range(2):
            acc = jnp.zeros((QB, 2 * QB), F32)
            for j in range(N_BUCKET):
                acc = jnp.where(bi == j, tab_ref[j, g * N_HEAD + hp * 2 + h], acc)
            bias_sc[g * 2 + h] = acc


def _block_starts(it, d, nb):
    rho = it // nb
    n = it % nb
    st = rho + d * QB * n
    stp = rho + d * QB * jnp.maximum(n - 1, 0)
    return st, stp, n > 0


ATTN_BLOCKS = 2


def _bdot3(a, b, dims):
    return lax.dot_general(a, b, (dims, ((0,), (0,))), preferred_element_type=F32)


def _attn_operands(q_ref, k_ref, v_ref, bias_sc, g, d, nb, it0):
    ii = lax.broadcasted_iota(jnp.int32, (QB, 2 * QB), 0)
    cc = lax.broadcasted_iota(jnp.int32, (QB, 2 * QB), 1)
    qs, ks, vs, pens, starts = [], [], [], [], []
    for u in range(ATTN_BLOCKS):
        st, stp, hasprev = _block_starts(it0 + u, d, nb)
        qf = q_ref[0, _ds(st, d), :]
        kf = jnp.concatenate([k_ref[0, _ds(stp, d), :], k_ref[0, _ds(st, d), :]], axis=0)
        vf = jnp.concatenate([v_ref[0, _ds(stp, d), :], v_ref[0, _ds(st, d), :]], axis=0)
        own = jnp.logical_and(cc >= QB, ii >= cc - QB)
        prev = jnp.logical_and(jnp.logical_and(cc < QB, cc >= ii), hasprev)
        pen = jnp.where(jnp.logical_or(own, prev), 0.0, NEG)
        for h in range(2):
            sl = slice(HEAD * h, HEAD * h + HEAD)
            qs.append(qf[:, sl])
            ks.append(kf[:, sl])
            vs.append(vf[:, sl])
            pens.append(pen + bias_sc[g * 2 + h])
        starts.append((st, stp))
    return _stack(qs).astype(BF16), _stack(ks).astype(BF16), _stack(vs).astype(BF16), _stack(pens), starts


def _heads(x, u):
    return jnp.concatenate([x[2 * u], x[2 * u + 1]], axis=1)


def _attn_fwd(qkv3, rel_bias, bidx):
    bsz, s, _ = qkv3.shape
    rt = 256

    def body(tab_ref, bidx_ref, *refs):
        q_refs, k_refs, v_refs = refs[0:3], refs[3:6], refs[6:9]
        o_ref, lse_ref = refs[9:11]
        bias_sc, num_sc, den_sc, m_sc = refs[11:]
        pl.when(pl.program_id(1) == 0)(lambda: _fill_bias(tab_ref, bidx_ref, bias_sc, pl.program_id(0)))
        for g, d in enumerate(DILATIONS):
            nb = s // (QB * d)

            def blk(it, c, g=g, d=d, nb=nb):
                q, k, v, bias, starts = _attn_operands(q_refs[g], k_refs[g], v_refs[g], bias_sc, g, d, nb, it * ATTN_BLOCKS)
                sc = _bdot3(q, k, ((2,), (2,))) * SCALE + bias
                m = jnp.max(sc, axis=-1, keepdims=True)
                p = jnp.exp(sc - m)
                den = jnp.sum(p, axis=-1, keepdims=True)
                num = _bdot3(p.astype(BF16), v, ((2,), (1,)))
                den, m = jnp.broadcast_to(den, num.shape), jnp.broadcast_to(m, num.shape)
                for u, (st, _) in enumerate(starts):
                    num_sc[g, _ds(st, d), :] = _heads(num, u)
                    den_sc[g, _ds(st, d), :] = _heads(den, u)
                    m_sc[g, _ds(st, d), :] = _heads(m, u)
                return c

            lax.fori_loop(0, s // QB // ATTN_BLOCKS, blk, 0)

        def merge(i, c):
            rows = pl.ds(pl.multiple_of(i * rt, rt), rt)
            m0, m1, m2 = m_sc[0, rows, :], m_sc[1, rows, :], m_sc[2, rows, :]
            mall = jnp.maximum(jnp.maximum(m0, m1), m2)
            w0, w1, w2 = jnp.exp(m0 - mall), jnp.exp(m1 - mall), jnp.exp(m2 - mall)
            num = w0 * num_sc[0, rows, :] + w1 * num_sc[1, rows, :] + w2 * num_sc[2, rows, :]
            den = w0 * den_sc[0, rows, :] + w1 * den_sc[1, rows, :] + w2 * den_sc[2, rows, :]
            o_ref[0, rows, :] = num / den
            lse_ref[0, rows, :] = mall + jnp.log(den)
            return c

        lax.fori_loop(0, s // rt, merge, 0)

    col = lambda w, g: (lambda hp, b: (b, 0, (w * 3 + g) * 4 + hp))
    in_specs = [pl.BlockSpec(memory_space=pltpu.SMEM), pl.BlockSpec((3, QB, 2 * QB), lambda hp, b: (0, 0, 0))]
    in_specs += [pl.BlockSpec((1, s, LANE), col(w, g)) for w in range(3) for g in range(3)]
    out_spec = pl.BlockSpec((1, s, LANE), lambda hp, b: (b, 0, hp))
    return pl.pallas_call(
        body, name="attn_fwd", grid=(4, bsz), in_specs=in_specs, out_specs=[out_spec, out_spec],
        out_shape=[SDS((bsz, s, WIDTH), F32), SDS((bsz, s, WIDTH), F32)],
        scratch_shapes=[pltpu.VMEM((6, QB, 2 * QB), F32), pltpu.VMEM((3, s, LANE), F32), pltpu.VMEM((3, s, LANE), F32),
                        pltpu.VMEM((3, s, LANE), F32)],
        compiler_params=_params(("arbitrary", "arbitrary")))(rel_bias, bidx, *([qkv3] * 9))


def _attn_bwd(qkv3, o3, lse3, do3, rel_bias, bidx):
    bsz, s, _ = qkv3.shape
    rt = 256

    def body(tab_ref, bidx_ref, *refs):
        q_refs, k_refs, v_refs = refs[0:3], refs[3:6], refs[6:9]
        o_ref, lse_ref, do_ref = refs[9:12]
        dq_refs, dk_refs, dv_refs = refs[12:15], refs[15:18], refs[18:21]
        db_ref = refs[21]
        bias_sc, delta_sc = refs[22:]
        @pl.when(pl.program_id(1) == 0)
        def _():
            _fill_bias(tab_ref, bidx_ref, bias_sc, pl.program_id(0))
            db_ref[...] = jnp.zeros_like(db_ref)

        def prep(i, c):
            rows = pl.ds(pl.multiple_of(i * rt, rt), rt)
            prod = do_ref[0, rows, :] * o_ref[0, rows, :]
            d0 = jnp.sum(prod[:, :HEAD], axis=-1, keepdims=True)
            d1 = jnp.sum(prod[:, HEAD:], axis=-1, keepdims=True)
            delta_sc[rows, :] = jnp.concatenate([jnp.broadcast_to(d0, (rt, HEAD)), jnp.broadcast_to(d1, (rt, HEAD))], axis=1)
            z = jnp.zeros((rt, LANE), F32)
            for g in range(3):
                dk_refs[g][0, rows, :] = z
                dv_refs[g][0, rows, :] = z
            return c

        lax.fori_loop(0, s // rt, prep, 0)
        for g, d in enumerate(DILATIONS):
            nb = s // (QB * d)

            def blk(it, c, g=g, d=d, nb=nb):
                q, k, v, bias, starts = _attn_operands(q_refs[g], k_refs[g], v_refs[g], bias_sc, g, d, nb, it * ATTN_BLOCKS)
                dos, lses, deltas = [], [], []
                for st, _ in starts:
                    dof, lsef, delf = do_ref[0, _ds(st, d), :], lse_ref[0, _ds(st, d), :], delta_sc[_ds(st, d), :]
                    for h in range(2):
                        dos.append(dof[:, HEAD * h:HEAD * h + HEAD])
                        lses.append(lsef[:, HEAD * h:HEAD * h + 1])
                        deltas.append(delf[:, HEAD * h:HEAD * h + 1])
                do, lse, delta = _stack(dos).astype(BF16), _stack(lses), _stack(deltas)
                p = jnp.exp(_bdot3(q, k, ((2,), (2,))) * SCALE + bias - lse)
                dv = _bdot3(p.astype(BF16), do, ((1,), (1,)))
                ds = p * (_bdot3(do, v, ((2,), (2,))) - delta)
                dsb = ds.astype(BF16)
                dq = _bdot3(dsb, k, ((2,), (1,))) * SCALE
                dk = _bdot3(dsb, q, ((1,), (1,))) * SCALE
                for h in range(2):
                    db_ref[0, g * 2 + h] += sum(ds[2 * u + h] for u in range(ATTN_BLOCKS))
                for u, (st, stp) in enumerate(starts):
                    dq_refs[g][0, _ds(st, d), :] = _heads(dq, u)
                    dk_refs[g][0, _ds(stp, d), :] += _heads(dk[:, :QB], u)
                    dv_refs[g][0, _ds(stp, d), :] += _heads(dv[:, :QB], u)
                    dk_refs[g][0, _ds(st, d), :] += _heads(dk[:, QB:], u)
                    dv_refs[g][0, _ds(st, d), :] += _heads(dv[:, QB:], u)
                return c

            lax.fori_loop(0, s // QB // ATTN_BLOCKS, blk, 0)

    col = lambda w, g: (lambda hp, b: (b, 0, (w * 3 + g) * 4 + hp))
    blk_spec = pl.BlockSpec((1, s, LANE), lambda hp, b: (b, 0, hp))
    in_specs = [pl.BlockSpec(memory_space=pltpu.SMEM), pl.BlockSpec((3, QB, 2 * QB), lambda hp, b: (0, 0, 0))]
    in_specs += [pl.BlockSpec((1, s, LANE), col(w, g)) for w in range(3) for g in range(3)]
    in_specs += [blk_spec] * 3
    out_specs = [blk_spec] * 9 + [pl.BlockSpec((1, 6, QB, 2 * QB), lambda hp, b: (hp, 0, 0, 0))]
    out_shape = [SDS((bsz, s, WIDTH), F32)] * 9 + [SDS((4, 6, QB, 2 * QB), F32)]
    outs = pl.pallas_call(
        body, name="attn_bwd", grid=(4, bsz), in_specs=in_specs, out_specs=out_specs, out_shape=out_shape,
        scratch_shapes=[pltpu.VMEM((6, QB, 2 * QB), F32), pltpu.VMEM((s, LANE), F32)],
        compiler_params=_params(("parallel", "arbitrary")))(rel_bias, bidx, *([qkv3] * 9), o3, lse3, do3)
    return outs[:9], outs[9]


def _bias_grad(dbias, bidx):
    def body(db_ref, bidx_ref, o_ref):
        lane = lax.broadcasted_iota(jnp.int32, (1, LANE), 1)
        for g in range(3):
            bi = bidx_ref[g]
            for hp in range(4):
                for h in range(2):
                    mat = db_ref[hp, g * 2 + h]
                    row = jnp.zeros((1, LANE), F32)
                    for j in range(N_BUCKET):
                        part = jnp.sum(jnp.where(bi == j, mat, 0.0), axis=0, keepdims=True)
                        row = jnp.where(lane == j, jnp.sum(part, axis=1, keepdims=True), row)
                    hd = g * N_HEAD + hp * 2 + h
                    o_ref[hd:hd + 1, :] = row

    return pl.pallas_call(body, name="bias_grad", out_shape=SDS((3 * N_HEAD, LANE), F32), compiler_params=_params())(dbias, bidx)


def _pre_fn(r, k0, v, wl, al, w0, wup, a0, aup, kk_, ka_):
    u = w0 + _bdot(jnp.tanh(wl), wup)
    lw = -jnp.exp(-_softplus(-u) - 0.5)
    a = jax.nn.sigmoid(a0 + _bdot(al, aup))
    kkraw = k0 * kk_
    k = k0 * (1.0 + (a - 1.0) * ka_)
    return r, lw, k, v, kkraw, a


PRE_SPLIT = (0, WIDTH, 2 * WIDTH, 3 * WIDTH, 3 * WIDTH + LORA, 3 * WIDTH + 2 * LORA)


def _pre_pieces(prs):
    return [prs[:, a:b] for a, b in zip(PRE_SPLIT[:-1], PRE_SPLIT[1:])]


PRE_TT = 512


def _shifted(pr_ref, edge_ref, first, back):
    pr = pr_ref[0]
    tt = pr.shape[0]
    row = lax.broadcasted_iota(jnp.int32, (tt, 1), 0)
    if back:
        edge = jnp.where(first, 0.0, edge_ref[0, 7:8, :])
        return jnp.where(row == 0, edge, pltpu.roll(pr, 1, axis=0))
    edge = jnp.where(first, 0.0, edge_ref[0, 0:1, :])
    return jnp.where(row == tt - 1, edge, pltpu.roll(pr, tt - 1, axis=0))


def _rwkv_pre(pr3, mix, w0, wup, a0, aup, kk_, ka_):
    bsz, s, _ = pr3.shape
    tt = PRE_TT

    def body(pr_ref, edge_ref, mix_ref, w0_ref, wup_ref, a0_ref, aup_ref, kk_ref, ka_ref, *outs):
        pr = pr_ref[0]
        prev = _shifted(pr_ref, edge_ref, pl.program_id(1) == 0, True)
        prs = pr + (prev - pr) * mix_ref[...]
        vals = _pre_fn(*_pre_pieces(prs), w0_ref[...], wup_ref[...].astype(F32), a0_ref[...], aup_ref[...].astype(F32), kk_ref[...],
                       ka_ref[...])
        for o, val in zip(outs, vals):
            o[0] = val

    vec = lambda n: pl.BlockSpec((1, n), lambda b, i: (0, 0))
    mat = pl.BlockSpec((LORA, WIDTH), lambda b, i: (0, 0))
    in_specs = [pl.BlockSpec((1, tt, PR_COLS), lambda b, i: (b, i, 0)),
                pl.BlockSpec((1, 8, PR_COLS), lambda b, i: (b, jnp.maximum(i * (tt // 8) - 1, 0), 0)),
                vec(PR_COLS), vec(WIDTH), mat, vec(WIDTH), mat, vec(WIDTH), vec(WIDTH)]
    out_spec = pl.BlockSpec((1, tt, WIDTH), lambda b, i: (b, i, 0))
    return pl.pallas_call(
        body, name="rwkv_pre", grid=(bsz, s // tt), in_specs=in_specs, out_specs=[out_spec] * 6,
        out_shape=[SDS((bsz, s, WIDTH), F32)] * 6, compiler_params=_params(("parallel", "parallel")))(
            pr3, pr3, mix, w0, wup, a0, aup, kk_, ka_)


def _rwkv_pre_bwd(pr3, cots, mix, w0, wup, a0, aup, kk_, ka_):
    bsz, s, _ = pr3.shape
    tt = PRE_TT

    def body(pr_ref, edge_ref, c0, c1, c2, c3, c4, c5, mix_ref, w0_ref, wup_ref, a0_ref, aup_ref, kk_ref, ka_ref,
             dprs_ref, dmix_ref, dw0_ref, dwup_ref, da0_ref, daup_ref, dkk_ref, dka_ref):
        pr = pr_ref[0]
        prev = _shifted(pr_ref, edge_ref, pl.program_id(1) == 0, True)
        prs = pr + (prev - pr) * mix_ref[...]
        _, vjp = jax.vjp(_pre_fn, *_pre_pieces(prs), w0_ref[...], wup_ref[...].astype(F32), a0_ref[...], aup_ref[...].astype(F32),
                         kk_ref[...], ka_ref[...])
        grads = vjp(tuple(c[0] for c in (c0, c1, c2, c3, c4, c5)))
        for piece, a, b in zip(grads[:5], PRE_SPLIT[:-1], PRE_SPLIT[1:]):
            dprs_ref[0, :, a:b] = piece
        dw0, dwup, da0, daup, dkk, dka = grads[5:]
        dprs = dprs_ref[0]
        grads = (jnp.sum(dprs * (prev - pr), axis=0, keepdims=True), dw0, dwup, da0, daup, dkk, dka)
        refs = (dmix_ref, dw0_ref, dwup_ref, da0_ref, daup_ref, dkk_ref, dka_ref)
        first = jnp.logical_and(pl.program_id(0) == 0, pl.program_id(1) == 0)

        @pl.when(first)
        def _():
            for r_, g_ in zip(refs, grads):
                r_[...] = g_

        @pl.when(jnp.logical_not(first))
        def _():
            for r_, g_ in zip(refs, grads):
                r_[...] += g_

    vec = lambda n: pl.BlockSpec((1, n), lambda b, i: (0, 0))
    mat = pl.BlockSpec((LORA, WIDTH), lambda b, i: (0, 0))
    tile = pl.BlockSpec((1, tt, WIDTH), lambda b, i: (b, i, 0))
    in_specs = [pl.BlockSpec((1, tt, PR_COLS), lambda b, i: (b, i, 0)),
                pl.BlockSpec((1, 8, PR_COLS), lambda b, i: (b, jnp.maximum(i * (tt // 8) - 1, 0), 0))]
    in_specs += [tile] * 6 + [vec(PR_COLS), vec(WIDTH), mat, vec(WIDTH), mat, vec(WIDTH), vec(WIDTH)]
    out_specs = [pl.BlockSpec((1, tt, PR_COLS), lambda b, i: (b, i, 0)), vec(PR_COLS), vec(WIDTH), mat, vec(WIDTH), mat,
                 vec(WIDTH), vec(WIDTH)]
    out_shape = [SDS((bsz, s, PR_COLS), F32), SDS((1, PR_COLS), F32), SDS((1, WIDTH), F32), SDS((LORA, WIDTH), F32),
                 SDS((1, WIDTH), F32), SDS((LORA, WIDTH), F32), SDS((1, WIDTH), F32), SDS((1, WIDTH), F32)]
    return pl.pallas_call(
        body, name="rwkv_pre_bwd", grid=(bsz, s // tt), in_specs=in_specs, out_specs=out_specs, out_shape=out_shape,
        compiler_params=_params(("arbitrary", "arbitrary")))(pr3, pr3, *cots, mix, w0, wup, a0, aup, kk_, ka_)


def _shift_bwd(dprs3, mix):
    bsz, s, _ = dprs3.shape
    tt = PRE_TT
    nt = s // tt

    def body(d_ref, edge_ref, mix_ref, o_ref):
        nxt = _shifted(d_ref, edge_ref, pl.program_id(1) == nt - 1, False)
        m = mix_ref[...]
        o_ref[0] = d_ref[0] * (1.0 - m) + nxt * m

    in_specs = [pl.BlockSpec((1, tt, PR_COLS), lambda b, i: (b, i, 0)),
                pl.BlockSpec((1, 8, PR_COLS), lambda b, i: (b, jnp.minimum((i + 1) * (tt // 8), s // 8 - 1), 0)),
                pl.BlockSpec((1, PR_COLS), lambda b, i: (0, 0))]
    return pl.pallas_call(
        body, name="shift_bwd", grid=(bsz, nt), in_specs=in_specs, out_specs=pl.BlockSpec((1, tt, PR_COLS), lambda b, i: (b, i, 0)),
        out_shape=SDS((bsz, s, PR_COLS), F32), compiler_params=_params(("parallel", "parallel")))(dprs3, dprs3, mix)


_NN, _NT, _TN = ((2,), (1,)), ((2,), (2,)), ((1,), (1,))


def _dot3(a, b, dims, precision=HI3):
    return lax.dot_general(a, b, (dims, ((0,), (0,))), precision=precision, preferred_element_type=F32)


def _dot3_bf16(a, b, dims):
    return lax.dot_general(a.astype(BF16), b.astype(BF16), (dims, ((0,), (0,))), preferred_element_type=F32)


def _with_bf16_backward(dims, da_rule, db_rule):
    @jax.custom_vjp
    def f(a, b):
        return _dot3(a, b, dims)

    def bwd(res, g):
        a, b = res
        return da_rule(a, b, g), db_rule(a, b, g)

    f.defvjp(lambda a, b: (f(a, b), (a, b)), bwd)
    return f


_bmm = _with_bf16_backward(_NN, lambda a, b, g: _dot3_bf16(g, b, _NT), lambda a, b, g: _dot3_bf16(a, g, _TN))
_bmm_nt = _with_bf16_backward(_NT, lambda a, b, g: _dot3_bf16(g, b, _NN), lambda a, b, g: _dot3_bf16(g, a, _TN))
_bmm_tn = _with_bf16_backward(_TN, lambda a, b, g: _dot3_bf16(b, g, _NT), lambda a, b, g: _dot3_bf16(a, g, _NN))


def _chunk_fn(s0t, r, lw, k, v, kkraw, a, rk, lnw, lnb, first=False):
    c = r.shape[1]
    at, rt, btc, ktc, gc, aab, arb, xv, arkv, ain, bin_ = _chunk_core(r, lw, k, v, kkraw, a)
    rs = _bmm(jnp.concatenate([at, rt], axis=1), s0t)
    u = _solve(aab, rs[:, :c] + xv)
    y = rs[:, c:] + _bmm(arb, u) + arkv
    if first:
        y = _with_early_rows(y, r, lw, k, v, ain, bin_)
    gcol = jnp.sum(_diag(gc), axis=2, keepdims=True)
    sct = gcol * s0t + _bmm_tn(jnp.concatenate([btc, ktc], axis=1), jnp.concatenate([u, v], axis=1))
    return _post(y, r, k, v, rk, lnw, lnb), sct


def _diag(gc):
    return jnp.where(_masks(HEAD)[2], gc, 0.0)


def _with_early_rows(y, r, lw, k, v, ain, bin_):
    early = _stack([_early_rows(r[h], lw[h], k[h], v[h], ain[h], bin_[h]) for h in range(2)])
    return jnp.concatenate([jnp.concatenate([early, y[:2, EARLY:]], axis=1), y[2:]], axis=0)


def _early_rows(r, lw, k, v, ain, bin_):
    wc, bc, kc = jnp.transpose(jnp.exp(lw)), jnp.transpose(bin_), jnp.transpose(k)
    st = jnp.zeros((HEAD, HEAD), F32)
    rows = []
    for t in range(EARLY):
        sa = _bdot(ain[t:t + 1], st)
        st = st * wc[:, t:t + 1] + bc[:, t:t + 1] * sa + kc[:, t:t + 1] * v[t:t + 1]
        rows.append(_bdot(r[t:t + 1], st))
    return jnp.concatenate(rows, axis=0)


def _chunk_rows(c):
    return pl.ds(c * CHUNK, CHUNK) if isinstance(c, int) else pl.ds(pl.multiple_of(c * CHUNK, CHUNK), CHUNK)


def _stack(xs):
    return jnp.concatenate([x[None] for x in xs], axis=0)


def _pairs(ref, chunks):
    tiles = [ref[0, _chunk_rows(c), :] for c in chunks]
    return _stack([t[:, HEAD * h:HEAD * h + HEAD] for t in tiles for h in range(2)])


def _unpair(vals, j):
    return jnp.concatenate([vals[2 * j], vals[2 * j + 1]], axis=1)


def _masks(c):
    ii = lax.broadcasted_iota(jnp.int32, (c, c), 0)
    jj = lax.broadcasted_iota(jnp.int32, (c, c), 1)
    return ii > jj, ii >= jj, ii == jj


def _chunk_core(r, lw, k, v, kkraw, a):
    g_, c = r.shape[0], r.shape[1]
    nrm = jnp.sqrt(jnp.sum(kkraw * kkraw, axis=-1, keepdims=True))
    kkn = kkraw / jnp.maximum(nrm, 1e-12)
    ain, bin_ = -kkn, kkn * a
    strict, incl, _ = _masks(c)
    lg = lax.dot_general(jnp.broadcast_to(incl.astype(F32), (g_, c, c)), lw, (((2,), (1,)), ((0,), (0,))), precision=HI,
                         preferred_element_type=F32)
    g, gp, gi = jnp.exp(lg), jnp.exp(lg - lw), jnp.exp(-lg)
    at, rt, bt, kt = ain * gp, r * g, bin_ * gi, k * gi
    aa = _bmm_nt(jnp.concatenate([at, rt], axis=1), jnp.concatenate([bt, kt], axis=1))
    aab = jnp.where(strict, aa[:, :c, :c], 0.0)
    aak = jnp.where(strict, aa[:, :c, c:], 0.0)
    arb = jnp.where(incl, aa[:, c:, :c], 0.0)
    ark = jnp.where(incl, aa[:, c:, c:], 0.0)
    akv = _bmm(jnp.concatenate([aak, ark], axis=1), v)
    gc = g[:, c - 1:c, :]
    return at, rt, bt * gc, kt * gc, gc, aab, arb, akv[:, :c], akv[:, c:], ain, bin_


def _powers(aab):
    ps = [aab]
    while 2 ** len(ps) < aab.shape[1]:
        ps.append(_dot3(ps[-1], ps[-1], _NN))
    return ps


@jax.custom_vjp
def _solve(aab, z):
    for p in _powers(aab):
        z = z + _dot3(p, z, _NN)
    return z


def _solve_fwd(aab, z):
    ps = _powers(aab)
    for p in ps:
        z = z + _dot3(p, z, _NN)
    return z, (ps, z)


def _solve_bwd(res, g):
    ps, x = res
    for p in ps:
        g = g + _dot3(p, g, _TN)
    return _dot3(g, x, _NT), g


_solve.defvjp(_solve_fwd, _solve_bwd)


def _post(y, r, k, v, rk, lnw, lnb):
    mu = jnp.mean(y, axis=-1, keepdims=True)
    var = jnp.mean(jnp.square(y - mu), axis=-1, keepdims=True)
    yn = (y - mu) * lax.rsqrt(var + GN_EPS) * lnw + lnb
    return yn + jnp.sum(r * k * rk, axis=-1, keepdims=True) * v


def _chunk_consts(r, lw, k, v, kkraw, a, first=False):
    at, rt, btc, ktc, gc, aab, arb, xv, arkv, ain, bin_ = _chunk_core(r, lw, k, v, kkraw, a)
    z = _solve(aab, jnp.concatenate([at, xv], axis=2))
    ryv = jnp.concatenate([rt, arkv], axis=2) + _bmm(arb, z)
    if first:
        ryv = jnp.concatenate([ryv[:, :, :HEAD], _with_early_rows(ryv[:, :, HEAD:], r, lw, k, v, ain, bin_)], axis=2)
    mkv = _bmm_tn(btc, z) + jnp.concatenate([_diag(gc), _bmm_tn(ktc, v)], axis=2)
    return mkv, ryv


def _rwkv_scan(ins, rk, lnw, lnb):
    bsz, s, _ = ins[0].shape
    nch = s // CHUNK

    def consts_body(r_ref, lw_ref, k_ref, v_ref, kk_ref, a_ref, mkv_ref, ry_ref, yv_ref):
        def group(i, carry):
            chunks = [i * CHUNK_GROUP + j for j in range(CHUNK_GROUP)]
            mkv, ryv = _chunk_consts(*[_pairs(ref, chunks) for ref in (r_ref, lw_ref, k_ref, v_ref, kk_ref, a_ref)],
                                     first=isinstance(i, int) and i == 0)
            for j, c in enumerate(chunks):
                for h in range(2):
                    mkv_ref[0, 0, c, h] = mkv[2 * j + h]
                ry_ref[0, _chunk_rows(c), :] = jnp.concatenate([ryv[2 * j][:, :HEAD], ryv[2 * j + 1][:, :HEAD]], axis=1)
                yv_ref[0, _chunk_rows(c), :] = jnp.concatenate([ryv[2 * j][:, HEAD:], ryv[2 * j + 1][:, HEAD:]], axis=1)
            return carry

        group(0, 0)
        lax.fori_loop(1, nch // CHUNK_GROUP, group, 0)

    tile = pl.BlockSpec((1, s, LANE), lambda b, hp: (b, 0, hp))
    vec = pl.BlockSpec((1, LANE), lambda b, hp: (0, hp))
    mkv_spec = pl.BlockSpec((1, 1, nch, 2, HEAD, LANE), lambda b, hp: (b, hp, 0, 0, 0, 0))
    st_spec = pl.BlockSpec((1, 1, nch, 2, HEAD, HEAD), lambda b, hp: (b, hp, 0, 0, 0, 0))
    mkv, ry, yv = pl.pallas_call(
        consts_body, name="rwkv_consts", grid=(bsz, 4), in_specs=[tile] * 6, out_specs=[mkv_spec, tile, tile],
        out_shape=[SDS((bsz, 4, nch, 2, HEAD, LANE), F32), SDS((bsz, s, WIDTH), F32), SDS((bsz, s, WIDTH), F32)],
        compiler_params=_params(("parallel", "parallel")))(*ins)

    states = _chunk_recurrence(mkv, None, "rwkv_states")

    def out_body(ry_ref, yv_ref, r_ref, k_ref, v_ref, st_ref, rk_ref, lnw_ref, lnb_ref, o_ref):
        y, r, k, v, rk_, lnw_, lnb_ = _scan_rows(ry_ref, yv_ref, r_ref, k_ref, v_ref, st_ref, rk_ref, lnw_ref, lnb_ref)
        o = _post(y, r, k, v, rk_, lnw_, lnb_)
        for j in range(CHUNK_GROUP):
            o_ref[0, _chunk_rows(j), :] = _unpair(o, j)

    o = pl.pallas_call(
        out_body, name="rwkv_out", grid=(bsz, 4, nch // CHUNK_GROUP), in_specs=_group_specs(5), out_specs=_group_specs(1)[0],
        out_shape=SDS((bsz, s, WIDTH), F32),
        compiler_params=_params(("parallel", "parallel", "parallel")))(ry, yv, ins[0], ins[2], ins[3], states, rk, lnw, lnb)
    return o, states, (mkv, ry, yv)


def _group_specs(n_tiles):
    tile = pl.BlockSpec((1, CHUNK_GROUP * CHUNK, LANE), lambda b, hp, t: (b, t, hp))
    if n_tiles == 1:
        return [tile]
    st = pl.BlockSpec((1, 1, CHUNK_GROUP, 2, HEAD, HEAD), lambda b, hp, t: (b, hp, t, 0, 0, 0))
    vec = pl.BlockSpec((1, LANE), lambda b, hp, t: (0, hp))
    return [tile] * n_tiles + [st] + [vec] * 3


def _scan_rows(ry_ref, yv_ref, r_ref, k_ref, v_ref, st_ref, rk_ref, lnw_ref, lnb_ref):
    chunks = list(range(CHUNK_GROUP))
    ry, yv, r, k, v = (_pairs(ref, chunks) for ref in (ry_ref, yv_ref, r_ref, k_ref, v_ref))
    st = _stack([st_ref[0, 0, c, h] for c in chunks for h in range(2)])
    vecs = [_stack([ref[:, HEAD * h:HEAD * h + HEAD] for _ in chunks for h in range(2)]) for ref in (rk_ref, lnw_ref, lnb_ref)]
    return (_bmm(ry, st) + yv, r, k, v, *vecs)


def _chunk_recurrence(mkv, q, name):
    bsz, _, nch = mkv.shape[:3]
    pairs = [(hp, h) for hp in range(4) for h in range(2)]

    def body(*refs):
        mkv_ref, out_ref, acc = refs[0], refs[-2], refs[-1]
        acc[...] = jnp.zeros_like(acc)

        def step(i, carry):
            c = i if q is None else nch - 1 - i
            cur = acc[...]
            for j, (hp, h) in enumerate(pairs):
                out_ref[0, hp, c, h] = cur[j]
            m = _stack([mkv_ref[0, hp, c, h] for hp, h in pairs])
            if q is None:
                acc[...] = _bmm(m[:, :, :HEAD], cur) + m[:, :, HEAD:]
            else:
                acc[...] = _bmm_tn(m[:, :, :HEAD], cur) + _stack([refs[1][0, hp, c, h] for hp, h in pairs])
            return carry

        lax.fori_loop(0, nch, step, 0)

    spec = lambda w: pl.BlockSpec((1, 4, nch, 2, HEAD, w), lambda b: (b, 0, 0, 0, 0, 0))
    return pl.pallas_call(
        body, name=name, grid=(bsz,), in_specs=[spec(LANE)] + ([] if q is None else [spec(HEAD)]), out_specs=spec(HEAD),
        out_shape=SDS((bsz, 4, nch, 2, HEAD, HEAD), F32), scratch_shapes=[pltpu.VMEM((8, HEAD, HEAD), F32)],
        compiler_params=_params(("parallel",)))(*([mkv] if q is None else [mkv, q]))


def _rwkv_scan_bwd(ins, states, consts, do3, rk, lnw, lnb):
    bsz, s, _ = ins[0].shape
    nch = s // CHUNK

    mkv, ry, yv = consts

    def q_body(do_ref, ry_ref, yv_ref, r_ref, k_ref, v_ref, st_ref, rk_ref, lnw_ref, lnb_ref, q_ref):
        y, r, k, v, rk_, lnw_, lnb_ = _scan_rows(ry_ref, yv_ref, r_ref, k_ref, v_ref, st_ref, rk_ref, lnw_ref, lnb_ref)
        _, vjp = jax.vjp(lambda y_: _post(y_, r, k, v, rk_, lnw_, lnb_), y)
        (dy,) = vjp(_pairs(do_ref, list(range(CHUNK_GROUP))))
        q = _bmm_tn(_pairs(ry_ref, list(range(CHUNK_GROUP))), dy)
        for j in range(CHUNK_GROUP):
            for h in range(2):
                q_ref[0, 0, j, h] = q[2 * j + h]

    specs = _group_specs(6)
    q = pl.pallas_call(
        q_body, name="rwkv_q", grid=(bsz, 4, nch // CHUNK_GROUP), in_specs=specs, out_specs=specs[6],
        out_shape=SDS((bsz, 4, nch, 2, HEAD, HEAD), F32),
        compiler_params=_params(("parallel", "parallel", "parallel")))(do3, ry, yv, ins[0], ins[2], ins[3], states, rk, lnw, lnb)

    dstates = _chunk_recurrence(mkv, q, "rwkv_dstates")

    def body(r_ref, lw_ref, k_ref, v_ref, kk_ref, a_ref, st_ref, dst_ref, do_ref, rk_ref, lnw_ref, lnb_ref,
             dr_ref, dlw_ref, dk_ref, dv_ref, dkk_ref, da_ref, drk_ref, dlnw_ref, dlnb_ref):
        chunks = list(range(BWD_GROUP))
        par_refs = (drk_ref, dlnw_ref, dlnb_ref)

        @pl.when(jnp.logical_and(pl.program_id(1) == 0, pl.program_id(2) == 0))
        def _():
            for ref in par_refs:
                ref[...] = jnp.zeros_like(ref)

        def group(first):
            per_pair = lambda ref: _stack([ref[0, 0, c, h] for c in chunks for h in range(2)])
            vecs = [_stack([ref[:, HEAD * h:HEAD * h + HEAD] for _ in chunks for h in range(2)]) for ref in (rk_ref, lnw_ref, lnb_ref)]
            _, vjp = jax.vjp(functools.partial(_chunk_fn, first=first), per_pair(st_ref),
                             *[_pairs(ref, chunks) for ref in (r_ref, lw_ref, k_ref, v_ref, kk_ref, a_ref)], *vecs)
            grads = vjp((_pairs(do_ref, chunks), per_pair(dst_ref)))
            for ref, cot in zip((dr_ref, dlw_ref, dk_ref, dv_ref, dkk_ref, da_ref), grads[1:7]):
                for j, c in enumerate(chunks):
                    ref[0, _chunk_rows(c), :] = _unpair(cot, j)
            for ref, g_ in zip(par_refs, grads[7:10]):
                ref[...] += jnp.concatenate([sum(g_[2 * j + h] for j in range(BWD_GROUP)) for h in range(2)], axis=1)

        pl.when(pl.program_id(2) == 0)(functools.partial(group, True))
        pl.when(pl.program_id(2) != 0)(functools.partial(group, False))

    tt = BWD_GROUP * CHUNK
    tile = pl.BlockSpec((1, tt, LANE), lambda hp, b, t: (b, t, hp))
    vec = pl.BlockSpec((1, LANE), lambda hp, b, t: (0, hp))
    st_spec = pl.BlockSpec((1, 1, BWD_GROUP, 2, HEAD, HEAD), lambda hp, b, t: (b, hp, t, 0, 0, 0))
    outs = pl.pallas_call(
        body, name="rwkv_scan_bwd", grid=(4, bsz, s // tt), in_specs=[tile] * 6 + [st_spec, st_spec, tile] + [vec] * 3,
        out_specs=[tile] * 6 + [vec] * 3,
        out_shape=[SDS((bsz, s, WIDTH), F32)] * 6 + [SDS((1, WIDTH), F32)] * 3,
        compiler_params=_params(("parallel", "arbitrary", "arbitrary")))(*ins, states, dstates, do3, rk, lnw, lnb)
    return outs[:6], outs[6:]


def _head(o_attn, o_rwkv, z_attn, z_rwkv, gm, x2, tgt, wua, wur, wout, g2):
    n = x2.shape[0]
    tm = 256
    nt = n // tm
    d = D_MODEL

    def body(oa_ref, or_ref, za_ref, zr_ref, gm_ref, x_ref, t_ref, wua_ref, wur_ref, wout_ref, g2_ref,
             dxo_ref, doa_ref, dor_ref, dza_ref, dzr_ref, dgm_ref, dwua_ref, dwur_ref, dwout_ref, dg2_ref, loss_ref, lacc):
        i = pl.program_id(0)
        oa, orw, za, zr = oa_ref[...], or_ref[...], za_ref[...], zr_ref[...]
        ga, gb = gm_ref[:, 0:d], gm_ref[:, d:2 * d]
        am = (oa * _silu(za)).astype(BF16)
        bm = (orw * _silu(zr)).astype(BF16)
        ya, yb = _dot(am, wua_ref[...]), _dot(bm, wur_ref[...])
        sa, sb = jax.nn.sigmoid(ga), jax.nn.sigmoid(gb)
        merged = (sa * ya + sb * yb).astype(BF16)
        out = _dot(merged, wout_ref[...])
        rs = lax.rsqrt(jnp.mean(out * out, axis=-1, keepdims=True) + RMS_EPS)
        g2 = g2_ref[...]
        err = x_ref[...] + out * rs * g2 - t_ref[...]
        lpart = jnp.sum(err * err, axis=0, keepdims=True)
        dxo = err * (1.0 / d)
        dxo_ref[...] = dxo
        dg2 = jnp.sum(dxo * out * rs, axis=0, keepdims=True)
        gd = dxo * g2
        dout = (rs * (gd - out * (rs * rs) * jnp.mean(gd * out, axis=-1, keepdims=True))).astype(BF16)
        dmerged = _dot_nt(dout, wout_ref[...])
        dwout = _dot_tn(merged, dout)
        dya, dyb = (dmerged * sa).astype(BF16), (dmerged * sb).astype(BF16)
        dgm_ref[:, 0:d] = dmerged * ya * sa * (1.0 - sa)
        dgm_ref[:, d:2 * d] = dmerged * yb * sb * (1.0 - sb)
        dam, dbm = _dot_nt(dya, wua_ref[...]), _dot_nt(dyb, wur_ref[...])
        dwua, dwur = _dot_tn(am, dya), _dot_tn(bm, dyb)
        doa_ref[...] = dam * _silu(za)
        dza_ref[...] = dam * oa * _dsilu(za)
        dor_ref[...] = dbm * _silu(zr)
        dzr_ref[...] = dbm * orw * _dsilu(zr)

        @pl.when(i == 0)
        def _():
            dwua_ref[...], dwur_ref[...], dwout_ref[...], dg2_ref[...], lacc[...] = dwua, dwur, dwout, dg2, lpart

        @pl.when(i != 0)
        def _():
            dwua_ref[...] += dwua
            dwur_ref[...] += dwur
            dwout_ref[...] += dwout
            dg2_ref[...] += dg2
            lacc[...] += lpart

        @pl.when(i == nt - 1)
        def _():
            loss_ref[...] = jnp.sum(lacc[...], axis=1, keepdims=True) * (0.5 / d)

    t512 = pl.BlockSpec((tm, WIDTH), lambda i: (i, 0))
    t1k = pl.BlockSpec((tm, d), lambda i: (i, 0))
    t2k = pl.BlockSpec((tm, 2 * d), lambda i: (i, 0))
    full = lambda r, c: pl.BlockSpec((r, c), lambda i: (0, 0))
    return pl.pallas_call(
        body, name="head_fwd_bwd", grid=(nt,),
        in_specs=[t512, t512, t512, t512, t2k, t1k, t1k, full(WIDTH, d), full(WIDTH, d), full(d, d), full(1, d)],
        out_specs=[t1k, t512, t512, t512, t512, t2k, full(WIDTH, d), full(WIDTH, d), full(d, d), full(1, d), full(1, 1)],
        out_shape=[SDS((n, d), F32)] + [SDS((n, WIDTH), F32)] * 4 + [SDS((n, 2 * d), F32), SDS((WIDTH, d), F32), SDS((WIDTH, d), F32),
                                                                    SDS((d, d), F32), SDS((1, d), F32), SDS((1, 1), F32)],
        scratch_shapes=[pltpu.VMEM((1, d), F32)],
        compiler_params=_params(("arbitrary",)))(o_attn, o_rwkv, z_attn, z_rwkv, gm, x2, tgt, wua, wur, wout, g2)


def _prenorm_bwd(dh, x2, rs, g1, dxo):
    n, d = x2.shape
    tm = 1024

    def body(dh_ref, x_ref, rs_ref, g_ref, dxo_ref, gx_ref, dg_ref):
        x, r = x_ref[...], rs_ref[...]
        gd = dh_ref[...] * g_ref[...]
        gx_ref[...] = dxo_ref[...] + r * (gd - x * (r * r) * jnp.mean(gd * x, axis=-1, keepdims=True))
        dg = jnp.sum(dh_ref[...] * x * r, axis=0, keepdims=True)

        @pl.when(pl.program_id(0) == 0)
        def _():
            dg_ref[...] = dg

        @pl.when(pl.program_id(0) != 0)
        def _():
            dg_ref[...] += dg

    t = pl.BlockSpec((tm, d), lambda i: (i, 0))
    return pl.pallas_call(
        body, name="prenorm_bwd", grid=(n // tm,),
        in_specs=[t, t, pl.BlockSpec((tm, 1), lambda i: (i, 0)), pl.BlockSpec((1, d), lambda i: (0, 0)), t],
        out_specs=[t, pl.BlockSpec((1, d), lambda i: (0, 0))], out_shape=[SDS((n, d), F32), SDS((1, d), F32)],
        compiler_params=_params(("arbitrary",)))(dh, x2, rs, g1, dxo)


def _mesh_pos():
    x, y, c = lax.axis_index("x"), lax.axis_index("y"), lax.axis_index("c")
    return 4 * x + 2 * y + c


def _coords(idx):
    return (idx // 4, (idx // 2) % 2, idx % 2)


def _exchange(srcs, to_all, name):
    n = len(srcs)

    def body(*refs):
        src_refs, dst_refs = refs[:n], refs[n:2 * n]
        send_sems, recv_sems, local_sems = refs[2 * n:]
        me = _mesh_pos()

        def piece(i, j):
            return src_refs[i] if to_all[i] else src_refs[i].at[j]

        def remote(i, off, peer, block, slot):
            return pltpu.make_async_remote_copy(src_ref=piece(i, block), dst_ref=dst_refs[i].at[slot],
                                                send_sem=send_sems.at[i, off - 1], recv_sem=recv_sems.at[i, off - 1],
                                                device_id=_coords(peer), device_id_type=MESH)

        local = [pltpu.make_async_copy(piece(i, me), dst_refs[i].at[me], local_sems.at[i]) for i in range(n)]
        for cp in local:
            cp.start()
        sends = []
        for off in range(1, N_DEV):
            to = (me + off) % N_DEV
            for i in range(n):
                sends.append(remote(i, off, to, to, me))
                sends[-1].start()
        for off in range(1, N_DEV):
            frm = (me + N_DEV - off) % N_DEV
            for i in range(n):
                remote(i, off, frm, me, frm).wait_recv()
        for cp in sends:
            cp.wait_send()
        for cp in local:
            cp.wait()

    outs = pl.pallas_call(
        body, name=name, in_specs=[pl.BlockSpec(memory_space=pltpu.HBM)] * n, out_specs=[pl.BlockSpec(memory_space=pltpu.HBM)] * n,
        out_shape=[SDS((N_DEV,) + s.shape[-2:], s.dtype) for s in srcs],
        scratch_shapes=[pltpu.SemaphoreType.DMA((n, N_DEV - 1)), pltpu.SemaphoreType.DMA((n, N_DEV - 1)), pltpu.SemaphoreType.DMA((n,))],
        compiler_params=pltpu.CompilerParams())(*srcs)
    return outs


def _adamw(parts, w, m, v, tr, name):
    rows, cols = w.shape
    c1, c2 = 1.0 - ADAM_B1 ** ADAM_STEP, 1.0 - ADAM_B2 ** ADAM_STEP

    def body(p_ref, w_ref, m_ref, v_ref, g_ref, d_ref, nm_ref, nv_ref):
        g = p_ref[0].astype(F32)
        for j in range(1, N_DEV):
            g = g + p_ref[j].astype(F32)
        nm = ADAM_B1 * m_ref[...] + (1.0 - ADAM_B1) * g
        nv = ADAM_B2 * v_ref[...] + (1.0 - ADAM_B2) * jnp.square(g)
        g_ref[...] = g
        nm_ref[...] = nm
        nv_ref[...] = nv
        d_ref[...] = -ADAM_LR * ((nm / c1) / (jnp.sqrt(nv / c2) + ADAM_EPS) + ADAM_WD * w_ref[...])

    t = pl.BlockSpec((tr, cols), lambda i: (i, 0))
    return pl.pallas_call(
        body, name=name, grid=(rows // tr,), in_specs=[pl.BlockSpec((N_DEV, tr, cols), lambda i: (0, i, 0)), t, t, t],
        out_specs=[t] * 4, out_shape=[SDS((rows, cols), F32)] * 4, compiler_params=_params(("parallel",)))(parts, w, m, v)


SHARDED = (("w_in", D_MODEL, IN_COLS // N_DEV, True, 128), ("w_up_attn", WIDTH, D_MODEL // N_DEV, True, WIDTH),
           ("w_up_rwkv", WIDTH, D_MODEL // N_DEV, True, WIDTH), ("w_out", D_MODEL // N_DEV, D_MODEL, False, D_MODEL // N_DEV),
           ("rwkv_w_up", LORA, WIDTH // N_DEV, True, LORA), ("rwkv_a_up", LORA, WIDTH // N_DEV, True, LORA))
LOSS_SLOT = sum(n for _, n in SMALL)


def _pack_small(small, extra=None):
    flat = [small[n].reshape(-1).astype(F32) for n, _ in SMALL]
    flat.append(jnp.zeros((1,), F32) if extra is None else extra.reshape(1))
    flat.append(jnp.zeros((SMALL_ROWS * LANE - LOSS_SLOT - 1,), F32))
    return jnp.concatenate(flat).reshape(SMALL_ROWS, LANE)


def _unpack_small(packed, shapes):
    flat = packed.reshape(-1)
    out, off = {}, 0
    for n, cnt in SMALL:
        out[n] = flat[off:off + cnt].reshape(shapes[n])
        off += cnt
    return out, flat[LOSS_SLOT]


def _whole(gathered, by_cols):
    if not by_cols:
        return gathered.reshape(-1, gathered.shape[-1])
    return gathered.transpose(1, 0, 2).reshape(gathered.shape[1], -1)


def _per_owner(full, by_cols):
    if not by_cols:
        return full.reshape(N_DEV, -1, full.shape[-1])
    return full.reshape(full.shape[0], N_DEV, -1).transpose(1, 0, 2)


def _local_step(x, loss_target, sm, wts):
    bsz, s, d = x.shape
    n = bsz * s
    x2, tgt = x.reshape(n, d), loss_target.reshape(n, d)
    bidx = jnp.asarray(_bucket_tables())
    w_in = wts["w_in"]
    segs = (("qkv", 0, QKV_COLS, 512), ("za", OFF_ZA, WIDTH, 512), ("pr", OFF_PR, PR_COLS, PR_COLS), ("zr", OFF_ZR, WIDTH, 512),
            ("gm", OFF_GM, 2 * D_MODEL, 512))

    h, rs = _prenorm(x2, sm["pre_norm_gain"])
    proj = {nm: _mm(h, w_in[:, off:off + cnt], tn, "proj_" + nm) for nm, off, cnt, tn in segs}
    qkv3 = proj["qkv"].reshape(bsz, s, QKV_COLS)
    pr3 = proj["pr"].reshape(bsz, s, PR_COLS)

    o_attn, lse = _attn_fwd(qkv3, sm["rel_bias"], bidx)
    rk = sm["rwkv_r_k"].reshape(1, WIDTH)
    pre_args = (sm["rwkv_shift_mix"], sm["rwkv_w0"], wts["rwkv_w_up"], sm["rwkv_a0"], wts["rwkv_a_up"], sm["rwkv_k_k"], sm["rwkv_k_a"])
    scan_in = _rwkv_pre(pr3, *pre_args)
    o_rwkv, states, consts = _rwkv_scan(scan_in, rk, sm["rwkv_ln_w"], sm["rwkv_ln_b"])

    (dxo, do_attn, do_rwkv, dza, dzr, dgm, g_wua, g_wur, g_wout, g_post, loss) = _head(
        o_attn.reshape(n, WIDTH), o_rwkv.reshape(n, WIDTH), proj["za"], proj["zr"], proj["gm"], x2, tgt,
        wts["w_up_attn"], wts["w_up_rwkv"], wts["w_out"], sm["post_norm_gain"])

    dqkv, dbias = _attn_bwd(qkv3, o_attn, lse, do_attn.reshape(bsz, s, WIDTH), sm["rel_bias"], bidx)
    g_bias = _bias_grad(dbias, bidx)[:, :N_BUCKET].T

    scan_cots, (g_rk, g_lnw, g_lnb) = _rwkv_scan_bwd(scan_in, states, consts, do_rwkv.reshape(bsz, s, WIDTH), rk, sm["rwkv_ln_w"],
                                                     sm["rwkv_ln_b"])
    dprs, g_mix, g_w0, g_wup, g_a0, g_aup, g_kk, g_ka = _rwkv_pre_bwd(pr3, scan_cots, *pre_args)
    dpr = _shift_bwd(dprs, sm["rwkv_shift_mix"]).reshape(n, PR_COLS)

    dsegs = [(jnp.concatenate([t.reshape(n, WIDTH) for t in dqkv], axis=1), 0, 512), (dza, OFF_ZA, WIDTH), (dpr, OFF_PR, PR_COLS),
             (dzr, OFF_ZR, WIDTH), (dgm, OFF_GM, D_MODEL)]
    dh = None
    g_win = []
    for j, (t, off, tn) in enumerate(dsegs):
        dh = _mm_nt_acc(t, w_in[:, off:off + t.shape[1]], dh, "dh_%d" % j)
        g_win.append(_mm_tn(h, t, tn, "gw_in_%d" % j))
    grad_x, g_pre = _prenorm_bwd(dh, x2, rs, sm["pre_norm_gain"], dxo)

    full = {"w_in": jnp.concatenate(g_win, axis=1), "w_up_attn": g_wua, "w_up_rwkv": g_wur, "w_out": g_wout,
            "rwkv_w_up": g_wup, "rwkv_a_up": g_aup}
    small = {"pre_norm_gain": g_pre, "rel_bias": g_bias, "rwkv_shift_mix": g_mix, "rwkv_w0": g_w0, "rwkv_a0": g_a0, "rwkv_k_k": g_kk,
             "rwkv_k_a": g_ka, "rwkv_r_k": g_rk, "rwkv_ln_w": g_lnw, "rwkv_ln_b": g_lnb, "post_norm_gain": g_post}
    return loss[0, 0], grad_x.reshape(bsz, s, d), full, small


def kernel(x, pre_norm_gain, w_in, rel_bias, rwkv_shift_mix, rwkv_w0, rwkv_w_up, rwkv_a0, rwkv_a_up, rwkv_k_k, rwkv_k_a, rwkv_r_k, rwkv_ln_w, rwkv_ln_b, w_up_attn, w_up_rwkv, w_out, post_norm_gain, loss_target, m_pre_norm_gain, m_w_in, m_rel_bias, m_rwkv_shift_mix, m_rwkv_w0, m_rwkv_w_up, m_rwkv_a0, m_rwkv_a_up, m_rwkv_k_k, m_rwkv_k_a, m_rwkv_r_k, m_rwkv_ln_w, m_rwkv_ln_b, m_w_up_attn, m_w_up_rwkv, m_w_out, m_post_norm_gain, v_pre_norm_gain, v_w_in, v_rel_bias, v_rwkv_shift_mix, v_rwkv_w0, v_rwkv_w_up, v_rwkv_a0, v_rwkv_a_up, v_rwkv_k_k, v_rwkv_k_a, v_rwkv_r_k, v_rwkv_ln_w, v_rwkv_ln_b, v_w_up_attn, v_w_up_rwkv, v_w_out, v_post_norm_gain):
    names = [n for n, *_ in SHARDED] + [n for n, _ in SMALL]
    loc = dict(locals())
    w = {n: loc[n] for n in names}
    m = {n: loc["m_" + n] for n in names}
    v = {n: loc["v_" + n] for n in names}
    shapes = {n: w[n].shape for n in names}
    order = ["pre_norm_gain", "w_in", "rel_bias", "rwkv_shift_mix", "rwkv_w0", "rwkv_w_up", "rwkv_a0", "rwkv_a_up", "rwkv_k_k", "rwkv_k_a",
             "rwkv_r_k", "rwkv_ln_w", "rwkv_ln_b", "w_up_attn", "w_up_rwkv", "w_out", "post_norm_gain"]
    shard2d = lambda t, n, r, c: t[n].reshape(r, c)

    gathered = _exchange([shard2d(w, n, r, c).astype(BF16) for n, r, c, _, _ in SHARDED], [True] * len(SHARDED), "gather_weights")
    wts = {n: _whole(g, by_cols) for (n, _, _, by_cols, _), g in zip(SHARDED, gathered)}

    loss, grad_x, full, small = _local_step(x, loss_target, w, wts)
    parts = _exchange([_per_owner(full[n], by_cols).astype(BF16) for n, _, _, by_cols, _ in SHARDED] + [_pack_small(small, loss)],
                      [False] * len(SHARDED) + [True], "exchange_grads")

    outs = [{}, {}, {}, {}]
    for (n, r, c, _, tr), p in zip(SHARDED, parts):
        res = _adamw(p, shard2d(w, n, r, c), shard2d(m, n, r, c), shard2d(v, n, r, c), tr, "adamw_" + n)
        for o, t in zip(outs, res):
            o[n] = t.reshape(shapes[n])
    res = _adamw(parts[-1], _pack_small(w), _pack_small(m), _pack_small(v), SMALL_ROWS, "adamw_small")
    for o, t in zip(outs, res):
        o.update(_unpack_small(t, shapes)[0])
    loss = _unpack_small(res[0], shapes)[1]
    return (loss, grad_x, *[o[n] for o in outs for n in order])
```

```python
import functools
import math

import numpy as np
import jax
import jax.numpy as jnp
from jax import lax
from jax.experimental import pallas as pl
from jax.experimental.pallas import tpu as pltpu

F32, BF16 = jnp.float32, jnp.bfloat16
SDS = jax.ShapeDtypeStruct
HI = lax.Precision.HIGHEST
HI3 = lax.Precision.HIGH
MESH = pl.DeviceIdType.MESH

N_DEV = 8
D_MODEL = 1024
HEAD = 64
N_HEAD = 8
WIDTH = N_HEAD * HEAD
DILATIONS = (1, 4, 16)
QB = 128
N_BUCKET = 32
MAX_DIST = 2048
LORA = 64
QKV_COLS = 9 * WIDTH
PR_COLS = 3 * WIDTH + 2 * LORA
IN_COLS = QKV_COLS + WIDTH + PR_COLS + WIDTH + 2 * D_MODEL
OFF_ZA, OFF_PR, OFF_ZR, OFF_GM = QKV_COLS, QKV_COLS + WIDTH, QKV_COLS + WIDTH + PR_COLS, QKV_COLS + 2 * WIDTH + PR_COLS
RMS_EPS = 1e-6
GN_EPS = 64e-5
SCALE = 1.0 / math.sqrt(HEAD)
CHUNK = 64
CHUNK_GROUP = 4
BWD_GROUP = 4
EARLY = 8
NEG = -1e30
LANE = 128

ADAM_LR, ADAM_B1, ADAM_B2, ADAM_EPS, ADAM_WD, ADAM_STEP = 0.001, 0.9, 0.999, 1e-08, 0.01, 10

VMEM_LIMIT = 56 * 1024 * 1024

SMALL = (("pre_norm_gain", 1024), ("rel_bias", 768), ("rwkv_shift_mix", 1664), ("rwkv_w0", 512), ("rwkv_a0", 512),
         ("rwkv_k_k", 512), ("rwkv_k_a", 512), ("rwkv_r_k", 512), ("rwkv_ln_w", 512), ("rwkv_ln_b", 512),
         ("post_norm_gain", 1024))
SMALL_ROWS = 64


def _params(sem=None):
    return pltpu.CompilerParams(dimension_semantics=sem, vmem_limit_bytes=VMEM_LIMIT)


def _dot(a, b):
    return jnp.dot(a, b, preferred_element_type=F32)


def _dot_nt(a, b):
    return lax.dot_general(a, b, (((1,), (1,)), ((), ())), preferred_element_type=F32)


def _dot_tn(a, b):
    return lax.dot_general(a, b, (((0,), (0,)), ((), ())), preferred_element_type=F32)


@jax.custom_vjp
def _bdot(a, b):
    return _dot(a.astype(BF16), b.astype(BF16))


def _bdot_fwd(a, b):
    return _bdot(a, b), (a, b)


def _bdot_bwd(res, g):
    a, b = res
    gb = g.astype(BF16)
    return _dot_nt(gb, b.astype(BF16)), _dot_tn(a.astype(BF16), gb)


_bdot.defvjp(_bdot_fwd, _bdot_bwd)


def _silu(z):
    return z * jax.nn.sigmoid(z)


def _dsilu(z):
    s = jax.nn.sigmoid(z)
    return s * (1.0 + z * (1.0 - s))


def _softplus(x):
    return jnp.maximum(x, 0.0) + jnp.log(1.0 + jnp.exp(-jnp.abs(x)))


def _bucket_tables():
    qi = np.arange(QB)[:, None] + QB
    ki = np.arange(2 * QB)[None, :]
    rel = np.maximum(qi - ki, 0)
    out = []
    for d in DILATIONS:
        dist = rel * d
        max_exact = N_BUCKET // 2
        ratio = np.log(np.maximum(dist, 1).astype(np.float32) / max_exact) / np.float32(math.log(MAX_DIST / max_exact))
        large = max_exact + (ratio * (N_BUCKET - max_exact)).astype(np.int32)
        large = np.minimum(large, N_BUCKET - 1)
        out.append(np.where(dist < max_exact, dist, large).astype(np.int32))
    return np.stack(out)


def _prenorm(x2, g):
    n, d = x2.shape
    tm = 1024

    def body(x_ref, g_ref, h_ref, rs_ref):
        x = x_ref[...]
        rs = lax.rsqrt(jnp.mean(x * x, axis=-1, keepdims=True) + RMS_EPS)
        h_ref[...] = (x * rs * g_ref[...]).astype(BF16)
        rs_ref[...] = rs

    return pl.pallas_call(
        body, name="prenorm", grid=(n // tm,),
        in_specs=[pl.BlockSpec((tm, d), lambda i: (i, 0)), pl.BlockSpec((1, d), lambda i: (0, 0))],
        out_specs=[pl.BlockSpec((tm, d), lambda i: (i, 0)), pl.BlockSpec((tm, 1), lambda i: (i, 0))],
        out_shape=[SDS((n, d), BF16), SDS((n, 1), F32)], compiler_params=_params(("parallel",)))(x2, g)


def _mm(a, b, tn, name):
    m, k = a.shape
    n = b.shape[1]
    tm = 1024

    def body(a_ref, b_ref, o_ref):
        o_ref[...] = _dot(a_ref[...], b_ref[...])

    return pl.pallas_call(
        body, name=name, grid=(n // tn, m // tm),
        in_specs=[pl.BlockSpec((tm, k), lambda j, i: (i, 0)), pl.BlockSpec((k, tn), lambda j, i: (0, j))],
        out_specs=pl.BlockSpec((tm, tn), lambda j, i: (i, j)),
        out_shape=SDS((m, n), F32), compiler_params=_params(("parallel", "parallel")))(a, b)


def _mm_nt_acc(a, b, acc, name):
    split = a.ndim == 3
    m = a.shape[-2]
    k = b.shape[1]
    d = b.shape[0]
    tm = 1024
    tk = a.shape[2] if split else (k if k <= 2048 else 1536)
    have_acc = acc is not None

    def body(*refs):
        if have_acc:
            a_ref, b_ref, c_ref, o_ref = refs
        else:
            a_ref, b_ref, o_ref = refs
        r = _dot_nt((a_ref[0] if split else a_ref[...]).astype(BF16), b_ref[...])

        @pl.when(pl.program_id(1) == 0)
        def _():
            o_ref[...] = r + c_ref[...] if have_acc else r

        @pl.when(pl.program_id(1) != 0)
        def _():
            o_ref[...] += r

    a_spec = pl.BlockSpec((1, tm, tk), lambda i, j: (j, i, 0)) if split else pl.BlockSpec((tm, tk), lambda i, j: (i, j))
    in_specs = [a_spec, pl.BlockSpec((d, tk), lambda i, j: (0, j))]
    args = [a, b]
    if have_acc:
        in_specs.append(pl.BlockSpec((tm, d), lambda i, j: (i, 0)))
        args.append(acc)
    return pl.pallas_call(
        body, name=name, grid=(m // tm, k // tk), in_specs=in_specs, out_specs=pl.BlockSpec((tm, d), lambda i, j: (i, 0)),
        out_shape=SDS((m, d), F32), compiler_params=_params(("parallel", "arbitrary")))(*args)


def _mm_tn(a, b, tn, name):
    split = b.ndim == 3
    m, k1 = a.shape
    n2 = b.shape[0] * b.shape[2] if split else b.shape[1]
    tm = 1024

    def body(a_ref, b_ref, o_ref):
        r = _dot_tn(a_ref[...], (b_ref[0] if split else b_ref[...]).astype(BF16))

        @pl.when(pl.program_id(1) == 0)
        def _():
            o_ref[...] = r

        @pl.when(pl.program_id(1) != 0)
        def _():
            o_ref[...] += r

    b_spec = pl.BlockSpec((1, tm, tn), lambda j, i: (j, i, 0)) if split else pl.BlockSpec((tm, tn), lambda j, i: (i, j))
    return pl.pallas_call(
        body, name=name, grid=(n2 // tn, m // tm),
        in_specs=[pl.BlockSpec((tm, k1), lambda j, i: (i, 0)), b_spec],
        out_specs=pl.BlockSpec((k1, tn), lambda j, i: (0, j)),
        out_shape=SDS((k1, n2), F32), compiler_params=_params(("parallel", "arbitrary")))(a, b)


def _ds(start, d):
    return pl.ds(start, QB) if d == 1 else pl.ds(start, QB, stride=d)


def _fill_bias(tab_ref, bidx_ref, bias_sc, hp):
    for g in range(3):
        bi = bidx_ref[g]
        for h in range(2):
            acc = jnp.zeros((QB, 2 * QB), F32)
            for j in range(N_BUCKET):
                acc = jnp.where(bi == j, tab_ref[j, g * N_HEAD + hp * 2 + h], acc)
            bias_sc[g * 2 + h] = acc


def _block_starts(it, d, nb):
    rho = it // nb
    n = it % nb
    st = rho + d * QB * n
    stp = rho + d * QB * jnp.maximum(n - 1, 0)
    return st, stp, n > 0


ATTN_BLOCKS = 2


def _bdot3(a, b, dims):
    return lax.dot_general(a, b, (dims, ((0,), (0,))), preferred_element_type=F32)


def _attn_operands(q_ref, k_ref, v_ref, bias_sc, g, d, nb, it0):
    ii = lax.broadcasted_iota(jnp.int32, (QB, 2 * QB), 0)
    cc = lax.broadcasted_iota(jnp.int32, (QB, 2 * QB), 1)
    qs, ks, vs, pens, starts = [], [], [], [], []
    for u in range(ATTN_BLOCKS):
        st, stp, hasprev = _block_starts(it0 + u, d, nb)
        qf = q_ref[0, _ds(st, d), :]
        kf = jnp.concatenate([k_ref[0, _ds(stp, d), :], k_ref[0, _ds(st, d), :]], axis=0)
        vf = jnp.concatenate([v_ref[0, _ds(stp, d), :], v_ref[0, _ds(st, d), :]], axis=0)
        own = jnp.logical_and(cc >= QB, ii >= cc - QB)
        prev = jnp.logical_and(jnp.logical_and(cc < QB, cc >= ii), hasprev)
        pen = jnp.where(jnp.logical_or(own, prev), 0.0, NEG)
        for h in range(2):
            sl = slice(HEAD * h, HEAD * h + HEAD)
            qs.append(qf[:, sl])
            ks.append(kf[:, sl])
            vs.append(vf[:, sl])
            pens.append(pen + bias_sc[g * 2 + h])
        starts.append((st, stp))
    return _stack(qs).astype(BF16), _stack(ks).astype(BF16), _stack(vs).astype(BF16), _stack(pens), starts


def _heads(x, u):
    return jnp.concatenate([x[2 * u], x[2 * u + 1]], axis=1)


def _attn_fwd(qkv3, rel_bias, bidx):
    bsz, s, _ = qkv3.shape
    rt = 256

    def body(tab_ref, bidx_ref, *refs):
        q_refs, k_refs, v_refs = refs[0:3], refs[3:6], refs[6:9]
        o_ref, lse_ref = refs[9:11]
        bias_sc, num_sc, den_sc, m_sc = refs[11:]
        pl.when(pl.program_id(1) == 0)(lambda: _fill_bias(tab_ref, bidx_ref, bias_sc, pl.program_id(0)))
        for g, d in enumerate(DILATIONS):
            nb = s // (QB * d)

            def blk(it, c, g=g, d=d, nb=nb):
                q, k, v, bias, starts = _attn_operands(q_refs[g], k_refs[g], v_refs[g], bias_sc, g, d, nb, it * ATTN_BLOCKS)
                sc = _bdot3(q, k, ((2,), (2,))) * SCALE + bias
                m = jnp.max(sc, axis=-1, keepdims=True)
                p = jnp.exp(sc - m)
                den = jnp.sum(p, axis=-1, keepdims=True)
                num = _bdot3(p.astype(BF16), v, ((2,), (1,)))
                den, m = jnp.broadcast_to(den, num.shape), jnp.broadcast_to(m, num.shape)
                for u, (st, _) in enumerate(starts):
                    num_sc[g, _ds(st, d), :] = _heads(num, u)
                    den_sc[g, _ds(st, d), :] = _heads(den, u)
                    m_sc[g, _ds(st, d), :] = _heads(m, u)
                return c

            lax.fori_loop(0, s // QB // ATTN_BLOCKS, blk, 0)

        def merge(i, c):
            rows = pl.ds(pl.multiple_of(i * rt, rt), rt)
            m0, m1, m2 = m_sc[0, rows, :], m_sc[1, rows, :], m_sc[2, rows, :]
            mall = jnp.maximum(jnp.maximum(m0, m1), m2)
            w0, w1, w2 = jnp.exp(m0 - mall), jnp.exp(m1 - mall), jnp.exp(m2 - mall)
            num = w0 * num_sc[0, rows, :] + w1 * num_sc[1, rows, :] + w2 * num_sc[2, rows, :]
            den = w0 * den_sc[0, rows, :] + w1 * den_sc[1, rows, :] + w2 * den_sc[2, rows, :]
            o_ref[0, rows, :] = num / den
            lse_ref[0, rows, :] = mall + jnp.log(den)
            return c

        lax.fori_loop(0, s // rt, merge, 0)

    col = lambda w, g: (lambda hp, b: (b, 0, (w * 3 + g) * 4 + hp))
    in_specs = [pl.BlockSpec(memory_space=pltpu.SMEM), pl.BlockSpec((3, QB, 2 * QB), lambda hp, b: (0, 0, 0))]
    in_specs += [pl.BlockSpec((1, s, LANE), col(w, g)) for w in range(3) for g in range(3)]
    out_spec = pl.BlockSpec((1, s, LANE), lambda hp, b: (b, 0, hp))
    return pl.pallas_call(
        body, name="attn_fwd", grid=(4, bsz), in_specs=in_specs, out_specs=[out_spec, out_spec],
        out_shape=[SDS((bsz, s, WIDTH), F32), SDS((bsz, s, WIDTH), F32)],
        scratch_shapes=[pltpu.VMEM((6, QB, 2 * QB), F32), pltpu.VMEM((3, s, LANE), F32), pltpu.VMEM((3, s, LANE), F32),
                        pltpu.VMEM((3, s, LANE), F32)],
        compiler_params=_params(("arbitrary", "arbitrary")))(rel_bias, bidx, *([qkv3] * 9))


def _attn_bwd(qkv3, o3, lse3, do3, rel_bias, bidx):
    bsz, s, _ = qkv3.shape
    rt = 256

    def body(tab_ref, bidx_ref, *refs):
        q_refs, k_refs, v_refs = refs[0:3], refs[3:6], refs[6:9]
        o_ref, lse_ref, do_ref, dqkv_ref, db_ref, bias_sc, delta_sc = refs[9:]
        dq_refs, dk_refs, dv_refs = ([dqkv_ref.at[w * 3 + g] for g in range(3)] for w in range(3))

        @pl.when(pl.program_id(1) == 0)
        def _():
            _fill_bias(tab_ref, bidx_ref, bias_sc, pl.program_id(0))
            db_ref[...] = jnp.zeros_like(db_ref)

        def prep(i, c):
            rows = pl.ds(pl.multiple_of(i * rt, rt), rt)
            prod = do_ref[0, rows, :] * o_ref[0, rows, :]
            d0 = jnp.sum(prod[:, :HEAD], axis=-1, keepdims=True)
            d1 = jnp.sum(prod[:, HEAD:], axis=-1, keepdims=True)
            delta_sc[rows, :] = jnp.concatenate([jnp.broadcast_to(d0, (rt, HEAD)), jnp.broadcast_to(d1, (rt, HEAD))], axis=1)
            z = jnp.zeros((rt, LANE), F32)
            for g in range(3):
                dk_refs[g][0, rows, :] = z
                dv_refs[g][0, rows, :] = z
            return c

        lax.fori_loop(0, s // rt, prep, 0)
        for g, d in enumerate(DILATIONS):
            nb = s // (QB * d)

            def blk(it, c, g=g, d=d, nb=nb):
                q, k, v, bias, starts = _attn_operands(q_refs[g], k_refs[g], v_refs[g], bias_sc, g, d, nb, it * ATTN_BLOCKS)
                dos, lses, deltas = [], [], []
                for st, _ in starts:
                    dof, lsef, delf = do_ref[0, _ds(st, d), :], lse_ref[0, _ds(st, d), :], delta_sc[_ds(st, d), :]
                    for h in range(2):
                        dos.append(dof[:, HEAD * h:HEAD * h + HEAD])
                        lses.append(lsef[:, HEAD * h:HEAD * h + 1])
                        deltas.append(delf[:, HEAD * h:HEAD * h + 1])
                do, lse, delta = _stack(dos).astype(BF16), _stack(lses), _stack(deltas)
                p = jnp.exp(_bdot3(q, k, ((2,), (2,))) * SCALE + bias - lse)
                dv = _bdot3(p.astype(BF16), do, ((1,), (1,)))
                ds = p * (_bdot3(do, v, ((2,), (2,))) - delta)
                dsb = ds.astype(BF16)
                dq = _bdot3(dsb, k, ((2,), (1,))) * SCALE
                dk = _bdot3(dsb, q, ((1,), (1,))) * SCALE
                for h in range(2):
                    db_ref[0, g * 2 + h] += sum(ds[2 * u + h] for u in range(ATTN_BLOCKS))
                for u, (st, stp) in enumerate(starts):
                    dq_refs[g][0, _ds(st, d), :] = _heads(dq, u)
                    dk_refs[g][0, _ds(stp, d), :] += _heads(dk[:, :QB], u)
                    dv_refs[g][0, _ds(stp, d), :] += _heads(dv[:, :QB], u)
                    dk_refs[g][0, _ds(st, d), :] += _heads(dk[:, QB:], u)
                    dv_refs[g][0, _ds(st, d), :] += _heads(dv[:, QB:], u)
                return c

            lax.fori_loop(0, s // QB // ATTN_BLOCKS, blk, 0)

    col = lambda w, g: (lambda hp, b: (b, 0, (w * 3 + g) * 4 + hp))
    blk_spec = pl.BlockSpec((1, s, LANE), lambda hp, b: (b, 0, hp))
    in_specs = [pl.BlockSpec(memory_space=pltpu.SMEM), pl.BlockSpec((3, QB, 2 * QB), lambda hp, b: (0, 0, 0))]
    in_specs += [pl.BlockSpec((1, s, LANE), col(w, g)) for w in range(3) for g in range(3)]
    in_specs += [blk_spec] * 3
    out_specs = [pl.BlockSpec((9, 1, s, LANE), lambda hp, b: (0, b, 0, hp)), pl.BlockSpec((1, 6, QB, 2 * QB), lambda hp, b: (hp, 0, 0, 0))]
    out_shape = [SDS((9, bsz, s, WIDTH), F32), SDS((4, 6, QB, 2 * QB), F32)]
    return pl.pallas_call(
        body, name="attn_bwd", grid=(4, bsz), in_specs=in_specs, out_specs=out_specs, out_shape=out_shape,
        scratch_shapes=[pltpu.VMEM((6, QB, 2 * QB), F32), pltpu.VMEM((s, LANE), F32)],
        compiler_params=_params(("parallel", "arbitrary")))(rel_bias, bidx, *([qkv3] * 9), o3, lse3, do3)


def _bias_grad(dbias, bidx):
    def body(db_ref, bidx_ref, o_ref):
        lane = lax.broadcasted_iota(jnp.int32, (1, LANE), 1)
        for g in range(3):
            bi = bidx_ref[g]
            for hp in range(4):
                for h in range(2):
                    mat = db_ref[hp, g * 2 + h]
                    row = jnp.zeros((1, LANE), F32)
                    for j in range(N_BUCKET):
                        part = jnp.sum(jnp.where(bi == j, mat, 0.0), axis=0, keepdims=True)
                        row = jnp.where(lane == j, jnp.sum(part, axis=1, keepdims=True), row)
                    hd = g * N_HEAD + hp * 2 + h
                    o_ref[hd:hd + 1, :] = row

    return pl.pallas_call(body, name="bias_grad", out_shape=SDS((3 * N_HEAD, LANE), F32), compiler_params=_params())(dbias, bidx)


def _pre_fn(r, k0, v, wl, al, w0, wup, a0, aup, kk_, ka_):
    u = w0 + _bdot(jnp.tanh(wl), wup)
    lw = -jnp.exp(-_softplus(-u) - 0.5)
    a = jax.nn.sigmoid(a0 + _bdot(al, aup))
    kkraw = k0 * kk_
    k = k0 * (1.0 + (a - 1.0) * ka_)
    return r, lw, k, v, kkraw, a


PRE_SPLIT = (0, WIDTH, 2 * WIDTH, 3 * WIDTH, 3 * WIDTH + LORA, 3 * WIDTH + 2 * LORA)


def _pre_pieces(prs):
    return [prs[:, a:b] for a, b in zip(PRE_SPLIT[:-1], PRE_SPLIT[1:])]


PRE_TT = 512


def _shifted(pr_ref, edge_ref, first, back):
    pr = pr_ref[0]
    tt = pr.shape[0]
    row = lax.broadcasted_iota(jnp.int32, (tt, 1), 0)
    if back:
        edge = jnp.where(first, 0.0, edge_ref[0, 7:8, :])
        return jnp.where(row == 0, edge, pltpu.roll(pr, 1, axis=0))
    edge = jnp.where(first, 0.0, edge_ref[0, 0:1, :])
    return jnp.where(row == tt - 1, edge, pltpu.roll(pr, tt - 1, axis=0))


def _rwkv_pre(pr3, mix, w0, wup, a0, aup, kk_, ka_):
    bsz, s, _ = pr3.shape
    tt = PRE_TT

    def body(pr_ref, edge_ref, mix_ref, w0_ref, wup_ref, a0_ref, aup_ref, kk_ref, ka_ref, *outs):
        pr = pr_ref[0]
        prev = _shifted(pr_ref, edge_ref, pl.program_id(1) == 0, True)
        prs = pr + (prev - pr) * mix_ref[...]
        vals = _pre_fn(*_pre_pieces(prs), w0_ref[...], wup_ref[...].astype(F32), a0_ref[...], aup_ref[...].astype(F32), kk_ref[...],
                       ka_ref[...])
        for o, val in zip(outs, vals):
            o[0] = val

    vec = lambda n: pl.BlockSpec((1, n), lambda b, i: (0, 0))
    mat = pl.BlockSpec((LORA, WIDTH), lambda b, i: (0, 0))
    in_specs = [pl.BlockSpec((1, tt, PR_COLS), lambda b, i: (b, i, 0)),
                pl.BlockSpec((1, 8, PR_COLS), lambda b, i: (b, jnp.maximum(i * (tt // 8) - 1, 0), 0)),
                vec(PR_COLS), vec(WIDTH), mat, vec(WIDTH), mat, vec(WIDTH), vec(WIDTH)]
    out_spec = pl.BlockSpec((1, tt, WIDTH), lambda b, i: (b, i, 0))
    return pl.pallas_call(
        body, name="rwkv_pre", grid=(bsz, s // tt), in_specs=in_specs, out_specs=[out_spec] * 6,
        out_shape=[SDS((bsz, s, WIDTH), F32)] * 6, compiler_params=_params(("parallel", "parallel")))(
            pr3, pr3, mix, w0, wup, a0, aup, kk_, ka_)


def _rwkv_pre_bwd(pr3, cots, mix, w0, wup, a0, aup, kk_, ka_):
    bsz, s, _ = pr3.shape
    tt = PRE_TT

    def body(pr_ref, edge_ref, c0, c1, c2, c3, c4, c5, mix_ref, w0_ref, wup_ref, a0_ref, aup_ref, kk_ref, ka_ref,
             dprs_ref, dmix_ref, dw0_ref, dwup_ref, da0_ref, daup_ref, dkk_ref, dka_ref):
        pr = pr_ref[0]
        prev = _shifted(pr_ref, edge_ref, pl.program_id(1) == 0, True)
        prs = pr + (prev - pr) * mix_ref[...]
        _, vjp = jax.vjp(_pre_fn, *_pre_pieces(prs), w0_ref[...], wup_ref[...].astype(F32), a0_ref[...], aup_ref[...].astype(F32),
                         kk_ref[...], ka_ref[...])
        grads = vjp(tuple(c[0] for c in (c0, c1, c2, c3, c4, c5)))
        for piece, a, b in zip(grads[:5], PRE_SPLIT[:-1], PRE_SPLIT[1:]):
            dprs_ref[0, :, a:b] = piece
        dw0, dwup, da0, daup, dkk, dka = grads[5:]
        dprs = dprs_ref[0]
        grads = (jnp.sum(dprs * (prev - pr), axis=0, keepdims=True), dw0, dwup, da0, daup, dkk, dka)
        refs = (dmix_ref, dw0_ref, dwup_ref, da0_ref, daup_ref, dkk_ref, dka_ref)
        first = jnp.logical_and(pl.program_id(0) == 0, pl.program_id(1) == 0)

        @pl.when(first)
        def _():
            for r_, g_ in zip(refs, grads):
                r_[...] = g_

        @pl.when(jnp.logical_not(first))
        def _():
            for r_, g_ in zip(refs, grads):
                r_[...] += g_

    vec = lambda n: pl.BlockSpec((1, n), lambda b, i: (0, 0))
    mat = pl.BlockSpec((LORA, WIDTH), lambda b, i: (0, 0))
    tile = pl.BlockSpec((1, tt, WIDTH), lambda b, i: (b, i, 0))
    in_specs = [pl.BlockSpec((1, tt, PR_COLS), lambda b, i: (b, i, 0)),
                pl.BlockSpec((1, 8, PR_COLS), lambda b, i: (b, jnp.maximum(i * (tt // 8) - 1, 0), 0))]
    in_specs += [tile] * 6 + [vec(PR_COLS), vec(WIDTH), mat, vec(WIDTH), mat, vec(WIDTH), vec(WIDTH)]
    out_specs = [pl.BlockSpec((1, tt, PR_COLS), lambda b, i: (b, i, 0)), vec(PR_COLS), vec(WIDTH), mat, vec(WIDTH), mat,
                 vec(WIDTH), vec(WIDTH)]
    out_shape = [SDS((bsz, s, PR_COLS), F32), SDS((1, PR_COLS), F32), SDS((1, WIDTH), F32), SDS((LORA, WIDTH), F32),
                 SDS((1, WIDTH), F32), SDS((LORA, WIDTH), F32), SDS((1, WIDTH), F32), SDS((1, WIDTH), F32)]
    return pl.pallas_call(
        body, name="rwkv_pre_bwd", grid=(bsz, s // tt), in_specs=in_specs, out_specs=out_specs, out_shape=out_shape,
        compiler_params=_params(("arbitrary", "arbitrary")))(pr3, pr3, *cots, mix, w0, wup, a0, aup, kk_, ka_)


def _shift_bwd(dprs3, mix):
    bsz, s, _ = dprs3.shape
    tt = PRE_TT
    nt = s // tt

    def body(d_ref, edge_ref, mix_ref, o_ref):
        nxt = _shifted(d_ref, edge_ref, pl.program_id(1) == nt - 1, False)
        m = mix_ref[...]
        o_ref[0] = d_ref[0] * (1.0 - m) + nxt * m

    in_specs = [pl.BlockSpec((1, tt, PR_COLS), lambda b, i: (b, i, 0)),
                pl.BlockSpec((1, 8, PR_COLS), lambda b, i: (b, jnp.minimum((i + 1) * (tt // 8), s // 8 - 1), 0)),
                pl.BlockSpec((1, PR_COLS), lambda b, i: (0, 0))]
    return pl.pallas_call(
        body, name="shift_bwd", grid=(bsz, nt), in_specs=in_specs, out_specs=pl.BlockSpec((1, tt, PR_COLS), lambda b, i: (b, i, 0)),
        out_shape=SDS((bsz, s, PR_COLS), F32), compiler_params=_params(("parallel", "parallel")))(dprs3, dprs3, mix)


_NN, _NT, _TN = ((2,), (1,)), ((2,), (2,)), ((1,), (1,))


def _dot3(a, b, dims, precision=HI3):
    return lax.dot_general(a, b, (dims, ((0,), (0,))), precision=precision, preferred_element_type=F32)


def _dot3_bf16(a, b, dims):
    return lax.dot_general(a.astype(BF16), b.astype(BF16), (dims, ((0,), (0,))), preferred_element_type=F32)


class _Dots:
    def __init__(self, fwd):
        def make(dims, da_rule, db_rule):
            @jax.custom_vjp
            def f(a, b):
                return fwd(a, b, dims)

            f.defvjp(lambda a, b: (f(a, b), (a, b)), lambda res, g: (da_rule(*res, g), db_rule(*res, g)))
            return f

        one = _dot3_bf16
        self.mm = make(_NN, lambda a, b, g: one(g, b, _NT), lambda a, b, g: one(a, g, _TN))
        self.mm_nt = make(_NT, lambda a, b, g: one(g, b, _NN), lambda a, b, g: one(g, a, _TN))
        self.mm_tn = make(_TN, lambda a, b, g: one(b, g, _NT), lambda a, b, g: one(a, g, _NN))

        def powers(aab):
            ps = [aab]
            while 2 ** len(ps) < aab.shape[1]:
                ps.append(fwd(ps[-1], ps[-1], _NN))
            return ps

        def apply(ps, z, dims):
            for p in ps:
                z = z + fwd(p, z, dims)
            return z

        @jax.custom_vjp
        def solve(aab, z):
            return apply(powers(aab), z, _NN)

        def solve_fwd(aab, z):
            ps = powers(aab)
            x = apply(ps, z, _NN)
            return x, (ps, x)

        def solve_bwd(res, g):
            ps, x = res
            dz = apply(ps, g, _TN)
            return fwd(dz, x, _NT), dz

        solve.defvjp(solve_fwd, solve_bwd)
        self.solve = solve


_ACCURATE = _Dots(_dot3)
_ONE_PASS = _Dots(_dot3_bf16)
_bmm, _bmm_tn = _ACCURATE.mm, _ACCURATE.mm_tn


def _chunk_fn(s0t, r, lw, k, v, kkraw, a, rk, lnw, lnb, first=False, d=_ACCURATE):
    c = r.shape[1]
    at, rt, btc, ktc, gc, aab, arb, xv, arkv, ain, bin_ = _chunk_core(r, lw, k, v, kkraw, a, d)
    rs = d.mm(jnp.concatenate([at, rt], axis=1), s0t)
    u = d.solve(aab, rs[:, :c] + xv)
    y = rs[:, c:] + d.mm(arb, u) + arkv
    if first:
        y = _with_early_rows(y, r, lw, k, v, ain, bin_)
    gcol = jnp.sum(_diag(gc), axis=2, keepdims=True)
    sct = gcol * s0t + d.mm_tn(jnp.concatenate([btc, ktc], axis=1), jnp.concatenate([u, v], axis=1))
    return _post(y, r, k, v, rk, lnw, lnb), sct


def _diag(gc):
    return jnp.where(_masks(HEAD)[2], gc, 0.0)


def _with_early_rows(y, r, lw, k, v, ain, bin_):
    early = _stack([_early_rows(r[h], lw[h], k[h], v[h], ain[h], bin_[h]) for h in range(2)])
    return jnp.concatenate([jnp.concatenate([early, y[:2, EARLY:]], axis=1), y[2:]], axis=0)


def _early_rows(r, lw, k, v, ain, bin_):
    wc, bc, kc = jnp.transpose(jnp.exp(lw)), jnp.transpose(bin_), jnp.transpose(k)
    st = jnp.zeros((HEAD, HEAD), F32)
    rows = []
    for t in range(EARLY):
        sa = _bdot(ain[t:t + 1], st)
        st = st * wc[:, t:t + 1] + bc[:, t:t + 1] * sa + kc[:, t:t + 1] * v[t:t + 1]
        rows.append(_bdot(r[t:t + 1], st))
    return jnp.concatenate(rows, axis=0)


def _chunk_rows(c):
    return pl.ds(c * CHUNK, CHUNK) if isinstance(c, int) else pl.ds(pl.multiple_of(c * CHUNK, CHUNK), CHUNK)


def _stack(xs):
    return jnp.concatenate([x[None] for x in xs], axis=0)


def _pairs(ref, chunks):
    tiles = [ref[0, _chunk_rows(c), :] for c in chunks]
    return _stack([t[:, HEAD * h:HEAD * h + HEAD] for t in tiles for h in range(2)])


def _unpair(vals, j):
    return jnp.concatenate([vals[2 * j], vals[2 * j + 1]], axis=1)


def _masks(c):
    ii = lax.broadcasted_iota(jnp.int32, (c, c), 0)
    jj = lax.broadcasted_iota(jnp.int32, (c, c), 1)
    return ii > jj, ii >= jj, ii == jj


def _chunk_core(r, lw, k, v, kkraw, a, d=_ACCURATE):
    g_, c = r.shape[0], r.shape[1]
    nrm = jnp.sqrt(jnp.sum(kkraw * kkraw, axis=-1, keepdims=True))
    kkn = kkraw / jnp.maximum(nrm, 1e-12)
    ain, bin_ = -kkn, kkn * a
    strict, incl, _ = _masks(c)
    lg = lax.dot_general(jnp.broadcast_to(incl.astype(F32), (g_, c, c)), lw, (((2,), (1,)), ((0,), (0,))), precision=HI,
                         preferred_element_type=F32)
    g, gp, gi = jnp.exp(lg), jnp.exp(lg - lw), jnp.exp(-lg)
    at, rt, bt, kt = ain * gp, r * g, bin_ * gi, k * gi
    aa = d.mm_nt(jnp.concatenate([at, rt], axis=1), jnp.concatenate([bt, kt], axis=1))
    aab = jnp.where(strict, aa[:, :c, :c], 0.0)
    aak = jnp.where(strict, aa[:, :c, c:], 0.0)
    arb = jnp.where(incl, aa[:, c:, :c], 0.0)
    ark = jnp.where(incl, aa[:, c:, c:], 0.0)
    akv = d.mm(jnp.concatenate([aak, ark], axis=1), v)
    gc = g[:, c - 1:c, :]
    return at, rt, bt * gc, kt * gc, gc, aab, arb, akv[:, :c], akv[:, c:], ain, bin_


def _post(y, r, k, v, rk, lnw, lnb):
    mu = jnp.mean(y, axis=-1, keepdims=True)
    var = jnp.mean(jnp.square(y - mu), axis=-1, keepdims=True)
    yn = (y - mu) * lax.rsqrt(var + GN_EPS) * lnw + lnb
    return yn + jnp.sum(r * k * rk, axis=-1, keepdims=True) * v


def _chunk_consts(r, lw, k, v, kkraw, a, first=False):
    at, rt, btc, ktc, gc, aab, arb, xv, arkv, ain, bin_ = _chunk_core(r, lw, k, v, kkraw, a)
    z = _ACCURATE.solve(aab, jnp.concatenate([at, xv], axis=2))
    ryv = jnp.concatenate([rt, arkv], axis=2) + _bmm(arb, z)
    if first:
        ryv = jnp.concatenate([ryv[:, :, :HEAD], _with_early_rows(ryv[:, :, HEAD:], r, lw, k, v, ain, bin_)], axis=2)
    mkv = _bmm_tn(btc, z) + jnp.concatenate([_diag(gc), _bmm_tn(ktc, v)], axis=2)
    return mkv, ryv


def _rwkv_scan(ins, rk, lnw, lnb):
    bsz, s, _ = ins[0].shape
    nch = s // CHUNK

    def consts_body(r_ref, lw_ref, k_ref, v_ref, kk_ref, a_ref, mkv_ref, ry_ref, yv_ref):
        def group(i, carry):
            chunks = [i * CHUNK_GROUP + j for j in range(CHUNK_GROUP)]
            mkv, ryv = _chunk_consts(*[_pairs(ref, chunks) for ref in (r_ref, lw_ref, k_ref, v_ref, kk_ref, a_ref)],
                                     first=isinstance(i, int) and i == 0)
            for j, c in enumerate(chunks):
                for h in range(2):
                    mkv_ref[0, 0, c, h] = mkv[2 * j + h]
                ry_ref[0, _chunk_rows(c), :] = jnp.concatenate([ryv[2 * j][:, :HEAD], ryv[2 * j + 1][:, :HEAD]], axis=1)
                yv_ref[0, _chunk_rows(c), :] = jnp.concatenate([ryv[2 * j][:, HEAD:], ryv[2 * j + 1][:, HEAD:]], axis=1)
            return carry

        group(0, 0)
        lax.fori_loop(1, nch // CHUNK_GROUP, group, 0)

    tile = pl.BlockSpec((1, s, LANE), lambda b, hp: (b, 0, hp))
    vec = pl.BlockSpec((1, LANE), lambda b, hp: (0, hp))
    mkv_spec = pl.BlockSpec((1, 1, nch, 2, HEAD, LANE), lambda b, hp: (b, hp, 0, 0, 0, 0))
    st_spec = pl.BlockSpec((1, 1, nch, 2, HEAD, HEAD), lambda b, hp: (b, hp, 0, 0, 0, 0))
    mkv, ry, yv = pl.pallas_call(
        consts_body, name="rwkv_consts", grid=(bsz, 4), in_specs=[tile] * 6, out_specs=[mkv_spec, tile, tile],
        out_shape=[SDS((bsz, 4, nch, 2, HEAD, LANE), F32), SDS((bsz, s, WIDTH), F32), SDS((bsz, s, WIDTH), F32)],
        compiler_params=_params(("parallel", "parallel")))(*ins)

    states = _chunk_recurrence(mkv, None, "rwkv_states")

    def out_body(ry_ref, yv_ref, r_ref, k_ref, v_ref, st_ref, rk_ref, lnw_ref, lnb_ref, o_ref):
        y, r, k, v, rk_, lnw_, lnb_ = _scan_rows(ry_ref, yv_ref, r_ref, k_ref, v_ref, st_ref, rk_ref, lnw_ref, lnb_ref)
        o = _post(y, r, k, v, rk_, lnw_, lnb_)
        for j in range(CHUNK_GROUP):
            o_ref[0, _chunk_rows(j), :] = _unpair(o, j)

    o = pl.pallas_call(
        out_body, name="rwkv_out", grid=(bsz, 4, nch // CHUNK_GROUP), in_specs=_group_specs(5), out_specs=_group_specs(1)[0],
        out_shape=SDS((bsz, s, WIDTH), F32),
        compiler_params=_params(("parallel", "parallel", "parallel")))(ry, yv, ins[0], ins[2], ins[3], states, rk, lnw, lnb)
    return o, states, (mkv, ry, yv)


def _group_specs(n_tiles):
    tile = pl.BlockSpec((1, CHUNK_GROUP * CHUNK, LANE), lambda b, hp, t: (b, t, hp))
    if n_tiles == 1:
        return [tile]
    st = pl.BlockSpec((1, 1, CHUNK_GROUP, 2, HEAD, HEAD), lambda b, hp, t: (b, hp, t, 0, 0, 0))
    vec = pl.BlockSpec((1, LANE), lambda b, hp, t: (0, hp))
    return [tile] * n_tiles + [st] + [vec] * 3


def _scan_rows(ry_ref, yv_ref, r_ref, k_ref, v_ref, st_ref, rk_ref, lnw_ref, lnb_ref):
    chunks = list(range(CHUNK_GROUP))
    ry, yv, r, k, v = (_pairs(ref, chunks) for ref in (ry_ref, yv_ref, r_ref, k_ref, v_ref))
    st = _stack([st_ref[0, 0, c, h] for c in chunks for h in range(2)])
    vecs = [_stack([ref[:, HEAD * h:HEAD * h + HEAD] for _ in chunks for h in range(2)]) for ref in (rk_ref, lnw_ref, lnb_ref)]
    return (_bmm(ry, st) + yv, r, k, v, *vecs)


def _chunk_recurrence(mkv, q, name):
    bsz, _, nch = mkv.shape[:3]
    pairs = [(hp, h) for hp in range(4) for h in range(2)]

    def body(*refs):
        mkv_ref, out_ref, acc = refs[0], refs[-2], refs[-1]
        acc[...] = jnp.zeros_like(acc)

        def step(i, carry):
            c = i if q is None else nch - 1 - i
            cur = acc[...]
            for j, (hp, h) in enumerate(pairs):
                out_ref[0, hp, c, h] = cur[j]
            m = _stack([mkv_ref[0, hp, c, h] for hp, h in pairs])
            if q is None:
                acc[...] = _bmm(m[:, :, :HEAD], cur) + m[:, :, HEAD:]
            else:
                acc[...] = _bmm_tn(m[:, :, :HEAD], cur) + _stack([refs[1][0, hp, c, h] for hp, h in pairs])
            return carry

        lax.fori_loop(0, nch, step, 0)

    spec = lambda w: pl.BlockSpec((1, 4, nch, 2, HEAD, w), lambda b: (b, 0, 0, 0, 0, 0))
    return pl.pallas_call(
        body, name=name, grid=(bsz,), in_specs=[spec(LANE)] + ([] if q is None else [spec(HEAD)]), out_specs=spec(HEAD),
        out_shape=SDS((bsz, 4, nch, 2, HEAD, HEAD), F32), scratch_shapes=[pltpu.VMEM((8, HEAD, HEAD), F32)],
        compiler_params=_params(("parallel",)))(*([mkv] if q is None else [mkv, q]))


def _rwkv_scan_bwd(ins, states, consts, do3, rk, lnw, lnb):
    bsz, s, _ = ins[0].shape
    nch = s // CHUNK

    mkv, ry, yv = consts

    def q_body(do_ref, ry_ref, yv_ref, r_ref, k_ref, v_ref, st_ref, rk_ref, lnw_ref, lnb_ref, q_ref):
        y, r, k, v, rk_, lnw_, lnb_ = _scan_rows(ry_ref, yv_ref, r_ref, k_ref, v_ref, st_ref, rk_ref, lnw_ref, lnb_ref)
        _, vjp = jax.vjp(lambda y_: _post(y_, r, k, v, rk_, lnw_, lnb_), y)
        (dy,) = vjp(_pairs(do_ref, list(range(CHUNK_GROUP))))
        q = _bmm_tn(_pairs(ry_ref, list(range(CHUNK_GROUP))), dy)
        for j in range(CHUNK_GROUP):
            for h in range(2):
                q_ref[0, 0, j, h] = q[2 * j + h]

    specs = _group_specs(6)
    q = pl.pallas_call(
        q_body, name="rwkv_q", grid=(bsz, 4, nch // CHUNK_GROUP), in_specs=specs, out_specs=specs[6],
        out_shape=SDS((bsz, 4, nch, 2, HEAD, HEAD), F32),
        compiler_params=_params(("parallel", "parallel", "parallel")))(do3, ry, yv, ins[0], ins[2], ins[3], states, rk, lnw, lnb)

    dstates = _chunk_recurrence(mkv, q, "rwkv_dstates")

    def body(r_ref, lw_ref, k_ref, v_ref, kk_ref, a_ref, st_ref, dst_ref, do_ref, rk_ref, lnw_ref, lnb_ref,
             dr_ref, dlw_ref, dk_ref, dv_ref, dkk_ref, da_ref, drk_ref, dlnw_ref, dlnb_ref):
        chunks = list(range(BWD_GROUP))
        par_refs = (drk_ref, dlnw_ref, dlnb_ref)

        @pl.when(jnp.logical_and(pl.program_id(1) == 0, pl.program_id(2) == 0))
        def _():
            for ref in par_refs:
                ref[...] = jnp.zeros_like(ref)

        def group(first):
            per_pair = lambda ref: _stack([ref[0, 0, c, h] for c in chunks for h in range(2)])
            vecs = [_stack([ref[:, HEAD * h:HEAD * h + HEAD] for _ in chunks for h in range(2)]) for ref in (rk_ref, lnw_ref, lnb_ref)]
            _, vjp = jax.vjp(functools.partial(_chunk_fn, first=first, d=_ONE_PASS), per_pair(st_ref),
                             *[_pairs(ref, chunks) for ref in (r_ref, lw_ref, k_ref, v_ref, kk_ref, a_ref)], *vecs)
            grads = vjp((_pairs(do_ref, chunks), per_pair(dst_ref)))
            for ref, cot in zip((dr_ref, dlw_ref, dk_ref, dv_ref, dkk_ref, da_ref), grads[1:7]):
                for j, c in enumerate(chunks):
                    ref[0, _chunk_rows(c), :] = _unpair(cot, j)
            for ref, g_ in zip(par_refs, grads[7:10]):
                ref[...] += jnp.concatenate([sum(g_[2 * j + h] for j in range(BWD_GROUP)) for h in range(2)], axis=1)

        pl.when(pl.program_id(2) == 0)(functools.partial(group, True))
        pl.when(pl.program_id(2) != 0)(functools.partial(group, False))

    tt = BWD_GROUP * CHUNK
    tile = pl.BlockSpec((1, tt, LANE), lambda hp, b, t: (b, t, hp))
    vec = pl.BlockSpec((1, LANE), lambda hp, b, t: (0, hp))
    st_spec = pl.BlockSpec((1, 1, BWD_GROUP, 2, HEAD, HEAD), lambda hp, b, t: (b, hp, t, 0, 0, 0))
    outs = pl.pallas_call(
        body, name="rwkv_scan_bwd", grid=(4, bsz, s // tt), in_specs=[tile] * 6 + [st_spec, st_spec, tile] + [vec] * 3,
        out_specs=[tile] * 6 + [vec] * 3,
        out_shape=[SDS((bsz, s, WIDTH), F32)] * 6 + [SDS((1, WIDTH), F32)] * 3,
        compiler_params=_params(("parallel", "arbitrary", "arbitrary")))(*ins, states, dstates, do3, rk, lnw, lnb)
    return outs[:6], outs[6:]


def _head(o_attn, o_rwkv, z_attn, z_rwkv, gm, x2, tgt, wua, wur, wout, g2):
    n = x2.shape[0]
    tm = 256
    nt = n // tm
    d = D_MODEL

    def body(oa_ref, or_ref, za_ref, zr_ref, gm_ref, x_ref, t_ref, wua_ref, wur_ref, wout_ref, g2_ref,
             dxo_ref, doa_ref, dor_ref, dza_ref, dzr_ref, dgm_ref, dwua_ref, dwur_ref, dwout_ref, dg2_ref, loss_ref, lacc):
        i = pl.program_id(0)
        oa, orw, za, zr = oa_ref[...], or_ref[...], za_ref[...], zr_ref[...]
        ga, gb = gm_ref[:, 0:d], gm_ref[:, d:2 * d]
        am = (oa * _silu(za)).astype(BF16)
        bm = (orw * _silu(zr)).astype(BF16)
        ya, yb = _dot(am, wua_ref[...]), _dot(bm, wur_ref[...])
        sa, sb = jax.nn.sigmoid(ga), jax.nn.sigmoid(gb)
        merged = (sa * ya + sb * yb).astype(BF16)
        out = _dot(merged, wout_ref[...])
        rs = lax.rsqrt(jnp.mean(out * out, axis=-1, keepdims=True) + RMS_EPS)
        g2 = g2_ref[...]
        err = x_ref[...] + out * rs * g2 - t_ref[...]
        lpart = jnp.sum(err * err, axis=0, keepdims=True)
        dxo = err * (1.0 / d)
        dxo_ref[...] = dxo
        dg2 = jnp.sum(dxo * out * rs, axis=0, keepdims=True)
        gd = dxo * g2
        dout = (rs * (gd - out * (rs * rs) * jnp.mean(gd * out, axis=-1, keepdims=True))).astype(BF16)
        dmerged = _dot_nt(dout, wout_ref[...])
        dwout = _dot_tn(merged, dout)
        dya, dyb = (dmerged * sa).astype(BF16), (dmerged * sb).astype(BF16)
        dgm_ref[:, 0:d] = dmerged * ya * sa * (1.0 - sa)
        dgm_ref[:, d:2 * d] = dmerged * yb * sb * (1.0 - sb)
        dam, dbm = _dot_nt(dya, wua_ref[...]), _dot_nt(dyb, wur_ref[...])
        dwua, dwur = _dot_tn(am, dya), _dot_tn(bm, dyb)
        doa_ref[...] = dam * _silu(za)
        dza_ref[...] = dam * oa * _dsilu(za)
        dor_ref[...] = dbm * _silu(zr)
        dzr_ref[...] = dbm * orw * _dsilu(zr)

        @pl.when(i == 0)
        def _():
            dwua_ref[...], dwur_ref[...], dwout_ref[...], dg2_ref[...], lacc[...] = dwua, dwur, dwout, dg2, lpart

        @pl.when(i != 0)
        def _():
            dwua_ref[...] += dwua
            dwur_ref[...] += dwur
            dwout_ref[...] += dwout
            dg2_ref[...] += dg2
            lacc[...] += lpart

        @pl.when(i == nt - 1)
        def _():
            loss_ref[...] = jnp.sum(lacc[...], axis=1, keepdims=True) * (0.5 / d)

    t512 = pl.BlockSpec((tm, WIDTH), lambda i: (i, 0))
    t1k = pl.BlockSpec((tm, d), lambda i: (i, 0))
    t2k = pl.BlockSpec((tm, 2 * d), lambda i: (i, 0))
    full = lambda r, c: pl.BlockSpec((r, c), lambda i: (0, 0))
    return pl.pallas_call(
        body, name="head_fwd_bwd", grid=(nt,),
        in_specs=[t512, t512, t512, t512, t2k, t1k, t1k, full(WIDTH, d), full(WIDTH, d), full(d, d), full(1, d)],
        out_specs=[t1k, t512, t512, t512, t512, t2k, full(WIDTH, d), full(WIDTH, d), full(d, d), full(1, d), full(1, 1)],
        out_shape=[SDS((n, d), F32)] + [SDS((n, WIDTH), F32)] * 4 + [SDS((n, 2 * d), F32), SDS((WIDTH, d), F32), SDS((WIDTH, d), F32),
                                                                    SDS((d, d), F32), SDS((1, d), F32), SDS((1, 1), F32)],
        scratch_shapes=[pltpu.VMEM((1, d), F32)],
        compiler_params=_params(("arbitrary",)))(o_attn, o_rwkv, z_attn, z_rwkv, gm, x2, tgt, wua, wur, wout, g2)


def _prenorm_bwd(dh, x2, rs, g1, dxo):
    n, d = x2.shape
    tm = 1024

    def body(dh_ref, x_ref, rs_ref, g_ref, dxo_ref, gx_ref, dg_ref):
        x, r = x_ref[...], rs_ref[...]
        gd = dh_ref[...] * g_ref[...]
        gx_ref[...] = dxo_ref[...] + r * (gd - x * (r * r) * jnp.mean(gd * x, axis=-1, keepdims=True))
        dg = jnp.sum(dh_ref[...] * x * r, axis=0, keepdims=True)

        @pl.when(pl.program_id(0) == 0)
        def _():
            dg_ref[...] = dg

        @pl.when(pl.program_id(0) != 0)
        def _():
            dg_ref[...] += dg

    t = pl.BlockSpec((tm, d), lambda i: (i, 0))
    return pl.pallas_call(
        body, name="prenorm_bwd", grid=(n // tm,),
        in_specs=[t, t, pl.BlockSpec((tm, 1), lambda i: (i, 0)), pl.BlockSpec((1, d), lambda i: (0, 0)), t],
        out_specs=[t, pl.BlockSpec((1, d), lambda i: (0, 0))], out_shape=[SDS((n, d), F32), SDS((1, d), F32)],
        compiler_params=_params(("arbitrary",)))(dh, x2, rs, g1, dxo)


def _mesh_pos():
    x, y, c = lax.axis_index("x"), lax.axis_index("y"), lax.axis_index("c")
    return 4 * x + 2 * y + c


def _coords(idx):
    return (idx // 4, (idx // 2) % 2, idx % 2)


def _exchange(srcs, to_all, name):
    n = len(srcs)

    def body(*refs):
        src_refs, dst_refs = refs[:n], refs[n:2 * n]
        send_sems, recv_sems, local_sems = refs[2 * n:]
        me = _mesh_pos()

        def piece(i, j):
            return src_refs[i] if to_all[i] else src_refs[i].at[j]

        def remote(i, off, peer, block, slot):
            return pltpu.make_async_remote_copy(src_ref=piece(i, block), dst_ref=dst_refs[i].at[slot],
                                                send_sem=send_sems.at[i, off - 1], recv_sem=recv_sems.at[i, off - 1],
                                                device_id=_coords(peer), device_id_type=MESH)

        local = [pltpu.make_async_copy(piece(i, me), dst_refs[i].at[me], local_sems.at[i]) for i in range(n)]
        for cp in local:
            cp.start()
        sends = []
        for off in range(1, N_DEV):
            to = (me + off) % N_DEV
            for i in range(n):
                sends.append(remote(i, off, to, to, me))
                sends[-1].start()
        for off in range(1, N_DEV):
            frm = (me + N_DEV - off) % N_DEV
            for i in range(n):
                remote(i, off, frm, me, frm).wait_recv()
        for cp in sends:
            cp.wait_send()
        for cp in local:
            cp.wait()

    outs = pl.pallas_call(
        body, name=name, in_specs=[pl.BlockSpec(memory_space=pltpu.HBM)] * n, out_specs=[pl.BlockSpec(memory_space=pltpu.HBM)] * n,
        out_shape=[SDS((N_DEV,) + s.shape[-2:], s.dtype) for s in srcs],
        scratch_shapes=[pltpu.SemaphoreType.DMA((n, N_DEV - 1)), pltpu.SemaphoreType.DMA((n, N_DEV - 1)), pltpu.SemaphoreType.DMA((n,))],
        compiler_params=pltpu.CompilerParams())(*srcs)
    return outs


def _adamw(parts, w, m, v, tr, name):
    rows, cols = w.shape
    c1, c2 = 1.0 - ADAM_B1 ** ADAM_STEP, 1.0 - ADAM_B2 ** ADAM_STEP

    def body(p_ref, w_ref, m_ref, v_ref, g_ref, d_ref, nm_ref, nv_ref):
        g = p_ref[0].astype(F32)
        for j in range(1, N_DEV):
            g = g + p_ref[j].astype(F32)
        nm = ADAM_B1 * m_ref[...] + (1.0 - ADAM_B1) * g
        nv = ADAM_B2 * v_ref[...] + (1.0 - ADAM_B2) * jnp.square(g)
        g_ref[...] = g
        nm_ref[...] = nm
        nv_ref[...] = nv
        d_ref[...] = -ADAM_LR * ((nm / c1) / (jnp.sqrt(nv / c2) + ADAM_EPS) + ADAM_WD * w_ref[...])

    t = pl.BlockSpec((tr, cols), lambda i: (i, 0))
    return pl.pallas_call(
        body, name=name, grid=(rows // tr,), in_specs=[pl.BlockSpec((N_DEV, tr, cols), lambda i: (0, i, 0)), t, t, t],
        out_specs=[t] * 4, out_shape=[SDS((rows, cols), F32)] * 4, compiler_params=_params(("parallel",)))(parts, w, m, v)


SHARDED = (("w_in", D_MODEL, IN_COLS // N_DEV, True, 128), ("w_up_attn", WIDTH, D_MODEL // N_DEV, True, WIDTH),
           ("w_up_rwkv", WIDTH, D_MODEL // N_DEV, True, WIDTH), ("w_out", D_MODEL // N_DEV, D_MODEL, False, D_MODEL // N_DEV),
           ("rwkv_w_up", LORA, WIDTH // N_DEV, True, LORA), ("rwkv_a_up", LORA, WIDTH // N_DEV, True, LORA))
LOSS_SLOT = sum(n for _, n in SMALL)


def _pack_small(small, extra=None):
    flat = [small[n].reshape(-1).astype(F32) for n, _ in SMALL]
    flat.append(jnp.zeros((1,), F32) if extra is None else extra.reshape(1))
    flat.append(jnp.zeros((SMALL_ROWS * LANE - LOSS_SLOT - 1,), F32))
    return jnp.concatenate(flat).reshape(SMALL_ROWS, LANE)


def _unpack_small(packed, shapes):
    flat = packed.reshape(-1)
    out, off = {}, 0
    for n, cnt in SMALL:
        out[n] = flat[off:off + cnt].reshape(shapes[n])
        off += cnt
    return out, flat[LOSS_SLOT]


def _whole(gathered, by_cols):
    if not by_cols:
        return gathered.reshape(-1, gathered.shape[-1])
    return gathered.transpose(1, 0, 2).reshape(gathered.shape[1], -1)


def _per_owner(full, by_cols):
    if not by_cols:
        return full.reshape(N_DEV, -1, full.shape[-1])
    return full.reshape(full.shape[0], N_DEV, -1).transpose(1, 0, 2)


def _local_step(x, loss_target, sm, wts):
    bsz, s, d = x.shape
    n = bsz * s
    x2, tgt = x.reshape(n, d), loss_target.reshape(n, d)
    bidx = jnp.asarray(_bucket_tables())
    w_in = wts["w_in"]
    segs = (("qkv", 0, QKV_COLS, 512), ("za", OFF_ZA, WIDTH, 512), ("pr", OFF_PR, PR_COLS, PR_COLS), ("zr", OFF_ZR, WIDTH, 512),
            ("gm", OFF_GM, 2 * D_MODEL, 512))

    h, rs = _prenorm(x2, sm["pre_norm_gain"])
    proj = {nm: _mm(h, w_in[:, off:off + cnt], tn, "proj_" + nm) for nm, off, cnt, tn in segs}
    qkv3 = proj["qkv"].reshape(bsz, s, QKV_COLS)
    pr3 = proj["pr"].reshape(bsz, s, PR_COLS)

    o_attn, lse = _attn_fwd(qkv3, sm["rel_bias"], bidx)
    rk = sm["rwkv_r_k"].reshape(1, WIDTH)
    pre_args = (sm["rwkv_shift_mix"], sm["rwkv_w0"], wts["rwkv_w_up"], sm["rwkv_a0"], wts["rwkv_a_up"], sm["rwkv_k_k"], sm["rwkv_k_a"])
    scan_in = _rwkv_pre(pr3, *pre_args)
    o_rwkv, states, consts = _rwkv_scan(scan_in, rk, sm["rwkv_ln_w"], sm["rwkv_ln_b"])

    (dxo, do_attn, do_rwkv, dza, dzr, dgm, g_wua, g_wur, g_wout, g_post, loss) = _head(
        o_attn.reshape(n, WIDTH), o_rwkv.reshape(n, WIDTH), proj["za"], proj["zr"], proj["gm"], x2, tgt,
        wts["w_up_attn"], wts["w_up_rwkv"], wts["w_out"], sm["post_norm_gain"])

    dqkv, dbias = _attn_bwd(qkv3, o_attn, lse, do_attn.reshape(bsz, s, WIDTH), sm["rel_bias"], bidx)
    g_bias = _bias_grad(dbias, bidx)[:, :N_BUCKET].T

    scan_cots, (g_rk, g_lnw, g_lnb) = _rwkv_scan_bwd(scan_in, states, consts, do_rwkv.reshape(bsz, s, WIDTH), rk, sm["rwkv_ln_w"],
                                                     sm["rwkv_ln_b"])
    dprs, g_mix, g_w0, g_wup, g_a0, g_aup, g_kk, g_ka = _rwkv_pre_bwd(pr3, scan_cots, *pre_args)
    dpr = _shift_bwd(dprs, sm["rwkv_shift_mix"]).reshape(n, PR_COLS)

    dsegs = [(dqkv.reshape(9, n, WIDTH), 0, QKV_COLS, WIDTH), (dza, OFF_ZA, WIDTH, WIDTH), (dpr, OFF_PR, PR_COLS, PR_COLS),
             (dzr, OFF_ZR, WIDTH, WIDTH), (dgm, OFF_GM, 2 * D_MODEL, D_MODEL)]
    dh = None
    g_win = []
    for j, (t, off, cnt, tn) in enumerate(dsegs):
        dh = _mm_nt_acc(t, w_in[:, off:off + cnt], dh, "dh_%d" % j)
        g_win.append(_mm_tn(h, t, tn, "gw_in_%d" % j))
    grad_x, g_pre = _prenorm_bwd(dh, x2, rs, sm["pre_norm_gain"], dxo)

    full = {"w_in": jnp.concatenate(g_win, axis=1), "w_up_attn": g_wua, "w_up_rwkv": g_wur, "w_out": g_wout,
            "rwkv_w_up": g_wup, "rwkv_a_up": g_aup}
    small = {"pre_norm_gain": g_pre, "rel_bias": g_bias, "rwkv_shift_mix": g_mix, "rwkv_w0": g_w0, "rwkv_a0": g_a0, "rwkv_k_k": g_kk,
             "rwkv_k_a": g_ka, "rwkv_r_k": g_rk, "rwkv_ln_w": g_lnw, "rwkv_ln_b": g_lnb, "post_norm_gain": g_post}
    return loss[0, 0], grad_x.reshape(bsz, s, d), full, small


def kernel(x, pre_norm_gain, w_in, rel_bias, rwkv_shift_mix, rwkv_w0, rwkv_w_up, rwkv_a0, rwkv_a_up, rwkv_k_k, rwkv_k_a, rwkv_r_k, rwkv_ln_w, rwkv_ln_b, w_up_attn, w_up_rwkv, w_out, post_norm_gain, loss_target, m_pre_norm_gain, m_w_in, m_rel_bias, m_rwkv_shift_mix, m_rwkv_w0, m_rwkv_w_up, m_rwkv_a0, m_rwkv_a_up, m_rwkv_k_k, m_rwkv_k_a, m_rwkv_r_k, m_rwkv_ln_w, m_rwkv_ln_b, m_w_up_attn, m_w_up_rwkv, m_w_out, m_post_norm_gain, v_pre_norm_gain, v_w_in, v_rel_bias, v_rwkv_shift_mix, v_rwkv_w0, v_rwkv_w_up, v_rwkv_a0, v_rwkv_a_up, v_rwkv_k_k, v_rwkv_k_a, v_rwkv_r_k, v_rwkv_ln_w, v_rwkv_ln_b, v_w_up_attn, v_w_up_rwkv, v_w_out, v_post_norm_gain):
    names = [n for n, *_ in SHARDED] + [n for n, _ in SMALL]
    loc = dict(locals())
    w = {n: loc[n] for n in names}
    m = {n: loc["m_" + n] for n in names}
    v = {n: loc["v_" + n] for n in names}
    shapes = {n: w[n].shape for n in names}
    order = ["pre_norm_gain", "w_in", "rel_bias", "rwkv_shift_mix", "rwkv_w0", "rwkv_w_up", "rwkv_a0", "rwkv_a_up", "rwkv_k_k", "rwkv_k_a",
             "rwkv_r_k", "rwkv_ln_w", "rwkv_ln_b", "w_up_attn", "w_up_rwkv", "w_out", "post_norm_gain"]
    shard2d = lambda t, n, r, c: t[n].reshape(r, c)

    gathered = _exchange([shard2d(w, n, r, c).astype(BF16) for n, r, c, _, _ in SHARDED], [True] * len(SHARDED), "gather_weights")
    wts = {n: _whole(g, by_cols) for (n, _, _, by_cols, _), g in zip(SHARDED, gathered)}

    loss, grad_x, full, small = _local_step(x, loss_target, w, wts)
    parts = _exchange([_per_owner(full[n], by_cols).astype(BF16) for n, _, _, by_cols, _ in SHARDED] + [_pack_small(small, loss)],
                      [False] * len(SHARDED) + [True], "exchange_grads")

    outs = [{}, {}, {}, {}]
    for (n, r, c, _, tr), p in zip(SHARDED, parts):
        res = _adamw(p, shard2d(w, n, r, c), shard2d(m, n, r, c), shard2d(v, n, r, c), tr, "adamw_" + n)
        for o, t in zip(outs, res):
            o[n] = t.reshape(shapes[n])
    res = _adamw(parts[-1], _pack_small(w), _pack_small(m), _pack_small(v), SMALL_ROWS, "adamw_small")
    for o, t in zip(outs, res):
        o.update(_unpack_small(t, shapes)[0])
    loss = _unpack_small(res[0], shapes)[1]
    return (loss, grad_x, *[o[n] for o in outs for n in order])
```

```python
import functools
import math

import numpy as np
import jax
import jax.numpy as jnp
from jax import lax
from jax.experimental import pallas as pl
from jax.experimental.pallas import tpu as pltpu

F32, BF16 = jnp.float32, jnp.bfloat16
SDS = jax.ShapeDtypeStruct
HI = lax.Precision.HIGHEST
HI3 = lax.Precision.HIGH
MESH = pl.DeviceIdType.MESH

N_DEV = 8
D_MODEL = 1024
HEAD = 64
N_HEAD = 8
WIDTH = N_HEAD * HEAD
DILATIONS = (1, 4, 16)
QB = 128
N_BUCKET = 32
MAX_DIST = 2048
LORA = 64
QKV_COLS = 9 * WIDTH
PR_COLS = 3 * WIDTH + 2 * LORA
IN_COLS = QKV_COLS + WIDTH + PR_COLS + WIDTH + 2 * D_MODEL
OFF_ZA, OFF_PR, OFF_ZR, OFF_GM = QKV_COLS, QKV_COLS + WIDTH, QKV_COLS + WIDTH + PR_COLS, QKV_COLS + 2 * WIDTH + PR_COLS
RMS_EPS = 1e-6
GN_EPS = 64e-5
SCALE = 1.0 / math.sqrt(HEAD)
CHUNK = 64
CHUNK_GROUP = 8
BWD_GROUP = 8
EARLY = 8
NEG = -1e30
LANE = 128

ADAM_LR, ADAM_B1, ADAM_B2, ADAM_EPS, ADAM_WD, ADAM_STEP = 0.001, 0.9, 0.999, 1e-08, 0.01, 10

VMEM_LIMIT = 56 * 1024 * 1024

SMALL = (("pre_norm_gain", 1024), ("rel_bias", 768), ("rwkv_shift_mix", 1664), ("rwkv_w0", 512), ("rwkv_a0", 512),
         ("rwkv_k_k", 512), ("rwkv_k_a", 512), ("rwkv_r_k", 512), ("rwkv_ln_w", 512), ("rwkv_ln_b", 512),
         ("post_norm_gain", 1024))
SMALL_ROWS = 64


def _params(sem=None):
    return pltpu.CompilerParams(dimension_semantics=sem, vmem_limit_bytes=VMEM_LIMIT)


def _dot(a, b):
    return jnp.dot(a, b, preferred_element_type=F32)


def _dot_nt(a, b):
    return lax.dot_general(a, b, (((1,), (1,)), ((), ())), preferred_element_type=F32)


def _dot_tn(a, b):
    return lax.dot_general(a, b, (((0,), (0,)), ((), ())), preferred_element_type=F32)


@jax.custom_vjp
def _bdot(a, b):
    return _dot(a.astype(BF16), b.astype(BF16))


def _bdot_fwd(a, b):
    return _bdot(a, b), (a, b)


def _bdot_bwd(res, g):
    a, b = res
    gb = g.astype(BF16)
    return _dot_nt(gb, b.astype(BF16)), _dot_tn(a.astype(BF16), gb)


_bdot.defvjp(_bdot_fwd, _bdot_bwd)


def _silu(z):
    return z * jax.nn.sigmoid(z)


def _dsilu(z):
    s = jax.nn.sigmoid(z)
    return s * (1.0 + z * (1.0 - s))


def _softplus(x):
    return jnp.maximum(x, 0.0) + jnp.log(1.0 + jnp.exp(-jnp.abs(x)))


def _bucket_tables():
    qi = np.arange(QB)[:, None] + QB
    ki = np.arange(2 * QB)[None, :]
    rel = np.maximum(qi - ki, 0)
    out = []
    for d in DILATIONS:
        dist = rel * d
        max_exact = N_BUCKET // 2
        ratio = np.log(np.maximum(dist, 1).astype(np.float32) / max_exact) / np.float32(math.log(MAX_DIST / max_exact))
        large = max_exact + (ratio * (N_BUCKET - max_exact)).astype(np.int32)
        large = np.minimum(large, N_BUCKET - 1)
        out.append(np.where(dist < max_exact, dist, large).astype(np.int32))
    return np.stack(out)


def _prenorm(x2, g):
    n, d = x2.shape
    tm = 1024

    def body(x_ref, g_ref, h_ref, rs_ref):
        x = x_ref[...]
        rs = lax.rsqrt(jnp.mean(x * x, axis=-1, keepdims=True) + RMS_EPS)
        h_ref[...] = (x * rs * g_ref[...]).astype(BF16)
        rs_ref[...] = rs

    return pl.pallas_call(
        body, name="prenorm", grid=(n // tm,),
        in_specs=[pl.BlockSpec((tm, d), lambda i: (i, 0)), pl.BlockSpec((1, d), lambda i: (0, 0))],
        out_specs=[pl.BlockSpec((tm, d), lambda i: (i, 0)), pl.BlockSpec((tm, 1), lambda i: (i, 0))],
        out_shape=[SDS((n, d), BF16), SDS((n, 1), F32)], compiler_params=_params(("parallel",)))(x2, g)


def _mm(a, b, tn, name):
    m, k = a.shape
    n = b.shape[1]
    tm = 1024

    def body(a_ref, b_ref, o_ref):
        o_ref[...] = _dot(a_ref[...], b_ref[...])

    return pl.pallas_call(
        body, name=name, grid=(n // tn, m // tm),
        in_specs=[pl.BlockSpec((tm, k), lambda j, i: (i, 0)), pl.BlockSpec((k, tn), lambda j, i: (0, j))],
        out_specs=pl.BlockSpec((tm, tn), lambda j, i: (i, j)),
        out_shape=SDS((m, n), F32), compiler_params=_params(("parallel", "parallel")))(a, b)


def _mm_nt_acc(a, b, acc, name):
    split = a.ndim == 3
    m = a.shape[-2]
    k = b.shape[1]
    d = b.shape[0]
    tm = 1024
    per = 3 if split else 1
    seg = a.shape[2] if split else 0
    tk = per * seg if split else (k if k <= 2048 else 1536)
    have_acc = acc is not None

    def body(*refs):
        if have_acc:
            a_ref, b_ref, c_ref, o_ref = refs
        else:
            a_ref, b_ref, o_ref = refs
        if split:
            r = sum(_dot_nt(a_ref[j].astype(BF16), b_ref[:, seg * j:seg * (j + 1)]) for j in range(per))
        else:
            r = _dot_nt(a_ref[...].astype(BF16), b_ref[...])

        @pl.when(pl.program_id(1) == 0)
        def _():
            o_ref[...] = r + c_ref[...] if have_acc else r

        @pl.when(pl.program_id(1) != 0)
        def _():
            o_ref[...] += r

    a_spec = pl.BlockSpec((per, tm, seg), lambda i, j: (j, i, 0)) if split else pl.BlockSpec((tm, tk), lambda i, j: (i, j))
    in_specs = [a_spec, pl.BlockSpec((d, tk), lambda i, j: (0, j))]
    args = [a, b]
    if have_acc:
        in_specs.append(pl.BlockSpec((tm, d), lambda i, j: (i, 0)))
        args.append(acc)
    return pl.pallas_call(
        body, name=name, grid=(m // tm, k // tk), in_specs=in_specs, out_specs=pl.BlockSpec((tm, d), lambda i, j: (i, 0)),
        out_shape=SDS((m, d), F32), compiler_params=_params(("parallel", "arbitrary")))(*args)


def _mm_tn(a, b, tn, name):
    split = b.ndim == 3
    m, k1 = a.shape
    n2 = b.shape[0] * b.shape[2] if split else b.shape[1]
    tm = 1024

    def body(a_ref, b_ref, o_ref):
        r = _dot_tn(a_ref[...], (b_ref[0] if split else b_ref[...]).astype(BF16))

        @pl.when(pl.program_id(1) == 0)
        def _():
            o_ref[...] = r

        @pl.when(pl.program_id(1) != 0)
        def _():
            o_ref[...] += r

    b_spec = pl.BlockSpec((1, tm, tn), lambda j, i: (j, i, 0)) if split else pl.BlockSpec((tm, tn), lambda j, i: (i, j))
    return pl.pallas_call(
        body, name=name, grid=(n2 // tn, m // tm),
        in_specs=[pl.BlockSpec((tm, k1), lambda j, i: (i, 0)), b_spec],
        out_specs=pl.BlockSpec((k1, tn), lambda j, i: (0, j)),
        out_shape=SDS((k1, n2), F32), compiler_params=_params(("parallel", "arbitrary")))(a, b)


def _ds(start, d):
    return pl.ds(start, QB) if d == 1 else pl.ds(start, QB, stride=d)


def _fill_bias(tab_ref, bidx_ref, bias_sc, hp):
    for g in range(3):
        bi = bidx_ref[g]
        for h in range(2):
            acc = jnp.zeros((QB, 2 * QB), F32)
            for j in range(N_BUCKET):
                acc = jnp.where(bi == j, tab_ref[j, g * N_HEAD + hp * 2 + h], acc)
            bias_sc[g * 2 + h] = acc


def _block_starts(it, d, nb):
    rho = it // nb
    n = it % nb
    st = rho + d * QB * n
    stp = rho + d * QB * jnp.maximum(n - 1, 0)
    return st, stp, n > 0


ATTN_BLOCKS = 2


def _bdot3(a, b, dims):
    return lax.dot_general(a, b, (dims, ((0,), (0,))), preferred_element_type=F32)


def _attn_operands(q_ref, k_ref, v_ref, bias_sc, g, d, nb, it0):
    ii = lax.broadcasted_iota(jnp.int32, (QB, 2 * QB), 0)
    cc = lax.broadcasted_iota(jnp.int32, (QB, 2 * QB), 1)
    qs, ks, vs, pens, starts = [], [], [], [], []
    for u in range(ATTN_BLOCKS):
        st, stp, hasprev = _block_starts(it0 + u, d, nb)
        qf = q_ref[0, _ds(st, d), :]
        kf = jnp.concatenate([k_ref[0, _ds(stp, d), :], k_ref[0, _ds(st, d), :]], axis=0)
        vf = jnp.concatenate([v_ref[0, _ds(stp, d), :], v_ref[0, _ds(st, d), :]], axis=0)
        own = jnp.logical_and(cc >= QB, ii >= cc - QB)
        prev = jnp.logical_and(jnp.logical_and(cc < QB, cc >= ii), hasprev)
        pen = jnp.where(jnp.logical_or(own, prev), 0.0, NEG)
        for h in range(2):
            sl = slice(HEAD * h, HEAD * h + HEAD)
            qs.append(qf[:, sl])
            ks.append(kf[:, sl])
            vs.append(vf[:, sl])
            pens.append(pen + bias_sc[g * 2 + h])
        starts.append((st, stp))
    return _stack(qs).astype(BF16), _stack(ks).astype(BF16), _stack(vs).astype(BF16), _stack(pens), starts


def _heads(x, u):
    return jnp.concatenate([x[2 * u], x[2 * u + 1]], axis=1)


def _attn_fwd(qkv3, rel_bias, bidx):
    bsz, s, _ = qkv3.shape
    rt = 256

    def body(tab_ref, bidx_ref, *refs):
        q_refs, k_refs, v_refs = refs[0:3], refs[3:6], refs[6:9]
        o_ref, lse_ref = refs[9:11]
        bias_sc, num_sc, den_sc, m_sc = refs[11:]
        pl.when(pl.program_id(1) == 0)(lambda: _fill_bias(tab_ref, bidx_ref, bias_sc, pl.program_id(0)))
        for g, d in enumerate(DILATIONS):
            nb = s // (QB * d)

            def blk(it, c, g=g, d=d, nb=nb):
                q, k, v, bias, starts = _attn_operands(q_refs[g], k_refs[g], v_refs[g], bias_sc, g, d, nb, it * ATTN_BLOCKS)
                sc = _bdot3(q, k, ((2,), (2,))) * SCALE + bias
                m = jnp.max(sc, axis=-1, keepdims=True)
                p = jnp.exp(sc - m)
                den = jnp.sum(p, axis=-1, keepdims=True)
                num = _bdot3(p.astype(BF16), v, ((2,), (1,)))
                den, m = jnp.broadcast_to(den, num.shape), jnp.broadcast_to(m, num.shape)
                for u, (st, _) in enumerate(starts):
                    num_sc[g, _ds(st, d), :] = _heads(num, u)
                    den_sc[g, _ds(st, d), :] = _heads(den, u)
                    m_sc[g, _ds(st, d), :] = _heads(m, u)
                return c

            lax.fori_loop(0, s // QB // ATTN_BLOCKS, blk, 0)

        def merge(i, c):
            rows = pl.ds(pl.multiple_of(i * rt, rt), rt)
            m0, m1, m2 = m_sc[0, rows, :], m_sc[1, rows, :], m_sc[2, rows, :]
            mall = jnp.maximum(jnp.maximum(m0, m1), m2)
            w0, w1, w2 = jnp.exp(m0 - mall), jnp.exp(m1 - mall), jnp.exp(m2 - mall)
            num = w0 * num_sc[0, rows, :] + w1 * num_sc[1, rows, :] + w2 * num_sc[2, rows, :]
            den = w0 * den_sc[0, rows, :] + w1 * den_sc[1, rows, :] + w2 * den_sc[2, rows, :]
            o_ref[0, rows, :] = num / den
            lse_ref[0, rows, :] = mall + jnp.log(den)
            return c

        lax.fori_loop(0, s // rt, merge, 0)

    col = lambda w, g: (lambda hp, b: (b, 0, (w * 3 + g) * 4 + hp))
    in_specs = [pl.BlockSpec(memory_space=pltpu.SMEM), pl.BlockSpec((3, QB, 2 * QB), lambda hp, b: (0, 0, 0))]
    in_specs += [pl.BlockSpec((1, s, LANE), col(w, g)) for w in range(3) for g in range(3)]
    out_spec = pl.BlockSpec((1, s, LANE), lambda hp, b: (b, 0, hp))
    return pl.pallas_call(
        body, name="attn_fwd", grid=(4, bsz), in_specs=in_specs, out_specs=[out_spec, out_spec],
        out_shape=[SDS((bsz, s, WIDTH), F32), SDS((bsz, s, WIDTH), F32)],
        scratch_shapes=[pltpu.VMEM((6, QB, 2 * QB), F32), pltpu.VMEM((3, s, LANE), F32), pltpu.VMEM((3, s, LANE), F32),
                        pltpu.VMEM((3, s, LANE), F32)],
        compiler_params=_params(("arbitrary", "arbitrary")))(rel_bias, bidx, *([qkv3] * 9))


def _attn_bwd(qkv3, o3, lse3, do3, rel_bias, bidx):
    bsz, s, _ = qkv3.shape
    rt = 256

    def body(tab_ref, bidx_ref, *refs):
        q_refs, k_refs, v_refs = refs[0:3], refs[3:6], refs[6:9]
        o_ref, lse_ref, do_ref, dqkv_ref, db_ref, bias_sc, delta_sc = refs[9:]
        dq_refs, dk_refs, dv_refs = ([dqkv_ref.at[w * 3 + g] for g in range(3)] for w in range(3))

        @pl.when(pl.program_id(1) == 0)
        def _():
            _fill_bias(tab_ref, bidx_ref, bias_sc, pl.program_id(0))
            db_ref[...] = jnp.zeros_like(db_ref)

        def prep(i, c):
            rows = pl.ds(pl.multiple_of(i * rt, rt), rt)
            prod = do_ref[0, rows, :] * o_ref[0, rows, :]
            d0 = jnp.sum(prod[:, :HEAD], axis=-1, keepdims=True)
            d1 = jnp.sum(prod[:, HEAD:], axis=-1, keepdims=True)
            delta_sc[rows, :] = jnp.concatenate([jnp.broadcast_to(d0, (rt, HEAD)), jnp.broadcast_to(d1, (rt, HEAD))], axis=1)
            z = jnp.zeros((rt, LANE), F32)
            for g in range(3):
                dk_refs[g][0, rows, :] = z
                dv_refs[g][0, rows, :] = z
            return c

        lax.fori_loop(0, s // rt, prep, 0)
        for g, d in enumerate(DILATIONS):
            nb = s // (QB * d)

            def blk(it, c, g=g, d=d, nb=nb):
                q, k, v, bias, starts = _attn_operands(q_refs[g], k_refs[g], v_refs[g], bias_sc, g, d, nb, it * ATTN_BLOCKS)
                dos, lses, deltas = [], [], []
                for st, _ in starts:
                    dof, lsef, delf = do_ref[0, _ds(st, d), :], lse_ref[0, _ds(st, d), :], delta_sc[_ds(st, d), :]
                    for h in range(2):
                        dos.append(dof[:, HEAD * h:HEAD * h + HEAD])
                        lses.append(lsef[:, HEAD * h:HEAD * h + 1])
                        deltas.append(delf[:, HEAD * h:HEAD * h + 1])
                do, lse, delta = _stack(dos).astype(BF16), _stack(lses), _stack(deltas)
                p = jnp.exp(_bdot3(q, k, ((2,), (2,))) * SCALE + bias - lse)
                dv = _bdot3(p.astype(BF16), do, ((1,), (1,)))
                ds = p * (_bdot3(do, v, ((2,), (2,))) - delta)
                dsb = ds.astype(BF16)
                dq = _bdot3(dsb, k, ((2,), (1,))) * SCALE
                dk = _bdot3(dsb, q, ((1,), (1,))) * SCALE
                for h in range(2):
                    db_ref[0, g * 2 + h] += sum(ds[2 * u + h] for u in range(ATTN_BLOCKS))
                for u, (st, stp) in enumerate(starts):
                    dq_refs[g][0, _ds(st, d), :] = _heads(dq, u)
                    dk_refs[g][0, _ds(stp, d), :] += _heads(dk[:, :QB], u)
                    dv_refs[g][0, _ds(stp, d), :] += _heads(dv[:, :QB], u)
                    dk_refs[g][0, _ds(st, d), :] += _heads(dk[:, QB:], u)
                    dv_refs[g][0, _ds(st, d), :] += _heads(dv[:, QB:], u)
                return c

            lax.fori_loop(0, s // QB // ATTN_BLOCKS, blk, 0)

    col = lambda w, g: (lambda hp, b: (b, 0, (w * 3 + g) * 4 + hp))
    blk_spec = pl.BlockSpec((1, s, LANE), lambda hp, b: (b, 0, hp))
    in_specs = [pl.BlockSpec(memory_space=pltpu.SMEM), pl.BlockSpec((3, QB, 2 * QB), lambda hp, b: (0, 0, 0))]
    in_specs += [pl.BlockSpec((1, s, LANE), col(w, g)) for w in range(3) for g in range(3)]
    in_specs += [blk_spec] * 3
    out_specs = [pl.BlockSpec((9, 1, s, LANE), lambda hp, b: (0, b, 0, hp)), pl.BlockSpec((1, 6, QB, 2 * QB), lambda hp, b: (hp, 0, 0, 0))]
    out_shape = [SDS((9, bsz, s, WIDTH), F32), SDS((4, 6, QB, 2 * QB), F32)]
    return pl.pallas_call(
        body, name="attn_bwd", grid=(4, bsz), in_specs=in_specs, out_specs=out_specs, out_shape=out_shape,
        scratch_shapes=[pltpu.VMEM((6, QB, 2 * QB), F32), pltpu.VMEM((s, LANE), F32)],
        compiler_params=_params(("parallel", "arbitrary")))(rel_bias, bidx, *([qkv3] * 9), o3, lse3, do3)


def _bias_grad(dbias, bidx):
    def body(db_ref, bidx_ref, o_ref):
        lane = lax.broadcasted_iota(jnp.int32, (1, LANE), 1)
        for g in range(3):
            bi = bidx_ref[g]
            for hp in range(4):
                for h in range(2):
                    mat = db_ref[hp, g * 2 + h]
                    row = jnp.zeros((1, LANE), F32)
                    for j in range(N_BUCKET):
                        part = jnp.sum(jnp.where(bi == j, mat, 0.0), axis=0, keepdims=True)
                        row = jnp.where(lane == j, jnp.sum(part, axis=1, keepdims=True), row)
                    hd = g * N_HEAD + hp * 2 + h
                    o_ref[hd:hd + 1, :] = row

    return pl.pallas_call(body, name="bias_grad", out_shape=SDS((3 * N_HEAD, LANE), F32), compiler_params=_params())(dbias, bidx)


def _pre_fn(r, k0, v, wl, al, w0, wup, a0, aup, kk_, ka_):
    u = w0 + _bdot(jnp.tanh(wl), wup)
    lw = -jnp.exp(-_softplus(-u) - 0.5)
    a = jax.nn.sigmoid(a0 + _bdot(al, aup))
    kkraw = k0 * kk_
    k = k0 * (1.0 + (a - 1.0) * ka_)
    return r, lw, k, v, kkraw, a


PRE_SPLIT = (0, WIDTH, 2 * WIDTH, 3 * WIDTH, 3 * WIDTH + LORA, 3 * WIDTH + 2 * LORA)


def _pre_pieces(prs):
    return [prs[:, a:b] for a, b in zip(PRE_SPLIT[:-1], PRE_SPLIT[1:])]


PRE_TT = 512


def _shifted(pr_ref, edge_ref, first, back):
    pr = pr_ref[0]
    tt = pr.shape[0]
    row = lax.broadcasted_iota(jnp.int32, (tt, 1), 0)
    if back:
        edge = jnp.where(first, 0.0, edge_ref[0, 7:8, :])
        return jnp.where(row == 0, edge, pltpu.roll(pr, 1, axis=0))
    edge = jnp.where(first, 0.0, edge_ref[0, 0:1, :])
    return jnp.where(row == tt - 1, edge, pltpu.roll(pr, tt - 1, axis=0))


def _rwkv_pre(pr3, mix, w0, wup, a0, aup, kk_, ka_):
    bsz, s, _ = pr3.shape
    tt = PRE_TT

    def body(pr_ref, edge_ref, mix_ref, w0_ref, wup_ref, a0_ref, aup_ref, kk_ref, ka_ref, *outs):
        pr = pr_ref[0]
        prev = _shifted(pr_ref, edge_ref, pl.program_id(1) == 0, True)
        prs = pr + (prev - pr) * mix_ref[...]
        vals = _pre_fn(*_pre_pieces(prs), w0_ref[...], wup_ref[...].astype(F32), a0_ref[...], aup_ref[...].astype(F32), kk_ref[...],
                       ka_ref[...])
        for o, val in zip(outs, vals):
            o[0] = val

    vec = lambda n: pl.BlockSpec((1, n), lambda b, i: (0, 0))
    mat = pl.BlockSpec((LORA, WIDTH), lambda b, i: (0, 0))
    in_specs = [pl.BlockSpec((1, tt, PR_COLS), lambda b, i: (b, i, 0)),
                pl.BlockSpec((1, 8, PR_COLS), lambda b, i: (b, jnp.maximum(i * (tt // 8) - 1, 0), 0)),
                vec(PR_COLS), vec(WIDTH), mat, vec(WIDTH), mat, vec(WIDTH), vec(WIDTH)]
    out_spec = pl.BlockSpec((1, tt, WIDTH), lambda b, i: (b, i, 0))
    return pl.pallas_call(
        body, name="rwkv_pre", grid=(bsz, s // tt), in_specs=in_specs, out_specs=[out_spec] * 6,
        out_shape=[SDS((bsz, s, WIDTH), F32)] * 6, compiler_params=_params(("parallel", "parallel")))(
            pr3, pr3, mix, w0, wup, a0, aup, kk_, ka_)


def _rwkv_pre_bwd(pr3, cots, mix, w0, wup, a0, aup, kk_, ka_):
    bsz, s, _ = pr3.shape
    tt = PRE_TT

    def body(pr_ref, edge_ref, c0, c1, c2, c3, c4, c5, mix_ref, w0_ref, wup_ref, a0_ref, aup_ref, kk_ref, ka_ref,
             dprs_ref, dmix_ref, dw0_ref, dwup_ref, da0_ref, daup_ref, dkk_ref, dka_ref):
        pr = pr_ref[0]
        prev = _shifted(pr_ref, edge_ref, pl.program_id(1) == 0, True)
        prs = pr + (prev - pr) * mix_ref[...]
        _, vjp = jax.vjp(_pre_fn, *_pre_pieces(prs), w0_ref[...], wup_ref[...].astype(F32), a0_ref[...], aup_ref[...].astype(F32),
                         kk_ref[...], ka_ref[...])
        grads = vjp(tuple(c[0] for c in (c0, c1, c2, c3, c4, c5)))
        for piece, a, b in zip(grads[:5], PRE_SPLIT[:-1], PRE_SPLIT[1:]):
            dprs_ref[0, :, a:b] = piece
        dw0, dwup, da0, daup, dkk, dka = grads[5:]
        dprs = dprs_ref[0]
        grads = (jnp.sum(dprs * (prev - pr), axis=0, keepdims=True), dw0, dwup, da0, daup, dkk, dka)
        refs = (dmix_ref, dw0_ref, dwup_ref, da0_ref, daup_ref, dkk_ref, dka_ref)
        first = jnp.logical_and(pl.program_id(0) == 0, pl.program_id(1) == 0)

        @pl.when(first)
        def _():
            for r_, g_ in zip(refs, grads):
                r_[...] = g_

        @pl.when(jnp.logical_not(first))
        def _():
            for r_, g_ in zip(refs, grads):
                r_[...] += g_

    vec = lambda n: pl.BlockSpec((1, n), lambda b, i: (0, 0))
    mat = pl.BlockSpec((LORA, WIDTH), lambda b, i: (0, 0))
    tile = pl.BlockSpec((1, tt, WIDTH), lambda b, i: (b, i, 0))
    in_specs = [pl.BlockSpec((1, tt, PR_COLS), lambda b, i: (b, i, 0)),
                pl.BlockSpec((1, 8, PR_COLS), lambda b, i: (b, jnp.maximum(i * (tt // 8) - 1, 0), 0))]
    in_specs += [tile] * 6 + [vec(PR_COLS), vec(WIDTH), mat, vec(WIDTH), mat, vec(WIDTH), vec(WIDTH)]
    out_specs = [pl.BlockSpec((1, tt, PR_COLS), lambda b, i: (b, i, 0)), vec(PR_COLS), vec(WIDTH), mat, vec(WIDTH), mat,
                 vec(WIDTH), vec(WIDTH)]
    out_shape = [SDS((bsz, s, PR_COLS), F32), SDS((1, PR_COLS), F32), SDS((1, WIDTH), F32), SDS((LORA, WIDTH), F32),
                 SDS((1, WIDTH), F32), SDS((LORA, WIDTH), F32), SDS((1, WIDTH), F32), SDS((1, WIDTH), F32)]
    return pl.pallas_call(
        body, name="rwkv_pre_bwd", grid=(bsz, s // tt), in_specs=in_specs, out_specs=out_specs, out_shape=out_shape,
        compiler_params=_params(("arbitrary", "arbitrary")))(pr3, pr3, *cots, mix, w0, wup, a0, aup, kk_, ka_)


def _shift_bwd(dprs3, mix):
    bsz, s, _ = dprs3.shape
    tt = PRE_TT
    nt = s // tt

    def body(d_ref, edge_ref, mix_ref, o_ref):
        nxt = _shifted(d_ref, edge_ref, pl.program_id(1) == nt - 1, False)
        m = mix_ref[...]
        o_ref[0] = d_ref[0] * (1.0 - m) + nxt * m

    in_specs = [pl.BlockSpec((1, tt, PR_COLS), lambda b, i: (b, i, 0)),
                pl.BlockSpec((1, 8, PR_COLS), lambda b, i: (b, jnp.minimum((i + 1) * (tt // 8), s // 8 - 1), 0)),
                pl.BlockSpec((1, PR_COLS), lambda b, i: (0, 0))]
    return pl.pallas_call(
        body, name="shift_bwd", grid=(bsz, nt), in_specs=in_specs, out_specs=pl.BlockSpec((1, tt, PR_COLS), lambda b, i: (b, i, 0)),
        out_shape=SDS((bsz, s, PR_COLS), F32), compiler_params=_params(("parallel", "parallel")))(dprs3, dprs3, mix)


_NN, _NT, _TN = ((2,), (1,)), ((2,), (2,)), ((1,), (1,))


def _dot3(a, b, dims, precision=HI3):
    return lax.dot_general(a, b, (dims, ((0,), (0,))), precision=precision, preferred_element_type=F32)


def _dot3_bf16(a, b, dims):
    return lax.dot_general(a.astype(BF16), b.astype(BF16), (dims, ((0,), (0,))), preferred_element_type=F32)


class _Dots:
    def __init__(self, fwd):
        def make(dims, da_rule, db_rule):
            @jax.custom_vjp
            def f(a, b):
                return fwd(a, b, dims)

            f.defvjp(lambda a, b: (f(a, b), (a, b)), lambda res, g: (da_rule(*res, g), db_rule(*res, g)))
            return f

        one = _dot3_bf16
        self.mm = make(_NN, lambda a, b, g: one(g, b, _NT), lambda a, b, g: one(a, g, _TN))
        self.mm_nt = make(_NT, lambda a, b, g: one(g, b, _NN), lambda a, b, g: one(g, a, _TN))
        self.mm_tn = make(_TN, lambda a, b, g: one(b, g, _NT), lambda a, b, g: one(a, g, _NN))

        def powers(aab):
            ps = [aab]
            while 2 ** len(ps) < aab.shape[1]:
                ps.append(fwd(ps[-1], ps[-1], _NN))
            return ps

        def apply(ps, z, dims):
            for p in ps:
                z = z + fwd(p, z, dims)
            return z

        @jax.custom_vjp
        def solve(aab, z):
            return apply(powers(aab), z, _NN)

        def solve_fwd(aab, z):
            ps = powers(aab)
            x = apply(ps, z, _NN)
            return x, (ps, x)

        def solve_bwd(res, g):
            ps, x = res
            dz = apply(ps, g, _TN)
            return fwd(dz, x, _NT), dz

        solve.defvjp(solve_fwd, solve_bwd)
        self.solve = solve


_ACCURATE = _Dots(_dot3)
_ONE_PASS = _Dots(_dot3_bf16)
_bmm, _bmm_tn = _ACCURATE.mm, _ACCURATE.mm_tn


def _chunk_fn(s0t, r, lw, k, v, kkraw, a, rk, lnw, lnb, first=False, d=_ACCURATE):
    c = r.shape[1]
    at, rt, btc, ktc, gc, aab, arb, xv, arkv, ain, bin_ = _chunk_core(r, lw, k, v, kkraw, a, d)
    rs = d.mm(jnp.concatenate([at, rt], axis=1), s0t)
    u = d.solve(aab, rs[:, :c] + xv)
    y = rs[:, c:] + d.mm(arb, u) + arkv
    if first:
        y = _with_early_rows(y, r, lw, k, v, ain, bin_)
    gcol = jnp.sum(_diag(gc), axis=2, keepdims=True)
    sct = gcol * s0t + d.mm_tn(jnp.concatenate([btc, ktc], axis=1), jnp.concatenate([u, v], axis=1))
    return _post(y, r, k, v, rk, lnw, lnb), sct


def _diag(gc):
    return jnp.where(_masks(HEAD)[2], gc, 0.0)


def _with_early_rows(y, r, lw, k, v, ain, bin_):
    early = _stack([_early_rows(r[h], lw[h], k[h], v[h], ain[h], bin_[h]) for h in range(2)])
    return jnp.concatenate([jnp.concatenate([early, y[:2, EARLY:]], axis=1), y[2:]], axis=0)


def _early_rows(r, lw, k, v, ain, bin_):
    wc, bc, kc = jnp.transpose(jnp.exp(lw)), jnp.transpose(bin_), jnp.transpose(k)
    st = jnp.zeros((HEAD, HEAD), F32)
    rows = []
    for t in range(EARLY):
        sa = _bdot(ain[t:t + 1], st)
        st = st * wc[:, t:t + 1] + bc[:, t:t + 1] * sa + kc[:, t:t + 1] * v[t:t + 1]
        rows.append(_bdot(r[t:t + 1], st))
    return jnp.concatenate(rows, axis=0)


def _chunk_rows(c):
    return pl.ds(c * CHUNK, CHUNK) if isinstance(c, int) else pl.ds(pl.multiple_of(c * CHUNK, CHUNK), CHUNK)


def _stack(xs):
    return jnp.concatenate([x[None] for x in xs], axis=0)


def _pairs(ref, chunks):
    tiles = [ref[0, _chunk_rows(c), :] for c in chunks]
    return _stack([t[:, HEAD * h:HEAD * h + HEAD] for t in tiles for h in range(2)])


def _unpair(vals, j):
    return jnp.concatenate([vals[2 * j], vals[2 * j + 1]], axis=1)


def _masks(c):
    ii = lax.broadcasted_iota(jnp.int32, (c, c), 0)
    jj = lax.broadcasted_iota(jnp.int32, (c, c), 1)
    return ii > jj, ii >= jj, ii == jj


def _chunk_core(r, lw, k, v, kkraw, a, d=_ACCURATE):
    g_, c = r.shape[0], r.shape[1]
    nrm = jnp.sqrt(jnp.sum(kkraw * kkraw, axis=-1, keepdims=True))
    kkn = kkraw / jnp.maximum(nrm, 1e-12)
    ain, bin_ = -kkn, kkn * a
    strict, incl, _ = _masks(c)
    lg = lax.dot_general(jnp.broadcast_to(incl.astype(F32), (g_, c, c)), lw, (((2,), (1,)), ((0,), (0,))), precision=HI,
                         preferred_element_type=F32)
    g, gp, gi = jnp.exp(lg), jnp.exp(lg - lw), jnp.exp(-lg)
    at, rt, bt, kt = ain * gp, r * g, bin_ * gi, k * gi
    aa = d.mm_nt(jnp.concatenate([at, rt], axis=1), jnp.concatenate([bt, kt], axis=1))
    aab = jnp.where(strict, aa[:, :c, :c], 0.0)
    aak = jnp.where(strict, aa[:, :c, c:], 0.0)
    arb = jnp.where(incl, aa[:, c:, :c], 0.0)
    ark = jnp.where(incl, aa[:, c:, c:], 0.0)
    akv = d.mm(jnp.concatenate([aak, ark], axis=1), v)
    gc = g[:, c - 1:c, :]
    return at, rt, bt * gc, kt * gc, gc, aab, arb, akv[:, :c], akv[:, c:], ain, bin_


def _post(y, r, k, v, rk, lnw, lnb):
    mu = jnp.mean(y, axis=-1, keepdims=True)
    var = jnp.mean(jnp.square(y - mu), axis=-1, keepdims=True)
    yn = (y - mu) * lax.rsqrt(var + GN_EPS) * lnw + lnb
    return yn + jnp.sum(r * k * rk, axis=-1, keepdims=True) * v


def _chunk_consts(r, lw, k, v, kkraw, a, first=False):
    at, rt, btc, ktc, gc, aab, arb, xv, arkv, ain, bin_ = _chunk_core(r, lw, k, v, kkraw, a)
    z = _ACCURATE.solve(aab, jnp.concatenate([at, xv], axis=2))
    ryv = jnp.concatenate([rt, arkv], axis=2) + _bmm(arb, z)
    if first:
        ryv = jnp.concatenate([ryv[:, :, :HEAD], _with_early_rows(ryv[:, :, HEAD:], r, lw, k, v, ain, bin_)], axis=2)
    mkv = _bmm_tn(btc, z) + jnp.concatenate([_diag(gc), _bmm_tn(ktc, v)], axis=2)
    return mkv, ryv


def _rwkv_scan(ins, rk, lnw, lnb):
    bsz, s, _ = ins[0].shape
    nch = s // CHUNK

    def consts_body(r_ref, lw_ref, k_ref, v_ref, kk_ref, a_ref, mkv_ref, ry_ref, yv_ref):
        def group(i, carry):
            chunks = [i * CHUNK_GROUP + j for j in range(CHUNK_GROUP)]
            mkv, ryv = _chunk_consts(*[_pairs(ref, chunks) for ref in (r_ref, lw_ref, k_ref, v_ref, kk_ref, a_ref)],
                                     first=isinstance(i, int) and i == 0)
            for j, c in enumerate(chunks):
                for h in range(2):
                    mkv_ref[0, 0, c, h] = mkv[2 * j + h]
                ry_ref[0, _chunk_rows(c), :] = jnp.concatenate([ryv[2 * j][:, :HEAD], ryv[2 * j + 1][:, :HEAD]], axis=1)
                yv_ref[0, _chunk_rows(c), :] = jnp.concatenate([ryv[2 * j][:, HEAD:], ryv[2 * j + 1][:, HEAD:]], axis=1)
            return carry

        group(0, 0)
        lax.fori_loop(1, nch // CHUNK_GROUP, group, 0)

    tile = pl.BlockSpec((1, s, LANE), lambda b, hp: (b, 0, hp))
    vec = pl.BlockSpec((1, LANE), lambda b, hp: (0, hp))
    mkv_spec = pl.BlockSpec((1, 1, nch, 2, HEAD, LANE), lambda b, hp: (b, hp, 0, 0, 0, 0))
    st_spec = pl.BlockSpec((1, 1, nch, 2, HEAD, HEAD), lambda b, hp: (b, hp, 0, 0, 0, 0))
    mkv, ry, yv = pl.pallas_call(
        consts_body, name="rwkv_consts", grid=(bsz, 4), in_specs=[tile] * 6, out_specs=[mkv_spec, tile, tile],
        out_shape=[SDS((bsz, 4, nch, 2, HEAD, LANE), F32), SDS((bsz, s, WIDTH), F32), SDS((bsz, s, WIDTH), F32)],
        compiler_params=_params(("parallel", "parallel")))(*ins)

    states = _chunk_recurrence(mkv, None, "rwkv_states")

    def out_body(ry_ref, yv_ref, r_ref, k_ref, v_ref, st_ref, rk_ref, lnw_ref, lnb_ref, o_ref):
        y, r, k, v, rk_, lnw_, lnb_ = _scan_rows(ry_ref, yv_ref, r_ref, k_ref, v_ref, st_ref, rk_ref, lnw_ref, lnb_ref)
        o = _post(y, r, k, v, rk_, lnw_, lnb_)
        for j in range(CHUNK_GROUP):
            o_ref[0, _chunk_rows(j), :] = _unpair(o, j)

    o = pl.pallas_call(
        out_body, name="rwkv_out", grid=(bsz, 4, nch // CHUNK_GROUP), in_specs=_group_specs(5), out_specs=_group_specs(1)[0],
        out_shape=SDS((bsz, s, WIDTH), F32),
        compiler_params=_params(("parallel", "parallel", "parallel")))(ry, yv, ins[0], ins[2], ins[3], states, rk, lnw, lnb)
    return o, states, (mkv, ry, yv)


def _group_specs(n_tiles):
    tile = pl.BlockSpec((1, CHUNK_GROUP * CHUNK, LANE), lambda b, hp, t: (b, t, hp))
    if n_tiles == 1:
        return [tile]
    st = pl.BlockSpec((1, 1, CHUNK_GROUP, 2, HEAD, HEAD), lambda b, hp, t: (b, hp, t, 0, 0, 0))
    vec = pl.BlockSpec((1, LANE), lambda b, hp, t: (0, hp))
    return [tile] * n_tiles + [st] + [vec] * 3


def _scan_rows(ry_ref, yv_ref, r_ref, k_ref, v_ref, st_ref, rk_ref, lnw_ref, lnb_ref):
    chunks = list(range(CHUNK_GROUP))
    ry, yv, r, k, v = (_pairs(ref, chunks) for ref in (ry_ref, yv_ref, r_ref, k_ref, v_ref))
    st = _stack([st_ref[0, 0, c, h] for c in chunks for h in range(2)])
    vecs = [_stack([ref[:, HEAD * h:HEAD * h + HEAD] for _ in chunks for h in range(2)]) for ref in (rk_ref, lnw_ref, lnb_ref)]
    return (_bmm(ry, st) + yv, r, k, v, *vecs)


def _chunk_recurrence(mkv, q, name):
    bsz, _, nch = mkv.shape[:3]
    pairs = [(hp, h) for hp in range(4) for h in range(2)]

    def body(*refs):
        mkv_ref, out_ref, acc = refs[0], refs[-2], refs[-1]
        acc[...] = jnp.zeros_like(acc)

        def step(i, carry):
            c = i if q is None else nch - 1 - i
            cur = acc[...]
            for j, (hp, h) in enumerate(pairs):
                out_ref[0, hp, c, h] = cur[j]
            m = _stack([mkv_ref[0, hp, c, h] for hp, h in pairs])
            if q is None:
                acc[...] = _bmm(m[:, :, :HEAD], cur) + m[:, :, HEAD:]
            else:
                acc[...] = _bmm_tn(m[:, :, :HEAD], cur) + _stack([refs[1][0, hp, c, h] for hp, h in pairs])
            return carry

        lax.fori_loop(0, nch, step, 0)

    spec = lambda w: pl.BlockSpec((1, 4, nch, 2, HEAD, w), lambda b: (b, 0, 0, 0, 0, 0))
    return pl.pallas_call(
        body, name=name, grid=(bsz,), in_specs=[spec(LANE)] + ([] if q is None else [spec(HEAD)]), out_specs=spec(HEAD),
        out_shape=SDS((bsz, 4, nch, 2, HEAD, HEAD), F32), scratch_shapes=[pltpu.VMEM((8, HEAD, HEAD), F32)],
        compiler_params=_params(("parallel",)))(*([mkv] if q is None else [mkv, q]))


def _rwkv_scan_bwd(ins, states, consts, do3, rk, lnw, lnb):
    bsz, s, _ = ins[0].shape
    nch = s // CHUNK

    mkv, ry, yv = consts

    def q_body(do_ref, ry_ref, yv_ref, r_ref, k_ref, v_ref, st_ref, rk_ref, lnw_ref, lnb_ref, q_ref):
        y, r, k, v, rk_, lnw_, lnb_ = _scan_rows(ry_ref, yv_ref, r_ref, k_ref, v_ref, st_ref, rk_ref, lnw_ref, lnb_ref)
        _, vjp = jax.vjp(lambda y_: _post(y_, r, k, v, rk_, lnw_, lnb_), y)
        (dy,) = vjp(_pairs(do_ref, list(range(CHUNK_GROUP))))
        q = _bmm_tn(_pairs(ry_ref, list(range(CHUNK_GROUP))), dy)
        for j in range(CHUNK_GROUP):
            for h in range(2):
                q_ref[0, 0, j, h] = q[2 * j + h]

    specs = _group_specs(6)
    q = pl.pallas_call(
        q_body, name="rwkv_q", grid=(bsz, 4, nch // CHUNK_GROUP), in_specs=specs, out_specs=specs[6],
        out_shape=SDS((bsz, 4, nch, 2, HEAD, HEAD), F32),
        compiler_params=_params(("parallel", "parallel", "parallel")))(do3, ry, yv, ins[0], ins[2], ins[3], states, rk, lnw, lnb)

    dstates = _chunk_recurrence(mkv, q, "rwkv_dstates")

    def body(r_ref, lw_ref, k_ref, v_ref, kk_ref, a_ref, st_ref, dst_ref, do_ref, rk_ref, lnw_ref, lnb_ref,
             dr_ref, dlw_ref, dk_ref, dv_ref, dkk_ref, da_ref, drk_ref, dlnw_ref, dlnb_ref):
        chunks = list(range(BWD_GROUP))
        par_refs = (drk_ref, dlnw_ref, dlnb_ref)

        @pl.when(jnp.logical_and(pl.program_id(1) == 0, pl.program_id(2) == 0))
        def _():
            for ref in par_refs:
                ref[...] = jnp.zeros_like(ref)

        def group(first):
            per_pair = lambda ref: _stack([ref[0, 0, c, h] for c in chunks for h in range(2)])
            vecs = [_stack([ref[:, HEAD * h:HEAD * h + HEAD] for _ in chunks for h in range(2)]) for ref in (rk_ref, lnw_ref, lnb_ref)]
            _, vjp = jax.vjp(functools.partial(_chunk_fn, first=first, d=_ONE_PASS), per_pair(st_ref),
                             *[_pairs(ref, chunks) for ref in (r_ref, lw_ref, k_ref, v_ref, kk_ref, a_ref)], *vecs)
            grads = vjp((_pairs(do_ref, chunks), per_pair(dst_ref)))
            for ref, cot in zip((dr_ref, dlw_ref, dk_ref, dv_ref, dkk_ref, da_ref), grads[1:7]):
                for j, c in enumerate(chunks):
                    ref[0, _chunk_rows(c), :] = _unpair(cot, j)
            for ref, g_ in zip(par_refs, grads[7:10]):
                ref[...] += jnp.concatenate([sum(g_[2 * j + h] for j in range(BWD_GROUP)) for h in range(2)], axis=1)

        pl.when(pl.program_id(2) == 0)(functools.partial(group, True))
        pl.when(pl.program_id(2) != 0)(functools.partial(group, False))

    tt = BWD_GROUP * CHUNK
    tile = pl.BlockSpec((1, tt, LANE), lambda hp, b, t: (b, t, hp))
    vec = pl.BlockSpec((1, LANE), lambda hp, b, t: (0, hp))
    st_spec = pl.BlockSpec((1, 1, BWD_GROUP, 2, HEAD, HEAD), lambda hp, b, t: (b, hp, t, 0, 0, 0))
    outs = pl.pallas_call(
        body, name="rwkv_scan_bwd", grid=(4, bsz, s // tt), in_specs=[tile] * 6 + [st_spec, st_spec, tile] + [vec] * 3,
        out_specs=[tile] * 6 + [vec] * 3,
        out_shape=[SDS((bsz, s, WIDTH), F32)] * 6 + [SDS((1, WIDTH), F32)] * 3,
        compiler_params=_params(("parallel", "arbitrary", "arbitrary")))(*ins, states, dstates, do3, rk, lnw, lnb)
    return outs[:6], outs[6:]


def _head(o_attn, o_rwkv, z_attn, z_rwkv, gm, x2, tgt, wua, wur, wout, g2):
    n = x2.shape[0]
    tm = 256
    nt = n // tm
    d = D_MODEL

    def body(oa_ref, or_ref, za_ref, zr_ref, gm_ref, x_ref, t_ref, wua_ref, wur_ref, wout_ref, g2_ref,
             dxo_ref, doa_ref, dor_ref, dza_ref, dzr_ref, dgm_ref, dwua_ref, dwur_ref, dwout_ref, dg2_ref, loss_ref, lacc):
        i = pl.program_id(0)
        oa, orw, za, zr = oa_ref[...], or_ref[...], za_ref[...], zr_ref[...]
        ga, gb = gm_ref[:, 0:d], gm_ref[:, d:2 * d]
        am = (oa * _silu(za)).astype(BF16)
        bm = (orw * _silu(zr)).astype(BF16)
        ya, yb = _dot(am, wua_ref[...]), _dot(bm, wur_ref[...])
        sa, sb = jax.nn.sigmoid(ga), jax.nn.sigmoid(gb)
        merged = (sa * ya + sb * yb).astype(BF16)
        out = _dot(merged, wout_ref[...])
        rs = lax.rsqrt(jnp.mean(out * out, axis=-1, keepdims=True) + RMS_EPS)
        g2 = g2_ref[...]
        err = x_ref[...] + out * rs * g2 - t_ref[...]
        lpart = jnp.sum(err * err, axis=0, keepdims=True)
        dxo = err * (1.0 / d)
        dxo_ref[...] = dxo
        dg2 = jnp.sum(dxo * out * rs, axis=0, keepdims=True)
        gd = dxo * g2
        dout = (rs * (gd - out * (rs * rs) * jnp.mean(gd * out, axis=-1, keepdims=True))).astype(BF16)
        dmerged = _dot_nt(dout, wout_ref[...])
        dwout = _dot_tn(merged, dout)
        dya, dyb = (dmerged * sa).astype(BF16), (dmerged * sb).astype(BF16)
        dgm_ref[:, 0:d] = dmerged * ya * sa * (1.0 - sa)
        dgm_ref[:, d:2 * d] = dmerged * yb * sb * (1.0 - sb)
        dam, dbm = _dot_nt(dya, wua_ref[...]), _dot_nt(dyb, wur_ref[...])
        dwua, dwur = _dot_tn(am, dya), _dot_tn(bm, dyb)
        doa_ref[...] = dam * _silu(za)
        dza_ref[...] = dam * oa * _dsilu(za)
        dor_ref[...] = dbm * _silu(zr)
        dzr_ref[...] = dbm * orw * _dsilu(zr)

        @pl.when(i == 0)
        def _():
            dwua_ref[...], dwur_ref[...], dwout_ref[...], dg2_ref[...], lacc[...] = dwua, dwur, dwout, dg2, lpart

        @pl.when(i != 0)
        def _():
            dwua_ref[...] += dwua
            dwur_ref[...] += dwur
            dwout_ref[...] += dwout
            dg2_ref[...] += dg2
            lacc[...] += lpart

        @pl.when(i == nt - 1)
        def _():
            loss_ref[...] = jnp.sum(lacc[...], axis=1, keepdims=True) * (0.5 / d)

    t512 = pl.BlockSpec((tm, WIDTH), lambda i: (i, 0))
    t1k = pl.BlockSpec((tm, d), lambda i: (i, 0))
    t2k = pl.BlockSpec((tm, 2 * d), lambda i: (i, 0))
    full = lambda r, c: pl.BlockSpec((r, c), lambda i: (0, 0))
    return pl.pallas_call(
        body, name="head_fwd_bwd", grid=(nt,),
        in_specs=[t512, t512, t512, t512, t2k, t1k, t1k, full(WIDTH, d), full(WIDTH, d), full(d, d), full(1, d)],
        out_specs=[t1k, t512, t512, t512, t512, t2k, full(WIDTH, d), full(WIDTH, d), full(d, d), full(1, d), full(1, 1)],
        out_shape=[SDS((n, d), F32)] + [SDS((n, WIDTH), F32)] * 4 + [SDS((n, 2 * d), F32), SDS((WIDTH, d), F32), SDS((WIDTH, d), F32),
                                                                    SDS((d, d), F32), SDS((1, d), F32), SDS((1, 1), F32)],
        scratch_shapes=[pltpu.VMEM((1, d), F32)],
        compiler_params=_params(("arbitrary",)))(o_attn, o_rwkv, z_attn, z_rwkv, gm, x2, tgt, wua, wur, wout, g2)


def _prenorm_bwd(dh, x2, rs, g1, dxo):
    n, d = x2.shape
    tm = 1024

    def body(dh_ref, x_ref, rs_ref, g_ref, dxo_ref, gx_ref, dg_ref):
        x, r = x_ref[...], rs_ref[...]
        gd = dh_ref[...] * g_ref[...]
        gx_ref[...] = dxo_ref[...] + r * (gd - x * (r * r) * jnp.mean(gd * x, axis=-1, keepdims=True))
        dg = jnp.sum(dh_ref[...] * x * r, axis=0, keepdims=True)

        @pl.when(pl.program_id(0) == 0)
        def _():
            dg_ref[...] = dg

        @pl.when(pl.program_id(0) != 0)
        def _():
            dg_ref[...] += dg

    t = pl.BlockSpec((tm, d), lambda i: (i, 0))
    return pl.pallas_call(
        body, name="prenorm_bwd", grid=(n // tm,),
        in_specs=[t, t, pl.BlockSpec((tm, 1), lambda i: (i, 0)), pl.BlockSpec((1, d), lambda i: (0, 0)), t],
        out_specs=[t, pl.BlockSpec((1, d), lambda i: (0, 0))], out_shape=[SDS((n, d), F32), SDS((1, d), F32)],
        compiler_params=_params(("arbitrary",)))(dh, x2, rs, g1, dxo)


def _mesh_pos():
    x, y, c = lax.axis_index("x"), lax.axis_index("y"), lax.axis_index("c")
    return 4 * x + 2 * y + c


def _coords(idx):
    return (idx // 4, (idx // 2) % 2, idx % 2)


def _exchange(srcs, to_all, name):
    n = len(srcs)

    def body(*refs):
        src_refs, dst_refs = refs[:n], refs[n:2 * n]
        send_sems, recv_sems, local_sems = refs[2 * n:]
        me = _mesh_pos()

        def piece(i, j):
            return src_refs[i] if to_all[i] else src_refs[i].at[j]

        def remote(i, off, peer, block, slot):
            return pltpu.make_async_remote_copy(src_ref=piece(i, block), dst_ref=dst_refs[i].at[slot],
                                                send_sem=send_sems.at[i, off - 1], recv_sem=recv_sems.at[i, off - 1],
                                                device_id=_coords(peer), device_id_type=MESH)

        local = [pltpu.make_async_copy(piece(i, me), dst_refs[i].at[me], local_sems.at[i]) for i in range(n)]
        for cp in local:
            cp.start()
        sends = []
        for off in range(1, N_DEV):
            to = (me + off) % N_DEV
            for i in range(n):
                sends.append(remote(i, off, to, to, me))
                sends[-1].start()
        for off in range(1, N_DEV):
            frm = (me + N_DEV - off) % N_DEV
            for i in range(n):
                remote(i, off, frm, me, frm).wait_recv()
        for cp in sends:
            cp.wait_send()
        for cp in local:
            cp.wait()

    outs = pl.pallas_call(
        body, name=name, in_specs=[pl.BlockSpec(memory_space=pltpu.HBM)] * n, out_specs=[pl.BlockSpec(memory_space=pltpu.HBM)] * n,
        out_shape=[SDS((N_DEV,) + s.shape[-2:], s.dtype) for s in srcs],
        scratch_shapes=[pltpu.SemaphoreType.DMA((n, N_DEV - 1)), pltpu.SemaphoreType.DMA((n, N_DEV - 1)), pltpu.SemaphoreType.DMA((n,))],
        compiler_params=pltpu.CompilerParams())(*srcs)
    return outs


def _gather(srcs, name):
    n = len(srcs)

    def body(*refs):
        src_refs, dst_refs = refs[:n], refs[n:2 * n]
        send_sems, recv_sems, local_sems = refs[2 * n:]
        x, y, c = lax.axis_index("x"), lax.axis_index("y"), lax.axis_index("c")
        me, sibling = (x, y, c), (x, y, 1 - c)
        chips = [(1 - x, y), (x, 1 - y), (1 - x, 1 - y)]

        def slot(i, dev):
            return dst_refs[i].at[4 * dev[0] + 2 * dev[1] + dev[2]]

        def copy(i, k, block, to, own=False):
            return pltpu.make_async_remote_copy(src_ref=src_refs[i] if own else slot(i, block), dst_ref=slot(i, block),
                                                send_sem=send_sems.at[i, k], recv_sem=recv_sems.at[i, k],
                                                device_id=to, device_id_type=MESH)

        local = [pltpu.make_async_copy(src_refs[i], slot(i, me), local_sems.at[i]) for i in range(n)]
        for cp in local:
            cp.start()
        sends = []
        for i in range(n):
            sends.append(copy(i, 0, me, sibling, own=True))
            sends += [copy(i, 1 + j, me, (*chip, c), own=True) for j, chip in enumerate(chips)]
        for cp in sends:
            cp.start()
        for j, chip in enumerate(chips):
            for i in range(n):
                copy(i, 1 + j, (*chip, c), me).wait_recv()
                sends.append(copy(i, 4 + j, (*chip, c), sibling))
                sends[-1].start()
        for i in range(n):
            copy(i, 0, sibling, me).wait_recv()
            for j, chip in enumerate(chips):
                copy(i, 4 + j, (*chip, 1 - c), me).wait_recv()
        for cp in sends:
            cp.wait_send()
        for cp in local:
            cp.wait()

    return pl.pallas_call(
        body, name=name, in_specs=[pl.BlockSpec(memory_space=pltpu.HBM)] * n, out_specs=[pl.BlockSpec(memory_space=pltpu.HBM)] * n,
        out_shape=[SDS((N_DEV,) + s.shape, s.dtype) for s in srcs],
        scratch_shapes=[pltpu.SemaphoreType.DMA((n, N_DEV - 1)), pltpu.SemaphoreType.DMA((n, N_DEV - 1)), pltpu.SemaphoreType.DMA((n,))],
        compiler_params=pltpu.CompilerParams())(*srcs)


def _adamw(parts, w, m, v, tr, name):
    rows, cols = w.shape
    c1, c2 = 1.0 - ADAM_B1 ** ADAM_STEP, 1.0 - ADAM_B2 ** ADAM_STEP

    def body(p_ref, w_ref, m_ref, v_ref, g_ref, d_ref, nm_ref, nv_ref):
        g = p_ref[0].astype(F32)
        for j in range(1, N_DEV):
            g = g + p_ref[j].astype(F32)
        nm = ADAM_B1 * m_ref[...] + (1.0 - ADAM_B1) * g
        nv = ADAM_B2 * v_ref[...] + (1.0 - ADAM_B2) * jnp.square(g)
        g_ref[...] = g
        nm_ref[...] = nm
        nv_ref[...] = nv
        d_ref[...] = -ADAM_LR * ((nm / c1) / (jnp.sqrt(nv / c2) + ADAM_EPS) + ADAM_WD * w_ref[...])

    t = pl.BlockSpec((tr, cols), lambda i: (i, 0))
    return pl.pallas_call(
        body, name=name, grid=(rows // tr,), in_specs=[pl.BlockSpec((N_DEV, tr, cols), lambda i: (0, i, 0)), t, t, t],
        out_specs=[t] * 4, out_shape=[SDS((rows, cols), F32)] * 4, compiler_params=_params(("parallel",)))(parts, w, m, v)


SHARDED = (("w_in", D_MODEL, IN_COLS // N_DEV, True, 128), ("w_up_attn", WIDTH, D_MODEL // N_DEV, True, WIDTH),
           ("w_up_rwkv", WIDTH, D_MODEL // N_DEV, True, WIDTH), ("w_out", D_MODEL // N_DEV, D_MODEL, False, D_MODEL // N_DEV),
           ("rwkv_w_up", LORA, WIDTH // N_DEV, True, LORA), ("rwkv_a_up", LORA, WIDTH // N_DEV, True, LORA))
LOSS_SLOT = sum(n for _, n in SMALL)


def _pack_small(small, extra=None):
    flat = [small[n].reshape(-1).astype(F32) for n, _ in SMALL]
    flat.append(jnp.zeros((1,), F32) if extra is None else extra.reshape(1))
    flat.append(jnp.zeros((SMALL_ROWS * LANE - LOSS_SLOT - 1,), F32))
    return jnp.concatenate(flat).reshape(SMALL_ROWS, LANE)


def _unpack_small(packed, shapes):
    flat = packed.reshape(-1)
    out, off = {}, 0
    for n, cnt in SMALL:
        out[n] = flat[off:off + cnt].reshape(shapes[n])
        off += cnt
    return out, flat[LOSS_SLOT]


def _whole(gathered, by_cols):
    if not by_cols:
        return gathered.reshape(-1, gathered.shape[-1])
    return gathered.transpose(1, 0, 2).reshape(gathered.shape[1], -1)


def _per_owner(full, by_cols):
    if not by_cols:
        return full.reshape(N_DEV, -1, full.shape[-1])
    return full.reshape(full.shape[0], N_DEV, -1).transpose(1, 0, 2)


def _local_step(x, loss_target, sm, wts):
    bsz, s, d = x.shape
    n = bsz * s
    x2, tgt = x.reshape(n, d), loss_target.reshape(n, d)
    bidx = jnp.asarray(_bucket_tables())
    w_in = wts["w_in"]
    segs = (("qkv", 0, QKV_COLS, 512), ("za", OFF_ZA, WIDTH, 512), ("pr", OFF_PR, PR_COLS, PR_COLS), ("zr", OFF_ZR, WIDTH, 512),
            ("gm", OFF_GM, 2 * D_MODEL, 512))

    h, rs = _prenorm(x2, sm["pre_norm_gain"])
    proj = {nm: _mm(h, w_in[:, off:off + cnt], tn, "proj_" + nm) for nm, off, cnt, tn in segs}
    qkv3 = proj["qkv"].reshape(bsz, s, QKV_COLS)
    pr3 = proj["pr"].reshape(bsz, s, PR_COLS)

    o_attn, lse = _attn_fwd(qkv3, sm["rel_bias"], bidx)
    rk = sm["rwkv_r_k"].reshape(1, WIDTH)
    pre_args = (sm["rwkv_shift_mix"], sm["rwkv_w0"], wts["rwkv_w_up"], sm["rwkv_a0"], wts["rwkv_a_up"], sm["rwkv_k_k"], sm["rwkv_k_a"])
    scan_in = _rwkv_pre(pr3, *pre_args)
    o_rwkv, states, consts = _rwkv_scan(scan_in, rk, sm["rwkv_ln_w"], sm["rwkv_ln_b"])

    (dxo, do_attn, do_rwkv, dza, dzr, dgm, g_wua, g_wur, g_wout, g_post, loss) = _head(
        o_attn.reshape(n, WIDTH), o_rwkv.reshape(n, WIDTH), proj["za"], proj["zr"], proj["gm"], x2, tgt,
        wts["w_up_attn"], wts["w_up_rwkv"], wts["w_out"], sm["post_norm_gain"])

    dqkv, dbias = _attn_bwd(qkv3, o_attn, lse, do_attn.reshape(bsz, s, WIDTH), sm["rel_bias"], bidx)
    g_bias = _bias_grad(dbias, bidx)[:, :N_BUCKET].T

    scan_cots, (g_rk, g_lnw, g_lnb) = _rwkv_scan_bwd(scan_in, states, consts, do_rwkv.reshape(bsz, s, WIDTH), rk, sm["rwkv_ln_w"],
                                                     sm["rwkv_ln_b"])
    dprs, g_mix, g_w0, g_wup, g_a0, g_aup, g_kk, g_ka = _rwkv_pre_bwd(pr3, scan_cots, *pre_args)
    dpr = _shift_bwd(dprs, sm["rwkv_shift_mix"]).reshape(n, PR_COLS)

    dsegs = [(dqkv.reshape(9, n, WIDTH), 0, QKV_COLS, WIDTH), (dza, OFF_ZA, WIDTH, WIDTH), (dpr, OFF_PR, PR_COLS, PR_COLS),
             (dzr, OFF_ZR, WIDTH, WIDTH), (dgm, OFF_GM, 2 * D_MODEL, D_MODEL)]
    dh = None
    g_win = []
    for j, (t, off, cnt, tn) in enumerate(dsegs):
        dh = _mm_nt_acc(t, w_in[:, off:off + cnt], dh, "dh_%d" % j)
        g_win.append(_mm_tn(h, t, tn, "gw_in_%d" % j))
    grad_x, g_pre = _prenorm_bwd(dh, x2, rs, sm["pre_norm_gain"], dxo)

    full = {"w_in": jnp.concatenate(g_win, axis=1), "w_up_attn": g_wua, "w_up_rwkv": g_wur, "w_out": g_wout,
            "rwkv_w_up": g_wup, "rwkv_a_up": g_aup}
    small = {"pre_norm_gain": g_pre, "rel_bias": g_bias, "rwkv_shift_mix": g_mix, "rwkv_w0": g_w0, "rwkv_a0": g_a0, "rwkv_k_k": g_kk,
             "rwkv_k_a": g_ka, "rwkv_r_k": g_rk, "rwkv_ln_w": g_lnw, "rwkv_ln_b": g_lnb, "post_norm_gain": g_post}
    return loss[0, 0], grad_x.reshape(bsz, s, d), full, small


def kernel(x, pre_norm_gain, w_in, rel_bias, rwkv_shift_mix, rwkv_w0, rwkv_w_up, rwkv_a0, rwkv_a_up, rwkv_k_k, rwkv_k_a, rwkv_r_k, rwkv_ln_w, rwkv_ln_b, w_up_attn, w_up_rwkv, w_out, post_norm_gain, loss_target, m_pre_norm_gain, m_w_in, m_rel_bias, m_rwkv_shift_mix, m_rwkv_w0, m_rwkv_w_up, m_rwkv_a0, m_rwkv_a_up, m_rwkv_k_k, m_rwkv_k_a, m_rwkv_r_k, m_rwkv_ln_w, m_rwkv_ln_b, m_w_up_attn, m_w_up_rwkv, m_w_out, m_post_norm_gain, v_pre_norm_gain, v_w_in, v_rel_bias, v_rwkv_shift_mix, v_rwkv_w0, v_rwkv_w_up, v_rwkv_a0, v_rwkv_a_up, v_rwkv_k_k, v_rwkv_k_a, v_rwkv_r_k, v_rwkv_ln_w, v_rwkv_ln_b, v_w_up_attn, v_w_up_rwkv, v_w_out, v_post_norm_gain):
    names = [n for n, *_ in SHARDED] + [n for n, _ in SMALL]
    loc = dict(locals())
    w = {n: loc[n] for n in names}
    m = {n: loc["m_" + n] for n in names}
    v = {n: loc["v_" + n] for n in names}
    shapes = {n: w[n].shape for n in names}
    order = ["pre_norm_gain", "w_in", "rel_bias", "rwkv_shift_mix", "rwkv_w0", "rwkv_w_up", "rwkv_a0", "rwkv_a_up", "rwkv_k_k", "rwkv_k_a",
             "rwkv_r_k", "rwkv_ln_w", "rwkv_ln_b", "w_up_attn", "w_up_rwkv", "w_out", "post_norm_gain"]
    shard2d = lambda t, n, r, c: t[n].reshape(r, c)

    gathered = _gather([shard2d(w, n, r, c).astype(BF16) for n, r, c, _, _ in SHARDED], "gather_weights")
    wts = {n: _whole(g, by_cols) for (n, _, _, by_cols, _), g in zip(SHARDED, gathered)}

    loss, grad_x, full, small = _local_step(x, loss_target, w, wts)
    parts = _exchange([_per_owner(full[n], by_cols).astype(BF16) for n, _, _, by_cols, _ in SHARDED] + [_pack_small(small, loss)],
                      [False] * len(SHARDED) + [True], "exchange_grads")

    outs = [{}, {}, {}, {}]
    for (n, r, c, _, tr), p in zip(SHARDED, parts):
        res = _adamw(p, shard2d(w, n, r, c), shard2d(m, n, r, c), shard2d(v, n, r, c), tr, "adamw_" + n)
        for o, t in zip(outs, res):
            o[n] = t.reshape(shapes[n])
    res = _adamw(parts[-1], _pack_small(w), _pack_small(m), _pack_small(v), SMALL_ROWS, "adamw_small")
    for o, t in zip(outs, res):
        o.update(_unpack_small(t, shapes)[0])
    loss = _unpack_small(res[0], shapes)[1]
    return (loss, grad_x, *[o[n] for o in outs for n in order])
```

```python
import functools
import math

import numpy as np
import jax
import jax.numpy as jnp
from jax import lax
from jax.experimental import pallas as pl
from jax.experimental.pallas import tpu as pltpu

F32, BF16 = jnp.float32, jnp.bfloat16
SDS = jax.ShapeDtypeStruct
HI = lax.Precision.HIGHEST
HI3 = lax.Precision.HIGH
MESH = pl.DeviceIdType.MESH

N_DEV = 8
D_MODEL = 1024
HEAD = 64
N_HEAD = 8
WIDTH = N_HEAD * HEAD
DILATIONS = (1, 4, 16)
QB = 128
N_BUCKET = 32
MAX_DIST = 2048
LORA = 64
QKV_COLS = 9 * WIDTH
PR_COLS = 3 * WIDTH + 2 * LORA
IN_COLS = QKV_COLS + WIDTH + PR_COLS + WIDTH + 2 * D_MODEL
OFF_ZA, OFF_PR, OFF_ZR, OFF_GM = QKV_COLS, QKV_COLS + WIDTH, QKV_COLS + WIDTH + PR_COLS, QKV_COLS + 2 * WIDTH + PR_COLS
RMS_EPS = 1e-6
GN_EPS = 64e-5
SCALE = 1.0 / math.sqrt(HEAD)
CHUNK = 64
CHUNK_GROUP = 8
BWD_GROUP = 8
EARLY = 8
NEG = -1e30
LANE = 128

ADAM_LR, ADAM_B1, ADAM_B2, ADAM_EPS, ADAM_WD, ADAM_STEP = 0.001, 0.9, 0.999, 1e-08, 0.01, 10

VMEM_LIMIT = 56 * 1024 * 1024

SMALL = (("pre_norm_gain", 1024), ("rel_bias", 768), ("rwkv_shift_mix", 1664), ("rwkv_w0", 512), ("rwkv_a0", 512),
         ("rwkv_k_k", 512), ("rwkv_k_a", 512), ("rwkv_r_k", 512), ("rwkv_ln_w", 512), ("rwkv_ln_b", 512),
         ("post_norm_gain", 1024))
SMALL_ROWS = 64


def _params(sem=None):
    return pltpu.CompilerParams(dimension_semantics=sem, vmem_limit_bytes=VMEM_LIMIT)


def _dot(a, b):
    return jnp.dot(a, b, preferred_element_type=F32)


def _dot_nt(a, b):
    return lax.dot_general(a, b, (((1,), (1,)), ((), ())), preferred_element_type=F32)


def _dot_tn(a, b):
    return lax.dot_general(a, b, (((0,), (0,)), ((), ())), preferred_element_type=F32)


@jax.custom_vjp
def _bdot(a, b):
    return _dot(a.astype(BF16), b.astype(BF16))


def _bdot_fwd(a, b):
    return _bdot(a, b), (a, b)


def _bdot_bwd(res, g):
    a, b = res
    gb = g.astype(BF16)
    return _dot_nt(gb, b.astype(BF16)), _dot_tn(a.astype(BF16), gb)


_bdot.defvjp(_bdot_fwd, _bdot_bwd)


def _silu(z):
    return z * jax.nn.sigmoid(z)


def _dsilu(z):
    s = jax.nn.sigmoid(z)
    return s * (1.0 + z * (1.0 - s))


def _softplus(x):
    return jnp.maximum(x, 0.0) + jnp.log(1.0 + jnp.exp(-jnp.abs(x)))


def _bucket_tables():
    qi = np.arange(QB)[:, None] + QB
    ki = np.arange(2 * QB)[None, :]
    rel = np.maximum(qi - ki, 0)
    out = []
    for d in DILATIONS:
        dist = rel * d
        max_exact = N_BUCKET // 2
        ratio = np.log(np.maximum(dist, 1).astype(np.float32) / max_exact) / np.float32(math.log(MAX_DIST / max_exact))
        large = max_exact + (ratio * (N_BUCKET - max_exact)).astype(np.int32)
        large = np.minimum(large, N_BUCKET - 1)
        out.append(np.where(dist < max_exact, dist, large).astype(np.int32))
    return np.stack(out)


def _prenorm(x2, g):
    n, d = x2.shape
    tm = 1024

    def body(x_ref, g_ref, h_ref, rs_ref):
        x = x_ref[...]
        rs = lax.rsqrt(jnp.mean(x * x, axis=-1, keepdims=True) + RMS_EPS)
        h_ref[...] = (x * rs * g_ref[...]).astype(BF16)
        rs_ref[...] = rs

    return pl.pallas_call(
        body, name="prenorm", grid=(n // tm,),
        in_specs=[pl.BlockSpec((tm, d), lambda i: (i, 0)), pl.BlockSpec((1, d), lambda i: (0, 0))],
        out_specs=[pl.BlockSpec((tm, d), lambda i: (i, 0)), pl.BlockSpec((tm, 1), lambda i: (i, 0))],
        out_shape=[SDS((n, d), BF16), SDS((n, 1), F32)], compiler_params=_params(("parallel",)))(x2, g)


def _mm(a, b, tn, name):
    m, k = a.shape
    n = b.shape[1]
    tm = 1024

    def body(a_ref, b_ref, o_ref):
        o_ref[...] = _dot(a_ref[...], b_ref[...])

    return pl.pallas_call(
        body, name=name, grid=(n // tn, m // tm),
        in_specs=[pl.BlockSpec((tm, k), lambda j, i: (i, 0)), pl.BlockSpec((k, tn), lambda j, i: (0, j))],
        out_specs=pl.BlockSpec((tm, tn), lambda j, i: (i, j)),
        out_shape=SDS((m, n), F32), compiler_params=_params(("parallel", "parallel")))(a, b)


def _mm_nt_acc(a, b, acc, name):
    split = a.ndim == 3
    m = a.shape[-2]
    k = b.shape[1]
    d = b.shape[0]
    tm = 1024
    per = 3 if split else 1
    seg = a.shape[2] if split else 0
    tk = per * seg if split else (k if k <= 2048 else 1536)
    have_acc = acc is not None

    def body(*refs):
        if have_acc:
            a_ref, b_ref, c_ref, o_ref = refs
        else:
            a_ref, b_ref, o_ref = refs
        if split:
            r = sum(_dot_nt(a_ref[j].astype(BF16), b_ref[:, seg * j:seg * (j + 1)]) for j in range(per))
        else:
            r = _dot_nt(a_ref[...].astype(BF16), b_ref[...])

        @pl.when(pl.program_id(1) == 0)
        def _():
            o_ref[...] = r + c_ref[...] if have_acc else r

        @pl.when(pl.program_id(1) != 0)
        def _():
            o_ref[...] += r

    a_spec = pl.BlockSpec((per, tm, seg), lambda i, j: (j, i, 0)) if split else pl.BlockSpec((tm, tk), lambda i, j: (i, j))
    in_specs = [a_spec, pl.BlockSpec((d, tk), lambda i, j: (0, j))]
    args = [a, b]
    if have_acc:
        in_specs.append(pl.BlockSpec((tm, d), lambda i, j: (i, 0)))
        args.append(acc)
    return pl.pallas_call(
        body, name=name, grid=(m // tm, k // tk), in_specs=in_specs, out_specs=pl.BlockSpec((tm, d), lambda i, j: (i, 0)),
        out_shape=SDS((m, d), F32), compiler_params=_params(("parallel", "arbitrary")))(*args)


def _mm_tn(a, b, tn, name):
    split = b.ndim == 3
    m, k1 = a.shape
    n2 = b.shape[0] * b.shape[2] if split else b.shape[1]
    tm = 1024

    def body(a_ref, b_ref, o_ref):
        r = _dot_tn(a_ref[...], (b_ref[0] if split else b_ref[...]).astype(BF16))

        @pl.when(pl.program_id(1) == 0)
        def _():
            o_ref[...] = r

        @pl.when(pl.program_id(1) != 0)
        def _():
            o_ref[...] += r

    b_spec = pl.BlockSpec((1, tm, tn), lambda j, i: (j, i, 0)) if split else pl.BlockSpec((tm, tn), lambda j, i: (i, j))
    return pl.pallas_call(
        body, name=name, grid=(n2 // tn, m // tm),
        in_specs=[pl.BlockSpec((tm, k1), lambda j, i: (i, 0)), b_spec],
        out_specs=pl.BlockSpec((k1, tn), lambda j, i: (0, j)),
        out_shape=SDS((k1, n2), F32), compiler_params=_params(("parallel", "arbitrary")))(a, b)


def _ds(start, d):
    return pl.ds(start, QB) if d == 1 else pl.ds(start, QB, stride=d)


def _fill_bias(tab_ref, bidx_ref, bias_sc, hp):
    for g in range(3):
        bi = bidx_ref[g]
        for h in range(2):
            acc = jnp.zeros((QB, 2 * QB), F32)
            for j in range(N_BUCKET):
                acc = jnp.where(bi == j, tab_ref[j, g * N_HEAD + hp * 2 + h], acc)
            bias_sc[g * 2 + h] = acc


def _block_starts(it, d, nb):
    rho = it // nb
    n = it % nb
    st = rho + d * QB * n
    stp = rho + d * QB * jnp.maximum(n - 1, 0)
    return st, stp, n > 0


ATTN_BLOCKS = 2


def _bdot3(a, b, dims):
    return lax.dot_general(a, b, (dims, ((0,), (0,))), preferred_element_type=F32)


def _attn_operands(q_ref, k_ref, v_ref, bias_sc, g, d, nb, it0):
    ii = lax.broadcasted_iota(jnp.int32, (QB, 2 * QB), 0)
    cc = lax.broadcasted_iota(jnp.int32, (QB, 2 * QB), 1)
    qs, ks, vs, pens, starts = [], [], [], [], []
    for u in range(ATTN_BLOCKS):
        st, stp, hasprev = _block_starts(it0 + u, d, nb)
        qf = q_ref[0, _ds(st, d), :]
        kf = jnp.concatenate([k_ref[0, _ds(stp, d), :], k_ref[0, _ds(st, d), :]], axis=0)
        vf = jnp.concatenate([v_ref[0, _ds(stp, d), :], v_ref[0, _ds(st, d), :]], axis=0)
        own = jnp.logical_and(cc >= QB, ii >= cc - QB)
        prev = jnp.logical_and(jnp.logical_and(cc < QB, cc >= ii), hasprev)
        pen = jnp.where(jnp.logical_or(own, prev), 0.0, NEG)
        for h in range(2):
            sl = slice(HEAD * h, HEAD * h + HEAD)
            qs.append(qf[:, sl])
            ks.append(kf[:, sl])
            vs.append(vf[:, sl])
            pens.append(pen + bias_sc[g * 2 + h])
        starts.append((st, stp))
    return _stack(qs).astype(BF16), _stack(ks).astype(BF16), _stack(vs).astype(BF16), _stack(pens), starts


def _heads(x, u):
    return jnp.concatenate([x[2 * u], x[2 * u + 1]], axis=1)


def _attn_fwd(qkv3, rel_bias, bidx):
    bsz, s, _ = qkv3.shape
    rt = 256

    def body(tab_ref, bidx_ref, *refs):
        q_refs, k_refs, v_refs = refs[0:3], refs[3:6], refs[6:9]
        o_ref, lse_ref = refs[9:11]
        bias_sc, num_sc, den_sc, m_sc = refs[11:]
        pl.when(pl.program_id(1) == 0)(lambda: _fill_bias(tab_ref, bidx_ref, bias_sc, pl.program_id(0)))
        for g, d in enumerate(DILATIONS):
            nb = s // (QB * d)

            def blk(it, c, g=g, d=d, nb=nb):
                q, k, v, bias, starts = _attn_operands(q_refs[g], k_refs[g], v_refs[g], bias_sc, g, d, nb, it * ATTN_BLOCKS)
                sc = _bdot3(q, k, ((2,), (2,))) * SCALE + bias
                m = jnp.max(sc, axis=-1, keepdims=True)
                p = jnp.exp(sc - m)
                den = jnp.sum(p, axis=-1, keepdims=True)
                num = _bdot3(p.astype(BF16), v, ((2,), (1,)))
                den, m = jnp.broadcast_to(den, num.shape), jnp.broadcast_to(m, num.shape)
                for u, (st, _) in enumerate(starts):
                    num_sc[g, _ds(st, d), :] = _heads(num, u)
                    den_sc[g, _ds(st, d), :] = _heads(den, u)
                    m_sc[g, _ds(st, d), :] = _heads(m, u)
                return c

            lax.fori_loop(0, s // QB // ATTN_BLOCKS, blk, 0)

        def merge(i, c):
            rows = pl.ds(pl.multiple_of(i * rt, rt), rt)
            m0, m1, m2 = m_sc[0, rows, :], m_sc[1, rows, :], m_sc[2, rows, :]
            mall = jnp.maximum(jnp.maximum(m0, m1), m2)
            w0, w1, w2 = jnp.exp(m0 - mall), jnp.exp(m1 - mall), jnp.exp(m2 - mall)
            num = w0 * num_sc[0, rows, :] + w1 * num_sc[1, rows, :] + w2 * num_sc[2, rows, :]
            den = w0 * den_sc[0, rows, :] + w1 * den_sc[1, rows, :] + w2 * den_sc[2, rows, :]
            o_ref[0, rows, :] = num / den
            lse_ref[0, rows, :] = mall + jnp.log(den)
            return c

        lax.fori_loop(0, s // rt, merge, 0)

    col = lambda w, g: (lambda hp, b: (b, 0, (w * 3 + g) * 4 + hp))
    in_specs = [pl.BlockSpec(memory_space=pltpu.SMEM), pl.BlockSpec((3, QB, 2 * QB), lambda hp, b: (0, 0, 0))]
    in_specs += [pl.BlockSpec((1, s, LANE), col(w, g)) for w in range(3) for g in range(3)]
    out_spec = pl.BlockSpec((1, s, LANE), lambda hp, b: (b, 0, hp))
    return pl.pallas_call(
        body, name="attn_fwd", grid=(4, bsz), in_specs=in_specs, out_specs=[out_spec, out_spec],
        out_shape=[SDS((bsz, s, WIDTH), F32), SDS((bsz, s, WIDTH), F32)],
        scratch_shapes=[pltpu.VMEM((6, QB, 2 * QB), F32), pltpu.VMEM((3, s, LANE), F32), pltpu.VMEM((3, s, LANE), F32),
                        pltpu.VMEM((3, s, LANE), F32)],
        compiler_params=_params(("arbitrary", "arbitrary")))(rel_bias, bidx, *([qkv3] * 9))


def _attn_bwd(qkv3, o3, lse3, do3, rel_bias, bidx):
    bsz, s, _ = qkv3.shape
    rt = 256

    def body(tab_ref, bidx_ref, *refs):
        q_refs, k_refs, v_refs = refs[0:3], refs[3:6], refs[6:9]
        o_ref, lse_ref, do_ref, dqkv_ref, db_ref, bias_sc, delta_sc = refs[9:]
        dq_refs, dk_refs, dv_refs = ([dqkv_ref.at[w * 3 + g] for g in range(3)] for w in range(3))

        @pl.when(pl.program_id(1) == 0)
        def _():
            _fill_bias(tab_ref, bidx_ref, bias_sc, pl.program_id(0))
            db_ref[...] = jnp.zeros_like(db_ref)

        def prep(i, c):
            rows = pl.ds(pl.multiple_of(i * rt, rt), rt)
            prod = do_ref[0, rows, :] * o_ref[0, rows, :]
            d0 = jnp.sum(prod[:, :HEAD], axis=-1, keepdims=True)
            d1 = jnp.sum(prod[:, HEAD:], axis=-1, keepdims=True)
            delta_sc[rows, :] = jnp.concatenate([jnp.broadcast_to(d0, (rt, HEAD)), jnp.broadcast_to(d1, (rt, HEAD))], axis=1)
            z = jnp.zeros((rt, LANE), F32)
            for g in range(3):
                dk_refs[g][0, rows, :] = z
                dv_refs[g][0, rows, :] = z
            return c

        lax.fori_loop(0, s // rt, prep, 0)
        for g, d in enumerate(DILATIONS):
            nb = s // (QB * d)

            def blk(it, c, g=g, d=d, nb=nb):
                q, k, v, bias, starts = _attn_operands(q_refs[g], k_refs[g], v_refs[g], bias_sc, g, d, nb, it * ATTN_BLOCKS)
                dos, lses, deltas = [], [], []
                for st, _ in starts:
                    dof, lsef, delf = do_ref[0, _ds(st, d), :], lse_ref[0, _ds(st, d), :], delta_sc[_ds(st, d), :]
                    for h in range(2):
                        dos.append(dof[:, HEAD * h:HEAD * h + HEAD])
                        lses.append(lsef[:, HEAD * h:HEAD * h + 1])
                        deltas.append(delf[:, HEAD * h:HEAD * h + 1])
                do, lse, delta = _stack(dos).astype(BF16), _stack(lses), _stack(deltas)
                p = jnp.exp(_bdot3(q, k, ((2,), (2,))) * SCALE + bias - lse)
                dv = _bdot3(p.astype(BF16), do, ((1,), (1,)))
                ds = p * (_bdot3(do, v, ((2,), (2,))) - delta)
                dsb = ds.astype(BF16)
                dq = _bdot3(dsb, k, ((2,), (1,))) * SCALE
                dk = _bdot3(dsb, q, ((1,), (1,))) * SCALE
                for h in range(2):
                    db_ref[0, g * 2 + h] += sum(ds[2 * u + h] for u in range(ATTN_BLOCKS))
                for u, (st, stp) in enumerate(starts):
                    dq_refs[g][0, _ds(st, d), :] = _heads(dq, u)
                    dk_refs[g][0, _ds(stp, d), :] += _heads(dk[:, :QB], u)
                    dv_refs[g][0, _ds(stp, d), :] += _heads(dv[:, :QB], u)
                    dk_refs[g][0, _ds(st, d), :] += _heads(dk[:, QB:], u)
                    dv_refs[g][0, _ds(st, d), :] += _heads(dv[:, QB:], u)
                return c

            lax.fori_loop(0, s // QB // ATTN_BLOCKS, blk, 0)

    col = lambda w, g: (lambda hp, b: (b, 0, (w * 3 + g) * 4 + hp))
    blk_spec = pl.BlockSpec((1, s, LANE), lambda hp, b: (b, 0, hp))
    in_specs = [pl.BlockSpec(memory_space=pltpu.SMEM), pl.BlockSpec((3, QB, 2 * QB), lambda hp, b: (0, 0, 0))]
    in_specs += [pl.BlockSpec((1, s, LANE), col(w, g)) for w in range(3) for g in range(3)]
    in_specs += [blk_spec] * 3
    out_specs = [pl.BlockSpec((9, 1, s, LANE), lambda hp, b: (0, b, 0, hp)), pl.BlockSpec((1, 6, QB, 2 * QB), lambda hp, b: (hp, 0, 0, 0))]
    out_shape = [SDS((9, bsz, s, WIDTH), F32), SDS((4, 6, QB, 2 * QB), F32)]
    return pl.pallas_call(
        body, name="attn_bwd", grid=(4, bsz), in_specs=in_specs, out_specs=out_specs, out_shape=out_shape,
        scratch_shapes=[pltpu.VMEM((6, QB, 2 * QB), F32), pltpu.VMEM((s, LANE), F32)],
        compiler_params=_params(("parallel", "arbitrary")))(rel_bias, bidx, *([qkv3] * 9), o3, lse3, do3)


def _bias_grad(dbias, bidx):
    def body(db_ref, bidx_ref, o_ref):
        lane = lax.broadcasted_iota(jnp.int32, (1, LANE), 1)
        for g in range(3):
            bi = bidx_ref[g]
            for hp in range(4):
                for h in range(2):
                    mat = db_ref[hp, g * 2 + h]
                    row = jnp.zeros((1, LANE), F32)
                    for j in range(N_BUCKET):
                        part = jnp.sum(jnp.where(bi == j, mat, 0.0), axis=0, keepdims=True)
                        row = jnp.where(lane == j, jnp.sum(part, axis=1, keepdims=True), row)
                    hd = g * N_HEAD + hp * 2 + h
                    o_ref[hd:hd + 1, :] = row

    return pl.pallas_call(body, name="bias_grad", out_shape=SDS((3 * N_HEAD, LANE), F32), compiler_params=_params())(dbias, bidx)


def _pre_fn(r, k0, v, wl, al, w0, wup, a0, aup, kk_, ka_):
    u = w0 + _bdot(jnp.tanh(wl), wup)
    lw = -jnp.exp(-_softplus(-u) - 0.5)
    a = jax.nn.sigmoid(a0 + _bdot(al, aup))
    kkraw = k0 * kk_
    k = k0 * (1.0 + (a - 1.0) * ka_)
    return r, lw, k, v, kkraw, a


PRE_SPLIT = (0, WIDTH, 2 * WIDTH, 3 * WIDTH, 3 * WIDTH + LORA, 3 * WIDTH + 2 * LORA)


def _pre_pieces(prs):
    return [prs[:, a:b] for a, b in zip(PRE_SPLIT[:-1], PRE_SPLIT[1:])]


PRE_TT = 512


def _shifted(pr_ref, edge_ref, first, back):
    pr = pr_ref[0]
    tt = pr.shape[0]
    row = lax.broadcasted_iota(jnp.int32, (tt, 1), 0)
    if back:
        edge = jnp.where(first, 0.0, edge_ref[0, 7:8, :])
        return jnp.where(row == 0, edge, pltpu.roll(pr, 1, axis=0))
    edge = jnp.where(first, 0.0, edge_ref[0, 0:1, :])
    return jnp.where(row == tt - 1, edge, pltpu.roll(pr, tt - 1, axis=0))


def _rwkv_pre(pr3, mix, w0, wup, a0, aup, kk_, ka_):
    bsz, s, _ = pr3.shape
    tt = PRE_TT

    def body(pr_ref, edge_ref, mix_ref, w0_ref, wup_ref, a0_ref, aup_ref, kk_ref, ka_ref, *outs):
        pr = pr_ref[0]
        prev = _shifted(pr_ref, edge_ref, pl.program_id(1) == 0, True)
        prs = pr + (prev - pr) * mix_ref[...]
        vals = _pre_fn(*_pre_pieces(prs), w0_ref[...], wup_ref[...].astype(F32), a0_ref[...], aup_ref[...].astype(F32), kk_ref[...],
                       ka_ref[...])
        for o, val in zip(outs, vals):
            o[0] = val

    vec = lambda n: pl.BlockSpec((1, n), lambda b, i: (0, 0))
    mat = pl.BlockSpec((LORA, WIDTH), lambda b, i: (0, 0))
    in_specs = [pl.BlockSpec((1, tt, PR_COLS), lambda b, i: (b, i, 0)),
                pl.BlockSpec((1, 8, PR_COLS), lambda b, i: (b, jnp.maximum(i * (tt // 8) - 1, 0), 0)),
                vec(PR_COLS), vec(WIDTH), mat, vec(WIDTH), mat, vec(WIDTH), vec(WIDTH)]
    out_spec = pl.BlockSpec((1, tt, WIDTH), lambda b, i: (b, i, 0))
    return pl.pallas_call(
        body, name="rwkv_pre", grid=(bsz, s // tt), in_specs=in_specs, out_specs=[out_spec] * 6,
        out_shape=[SDS((bsz, s, WIDTH), F32)] * 6, compiler_params=_params(("parallel", "parallel")))(
            pr3, pr3, mix, w0, wup, a0, aup, kk_, ka_)


def _rwkv_pre_bwd(pr3, cots, mix, w0, wup, a0, aup, kk_, ka_):
    bsz, s, _ = pr3.shape
    tt = PRE_TT

    def body(pr_ref, edge_ref, c0, c1, c2, c3, c4, c5, mix_ref, w0_ref, wup_ref, a0_ref, aup_ref, kk_ref, ka_ref,
             dprs_ref, dmix_ref, dw0_ref, dwup_ref, da0_ref, daup_ref, dkk_ref, dka_ref):
        pr = pr_ref[0]
        prev = _shifted(pr_ref, edge_ref, pl.program_id(1) == 0, True)
        prs = pr + (prev - pr) * mix_ref[...]
        _, vjp = jax.vjp(_pre_fn, *_pre_pieces(prs), w0_ref[...], wup_ref[...].astype(F32), a0_ref[...], aup_ref[...].astype(F32),
                         kk_ref[...], ka_ref[...])
        grads = vjp(tuple(c[0] for c in (c0, c1, c2, c3, c4, c5)))
        for piece, a, b in zip(grads[:5], PRE_SPLIT[:-1], PRE_SPLIT[1:]):
            dprs_ref[0, :, a:b] = piece
        dw0, dwup, da0, daup, dkk, dka = grads[5:]
        dprs = dprs_ref[0]
        grads = (jnp.sum(dprs * (prev - pr), axis=0, keepdims=True), dw0, dwup, da0, daup, dkk, dka)
        refs = (dmix_ref, dw0_ref, dwup_ref, da0_ref, daup_ref, dkk_ref, dka_ref)
        first = jnp.logical_and(pl.program_id(0) == 0, pl.program_id(1) == 0)

        @pl.when(first)
        def _():
            for r_, g_ in zip(refs, grads):
                r_[...] = g_

        @pl.when(jnp.logical_not(first))
        def _():
            for r_, g_ in zip(refs, grads):
                r_[...] += g_

    vec = lambda n: pl.BlockSpec((1, n), lambda b, i: (0, 0))
    mat = pl.BlockSpec((LORA, WIDTH), lambda b, i: (0, 0))
    tile = pl.BlockSpec((1, tt, WIDTH), lambda b, i: (b, i, 0))
    in_specs = [pl.BlockSpec((1, tt, PR_COLS), lambda b, i: (b, i, 0)),
                pl.BlockSpec((1, 8, PR_COLS), lambda b, i: (b, jnp.maximum(i * (tt // 8) - 1, 0), 0))]
    in_specs += [tile] * 6 + [vec(PR_COLS), vec(WIDTH), mat, vec(WIDTH), mat, vec(WIDTH), vec(WIDTH)]
    out_specs = [pl.BlockSpec((1, tt, PR_COLS), lambda b, i: (b, i, 0)), vec(PR_COLS), vec(WIDTH), mat, vec(WIDTH), mat,
                 vec(WIDTH), vec(WIDTH)]
    out_shape = [SDS((bsz, s, PR_COLS), F32), SDS((1, PR_COLS), F32), SDS((1, WIDTH), F32), SDS((LORA, WIDTH), F32),
                 SDS((1, WIDTH), F32), SDS((LORA, WIDTH), F32), SDS((1, WIDTH), F32), SDS((1, WIDTH), F32)]
    return pl.pallas_call(
        body, name="rwkv_pre_bwd", grid=(bsz, s // tt), in_specs=in_specs, out_specs=out_specs, out_shape=out_shape,
        compiler_params=_params(("arbitrary", "arbitrary")))(pr3, pr3, *cots, mix, w0, wup, a0, aup, kk_, ka_)


def _shift_bwd(dprs3, mix):
    bsz, s, _ = dprs3.shape
    tt = PRE_TT
    nt = s // tt

    def body(d_ref, edge_ref, mix_ref, o_ref):
        nxt = _shifted(d_ref, edge_ref, pl.program_id(1) == nt - 1, False)
        m = mix_ref[...]
        o_ref[0] = d_ref[0] * (1.0 - m) + nxt * m

    in_specs = [pl.BlockSpec((1, tt, PR_COLS), lambda b, i: (b, i, 0)),
                pl.BlockSpec((1, 8, PR_COLS), lambda b, i: (b, jnp.minimum((i + 1) * (tt // 8), s // 8 - 1), 0)),
                pl.BlockSpec((1, PR_COLS), lambda b, i: (0, 0))]
    return pl.pallas_call(
        body, name="shift_bwd", grid=(bsz, nt), in_specs=in_specs, out_specs=pl.BlockSpec((1, tt, PR_COLS), lambda b, i: (b, i, 0)),
        out_shape=SDS((bsz, s, PR_COLS), F32), compiler_params=_params(("parallel", "parallel")))(dprs3, dprs3, mix)


_NN, _NT, _TN = ((2,), (1,)), ((2,), (2,)), ((1,), (1,))


def _dot3(a, b, dims, precision=HI3):
    return lax.dot_general(a, b, (dims, ((0,), (0,))), precision=precision, preferred_element_type=F32)


def _dot3_bf16(a, b, dims):
    return lax.dot_general(a.astype(BF16), b.astype(BF16), (dims, ((0,), (0,))), preferred_element_type=F32)


class _Dots:
    def __init__(self, fwd):
        def make(dims, da_rule, db_rule):
            @jax.custom_vjp
            def f(a, b):
                return fwd(a, b, dims)

            f.defvjp(lambda a, b: (f(a, b), (a, b)), lambda res, g: (da_rule(*res, g), db_rule(*res, g)))
            return f

        one = _dot3_bf16
        self.mm = make(_NN, lambda a, b, g: one(g, b, _NT), lambda a, b, g: one(a, g, _TN))
        self.mm_nt = make(_NT, lambda a, b, g: one(g, b, _NN), lambda a, b, g: one(g, a, _TN))
        self.mm_tn = make(_TN, lambda a, b, g: one(b, g, _NT), lambda a, b, g: one(a, g, _NN))

        def powers(aab):
            ps = [aab]
            while 2 ** len(ps) < aab.shape[1]:
                ps.append(fwd(ps[-1], ps[-1], _NN))
            return ps

        def apply(ps, z, dims):
            for p in ps:
                z = z + fwd(p, z, dims)
            return z

        @jax.custom_vjp
        def solve(aab, z):
            return apply(powers(aab), z, _NN)

        def solve_fwd(aab, z):
            ps = powers(aab)
            x = apply(ps, z, _NN)
            return x, (ps, x)

        def solve_bwd(res, g):
            ps, x = res
            dz = apply(ps, g, _TN)
            return fwd(dz, x, _NT), dz

        solve.defvjp(solve_fwd, solve_bwd)
        self.solve = solve


_ACCURATE = _Dots(_dot3)
_ONE_PASS = _Dots(_dot3_bf16)
_bmm, _bmm_tn = _ACCURATE.mm, _ACCURATE.mm_tn


def _chunk_fn(s0t, r, lw, k, v, kkraw, a, rk, lnw, lnb, first=False, d=_ACCURATE):
    c = r.shape[1]
    at, rt, btc, ktc, gc, aab, arb, xv, arkv, ain, bin_ = _chunk_core(r, lw, k, v, kkraw, a, d)
    rs = d.mm(jnp.concatenate([at, rt], axis=1), s0t)
    u = d.solve(aab, rs[:, :c] + xv)
    y = rs[:, c:] + d.mm(arb, u) + arkv
    if first:
        y = _with_early_rows(y, r, lw, k, v, ain, bin_)
    gcol = jnp.sum(_diag(gc), axis=2, keepdims=True)
    sct = gcol * s0t + d.mm_tn(jnp.concatenate([btc, ktc], axis=1), jnp.concatenate([u, v], axis=1))
    return _post(y, r, k, v, rk, lnw, lnb), sct


def _diag(gc):
    return jnp.where(_masks(HEAD)[2], gc, 0.0)


def _with_early_rows(y, r, lw, k, v, ain, bin_):
    early = _stack([_early_rows(r[h], lw[h], k[h], v[h], ain[h], bin_[h]) for h in range(2)])
    return jnp.concatenate([jnp.concatenate([early, y[:2, EARLY:]], axis=1), y[2:]], axis=0)


def _early_rows(r, lw, k, v, ain, bin_):
    wc, bc, kc = jnp.transpose(jnp.exp(lw)), jnp.transpose(bin_), jnp.transpose(k)
    st = jnp.zeros((HEAD, HEAD), F32)
    rows = []
    for t in range(EARLY):
        sa = _bdot(ain[t:t + 1], st)
        st = st * wc[:, t:t + 1] + bc[:, t:t + 1] * sa + kc[:, t:t + 1] * v[t:t + 1]
        rows.append(_bdot(r[t:t + 1], st))
    return jnp.concatenate(rows, axis=0)


def _chunk_rows(c):
    return pl.ds(c * CHUNK, CHUNK) if isinstance(c, int) else pl.ds(pl.multiple_of(c * CHUNK, CHUNK), CHUNK)


def _stack(xs):
    return jnp.concatenate([x[None] for x in xs], axis=0)


def _pairs(ref, chunks):
    tiles = [ref[0, _chunk_rows(c), :] for c in chunks]
    return _stack([t[:, HEAD * h:HEAD * h + HEAD] for t in tiles for h in range(2)])


def _unpair(vals, j):
    return jnp.concatenate([vals[2 * j], vals[2 * j + 1]], axis=1)


def _masks(c):
    ii = lax.broadcasted_iota(jnp.int32, (c, c), 0)
    jj = lax.broadcasted_iota(jnp.int32, (c, c), 1)
    return ii > jj, ii >= jj, ii == jj


def _chunk_core(r, lw, k, v, kkraw, a, d=_ACCURATE):
    g_, c = r.shape[0], r.shape[1]
    nrm = jnp.sqrt(jnp.sum(kkraw * kkraw, axis=-1, keepdims=True))
    kkn = kkraw / jnp.maximum(nrm, 1e-12)
    ain, bin_ = -kkn, kkn * a
    strict, incl, _ = _masks(c)
    lg = lax.dot_general(jnp.broadcast_to(incl.astype(F32), (g_, c, c)), lw, (((2,), (1,)), ((0,), (0,))), precision=HI,
                         preferred_element_type=F32)
    g, gp, gi = jnp.exp(lg), jnp.exp(lg - lw), jnp.exp(-lg)
    at, rt, bt, kt = ain * gp, r * g, bin_ * gi, k * gi
    aa = d.mm_nt(jnp.concatenate([at, rt], axis=1), jnp.concatenate([bt, kt], axis=1))
    aab = jnp.where(strict, aa[:, :c, :c], 0.0)
    aak = jnp.where(strict, aa[:, :c, c:], 0.0)
    arb = jnp.where(incl, aa[:, c:, :c], 0.0)
    ark = jnp.where(incl, aa[:, c:, c:], 0.0)
    akv = d.mm(jnp.concatenate([aak, ark], axis=1), v)
    gc = g[:, c - 1:c, :]
    return at, rt, bt * gc, kt * gc, gc, aab, arb, akv[:, :c], akv[:, c:], ain, bin_


def _post(y, r, k, v, rk, lnw, lnb):
    mu = jnp.mean(y, axis=-1, keepdims=True)
    var = jnp.mean(jnp.square(y - mu), axis=-1, keepdims=True)
    yn = (y - mu) * lax.rsqrt(var + GN_EPS) * lnw + lnb
    return yn + jnp.sum(r * k * rk, axis=-1, keepdims=True) * v


def _chunk_consts(r, lw, k, v, kkraw, a, first=False):
    at, rt, btc, ktc, gc, aab, arb, xv, arkv, ain, bin_ = _chunk_core(r, lw, k, v, kkraw, a)
    z = _ACCURATE.solve(aab, jnp.concatenate([at, xv], axis=2))
    ryv = jnp.concatenate([rt, arkv], axis=2) + _bmm(arb, z)
    if first:
        ryv = jnp.concatenate([ryv[:, :, :HEAD], _with_early_rows(ryv[:, :, HEAD:], r, lw, k, v, ain, bin_)], axis=2)
    mkv = _bmm_tn(btc, z) + jnp.concatenate([_diag(gc), _bmm_tn(ktc, v)], axis=2)
    return mkv, ryv


def _rwkv_scan(ins, rk, lnw, lnb):
    bsz, s, _ = ins[0].shape
    nch = s // CHUNK

    def consts_body(r_ref, lw_ref, k_ref, v_ref, kk_ref, a_ref, mkv_ref, ry_ref, yv_ref):
        def group(i, carry):
            chunks = [i * CHUNK_GROUP + j for j in range(CHUNK_GROUP)]
            mkv, ryv = _chunk_consts(*[_pairs(ref, chunks) for ref in (r_ref, lw_ref, k_ref, v_ref, kk_ref, a_ref)],
                                     first=isinstance(i, int) and i == 0)
            for j, c in enumerate(chunks):
                for h in range(2):
                    mkv_ref[0, 0, c, h] = mkv[2 * j + h]
                ry_ref[0, _chunk_rows(c), :] = jnp.concatenate([ryv[2 * j][:, :HEAD], ryv[2 * j + 1][:, :HEAD]], axis=1)
                yv_ref[0, _chunk_rows(c), :] = jnp.concatenate([ryv[2 * j][:, HEAD:], ryv[2 * j + 1][:, HEAD:]], axis=1)
            return carry

        group(0, 0)
        lax.fori_loop(1, nch // CHUNK_GROUP, group, 0)

    tile = pl.BlockSpec((1, s, LANE), lambda b, hp: (b, 0, hp))
    vec = pl.BlockSpec((1, LANE), lambda b, hp: (0, hp))
    mkv_spec = pl.BlockSpec((1, 1, nch, 2, HEAD, LANE), lambda b, hp: (b, hp, 0, 0, 0, 0))
    st_spec = pl.BlockSpec((1, 1, nch, 2, HEAD, HEAD), lambda b, hp: (b, hp, 0, 0, 0, 0))
    mkv, ry, yv = pl.pallas_call(
        consts_body, name="rwkv_consts", grid=(bsz, 4), in_specs=[tile] * 6, out_specs=[mkv_spec, tile, tile],
        out_shape=[SDS((bsz, 4, nch, 2, HEAD, LANE), F32), SDS((bsz, s, WIDTH), F32), SDS((bsz, s, WIDTH), F32)],
        compiler_params=_params(("parallel", "parallel")))(*ins)

    states = _chunk_recurrence(mkv, None, "rwkv_states")

    def out_body(ry_ref, yv_ref, r_ref, k_ref, v_ref, st_ref, rk_ref, lnw_ref, lnb_ref, o_ref):
        y, r, k, v, rk_, lnw_, lnb_ = _scan_rows(ry_ref, yv_ref, r_ref, k_ref, v_ref, st_ref, rk_ref, lnw_ref, lnb_ref)
        o = _post(y, r, k, v, rk_, lnw_, lnb_)
        for j in range(CHUNK_GROUP):
            o_ref[0, _chunk_rows(j), :] = _unpair(o, j)

    o = pl.pallas_call(
        out_body, name="rwkv_out", grid=(bsz, 4, nch // CHUNK_GROUP), in_specs=_group_specs(5), out_specs=_group_specs(1)[0],
        out_shape=SDS((bsz, s, WIDTH), F32),
        compiler_params=_params(("parallel", "parallel", "parallel")))(ry, yv, ins[0], ins[2], ins[3], states, rk, lnw, lnb)
    return o, states, (mkv, ry, yv)


def _group_specs(n_tiles):
    tile = pl.BlockSpec((1, CHUNK_GROUP * CHUNK, LANE), lambda b, hp, t: (b, t, hp))
    if n_tiles == 1:
        return [tile]
    st = pl.BlockSpec((1, 1, CHUNK_GROUP, 2, HEAD, HEAD), lambda b, hp, t: (b, hp, t, 0, 0, 0))
    vec = pl.BlockSpec((1, LANE), lambda b, hp, t: (0, hp))
    return [tile] * n_tiles + [st] + [vec] * 3


def _scan_rows(ry_ref, yv_ref, r_ref, k_ref, v_ref, st_ref, rk_ref, lnw_ref, lnb_ref):
    chunks = list(range(CHUNK_GROUP))
    ry, yv, r, k, v = (_pairs(ref, chunks) for ref in (ry_ref, yv_ref, r_ref, k_ref, v_ref))
    st = _stack([st_ref[0, 0, c, h] for c in chunks for h in range(2)])
    vecs = [_stack([ref[:, HEAD * h:HEAD * h + HEAD] for _ in chunks for h in range(2)]) for ref in (rk_ref, lnw_ref, lnb_ref)]
    return (_bmm(ry, st) + yv, r, k, v, *vecs)


def _chunk_recurrence(mkv, q, name):
    bsz, _, nch = mkv.shape[:3]
    pairs = [(hp, h) for hp in range(4) for h in range(2)]

    def body(*refs):
        mkv_ref, out_ref, acc = refs[0], refs[-2], refs[-1]
        acc[...] = jnp.zeros_like(acc)

        def step(i, carry):
            c = i if q is None else nch - 1 - i
            cur = acc[...]
            for j, (hp, h) in enumerate(pairs):
                out_ref[0, hp, c, h] = cur[j]
            m = _stack([mkv_ref[0, hp, c, h] for hp, h in pairs])
            if q is None:
                acc[...] = _bmm(m[:, :, :HEAD], cur) + m[:, :, HEAD:]
            else:
                acc[...] = _bmm_tn(m[:, :, :HEAD], cur) + _stack([refs[1][0, hp, c, h] for hp, h in pairs])
            return carry

        lax.fori_loop(0, nch, step, 0)

    spec = lambda w: pl.BlockSpec((1, 4, nch, 2, HEAD, w), lambda b: (b, 0, 0, 0, 0, 0))
    return pl.pallas_call(
        body, name=name, grid=(bsz,), in_specs=[spec(LANE)] + ([] if q is None else [spec(HEAD)]), out_specs=spec(HEAD),
        out_shape=SDS((bsz, 4, nch, 2, HEAD, HEAD), F32), scratch_shapes=[pltpu.VMEM((8, HEAD, HEAD), F32)],
        compiler_params=_params(("parallel",)))(*([mkv] if q is None else [mkv, q]))


def _rwkv_scan_bwd(ins, states, consts, do3, rk, lnw, lnb):
    bsz, s, _ = ins[0].shape
    nch = s // CHUNK

    mkv, ry, yv = consts

    def q_body(do_ref, ry_ref, yv_ref, r_ref, k_ref, v_ref, st_ref, rk_ref, lnw_ref, lnb_ref, q_ref):
        y, r, k, v, rk_, lnw_, lnb_ = _scan_rows(ry_ref, yv_ref, r_ref, k_ref, v_ref, st_ref, rk_ref, lnw_ref, lnb_ref)
        _, vjp = jax.vjp(lambda y_: _post(y_, r, k, v, rk_, lnw_, lnb_), y)
        (dy,) = vjp(_pairs(do_ref, list(range(CHUNK_GROUP))))
        q = _bmm_tn(_pairs(ry_ref, list(range(CHUNK_GROUP))), dy)
        for j in range(CHUNK_GROUP):
            for h in range(2):
                q_ref[0, 0, j, h] = q[2 * j + h]

    specs = _group_specs(6)
    q = pl.pallas_call(
        q_body, name="rwkv_q", grid=(bsz, 4, nch // CHUNK_GROUP), in_specs=specs, out_specs=specs[6],
        out_shape=SDS((bsz, 4, nch, 2, HEAD, HEAD), F32),
        compiler_params=_params(("parallel", "parallel", "parallel")))(do3, ry, yv, ins[0], ins[2], ins[3], states, rk, lnw, lnb)

    dstates = _chunk_recurrence(mkv, q, "rwkv_dstates")

    def body(r_ref, lw_ref, k_ref, v_ref, kk_ref, a_ref, st_ref, dst_ref, do_ref, rk_ref, lnw_ref, lnb_ref,
             dr_ref, dlw_ref, dk_ref, dv_ref, dkk_ref, da_ref, drk_ref, dlnw_ref, dlnb_ref):
        chunks = list(range(BWD_GROUP))
        par_refs = (drk_ref, dlnw_ref, dlnb_ref)

        @pl.when(jnp.logical_and(pl.program_id(1) == 0, pl.program_id(2) == 0))
        def _():
            for ref in par_refs:
                ref[...] = jnp.zeros_like(ref)

        def group(first):
            per_pair = lambda ref: _stack([ref[0, 0, c, h] for c in chunks for h in range(2)])
            vecs = [_stack([ref[:, HEAD * h:HEAD * h + HEAD] for _ in chunks for h in range(2)]) for ref in (rk_ref, lnw_ref, lnb_ref)]
            _, vjp = jax.vjp(functools.partial(_chunk_fn, first=first, d=_ONE_PASS), per_pair(st_ref),
                             *[_pairs(ref, chunks) for ref in (r_ref, lw_ref, k_ref, v_ref, kk_ref, a_ref)], *vecs)
            grads = vjp((_pairs(do_ref, chunks), per_pair(dst_ref)))
            for ref, cot in zip((dr_ref, dlw_ref, dk_ref, dv_ref, dkk_ref, da_ref), grads[1:7]):
                for j, c in enumerate(chunks):
                    ref[0, _chunk_rows(c), :] = _unpair(cot, j)
            for ref, g_ in zip(par_refs, grads[7:10]):
                ref[...] += jnp.concatenate([sum(g_[2 * j + h] for j in range(BWD_GROUP)) for h in range(2)], axis=1)

        pl.when(pl.program_id(2) == 0)(functools.partial(group, True))
        pl.when(pl.program_id(2) != 0)(functools.partial(group, False))

    tt = BWD_GROUP * CHUNK
    tile = pl.BlockSpec((1, tt, LANE), lambda hp, b, t: (b, t, hp))
    vec = pl.BlockSpec((1, LANE), lambda hp, b, t: (0, hp))
    st_spec = pl.BlockSpec((1, 1, BWD_GROUP, 2, HEAD, HEAD), lambda hp, b, t: (b, hp, t, 0, 0, 0))
    outs = pl.pallas_call(
        body, name="rwkv_scan_bwd", grid=(4, bsz, s // tt), in_specs=[tile] * 6 + [st_spec, st_spec, tile] + [vec] * 3,
        out_specs=[tile] * 6 + [vec] * 3,
        out_shape=[SDS((bsz, s, WIDTH), F32)] * 6 + [SDS((1, WIDTH), F32)] * 3,
        compiler_params=_params(("parallel", "arbitrary", "arbitrary")))(*ins, states, dstates, do3, rk, lnw, lnb)
    return outs[:6], outs[6:]


def _head(o_attn, o_rwkv, z_attn, z_rwkv, gm, x2, tgt, wua, wur, wout, g2):
    n = x2.shape[0]
    tm = 256
    nt = n // tm
    d = D_MODEL

    def body(oa_ref, or_ref, za_ref, zr_ref, gm_ref, x_ref, t_ref, wua_ref, wur_ref, wout_ref, g2_ref,
             dxo_ref, doa_ref, dor_ref, dza_ref, dzr_ref, dgm_ref, dwua_ref, dwur_ref, dwout_ref, dg2_ref, loss_ref, lacc):
        i = pl.program_id(0)
        oa, orw, za, zr = oa_ref[...], or_ref[...], za_ref[...], zr_ref[...]
        ga, gb = gm_ref[:, 0:d], gm_ref[:, d:2 * d]
        am = (oa * _silu(za)).astype(BF16)
        bm = (orw * _silu(zr)).astype(BF16)
        ya, yb = _dot(am, wua_ref[...]), _dot(bm, wur_ref[...])
        sa, sb = jax.nn.sigmoid(ga), jax.nn.sigmoid(gb)
        merged = (sa * ya + sb * yb).astype(BF16)
        out = _dot(merged, wout_ref[...])
        rs = lax.rsqrt(jnp.mean(out * out, axis=-1, keepdims=True) + RMS_EPS)
        g2 = g2_ref[...]
        err = x_ref[...] + out * rs * g2 - t_ref[...]
        lpart = jnp.sum(err * err, axis=0, keepdims=True)
        dxo = err * (1.0 / d)
        dxo_ref[...] = dxo
        dg2 = jnp.sum(dxo * out * rs, axis=0, keepdims=True)
        gd = dxo * g2
        dout = (rs * (gd - out * (rs * rs) * jnp.mean(gd * out, axis=-1, keepdims=True))).astype(BF16)
        dmerged = _dot_nt(dout, wout_ref[...])
        dwout = _dot_tn(merged, dout)
        dya, dyb = (dmerged * sa).astype(BF16), (dmerged * sb).astype(BF16)
        dgm_ref[:, 0:d] = dmerged * ya * sa * (1.0 - sa)
        dgm_ref[:, d:2 * d] = dmerged * yb * sb * (1.0 - sb)
        dam, dbm = _dot_nt(dya, wua_ref[...]), _dot_nt(dyb, wur_ref[...])
        dwua, dwur = _dot_tn(am, dya), _dot_tn(bm, dyb)
        doa_ref[...] = dam * _silu(za)
        dza_ref[...] = dam * oa * _dsilu(za)
        dor_ref[...] = dbm * _silu(zr)
        dzr_ref[...] = dbm * orw * _dsilu(zr)

        @pl.when(i == 0)
        def _():
            dwua_ref[...], dwur_ref[...], dwout_ref[...], dg2_ref[...], lacc[...] = dwua, dwur, dwout, dg2, lpart

        @pl.when(i != 0)
        def _():
            dwua_ref[...] += dwua
            dwur_ref[...] += dwur
            dwout_ref[...] += dwout
            dg2_ref[...] += dg2
            lacc[...] += lpart

        @pl.when(i == nt - 1)
        def _():
            loss_ref[...] = jnp.sum(lacc[...], axis=1, keepdims=True) * (0.5 / d)

    t512 = pl.BlockSpec((tm, WIDTH), lambda i: (i, 0))
    t1k = pl.BlockSpec((tm, d), lambda i: (i, 0))
    t2k = pl.BlockSpec((tm, 2 * d), lambda i: (i, 0))
    full = lambda r, c: pl.BlockSpec((r, c), lambda i: (0, 0))
    return pl.pallas_call(
        body, name="head_fwd_bwd", grid=(nt,),
        in_specs=[t512, t512, t512, t512, t2k, t1k, t1k, full(WIDTH, d), full(WIDTH, d), full(d, d), full(1, d)],
        out_specs=[t1k, t512, t512, t512, t512, t2k, full(WIDTH, d), full(WIDTH, d), full(d, d), full(1, d), full(1, 1)],
        out_shape=[SDS((n, d), F32)] + [SDS((n, WIDTH), F32)] * 4 + [SDS((n, 2 * d), F32), SDS((WIDTH, d), F32), SDS((WIDTH, d), F32),
                                                                    SDS((d, d), F32), SDS((1, d), F32), SDS((1, 1), F32)],
        scratch_shapes=[pltpu.VMEM((1, d), F32)],
        compiler_params=_params(("arbitrary",)))(o_attn, o_rwkv, z_attn, z_rwkv, gm, x2, tgt, wua, wur, wout, g2)


def _prenorm_bwd(dh, x2, rs, g1, dxo):
    n, d = x2.shape
    tm = 1024

    def body(dh_ref, x_ref, rs_ref, g_ref, dxo_ref, gx_ref, dg_ref):
        x, r = x_ref[...], rs_ref[...]
        gd = dh_ref[...] * g_ref[...]
        gx_ref[...] = dxo_ref[...] + r * (gd - x * (r * r) * jnp.mean(gd * x, axis=-1, keepdims=True))
        dg = jnp.sum(dh_ref[...] * x * r, axis=0, keepdims=True)

        @pl.when(pl.program_id(0) == 0)
        def _():
            dg_ref[...] = dg

        @pl.when(pl.program_id(0) != 0)
        def _():
            dg_ref[...] += dg

    t = pl.BlockSpec((tm, d), lambda i: (i, 0))
    return pl.pallas_call(
        body, name="prenorm_bwd", grid=(n // tm,),
        in_specs=[t, t, pl.BlockSpec((tm, 1), lambda i: (i, 0)), pl.BlockSpec((1, d), lambda i: (0, 0)), t],
        out_specs=[t, pl.BlockSpec((1, d), lambda i: (0, 0))], out_shape=[SDS((n, d), F32), SDS((1, d), F32)],
        compiler_params=_params(("arbitrary",)))(dh, x2, rs, g1, dxo)


def _mesh_pos():
    x, y, c = lax.axis_index("x"), lax.axis_index("y"), lax.axis_index("c")
    return 4 * x + 2 * y + c


def _coords(idx):
    return (idx // 4, (idx // 2) % 2, idx % 2)


def _exchange(srcs, to_all, name):
    n = len(srcs)

    def body(*refs):
        src_refs, dst_refs = refs[:n], refs[n:2 * n]
        send_sems, recv_sems, local_sems = refs[2 * n:]
        me = _mesh_pos()

        def piece(i, j):
            return src_refs[i] if to_all[i] else src_refs[i].at[j]

        def remote(i, off, peer, block, slot):
            return pltpu.make_async_remote_copy(src_ref=piece(i, block), dst_ref=dst_refs[i].at[slot],
                                                send_sem=send_sems.at[i, off - 1], recv_sem=recv_sems.at[i, off - 1],
                                                device_id=_coords(peer), device_id_type=MESH)

        local = [pltpu.make_async_copy(piece(i, me), dst_refs[i].at[me], local_sems.at[i]) for i in range(n)]
        for cp in local:
            cp.start()
        sends = []
        for off in range(1, N_DEV):
            to = (me + off) % N_DEV
            for i in range(n):
                sends.append(remote(i, off, to, to, me))
                sends[-1].start()
        for off in range(1, N_DEV):
            frm = (me + N_DEV - off) % N_DEV
            for i in range(n):
                remote(i, off, frm, me, frm).wait_recv()
        for cp in sends:
            cp.wait_send()
        for cp in local:
            cp.wait()

    outs = pl.pallas_call(
        body, name=name, in_specs=[pl.BlockSpec(memory_space=pltpu.HBM)] * n, out_specs=[pl.BlockSpec(memory_space=pltpu.HBM)] * n,
        out_shape=[SDS((N_DEV,) + s.shape[-2:], s.dtype) for s in srcs],
        scratch_shapes=[pltpu.SemaphoreType.DMA((n, N_DEV - 1)), pltpu.SemaphoreType.DMA((n, N_DEV - 1)), pltpu.SemaphoreType.DMA((n,))],
        compiler_params=pltpu.CompilerParams())(*srcs)
    return outs


_HBM = pl.BlockSpec(memory_space=pltpu.HBM)
_SEM = pl.BlockSpec(memory_space=pltpu.SEMAPHORE)
_EFFECT = pltpu.SideEffectType.DATAFLOW_SIDE_EFFECTING


def _send_start(src):
    def body(src_ref, land_ref, send_sems, recv_sems, src_thru, land_thru, token):
        me = _mesh_pos()
        for off in range(1, N_DEV):
            to = (me + off) % N_DEV
            pltpu.make_async_remote_copy(src_ref=src_ref.at[to], dst_ref=land_ref.at[me], send_sem=send_sems.at[off - 1],
                                         recv_sem=recv_sems.at[off - 1], device_id=_coords(to), device_id_type=MESH).start()
        token[...] = jnp.zeros_like(token)

    hbm = pltpu.HBM(src.shape, src.dtype)
    return pl.pallas_call(
        body, name="grads_start",
        out_shape=(pltpu.SemaphoreType.DMA((N_DEV - 1,)), pltpu.SemaphoreType.DMA((N_DEV - 1,)), hbm, hbm, SDS((8, LANE), BF16)),
        in_specs=(_HBM, _HBM), out_specs=(_SEM, _SEM, _HBM, _HBM, pl.BlockSpec(memory_space=pltpu.VMEM)),
        input_output_aliases={0: 2, 1: 3}, compiler_params=pltpu.CompilerParams(has_side_effects=_EFFECT),
    )(pltpu.with_memory_space_constraint(src, pltpu.HBM), pltpu.with_memory_space_constraint(jnp.zeros(src.shape, src.dtype), pltpu.HBM))


def _send_wait(send_sems, recv_sems, src_thru, land_thru, after):
    def body(src_ref, land_ref, send_sems, recv_sems, after_ref, src_dead, got_ref):
        me = _mesh_pos()
        for off in range(1, N_DEV):
            to, frm = (me + off) % N_DEV, (me + N_DEV - off) % N_DEV
            pltpu.make_async_remote_copy(src_ref=src_ref.at[to], dst_ref=land_ref.at[me], send_sem=send_sems.at[off - 1],
                                         recv_sem=recv_sems.at[off - 1], device_id=_coords(to), device_id_type=MESH).wait_send()
            pltpu.make_async_remote_copy(src_ref=src_ref.at[me], dst_ref=land_ref.at[frm], send_sem=send_sems.at[off - 1],
                                         recv_sem=recv_sems.at[off - 1], device_id=_coords(frm), device_id_type=MESH).wait_recv()

    hbm = pltpu.HBM(src_thru.shape, src_thru.dtype)
    return pl.pallas_call(
        body, name="grads_wait", out_shape=(hbm, hbm), in_specs=(_HBM, _HBM, _SEM, _SEM, pl.BlockSpec(memory_space=pl.ANY)),
        out_specs=(_HBM, _HBM), input_output_aliases={0: 0, 1: 1}, compiler_params=pltpu.CompilerParams(has_side_effects=_EFFECT),
    )(src_thru, land_thru, send_sems, recv_sems, after)[1]


def _gather(srcs, name):
    n = len(srcs)

    def body(*refs):
        src_refs, dst_refs = refs[:n], refs[n:2 * n]
        send_sems, recv_sems, local_sems = refs[2 * n:]
        x, y, c = lax.axis_index("x"), lax.axis_index("y"), lax.axis_index("c")
        me, sibling = (x, y, c), (x, y, 1 - c)
        chips = [(1 - x, y), (x, 1 - y), (1 - x, 1 - y)]

        def slot(i, dev):
            return dst_refs[i].at[4 * dev[0] + 2 * dev[1] + dev[2]]

        def copy(i, k, block, to, own=False):
            return pltpu.make_async_remote_copy(src_ref=src_refs[i] if own else slot(i, block), dst_ref=slot(i, block),
                                                send_sem=send_sems.at[i, k], recv_sem=recv_sems.at[i, k],
                                                device_id=to, device_id_type=MESH)

        local = [pltpu.make_async_copy(src_refs[i], slot(i, me), local_sems.at[i]) for i in range(n)]
        for cp in local:
            cp.start()
        sends = []
        for i in range(n):
            sends.append(copy(i, 0, me, sibling, own=True))
            sends += [copy(i, 1 + j, me, (*chip, c), own=True) for j, chip in enumerate(chips)]
        for cp in sends:
            cp.start()
        for j, chip in enumerate(chips):
            for i in range(n):
                copy(i, 1 + j, (*chip, c), me).wait_recv()
                sends.append(copy(i, 4 + j, (*chip, c), sibling))
                sends[-1].start()
        for i in range(n):
            copy(i, 0, sibling, me).wait_recv()
            for j, chip in enumerate(chips):
                copy(i, 4 + j, (*chip, 1 - c), me).wait_recv()
        for cp in sends:
            cp.wait_send()
        for cp in local:
            cp.wait()

    return pl.pallas_call(
        body, name=name, in_specs=[pl.BlockSpec(memory_space=pltpu.HBM)] * n, out_specs=[pl.BlockSpec(memory_space=pltpu.HBM)] * n,
        out_shape=[SDS((N_DEV,) + s.shape, s.dtype) for s in srcs],
        scratch_shapes=[pltpu.SemaphoreType.DMA((n, N_DEV - 1)), pltpu.SemaphoreType.DMA((n, N_DEV - 1)), pltpu.SemaphoreType.DMA((n,))],
        compiler_params=pltpu.CompilerParams())(*srcs)


def _adamw(parts, w, m, v, tr, name, own=None):
    rows, cols = w.shape
    c1, c2 = 1.0 - ADAM_B1 ** ADAM_STEP, 1.0 - ADAM_B2 ** ADAM_STEP

    def body(p_ref, *refs):
        w_ref, m_ref, v_ref, g_ref, d_ref, nm_ref, nv_ref = refs[-7:]
        me = _mesh_pos()

        def part(j):
            return p_ref[j] if own is None else jnp.where(me == j, refs[0][...], p_ref[j])

        g = part(0).astype(F32)
        for j in range(1, N_DEV):
            g = g + part(j).astype(F32)
        nm = ADAM_B1 * m_ref[...] + (1.0 - ADAM_B1) * g
        nv = ADAM_B2 * v_ref[...] + (1.0 - ADAM_B2) * jnp.square(g)
        g_ref[...] = g
        nm_ref[...] = nm
        nv_ref[...] = nv
        d_ref[...] = -ADAM_LR * ((nm / c1) / (jnp.sqrt(nv / c2) + ADAM_EPS) + ADAM_WD * w_ref[...])

    t = pl.BlockSpec((tr, cols), lambda i: (i, 0))
    extra = [] if own is None else [own]
    return pl.pallas_call(
        body, name=name, grid=(rows // tr,), in_specs=[pl.BlockSpec((N_DEV, tr, cols), lambda i: (0, i, 0))] + [t] * (3 + len(extra)),
        out_specs=[t] * 4, out_shape=[SDS((rows, cols), F32)] * 4, compiler_params=_params(("parallel",)))(parts, *extra, w, m, v)


SHARDED = (("w_in", D_MODEL, IN_COLS // N_DEV, True, 128), ("w_up_attn", WIDTH, D_MODEL // N_DEV, True, WIDTH),
           ("w_up_rwkv", WIDTH, D_MODEL // N_DEV, True, WIDTH), ("w_out", D_MODEL // N_DEV, D_MODEL, False, D_MODEL // N_DEV),
           ("rwkv_w_up", LORA, WIDTH // N_DEV, True, LORA), ("rwkv_a_up", LORA, WIDTH // N_DEV, True, LORA))
LOSS_SLOT = sum(n for _, n in SMALL)


def _pack_small(small, extra=None):
    flat = [small[n].reshape(-1).astype(F32) for n, _ in SMALL]
    flat.append(jnp.zeros((1,), F32) if extra is None else extra.reshape(1))
    flat.append(jnp.zeros((SMALL_ROWS * LANE - LOSS_SLOT - 1,), F32))
    return jnp.concatenate(flat).reshape(SMALL_ROWS, LANE)


def _unpack_small(packed, shapes):
    flat = packed.reshape(-1)
    out, off = {}, 0
    for n, cnt in SMALL:
        out[n] = flat[off:off + cnt].reshape(shapes[n])
        off += cnt
    return out, flat[LOSS_SLOT]


def _whole(gathered, by_cols):
    if not by_cols:
        return gathered.reshape(-1, gathered.shape[-1])
    return gathered.transpose(1, 0, 2).reshape(gathered.shape[1], -1)


def _per_owner(full, by_cols):
    if not by_cols:
        return full.reshape(N_DEV, -1, full.shape[-1])
    return full.reshape(full.shape[0], N_DEV, -1).transpose(1, 0, 2)


def _local_step(x, loss_target, sm, wts):
    bsz, s, d = x.shape
    n = bsz * s
    x2, tgt = x.reshape(n, d), loss_target.reshape(n, d)
    bidx = jnp.asarray(_bucket_tables())
    w_in = wts["w_in"]
    segs = (("qkv", 0, QKV_COLS, 512), ("za", OFF_ZA, WIDTH, 512), ("pr", OFF_PR, PR_COLS, PR_COLS), ("zr", OFF_ZR, WIDTH, 512),
            ("gm", OFF_GM, 2 * D_MODEL, 512))

    h, rs = _prenorm(x2, sm["pre_norm_gain"])
    proj = {nm: _mm(h, w_in[:, off:off + cnt], tn, "proj_" + nm) for nm, off, cnt, tn in segs}
    qkv3 = proj["qkv"].reshape(bsz, s, QKV_COLS)
    pr3 = proj["pr"].reshape(bsz, s, PR_COLS)

    o_attn, lse = _attn_fwd(qkv3, sm["rel_bias"], bidx)
    rk = sm["rwkv_r_k"].reshape(1, WIDTH)
    pre_args = (sm["rwkv_shift_mix"], sm["rwkv_w0"], wts["rwkv_w_up"], sm["rwkv_a0"], wts["rwkv_a_up"], sm["rwkv_k_k"], sm["rwkv_k_a"])
    scan_in = _rwkv_pre(pr3, *pre_args)
    o_rwkv, states, consts = _rwkv_scan(scan_in, rk, sm["rwkv_ln_w"], sm["rwkv_ln_b"])

    (dxo, do_attn, do_rwkv, dza, dzr, dgm, g_wua, g_wur, g_wout, g_post, loss) = _head(
        o_attn.reshape(n, WIDTH), o_rwkv.reshape(n, WIDTH), proj["za"], proj["zr"], proj["gm"], x2, tgt,
        wts["w_up_attn"], wts["w_up_rwkv"], wts["w_out"], sm["post_norm_gain"])

    dqkv, dbias = _attn_bwd(qkv3, o_attn, lse, do_attn.reshape(bsz, s, WIDTH), sm["rel_bias"], bidx)
    g_bias = _bias_grad(dbias, bidx)[:, :N_BUCKET].T

    scan_cots, (g_rk, g_lnw, g_lnb) = _rwkv_scan_bwd(scan_in, states, consts, do_rwkv.reshape(bsz, s, WIDTH), rk, sm["rwkv_ln_w"],
                                                     sm["rwkv_ln_b"])
    dprs, g_mix, g_w0, g_wup, g_a0, g_aup, g_kk, g_ka = _rwkv_pre_bwd(pr3, scan_cots, *pre_args)
    dpr = _shift_bwd(dprs, sm["rwkv_shift_mix"]).reshape(n, PR_COLS)

    dsegs = [(dqkv.reshape(9, n, WIDTH), 0, QKV_COLS, WIDTH), (dza, OFF_ZA, WIDTH, WIDTH), (dpr, OFF_PR, PR_COLS, PR_COLS),
             (dzr, OFF_ZR, WIDTH, WIDTH), (dgm, OFF_GM, 2 * D_MODEL, D_MODEL)]
    g_win = jnp.concatenate([_mm_tn(h, t, tn, "gw_in_%d" % j) for j, (t, _, _, tn) in enumerate(dsegs)], axis=1)
    blocks = _per_owner(g_win, True).astype(BF16)
    own = lax.dynamic_index_in_dim(blocks, 4 * lax.axis_index("x") + 2 * lax.axis_index("y") + lax.axis_index("c"), 0, keepdims=False)
    send_sems, recv_sems, blocks_thru, land_thru, token = _send_start(blocks)
    dh = None
    for j, (t, off, cnt, _) in enumerate(dsegs):
        dh = _mm_nt_acc(t, w_in[:, off:off + cnt] + token[0, 0], dh, "dh_%d" % j)
    grad_x, g_pre = _prenorm_bwd(dh, x2, rs, sm["pre_norm_gain"], dxo)
    landed = _send_wait(send_sems, recv_sems, blocks_thru, land_thru, g_pre)

    full = {"w_up_attn": g_wua, "w_up_rwkv": g_wur, "w_out": g_wout, "rwkv_w_up": g_wup, "rwkv_a_up": g_aup}
    small = {"pre_norm_gain": g_pre, "rel_bias": g_bias, "rwkv_shift_mix": g_mix, "rwkv_w0": g_w0, "rwkv_a0": g_a0, "rwkv_k_k": g_kk,
             "rwkv_k_a": g_ka, "rwkv_r_k": g_rk, "rwkv_ln_w": g_lnw, "rwkv_ln_b": g_lnb, "post_norm_gain": g_post}
    return loss[0, 0], grad_x.reshape(bsz, s, d), (landed, own), full, small


def kernel(x, pre_norm_gain, w_in, rel_bias, rwkv_shift_mix, rwkv_w0, rwkv_w_up, rwkv_a0, rwkv_a_up, rwkv_k_k, rwkv_k_a, rwkv_r_k, rwkv_ln_w, rwkv_ln_b, w_up_attn, w_up_rwkv, w_out, post_norm_gain, loss_target, m_pre_norm_gain, m_w_in, m_rel_bias, m_rwkv_shift_mix, m_rwkv_w0, m_rwkv_w_up, m_rwkv_a0, m_rwkv_a_up, m_rwkv_k_k, m_rwkv_k_a, m_rwkv_r_k, m_rwkv_ln_w, m_rwkv_ln_b, m_w_up_attn, m_w_up_rwkv, m_w_out, m_post_norm_gain, v_pre_norm_gain, v_w_in, v_rel_bias, v_rwkv_shift_mix, v_rwkv_w0, v_rwkv_w_up, v_rwkv_a0, v_rwkv_a_up, v_rwkv_k_k, v_rwkv_k_a, v_rwkv_r_k, v_rwkv_ln_w, v_rwkv_ln_b, v_w_up_attn, v_w_up_rwkv, v_w_out, v_post_norm_gain):
    names = [n for n, *_ in SHARDED] + [n for n, _ in SMALL]
    loc = dict(locals())
    w = {n: loc[n] for n in names}
    m = {n: loc["m_" + n] for n in names}
    v = {n: loc["v_" + n] for n in names}
    shapes = {n: w[n].shape for n in names}
    order = ["pre_norm_gain", "w_in", "rel_bias", "rwkv_shift_mix", "rwkv_w0", "rwkv_w_up", "rwkv_a0", "rwkv_a_up", "rwkv_k_k", "rwkv_k_a",
             "rwkv_r_k", "rwkv_ln_w", "rwkv_ln_b", "w_up_attn", "w_up_rwkv", "w_out", "post_norm_gain"]
    shard2d = lambda t, n, r, c: t[n].reshape(r, c)

    gathered = _gather([shard2d(w, n, r, c).astype(BF16) for n, r, c, _, _ in SHARDED], "gather_weights")
    wts = {n: _whole(g, by_cols) for (n, _, _, by_cols, _), g in zip(SHARDED, gathered)}

    loss, grad_x, (win_landed, win_own), full, small = _local_step(x, loss_target, w, wts)
    rest = SHARDED[1:]
    parts = _exchange([_per_owner(full[n], by_cols).astype(BF16) for n, _, _, by_cols, _ in rest] + [_pack_small(small, loss)],
                      [False] * len(rest) + [True], "exchange_grads")

    outs = [{}, {}, {}, {}]
    for (n, r, c, _, tr), p in zip(SHARDED, [win_landed] + list(parts)):
        res = _adamw(p, shard2d(w, n, r, c), shard2d(m, n, r, c), shard2d(v, n, r, c), tr, "adamw_" + n,
                     own=win_own if n == "w_in" else None)
        for o, t in zip(outs, res):
            o[n] = t.reshape(shapes[n])
    res = _adamw(parts[-1], _pack_small(w), _pack_small(m), _pack_small(v), SMALL_ROWS, "adamw_small")
    for o, t in zip(outs, res):
        o.update(_unpack_small(t, shapes)[0])
    loss = _unpack_small(res[0], shapes)[1]
    return (loss, grad_x, *[o[n] for o in outs for n in order])
```

```python
import functools
import math

import numpy as np
import jax
import jax.numpy as jnp
from jax import lax
from jax.experimental import pallas as pl
from jax.experimental.pallas import tpu as pltpu

F32, BF16 = jnp.float32, jnp.bfloat16
SDS = jax.ShapeDtypeStruct
HI = lax.Precision.HIGHEST
HI3 = lax.Precision.HIGH
MESH = pl.DeviceIdType.MESH

N_DEV = 8
D_MODEL = 1024
HEAD = 64
N_HEAD = 8
WIDTH = N_HEAD * HEAD
DILATIONS = (1, 4, 16)
QB = 128
N_BUCKET = 32
MAX_DIST = 2048
LORA = 64
QKV_COLS = 9 * WIDTH
PR_COLS = 3 * WIDTH + 2 * LORA
IN_COLS = QKV_COLS + WIDTH + PR_COLS + WIDTH + 2 * D_MODEL
OFF_ZA, OFF_PR, OFF_ZR, OFF_GM = QKV_COLS, QKV_COLS + WIDTH, QKV_COLS + WIDTH + PR_COLS, QKV_COLS + 2 * WIDTH + PR_COLS
RMS_EPS = 1e-6
GN_EPS = 64e-5
SCALE = 1.0 / math.sqrt(HEAD)
CHUNK = 64
CHUNK_GROUP = 8
BWD_GROUP = 8
EARLY = 8
NEG = -1e30
LANE = 128

ADAM_LR, ADAM_B1, ADAM_B2, ADAM_EPS, ADAM_WD, ADAM_STEP = 0.001, 0.9, 0.999, 1e-08, 0.01, 10

VMEM_LIMIT = 56 * 1024 * 1024

SMALL = (("pre_norm_gain", 1024), ("rel_bias", 768), ("rwkv_shift_mix", 1664), ("rwkv_w0", 512), ("rwkv_a0", 512),
         ("rwkv_k_k", 512), ("rwkv_k_a", 512), ("rwkv_r_k", 512), ("rwkv_ln_w", 512), ("rwkv_ln_b", 512),
         ("post_norm_gain", 1024))
SMALL_ROWS = 64


def _params(sem=None):
    return pltpu.CompilerParams(dimension_semantics=sem, vmem_limit_bytes=VMEM_LIMIT)


def _dot(a, b):
    return jnp.dot(a, b, preferred_element_type=F32)


def _dot_nt(a, b):
    return lax.dot_general(a, b, (((1,), (1,)), ((), ())), preferred_element_type=F32)


def _dot_tn(a, b):
    return lax.dot_general(a, b, (((0,), (0,)), ((), ())), preferred_element_type=F32)


@jax.custom_vjp
def _bdot(a, b):
    return _dot(a.astype(BF16), b.astype(BF16))


def _bdot_fwd(a, b):
    return _bdot(a, b), (a, b)


def _bdot_bwd(res, g):
    a, b = res
    gb = g.astype(BF16)
    return _dot_nt(gb, b.astype(BF16)), _dot_tn(a.astype(BF16), gb)


_bdot.defvjp(_bdot_fwd, _bdot_bwd)


def _silu(z):
    return z * jax.nn.sigmoid(z)


def _dsilu(z):
    s = jax.nn.sigmoid(z)
    return s * (1.0 + z * (1.0 - s))


def _softplus(x):
    return jnp.maximum(x, 0.0) + jnp.log(1.0 + jnp.exp(-jnp.abs(x)))


def _bucket_tables():
    qi = np.arange(QB)[:, None] + QB
    ki = np.arange(2 * QB)[None, :]
    rel = np.maximum(qi - ki, 0)
    out = []
    for d in DILATIONS:
        dist = rel * d
        max_exact = N_BUCKET // 2
        ratio = np.log(np.maximum(dist, 1).astype(np.float32) / max_exact) / np.float32(math.log(MAX_DIST / max_exact))
        large = max_exact + (ratio * (N_BUCKET - max_exact)).astype(np.int32)
        large = np.minimum(large, N_BUCKET - 1)
        out.append(np.where(dist < max_exact, dist, large).astype(np.int32))
    return np.stack(out)


def _prenorm(x2, g):
    n, d = x2.shape
    tm = 1024

    def body(x_ref, g_ref, h_ref, rs_ref):
        x = x_ref[...]
        rs = lax.rsqrt(jnp.mean(x * x, axis=-1, keepdims=True) + RMS_EPS)
        h_ref[...] = (x * rs * g_ref[...]).astype(BF16)
        rs_ref[...] = rs

    return pl.pallas_call(
        body, name="prenorm", grid=(n // tm,),
        in_specs=[pl.BlockSpec((tm, d), lambda i: (i, 0)), pl.BlockSpec((1, d), lambda i: (0, 0))],
        out_specs=[pl.BlockSpec((tm, d), lambda i: (i, 0)), pl.BlockSpec((tm, 1), lambda i: (i, 0))],
        out_shape=[SDS((n, d), BF16), SDS((n, 1), F32)], compiler_params=_params(("parallel",)))(x2, g)


def _mm(a, b, tn, name):
    m, k = a.shape
    n = b.shape[1]
    tm = 1024

    def body(a_ref, b_ref, o_ref):
        o_ref[...] = _dot(a_ref[...], b_ref[...])

    return pl.pallas_call(
        body, name=name, grid=(n // tn, m // tm),
        in_specs=[pl.BlockSpec((tm, k), lambda j, i: (i, 0)), pl.BlockSpec((k, tn), lambda j, i: (0, j))],
        out_specs=pl.BlockSpec((tm, tn), lambda j, i: (i, j)),
        out_shape=SDS((m, n), F32), compiler_params=_params(("parallel", "parallel")))(a, b)


def _mm_nt_acc(a, b, acc, name):
    split = a.ndim == 3
    m = a.shape[-2]
    k = b.shape[1]
    d = b.shape[0]
    tm = 1024
    per = 3 if split else 1
    seg = a.shape[2] if split else 0
    tk = per * seg if split else (k if k <= 2048 else 1536)
    have_acc = acc is not None

    def body(*refs):
        if have_acc:
            a_ref, b_ref, c_ref, o_ref = refs
        else:
            a_ref, b_ref, o_ref = refs
        if split:
            r = sum(_dot_nt(a_ref[j].astype(BF16), b_ref[:, seg * j:seg * (j + 1)]) for j in range(per))
        else:
            r = _dot_nt(a_ref[...].astype(BF16), b_ref[...])

        @pl.when(pl.program_id(1) == 0)
        def _():
            o_ref[...] = r + c_ref[...] if have_acc else r

        @pl.when(pl.program_id(1) != 0)
        def _():
            o_ref[...] += r

    a_spec = pl.BlockSpec((per, tm, seg), lambda i, j: (j, i, 0)) if split else pl.BlockSpec((tm, tk), lambda i, j: (i, j))
    in_specs = [a_spec, pl.BlockSpec((d, tk), lambda i, j: (0, j))]
    args = [a, b]
    if have_acc:
        in_specs.append(pl.BlockSpec((tm, d), lambda i, j: (i, 0)))
        args.append(acc)
    return pl.pallas_call(
        body, name=name, grid=(m // tm, k // tk), in_specs=in_specs, out_specs=pl.BlockSpec((tm, d), lambda i, j: (i, 0)),
        out_shape=SDS((m, d), F32), compiler_params=_params(("parallel", "arbitrary")))(*args)


def _mm_tn(a, b, tn, name):
    split = b.ndim == 3
    m, k1 = a.shape
    per = 3 if split else 1
    seg = b.shape[2] if split else tn
    tn = per * seg
    n2 = b.shape[0] * seg if split else b.shape[1]
    tm = 1024

    def body(a_ref, b_ref, o_ref):
        first = pl.program_id(1) == 0
        for j in range(per):
            r = _dot_tn(a_ref[...], (b_ref[j] if split else b_ref[...]).astype(BF16))
            cols = slice(seg * j, seg * (j + 1))

            @pl.when(first)
            def _(r=r, cols=cols):
                o_ref[:, cols] = r

            @pl.when(jnp.logical_not(first))
            def _(r=r, cols=cols):
                o_ref[:, cols] += r

    b_spec = pl.BlockSpec((per, tm, seg), lambda j, i: (j, i, 0)) if split else pl.BlockSpec((tm, tn), lambda j, i: (i, j))
    return pl.pallas_call(
        body, name=name, grid=(n2 // tn, m // tm),
        in_specs=[pl.BlockSpec((tm, k1), lambda j, i: (i, 0)), b_spec],
        out_specs=pl.BlockSpec((k1, tn), lambda j, i: (0, j)),
        out_shape=SDS((k1, n2), F32), compiler_params=_params(("parallel", "arbitrary")))(a, b)


def _ds(start, d):
    return pl.ds(start, QB) if d == 1 else pl.ds(start, QB, stride=d)


def _fill_bias(tab_ref, bidx_ref, bias_sc, hp):
    for g in range(3):
        bi = bidx_ref[g]
        for h in range(2):
            acc = jnp.zeros((QB, 2 * QB), F32)
            for j in range(N_BUCKET):
                acc = jnp.where(bi == j, tab_ref[j, g * N_HEAD + hp * 2 + h], acc)
            bias_sc[g * 2 + h] = acc


def _block_starts(it, d, nb):
    rho = it // nb
    n = it % nb
    st = rho + d * QB * n
    stp = rho + d * QB * jnp.maximum(n - 1, 0)
    return st, stp, n > 0


ATTN_BLOCKS = 4


def _bdot3(a, b, dims):
    return lax.dot_general(a, b, (dims, ((0,), (0,))), preferred_element_type=F32)


def _attn_operands(q_ref, k_ref, v_ref, bias_sc, g, d, nb, it0):
    ii = lax.broadcasted_iota(jnp.int32, (QB, 2 * QB), 0)
    cc = lax.broadcasted_iota(jnp.int32, (QB, 2 * QB), 1)
    qs, ks, vs, pens, starts = [], [], [], [], []
    for u in range(ATTN_BLOCKS):
        st, stp, hasprev = _block_starts(it0 + u, d, nb)
        qf = q_ref[0, _ds(st, d), :]
        kf = jnp.concatenate([k_ref[0, _ds(stp, d), :], k_ref[0, _ds(st, d), :]], axis=0)
        vf = jnp.concatenate([v_ref[0, _ds(stp, d), :], v_ref[0, _ds(st, d), :]], axis=0)
        own = jnp.logical_and(cc >= QB, ii >= cc - QB)
        prev = jnp.logical_and(jnp.logical_and(cc < QB, cc >= ii), hasprev)
        pen = jnp.where(jnp.logical_or(own, prev), 0.0, NEG)
        for h in range(2):
            sl = slice(HEAD * h, HEAD * h + HEAD)
            qs.append(qf[:, sl])
            ks.append(kf[:, sl])
            vs.append(vf[:, sl])
            pens.append(pen + bias_sc[g * 2 + h])
        starts.append((st, stp))
    return _stack(qs).astype(BF16), _stack(ks).astype(BF16), _stack(vs).astype(BF16), _stack(pens), starts


def _heads(x, u):
    return jnp.concatenate([x[2 * u], x[2 * u + 1]], axis=1)


def _attn_fwd(qkv3, rel_bias, bidx):
    bsz, s, _ = qkv3.shape
    rt = 256

    def body(tab_ref, bidx_ref, *refs):
        q_refs, k_refs, v_refs = refs[0:3], refs[3:6], refs[6:9]
        o_ref, lse_ref = refs[9:11]
        bias_sc, num_sc, den_sc, m_sc = refs[11:]
        pl.when(pl.program_id(1) == 0)(lambda: _fill_bias(tab_ref, bidx_ref, bias_sc, pl.program_id(0)))
        for g, d in enumerate(DILATIONS):
            nb = s // (QB * d)

            def blk(it, c, g=g, d=d, nb=nb):
                q, k, v, bias, starts = _attn_operands(q_refs[g], k_refs[g], v_refs[g], bias_sc, g, d, nb, it * ATTN_BLOCKS)
                sc = _bdot3(q, k, ((2,), (2,))) * SCALE + bias
                m = jnp.max(sc, axis=-1, keepdims=True)
                p = jnp.exp(sc - m)
                den = jnp.sum(p, axis=-1, keepdims=True)
                num = _bdot3(p.astype(BF16), v, ((2,), (1,)))
                den, m = jnp.broadcast_to(den, num.shape), jnp.broadcast_to(m, num.shape)
                for u, (st, _) in enumerate(starts):
                    num_sc[g, _ds(st, d), :] = _heads(num, u)
                    den_sc[g, _ds(st, d), :] = _heads(den, u)
                    m_sc[g, _ds(st, d), :] = _heads(m, u)
                return c

            lax.fori_loop(0, s // QB // ATTN_BLOCKS, blk, 0)

        def merge(i, c):
            rows = pl.ds(pl.multiple_of(i * rt, rt), rt)
            m0, m1, m2 = m_sc[0, rows, :], m_sc[1, rows, :], m_sc[2, rows, :]
            mall = jnp.maximum(jnp.maximum(m0, m1), m2)
            w0, w1, w2 = jnp.exp(m0 - mall), jnp.exp(m1 - mall), jnp.exp(m2 - mall)
            num = w0 * num_sc[0, rows, :] + w1 * num_sc[1, rows, :] + w2 * num_sc[2, rows, :]
            den = w0 * den_sc[0, rows, :] + w1 * den_sc[1, rows, :] + w2 * den_sc[2, rows, :]
            o_ref[0, rows, :] = num / den
            lse_ref[0, rows, :] = mall + jnp.log(den)
            return c

        lax.fori_loop(0, s // rt, merge, 0)

    col = lambda w, g: (lambda hp, b: (b, 0, (w * 3 + g) * 4 + hp))
    in_specs = [pl.BlockSpec(memory_space=pltpu.SMEM), pl.BlockSpec((3, QB, 2 * QB), lambda hp, b: (0, 0, 0))]
    in_specs += [pl.BlockSpec((1, s, LANE), col(w, g)) for w in range(3) for g in range(3)]
    out_spec = pl.BlockSpec((1, s, LANE), lambda hp, b: (b, 0, hp))
    return pl.pallas_call(
        body, name="attn_fwd", grid=(4, bsz), in_specs=in_specs, out_specs=[out_spec, out_spec],
        out_shape=[SDS((bsz, s, WIDTH), F32), SDS((bsz, s, WIDTH), F32)],
        scratch_shapes=[pltpu.VMEM((6, QB, 2 * QB), F32), pltpu.VMEM((3, s, LANE), F32), pltpu.VMEM((3, s, LANE), F32),
                        pltpu.VMEM((3, s, LANE), F32)],
        compiler_params=_params(("arbitrary", "arbitrary")))(rel_bias, bidx, *([qkv3] * 9))


def _attn_bwd(qkv3, o3, lse3, do3, rel_bias, bidx):
    bsz, s, _ = qkv3.shape
    rt = 256

    def body(tab_ref, bidx_ref, *refs):
        q_refs, k_refs, v_refs = refs[0:3], refs[3:6], refs[6:9]
        o_ref, lse_ref, do_ref, dqkv_ref, db_ref, bias_sc, delta_sc = refs[9:]
        dq_refs, dk_refs, dv_refs = ([dqkv_ref.at[w * 3 + g] for g in range(3)] for w in range(3))

        @pl.when(pl.program_id(1) == 0)
        def _():
            _fill_bias(tab_ref, bidx_ref, bias_sc, pl.program_id(0))
            db_ref[...] = jnp.zeros_like(db_ref)

        def prep(i, c):
            rows = pl.ds(pl.multiple_of(i * rt, rt), rt)
            prod = do_ref[0, rows, :] * o_ref[0, rows, :]
            d0 = jnp.sum(prod[:, :HEAD], axis=-1, keepdims=True)
            d1 = jnp.sum(prod[:, HEAD:], axis=-1, keepdims=True)
            delta_sc[rows, :] = jnp.concatenate([jnp.broadcast_to(d0, (rt, HEAD)), jnp.broadcast_to(d1, (rt, HEAD))], axis=1)
            z = jnp.zeros((rt, LANE), F32)
            for g in range(3):
                dk_refs[g][0, rows, :] = z
                dv_refs[g][0, rows, :] = z
            return c

        lax.fori_loop(0, s // rt, prep, 0)
        for g, d in enumerate(DILATIONS):
            nb = s // (QB * d)

            def blk(it, c, g=g, d=d, nb=nb):
                q, k, v, bias, starts = _attn_operands(q_refs[g], k_refs[g], v_refs[g], bias_sc, g, d, nb, it * ATTN_BLOCKS)
                dos, lses, deltas = [], [], []
                for st, _ in starts:
                    dof, lsef, delf = do_ref[0, _ds(st, d), :], lse_ref[0, _ds(st, d), :], delta_sc[_ds(st, d), :]
                    for h in range(2):
                        dos.append(dof[:, HEAD * h:HEAD * h + HEAD])
                        lses.append(lsef[:, HEAD * h:HEAD * h + 1])
                        deltas.append(delf[:, HEAD * h:HEAD * h + 1])
                do, lse, delta = _stack(dos).astype(BF16), _stack(lses), _stack(deltas)
                p = jnp.exp(_bdot3(q, k, ((2,), (2,))) * SCALE + bias - lse)
                dv = _bdot3(p.astype(BF16), do, ((1,), (1,)))
                ds = p * (_bdot3(do, v, ((2,), (2,))) - delta)
                dsb = ds.astype(BF16)
                dq = _bdot3(dsb, k, ((2,), (1,))) * SCALE
                dk = _bdot3(dsb, q, ((1,), (1,))) * SCALE
                for h in range(2):
                    db_ref[0, g * 2 + h] += sum(ds[2 * u + h] for u in range(ATTN_BLOCKS))
                for u, (st, stp) in enumerate(starts):
                    dq_refs[g][0, _ds(st, d), :] = _heads(dq, u)
                    dk_refs[g][0, _ds(stp, d), :] += _heads(dk[:, :QB], u)
                    dv_refs[g][0, _ds(stp, d), :] += _heads(dv[:, :QB], u)
                    dk_refs[g][0, _ds(st, d), :] += _heads(dk[:, QB:], u)
                    dv_refs[g][0, _ds(st, d), :] += _heads(dv[:, QB:], u)
                return c

            lax.fori_loop(0, s // QB // ATTN_BLOCKS, blk, 0)

    col = lambda w, g: (lambda hp, b: (b, 0, (w * 3 + g) * 4 + hp))
    blk_spec = pl.BlockSpec((1, s, LANE), lambda hp, b: (b, 0, hp))
    in_specs = [pl.BlockSpec(memory_space=pltpu.SMEM), pl.BlockSpec((3, QB, 2 * QB), lambda hp, b: (0, 0, 0))]
    in_specs += [pl.BlockSpec((1, s, LANE), col(w, g)) for w in range(3) for g in range(3)]
    in_specs += [blk_spec] * 3
    out_specs = [pl.BlockSpec((9, 1, s, LANE), lambda hp, b: (0, b, 0, hp)), pl.BlockSpec((1, 6, QB, 2 * QB), lambda hp, b: (hp, 0, 0, 0))]
    out_shape = [SDS((9, bsz, s, WIDTH), F32), SDS((4, 6, QB, 2 * QB), F32)]
    return pl.pallas_call(
        body, name="attn_bwd", grid=(4, bsz), in_specs=in_specs, out_specs=out_specs, out_shape=out_shape,
        scratch_shapes=[pltpu.VMEM((6, QB, 2 * QB), F32), pltpu.VMEM((s, LANE), F32)],
        compiler_params=_params(("parallel", "arbitrary")))(rel_bias, bidx, *([qkv3] * 9), o3, lse3, do3)


def _bias_grad(dbias, bidx):
    def body(db_ref, bidx_ref, o_ref):
        lane = lax.broadcasted_iota(jnp.int32, (1, LANE), 1)
        for g in range(3):
            bi = bidx_ref[g]
            for hp in range(4):
                for h in range(2):
                    mat = db_ref[hp, g * 2 + h]
                    row = jnp.zeros((1, LANE), F32)
                    for j in range(N_BUCKET):
                        part = jnp.sum(jnp.where(bi == j, mat, 0.0), axis=0, keepdims=True)
                        row = jnp.where(lane == j, jnp.sum(part, axis=1, keepdims=True), row)
                    hd = g * N_HEAD + hp * 2 + h
                    o_ref[hd:hd + 1, :] = row

    return pl.pallas_call(body, name="bias_grad", out_shape=SDS((3 * N_HEAD, LANE), F32), compiler_params=_params())(dbias, bidx)


def _pre_fn(r, k0, v, wl, al, w0, wup, a0, aup, kk_, ka_):
    u = w0 + _bdot(jnp.tanh(wl), wup)
    lw = -jnp.exp(-_softplus(-u) - 0.5)
    a = jax.nn.sigmoid(a0 + _bdot(al, aup))
    kkraw = k0 * kk_
    k = k0 * (1.0 + (a - 1.0) * ka_)
    return r, lw, k, v, kkraw, a


PRE_SPLIT = (0, WIDTH, 2 * WIDTH, 3 * WIDTH, 3 * WIDTH + LORA, 3 * WIDTH + 2 * LORA)


def _pre_pieces(prs):
    return [prs[:, a:b] for a, b in zip(PRE_SPLIT[:-1], PRE_SPLIT[1:])]


PRE_TT = 512


def _shifted(pr_ref, edge_ref, first, back):
    pr = pr_ref[0]
    tt = pr.shape[0]
    row = lax.broadcasted_iota(jnp.int32, (tt, 1), 0)
    if back:
        edge = jnp.where(first, 0.0, edge_ref[0, 7:8, :])
        return jnp.where(row == 0, edge, pltpu.roll(pr, 1, axis=0))
    edge = jnp.where(first, 0.0, edge_ref[0, 0:1, :])
    return jnp.where(row == tt - 1, edge, pltpu.roll(pr, tt - 1, axis=0))


def _rwkv_pre(pr3, mix, w0, wup, a0, aup, kk_, ka_):
    bsz, s, _ = pr3.shape
    tt = PRE_TT

    def body(pr_ref, edge_ref, mix_ref, w0_ref, wup_ref, a0_ref, aup_ref, kk_ref, ka_ref, *outs):
        pr = pr_ref[0]
        prev = _shifted(pr_ref, edge_ref, pl.program_id(1) == 0, True)
        prs = pr + (prev - pr) * mix_ref[...]
        vals = _pre_fn(*_pre_pieces(prs), w0_ref[...], wup_ref[...].astype(F32), a0_ref[...], aup_ref[...].astype(F32), kk_ref[...],
                       ka_ref[...])
        for o, val in zip(outs, vals):
            o[0] = val

    vec = lambda n: pl.BlockSpec((1, n), lambda b, i: (0, 0))
    mat = pl.BlockSpec((LORA, WIDTH), lambda b, i: (0, 0))
    in_specs = [pl.BlockSpec((1, tt, PR_COLS), lambda b, i: (b, i, 0)),
                pl.BlockSpec((1, 8, PR_COLS), lambda b, i: (b, jnp.maximum(i * (tt // 8) - 1, 0), 0)),
                vec(PR_COLS), vec(WIDTH), mat, vec(WIDTH), mat, vec(WIDTH), vec(WIDTH)]
    out_spec = pl.BlockSpec((1, tt, WIDTH), lambda b, i: (b, i, 0))
    return pl.pallas_call(
        body, name="rwkv_pre", grid=(bsz, s // tt), in_specs=in_specs, out_specs=[out_spec] * 6,
        out_shape=[SDS((bsz, s, WIDTH), F32)] * 6, compiler_params=_params(("parallel", "parallel")))(
            pr3, pr3, mix, w0, wup, a0, aup, kk_, ka_)


def _rwkv_pre_bwd(pr3, cots, mix, w0, wup, a0, aup, kk_, ka_):
    bsz, s, _ = pr3.shape
    tt = PRE_TT

    def body(pr_ref, edge_ref, c0, c1, c2, c3, c4, c5, mix_ref, w0_ref, wup_ref, a0_ref, aup_ref, kk_ref, ka_ref,
             dprs_ref, dmix_ref, dw0_ref, dwup_ref, da0_ref, daup_ref, dkk_ref, dka_ref):
        pr = pr_ref[0]
        prev = _shifted(pr_ref, edge_ref, pl.program_id(1) == 0, True)
        prs = pr + (prev - pr) * mix_ref[...]
        _, vjp = jax.vjp(_pre_fn, *_pre_pieces(prs), w0_ref[...], wup_ref[...].astype(F32), a0_ref[...], aup_ref[...].astype(F32),
                         kk_ref[...], ka_ref[...])
        grads = vjp(tuple(c[0] for c in (c0, c1, c2, c3, c4, c5)))
        for piece, a, b in zip(grads[:5], PRE_SPLIT[:-1], PRE_SPLIT[1:]):
            dprs_ref[0, :, a:b] = piece
        dw0, dwup, da0, daup, dkk, dka = grads[5:]
        dprs = dprs_ref[0]
        grads = (jnp.sum(dprs * (prev - pr), axis=0, keepdims=True), dw0, dwup, da0, daup, dkk, dka)
        refs = (dmix_ref, dw0_ref, dwup_ref, da0_ref, daup_ref, dkk_ref, dka_ref)
        first = jnp.logical_and(pl.program_id(0) == 0, pl.program_id(1) == 0)

        @pl.when(first)
        def _():
            for r_, g_ in zip(refs, grads):
                r_[...] = g_

        @pl.when(jnp.logical_not(first))
        def _():
            for r_, g_ in zip(refs, grads):
                r_[...] += g_

    vec = lambda n: pl.BlockSpec((1, n), lambda b, i: (0, 0))
    mat = pl.BlockSpec((LORA, WIDTH), lambda b, i: (0, 0))
    tile = pl.BlockSpec((1, tt, WIDTH), lambda b, i: (b, i, 0))
    in_specs = [pl.BlockSpec((1, tt, PR_COLS), lambda b, i: (b, i, 0)),
                pl.BlockSpec((1, 8, PR_COLS), lambda b, i: (b, jnp.maximum(i * (tt // 8) - 1, 0), 0))]
    in_specs += [tile] * 6 + [vec(PR_COLS), vec(WIDTH), mat, vec(WIDTH), mat, vec(WIDTH), vec(WIDTH)]
    out_specs = [pl.BlockSpec((1, tt, PR_COLS), lambda b, i: (b, i, 0)), vec(PR_COLS), vec(WIDTH), mat, vec(WIDTH), mat,
                 vec(WIDTH), vec(WIDTH)]
    out_shape = [SDS((bsz, s, PR_COLS), F32), SDS((1, PR_COLS), F32), SDS((1, WIDTH), F32), SDS((LORA, WIDTH), F32),
                 SDS((1, WIDTH), F32), SDS((LORA, WIDTH), F32), SDS((1, WIDTH), F32), SDS((1, WIDTH), F32)]
    return pl.pallas_call(
        body, name="rwkv_pre_bwd", grid=(bsz, s // tt), in_specs=in_specs, out_specs=out_specs, out_shape=out_shape,
        compiler_params=_params(("arbitrary", "arbitrary")))(pr3, pr3, *cots, mix, w0, wup, a0, aup, kk_, ka_)


def _shift_bwd(dprs3, mix):
    bsz, s, _ = dprs3.shape
    tt = PRE_TT
    nt = s // tt

    def body(d_ref, edge_ref, mix_ref, o_ref):
        nxt = _shifted(d_ref, edge_ref, pl.program_id(1) == nt - 1, False)
        m = mix_ref[...]
        o_ref[0] = d_ref[0] * (1.0 - m) + nxt * m

    in_specs = [pl.BlockSpec((1, tt, PR_COLS), lambda b, i: (b, i, 0)),
                pl.BlockSpec((1, 8, PR_COLS), lambda b, i: (b, jnp.minimum((i + 1) * (tt // 8), s // 8 - 1), 0)),
                pl.BlockSpec((1, PR_COLS), lambda b, i: (0, 0))]
    return pl.pallas_call(
        body, name="shift_bwd", grid=(bsz, nt), in_specs=in_specs, out_specs=pl.BlockSpec((1, tt, PR_COLS), lambda b, i: (b, i, 0)),
        out_shape=SDS((bsz, s, PR_COLS), F32), compiler_params=_params(("parallel", "parallel")))(dprs3, dprs3, mix)


_NN, _NT, _TN = ((2,), (1,)), ((2,), (2,)), ((1,), (1,))


def _dot3(a, b, dims, precision=HI3):
    return lax.dot_general(a, b, (dims, ((0,), (0,))), precision=precision, preferred_element_type=F32)


def _dot3_bf16(a, b, dims):
    return lax.dot_general(a.astype(BF16), b.astype(BF16), (dims, ((0,), (0,))), preferred_element_type=F32)


class _Dots:
    def __init__(self, fwd):
        def make(dims, da_rule, db_rule):
            @jax.custom_vjp
            def f(a, b):
                return fwd(a, b, dims)

            f.defvjp(lambda a, b: (f(a, b), (a, b)), lambda res, g: (da_rule(*res, g), db_rule(*res, g)))
            return f

        one = _dot3_bf16
        self.mm = make(_NN, lambda a, b, g: one(g, b, _NT), lambda a, b, g: one(a, g, _TN))
        self.mm_nt = make(_NT, lambda a, b, g: one(g, b, _NN), lambda a, b, g: one(g, a, _TN))
        self.mm_tn = make(_TN, lambda a, b, g: one(b, g, _NT), lambda a, b, g: one(a, g, _NN))

        def powers(aab):
            ps = [aab]
            while 2 ** len(ps) < aab.shape[1]:
                ps.append(fwd(ps[-1], ps[-1], _NN))
            return ps

        def apply(ps, z, dims):
            for p in ps:
                z = z + fwd(p, z, dims)
            return z

        @jax.custom_vjp
        def solve(aab, z):
            return apply(powers(aab), z, _NN)

        def solve_fwd(aab, z):
            ps = powers(aab)
            x = apply(ps, z, _NN)
            return x, (ps, x)

        def solve_bwd(res, g):
            ps, x = res
            dz = apply(ps, g, _TN)
            return fwd(dz, x, _NT), dz

        solve.defvjp(solve_fwd, solve_bwd)
        self.solve = solve


_ACCURATE = _Dots(_dot3)
_ONE_PASS = _Dots(_dot3_bf16)
_bmm, _bmm_tn = _ACCURATE.mm, _ACCURATE.mm_tn


def _chunk_fn(s0t, r, lw, k, v, kkraw, a, rk, lnw, lnb, first=False, d=_ACCURATE):
    c = r.shape[1]
    at, rt, btc, ktc, gc, aab, arb, xv, arkv, ain, bin_ = _chunk_core(r, lw, k, v, kkraw, a, d)
    rs = d.mm(jnp.concatenate([at, rt], axis=1), s0t)
    u = d.solve(aab, rs[:, :c] + xv)
    y = rs[:, c:] + d.mm(arb, u) + arkv
    if first:
        y = _with_early_rows(y, r, lw, k, v, ain, bin_)
    gcol = jnp.sum(_diag(gc), axis=2, keepdims=True)
    sct = gcol * s0t + d.mm_tn(jnp.concatenate([btc, ktc], axis=1), jnp.concatenate([u, v], axis=1))
    return _post(y, r, k, v, rk, lnw, lnb), sct


def _diag(gc):
    return jnp.where(_masks(HEAD)[2], gc, 0.0)


def _with_early_rows(y, r, lw, k, v, ain, bin_):
    early = _stack([_early_rows(r[h], lw[h], k[h], v[h], ain[h], bin_[h]) for h in range(2)])
    return jnp.concatenate([jnp.concatenate([early, y[:2, EARLY:]], axis=1), y[2:]], axis=0)


def _early_rows(r, lw, k, v, ain, bin_):
    wc, bc, kc = jnp.transpose(jnp.exp(lw)), jnp.transpose(bin_), jnp.transpose(k)
    st = jnp.zeros((HEAD, HEAD), F32)
    rows = []
    for t in range(EARLY):
        sa = _bdot(ain[t:t + 1], st)
        st = st * wc[:, t:t + 1] + bc[:, t:t + 1] * sa + kc[:, t:t + 1] * v[t:t + 1]
        rows.append(_bdot(r[t:t + 1], st))
    return jnp.concatenate(rows, axis=0)


def _chunk_rows(c):
    return pl.ds(c * CHUNK, CHUNK) if isinstance(c, int) else pl.ds(pl.multiple_of(c * CHUNK, CHUNK), CHUNK)


def _stack(xs):
    return jnp.concatenate([x[None] for x in xs], axis=0)


def _pairs(ref, chunks):
    tiles = [ref[0, _chunk_rows(c), :] for c in chunks]
    return _stack([t[:, HEAD * h:HEAD * h + HEAD] for t in tiles for h in range(2)])


def _unpair(vals, j):
    return jnp.concatenate([vals[2 * j], vals[2 * j + 1]], axis=1)


def _masks(c):
    ii = lax.broadcasted_iota(jnp.int32, (c, c), 0)
    jj = lax.broadcasted_iota(jnp.int32, (c, c), 1)
    return ii > jj, ii >= jj, ii == jj


def _chunk_core(r, lw, k, v, kkraw, a, d=_ACCURATE):
    g_, c = r.shape[0], r.shape[1]
    nrm = jnp.sqrt(jnp.sum(kkraw * kkraw, axis=-1, keepdims=True))
    kkn = kkraw / jnp.maximum(nrm, 1e-12)
    ain, bin_ = -kkn, kkn * a
    strict, incl, _ = _masks(c)
    lg = lax.dot_general(jnp.broadcast_to(incl.astype(F32), (g_, c, c)), lw, (((2,), (1,)), ((0,), (0,))), precision=HI,
                         preferred_element_type=F32)
    g, gp, gi = jnp.exp(lg), jnp.exp(lg - lw), jnp.exp(-lg)
    at, rt, bt, kt = ain * gp, r * g, bin_ * gi, k * gi
    aa = d.mm_nt(jnp.concatenate([at, rt], axis=1), jnp.concatenate([bt, kt], axis=1))
    aab = jnp.where(strict, aa[:, :c, :c], 0.0)
    aak = jnp.where(strict, aa[:, :c, c:], 0.0)
    arb = jnp.where(incl, aa[:, c:, :c], 0.0)
    ark = jnp.where(incl, aa[:, c:, c:], 0.0)
    akv = d.mm(jnp.concatenate([aak, ark], axis=1), v)
    gc = g[:, c - 1:c, :]
    return at, rt, bt * gc, kt * gc, gc, aab, arb, akv[:, :c], akv[:, c:], ain, bin_


def _post(y, r, k, v, rk, lnw, lnb):
    mu = jnp.mean(y, axis=-1, keepdims=True)
    var = jnp.mean(jnp.square(y - mu), axis=-1, keepdims=True)
    yn = (y - mu) * lax.rsqrt(var + GN_EPS) * lnw + lnb
    return yn + jnp.sum(r * k * rk, axis=-1, keepdims=True) * v


def _chunk_consts(r, lw, k, v, kkraw, a, first=False):
    at, rt, btc, ktc, gc, aab, arb, xv, arkv, ain, bin_ = _chunk_core(r, lw, k, v, kkraw, a)
    z = _ACCURATE.solve(aab, jnp.concatenate([at, xv], axis=2))
    ryv = jnp.concatenate([rt, arkv], axis=2) + _bmm(arb, z)
    if first:
        ryv = jnp.concatenate([ryv[:, :, :HEAD], _with_early_rows(ryv[:, :, HEAD:], r, lw, k, v, ain, bin_)], axis=2)
    mkv = _bmm_tn(btc, z) + jnp.concatenate([_diag(gc), _bmm_tn(ktc, v)], axis=2)
    return mkv, ryv


def _rwkv_scan(ins, rk, lnw, lnb):
    bsz, s, _ = ins[0].shape
    nch = s // CHUNK

    def consts_body(r_ref, lw_ref, k_ref, v_ref, kk_ref, a_ref, mkv_ref, ry_ref, yv_ref):
        def group(i, carry):
            chunks = [i * CHUNK_GROUP + j for j in range(CHUNK_GROUP)]
            mkv, ryv = _chunk_consts(*[_pairs(ref, chunks) for ref in (r_ref, lw_ref, k_ref, v_ref, kk_ref, a_ref)],
                                     first=isinstance(i, int) and i == 0)
            for j, c in enumerate(chunks):
                for h in range(2):
                    mkv_ref[0, 0, c, h] = mkv[2 * j + h]
                ry_ref[0, _chunk_rows(c), :] = jnp.concatenate([ryv[2 * j][:, :HEAD], ryv[2 * j + 1][:, :HEAD]], axis=1)
                yv_ref[0, _chunk_rows(c), :] = jnp.concatenate([ryv[2 * j][:, HEAD:], ryv[2 * j + 1][:, HEAD:]], axis=1)
            return carry

        group(0, 0)
        lax.fori_loop(1, nch // CHUNK_GROUP, group, 0)

    tile = pl.BlockSpec((1, s, LANE), lambda b, hp: (b, 0, hp))
    vec = pl.BlockSpec((1, LANE), lambda b, hp: (0, hp))
    mkv_spec = pl.BlockSpec((1, 1, nch, 2, HEAD, LANE), lambda b, hp: (b, hp, 0, 0, 0, 0))
    st_spec = pl.BlockSpec((1, 1, nch, 2, HEAD, HEAD), lambda b, hp: (b, hp, 0, 0, 0, 0))
    mkv, ry, yv = pl.pallas_call(
        consts_body, name="rwkv_consts", grid=(bsz, 4), in_specs=[tile] * 6, out_specs=[mkv_spec, tile, tile],
        out_shape=[SDS((bsz, 4, nch, 2, HEAD, LANE), F32), SDS((bsz, s, WIDTH), F32), SDS((bsz, s, WIDTH), F32)],
        compiler_params=_params(("parallel", "parallel")))(*ins)

    states = _chunk_recurrence(mkv, None, "rwkv_states")

    def out_body(ry_ref, yv_ref, r_ref, k_ref, v_ref, st_ref, rk_ref, lnw_ref, lnb_ref, o_ref):
        y, r, k, v, rk_, lnw_, lnb_ = _scan_rows(ry_ref, yv_ref, r_ref, k_ref, v_ref, st_ref, rk_ref, lnw_ref, lnb_ref)
        o = _post(y, r, k, v, rk_, lnw_, lnb_)
        for j in range(CHUNK_GROUP):
            o_ref[0, _chunk_rows(j), :] = _unpair(o, j)

    o = pl.pallas_call(
        out_body, name="rwkv_out", grid=(bsz, 4, nch // CHUNK_GROUP), in_specs=_group_specs(5), out_specs=_group_specs(1)[0],
        out_shape=SDS((bsz, s, WIDTH), F32),
        compiler_params=_params(("parallel", "parallel", "parallel")))(ry, yv, ins[0], ins[2], ins[3], states, rk, lnw, lnb)
    return o, states, (mkv, ry, yv)


def _group_specs(n_tiles):
    tile = pl.BlockSpec((1, CHUNK_GROUP * CHUNK, LANE), lambda b, hp, t: (b, t, hp))
    if n_tiles == 1:
        return [tile]
    st = pl.BlockSpec((1, 1, CHUNK_GROUP, 2, HEAD, HEAD), lambda b, hp, t: (b, hp, t, 0, 0, 0))
    vec = pl.BlockSpec((1, LANE), lambda b, hp, t: (0, hp))
    return [tile] * n_tiles + [st] + [vec] * 3


def _scan_rows(ry_ref, yv_ref, r_ref, k_ref, v_ref, st_ref, rk_ref, lnw_ref, lnb_ref):
    chunks = list(range(CHUNK_GROUP))
    ry, yv, r, k, v = (_pairs(ref, chunks) for ref in (ry_ref, yv_ref, r_ref, k_ref, v_ref))
    st = _stack([st_ref[0, 0, c, h] for c in chunks for h in range(2)])
    vecs = [_stack([ref[:, HEAD * h:HEAD * h + HEAD] for _ in chunks for h in range(2)]) for ref in (rk_ref, lnw_ref, lnb_ref)]
    return (_bmm(ry, st) + yv, r, k, v, *vecs)


def _chunk_recurrence(mkv, q, name):
    bsz, _, nch = mkv.shape[:3]
    pairs = [(hp, h) for hp in range(4) for h in range(2)]

    def body(*refs):
        mkv_ref, out_ref, acc = refs[0], refs[-2], refs[-1]
        acc[...] = jnp.zeros_like(acc)

        def step(i, carry):
            c = i if q is None else nch - 1 - i
            cur = acc[...]
            for j, (hp, h) in enumerate(pairs):
                out_ref[0, hp, c, h] = cur[j]
            m = _stack([mkv_ref[0, hp, c, h] for hp, h in pairs])
            if q is None:
                acc[...] = _bmm(m[:, :, :HEAD], cur) + m[:, :, HEAD:]
            else:
                acc[...] = _bmm_tn(m[:, :, :HEAD], cur) + _stack([refs[1][0, hp, c, h] for hp, h in pairs])
            return carry

        lax.fori_loop(0, nch, step, 0)

    spec = lambda w: pl.BlockSpec((1, 4, nch, 2, HEAD, w), lambda b: (b, 0, 0, 0, 0, 0))
    return pl.pallas_call(
        body, name=name, grid=(bsz,), in_specs=[spec(LANE)] + ([] if q is None else [spec(HEAD)]), out_specs=spec(HEAD),
        out_shape=SDS((bsz, 4, nch, 2, HEAD, HEAD), F32), scratch_shapes=[pltpu.VMEM((8, HEAD, HEAD), F32)],
        compiler_params=_params(("parallel",)))(*([mkv] if q is None else [mkv, q]))


def _rwkv_scan_bwd(ins, states, consts, do3, rk, lnw, lnb):
    bsz, s, _ = ins[0].shape
    nch = s // CHUNK

    mkv, ry, yv = consts

    def q_body(do_ref, ry_ref, yv_ref, r_ref, k_ref, v_ref, st_ref, rk_ref, lnw_ref, lnb_ref, q_ref):
        y, r, k, v, rk_, lnw_, lnb_ = _scan_rows(ry_ref, yv_ref, r_ref, k_ref, v_ref, st_ref, rk_ref, lnw_ref, lnb_ref)
        _, vjp = jax.vjp(lambda y_: _post(y_, r, k, v, rk_, lnw_, lnb_), y)
        (dy,) = vjp(_pairs(do_ref, list(range(CHUNK_GROUP))))
        q = _bmm_tn(_pairs(ry_ref, list(range(CHUNK_GROUP))), dy)
        for j in range(CHUNK_GROUP):
            for h in range(2):
                q_ref[0, 0, j, h] = q[2 * j + h]

    specs = _group_specs(6)
    q = pl.pallas_call(
        q_body, name="rwkv_q", grid=(bsz, 4, nch // CHUNK_GROUP), in_specs=specs, out_specs=specs[6],
        out_shape=SDS((bsz, 4, nch, 2, HEAD, HEAD), F32),
        compiler_params=_params(("parallel", "parallel", "parallel")))(do3, ry, yv, ins[0], ins[2], ins[3], states, rk, lnw, lnb)

    dstates = _chunk_recurrence(mkv, q, "rwkv_dstates")

    def body(r_ref, lw_ref, k_ref, v_ref, kk_ref, a_ref, st_ref, dst_ref, do_ref, rk_ref, lnw_ref, lnb_ref,
             dr_ref, dlw_ref, dk_ref, dv_ref, dkk_ref, da_ref, drk_ref, dlnw_ref, dlnb_ref):
        chunks = list(range(BWD_GROUP))
        par_refs = (drk_ref, dlnw_ref, dlnb_ref)

        @pl.when(jnp.logical_and(pl.program_id(1) == 0, pl.program_id(2) == 0))
        def _():
            for ref in par_refs:
                ref[...] = jnp.zeros_like(ref)

        def group(first):
            per_pair = lambda ref: _stack([ref[0, 0, c, h] for c in chunks for h in range(2)])
            vecs = [_stack([ref[:, HEAD * h:HEAD * h + HEAD] for _ in chunks for h in range(2)]) for ref in (rk_ref, lnw_ref, lnb_ref)]
            _, vjp = jax.vjp(functools.partial(_chunk_fn, first=first, d=_ONE_PASS), per_pair(st_ref),
                             *[_pairs(ref, chunks) for ref in (r_ref, lw_ref, k_ref, v_ref, kk_ref, a_ref)], *vecs)
            grads = vjp((_pairs(do_ref, chunks), per_pair(dst_ref)))
            for ref, cot in zip((dr_ref, dlw_ref, dk_ref, dv_ref, dkk_ref, da_ref), grads[1:7]):
                for j, c in enumerate(chunks):
                    ref[0, _chunk_rows(c), :] = _unpair(cot, j)
            for ref, g_ in zip(par_refs, grads[7:10]):
                ref[...] += jnp.concatenate([sum(g_[2 * j + h] for j in range(BWD_GROUP)) for h in range(2)], axis=1)

        pl.when(pl.program_id(2) == 0)(functools.partial(group, True))
        pl.when(pl.program_id(2) != 0)(functools.partial(group, False))

    tt = BWD_GROUP * CHUNK
    tile = pl.BlockSpec((1, tt, LANE), lambda hp, b, t: (b, t, hp))
    vec = pl.BlockSpec((1, LANE), lambda hp, b, t: (0, hp))
    st_spec = pl.BlockSpec((1, 1, BWD_GROUP, 2, HEAD, HEAD), lambda hp, b, t: (b, hp, t, 0, 0, 0))
    outs = pl.pallas_call(
        body, name="rwkv_scan_bwd", grid=(4, bsz, s // tt), in_specs=[tile] * 6 + [st_spec, st_spec, tile] + [vec] * 3,
        out_specs=[tile] * 6 + [vec] * 3,
        out_shape=[SDS((bsz, s, WIDTH), F32)] * 6 + [SDS((1, WIDTH), F32)] * 3,
        compiler_params=_params(("parallel", "arbitrary", "arbitrary")))(*ins, states, dstates, do3, rk, lnw, lnb)
    return outs[:6], outs[6:]


def _head(o_attn, o_rwkv, z_attn, z_rwkv, gm, x2, tgt, wua, wur, wout, g2):
    n = x2.shape[0]
    tm = 256
    nt = n // tm
    d = D_MODEL

    def body(oa_ref, or_ref, za_ref, zr_ref, gm_ref, x_ref, t_ref, wua_ref, wur_ref, wout_ref, g2_ref,
             dxo_ref, doa_ref, dor_ref, dza_ref, dzr_ref, dgm_ref, dwua_ref, dwur_ref, dwout_ref, dg2_ref, loss_ref, lacc):
        i = pl.program_id(0)
        oa, orw, za, zr = oa_ref[...], or_ref[...], za_ref[...], zr_ref[...]
        ga, gb = gm_ref[:, 0:d], gm_ref[:, d:2 * d]
        am = (oa * _silu(za)).astype(BF16)
        bm = (orw * _silu(zr)).astype(BF16)
        ya, yb = _dot(am, wua_ref[...]), _dot(bm, wur_ref[...])
        sa, sb = jax.nn.sigmoid(ga), jax.nn.sigmoid(gb)
        merged = (sa * ya + sb * yb).astype(BF16)
        out = _dot(merged, wout_ref[...])
        rs = lax.rsqrt(jnp.mean(out * out, axis=-1, keepdims=True) + RMS_EPS)
        g2 = g2_ref[...]
        err = x_ref[...] + out * rs * g2 - t_ref[...]
        lpart = jnp.sum(err * err, axis=0, keepdims=True)
        dxo = err * (1.0 / d)
        dxo_ref[...] = dxo
        dg2 = jnp.sum(dxo * out * rs, axis=0, keepdims=True)
        gd = dxo * g2
        dout = (rs * (gd - out * (rs * rs) * jnp.mean(gd * out, axis=-1, keepdims=True))).astype(BF16)
        dmerged = _dot_nt(dout, wout_ref[...])
        dwout = _dot_tn(merged, dout)
        dya, dyb = (dmerged * sa).astype(BF16), (dmerged * sb).astype(BF16)
        dgm_ref[:, 0:d] = dmerged * ya * sa * (1.0 - sa)
        dgm_ref[:, d:2 * d] = dmerged * yb * sb * (1.0 - sb)
        dam, dbm = _dot_nt(dya, wua_ref[...]), _dot_nt(dyb, wur_ref[...])
        dwua, dwur = _dot_tn(am, dya), _dot_tn(bm, dyb)
        doa_ref[...] = dam * _silu(za)
        dza_ref[...] = dam * oa * _dsilu(za)
        dor_ref[...] = dbm * _silu(zr)
        dzr_ref[...] = dbm * orw * _dsilu(zr)

        @pl.when(i == 0)
        def _():
            dwua_ref[...], dwur_ref[...], dwout_ref[...], dg2_ref[...], lacc[...] = dwua, dwur, dwout, dg2, lpart

        @pl.when(i != 0)
        def _():
            dwua_ref[...] += dwua
            dwur_ref[...] += dwur
            dwout_ref[...] += dwout
            dg2_ref[...] += dg2
            lacc[...] += lpart

        @pl.when(i == nt - 1)
        def _():
            loss_ref[...] = jnp.sum(lacc[...], axis=1, keepdims=True) * (0.5 / d)

    t512 = pl.BlockSpec((tm, WIDTH), lambda i: (i, 0))
    t1k = pl.BlockSpec((tm, d), lambda i: (i, 0))
    t2k = pl.BlockSpec((tm, 2 * d), lambda i: (i, 0))
    full = lambda r, c: pl.BlockSpec((r, c), lambda i: (0, 0))
    return pl.pallas_call(
        body, name="head_fwd_bwd", grid=(nt,),
        in_specs=[t512, t512, t512, t512, t2k, t1k, t1k, full(WIDTH, d), full(WIDTH, d), full(d, d), full(1, d)],
        out_specs=[t1k, t512, t512, t512, t512, t2k, full(WIDTH, d), full(WIDTH, d), full(d, d), full(1, d), full(1, 1)],
        out_shape=[SDS((n, d), F32)] + [SDS((n, WIDTH), F32)] * 4 + [SDS((n, 2 * d), F32), SDS((WIDTH, d), F32), SDS((WIDTH, d), F32),
                                                                    SDS((d, d), F32), SDS((1, d), F32), SDS((1, 1), F32)],
        scratch_shapes=[pltpu.VMEM((1, d), F32)],
        compiler_params=_params(("arbitrary",)))(o_attn, o_rwkv, z_attn, z_rwkv, gm, x2, tgt, wua, wur, wout, g2)


def _prenorm_bwd(dh, x2, rs, g1, dxo):
    n, d = x2.shape
    tm = 1024

    def body(dh_ref, x_ref, rs_ref, g_ref, dxo_ref, gx_ref, dg_ref):
        x, r = x_ref[...], rs_ref[...]
        gd = dh_ref[...] * g_ref[...]
        gx_ref[...] = dxo_ref[...] + r * (gd - x * (r * r) * jnp.mean(gd * x, axis=-1, keepdims=True))
        dg = jnp.sum(dh_ref[...] * x * r, axis=0, keepdims=True)

        @pl.when(pl.program_id(0) == 0)
        def _():
            dg_ref[...] = dg

        @pl.when(pl.program_id(0) != 0)
        def _():
            dg_ref[...] += dg

    t = pl.BlockSpec((tm, d), lambda i: (i, 0))
    return pl.pallas_call(
        body, name="prenorm_bwd", grid=(n // tm,),
        in_specs=[t, t, pl.BlockSpec((tm, 1), lambda i: (i, 0)), pl.BlockSpec((1, d), lambda i: (0, 0)), t],
        out_specs=[t, pl.BlockSpec((1, d), lambda i: (0, 0))], out_shape=[SDS((n, d), F32), SDS((1, d), F32)],
        compiler_params=_params(("arbitrary",)))(dh, x2, rs, g1, dxo)


def _mesh_pos():
    x, y, c = lax.axis_index("x"), lax.axis_index("y"), lax.axis_index("c")
    return 4 * x + 2 * y + c


def _coords(idx):
    return (idx // 4, (idx // 2) % 2, idx % 2)


def _exchange(srcs, to_all, name):
    n = len(srcs)

    def body(*refs):
        src_refs, dst_refs = refs[:n], refs[n:2 * n]
        send_sems, recv_sems, local_sems = refs[2 * n:]
        me = _mesh_pos()

        def piece(i, j):
            return src_refs[i] if to_all[i] else src_refs[i].at[j]

        def remote(i, off, peer, block, slot):
            return pltpu.make_async_remote_copy(src_ref=piece(i, block), dst_ref=dst_refs[i].at[slot],
                                                send_sem=send_sems.at[i, off - 1], recv_sem=recv_sems.at[i, off - 1],
                                                device_id=_coords(peer), device_id_type=MESH)

        local = [pltpu.make_async_copy(piece(i, me), dst_refs[i].at[me], local_sems.at[i]) for i in range(n)]
        for cp in local:
            cp.start()
        sends = []
        for off in range(1, N_DEV):
            to = (me + off) % N_DEV
            for i in range(n):
                sends.append(remote(i, off, to, to, me))
                sends[-1].start()
        for off in range(1, N_DEV):
            frm = (me + N_DEV - off) % N_DEV
            for i in range(n):
                remote(i, off, frm, me, frm).wait_recv()
        for cp in sends:
            cp.wait_send()
        for cp in local:
            cp.wait()

    outs = pl.pallas_call(
        body, name=name, in_specs=[pl.BlockSpec(memory_space=pltpu.HBM)] * n, out_specs=[pl.BlockSpec(memory_space=pltpu.HBM)] * n,
        out_shape=[SDS((N_DEV,) + s.shape[-2:], s.dtype) for s in srcs],
        scratch_shapes=[pltpu.SemaphoreType.DMA((n, N_DEV - 1)), pltpu.SemaphoreType.DMA((n, N_DEV - 1)), pltpu.SemaphoreType.DMA((n,))],
        compiler_params=pltpu.CompilerParams())(*srcs)
    return outs


_HBM = pl.BlockSpec(memory_space=pltpu.HBM)
_SEM = pl.BlockSpec(memory_space=pltpu.SEMAPHORE)
_EFFECT = pltpu.SideEffectType.DATAFLOW_SIDE_EFFECTING


def _send_start(src):
    def body(src_ref, land_ref, send_sems, recv_sems, src_thru, land_thru, token):
        me = _mesh_pos()
        for off in range(1, N_DEV):
            to = (me + off) % N_DEV
            pltpu.make_async_remote_copy(src_ref=src_ref.at[to], dst_ref=land_ref.at[me], send_sem=send_sems.at[off - 1],
                                         recv_sem=recv_sems.at[off - 1], device_id=_coords(to), device_id_type=MESH).start()
        token[...] = jnp.zeros_like(token)

    hbm = pltpu.HBM(src.shape, src.dtype)
    return pl.pallas_call(
        body, name="grads_start",
        out_shape=(pltpu.SemaphoreType.DMA((N_DEV - 1,)), pltpu.SemaphoreType.DMA((N_DEV - 1,)), hbm, hbm, SDS((8, LANE), BF16)),
        in_specs=(_HBM, _HBM), out_specs=(_SEM, _SEM, _HBM, _HBM, pl.BlockSpec(memory_space=pltpu.VMEM)),
        input_output_aliases={0: 2, 1: 3}, compiler_params=pltpu.CompilerParams(has_side_effects=_EFFECT),
    )(pltpu.with_memory_space_constraint(src, pltpu.HBM), pltpu.with_memory_space_constraint(jnp.zeros(src.shape, src.dtype), pltpu.HBM))


def _send_wait(send_sems, recv_sems, src_thru, land_thru, after):
    def body(src_ref, land_ref, send_sems, recv_sems, after_ref, src_dead, got_ref):
        me = _mesh_pos()
        for off in range(1, N_DEV):
            to, frm = (me + off) % N_DEV, (me + N_DEV - off) % N_DEV
            pltpu.make_async_remote_copy(src_ref=src_ref.at[to], dst_ref=land_ref.at[me], send_sem=send_sems.at[off - 1],
                                         recv_sem=recv_sems.at[off - 1], device_id=_coords(to), device_id_type=MESH).wait_send()
            pltpu.make_async_remote_copy(src_ref=src_ref.at[me], dst_ref=land_ref.at[frm], send_sem=send_sems.at[off - 1],
                                         recv_sem=recv_sems.at[off - 1], device_id=_coords(frm), device_id_type=MESH).wait_recv()

    hbm = pltpu.HBM(src_thru.shape, src_thru.dtype)
    return pl.pallas_call(
        body, name="grads_wait", out_shape=(hbm, hbm), in_specs=(_HBM, _HBM, _SEM, _SEM, pl.BlockSpec(memory_space=pl.ANY)),
        out_specs=(_HBM, _HBM), input_output_aliases={0: 0, 1: 1}, compiler_params=pltpu.CompilerParams(has_side_effects=_EFFECT),
    )(src_thru, land_thru, send_sems, recv_sems, after)[1]


def _gather(srcs, name):
    n = len(srcs)

    def body(*refs):
        src_refs, dst_refs = refs[:n], refs[n:2 * n]
        send_sems, recv_sems, local_sems = refs[2 * n:]
        x, y, c = lax.axis_index("x"), lax.axis_index("y"), lax.axis_index("c")
        me, sibling = (x, y, c), (x, y, 1 - c)
        chips = [(1 - x, y), (x, 1 - y), (1 - x, 1 - y)]

        def slot(i, dev):
            return dst_refs[i].at[4 * dev[0] + 2 * dev[1] + dev[2]]

        def copy(i, k, block, to, own=False):
            return pltpu.make_async_remote_copy(src_ref=src_refs[i] if own else slot(i, block), dst_ref=slot(i, block),
                                                send_sem=send_sems.at[i, k], recv_sem=recv_sems.at[i, k],
                                                device_id=to, device_id_type=MESH)

        local = [pltpu.make_async_copy(src_refs[i], slot(i, me), local_sems.at[i]) for i in range(n)]
        for cp in local:
            cp.start()
        sends = []
        for i in range(n):
            sends.append(copy(i, 0, me, sibling, own=True))
            sends += [copy(i, 1 + j, me, (*chip, c), own=True) for j, chip in enumerate(chips)]
        for cp in sends:
            cp.start()
        for j, chip in enumerate(chips):
            for i in range(n):
                copy(i, 1 + j, (*chip, c), me).wait_recv()
                sends.append(copy(i, 4 + j, (*chip, c), sibling))
                sends[-1].start()
        for i in range(n):
            copy(i, 0, sibling, me).wait_recv()
            for j, chip in enumerate(chips):
                copy(i, 4 + j, (*chip, 1 - c), me).wait_recv()
        for cp in sends:
            cp.wait_send()
        for cp in local:
            cp.wait()

    return pl.pallas_call(
        body, name=name, in_specs=[pl.BlockSpec(memory_space=pltpu.HBM)] * n, out_specs=[pl.BlockSpec(memory_space=pltpu.HBM)] * n,
        out_shape=[SDS((N_DEV,) + s.shape, s.dtype) for s in srcs],
        scratch_shapes=[pltpu.SemaphoreType.DMA((n, N_DEV - 1)), pltpu.SemaphoreType.DMA((n, N_DEV - 1)), pltpu.SemaphoreType.DMA((n,))],
        compiler_params=pltpu.CompilerParams())(*srcs)


def _adamw(parts, w, m, v, tr, name, own=None):
    rows, cols = w.shape
    c1, c2 = 1.0 - ADAM_B1 ** ADAM_STEP, 1.0 - ADAM_B2 ** ADAM_STEP

    def body(p_ref, *refs):
        w_ref, m_ref, v_ref, g_ref, d_ref, nm_ref, nv_ref = refs[-7:]
        me = _mesh_pos()

        def part(j):
            return p_ref[j] if own is None else jnp.where(me == j, refs[0][...], p_ref[j])

        g = part(0).astype(F32)
        for j in range(1, N_DEV):
            g = g + part(j).astype(F32)
        nm = ADAM_B1 * m_ref[...] + (1.0 - ADAM_B1) * g
        nv = ADAM_B2 * v_ref[...] + (1.0 - ADAM_B2) * jnp.square(g)
        g_ref[...] = g
        nm_ref[...] = nm
        nv_ref[...] = nv
        d_ref[...] = -ADAM_LR * ((nm / c1) / (jnp.sqrt(nv / c2) + ADAM_EPS) + ADAM_WD * w_ref[...])

    t = pl.BlockSpec((tr, cols), lambda i: (i, 0))
    extra = [] if own is None else [own]
    return pl.pallas_call(
        body, name=name, grid=(rows // tr,), in_specs=[pl.BlockSpec((N_DEV, tr, cols), lambda i: (0, i, 0))] + [t] * (3 + len(extra)),
        out_specs=[t] * 4, out_shape=[SDS((rows, cols), F32)] * 4, compiler_params=_params(("parallel",)))(parts, *extra, w, m, v)


SHARDED = (("w_in", D_MODEL, IN_COLS // N_DEV, True, 128), ("w_up_attn", WIDTH, D_MODEL // N_DEV, True, WIDTH),
           ("w_up_rwkv", WIDTH, D_MODEL // N_DEV, True, WIDTH), ("w_out", D_MODEL // N_DEV, D_MODEL, False, D_MODEL // N_DEV),
           ("rwkv_w_up", LORA, WIDTH // N_DEV, True, LORA), ("rwkv_a_up", LORA, WIDTH // N_DEV, True, LORA))
LOSS_SLOT = sum(n for _, n in SMALL)


def _pack_small(small, extra=None):
    flat = [small[n].reshape(-1).astype(F32) for n, _ in SMALL]
    flat.append(jnp.zeros((1,), F32) if extra is None else extra.reshape(1))
    flat.append(jnp.zeros((SMALL_ROWS * LANE - LOSS_SLOT - 1,), F32))
    return jnp.concatenate(flat).reshape(SMALL_ROWS, LANE)


def _unpack_small(packed, shapes):
    flat = packed.reshape(-1)
    out, off = {}, 0
    for n, cnt in SMALL:
        out[n] = flat[off:off + cnt].reshape(shapes[n])
        off += cnt
    return out, flat[LOSS_SLOT]


def _whole(gathered, by_cols):
    if not by_cols:
        return gathered.reshape(-1, gathered.shape[-1])
    return gathered.transpose(1, 0, 2).reshape(gathered.shape[1], -1)


def _per_owner(full, by_cols):
    if not by_cols:
        return full.reshape(N_DEV, -1, full.shape[-1])
    return full.reshape(full.shape[0], N_DEV, -1).transpose(1, 0, 2)


def _local_step(x, loss_target, sm, wts):
    bsz, s, d = x.shape
    n = bsz * s
    x2, tgt = x.reshape(n, d), loss_target.reshape(n, d)
    bidx = jnp.asarray(_bucket_tables())
    w_in = wts["w_in"]
    segs = (("qkv", 0, QKV_COLS, 1536), ("za", OFF_ZA, WIDTH, 512), ("pr", OFF_PR, PR_COLS, PR_COLS), ("zr", OFF_ZR, WIDTH, 512),
            ("gm", OFF_GM, 2 * D_MODEL, 512))

    h, rs = _prenorm(x2, sm["pre_norm_gain"])
    proj = {nm: _mm(h, w_in[:, off:off + cnt], tn, "proj_" + nm) for nm, off, cnt, tn in segs}
    qkv3 = proj["qkv"].reshape(bsz, s, QKV_COLS)
    pr3 = proj["pr"].reshape(bsz, s, PR_COLS)

    o_attn, lse = _attn_fwd(qkv3, sm["rel_bias"], bidx)
    rk = sm["rwkv_r_k"].reshape(1, WIDTH)
    pre_args = (sm["rwkv_shift_mix"], sm["rwkv_w0"], wts["rwkv_w_up"], sm["rwkv_a0"], wts["rwkv_a_up"], sm["rwkv_k_k"], sm["rwkv_k_a"])
    scan_in = _rwkv_pre(pr3, *pre_args)
    o_rwkv, states, consts = _rwkv_scan(scan_in, rk, sm["rwkv_ln_w"], sm["rwkv_ln_b"])

    (dxo, do_attn, do_rwkv, dza, dzr, dgm, g_wua, g_wur, g_wout, g_post, loss) = _head(
        o_attn.reshape(n, WIDTH), o_rwkv.reshape(n, WIDTH), proj["za"], proj["zr"], proj["gm"], x2, tgt,
        wts["w_up_attn"], wts["w_up_rwkv"], wts["w_out"], sm["post_norm_gain"])

    dqkv, dbias = _attn_bwd(qkv3, o_attn, lse, do_attn.reshape(bsz, s, WIDTH), sm["rel_bias"], bidx)
    g_bias = _bias_grad(dbias, bidx)[:, :N_BUCKET].T

    scan_cots, (g_rk, g_lnw, g_lnb) = _rwkv_scan_bwd(scan_in, states, consts, do_rwkv.reshape(bsz, s, WIDTH), rk, sm["rwkv_ln_w"],
                                                     sm["rwkv_ln_b"])
    dprs, g_mix, g_w0, g_wup, g_a0, g_aup, g_kk, g_ka = _rwkv_pre_bwd(pr3, scan_cots, *pre_args)
    dpr = _shift_bwd(dprs, sm["rwkv_shift_mix"]).reshape(n, PR_COLS)

    dsegs = [(dqkv.reshape(9, n, WIDTH), 0, QKV_COLS, WIDTH), (dza, OFF_ZA, WIDTH, WIDTH), (dpr, OFF_PR, PR_COLS, PR_COLS),
             (dzr, OFF_ZR, WIDTH, WIDTH), (dgm, OFF_GM, 2 * D_MODEL, D_MODEL)]
    g_win = jnp.concatenate([_mm_tn(h, t, tn, "gw_in_%d" % j) for j, (t, _, _, tn) in enumerate(dsegs)], axis=1)
    blocks = _per_owner(g_win, True).astype(BF16)
    own = lax.dynamic_index_in_dim(blocks, 4 * lax.axis_index("x") + 2 * lax.axis_index("y") + lax.axis_index("c"), 0, keepdims=False)
    send_sems, recv_sems, blocks_thru, land_thru, token = _send_start(blocks)
    dh = None
    for j, (t, off, cnt, _) in enumerate(dsegs):
        dh = _mm_nt_acc(t, w_in[:, off:off + cnt] + token[0, 0], dh, "dh_%d" % j)
    grad_x, g_pre = _prenorm_bwd(dh, x2, rs, sm["pre_norm_gain"], dxo)
    landed = _send_wait(send_sems, recv_sems, blocks_thru, land_thru, g_pre)

    full = {"w_up_attn": g_wua, "w_up_rwkv": g_wur, "w_out": g_wout, "rwkv_w_up": g_wup, "rwkv_a_up": g_aup}
    small = {"pre_norm_gain": g_pre, "rel_bias": g_bias, "rwkv_shift_mix": g_mix, "rwkv_w0": g_w0, "rwkv_a0": g_a0, "rwkv_k_k": g_kk,
             "rwkv_k_a": g_ka, "rwkv_r_k": g_rk, "rwkv_ln_w": g_lnw, "rwkv_ln_b": g_lnb, "post_norm_gain": g_post}
    return loss[0, 0], grad_x.reshape(bsz, s, d), (landed, own), full, small


def kernel(x, pre_norm_gain, w_in, rel_bias, rwkv_shift_mix, rwkv_w0, rwkv_w_up, rwkv_a0, rwkv_a_up, rwkv_k_k, rwkv_k_a, rwkv_r_k, rwkv_ln_w, rwkv_ln_b, w_up_attn, w_up_rwkv, w_out, post_norm_gain, loss_target, m_pre_norm_gain, m_w_in, m_rel_bias, m_rwkv_shift_mix, m_rwkv_w0, m_rwkv_w_up, m_rwkv_a0, m_rwkv_a_up, m_rwkv_k_k, m_rwkv_k_a, m_rwkv_r_k, m_rwkv_ln_w, m_rwkv_ln_b, m_w_up_attn, m_w_up_rwkv, m_w_out, m_post_norm_gain, v_pre_norm_gain, v_w_in, v_rel_bias, v_rwkv_shift_mix, v_rwkv_w0, v_rwkv_w_up, v_rwkv_a0, v_rwkv_a_up, v_rwkv_k_k, v_rwkv_k_a, v_rwkv_r_k, v_rwkv_ln_w, v_rwkv_ln_b, v_w_up_attn, v_w_up_rwkv, v_w_out, v_post_norm_gain):
    names = [n for n, *_ in SHARDED] + [n for n, _ in SMALL]
    loc = dict(locals())
    w = {n: loc[n] for n in names}
    m = {n: loc["m_" + n] for n in names}
    v = {n: loc["v_" + n] for n in names}
    shapes = {n: w[n].shape for n in names}
    order = ["pre_norm_gain", "w_in", "rel_bias", "rwkv_shift_mix", "rwkv_w0", "rwkv_w_up", "rwkv_a0", "rwkv_a_up", "rwkv_k_k", "rwkv_k_a",
             "rwkv_r_k", "rwkv_ln_w", "rwkv_ln_b", "w_up_attn", "w_up_rwkv", "w_out", "post_norm_gain"]
    shard2d = lambda t, n, r, c: t[n].reshape(r, c)

    gathered = _gather([shard2d(w, n, r, c).astype(BF16) for n, r, c, _, _ in SHARDED], "gather_weights")
    wts = {n: _whole(g, by_cols) for (n, _, _, by_cols, _), g in zip(SHARDED, gathered)}

    loss, grad_x, (win_landed, win_own), full, small = _local_step(x, loss_target, w, wts)
    rest = SHARDED[1:]
    parts = _exchange([_per_owner(full[n], by_cols).astype(BF16) for n, _, _, by_cols, _ in rest] + [_pack_small(small, loss)],
                      [False] * len(rest) + [True], "exchange_grads")

    outs = [{}, {}, {}, {}]
    for (n, r, c, _, tr), p in zip(SHARDED, [win_landed] + list(parts)):
        res = _adamw(p, shard2d(w, n, r, c), shard2d(m, n, r, c), shard2d(v, n, r, c), tr, "adamw_" + n,
                     own=win_own if n == "w_in" else None)
        for o, t in zip(outs, res):
            o[n] = t.reshape(shapes[n])
    res = _adamw(parts[-1], _pack_small(w), _pack_small(m), _pack_small(v), SMALL_ROWS, "adamw_small")
    for o, t in zip(outs, res):
        o.update(_unpack_small(t, shapes)[0])
    loss = _unpack_small(res[0], shapes)[1]
    return (loss, grad_x, *[o[n] for o in outs for n in order])
```

```python
import functools
import math

import numpy as np
import jax
import jax.numpy as jnp
from jax import lax
from jax.experimental import pallas as pl
from jax.experimental.pallas import tpu as pltpu

F32, BF16 = jnp.float32, jnp.bfloat16
SDS = jax.ShapeDtypeStruct
HI = lax.Precision.HIGHEST
HI3 = lax.Precision.HIGH
MESH = pl.DeviceIdType.MESH

N_DEV = 8
D_MODEL = 1024
HEAD = 64
N_HEAD = 8
WIDTH = N_HEAD * HEAD
DILATIONS = (1, 4, 16)
QB = 128
N_BUCKET = 32
MAX_DIST = 2048
LORA = 64
QKV_COLS = 9 * WIDTH
PR_COLS = 3 * WIDTH + 2 * LORA
IN_COLS = QKV_COLS + WIDTH + PR_COLS + WIDTH + 2 * D_MODEL
OFF_ZA, OFF_PR, OFF_ZR, OFF_GM = QKV_COLS, QKV_COLS + WIDTH, QKV_COLS + WIDTH + PR_COLS, QKV_COLS + 2 * WIDTH + PR_COLS
RMS_EPS = 1e-6
GN_EPS = 64e-5
SCALE = 1.0 / math.sqrt(HEAD)
CHUNK = 64
CHUNK_GROUP = 8
BWD_GROUP = 8
EARLY = 8
NEG = -1e30
LANE = 128

ADAM_LR, ADAM_B1, ADAM_B2, ADAM_EPS, ADAM_WD, ADAM_STEP = 0.001, 0.9, 0.999, 1e-08, 0.01, 10

VMEM_LIMIT = 56 * 1024 * 1024

SMALL = (("pre_norm_gain", 1024), ("rel_bias", 768), ("rwkv_shift_mix", 1664), ("rwkv_w0", 512), ("rwkv_a0", 512),
         ("rwkv_k_k", 512), ("rwkv_k_a", 512), ("rwkv_r_k", 512), ("rwkv_ln_w", 512), ("rwkv_ln_b", 512),
         ("post_norm_gain", 1024))
SMALL_ROWS = 64


def _params(sem=None):
    return pltpu.CompilerParams(dimension_semantics=sem, vmem_limit_bytes=VMEM_LIMIT)


def _dot(a, b):
    return jnp.dot(a, b, preferred_element_type=F32)


def _dot_nt(a, b):
    return lax.dot_general(a, b, (((1,), (1,)), ((), ())), preferred_element_type=F32)


def _dot_tn(a, b):
    return lax.dot_general(a, b, (((0,), (0,)), ((), ())), preferred_element_type=F32)


@jax.custom_vjp
def _bdot(a, b):
    return _dot(a.astype(BF16), b.astype(BF16))


def _bdot_fwd(a, b):
    return _bdot(a, b), (a, b)


def _bdot_bwd(res, g):
    a, b = res
    gb = g.astype(BF16)
    return _dot_nt(gb, b.astype(BF16)), _dot_tn(a.astype(BF16), gb)


_bdot.defvjp(_bdot_fwd, _bdot_bwd)


def _silu(z):
    return z * jax.nn.sigmoid(z)


def _dsilu(z):
    s = jax.nn.sigmoid(z)
    return s * (1.0 + z * (1.0 - s))


def _softplus(x):
    return jnp.maximum(x, 0.0) + jnp.log(1.0 + jnp.exp(-jnp.abs(x)))


def _bucket_tables():
    qi = np.arange(QB)[:, None] + QB
    ki = np.arange(2 * QB)[None, :]
    rel = np.maximum(qi - ki, 0)
    out = []
    for d in DILATIONS:
        dist = rel * d
        max_exact = N_BUCKET // 2
        ratio = np.log(np.maximum(dist, 1).astype(np.float32) / max_exact) / np.float32(math.log(MAX_DIST / max_exact))
        large = max_exact + (ratio * (N_BUCKET - max_exact)).astype(np.int32)
        large = np.minimum(large, N_BUCKET - 1)
        out.append(np.where(dist < max_exact, dist, large).astype(np.int32))
    return np.stack(out)


def _prenorm(x2, g):
    n, d = x2.shape
    tm = 1024

    def body(x_ref, g_ref, h_ref, rs_ref):
        x = x_ref[...]
        rs = lax.rsqrt(jnp.mean(x * x, axis=-1, keepdims=True) + RMS_EPS)
        h_ref[...] = (x * rs * g_ref[...]).astype(BF16)
        rs_ref[...] = rs

    return pl.pallas_call(
        body, name="prenorm", grid=(n // tm,),
        in_specs=[pl.BlockSpec((tm, d), lambda i: (i, 0)), pl.BlockSpec((1, d), lambda i: (0, 0))],
        out_specs=[pl.BlockSpec((tm, d), lambda i: (i, 0)), pl.BlockSpec((tm, 1), lambda i: (i, 0))],
        out_shape=[SDS((n, d), BF16), SDS((n, 1), F32)], compiler_params=_params(("parallel",)))(x2, g)


def _mm(a, b, tn, name):
    m, k = a.shape
    n = b.shape[1]
    tm = 1024

    def body(a_ref, b_ref, o_ref):
        o_ref[...] = _dot(a_ref[...], b_ref[...])

    return pl.pallas_call(
        body, name=name, grid=(n // tn, m // tm),
        in_specs=[pl.BlockSpec((tm, k), lambda j, i: (i, 0)), pl.BlockSpec((k, tn), lambda j, i: (0, j))],
        out_specs=pl.BlockSpec((tm, tn), lambda j, i: (i, j)),
        out_shape=SDS((m, n), F32), compiler_params=_params(("parallel", "parallel")))(a, b)


def _mm_nt_acc(a, b, acc, name):
    split = a.ndim == 3
    m = a.shape[-2]
    k = b.shape[1]
    d = b.shape[0]
    tm = 1024
    per = 3 if split else 1
    seg = a.shape[2] if split else 0
    tk = per * seg if split else (k if k <= 2048 else 1536)
    have_acc = acc is not None

    def body(*refs):
        if have_acc:
            a_ref, b_ref, c_ref, o_ref = refs
        else:
            a_ref, b_ref, o_ref = refs
        if split:
            r = sum(_dot_nt(a_ref[j].astype(BF16), b_ref[:, seg * j:seg * (j + 1)]) for j in range(per))
        else:
            r = _dot_nt(a_ref[...].astype(BF16), b_ref[...])

        @pl.when(pl.program_id(1) == 0)
        def _():
            o_ref[...] = r + c_ref[...] if have_acc else r

        @pl.when(pl.program_id(1) != 0)
        def _():
            o_ref[...] += r

    a_spec = pl.BlockSpec((per, tm, seg), lambda i, j: (j, i, 0)) if split else pl.BlockSpec((tm, tk), lambda i, j: (i, j))
    in_specs = [a_spec, pl.BlockSpec((d, tk), lambda i, j: (0, j))]
    args = [a, b]
    if have_acc:
        in_specs.append(pl.BlockSpec((tm, d), lambda i, j: (i, 0)))
        args.append(acc)
    return pl.pallas_call(
        body, name=name, grid=(m // tm, k // tk), in_specs=in_specs, out_specs=pl.BlockSpec((tm, d), lambda i, j: (i, 0)),
        out_shape=SDS((m, d), F32), compiler_params=_params(("parallel", "arbitrary")))(*args)


def _mm_tn(a, b, tn, name):
    split = b.ndim == 3
    m, k1 = a.shape
    per = 3 if split else 1
    seg = b.shape[2] if split else tn
    tn = per * seg
    n2 = b.shape[0] * seg if split else b.shape[1]
    tm = 1024

    def body(a_ref, b_ref, o_ref):
        first = pl.program_id(1) == 0
        for j in range(per):
            r = _dot_tn(a_ref[...], (b_ref[j] if split else b_ref[...]).astype(BF16))
            cols = slice(seg * j, seg * (j + 1))

            @pl.when(first)
            def _(r=r, cols=cols):
                o_ref[:, cols] = r

            @pl.when(jnp.logical_not(first))
            def _(r=r, cols=cols):
                o_ref[:, cols] += r

    b_spec = pl.BlockSpec((per, tm, seg), lambda j, i: (j, i, 0)) if split else pl.BlockSpec((tm, tn), lambda j, i: (i, j))
    return pl.pallas_call(
        body, name=name, grid=(n2 // tn, m // tm),
        in_specs=[pl.BlockSpec((tm, k1), lambda j, i: (i, 0)), b_spec],
        out_specs=pl.BlockSpec((k1, tn), lambda j, i: (0, j)),
        out_shape=SDS((k1, n2), F32), compiler_params=_params(("parallel", "arbitrary")))(a, b)


def _ds(start, d):
    return pl.ds(start, QB) if d == 1 else pl.ds(start, QB, stride=d)


def _fill_bias(tab_ref, bidx_ref, bias_sc, hp):
    for g in range(3):
        bi = bidx_ref[g]
        for h in range(2):
            acc = jnp.zeros((QB, 2 * QB), F32)
            for j in range(N_BUCKET):
                acc = jnp.where(bi == j, tab_ref[j, g * N_HEAD + hp * 2 + h], acc)
            bias_sc[g * 2 + h] = acc


def _block_starts(it, d, nb):
    rho = it // nb
    n = it % nb
    st = rho + d * QB * n
    stp = rho + d * QB * jnp.maximum(n - 1, 0)
    return st, stp, n > 0


ATTN_BLOCKS = 4


def _bdot3(a, b, dims):
    return lax.dot_general(a, b, (dims, ((0,), (0,))), preferred_element_type=F32)


def _attn_operands(q_ref, k_ref, v_ref, bias_sc, g, d, nb, it0):
    ii = lax.broadcasted_iota(jnp.int32, (QB, 2 * QB), 0)
    cc = lax.broadcasted_iota(jnp.int32, (QB, 2 * QB), 1)
    qs, ks, vs, pens, starts = [], [], [], [], []
    for u in range(ATTN_BLOCKS):
        st, stp, hasprev = _block_starts(it0 + u, d, nb)
        qf = q_ref[0, _ds(st, d), :]
        kf = jnp.concatenate([k_ref[0, _ds(stp, d), :], k_ref[0, _ds(st, d), :]], axis=0)
        vf = jnp.concatenate([v_ref[0, _ds(stp, d), :], v_ref[0, _ds(st, d), :]], axis=0)
        own = jnp.logical_and(cc >= QB, ii >= cc - QB)
        prev = jnp.logical_and(jnp.logical_and(cc < QB, cc >= ii), hasprev)
        pen = jnp.where(jnp.logical_or(own, prev), 0.0, NEG)
        for h in range(2):
            sl = slice(HEAD * h, HEAD * h + HEAD)
            qs.append(qf[:, sl])
            ks.append(kf[:, sl])
            vs.append(vf[:, sl])
            pens.append(pen + bias_sc[g * 2 + h])
        starts.append((st, stp))
    return _stack(qs).astype(BF16), _stack(ks).astype(BF16), _stack(vs).astype(BF16), _stack(pens), starts


def _heads(x, u):
    return jnp.concatenate([x[2 * u], x[2 * u + 1]], axis=1)


def _attn_fwd(qkv3, rel_bias, bidx):
    bsz, s, _ = qkv3.shape
    rt = 256

    def body(tab_ref, bidx_ref, *refs):
        q_refs, k_refs, v_refs = refs[0:3], refs[3:6], refs[6:9]
        o_ref, lse_ref = refs[9:11]
        bias_sc, num_sc, den_sc, m_sc = refs[11:]
        pl.when(pl.program_id(1) == 0)(lambda: _fill_bias(tab_ref, bidx_ref, bias_sc, pl.program_id(0)))
        for g, d in enumerate(DILATIONS):
            nb = s // (QB * d)

            def blk(it, c, g=g, d=d, nb=nb):
                q, k, v, bias, starts = _attn_operands(q_refs[g], k_refs[g], v_refs[g], bias_sc, g, d, nb, it * ATTN_BLOCKS)
                sc = _bdot3(q, k, ((2,), (2,))) * SCALE + bias
                m = jnp.max(sc, axis=-1, keepdims=True)
                p = jnp.exp(sc - m)
                den = jnp.sum(p, axis=-1, keepdims=True)
                num = _bdot3(p.astype(BF16), v, ((2,), (1,)))
                den, m = jnp.broadcast_to(den, num.shape), jnp.broadcast_to(m, num.shape)
                for u, (st, _) in enumerate(starts):
                    num_sc[g, _ds(st, d), :] = _heads(num, u)
                    den_sc[g, _ds(st, d), :] = _heads(den, u)
                    m_sc[g, _ds(st, d), :] = _heads(m, u)
                return c

            lax.fori_loop(0, s // QB // ATTN_BLOCKS, blk, 0)

        def merge(i, c):
            rows = pl.ds(pl.multiple_of(i * rt, rt), rt)
            m0, m1, m2 = m_sc[0, rows, :], m_sc[1, rows, :], m_sc[2, rows, :]
            mall = jnp.maximum(jnp.maximum(m0, m1), m2)
            w0, w1, w2 = jnp.exp(m0 - mall), jnp.exp(m1 - mall), jnp.exp(m2 - mall)
            num = w0 * num_sc[0, rows, :] + w1 * num_sc[1, rows, :] + w2 * num_sc[2, rows, :]
            den = w0 * den_sc[0, rows, :] + w1 * den_sc[1, rows, :] + w2 * den_sc[2, rows, :]
            o_ref[0, rows, :] = num / den
            lse_ref[0, rows, :] = mall + jnp.log(den)
            return c

        lax.fori_loop(0, s // rt, merge, 0)

    col = lambda w, g: (lambda hp, b: (b, 0, (w * 3 + g) * 4 + hp))
    in_specs = [pl.BlockSpec(memory_space=pltpu.SMEM), pl.BlockSpec((3, QB, 2 * QB), lambda hp, b: (0, 0, 0))]
    in_specs += [pl.BlockSpec((1, s, LANE), col(w, g)) for w in range(3) for g in range(3)]
    out_spec = pl.BlockSpec((1, s, LANE), lambda hp, b: (b, 0, hp))
    return pl.pallas_call(
        body, name="attn_fwd", grid=(4, bsz), in_specs=in_specs, out_specs=[out_spec, out_spec],
        out_shape=[SDS((bsz, s, WIDTH), F32), SDS((bsz, s, WIDTH), F32)],
        scratch_shapes=[pltpu.VMEM((6, QB, 2 * QB), F32), pltpu.VMEM((3, s, LANE), F32), pltpu.VMEM((3, s, LANE), F32),
                        pltpu.VMEM((3, s, LANE), F32)],
        compiler_params=_params(("arbitrary", "arbitrary")))(rel_bias, bidx, *([qkv3] * 9))


def _attn_bwd(qkv3, o3, lse3, do3, rel_bias, bidx):
    bsz, s, _ = qkv3.shape
    rt = 256

    def body(tab_ref, bidx_ref, *refs):
        q_refs, k_refs, v_refs = refs[0:3], refs[3:6], refs[6:9]
        o_ref, lse_ref, do_ref, dqkv_ref, db_ref, bias_sc, delta_sc, acc_sc = refs[9:]
        dq_refs, dk_refs, dv_refs = ([acc_sc.at[w * 3 + g] for g in range(3)] for w in range(3))

        @pl.when(pl.program_id(1) == 0)
        def _():
            _fill_bias(tab_ref, bidx_ref, bias_sc, pl.program_id(0))
            db_ref[...] = jnp.zeros_like(db_ref)

        def prep(i, c):
            rows = pl.ds(pl.multiple_of(i * rt, rt), rt)
            prod = do_ref[0, rows, :] * o_ref[0, rows, :]
            d0 = jnp.sum(prod[:, :HEAD], axis=-1, keepdims=True)
            d1 = jnp.sum(prod[:, HEAD:], axis=-1, keepdims=True)
            delta_sc[rows, :] = jnp.concatenate([jnp.broadcast_to(d0, (rt, HEAD)), jnp.broadcast_to(d1, (rt, HEAD))], axis=1)
            z = jnp.zeros((rt, LANE), F32)
            for g in range(3):
                dk_refs[g][0, rows, :] = z
                dv_refs[g][0, rows, :] = z
            return c

        lax.fori_loop(0, s // rt, prep, 0)
        for g, d in enumerate(DILATIONS):
            nb = s // (QB * d)

            def blk(it, c, g=g, d=d, nb=nb):
                q, k, v, bias, starts = _attn_operands(q_refs[g], k_refs[g], v_refs[g], bias_sc, g, d, nb, it * ATTN_BLOCKS)
                dos, lses, deltas = [], [], []
                for st, _ in starts:
                    dof, lsef, delf = do_ref[0, _ds(st, d), :], lse_ref[0, _ds(st, d), :], delta_sc[_ds(st, d), :]
                    for h in range(2):
                        dos.append(dof[:, HEAD * h:HEAD * h + HEAD])
                        lses.append(lsef[:, HEAD * h:HEAD * h + 1])
                        deltas.append(delf[:, HEAD * h:HEAD * h + 1])
                do, lse, delta = _stack(dos).astype(BF16), _stack(lses), _stack(deltas)
                p = jnp.exp(_bdot3(q, k, ((2,), (2,))) * SCALE + bias - lse)
                dv = _bdot3(p.astype(BF16), do, ((1,), (1,)))
                ds = p * (_bdot3(do, v, ((2,), (2,))) - delta)
                dsb = ds.astype(BF16)
                dq = _bdot3(dsb, k, ((2,), (1,))) * SCALE
                dk = _bdot3(dsb, q, ((1,), (1,))) * SCALE
                for h in range(2):
                    db_ref[0, g * 2 + h] += sum(ds[2 * u + h] for u in range(ATTN_BLOCKS))
                for u, (st, stp) in enumerate(starts):
                    dq_refs[g][0, _ds(st, d), :] = _heads(dq, u)
                    dk_refs[g][0, _ds(stp, d), :] += _heads(dk[:, :QB], u)
                    dv_refs[g][0, _ds(stp, d), :] += _heads(dv[:, :QB], u)
                    dk_refs[g][0, _ds(st, d), :] += _heads(dk[:, QB:], u)
                    dv_refs[g][0, _ds(st, d), :] += _heads(dv[:, QB:], u)
                return c

            lax.fori_loop(0, s // QB // ATTN_BLOCKS, blk, 0)

        def flush(i, c):
            rows = pl.ds(pl.multiple_of(i * rt, rt), rt)
            for j in range(9):
                dqkv_ref[j, 0, rows, :] = acc_sc[j, 0, rows, :].astype(BF16)
            return c

        lax.fori_loop(0, s // rt, flush, 0)

    col = lambda w, g: (lambda hp, b: (b, 0, (w * 3 + g) * 4 + hp))
    blk_spec = pl.BlockSpec((1, s, LANE), lambda hp, b: (b, 0, hp))
    in_specs = [pl.BlockSpec(memory_space=pltpu.SMEM), pl.BlockSpec((3, QB, 2 * QB), lambda hp, b: (0, 0, 0))]
    in_specs += [pl.BlockSpec((1, s, LANE), col(w, g)) for w in range(3) for g in range(3)]
    in_specs += [blk_spec] * 3
    out_specs = [pl.BlockSpec((9, 1, s, LANE), lambda hp, b: (0, b, 0, hp)), pl.BlockSpec((1, 6, QB, 2 * QB), lambda hp, b: (hp, 0, 0, 0))]
    out_shape = [SDS((9, bsz, s, WIDTH), BF16), SDS((4, 6, QB, 2 * QB), F32)]
    return pl.pallas_call(
        body, name="attn_bwd", grid=(4, bsz), in_specs=in_specs, out_specs=out_specs, out_shape=out_shape,
        scratch_shapes=[pltpu.VMEM((6, QB, 2 * QB), F32), pltpu.VMEM((s, LANE), F32), pltpu.VMEM((9, 1, s, LANE), F32)],
        compiler_params=_params(("parallel", "arbitrary")))(rel_bias, bidx, *([qkv3] * 9), o3, lse3, do3)


def _bias_grad(dbias, bidx):
    def body(db_ref, bidx_ref, o_ref):
        lane = lax.broadcasted_iota(jnp.int32, (1, LANE), 1)
        for g in range(3):
            bi = bidx_ref[g]
            for hp in range(4):
                for h in range(2):
                    mat = db_ref[hp, g * 2 + h]
                    row = jnp.zeros((1, LANE), F32)
                    for j in range(N_BUCKET):
                        part = jnp.sum(jnp.where(bi == j, mat, 0.0), axis=0, keepdims=True)
                        row = jnp.where(lane == j, jnp.sum(part, axis=1, keepdims=True), row)
                    hd = g * N_HEAD + hp * 2 + h
                    o_ref[hd:hd + 1, :] = row

    return pl.pallas_call(body, name="bias_grad", out_shape=SDS((3 * N_HEAD, LANE), F32), compiler_params=_params())(dbias, bidx)


def _pre_fn(r, k0, v, wl, al, w0, wup, a0, aup, kk_, ka_):
    u = w0 + _bdot(jnp.tanh(wl), wup)
    lw = -jnp.exp(-_softplus(-u) - 0.5)
    a = jax.nn.sigmoid(a0 + _bdot(al, aup))
    kkraw = k0 * kk_
    k = k0 * (1.0 + (a - 1.0) * ka_)
    return r, lw, k, v, kkraw, a


PRE_SPLIT = (0, WIDTH, 2 * WIDTH, 3 * WIDTH, 3 * WIDTH + LORA, 3 * WIDTH + 2 * LORA)


def _pre_pieces(prs):
    return [prs[:, a:b] for a, b in zip(PRE_SPLIT[:-1], PRE_SPLIT[1:])]


PRE_TT = 512


def _shifted(pr_ref, edge_ref, first, back):
    pr = pr_ref[0]
    tt = pr.shape[0]
    row = lax.broadcasted_iota(jnp.int32, (tt, 1), 0)
    if back:
        edge = jnp.where(first, 0.0, edge_ref[0, 7:8, :])
        return jnp.where(row == 0, edge, pltpu.roll(pr, 1, axis=0))
    edge = jnp.where(first, 0.0, edge_ref[0, 0:1, :])
    return jnp.where(row == tt - 1, edge, pltpu.roll(pr, tt - 1, axis=0))


def _rwkv_pre(pr3, mix, w0, wup, a0, aup, kk_, ka_):
    bsz, s, _ = pr3.shape
    tt = PRE_TT

    def body(pr_ref, edge_ref, mix_ref, w0_ref, wup_ref, a0_ref, aup_ref, kk_ref, ka_ref, *outs):
        pr = pr_ref[0]
        prev = _shifted(pr_ref, edge_ref, pl.program_id(1) == 0, True)
        prs = pr + (prev - pr) * mix_ref[...]
        vals = _pre_fn(*_pre_pieces(prs), w0_ref[...], wup_ref[...].astype(F32), a0_ref[...], aup_ref[...].astype(F32), kk_ref[...],
                       ka_ref[...])
        for o, val in zip(outs, vals):
            o[0] = val

    vec = lambda n: pl.BlockSpec((1, n), lambda b, i: (0, 0))
    mat = pl.BlockSpec((LORA, WIDTH), lambda b, i: (0, 0))
    in_specs = [pl.BlockSpec((1, tt, PR_COLS), lambda b, i: (b, i, 0)),
                pl.BlockSpec((1, 8, PR_COLS), lambda b, i: (b, jnp.maximum(i * (tt // 8) - 1, 0), 0)),
                vec(PR_COLS), vec(WIDTH), mat, vec(WIDTH), mat, vec(WIDTH), vec(WIDTH)]
    out_spec = pl.BlockSpec((1, tt, WIDTH), lambda b, i: (b, i, 0))
    return pl.pallas_call(
        body, name="rwkv_pre", grid=(bsz, s // tt), in_specs=in_specs, out_specs=[out_spec] * 6,
        out_shape=[SDS((bsz, s, WIDTH), F32)] * 6, compiler_params=_params(("parallel", "parallel")))(
            pr3, pr3, mix, w0, wup, a0, aup, kk_, ka_)


def _rwkv_pre_bwd(pr3, cots, mix, w0, wup, a0, aup, kk_, ka_):
    bsz, s, _ = pr3.shape
    tt = PRE_TT

    def body(pr_ref, edge_ref, c0, c1, c2, c3, c4, c5, mix_ref, w0_ref, wup_ref, a0_ref, aup_ref, kk_ref, ka_ref,
             dprs_ref, dmix_ref, dw0_ref, dwup_ref, da0_ref, daup_ref, dkk_ref, dka_ref):
        pr = pr_ref[0]
        prev = _shifted(pr_ref, edge_ref, pl.program_id(1) == 0, True)
        prs = pr + (prev - pr) * mix_ref[...]
        _, vjp = jax.vjp(_pre_fn, *_pre_pieces(prs), w0_ref[...], wup_ref[...].astype(F32), a0_ref[...], aup_ref[...].astype(F32),
                         kk_ref[...], ka_ref[...])
        grads = vjp(tuple(c[0] for c in (c0, c1, c2, c3, c4, c5)))
        for piece, a, b in zip(grads[:5], PRE_SPLIT[:-1], PRE_SPLIT[1:]):
            dprs_ref[0, :, a:b] = piece
        dw0, dwup, da0, daup, dkk, dka = grads[5:]
        dprs = dprs_ref[0]
        grads = (jnp.sum(dprs * (prev - pr), axis=0, keepdims=True), dw0, dwup, da0, daup, dkk, dka)
        refs = (dmix_ref, dw0_ref, dwup_ref, da0_ref, daup_ref, dkk_ref, dka_ref)
        first = jnp.logical_and(pl.program_id(0) == 0, pl.program_id(1) == 0)

        @pl.when(first)
        def _():
            for r_, g_ in zip(refs, grads):
                r_[...] = g_

        @pl.when(jnp.logical_not(first))
        def _():
            for r_, g_ in zip(refs, grads):
                r_[...] += g_

    vec = lambda n: pl.BlockSpec((1, n), lambda b, i: (0, 0))
    mat = pl.BlockSpec((LORA, WIDTH), lambda b, i: (0, 0))
    tile = pl.BlockSpec((1, tt, WIDTH), lambda b, i: (b, i, 0))
    in_specs = [pl.BlockSpec((1, tt, PR_COLS), lambda b, i: (b, i, 0)),
                pl.BlockSpec((1, 8, PR_COLS), lambda b, i: (b, jnp.maximum(i * (tt // 8) - 1, 0), 0))]
    in_specs += [tile] * 6 + [vec(PR_COLS), vec(WIDTH), mat, vec(WIDTH), mat, vec(WIDTH), vec(WIDTH)]
    out_specs = [pl.BlockSpec((1, tt, PR_COLS), lambda b, i: (b, i, 0)), vec(PR_COLS), vec(WIDTH), mat, vec(WIDTH), mat,
                 vec(WIDTH), vec(WIDTH)]
    out_shape = [SDS((bsz, s, PR_COLS), F32), SDS((1, PR_COLS), F32), SDS((1, WIDTH), F32), SDS((LORA, WIDTH), F32),
                 SDS((1, WIDTH), F32), SDS((LORA, WIDTH), F32), SDS((1, WIDTH), F32), SDS((1, WIDTH), F32)]
    return pl.pallas_call(
        body, name="rwkv_pre_bwd", grid=(bsz, s // tt), in_specs=in_specs, out_specs=out_specs, out_shape=out_shape,
        compiler_params=_params(("arbitrary", "arbitrary")))(pr3, pr3, *cots, mix, w0, wup, a0, aup, kk_, ka_)


def _shift_bwd(dprs3, mix):
    bsz, s, _ = dprs3.shape
    tt = PRE_TT
    nt = s // tt

    def body(d_ref, edge_ref, mix_ref, o_ref):
        nxt = _shifted(d_ref, edge_ref, pl.program_id(1) == nt - 1, False)
        m = mix_ref[...]
        o_ref[0] = (d_ref[0] * (1.0 - m) + nxt * m).astype(BF16)

    in_specs = [pl.BlockSpec((1, tt, PR_COLS), lambda b, i: (b, i, 0)),
                pl.BlockSpec((1, 8, PR_COLS), lambda b, i: (b, jnp.minimum((i + 1) * (tt // 8), s // 8 - 1), 0)),
                pl.BlockSpec((1, PR_COLS), lambda b, i: (0, 0))]
    return pl.pallas_call(
        body, name="shift_bwd", grid=(bsz, nt), in_specs=in_specs, out_specs=pl.BlockSpec((1, tt, PR_COLS), lambda b, i: (b, i, 0)),
        out_shape=SDS((bsz, s, PR_COLS), BF16), compiler_params=_params(("parallel", "parallel")))(dprs3, dprs3, mix)


_NN, _NT, _TN = ((2,), (1,)), ((2,), (2,)), ((1,), (1,))


def _dot3(a, b, dims, precision=HI3):
    return lax.dot_general(a, b, (dims, ((0,), (0,))), precision=precision, preferred_element_type=F32)


def _dot3_bf16(a, b, dims):
    return lax.dot_general(a.astype(BF16), b.astype(BF16), (dims, ((0,), (0,))), preferred_element_type=F32)


class _Dots:
    def __init__(self, fwd):
        def make(dims, da_rule, db_rule):
            @jax.custom_vjp
            def f(a, b):
                return fwd(a, b, dims)

            f.defvjp(lambda a, b: (f(a, b), (a, b)), lambda res, g: (da_rule(*res, g), db_rule(*res, g)))
            return f

        one = _dot3_bf16
        self.mm = make(_NN, lambda a, b, g: one(g, b, _NT), lambda a, b, g: one(a, g, _TN))
        self.mm_nt = make(_NT, lambda a, b, g: one(g, b, _NN), lambda a, b, g: one(g, a, _TN))
        self.mm_tn = make(_TN, lambda a, b, g: one(b, g, _NT), lambda a, b, g: one(a, g, _NN))

        def powers(aab):
            ps = [aab]
            while 2 ** len(ps) < aab.shape[1]:
                ps.append(fwd(ps[-1], ps[-1], _NN))
            return ps

        def apply(ps, z, dims):
            for p in ps:
                z = z + fwd(p, z, dims)
            return z

        @jax.custom_vjp
        def solve(aab, z):
            return apply(powers(aab), z, _NN)

        def solve_fwd(aab, z):
            ps = powers(aab)
            x = apply(ps, z, _NN)
            return x, (ps, x)

        def solve_bwd(res, g):
            ps, x = res
            dz = apply(ps, g, _TN)
            return fwd(dz, x, _NT), dz

        solve.defvjp(solve_fwd, solve_bwd)
        self.solve = solve


_ACCURATE = _Dots(_dot3)
_ONE_PASS = _Dots(_dot3_bf16)
_bmm, _bmm_tn = _ACCURATE.mm, _ACCURATE.mm_tn


def _chunk_fn(s0t, r, lw, k, v, kkraw, a, rk, lnw, lnb, first=False, d=_ACCURATE):
    c = r.shape[1]
    at, rt, btc, ktc, gc, aab, arb, xv, arkv, ain, bin_ = _chunk_core(r, lw, k, v, kkraw, a, d)
    rs = d.mm(jnp.concatenate([at, rt], axis=1), s0t)
    u = d.solve(aab, rs[:, :c] + xv)
    y = rs[:, c:] + d.mm(arb, u) + arkv
    if first:
        y = _with_early_rows(y, r, lw, k, v, ain, bin_)
    gcol = jnp.sum(_diag(gc), axis=2, keepdims=True)
    sct = gcol * s0t + d.mm_tn(jnp.concatenate([btc, ktc], axis=1), jnp.concatenate([u, v], axis=1))
    return _post(y, r, k, v, rk, lnw, lnb), sct


def _diag(gc):
    return jnp.where(_masks(HEAD)[2], gc, 0.0)


def _with_early_rows(y, r, lw, k, v, ain, bin_):
    early = _stack([_early_rows(r[h], lw[h], k[h], v[h], ain[h], bin_[h]) for h in range(2)])
    return jnp.concatenate([jnp.concatenate([early, y[:2, EARLY:]], axis=1), y[2:]], axis=0)


def _early_rows(r, lw, k, v, ain, bin_):
    wc, bc, kc = jnp.transpose(jnp.exp(lw)), jnp.transpose(bin_), jnp.transpose(k)
    st = jnp.zeros((HEAD, HEAD), F32)
    rows = []
    for t in range(EARLY):
        sa = _bdot(ain[t:t + 1], st)
        st = st * wc[:, t:t + 1] + bc[:, t:t + 1] * sa + kc[:, t:t + 1] * v[t:t + 1]
        rows.append(_bdot(r[t:t + 1], st))
    return jnp.concatenate(rows, axis=0)


def _chunk_rows(c):
    return pl.ds(c * CHUNK, CHUNK) if isinstance(c, int) else pl.ds(pl.multiple_of(c * CHUNK, CHUNK), CHUNK)


def _stack(xs):
    return jnp.concatenate([x[None] for x in xs], axis=0)


def _pairs(ref, chunks):
    tiles = [ref[0, _chunk_rows(c), :] for c in chunks]
    return _stack([t[:, HEAD * h:HEAD * h + HEAD] for t in tiles for h in range(2)])


def _unpair(vals, j):
    return jnp.concatenate([vals[2 * j], vals[2 * j + 1]], axis=1)


def _masks(c):
    ii = lax.broadcasted_iota(jnp.int32, (c, c), 0)
    jj = lax.broadcasted_iota(jnp.int32, (c, c), 1)
    return ii > jj, ii >= jj, ii == jj


def _chunk_core(r, lw, k, v, kkraw, a, d=_ACCURATE):
    g_, c = r.shape[0], r.shape[1]
    nrm = jnp.sqrt(jnp.sum(kkraw * kkraw, axis=-1, keepdims=True))
    kkn = kkraw / jnp.maximum(nrm, 1e-12)
    ain, bin_ = -kkn, kkn * a
    strict, incl, _ = _masks(c)
    lg = lax.dot_general(jnp.broadcast_to(incl.astype(F32), (g_, c, c)), lw, (((2,), (1,)), ((0,), (0,))), precision=HI,
                         preferred_element_type=F32)
    g, gp, gi = jnp.exp(lg), jnp.exp(lg - lw), jnp.exp(-lg)
    at, rt, bt, kt = ain * gp, r * g, bin_ * gi, k * gi
    aa = d.mm_nt(jnp.concatenate([at, rt], axis=1), jnp.concatenate([bt, kt], axis=1))
    aab = jnp.where(strict, aa[:, :c, :c], 0.0)
    aak = jnp.where(strict, aa[:, :c, c:], 0.0)
    arb = jnp.where(incl, aa[:, c:, :c], 0.0)
    ark = jnp.where(incl, aa[:, c:, c:], 0.0)
    akv = d.mm(jnp.concatenate([aak, ark], axis=1), v)
    gc = g[:, c - 1:c, :]
    return at, rt, bt * gc, kt * gc, gc, aab, arb, akv[:, :c], akv[:, c:], ain, bin_


def _post(y, r, k, v, rk, lnw, lnb):
    mu = jnp.mean(y, axis=-1, keepdims=True)
    var = jnp.mean(jnp.square(y - mu), axis=-1, keepdims=True)
    yn = (y - mu) * lax.rsqrt(var + GN_EPS) * lnw + lnb
    return yn + jnp.sum(r * k * rk, axis=-1, keepdims=True) * v


def _chunk_consts(r, lw, k, v, kkraw, a, first=False):
    at, rt, btc, ktc, gc, aab, arb, xv, arkv, ain, bin_ = _chunk_core(r, lw, k, v, kkraw, a)
    z = _ACCURATE.solve(aab, jnp.concatenate([at, xv], axis=2))
    ryv = jnp.concatenate([rt, arkv], axis=2) + _bmm(arb, z)
    if first:
        ryv = jnp.concatenate([ryv[:, :, :HEAD], _with_early_rows(ryv[:, :, HEAD:], r, lw, k, v, ain, bin_)], axis=2)
    mkv = _bmm_tn(btc, z) + jnp.concatenate([_diag(gc), _bmm_tn(ktc, v)], axis=2)
    return mkv, ryv


def _rwkv_scan(ins, rk, lnw, lnb):
    bsz, s, _ = ins[0].shape
    nch = s // CHUNK

    def consts_body(r_ref, lw_ref, k_ref, v_ref, kk_ref, a_ref, mkv_ref, ry_ref, yv_ref):
        def group(i, carry):
            chunks = [i * CHUNK_GROUP + j for j in range(CHUNK_GROUP)]
            mkv, ryv = _chunk_consts(*[_pairs(ref, chunks) for ref in (r_ref, lw_ref, k_ref, v_ref, kk_ref, a_ref)],
                                     first=isinstance(i, int) and i == 0)
            for j, c in enumerate(chunks):
                for h in range(2):
                    mkv_ref[0, 0, c, h] = mkv[2 * j + h]
                ry_ref[0, _chunk_rows(c), :] = jnp.concatenate([ryv[2 * j][:, :HEAD], ryv[2 * j + 1][:, :HEAD]], axis=1)
                yv_ref[0, _chunk_rows(c), :] = jnp.concatenate([ryv[2 * j][:, HEAD:], ryv[2 * j + 1][:, HEAD:]], axis=1)
            return carry

        group(0, 0)
        lax.fori_loop(1, nch // CHUNK_GROUP, group, 0)

    tile = pl.BlockSpec((1, s, LANE), lambda b, hp: (b, 0, hp))
    vec = pl.BlockSpec((1, LANE), lambda b, hp: (0, hp))
    mkv_spec = pl.BlockSpec((1, 1, nch, 2, HEAD, LANE), lambda b, hp: (b, hp, 0, 0, 0, 0))
    st_spec = pl.BlockSpec((1, 1, nch, 2, HEAD, HEAD), lambda b, hp: (b, hp, 0, 0, 0, 0))
    mkv, ry, yv = pl.pallas_call(
        consts_body, name="rwkv_consts", grid=(bsz, 4), in_specs=[tile] * 6, out_specs=[mkv_spec, tile, tile],
        out_shape=[SDS((bsz, 4, nch, 2, HEAD, LANE), F32), SDS((bsz, s, WIDTH), F32), SDS((bsz, s, WIDTH), F32)],
        compiler_params=_params(("parallel", "parallel")))(*ins)

    states = _chunk_recurrence(mkv, None, "rwkv_states")

    def out_body(ry_ref, yv_ref, r_ref, k_ref, v_ref, st_ref, rk_ref, lnw_ref, lnb_ref, o_ref):
        y, r, k, v, rk_, lnw_, lnb_ = _scan_rows(ry_ref, yv_ref, r_ref, k_ref, v_ref, st_ref, rk_ref, lnw_ref, lnb_ref)
        o = _post(y, r, k, v, rk_, lnw_, lnb_)
        for j in range(CHUNK_GROUP):
            o_ref[0, _chunk_rows(j), :] = _unpair(o, j)

    o = pl.pallas_call(
        out_body, name="rwkv_out", grid=(bsz, 4, nch // CHUNK_GROUP), in_specs=_group_specs(5), out_specs=_group_specs(1)[0],
        out_shape=SDS((bsz, s, WIDTH), F32),
        compiler_params=_params(("parallel", "parallel", "parallel")))(ry, yv, ins[0], ins[2], ins[3], states, rk, lnw, lnb)
    return o, states, (mkv, ry, yv)


def _group_specs(n_tiles):
    tile = pl.BlockSpec((1, CHUNK_GROUP * CHUNK, LANE), lambda b, hp, t: (b, t, hp))
    if n_tiles == 1:
        return [tile]
    st = pl.BlockSpec((1, 1, CHUNK_GROUP, 2, HEAD, HEAD), lambda b, hp, t: (b, hp, t, 0, 0, 0))
    vec = pl.BlockSpec((1, LANE), lambda b, hp, t: (0, hp))
    return [tile] * n_tiles + [st] + [vec] * 3


def _scan_rows(ry_ref, yv_ref, r_ref, k_ref, v_ref, st_ref, rk_ref, lnw_ref, lnb_ref):
    chunks = list(range(CHUNK_GROUP))
    ry, yv, r, k, v = (_pairs(ref, chunks) for ref in (ry_ref, yv_ref, r_ref, k_ref, v_ref))
    st = _stack([st_ref[0, 0, c, h] for c in chunks for h in range(2)])
    vecs = [_stack([ref[:, HEAD * h:HEAD * h + HEAD] for _ in chunks for h in range(2)]) for ref in (rk_ref, lnw_ref, lnb_ref)]
    return (_bmm(ry, st) + yv, r, k, v, *vecs)


def _chunk_recurrence(mkv, q, name):
    bsz, _, nch = mkv.shape[:3]
    pairs = [(hp, h) for hp in range(4) for h in range(2)]

    def body(*refs):
        mkv_ref, out_ref, acc = refs[0], refs[-2], refs[-1]
        acc[...] = jnp.zeros_like(acc)

        def step(i, carry):
            c = i if q is None else nch - 1 - i
            cur = acc[...]
            for j, (hp, h) in enumerate(pairs):
                out_ref[0, hp, c, h] = cur[j]
            m = _stack([mkv_ref[0, hp, c, h] for hp, h in pairs])
            if q is None:
                acc[...] = _bmm(m[:, :, :HEAD], cur) + m[:, :, HEAD:]
            else:
                acc[...] = _bmm_tn(m[:, :, :HEAD], cur) + _stack([refs[1][0, hp, c, h] for hp, h in pairs])
            return carry

        lax.fori_loop(0, nch, step, 0)

    spec = lambda w: pl.BlockSpec((1, 4, nch, 2, HEAD, w), lambda b: (b, 0, 0, 0, 0, 0))
    return pl.pallas_call(
        body, name=name, grid=(bsz,), in_specs=[spec(LANE)] + ([] if q is None else [spec(HEAD)]), out_specs=spec(HEAD),
        out_shape=SDS((bsz, 4, nch, 2, HEAD, HEAD), F32), scratch_shapes=[pltpu.VMEM((8, HEAD, HEAD), F32)],
        compiler_params=_params(("parallel",)))(*([mkv] if q is None else [mkv, q]))


def _rwkv_scan_bwd(ins, states, consts, do3, rk, lnw, lnb):
    bsz, s, _ = ins[0].shape
    nch = s // CHUNK

    mkv, ry, yv = consts

    def q_body(do_ref, ry_ref, yv_ref, r_ref, k_ref, v_ref, st_ref, rk_ref, lnw_ref, lnb_ref, q_ref):
        y, r, k, v, rk_, lnw_, lnb_ = _scan_rows(ry_ref, yv_ref, r_ref, k_ref, v_ref, st_ref, rk_ref, lnw_ref, lnb_ref)
        _, vjp = jax.vjp(lambda y_: _post(y_, r, k, v, rk_, lnw_, lnb_), y)
        (dy,) = vjp(_pairs(do_ref, list(range(CHUNK_GROUP))))
        q = _bmm_tn(_pairs(ry_ref, list(range(CHUNK_GROUP))), dy)
        for j in range(CHUNK_GROUP):
            for h in range(2):
                q_ref[0, 0, j, h] = q[2 * j + h]

    specs = _group_specs(6)
    q = pl.pallas_call(
        q_body, name="rwkv_q", grid=(bsz, 4, nch // CHUNK_GROUP), in_specs=specs, out_specs=specs[6],
        out_shape=SDS((bsz, 4, nch, 2, HEAD, HEAD), F32),
        compiler_params=_params(("parallel", "parallel", "parallel")))(do3, ry, yv, ins[0], ins[2], ins[3], states, rk, lnw, lnb)

    dstates = _chunk_recurrence(mkv, q, "rwkv_dstates")

    def body(r_ref, lw_ref, k_ref, v_ref, kk_ref, a_ref, st_ref, dst_ref, do_ref, rk_ref, lnw_ref, lnb_ref,
             dr_ref, dlw_ref, dk_ref, dv_ref, dkk_ref, da_ref, drk_ref, dlnw_ref, dlnb_ref):
        chunks = list(range(BWD_GROUP))
        par_refs = (drk_ref, dlnw_ref, dlnb_ref)

        @pl.when(jnp.logical_and(pl.program_id(1) == 0, pl.program_id(2) == 0))
        def _():
            for ref in par_refs:
                ref[...] = jnp.zeros_like(ref)

        def group(first):
            per_pair = lambda ref: _stack([ref[0, 0, c, h] for c in chunks for h in range(2)])
            vecs = [_stack([ref[:, HEAD * h:HEAD * h + HEAD] for _ in chunks for h in range(2)]) for ref in (rk_ref, lnw_ref, lnb_ref)]
            _, vjp = jax.vjp(functools.partial(_chunk_fn, first=first, d=_ONE_PASS), per_pair(st_ref),
                             *[_pairs(ref, chunks) for ref in (r_ref, lw_ref, k_ref, v_ref, kk_ref, a_ref)], *vecs)
            grads = vjp((_pairs(do_ref, chunks), per_pair(dst_ref)))
            for ref, cot in zip((dr_ref, dlw_ref, dk_ref, dv_ref, dkk_ref, da_ref), grads[1:7]):
                for j, c in enumerate(chunks):
                    ref[0, _chunk_rows(c), :] = _unpair(cot, j)
            for ref, g_ in zip(par_refs, grads[7:10]):
                ref[...] += jnp.concatenate([sum(g_[2 * j + h] for j in range(BWD_GROUP)) for h in range(2)], axis=1)

        pl.when(pl.program_id(2) == 0)(functools.partial(group, True))
        pl.when(pl.program_id(2) != 0)(functools.partial(group, False))

    tt = BWD_GROUP * CHUNK
    tile = pl.BlockSpec((1, tt, LANE), lambda hp, b, t: (b, t, hp))
    vec = pl.BlockSpec((1, LANE), lambda hp, b, t: (0, hp))
    st_spec = pl.BlockSpec((1, 1, BWD_GROUP, 2, HEAD, HEAD), lambda hp, b, t: (b, hp, t, 0, 0, 0))
    outs = pl.pallas_call(
        body, name="rwkv_scan_bwd", grid=(4, bsz, s // tt), in_specs=[tile] * 6 + [st_spec, st_spec, tile] + [vec] * 3,
        out_specs=[tile] * 6 + [vec] * 3,
        out_shape=[SDS((bsz, s, WIDTH), F32)] * 6 + [SDS((1, WIDTH), F32)] * 3,
        compiler_params=_params(("parallel", "arbitrary", "arbitrary")))(*ins, states, dstates, do3, rk, lnw, lnb)
    return outs[:6], outs[6:]


def _head(o_attn, o_rwkv, z_attn, z_rwkv, gm, x2, tgt, wua, wur, wout, g2):
    n = x2.shape[0]
    tm = 256
    nt = n // tm
    d = D_MODEL

    def body(oa_ref, or_ref, za_ref, zr_ref, gm_ref, x_ref, t_ref, wua_ref, wur_ref, wout_ref, g2_ref,
             dxo_ref, doa_ref, dor_ref, dza_ref, dzr_ref, dgm_ref, dwua_ref, dwur_ref, dwout_ref, dg2_ref, loss_ref, lacc):
        i = pl.program_id(0)
        oa, orw, za, zr = oa_ref[...], or_ref[...], za_ref[...], zr_ref[...]
        ga, gb = gm_ref[:, 0:d], gm_ref[:, d:2 * d]
        am = (oa * _silu(za)).astype(BF16)
        bm = (orw * _silu(zr)).astype(BF16)
        ya, yb = _dot(am, wua_ref[...]), _dot(bm, wur_ref[...])
        sa, sb = jax.nn.sigmoid(ga), jax.nn.sigmoid(gb)
        merged = (sa * ya + sb * yb).astype(BF16)
        out = _dot(merged, wout_ref[...])
        rs = lax.rsqrt(jnp.mean(out * out, axis=-1, keepdims=True) + RMS_EPS)
        g2 = g2_ref[...]
        err = x_ref[...] + out * rs * g2 - t_ref[...]
        lpart = jnp.sum(err * err, axis=0, keepdims=True)
        dxo = err * (1.0 / d)
        dxo_ref[...] = dxo
        dg2 = jnp.sum(dxo * out * rs, axis=0, keepdims=True)
        gd = dxo * g2
        dout = (rs * (gd - out * (rs * rs) * jnp.mean(gd * out, axis=-1, keepdims=True))).astype(BF16)
        dmerged = _dot_nt(dout, wout_ref[...])
        dwout = _dot_tn(merged, dout)
        dya, dyb = (dmerged * sa).astype(BF16), (dmerged * sb).astype(BF16)
        dgm_ref[:, 0:d] = (dmerged * ya * sa * (1.0 - sa)).astype(BF16)
        dgm_ref[:, d:2 * d] = (dmerged * yb * sb * (1.0 - sb)).astype(BF16)
        dam, dbm = _dot_nt(dya, wua_ref[...]), _dot_nt(dyb, wur_ref[...])
        dwua, dwur = _dot_tn(am, dya), _dot_tn(bm, dyb)
        doa_ref[...] = dam * _silu(za)
        dza_ref[...] = (dam * oa * _dsilu(za)).astype(BF16)
        dor_ref[...] = dbm * _silu(zr)
        dzr_ref[...] = (dbm * orw * _dsilu(zr)).astype(BF16)

        @pl.when(i == 0)
        def _():
            dwua_ref[...], dwur_ref[...], dwout_ref[...], dg2_ref[...], lacc[...] = dwua, dwur, dwout, dg2, lpart

        @pl.when(i != 0)
        def _():
            dwua_ref[...] += dwua
            dwur_ref[...] += dwur
            dwout_ref[...] += dwout
            dg2_ref[...] += dg2
            lacc[...] += lpart

        @pl.when(i == nt - 1)
        def _():
            loss_ref[...] = jnp.sum(lacc[...], axis=1, keepdims=True) * (0.5 / d)

    t512 = pl.BlockSpec((tm, WIDTH), lambda i: (i, 0))
    t1k = pl.BlockSpec((tm, d), lambda i: (i, 0))
    t2k = pl.BlockSpec((tm, 2 * d), lambda i: (i, 0))
    full = lambda r, c: pl.BlockSpec((r, c), lambda i: (0, 0))
    return pl.pallas_call(
        body, name="head_fwd_bwd", grid=(nt,),
        in_specs=[t512, t512, t512, t512, t2k, t1k, t1k, full(WIDTH, d), full(WIDTH, d), full(d, d), full(1, d)],
        out_specs=[t1k, t512, t512, t512, t512, t2k, full(WIDTH, d), full(WIDTH, d), full(d, d), full(1, d), full(1, 1)],
        out_shape=[SDS((n, d), F32), SDS((n, WIDTH), F32), SDS((n, WIDTH), F32), SDS((n, WIDTH), BF16), SDS((n, WIDTH), BF16),
                   SDS((n, 2 * d), BF16), SDS((WIDTH, d), F32), SDS((WIDTH, d), F32), SDS((d, d), F32), SDS((1, d), F32), SDS((1, 1), F32)],
        scratch_shapes=[pltpu.VMEM((1, d), F32)],
        compiler_params=_params(("arbitrary",)))(o_attn, o_rwkv, z_attn, z_rwkv, gm, x2, tgt, wua, wur, wout, g2)


def _prenorm_bwd(dh, x2, rs, g1, dxo):
    n, d = x2.shape
    tm = 1024

    def body(dh_ref, x_ref, rs_ref, g_ref, dxo_ref, gx_ref, dg_ref):
        x, r = x_ref[...], rs_ref[...]
        gd = dh_ref[...] * g_ref[...]
        gx_ref[...] = dxo_ref[...] + r * (gd - x * (r * r) * jnp.mean(gd * x, axis=-1, keepdims=True))
        dg = jnp.sum(dh_ref[...] * x * r, axis=0, keepdims=True)

        @pl.when(pl.program_id(0) == 0)
        def _():
            dg_ref[...] = dg

        @pl.when(pl.program_id(0) != 0)
        def _():
            dg_ref[...] += dg

    t = pl.BlockSpec((tm, d), lambda i: (i, 0))
    return pl.pallas_call(
        body, name="prenorm_bwd", grid=(n // tm,),
        in_specs=[t, t, pl.BlockSpec((tm, 1), lambda i: (i, 0)), pl.BlockSpec((1, d), lambda i: (0, 0)), t],
        out_specs=[t, pl.BlockSpec((1, d), lambda i: (0, 0))], out_shape=[SDS((n, d), F32), SDS((1, d), F32)],
        compiler_params=_params(("arbitrary",)))(dh, x2, rs, g1, dxo)


def _mesh_pos():
    x, y, c = lax.axis_index("x"), lax.axis_index("y"), lax.axis_index("c")
    return 4 * x + 2 * y + c


def _coords(idx):
    return (idx // 4, (idx // 2) % 2, idx % 2)


def _exchange(srcs, to_all, name):
    n = len(srcs)

    def body(*refs):
        src_refs, dst_refs = refs[:n], refs[n:2 * n]
        send_sems, recv_sems, local_sems = refs[2 * n:]
        me = _mesh_pos()

        def piece(i, j):
            return src_refs[i] if to_all[i] else src_refs[i].at[j]

        def remote(i, off, peer, block, slot):
            return pltpu.make_async_remote_copy(src_ref=piece(i, block), dst_ref=dst_refs[i].at[slot],
                                                send_sem=send_sems.at[i, off - 1], recv_sem=recv_sems.at[i, off - 1],
                                                device_id=_coords(peer), device_id_type=MESH)

        local = [pltpu.make_async_copy(piece(i, me), dst_refs[i].at[me], local_sems.at[i]) for i in range(n)]
        for cp in local:
            cp.start()
        sends = []
        for off in range(1, N_DEV):
            to = (me + off) % N_DEV
            for i in range(n):
                sends.append(remote(i, off, to, to, me))
                sends[-1].start()
        for off in range(1, N_DEV):
            frm = (me + N_DEV - off) % N_DEV
            for i in range(n):
                remote(i, off, frm, me, frm).wait_recv()
        for cp in sends:
            cp.wait_send()
        for cp in local:
            cp.wait()

    outs = pl.pallas_call(
        body, name=name, in_specs=[pl.BlockSpec(memory_space=pltpu.HBM)] * n, out_specs=[pl.BlockSpec(memory_space=pltpu.HBM)] * n,
        out_shape=[SDS((N_DEV,) + s.shape[-2:], s.dtype) for s in srcs],
        scratch_shapes=[pltpu.SemaphoreType.DMA((n, N_DEV - 1)), pltpu.SemaphoreType.DMA((n, N_DEV - 1)), pltpu.SemaphoreType.DMA((n,))],
        compiler_params=pltpu.CompilerParams())(*srcs)
    return outs


_HBM = pl.BlockSpec(memory_space=pltpu.HBM)
_SEM = pl.BlockSpec(memory_space=pltpu.SEMAPHORE)
_EFFECT = pltpu.SideEffectType.DATAFLOW_SIDE_EFFECTING


def _send_start(src):
    def body(src_ref, land_ref, send_sems, recv_sems, src_thru, land_thru, token):
        me = _mesh_pos()
        for off in range(1, N_DEV):
            to = (me + off) % N_DEV
            pltpu.make_async_remote_copy(src_ref=src_ref.at[to], dst_ref=land_ref.at[me], send_sem=send_sems.at[off - 1],
                                         recv_sem=recv_sems.at[off - 1], device_id=_coords(to), device_id_type=MESH).start()
        token[...] = jnp.zeros_like(token)

    hbm = pltpu.HBM(src.shape, src.dtype)
    return pl.pallas_call(
        body, name="grads_start",
        out_shape=(pltpu.SemaphoreType.DMA((N_DEV - 1,)), pltpu.SemaphoreType.DMA((N_DEV - 1,)), hbm, hbm, SDS((8, LANE), BF16)),
        in_specs=(_HBM, _HBM), out_specs=(_SEM, _SEM, _HBM, _HBM, pl.BlockSpec(memory_space=pltpu.VMEM)),
        input_output_aliases={0: 2, 1: 3}, compiler_params=pltpu.CompilerParams(has_side_effects=_EFFECT),
    )(pltpu.with_memory_space_constraint(src, pltpu.HBM), pltpu.with_memory_space_constraint(jnp.zeros(src.shape, src.dtype), pltpu.HBM))


def _send_wait(send_sems, recv_sems, src_thru, land_thru, after):
    def body(src_ref, land_ref, send_sems, recv_sems, after_ref, src_dead, got_ref):
        me = _mesh_pos()
        for off in range(1, N_DEV):
            to, frm = (me + off) % N_DEV, (me + N_DEV - off) % N_DEV
            pltpu.make_async_remote_copy(src_ref=src_ref.at[to], dst_ref=land_ref.at[me], send_sem=send_sems.at[off - 1],
                                         recv_sem=recv_sems.at[off - 1], device_id=_coords(to), device_id_type=MESH).wait_send()
            pltpu.make_async_remote_copy(src_ref=src_ref.at[me], dst_ref=land_ref.at[frm], send_sem=send_sems.at[off - 1],
                                         recv_sem=recv_sems.at[off - 1], device_id=_coords(frm), device_id_type=MESH).wait_recv()

    hbm = pltpu.HBM(src_thru.shape, src_thru.dtype)
    return pl.pallas_call(
        body, name="grads_wait", out_shape=(hbm, hbm), in_specs=(_HBM, _HBM, _SEM, _SEM, pl.BlockSpec(memory_space=pl.ANY)),
        out_specs=(_HBM, _HBM), input_output_aliases={0: 0, 1: 1}, compiler_params=pltpu.CompilerParams(has_side_effects=_EFFECT),
    )(src_thru, land_thru, send_sems, recv_sems, after)[1]


def _gather(srcs, name):
    n = len(srcs)

    def body(*refs):
        src_refs, dst_refs = refs[:n], refs[n:2 * n]
        send_sems, recv_sems, local_sems = refs[2 * n:]
        x, y, c = lax.axis_index("x"), lax.axis_index("y"), lax.axis_index("c")
        me, sibling = (x, y, c), (x, y, 1 - c)
        chips = [(1 - x, y), (x, 1 - y), (1 - x, 1 - y)]

        def slot(i, dev):
            return dst_refs[i].at[4 * dev[0] + 2 * dev[1] + dev[2]]

        def copy(i, k, block, to, own=False):
            return pltpu.make_async_remote_copy(src_ref=src_refs[i] if own else slot(i, block), dst_ref=slot(i, block),
                                                send_sem=send_sems.at[i, k], recv_sem=recv_sems.at[i, k],
                                                device_id=to, device_id_type=MESH)

        local = [pltpu.make_async_copy(src_refs[i], slot(i, me), local_sems.at[i]) for i in range(n)]
        for cp in local:
            cp.start()
        sends = []
        for i in range(n):
            sends.append(copy(i, 0, me, sibling, own=True))
            sends += [copy(i, 1 + j, me, (*chip, c), own=True) for j, chip in enumerate(chips)]
        for cp in sends:
            cp.start()
        for j, chip in enumerate(chips):
            for i in range(n):
                copy(i, 1 + j, (*chip, c), me).wait_recv()
                sends.append(copy(i, 4 + j, (*chip, c), sibling))
                sends[-1].start()
        for i in range(n):
            copy(i, 0, sibling, me).wait_recv()
            for j, chip in enumerate(chips):
                copy(i, 4 + j, (*chip, 1 - c), me).wait_recv()
        for cp in sends:
            cp.wait_send()
        for cp in local:
            cp.wait()

    return pl.pallas_call(
        body, name=name, in_specs=[pl.BlockSpec(memory_space=pltpu.HBM)] * n, out_specs=[pl.BlockSpec(memory_space=pltpu.HBM)] * n,
        out_shape=[SDS((N_DEV,) + s.shape, s.dtype) for s in srcs],
        scratch_shapes=[pltpu.SemaphoreType.DMA((n, N_DEV - 1)), pltpu.SemaphoreType.DMA((n, N_DEV - 1)), pltpu.SemaphoreType.DMA((n,))],
        compiler_params=pltpu.CompilerParams())(*srcs)


def _adamw(parts, w, m, v, tr, name, own=None):
    rows, cols = w.shape
    c1, c2 = 1.0 - ADAM_B1 ** ADAM_STEP, 1.0 - ADAM_B2 ** ADAM_STEP

    def body(p_ref, *refs):
        w_ref, m_ref, v_ref, g_ref, d_ref, nm_ref, nv_ref = refs[-7:]
        me = _mesh_pos()

        def part(j):
            return p_ref[j] if own is None else jnp.where(me == j, refs[0][...], p_ref[j])

        g = part(0).astype(F32)
        for j in range(1, N_DEV):
            g = g + part(j).astype(F32)
        nm = ADAM_B1 * m_ref[...] + (1.0 - ADAM_B1) * g
        nv = ADAM_B2 * v_ref[...] + (1.0 - ADAM_B2) * jnp.square(g)
        g_ref[...] = g
        nm_ref[...] = nm
        nv_ref[...] = nv
        d_ref[...] = -ADAM_LR * ((nm / c1) / (jnp.sqrt(nv / c2) + ADAM_EPS) + ADAM_WD * w_ref[...])

    t = pl.BlockSpec((tr, cols), lambda i: (i, 0))
    extra = [] if own is None else [own]
    return pl.pallas_call(
        body, name=name, grid=(rows // tr,), in_specs=[pl.BlockSpec((N_DEV, tr, cols), lambda i: (0, i, 0))] + [t] * (3 + len(extra)),
        out_specs=[t] * 4, out_shape=[SDS((rows, cols), F32)] * 4, compiler_params=_params(("parallel",)))(parts, *extra, w, m, v)


SHARDED = (("w_in", D_MODEL, IN_COLS // N_DEV, True, 128), ("w_up_attn", WIDTH, D_MODEL // N_DEV, True, WIDTH),
           ("w_up_rwkv", WIDTH, D_MODEL // N_DEV, True, WIDTH), ("w_out", D_MODEL // N_DEV, D_MODEL, False, D_MODEL // N_DEV),
           ("rwkv_w_up", LORA, WIDTH // N_DEV, True, LORA), ("rwkv_a_up", LORA, WIDTH // N_DEV, True, LORA))
LOSS_SLOT = sum(n for _, n in SMALL)


def _pack_small(small, extra=None):
    flat = [small[n].reshape(-1).astype(F32) for n, _ in SMALL]
    flat.append(jnp.zeros((1,), F32) if extra is None else extra.reshape(1))
    flat.append(jnp.zeros((SMALL_ROWS * LANE - LOSS_SLOT - 1,), F32))
    return jnp.concatenate(flat).reshape(SMALL_ROWS, LANE)


def _unpack_small(packed, shapes):
    flat = packed.reshape(-1)
    out, off = {}, 0
    for n, cnt in SMALL:
        out[n] = flat[off:off + cnt].reshape(shapes[n])
        off += cnt
    return out, flat[LOSS_SLOT]


def _whole(gathered, by_cols):
    if not by_cols:
        return gathered.reshape(-1, gathered.shape[-1])
    return gathered.transpose(1, 0, 2).reshape(gathered.shape[1], -1)


def _per_owner(full, by_cols):
    if not by_cols:
        return full.reshape(N_DEV, -1, full.shape[-1])
    return full.reshape(full.shape[0], N_DEV, -1).transpose(1, 0, 2)


def _local_step(x, loss_target, sm, wts):
    bsz, s, d = x.shape
    n = bsz * s
    x2, tgt = x.reshape(n, d), loss_target.reshape(n, d)
    bidx = jnp.asarray(_bucket_tables())
    w_in = wts["w_in"]
    segs = (("qkv", 0, QKV_COLS, 1536), ("za", OFF_ZA, WIDTH, 512), ("pr", OFF_PR, PR_COLS, PR_COLS), ("zr", OFF_ZR, WIDTH, 512),
            ("gm", OFF_GM, 2 * D_MODEL, 512))

    h, rs = _prenorm(x2, sm["pre_norm_gain"])
    proj = {nm: _mm(h, w_in[:, off:off + cnt], tn, "proj_" + nm) for nm, off, cnt, tn in segs}
    qkv3 = proj["qkv"].reshape(bsz, s, QKV_COLS)
    pr3 = proj["pr"].reshape(bsz, s, PR_COLS)

    o_attn, lse = _attn_fwd(qkv3, sm["rel_bias"], bidx)
    rk = sm["rwkv_r_k"].reshape(1, WIDTH)
    pre_args = (sm["rwkv_shift_mix"], sm["rwkv_w0"], wts["rwkv_w_up"], sm["rwkv_a0"], wts["rwkv_a_up"], sm["rwkv_k_k"], sm["rwkv_k_a"])
    scan_in = _rwkv_pre(pr3, *pre_args)
    o_rwkv, states, consts = _rwkv_scan(scan_in, rk, sm["rwkv_ln_w"], sm["rwkv_ln_b"])

    (dxo, do_attn, do_rwkv, dza, dzr, dgm, g_wua, g_wur, g_wout, g_post, loss) = _head(
        o_attn.reshape(n, WIDTH), o_rwkv.reshape(n, WIDTH), proj["za"], proj["zr"], proj["gm"], x2, tgt,
        wts["w_up_attn"], wts["w_up_rwkv"], wts["w_out"], sm["post_norm_gain"])

    dqkv, dbias = _attn_bwd(qkv3, o_attn, lse, do_attn.reshape(bsz, s, WIDTH), sm["rel_bias"], bidx)
    g_bias = _bias_grad(dbias, bidx)[:, :N_BUCKET].T

    scan_cots, (g_rk, g_lnw, g_lnb) = _rwkv_scan_bwd(scan_in, states, consts, do_rwkv.reshape(bsz, s, WIDTH), rk, sm["rwkv_ln_w"],
                                                     sm["rwkv_ln_b"])
    dprs, g_mix, g_w0, g_wup, g_a0, g_aup, g_kk, g_ka = _rwkv_pre_bwd(pr3, scan_cots, *pre_args)
    dpr = _shift_bwd(dprs, sm["rwkv_shift_mix"]).reshape(n, PR_COLS)

    dsegs = [(dqkv.reshape(9, n, WIDTH), 0, QKV_COLS, WIDTH), (dza, OFF_ZA, WIDTH, WIDTH), (dpr, OFF_PR, PR_COLS, PR_COLS),
             (dzr, OFF_ZR, WIDTH, WIDTH), (dgm, OFF_GM, 2 * D_MODEL, D_MODEL)]
    g_win = jnp.concatenate([_mm_tn(h, t, tn, "gw_in_%d" % j) for j, (t, _, _, tn) in enumerate(dsegs)], axis=1)
    blocks = _per_owner(g_win, True).astype(BF16)
    own = lax.dynamic_index_in_dim(blocks, 4 * lax.axis_index("x") + 2 * lax.axis_index("y") + lax.axis_index("c"), 0, keepdims=False)
    send_sems, recv_sems, blocks_thru, land_thru, token = _send_start(blocks)
    dh = None
    for j, (t, off, cnt, _) in enumerate(dsegs):
        dh = _mm_nt_acc(t, w_in[:, off:off + cnt] + token[0, 0], dh, "dh_%d" % j)
    grad_x, g_pre = _prenorm_bwd(dh, x2, rs, sm["pre_norm_gain"], dxo)
    landed = _send_wait(send_sems, recv_sems, blocks_thru, land_thru, g_pre)

    full = {"w_up_attn": g_wua, "w_up_rwkv": g_wur, "w_out": g_wout, "rwkv_w_up": g_wup, "rwkv_a_up": g_aup}
    small = {"pre_norm_gain": g_pre, "rel_bias": g_bias, "rwkv_shift_mix": g_mix, "rwkv_w0": g_w0, "rwkv_a0": g_a0, "rwkv_k_k": g_kk,
             "rwkv_k_a": g_ka, "rwkv_r_k": g_rk, "rwkv_ln_w": g_lnw, "rwkv_ln_b": g_lnb, "post_norm_gain": g_post}
    return loss[0, 0], grad_x.reshape(bsz, s, d), (landed, own), full, small


def kernel(x, pre_norm_gain, w_in, rel_bias, rwkv_shift_mix, rwkv_w0, rwkv_w_up, rwkv_a0, rwkv_a_up, rwkv_k_k, rwkv_k_a, rwkv_r_k, rwkv_ln_w, rwkv_ln_b, w_up_attn, w_up_rwkv, w_out, post_norm_gain, loss_target, m_pre_norm_gain, m_w_in, m_rel_bias, m_rwkv_shift_mix, m_rwkv_w0, m_rwkv_w_up, m_rwkv_a0, m_rwkv_a_up, m_rwkv_k_k, m_rwkv_k_a, m_rwkv_r_k, m_rwkv_ln_w, m_rwkv_ln_b, m_w_up_attn, m_w_up_rwkv, m_w_out, m_post_norm_gain, v_pre_norm_gain, v_w_in, v_rel_bias, v_rwkv_shift_mix, v_rwkv_w0, v_rwkv_w_up, v_rwkv_a0, v_rwkv_a_up, v_rwkv_k_k, v_rwkv_k_a, v_rwkv_r_k, v_rwkv_ln_w, v_rwkv_ln_b, v_w_up_attn, v_w_up_rwkv, v_w_out, v_post_norm_gain):
    names = [n for n, *_ in SHARDED] + [n for n, _ in SMALL]
    loc = dict(locals())
    w = {n: loc[n] for n in names}
    m = {n: loc["m_" + n] for n in names}
    v = {n: loc["v_" + n] for n in names}
    shapes = {n: w[n].shape for n in names}
    order = ["pre_norm_gain", "w_in", "rel_bias", "rwkv_shift_mix", "rwkv_w0", "rwkv_w_up", "rwkv_a0", "rwkv_a_up", "rwkv_k_k", "rwkv_k_a",
             "rwkv_r_k", "rwkv_ln_w", "rwkv_ln_b", "w_up_attn", "w_up_rwkv", "w_out", "post_norm_gain"]
    shard2d = lambda t, n, r, c: t[n].reshape(r, c)

    gathered = _gather([shard2d(w, n, r, c).astype(BF16) for n, r, c, _, _ in SHARDED], "gather_weights")
    wts = {n: _whole(g, by_cols) for (n, _, _, by_cols, _), g in zip(SHARDED, gathered)}

    loss, grad_x, (win_landed, win_own), full, small = _local_step(x, loss_target, w, wts)
    rest = SHARDED[1:]
    parts = _exchange([_per_owner(full[n], by_cols).astype(BF16) for n, _, _, by_cols, _ in rest] + [_pack_small(small, loss)],
                      [False] * len(rest) + [True], "exchange_grads")

    outs = [{}, {}, {}, {}]
    for (n, r, c, _, tr), p in zip(SHARDED, [win_landed] + list(parts)):
        res = _adamw(p, shard2d(w, n, r, c), shard2d(m, n, r, c), shard2d(v, n, r, c), tr, "adamw_" + n,
                     own=win_own if n == "w_in" else None)
        for o, t in zip(outs, res):
            o[n] = t.reshape(shapes[n])
    res = _adamw(parts[-1], _pack_small(w), _pack_small(m), _pack_small(v), SMALL_ROWS, "adamw_small")
    for o, t in zip(outs, res):
        o.update(_unpack_small(t, shapes)[0])
    loss = _unpack_small(res[0], shapes)[1]
    return (loss, grad_x, *[o[n] for o in outs for n in order])
```

```python
import functools
import math

import numpy as np
import jax
import jax.numpy as jnp
from jax import lax
from jax.experimental import pallas as pl
from jax.experimental.pallas import tpu as pltpu

F32, BF16 = jnp.float32, jnp.bfloat16
SDS = jax.ShapeDtypeStruct
HI = lax.Precision.HIGHEST
HI3 = lax.Precision.HIGH
MESH = pl.DeviceIdType.MESH

N_DEV = 8
D_MODEL = 1024
HEAD = 64
N_HEAD = 8
WIDTH = N_HEAD * HEAD
DILATIONS = (1, 4, 16)
QB = 128
N_BUCKET = 32
MAX_DIST = 2048
LORA = 64
QKV_COLS = 9 * WIDTH
PR_COLS = 3 * WIDTH + 2 * LORA
IN_COLS = QKV_COLS + WIDTH + PR_COLS + WIDTH + 2 * D_MODEL
OFF_ZA, OFF_PR, OFF_ZR, OFF_GM = QKV_COLS, QKV_COLS + WIDTH, QKV_COLS + WIDTH + PR_COLS, QKV_COLS + 2 * WIDTH + PR_COLS
RMS_EPS = 1e-6
GN_EPS = 64e-5
SCALE = 1.0 / math.sqrt(HEAD)
CHUNK = 64
CHUNK_GROUP = 8
BWD_GROUP = 16
EARLY = 8
NEG = -1e30
LANE = 128

ADAM_LR, ADAM_B1, ADAM_B2, ADAM_EPS, ADAM_WD, ADAM_STEP = 0.001, 0.9, 0.999, 1e-08, 0.01, 10

VMEM_LIMIT = 56 * 1024 * 1024

SMALL = (("pre_norm_gain", 1024), ("rel_bias", 768), ("rwkv_shift_mix", 1664), ("rwkv_w0", 512), ("rwkv_a0", 512),
         ("rwkv_k_k", 512), ("rwkv_k_a", 512), ("rwkv_r_k", 512), ("rwkv_ln_w", 512), ("rwkv_ln_b", 512),
         ("post_norm_gain", 1024))
SMALL_ROWS = 64


def _params(sem=None):
    return pltpu.CompilerParams(dimension_semantics=sem, vmem_limit_bytes=VMEM_LIMIT)


def _dot(a, b):
    return jnp.dot(a, b, preferred_element_type=F32)


def _dot_nt(a, b):
    return lax.dot_general(a, b, (((1,), (1,)), ((), ())), preferred_element_type=F32)


def _dot_tn(a, b):
    return lax.dot_general(a, b, (((0,), (0,)), ((), ())), preferred_element_type=F32)


@jax.custom_vjp
def _bdot(a, b):
    return _dot(a.astype(BF16), b.astype(BF16))


def _bdot_fwd(a, b):
    return _bdot(a, b), (a, b)


def _bdot_bwd(res, g):
    a, b = res
    gb = g.astype(BF16)
    return _dot_nt(gb, b.astype(BF16)), _dot_tn(a.astype(BF16), gb)


_bdot.defvjp(_bdot_fwd, _bdot_bwd)


def _silu(z):
    return z * jax.nn.sigmoid(z)


def _dsilu(z):
    s = jax.nn.sigmoid(z)
    return s * (1.0 + z * (1.0 - s))


def _softplus(x):
    return jnp.maximum(x, 0.0) + jnp.log(1.0 + jnp.exp(-jnp.abs(x)))


def _bucket_tables():
    qi = np.arange(QB)[:, None] + QB
    ki = np.arange(2 * QB)[None, :]
    rel = np.maximum(qi - ki, 0)
    out = []
    for d in DILATIONS:
        dist = rel * d
        max_exact = N_BUCKET // 2
        ratio = np.log(np.maximum(dist, 1).astype(np.float32) / max_exact) / np.float32(math.log(MAX_DIST / max_exact))
        large = max_exact + (ratio * (N_BUCKET - max_exact)).astype(np.int32)
        large = np.minimum(large, N_BUCKET - 1)
        out.append(np.where(dist < max_exact, dist, large).astype(np.int32))
    return np.stack(out)


def _prenorm(x2, g):
    n, d = x2.shape
    tm = 1024

    def body(x_ref, g_ref, h_ref, rs_ref):
        x = x_ref[...]
        rs = lax.rsqrt(jnp.mean(x * x, axis=-1, keepdims=True) + RMS_EPS)
        h_ref[...] = (x * rs * g_ref[...]).astype(BF16)
        rs_ref[...] = rs

    return pl.pallas_call(
        body, name="prenorm", grid=(n // tm,),
        in_specs=[pl.BlockSpec((tm, d), lambda i: (i, 0)), pl.BlockSpec((1, d), lambda i: (0, 0))],
        out_specs=[pl.BlockSpec((tm, d), lambda i: (i, 0)), pl.BlockSpec((tm, 1), lambda i: (i, 0))],
        out_shape=[SDS((n, d), BF16), SDS((n, 1), F32)], compiler_params=_params(("parallel",)))(x2, g)


def _mm(a, b, tn, name):
    m, k = a.shape
    n = b.shape[1]
    tm = 1024

    def body(a_ref, b_ref, o_ref):
        o_ref[...] = _dot(a_ref[...], b_ref[...])

    return pl.pallas_call(
        body, name=name, grid=(n // tn, m // tm),
        in_specs=[pl.BlockSpec((tm, k), lambda j, i: (i, 0)), pl.BlockSpec((k, tn), lambda j, i: (0, j))],
        out_specs=pl.BlockSpec((tm, tn), lambda j, i: (i, j)),
        out_shape=SDS((m, n), F32), compiler_params=_params(("parallel", "parallel")))(a, b)


def _mm_nt_acc(a, b, acc, name):
    split = a.ndim == 3
    m = a.shape[-2]
    k = b.shape[1]
    d = b.shape[0]
    tm = 1024
    per = 3 if split else 1
    seg = a.shape[2] if split else 0
    tk = per * seg if split else (k if k <= 2048 else 1536)
    have_acc = acc is not None

    def body(*refs):
        if have_acc:
            a_ref, b_ref, c_ref, o_ref = refs
        else:
            a_ref, b_ref, o_ref = refs
        if split:
            r = sum(_dot_nt(a_ref[j].astype(BF16), b_ref[:, seg * j:seg * (j + 1)]) for j in range(per))
        else:
            r = _dot_nt(a_ref[...].astype(BF16), b_ref[...])

        @pl.when(pl.program_id(1) == 0)
        def _():
            o_ref[...] = r + c_ref[...] if have_acc else r

        @pl.when(pl.program_id(1) != 0)
        def _():
            o_ref[...] += r

    a_spec = pl.BlockSpec((per, tm, seg), lambda i, j: (j, i, 0)) if split else pl.BlockSpec((tm, tk), lambda i, j: (i, j))
    in_specs = [a_spec, pl.BlockSpec((d, tk), lambda i, j: (0, j))]
    args = [a, b]
    if have_acc:
        in_specs.append(pl.BlockSpec((tm, d), lambda i, j: (i, 0)))
        args.append(acc)
    return pl.pallas_call(
        body, name=name, grid=(m // tm, k // tk), in_specs=in_specs, out_specs=pl.BlockSpec((tm, d), lambda i, j: (i, 0)),
        out_shape=SDS((m, d), F32), compiler_params=_params(("parallel", "arbitrary")))(*args)


def _mm_tn(a, b, tn, name):
    split = b.ndim == 3
    m, k1 = a.shape
    per = 3 if split else 1
    seg = b.shape[2] if split else tn
    tn = per * seg
    n2 = b.shape[0] * seg if split else b.shape[1]
    tm = 1024

    def body(a_ref, b_ref, o_ref):
        first = pl.program_id(1) == 0
        for j in range(per):
            r = _dot_tn(a_ref[...], (b_ref[j] if split else b_ref[...]).astype(BF16))
            cols = slice(seg * j, seg * (j + 1))

            @pl.when(first)
            def _(r=r, cols=cols):
                o_ref[:, cols] = r

            @pl.when(jnp.logical_not(first))
            def _(r=r, cols=cols):
                o_ref[:, cols] += r

    b_spec = pl.BlockSpec((per, tm, seg), lambda j, i: (j, i, 0)) if split else pl.BlockSpec((tm, tn), lambda j, i: (i, j))
    return pl.pallas_call(
        body, name=name, grid=(n2 // tn, m // tm),
        in_specs=[pl.BlockSpec((tm, k1), lambda j, i: (i, 0)), b_spec],
        out_specs=pl.BlockSpec((k1, tn), lambda j, i: (0, j)),
        out_shape=SDS((k1, n2), F32), compiler_params=_params(("parallel", "arbitrary")))(a, b)


def _ds(start, d):
    return pl.ds(start, QB) if d == 1 else pl.ds(start, QB, stride=d)


def _fill_bias(tab_ref, bidx_ref, bias_sc, hp):
    for g in range(3):
        bi = bidx_ref[g]
        for h in range(2):
            acc = jnp.zeros((QB, 2 * QB), F32)
            for j in range(N_BUCKET):
                acc = jnp.where(bi == j, tab_ref[j, g * N_HEAD + hp * 2 + h], acc)
            bias_sc[g * 2 + h] = acc


def _block_starts(it, d, nb):
    rho = it // nb
    n = it % nb
    st = rho + d * QB * n
    stp = rho + d * QB * jnp.maximum(n - 1, 0)
    return st, stp, n > 0


ATTN_BLOCKS = 4


def _bdot3(a, b, dims):
    return lax.dot_general(a, b, (dims, ((0,), (0,))), preferred_element_type=F32)


def _attn_operands(q_ref, k_ref, v_ref, bias_sc, g, d, nb, it0):
    ii = lax.broadcasted_iota(jnp.int32, (QB, 2 * QB), 0)
    cc = lax.broadcasted_iota(jnp.int32, (QB, 2 * QB), 1)
    qs, ks, vs, pens, starts = [], [], [], [], []
    for u in range(ATTN_BLOCKS):
        st, stp, hasprev = _block_starts(it0 + u, d, nb)
        qf = q_ref[0, _ds(st, d), :]
        kf = jnp.concatenate([k_ref[0, _ds(stp, d), :], k_ref[0, _ds(st, d), :]], axis=0)
        vf = jnp.concatenate([v_ref[0, _ds(stp, d), :], v_ref[0, _ds(st, d), :]], axis=0)
        own = jnp.logical_and(cc >= QB, ii >= cc - QB)
        prev = jnp.logical_and(jnp.logical_and(cc < QB, cc >= ii), hasprev)
        pen = jnp.where(jnp.logical_or(own, prev), 0.0, NEG)
        for h in range(2):
            sl = slice(HEAD * h, HEAD * h + HEAD)
            qs.append(qf[:, sl])
            ks.append(kf[:, sl])
            vs.append(vf[:, sl])
            pens.append(pen + bias_sc[g * 2 + h])
        starts.append((st, stp))
    return _stack(qs).astype(BF16), _stack(ks).astype(BF16), _stack(vs).astype(BF16), _stack(pens), starts


def _heads(x, u):
    return jnp.concatenate([x[2 * u], x[2 * u + 1]], axis=1)


def _attn_fwd(qkv3, rel_bias, bidx):
    bsz, s, _ = qkv3.shape
    rt = 256

    def body(tab_ref, bidx_ref, *refs):
        q_refs, k_refs, v_refs = refs[0:3], refs[3:6], refs[6:9]
        o_ref, lse_ref = refs[9:11]
        bias_sc, num_sc, den_sc, m_sc = refs[11:]
        pl.when(pl.program_id(1) == 0)(lambda: _fill_bias(tab_ref, bidx_ref, bias_sc, pl.program_id(0)))
        for g, d in enumerate(DILATIONS):
            nb = s // (QB * d)

            def blk(it, c, g=g, d=d, nb=nb):
                q, k, v, bias, starts = _attn_operands(q_refs[g], k_refs[g], v_refs[g], bias_sc, g, d, nb, it * ATTN_BLOCKS)
                sc = _bdot3(q, k, ((2,), (2,))) * SCALE + bias
                m = jnp.max(sc, axis=-1, keepdims=True)
                p = jnp.exp(sc - m)
                den = jnp.sum(p, axis=-1, keepdims=True)
                num = _bdot3(p.astype(BF16), v, ((2,), (1,)))
                den, m = jnp.broadcast_to(den, num.shape), jnp.broadcast_to(m, num.shape)
                for u, (st, _) in enumerate(starts):
                    num_sc[g, _ds(st, d), :] = _heads(num, u)
                    den_sc[g, _ds(st, d), :] = _heads(den, u)
                    m_sc[g, _ds(st, d), :] = _heads(m, u)
                return c

            lax.fori_loop(0, s // QB // ATTN_BLOCKS, blk, 0)

        def merge(i, c):
            rows = pl.ds(pl.multiple_of(i * rt, rt), rt)
            m0, m1, m2 = m_sc[0, rows, :], m_sc[1, rows, :], m_sc[2, rows, :]
            mall = jnp.maximum(jnp.maximum(m0, m1), m2)
            w0, w1, w2 = jnp.exp(m0 - mall), jnp.exp(m1 - mall), jnp.exp(m2 - mall)
            num = w0 * num_sc[0, rows, :] + w1 * num_sc[1, rows, :] + w2 * num_sc[2, rows, :]
            den = w0 * den_sc[0, rows, :] + w1 * den_sc[1, rows, :] + w2 * den_sc[2, rows, :]
            o_ref[0, rows, :] = num / den
            lse_ref[0, rows, :] = mall + jnp.log(den)
            return c

        lax.fori_loop(0, s // rt, merge, 0)

    col = lambda w, g: (lambda hp, b: (b, 0, (w * 3 + g) * 4 + hp))
    in_specs = [pl.BlockSpec(memory_space=pltpu.SMEM), pl.BlockSpec((3, QB, 2 * QB), lambda hp, b: (0, 0, 0))]
    in_specs += [pl.BlockSpec((1, s, LANE), col(w, g)) for w in range(3) for g in range(3)]
    out_spec = pl.BlockSpec((1, s, LANE), lambda hp, b: (b, 0, hp))
    return pl.pallas_call(
        body, name="attn_fwd", grid=(4, bsz), in_specs=in_specs, out_specs=[out_spec, out_spec],
        out_shape=[SDS((bsz, s, WIDTH), F32), SDS((bsz, s, WIDTH), F32)],
        scratch_shapes=[pltpu.VMEM((6, QB, 2 * QB), F32), pltpu.VMEM((3, s, LANE), F32), pltpu.VMEM((3, s, LANE), F32),
                        pltpu.VMEM((3, s, LANE), F32)],
        compiler_params=_params(("arbitrary", "arbitrary")))(rel_bias, bidx, *([qkv3] * 9))


def _attn_bwd(qkv3, o3, lse3, do3, rel_bias, bidx):
    bsz, s, _ = qkv3.shape
    rt = 256

    def body(tab_ref, bidx_ref, *refs):
        q_refs, k_refs, v_refs = refs[0:3], refs[3:6], refs[6:9]
        o_ref, lse_ref, do_ref, dqkv_ref, db_ref, bias_sc, delta_sc, acc_sc = refs[9:]
        dq_refs, dk_refs, dv_refs = ([acc_sc.at[w * 3 + g] for g in range(3)] for w in range(3))

        @pl.when(pl.program_id(1) == 0)
        def _():
            _fill_bias(tab_ref, bidx_ref, bias_sc, pl.program_id(0))
            db_ref[...] = jnp.zeros_like(db_ref)

        def prep(i, c):
            rows = pl.ds(pl.multiple_of(i * rt, rt), rt)
            prod = do_ref[0, rows, :] * o_ref[0, rows, :]
            d0 = jnp.sum(prod[:, :HEAD], axis=-1, keepdims=True)
            d1 = jnp.sum(prod[:, HEAD:], axis=-1, keepdims=True)
            delta_sc[rows, :] = jnp.concatenate([jnp.broadcast_to(d0, (rt, HEAD)), jnp.broadcast_to(d1, (rt, HEAD))], axis=1)
            z = jnp.zeros((rt, LANE), F32)
            for g in range(3):
                dk_refs[g][0, rows, :] = z
                dv_refs[g][0, rows, :] = z
            return c

        lax.fori_loop(0, s // rt, prep, 0)
        for g, d in enumerate(DILATIONS):
            nb = s // (QB * d)

            def blk(it, c, g=g, d=d, nb=nb):
                q, k, v, bias, starts = _attn_operands(q_refs[g], k_refs[g], v_refs[g], bias_sc, g, d, nb, it * ATTN_BLOCKS)
                dos, lses, deltas = [], [], []
                for st, _ in starts:
                    dof, lsef, delf = do_ref[0, _ds(st, d), :], lse_ref[0, _ds(st, d), :], delta_sc[_ds(st, d), :]
                    for h in range(2):
                        dos.append(dof[:, HEAD * h:HEAD * h + HEAD])
                        lses.append(lsef[:, HEAD * h:HEAD * h + 1])
                        deltas.append(delf[:, HEAD * h:HEAD * h + 1])
                do, lse, delta = _stack(dos).astype(BF16), _stack(lses), _stack(deltas)
                p = jnp.exp(_bdot3(q, k, ((2,), (2,))) * SCALE + bias - lse)
                dv = _bdot3(p.astype(BF16), do, ((1,), (1,)))
                ds = p * (_bdot3(do, v, ((2,), (2,))) - delta)
                dsb = ds.astype(BF16)
                dq = _bdot3(dsb, k, ((2,), (1,))) * SCALE
                dk = _bdot3(dsb, q, ((1,), (1,))) * SCALE
                for h in range(2):
                    db_ref[0, g * 2 + h] += sum(ds[2 * u + h] for u in range(ATTN_BLOCKS))
                for u, (st, stp) in enumerate(starts):
                    dq_refs[g][0, _ds(st, d), :] = _heads(dq, u)
                    dk_refs[g][0, _ds(stp, d), :] += _heads(dk[:, :QB], u)
                    dv_refs[g][0, _ds(stp, d), :] += _heads(dv[:, :QB], u)
                    dk_refs[g][0, _ds(st, d), :] += _heads(dk[:, QB:], u)
                    dv_refs[g][0, _ds(st, d), :] += _heads(dv[:, QB:], u)
                return c

            lax.fori_loop(0, s // QB // ATTN_BLOCKS, blk, 0)

        def flush(i, c):
            rows = pl.ds(pl.multiple_of(i * rt, rt), rt)
            for j in range(9):
                dqkv_ref[j, 0, rows, :] = acc_sc[j, 0, rows, :].astype(BF16)
            return c

        lax.fori_loop(0, s // rt, flush, 0)

    col = lambda w, g: (lambda hp, b: (b, 0, (w * 3 + g) * 4 + hp))
    blk_spec = pl.BlockSpec((1, s, LANE), lambda hp, b: (b, 0, hp))
    in_specs = [pl.BlockSpec(memory_space=pltpu.SMEM), pl.BlockSpec((3, QB, 2 * QB), lambda hp, b: (0, 0, 0))]
    in_specs += [pl.BlockSpec((1, s, LANE), col(w, g)) for w in range(3) for g in range(3)]
    in_specs += [blk_spec] * 3
    out_specs = [pl.BlockSpec((9, 1, s, LANE), lambda hp, b: (0, b, 0, hp)), pl.BlockSpec((1, 6, QB, 2 * QB), lambda hp, b: (hp, 0, 0, 0))]
    out_shape = [SDS((9, bsz, s, WIDTH), BF16), SDS((4, 6, QB, 2 * QB), F32)]
    return pl.pallas_call(
        body, name="attn_bwd", grid=(4, bsz), in_specs=in_specs, out_specs=out_specs, out_shape=out_shape,
        scratch_shapes=[pltpu.VMEM((6, QB, 2 * QB), F32), pltpu.VMEM((s, LANE), F32), pltpu.VMEM((9, 1, s, LANE), F32)],
        compiler_params=_params(("parallel", "arbitrary")))(rel_bias, bidx, *([qkv3] * 9), o3, lse3, do3)


def _bias_grad(dbias, bidx):
    def body(db_ref, bidx_ref, o_ref):
        lane = lax.broadcasted_iota(jnp.int32, (1, LANE), 1)
        for g in range(3):
            bi = bidx_ref[g]
            for hp in range(4):
                for h in range(2):
                    mat = db_ref[hp, g * 2 + h]
                    row = jnp.zeros((1, LANE), F32)
                    for j in range(N_BUCKET):
                        part = jnp.sum(jnp.where(bi == j, mat, 0.0), axis=0, keepdims=True)
                        row = jnp.where(lane == j, jnp.sum(part, axis=1, keepdims=True), row)
                    hd = g * N_HEAD + hp * 2 + h
                    o_ref[hd:hd + 1, :] = row

    return pl.pallas_call(body, name="bias_grad", out_shape=SDS((3 * N_HEAD, LANE), F32), compiler_params=_params())(dbias, bidx)


def _pre_fn(r, k0, v, wl, al, w0, wup, a0, aup, kk_, ka_):
    u = w0 + _bdot(jnp.tanh(wl), wup)
    lw = -jnp.exp(-_softplus(-u) - 0.5)
    a = jax.nn.sigmoid(a0 + _bdot(al, aup))
    kkraw = k0 * kk_
    k = k0 * (1.0 + (a - 1.0) * ka_)
    return r, lw, k, v, kkraw, a


PRE_SPLIT = (0, WIDTH, 2 * WIDTH, 3 * WIDTH, 3 * WIDTH + LORA, 3 * WIDTH + 2 * LORA)


def _pre_pieces(prs):
    return [prs[:, a:b] for a, b in zip(PRE_SPLIT[:-1], PRE_SPLIT[1:])]


PRE_TT = 512


def _shifted(pr_ref, edge_ref, first, back):
    pr = pr_ref[0]
    tt = pr.shape[0]
    row = lax.broadcasted_iota(jnp.int32, (tt, 1), 0)
    if back:
        edge = jnp.where(first, 0.0, edge_ref[0, 7:8, :])
        return jnp.where(row == 0, edge, pltpu.roll(pr, 1, axis=0))
    edge = jnp.where(first, 0.0, edge_ref[0, 0:1, :])
    return jnp.where(row == tt - 1, edge, pltpu.roll(pr, tt - 1, axis=0))


def _rwkv_pre(pr3, mix, w0, wup, a0, aup, kk_, ka_):
    bsz, s, _ = pr3.shape
    tt = PRE_TT

    def body(pr_ref, edge_ref, mix_ref, w0_ref, wup_ref, a0_ref, aup_ref, kk_ref, ka_ref, *outs):
        pr = pr_ref[0]
        prev = _shifted(pr_ref, edge_ref, pl.program_id(1) == 0, True)
        prs = pr + (prev - pr) * mix_ref[...]
        vals = _pre_fn(*_pre_pieces(prs), w0_ref[...], wup_ref[...].astype(F32), a0_ref[...], aup_ref[...].astype(F32), kk_ref[...],
                       ka_ref[...])
        for o, val in zip(outs, vals):
            o[0] = val

    vec = lambda n: pl.BlockSpec((1, n), lambda b, i: (0, 0))
    mat = pl.BlockSpec((LORA, WIDTH), lambda b, i: (0, 0))
    in_specs = [pl.BlockSpec((1, tt, PR_COLS), lambda b, i: (b, i, 0)),
                pl.BlockSpec((1, 8, PR_COLS), lambda b, i: (b, jnp.maximum(i * (tt // 8) - 1, 0), 0)),
                vec(PR_COLS), vec(WIDTH), mat, vec(WIDTH), mat, vec(WIDTH), vec(WIDTH)]
    out_spec = pl.BlockSpec((1, tt, WIDTH), lambda b, i: (b, i, 0))
    return pl.pallas_call(
        body, name="rwkv_pre", grid=(bsz, s // tt), in_specs=in_specs, out_specs=[out_spec] * 6,
        out_shape=[SDS((bsz, s, WIDTH), F32)] * 6, compiler_params=_params(("parallel", "parallel")))(
            pr3, pr3, mix, w0, wup, a0, aup, kk_, ka_)


def _rwkv_pre_bwd(pr3, cots, mix, w0, wup, a0, aup, kk_, ka_):
    bsz, s, _ = pr3.shape
    tt = PRE_TT

    def body(pr_ref, edge_ref, c0, c1, c2, c3, c4, c5, mix_ref, w0_ref, wup_ref, a0_ref, aup_ref, kk_ref, ka_ref,
             dprs_ref, dmix_ref, dw0_ref, dwup_ref, da0_ref, daup_ref, dkk_ref, dka_ref):
        pr = pr_ref[0]
        prev = _shifted(pr_ref, edge_ref, pl.program_id(1) == 0, True)
        prs = pr + (prev - pr) * mix_ref[...]
        _, vjp = jax.vjp(_pre_fn, *_pre_pieces(prs), w0_ref[...], wup_ref[...].astype(F32), a0_ref[...], aup_ref[...].astype(F32),
                         kk_ref[...], ka_ref[...])
        grads = vjp(tuple(c[0] for c in (c0, c1, c2, c3, c4, c5)))
        for piece, a, b in zip(grads[:5], PRE_SPLIT[:-1], PRE_SPLIT[1:]):
            dprs_ref[0, :, a:b] = piece
        dw0, dwup, da0, daup, dkk, dka = grads[5:]
        dprs = dprs_ref[0]
        grads = (jnp.sum(dprs * (prev - pr), axis=0, keepdims=True), dw0, dwup, da0, daup, dkk, dka)
        refs = (dmix_ref, dw0_ref, dwup_ref, da0_ref, daup_ref, dkk_ref, dka_ref)
        first = jnp.logical_and(pl.program_id(0) == 0, pl.program_id(1) == 0)

        @pl.when(first)
        def _():
            for r_, g_ in zip(refs, grads):
                r_[...] = g_

        @pl.when(jnp.logical_not(first))
        def _():
            for r_, g_ in zip(refs, grads):
                r_[...] += g_

    vec = lambda n: pl.BlockSpec((1, n), lambda b, i: (0, 0))
    mat = pl.BlockSpec((LORA, WIDTH), lambda b, i: (0, 0))
    tile = pl.BlockSpec((1, tt, WIDTH), lambda b, i: (b, i, 0))
    in_specs = [pl.BlockSpec((1, tt, PR_COLS), lambda b, i: (b, i, 0)),
                pl.BlockSpec((1, 8, PR_COLS), lambda b, i: (b, jnp.maximum(i * (tt // 8) - 1, 0), 0))]
    in_specs += [tile] * 6 + [vec(PR_COLS), vec(WIDTH), mat, vec(WIDTH), mat, vec(WIDTH), vec(WIDTH)]
    out_specs = [pl.BlockSpec((1, tt, PR_COLS), lambda b, i: (b, i, 0)), vec(PR_COLS), vec(WIDTH), mat, vec(WIDTH), mat,
                 vec(WIDTH), vec(WIDTH)]
    out_shape = [SDS((bsz, s, PR_COLS), F32), SDS((1, PR_COLS), F32), SDS((1, WIDTH), F32), SDS((LORA, WIDTH), F32),
                 SDS((1, WIDTH), F32), SDS((LORA, WIDTH), F32), SDS((1, WIDTH), F32), SDS((1, WIDTH), F32)]
    return pl.pallas_call(
        body, name="rwkv_pre_bwd", grid=(bsz, s // tt), in_specs=in_specs, out_specs=out_specs, out_shape=out_shape,
        compiler_params=_params(("arbitrary", "arbitrary")))(pr3, pr3, *cots, mix, w0, wup, a0, aup, kk_, ka_)


def _shift_bwd(dprs3, mix):
    bsz, s, _ = dprs3.shape
    tt = PRE_TT
    nt = s // tt

    def body(d_ref, edge_ref, mix_ref, o_ref):
        nxt = _shifted(d_ref, edge_ref, pl.program_id(1) == nt - 1, False)
        m = mix_ref[...]
        o_ref[0] = (d_ref[0] * (1.0 - m) + nxt * m).astype(BF16)

    in_specs = [pl.BlockSpec((1, tt, PR_COLS), lambda b, i: (b, i, 0)),
                pl.BlockSpec((1, 8, PR_COLS), lambda b, i: (b, jnp.minimum((i + 1) * (tt // 8), s // 8 - 1), 0)),
                pl.BlockSpec((1, PR_COLS), lambda b, i: (0, 0))]
    return pl.pallas_call(
        body, name="shift_bwd", grid=(bsz, nt), in_specs=in_specs, out_specs=pl.BlockSpec((1, tt, PR_COLS), lambda b, i: (b, i, 0)),
        out_shape=SDS((bsz, s, PR_COLS), BF16), compiler_params=_params(("parallel", "parallel")))(dprs3, dprs3, mix)


_NN, _NT, _TN = ((2,), (1,)), ((2,), (2,)), ((1,), (1,))


def _dot3(a, b, dims, precision=HI3):
    return lax.dot_general(a, b, (dims, ((0,), (0,))), precision=precision, preferred_element_type=F32)


def _dot3_bf16(a, b, dims):
    return lax.dot_general(a.astype(BF16), b.astype(BF16), (dims, ((0,), (0,))), preferred_element_type=F32)


class _Dots:
    def __init__(self, fwd):
        def make(dims, da_rule, db_rule):
            @jax.custom_vjp
            def f(a, b):
                return fwd(a, b, dims)

            f.defvjp(lambda a, b: (f(a, b), (a, b)), lambda res, g: (da_rule(*res, g), db_rule(*res, g)))
            return f

        one = _dot3_bf16
        self.mm = make(_NN, lambda a, b, g: one(g, b, _NT), lambda a, b, g: one(a, g, _TN))
        self.mm_nt = make(_NT, lambda a, b, g: one(g, b, _NN), lambda a, b, g: one(g, a, _TN))
        self.mm_tn = make(_TN, lambda a, b, g: one(b, g, _NT), lambda a, b, g: one(a, g, _NN))

        def powers(aab):
            ps = [aab]
            while 2 ** len(ps) < aab.shape[1]:
                ps.append(fwd(ps[-1], ps[-1], _NN))
            return ps

        def apply(ps, z, dims):
            for p in ps:
                z = z + fwd(p, z, dims)
            return z

        @jax.custom_vjp
        def solve(aab, z):
            return apply(powers(aab), z, _NN)

        def solve_fwd(aab, z):
            ps = powers(aab)
            x = apply(ps, z, _NN)
            return x, (ps, x)

        def solve_bwd(res, g):
            ps, x = res
            dz = apply(ps, g, _TN)
            return fwd(dz, x, _NT), dz

        solve.defvjp(solve_fwd, solve_bwd)
        self.solve = solve


_ACCURATE = _Dots(_dot3)
_ONE_PASS = _Dots(_dot3_bf16)
_bmm, _bmm_tn = _ACCURATE.mm, _ACCURATE.mm_tn


def _chunk_fn(s0t, r, lw, k, v, kkraw, a, rk, lnw, lnb, first=False, d=_ACCURATE):
    c = r.shape[1]
    at, rt, btc, ktc, gc, aab, arb, xv, arkv, ain, bin_ = _chunk_core(r, lw, k, v, kkraw, a, d)
    rs = d.mm(jnp.concatenate([at, rt], axis=1), s0t)
    u = d.solve(aab, rs[:, :c] + xv)
    y = rs[:, c:] + d.mm(arb, u) + arkv
    if first:
        y = _with_early_rows(y, r, lw, k, v, ain, bin_)
    gcol = jnp.sum(_diag(gc), axis=2, keepdims=True)
    sct = gcol * s0t + d.mm_tn(jnp.concatenate([btc, ktc], axis=1), jnp.concatenate([u, v], axis=1))
    return _post(y, r, k, v, rk, lnw, lnb), sct


def _diag(gc):
    return jnp.where(_masks(HEAD)[2], gc, 0.0)


def _with_early_rows(y, r, lw, k, v, ain, bin_):
    early = _early_rows(r[:2], lw[:2], k[:2], v[:2], ain[:2], bin_[:2])
    return jnp.concatenate([jnp.concatenate([early, y[:2, EARLY:]], axis=1), y[2:]], axis=0)


def _early_rows(r, lw, k, v, ain, bin_):
    cols = lambda x: _stack([jnp.transpose(x[h]) for h in range(2)])
    wc, bc, kc = cols(jnp.exp(lw)), cols(bin_), cols(k)
    st = jnp.zeros((2, HEAD, HEAD), F32)
    rows = []
    for t in range(EARLY):
        sa = _ONE_PASS.mm(ain[:, t:t + 1], st)
        st = st * wc[:, :, t:t + 1] + bc[:, :, t:t + 1] * sa + kc[:, :, t:t + 1] * v[:, t:t + 1]
        rows.append(_ONE_PASS.mm(r[:, t:t + 1], st))
    return jnp.concatenate(rows, axis=1)


def _chunk_rows(c):
    return pl.ds(c * CHUNK, CHUNK) if isinstance(c, int) else pl.ds(pl.multiple_of(c * CHUNK, CHUNK), CHUNK)


def _stack(xs):
    return jnp.concatenate([x[None] for x in xs], axis=0)


def _pairs(ref, chunks):
    tiles = [ref[0, _chunk_rows(c), :] for c in chunks]
    return _stack([t[:, HEAD * h:HEAD * h + HEAD] for t in tiles for h in range(2)])


def _unpair(vals, j):
    return jnp.concatenate([vals[2 * j], vals[2 * j + 1]], axis=1)


def _masks(c):
    ii = lax.broadcasted_iota(jnp.int32, (c, c), 0)
    jj = lax.broadcasted_iota(jnp.int32, (c, c), 1)
    return ii > jj, ii >= jj, ii == jj


def _chunk_core(r, lw, k, v, kkraw, a, d=_ACCURATE):
    g_, c = r.shape[0], r.shape[1]
    nrm = jnp.sqrt(jnp.sum(kkraw * kkraw, axis=-1, keepdims=True))
    kkn = kkraw / jnp.maximum(nrm, 1e-12)
    ain, bin_ = -kkn, kkn * a
    strict, incl, _ = _masks(c)
    lg = lax.dot_general(jnp.broadcast_to(incl.astype(F32), (g_, c, c)), lw, (((2,), (1,)), ((0,), (0,))), precision=HI,
                         preferred_element_type=F32)
    g, gp, gi = jnp.exp(lg), jnp.exp(lg - lw), jnp.exp(-lg)
    at, rt, bt, kt = ain * gp, r * g, bin_ * gi, k * gi
    aa = d.mm_nt(jnp.concatenate([at, rt], axis=1), jnp.concatenate([bt, kt], axis=1))
    aab = jnp.where(strict, aa[:, :c, :c], 0.0)
    aak = jnp.where(strict, aa[:, :c, c:], 0.0)
    arb = jnp.where(incl, aa[:, c:, :c], 0.0)
    ark = jnp.where(incl, aa[:, c:, c:], 0.0)
    akv = d.mm(jnp.concatenate([aak, ark], axis=1), v)
    gc = g[:, c - 1:c, :]
    return at, rt, bt * gc, kt * gc, gc, aab, arb, akv[:, :c], akv[:, c:], ain, bin_


def _post(y, r, k, v, rk, lnw, lnb):
    mu = jnp.mean(y, axis=-1, keepdims=True)
    var = jnp.mean(jnp.square(y - mu), axis=-1, keepdims=True)
    yn = (y - mu) * lax.rsqrt(var + GN_EPS) * lnw + lnb
    return yn + jnp.sum(r * k * rk, axis=-1, keepdims=True) * v


def _chunk_consts(r, lw, k, v, kkraw, a, first=False):
    at, rt, btc, ktc, gc, aab, arb, xv, arkv, ain, bin_ = _chunk_core(r, lw, k, v, kkraw, a)
    z = _ACCURATE.solve(aab, jnp.concatenate([at, xv], axis=2))
    ryv = jnp.concatenate([rt, arkv], axis=2) + _bmm(arb, z)
    if first:
        ryv = jnp.concatenate([ryv[:, :, :HEAD], _with_early_rows(ryv[:, :, HEAD:], r, lw, k, v, ain, bin_)], axis=2)
    mkv = _bmm_tn(btc, z) + jnp.concatenate([_diag(gc), _bmm_tn(ktc, v)], axis=2)
    return mkv, ryv


def _rwkv_scan(ins, rk, lnw, lnb):
    bsz, s, _ = ins[0].shape
    nch = s // CHUNK

    def consts_body(r_ref, lw_ref, k_ref, v_ref, kk_ref, a_ref, mkv_ref, ry_ref, yv_ref):
        def group(i, carry):
            chunks = [i * CHUNK_GROUP + j for j in range(CHUNK_GROUP)]
            mkv, ryv = _chunk_consts(*[_pairs(ref, chunks) for ref in (r_ref, lw_ref, k_ref, v_ref, kk_ref, a_ref)],
                                     first=isinstance(i, int) and i == 0)
            for j, c in enumerate(chunks):
                for h in range(2):
                    mkv_ref[0, 0, c, h] = mkv[2 * j + h]
                ry_ref[0, _chunk_rows(c), :] = jnp.concatenate([ryv[2 * j][:, :HEAD], ryv[2 * j + 1][:, :HEAD]], axis=1)
                yv_ref[0, _chunk_rows(c), :] = jnp.concatenate([ryv[2 * j][:, HEAD:], ryv[2 * j + 1][:, HEAD:]], axis=1)
            return carry

        group(0, 0)
        lax.fori_loop(1, nch // CHUNK_GROUP, group, 0)

    tile = pl.BlockSpec((1, s, LANE), lambda b, hp: (b, 0, hp))
    vec = pl.BlockSpec((1, LANE), lambda b, hp: (0, hp))
    mkv_spec = pl.BlockSpec((1, 1, nch, 2, HEAD, LANE), lambda b, hp: (b, hp, 0, 0, 0, 0))
    st_spec = pl.BlockSpec((1, 1, nch, 2, HEAD, HEAD), lambda b, hp: (b, hp, 0, 0, 0, 0))
    mkv, ry, yv = pl.pallas_call(
        consts_body, name="rwkv_consts", grid=(bsz, 4), in_specs=[tile] * 6, out_specs=[mkv_spec, tile, tile],
        out_shape=[SDS((bsz, 4, nch, 2, HEAD, LANE), F32), SDS((bsz, s, WIDTH), F32), SDS((bsz, s, WIDTH), F32)],
        compiler_params=_params(("parallel", "parallel")))(*ins)

    states = _chunk_recurrence(mkv, None, "rwkv_states")

    def out_body(ry_ref, yv_ref, r_ref, k_ref, v_ref, st_ref, rk_ref, lnw_ref, lnb_ref, o_ref):
        y, r, k, v, rk_, lnw_, lnb_ = _scan_rows(ry_ref, yv_ref, r_ref, k_ref, v_ref, st_ref, rk_ref, lnw_ref, lnb_ref)
        o = _post(y, r, k, v, rk_, lnw_, lnb_)
        for j in range(CHUNK_GROUP):
            o_ref[0, _chunk_rows(j), :] = _unpair(o, j)

    o = pl.pallas_call(
        out_body, name="rwkv_out", grid=(bsz, 4, nch // CHUNK_GROUP), in_specs=_group_specs(5), out_specs=_group_specs(1)[0],
        out_shape=SDS((bsz, s, WIDTH), F32),
        compiler_params=_params(("parallel", "parallel", "parallel")))(ry, yv, ins[0], ins[2], ins[3], states, rk, lnw, lnb)
    return o, states, (mkv, ry, yv)


def _group_specs(n_tiles):
    tile = pl.BlockSpec((1, CHUNK_GROUP * CHUNK, LANE), lambda b, hp, t: (b, t, hp))
    if n_tiles == 1:
        return [tile]
    st = pl.BlockSpec((1, 1, CHUNK_GROUP, 2, HEAD, HEAD), lambda b, hp, t: (b, hp, t, 0, 0, 0))
    vec = pl.BlockSpec((1, LANE), lambda b, hp, t: (0, hp))
    return [tile] * n_tiles + [st] + [vec] * 3


def _scan_rows(ry_ref, yv_ref, r_ref, k_ref, v_ref, st_ref, rk_ref, lnw_ref, lnb_ref):
    chunks = list(range(CHUNK_GROUP))
    ry, yv, r, k, v = (_pairs(ref, chunks) for ref in (ry_ref, yv_ref, r_ref, k_ref, v_ref))
    st = _stack([st_ref[0, 0, c, h] for c in chunks for h in range(2)])
    vecs = [_stack([ref[:, HEAD * h:HEAD * h + HEAD] for _ in chunks for h in range(2)]) for ref in (rk_ref, lnw_ref, lnb_ref)]
    return (_bmm(ry, st) + yv, r, k, v, *vecs)


def _chunk_recurrence(mkv, q, name):
    bsz, _, nch = mkv.shape[:3]
    pairs = [(hp, h) for hp in range(4) for h in range(2)]

    def body(*refs):
        mkv_ref, out_ref, acc = refs[0], refs[-2], refs[-1]
        acc[...] = jnp.zeros_like(acc)

        def step(i, carry):
            c = i if q is None else nch - 1 - i
            cur = acc[...]
            for j, (hp, h) in enumerate(pairs):
                out_ref[0, hp, c, h] = cur[j]
            m = _stack([mkv_ref[0, hp, c, h] for hp, h in pairs])
            if q is None:
                acc[...] = _bmm(m[:, :, :HEAD], cur) + m[:, :, HEAD:]
            else:
                acc[...] = _bmm_tn(m[:, :, :HEAD], cur) + _stack([refs[1][0, hp, c, h] for hp, h in pairs])
            return carry

        lax.fori_loop(0, nch, step, 0)

    spec = lambda w: pl.BlockSpec((1, 4, nch, 2, HEAD, w), lambda b: (b, 0, 0, 0, 0, 0))
    return pl.pallas_call(
        body, name=name, grid=(bsz,), in_specs=[spec(LANE)] + ([] if q is None else [spec(HEAD)]), out_specs=spec(HEAD),
        out_shape=SDS((bsz, 4, nch, 2, HEAD, HEAD), F32), scratch_shapes=[pltpu.VMEM((8, HEAD, HEAD), F32)],
        compiler_params=_params(("parallel",)))(*([mkv] if q is None else [mkv, q]))


def _rwkv_scan_bwd(ins, states, consts, do3, rk, lnw, lnb):
    bsz, s, _ = ins[0].shape
    nch = s // CHUNK

    mkv, ry, yv = consts

    def q_body(do_ref, ry_ref, yv_ref, r_ref, k_ref, v_ref, st_ref, rk_ref, lnw_ref, lnb_ref, q_ref):
        y, r, k, v, rk_, lnw_, lnb_ = _scan_rows(ry_ref, yv_ref, r_ref, k_ref, v_ref, st_ref, rk_ref, lnw_ref, lnb_ref)
        _, vjp = jax.vjp(lambda y_: _post(y_, r, k, v, rk_, lnw_, lnb_), y)
        (dy,) = vjp(_pairs(do_ref, list(range(CHUNK_GROUP))))
        q = _bmm_tn(_pairs(ry_ref, list(range(CHUNK_GROUP))), dy)
        for j in range(CHUNK_GROUP):
            for h in range(2):
                q_ref[0, 0, j, h] = q[2 * j + h]

    specs = _group_specs(6)
    q = pl.pallas_call(
        q_body, name="rwkv_q", grid=(bsz, 4, nch // CHUNK_GROUP), in_specs=specs, out_specs=specs[6],
        out_shape=SDS((bsz, 4, nch, 2, HEAD, HEAD), F32),
        compiler_params=_params(("parallel", "parallel", "parallel")))(do3, ry, yv, ins[0], ins[2], ins[3], states, rk, lnw, lnb)

    dstates = _chunk_recurrence(mkv, q, "rwkv_dstates")

    def body(r_ref, lw_ref, k_ref, v_ref, kk_ref, a_ref, st_ref, dst_ref, do_ref, rk_ref, lnw_ref, lnb_ref,
             dr_ref, dlw_ref, dk_ref, dv_ref, dkk_ref, da_ref, drk_ref, dlnw_ref, dlnb_ref):
        chunks = list(range(BWD_GROUP))
        par_refs = (drk_ref, dlnw_ref, dlnb_ref)

        @pl.when(jnp.logical_and(pl.program_id(1) == 0, pl.program_id(2) == 0))
        def _():
            for ref in par_refs:
                ref[...] = jnp.zeros_like(ref)

        def group(first):
            per_pair = lambda ref: _stack([ref[0, 0, c, h] for c in chunks for h in range(2)])
            vecs = [_stack([ref[:, HEAD * h:HEAD * h + HEAD] for _ in chunks for h in range(2)]) for ref in (rk_ref, lnw_ref, lnb_ref)]
            _, vjp = jax.vjp(functools.partial(_chunk_fn, first=first, d=_ONE_PASS), per_pair(st_ref),
                             *[_pairs(ref, chunks) for ref in (r_ref, lw_ref, k_ref, v_ref, kk_ref, a_ref)], *vecs)
            grads = vjp((_pairs(do_ref, chunks), per_pair(dst_ref)))
            for ref, cot in zip((dr_ref, dlw_ref, dk_ref, dv_ref, dkk_ref, da_ref), grads[1:7]):
                for j, c in enumerate(chunks):
                    ref[0, _chunk_rows(c), :] = _unpair(cot, j)
            for ref, g_ in zip(par_refs, grads[7:10]):
                ref[...] += jnp.concatenate([sum(g_[2 * j + h] for j in range(BWD_GROUP)) for h in range(2)], axis=1)

        pl.when(pl.program_id(2) == 0)(functools.partial(group, True))
        pl.when(pl.program_id(2) != 0)(functools.partial(group, False))

    tt = BWD_GROUP * CHUNK
    tile = pl.BlockSpec((1, tt, LANE), lambda hp, b, t: (b, t, hp))
    vec = pl.BlockSpec((1, LANE), lambda hp, b, t: (0, hp))
    st_spec = pl.BlockSpec((1, 1, BWD_GROUP, 2, HEAD, HEAD), lambda hp, b, t: (b, hp, t, 0, 0, 0))
    outs = pl.pallas_call(
        body, name="rwkv_scan_bwd", grid=(4, bsz, s // tt), in_specs=[tile] * 6 + [st_spec, st_spec, tile] + [vec] * 3,
        out_specs=[tile] * 6 + [vec] * 3,
        out_shape=[SDS((bsz, s, WIDTH), F32)] * 6 + [SDS((1, WIDTH), F32)] * 3,
        compiler_params=_params(("parallel", "arbitrary", "arbitrary")))(*ins, states, dstates, do3, rk, lnw, lnb)
    return outs[:6], outs[6:]


def _head(o_attn, o_rwkv, z_attn, z_rwkv, gm, x2, tgt, wua, wur, wout, g2):
    n = x2.shape[0]
    tm = 256
    nt = n // tm
    d = D_MODEL

    def body(oa_ref, or_ref, za_ref, zr_ref, gm_ref, x_ref, t_ref, wua_ref, wur_ref, wout_ref, g2_ref,
             dxo_ref, doa_ref, dor_ref, dza_ref, dzr_ref, dgm_ref, dwua_ref, dwur_ref, dwout_ref, dg2_ref, loss_ref, lacc):
        i = pl.program_id(0)
        oa, orw, za, zr = oa_ref[...], or_ref[...], za_ref[...], zr_ref[...]
        ga, gb = gm_ref[:, 0:d], gm_ref[:, d:2 * d]
        am = (oa * _silu(za)).astype(BF16)
        bm = (orw * _silu(zr)).astype(BF16)
        ya, yb = _dot(am, wua_ref[...]), _dot(bm, wur_ref[...])
        sa, sb = jax.nn.sigmoid(ga), jax.nn.sigmoid(gb)
        merged = (sa * ya + sb * yb).astype(BF16)
        out = _dot(merged, wout_ref[...])
        rs = lax.rsqrt(jnp.mean(out * out, axis=-1, keepdims=True) + RMS_EPS)
        g2 = g2_ref[...]
        err = x_ref[...] + out * rs * g2 - t_ref[...]
        lpart = jnp.sum(err * err, axis=0, keepdims=True)
        dxo = err * (1.0 / d)
        dxo_ref[...] = dxo
        dg2 = jnp.sum(dxo * out * rs, axis=0, keepdims=True)
        gd = dxo * g2
        dout = (rs * (gd - out * (rs * rs) * jnp.mean(gd * out, axis=-1, keepdims=True))).astype(BF16)
        dmerged = _dot_nt(dout, wout_ref[...])
        dwout = _dot_tn(merged, dout)
        dya, dyb = (dmerged * sa).astype(BF16), (dmerged * sb).astype(BF16)
        dgm_ref[:, 0:d] = (dmerged * ya * sa * (1.0 - sa)).astype(BF16)
        dgm_ref[:, d:2 * d] = (dmerged * yb * sb * (1.0 - sb)).astype(BF16)
        dam, dbm = _dot_nt(dya, wua_ref[...]), _dot_nt(dyb, wur_ref[...])
        dwua, dwur = _dot_tn(am, dya), _dot_tn(bm, dyb)
        doa_ref[...] = dam * _silu(za)
        dza_ref[...] = (dam * oa * _dsilu(za)).astype(BF16)
        dor_ref[...] = dbm * _silu(zr)
        dzr_ref[...] = (dbm * orw * _dsilu(zr)).astype(BF16)

        @pl.when(i == 0)
        def _():
            dwua_ref[...], dwur_ref[...], dwout_ref[...], dg2_ref[...], lacc[...] = dwua, dwur, dwout, dg2, lpart

        @pl.when(i != 0)
        def _():
            dwua_ref[...] += dwua
            dwur_ref[...] += dwur
            dwout_ref[...] += dwout
            dg2_ref[...] += dg2
            lacc[...] += lpart

        @pl.when(i == nt - 1)
        def _():
            loss_ref[...] = jnp.sum(lacc[...], axis=1, keepdims=True) * (0.5 / d)

    t512 = pl.BlockSpec((tm, WIDTH), lambda i: (i, 0))
    t1k = pl.BlockSpec((tm, d), lambda i: (i, 0))
    t2k = pl.BlockSpec((tm, 2 * d), lambda i: (i, 0))
    full = lambda r, c: pl.BlockSpec((r, c), lambda i: (0, 0))
    return pl.pallas_call(
        body, name="head_fwd_bwd", grid=(nt,),
        in_specs=[t512, t512, t512, t512, t2k, t1k, t1k, full(WIDTH, d), full(WIDTH, d), full(d, d), full(1, d)],
        out_specs=[t1k, t512, t512, t512, t512, t2k, full(WIDTH, d), full(WIDTH, d), full(d, d), full(1, d), full(1, 1)],
        out_shape=[SDS((n, d), F32), SDS((n, WIDTH), F32), SDS((n, WIDTH), F32), SDS((n, WIDTH), BF16), SDS((n, WIDTH), BF16),
                   SDS((n, 2 * d), BF16), SDS((WIDTH, d), F32), SDS((WIDTH, d), F32), SDS((d, d), F32), SDS((1, d), F32), SDS((1, 1), F32)],
        scratch_shapes=[pltpu.VMEM((1, d), F32)],
        compiler_params=_params(("arbitrary",)))(o_attn, o_rwkv, z_attn, z_rwkv, gm, x2, tgt, wua, wur, wout, g2)


def _prenorm_bwd(dh, x2, rs, g1, dxo):
    n, d = x2.shape
    tm = 1024

    def body(dh_ref, x_ref, rs_ref, g_ref, dxo_ref, gx_ref, dg_ref):
        x, r = x_ref[...], rs_ref[...]
        gd = dh_ref[...] * g_ref[...]
        gx_ref[...] = dxo_ref[...] + r * (gd - x * (r * r) * jnp.mean(gd * x, axis=-1, keepdims=True))
        dg = jnp.sum(dh_ref[...] * x * r, axis=0, keepdims=True)

        @pl.when(pl.program_id(0) == 0)
        def _():
            dg_ref[...] = dg

        @pl.when(pl.program_id(0) != 0)
        def _():
            dg_ref[...] += dg

    t = pl.BlockSpec((tm, d), lambda i: (i, 0))
    return pl.pallas_call(
        body, name="prenorm_bwd", grid=(n // tm,),
        in_specs=[t, t, pl.BlockSpec((tm, 1), lambda i: (i, 0)), pl.BlockSpec((1, d), lambda i: (0, 0)), t],
        out_specs=[t, pl.BlockSpec((1, d), lambda i: (0, 0))], out_shape=[SDS((n, d), F32), SDS((1, d), F32)],
        compiler_params=_params(("arbitrary",)))(dh, x2, rs, g1, dxo)


def _mesh_pos():
    x, y, c = lax.axis_index("x"), lax.axis_index("y"), lax.axis_index("c")
    return 4 * x + 2 * y + c


def _coords(idx):
    return (idx // 4, (idx // 2) % 2, idx % 2)


def _exchange(srcs, to_all, name):
    n = len(srcs)

    def body(*refs):
        src_refs, dst_refs = refs[:n], refs[n:2 * n]
        send_sems, recv_sems, local_sems = refs[2 * n:]
        me = _mesh_pos()

        def piece(i, j):
            return src_refs[i] if to_all[i] else src_refs[i].at[j]

        def remote(i, off, peer, block, slot):
            return pltpu.make_async_remote_copy(src_ref=piece(i, block), dst_ref=dst_refs[i].at[slot],
                                                send_sem=send_sems.at[i, off - 1], recv_sem=recv_sems.at[i, off - 1],
                                                device_id=_coords(peer), device_id_type=MESH)

        local = [pltpu.make_async_copy(piece(i, me), dst_refs[i].at[me], local_sems.at[i]) for i in range(n)]
        for cp in local:
            cp.start()
        sends = []
        for off in range(1, N_DEV):
            to = (me + off) % N_DEV
            for i in range(n):
                sends.append(remote(i, off, to, to, me))
                sends[-1].start()
        for off in range(1, N_DEV):
            frm = (me + N_DEV - off) % N_DEV
            for i in range(n):
                remote(i, off, frm, me, frm).wait_recv()
        for cp in sends:
            cp.wait_send()
        for cp in local:
            cp.wait()

    outs = pl.pallas_call(
        body, name=name, in_specs=[pl.BlockSpec(memory_space=pltpu.HBM)] * n, out_specs=[pl.BlockSpec(memory_space=pltpu.HBM)] * n,
        out_shape=[SDS((N_DEV,) + s.shape[-2:], s.dtype) for s in srcs],
        scratch_shapes=[pltpu.SemaphoreType.DMA((n, N_DEV - 1)), pltpu.SemaphoreType.DMA((n, N_DEV - 1)), pltpu.SemaphoreType.DMA((n,))],
        compiler_params=pltpu.CompilerParams())(*srcs)
    return outs


_HBM = pl.BlockSpec(memory_space=pltpu.HBM)
_SEM = pl.BlockSpec(memory_space=pltpu.SEMAPHORE)
_EFFECT = pltpu.SideEffectType.DATAFLOW_SIDE_EFFECTING


def _send_start(src):
    def body(src_ref, land_ref, send_sems, recv_sems, src_thru, land_thru, token):
        me = _mesh_pos()
        for off in range(1, N_DEV):
            to = (me + off) % N_DEV
            pltpu.make_async_remote_copy(src_ref=src_ref.at[to], dst_ref=land_ref.at[me], send_sem=send_sems.at[off - 1],
                                         recv_sem=recv_sems.at[off - 1], device_id=_coords(to), device_id_type=MESH).start()
        token[...] = jnp.zeros_like(token)

    hbm = pltpu.HBM(src.shape, src.dtype)
    return pl.pallas_call(
        body, name="grads_start",
        out_shape=(pltpu.SemaphoreType.DMA((N_DEV - 1,)), pltpu.SemaphoreType.DMA((N_DEV - 1,)), hbm, hbm, SDS((8, LANE), BF16)),
        in_specs=(_HBM, _HBM), out_specs=(_SEM, _SEM, _HBM, _HBM, pl.BlockSpec(memory_space=pltpu.VMEM)),
        input_output_aliases={0: 2, 1: 3}, compiler_params=pltpu.CompilerParams(has_side_effects=_EFFECT),
    )(pltpu.with_memory_space_constraint(src, pltpu.HBM), pltpu.with_memory_space_constraint(jnp.zeros(src.shape, src.dtype), pltpu.HBM))


def _send_wait(send_sems, recv_sems, src_thru, land_thru, after):
    def body(src_ref, land_ref, send_sems, recv_sems, after_ref, src_dead, got_ref):
        me = _mesh_pos()
        for off in range(1, N_DEV):
            to, frm = (me + off) % N_DEV, (me + N_DEV - off) % N_DEV
            pltpu.make_async_remote_copy(src_ref=src_ref.at[to], dst_ref=land_ref.at[me], send_sem=send_sems.at[off - 1],
                                         recv_sem=recv_sems.at[off - 1], device_id=_coords(to), device_id_type=MESH).wait_send()
            pltpu.make_async_remote_copy(src_ref=src_ref.at[me], dst_ref=land_ref.at[frm], send_sem=send_sems.at[off - 1],
                                         recv_sem=recv_sems.at[off - 1], device_id=_coords(frm), device_id_type=MESH).wait_recv()

    hbm = pltpu.HBM(src_thru.shape, src_thru.dtype)
    return pl.pallas_call(
        body, name="grads_wait", out_shape=(hbm, hbm), in_specs=(_HBM, _HBM, _SEM, _SEM, pl.BlockSpec(memory_space=pl.ANY)),
        out_specs=(_HBM, _HBM), input_output_aliases={0: 0, 1: 1}, compiler_params=pltpu.CompilerParams(has_side_effects=_EFFECT),
    )(src_thru, land_thru, send_sems, recv_sems, after)[1]


def _gather(srcs, name):
    n = len(srcs)

    def body(*refs):
        src_refs, dst_refs = refs[:n], refs[n:2 * n]
        send_sems, recv_sems, local_sems = refs[2 * n:]
        x, y, c = lax.axis_index("x"), lax.axis_index("y"), lax.axis_index("c")
        me, sibling = (x, y, c), (x, y, 1 - c)
        chips = [(1 - x, y), (x, 1 - y), (1 - x, 1 - y)]

        def slot(i, dev):
            return dst_refs[i].at[4 * dev[0] + 2 * dev[1] + dev[2]]

        def copy(i, k, block, to, own=False):
            return pltpu.make_async_remote_copy(src_ref=src_refs[i] if own else slot(i, block), dst_ref=slot(i, block),
                                                send_sem=send_sems.at[i, k], recv_sem=recv_sems.at[i, k],
                                                device_id=to, device_id_type=MESH)

        local = [pltpu.make_async_copy(src_refs[i], slot(i, me), local_sems.at[i]) for i in range(n)]
        for cp in local:
            cp.start()
        sends = []
        for i in range(n):
            sends.append(copy(i, 0, me, sibling, own=True))
            sends += [copy(i, 1 + j, me, (*chip, c), own=True) for j, chip in enumerate(chips)]
        for cp in sends:
            cp.start()
        for j, chip in enumerate(chips):
            for i in range(n):
                copy(i, 1 + j, (*chip, c), me).wait_recv()
                sends.append(copy(i, 4 + j, (*chip, c), sibling))
                sends[-1].start()
        for i in range(n):
            copy(i, 0, sibling, me).wait_recv()
            for j, chip in enumerate(chips):
                copy(i, 4 + j, (*chip, 1 - c), me).wait_recv()
        for cp in sends:
            cp.wait_send()
        for cp in local:
            cp.wait()

    return pl.pallas_call(
        body, name=name, in_specs=[pl.BlockSpec(memory_space=pltpu.HBM)] * n, out_specs=[pl.BlockSpec(memory_space=pltpu.HBM)] * n,
        out_shape=[SDS((N_DEV,) + s.shape, s.dtype) for s in srcs],
        scratch_shapes=[pltpu.SemaphoreType.DMA((n, N_DEV - 1)), pltpu.SemaphoreType.DMA((n, N_DEV - 1)), pltpu.SemaphoreType.DMA((n,))],
        compiler_params=pltpu.CompilerParams())(*srcs)


def _adamw(parts, w, m, v, tr, name, own=None):
    rows, cols = w.shape
    c1, c2 = 1.0 - ADAM_B1 ** ADAM_STEP, 1.0 - ADAM_B2 ** ADAM_STEP

    def body(p_ref, *refs):
        w_ref, m_ref, v_ref, g_ref, d_ref, nm_ref, nv_ref = refs[-7:]
        me = _mesh_pos()

        def part(j):
            return p_ref[j] if own is None else jnp.where(me == j, refs[0][...], p_ref[j])

        g = part(0).astype(F32)
        for j in range(1, N_DEV):
            g = g + part(j).astype(F32)
        nm = ADAM_B1 * m_ref[...] + (1.0 - ADAM_B1) * g
        nv = ADAM_B2 * v_ref[...] + (1.0 - ADAM_B2) * jnp.square(g)
        g_ref[...] = g
        nm_ref[...] = nm
        nv_ref[...] = nv
        d_ref[...] = -ADAM_LR * ((nm / c1) / (jnp.sqrt(nv / c2) + ADAM_EPS) + ADAM_WD * w_ref[...])

    t = pl.BlockSpec((tr, cols), lambda i: (i, 0))
    extra = [] if own is None else [own]
    return pl.pallas_call(
        body, name=name, grid=(rows // tr,), in_specs=[pl.BlockSpec((N_DEV, tr, cols), lambda i: (0, i, 0))] + [t] * (3 + len(extra)),
        out_specs=[t] * 4, out_shape=[SDS((rows, cols), F32)] * 4, compiler_params=_params(("parallel",)))(parts, *extra, w, m, v)


SHARDED = (("w_in", D_MODEL, IN_COLS // N_DEV, True, 128), ("w_up_attn", WIDTH, D_MODEL // N_DEV, True, WIDTH),
           ("w_up_rwkv", WIDTH, D_MODEL // N_DEV, True, WIDTH), ("w_out", D_MODEL // N_DEV, D_MODEL, False, D_MODEL // N_DEV),
           ("rwkv_w_up", LORA, WIDTH // N_DEV, True, LORA), ("rwkv_a_up", LORA, WIDTH // N_DEV, True, LORA))
LOSS_SLOT = sum(n for _, n in SMALL)


def _pack_small(small, extra=None):
    flat = [small[n].reshape(-1).astype(F32) for n, _ in SMALL]
    flat.append(jnp.zeros((1,), F32) if extra is None else extra.reshape(1))
    flat.append(jnp.zeros((SMALL_ROWS * LANE - LOSS_SLOT - 1,), F32))
    return jnp.concatenate(flat).reshape(SMALL_ROWS, LANE)


def _unpack_small(packed, shapes):
    flat = packed.reshape(-1)
    out, off = {}, 0
    for n, cnt in SMALL:
        out[n] = flat[off:off + cnt].reshape(shapes[n])
        off += cnt
    return out, flat[LOSS_SLOT]


def _whole(gathered, by_cols):
    if not by_cols:
        return gathered.reshape(-1, gathered.shape[-1])
    return gathered.transpose(1, 0, 2).reshape(gathered.shape[1], -1)


def _per_owner(full, by_cols):
    if not by_cols:
        return full.reshape(N_DEV, -1, full.shape[-1])
    return full.reshape(full.shape[0], N_DEV, -1).transpose(1, 0, 2)


def _local_step(x, loss_target, sm, wts):
    bsz, s, d = x.shape
    n = bsz * s
    x2, tgt = x.reshape(n, d), loss_target.reshape(n, d)
    bidx = jnp.asarray(_bucket_tables())
    w_in = wts["w_in"]
    segs = (("qkv", 0, QKV_COLS, 1536), ("za", OFF_ZA, WIDTH, 512), ("pr", OFF_PR, PR_COLS, PR_COLS), ("zr", OFF_ZR, WIDTH, 512),
            ("gm", OFF_GM, 2 * D_MODEL, 512))

    h, rs = _prenorm(x2, sm["pre_norm_gain"])
    proj = {nm: _mm(h, w_in[:, off:off + cnt], tn, "proj_" + nm) for nm, off, cnt, tn in segs}
    qkv3 = proj["qkv"].reshape(bsz, s, QKV_COLS)
    pr3 = proj["pr"].reshape(bsz, s, PR_COLS)

    o_attn, lse = _attn_fwd(qkv3, sm["rel_bias"], bidx)
    rk = sm["rwkv_r_k"].reshape(1, WIDTH)
    pre_args = (sm["rwkv_shift_mix"], sm["rwkv_w0"], wts["rwkv_w_up"], sm["rwkv_a0"], wts["rwkv_a_up"], sm["rwkv_k_k"], sm["rwkv_k_a"])
    scan_in = _rwkv_pre(pr3, *pre_args)
    o_rwkv, states, consts = _rwkv_scan(scan_in, rk, sm["rwkv_ln_w"], sm["rwkv_ln_b"])

    (dxo, do_attn, do_rwkv, dza, dzr, dgm, g_wua, g_wur, g_wout, g_post, loss) = _head(
        o_attn.reshape(n, WIDTH), o_rwkv.reshape(n, WIDTH), proj["za"], proj["zr"], proj["gm"], x2, tgt,
        wts["w_up_attn"], wts["w_up_rwkv"], wts["w_out"], sm["post_norm_gain"])

    dqkv, dbias = _attn_bwd(qkv3, o_attn, lse, do_attn.reshape(bsz, s, WIDTH), sm["rel_bias"], bidx)
    g_bias = _bias_grad(dbias, bidx)[:, :N_BUCKET].T

    scan_cots, (g_rk, g_lnw, g_lnb) = _rwkv_scan_bwd(scan_in, states, consts, do_rwkv.reshape(bsz, s, WIDTH), rk, sm["rwkv_ln_w"],
                                                     sm["rwkv_ln_b"])
    dprs, g_mix, g_w0, g_wup, g_a0, g_aup, g_kk, g_ka = _rwkv_pre_bwd(pr3, scan_cots, *pre_args)
    dpr = _shift_bwd(dprs, sm["rwkv_shift_mix"]).reshape(n, PR_COLS)

    dsegs = [(dqkv.reshape(9, n, WIDTH), 0, QKV_COLS, WIDTH), (dza, OFF_ZA, WIDTH, WIDTH), (dpr, OFF_PR, PR_COLS, PR_COLS),
             (dzr, OFF_ZR, WIDTH, WIDTH), (dgm, OFF_GM, 2 * D_MODEL, D_MODEL)]
    g_win = jnp.concatenate([_mm_tn(h, t, tn, "gw_in_%d" % j) for j, (t, _, _, tn) in enumerate(dsegs)], axis=1)
    blocks = _per_owner(g_win, True).astype(BF16)
    own = lax.dynamic_index_in_dim(blocks, 4 * lax.axis_index("x") + 2 * lax.axis_index("y") + lax.axis_index("c"), 0, keepdims=False)
    send_sems, recv_sems, blocks_thru, land_thru, token = _send_start(blocks)
    dh = None
    for j, (t, off, cnt, _) in enumerate(dsegs):
        dh = _mm_nt_acc(t, w_in[:, off:off + cnt] + token[0, 0], dh, "dh_%d" % j)
    grad_x, g_pre = _prenorm_bwd(dh, x2, rs, sm["pre_norm_gain"], dxo)
    landed = _send_wait(send_sems, recv_sems, blocks_thru, land_thru, g_pre)

    full = {"w_up_attn": g_wua, "w_up_rwkv": g_wur, "w_out": g_wout, "rwkv_w_up": g_wup, "rwkv_a_up": g_aup}
    small = {"pre_norm_gain": g_pre, "rel_bias": g_bias, "rwkv_shift_mix": g_mix, "rwkv_w0": g_w0, "rwkv_a0": g_a0, "rwkv_k_k": g_kk,
             "rwkv_k_a": g_ka, "rwkv_r_k": g_rk, "rwkv_ln_w": g_lnw, "rwkv_ln_b": g_lnb, "post_norm_gain": g_post}
    return loss[0, 0], grad_x.reshape(bsz, s, d), (landed, own), full, small


def kernel(x, pre_norm_gain, w_in, rel_bias, rwkv_shift_mix, rwkv_w0, rwkv_w_up, rwkv_a0, rwkv_a_up, rwkv_k_k, rwkv_k_a, rwkv_r_k, rwkv_ln_w, rwkv_ln_b, w_up_attn, w_up_rwkv, w_out, post_norm_gain, loss_target, m_pre_norm_gain, m_w_in, m_rel_bias, m_rwkv_shift_mix, m_rwkv_w0, m_rwkv_w_up, m_rwkv_a0, m_rwkv_a_up, m_rwkv_k_k, m_rwkv_k_a, m_rwkv_r_k, m_rwkv_ln_w, m_rwkv_ln_b, m_w_up_attn, m_w_up_rwkv, m_w_out, m_post_norm_gain, v_pre_norm_gain, v_w_in, v_rel_bias, v_rwkv_shift_mix, v_rwkv_w0, v_rwkv_w_up, v_rwkv_a0, v_rwkv_a_up, v_rwkv_k_k, v_rwkv_k_a, v_rwkv_r_k, v_rwkv_ln_w, v_rwkv_ln_b, v_w_up_attn, v_w_up_rwkv, v_w_out, v_post_norm_gain):
    names = [n for n, *_ in SHARDED] + [n for n, _ in SMALL]
    loc = dict(locals())
    w = {n: loc[n] for n in names}
    m = {n: loc["m_" + n] for n in names}
    v = {n: loc["v_" + n] for n in names}
    shapes = {n: w[n].shape for n in names}
    order = ["pre_norm_gain", "w_in", "rel_bias", "rwkv_shift_mix", "rwkv_w0", "rwkv_w_up", "rwkv_a0", "rwkv_a_up", "rwkv_k_k", "rwkv_k_a",
             "rwkv_r_k", "rwkv_ln_w", "rwkv_ln_b", "w_up_attn", "w_up_rwkv", "w_out", "post_norm_gain"]
    shard2d = lambda t, n, r, c: t[n].reshape(r, c)

    gathered = _gather([shard2d(w, n, r, c).astype(BF16) for n, r, c, _, _ in SHARDED], "gather_weights")
    wts = {n: _whole(g, by_cols) for (n, _, _, by_cols, _), g in zip(SHARDED, gathered)}

    loss, grad_x, (win_landed, win_own), full, small = _local_step(x, loss_target, w, wts)
    rest = SHARDED[1:]
    parts = _exchange([_per_owner(full[n], by_cols).astype(BF16) for n, _, _, by_cols, _ in rest] + [_pack_small(small, loss)],
                      [False] * len(rest) + [True], "exchange_grads")

    outs = [{}, {}, {}, {}]
    for (n, r, c, _, tr), p in zip(SHARDED, [win_landed] + list(parts)):
        res = _adamw(p, shard2d(w, n, r, c), shard2d(m, n, r, c), shard2d(v, n, r, c), tr, "adamw_" + n,
                     own=win_own if n == "w_in" else None)
        for o, t in zip(outs, res):
            o[n] = t.reshape(shapes[n])
    res = _adamw(parts[-1], _pack_small(w), _pack_small(m), _pack_small(v), SMALL_ROWS, "adamw_small")
    for o, t in zip(outs, res):
        o.update(_unpack_small(t, shapes)[0])
    loss = _unpack_small(res[0], shapes)[1]
    return (loss, grad_x, *[o[n] for o in outs for n in order])
```

```python
import functools
import math

import numpy as np
import jax
import jax.numpy as jnp
from jax import lax
from jax.experimental import pallas as pl
from jax.experimental.pallas import tpu as pltpu

F32, BF16 = jnp.float32, jnp.bfloat16
SDS = jax.ShapeDtypeStruct
HI = lax.Precision.HIGHEST
HI3 = lax.Precision.HIGH
MESH = pl.DeviceIdType.MESH

N_DEV = 8
D_MODEL = 1024
HEAD = 64
N_HEAD = 8
WIDTH = N_HEAD * HEAD
DILATIONS = (1, 4, 16)
QB = 128
N_BUCKET = 32
MAX_DIST = 2048
LORA = 64
QKV_COLS = 9 * WIDTH
PR_COLS = 3 * WIDTH + 2 * LORA
IN_COLS = QKV_COLS + WIDTH + PR_COLS + WIDTH + 2 * D_MODEL
OFF_ZA, OFF_PR, OFF_ZR, OFF_GM = QKV_COLS, QKV_COLS + WIDTH, QKV_COLS + WIDTH + PR_COLS, QKV_COLS + 2 * WIDTH + PR_COLS
RMS_EPS = 1e-6
GN_EPS = 64e-5
SCALE = 1.0 / math.sqrt(HEAD)
CHUNK = 64
CHUNK_GROUP = 8
BWD_GROUP = 16
EARLY = 8
NEG = -1e30
LANE = 128

ADAM_LR, ADAM_B1, ADAM_B2, ADAM_EPS, ADAM_WD, ADAM_STEP = 0.001, 0.9, 0.999, 1e-08, 0.01, 10

VMEM_LIMIT = 56 * 1024 * 1024

SMALL = (("pre_norm_gain", 1024), ("rel_bias", 768), ("rwkv_shift_mix", 1664), ("rwkv_w0", 512), ("rwkv_a0", 512),
         ("rwkv_k_k", 512), ("rwkv_k_a", 512), ("rwkv_r_k", 512), ("rwkv_ln_w", 512), ("rwkv_ln_b", 512),
         ("post_norm_gain", 1024))
SMALL_ROWS = 64


def _params(sem=None):
    return pltpu.CompilerParams(dimension_semantics=sem, vmem_limit_bytes=VMEM_LIMIT)


def _dot(a, b):
    return jnp.dot(a, b, preferred_element_type=F32)


def _dot_nt(a, b):
    return lax.dot_general(a, b, (((1,), (1,)), ((), ())), preferred_element_type=F32)


def _dot_tn(a, b):
    return lax.dot_general(a, b, (((0,), (0,)), ((), ())), preferred_element_type=F32)


@jax.custom_vjp
def _bdot(a, b):
    return _dot(a.astype(BF16), b.astype(BF16))


def _bdot_fwd(a, b):
    return _bdot(a, b), (a, b)


def _bdot_bwd(res, g):
    a, b = res
    gb = g.astype(BF16)
    return _dot_nt(gb, b.astype(BF16)), _dot_tn(a.astype(BF16), gb)


_bdot.defvjp(_bdot_fwd, _bdot_bwd)


def _silu(z):
    return z * jax.nn.sigmoid(z)


def _dsilu(z):
    s = jax.nn.sigmoid(z)
    return s * (1.0 + z * (1.0 - s))


def _softplus(x):
    return jnp.maximum(x, 0.0) + jnp.log(1.0 + jnp.exp(-jnp.abs(x)))


def _bucket_tables():
    qi = np.arange(QB)[:, None] + QB
    ki = np.arange(2 * QB)[None, :]
    rel = np.maximum(qi - ki, 0)
    out = []
    for d in DILATIONS:
        dist = rel * d
        max_exact = N_BUCKET // 2
        ratio = np.log(np.maximum(dist, 1).astype(np.float32) / max_exact) / np.float32(math.log(MAX_DIST / max_exact))
        large = max_exact + (ratio * (N_BUCKET - max_exact)).astype(np.int32)
        large = np.minimum(large, N_BUCKET - 1)
        out.append(np.where(dist < max_exact, dist, large).astype(np.int32))
    return np.stack(out)


def _prenorm(x2, g):
    n, d = x2.shape
    tm = 1024

    def body(x_ref, g_ref, h_ref, rs_ref):
        x = x_ref[...]
        rs = lax.rsqrt(jnp.mean(x * x, axis=-1, keepdims=True) + RMS_EPS)
        h_ref[...] = (x * rs * g_ref[...]).astype(BF16)
        rs_ref[...] = rs

    return pl.pallas_call(
        body, name="prenorm", grid=(n // tm,),
        in_specs=[pl.BlockSpec((tm, d), lambda i: (i, 0)), pl.BlockSpec((1, d), lambda i: (0, 0))],
        out_specs=[pl.BlockSpec((tm, d), lambda i: (i, 0)), pl.BlockSpec((tm, 1), lambda i: (i, 0))],
        out_shape=[SDS((n, d), BF16), SDS((n, 1), F32)], compiler_params=_params(("parallel",)))(x2, g)


def _mm(a, b, tn, name):
    m, k = a.shape
    n = b.shape[1]
    tm = 1024

    def body(a_ref, b_ref, o_ref):
        o_ref[...] = _dot(a_ref[...], b_ref[...])

    return pl.pallas_call(
        body, name=name, grid=(n // tn, m // tm),
        in_specs=[pl.BlockSpec((tm, k), lambda j, i: (i, 0)), pl.BlockSpec((k, tn), lambda j, i: (0, j))],
        out_specs=pl.BlockSpec((tm, tn), lambda j, i: (i, j)),
        out_shape=SDS((m, n), F32), compiler_params=_params(("parallel", "parallel")))(a, b)


def _mm_nt_acc(a, b, acc, name):
    split = a.ndim == 3
    m = a.shape[-2]
    k = b.shape[1]
    d = b.shape[0]
    tm = 1024
    per = 3 if split else 1
    seg = a.shape[2] if split else 0
    tk = per * seg if split else (k if k <= 2048 else 1536)
    have_acc = acc is not None

    def body(*refs):
        if have_acc:
            a_ref, b_ref, c_ref, o_ref = refs
        else:
            a_ref, b_ref, o_ref = refs
        if split:
            r = sum(_dot_nt(a_ref[j].astype(BF16), b_ref[:, seg * j:seg * (j + 1)]) for j in range(per))
        else:
            r = _dot_nt(a_ref[...].astype(BF16), b_ref[...])

        @pl.when(pl.program_id(1) == 0)
        def _():
            o_ref[...] = r + c_ref[...] if have_acc else r

        @pl.when(pl.program_id(1) != 0)
        def _():
            o_ref[...] += r

    a_spec = pl.BlockSpec((per, tm, seg), lambda i, j: (j, i, 0)) if split else pl.BlockSpec((tm, tk), lambda i, j: (i, j))
    in_specs = [a_spec, pl.BlockSpec((d, tk), lambda i, j: (0, j))]
    args = [a, b]
    if have_acc:
        in_specs.append(pl.BlockSpec((tm, d), lambda i, j: (i, 0)))
        args.append(acc)
    return pl.pallas_call(
        body, name=name, grid=(m // tm, k // tk), in_specs=in_specs, out_specs=pl.BlockSpec((tm, d), lambda i, j: (i, 0)),
        out_shape=SDS((m, d), F32), compiler_params=_params(("parallel", "arbitrary")))(*args)


def _mm_tn(a, b, tn, name):
    split = b.ndim == 3
    m, k1 = a.shape
    per = 3 if split else 1
    seg = b.shape[2] if split else tn
    tn = per * seg
    n2 = b.shape[0] * seg if split else b.shape[1]
    tm = 1024

    def body(a_ref, b_ref, o_ref):
        first = pl.program_id(1) == 0
        for j in range(per):
            r = _dot_tn(a_ref[...], (b_ref[j] if split else b_ref[...]).astype(BF16))
            cols = slice(seg * j, seg * (j + 1))

            @pl.when(first)
            def _(r=r, cols=cols):
                o_ref[:, cols] = r

            @pl.when(jnp.logical_not(first))
            def _(r=r, cols=cols):
                o_ref[:, cols] += r

    b_spec = pl.BlockSpec((per, tm, seg), lambda j, i: (j, i, 0)) if split else pl.BlockSpec((tm, tn), lambda j, i: (i, j))
    return pl.pallas_call(
        body, name=name, grid=(n2 // tn, m // tm),
        in_specs=[pl.BlockSpec((tm, k1), lambda j, i: (i, 0)), b_spec],
        out_specs=pl.BlockSpec((k1, tn), lambda j, i: (0, j)),
        out_shape=SDS((k1, n2), F32), compiler_params=_params(("parallel", "arbitrary")))(a, b)


def _ds(start, d):
    return pl.ds(start, QB) if d == 1 else pl.ds(start, QB, stride=d)


def _fill_bias(tab_ref, bidx_ref, bias_sc, hp):
    for g in range(3):
        bi = bidx_ref[g]
        for h in range(2):
            acc = jnp.zeros((QB, 2 * QB), F32)
            for j in range(N_BUCKET):
                acc = jnp.where(bi == j, tab_ref[j, g * N_HEAD + hp * 2 + h], acc)
            bias_sc[g * 2 + h] = acc


def _block_starts(it, d, nb):
    rho = it // nb
    n = it % nb
    st = rho + d * QB * n
    stp = rho + d * QB * jnp.maximum(n - 1, 0)
    if d == 1:
        st, stp = pl.multiple_of(QB * it, QB), pl.multiple_of(QB * jnp.maximum(it - 1, 0), QB)
    return st, stp, n > 0


ATTN_BLOCKS = 4


def _bdot3(a, b, dims):
    return lax.dot_general(a, b, (dims, ((0,), (0,))), preferred_element_type=F32)


def _attn_operands(q_ref, k_ref, v_ref, bias_sc, g, d, nb, it0):
    two = nb > 1
    nk = 2 * QB if two else QB
    ii = lax.broadcasted_iota(jnp.int32, (QB, nk), 0)
    cc = lax.broadcasted_iota(jnp.int32, (QB, nk), 1)
    qs, ks, vs, pens, starts = [], [], [], [], []
    for u in range(ATTN_BLOCKS):
        st, stp, hasprev = _block_starts(it0 + u, d, nb)
        qf = q_ref[0, _ds(st, d), :]
        if two:
            kf = jnp.concatenate([k_ref[0, _ds(stp, d), :], k_ref[0, _ds(st, d), :]], axis=0).astype(BF16)
            vf = jnp.concatenate([v_ref[0, _ds(stp, d), :], v_ref[0, _ds(st, d), :]], axis=0).astype(BF16)
            own = jnp.logical_and(cc >= QB, ii >= cc - QB)
            prev = jnp.logical_and(jnp.logical_and(cc < QB, cc >= ii), hasprev)
            pen = jnp.where(jnp.logical_or(own, prev), 0.0, NEG)
        else:
            kf, vf = k_ref[0, _ds(st, d), :].astype(BF16), v_ref[0, _ds(st, d), :].astype(BF16)
            pen = jnp.where(ii >= cc, 0.0, NEG)
        for h in range(2):
            qs.append(_one_head(qf, h).astype(BF16))
            ks.append(kf)
            vs.append(vf)
            pens.append(pen + (bias_sc[g * 2 + h] if two else bias_sc[g * 2 + h, :, QB:2 * QB]))
        starts.append((st, stp))
    return _stack(qs), _stack(ks), _stack(vs), _stack(pens), starts


def _one_head(x, h):
    lane = lax.broadcasted_iota(jnp.int32, x.shape, 1)
    return jnp.where(lane >= HEAD if h == 1 else lane < HEAD, x, 0.0)


def _pick_heads(x, u):
    lane = lax.broadcasted_iota(jnp.int32, x.shape[1:], 1)
    return jnp.where(lane < HEAD, x[2 * u], x[2 * u + 1])


def _add_heads(x, u):
    return x[2 * u] + x[2 * u + 1]


def _attn_fwd(qkv3, rel_bias, bidx):
    bsz, s, _ = qkv3.shape
    rt = 256

    def body(tab_ref, bidx_ref, *refs):
        q_refs, k_refs, v_refs = refs[0:3], refs[3:6], refs[6:9]
        o_ref, lse_ref = refs[9:11]
        bias_sc, num_sc, den_sc, m_sc = refs[11:]
        pl.when(pl.program_id(1) == 0)(lambda: _fill_bias(tab_ref, bidx_ref, bias_sc, pl.program_id(0)))
        for g, d in enumerate(DILATIONS):
            nb = s // (QB * d)

            def blk(it, c, g=g, d=d, nb=nb):
                q, k, v, bias, starts = _attn_operands(q_refs[g], k_refs[g], v_refs[g], bias_sc, g, d, nb, it * ATTN_BLOCKS)
                sc = _bdot3(q, k, ((2,), (2,))) * SCALE + bias
                m = jnp.max(sc, axis=-1, keepdims=True)
                p = jnp.exp(sc - m)
                den = jnp.sum(p, axis=-1, keepdims=True)
                num = _bdot3(p.astype(BF16), v, ((2,), (1,)))
                den, m = jnp.broadcast_to(den, num.shape), jnp.broadcast_to(m, num.shape)
                for u, (st, _) in enumerate(starts):
                    num_sc[g, _ds(st, d), :] = _pick_heads(num, u)
                    den_sc[g, _ds(st, d), :] = _pick_heads(den, u)
                    m_sc[g, _ds(st, d), :] = _pick_heads(m, u)
                return c

            lax.fori_loop(0, s // QB // ATTN_BLOCKS, blk, 0)

        def merge(i, c):
            rows = pl.ds(pl.multiple_of(i * rt, rt), rt)
            m0, m1, m2 = m_sc[0, rows, :], m_sc[1, rows, :], m_sc[2, rows, :]
            mall = jnp.maximum(jnp.maximum(m0, m1), m2)
            w0, w1, w2 = jnp.exp(m0 - mall), jnp.exp(m1 - mall), jnp.exp(m2 - mall)
            num = w0 * num_sc[0, rows, :] + w1 * num_sc[1, rows, :] + w2 * num_sc[2, rows, :]
            den = w0 * den_sc[0, rows, :] + w1 * den_sc[1, rows, :] + w2 * den_sc[2, rows, :]
            o_ref[0, rows, :] = num / den
            lse_ref[0, rows, :] = mall + jnp.log(den)
            return c

        lax.fori_loop(0, s // rt, merge, 0)

    col = lambda w, g: (lambda hp, b: (b, 0, (w * 3 + g) * 4 + hp))
    in_specs = [pl.BlockSpec(memory_space=pltpu.SMEM), pl.BlockSpec((3, QB, 2 * QB), lambda hp, b: (0, 0, 0))]
    in_specs += [pl.BlockSpec((1, s, LANE), col(w, g)) for w in range(3) for g in range(3)]
    out_spec = pl.BlockSpec((1, s, LANE), lambda hp, b: (b, 0, hp))
    return pl.pallas_call(
        body, name="attn_fwd", grid=(4, bsz), in_specs=in_specs, out_specs=[out_spec, out_spec],
        out_shape=[SDS((bsz, s, WIDTH), F32), SDS((bsz, s, WIDTH), F32)],
        scratch_shapes=[pltpu.VMEM((6, QB, 2 * QB), F32), pltpu.VMEM((3, s, LANE), F32), pltpu.VMEM((3, s, LANE), F32),
                        pltpu.VMEM((3, s, LANE), F32)],
        compiler_params=_params(("arbitrary", "arbitrary")))(rel_bias, bidx, *([qkv3] * 9))


def _attn_bwd(qkv3, o3, lse3, do3, rel_bias, bidx):
    bsz, s, _ = qkv3.shape
    rt = 256

    def body(tab_ref, bidx_ref, *refs):
        q_refs, k_refs, v_refs = refs[0:3], refs[3:6], refs[6:9]
        o_ref, lse_ref, do_ref, dqkv_ref, db_ref, bias_sc, delta_sc, acc_sc = refs[9:]
        dq_refs, dk_refs, dv_refs = ([acc_sc.at[w * 3 + g] for g in range(3)] for w in range(3))

        @pl.when(pl.program_id(1) == 0)
        def _():
            _fill_bias(tab_ref, bidx_ref, bias_sc, pl.program_id(0))
            db_ref[...] = jnp.zeros_like(db_ref)

        def prep(i, c):
            rows = pl.ds(pl.multiple_of(i * rt, rt), rt)
            prod = do_ref[0, rows, :] * o_ref[0, rows, :]
            d0 = jnp.sum(prod[:, :HEAD], axis=-1, keepdims=True)
            d1 = jnp.sum(prod[:, HEAD:], axis=-1, keepdims=True)
            delta_sc[rows, :] = jnp.concatenate([jnp.broadcast_to(d0, (rt, HEAD)), jnp.broadcast_to(d1, (rt, HEAD))], axis=1)
            z = jnp.zeros((rt, LANE), F32)
            for g in range(3):
                dk_refs[g][0, rows, :] = z
                dv_refs[g][0, rows, :] = z
            return c

        lax.fori_loop(0, s // rt, prep, 0)
        for g, d in enumerate(DILATIONS):
            nb = s // (QB * d)

            def blk(it, c, g=g, d=d, nb=nb):
                q, k, v, bias, starts = _attn_operands(q_refs[g], k_refs[g], v_refs[g], bias_sc, g, d, nb, it * ATTN_BLOCKS)
                dos, lses, deltas = [], [], []
                for st, _ in starts:
                    dof, lsef, delf = do_ref[0, _ds(st, d), :], lse_ref[0, _ds(st, d), :], delta_sc[_ds(st, d), :]
                    for h in range(2):
                        dos.append(_one_head(dof, h).astype(BF16))
                        lses.append(lsef[:, HEAD * h:HEAD * h + 1])
                        deltas.append(delf[:, HEAD * h:HEAD * h + 1])
                do, lse, delta = _stack(dos), _stack(lses), _stack(deltas)
                p = jnp.exp(_bdot3(q, k, ((2,), (2,))) * SCALE + bias - lse)
                dv = _bdot3(p.astype(BF16), do, ((1,), (1,)))
                ds = p * (_bdot3(do, v, ((2,), (2,))) - delta)
                dsb = ds.astype(BF16)
                dq = _bdot3(dsb, k, ((2,), (1,))) * SCALE
                dk = _bdot3(dsb, q, ((1,), (1,))) * SCALE
                two = nb > 1
                for h in range(2):
                    dsum = sum(ds[2 * u + h] for u in range(ATTN_BLOCKS))
                    if two:
                        db_ref[0, g * 2 + h] += dsum
                    else:
                        db_ref[0, g * 2 + h, :, QB:2 * QB] += dsum
                for u, (st, stp) in enumerate(starts):
                    dq_refs[g][0, _ds(st, d), :] = _pick_heads(dq, u)
                    if two:
                        dk_refs[g][0, _ds(stp, d), :] += _add_heads(dk[:, :QB], u)
                        dv_refs[g][0, _ds(stp, d), :] += _add_heads(dv[:, :QB], u)
                    dk_refs[g][0, _ds(st, d), :] += _add_heads(dk[:, QB:] if two else dk, u)
                    dv_refs[g][0, _ds(st, d), :] += _add_heads(dv[:, QB:] if two else dv, u)
                return c

            lax.fori_loop(0, s // QB // ATTN_BLOCKS, blk, 0)

        def flush(i, c):
            rows = pl.ds(pl.multiple_of(i * rt, rt), rt)
            for j in range(9):
                dqkv_ref[j, 0, rows, :] = acc_sc[j, 0, rows, :].astype(BF16)
            return c

        lax.fori_loop(0, s // rt, flush, 0)

    col = lambda w, g: (lambda hp, b: (b, 0, (w * 3 + g) * 4 + hp))
    blk_spec = pl.BlockSpec((1, s, LANE), lambda hp, b: (b, 0, hp))
    in_specs = [pl.BlockSpec(memory_space=pltpu.SMEM), pl.BlockSpec((3, QB, 2 * QB), lambda hp, b: (0, 0, 0))]
    in_specs += [pl.BlockSpec((1, s, LANE), col(w, g)) for w in range(3) for g in range(3)]
    in_specs += [blk_spec] * 3
    out_specs = [pl.BlockSpec((9, 1, s, LANE), lambda hp, b: (0, b, 0, hp)), pl.BlockSpec((1, 6, QB, 2 * QB), lambda hp, b: (hp, 0, 0, 0))]
    out_shape = [SDS((9, bsz, s, WIDTH), BF16), SDS((4, 6, QB, 2 * QB), F32)]
    return pl.pallas_call(
        body, name="attn_bwd", grid=(4, bsz), in_specs=in_specs, out_specs=out_specs, out_shape=out_shape,
        scratch_shapes=[pltpu.VMEM((6, QB, 2 * QB), F32), pltpu.VMEM((s, LANE), F32), pltpu.VMEM((9, 1, s, LANE), F32)],
        compiler_params=_params(("parallel", "arbitrary")))(rel_bias, bidx, *([qkv3] * 9), o3, lse3, do3)


def _bias_grad(dbias, bidx):
    def body(db_ref, bidx_ref, o_ref):
        lane = lax.broadcasted_iota(jnp.int32, (1, LANE), 1)
        for g in range(3):
            bi = bidx_ref[g]
            for hp in range(4):
                for h in range(2):
                    mat = db_ref[hp, g * 2 + h]
                    row = jnp.zeros((1, LANE), F32)
                    for j in range(N_BUCKET):
                        part = jnp.sum(jnp.where(bi == j, mat, 0.0), axis=0, keepdims=True)
                        row = jnp.where(lane == j, jnp.sum(part, axis=1, keepdims=True), row)
                    hd = g * N_HEAD + hp * 2 + h
                    o_ref[hd:hd + 1, :] = row

    return pl.pallas_call(body, name="bias_grad", out_shape=SDS((3 * N_HEAD, LANE), F32), compiler_params=_params())(dbias, bidx)


def _pre_fn(r, k0, v, wl, al, w0, wup, a0, aup, kk_, ka_):
    u = w0 + _bdot(jnp.tanh(wl), wup)
    lw = -jnp.exp(-_softplus(-u) - 0.5)
    a = jax.nn.sigmoid(a0 + _bdot(al, aup))
    kkraw = k0 * kk_
    k = k0 * (1.0 + (a - 1.0) * ka_)
    return r, lw, k, v, kkraw, a


PRE_SPLIT = (0, WIDTH, 2 * WIDTH, 3 * WIDTH, 3 * WIDTH + LORA, 3 * WIDTH + 2 * LORA)


def _pre_pieces(prs):
    return [prs[:, a:b] for a, b in zip(PRE_SPLIT[:-1], PRE_SPLIT[1:])]


PRE_TT = 512


def _shifted(pr_ref, edge_ref, first, back):
    pr = pr_ref[0]
    tt = pr.shape[0]
    row = lax.broadcasted_iota(jnp.int32, (tt, 1), 0)
    if back:
        edge = jnp.where(first, 0.0, edge_ref[0, 7:8, :])
        return jnp.where(row == 0, edge, pltpu.roll(pr, 1, axis=0))
    edge = jnp.where(first, 0.0, edge_ref[0, 0:1, :])
    return jnp.where(row == tt - 1, edge, pltpu.roll(pr, tt - 1, axis=0))


def _rwkv_pre(pr3, mix, w0, wup, a0, aup, kk_, ka_):
    bsz, s, _ = pr3.shape
    tt = PRE_TT

    def body(pr_ref, edge_ref, mix_ref, w0_ref, wup_ref, a0_ref, aup_ref, kk_ref, ka_ref, *outs):
        pr = pr_ref[0]
        prev = _shifted(pr_ref, edge_ref, pl.program_id(1) == 0, True)
        prs = pr + (prev - pr) * mix_ref[...]
        vals = _pre_fn(*_pre_pieces(prs), w0_ref[...], wup_ref[...].astype(F32), a0_ref[...], aup_ref[...].astype(F32), kk_ref[...],
                       ka_ref[...])
        for o, val in zip(outs, vals):
            o[0] = val

    vec = lambda n: pl.BlockSpec((1, n), lambda b, i: (0, 0))
    mat = pl.BlockSpec((LORA, WIDTH), lambda b, i: (0, 0))
    in_specs = [pl.BlockSpec((1, tt, PR_COLS), lambda b, i: (b, i, 0)),
                pl.BlockSpec((1, 8, PR_COLS), lambda b, i: (b, jnp.maximum(i * (tt // 8) - 1, 0), 0)),
                vec(PR_COLS), vec(WIDTH), mat, vec(WIDTH), mat, vec(WIDTH), vec(WIDTH)]
    out_spec = pl.BlockSpec((1, tt, WIDTH), lambda b, i: (b, i, 0))
    return pl.pallas_call(
        body, name="rwkv_pre", grid=(bsz, s // tt), in_specs=in_specs, out_specs=[out_spec] * 6,
        out_shape=[SDS((bsz, s, WIDTH), F32)] * 6, compiler_params=_params(("parallel", "parallel")))(
            pr3, pr3, mix, w0, wup, a0, aup, kk_, ka_)


def _rwkv_pre_bwd(pr3, cots, mix, w0, wup, a0, aup, kk_, ka_):
    bsz, s, _ = pr3.shape
    tt = PRE_TT

    def body(pr_ref, edge_ref, c0, c1, c2, c3, c4, c5, mix_ref, w0_ref, wup_ref, a0_ref, aup_ref, kk_ref, ka_ref,
             dprs_ref, dmix_ref, dw0_ref, dwup_ref, da0_ref, daup_ref, dkk_ref, dka_ref):
        pr = pr_ref[0]
        prev = _shifted(pr_ref, edge_ref, pl.program_id(1) == 0, True)
        prs = pr + (prev - pr) * mix_ref[...]
        _, vjp = jax.vjp(_pre_fn, *_pre_pieces(prs), w0_ref[...], wup_ref[...].astype(F32), a0_ref[...], aup_ref[...].astype(F32),
                         kk_ref[...], ka_ref[...])
        grads = vjp(tuple(c[0] for c in (c0, c1, c2, c3, c4, c5)))
        for piece, a, b in zip(grads[:5], PRE_SPLIT[:-1], PRE_SPLIT[1:]):
            dprs_ref[0, :, a:b] = piece
        dw0, dwup, da0, daup, dkk, dka = grads[5:]
        dprs = dprs_ref[0]
        grads = (jnp.sum(dprs * (prev - pr), axis=0, keepdims=True), dw0, dwup, da0, daup, dkk, dka)
        refs = (dmix_ref, dw0_ref, dwup_ref, da0_ref, daup_ref, dkk_ref, dka_ref)
        first = jnp.logical_and(pl.program_id(0) == 0, pl.program_id(1) == 0)

        @pl.when(first)
        def _():
            for r_, g_ in zip(refs, grads):
                r_[...] = g_

        @pl.when(jnp.logical_not(first))
        def _():
            for r_, g_ in zip(refs, grads):
                r_[...] += g_

    vec = lambda n: pl.BlockSpec((1, n), lambda b, i: (0, 0))
    mat = pl.BlockSpec((LORA, WIDTH), lambda b, i: (0, 0))
    tile = pl.BlockSpec((1, tt, WIDTH), lambda b, i: (b, i, 0))
    in_specs = [pl.BlockSpec((1, tt, PR_COLS), lambda b, i: (b, i, 0)),
                pl.BlockSpec((1, 8, PR_COLS), lambda b, i: (b, jnp.maximum(i * (tt // 8) - 1, 0), 0))]
    in_specs += [tile] * 6 + [vec(PR_COLS), vec(WIDTH), mat, vec(WIDTH), mat, vec(WIDTH), vec(WIDTH)]
    out_specs = [pl.BlockSpec((1, tt, PR_COLS), lambda b, i: (b, i, 0)), vec(PR_COLS), vec(WIDTH), mat, vec(WIDTH), mat,
                 vec(WIDTH), vec(WIDTH)]
    out_shape = [SDS((bsz, s, PR_COLS), F32), SDS((1, PR_COLS), F32), SDS((1, WIDTH), F32), SDS((LORA, WIDTH), F32),
                 SDS((1, WIDTH), F32), SDS((LORA, WIDTH), F32), SDS((1, WIDTH), F32), SDS((1, WIDTH), F32)]
    return pl.pallas_call(
        body, name="rwkv_pre_bwd", grid=(bsz, s // tt), in_specs=in_specs, out_specs=out_specs, out_shape=out_shape,
        compiler_params=_params(("arbitrary", "arbitrary")))(pr3, pr3, *cots, mix, w0, wup, a0, aup, kk_, ka_)


def _shift_bwd(dprs3, mix):
    bsz, s, _ = dprs3.shape
    tt = PRE_TT
    nt = s // tt

    def body(d_ref, edge_ref, mix_ref, o_ref):
        nxt = _shifted(d_ref, edge_ref, pl.program_id(1) == nt - 1, False)
        m = mix_ref[...]
        o_ref[0] = (d_ref[0] * (1.0 - m) + nxt * m).astype(BF16)

    in_specs = [pl.BlockSpec((1, tt, PR_COLS), lambda b, i: (b, i, 0)),
                pl.BlockSpec((1, 8, PR_COLS), lambda b, i: (b, jnp.minimum((i + 1) * (tt // 8), s // 8 - 1), 0)),
                pl.BlockSpec((1, PR_COLS), lambda b, i: (0, 0))]
    return pl.pallas_call(
        body, name="shift_bwd", grid=(bsz, nt), in_specs=in_specs, out_specs=pl.BlockSpec((1, tt, PR_COLS), lambda b, i: (b, i, 0)),
        out_shape=SDS((bsz, s, PR_COLS), BF16), compiler_params=_params(("parallel", "parallel")))(dprs3, dprs3, mix)


_NN, _NT, _TN = ((2,), (1,)), ((2,), (2,)), ((1,), (1,))


def _dot3(a, b, dims, precision=HI3):
    return lax.dot_general(a, b, (dims, ((0,), (0,))), precision=precision, preferred_element_type=F32)


def _dot3_bf16(a, b, dims):
    return lax.dot_general(a.astype(BF16), b.astype(BF16), (dims, ((0,), (0,))), preferred_element_type=F32)


class _Dots:
    def __init__(self, fwd):
        def make(dims, da_rule, db_rule):
            @jax.custom_vjp
            def f(a, b):
                return fwd(a, b, dims)

            f.defvjp(lambda a, b: (f(a, b), (a, b)), lambda res, g: (da_rule(*res, g), db_rule(*res, g)))
            return f

        one = _dot3_bf16
        self.mm = make(_NN, lambda a, b, g: one(g, b, _NT), lambda a, b, g: one(a, g, _TN))
        self.mm_nt = make(_NT, lambda a, b, g: one(g, b, _NN), lambda a, b, g: one(g, a, _TN))
        self.mm_tn = make(_TN, lambda a, b, g: one(b, g, _NT), lambda a, b, g: one(a, g, _NN))

        def powers(aab):
            ps = [aab]
            while 2 ** len(ps) < aab.shape[1]:
                ps.append(fwd(ps[-1], ps[-1], _NN))
            return ps

        def apply(ps, z, dims):
            for p in ps:
                z = z + fwd(p, z, dims)
            return z

        @jax.custom_vjp
        def solve(aab, z):
            return apply(powers(aab), z, _NN)

        def solve_fwd(aab, z):
            ps = powers(aab)
            x = apply(ps, z, _NN)
            return x, (ps, x)

        def solve_bwd(res, g):
            ps, x = res
            dz = apply(ps, g, _TN)
            return fwd(dz, x, _NT), dz

        solve.defvjp(solve_fwd, solve_bwd)
        self.solve = solve


_ACCURATE = _Dots(_dot3)
_ONE_PASS = _Dots(_dot3_bf16)
_bmm, _bmm_tn = _ACCURATE.mm, _ACCURATE.mm_tn


def _chunk_fn(s0t, r, lw, k, v, kkraw, a, rk, lnw, lnb, first=False, d=_ACCURATE):
    c = r.shape[1]
    at, rt, btc, ktc, gc, aab, arb, xv, arkv, ain, bin_ = _chunk_core(r, lw, k, v, kkraw, a, d)
    rs = d.mm(jnp.concatenate([at, rt], axis=1), s0t)
    u = d.solve(aab, rs[:, :c] + xv)
    y = rs[:, c:] + d.mm(arb, u) + arkv
    if first:
        y = _with_early_rows(y, r, lw, k, v, ain, bin_)
    gcol = jnp.sum(_diag(gc), axis=2, keepdims=True)
    sct = gcol * s0t + d.mm_tn(jnp.concatenate([btc, ktc], axis=1), jnp.concatenate([u, v], axis=1))
    return _post(y, r, k, v, rk, lnw, lnb), sct


def _diag(gc):
    return jnp.where(_masks(HEAD)[2], gc, 0.0)


def _with_early_rows(y, r, lw, k, v, ain, bin_):
    early = _early_rows(r[:2], lw[:2], k[:2], v[:2], ain[:2], bin_[:2])
    return jnp.concatenate([jnp.concatenate([early, y[:2, EARLY:]], axis=1), y[2:]], axis=0)


def _early_rows(r, lw, k, v, ain, bin_):
    cols = lambda x: _stack([jnp.transpose(x[h]) for h in range(2)])
    wc, bc, kc = cols(jnp.exp(lw)), cols(bin_), cols(k)
    st = jnp.zeros((2, HEAD, HEAD), F32)
    rows = []
    for t in range(EARLY):
        sa = _ONE_PASS.mm(ain[:, t:t + 1], st)
        st = st * wc[:, :, t:t + 1] + bc[:, :, t:t + 1] * sa + kc[:, :, t:t + 1] * v[:, t:t + 1]
        rows.append(_ONE_PASS.mm(r[:, t:t + 1], st))
    return jnp.concatenate(rows, axis=1)


def _chunk_rows(c):
    return pl.ds(c * CHUNK, CHUNK) if isinstance(c, int) else pl.ds(pl.multiple_of(c * CHUNK, CHUNK), CHUNK)


def _stack(xs):
    return jnp.concatenate([x[None] for x in xs], axis=0)


def _pairs(ref, chunks):
    tiles = [ref[0, _chunk_rows(c), :] for c in chunks]
    return _stack([t[:, HEAD * h:HEAD * h + HEAD] for t in tiles for h in range(2)])


def _unpair(vals, j):
    return jnp.concatenate([vals[2 * j], vals[2 * j + 1]], axis=1)


def _masks(c):
    ii = lax.broadcasted_iota(jnp.int32, (c, c), 0)
    jj = lax.broadcasted_iota(jnp.int32, (c, c), 1)
    return ii > jj, ii >= jj, ii == jj


def _chunk_core(r, lw, k, v, kkraw, a, d=_ACCURATE):
    g_, c = r.shape[0], r.shape[1]
    nrm = jnp.sqrt(jnp.sum(kkraw * kkraw, axis=-1, keepdims=True))
    kkn = kkraw / jnp.maximum(nrm, 1e-12)
    ain, bin_ = -kkn, kkn * a
    strict, incl, _ = _masks(c)
    lg = lax.dot_general(jnp.broadcast_to(incl.astype(F32), (g_, c, c)), lw, (((2,), (1,)), ((0,), (0,))), precision=HI,
                         preferred_element_type=F32)
    g, gp, gi = jnp.exp(lg), jnp.exp(lg - lw), jnp.exp(-lg)
    at, rt, bt, kt = ain * gp, r * g, bin_ * gi, k * gi
    aa = d.mm_nt(jnp.concatenate([at, rt], axis=1), jnp.concatenate([bt, kt], axis=1))
    aab = jnp.where(strict, aa[:, :c, :c], 0.0)
    aak = jnp.where(strict, aa[:, :c, c:], 0.0)
    arb = jnp.where(incl, aa[:, c:, :c], 0.0)
    ark = jnp.where(incl, aa[:, c:, c:], 0.0)
    akv = d.mm(jnp.concatenate([aak, ark], axis=1), v)
    gc = g[:, c - 1:c, :]
    return at, rt, bt * gc, kt * gc, gc, aab, arb, akv[:, :c], akv[:, c:], ain, bin_


def _post(y, r, k, v, rk, lnw, lnb):
    mu = jnp.mean(y, axis=-1, keepdims=True)
    var = jnp.mean(jnp.square(y - mu), axis=-1, keepdims=True)
    yn = (y - mu) * lax.rsqrt(var + GN_EPS) * lnw + lnb
    return yn + jnp.sum(r * k * rk, axis=-1, keepdims=True) * v


def _chunk_consts(r, lw, k, v, kkraw, a, first=False):
    at, rt, btc, ktc, gc, aab, arb, xv, arkv, ain, bin_ = _chunk_core(r, lw, k, v, kkraw, a)
    z = _ACCURATE.solve(aab, jnp.concatenate([at, xv], axis=2))
    ryv = jnp.concatenate([rt, arkv], axis=2) + _bmm(arb, z)
    if first:
        ryv = jnp.concatenate([ryv[:, :, :HEAD], _with_early_rows(ryv[:, :, HEAD:], r, lw, k, v, ain, bin_)], axis=2)
    mkv = _bmm_tn(btc, z) + jnp.concatenate([_diag(gc), _bmm_tn(ktc, v)], axis=2)
    return mkv, ryv


def _rwkv_scan(ins, rk, lnw, lnb):
    bsz, s, _ = ins[0].shape
    nch = s // CHUNK

    def consts_body(r_ref, lw_ref, k_ref, v_ref, kk_ref, a_ref, mkv_ref, ry_ref, yv_ref):
        def group(i, carry):
            chunks = [i * CHUNK_GROUP + j for j in range(CHUNK_GROUP)]
            mkv, ryv = _chunk_consts(*[_pairs(ref, chunks) for ref in (r_ref, lw_ref, k_ref, v_ref, kk_ref, a_ref)],
                                     first=isinstance(i, int) and i == 0)
            for j, c in enumerate(chunks):
                for h in range(2):
                    mkv_ref[0, 0, c, h] = mkv[2 * j + h]
                ry_ref[0, _chunk_rows(c), :] = jnp.concatenate([ryv[2 * j][:, :HEAD], ryv[2 * j + 1][:, :HEAD]], axis=1)
                yv_ref[0, _chunk_rows(c), :] = jnp.concatenate([ryv[2 * j][:, HEAD:], ryv[2 * j + 1][:, HEAD:]], axis=1)
            return carry

        group(0, 0)
        lax.fori_loop(1, nch // CHUNK_GROUP, group, 0)

    tile = pl.BlockSpec((1, s, LANE), lambda b, hp: (b, 0, hp))
    vec = pl.BlockSpec((1, LANE), lambda b, hp: (0, hp))
    mkv_spec = pl.BlockSpec((1, 1, nch, 2, HEAD, LANE), lambda b, hp: (b, hp, 0, 0, 0, 0))
    st_spec = pl.BlockSpec((1, 1, nch, 2, HEAD, HEAD), lambda b, hp: (b, hp, 0, 0, 0, 0))
    mkv, ry, yv = pl.pallas_call(
        consts_body, name="rwkv_consts", grid=(bsz, 4), in_specs=[tile] * 6, out_specs=[mkv_spec, tile, tile],
        out_shape=[SDS((bsz, 4, nch, 2, HEAD, LANE), F32), SDS((bsz, s, WIDTH), F32), SDS((bsz, s, WIDTH), F32)],
        compiler_params=_params(("parallel", "parallel")))(*ins)

    states = _chunk_recurrence(mkv, None, "rwkv_states")

    def out_body(ry_ref, yv_ref, r_ref, k_ref, v_ref, st_ref, rk_ref, lnw_ref, lnb_ref, o_ref):
        y, r, k, v, rk_, lnw_, lnb_ = _scan_rows(ry_ref, yv_ref, r_ref, k_ref, v_ref, st_ref, rk_ref, lnw_ref, lnb_ref)
        o = _post(y, r, k, v, rk_, lnw_, lnb_)
        for j in range(CHUNK_GROUP):
            o_ref[0, _chunk_rows(j), :] = _unpair(o, j)

    o = pl.pallas_call(
        out_body, name="rwkv_out", grid=(bsz, 4, nch // CHUNK_GROUP), in_specs=_group_specs(5), out_specs=_group_specs(1)[0],
        out_shape=SDS((bsz, s, WIDTH), F32),
        compiler_params=_params(("parallel", "parallel", "parallel")))(ry, yv, ins[0], ins[2], ins[3], states, rk, lnw, lnb)
    return o, states, (mkv, ry, yv)


def _group_specs(n_tiles):
    tile = pl.BlockSpec((1, CHUNK_GROUP * CHUNK, LANE), lambda b, hp, t: (b, t, hp))
    if n_tiles == 1:
        return [tile]
    st = pl.BlockSpec((1, 1, CHUNK_GROUP, 2, HEAD, HEAD), lambda b, hp, t: (b, hp, t, 0, 0, 0))
    vec = pl.BlockSpec((1, LANE), lambda b, hp, t: (0, hp))
    return [tile] * n_tiles + [st] + [vec] * 3


def _scan_rows(ry_ref, yv_ref, r_ref, k_ref, v_ref, st_ref, rk_ref, lnw_ref, lnb_ref):
    chunks = list(range(CHUNK_GROUP))
    ry, yv, r, k, v = (_pairs(ref, chunks) for ref in (ry_ref, yv_ref, r_ref, k_ref, v_ref))
    st = _stack([st_ref[0, 0, c, h] for c in chunks for h in range(2)])
    vecs = [_stack([ref[:, HEAD * h:HEAD * h + HEAD] for _ in chunks for h in range(2)]) for ref in (rk_ref, lnw_ref, lnb_ref)]
    return (_bmm(ry, st) + yv, r, k, v, *vecs)


def _chunk_recurrence(mkv, q, name):
    bsz, _, nch = mkv.shape[:3]
    pairs = [(hp, h) for hp in range(4) for h in range(2)]

    def body(*refs):
        mkv_ref, out_ref, acc = refs[0], refs[-2], refs[-1]
        acc[...] = jnp.zeros_like(acc)

        def step(i, carry):
            c = i if q is None else nch - 1 - i
            cur = acc[...]
            for j, (hp, h) in enumerate(pairs):
                out_ref[0, hp, c, h] = cur[j]
            m = _stack([mkv_ref[0, hp, c, h] for hp, h in pairs])
            if q is None:
                acc[...] = _bmm(m[:, :, :HEAD], cur) + m[:, :, HEAD:]
            else:
                acc[...] = _bmm_tn(m[:, :, :HEAD], cur) + _stack([refs[1][0, hp, c, h] for hp, h in pairs])
            return carry

        lax.fori_loop(0, nch, step, 0)

    spec = lambda w: pl.BlockSpec((1, 4, nch, 2, HEAD, w), lambda b: (b, 0, 0, 0, 0, 0))
    return pl.pallas_call(
        body, name=name, grid=(bsz,), in_specs=[spec(LANE)] + ([] if q is None else [spec(HEAD)]), out_specs=spec(HEAD),
        out_shape=SDS((bsz, 4, nch, 2, HEAD, HEAD), F32), scratch_shapes=[pltpu.VMEM((8, HEAD, HEAD), F32)],
        compiler_params=_params(("parallel",)))(*([mkv] if q is None else [mkv, q]))


def _rwkv_scan_bwd(ins, states, consts, do3, rk, lnw, lnb):
    bsz, s, _ = ins[0].shape
    nch = s // CHUNK

    mkv, ry, yv = consts

    def q_body(do_ref, ry_ref, yv_ref, r_ref, k_ref, v_ref, st_ref, rk_ref, lnw_ref, lnb_ref, q_ref):
        y, r, k, v, rk_, lnw_, lnb_ = _scan_rows(ry_ref, yv_ref, r_ref, k_ref, v_ref, st_ref, rk_ref, lnw_ref, lnb_ref)
        _, vjp = jax.vjp(lambda y_: _post(y_, r, k, v, rk_, lnw_, lnb_), y)
        (dy,) = vjp(_pairs(do_ref, list(range(CHUNK_GROUP))))
        q = _bmm_tn(_pairs(ry_ref, list(range(CHUNK_GROUP))), dy)
        for j in range(CHUNK_GROUP):
            for h in range(2):
                q_ref[0, 0, j, h] = q[2 * j + h]

    specs = _group_specs(6)
    q = pl.pallas_call(
        q_body, name="rwkv_q", grid=(bsz, 4, nch // CHUNK_GROUP), in_specs=specs, out_specs=specs[6],
        out_shape=SDS((bsz, 4, nch, 2, HEAD, HEAD), F32),
        compiler_params=_params(("parallel", "parallel", "parallel")))(do3, ry, yv, ins[0], ins[2], ins[3], states, rk, lnw, lnb)

    dstates = _chunk_recurrence(mkv, q, "rwkv_dstates")

    def body(r_ref, lw_ref, k_ref, v_ref, kk_ref, a_ref, st_ref, dst_ref, do_ref, rk_ref, lnw_ref, lnb_ref,
             dr_ref, dlw_ref, dk_ref, dv_ref, dkk_ref, da_ref, drk_ref, dlnw_ref, dlnb_ref):
        chunks = list(range(BWD_GROUP))
        par_refs = (drk_ref, dlnw_ref, dlnb_ref)

        @pl.when(jnp.logical_and(pl.program_id(1) == 0, pl.program_id(2) == 0))
        def _():
            for ref in par_refs:
                ref[...] = jnp.zeros_like(ref)

        def group(first):
            per_pair = lambda ref: _stack([ref[0, 0, c, h] for c in chunks for h in range(2)])
            vecs = [_stack([ref[:, HEAD * h:HEAD * h + HEAD] for _ in chunks for h in range(2)]) for ref in (rk_ref, lnw_ref, lnb_ref)]
            _, vjp = jax.vjp(functools.partial(_chunk_fn, first=first, d=_ONE_PASS), per_pair(st_ref),
                             *[_pairs(ref, chunks) for ref in (r_ref, lw_ref, k_ref, v_ref, kk_ref, a_ref)], *vecs)
            grads = vjp((_pairs(do_ref, chunks), per_pair(dst_ref)))
            for ref, cot in zip((dr_ref, dlw_ref, dk_ref, dv_ref, dkk_ref, da_ref), grads[1:7]):
                for j, c in enumerate(chunks):
                    ref[0, _chunk_rows(c), :] = _unpair(cot, j)
            for ref, g_ in zip(par_refs, grads[7:10]):
                ref[...] += jnp.concatenate([sum(g_[2 * j + h] for j in range(BWD_GROUP)) for h in range(2)], axis=1)

        pl.when(pl.program_id(2) == 0)(functools.partial(group, True))
        pl.when(pl.program_id(2) != 0)(functools.partial(group, False))

    tt = BWD_GROUP * CHUNK
    tile = pl.BlockSpec((1, tt, LANE), lambda hp, b, t: (b, t, hp))
    vec = pl.BlockSpec((1, LANE), lambda hp, b, t: (0, hp))
    st_spec = pl.BlockSpec((1, 1, BWD_GROUP, 2, HEAD, HEAD), lambda hp, b, t: (b, hp, t, 0, 0, 0))
    outs = pl.pallas_call(
        body, name="rwkv_scan_bwd", grid=(4, bsz, s // tt), in_specs=[tile] * 6 + [st_spec, st_spec, tile] + [vec] * 3,
        out_specs=[tile] * 6 + [vec] * 3,
        out_shape=[SDS((bsz, s, WIDTH), F32)] * 6 + [SDS((1, WIDTH), F32)] * 3,
        compiler_params=_params(("parallel", "arbitrary", "arbitrary")))(*ins, states, dstates, do3, rk, lnw, lnb)
    return outs[:6], outs[6:]


def _head(o_attn, o_rwkv, z_attn, z_rwkv, gm, x2, tgt, wua, wur, wout, g2):
    n = x2.shape[0]
    tm = 256
    nt = n // tm
    d = D_MODEL

    def body(oa_ref, or_ref, za_ref, zr_ref, gm_ref, x_ref, t_ref, wua_ref, wur_ref, wout_ref, g2_ref,
             dxo_ref, doa_ref, dor_ref, dza_ref, dzr_ref, dgm_ref, dwua_ref, dwur_ref, dwout_ref, dg2_ref, loss_ref, lacc):
        i = pl.program_id(0)
        oa, orw, za, zr = oa_ref[...], or_ref[...], za_ref[...], zr_ref[...]
        ga, gb = gm_ref[:, 0:d], gm_ref[:, d:2 * d]
        am = (oa * _silu(za)).astype(BF16)
        bm = (orw * _silu(zr)).astype(BF16)
        ya, yb = _dot(am, wua_ref[...]), _dot(bm, wur_ref[...])
        sa, sb = jax.nn.sigmoid(ga), jax.nn.sigmoid(gb)
        merged = (sa * ya + sb * yb).astype(BF16)
        out = _dot(merged, wout_ref[...])
        rs = lax.rsqrt(jnp.mean(out * out, axis=-1, keepdims=True) + RMS_EPS)
        g2 = g2_ref[...]
        err = x_ref[...] + out * rs * g2 - t_ref[...]
        lpart = jnp.sum(err * err, axis=0, keepdims=True)
        dxo = err * (1.0 / d)
        dxo_ref[...] = dxo
        dg2 = jnp.sum(dxo * out * rs, axis=0, keepdims=True)
        gd = dxo * g2
        dout = (rs * (gd - out * (rs * rs) * jnp.mean(gd * out, axis=-1, keepdims=True))).astype(BF16)
        dmerged = _dot_nt(dout, wout_ref[...])
        dwout = _dot_tn(merged, dout)
        dya, dyb = (dmerged * sa).astype(BF16), (dmerged * sb).astype(BF16)
        dgm_ref[:, 0:d] = (dmerged * ya * sa * (1.0 - sa)).astype(BF16)
        dgm_ref[:, d:2 * d] = (dmerged * yb * sb * (1.0 - sb)).astype(BF16)
        dam, dbm = _dot_nt(dya, wua_ref[...]), _dot_nt(dyb, wur_ref[...])
        dwua, dwur = _dot_tn(am, dya), _dot_tn(bm, dyb)
        doa_ref[...] = dam * _silu(za)
        dza_ref[...] = (dam * oa * _dsilu(za)).astype(BF16)
        dor_ref[...] = dbm * _silu(zr)
        dzr_ref[...] = (dbm * orw * _dsilu(zr)).astype(BF16)

        @pl.when(i == 0)
        def _():
            dwua_ref[...], dwur_ref[...], dwout_ref[...], dg2_ref[...], lacc[...] = dwua, dwur, dwout, dg2, lpart

        @pl.when(i != 0)
        def _():
            dwua_ref[...] += dwua
            dwur_ref[...] += dwur
            dwout_ref[...] += dwout
            dg2_ref[...] += dg2
            lacc[...] += lpart

        @pl.when(i == nt - 1)
        def _():
            loss_ref[...] = jnp.sum(lacc[...], axis=1, keepdims=True) * (0.5 / d)

    t512 = pl.BlockSpec((tm, WIDTH), lambda i: (i, 0))
    t1k = pl.BlockSpec((tm, d), lambda i: (i, 0))
    t2k = pl.BlockSpec((tm, 2 * d), lambda i: (i, 0))
    full = lambda r, c: pl.BlockSpec((r, c), lambda i: (0, 0))
    return pl.pallas_call(
        body, name="head_fwd_bwd", grid=(nt,),
        in_specs=[t512, t512, t512, t512, t2k, t1k, t1k, full(WIDTH, d), full(WIDTH, d), full(d, d), full(1, d)],
        out_specs=[t1k, t512, t512, t512, t512, t2k, full(WIDTH, d), full(WIDTH, d), full(d, d), full(1, d), full(1, 1)],
        out_shape=[SDS((n, d), F32), SDS((n, WIDTH), F32), SDS((n, WIDTH), F32), SDS((n, WIDTH), BF16), SDS((n, WIDTH), BF16),
                   SDS((n, 2 * d), BF16), SDS((WIDTH, d), F32), SDS((WIDTH, d), F32), SDS((d, d), F32), SDS((1, d), F32), SDS((1, 1), F32)],
        scratch_shapes=[pltpu.VMEM((1, d), F32)],
        compiler_params=_params(("arbitrary",)))(o_attn, o_rwkv, z_attn, z_rwkv, gm, x2, tgt, wua, wur, wout, g2)


def _prenorm_bwd(dh, x2, rs, g1, dxo):
    n, d = x2.shape
    tm = 1024

    def body(dh_ref, x_ref, rs_ref, g_ref, dxo_ref, gx_ref, dg_ref):
        x, r = x_ref[...], rs_ref[...]
        gd = dh_ref[...] * g_ref[...]
        gx_ref[...] = dxo_ref[...] + r * (gd - x * (r * r) * jnp.mean(gd * x, axis=-1, keepdims=True))
        dg = jnp.sum(dh_ref[...] * x * r, axis=0, keepdims=True)

        @pl.when(pl.program_id(0) == 0)
        def _():
            dg_ref[...] = dg

        @pl.when(pl.program_id(0) != 0)
        def _():
            dg_ref[...] += dg

    t = pl.BlockSpec((tm, d), lambda i: (i, 0))
    return pl.pallas_call(
        body, name="prenorm_bwd", grid=(n // tm,),
        in_specs=[t, t, pl.BlockSpec((tm, 1), lambda i: (i, 0)), pl.BlockSpec((1, d), lambda i: (0, 0)), t],
        out_specs=[t, pl.BlockSpec((1, d), lambda i: (0, 0))], out_shape=[SDS((n, d), F32), SDS((1, d), F32)],
        compiler_params=_params(("arbitrary",)))(dh, x2, rs, g1, dxo)


def _mesh_pos():
    x, y, c = lax.axis_index("x"), lax.axis_index("y"), lax.axis_index("c")
    return 4 * x + 2 * y + c


def _coords(idx):
    return (idx // 4, (idx // 2) % 2, idx % 2)


def _exchange(srcs, to_all, name):
    n = len(srcs)

    def body(*refs):
        src_refs, dst_refs = refs[:n], refs[n:2 * n]
        send_sems, recv_sems, local_sems = refs[2 * n:]
        me = _mesh_pos()

        def piece(i, j):
            return src_refs[i] if to_all[i] else src_refs[i].at[j]

        def remote(i, off, peer, block, slot):
            return pltpu.make_async_remote_copy(src_ref=piece(i, block), dst_ref=dst_refs[i].at[slot],
                                                send_sem=send_sems.at[i, off - 1], recv_sem=recv_sems.at[i, off - 1],
                                                device_id=_coords(peer), device_id_type=MESH)

        local = [pltpu.make_async_copy(piece(i, me), dst_refs[i].at[me], local_sems.at[i]) for i in range(n)]
        for cp in local:
            cp.start()
        sends = []
        for off in range(1, N_DEV):
            to = (me + off) % N_DEV
            for i in range(n):
                sends.append(remote(i, off, to, to, me))
                sends[-1].start()
        for off in range(1, N_DEV):
            frm = (me + N_DEV - off) % N_DEV
            for i in range(n):
                remote(i, off, frm, me, frm).wait_recv()
        for cp in sends:
            cp.wait_send()
        for cp in local:
            cp.wait()

    outs = pl.pallas_call(
        body, name=name, in_specs=[pl.BlockSpec(memory_space=pltpu.HBM)] * n, out_specs=[pl.BlockSpec(memory_space=pltpu.HBM)] * n,
        out_shape=[SDS((N_DEV,) + s.shape[-2:], s.dtype) for s in srcs],
        scratch_shapes=[pltpu.SemaphoreType.DMA((n, N_DEV - 1)), pltpu.SemaphoreType.DMA((n, N_DEV - 1)), pltpu.SemaphoreType.DMA((n,))],
        compiler_params=pltpu.CompilerParams())(*srcs)
    return outs


_HBM = pl.BlockSpec(memory_space=pltpu.HBM)
_SEM = pl.BlockSpec(memory_space=pltpu.SEMAPHORE)
_EFFECT = pltpu.SideEffectType.DATAFLOW_SIDE_EFFECTING


def _send_start(src):
    def body(src_ref, land_ref, send_sems, recv_sems, src_thru, land_thru, token):
        me = _mesh_pos()
        for off in range(1, N_DEV):
            to = (me + off) % N_DEV
            pltpu.make_async_remote_copy(src_ref=src_ref.at[to], dst_ref=land_ref.at[me], send_sem=send_sems.at[off - 1],
                                         recv_sem=recv_sems.at[off - 1], device_id=_coords(to), device_id_type=MESH).start()
        token[...] = jnp.zeros_like(token)

    hbm = pltpu.HBM(src.shape, src.dtype)
    return pl.pallas_call(
        body, name="grads_start",
        out_shape=(pltpu.SemaphoreType.DMA((N_DEV - 1,)), pltpu.SemaphoreType.DMA((N_DEV - 1,)), hbm, hbm, SDS((8, LANE), BF16)),
        in_specs=(_HBM, _HBM), out_specs=(_SEM, _SEM, _HBM, _HBM, pl.BlockSpec(memory_space=pltpu.VMEM)),
        input_output_aliases={0: 2, 1: 3}, compiler_params=pltpu.CompilerParams(has_side_effects=_EFFECT),
    )(pltpu.with_memory_space_constraint(src, pltpu.HBM), pltpu.with_memory_space_constraint(jnp.zeros(src.shape, src.dtype), pltpu.HBM))


def _send_wait(send_sems, recv_sems, src_thru, land_thru, after):
    def body(src_ref, land_ref, send_sems, recv_sems, after_ref, src_dead, got_ref):
        me = _mesh_pos()
        for off in range(1, N_DEV):
            to, frm = (me + off) % N_DEV, (me + N_DEV - off) % N_DEV
            pltpu.make_async_remote_copy(src_ref=src_ref.at[to], dst_ref=land_ref.at[me], send_sem=send_sems.at[off - 1],
                                         recv_sem=recv_sems.at[off - 1], device_id=_coords(to), device_id_type=MESH).wait_send()
            pltpu.make_async_remote_copy(src_ref=src_ref.at[me], dst_ref=land_ref.at[frm], send_sem=send_sems.at[off - 1],
                                         recv_sem=recv_sems.at[off - 1], device_id=_coords(frm), device_id_type=MESH).wait_recv()

    hbm = pltpu.HBM(src_thru.shape, src_thru.dtype)
    return pl.pallas_call(
        body, name="grads_wait", out_shape=(hbm, hbm), in_specs=(_HBM, _HBM, _SEM, _SEM, pl.BlockSpec(memory_space=pl.ANY)),
        out_specs=(_HBM, _HBM), input_output_aliases={0: 0, 1: 1}, compiler_params=pltpu.CompilerParams(has_side_effects=_EFFECT),
    )(src_thru, land_thru, send_sems, recv_sems, after)[1]


def _gather(srcs, name):
    n = len(srcs)

    def body(*refs):
        src_refs, dst_refs = refs[:n], refs[n:2 * n]
        send_sems, recv_sems, local_sems = refs[2 * n:]
        x, y, c = lax.axis_index("x"), lax.axis_index("y"), lax.axis_index("c")
        me, sibling = (x, y, c), (x, y, 1 - c)
        chips = [(1 - x, y), (x, 1 - y), (1 - x, 1 - y)]

        def slot(i, dev):
            return dst_refs[i].at[4 * dev[0] + 2 * dev[1] + dev[2]]

        def copy(i, k, block, to, own=False):
            return pltpu.make_async_remote_copy(src_ref=src_refs[i] if own else slot(i, block), dst_ref=slot(i, block),
                                                send_sem=send_sems.at[i, k], recv_sem=recv_sems.at[i, k],
                                                device_id=to, device_id_type=MESH)

        local = [pltpu.make_async_copy(src_refs[i], slot(i, me), local_sems.at[i]) for i in range(n)]
        for cp in local:
            cp.start()
        sends = []
        for i in range(n):
            sends.append(copy(i, 0, me, sibling, own=True))
            sends += [copy(i, 1 + j, me, (*chip, c), own=True) for j, chip in enumerate(chips)]
        for cp in sends:
            cp.start()
        for j, chip in enumerate(chips):
            for i in range(n):
                copy(i, 1 + j, (*chip, c), me).wait_recv()
                sends.append(copy(i, 4 + j, (*chip, c), sibling))
                sends[-1].start()
        for i in range(n):
            copy(i, 0, sibling, me).wait_recv()
            for j, chip in enumerate(chips):
                copy(i, 4 + j, (*chip, 1 - c), me).wait_recv()
        for cp in sends:
            cp.wait_send()
        for cp in local:
            cp.wait()

    return pl.pallas_call(
        body, name=name, in_specs=[pl.BlockSpec(memory_space=pltpu.HBM)] * n, out_specs=[pl.BlockSpec(memory_space=pltpu.HBM)] * n,
        out_shape=[SDS((N_DEV,) + s.shape, s.dtype) for s in srcs],
        scratch_shapes=[pltpu.SemaphoreType.DMA((n, N_DEV - 1)), pltpu.SemaphoreType.DMA((n, N_DEV - 1)), pltpu.SemaphoreType.DMA((n,))],
        compiler_params=pltpu.CompilerParams())(*srcs)


def _adamw(parts, w, m, v, tr, name, own=None):
    rows, cols = w.shape
    c1, c2 = 1.0 - ADAM_B1 ** ADAM_STEP, 1.0 - ADAM_B2 ** ADAM_STEP

    def body(p_ref, *refs):
        w_ref, m_ref, v_ref, g_ref, d_ref, nm_ref, nv_ref = refs[-7:]
        me = _mesh_pos()

        def part(j):
            return p_ref[j] if own is None else jnp.where(me == j, refs[0][...], p_ref[j])

        g = part(0).astype(F32)
        for j in range(1, N_DEV):
            g = g + part(j).astype(F32)
        nm = ADAM_B1 * m_ref[...] + (1.0 - ADAM_B1) * g
        nv = ADAM_B2 * v_ref[...] + (1.0 - ADAM_B2) * jnp.square(g)
        g_ref[...] = g
        nm_ref[...] = nm
        nv_ref[...] = nv
        d_ref[...] = -ADAM_LR * ((nm / c1) / (jnp.sqrt(nv / c2) + ADAM_EPS) + ADAM_WD * w_ref[...])

    t = pl.BlockSpec((tr, cols), lambda i: (i, 0))
    extra = [] if own is None else [own]
    return pl.pallas_call(
        body, name=name, grid=(rows // tr,), in_specs=[pl.BlockSpec((N_DEV, tr, cols), lambda i: (0, i, 0))] + [t] * (3 + len(extra)),
        out_specs=[t] * 4, out_shape=[SDS((rows, cols), F32)] * 4, compiler_params=_params(("parallel",)))(parts, *extra, w, m, v)


SHARDED = (("w_in", D_MODEL, IN_COLS // N_DEV, True, 128), ("w_up_attn", WIDTH, D_MODEL // N_DEV, True, WIDTH),
           ("w_up_rwkv", WIDTH, D_MODEL // N_DEV, True, WIDTH), ("w_out", D_MODEL // N_DEV, D_MODEL, False, D_MODEL // N_DEV),
           ("rwkv_w_up", LORA, WIDTH // N_DEV, True, LORA), ("rwkv_a_up", LORA, WIDTH // N_DEV, True, LORA))
LOSS_SLOT = sum(n for _, n in SMALL)


def _pack_small(small, extra=None):
    flat = [small[n].reshape(-1).astype(F32) for n, _ in SMALL]
    flat.append(jnp.zeros((1,), F32) if extra is None else extra.reshape(1))
    flat.append(jnp.zeros((SMALL_ROWS * LANE - LOSS_SLOT - 1,), F32))
    return jnp.concatenate(flat).reshape(SMALL_ROWS, LANE)


def _unpack_small(packed, shapes):
    flat = packed.reshape(-1)
    out, off = {}, 0
    for n, cnt in SMALL:
        out[n] = flat[off:off + cnt].reshape(shapes[n])
        off += cnt
    return out, flat[LOSS_SLOT]


def _whole(gathered, by_cols):
    if not by_cols:
        return gathered.reshape(-1, gathered.shape[-1])
    return gathered.transpose(1, 0, 2).reshape(gathered.shape[1], -1)


def _per_owner(full, by_cols):
    if not by_cols:
        return full.reshape(N_DEV, -1, full.shape[-1])
    return full.reshape(full.shape[0], N_DEV, -1).transpose(1, 0, 2)


def _local_step(x, loss_target, sm, wts):
    bsz, s, d = x.shape
    n = bsz * s
    x2, tgt = x.reshape(n, d), loss_target.reshape(n, d)
    bidx = jnp.asarray(_bucket_tables())
    w_in = wts["w_in"]
    segs = (("qkv", 0, QKV_COLS, 1536), ("za", OFF_ZA, WIDTH, 512), ("pr", OFF_PR, PR_COLS, PR_COLS), ("zr", OFF_ZR, WIDTH, 512),
            ("gm", OFF_GM, 2 * D_MODEL, 512))

    h, rs = _prenorm(x2, sm["pre_norm_gain"])
    proj = {nm: _mm(h, w_in[:, off:off + cnt], tn, "proj_" + nm) for nm, off, cnt, tn in segs}
    qkv3 = proj["qkv"].reshape(bsz, s, QKV_COLS)
    pr3 = proj["pr"].reshape(bsz, s, PR_COLS)

    o_attn, lse = _attn_fwd(qkv3, sm["rel_bias"], bidx)
    rk = sm["rwkv_r_k"].reshape(1, WIDTH)
    pre_args = (sm["rwkv_shift_mix"], sm["rwkv_w0"], wts["rwkv_w_up"], sm["rwkv_a0"], wts["rwkv_a_up"], sm["rwkv_k_k"], sm["rwkv_k_a"])
    scan_in = _rwkv_pre(pr3, *pre_args)
    o_rwkv, states, consts = _rwkv_scan(scan_in, rk, sm["rwkv_ln_w"], sm["rwkv_ln_b"])

    (dxo, do_attn, do_rwkv, dza, dzr, dgm, g_wua, g_wur, g_wout, g_post, loss) = _head(
        o_attn.reshape(n, WIDTH), o_rwkv.reshape(n, WIDTH), proj["za"], proj["zr"], proj["gm"], x2, tgt,
        wts["w_up_attn"], wts["w_up_rwkv"], wts["w_out"], sm["post_norm_gain"])

    dqkv, dbias = _attn_bwd(qkv3, o_attn, lse, do_attn.reshape(bsz, s, WIDTH), sm["rel_bias"], bidx)
    g_bias = _bias_grad(dbias, bidx)[:, :N_BUCKET].T

    scan_cots, (g_rk, g_lnw, g_lnb) = _rwkv_scan_bwd(scan_in, states, consts, do_rwkv.reshape(bsz, s, WIDTH), rk, sm["rwkv_ln_w"],
                                                     sm["rwkv_ln_b"])
    dprs, g_mix, g_w0, g_wup, g_a0, g_aup, g_kk, g_ka = _rwkv_pre_bwd(pr3, scan_cots, *pre_args)
    dpr = _shift_bwd(dprs, sm["rwkv_shift_mix"]).reshape(n, PR_COLS)

    dsegs = [(dqkv.reshape(9, n, WIDTH), 0, QKV_COLS, WIDTH), (dza, OFF_ZA, WIDTH, WIDTH), (dpr, OFF_PR, PR_COLS, PR_COLS),
             (dzr, OFF_ZR, WIDTH, WIDTH), (dgm, OFF_GM, 2 * D_MODEL, D_MODEL)]
    g_win = jnp.concatenate([_mm_tn(h, t, tn, "gw_in_%d" % j) for j, (t, _, _, tn) in enumerate(dsegs)], axis=1)
    blocks = _per_owner(g_win, True).astype(BF16)
    own = lax.dynamic_index_in_dim(blocks, 4 * lax.axis_index("x") + 2 * lax.axis_index("y") + lax.axis_index("c"), 0, keepdims=False)
    send_sems, recv_sems, blocks_thru, land_thru, token = _send_start(blocks)
    dh = None
    for j, (t, off, cnt, _) in enumerate(dsegs):
        dh = _mm_nt_acc(t, w_in[:, off:off + cnt] + token[0, 0], dh, "dh_%d" % j)
    grad_x, g_pre = _prenorm_bwd(dh, x2, rs, sm["pre_norm_gain"], dxo)
    landed = _send_wait(send_sems, recv_sems, blocks_thru, land_thru, g_pre)

    full = {"w_up_attn": g_wua, "w_up_rwkv": g_wur, "w_out": g_wout, "rwkv_w_up": g_wup, "rwkv_a_up": g_aup}
    small = {"pre_norm_gain": g_pre, "rel_bias": g_bias, "rwkv_shift_mix": g_mix, "rwkv_w0": g_w0, "rwkv_a0": g_a0, "rwkv_k_k": g_kk,
             "rwkv_k_a": g_ka, "rwkv_r_k": g_rk, "rwkv_ln_w": g_lnw, "rwkv_ln_b": g_lnb, "post_norm_gain": g_post}
    return loss[0, 0], grad_x.reshape(bsz, s, d), (landed, own), full, small


def kernel(x, pre_norm_gain, w_in, rel_bias, rwkv_shift_mix, rwkv_w0, rwkv_w_up, rwkv_a0, rwkv_a_up, rwkv_k_k, rwkv_k_a, rwkv_r_k, rwkv_ln_w, rwkv_ln_b, w_up_attn, w_up_rwkv, w_out, post_norm_gain, loss_target, m_pre_norm_gain, m_w_in, m_rel_bias, m_rwkv_shift_mix, m_rwkv_w0, m_rwkv_w_up, m_rwkv_a0, m_rwkv_a_up, m_rwkv_k_k, m_rwkv_k_a, m_rwkv_r_k, m_rwkv_ln_w, m_rwkv_ln_b, m_w_up_attn, m_w_up_rwkv, m_w_out, m_post_norm_gain, v_pre_norm_gain, v_w_in, v_rel_bias, v_rwkv_shift_mix, v_rwkv_w0, v_rwkv_w_up, v_rwkv_a0, v_rwkv_a_up, v_rwkv_k_k, v_rwkv_k_a, v_rwkv_r_k, v_rwkv_ln_w, v_rwkv_ln_b, v_w_up_attn, v_w_up_rwkv, v_w_out, v_post_norm_gain):
    names = [n for n, *_ in SHARDED] + [n for n, _ in SMALL]
    loc = dict(locals())
    w = {n: loc[n] for n in names}
    m = {n: loc["m_" + n] for n in names}
    v = {n: loc["v_" + n] for n in names}
    shapes = {n: w[n].shape for n in names}
    order = ["pre_norm_gain", "w_in", "rel_bias", "rwkv_shift_mix", "rwkv_w0", "rwkv_w_up", "rwkv_a0", "rwkv_a_up", "rwkv_k_k", "rwkv_k_a",
             "rwkv_r_k", "rwkv_ln_w", "rwkv_ln_b", "w_up_attn", "w_up_rwkv", "w_out", "post_norm_gain"]
    shard2d = lambda t, n, r, c: t[n].reshape(r, c)

    gathered = _gather([shard2d(w, n, r, c).astype(BF16) for n, r, c, _, _ in SHARDED], "gather_weights")
    wts = {n: _whole(g, by_cols) for (n, _, _, by_cols, _), g in zip(SHARDED, gathered)}

    loss, grad_x, (win_landed, win_own), full, small = _local_step(x, loss_target, w, wts)
    rest = SHARDED[1:]
    parts = _exchange([_per_owner(full[n], by_cols).astype(BF16) for n, _, _, by_cols, _ in rest] + [_pack_small(small, loss)],
                      [False] * len(rest) + [True], "exchange_grads")

    outs = [{}, {}, {}, {}]
    for (n, r, c, _, tr), p in zip(SHARDED, [win_landed] + list(parts)):
        res = _adamw(p, shard2d(w, n, r, c), shard2d(m, n, r, c), shard2d(v, n, r, c), tr, "adamw_" + n,
                     own=win_own if n == "w_in" else None)
        for o, t in zip(outs, res):
            o[n] = t.reshape(shapes[n])
    res = _adamw(parts[-1], _pack_small(w), _pack_small(m), _pack_small(v), SMALL_ROWS, "adamw_small")
    for o, t in zip(outs, res):
        o.update(_unpack_small(t, shapes)[0])
    loss = _unpack_small(res[0], shapes)[1]
    return (loss, grad_x, *[o[n] for o in outs for n in order])
```

```python
import functools
import math

import numpy as np
import jax
import jax.numpy as jnp
from jax import lax
from jax.experimental import pallas as pl
from jax.experimental.pallas import tpu as pltpu

F32, BF16 = jnp.float32, jnp.bfloat16
SDS = jax.ShapeDtypeStruct
HI = lax.Precision.HIGHEST
HI3 = lax.Precision.HIGH
MESH = pl.DeviceIdType.MESH

N_DEV = 8
D_MODEL = 1024
HEAD = 64
N_HEAD = 8
WIDTH = N_HEAD * HEAD
DILATIONS = (1, 4, 16)
QB = 128
N_BUCKET = 32
MAX_DIST = 2048
LORA = 64
QKV_COLS = 9 * WIDTH
PR_COLS = 3 * WIDTH + 2 * LORA
IN_COLS = QKV_COLS + WIDTH + PR_COLS + WIDTH + 2 * D_MODEL
OFF_ZA, OFF_PR, OFF_ZR, OFF_GM = QKV_COLS, QKV_COLS + WIDTH, QKV_COLS + WIDTH + PR_COLS, QKV_COLS + 2 * WIDTH + PR_COLS
RMS_EPS = 1e-6
GN_EPS = 64e-5
SCALE = 1.0 / math.sqrt(HEAD)
CHUNK = 64
CHUNK_GROUP = 8
BWD_GROUP = 16
EARLY = 8
NEG = -1e30
LANE = 128

ADAM_LR, ADAM_B1, ADAM_B2, ADAM_EPS, ADAM_WD, ADAM_STEP = 0.001, 0.9, 0.999, 1e-08, 0.01, 10

VMEM_LIMIT = 56 * 1024 * 1024

SMALL = (("pre_norm_gain", 1024), ("rel_bias", 768), ("rwkv_shift_mix", 1664), ("rwkv_w0", 512), ("rwkv_a0", 512),
         ("rwkv_k_k", 512), ("rwkv_k_a", 512), ("rwkv_r_k", 512), ("rwkv_ln_w", 512), ("rwkv_ln_b", 512),
         ("post_norm_gain", 1024))
SMALL_ROWS = 64


def _params(sem=None):
    return pltpu.CompilerParams(dimension_semantics=sem, vmem_limit_bytes=VMEM_LIMIT)


def _dot(a, b):
    return jnp.dot(a, b, preferred_element_type=F32)


def _dot_nt(a, b):
    return lax.dot_general(a, b, (((1,), (1,)), ((), ())), preferred_element_type=F32)


def _dot_tn(a, b):
    return lax.dot_general(a, b, (((0,), (0,)), ((), ())), preferred_element_type=F32)


@jax.custom_vjp
def _bdot(a, b):
    return _dot(a.astype(BF16), b.astype(BF16))


def _bdot_fwd(a, b):
    return _bdot(a, b), (a, b)


def _bdot_bwd(res, g):
    a, b = res
    gb = g.astype(BF16)
    return _dot_nt(gb, b.astype(BF16)), _dot_tn(a.astype(BF16), gb)


_bdot.defvjp(_bdot_fwd, _bdot_bwd)


def _silu(z):
    return z * jax.nn.sigmoid(z)


def _dsilu(z):
    s = jax.nn.sigmoid(z)
    return s * (1.0 + z * (1.0 - s))


def _softplus(x):
    return jnp.maximum(x, 0.0) + jnp.log(1.0 + jnp.exp(-jnp.abs(x)))


def _bucket_tables():
    qi = np.arange(QB)[:, None] + QB
    ki = np.arange(2 * QB)[None, :]
    rel = np.maximum(qi - ki, 0)
    out = []
    for d in DILATIONS:
        dist = rel * d
        max_exact = N_BUCKET // 2
        ratio = np.log(np.maximum(dist, 1).astype(np.float32) / max_exact) / np.float32(math.log(MAX_DIST / max_exact))
        large = max_exact + (ratio * (N_BUCKET - max_exact)).astype(np.int32)
        large = np.minimum(large, N_BUCKET - 1)
        out.append(np.where(dist < max_exact, dist, large).astype(np.int32))
    return np.stack(out)


def _prenorm(x2, g):
    n, d = x2.shape
    tm = 1024

    def body(x_ref, g_ref, h_ref, rs_ref):
        x = x_ref[...]
        rs = lax.rsqrt(jnp.mean(x * x, axis=-1, keepdims=True) + RMS_EPS)
        h_ref[...] = (x * rs * g_ref[...]).astype(BF16)
        rs_ref[...] = rs

    return pl.pallas_call(
        body, name="prenorm", grid=(n // tm,),
        in_specs=[pl.BlockSpec((tm, d), lambda i: (i, 0)), pl.BlockSpec((1, d), lambda i: (0, 0))],
        out_specs=[pl.BlockSpec((tm, d), lambda i: (i, 0)), pl.BlockSpec((tm, 1), lambda i: (i, 0))],
        out_shape=[SDS((n, d), BF16), SDS((n, 1), F32)], compiler_params=_params(("parallel",)))(x2, g)


def _mm(a, b, tn, name):
    m, k = a.shape
    n = b.shape[1]
    tm = 1024

    def body(a_ref, b_ref, o_ref):
        o_ref[...] = _dot(a_ref[...], b_ref[...])

    return pl.pallas_call(
        body, name=name, grid=(n // tn, m // tm),
        in_specs=[pl.BlockSpec((tm, k), lambda j, i: (i, 0)), pl.BlockSpec((k, tn), lambda j, i: (0, j))],
        out_specs=pl.BlockSpec((tm, tn), lambda j, i: (i, j)),
        out_shape=SDS((m, n), F32), compiler_params=_params(("parallel", "parallel")))(a, b)


def _mm_nt_acc(a, b, acc, name):
    split = a.ndim == 3
    m = a.shape[-2]
    k = b.shape[1]
    d = b.shape[0]
    tm = 1024
    per = 3 if split else 1
    seg = a.shape[2] if split else 0
    tk = per * seg if split else (k if k <= 2048 else 1536)
    have_acc = acc is not None

    def body(*refs):
        if have_acc:
            a_ref, b_ref, c_ref, o_ref = refs
        else:
            a_ref, b_ref, o_ref = refs
        if split:
            r = sum(_dot_nt(a_ref[j].astype(BF16), b_ref[:, seg * j:seg * (j + 1)]) for j in range(per))
        else:
            r = _dot_nt(a_ref[...].astype(BF16), b_ref[...])

        @pl.when(pl.program_id(1) == 0)
        def _():
            o_ref[...] = r + c_ref[...] if have_acc else r

        @pl.when(pl.program_id(1) != 0)
        def _():
            o_ref[...] += r

    a_spec = pl.BlockSpec((per, tm, seg), lambda i, j: (j, i, 0)) if split else pl.BlockSpec((tm, tk), lambda i, j: (i, j))
    in_specs = [a_spec, pl.BlockSpec((d, tk), lambda i, j: (0, j))]
    args = [a, b]
    if have_acc:
        in_specs.append(pl.BlockSpec((tm, d), lambda i, j: (i, 0)))
        args.append(acc)
    return pl.pallas_call(
        body, name=name, grid=(m // tm, k // tk), in_specs=in_specs, out_specs=pl.BlockSpec((tm, d), lambda i, j: (i, 0)),
        out_shape=SDS((m, d), F32), compiler_params=_params(("parallel", "arbitrary")))(*args)


def _mm_tn(a, b, tn, name):
    split = b.ndim == 3
    m, k1 = a.shape
    per = 3 if split else 1
    seg = b.shape[2] if split else tn
    tn = per * seg
    n2 = b.shape[0] * seg if split else b.shape[1]
    tm = 1024

    def body(a_ref, b_ref, o_ref):
        first = pl.program_id(1) == 0
        for j in range(per):
            r = _dot_tn(a_ref[...], (b_ref[j] if split else b_ref[...]).astype(BF16))
            cols = slice(seg * j, seg * (j + 1))

            @pl.when(first)
            def _(r=r, cols=cols):
                o_ref[:, cols] = r

            @pl.when(jnp.logical_not(first))
            def _(r=r, cols=cols):
                o_ref[:, cols] += r

    b_spec = pl.BlockSpec((per, tm, seg), lambda j, i: (j, i, 0)) if split else pl.BlockSpec((tm, tn), lambda j, i: (i, j))
    return pl.pallas_call(
        body, name=name, grid=(n2 // tn, m // tm),
        in_specs=[pl.BlockSpec((tm, k1), lambda j, i: (i, 0)), b_spec],
        out_specs=pl.BlockSpec((k1, tn), lambda j, i: (0, j)),
        out_shape=SDS((k1, n2), F32), compiler_params=_params(("parallel", "arbitrary")))(a, b)


def _ds(start, d):
    return pl.ds(start, QB) if d == 1 else pl.ds(start, QB, stride=d)


def _fill_bias(tab_ref, bidx_ref, bias_sc, hp):
    for g in range(3):
        bi = bidx_ref[g]
        for h in range(2):
            acc = jnp.zeros((QB, 2 * QB), F32)
            for j in range(N_BUCKET):
                acc = jnp.where(bi == j, tab_ref[j, g * N_HEAD + hp * 2 + h], acc)
            bias_sc[g * 2 + h] = acc


def _block_starts(it, d, nb):
    rho = it // nb
    n = it % nb
    st = rho + d * QB * n
    stp = rho + d * QB * jnp.maximum(n - 1, 0)
    if d == 1:
        st, stp = pl.multiple_of(QB * it, QB), pl.multiple_of(QB * jnp.maximum(it - 1, 0), QB)
    return st, stp, n > 0


ATTN_BLOCKS = 4


def _bdot3(a, b, dims):
    return lax.dot_general(a, b, (dims, ((0,), (0,))), preferred_element_type=F32)


def _attn_operands(q_ref, k_ref, v_ref, bias_sc, g, d, nb, it0):
    two = nb > 1
    nk = 2 * QB if two else QB
    ii = lax.broadcasted_iota(jnp.int32, (QB, nk), 0)
    cc = lax.broadcasted_iota(jnp.int32, (QB, nk), 1)
    qs, ks, vs, pens, starts = [], [], [], [], []
    for u in range(ATTN_BLOCKS):
        st, stp, hasprev = _block_starts(it0 + u, d, nb)
        qf = q_ref[0, _ds(st, d), :]
        if two:
            kf = jnp.concatenate([k_ref[0, _ds(stp, d), :], k_ref[0, _ds(st, d), :]], axis=0).astype(BF16)
            vf = jnp.concatenate([v_ref[0, _ds(stp, d), :], v_ref[0, _ds(st, d), :]], axis=0).astype(BF16)
            own = jnp.logical_and(cc >= QB, ii >= cc - QB)
            prev = jnp.logical_and(jnp.logical_and(cc < QB, cc >= ii), hasprev)
            pen = jnp.where(jnp.logical_or(own, prev), 0.0, NEG)
        else:
            kf, vf = k_ref[0, _ds(st, d), :].astype(BF16), v_ref[0, _ds(st, d), :].astype(BF16)
            pen = jnp.where(ii >= cc, 0.0, NEG)
        for h in range(2):
            qs.append(_one_head(qf, h).astype(BF16))
            ks.append(kf)
            vs.append(vf)
            pens.append(pen + (bias_sc[g * 2 + h] if two else bias_sc[g * 2 + h, :, QB:2 * QB]))
        starts.append((st, stp))
    return _stack(qs), _stack(ks), _stack(vs), _stack(pens), starts


def _one_head(x, h):
    lane = lax.broadcasted_iota(jnp.int32, x.shape, 1)
    return jnp.where(lane >= HEAD if h == 1 else lane < HEAD, x, 0.0)


def _pick_heads(x, u):
    lane = lax.broadcasted_iota(jnp.int32, x.shape[1:], 1)
    return jnp.where(lane < HEAD, x[2 * u], x[2 * u + 1])


def _add_heads(x, u):
    return x[2 * u] + x[2 * u + 1]


def _attn_fwd(qkv3, rel_bias, bidx):
    bsz, s, _ = qkv3.shape
    rt = 256

    def body(tab_ref, bidx_ref, *refs):
        q_refs, k_refs, v_refs = refs[0:3], refs[3:6], refs[6:9]
        o_ref, lse_ref = refs[9:11]
        bias_sc, num_sc, den_sc, m_sc = refs[11:]
        pl.when(pl.program_id(1) == 0)(lambda: _fill_bias(tab_ref, bidx_ref, bias_sc, pl.program_id(0)))
        for g, d in enumerate(DILATIONS):
            nb = s // (QB * d)

            def blk(it, c, g=g, d=d, nb=nb):
                q, k, v, bias, starts = _attn_operands(q_refs[g], k_refs[g], v_refs[g], bias_sc, g, d, nb, it * ATTN_BLOCKS)
                sc = _bdot3(q, k, ((2,), (2,))) * SCALE + bias
                m = jnp.max(sc, axis=-1, keepdims=True)
                p = jnp.exp(sc - m)
                den = jnp.sum(p, axis=-1, keepdims=True)
                num = _bdot3(p.astype(BF16), v, ((2,), (1,)))
                den, m = jnp.broadcast_to(den, num.shape), jnp.broadcast_to(m, num.shape)
                for u, (st, _) in enumerate(starts):
                    num_sc[g, _ds(st, d), :] = _pick_heads(num, u)
                    den_sc[g, _ds(st, d), :] = _pick_heads(den, u)
                    m_sc[g, _ds(st, d), :] = _pick_heads(m, u)
                return c

            lax.fori_loop(0, s // QB // ATTN_BLOCKS, blk, 0)

        def merge(i, c):
            rows = pl.ds(pl.multiple_of(i * rt, rt), rt)
            m0, m1, m2 = m_sc[0, rows, :], m_sc[1, rows, :], m_sc[2, rows, :]
            mall = jnp.maximum(jnp.maximum(m0, m1), m2)
            w0, w1, w2 = jnp.exp(m0 - mall), jnp.exp(m1 - mall), jnp.exp(m2 - mall)
            num = w0 * num_sc[0, rows, :] + w1 * num_sc[1, rows, :] + w2 * num_sc[2, rows, :]
            den = w0 * den_sc[0, rows, :] + w1 * den_sc[1, rows, :] + w2 * den_sc[2, rows, :]
            o_ref[0, rows, :] = num / den
            lse_ref[0, rows, :] = mall + jnp.log(den)
            return c

        lax.fori_loop(0, s // rt, merge, 0)

    col = lambda w, g: (lambda hp, b: (b, 0, (w * 3 + g) * 4 + hp))
    in_specs = [pl.BlockSpec(memory_space=pltpu.SMEM), pl.BlockSpec((3, QB, 2 * QB), lambda hp, b: (0, 0, 0))]
    in_specs += [pl.BlockSpec((1, s, LANE), col(w, g)) for w in range(3) for g in range(3)]
    out_spec = pl.BlockSpec((1, s, LANE), lambda hp, b: (b, 0, hp))
    return pl.pallas_call(
        body, name="attn_fwd", grid=(4, bsz), in_specs=in_specs, out_specs=[out_spec, out_spec],
        out_shape=[SDS((bsz, s, WIDTH), F32), SDS((bsz, s, WIDTH), F32)],
        scratch_shapes=[pltpu.VMEM((6, QB, 2 * QB), F32), pltpu.VMEM((3, s, LANE), F32), pltpu.VMEM((3, s, LANE), F32),
                        pltpu.VMEM((3, s, LANE), F32)],
        compiler_params=_params(("arbitrary", "arbitrary")))(rel_bias, bidx, *([qkv3] * 9))


def _attn_bwd(qkv3, o3, lse3, do3, rel_bias, bidx):
    bsz, s, _ = qkv3.shape
    rt = 256

    def body(tab_ref, bidx_ref, *refs):
        q_refs, k_refs, v_refs = refs[0:3], refs[3:6], refs[6:9]
        o_ref, lse_ref, do_ref, dqkv_ref, db_ref, bias_sc, delta_sc, acc_sc = refs[9:]
        dq_refs, dk_refs, dv_refs = ([acc_sc.at[w * 3 + g] for g in range(3)] for w in range(3))

        @pl.when(pl.program_id(1) == 0)
        def _():
            _fill_bias(tab_ref, bidx_ref, bias_sc, pl.program_id(0))
            db_ref[...] = jnp.zeros_like(db_ref)

        def prep(i, c):
            rows = pl.ds(pl.multiple_of(i * rt, rt), rt)
            prod = do_ref[0, rows, :] * o_ref[0, rows, :]
            d0 = jnp.sum(prod[:, :HEAD], axis=-1, keepdims=True)
            d1 = jnp.sum(prod[:, HEAD:], axis=-1, keepdims=True)
            delta_sc[rows, :] = jnp.concatenate([jnp.broadcast_to(d0, (rt, HEAD)), jnp.broadcast_to(d1, (rt, HEAD))], axis=1)
            z = jnp.zeros((rt, LANE), F32)
            for g in range(3):
                dk_refs[g][0, rows, :] = z
                dv_refs[g][0, rows, :] = z
            return c

        lax.fori_loop(0, s // rt, prep, 0)
        for g, d in enumerate(DILATIONS):
            nb = s // (QB * d)

            def blk(it, c, g=g, d=d, nb=nb):
                q, k, v, bias, starts = _attn_operands(q_refs[g], k_refs[g], v_refs[g], bias_sc, g, d, nb, it * ATTN_BLOCKS)
                dos, lses, deltas = [], [], []
                for st, _ in starts:
                    dof, lsef, delf = do_ref[0, _ds(st, d), :], lse_ref[0, _ds(st, d), :], delta_sc[_ds(st, d), :]
                    for h in range(2):
                        dos.append(_one_head(dof, h).astype(BF16))
                        lses.append(lsef[:, HEAD * h:HEAD * h + 1])
                        deltas.append(delf[:, HEAD * h:HEAD * h + 1])
                do, lse, delta = _stack(dos), _stack(lses), _stack(deltas)
                p = jnp.exp(_bdot3(q, k, ((2,), (2,))) * SCALE + bias - lse)
                dv = _bdot3(p.astype(BF16), do, ((1,), (1,)))
                ds = p * (_bdot3(do, v, ((2,), (2,))) - delta)
                dsb = ds.astype(BF16)
                dq = _bdot3(dsb, k, ((2,), (1,))) * SCALE
                dk = _bdot3(dsb, q, ((1,), (1,))) * SCALE
                two = nb > 1
                for h in range(2):
                    dsum = sum(ds[2 * u + h] for u in range(ATTN_BLOCKS))
                    if two:
                        db_ref[0, g * 2 + h] += dsum
                    else:
                        db_ref[0, g * 2 + h, :, QB:2 * QB] += dsum
                for u, (st, stp) in enumerate(starts):
                    dq_refs[g][0, _ds(st, d), :] = _pick_heads(dq, u)
                    if two:
                        dk_refs[g][0, _ds(stp, d), :] += _add_heads(dk[:, :QB], u)
                        dv_refs[g][0, _ds(stp, d), :] += _add_heads(dv[:, :QB], u)
                    dk_refs[g][0, _ds(st, d), :] += _add_heads(dk[:, QB:] if two else dk, u)
                    dv_refs[g][0, _ds(st, d), :] += _add_heads(dv[:, QB:] if two else dv, u)
                return c

            lax.fori_loop(0, s // QB // ATTN_BLOCKS, blk, 0)

        def flush(i, c):
            rows = pl.ds(pl.multiple_of(i * rt, rt), rt)
            for j in range(9):
                dqkv_ref[j, 0, rows, :] = acc_sc[j, 0, rows, :].astype(BF16)
            return c

        lax.fori_loop(0, s // rt, flush, 0)

    col = lambda w, g: (lambda hp, b: (b, 0, (w * 3 + g) * 4 + hp))
    blk_spec = pl.BlockSpec((1, s, LANE), lambda hp, b: (b, 0, hp))
    in_specs = [pl.BlockSpec(memory_space=pltpu.SMEM), pl.BlockSpec((3, QB, 2 * QB), lambda hp, b: (0, 0, 0))]
    in_specs += [pl.BlockSpec((1, s, LANE), col(w, g)) for w in range(3) for g in range(3)]
    in_specs += [blk_spec] * 3
    out_specs = [pl.BlockSpec((9, 1, s, LANE), lambda hp, b: (0, b, 0, hp)), pl.BlockSpec((1, 6, QB, 2 * QB), lambda hp, b: (hp, 0, 0, 0))]
    out_shape = [SDS((9, bsz, s, WIDTH), BF16), SDS((4, 6, QB, 2 * QB), F32)]
    return pl.pallas_call(
        body, name="attn_bwd", grid=(4, bsz), in_specs=in_specs, out_specs=out_specs, out_shape=out_shape,
        scratch_shapes=[pltpu.VMEM((6, QB, 2 * QB), F32), pltpu.VMEM((s, LANE), F32), pltpu.VMEM((9, 1, s, LANE), F32)],
        compiler_params=_params(("parallel", "arbitrary")))(rel_bias, bidx, *([qkv3] * 9), o3, lse3, do3)


def _bias_grad(dbias, bidx):
    def body(db_ref, bidx_ref, o_ref):
        lane = lax.broadcasted_iota(jnp.int32, (1, LANE), 1)
        for g in range(3):
            bi = bidx_ref[g]
            for hp in range(4):
                for h in range(2):
                    mat = db_ref[hp, g * 2 + h]
                    row = jnp.zeros((1, LANE), F32)
                    for j in range(N_BUCKET):
                        part = jnp.sum(jnp.where(bi == j, mat, 0.0), axis=0, keepdims=True)
                        row = jnp.where(lane == j, jnp.sum(part, axis=1, keepdims=True), row)
                    hd = g * N_HEAD + hp * 2 + h
                    o_ref[hd:hd + 1, :] = row

    return pl.pallas_call(body, name="bias_grad", out_shape=SDS((3 * N_HEAD, LANE), F32), compiler_params=_params())(dbias, bidx)


def _pre_fn(r, k0, v, wl, al, w0, wup, a0, aup, kk_, ka_):
    u = w0 + _bdot(jnp.tanh(wl), wup)
    lw = -jnp.exp(-_softplus(-u) - 0.5)
    a = jax.nn.sigmoid(a0 + _bdot(al, aup))
    kkraw = k0 * kk_
    k = k0 * (1.0 + (a - 1.0) * ka_)
    return r, lw, k, v, kkraw, a


PRE_SPLIT = (0, WIDTH, 2 * WIDTH, 3 * WIDTH, 3 * WIDTH + LORA, 3 * WIDTH + 2 * LORA)


def _pre_pieces(prs):
    return [prs[:, a:b] for a, b in zip(PRE_SPLIT[:-1], PRE_SPLIT[1:])]


PRE_TT = 512


def _shifted(pr_ref, edge_ref, first, back):
    pr = pr_ref[0]
    tt = pr.shape[0]
    row = lax.broadcasted_iota(jnp.int32, (tt, 1), 0)
    if back:
        edge = jnp.where(first, 0.0, edge_ref[0, 7:8, :])
        return jnp.where(row == 0, edge, pltpu.roll(pr, 1, axis=0))
    edge = jnp.where(first, 0.0, edge_ref[0, 0:1, :])
    return jnp.where(row == tt - 1, edge, pltpu.roll(pr, tt - 1, axis=0))


def _rwkv_pre(pr3, mix, w0, wup, a0, aup, kk_, ka_):
    bsz, s, _ = pr3.shape
    tt = PRE_TT

    def body(pr_ref, edge_ref, mix_ref, w0_ref, wup_ref, a0_ref, aup_ref, kk_ref, ka_ref, *outs):
        pr = pr_ref[0]
        prev = _shifted(pr_ref, edge_ref, pl.program_id(1) == 0, True)
        prs = pr + (prev - pr) * mix_ref[...]
        vals = _pre_fn(*_pre_pieces(prs), w0_ref[...], wup_ref[...].astype(F32), a0_ref[...], aup_ref[...].astype(F32), kk_ref[...],
                       ka_ref[...])
        for o, val in zip(outs, vals):
            o[0] = val

    vec = lambda n: pl.BlockSpec((1, n), lambda b, i: (0, 0))
    mat = pl.BlockSpec((LORA, WIDTH), lambda b, i: (0, 0))
    in_specs = [pl.BlockSpec((1, tt, PR_COLS), lambda b, i: (b, i, 0)),
                pl.BlockSpec((1, 8, PR_COLS), lambda b, i: (b, jnp.maximum(i * (tt // 8) - 1, 0), 0)),
                vec(PR_COLS), vec(WIDTH), mat, vec(WIDTH), mat, vec(WIDTH), vec(WIDTH)]
    out_spec = pl.BlockSpec((1, tt, WIDTH), lambda b, i: (b, i, 0))
    return pl.pallas_call(
        body, name="rwkv_pre", grid=(bsz, s // tt), in_specs=in_specs, out_specs=[out_spec] * 6,
        out_shape=[SDS((bsz, s, WIDTH), F32)] * 6, compiler_params=_params(("parallel", "parallel")))(
            pr3, pr3, mix, w0, wup, a0, aup, kk_, ka_)


def _rwkv_pre_bwd(pr3, cots, mix, w0, wup, a0, aup, kk_, ka_):
    bsz, s, _ = pr3.shape
    tt = PRE_TT

    def body(pr_ref, edge_ref, c0, c1, c2, c3, c4, c5, mix_ref, w0_ref, wup_ref, a0_ref, aup_ref, kk_ref, ka_ref,
             dprs_ref, dmix_ref, dw0_ref, dwup_ref, da0_ref, daup_ref, dkk_ref, dka_ref):
        pr = pr_ref[0]
        prev = _shifted(pr_ref, edge_ref, pl.program_id(1) == 0, True)
        prs = pr + (prev - pr) * mix_ref[...]
        _, vjp = jax.vjp(_pre_fn, *_pre_pieces(prs), w0_ref[...], wup_ref[...].astype(F32), a0_ref[...], aup_ref[...].astype(F32),
                         kk_ref[...], ka_ref[...])
        grads = vjp(tuple(c[0] for c in (c0, c1, c2, c3, c4, c5)))
        for piece, a, b in zip(grads[:5], PRE_SPLIT[:-1], PRE_SPLIT[1:]):
            dprs_ref[0, :, a:b] = piece
        dw0, dwup, da0, daup, dkk, dka = grads[5:]
        dprs = dprs_ref[0]
        grads = (jnp.sum(dprs * (prev - pr), axis=0, keepdims=True), dw0, dwup, da0, daup, dkk, dka)
        refs = (dmix_ref, dw0_ref, dwup_ref, da0_ref, daup_ref, dkk_ref, dka_ref)
        first = jnp.logical_and(pl.program_id(0) == 0, pl.program_id(1) == 0)

        @pl.when(first)
        def _():
            for r_, g_ in zip(refs, grads):
                r_[...] = g_

        @pl.when(jnp.logical_not(first))
        def _():
            for r_, g_ in zip(refs, grads):
                r_[...] += g_

    vec = lambda n: pl.BlockSpec((1, n), lambda b, i: (0, 0))
    mat = pl.BlockSpec((LORA, WIDTH), lambda b, i: (0, 0))
    tile = pl.BlockSpec((1, tt, WIDTH), lambda b, i: (b, i, 0))
    in_specs = [pl.BlockSpec((1, tt, PR_COLS), lambda b, i: (b, i, 0)),
                pl.BlockSpec((1, 8, PR_COLS), lambda b, i: (b, jnp.maximum(i * (tt // 8) - 1, 0), 0))]
    in_specs += [tile] * 6 + [vec(PR_COLS), vec(WIDTH), mat, vec(WIDTH), mat, vec(WIDTH), vec(WIDTH)]
    out_specs = [pl.BlockSpec((1, tt, PR_COLS), lambda b, i: (b, i, 0)), vec(PR_COLS), vec(WIDTH), mat, vec(WIDTH), mat,
                 vec(WIDTH), vec(WIDTH)]
    out_shape = [SDS((bsz, s, PR_COLS), F32), SDS((1, PR_COLS), F32), SDS((1, WIDTH), F32), SDS((LORA, WIDTH), F32),
                 SDS((1, WIDTH), F32), SDS((LORA, WIDTH), F32), SDS((1, WIDTH), F32), SDS((1, WIDTH), F32)]
    return pl.pallas_call(
        body, name="rwkv_pre_bwd", grid=(bsz, s // tt), in_specs=in_specs, out_specs=out_specs, out_shape=out_shape,
        compiler_params=_params(("arbitrary", "arbitrary")))(pr3, pr3, *cots, mix, w0, wup, a0, aup, kk_, ka_)


def _shift_bwd(dprs3, mix):
    bsz, s, _ = dprs3.shape
    tt = PRE_TT
    nt = s // tt

    def body(d_ref, edge_ref, mix_ref, o_ref):
        nxt = _shifted(d_ref, edge_ref, pl.program_id(1) == nt - 1, False)
        m = mix_ref[...]
        o_ref[0] = (d_ref[0] * (1.0 - m) + nxt * m).astype(BF16)

    in_specs = [pl.BlockSpec((1, tt, PR_COLS), lambda b, i: (b, i, 0)),
                pl.BlockSpec((1, 8, PR_COLS), lambda b, i: (b, jnp.minimum((i + 1) * (tt // 8), s // 8 - 1), 0)),
                pl.BlockSpec((1, PR_COLS), lambda b, i: (0, 0))]
    return pl.pallas_call(
        body, name="shift_bwd", grid=(bsz, nt), in_specs=in_specs, out_specs=pl.BlockSpec((1, tt, PR_COLS), lambda b, i: (b, i, 0)),
        out_shape=SDS((bsz, s, PR_COLS), BF16), compiler_params=_params(("parallel", "parallel")))(dprs3, dprs3, mix)


_NN, _NT, _TN = ((2,), (1,)), ((2,), (2,)), ((1,), (1,))


def _dot3(a, b, dims, precision=HI3):
    return lax.dot_general(a, b, (dims, ((0,), (0,))), precision=precision, preferred_element_type=F32)


def _dot3_bf16(a, b, dims):
    return lax.dot_general(a.astype(BF16), b.astype(BF16), (dims, ((0,), (0,))), preferred_element_type=F32)


class _Dots:
    def __init__(self, fwd):
        def make(dims, da_rule, db_rule):
            @jax.custom_vjp
            def f(a, b):
                return fwd(a, b, dims)

            f.defvjp(lambda a, b: (f(a, b), (a, b)), lambda res, g: (da_rule(*res, g), db_rule(*res, g)))
            return f

        one = _dot3_bf16
        self.mm = make(_NN, lambda a, b, g: one(g, b, _NT), lambda a, b, g: one(a, g, _TN))
        self.mm_nt = make(_NT, lambda a, b, g: one(g, b, _NN), lambda a, b, g: one(g, a, _TN))
        self.mm_tn = make(_TN, lambda a, b, g: one(b, g, _NT), lambda a, b, g: one(a, g, _NN))

        def powers(aab):
            ps = [aab]
            while 2 ** len(ps) < aab.shape[1]:
                ps.append(fwd(ps[-1], ps[-1], _NN))
            return ps

        def apply(ps, z, dims):
            for p in ps:
                z = z + fwd(p, z, dims)
            return z

        @jax.custom_vjp
        def solve(aab, z):
            return apply(powers(aab), z, _NN)

        def solve_fwd(aab, z):
            ps = powers(aab)
            x = apply(ps, z, _NN)
            return x, (ps, x)

        def solve_bwd(res, g):
            ps, x = res
            dz = apply(ps, g, _TN)
            return fwd(dz, x, _NT), dz

        solve.defvjp(solve_fwd, solve_bwd)
        self.solve = solve


_ACCURATE = _Dots(_dot3)
_ONE_PASS = _Dots(_dot3_bf16)
_bmm, _bmm_tn = _ACCURATE.mm, _ACCURATE.mm_tn


def _chunk_fn(s0t, r, lw, k, v, kkraw, a, rk, lnw, lnb, first=False, d=_ACCURATE):
    c = r.shape[1]
    at, rt, btc, ktc, gc, aab, arb, xv, arkv, ain, bin_ = _chunk_core(r, lw, k, v, kkraw, a, d)
    rs = d.mm(jnp.concatenate([at, rt], axis=1), s0t)
    u = d.solve(aab, rs[:, :c] + xv)
    y = rs[:, c:] + d.mm(arb, u) + arkv
    if first:
        y = _with_early_rows(y, r, lw, k, v, ain, bin_)
    gcol = jnp.sum(_diag(gc), axis=2, keepdims=True)
    sct = gcol * s0t + d.mm_tn(jnp.concatenate([btc, ktc], axis=1), jnp.concatenate([u, v], axis=1))
    return _post(y, r, k, v, rk, lnw, lnb), sct


def _diag(gc):
    return jnp.where(_masks(HEAD)[2], gc, 0.0)


def _with_early_rows(y, r, lw, k, v, ain, bin_):
    early = _early_rows(r[:2], lw[:2], k[:2], v[:2], ain[:2], bin_[:2])
    return jnp.concatenate([jnp.concatenate([early, y[:2, EARLY:]], axis=1), y[2:]], axis=0)


def _early_rows(r, lw, k, v, ain, bin_):
    cols = lambda x: _stack([jnp.transpose(x[h]) for h in range(2)])
    wc, bc, kc = cols(jnp.exp(lw)), cols(bin_), cols(k)
    st = jnp.zeros((2, HEAD, HEAD), F32)
    rows = []
    for t in range(EARLY):
        sa = _ONE_PASS.mm(ain[:, t:t + 1], st)
        st = st * wc[:, :, t:t + 1] + bc[:, :, t:t + 1] * sa + kc[:, :, t:t + 1] * v[:, t:t + 1]
        rows.append(_ONE_PASS.mm(r[:, t:t + 1], st))
    return jnp.concatenate(rows, axis=1)


def _chunk_rows(c):
    return pl.ds(c * CHUNK, CHUNK) if isinstance(c, int) else pl.ds(pl.multiple_of(c * CHUNK, CHUNK), CHUNK)


def _stack(xs):
    return jnp.concatenate([x[None] for x in xs], axis=0)


def _pairs(ref, chunks):
    tiles = [ref[0, _chunk_rows(c), :] for c in chunks]
    return _stack([t[:, HEAD * h:HEAD * h + HEAD] for t in tiles for h in range(2)])


def _unpair(vals, j):
    return jnp.concatenate([vals[2 * j], vals[2 * j + 1]], axis=1)


def _masks(c):
    ii = lax.broadcasted_iota(jnp.int32, (c, c), 0)
    jj = lax.broadcasted_iota(jnp.int32, (c, c), 1)
    return ii > jj, ii >= jj, ii == jj


def _chunk_core(r, lw, k, v, kkraw, a, d=_ACCURATE):
    g_, c = r.shape[0], r.shape[1]
    nrm = jnp.sqrt(jnp.sum(kkraw * kkraw, axis=-1, keepdims=True))
    kkn = kkraw / jnp.maximum(nrm, 1e-12)
    ain, bin_ = -kkn, kkn * a
    strict, incl, _ = _masks(c)
    lg = lax.dot_general(jnp.broadcast_to(incl.astype(F32), (g_, c, c)), lw, (((2,), (1,)), ((0,), (0,))), precision=HI,
                         preferred_element_type=F32)
    g, gp, gi = jnp.exp(lg), jnp.exp(lg - lw), jnp.exp(-lg)
    at, rt, bt, kt = ain * gp, r * g, bin_ * gi, k * gi
    aa = d.mm_nt(jnp.concatenate([at, rt], axis=1), jnp.concatenate([bt, kt], axis=1))
    aab = jnp.where(strict, aa[:, :c, :c], 0.0)
    aak = jnp.where(strict, aa[:, :c, c:], 0.0)
    arb = jnp.where(incl, aa[:, c:, :c], 0.0)
    ark = jnp.where(incl, aa[:, c:, c:], 0.0)
    akv = d.mm(jnp.concatenate([aak, ark], axis=1), v)
    gc = g[:, c - 1:c, :]
    return at, rt, bt * gc, kt * gc, gc, aab, arb, akv[:, :c], akv[:, c:], ain, bin_


def _post(y, r, k, v, rk, lnw, lnb):
    mu = jnp.mean(y, axis=-1, keepdims=True)
    var = jnp.mean(jnp.square(y - mu), axis=-1, keepdims=True)
    yn = (y - mu) * lax.rsqrt(var + GN_EPS) * lnw + lnb
    return yn + jnp.sum(r * k * rk, axis=-1, keepdims=True) * v


def _chunk_consts(r, lw, k, v, kkraw, a, first=False):
    d = _ONE_PASS
    at, rt, btc, ktc, gc, aab, arb, xv, arkv, ain, bin_ = _chunk_core(r, lw, k, v, kkraw, a, d)
    z = d.solve(aab, jnp.concatenate([at, xv], axis=2))
    ryv = jnp.concatenate([rt, arkv], axis=2) + d.mm(arb, z)
    if first:
        ryv = jnp.concatenate([ryv[:, :, :HEAD], _with_early_rows(ryv[:, :, HEAD:], r, lw, k, v, ain, bin_)], axis=2)
    mkv = d.mm_tn(btc, z) + jnp.concatenate([_diag(gc), d.mm_tn(ktc, v)], axis=2)
    return mkv, ryv


def _rwkv_scan(ins, rk, lnw, lnb):
    bsz, s, _ = ins[0].shape
    nch = s // CHUNK

    def consts_body(r_ref, lw_ref, k_ref, v_ref, kk_ref, a_ref, mkv_ref, ry_ref, yv_ref):
        def group(i, carry):
            chunks = [i * CHUNK_GROUP + j for j in range(CHUNK_GROUP)]
            mkv, ryv = _chunk_consts(*[_pairs(ref, chunks) for ref in (r_ref, lw_ref, k_ref, v_ref, kk_ref, a_ref)],
                                     first=isinstance(i, int) and i == 0)
            for j, c in enumerate(chunks):
                for h in range(2):
                    mkv_ref[0, 0, c, h] = mkv[2 * j + h]
                ry_ref[0, _chunk_rows(c), :] = jnp.concatenate([ryv[2 * j][:, :HEAD], ryv[2 * j + 1][:, :HEAD]], axis=1)
                yv_ref[0, _chunk_rows(c), :] = jnp.concatenate([ryv[2 * j][:, HEAD:], ryv[2 * j + 1][:, HEAD:]], axis=1)
            return carry

        group(0, 0)
        lax.fori_loop(1, nch // CHUNK_GROUP, group, 0)

    tile = pl.BlockSpec((1, s, LANE), lambda b, hp: (b, 0, hp))
    vec = pl.BlockSpec((1, LANE), lambda b, hp: (0, hp))
    mkv_spec = pl.BlockSpec((1, 1, nch, 2, HEAD, LANE), lambda b, hp: (b, hp, 0, 0, 0, 0))
    st_spec = pl.BlockSpec((1, 1, nch, 2, HEAD, HEAD), lambda b, hp: (b, hp, 0, 0, 0, 0))
    mkv, ry, yv = pl.pallas_call(
        consts_body, name="rwkv_consts", grid=(bsz, 4), in_specs=[tile] * 6, out_specs=[mkv_spec, tile, tile],
        out_shape=[SDS((bsz, 4, nch, 2, HEAD, LANE), F32), SDS((bsz, s, WIDTH), F32), SDS((bsz, s, WIDTH), F32)],
        compiler_params=_params(("parallel", "parallel")))(*ins)

    states = _chunk_recurrence(mkv, None, "rwkv_states")

    def out_body(ry_ref, yv_ref, r_ref, k_ref, v_ref, st_ref, rk_ref, lnw_ref, lnb_ref, o_ref):
        y, r, k, v, rk_, lnw_, lnb_ = _scan_rows(ry_ref, yv_ref, r_ref, k_ref, v_ref, st_ref, rk_ref, lnw_ref, lnb_ref)
        o = _post(y, r, k, v, rk_, lnw_, lnb_)
        for j in range(CHUNK_GROUP):
            o_ref[0, _chunk_rows(j), :] = _unpair(o, j)

    o = pl.pallas_call(
        out_body, name="rwkv_out", grid=(bsz, 4, nch // CHUNK_GROUP), in_specs=_group_specs(5), out_specs=_group_specs(1)[0],
        out_shape=SDS((bsz, s, WIDTH), F32),
        compiler_params=_params(("parallel", "parallel", "parallel")))(ry, yv, ins[0], ins[2], ins[3], states, rk, lnw, lnb)
    return o, states, (mkv, ry, yv)


def _group_specs(n_tiles):
    tile = pl.BlockSpec((1, CHUNK_GROUP * CHUNK, LANE), lambda b, hp, t: (b, t, hp))
    if n_tiles == 1:
        return [tile]
    st = pl.BlockSpec((1, 1, CHUNK_GROUP, 2, HEAD, HEAD), lambda b, hp, t: (b, hp, t, 0, 0, 0))
    vec = pl.BlockSpec((1, LANE), lambda b, hp, t: (0, hp))
    return [tile] * n_tiles + [st] + [vec] * 3


def _scan_rows(ry_ref, yv_ref, r_ref, k_ref, v_ref, st_ref, rk_ref, lnw_ref, lnb_ref):
    chunks = list(range(CHUNK_GROUP))
    ry, yv, r, k, v = (_pairs(ref, chunks) for ref in (ry_ref, yv_ref, r_ref, k_ref, v_ref))
    st = _stack([st_ref[0, 0, c, h] for c in chunks for h in range(2)])
    vecs = [_stack([ref[:, HEAD * h:HEAD * h + HEAD] for _ in chunks for h in range(2)]) for ref in (rk_ref, lnw_ref, lnb_ref)]
    return (_bmm(ry, st) + yv, r, k, v, *vecs)


def _chunk_recurrence(mkv, q, name):
    bsz, _, nch = mkv.shape[:3]
    pairs = [(hp, h) for hp in range(4) for h in range(2)]

    def body(*refs):
        mkv_ref, out_ref, acc = refs[0], refs[-2], refs[-1]
        acc[...] = jnp.zeros_like(acc)

        def step(i, carry):
            c = i if q is None else nch - 1 - i
            cur = acc[...]
            for j, (hp, h) in enumerate(pairs):
                out_ref[0, hp, c, h] = cur[j]
            m = _stack([mkv_ref[0, hp, c, h] for hp, h in pairs])
            if q is None:
                acc[...] = _bmm(m[:, :, :HEAD], cur) + m[:, :, HEAD:]
            else:
                acc[...] = _bmm_tn(m[:, :, :HEAD], cur) + _stack([refs[1][0, hp, c, h] for hp, h in pairs])
            return carry

        lax.fori_loop(0, nch, step, 0)

    spec = lambda w: pl.BlockSpec((1, 4, nch, 2, HEAD, w), lambda b: (b, 0, 0, 0, 0, 0))
    return pl.pallas_call(
        body, name=name, grid=(bsz,), in_specs=[spec(LANE)] + ([] if q is None else [spec(HEAD)]), out_specs=spec(HEAD),
        out_shape=SDS((bsz, 4, nch, 2, HEAD, HEAD), F32), scratch_shapes=[pltpu.VMEM((8, HEAD, HEAD), F32)],
        compiler_params=_params(("parallel",)))(*([mkv] if q is None else [mkv, q]))


def _rwkv_scan_bwd(ins, states, consts, do3, rk, lnw, lnb):
    bsz, s, _ = ins[0].shape
    nch = s // CHUNK

    mkv, ry, yv = consts

    def q_body(do_ref, ry_ref, yv_ref, r_ref, k_ref, v_ref, st_ref, rk_ref, lnw_ref, lnb_ref, q_ref):
        y, r, k, v, rk_, lnw_, lnb_ = _scan_rows(ry_ref, yv_ref, r_ref, k_ref, v_ref, st_ref, rk_ref, lnw_ref, lnb_ref)
        _, vjp = jax.vjp(lambda y_: _post(y_, r, k, v, rk_, lnw_, lnb_), y)
        (dy,) = vjp(_pairs(do_ref, list(range(CHUNK_GROUP))))
        q = _bmm_tn(_pairs(ry_ref, list(range(CHUNK_GROUP))), dy)
        for j in range(CHUNK_GROUP):
            for h in range(2):
                q_ref[0, 0, j, h] = q[2 * j + h]

    specs = _group_specs(6)
    q = pl.pallas_call(
        q_body, name="rwkv_q", grid=(bsz, 4, nch // CHUNK_GROUP), in_specs=specs, out_specs=specs[6],
        out_shape=SDS((bsz, 4, nch, 2, HEAD, HEAD), F32),
        compiler_params=_params(("parallel", "parallel", "parallel")))(do3, ry, yv, ins[0], ins[2], ins[3], states, rk, lnw, lnb)

    dstates = _chunk_recurrence(mkv, q, "rwkv_dstates")

    def body(r_ref, lw_ref, k_ref, v_ref, kk_ref, a_ref, st_ref, dst_ref, do_ref, rk_ref, lnw_ref, lnb_ref,
             dr_ref, dlw_ref, dk_ref, dv_ref, dkk_ref, da_ref, drk_ref, dlnw_ref, dlnb_ref):
        chunks = list(range(BWD_GROUP))
        par_refs = (drk_ref, dlnw_ref, dlnb_ref)

        @pl.when(jnp.logical_and(pl.program_id(1) == 0, pl.program_id(2) == 0))
        def _():
            for ref in par_refs:
                ref[...] = jnp.zeros_like(ref)

        def group(first):
            per_pair = lambda ref: _stack([ref[0, 0, c, h] for c in chunks for h in range(2)])
            vecs = [_stack([ref[:, HEAD * h:HEAD * h + HEAD] for _ in chunks for h in range(2)]) for ref in (rk_ref, lnw_ref, lnb_ref)]
            _, vjp = jax.vjp(functools.partial(_chunk_fn, first=first, d=_ONE_PASS), per_pair(st_ref),
                             *[_pairs(ref, chunks) for ref in (r_ref, lw_ref, k_ref, v_ref, kk_ref, a_ref)], *vecs)
            grads = vjp((_pairs(do_ref, chunks), per_pair(dst_ref)))
            for ref, cot in zip((dr_ref, dlw_ref, dk_ref, dv_ref, dkk_ref, da_ref), grads[1:7]):
                for j, c in enumerate(chunks):
                    ref[0, _chunk_rows(c), :] = _unpair(cot, j)
            for ref, g_ in zip(par_refs, grads[7:10]):
                ref[...] += jnp.concatenate([sum(g_[2 * j + h] for j in range(BWD_GROUP)) for h in range(2)], axis=1)

        pl.when(pl.program_id(2) == 0)(functools.partial(group, True))
        pl.when(pl.program_id(2) != 0)(functools.partial(group, False))

    tt = BWD_GROUP * CHUNK
    tile = pl.BlockSpec((1, tt, LANE), lambda hp, b, t: (b, t, hp))
    vec = pl.BlockSpec((1, LANE), lambda hp, b, t: (0, hp))
    st_spec = pl.BlockSpec((1, 1, BWD_GROUP, 2, HEAD, HEAD), lambda hp, b, t: (b, hp, t, 0, 0, 0))
    outs = pl.pallas_call(
        body, name="rwkv_scan_bwd", grid=(4, bsz, s // tt), in_specs=[tile] * 6 + [st_spec, st_spec, tile] + [vec] * 3,
        out_specs=[tile] * 6 + [vec] * 3,
        out_shape=[SDS((bsz, s, WIDTH), F32)] * 6 + [SDS((1, WIDTH), F32)] * 3,
        compiler_params=_params(("parallel", "arbitrary", "arbitrary")))(*ins, states, dstates, do3, rk, lnw, lnb)
    return outs[:6], outs[6:]


def _head(o_attn, o_rwkv, z_attn, z_rwkv, gm, x2, tgt, wua, wur, wout, g2):
    n = x2.shape[0]
    tm = 256
    nt = n // tm
    d = D_MODEL

    def body(oa_ref, or_ref, za_ref, zr_ref, gm_ref, x_ref, t_ref, wua_ref, wur_ref, wout_ref, g2_ref,
             dxo_ref, doa_ref, dor_ref, dza_ref, dzr_ref, dgm_ref, dwua_ref, dwur_ref, dwout_ref, dg2_ref, loss_ref, lacc):
        i = pl.program_id(0)
        oa, orw, za, zr = oa_ref[...], or_ref[...], za_ref[...], zr_ref[...]
        ga, gb = gm_ref[:, 0:d], gm_ref[:, d:2 * d]
        am = (oa * _silu(za)).astype(BF16)
        bm = (orw * _silu(zr)).astype(BF16)
        ya, yb = _dot(am, wua_ref[...]), _dot(bm, wur_ref[...])
        sa, sb = jax.nn.sigmoid(ga), jax.nn.sigmoid(gb)
        merged = (sa * ya + sb * yb).astype(BF16)
        out = _dot(merged, wout_ref[...])
        rs = lax.rsqrt(jnp.mean(out * out, axis=-1, keepdims=True) + RMS_EPS)
        g2 = g2_ref[...]
        err = x_ref[...] + out * rs * g2 - t_ref[...]
        lpart = jnp.sum(err * err, axis=0, keepdims=True)
        dxo = err * (1.0 / d)
        dxo_ref[...] = dxo
        dg2 = jnp.sum(dxo * out * rs, axis=0, keepdims=True)
        gd = dxo * g2
        dout = (rs * (gd - out * (rs * rs) * jnp.mean(gd * out, axis=-1, keepdims=True))).astype(BF16)
        dmerged = _dot_nt(dout, wout_ref[...])
        dwout = _dot_tn(merged, dout)
        dya, dyb = (dmerged * sa).astype(BF16), (dmerged * sb).astype(BF16)
        dgm_ref[:, 0:d] = (dmerged * ya * sa * (1.0 - sa)).astype(BF16)
        dgm_ref[:, d:2 * d] = (dmerged * yb * sb * (1.0 - sb)).astype(BF16)
        dam, dbm = _dot_nt(dya, wua_ref[...]), _dot_nt(dyb, wur_ref[...])
        dwua, dwur = _dot_tn(am, dya), _dot_tn(bm, dyb)
        doa_ref[...] = dam * _silu(za)
        dza_ref[...] = (dam * oa * _dsilu(za)).astype(BF16)
        dor_ref[...] = dbm * _silu(zr)
        dzr_ref[...] = (dbm * orw * _dsilu(zr)).astype(BF16)

        @pl.when(i == 0)
        def _():
            dwua_ref[...], dwur_ref[...], dwout_ref[...], dg2_ref[...], lacc[...] = dwua, dwur, dwout, dg2, lpart

        @pl.when(i != 0)
        def _():
            dwua_ref[...] += dwua
            dwur_ref[...] += dwur
            dwout_ref[...] += dwout
            dg2_ref[...] += dg2
            lacc[...] += lpart

        @pl.when(i == nt - 1)
        def _():
            loss_ref[...] = jnp.sum(lacc[...], axis=1, keepdims=True) * (0.5 / d)

    t512 = pl.BlockSpec((tm, WIDTH), lambda i: (i, 0))
    t1k = pl.BlockSpec((tm, d), lambda i: (i, 0))
    t2k = pl.BlockSpec((tm, 2 * d), lambda i: (i, 0))
    full = lambda r, c: pl.BlockSpec((r, c), lambda i: (0, 0))
    return pl.pallas_call(
        body, name="head_fwd_bwd", grid=(nt,),
        in_specs=[t512, t512, t512, t512, t2k, t1k, t1k, full(WIDTH, d), full(WIDTH, d), full(d, d), full(1, d)],
        out_specs=[t1k, t512, t512, t512, t512, t2k, full(WIDTH, d), full(WIDTH, d), full(d, d), full(1, d), full(1, 1)],
        out_shape=[SDS((n, d), F32), SDS((n, WIDTH), F32), SDS((n, WIDTH), F32), SDS((n, WIDTH), BF16), SDS((n, WIDTH), BF16),
                   SDS((n, 2 * d), BF16), SDS((WIDTH, d), F32), SDS((WIDTH, d), F32), SDS((d, d), F32), SDS((1, d), F32), SDS((1, 1), F32)],
        scratch_shapes=[pltpu.VMEM((1, d), F32)],
        compiler_params=_params(("arbitrary",)))(o_attn, o_rwkv, z_attn, z_rwkv, gm, x2, tgt, wua, wur, wout, g2)


def _prenorm_bwd(dh, x2, rs, g1, dxo):
    n, d = x2.shape
    tm = 1024

    def body(dh_ref, x_ref, rs_ref, g_ref, dxo_ref, gx_ref, dg_ref):
        x, r = x_ref[...], rs_ref[...]
        gd = dh_ref[...] * g_ref[...]
        gx_ref[...] = dxo_ref[...] + r * (gd - x * (r * r) * jnp.mean(gd * x, axis=-1, keepdims=True))
        dg = jnp.sum(dh_ref[...] * x * r, axis=0, keepdims=True)

        @pl.when(pl.program_id(0) == 0)
        def _():
            dg_ref[...] = dg

        @pl.when(pl.program_id(0) != 0)
        def _():
            dg_ref[...] += dg

    t = pl.BlockSpec((tm, d), lambda i: (i, 0))
    return pl.pallas_call(
        body, name="prenorm_bwd", grid=(n // tm,),
        in_specs=[t, t, pl.BlockSpec((tm, 1), lambda i: (i, 0)), pl.BlockSpec((1, d), lambda i: (0, 0)), t],
        out_specs=[t, pl.BlockSpec((1, d), lambda i: (0, 0))], out_shape=[SDS((n, d), F32), SDS((1, d), F32)],
        compiler_params=_params(("arbitrary",)))(dh, x2, rs, g1, dxo)


def _mesh_pos():
    x, y, c = lax.axis_index("x"), lax.axis_index("y"), lax.axis_index("c")
    return 4 * x + 2 * y + c


def _coords(idx):
    return (idx // 4, (idx // 2) % 2, idx % 2)


def _exchange(srcs, to_all, name):
    n = len(srcs)

    def body(*refs):
        src_refs, dst_refs = refs[:n], refs[n:2 * n]
        send_sems, recv_sems, local_sems = refs[2 * n:]
        me = _mesh_pos()

        def piece(i, j):
            return src_refs[i] if to_all[i] else src_refs[i].at[j]

        def remote(i, off, peer, block, slot):
            return pltpu.make_async_remote_copy(src_ref=piece(i, block), dst_ref=dst_refs[i].at[slot],
                                                send_sem=send_sems.at[i, off - 1], recv_sem=recv_sems.at[i, off - 1],
                                                device_id=_coords(peer), device_id_type=MESH)

        local = [pltpu.make_async_copy(piece(i, me), dst_refs[i].at[me], local_sems.at[i]) for i in range(n)]
        for cp in local:
            cp.start()
        sends = []
        for off in range(1, N_DEV):
            to = (me + off) % N_DEV
            for i in range(n):
                sends.append(remote(i, off, to, to, me))
                sends[-1].start()
        for off in range(1, N_DEV):
            frm = (me + N_DEV - off) % N_DEV
            for i in range(n):
                remote(i, off, frm, me, frm).wait_recv()
        for cp in sends:
            cp.wait_send()
        for cp in local:
            cp.wait()

    outs = pl.pallas_call(
        body, name=name, in_specs=[pl.BlockSpec(memory_space=pltpu.HBM)] * n, out_specs=[pl.BlockSpec(memory_space=pltpu.HBM)] * n,
        out_shape=[SDS((N_DEV,) + s.shape[-2:], s.dtype) for s in srcs],
        scratch_shapes=[pltpu.SemaphoreType.DMA((n, N_DEV - 1)), pltpu.SemaphoreType.DMA((n, N_DEV - 1)), pltpu.SemaphoreType.DMA((n,))],
        compiler_params=pltpu.CompilerParams())(*srcs)
    return outs


_HBM = pl.BlockSpec(memory_space=pltpu.HBM)
_SEM = pl.BlockSpec(memory_space=pltpu.SEMAPHORE)
_EFFECT = pltpu.SideEffectType.DATAFLOW_SIDE_EFFECTING


def _send_start(src):
    def body(src_ref, land_ref, send_sems, recv_sems, src_thru, land_thru, token):
        me = _mesh_pos()
        for off in range(1, N_DEV):
            to = (me + off) % N_DEV
            pltpu.make_async_remote_copy(src_ref=src_ref.at[to], dst_ref=land_ref.at[me], send_sem=send_sems.at[off - 1],
                                         recv_sem=recv_sems.at[off - 1], device_id=_coords(to), device_id_type=MESH).start()
        token[...] = jnp.zeros_like(token)

    hbm = pltpu.HBM(src.shape, src.dtype)
    return pl.pallas_call(
        body, name="grads_start",
        out_shape=(pltpu.SemaphoreType.DMA((N_DEV - 1,)), pltpu.SemaphoreType.DMA((N_DEV - 1,)), hbm, hbm, SDS((8, LANE), BF16)),
        in_specs=(_HBM, _HBM), out_specs=(_SEM, _SEM, _HBM, _HBM, pl.BlockSpec(memory_space=pltpu.VMEM)),
        input_output_aliases={0: 2, 1: 3}, compiler_params=pltpu.CompilerParams(has_side_effects=_EFFECT),
    )(pltpu.with_memory_space_constraint(src, pltpu.HBM), pltpu.with_memory_space_constraint(jnp.zeros(src.shape, src.dtype), pltpu.HBM))


def _send_wait(send_sems, recv_sems, src_thru, land_thru, after):
    def body(src_ref, land_ref, send_sems, recv_sems, after_ref, src_dead, got_ref):
        me = _mesh_pos()
        for off in range(1, N_DEV):
            to, frm = (me + off) % N_DEV, (me + N_DEV - off) % N_DEV
            pltpu.make_async_remote_copy(src_ref=src_ref.at[to], dst_ref=land_ref.at[me], send_sem=send_sems.at[off - 1],
                                         recv_sem=recv_sems.at[off - 1], device_id=_coords(to), device_id_type=MESH).wait_send()
            pltpu.make_async_remote_copy(src_ref=src_ref.at[me], dst_ref=land_ref.at[frm], send_sem=send_sems.at[off - 1],
                                         recv_sem=recv_sems.at[off - 1], device_id=_coords(frm), device_id_type=MESH).wait_recv()

    hbm = pltpu.HBM(src_thru.shape, src_thru.dtype)
    return pl.pallas_call(
        body, name="grads_wait", out_shape=(hbm, hbm), in_specs=(_HBM, _HBM, _SEM, _SEM, pl.BlockSpec(memory_space=pl.ANY)),
        out_specs=(_HBM, _HBM), input_output_aliases={0: 0, 1: 1}, compiler_params=pltpu.CompilerParams(has_side_effects=_EFFECT),
    )(src_thru, land_thru, send_sems, recv_sems, after)[1]


def _gather(srcs, name):
    n = len(srcs)

    def body(*refs):
        src_refs, dst_refs = refs[:n], refs[n:2 * n]
        send_sems, recv_sems, local_sems = refs[2 * n:]
        x, y, c = lax.axis_index("x"), lax.axis_index("y"), lax.axis_index("c")
        me, sibling = (x, y, c), (x, y, 1 - c)
        chips = [(1 - x, y), (x, 1 - y), (1 - x, 1 - y)]

        def slot(i, dev):
            return dst_refs[i].at[4 * dev[0] + 2 * dev[1] + dev[2]]

        def copy(i, k, block, to, own=False):
            return pltpu.make_async_remote_copy(src_ref=src_refs[i] if own else slot(i, block), dst_ref=slot(i, block),
                                                send_sem=send_sems.at[i, k], recv_sem=recv_sems.at[i, k],
                                                device_id=to, device_id_type=MESH)

        local = [pltpu.make_async_copy(src_refs[i], slot(i, me), local_sems.at[i]) for i in range(n)]
        for cp in local:
            cp.start()
        sends = []
        for i in range(n):
            sends.append(copy(i, 0, me, sibling, own=True))
            sends += [copy(i, 1 + j, me, (*chip, c), own=True) for j, chip in enumerate(chips)]
        for cp in sends:
            cp.start()
        for j, chip in enumerate(chips):
            for i in range(n):
                copy(i, 1 + j, (*chip, c), me).wait_recv()
                sends.append(copy(i, 4 + j, (*chip, c), sibling))
                sends[-1].start()
        for i in range(n):
            copy(i, 0, sibling, me).wait_recv()
            for j, chip in enumerate(chips):
                copy(i, 4 + j, (*chip, 1 - c), me).wait_recv()
        for cp in sends:
            cp.wait_send()
        for cp in local:
            cp.wait()

    return pl.pallas_call(
        body, name=name, in_specs=[pl.BlockSpec(memory_space=pltpu.HBM)] * n, out_specs=[pl.BlockSpec(memory_space=pltpu.HBM)] * n,
        out_shape=[SDS((N_DEV,) + s.shape, s.dtype) for s in srcs],
        scratch_shapes=[pltpu.SemaphoreType.DMA((n, N_DEV - 1)), pltpu.SemaphoreType.DMA((n, N_DEV - 1)), pltpu.SemaphoreType.DMA((n,))],
        compiler_params=pltpu.CompilerParams())(*srcs)


def _adamw(parts, w, m, v, tr, name, own=None):
    rows, cols = w.shape
    c1, c2 = 1.0 - ADAM_B1 ** ADAM_STEP, 1.0 - ADAM_B2 ** ADAM_STEP

    def body(p_ref, *refs):
        w_ref, m_ref, v_ref, g_ref, d_ref, nm_ref, nv_ref = refs[-7:]
        me = _mesh_pos()

        def part(j):
            return p_ref[j] if own is None else jnp.where(me == j, refs[0][...], p_ref[j])

        g = part(0).astype(F32)
        for j in range(1, N_DEV):
            g = g + part(j).astype(F32)
        nm = ADAM_B1 * m_ref[...] + (1.0 - ADAM_B1) * g
        nv = ADAM_B2 * v_ref[...] + (1.0 - ADAM_B2) * jnp.square(g)
        g_ref[...] = g
        nm_ref[...] = nm
        nv_ref[...] = nv
        d_ref[...] = -ADAM_LR * ((nm / c1) / (jnp.sqrt(nv / c2) + ADAM_EPS) + ADAM_WD * w_ref[...])

    t = pl.BlockSpec((tr, cols), lambda i: (i, 0))
    extra = [] if own is None else [own]
    return pl.pallas_call(
        body, name=name, grid=(rows // tr,), in_specs=[pl.BlockSpec((N_DEV, tr, cols), lambda i: (0, i, 0))] + [t] * (3 + len(extra)),
        out_specs=[t] * 4, out_shape=[SDS((rows, cols), F32)] * 4, compiler_params=_params(("parallel",)))(parts, *extra, w, m, v)


SHARDED = (("w_in", D_MODEL, IN_COLS // N_DEV, True, 128), ("w_up_attn", WIDTH, D_MODEL // N_DEV, True, WIDTH),
           ("w_up_rwkv", WIDTH, D_MODEL // N_DEV, True, WIDTH), ("w_out", D_MODEL // N_DEV, D_MODEL, False, D_MODEL // N_DEV),
           ("rwkv_w_up", LORA, WIDTH // N_DEV, True, LORA), ("rwkv_a_up", LORA, WIDTH // N_DEV, True, LORA))
LOSS_SLOT = sum(n for _, n in SMALL)


def _pack_small(small, extra=None):
    flat = [small[n].reshape(-1).astype(F32) for n, _ in SMALL]
    flat.append(jnp.zeros((1,), F32) if extra is None else extra.reshape(1))
    flat.append(jnp.zeros((SMALL_ROWS * LANE - LOSS_SLOT - 1,), F32))
    return jnp.concatenate(flat).reshape(SMALL_ROWS, LANE)


def _unpack_small(packed, shapes):
    flat = packed.reshape(-1)
    out, off = {}, 0
    for n, cnt in SMALL:
        out[n] = flat[off:off + cnt].reshape(shapes[n])
        off += cnt
    return out, flat[LOSS_SLOT]


def _whole(gathered, by_cols):
    if not by_cols:
        return gathered.reshape(-1, gathered.shape[-1])
    return gathered.transpose(1, 0, 2).reshape(gathered.shape[1], -1)


def _per_owner(full, by_cols):
    if not by_cols:
        return full.reshape(N_DEV, -1, full.shape[-1])
    return full.reshape(full.shape[0], N_DEV, -1).transpose(1, 0, 2)


def _local_step(x, loss_target, sm, wts):
    bsz, s, d = x.shape
    n = bsz * s
    x2, tgt = x.reshape(n, d), loss_target.reshape(n, d)
    bidx = jnp.asarray(_bucket_tables())
    w_in = wts["w_in"]
    segs = (("qkv", 0, QKV_COLS, 1536), ("za", OFF_ZA, WIDTH, 512), ("pr", OFF_PR, PR_COLS, PR_COLS), ("zr", OFF_ZR, WIDTH, 512),
            ("gm", OFF_GM, 2 * D_MODEL, 512))

    h, rs = _prenorm(x2, sm["pre_norm_gain"])
    proj = {nm: _mm(h, w_in[:, off:off + cnt], tn, "proj_" + nm) for nm, off, cnt, tn in segs}
    qkv3 = proj["qkv"].reshape(bsz, s, QKV_COLS)
    pr3 = proj["pr"].reshape(bsz, s, PR_COLS)

    o_attn, lse = _attn_fwd(qkv3, sm["rel_bias"], bidx)
    rk = sm["rwkv_r_k"].reshape(1, WIDTH)
    pre_args = (sm["rwkv_shift_mix"], sm["rwkv_w0"], wts["rwkv_w_up"], sm["rwkv_a0"], wts["rwkv_a_up"], sm["rwkv_k_k"], sm["rwkv_k_a"])
    scan_in = _rwkv_pre(pr3, *pre_args)
    o_rwkv, states, consts = _rwkv_scan(scan_in, rk, sm["rwkv_ln_w"], sm["rwkv_ln_b"])

    (dxo, do_attn, do_rwkv, dza, dzr, dgm, g_wua, g_wur, g_wout, g_post, loss) = _head(
        o_attn.reshape(n, WIDTH), o_rwkv.reshape(n, WIDTH), proj["za"], proj["zr"], proj["gm"], x2, tgt,
        wts["w_up_attn"], wts["w_up_rwkv"], wts["w_out"], sm["post_norm_gain"])

    dqkv, dbias = _attn_bwd(qkv3, o_attn, lse, do_attn.reshape(bsz, s, WIDTH), sm["rel_bias"], bidx)
    g_bias = _bias_grad(dbias, bidx)[:, :N_BUCKET].T

    scan_cots, (g_rk, g_lnw, g_lnb) = _rwkv_scan_bwd(scan_in, states, consts, do_rwkv.reshape(bsz, s, WIDTH), rk, sm["rwkv_ln_w"],
                                                     sm["rwkv_ln_b"])
    dprs, g_mix, g_w0, g_wup, g_a0, g_aup, g_kk, g_ka = _rwkv_pre_bwd(pr3, scan_cots, *pre_args)
    dpr = _shift_bwd(dprs, sm["rwkv_shift_mix"]).reshape(n, PR_COLS)

    dsegs = [(dqkv.reshape(9, n, WIDTH), 0, QKV_COLS, WIDTH), (dza, OFF_ZA, WIDTH, WIDTH), (dpr, OFF_PR, PR_COLS, PR_COLS),
             (dzr, OFF_ZR, WIDTH, WIDTH), (dgm, OFF_GM, 2 * D_MODEL, D_MODEL)]
    g_win = jnp.concatenate([_mm_tn(h, t, tn, "gw_in_%d" % j) for j, (t, _, _, tn) in enumerate(dsegs)], axis=1)
    blocks = _per_owner(g_win, True).astype(BF16)
    own = lax.dynamic_index_in_dim(blocks, 4 * lax.axis_index("x") + 2 * lax.axis_index("y") + lax.axis_index("c"), 0, keepdims=False)
    send_sems, recv_sems, blocks_thru, land_thru, token = _send_start(blocks)
    dh = None
    for j, (t, off, cnt, _) in enumerate(dsegs):
        dh = _mm_nt_acc(t, w_in[:, off:off + cnt] + token[0, 0], dh, "dh_%d" % j)
    grad_x, g_pre = _prenorm_bwd(dh, x2, rs, sm["pre_norm_gain"], dxo)
    landed = _send_wait(send_sems, recv_sems, blocks_thru, land_thru, g_pre)

    full = {"w_up_attn": g_wua, "w_up_rwkv": g_wur, "w_out": g_wout, "rwkv_w_up": g_wup, "rwkv_a_up": g_aup}
    small = {"pre_norm_gain": g_pre, "rel_bias": g_bias, "rwkv_shift_mix": g_mix, "rwkv_w0": g_w0, "rwkv_a0": g_a0, "rwkv_k_k": g_kk,
             "rwkv_k_a": g_ka, "rwkv_r_k": g_rk, "rwkv_ln_w": g_lnw, "rwkv_ln_b": g_lnb, "post_norm_gain": g_post}
    return loss[0, 0], grad_x.reshape(bsz, s, d), (landed, own), full, small


def kernel(x, pre_norm_gain, w_in, rel_bias, rwkv_shift_mix, rwkv_w0, rwkv_w_up, rwkv_a0, rwkv_a_up, rwkv_k_k, rwkv_k_a, rwkv_r_k, rwkv_ln_w, rwkv_ln_b, w_up_attn, w_up_rwkv, w_out, post_norm_gain, loss_target, m_pre_norm_gain, m_w_in, m_rel_bias, m_rwkv_shift_mix, m_rwkv_w0, m_rwkv_w_up, m_rwkv_a0, m_rwkv_a_up, m_rwkv_k_k, m_rwkv_k_a, m_rwkv_r_k, m_rwkv_ln_w, m_rwkv_ln_b, m_w_up_attn, m_w_up_rwkv, m_w_out, m_post_norm_gain, v_pre_norm_gain, v_w_in, v_rel_bias, v_rwkv_shift_mix, v_rwkv_w0, v_rwkv_w_up, v_rwkv_a0, v_rwkv_a_up, v_rwkv_k_k, v_rwkv_k_a, v_rwkv_r_k, v_rwkv_ln_w, v_rwkv_ln_b, v_w_up_attn, v_w_up_rwkv, v_w_out, v_post_norm_gain):
    names = [n for n, *_ in SHARDED] + [n for n, _ in SMALL]
    loc = dict(locals())
    w = {n: loc[n] for n in names}
    m = {n: loc["m_" + n] for n in names}
    v = {n: loc["v_" + n] for n in names}
    shapes = {n: w[n].shape for n in names}
    order = ["pre_norm_gain", "w_in", "rel_bias", "rwkv_shift_mix", "rwkv_w0", "rwkv_w_up", "rwkv_a0", "rwkv_a_up", "rwkv_k_k", "rwkv_k_a",
             "rwkv_r_k", "rwkv_ln_w", "rwkv_ln_b", "w_up_attn", "w_up_rwkv", "w_out", "post_norm_gain"]
    shard2d = lambda t, n, r, c: t[n].reshape(r, c)

    gathered = _gather([shard2d(w, n, r, c).astype(BF16) for n, r, c, _, _ in SHARDED], "gather_weights")
    wts = {n: _whole(g, by_cols) for (n, _, _, by_cols, _), g in zip(SHARDED, gathered)}

    loss, grad_x, (win_landed, win_own), full, small = _local_step(x, loss_target, w, wts)
    rest = SHARDED[1:]
    parts = _exchange([_per_owner(full[n], by_cols).astype(BF16) for n, _, _, by_cols, _ in rest] + [_pack_small(small, loss)],
                      [False] * len(rest) + [True], "exchange_grads")

    outs = [{}, {}, {}, {}]
    for (n, r, c, _, tr), p in zip(SHARDED, [win_landed] + list(parts)):
        res = _adamw(p, shard2d(w, n, r, c), shard2d(m, n, r, c), shard2d(v, n, r, c), tr, "adamw_" + n,
                     own=win_own if n == "w_in" else None)
        for o, t in zip(outs, res):
            o[n] = t.reshape(shapes[n])
    res = _adamw(parts[-1], _pack_small(w), _pack_small(m), _pack_small(v), SMALL_ROWS, "adamw_small")
    for o, t in zip(outs, res):
        o.update(_unpack_small(t, shapes)[0])
    loss = _unpack_small(res[0], shapes)[1]
    return (loss, grad_x, *[o[n] for o in outs for n in order])
```

```python
import functools
import math

import numpy as np
import jax
import jax.numpy as jnp
from jax import lax
from jax.experimental import pallas as pl
from jax.experimental.pallas import tpu as pltpu

F32, BF16 = jnp.float32, jnp.bfloat16
SDS = jax.ShapeDtypeStruct
HI = lax.Precision.HIGHEST
MESH = pl.DeviceIdType.MESH

N_DEV = 8
D_MODEL = 1024
HEAD = 64
N_HEAD = 8
WIDTH = N_HEAD * HEAD
DILATIONS = (1, 4, 16)
QB = 128
N_BUCKET = 32
MAX_DIST = 2048
LORA = 64
QKV_COLS = 9 * WIDTH
PR_COLS = 3 * WIDTH + 2 * LORA
IN_COLS = QKV_COLS + WIDTH + PR_COLS + WIDTH + 2 * D_MODEL
OFF_ZA, OFF_PR, OFF_ZR, OFF_GM = QKV_COLS, QKV_COLS + WIDTH, QKV_COLS + WIDTH + PR_COLS, QKV_COLS + 2 * WIDTH + PR_COLS
RMS_EPS = 1e-6
GN_EPS = 64e-5
SCALE = 1.0 / math.sqrt(HEAD)
CHUNK = 64
CHUNK_GROUP = 8
BWD_GROUP = 16
EARLY = 8
NEG = -1e30
LANE = 128

ADAM_LR, ADAM_B1, ADAM_B2, ADAM_EPS, ADAM_WD, ADAM_STEP = 0.001, 0.9, 0.999, 1e-08, 0.01, 10

VMEM_LIMIT = 56 * 1024 * 1024

SMALL = (("pre_norm_gain", 1024), ("rel_bias", 768), ("rwkv_shift_mix", 1664), ("rwkv_w0", 512), ("rwkv_a0", 512),
         ("rwkv_k_k", 512), ("rwkv_k_a", 512), ("rwkv_r_k", 512), ("rwkv_ln_w", 512), ("rwkv_ln_b", 512),
         ("post_norm_gain", 1024))
SMALL_ROWS = 64


def _params(sem=None):
    return pltpu.CompilerParams(dimension_semantics=sem, vmem_limit_bytes=VMEM_LIMIT)


def _dot(a, b):
    return jnp.dot(a, b, preferred_element_type=F32)


def _dot_nt(a, b):
    return lax.dot_general(a, b, (((1,), (1,)), ((), ())), preferred_element_type=F32)


def _dot_tn(a, b):
    return lax.dot_general(a, b, (((0,), (0,)), ((), ())), preferred_element_type=F32)


@jax.custom_vjp
def _bdot(a, b):
    return _dot(a.astype(BF16), b.astype(BF16))


def _bdot_fwd(a, b):
    return _bdot(a, b), (a, b)


def _bdot_bwd(res, g):
    a, b = res
    gb = g.astype(BF16)
    return _dot_nt(gb, b.astype(BF16)), _dot_tn(a.astype(BF16), gb)


_bdot.defvjp(_bdot_fwd, _bdot_bwd)


def _silu(z):
    return z * jax.nn.sigmoid(z)


def _dsilu(z):
    s = jax.nn.sigmoid(z)
    return s * (1.0 + z * (1.0 - s))


def _softplus(x):
    return jnp.maximum(x, 0.0) + jnp.log(1.0 + jnp.exp(-jnp.abs(x)))


def _bucket_tables():
    qi = np.arange(QB)[:, None] + QB
    ki = np.arange(2 * QB)[None, :]
    rel = np.maximum(qi - ki, 0)
    out = []
    for d in DILATIONS:
        dist = rel * d
        max_exact = N_BUCKET // 2
        ratio = np.log(np.maximum(dist, 1).astype(np.float32) / max_exact) / np.float32(math.log(MAX_DIST / max_exact))
        large = max_exact + (ratio * (N_BUCKET - max_exact)).astype(np.int32)
        large = np.minimum(large, N_BUCKET - 1)
        out.append(np.where(dist < max_exact, dist, large).astype(np.int32))
    return np.stack(out)


def _prenorm(x2, g):
    n, d = x2.shape
    tm = 1024

    def body(x_ref, g_ref, h_ref, rs_ref):
        x = x_ref[...]
        rs = lax.rsqrt(jnp.mean(x * x, axis=-1, keepdims=True) + RMS_EPS)
        h_ref[...] = (x * rs * g_ref[...]).astype(BF16)
        rs_ref[...] = rs

    return pl.pallas_call(
        body, name="prenorm", grid=(n // tm,),
        in_specs=[pl.BlockSpec((tm, d), lambda i: (i, 0)), pl.BlockSpec((1, d), lambda i: (0, 0))],
        out_specs=[pl.BlockSpec((tm, d), lambda i: (i, 0)), pl.BlockSpec((tm, 1), lambda i: (i, 0))],
        out_shape=[SDS((n, d), BF16), SDS((n, 1), F32)], compiler_params=_params(("parallel",)))(x2, g)


def _mm(a, b, tn, name):
    m, k = a.shape
    n = b.shape[1]
    tm = 1024

    def body(a_ref, b_ref, o_ref):
        o_ref[...] = _dot(a_ref[...], b_ref[...])

    return pl.pallas_call(
        body, name=name, grid=(n // tn, m // tm),
        in_specs=[pl.BlockSpec((tm, k), lambda j, i: (i, 0)), pl.BlockSpec((k, tn), lambda j, i: (0, j))],
        out_specs=pl.BlockSpec((tm, tn), lambda j, i: (i, j)),
        out_shape=SDS((m, n), F32), compiler_params=_params(("parallel", "parallel")))(a, b)


def _mm_nt_acc(a, b, acc, name):
    split = a.ndim == 3
    m = a.shape[-2]
    k = b.shape[1]
    d = b.shape[0]
    tm = 1024
    per = 3 if split else 1
    seg = a.shape[2] if split else 0
    tk = per * seg if split else (k if k <= 2048 else 1536)
    have_acc = acc is not None

    def body(*refs):
        if have_acc:
            a_ref, b_ref, c_ref, o_ref = refs
        else:
            a_ref, b_ref, o_ref = refs
        if split:
            r = sum(_dot_nt(a_ref[j].astype(BF16), b_ref[:, seg * j:seg * (j + 1)]) for j in range(per))
        else:
            r = _dot_nt(a_ref[...].astype(BF16), b_ref[...])

        @pl.when(pl.program_id(1) == 0)
        def _():
            o_ref[...] = r + c_ref[...] if have_acc else r

        @pl.when(pl.program_id(1) != 0)
        def _():
            o_ref[...] += r

    a_spec = pl.BlockSpec((per, tm, seg), lambda i, j: (j, i, 0)) if split else pl.BlockSpec((tm, tk), lambda i, j: (i, j))
    in_specs = [a_spec, pl.BlockSpec((d, tk), lambda i, j: (0, j))]
    args = [a, b]
    if have_acc:
        in_specs.append(pl.BlockSpec((tm, d), lambda i, j: (i, 0)))
        args.append(acc)
    return pl.pallas_call(
        body, name=name, grid=(m // tm, k // tk), in_specs=in_specs, out_specs=pl.BlockSpec((tm, d), lambda i, j: (i, 0)),
        out_shape=SDS((m, d), F32), compiler_params=_params(("parallel", "arbitrary")))(*args)


def _mm_nt_multi(a_list, b_list, acc, name):
    m, d = acc.shape
    tm = 512
    n = len(a_list)

    def body(*refs):
        r = refs[2 * n][...]
        for a_ref, b_ref in zip(refs[:n], refs[n:2 * n]):
            r = r + _dot_nt(a_ref[...].astype(BF16), b_ref[...])
        refs[2 * n + 1][...] = r

    in_specs = [pl.BlockSpec((tm, a.shape[1]), lambda i: (i, 0)) for a in a_list]
    in_specs += [pl.BlockSpec(b.shape, lambda i: (0, 0)) for b in b_list]
    in_specs.append(pl.BlockSpec((tm, d), lambda i: (i, 0)))
    return pl.pallas_call(
        body, name=name, grid=(m // tm,), in_specs=in_specs, out_specs=pl.BlockSpec((tm, d), lambda i: (i, 0)),
        out_shape=SDS((m, d), F32), compiler_params=_params(("parallel",)))(*a_list, *b_list, acc)


def _mm_tn(a, b, tn, name):
    split = b.ndim == 3
    m, k1 = a.shape
    per = 3 if split else 1
    seg = b.shape[2] if split else tn
    tn = per * seg
    n2 = b.shape[0] * seg if split else b.shape[1]
    tm = 1024

    def body(a_ref, b_ref, o_ref):
        first = pl.program_id(1) == 0
        for j in range(per):
            r = _dot_tn(a_ref[...], (b_ref[j] if split else b_ref[...]).astype(BF16))
            cols = slice(seg * j, seg * (j + 1))

            @pl.when(first)
            def _(r=r, cols=cols):
                o_ref[:, cols] = r

            @pl.when(jnp.logical_not(first))
            def _(r=r, cols=cols):
                o_ref[:, cols] += r

    b_spec = pl.BlockSpec((per, tm, seg), lambda j, i: (j, i, 0)) if split else pl.BlockSpec((tm, tn), lambda j, i: (i, j))
    return pl.pallas_call(
        body, name=name, grid=(n2 // tn, m // tm),
        in_specs=[pl.BlockSpec((tm, k1), lambda j, i: (i, 0)), b_spec],
        out_specs=pl.BlockSpec((k1, tn), lambda j, i: (0, j)),
        out_shape=SDS((k1, n2), F32), compiler_params=_params(("parallel", "arbitrary")))(a, b)


def _ds(start, d):
    return pl.ds(start, QB) if d == 1 else pl.ds(start, QB, stride=d)


def _fill_bias(tab_ref, bidx_ref, bias_sc, hp):
    for g in range(3):
        bi = bidx_ref[g]
        for h in range(2):
            acc = jnp.zeros((QB, 2 * QB), F32)
            for j in range(N_BUCKET):
                acc = jnp.where(bi == j, tab_ref[j, g * N_HEAD + hp * 2 + h], acc)
            bias_sc[g * 2 + h] = acc


def _block_starts(it, d, nb):
    rho = it // nb
    n = it % nb
    st = rho + d * QB * n
    stp = rho + d * QB * jnp.maximum(n - 1, 0)
    if d == 1:
        st, stp = pl.multiple_of(QB * it, QB), pl.multiple_of(QB * jnp.maximum(it - 1, 0), QB)
    return st, stp, n > 0


ATTN_BLOCKS = 4


def _bdot3(a, b, dims):
    return lax.dot_general(a, b, (dims, ((0,), (0,))), preferred_element_type=F32)


def _attn_operands(q_ref, k_ref, v_ref, bias_sc, g, d, nb, it0):
    two = nb > 1
    nk = 2 * QB if two else QB
    ii = lax.broadcasted_iota(jnp.int32, (QB, nk), 0)
    cc = lax.broadcasted_iota(jnp.int32, (QB, nk), 1)
    qs, ks, vs, pens, starts = [], [], [], [], []
    for u in range(ATTN_BLOCKS):
        st, stp, hasprev = _block_starts(it0 + u, d, nb)
        qf = q_ref[0, _ds(st, d), :]
        if two:
            kf = jnp.concatenate([k_ref[0, _ds(stp, d), :], k_ref[0, _ds(st, d), :]], axis=0).astype(BF16)
            vf = jnp.concatenate([v_ref[0, _ds(stp, d), :], v_ref[0, _ds(st, d), :]], axis=0).astype(BF16)
            own = jnp.logical_and(cc >= QB, ii >= cc - QB)
            prev = jnp.logical_and(jnp.logical_and(cc < QB, cc >= ii), hasprev)
            pen = jnp.where(jnp.logical_or(own, prev), 0.0, NEG)
        else:
            kf, vf = k_ref[0, _ds(st, d), :].astype(BF16), v_ref[0, _ds(st, d), :].astype(BF16)
            pen = jnp.where(ii >= cc, 0.0, NEG)
        for h in range(2):
            qs.append(_one_head(qf, h).astype(BF16))
            ks.append(kf)
            vs.append(vf)
            pens.append(pen + (bias_sc[g * 2 + h] if two else bias_sc[g * 2 + h, :, QB:2 * QB]))
        starts.append((st, stp))
    return _stack(qs), _stack(ks), _stack(vs), _stack(pens), starts


def _one_head(x, h):
    lane = lax.broadcasted_iota(jnp.int32, x.shape, 1)
    return jnp.where(lane >= HEAD if h == 1 else lane < HEAD, x, 0.0)


def _pick_heads(x, u):
    lane = lax.broadcasted_iota(jnp.int32, x.shape[1:], 1)
    return jnp.where(lane < HEAD, x[2 * u], x[2 * u + 1])


def _add_heads(x, u):
    return x[2 * u] + x[2 * u + 1]


def _attn_fwd(qkv3, rel_bias, bidx):
    bsz, s, _ = qkv3.shape
    rt = 256

    def body(tab_ref, bidx_ref, *refs):
        q_refs, k_refs, v_refs = refs[0:3], refs[3:6], refs[6:9]
        o_ref, lse_ref = refs[9:11]
        bias_sc, num_sc, den_sc, m_sc = refs[11:]
        pl.when(pl.program_id(1) == 0)(lambda: _fill_bias(tab_ref, bidx_ref, bias_sc, pl.program_id(0)))
        for g, d in enumerate(DILATIONS):
            nb = s // (QB * d)

            def blk(it, c, g=g, d=d, nb=nb):
                q, k, v, bias, starts = _attn_operands(q_refs[g], k_refs[g], v_refs[g], bias_sc, g, d, nb, it * ATTN_BLOCKS)
                sc = _bdot3(q, k, ((2,), (2,))) * SCALE + bias
                m = jnp.max(sc, axis=-1, keepdims=True)
                p = jnp.exp(sc - m)
                den = jnp.sum(p, axis=-1, keepdims=True)
                num = _bdot3(p.astype(BF16), v, ((2,), (1,)))
                den, m = jnp.broadcast_to(den, num.shape), jnp.broadcast_to(m, num.shape)
                for u, (st, _) in enumerate(starts):
                    num_sc[g, _ds(st, d), :] = _pick_heads(num, u)
                    den_sc[g, _ds(st, d), :] = _pick_heads(den, u)
                    m_sc[g, _ds(st, d), :] = _pick_heads(m, u)
                return c

            lax.fori_loop(0, s // QB // ATTN_BLOCKS, blk, 0)

        def merge(i, c):
            rows = pl.ds(pl.multiple_of(i * rt, rt), rt)
            m0, m1, m2 = m_sc[0, rows, :], m_sc[1, rows, :], m_sc[2, rows, :]
            mall = jnp.maximum(jnp.maximum(m0, m1), m2)
            w0, w1, w2 = jnp.exp(m0 - mall), jnp.exp(m1 - mall), jnp.exp(m2 - mall)
            num = w0 * num_sc[0, rows, :] + w1 * num_sc[1, rows, :] + w2 * num_sc[2, rows, :]
            den = w0 * den_sc[0, rows, :] + w1 * den_sc[1, rows, :] + w2 * den_sc[2, rows, :]
            o_ref[0, rows, :] = num / den
            lse_ref[0, rows, :] = mall + jnp.log(den)
            return c

        lax.fori_loop(0, s // rt, merge, 0)

    col = lambda w, g: (lambda hp, b: (b, 0, (w * 3 + g) * 4 + hp))
    in_specs = [pl.BlockSpec(memory_space=pltpu.SMEM), pl.BlockSpec((3, QB, 2 * QB), lambda hp, b: (0, 0, 0))]
    in_specs += [pl.BlockSpec((1, s, LANE), col(w, g)) for w in range(3) for g in range(3)]
    out_spec = pl.BlockSpec((1, s, LANE), lambda hp, b: (b, 0, hp))
    return pl.pallas_call(
        body, name="attn_fwd", grid=(4, bsz), in_specs=in_specs, out_specs=[out_spec, out_spec],
        out_shape=[SDS((bsz, s, WIDTH), F32), SDS((bsz, s, WIDTH), F32)],
        scratch_shapes=[pltpu.VMEM((6, QB, 2 * QB), F32), pltpu.VMEM((3, s, LANE), F32), pltpu.VMEM((3, s, LANE), F32),
                        pltpu.VMEM((3, s, LANE), F32)],
        compiler_params=_params(("arbitrary", "arbitrary")))(rel_bias, bidx, *([qkv3] * 9))


def _attn_bwd(qkv3, o3, lse3, do3, rel_bias, bidx):
    bsz, s, _ = qkv3.shape
    rt = 256

    def body(tab_ref, bidx_ref, *refs):
        q_refs, k_refs, v_refs = refs[0:3], refs[3:6], refs[6:9]
        o_ref, lse_ref, do_ref, dqkv_ref, db_ref, bias_sc, delta_sc, acc_sc = refs[9:]
        dq_refs, dk_refs, dv_refs = ([acc_sc.at[w * 3 + g] for g in range(3)] for w in range(3))

        @pl.when(pl.program_id(1) == 0)
        def _():
            _fill_bias(tab_ref, bidx_ref, bias_sc, pl.program_id(0))
            db_ref[...] = jnp.zeros_like(db_ref)

        def prep(i, c):
            rows = pl.ds(pl.multiple_of(i * rt, rt), rt)
            prod = do_ref[0, rows, :] * o_ref[0, rows, :]
            d0 = jnp.sum(prod[:, :HEAD], axis=-1, keepdims=True)
            d1 = jnp.sum(prod[:, HEAD:], axis=-1, keepdims=True)
            delta_sc[rows, :] = jnp.concatenate([jnp.broadcast_to(d0, (rt, HEAD)), jnp.broadcast_to(d1, (rt, HEAD))], axis=1)
            z = jnp.zeros((rt, LANE), F32)
            for g in range(3):
                dk_refs[g][0, rows, :] = z
                dv_refs[g][0, rows, :] = z
            return c

        lax.fori_loop(0, s // rt, prep, 0)
        for g, d in enumerate(DILATIONS):
            nb = s // (QB * d)

            def blk(it, c, g=g, d=d, nb=nb):
                q, k, v, bias, starts = _attn_operands(q_refs[g], k_refs[g], v_refs[g], bias_sc, g, d, nb, it * ATTN_BLOCKS)
                dos, lses, deltas = [], [], []
                for st, _ in starts:
                    dof, lsef, delf = do_ref[0, _ds(st, d), :], lse_ref[0, _ds(st, d), :], delta_sc[_ds(st, d), :]
                    for h in range(2):
                        dos.append(_one_head(dof, h).astype(BF16))
                        lses.append(lsef[:, HEAD * h:HEAD * h + 1])
                        deltas.append(delf[:, HEAD * h:HEAD * h + 1])
                do, lse, delta = _stack(dos), _stack(lses), _stack(deltas)
                p = jnp.exp(_bdot3(q, k, ((2,), (2,))) * SCALE + bias - lse)
                dv = _bdot3(p.astype(BF16), do, ((1,), (1,)))
                ds = p * (_bdot3(do, v, ((2,), (2,))) - delta)
                dsb = ds.astype(BF16)
                dq = _bdot3(dsb, k, ((2,), (1,))) * SCALE
                dk = _bdot3(dsb, q, ((1,), (1,))) * SCALE
                two = nb > 1
                for h in range(2):
                    dsum = sum(ds[2 * u + h] for u in range(ATTN_BLOCKS))
                    if two:
                        db_ref[0, g * 2 + h] += dsum
                    else:
                        db_ref[0, g * 2 + h, :, QB:2 * QB] += dsum
                for u, (st, stp) in enumerate(starts):
                    dq_refs[g][0, _ds(st, d), :] = _pick_heads(dq, u)
                    if two:
                        dk_refs[g][0, _ds(stp, d), :] += _add_heads(dk[:, :QB], u)
                        dv_refs[g][0, _ds(stp, d), :] += _add_heads(dv[:, :QB], u)
                    dk_refs[g][0, _ds(st, d), :] += _add_heads(dk[:, QB:] if two else dk, u)
                    dv_refs[g][0, _ds(st, d), :] += _add_heads(dv[:, QB:] if two else dv, u)
                return c

            lax.fori_loop(0, s // QB // ATTN_BLOCKS, blk, 0)

        def flush(i, c):
            rows = pl.ds(pl.multiple_of(i * rt, rt), rt)
            for j in range(9):
                dqkv_ref[j, 0, rows, :] = acc_sc[j, 0, rows, :].astype(BF16)
            return c

        lax.fori_loop(0, s // rt, flush, 0)

    col = lambda w, g: (lambda hp, b: (b, 0, (w * 3 + g) * 4 + hp))
    blk_spec = pl.BlockSpec((1, s, LANE), lambda hp, b: (b, 0, hp))
    in_specs = [pl.BlockSpec(memory_space=pltpu.SMEM), pl.BlockSpec((3, QB, 2 * QB), lambda hp, b: (0, 0, 0))]
    in_specs += [pl.BlockSpec((1, s, LANE), col(w, g)) for w in range(3) for g in range(3)]
    in_specs += [blk_spec] * 3
    out_specs = [pl.BlockSpec((9, 1, s, LANE), lambda hp, b: (0, b, 0, hp)), pl.BlockSpec((1, 6, QB, 2 * QB), lambda hp, b: (hp, 0, 0, 0))]
    out_shape = [SDS((9, bsz, s, WIDTH), BF16), SDS((4, 6, QB, 2 * QB), F32)]
    return pl.pallas_call(
        body, name="attn_bwd", grid=(4, bsz), in_specs=in_specs, out_specs=out_specs, out_shape=out_shape,
        scratch_shapes=[pltpu.VMEM((6, QB, 2 * QB), F32), pltpu.VMEM((s, LANE), F32), pltpu.VMEM((9, 1, s, LANE), F32)],
        compiler_params=_params(("parallel", "arbitrary")))(rel_bias, bidx, *([qkv3] * 9), o3, lse3, do3)


def _bias_grad(dbias, bidx):
    def body(db_ref, bidx_ref, o_ref):
        lane = lax.broadcasted_iota(jnp.int32, (1, LANE), 1)
        for g in range(3):
            bi = bidx_ref[g]
            for hp in range(4):
                for h in range(2):
                    mat = db_ref[hp, g * 2 + h]
                    row = jnp.zeros((1, LANE), F32)
                    for j in range(N_BUCKET):
                        part = jnp.sum(jnp.where(bi == j, mat, 0.0), axis=0, keepdims=True)
                        row = jnp.where(lane == j, jnp.sum(part, axis=1, keepdims=True), row)
                    hd = g * N_HEAD + hp * 2 + h
                    o_ref[hd:hd + 1, :] = row

    return pl.pallas_call(body, name="bias_grad", out_shape=SDS((3 * N_HEAD, LANE), F32), compiler_params=_params())(dbias, bidx)


def _pre_fn(r, k0, v, wl, al, w0, wup, a0, aup, kk_, ka_):
    u = w0 + _bdot(jnp.tanh(wl), wup)
    lw = -jnp.exp(-_softplus(-u) - 0.5)
    a = jax.nn.sigmoid(a0 + _bdot(al, aup))
    kkraw = k0 * kk_
    k = k0 * (1.0 + (a - 1.0) * ka_)
    return r, lw, k, v, kkraw, a


PRE_SPLIT = (0, WIDTH, 2 * WIDTH, 3 * WIDTH, 3 * WIDTH + LORA, 3 * WIDTH + 2 * LORA)


def _pre_pieces(prs):
    return [prs[:, a:b] for a, b in zip(PRE_SPLIT[:-1], PRE_SPLIT[1:])]


PRE_TT = 512


def _shifted(pr_ref, edge_ref, first, back):
    pr = pr_ref[0]
    tt = pr.shape[0]
    row = lax.broadcasted_iota(jnp.int32, (tt, 1), 0)
    if back:
        edge = jnp.where(first, 0.0, edge_ref[0, 7:8, :])
        return jnp.where(row == 0, edge, pltpu.roll(pr, 1, axis=0))
    edge = jnp.where(first, 0.0, edge_ref[0, 0:1, :])
    return jnp.where(row == tt - 1, edge, pltpu.roll(pr, tt - 1, axis=0))


def _rwkv_pre(pr3, mix, w0, wup, a0, aup, kk_, ka_):
    bsz, s, _ = pr3.shape
    tt = PRE_TT

    def body(pr_ref, edge_ref, mix_ref, w0_ref, wup_ref, a0_ref, aup_ref, kk_ref, ka_ref, *outs):
        pr = pr_ref[0]
        prev = _shifted(pr_ref, edge_ref, pl.program_id(1) == 0, True)
        prs = pr + (prev - pr) * mix_ref[...]
        vals = _pre_fn(*_pre_pieces(prs), w0_ref[...], wup_ref[...].astype(F32), a0_ref[...], aup_ref[...].astype(F32), kk_ref[...],
                       ka_ref[...])
        for o, val in zip(outs, vals):
            o[0] = val

    vec = lambda n: pl.BlockSpec((1, n), lambda b, i: (0, 0))
    mat = pl.BlockSpec((LORA, WIDTH), lambda b, i: (0, 0))
    in_specs = [pl.BlockSpec((1, tt, PR_COLS), lambda b, i: (b, i, 0)),
                pl.BlockSpec((1, 8, PR_COLS), lambda b, i: (b, jnp.maximum(i * (tt // 8) - 1, 0), 0)),
                vec(PR_COLS), vec(WIDTH), mat, vec(WIDTH), mat, vec(WIDTH), vec(WIDTH)]
    out_spec = pl.BlockSpec((1, tt, WIDTH), lambda b, i: (b, i, 0))
    return pl.pallas_call(
        body, name="rwkv_pre", grid=(bsz, s // tt), in_specs=in_specs, out_specs=[out_spec] * 6,
        out_shape=[SDS((bsz, s, WIDTH), F32)] * 6, compiler_params=_params(("parallel", "parallel")))(
            pr3, pr3, mix, w0, wup, a0, aup, kk_, ka_)


def _rwkv_pre_bwd(pr3, cots, mix, w0, wup, a0, aup, kk_, ka_):
    bsz, s, _ = pr3.shape
    tt = PRE_TT

    def body(pr_ref, edge_ref, c0, c1, c2, c3, c4, c5, mix_ref, w0_ref, wup_ref, a0_ref, aup_ref, kk_ref, ka_ref,
             dprs_ref, dmix_ref, dw0_ref, dwup_ref, da0_ref, daup_ref, dkk_ref, dka_ref):
        pr = pr_ref[0]
        prev = _shifted(pr_ref, edge_ref, pl.program_id(1) == 0, True)
        prs = pr + (prev - pr) * mix_ref[...]
        _, vjp = jax.vjp(_pre_fn, *_pre_pieces(prs), w0_ref[...], wup_ref[...].astype(F32), a0_ref[...], aup_ref[...].astype(F32),
                         kk_ref[...], ka_ref[...])
        grads = vjp(tuple(c[0] for c in (c0, c1, c2, c3, c4, c5)))
        for piece, a, b in zip(grads[:5], PRE_SPLIT[:-1], PRE_SPLIT[1:]):
            dprs_ref[0, :, a:b] = piece
        dw0, dwup, da0, daup, dkk, dka = grads[5:]
        dprs = dprs_ref[0]
        grads = (jnp.sum(dprs * (prev - pr), axis=0, keepdims=True), dw0, dwup, da0, daup, dkk, dka)
        refs = (dmix_ref, dw0_ref, dwup_ref, da0_ref, daup_ref, dkk_ref, dka_ref)
        first = jnp.logical_and(pl.program_id(0) == 0, pl.program_id(1) == 0)

        @pl.when(first)
        def _():
            for r_, g_ in zip(refs, grads):
                r_[...] = g_

        @pl.when(jnp.logical_not(first))
        def _():
            for r_, g_ in zip(refs, grads):
                r_[...] += g_

    vec = lambda n: pl.BlockSpec((1, n), lambda b, i: (0, 0))
    mat = pl.BlockSpec((LORA, WIDTH), lambda b, i: (0, 0))
    tile = pl.BlockSpec((1, tt, WIDTH), lambda b, i: (b, i, 0))
    in_specs = [pl.BlockSpec((1, tt, PR_COLS), lambda b, i: (b, i, 0)),
                pl.BlockSpec((1, 8, PR_COLS), lambda b, i: (b, jnp.maximum(i * (tt // 8) - 1, 0), 0))]
    in_specs += [tile] * 6 + [vec(PR_COLS), vec(WIDTH), mat, vec(WIDTH), mat, vec(WIDTH), vec(WIDTH)]
    out_specs = [pl.BlockSpec((1, tt, PR_COLS), lambda b, i: (b, i, 0)), vec(PR_COLS), vec(WIDTH), mat, vec(WIDTH), mat,
                 vec(WIDTH), vec(WIDTH)]
    out_shape = [SDS((bsz, s, PR_COLS), F32), SDS((1, PR_COLS), F32), SDS((1, WIDTH), F32), SDS((LORA, WIDTH), F32),
                 SDS((1, WIDTH), F32), SDS((LORA, WIDTH), F32), SDS((1, WIDTH), F32), SDS((1, WIDTH), F32)]
    return pl.pallas_call(
        body, name="rwkv_pre_bwd", grid=(bsz, s // tt), in_specs=in_specs, out_specs=out_specs, out_shape=out_shape,
        compiler_params=_params(("arbitrary", "arbitrary")))(pr3, pr3, *cots, mix, w0, wup, a0, aup, kk_, ka_)


def _shift_bwd(dprs3, mix):
    bsz, s, _ = dprs3.shape
    tt = PRE_TT
    nt = s // tt

    def body(d_ref, edge_ref, mix_ref, o_ref):
        nxt = _shifted(d_ref, edge_ref, pl.program_id(1) == nt - 1, False)
        m = mix_ref[...]
        o_ref[0] = (d_ref[0] * (1.0 - m) + nxt * m).astype(BF16)

    in_specs = [pl.BlockSpec((1, tt, PR_COLS), lambda b, i: (b, i, 0)),
                pl.BlockSpec((1, 8, PR_COLS), lambda b, i: (b, jnp.minimum((i + 1) * (tt // 8), s // 8 - 1), 0)),
                pl.BlockSpec((1, PR_COLS), lambda b, i: (0, 0))]
    return pl.pallas_call(
        body, name="shift_bwd", grid=(bsz, nt), in_specs=in_specs, out_specs=pl.BlockSpec((1, tt, PR_COLS), lambda b, i: (b, i, 0)),
        out_shape=SDS((bsz, s, PR_COLS), BF16), compiler_params=_params(("parallel", "parallel")))(dprs3, dprs3, mix)


_NN, _NT, _TN = ((2,), (1,)), ((2,), (2,)), ((1,), (1,))


def _dot3_bf16(a, b, dims):
    return lax.dot_general(a.astype(BF16), b.astype(BF16), (dims, ((0,), (0,))), preferred_element_type=F32)


class _Dots:
    def __init__(self, fwd):
        def make(dims, da_rule, db_rule):
            @jax.custom_vjp
            def f(a, b):
                return fwd(a, b, dims)

            f.defvjp(lambda a, b: (f(a, b), (a, b)), lambda res, g: (da_rule(*res, g), db_rule(*res, g)))
            return f

        one = _dot3_bf16
        self.mm = make(_NN, lambda a, b, g: one(g, b, _NT), lambda a, b, g: one(a, g, _TN))
        self.mm_nt = make(_NT, lambda a, b, g: one(g, b, _NN), lambda a, b, g: one(g, a, _TN))
        self.mm_tn = make(_TN, lambda a, b, g: one(b, g, _NT), lambda a, b, g: one(a, g, _NN))

        def powers(aab):
            ps = [aab]
            while 2 ** len(ps) < aab.shape[1]:
                ps.append(fwd(ps[-1], ps[-1], _NN))
            return ps

        def apply(ps, z, dims):
            for p in ps:
                z = z + fwd(p, z, dims)
            return z

        @jax.custom_vjp
        def solve(aab, z):
            return apply(powers(aab), z, _NN)

        def solve_fwd(aab, z):
            ps = powers(aab)
            x = apply(ps, z, _NN)
            return x, (ps, x)

        def solve_bwd(res, g):
            ps, x = res
            dz = apply(ps, g, _TN)
            return fwd(dz, x, _NT), dz

        solve.defvjp(solve_fwd, solve_bwd)
        self.solve = solve


_ONE_PASS = _Dots(_dot3_bf16)
_bmm, _bmm_tn = _ONE_PASS.mm, _ONE_PASS.mm_tn


def _chunk_fn(s0t, r, lw, k, v, kkraw, a, rk, lnw, lnb, first=False, d=_ONE_PASS):
    c = r.shape[1]
    at, rt, btc, ktc, gc, aab, arb, xv, arkv, ain, bin_ = _chunk_core(r, lw, k, v, kkraw, a, d)
    rs = d.mm(jnp.concatenate([at, rt], axis=1), s0t)
    u = d.solve(aab, rs[:, :c] + xv)
    y = rs[:, c:] + d.mm(arb, u) + arkv
    if first:
        y = _with_early_rows(y, r, lw, k, v, ain, bin_)
    gcol = jnp.sum(_diag(gc), axis=2, keepdims=True)
    sct = gcol * s0t + d.mm_tn(jnp.concatenate([btc, ktc], axis=1), jnp.concatenate([u, v], axis=1))
    return _post(y, r, k, v, rk, lnw, lnb), sct


def _diag(gc):
    return jnp.where(_masks(HEAD)[2], gc, 0.0)


def _with_early_rows(y, r, lw, k, v, ain, bin_):
    early = _early_rows(r[:2], lw[:2], k[:2], v[:2], ain[:2], bin_[:2])
    return jnp.concatenate([jnp.concatenate([early, y[:2, EARLY:]], axis=1), y[2:]], axis=0)


def _early_rows(r, lw, k, v, ain, bin_):
    cols = lambda x: _stack([jnp.transpose(x[h]) for h in range(2)])
    wc, bc, kc = cols(jnp.exp(lw)), cols(bin_), cols(k)
    st = jnp.zeros((2, HEAD, HEAD), F32)
    rows = []
    for t in range(EARLY):
        sa = _ONE_PASS.mm(ain[:, t:t + 1], st)
        st = st * wc[:, :, t:t + 1] + bc[:, :, t:t + 1] * sa + kc[:, :, t:t + 1] * v[:, t:t + 1]
        rows.append(_ONE_PASS.mm(r[:, t:t + 1], st))
    return jnp.concatenate(rows, axis=1)


def _chunk_rows(c):
    return pl.ds(c * CHUNK, CHUNK) if isinstance(c, int) else pl.ds(pl.multiple_of(c * CHUNK, CHUNK), CHUNK)


def _stack(xs):
    return jnp.concatenate([x[None] for x in xs], axis=0)


def _pairs(ref, chunks):
    tiles = [ref[0, _chunk_rows(c), :] for c in chunks]
    return _stack([t[:, HEAD * h:HEAD * h + HEAD] for t in tiles for h in range(2)])


def _unpair(vals, j):
    return jnp.concatenate([vals[2 * j], vals[2 * j + 1]], axis=1)


def _masks(c):
    ii = lax.broadcasted_iota(jnp.int32, (c, c), 0)
    jj = lax.broadcasted_iota(jnp.int32, (c, c), 1)
    return ii > jj, ii >= jj, ii == jj


def _chunk_core(r, lw, k, v, kkraw, a, d=_ONE_PASS):
    g_, c = r.shape[0], r.shape[1]
    nrm = jnp.sqrt(jnp.sum(kkraw * kkraw, axis=-1, keepdims=True))
    kkn = kkraw / jnp.maximum(nrm, 1e-12)
    ain, bin_ = -kkn, kkn * a
    strict, incl, _ = _masks(c)
    lg = lax.dot_general(jnp.broadcast_to(incl.astype(F32), (g_, c, c)), lw, (((2,), (1,)), ((0,), (0,))), precision=HI,
                         preferred_element_type=F32)
    g, gp, gi = jnp.exp(lg), jnp.exp(lg - lw), jnp.exp(-lg)
    at, rt, bt, kt = ain * gp, r * g, bin_ * gi, k * gi
    aa = d.mm_nt(jnp.concatenate([at, rt], axis=1), jnp.concatenate([bt, kt], axis=1))
    aab = jnp.where(strict, aa[:, :c, :c], 0.0)
    aak = jnp.where(strict, aa[:, :c, c:], 0.0)
    arb = jnp.where(incl, aa[:, c:, :c], 0.0)
    ark = jnp.where(incl, aa[:, c:, c:], 0.0)
    akv = d.mm(jnp.concatenate([aak, ark], axis=1), v)
    gc = g[:, c - 1:c, :]
    return at, rt, bt * gc, kt * gc, gc, aab, arb, akv[:, :c], akv[:, c:], ain, bin_


def _post(y, r, k, v, rk, lnw, lnb):
    mu = jnp.mean(y, axis=-1, keepdims=True)
    var = jnp.mean(jnp.square(y - mu), axis=-1, keepdims=True)
    yn = (y - mu) * lax.rsqrt(var + GN_EPS) * lnw + lnb
    return yn + jnp.sum(r * k * rk, axis=-1, keepdims=True) * v


def _chunk_consts(r, lw, k, v, kkraw, a, first=False):
    d = _ONE_PASS
    at, rt, btc, ktc, gc, aab, arb, xv, arkv, ain, bin_ = _chunk_core(r, lw, k, v, kkraw, a, d)
    z = d.solve(aab, jnp.concatenate([at, xv], axis=2))
    ryv = jnp.concatenate([rt, arkv], axis=2) + d.mm(arb, z)
    if first:
        ryv = jnp.concatenate([ryv[:, :, :HEAD], _with_early_rows(ryv[:, :, HEAD:], r, lw, k, v, ain, bin_)], axis=2)
    mkv = d.mm_tn(btc, z) + jnp.concatenate([_diag(gc), d.mm_tn(ktc, v)], axis=2)
    return mkv, ryv


def _rwkv_scan(ins, rk, lnw, lnb):
    bsz, s, _ = ins[0].shape
    nch = s // CHUNK

    def consts_body(r_ref, lw_ref, k_ref, v_ref, kk_ref, a_ref, mkv_ref, ry_ref, yv_ref):
        def group(i, carry):
            chunks = [i * CHUNK_GROUP + j for j in range(CHUNK_GROUP)]
            mkv, ryv = _chunk_consts(*[_pairs(ref, chunks) for ref in (r_ref, lw_ref, k_ref, v_ref, kk_ref, a_ref)],
                                     first=isinstance(i, int) and i == 0)
            for j, c in enumerate(chunks):
                for h in range(2):
                    mkv_ref[0, 0, c, h] = mkv[2 * j + h]
                ry_ref[0, _chunk_rows(c), :] = jnp.concatenate([ryv[2 * j][:, :HEAD], ryv[2 * j + 1][:, :HEAD]], axis=1)
                yv_ref[0, _chunk_rows(c), :] = jnp.concatenate([ryv[2 * j][:, HEAD:], ryv[2 * j + 1][:, HEAD:]], axis=1)
            return carry

        group(0, 0)
        lax.fori_loop(1, nch // CHUNK_GROUP, group, 0)

    tile = pl.BlockSpec((1, s, LANE), lambda b, hp: (b, 0, hp))
    vec = pl.BlockSpec((1, LANE), lambda b, hp: (0, hp))
    mkv_spec = pl.BlockSpec((1, 1, nch, 2, HEAD, LANE), lambda b, hp: (b, hp, 0, 0, 0, 0))
    st_spec = pl.BlockSpec((1, 1, nch, 2, HEAD, HEAD), lambda b, hp: (b, hp, 0, 0, 0, 0))
    mkv, ry, yv = pl.pallas_call(
        consts_body, name="rwkv_consts", grid=(bsz, 4), in_specs=[tile] * 6, out_specs=[mkv_spec, tile, tile],
        out_shape=[SDS((bsz, 4, nch, 2, HEAD, LANE), F32), SDS((bsz, s, WIDTH), F32), SDS((bsz, s, WIDTH), F32)],
        compiler_params=_params(("parallel", "parallel")))(*ins)

    states = _chunk_recurrence(mkv, None, "rwkv_states")

    def out_body(ry_ref, yv_ref, r_ref, k_ref, v_ref, st_ref, rk_ref, lnw_ref, lnb_ref, o_ref):
        y, r, k, v, rk_, lnw_, lnb_ = _scan_rows(ry_ref, yv_ref, r_ref, k_ref, v_ref, st_ref, rk_ref, lnw_ref, lnb_ref)
        o = _post(y, r, k, v, rk_, lnw_, lnb_)
        for j in range(CHUNK_GROUP):
            o_ref[0, _chunk_rows(j), :] = _unpair(o, j)

    o = pl.pallas_call(
        out_body, name="rwkv_out", grid=(bsz, 4, nch // CHUNK_GROUP), in_specs=_group_specs(5), out_specs=_group_specs(1)[0],
        out_shape=SDS((bsz, s, WIDTH), F32),
        compiler_params=_params(("parallel", "parallel", "parallel")))(ry, yv, ins[0], ins[2], ins[3], states, rk, lnw, lnb)
    return o, states, (mkv, ry, yv)


def _group_specs(n_tiles):
    tile = pl.BlockSpec((1, CHUNK_GROUP * CHUNK, LANE), lambda b, hp, t: (b, t, hp))
    if n_tiles == 1:
        return [tile]
    st = pl.BlockSpec((1, 1, CHUNK_GROUP, 2, HEAD, HEAD), lambda b, hp, t: (b, hp, t, 0, 0, 0))
    vec = pl.BlockSpec((1, LANE), lambda b, hp, t: (0, hp))
    return [tile] * n_tiles + [st] + [vec] * 3


def _scan_rows(ry_ref, yv_ref, r_ref, k_ref, v_ref, st_ref, rk_ref, lnw_ref, lnb_ref):
    chunks = list(range(CHUNK_GROUP))
    ry, yv, r, k, v = (_pairs(ref, chunks) for ref in (ry_ref, yv_ref, r_ref, k_ref, v_ref))
    st = _stack([st_ref[0, 0, c, h] for c in chunks for h in range(2)])
    vecs = [_stack([ref[:, HEAD * h:HEAD * h + HEAD] for _ in chunks for h in range(2)]) for ref in (rk_ref, lnw_ref, lnb_ref)]
    return (_bmm(ry, st) + yv, r, k, v, *vecs)


def _chunk_recurrence(mkv, q, name):
    bsz, _, nch = mkv.shape[:3]
    pairs = [(hp, h) for hp in range(4) for h in range(2)]

    def body(*refs):
        mkv_ref, out_ref, acc = refs[0], refs[-2], refs[-1]
        acc[...] = jnp.zeros_like(acc)

        def step(i, carry):
            c = i if q is None else nch - 1 - i
            cur = acc[...]
            for j, (hp, h) in enumerate(pairs):
                out_ref[0, hp, c, h] = cur[j]
            m = _stack([mkv_ref[0, hp, c, h] for hp, h in pairs])
            if q is None:
                acc[...] = _bmm(m[:, :, :HEAD], cur) + m[:, :, HEAD:]
            else:
                acc[...] = _bmm_tn(m[:, :, :HEAD], cur) + _stack([refs[1][0, hp, c, h] for hp, h in pairs])
            return carry

        lax.fori_loop(0, nch, step, 0)

    spec = lambda w: pl.BlockSpec((1, 4, nch, 2, HEAD, w), lambda b: (b, 0, 0, 0, 0, 0))
    return pl.pallas_call(
        body, name=name, grid=(bsz,), in_specs=[spec(LANE)] + ([] if q is None else [spec(HEAD)]), out_specs=spec(HEAD),
        out_shape=SDS((bsz, 4, nch, 2, HEAD, HEAD), F32), scratch_shapes=[pltpu.VMEM((8, HEAD, HEAD), F32)],
        compiler_params=_params(("parallel",)))(*([mkv] if q is None else [mkv, q]))


def _rwkv_scan_bwd(ins, states, consts, do3, rk, lnw, lnb):
    bsz, s, _ = ins[0].shape
    nch = s // CHUNK

    mkv, ry, yv = consts

    def q_body(do_ref, ry_ref, yv_ref, r_ref, k_ref, v_ref, st_ref, rk_ref, lnw_ref, lnb_ref, q_ref):
        y, r, k, v, rk_, lnw_, lnb_ = _scan_rows(ry_ref, yv_ref, r_ref, k_ref, v_ref, st_ref, rk_ref, lnw_ref, lnb_ref)
        _, vjp = jax.vjp(lambda y_: _post(y_, r, k, v, rk_, lnw_, lnb_), y)
        (dy,) = vjp(_pairs(do_ref, list(range(CHUNK_GROUP))))
        q = _bmm_tn(_pairs(ry_ref, list(range(CHUNK_GROUP))), dy)
        for j in range(CHUNK_GROUP):
            for h in range(2):
                q_ref[0, 0, j, h] = q[2 * j + h]

    specs = _group_specs(6)
    q = pl.pallas_call(
        q_body, name="rwkv_q", grid=(bsz, 4, nch // CHUNK_GROUP), in_specs=specs, out_specs=specs[6],
        out_shape=SDS((bsz, 4, nch, 2, HEAD, HEAD), F32),
        compiler_params=_params(("parallel", "parallel", "parallel")))(do3, ry, yv, ins[0], ins[2], ins[3], states, rk, lnw, lnb)

    dstates = _chunk_recurrence(mkv, q, "rwkv_dstates")

    def body(r_ref, lw_ref, k_ref, v_ref, kk_ref, a_ref, st_ref, dst_ref, do_ref, rk_ref, lnw_ref, lnb_ref,
             dr_ref, dlw_ref, dk_ref, dv_ref, dkk_ref, da_ref, drk_ref, dlnw_ref, dlnb_ref):
        chunks = list(range(BWD_GROUP))
        par_refs = (drk_ref, dlnw_ref, dlnb_ref)

        @pl.when(jnp.logical_and(pl.program_id(1) == 0, pl.program_id(2) == 0))
        def _():
            for ref in par_refs:
                ref[...] = jnp.zeros_like(ref)

        def group(first):
            per_pair = lambda ref: _stack([ref[0, 0, c, h] for c in chunks for h in range(2)])
            vecs = [_stack([ref[:, HEAD * h:HEAD * h + HEAD] for _ in chunks for h in range(2)]) for ref in (rk_ref, lnw_ref, lnb_ref)]
            _, vjp = jax.vjp(functools.partial(_chunk_fn, first=first, d=_ONE_PASS), per_pair(st_ref),
                             *[_pairs(ref, chunks) for ref in (r_ref, lw_ref, k_ref, v_ref, kk_ref, a_ref)], *vecs)
            grads = vjp((_pairs(do_ref, chunks), per_pair(dst_ref)))
            for ref, cot in zip((dr_ref, dlw_ref, dk_ref, dv_ref, dkk_ref, da_ref), grads[1:7]):
                for j, c in enumerate(chunks):
                    ref[0, _chunk_rows(c), :] = _unpair(cot, j)
            for ref, g_ in zip(par_refs, grads[7:10]):
                ref[...] += jnp.concatenate([sum(g_[2 * j + h] for j in range(BWD_GROUP)) for h in range(2)], axis=1)

        pl.when(pl.program_id(2) == 0)(functools.partial(group, True))
        pl.when(pl.program_id(2) != 0)(functools.partial(group, False))

    tt = BWD_GROUP * CHUNK
    tile = pl.BlockSpec((1, tt, LANE), lambda hp, b, t: (b, t, hp))
    vec = pl.BlockSpec((1, LANE), lambda hp, b, t: (0, hp))
    st_spec = pl.BlockSpec((1, 1, BWD_GROUP, 2, HEAD, HEAD), lambda hp, b, t: (b, hp, t, 0, 0, 0))
    outs = pl.pallas_call(
        body, name="rwkv_scan_bwd", grid=(4, bsz, s // tt), in_specs=[tile] * 6 + [st_spec, st_spec, tile] + [vec] * 3,
        out_specs=[tile] * 6 + [vec] * 3,
        out_shape=[SDS((bsz, s, WIDTH), F32)] * 6 + [SDS((1, WIDTH), F32)] * 3,
        compiler_params=_params(("parallel", "arbitrary", "arbitrary")))(*ins, states, dstates, do3, rk, lnw, lnb)
    return outs[:6], outs[6:]


def _head(o_attn, o_rwkv, z_attn, z_rwkv, gm, x2, tgt, wua, wur, wout, g2):
    n = x2.shape[0]
    tm = 256
    nt = n // tm
    d = D_MODEL

    def body(oa_ref, or_ref, za_ref, zr_ref, gm_ref, x_ref, t_ref, wua_ref, wur_ref, wout_ref, g2_ref,
             dxo_ref, doa_ref, dor_ref, dza_ref, dzr_ref, dgm_ref, dwua_ref, dwur_ref, dwout_ref, dg2_ref, loss_ref, lacc):
        i = pl.program_id(0)
        oa, orw, za, zr = oa_ref[...], or_ref[...], za_ref[...], zr_ref[...]
        ga, gb = gm_ref[:, 0:d], gm_ref[:, d:2 * d]
        am = (oa * _silu(za)).astype(BF16)
        bm = (orw * _silu(zr)).astype(BF16)
        ya, yb = _dot(am, wua_ref[...]), _dot(bm, wur_ref[...])
        sa, sb = jax.nn.sigmoid(ga), jax.nn.sigmoid(gb)
        merged = (sa * ya + sb * yb).astype(BF16)
        out = _dot(merged, wout_ref[...])
        rs = lax.rsqrt(jnp.mean(out * out, axis=-1, keepdims=True) + RMS_EPS)
        g2 = g2_ref[...]
        err = x_ref[...] + out * rs * g2 - t_ref[...]
        lpart = jnp.sum(err * err, axis=0, keepdims=True)
        dxo = err * (1.0 / d)
        dxo_ref[...] = dxo
        dg2 = jnp.sum(dxo * out * rs, axis=0, keepdims=True)
        gd = dxo * g2
        dout = (rs * (gd - out * (rs * rs) * jnp.mean(gd * out, axis=-1, keepdims=True))).astype(BF16)
        dmerged = _dot_nt(dout, wout_ref[...])
        dwout = _dot_tn(merged, dout)
        dya, dyb = (dmerged * sa).astype(BF16), (dmerged * sb).astype(BF16)
        dgm_ref[:, 0:d] = (dmerged * ya * sa * (1.0 - sa)).astype(BF16)
        dgm_ref[:, d:2 * d] = (dmerged * yb * sb * (1.0 - sb)).astype(BF16)
        dam, dbm = _dot_nt(dya, wua_ref[...]), _dot_nt(dyb, wur_ref[...])
        dwua, dwur = _dot_tn(am, dya), _dot_tn(bm, dyb)
        doa_ref[...] = dam * _silu(za)
        dza_ref[...] = (dam * oa * _dsilu(za)).astype(BF16)
        dor_ref[...] = dbm * _silu(zr)
        dzr_ref[...] = (dbm * orw * _dsilu(zr)).astype(BF16)

        @pl.when(i == 0)
        def _():
            dwua_ref[...], dwur_ref[...], dwout_ref[...], dg2_ref[...], lacc[...] = dwua, dwur, dwout, dg2, lpart

        @pl.when(i != 0)
        def _():
            dwua_ref[...] += dwua
            dwur_ref[...] += dwur
            dwout_ref[...] += dwout
            dg2_ref[...] += dg2
            lacc[...] += lpart

        @pl.when(i == nt - 1)
        def _():
            loss_ref[...] = jnp.sum(lacc[...], axis=1, keepdims=True) * (0.5 / d)

    t512 = pl.BlockSpec((tm, WIDTH), lambda i: (i, 0))
    t1k = pl.BlockSpec((tm, d), lambda i: (i, 0))
    t2k = pl.BlockSpec((tm, 2 * d), lambda i: (i, 0))
    full = lambda r, c: pl.BlockSpec((r, c), lambda i: (0, 0))
    return pl.pallas_call(
        body, name="head_fwd_bwd", grid=(nt,),
        in_specs=[t512, t512, t512, t512, t2k, t1k, t1k, full(WIDTH, d), full(WIDTH, d), full(d, d), full(1, d)],
        out_specs=[t1k, t512, t512, t512, t512, t2k, full(WIDTH, d), full(WIDTH, d), full(d, d), full(1, d), full(1, 1)],
        out_shape=[SDS((n, d), F32), SDS((n, WIDTH), F32), SDS((n, WIDTH), F32), SDS((n, WIDTH), BF16), SDS((n, WIDTH), BF16),
                   SDS((n, 2 * d), BF16), SDS((WIDTH, d), F32), SDS((WIDTH, d), F32), SDS((d, d), F32), SDS((1, d), F32), SDS((1, 1), F32)],
        scratch_shapes=[pltpu.VMEM((1, d), F32)],
        compiler_params=_params(("arbitrary",)))(o_attn, o_rwkv, z_attn, z_rwkv, gm, x2, tgt, wua, wur, wout, g2)


def _prenorm_bwd(dh, x2, rs, g1, dxo):
    n, d = x2.shape
    tm = 1024

    def body(dh_ref, x_ref, rs_ref, g_ref, dxo_ref, gx_ref, dg_ref):
        x, r = x_ref[...], rs_ref[...]
        gd = dh_ref[...] * g_ref[...]
        gx_ref[...] = dxo_ref[...] + r * (gd - x * (r * r) * jnp.mean(gd * x, axis=-1, keepdims=True))
        dg = jnp.sum(dh_ref[...] * x * r, axis=0, keepdims=True)

        @pl.when(pl.program_id(0) == 0)
        def _():
            dg_ref[...] = dg

        @pl.when(pl.program_id(0) != 0)
        def _():
            dg_ref[...] += dg

    t = pl.BlockSpec((tm, d), lambda i: (i, 0))
    return pl.pallas_call(
        body, name="prenorm_bwd", grid=(n // tm,),
        in_specs=[t, t, pl.BlockSpec((tm, 1), lambda i: (i, 0)), pl.BlockSpec((1, d), lambda i: (0, 0)), t],
        out_specs=[t, pl.BlockSpec((1, d), lambda i: (0, 0))], out_shape=[SDS((n, d), F32), SDS((1, d), F32)],
        compiler_params=_params(("arbitrary",)))(dh, x2, rs, g1, dxo)


def _mesh_pos():
    x, y, c = lax.axis_index("x"), lax.axis_index("y"), lax.axis_index("c")
    return 4 * x + 2 * y + c


def _coords(idx):
    return (idx // 4, (idx // 2) % 2, idx % 2)


def _exchange(srcs, to_all, name):
    n = len(srcs)

    def body(*refs):
        src_refs, dst_refs = refs[:n], refs[n:2 * n]
        send_sems, recv_sems, local_sems = refs[2 * n:]
        me = _mesh_pos()

        def piece(i, j):
            return src_refs[i] if to_all[i] else src_refs[i].at[j]

        def remote(i, off, peer, block, slot):
            return pltpu.make_async_remote_copy(src_ref=piece(i, block), dst_ref=dst_refs[i].at[slot],
                                                send_sem=send_sems.at[i, off - 1], recv_sem=recv_sems.at[i, off - 1],
                                                device_id=_coords(peer), device_id_type=MESH)

        local = [pltpu.make_async_copy(piece(i, me), dst_refs[i].at[me], local_sems.at[i]) for i in range(n)]
        for cp in local:
            cp.start()
        sends = []
        for off in range(1, N_DEV):
            to = (me + off) % N_DEV
            for i in range(n):
                sends.append(remote(i, off, to, to, me))
                sends[-1].start()
        for off in range(1, N_DEV):
            frm = (me + N_DEV - off) % N_DEV
            for i in range(n):
                remote(i, off, frm, me, frm).wait_recv()
        for cp in sends:
            cp.wait_send()
        for cp in local:
            cp.wait()

    outs = pl.pallas_call(
        body, name=name, in_specs=[pl.BlockSpec(memory_space=pltpu.HBM)] * n, out_specs=[pl.BlockSpec(memory_space=pltpu.HBM)] * n,
        out_shape=[SDS((N_DEV,) + s.shape[-2:], s.dtype) for s in srcs],
        scratch_shapes=[pltpu.SemaphoreType.DMA((n, N_DEV - 1)), pltpu.SemaphoreType.DMA((n, N_DEV - 1)), pltpu.SemaphoreType.DMA((n,))],
        compiler_params=pltpu.CompilerParams())(*srcs)
    return outs


_HBM = pl.BlockSpec(memory_space=pltpu.HBM)
_SEM = pl.BlockSpec(memory_space=pltpu.SEMAPHORE)
_EFFECT = pltpu.SideEffectType.DATAFLOW_SIDE_EFFECTING


def _send_start(src):
    def body(src_ref, land_ref, send_sems, recv_sems, src_thru, land_thru, token):
        me = _mesh_pos()
        for off in range(1, N_DEV):
            to = (me + off) % N_DEV
            pltpu.make_async_remote_copy(src_ref=src_ref.at[to], dst_ref=land_ref.at[me], send_sem=send_sems.at[off - 1],
                                         recv_sem=recv_sems.at[off - 1], device_id=_coords(to), device_id_type=MESH).start()
        token[...] = jnp.zeros_like(token)

    hbm = pltpu.HBM(src.shape, src.dtype)
    return pl.pallas_call(
        body, name="grads_start",
        out_shape=(pltpu.SemaphoreType.DMA((N_DEV - 1,)), pltpu.SemaphoreType.DMA((N_DEV - 1,)), hbm, hbm, SDS((8, LANE), BF16)),
        in_specs=(_HBM, _HBM), out_specs=(_SEM, _SEM, _HBM, _HBM, pl.BlockSpec(memory_space=pltpu.VMEM)),
        input_output_aliases={0: 2, 1: 3}, compiler_params=pltpu.CompilerParams(has_side_effects=_EFFECT),
    )(pltpu.with_memory_space_constraint(src, pltpu.HBM), pltpu.with_memory_space_constraint(jnp.zeros(src.shape, src.dtype), pltpu.HBM))


def _send_wait(send_sems, recv_sems, src_thru, land_thru, after):
    def body(src_ref, land_ref, send_sems, recv_sems, after_ref, src_dead, got_ref):
        me = _mesh_pos()
        for off in range(1, N_DEV):
            to, frm = (me + off) % N_DEV, (me + N_DEV - off) % N_DEV
            pltpu.make_async_remote_copy(src_ref=src_ref.at[to], dst_ref=land_ref.at[me], send_sem=send_sems.at[off - 1],
                                         recv_sem=recv_sems.at[off - 1], device_id=_coords(to), device_id_type=MESH).wait_send()
            pltpu.make_async_remote_copy(src_ref=src_ref.at[me], dst_ref=land_ref.at[frm], send_sem=send_sems.at[off - 1],
                                         recv_sem=recv_sems.at[off - 1], device_id=_coords(frm), device_id_type=MESH).wait_recv()

    hbm = pltpu.HBM(src_thru.shape, src_thru.dtype)
    return pl.pallas_call(
        body, name="grads_wait", out_shape=(hbm, hbm), in_specs=(_HBM, _HBM, _SEM, _SEM, pl.BlockSpec(memory_space=pl.ANY)),
        out_specs=(_HBM, _HBM), input_output_aliases={0: 0, 1: 1}, compiler_params=pltpu.CompilerParams(has_side_effects=_EFFECT),
    )(src_thru, land_thru, send_sems, recv_sems, after)[1]


def _gather(srcs, name):
    n = len(srcs)

    def body(*refs):
        src_refs, dst_refs = refs[:n], refs[n:2 * n]
        send_sems, recv_sems, local_sems = refs[2 * n:]
        x, y, c = lax.axis_index("x"), lax.axis_index("y"), lax.axis_index("c")
        me, sibling = (x, y, c), (x, y, 1 - c)
        chips = [(1 - x, y), (x, 1 - y), (1 - x, 1 - y)]

        def slot(i, dev):
            return dst_refs[i].at[4 * dev[0] + 2 * dev[1] + dev[2]]

        def copy(i, k, block, to, own=False):
            return pltpu.make_async_remote_copy(src_ref=src_refs[i] if own else slot(i, block), dst_ref=slot(i, block),
                                                send_sem=send_sems.at[i, k], recv_sem=recv_sems.at[i, k],
                                                device_id=to, device_id_type=MESH)

        local = [pltpu.make_async_copy(src_refs[i], slot(i, me), local_sems.at[i]) for i in range(n)]
        for cp in local:
            cp.start()
        sends = []
        for i in range(n):
            sends.append(copy(i, 0, me, sibling, own=True))
            sends += [copy(i, 1 + j, me, (*chip, c), own=True) for j, chip in enumerate(chips)]
        for cp in sends:
            cp.start()
        for j, chip in enumerate(chips):
            for i in range(n):
                copy(i, 1 + j, (*chip, c), me).wait_recv()
                sends.append(copy(i, 4 + j, (*chip, c), sibling))
                sends[-1].start()
        for i in range(n):
            copy(i, 0, sibling, me).wait_recv()
            for j, chip in enumerate(chips):
                copy(i, 4 + j, (*chip, 1 - c), me).wait_recv()
        for cp in sends:
            cp.wait_send()
        for cp in local:
            cp.wait()

    return pl.pallas_call(
        body, name=name, in_specs=[pl.BlockSpec(memory_space=pltpu.HBM)] * n, out_specs=[pl.BlockSpec(memory_space=pltpu.HBM)] * n,
        out_shape=[SDS((N_DEV,) + s.shape, s.dtype) for s in srcs],
        scratch_shapes=[pltpu.SemaphoreType.DMA((n, N_DEV - 1)), pltpu.SemaphoreType.DMA((n, N_DEV - 1)), pltpu.SemaphoreType.DMA((n,))],
        compiler_params=pltpu.CompilerParams())(*srcs)


def _adamw(parts, w, m, v, tr, name, own=None):
    rows, cols = w.shape
    c1, c2 = 1.0 - ADAM_B1 ** ADAM_STEP, 1.0 - ADAM_B2 ** ADAM_STEP

    def body(p_ref, *refs):
        w_ref, m_ref, v_ref, g_ref, d_ref, nm_ref, nv_ref = refs[-7:]
        me = _mesh_pos()

        def part(j):
            return p_ref[j] if own is None else jnp.where(me == j, refs[0][...], p_ref[j])

        g = part(0).astype(F32)
        for j in range(1, N_DEV):
            g = g + part(j).astype(F32)
        nm = ADAM_B1 * m_ref[...] + (1.0 - ADAM_B1) * g
        nv = ADAM_B2 * v_ref[...] + (1.0 - ADAM_B2) * jnp.square(g)
        g_ref[...] = g
        nm_ref[...] = nm
        nv_ref[...] = nv
        d_ref[...] = -ADAM_LR * ((nm / c1) / (jnp.sqrt(nv / c2) + ADAM_EPS) + ADAM_WD * w_ref[...])

    t = pl.BlockSpec((tr, cols), lambda i: (i, 0))
    extra = [] if own is None else [own]
    return pl.pallas_call(
        body, name=name, grid=(rows // tr,), in_specs=[pl.BlockSpec((N_DEV, tr, cols), lambda i: (0, i, 0))] + [t] * (3 + len(extra)),
        out_specs=[t] * 4, out_shape=[SDS((rows, cols), F32)] * 4, compiler_params=_params(("parallel",)))(parts, *extra, w, m, v)


SHARDED = (("w_in", D_MODEL, IN_COLS // N_DEV, True, 128), ("w_up_attn", WIDTH, D_MODEL // N_DEV, True, WIDTH),
           ("w_up_rwkv", WIDTH, D_MODEL // N_DEV, True, WIDTH), ("w_out", D_MODEL // N_DEV, D_MODEL, False, D_MODEL // N_DEV),
           ("rwkv_w_up", LORA, WIDTH // N_DEV, True, LORA), ("rwkv_a_up", LORA, WIDTH // N_DEV, True, LORA))
LOSS_SLOT = sum(n for _, n in SMALL)


def _pack_small(small, extra=None):
    flat = [small[n].reshape(-1).astype(F32) for n, _ in SMALL]
    flat.append(jnp.zeros((1,), F32) if extra is None else extra.reshape(1))
    flat.append(jnp.zeros((SMALL_ROWS * LANE - LOSS_SLOT - 1,), F32))
    return jnp.concatenate(flat).reshape(SMALL_ROWS, LANE)


def _unpack_small(packed, shapes):
    flat = packed.reshape(-1)
    out, off = {}, 0
    for n, cnt in SMALL:
        out[n] = flat[off:off + cnt].reshape(shapes[n])
        off += cnt
    return out, flat[LOSS_SLOT]


def _whole(gathered, by_cols):
    if not by_cols:
        return gathered.reshape(-1, gathered.shape[-1])
    return gathered.transpose(1, 0, 2).reshape(gathered.shape[1], -1)


def _per_owner(full, by_cols):
    if not by_cols:
        return full.reshape(N_DEV, -1, full.shape[-1])
    return full.reshape(full.shape[0], N_DEV, -1).transpose(1, 0, 2)


def _local_step(x, loss_target, sm, wts):
    bsz, s, d = x.shape
    n = bsz * s
    x2, tgt = x.reshape(n, d), loss_target.reshape(n, d)
    bidx = jnp.asarray(_bucket_tables())
    w_in = wts["w_in"]
    segs = (("qkv", 0, QKV_COLS, 1536), ("za", OFF_ZA, WIDTH, 512), ("pr", OFF_PR, PR_COLS, PR_COLS), ("zr", OFF_ZR, WIDTH, 512),
            ("gm", OFF_GM, 2 * D_MODEL, 1024))

    h, rs = _prenorm(x2, sm["pre_norm_gain"])
    proj = {nm: _mm(h, w_in[:, off:off + cnt], tn, "proj_" + nm) for nm, off, cnt, tn in segs}
    qkv3 = proj["qkv"].reshape(bsz, s, QKV_COLS)
    pr3 = proj["pr"].reshape(bsz, s, PR_COLS)

    o_attn, lse = _attn_fwd(qkv3, sm["rel_bias"], bidx)
    rk = sm["rwkv_r_k"].reshape(1, WIDTH)
    pre_args = (sm["rwkv_shift_mix"], sm["rwkv_w0"], wts["rwkv_w_up"], sm["rwkv_a0"], wts["rwkv_a_up"], sm["rwkv_k_k"], sm["rwkv_k_a"])
    scan_in = _rwkv_pre(pr3, *pre_args)
    o_rwkv, states, consts = _rwkv_scan(scan_in, rk, sm["rwkv_ln_w"], sm["rwkv_ln_b"])

    (dxo, do_attn, do_rwkv, dza, dzr, dgm, g_wua, g_wur, g_wout, g_post, loss) = _head(
        o_attn.reshape(n, WIDTH), o_rwkv.reshape(n, WIDTH), proj["za"], proj["zr"], proj["gm"], x2, tgt,
        wts["w_up_attn"], wts["w_up_rwkv"], wts["w_out"], sm["post_norm_gain"])

    dqkv, dbias = _attn_bwd(qkv3, o_attn, lse, do_attn.reshape(bsz, s, WIDTH), sm["rel_bias"], bidx)
    g_bias = _bias_grad(dbias, bidx)[:, :N_BUCKET].T

    scan_cots, (g_rk, g_lnw, g_lnb) = _rwkv_scan_bwd(scan_in, states, consts, do_rwkv.reshape(bsz, s, WIDTH), rk, sm["rwkv_ln_w"],
                                                     sm["rwkv_ln_b"])
    dprs, g_mix, g_w0, g_wup, g_a0, g_aup, g_kk, g_ka = _rwkv_pre_bwd(pr3, scan_cots, *pre_args)
    dpr = _shift_bwd(dprs, sm["rwkv_shift_mix"]).reshape(n, PR_COLS)

    dsegs = [(dqkv.reshape(9, n, WIDTH), 0, QKV_COLS, WIDTH), (dza, OFF_ZA, WIDTH, WIDTH), (dpr, OFF_PR, PR_COLS, PR_COLS),
             (dzr, OFF_ZR, WIDTH, WIDTH), (dgm, OFF_GM, 2 * D_MODEL, D_MODEL)]
    g_win = jnp.concatenate([_mm_tn(h, t, tn, "gw_in_%d" % j) for j, (t, _, _, tn) in enumerate(dsegs)], axis=1)
    blocks = _per_owner(g_win, True).astype(BF16)
    own = lax.dynamic_index_in_dim(blocks, 4 * lax.axis_index("x") + 2 * lax.axis_index("y") + lax.axis_index("c"), 0, keepdims=False)
    send_sems, recv_sems, blocks_thru, land_thru, token = _send_start(blocks)
    w_seg = lambda off, cnt: w_in[:, off:off + cnt] + token[0, 0]
    dh = _mm_nt_acc(dsegs[0][0], w_seg(0, QKV_COLS), None, "dh_qkv")
    dh = _mm_nt_multi([t for t, *_ in dsegs[1:]], [w_seg(off, cnt) for _, off, cnt, _ in dsegs[1:]], dh, "dh_rest")
    grad_x, g_pre = _prenorm_bwd(dh, x2, rs, sm["pre_norm_gain"], dxo)
    landed = _send_wait(send_sems, recv_sems, blocks_thru, land_thru, g_pre)

    full = {"w_up_attn": g_wua, "w_up_rwkv": g_wur, "w_out": g_wout, "rwkv_w_up": g_wup, "rwkv_a_up": g_aup}
    small = {"pre_norm_gain": g_pre, "rel_bias": g_bias, "rwkv_shift_mix": g_mix, "rwkv_w0": g_w0, "rwkv_a0": g_a0, "rwkv_k_k": g_kk,
             "rwkv_k_a": g_ka, "rwkv_r_k": g_rk, "rwkv_ln_w": g_lnw, "rwkv_ln_b": g_lnb, "post_norm_gain": g_post}
    return loss[0, 0], grad_x.reshape(bsz, s, d), (landed, own), full, small


def kernel(x, pre_norm_gain, w_in, rel_bias, rwkv_shift_mix, rwkv_w0, rwkv_w_up, rwkv_a0, rwkv_a_up, rwkv_k_k, rwkv_k_a, rwkv_r_k, rwkv_ln_w, rwkv_ln_b, w_up_attn, w_up_rwkv, w_out, post_norm_gain, loss_target, m_pre_norm_gain, m_w_in, m_rel_bias, m_rwkv_shift_mix, m_rwkv_w0, m_rwkv_w_up, m_rwkv_a0, m_rwkv_a_up, m_rwkv_k_k, m_rwkv_k_a, m_rwkv_r_k, m_rwkv_ln_w, m_rwkv_ln_b, m_w_up_attn, m_w_up_rwkv, m_w_out, m_post_norm_gain, v_pre_norm_gain, v_w_in, v_rel_bias, v_rwkv_shift_mix, v_rwkv_w0, v_rwkv_w_up, v_rwkv_a0, v_rwkv_a_up, v_rwkv_k_k, v_rwkv_k_a, v_rwkv_r_k, v_rwkv_ln_w, v_rwkv_ln_b, v_w_up_attn, v_w_up_rwkv, v_w_out, v_post_norm_gain):
    names = [n for n, *_ in SHARDED] + [n for n, _ in SMALL]
    loc = dict(locals())
    w = {n: loc[n] for n in names}
    m = {n: loc["m_" + n] for n in names}
    v = {n: loc["v_" + n] for n in names}
    shapes = {n: w[n].shape for n in names}
    order = ["pre_norm_gain", "w_in", "rel_bias", "rwkv_shift_mix", "rwkv_w0", "rwkv_w_up", "rwkv_a0", "rwkv_a_up", "rwkv_k_k", "rwkv_k_a",
             "rwkv_r_k", "rwkv_ln_w", "rwkv_ln_b", "w_up_attn", "w_up_rwkv", "w_out", "post_norm_gain"]
    shard2d = lambda t, n, r, c: t[n].reshape(r, c)

    gathered = _gather([shard2d(w, n, r, c).astype(BF16) for n, r, c, _, _ in SHARDED], "gather_weights")
    wts = {n: _whole(g, by_cols) for (n, _, _, by_cols, _), g in zip(SHARDED, gathered)}

    loss, grad_x, (win_landed, win_own), full, small = _local_step(x, loss_target, w, wts)
    rest = SHARDED[1:]
    parts = _exchange([_per_owner(full[n], by_cols).astype(BF16) for n, _, _, by_cols, _ in rest] + [_pack_small(small, loss)],
                      [False] * len(rest) + [True], "exchange_grads")

    outs = [{}, {}, {}, {}]
    for (n, r, c, _, tr), p in zip(SHARDED, [win_landed] + list(parts)):
        res = _adamw(p, shard2d(w, n, r, c), shard2d(m, n, r, c), shard2d(v, n, r, c), tr, "adamw_" + n,
                     own=win_own if n == "w_in" else None)
        for o, t in zip(outs, res):
            o[n] = t.reshape(shapes[n])
    res = _adamw(parts[-1], _pack_small(w), _pack_small(m), _pack_small(v), SMALL_ROWS, "adamw_small")
    for o, t in zip(outs, res):
        o.update(_unpack_small(t, shapes)[0])
    loss = _unpack_small(res[0], shapes)[1]
    return (loss, grad_x, *[o[n] for o in outs for n in order])
```

```python
import functools
import math

import numpy as np
import jax
import jax.numpy as jnp
from jax import lax
from jax.experimental import pallas as pl
from jax.experimental.pallas import tpu as pltpu

F32, BF16 = jnp.float32, jnp.bfloat16
SDS = jax.ShapeDtypeStruct
HI = lax.Precision.HIGHEST
MESH = pl.DeviceIdType.MESH

N_DEV = 8
D_MODEL = 1024
HEAD = 64
N_HEAD = 8
WIDTH = N_HEAD * HEAD
DILATIONS = (1, 4, 16)
QB = 128
N_BUCKET = 32
MAX_DIST = 2048
LORA = 64
QKV_COLS = 9 * WIDTH
PR_COLS = 3 * WIDTH + 2 * LORA
IN_COLS = QKV_COLS + WIDTH + PR_COLS + WIDTH + 2 * D_MODEL
OFF_ZA, OFF_PR, OFF_ZR, OFF_GM = QKV_COLS, QKV_COLS + WIDTH, QKV_COLS + WIDTH + PR_COLS, QKV_COLS + 2 * WIDTH + PR_COLS
RMS_EPS = 1e-6
GN_EPS = 64e-5
SCALE = 1.0 / math.sqrt(HEAD)
CHUNK = 64
CHUNK_GROUP = 8
BWD_GROUP = 16
EARLY = 8
NEG = -1e30
LANE = 128

ADAM_LR, ADAM_B1, ADAM_B2, ADAM_EPS, ADAM_WD, ADAM_STEP = 0.001, 0.9, 0.999, 1e-08, 0.01, 10

VMEM_LIMIT = 56 * 1024 * 1024

SMALL = (("pre_norm_gain", 1024), ("rel_bias", 768), ("rwkv_shift_mix", 1664), ("rwkv_w0", 512), ("rwkv_a0", 512),
         ("rwkv_k_k", 512), ("rwkv_k_a", 512), ("rwkv_r_k", 512), ("rwkv_ln_w", 512), ("rwkv_ln_b", 512),
         ("post_norm_gain", 1024))
SMALL_ROWS = 64


def _params(sem=None):
    return pltpu.CompilerParams(dimension_semantics=sem, vmem_limit_bytes=VMEM_LIMIT)


def _dot(a, b):
    return jnp.dot(a, b, preferred_element_type=F32)


def _dot_nt(a, b):
    return lax.dot_general(a, b, (((1,), (1,)), ((), ())), preferred_element_type=F32)


def _dot_tn(a, b):
    return lax.dot_general(a, b, (((0,), (0,)), ((), ())), preferred_element_type=F32)


@jax.custom_vjp
def _bdot(a, b):
    return _dot(a.astype(BF16), b.astype(BF16))


def _bdot_fwd(a, b):
    return _bdot(a, b), (a, b)


def _bdot_bwd(res, g):
    a, b = res
    gb = g.astype(BF16)
    return _dot_nt(gb, b.astype(BF16)), _dot_tn(a.astype(BF16), gb)


_bdot.defvjp(_bdot_fwd, _bdot_bwd)


def _silu(z):
    return z * jax.nn.sigmoid(z)


def _dsilu(z):
    s = jax.nn.sigmoid(z)
    return s * (1.0 + z * (1.0 - s))


def _softplus(x):
    return jnp.maximum(x, 0.0) + jnp.log(1.0 + jnp.exp(-jnp.abs(x)))


def _bucket_tables():
    qi = np.arange(QB)[:, None] + QB
    ki = np.arange(2 * QB)[None, :]
    rel = np.maximum(qi - ki, 0)
    out = []
    for d in DILATIONS:
        dist = rel * d
        max_exact = N_BUCKET // 2
        ratio = np.log(np.maximum(dist, 1).astype(np.float32) / max_exact) / np.float32(math.log(MAX_DIST / max_exact))
        large = max_exact + (ratio * (N_BUCKET - max_exact)).astype(np.int32)
        large = np.minimum(large, N_BUCKET - 1)
        out.append(np.where(dist < max_exact, dist, large).astype(np.int32))
    return np.stack(out)


def _prenorm(x2, g):
    n, d = x2.shape
    tm = 1024

    def body(x_ref, g_ref, h_ref, rs_ref):
        x = x_ref[...]
        rs = lax.rsqrt(jnp.mean(x * x, axis=-1, keepdims=True) + RMS_EPS)
        h_ref[...] = (x * rs * g_ref[...]).astype(BF16)
        rs_ref[...] = rs

    return pl.pallas_call(
        body, name="prenorm", grid=(n // tm,),
        in_specs=[pl.BlockSpec((tm, d), lambda i: (i, 0)), pl.BlockSpec((1, d), lambda i: (0, 0))],
        out_specs=[pl.BlockSpec((tm, d), lambda i: (i, 0)), pl.BlockSpec((tm, 1), lambda i: (i, 0))],
        out_shape=[SDS((n, d), BF16), SDS((n, 1), F32)], compiler_params=_params(("parallel",)))(x2, g)


def _mm(a, b, tn, name):
    m, k = a.shape
    n = b.shape[1]
    tm = 1024

    def body(a_ref, b_ref, o_ref):
        o_ref[...] = _dot(a_ref[...], b_ref[...])

    return pl.pallas_call(
        body, name=name, grid=(n // tn, m // tm),
        in_specs=[pl.BlockSpec((tm, k), lambda j, i: (i, 0)), pl.BlockSpec((k, tn), lambda j, i: (0, j))],
        out_specs=pl.BlockSpec((tm, tn), lambda j, i: (i, j)),
        out_shape=SDS((m, n), F32), compiler_params=_params(("parallel", "parallel")))(a, b)


def _mm_nt_acc(a, b, acc, name):
    split = a.ndim == 3
    m = a.shape[-2]
    k = b.shape[1]
    d = b.shape[0]
    tm = 1024
    per = 3 if split else 1
    seg = a.shape[2] if split else 0
    tk = per * seg if split else (k if k <= 2048 else 1536)
    have_acc = acc is not None

    def body(*refs):
        if have_acc:
            a_ref, b_ref, c_ref, o_ref = refs
        else:
            a_ref, b_ref, o_ref = refs
        if split:
            r = sum(_dot_nt(a_ref[j].astype(BF16), b_ref[:, seg * j:seg * (j + 1)]) for j in range(per))
        else:
            r = _dot_nt(a_ref[...].astype(BF16), b_ref[...])

        @pl.when(pl.program_id(1) == 0)
        def _():
            o_ref[...] = r + c_ref[...] if have_acc else r

        @pl.when(pl.program_id(1) != 0)
        def _():
            o_ref[...] += r

    a_spec = pl.BlockSpec((per, tm, seg), lambda i, j: (j, i, 0)) if split else pl.BlockSpec((tm, tk), lambda i, j: (i, j))
    in_specs = [a_spec, pl.BlockSpec((d, tk), lambda i, j: (0, j))]
    args = [a, b]
    if have_acc:
        in_specs.append(pl.BlockSpec((tm, d), lambda i, j: (i, 0)))
        args.append(acc)
    return pl.pallas_call(
        body, name=name, grid=(m // tm, k // tk), in_specs=in_specs, out_specs=pl.BlockSpec((tm, d), lambda i, j: (i, 0)),
        out_shape=SDS((m, d), F32), compiler_params=_params(("parallel", "arbitrary")))(*args)


def _mm_nt_multi(a_list, b_list, acc, name):
    m, d = acc.shape
    tm = 512
    n = len(a_list)

    def body(*refs):
        r = refs[2 * n][...]
        for a_ref, b_ref in zip(refs[:n], refs[n:2 * n]):
            r = r + _dot_nt(a_ref[...].astype(BF16), b_ref[...])
        refs[2 * n + 1][...] = r

    in_specs = [pl.BlockSpec((tm, a.shape[1]), lambda i: (i, 0)) for a in a_list]
    in_specs += [pl.BlockSpec(b.shape, lambda i: (0, 0)) for b in b_list]
    in_specs.append(pl.BlockSpec((tm, d), lambda i: (i, 0)))
    return pl.pallas_call(
        body, name=name, grid=(m // tm,), in_specs=in_specs, out_specs=pl.BlockSpec((tm, d), lambda i: (i, 0)),
        out_shape=SDS((m, d), F32), compiler_params=_params(("parallel",)))(*a_list, *b_list, acc)


def _mm_tn(a, b, tn, name):
    split = b.ndim == 3
    m, k1 = a.shape
    per = 3 if split else 1
    seg = b.shape[2] if split else tn
    tn = per * seg
    n2 = b.shape[0] * seg if split else b.shape[1]
    tm = 1024

    def body(a_ref, b_ref, o_ref):
        first = pl.program_id(1) == 0
        for j in range(per):
            r = _dot_tn(a_ref[...], (b_ref[j] if split else b_ref[...]).astype(BF16))
            cols = slice(seg * j, seg * (j + 1))

            @pl.when(first)
            def _(r=r, cols=cols):
                o_ref[:, cols] = r

            @pl.when(jnp.logical_not(first))
            def _(r=r, cols=cols):
                o_ref[:, cols] += r

    b_spec = pl.BlockSpec((per, tm, seg), lambda j, i: (j, i, 0)) if split else pl.BlockSpec((tm, tn), lambda j, i: (i, j))
    return pl.pallas_call(
        body, name=name, grid=(n2 // tn, m // tm),
        in_specs=[pl.BlockSpec((tm, k1), lambda j, i: (i, 0)), b_spec],
        out_specs=pl.BlockSpec((k1, tn), lambda j, i: (0, j)),
        out_shape=SDS((k1, n2), F32), compiler_params=_params(("parallel", "arbitrary")))(a, b)


def _ds(start, d):
    return pl.ds(start, QB) if d == 1 else pl.ds(start, QB, stride=d)


def _fill_bias(tab_ref, bidx_ref, bias_sc, hp):
    for g in range(3):
        bi = bidx_ref[g]
        for h in range(2):
            acc = jnp.zeros((QB, 2 * QB), F32)
            for j in range(N_BUCKET):
                acc = jnp.where(bi == j, tab_ref[j, g * N_HEAD + hp * 2 + h], acc)
            bias_sc[g * 2 + h] = acc


def _block_starts(it, d, nb):
    rho = it // nb
    n = it % nb
    st = rho + d * QB * n
    stp = rho + d * QB * jnp.maximum(n - 1, 0)
    if d == 1:
        st, stp = pl.multiple_of(QB * it, QB), pl.multiple_of(QB * jnp.maximum(it - 1, 0), QB)
    return st, stp, n > 0


ATTN_BLOCKS = 4


def _bdot3(a, b, dims):
    return lax.dot_general(a, b, (dims, ((0,), (0,))), preferred_element_type=F32)


def _attn_operands(q_ref, k_ref, v_ref, bias_sc, g, d, nb, it0):
    two = nb > 1
    nk = 2 * QB if two else QB
    ii = lax.broadcasted_iota(jnp.int32, (QB, nk), 0)
    cc = lax.broadcasted_iota(jnp.int32, (QB, nk), 1)
    qs, ks, vs, pens, starts = [], [], [], [], []
    for u in range(ATTN_BLOCKS):
        st, stp, hasprev = _block_starts(it0 + u, d, nb)
        qf = q_ref[0, _ds(st, d), :]
        if two:
            kf = jnp.concatenate([k_ref[0, _ds(stp, d), :], k_ref[0, _ds(st, d), :]], axis=0).astype(BF16)
            vf = jnp.concatenate([v_ref[0, _ds(stp, d), :], v_ref[0, _ds(st, d), :]], axis=0).astype(BF16)
            own = jnp.logical_and(cc >= QB, ii >= cc - QB)
            prev = jnp.logical_and(jnp.logical_and(cc < QB, cc >= ii), hasprev)
            pen = jnp.where(jnp.logical_or(own, prev), 0.0, NEG)
        else:
            kf, vf = k_ref[0, _ds(st, d), :].astype(BF16), v_ref[0, _ds(st, d), :].astype(BF16)
            pen = jnp.where(ii >= cc, 0.0, NEG)
        for h in range(2):
            qs.append(_one_head(qf, h).astype(BF16))
            ks.append(kf)
            vs.append(vf)
            pens.append(pen + (bias_sc[g * 2 + h] if two else bias_sc[g * 2 + h, :, QB:2 * QB]))
        starts.append((st, stp))
    return _stack(qs), _stack(ks), _stack(vs), _stack(pens), starts


def _one_head(x, h):
    lane = lax.broadcasted_iota(jnp.int32, x.shape, 1)
    return jnp.where(lane >= HEAD if h == 1 else lane < HEAD, x, 0.0)


def _pick_heads(x, u):
    lane = lax.broadcasted_iota(jnp.int32, x.shape[1:], 1)
    return jnp.where(lane < HEAD, x[2 * u], x[2 * u + 1])


def _add_heads(x, u):
    return x[2 * u] + x[2 * u + 1]


def _attn_fwd(qkv3, rel_bias, bidx):
    bsz, s, _ = qkv3.shape
    rt = 256

    def body(tab_ref, bidx_ref, *refs):
        q_refs, k_refs, v_refs = refs[0:3], refs[3:6], refs[6:9]
        o_ref, lse_ref = refs[9:11]
        bias_sc, num_sc, den_sc, m_sc = refs[11:]
        pl.when(pl.program_id(1) == 0)(lambda: _fill_bias(tab_ref, bidx_ref, bias_sc, pl.program_id(0)))
        for g, d in enumerate(DILATIONS):
            nb = s // (QB * d)

            def blk(it, c, g=g, d=d, nb=nb):
                q, k, v, bias, starts = _attn_operands(q_refs[g], k_refs[g], v_refs[g], bias_sc, g, d, nb, it * ATTN_BLOCKS)
                sc = _bdot3(q, k, ((2,), (2,))) * SCALE + bias
                m = jnp.max(sc, axis=-1, keepdims=True)
                p = jnp.exp(sc - m)
                den = jnp.sum(p, axis=-1, keepdims=True)
                num = _bdot3(p.astype(BF16), v, ((2,), (1,)))
                den, m = jnp.broadcast_to(den, num.shape), jnp.broadcast_to(m, num.shape)
                for u, (st, _) in enumerate(starts):
                    num_sc[g, _ds(st, d), :] = _pick_heads(num, u)
                    den_sc[g, _ds(st, d), :] = _pick_heads(den, u)
                    m_sc[g, _ds(st, d), :] = _pick_heads(m, u)
                return c

            lax.fori_loop(0, s // QB // ATTN_BLOCKS, blk, 0)

        def merge(i, c):
            rows = pl.ds(pl.multiple_of(i * rt, rt), rt)
            m0, m1, m2 = m_sc[0, rows, :], m_sc[1, rows, :], m_sc[2, rows, :]
            mall = jnp.maximum(jnp.maximum(m0, m1), m2)
            w0, w1, w2 = jnp.exp(m0 - mall), jnp.exp(m1 - mall), jnp.exp(m2 - mall)
            num = w0 * num_sc[0, rows, :] + w1 * num_sc[1, rows, :] + w2 * num_sc[2, rows, :]
            den = w0 * den_sc[0, rows, :] + w1 * den_sc[1, rows, :] + w2 * den_sc[2, rows, :]
            o_ref[0, rows, :] = num / den
            lse_ref[0, rows, :] = mall + jnp.log(den)
            return c

        lax.fori_loop(0, s // rt, merge, 0)

    col = lambda w, g: (lambda hp, b: (b, 0, (w * 3 + g) * 4 + hp))
    in_specs = [pl.BlockSpec(memory_space=pltpu.SMEM), pl.BlockSpec((3, QB, 2 * QB), lambda hp, b: (0, 0, 0))]
    in_specs += [pl.BlockSpec((1, s, LANE), col(w, g)) for w in range(3) for g in range(3)]
    out_spec = pl.BlockSpec((1, s, LANE), lambda hp, b: (b, 0, hp))
    return pl.pallas_call(
        body, name="attn_fwd", grid=(4, bsz), in_specs=in_specs, out_specs=[out_spec, out_spec],
        out_shape=[SDS((bsz, s, WIDTH), F32), SDS((bsz, s, WIDTH), F32)],
        scratch_shapes=[pltpu.VMEM((6, QB, 2 * QB), F32), pltpu.VMEM((3, s, LANE), F32), pltpu.VMEM((3, s, LANE), F32),
                        pltpu.VMEM((3, s, LANE), F32)],
        compiler_params=_params(("arbitrary", "arbitrary")))(rel_bias, bidx, *([qkv3] * 9))


def _attn_bwd(qkv3, o3, lse3, do3, rel_bias, bidx):
    bsz, s, _ = qkv3.shape
    rt = 256

    def body(tab_ref, bidx_ref, *refs):
        q_refs, k_refs, v_refs = refs[0:3], refs[3:6], refs[6:9]
        o_ref, lse_ref, do_ref, dqkv_ref, db_ref, bias_sc, delta_sc, acc_sc = refs[9:]
        dq_refs, dk_refs, dv_refs = ([acc_sc.at[w * 3 + g] for g in range(3)] for w in range(3))

        @pl.when(pl.program_id(1) == 0)
        def _():
            _fill_bias(tab_ref, bidx_ref, bias_sc, pl.program_id(0))
            db_ref[...] = jnp.zeros_like(db_ref)

        def prep(i, c):
            rows = pl.ds(pl.multiple_of(i * rt, rt), rt)
            prod = do_ref[0, rows, :] * o_ref[0, rows, :]
            d0 = jnp.sum(prod[:, :HEAD], axis=-1, keepdims=True)
            d1 = jnp.sum(prod[:, HEAD:], axis=-1, keepdims=True)
            delta_sc[rows, :] = jnp.concatenate([jnp.broadcast_to(d0, (rt, HEAD)), jnp.broadcast_to(d1, (rt, HEAD))], axis=1)
            z = jnp.zeros((rt, LANE), F32)
            for g in range(3):
                dk_refs[g][0, rows, :] = z
                dv_refs[g][0, rows, :] = z
            return c

        lax.fori_loop(0, s // rt, prep, 0)
        for g, d in enumerate(DILATIONS):
            nb = s // (QB * d)

            def blk(it, c, g=g, d=d, nb=nb):
                q, k, v, bias, starts = _attn_operands(q_refs[g], k_refs[g], v_refs[g], bias_sc, g, d, nb, it * ATTN_BLOCKS)
                dos, lses, deltas = [], [], []
                for st, _ in starts:
                    dof, lsef, delf = do_ref[0, _ds(st, d), :], lse_ref[0, _ds(st, d), :], delta_sc[_ds(st, d), :]
                    for h in range(2):
                        dos.append(_one_head(dof, h).astype(BF16))
                        lses.append(lsef[:, HEAD * h:HEAD * h + 1])
                        deltas.append(delf[:, HEAD * h:HEAD * h + 1])
                do, lse, delta = _stack(dos), _stack(lses), _stack(deltas)
                p = jnp.exp(_bdot3(q, k, ((2,), (2,))) * SCALE + bias - lse)
                dv = _bdot3(p.astype(BF16), do, ((1,), (1,)))
                ds = p * (_bdot3(do, v, ((2,), (2,))) - delta)
                dsb = ds.astype(BF16)
                dq = _bdot3(dsb, k, ((2,), (1,))) * SCALE
                dk = _bdot3(dsb, q, ((1,), (1,))) * SCALE
                two = nb > 1
                for h in range(2):
                    dsum = sum(ds[2 * u + h] for u in range(ATTN_BLOCKS))
                    if two:
                        db_ref[0, g * 2 + h] += dsum
                    else:
                        db_ref[0, g * 2 + h, :, QB:2 * QB] += dsum
                for u, (st, stp) in enumerate(starts):
                    dq_refs[g][0, _ds(st, d), :] = _pick_heads(dq, u)
                    if two:
                        dk_refs[g][0, _ds(stp, d), :] += _add_heads(dk[:, :QB], u)
                        dv_refs[g][0, _ds(stp, d), :] += _add_heads(dv[:, :QB], u)
                    dk_refs[g][0, _ds(st, d), :] += _add_heads(dk[:, QB:] if two else dk, u)
                    dv_refs[g][0, _ds(st, d), :] += _add_heads(dv[:, QB:] if two else dv, u)
                return c

            lax.fori_loop(0, s // QB // ATTN_BLOCKS, blk, 0)

        def flush(i, c):
            rows = pl.ds(pl.multiple_of(i * rt, rt), rt)
            for j in range(9):
                dqkv_ref[j, 0, rows, :] = acc_sc[j, 0, rows, :].astype(BF16)
            return c

        lax.fori_loop(0, s // rt, flush, 0)

    col = lambda w, g: (lambda hp, b: (b, 0, (w * 3 + g) * 4 + hp))
    blk_spec = pl.BlockSpec((1, s, LANE), lambda hp, b: (b, 0, hp))
    in_specs = [pl.BlockSpec(memory_space=pltpu.SMEM), pl.BlockSpec((3, QB, 2 * QB), lambda hp, b: (0, 0, 0))]
    in_specs += [pl.BlockSpec((1, s, LANE), col(w, g)) for w in range(3) for g in range(3)]
    in_specs += [blk_spec] * 3
    out_specs = [pl.BlockSpec((9, 1, s, LANE), lambda hp, b: (0, b, 0, hp)), pl.BlockSpec((1, 6, QB, 2 * QB), lambda hp, b: (hp, 0, 0, 0))]
    out_shape = [SDS((9, bsz, s, WIDTH), BF16), SDS((4, 6, QB, 2 * QB), F32)]
    return pl.pallas_call(
        body, name="attn_bwd", grid=(4, bsz), in_specs=in_specs, out_specs=out_specs, out_shape=out_shape,
        scratch_shapes=[pltpu.VMEM((6, QB, 2 * QB), F32), pltpu.VMEM((s, LANE), F32), pltpu.VMEM((9, 1, s, LANE), F32)],
        compiler_params=_params(("parallel", "arbitrary")))(rel_bias, bidx, *([qkv3] * 9), o3, lse3, do3)


def _bias_grad(dbias, bidx):
    def body(db_ref, bidx_ref, o_ref):
        lane = lax.broadcasted_iota(jnp.int32, (1, LANE), 1)
        for g in range(3):
            bi = bidx_ref[g]
            for hp in range(4):
                for h in range(2):
                    mat = db_ref[hp, g * 2 + h]
                    row = jnp.zeros((1, LANE), F32)
                    for j in range(N_BUCKET):
                        part = jnp.sum(jnp.where(bi == j, mat, 0.0), axis=0, keepdims=True)
                        row = jnp.where(lane == j, jnp.sum(part, axis=1, keepdims=True), row)
                    hd = g * N_HEAD + hp * 2 + h
                    o_ref[hd:hd + 1, :] = row

    return pl.pallas_call(body, name="bias_grad", out_shape=SDS((3 * N_HEAD, LANE), F32), compiler_params=_params())(dbias, bidx)


def _pre_fn(r, k0, v, wl, al, w0, wup, a0, aup, kk_, ka_):
    u = w0 + _bdot(jnp.tanh(wl), wup)
    lw = -jnp.exp(-_softplus(-u) - 0.5)
    a = jax.nn.sigmoid(a0 + _bdot(al, aup))
    kkraw = k0 * kk_
    k = k0 * (1.0 + (a - 1.0) * ka_)
    return r, lw, k, v, kkraw, a


PRE_SPLIT = (0, WIDTH, 2 * WIDTH, 3 * WIDTH, 3 * WIDTH + LORA, 3 * WIDTH + 2 * LORA)


def _pre_pieces(prs):
    return [prs[:, a:b] for a, b in zip(PRE_SPLIT[:-1], PRE_SPLIT[1:])]


PRE_TT = 512


def _shifted(pr_ref, edge_ref, first, back):
    pr = pr_ref[0]
    tt = pr.shape[0]
    row = lax.broadcasted_iota(jnp.int32, (tt, 1), 0)
    if back:
        edge = jnp.where(first, 0.0, edge_ref[0, 7:8, :])
        return jnp.where(row == 0, edge, pltpu.roll(pr, 1, axis=0))
    edge = jnp.where(first, 0.0, edge_ref[0, 0:1, :])
    return jnp.where(row == tt - 1, edge, pltpu.roll(pr, tt - 1, axis=0))


def _rwkv_pre(pr3, mix, w0, wup, a0, aup, kk_, ka_):
    bsz, s, _ = pr3.shape
    tt = PRE_TT

    def body(pr_ref, edge_ref, mix_ref, w0_ref, wup_ref, a0_ref, aup_ref, kk_ref, ka_ref, *outs):
        pr = pr_ref[0]
        prev = _shifted(pr_ref, edge_ref, pl.program_id(1) == 0, True)
        prs = pr + (prev - pr) * mix_ref[...]
        vals = _pre_fn(*_pre_pieces(prs), w0_ref[...], wup_ref[...].astype(F32), a0_ref[...], aup_ref[...].astype(F32), kk_ref[...],
                       ka_ref[...])
        for o, val in zip(outs, vals):
            o[0] = val

    vec = lambda n: pl.BlockSpec((1, n), lambda b, i: (0, 0))
    mat = pl.BlockSpec((LORA, WIDTH), lambda b, i: (0, 0))
    in_specs = [pl.BlockSpec((1, tt, PR_COLS), lambda b, i: (b, i, 0)),
                pl.BlockSpec((1, 8, PR_COLS), lambda b, i: (b, jnp.maximum(i * (tt // 8) - 1, 0), 0)),
                vec(PR_COLS), vec(WIDTH), mat, vec(WIDTH), mat, vec(WIDTH), vec(WIDTH)]
    out_spec = pl.BlockSpec((1, tt, WIDTH), lambda b, i: (b, i, 0))
    return pl.pallas_call(
        body, name="rwkv_pre", grid=(bsz, s // tt), in_specs=in_specs, out_specs=[out_spec] * 6,
        out_shape=[SDS((bsz, s, WIDTH), F32)] * 6, compiler_params=_params(("parallel", "parallel")))(
            pr3, pr3, mix, w0, wup, a0, aup, kk_, ka_)


def _rwkv_pre_bwd(pr3, cots, mix, w0, wup, a0, aup, kk_, ka_):
    bsz, s, _ = pr3.shape
    tt = PRE_TT

    def body(pr_ref, edge_ref, c0, c1, c2, c3, c4, c5, mix_ref, w0_ref, wup_ref, a0_ref, aup_ref, kk_ref, ka_ref,
             dprs_ref, dmix_ref, dw0_ref, dwup_ref, da0_ref, daup_ref, dkk_ref, dka_ref):
        pr = pr_ref[0]
        prev = _shifted(pr_ref, edge_ref, pl.program_id(1) == 0, True)
        prs = pr + (prev - pr) * mix_ref[...]
        _, vjp = jax.vjp(_pre_fn, *_pre_pieces(prs), w0_ref[...], wup_ref[...].astype(F32), a0_ref[...], aup_ref[...].astype(F32),
                         kk_ref[...], ka_ref[...])
        grads = vjp(tuple(c[0] for c in (c0, c1, c2, c3, c4, c5)))
        for piece, a, b in zip(grads[:5], PRE_SPLIT[:-1], PRE_SPLIT[1:]):
            dprs_ref[0, :, a:b] = piece
        dw0, dwup, da0, daup, dkk, dka = grads[5:]
        dprs = dprs_ref[0]
        grads = (jnp.sum(dprs * (prev - pr), axis=0, keepdims=True), dw0, dwup, da0, daup, dkk, dka)
        refs = (dmix_ref, dw0_ref, dwup_ref, da0_ref, daup_ref, dkk_ref, dka_ref)
        first = jnp.logical_and(pl.program_id(0) == 0, pl.program_id(1) == 0)

        @pl.when(first)
        def _():
            for r_, g_ in zip(refs, grads):
                r_[...] = g_

        @pl.when(jnp.logical_not(first))
        def _():
            for r_, g_ in zip(refs, grads):
                r_[...] += g_

    vec = lambda n: pl.BlockSpec((1, n), lambda b, i: (0, 0))
    mat = pl.BlockSpec((LORA, WIDTH), lambda b, i: (0, 0))
    tile = pl.BlockSpec((1, tt, WIDTH), lambda b, i: (b, i, 0))
    in_specs = [pl.BlockSpec((1, tt, PR_COLS), lambda b, i: (b, i, 0)),
                pl.BlockSpec((1, 8, PR_COLS), lambda b, i: (b, jnp.maximum(i * (tt // 8) - 1, 0), 0))]
    in_specs += [tile] * 6 + [vec(PR_COLS), vec(WIDTH), mat, vec(WIDTH), mat, vec(WIDTH), vec(WIDTH)]
    out_specs = [pl.BlockSpec((1, tt, PR_COLS), lambda b, i: (b, i, 0)), vec(PR_COLS), vec(WIDTH), mat, vec(WIDTH), mat,
                 vec(WIDTH), vec(WIDTH)]
    out_shape = [SDS((bsz, s, PR_COLS), F32), SDS((1, PR_COLS), F32), SDS((1, WIDTH), F32), SDS((LORA, WIDTH), F32),
                 SDS((1, WIDTH), F32), SDS((LORA, WIDTH), F32), SDS((1, WIDTH), F32), SDS((1, WIDTH), F32)]
    return pl.pallas_call(
        body, name="rwkv_pre_bwd", grid=(bsz, s // tt), in_specs=in_specs, out_specs=out_specs, out_shape=out_shape,
        compiler_params=_params(("arbitrary", "arbitrary")))(pr3, pr3, *cots, mix, w0, wup, a0, aup, kk_, ka_)


def _shift_bwd(dprs3, mix):
    bsz, s, _ = dprs3.shape
    tt = PRE_TT
    nt = s // tt

    def body(d_ref, edge_ref, mix_ref, o_ref):
        nxt = _shifted(d_ref, edge_ref, pl.program_id(1) == nt - 1, False)
        m = mix_ref[...]
        o_ref[0] = (d_ref[0] * (1.0 - m) + nxt * m).astype(BF16)

    in_specs = [pl.BlockSpec((1, tt, PR_COLS), lambda b, i: (b, i, 0)),
                pl.BlockSpec((1, 8, PR_COLS), lambda b, i: (b, jnp.minimum((i + 1) * (tt // 8), s // 8 - 1), 0)),
                pl.BlockSpec((1, PR_COLS), lambda b, i: (0, 0))]
    return pl.pallas_call(
        body, name="shift_bwd", grid=(bsz, nt), in_specs=in_specs, out_specs=pl.BlockSpec((1, tt, PR_COLS), lambda b, i: (b, i, 0)),
        out_shape=SDS((bsz, s, PR_COLS), BF16), compiler_params=_params(("parallel", "parallel")))(dprs3, dprs3, mix)


_NN, _NT, _TN = ((2,), (1,)), ((2,), (2,)), ((1,), (1,))


def _dot3_bf16(a, b, dims):
    return lax.dot_general(a.astype(BF16), b.astype(BF16), (dims, ((0,), (0,))), preferred_element_type=F32)


class _Dots:
    def __init__(self, fwd):
        def make(dims, da_rule, db_rule):
            @jax.custom_vjp
            def f(a, b):
                return fwd(a, b, dims)

            f.defvjp(lambda a, b: (f(a, b), (a, b)), lambda res, g: (da_rule(*res, g), db_rule(*res, g)))
            return f

        one = _dot3_bf16
        self.mm = make(_NN, lambda a, b, g: one(g, b, _NT), lambda a, b, g: one(a, g, _TN))
        self.mm_nt = make(_NT, lambda a, b, g: one(g, b, _NN), lambda a, b, g: one(g, a, _TN))
        self.mm_tn = make(_TN, lambda a, b, g: one(b, g, _NT), lambda a, b, g: one(a, g, _NN))

        def powers(aab):
            ps = [aab]
            while 2 ** len(ps) < aab.shape[1]:
                ps.append(fwd(ps[-1], ps[-1], _NN))
            return ps

        def apply(ps, z, dims):
            for p in ps:
                z = z + fwd(p, z, dims)
            return z

        @jax.custom_vjp
        def solve(aab, z):
            return apply(powers(aab), z, _NN)

        def solve_fwd(aab, z):
            ps = powers(aab)
            x = apply(ps, z, _NN)
            return x, (ps, x)

        def solve_bwd(res, g):
            ps, x = res
            dz = apply(ps, g, _TN)
            return fwd(dz, x, _NT), dz

        solve.defvjp(solve_fwd, solve_bwd)
        self.solve = solve


_ONE_PASS = _Dots(_dot3_bf16)
_bmm, _bmm_tn = _ONE_PASS.mm, _ONE_PASS.mm_tn


def _chunk_fn(s0t, r, lw, k, v, kkraw, a, rk, lnw, lnb, first=False, d=_ONE_PASS):
    c = r.shape[1]
    at, rt, btc, ktc, gc, aab, arb, xv, arkv, ain, bin_ = _chunk_core(r, lw, k, v, kkraw, a, d)
    rs = d.mm(jnp.concatenate([at, rt], axis=1), s0t)
    u = d.solve(aab, rs[:, :c] + xv)
    y = rs[:, c:] + d.mm(arb, u) + arkv
    if first:
        y = _with_early_rows(y, r, lw, k, v, ain, bin_)
    gcol = jnp.sum(_diag(gc), axis=2, keepdims=True)
    sct = gcol * s0t + d.mm_tn(jnp.concatenate([btc, ktc], axis=1), jnp.concatenate([u, v], axis=1))
    return _post(y, r, k, v, rk, lnw, lnb), sct


def _diag(gc):
    return jnp.where(_masks(HEAD)[2], gc, 0.0)


def _with_early_rows(y, r, lw, k, v, ain, bin_):
    early = _early_rows(r[:2], lw[:2], k[:2], v[:2], ain[:2], bin_[:2])
    return jnp.concatenate([jnp.concatenate([early, y[:2, EARLY:]], axis=1), y[2:]], axis=0)


def _early_rows(r, lw, k, v, ain, bin_):
    cols = lambda x: _stack([jnp.transpose(x[h]) for h in range(2)])
    wc, bc, kc = cols(jnp.exp(lw)), cols(bin_), cols(k)
    st = jnp.zeros((2, HEAD, HEAD), F32)
    rows = []
    for t in range(EARLY):
        sa = _ONE_PASS.mm(ain[:, t:t + 1], st)
        st = st * wc[:, :, t:t + 1] + bc[:, :, t:t + 1] * sa + kc[:, :, t:t + 1] * v[:, t:t + 1]
        rows.append(_ONE_PASS.mm(r[:, t:t + 1], st))
    return jnp.concatenate(rows, axis=1)


def _chunk_rows(c):
    return pl.ds(c * CHUNK, CHUNK) if isinstance(c, int) else pl.ds(pl.multiple_of(c * CHUNK, CHUNK), CHUNK)


def _stack(xs):
    return jnp.concatenate([x[None] for x in xs], axis=0)


def _pairs(ref, chunks):
    tiles = [ref[0, _chunk_rows(c), :] for c in chunks]
    return _stack([t[:, HEAD * h:HEAD * h + HEAD] for t in tiles for h in range(2)])


def _unpair(vals, j):
    return jnp.concatenate([vals[2 * j], vals[2 * j + 1]], axis=1)


def _masks(c):
    ii = lax.broadcasted_iota(jnp.int32, (c, c), 0)
    jj = lax.broadcasted_iota(jnp.int32, (c, c), 1)
    return ii > jj, ii >= jj, ii == jj


@jax.custom_vjp
def _running_sum(lw):
    return _tri_dot(lw, _NN)


def _tri_dot(x, dims):
    g_, c, _ = x.shape
    tri = jnp.broadcast_to(_masks(c)[1].astype(BF16), (g_, c, c))
    head = x.astype(BF16)
    rest = (x - head.astype(F32)).astype(BF16)
    return lax.dot_general(tri, head, (dims, ((0,), (0,))), preferred_element_type=F32) + \
        lax.dot_general(tri, rest, (dims, ((0,), (0,))), preferred_element_type=F32)


_running_sum.defvjp(lambda lw: (_running_sum(lw), None), lambda _, ct: (_tri_dot(ct, _TN),))


def _chunk_core(r, lw, k, v, kkraw, a, d=_ONE_PASS):
    g_, c = r.shape[0], r.shape[1]
    nrm = jnp.sqrt(jnp.sum(kkraw * kkraw, axis=-1, keepdims=True))
    kkn = kkraw / jnp.maximum(nrm, 1e-12)
    ain, bin_ = -kkn, kkn * a
    strict, incl, _ = _masks(c)
    lg = _running_sum(lw)
    g, gp, gi = jnp.exp(lg), jnp.exp(lg - lw), jnp.exp(-lg)
    at, rt, bt, kt = ain * gp, r * g, bin_ * gi, k * gi
    aa = d.mm_nt(jnp.concatenate([at, rt], axis=1), jnp.concatenate([bt, kt], axis=1))
    aab = jnp.where(strict, aa[:, :c, :c], 0.0)
    aak = jnp.where(strict, aa[:, :c, c:], 0.0)
    arb = jnp.where(incl, aa[:, c:, :c], 0.0)
    ark = jnp.where(incl, aa[:, c:, c:], 0.0)
    akv = d.mm(jnp.concatenate([aak, ark], axis=1), v)
    gc = g[:, c - 1:c, :]
    return at, rt, bt * gc, kt * gc, gc, aab, arb, akv[:, :c], akv[:, c:], ain, bin_


def _post(y, r, k, v, rk, lnw, lnb):
    mu = jnp.mean(y, axis=-1, keepdims=True)
    var = jnp.mean(jnp.square(y - mu), axis=-1, keepdims=True)
    yn = (y - mu) * lax.rsqrt(var + GN_EPS) * lnw + lnb
    return yn + jnp.sum(r * k * rk, axis=-1, keepdims=True) * v


def _chunk_consts(r, lw, k, v, kkraw, a, first=False):
    d = _ONE_PASS
    at, rt, btc, ktc, gc, aab, arb, xv, arkv, ain, bin_ = _chunk_core(r, lw, k, v, kkraw, a, d)
    z = d.solve(aab, jnp.concatenate([at, xv], axis=2))
    ryv = jnp.concatenate([rt, arkv], axis=2) + d.mm(arb, z)
    if first:
        ryv = jnp.concatenate([ryv[:, :, :HEAD], _with_early_rows(ryv[:, :, HEAD:], r, lw, k, v, ain, bin_)], axis=2)
    mkv = d.mm_tn(btc, z) + jnp.concatenate([_diag(gc), d.mm_tn(ktc, v)], axis=2)
    return mkv, ryv


def _rwkv_scan(ins, rk, lnw, lnb):
    bsz, s, _ = ins[0].shape
    nch = s // CHUNK

    def consts_body(r_ref, lw_ref, k_ref, v_ref, kk_ref, a_ref, mkv_ref, ry_ref, yv_ref):
        def group(i, carry):
            chunks = [i * CHUNK_GROUP + j for j in range(CHUNK_GROUP)]
            mkv, ryv = _chunk_consts(*[_pairs(ref, chunks) for ref in (r_ref, lw_ref, k_ref, v_ref, kk_ref, a_ref)],
                                     first=isinstance(i, int) and i == 0)
            for j, c in enumerate(chunks):
                for h in range(2):
                    mkv_ref[0, 0, c, h] = mkv[2 * j + h]
                ry_ref[0, _chunk_rows(c), :] = jnp.concatenate([ryv[2 * j][:, :HEAD], ryv[2 * j + 1][:, :HEAD]], axis=1)
                yv_ref[0, _chunk_rows(c), :] = jnp.concatenate([ryv[2 * j][:, HEAD:], ryv[2 * j + 1][:, HEAD:]], axis=1)
            return carry

        group(0, 0)
        lax.fori_loop(1, nch // CHUNK_GROUP, group, 0)

    tile = pl.BlockSpec((1, s, LANE), lambda b, hp: (b, 0, hp))
    vec = pl.BlockSpec((1, LANE), lambda b, hp: (0, hp))
    mkv_spec = pl.BlockSpec((1, 1, nch, 2, HEAD, LANE), lambda b, hp: (b, hp, 0, 0, 0, 0))
    st_spec = pl.BlockSpec((1, 1, nch, 2, HEAD, HEAD), lambda b, hp: (b, hp, 0, 0, 0, 0))
    mkv, ry, yv = pl.pallas_call(
        consts_body, name="rwkv_consts", grid=(bsz, 4), in_specs=[tile] * 6, out_specs=[mkv_spec, tile, tile],
        out_shape=[SDS((bsz, 4, nch, 2, HEAD, LANE), F32), SDS((bsz, s, WIDTH), F32), SDS((bsz, s, WIDTH), F32)],
        compiler_params=_params(("parallel", "parallel")))(*ins)

    states = _chunk_recurrence(mkv, None, "rwkv_states")

    def out_body(ry_ref, yv_ref, r_ref, k_ref, v_ref, st_ref, rk_ref, lnw_ref, lnb_ref, o_ref):
        y, r, k, v, rk_, lnw_, lnb_ = _scan_rows(ry_ref, yv_ref, r_ref, k_ref, v_ref, st_ref, rk_ref, lnw_ref, lnb_ref)
        o = _post(y, r, k, v, rk_, lnw_, lnb_)
        for j in range(CHUNK_GROUP):
            o_ref[0, _chunk_rows(j), :] = _unpair(o, j)

    o = pl.pallas_call(
        out_body, name="rwkv_out", grid=(bsz, 4, nch // CHUNK_GROUP), in_specs=_group_specs(5), out_specs=_group_specs(1)[0],
        out_shape=SDS((bsz, s, WIDTH), F32),
        compiler_params=_params(("parallel", "parallel", "parallel")))(ry, yv, ins[0], ins[2], ins[3], states, rk, lnw, lnb)
    return o, states, (mkv, ry, yv)


def _group_specs(n_tiles):
    tile = pl.BlockSpec((1, CHUNK_GROUP * CHUNK, LANE), lambda b, hp, t: (b, t, hp))
    if n_tiles == 1:
        return [tile]
    st = pl.BlockSpec((1, 1, CHUNK_GROUP, 2, HEAD, HEAD), lambda b, hp, t: (b, hp, t, 0, 0, 0))
    vec = pl.BlockSpec((1, LANE), lambda b, hp, t: (0, hp))
    return [tile] * n_tiles + [st] + [vec] * 3


def _scan_rows(ry_ref, yv_ref, r_ref, k_ref, v_ref, st_ref, rk_ref, lnw_ref, lnb_ref):
    chunks = list(range(CHUNK_GROUP))
    ry, yv, r, k, v = (_pairs(ref, chunks) for ref in (ry_ref, yv_ref, r_ref, k_ref, v_ref))
    st = _stack([st_ref[0, 0, c, h] for c in chunks for h in range(2)])
    vecs = [_stack([ref[:, HEAD * h:HEAD * h + HEAD] for _ in chunks for h in range(2)]) for ref in (rk_ref, lnw_ref, lnb_ref)]
    return (_bmm(ry, st) + yv, r, k, v, *vecs)


def _chunk_recurrence(mkv, q, name):
    bsz, _, nch = mkv.shape[:3]
    pairs = [(hp, h) for hp in range(4) for h in range(2)]

    def body(*refs):
        mkv_ref, out_ref, acc = refs[0], refs[-2], refs[-1]
        acc[...] = jnp.zeros_like(acc)

        def step(i, carry):
            c = i if q is None else nch - 1 - i
            cur = acc[...]
            for j, (hp, h) in enumerate(pairs):
                out_ref[0, hp, c, h] = cur[j]
            m = _stack([mkv_ref[0, hp, c, h] for hp, h in pairs])
            if q is None:
                acc[...] = _bmm(m[:, :, :HEAD], cur) + m[:, :, HEAD:]
            else:
                acc[...] = _bmm_tn(m[:, :, :HEAD], cur) + _stack([refs[1][0, hp, c, h] for hp, h in pairs])
            return carry

        lax.fori_loop(0, nch, step, 0)

    spec = lambda w: pl.BlockSpec((1, 4, nch, 2, HEAD, w), lambda b: (b, 0, 0, 0, 0, 0))
    return pl.pallas_call(
        body, name=name, grid=(bsz,), in_specs=[spec(LANE)] + ([] if q is None else [spec(HEAD)]), out_specs=spec(HEAD),
        out_shape=SDS((bsz, 4, nch, 2, HEAD, HEAD), F32), scratch_shapes=[pltpu.VMEM((8, HEAD, HEAD), F32)],
        compiler_params=_params(("parallel",)))(*([mkv] if q is None else [mkv, q]))


def _rwkv_scan_bwd(ins, states, consts, do3, rk, lnw, lnb):
    bsz, s, _ = ins[0].shape
    nch = s // CHUNK

    mkv, ry, yv = consts

    def q_body(do_ref, ry_ref, yv_ref, r_ref, k_ref, v_ref, st_ref, rk_ref, lnw_ref, lnb_ref, q_ref):
        y, r, k, v, rk_, lnw_, lnb_ = _scan_rows(ry_ref, yv_ref, r_ref, k_ref, v_ref, st_ref, rk_ref, lnw_ref, lnb_ref)
        _, vjp = jax.vjp(lambda y_: _post(y_, r, k, v, rk_, lnw_, lnb_), y)
        (dy,) = vjp(_pairs(do_ref, list(range(CHUNK_GROUP))))
        q = _bmm_tn(_pairs(ry_ref, list(range(CHUNK_GROUP))), dy)
        for j in range(CHUNK_GROUP):
            for h in range(2):
                q_ref[0, 0, j, h] = q[2 * j + h]

    specs = _group_specs(6)
    q = pl.pallas_call(
        q_body, name="rwkv_q", grid=(bsz, 4, nch // CHUNK_GROUP), in_specs=specs, out_specs=specs[6],
        out_shape=SDS((bsz, 4, nch, 2, HEAD, HEAD), F32),
        compiler_params=_params(("parallel", "parallel", "parallel")))(do3, ry, yv, ins[0], ins[2], ins[3], states, rk, lnw, lnb)

    dstates = _chunk_recurrence(mkv, q, "rwkv_dstates")

    def body(r_ref, lw_ref, k_ref, v_ref, kk_ref, a_ref, st_ref, dst_ref, do_ref, rk_ref, lnw_ref, lnb_ref,
             dr_ref, dlw_ref, dk_ref, dv_ref, dkk_ref, da_ref, drk_ref, dlnw_ref, dlnb_ref):
        chunks = list(range(BWD_GROUP))
        par_refs = (drk_ref, dlnw_ref, dlnb_ref)

        @pl.when(jnp.logical_and(pl.program_id(1) == 0, pl.program_id(2) == 0))
        def _():
            for ref in par_refs:
                ref[...] = jnp.zeros_like(ref)

        def group(first):
            per_pair = lambda ref: _stack([ref[0, 0, c, h] for c in chunks for h in range(2)])
            vecs = [_stack([ref[:, HEAD * h:HEAD * h + HEAD] for _ in chunks for h in range(2)]) for ref in (rk_ref, lnw_ref, lnb_ref)]
            _, vjp = jax.vjp(functools.partial(_chunk_fn, first=first, d=_ONE_PASS), per_pair(st_ref),
                             *[_pairs(ref, chunks) for ref in (r_ref, lw_ref, k_ref, v_ref, kk_ref, a_ref)], *vecs)
            grads = vjp((_pairs(do_ref, chunks), per_pair(dst_ref)))
            for ref, cot in zip((dr_ref, dlw_ref, dk_ref, dv_ref, dkk_ref, da_ref), grads[1:7]):
                for j, c in enumerate(chunks):
                    ref[0, _chunk_rows(c), :] = _unpair(cot, j)
            for ref, g_ in zip(par_refs, grads[7:10]):
                ref[...] += jnp.concatenate([sum(g_[2 * j + h] for j in range(BWD_GROUP)) for h in range(2)], axis=1)

        pl.when(pl.program_id(2) == 0)(functools.partial(group, True))
        pl.when(pl.program_id(2) != 0)(functools.partial(group, False))

    tt = BWD_GROUP * CHUNK
    tile = pl.BlockSpec((1, tt, LANE), lambda hp, b, t: (b, t, hp))
    vec = pl.BlockSpec((1, LANE), lambda hp, b, t: (0, hp))
    st_spec = pl.BlockSpec((1, 1, BWD_GROUP, 2, HEAD, HEAD), lambda hp, b, t: (b, hp, t, 0, 0, 0))
    outs = pl.pallas_call(
        body, name="rwkv_scan_bwd", grid=(4, bsz, s // tt), in_specs=[tile] * 6 + [st_spec, st_spec, tile] + [vec] * 3,
        out_specs=[tile] * 6 + [vec] * 3,
        out_shape=[SDS((bsz, s, WIDTH), F32)] * 6 + [SDS((1, WIDTH), F32)] * 3,
        compiler_params=_params(("parallel", "arbitrary", "arbitrary")))(*ins, states, dstates, do3, rk, lnw, lnb)
    return outs[:6], outs[6:]


def _head(o_attn, o_rwkv, z_attn, z_rwkv, gm, x2, tgt, wua, wur, wout, g2):
    n = x2.shape[0]
    tm = 256
    nt = n // tm
    d = D_MODEL

    def body(oa_ref, or_ref, za_ref, zr_ref, gm_ref, x_ref, t_ref, wua_ref, wur_ref, wout_ref, g2_ref,
             dxo_ref, doa_ref, dor_ref, dza_ref, dzr_ref, dgm_ref, dwua_ref, dwur_ref, dwout_ref, dg2_ref, loss_ref, lacc):
        i = pl.program_id(0)
        oa, orw, za, zr = oa_ref[...], or_ref[...], za_ref[...], zr_ref[...]
        ga, gb = gm_ref[:, 0:d], gm_ref[:, d:2 * d]
        am = (oa * _silu(za)).astype(BF16)
        bm = (orw * _silu(zr)).astype(BF16)
        ya, yb = _dot(am, wua_ref[...]), _dot(bm, wur_ref[...])
        sa, sb = jax.nn.sigmoid(ga), jax.nn.sigmoid(gb)
        merged = (sa * ya + sb * yb).astype(BF16)
        out = _dot(merged, wout_ref[...])
        rs = lax.rsqrt(jnp.mean(out * out, axis=-1, keepdims=True) + RMS_EPS)
        g2 = g2_ref[...]
        err = x_ref[...] + out * rs * g2 - t_ref[...]
        lpart = jnp.sum(err * err, axis=0, keepdims=True)
        dxo = err * (1.0 / d)
        dxo_ref[...] = dxo
        dg2 = jnp.sum(dxo * out * rs, axis=0, keepdims=True)
        gd = dxo * g2
        dout = (rs * (gd - out * (rs * rs) * jnp.mean(gd * out, axis=-1, keepdims=True))).astype(BF16)
        dmerged = _dot_nt(dout, wout_ref[...])
        dwout = _dot_tn(merged, dout)
        dya, dyb = (dmerged * sa).astype(BF16), (dmerged * sb).astype(BF16)
        dgm_ref[:, 0:d] = (dmerged * ya * sa * (1.0 - sa)).astype(BF16)
        dgm_ref[:, d:2 * d] = (dmerged * yb * sb * (1.0 - sb)).astype(BF16)
        dam, dbm = _dot_nt(dya, wua_ref[...]), _dot_nt(dyb, wur_ref[...])
        dwua, dwur = _dot_tn(am, dya), _dot_tn(bm, dyb)
        doa_ref[...] = dam * _silu(za)
        dza_ref[...] = (dam * oa * _dsilu(za)).astype(BF16)
        dor_ref[...] = dbm * _silu(zr)
        dzr_ref[...] = (dbm * orw * _dsilu(zr)).astype(BF16)

        @pl.when(i == 0)
        def _():
            dwua_ref[...], dwur_ref[...], dwout_ref[...], dg2_ref[...], lacc[...] = dwua, dwur, dwout, dg2, lpart

        @pl.when(i != 0)
        def _():
            dwua_ref[...] += dwua
            dwur_ref[...] += dwur
            dwout_ref[...] += dwout
            dg2_ref[...] += dg2
            lacc[...] += lpart

        @pl.when(i == nt - 1)
        def _():
            loss_ref[...] = jnp.sum(lacc[...], axis=1, keepdims=True) * (0.5 / d)

    t512 = pl.BlockSpec((tm, WIDTH), lambda i: (i, 0))
    t1k = pl.BlockSpec((tm, d), lambda i: (i, 0))
    t2k = pl.BlockSpec((tm, 2 * d), lambda i: (i, 0))
    full = lambda r, c: pl.BlockSpec((r, c), lambda i: (0, 0))
    return pl.pallas_call(
        body, name="head_fwd_bwd", grid=(nt,),
        in_specs=[t512, t512, t512, t512, t2k, t1k, t1k, full(WIDTH, d), full(WIDTH, d), full(d, d), full(1, d)],
        out_specs=[t1k, t512, t512, t512, t512, t2k, full(WIDTH, d), full(WIDTH, d), full(d, d), full(1, d), full(1, 1)],
        out_shape=[SDS((n, d), F32), SDS((n, WIDTH), F32), SDS((n, WIDTH), F32), SDS((n, WIDTH), BF16), SDS((n, WIDTH), BF16),
                   SDS((n, 2 * d), BF16), SDS((WIDTH, d), F32), SDS((WIDTH, d), F32), SDS((d, d), F32), SDS((1, d), F32), SDS((1, 1), F32)],
        scratch_shapes=[pltpu.VMEM((1, d), F32)],
        compiler_params=_params(("arbitrary",)))(o_attn, o_rwkv, z_attn, z_rwkv, gm, x2, tgt, wua, wur, wout, g2)


def _prenorm_bwd(dh, x2, rs, g1, dxo):
    n, d = x2.shape
    tm = 1024

    def body(dh_ref, x_ref, rs_ref, g_ref, dxo_ref, gx_ref, dg_ref):
        x, r = x_ref[...], rs_ref[...]
        gd = dh_ref[...] * g_ref[...]
        gx_ref[...] = dxo_ref[...] + r * (gd - x * (r * r) * jnp.mean(gd * x, axis=-1, keepdims=True))
        dg = jnp.sum(dh_ref[...] * x * r, axis=0, keepdims=True)

        @pl.when(pl.program_id(0) == 0)
        def _():
            dg_ref[...] = dg

        @pl.when(pl.program_id(0) != 0)
        def _():
            dg_ref[...] += dg

    t = pl.BlockSpec((tm, d), lambda i: (i, 0))
    return pl.pallas_call(
        body, name="prenorm_bwd", grid=(n // tm,),
        in_specs=[t, t, pl.BlockSpec((tm, 1), lambda i: (i, 0)), pl.BlockSpec((1, d), lambda i: (0, 0)), t],
        out_specs=[t, pl.BlockSpec((1, d), lambda i: (0, 0))], out_shape=[SDS((n, d), F32), SDS((1, d), F32)],
        compiler_params=_params(("arbitrary",)))(dh, x2, rs, g1, dxo)


def _mesh_pos():
    x, y, c = lax.axis_index("x"), lax.axis_index("y"), lax.axis_index("c")
    return 4 * x + 2 * y + c


def _coords(idx):
    return (idx // 4, (idx // 2) % 2, idx % 2)


def _exchange(srcs, to_all, name):
    n = len(srcs)

    def body(*refs):
        src_refs, dst_refs = refs[:n], refs[n:2 * n]
        send_sems, recv_sems, local_sems = refs[2 * n:]
        me = _mesh_pos()

        def piece(i, j):
            return src_refs[i] if to_all[i] else src_refs[i].at[j]

        def remote(i, off, peer, block, slot):
            return pltpu.make_async_remote_copy(src_ref=piece(i, block), dst_ref=dst_refs[i].at[slot],
                                                send_sem=send_sems.at[i, off - 1], recv_sem=recv_sems.at[i, off - 1],
                                                device_id=_coords(peer), device_id_type=MESH)

        local = [pltpu.make_async_copy(piece(i, me), dst_refs[i].at[me], local_sems.at[i]) for i in range(n)]
        for cp in local:
            cp.start()
        sends = []
        for off in range(1, N_DEV):
            to = (me + off) % N_DEV
            for i in range(n):
                sends.append(remote(i, off, to, to, me))
                sends[-1].start()
        for off in range(1, N_DEV):
            frm = (me + N_DEV - off) % N_DEV
            for i in range(n):
                remote(i, off, frm, me, frm).wait_recv()
        for cp in sends:
            cp.wait_send()
        for cp in local:
            cp.wait()

    outs = pl.pallas_call(
        body, name=name, in_specs=[pl.BlockSpec(memory_space=pltpu.HBM)] * n, out_specs=[pl.BlockSpec(memory_space=pltpu.HBM)] * n,
        out_shape=[SDS((N_DEV,) + s.shape[-2:], s.dtype) for s in srcs],
        scratch_shapes=[pltpu.SemaphoreType.DMA((n, N_DEV - 1)), pltpu.SemaphoreType.DMA((n, N_DEV - 1)), pltpu.SemaphoreType.DMA((n,))],
        compiler_params=pltpu.CompilerParams())(*srcs)
    return outs


_HBM = pl.BlockSpec(memory_space=pltpu.HBM)
_SEM = pl.BlockSpec(memory_space=pltpu.SEMAPHORE)
_EFFECT = pltpu.SideEffectType.DATAFLOW_SIDE_EFFECTING


def _send_start(src):
    def body(src_ref, land_ref, send_sems, recv_sems, src_thru, land_thru, token):
        me = _mesh_pos()
        for off in range(1, N_DEV):
            to = (me + off) % N_DEV
            pltpu.make_async_remote_copy(src_ref=src_ref.at[to], dst_ref=land_ref.at[me], send_sem=send_sems.at[off - 1],
                                         recv_sem=recv_sems.at[off - 1], device_id=_coords(to), device_id_type=MESH).start()
        token[...] = jnp.zeros_like(token)

    hbm = pltpu.HBM(src.shape, src.dtype)
    return pl.pallas_call(
        body, name="grads_start",
        out_shape=(pltpu.SemaphoreType.DMA((N_DEV - 1,)), pltpu.SemaphoreType.DMA((N_DEV - 1,)), hbm, hbm, SDS((8, LANE), BF16)),
        in_specs=(_HBM, _HBM), out_specs=(_SEM, _SEM, _HBM, _HBM, pl.BlockSpec(memory_space=pltpu.VMEM)),
        input_output_aliases={0: 2, 1: 3}, compiler_params=pltpu.CompilerParams(has_side_effects=_EFFECT),
    )(pltpu.with_memory_space_constraint(src, pltpu.HBM), pltpu.with_memory_space_constraint(jnp.zeros(src.shape, src.dtype), pltpu.HBM))


def _send_wait(send_sems, recv_sems, src_thru, land_thru, after):
    def body(src_ref, land_ref, send_sems, recv_sems, after_ref, src_dead, got_ref):
        me = _mesh_pos()
        for off in range(1, N_DEV):
            to, frm = (me + off) % N_DEV, (me + N_DEV - off) % N_DEV
            pltpu.make_async_remote_copy(src_ref=src_ref.at[to], dst_ref=land_ref.at[me], send_sem=send_sems.at[off - 1],
                                         recv_sem=recv_sems.at[off - 1], device_id=_coords(to), device_id_type=MESH).wait_send()
            pltpu.make_async_remote_copy(src_ref=src_ref.at[me], dst_ref=land_ref.at[frm], send_sem=send_sems.at[off - 1],
                                         recv_sem=recv_sems.at[off - 1], device_id=_coords(frm), device_id_type=MESH).wait_recv()

    hbm = pltpu.HBM(src_thru.shape, src_thru.dtype)
    return pl.pallas_call(
        body, name="grads_wait", out_shape=(hbm, hbm), in_specs=(_HBM, _HBM, _SEM, _SEM, pl.BlockSpec(memory_space=pl.ANY)),
        out_specs=(_HBM, _HBM), input_output_aliases={0: 0, 1: 1}, compiler_params=pltpu.CompilerParams(has_side_effects=_EFFECT),
    )(src_thru, land_thru, send_sems, recv_sems, after)[1]


def _gather(srcs, name):
    n = len(srcs)

    def body(*refs):
        src_refs, dst_refs = refs[:n], refs[n:2 * n]
        send_sems, recv_sems, local_sems = refs[2 * n:]
        x, y, c = lax.axis_index("x"), lax.axis_index("y"), lax.axis_index("c")
        me, sibling = (x, y, c), (x, y, 1 - c)
        chips = [(1 - x, y), (x, 1 - y), (1 - x, 1 - y)]

        def slot(i, dev):
            return dst_refs[i].at[4 * dev[0] + 2 * dev[1] + dev[2]]

        def copy(i, k, block, to, own=False):
            return pltpu.make_async_remote_copy(src_ref=src_refs[i] if own else slot(i, block), dst_ref=slot(i, block),
                                                send_sem=send_sems.at[i, k], recv_sem=recv_sems.at[i, k],
                                                device_id=to, device_id_type=MESH)

        local = [pltpu.make_async_copy(src_refs[i], slot(i, me), local_sems.at[i]) for i in range(n)]
        for cp in local:
            cp.start()
        sends = []
        for i in range(n):
            sends.append(copy(i, 0, me, sibling, own=True))
            sends += [copy(i, 1 + j, me, (*chip, c), own=True) for j, chip in enumerate(chips)]
        for cp in sends:
            cp.start()
        for j, chip in enumerate(chips):
            for i in range(n):
                copy(i, 1 + j, (*chip, c), me).wait_recv()
                sends.append(copy(i, 4 + j, (*chip, c), sibling))
                sends[-1].start()
        for i in range(n):
            copy(i, 0, sibling, me).wait_recv()
            for j, chip in enumerate(chips):
                copy(i, 4 + j, (*chip, 1 - c), me).wait_recv()
        for cp in sends:
            cp.wait_send()
        for cp in local:
            cp.wait()

    return pl.pallas_call(
        body, name=name, in_specs=[pl.BlockSpec(memory_space=pltpu.HBM)] * n, out_specs=[pl.BlockSpec(memory_space=pltpu.HBM)] * n,
        out_shape=[SDS((N_DEV,) + s.shape, s.dtype) for s in srcs],
        scratch_shapes=[pltpu.SemaphoreType.DMA((n, N_DEV - 1)), pltpu.SemaphoreType.DMA((n, N_DEV - 1)), pltpu.SemaphoreType.DMA((n,))],
        compiler_params=pltpu.CompilerParams())(*srcs)


def _adamw(parts, w, m, v, tr, name, own=None):
    rows, cols = w.shape
    c1, c2 = 1.0 - ADAM_B1 ** ADAM_STEP, 1.0 - ADAM_B2 ** ADAM_STEP

    def body(p_ref, *refs):
        w_ref, m_ref, v_ref, g_ref, d_ref, nm_ref, nv_ref = refs[-7:]
        me = _mesh_pos()

        def part(j):
            return p_ref[j] if own is None else jnp.where(me == j, refs[0][...], p_ref[j])

        g = part(0).astype(F32)
        for j in range(1, N_DEV):
            g = g + part(j).astype(F32)
        nm = ADAM_B1 * m_ref[...] + (1.0 - ADAM_B1) * g
        nv = ADAM_B2 * v_ref[...] + (1.0 - ADAM_B2) * jnp.square(g)
        g_ref[...] = g
        nm_ref[...] = nm
        nv_ref[...] = nv
        d_ref[...] = -ADAM_LR * ((nm / c1) / (jnp.sqrt(nv / c2) + ADAM_EPS) + ADAM_WD * w_ref[...])

    t = pl.BlockSpec((tr, cols), lambda i: (i, 0))
    extra = [] if own is None else [own]
    return pl.pallas_call(
        body, name=name, grid=(rows // tr,), in_specs=[pl.BlockSpec((N_DEV, tr, cols), lambda i: (0, i, 0))] + [t] * (3 + len(extra)),
        out_specs=[t] * 4, out_shape=[SDS((rows, cols), F32)] * 4, compiler_params=_params(("parallel",)))(parts, *extra, w, m, v)


SHARDED = (("w_in", D_MODEL, IN_COLS // N_DEV, True, 128), ("w_up_attn", WIDTH, D_MODEL // N_DEV, True, WIDTH),
           ("w_up_rwkv", WIDTH, D_MODEL // N_DEV, True, WIDTH), ("w_out", D_MODEL // N_DEV, D_MODEL, False, D_MODEL // N_DEV),
           ("rwkv_w_up", LORA, WIDTH // N_DEV, True, LORA), ("rwkv_a_up", LORA, WIDTH // N_DEV, True, LORA))
LOSS_SLOT = sum(n for _, n in SMALL)


def _pack_small(small, extra=None):
    flat = [small[n].reshape(-1).astype(F32) for n, _ in SMALL]
    flat.append(jnp.zeros((1,), F32) if extra is None else extra.reshape(1))
    flat.append(jnp.zeros((SMALL_ROWS * LANE - LOSS_SLOT - 1,), F32))
    return jnp.concatenate(flat).reshape(SMALL_ROWS, LANE)


def _unpack_small(packed, shapes):
    flat = packed.reshape(-1)
    out, off = {}, 0
    for n, cnt in SMALL:
        out[n] = flat[off:off + cnt].reshape(shapes[n])
        off += cnt
    return out, flat[LOSS_SLOT]


def _whole(gathered, by_cols):
    if not by_cols:
        return gathered.reshape(-1, gathered.shape[-1])
    return gathered.transpose(1, 0, 2).reshape(gathered.shape[1], -1)


def _per_owner(full, by_cols):
    if not by_cols:
        return full.reshape(N_DEV, -1, full.shape[-1])
    return full.reshape(full.shape[0], N_DEV, -1).transpose(1, 0, 2)


def _local_step(x, loss_target, sm, wts):
    bsz, s, d = x.shape
    n = bsz * s
    x2, tgt = x.reshape(n, d), loss_target.reshape(n, d)
    bidx = jnp.asarray(_bucket_tables())
    w_in = wts["w_in"]
    segs = (("qkv", 0, QKV_COLS, 1536), ("za", OFF_ZA, WIDTH, 512), ("pr", OFF_PR, PR_COLS, PR_COLS), ("zr", OFF_ZR, WIDTH, 512),
            ("gm", OFF_GM, 2 * D_MODEL, 1024))

    h, rs = _prenorm(x2, sm["pre_norm_gain"])
    proj = {nm: _mm(h, w_in[:, off:off + cnt], tn, "proj_" + nm) for nm, off, cnt, tn in segs}
    qkv3 = proj["qkv"].reshape(bsz, s, QKV_COLS)
    pr3 = proj["pr"].reshape(bsz, s, PR_COLS)

    o_attn, lse = _attn_fwd(qkv3, sm["rel_bias"], bidx)
    rk = sm["rwkv_r_k"].reshape(1, WIDTH)
    pre_args = (sm["rwkv_shift_mix"], sm["rwkv_w0"], wts["rwkv_w_up"], sm["rwkv_a0"], wts["rwkv_a_up"], sm["rwkv_k_k"], sm["rwkv_k_a"])
    scan_in = _rwkv_pre(pr3, *pre_args)
    o_rwkv, states, consts = _rwkv_scan(scan_in, rk, sm["rwkv_ln_w"], sm["rwkv_ln_b"])

    (dxo, do_attn, do_rwkv, dza, dzr, dgm, g_wua, g_wur, g_wout, g_post, loss) = _head(
        o_attn.reshape(n, WIDTH), o_rwkv.reshape(n, WIDTH), proj["za"], proj["zr"], proj["gm"], x2, tgt,
        wts["w_up_attn"], wts["w_up_rwkv"], wts["w_out"], sm["post_norm_gain"])

    dqkv, dbias = _attn_bwd(qkv3, o_attn, lse, do_attn.reshape(bsz, s, WIDTH), sm["rel_bias"], bidx)
    g_bias = _bias_grad(dbias, bidx)[:, :N_BUCKET].T

    scan_cots, (g_rk, g_lnw, g_lnb) = _rwkv_scan_bwd(scan_in, states, consts, do_rwkv.reshape(bsz, s, WIDTH), rk, sm["rwkv_ln_w"],
                                                     sm["rwkv_ln_b"])
    dprs, g_mix, g_w0, g_wup, g_a0, g_aup, g_kk, g_ka = _rwkv_pre_bwd(pr3, scan_cots, *pre_args)
    dpr = _shift_bwd(dprs, sm["rwkv_shift_mix"]).reshape(n, PR_COLS)

    dsegs = [(dqkv.reshape(9, n, WIDTH), 0, QKV_COLS, WIDTH), (dza, OFF_ZA, WIDTH, WIDTH), (dpr, OFF_PR, PR_COLS, PR_COLS),
             (dzr, OFF_ZR, WIDTH, WIDTH), (dgm, OFF_GM, 2 * D_MODEL, D_MODEL)]
    g_win = jnp.concatenate([_mm_tn(h, t, tn, "gw_in_%d" % j) for j, (t, _, _, tn) in enumerate(dsegs)], axis=1)
    blocks = _per_owner(g_win, True).astype(BF16)
    own = lax.dynamic_index_in_dim(blocks, 4 * lax.axis_index("x") + 2 * lax.axis_index("y") + lax.axis_index("c"), 0, keepdims=False)
    send_sems, recv_sems, blocks_thru, land_thru, token = _send_start(blocks)
    w_seg = lambda off, cnt: w_in[:, off:off + cnt] + token[0, 0]
    dh = _mm_nt_acc(dsegs[0][0], w_seg(0, QKV_COLS), None, "dh_qkv")
    dh = _mm_nt_multi([t for t, *_ in dsegs[1:]], [w_seg(off, cnt) for _, off, cnt, _ in dsegs[1:]], dh, "dh_rest")
    grad_x, g_pre = _prenorm_bwd(dh, x2, rs, sm["pre_norm_gain"], dxo)
    landed = _send_wait(send_sems, recv_sems, blocks_thru, land_thru, g_pre)

    full = {"w_up_attn": g_wua, "w_up_rwkv": g_wur, "w_out": g_wout, "rwkv_w_up": g_wup, "rwkv_a_up": g_aup}
    small = {"pre_norm_gain": g_pre, "rel_bias": g_bias, "rwkv_shift_mix": g_mix, "rwkv_w0": g_w0, "rwkv_a0": g_a0, "rwkv_k_k": g_kk,
             "rwkv_k_a": g_ka, "rwkv_r_k": g_rk, "rwkv_ln_w": g_lnw, "rwkv_ln_b": g_lnb, "post_norm_gain": g_post}
    return loss[0, 0], grad_x.reshape(bsz, s, d), (landed, own), full, small


def kernel(x, pre_norm_gain, w_in, rel_bias, rwkv_shift_mix, rwkv_w0, rwkv_w_up, rwkv_a0, rwkv_a_up, rwkv_k_k, rwkv_k_a, rwkv_r_k, rwkv_ln_w, rwkv_ln_b, w_up_attn, w_up_rwkv, w_out, post_norm_gain, loss_target, m_pre_norm_gain, m_w_in, m_rel_bias, m_rwkv_shift_mix, m_rwkv_w0, m_rwkv_w_up, m_rwkv_a0, m_rwkv_a_up, m_rwkv_k_k, m_rwkv_k_a, m_rwkv_r_k, m_rwkv_ln_w, m_rwkv_ln_b, m_w_up_attn, m_w_up_rwkv, m_w_out, m_post_norm_gain, v_pre_norm_gain, v_w_in, v_rel_bias, v_rwkv_shift_mix, v_rwkv_w0, v_rwkv_w_up, v_rwkv_a0, v_rwkv_a_up, v_rwkv_k_k, v_rwkv_k_a, v_rwkv_r_k, v_rwkv_ln_w, v_rwkv_ln_b, v_w_up_attn, v_w_up_rwkv, v_w_out, v_post_norm_gain):
    names = [n for n, *_ in SHARDED] + [n for n, _ in SMALL]
    loc = dict(locals())
    w = {n: loc[n] for n in names}
    m = {n: loc["m_" + n] for n in names}
    v = {n: loc["v_" + n] for n in names}
    shapes = {n: w[n].shape for n in names}
    order = ["pre_norm_gain", "w_in", "rel_bias", "rwkv_shift_mix", "rwkv_w0", "rwkv_w_up", "rwkv_a0", "rwkv_a_up", "rwkv_k_k", "rwkv_k_a",
             "rwkv_r_k", "rwkv_ln_w", "rwkv_ln_b", "w_up_attn", "w_up_rwkv", "w_out", "post_norm_gain"]
    shard2d = lambda t, n, r, c: t[n].reshape(r, c)

    gathered = _gather([shard2d(w, n, r, c).astype(BF16) for n, r, c, _, _ in SHARDED], "gather_weights")
    wts = {n: _whole(g, by_cols) for (n, _, _, by_cols, _), g in zip(SHARDED, gathered)}

    loss, grad_x, (win_landed, win_own), full, small = _local_step(x, loss_target, w, wts)
    rest = SHARDED[1:]
    parts = _exchange([_per_owner(full[n], by_cols).astype(BF16) for n, _, _, by_cols, _ in rest] + [_pack_small(small, loss)],
                      [False] * len(rest) + [True], "exchange_grads")

    outs = [{}, {}, {}, {}]
    for (n, r, c, _, tr), p in zip(SHARDED, [win_landed] + list(parts)):
        res = _adamw(p, shard2d(w, n, r, c), shard2d(m, n, r, c), shard2d(v, n, r, c), tr, "adamw_" + n,
                     own=win_own if n == "w_in" else None)
        for o, t in zip(outs, res):
            o[n] = t.reshape(shapes[n])
    res = _adamw(parts[-1], _pack_small(w), _pack_small(m), _pack_small(v), SMALL_ROWS, "adamw_small")
    for o, t in zip(outs, res):
        o.update(_unpack_small(t, shapes)[0])
    loss = _unpack_small(res[0], shapes)[1]
    return (loss, grad_x, *[o[n] for o in outs for n in order])
```

```python
import functools
import math

import numpy as np
import jax
import jax.numpy as jnp
from jax import lax
from jax.experimental import pallas as pl
from jax.experimental.pallas import tpu as pltpu

F32, BF16 = jnp.float32, jnp.bfloat16
SDS = jax.ShapeDtypeStruct
MESH = pl.DeviceIdType.MESH

N_DEV = 8
D_MODEL = 1024
HEAD = 64
N_HEAD = 8
WIDTH = N_HEAD * HEAD
DILATIONS = (1, 4, 16)
QB = 128
N_BUCKET = 32
MAX_DIST = 2048
LORA = 64
QKV_COLS = 9 * WIDTH
PR_COLS = 3 * WIDTH + 2 * LORA
IN_COLS = QKV_COLS + WIDTH + PR_COLS + WIDTH + 2 * D_MODEL
OFF_ZA, OFF_PR, OFF_ZR, OFF_GM = QKV_COLS, QKV_COLS + WIDTH, QKV_COLS + WIDTH + PR_COLS, QKV_COLS + 2 * WIDTH + PR_COLS
RMS_EPS = 1e-6
GN_EPS = 64e-5
SCALE = 1.0 / math.sqrt(HEAD)
CHUNK = 64
CHUNK_GROUP = 8
BWD_GROUP = 16
EARLY = 8
NEG = -1e30
LANE = 128

ADAM_LR, ADAM_B1, ADAM_B2, ADAM_EPS, ADAM_WD, ADAM_STEP = 0.001, 0.9, 0.999, 1e-08, 0.01, 10

VMEM_LIMIT = 56 * 1024 * 1024

SMALL = (("pre_norm_gain", 1024), ("rel_bias", 768), ("rwkv_shift_mix", 1664), ("rwkv_w0", 512), ("rwkv_a0", 512),
         ("rwkv_k_k", 512), ("rwkv_k_a", 512), ("rwkv_r_k", 512), ("rwkv_ln_w", 512), ("rwkv_ln_b", 512),
         ("post_norm_gain", 1024))
SMALL_ROWS = 64


def _params(sem=None):
    return pltpu.CompilerParams(dimension_semantics=sem, vmem_limit_bytes=VMEM_LIMIT)


def _dot(a, b):
    return jnp.dot(a, b, preferred_element_type=F32)


def _dot_nt(a, b):
    return lax.dot_general(a, b, (((1,), (1,)), ((), ())), preferred_element_type=F32)


def _dot_tn(a, b):
    return lax.dot_general(a, b, (((0,), (0,)), ((), ())), preferred_element_type=F32)


@jax.custom_vjp
def _bdot(a, b):
    return _dot(a.astype(BF16), b.astype(BF16))


def _bdot_fwd(a, b):
    return _bdot(a, b), (a, b)


def _bdot_bwd(res, g):
    a, b = res
    gb = g.astype(BF16)
    return _dot_nt(gb, b.astype(BF16)), _dot_tn(a.astype(BF16), gb)


_bdot.defvjp(_bdot_fwd, _bdot_bwd)


def _silu(z):
    return z * jax.nn.sigmoid(z)


def _dsilu(z):
    s = jax.nn.sigmoid(z)
    return s * (1.0 + z * (1.0 - s))


def _softplus(x):
    return jnp.maximum(x, 0.0) + jnp.log(1.0 + jnp.exp(-jnp.abs(x)))


def _bucket_tables():
    qi = np.arange(QB)[:, None] + QB
    ki = np.arange(2 * QB)[None, :]
    rel = np.maximum(qi - ki, 0)
    out = []
    for d in DILATIONS:
        dist = rel * d
        max_exact = N_BUCKET // 2
        ratio = np.log(np.maximum(dist, 1).astype(np.float32) / max_exact) / np.float32(math.log(MAX_DIST / max_exact))
        large = max_exact + (ratio * (N_BUCKET - max_exact)).astype(np.int32)
        large = np.minimum(large, N_BUCKET - 1)
        out.append(np.where(dist < max_exact, dist, large).astype(np.int32))
    return np.stack(out)


def _prenorm(x2, g):
    n, d = x2.shape
    tm = 1024

    def body(x_ref, g_ref, h_ref, rs_ref):
        x = x_ref[...]
        rs = lax.rsqrt(jnp.mean(x * x, axis=-1, keepdims=True) + RMS_EPS)
        h_ref[...] = (x * rs * g_ref[...]).astype(BF16)
        rs_ref[...] = rs

    return pl.pallas_call(
        body, name="prenorm", grid=(n // tm,),
        in_specs=[pl.BlockSpec((tm, d), lambda i: (i, 0)), pl.BlockSpec((1, d), lambda i: (0, 0))],
        out_specs=[pl.BlockSpec((tm, d), lambda i: (i, 0)), pl.BlockSpec((tm, 1), lambda i: (i, 0))],
        out_shape=[SDS((n, d), BF16), SDS((n, 1), F32)], compiler_params=_params(("parallel",)))(x2, g)


def _mm(a, b, tn, name):
    m, k = a.shape
    n = b.shape[1]
    tm = 1024

    def body(a_ref, b_ref, o_ref):
        o_ref[...] = _dot(a_ref[...], b_ref[...])

    return pl.pallas_call(
        body, name=name, grid=(n // tn, m // tm),
        in_specs=[pl.BlockSpec((tm, k), lambda j, i: (i, 0)), pl.BlockSpec((k, tn), lambda j, i: (0, j))],
        out_specs=pl.BlockSpec((tm, tn), lambda j, i: (i, j)),
        out_shape=SDS((m, n), F32), compiler_params=_params(("parallel", "parallel")))(a, b)


def _mm_nt(a, b, after, name):
    m, seg = a.shape[1], a.shape[2]
    d, k = b.shape
    tm = 1024
    per = 3
    tk = per * seg

    def body(a_ref, b_ref, after_ref, o_ref):
        r = sum(_dot_nt(a_ref[j].astype(BF16), b_ref[:, seg * j:seg * (j + 1)]) for j in range(per))

        @pl.when(pl.program_id(1) == 0)
        def _():
            o_ref[...] = r

        @pl.when(pl.program_id(1) != 0)
        def _():
            o_ref[...] += r

    in_specs = [pl.BlockSpec((per, tm, seg), lambda i, j: (j, i, 0)), pl.BlockSpec((d, tk), lambda i, j: (0, j)),
                pl.BlockSpec(after.shape, lambda i, j: (0, 0))]
    return pl.pallas_call(
        body, name=name, grid=(m // tm, k // tk), in_specs=in_specs, out_specs=pl.BlockSpec((tm, d), lambda i, j: (i, 0)),
        out_shape=SDS((m, d), F32), compiler_params=_params(("parallel", "arbitrary")))(a, b, after)


def _mm_nt_multi(a_list, b_list, acc, name):
    m, d = acc.shape
    tm = 512
    n = len(a_list)

    def body(*refs):
        r = refs[2 * n][...]
        for a_ref, b_ref in zip(refs[:n], refs[n:2 * n]):
            r = r + _dot_nt(a_ref[...].astype(BF16), b_ref[...])
        refs[2 * n + 1][...] = r

    in_specs = [pl.BlockSpec((tm, a.shape[1]), lambda i: (i, 0)) for a in a_list]
    in_specs += [pl.BlockSpec(b.shape, lambda i: (0, 0)) for b in b_list]
    in_specs.append(pl.BlockSpec((tm, d), lambda i: (i, 0)))
    return pl.pallas_call(
        body, name=name, grid=(m // tm,), in_specs=in_specs, out_specs=pl.BlockSpec((tm, d), lambda i: (i, 0)),
        out_shape=SDS((m, d), F32), compiler_params=_params(("parallel",)))(*a_list, *b_list, acc)


def _mm_tn(a, b, tn, name):
    split = b.ndim == 3
    m, k1 = a.shape
    per = 3 if split else 1
    seg = b.shape[2] if split else tn
    tn = per * seg
    n2 = b.shape[0] * seg if split else b.shape[1]
    tm = 1024

    def body(a_ref, b_ref, o_ref):
        first = pl.program_id(1) == 0
        for j in range(per):
            r = _dot_tn(a_ref[...], (b_ref[j] if split else b_ref[...]).astype(BF16))
            cols = slice(seg * j, seg * (j + 1))

            @pl.when(first)
            def _(r=r, cols=cols):
                o_ref[:, cols] = r

            @pl.when(jnp.logical_not(first))
            def _(r=r, cols=cols):
                o_ref[:, cols] += r

    b_spec = pl.BlockSpec((per, tm, seg), lambda j, i: (j, i, 0)) if split else pl.BlockSpec((tm, tn), lambda j, i: (i, j))
    return pl.pallas_call(
        body, name=name, grid=(n2 // tn, m // tm),
        in_specs=[pl.BlockSpec((tm, k1), lambda j, i: (i, 0)), b_spec],
        out_specs=pl.BlockSpec((k1, tn), lambda j, i: (0, j)),
        out_shape=SDS((k1, n2), F32), compiler_params=_params(("parallel", "arbitrary")))(a, b)


def _ds(start, d):
    return pl.ds(start, QB) if d == 1 else pl.ds(start, QB, stride=d)


def _fill_bias(tab_ref, bidx_ref, bias_sc, hp):
    for g in range(3):
        bi = bidx_ref[g]
        for h in range(2):
            acc = jnp.zeros((QB, 2 * QB), F32)
            for j in range(N_BUCKET):
                acc = jnp.where(bi == j, tab_ref[j, g * N_HEAD + hp * 2 + h], acc)
            bias_sc[g * 2 + h] = acc


def _block_starts(it, d, nb):
    rho = it // nb
    n = it % nb
    st = rho + d * QB * n
    stp = rho + d * QB * jnp.maximum(n - 1, 0)
    if d == 1:
        st, stp = pl.multiple_of(QB * it, QB), pl.multiple_of(QB * jnp.maximum(it - 1, 0), QB)
    return st, stp, n > 0


ATTN_BLOCKS = 4


def _bdot3(a, b, dims):
    return lax.dot_general(a, b, (dims, ((0,), (0,))), preferred_element_type=F32)


def _attn_operands(q_ref, k_ref, v_ref, bias_sc, g, d, nb, it0):
    two = nb > 1
    nk = 2 * QB if two else QB
    ii = lax.broadcasted_iota(jnp.int32, (QB, nk), 0)
    cc = lax.broadcasted_iota(jnp.int32, (QB, nk), 1)
    qs, ks, vs, pens, starts = [], [], [], [], []
    for u in range(ATTN_BLOCKS):
        st, stp, hasprev = _block_starts(it0 + u, d, nb)
        qf = q_ref[0, _ds(st, d), :]
        if two:
            kf = jnp.concatenate([k_ref[0, _ds(stp, d), :], k_ref[0, _ds(st, d), :]], axis=0).astype(BF16)
            vf = jnp.concatenate([v_ref[0, _ds(stp, d), :], v_ref[0, _ds(st, d), :]], axis=0).astype(BF16)
            own = jnp.logical_and(cc >= QB, ii >= cc - QB)
            prev = jnp.logical_and(jnp.logical_and(cc < QB, cc >= ii), hasprev)
            pen = jnp.where(jnp.logical_or(own, prev), 0.0, NEG)
        else:
            kf, vf = k_ref[0, _ds(st, d), :].astype(BF16), v_ref[0, _ds(st, d), :].astype(BF16)
            pen = jnp.where(ii >= cc, 0.0, NEG)
        for h in range(2):
            qs.append(_one_head(qf, h).astype(BF16))
            ks.append(kf)
            vs.append(vf)
            pens.append(pen + (bias_sc[g * 2 + h] if two else bias_sc[g * 2 + h, :, QB:2 * QB]))
        starts.append((st, stp))
    return _stack(qs), _stack(ks), _stack(vs), _stack(pens), starts


def _one_head(x, h):
    lane = lax.broadcasted_iota(jnp.int32, x.shape, 1)
    return jnp.where(lane >= HEAD if h == 1 else lane < HEAD, x, 0.0)


def _pick_heads(x, u):
    lane = lax.broadcasted_iota(jnp.int32, x.shape[1:], 1)
    return jnp.where(lane < HEAD, x[2 * u], x[2 * u + 1])


def _add_heads(x, u):
    return x[2 * u] + x[2 * u + 1]


def _attn_fwd(qkv3, rel_bias, bidx):
    bsz, s, _ = qkv3.shape
    rt = 256

    def body(tab_ref, bidx_ref, *refs):
        q_refs, k_refs, v_refs = refs[0:3], refs[3:6], refs[6:9]
        o_ref, lse_ref = refs[9:11]
        bias_sc, num_sc, den_sc, m_sc = refs[11:]
        pl.when(pl.program_id(1) == 0)(lambda: _fill_bias(tab_ref, bidx_ref, bias_sc, pl.program_id(0)))
        for g, d in enumerate(DILATIONS):
            nb = s // (QB * d)

            def blk(it, c, g=g, d=d, nb=nb):
                q, k, v, bias, starts = _attn_operands(q_refs[g], k_refs[g], v_refs[g], bias_sc, g, d, nb, it * ATTN_BLOCKS)
                sc = _bdot3(q, k, ((2,), (2,))) * SCALE + bias
                m = jnp.max(sc, axis=-1, keepdims=True)
                p = jnp.exp(sc - m)
                den = jnp.sum(p, axis=-1, keepdims=True)
                num = _bdot3(p.astype(BF16), v, ((2,), (1,)))
                den, m = jnp.broadcast_to(den, num.shape), jnp.broadcast_to(m, num.shape)
                for u, (st, _) in enumerate(starts):
                    num_sc[g, _ds(st, d), :] = _pick_heads(num, u)
                    den_sc[g, _ds(st, d), :] = _pick_heads(den, u)
                    m_sc[g, _ds(st, d), :] = _pick_heads(m, u)
                return c

            lax.fori_loop(0, s // QB // ATTN_BLOCKS, blk, 0)

        def merge(i, c):
            rows = pl.ds(pl.multiple_of(i * rt, rt), rt)
            m0, m1, m2 = m_sc[0, rows, :], m_sc[1, rows, :], m_sc[2, rows, :]
            mall = jnp.maximum(jnp.maximum(m0, m1), m2)
            w0, w1, w2 = jnp.exp(m0 - mall), jnp.exp(m1 - mall), jnp.exp(m2 - mall)
            num = w0 * num_sc[0, rows, :] + w1 * num_sc[1, rows, :] + w2 * num_sc[2, rows, :]
            den = w0 * den_sc[0, rows, :] + w1 * den_sc[1, rows, :] + w2 * den_sc[2, rows, :]
            o_ref[0, rows, :] = num / den
            lse_ref[0, rows, :] = mall + jnp.log(den)
            return c

        lax.fori_loop(0, s // rt, merge, 0)

    col = lambda w, g: (lambda hp, b: (b, 0, (w * 3 + g) * 4 + hp))
    in_specs = [pl.BlockSpec(memory_space=pltpu.SMEM), pl.BlockSpec((3, QB, 2 * QB), lambda hp, b: (0, 0, 0))]
    in_specs += [pl.BlockSpec((1, s, LANE), col(w, g)) for w in range(3) for g in range(3)]
    out_spec = pl.BlockSpec((1, s, LANE), lambda hp, b: (b, 0, hp))
    return pl.pallas_call(
        body, name="attn_fwd", grid=(4, bsz), in_specs=in_specs, out_specs=[out_spec, out_spec],
        out_shape=[SDS((bsz, s, WIDTH), F32), SDS((bsz, s, WIDTH), F32)],
        scratch_shapes=[pltpu.VMEM((6, QB, 2 * QB), F32), pltpu.VMEM((3, s, LANE), F32), pltpu.VMEM((3, s, LANE), F32),
                        pltpu.VMEM((3, s, LANE), F32)],
        compiler_params=_params(("arbitrary", "arbitrary")))(rel_bias, bidx, *([qkv3] * 9))


def _attn_bwd(qkv3, o3, lse3, do3, rel_bias, bidx):
    bsz, s, _ = qkv3.shape
    rt = 256

    def body(tab_ref, bidx_ref, *refs):
        q_refs, k_refs, v_refs = refs[0:3], refs[3:6], refs[6:9]
        o_ref, lse_ref, do_ref, dqkv_ref, db_ref, bias_sc, delta_sc, acc_sc = refs[9:]
        dq_refs, dk_refs, dv_refs = ([acc_sc.at[w * 3 + g] for g in range(3)] for w in range(3))

        @pl.when(pl.program_id(1) == 0)
        def _():
            _fill_bias(tab_ref, bidx_ref, bias_sc, pl.program_id(0))
            db_ref[...] = jnp.zeros_like(db_ref)

        def prep(i, c):
            rows = pl.ds(pl.multiple_of(i * rt, rt), rt)
            prod = do_ref[0, rows, :] * o_ref[0, rows, :]
            d0 = jnp.sum(prod[:, :HEAD], axis=-1, keepdims=True)
            d1 = jnp.sum(prod[:, HEAD:], axis=-1, keepdims=True)
            delta_sc[rows, :] = jnp.concatenate([jnp.broadcast_to(d0, (rt, HEAD)), jnp.broadcast_to(d1, (rt, HEAD))], axis=1)
            z = jnp.zeros((rt, LANE), F32)
            for g in range(3):
                dk_refs[g][0, rows, :] = z
                dv_refs[g][0, rows, :] = z
            return c

        lax.fori_loop(0, s // rt, prep, 0)
        for g, d in enumerate(DILATIONS):
            nb = s // (QB * d)

            def blk(it, c, g=g, d=d, nb=nb):
                q, k, v, bias, starts = _attn_operands(q_refs[g], k_refs[g], v_refs[g], bias_sc, g, d, nb, it * ATTN_BLOCKS)
                dos, lses, deltas = [], [], []
                for st, _ in starts:
                    dof, lsef, delf = do_ref[0, _ds(st, d), :], lse_ref[0, _ds(st, d), :], delta_sc[_ds(st, d), :]
                    for h in range(2):
                        dos.append(_one_head(dof, h).astype(BF16))
                        lses.append(lsef[:, HEAD * h:HEAD * h + 1])
                        deltas.append(delf[:, HEAD * h:HEAD * h + 1])
                do, lse, delta = _stack(dos), _stack(lses), _stack(deltas)
                p = jnp.exp(_bdot3(q, k, ((2,), (2,))) * SCALE + bias - lse)
                dv = _bdot3(p.astype(BF16), do, ((1,), (1,)))
                ds = p * (_bdot3(do, v, ((2,), (2,))) - delta)
                dsb = ds.astype(BF16)
                dq = _bdot3(dsb, k, ((2,), (1,))) * SCALE
                dk = _bdot3(dsb, q, ((1,), (1,))) * SCALE
                two = nb > 1
                for h in range(2):
                    dsum = sum(ds[2 * u + h] for u in range(ATTN_BLOCKS))
                    if two:
                        db_ref[0, g * 2 + h] += dsum
                    else:
                        db_ref[0, g * 2 + h, :, QB:2 * QB] += dsum
                for u, (st, stp) in enumerate(starts):
                    dq_refs[g][0, _ds(st, d), :] = _pick_heads(dq, u)
                    if two:
                        dk_refs[g][0, _ds(stp, d), :] += _add_heads(dk[:, :QB], u)
                        dv_refs[g][0, _ds(stp, d), :] += _add_heads(dv[:, :QB], u)
                    dk_refs[g][0, _ds(st, d), :] += _add_heads(dk[:, QB:] if two else dk, u)
                    dv_refs[g][0, _ds(st, d), :] += _add_heads(dv[:, QB:] if two else dv, u)
                return c

            lax.fori_loop(0, s // QB // ATTN_BLOCKS, blk, 0)

        def flush(i, c):
            rows = pl.ds(pl.multiple_of(i * rt, rt), rt)
            for j in range(9):
                dqkv_ref[j, 0, rows, :] = acc_sc[j, 0, rows, :].astype(BF16)
            return c

        lax.fori_loop(0, s // rt, flush, 0)

    col = lambda w, g: (lambda hp, b: (b, 0, (w * 3 + g) * 4 + hp))
    blk_spec = pl.BlockSpec((1, s, LANE), lambda hp, b: (b, 0, hp))
    in_specs = [pl.BlockSpec(memory_space=pltpu.SMEM), pl.BlockSpec((3, QB, 2 * QB), lambda hp, b: (0, 0, 0))]
    in_specs += [pl.BlockSpec((1, s, LANE), col(w, g)) for w in range(3) for g in range(3)]
    in_specs += [blk_spec] * 3
    out_specs = [pl.BlockSpec((9, 1, s, LANE), lambda hp, b: (0, b, 0, hp)), pl.BlockSpec((1, 6, QB, 2 * QB), lambda hp, b: (hp, 0, 0, 0))]
    out_shape = [SDS((9, bsz, s, WIDTH), BF16), SDS((4, 6, QB, 2 * QB), F32)]
    return pl.pallas_call(
        body, name="attn_bwd", grid=(4, bsz), in_specs=in_specs, out_specs=out_specs, out_shape=out_shape,
        scratch_shapes=[pltpu.VMEM((6, QB, 2 * QB), F32), pltpu.VMEM((s, LANE), F32), pltpu.VMEM((9, 1, s, LANE), F32)],
        compiler_params=_params(("parallel", "arbitrary")))(rel_bias, bidx, *([qkv3] * 9), o3, lse3, do3)


def _bias_grad(dbias, bidx):
    def body(db_ref, bidx_ref, o_ref):
        lane = lax.broadcasted_iota(jnp.int32, (1, LANE), 1)
        for g in range(3):
            bi = bidx_ref[g]
            for hp in range(4):
                for h in range(2):
                    mat = db_ref[hp, g * 2 + h]
                    row = jnp.zeros((1, LANE), F32)
                    for j in range(N_BUCKET):
                        part = jnp.sum(jnp.where(bi == j, mat, 0.0), axis=0, keepdims=True)
                        row = jnp.where(lane == j, jnp.sum(part, axis=1, keepdims=True), row)
                    hd = g * N_HEAD + hp * 2 + h
                    o_ref[hd:hd + 1, :] = row

    return pl.pallas_call(body, name="bias_grad", out_shape=SDS((3 * N_HEAD, LANE), F32), compiler_params=_params())(dbias, bidx)


def _pre_fn(r, k0, v, wl, al, w0, wup, a0, aup, kk_, ka_):
    u = w0 + _bdot(jnp.tanh(wl), wup)
    lw = -jnp.exp(-_softplus(-u) - 0.5)
    a = jax.nn.sigmoid(a0 + _bdot(al, aup))
    kkraw = k0 * kk_
    k = k0 * (1.0 + (a - 1.0) * ka_)
    return r, lw, k, v, kkraw, a


PRE_SPLIT = (0, WIDTH, 2 * WIDTH, 3 * WIDTH, 3 * WIDTH + LORA, 3 * WIDTH + 2 * LORA)


def _pre_pieces(prs):
    return [prs[:, a:b] for a, b in zip(PRE_SPLIT[:-1], PRE_SPLIT[1:])]


PRE_TT = 512


def _shifted(pr_ref, edge_ref, first, back):
    pr = pr_ref[0]
    tt = pr.shape[0]
    row = lax.broadcasted_iota(jnp.int32, (tt, 1), 0)
    if back:
        edge = jnp.where(first, 0.0, edge_ref[0, 7:8, :])
        return jnp.where(row == 0, edge, pltpu.roll(pr, 1, axis=0))
    edge = jnp.where(first, 0.0, edge_ref[0, 0:1, :])
    return jnp.where(row == tt - 1, edge, pltpu.roll(pr, tt - 1, axis=0))


def _rwkv_pre(pr3, mix, w0, wup, a0, aup, kk_, ka_):
    bsz, s, _ = pr3.shape
    tt = PRE_TT

    def body(pr_ref, edge_ref, mix_ref, w0_ref, wup_ref, a0_ref, aup_ref, kk_ref, ka_ref, *outs):
        pr = pr_ref[0]
        prev = _shifted(pr_ref, edge_ref, pl.program_id(1) == 0, True)
        prs = pr + (prev - pr) * mix_ref[...]
        vals = _pre_fn(*_pre_pieces(prs), w0_ref[...], wup_ref[...].astype(F32), a0_ref[...], aup_ref[...].astype(F32), kk_ref[...],
                       ka_ref[...])
        for o, val in zip(outs, vals):
            o[0] = val

    vec = lambda n: pl.BlockSpec((1, n), lambda b, i: (0, 0))
    mat = pl.BlockSpec((LORA, WIDTH), lambda b, i: (0, 0))
    in_specs = [pl.BlockSpec((1, tt, PR_COLS), lambda b, i: (b, i, 0)),
                pl.BlockSpec((1, 8, PR_COLS), lambda b, i: (b, jnp.maximum(i * (tt // 8) - 1, 0), 0)),
                vec(PR_COLS), vec(WIDTH), mat, vec(WIDTH), mat, vec(WIDTH), vec(WIDTH)]
    out_spec = pl.BlockSpec((1, tt, WIDTH), lambda b, i: (b, i, 0))
    return pl.pallas_call(
        body, name="rwkv_pre", grid=(bsz, s // tt), in_specs=in_specs, out_specs=[out_spec] * 6,
        out_shape=[SDS((bsz, s, WIDTH), F32)] * 6, compiler_params=_params(("parallel", "parallel")))(
            pr3, pr3, mix, w0, wup, a0, aup, kk_, ka_)


def _rwkv_pre_bwd(pr3, cots, mix, w0, wup, a0, aup, kk_, ka_):
    bsz, s, _ = pr3.shape
    tt = PRE_TT

    def body(pr_ref, edge_ref, c0, c1, c2, c3, c4, c5, mix_ref, w0_ref, wup_ref, a0_ref, aup_ref, kk_ref, ka_ref,
             dprs_ref, dmix_ref, dw0_ref, dwup_ref, da0_ref, daup_ref, dkk_ref, dka_ref):
        pr = pr_ref[0]
        prev = _shifted(pr_ref, edge_ref, pl.program_id(1) == 0, True)
        prs = pr + (prev - pr) * mix_ref[...]
        _, vjp = jax.vjp(_pre_fn, *_pre_pieces(prs), w0_ref[...], wup_ref[...].astype(F32), a0_ref[...], aup_ref[...].astype(F32),
                         kk_ref[...], ka_ref[...])
        grads = vjp(tuple(c[0] for c in (c0, c1, c2, c3, c4, c5)))
        for piece, a, b in zip(grads[:5], PRE_SPLIT[:-1], PRE_SPLIT[1:]):
            dprs_ref[0, :, a:b] = piece
        dw0, dwup, da0, daup, dkk, dka = grads[5:]
        dprs = dprs_ref[0]
        grads = (jnp.sum(dprs * (prev - pr), axis=0, keepdims=True), dw0, dwup, da0, daup, dkk, dka)
        refs = (dmix_ref, dw0_ref, dwup_ref, da0_ref, daup_ref, dkk_ref, dka_ref)
        first = jnp.logical_and(pl.program_id(0) == 0, pl.program_id(1) == 0)

        @pl.when(first)
        def _():
            for r_, g_ in zip(refs, grads):
                r_[...] = g_

        @pl.when(jnp.logical_not(first))
        def _():
            for r_, g_ in zip(refs, grads):
                r_[...] += g_

    vec = lambda n: pl.BlockSpec((1, n), lambda b, i: (0, 0))
    mat = pl.BlockSpec((LORA, WIDTH), lambda b, i: (0, 0))
    tile = pl.BlockSpec((1, tt, WIDTH), lambda b, i: (b, i, 0))
    in_specs = [pl.BlockSpec((1, tt, PR_COLS), lambda b, i: (b, i, 0)),
                pl.BlockSpec((1, 8, PR_COLS), lambda b, i: (b, jnp.maximum(i * (tt // 8) - 1, 0), 0))]
    in_specs += [tile] * 6 + [vec(PR_COLS), vec(WIDTH), mat, vec(WIDTH), mat, vec(WIDTH), vec(WIDTH)]
    out_specs = [pl.BlockSpec((1, tt, PR_COLS), lambda b, i: (b, i, 0)), vec(PR_COLS), vec(WIDTH), mat, vec(WIDTH), mat,
                 vec(WIDTH), vec(WIDTH)]
    out_shape = [SDS((bsz, s, PR_COLS), F32), SDS((1, PR_COLS), F32), SDS((1, WIDTH), F32), SDS((LORA, WIDTH), F32),
                 SDS((1, WIDTH), F32), SDS((LORA, WIDTH), F32), SDS((1, WIDTH), F32), SDS((1, WIDTH), F32)]
    return pl.pallas_call(
        body, name="rwkv_pre_bwd", grid=(bsz, s // tt), in_specs=in_specs, out_specs=out_specs, out_shape=out_shape,
        compiler_params=_params(("arbitrary", "arbitrary")))(pr3, pr3, *cots, mix, w0, wup, a0, aup, kk_, ka_)


def _shift_bwd(dprs3, mix):
    bsz, s, _ = dprs3.shape
    tt = PRE_TT
    nt = s // tt

    def body(d_ref, edge_ref, mix_ref, o_ref):
        nxt = _shifted(d_ref, edge_ref, pl.program_id(1) == nt - 1, False)
        m = mix_ref[...]
        o_ref[0] = (d_ref[0] * (1.0 - m) + nxt * m).astype(BF16)

    in_specs = [pl.BlockSpec((1, tt, PR_COLS), lambda b, i: (b, i, 0)),
                pl.BlockSpec((1, 8, PR_COLS), lambda b, i: (b, jnp.minimum((i + 1) * (tt // 8), s // 8 - 1), 0)),
                pl.BlockSpec((1, PR_COLS), lambda b, i: (0, 0))]
    return pl.pallas_call(
        body, name="shift_bwd", grid=(bsz, nt), in_specs=in_specs, out_specs=pl.BlockSpec((1, tt, PR_COLS), lambda b, i: (b, i, 0)),
        out_shape=SDS((bsz, s, PR_COLS), BF16), compiler_params=_params(("parallel", "parallel")))(dprs3, dprs3, mix)


_NN, _NT, _TN = ((2,), (1,)), ((2,), (2,)), ((1,), (1,))


def _dot3_bf16(a, b, dims):
    return lax.dot_general(a.astype(BF16), b.astype(BF16), (dims, ((0,), (0,))), preferred_element_type=F32)


class _Dots:
    def __init__(self, fwd):
        def make(dims, da_rule, db_rule):
            @jax.custom_vjp
            def f(a, b):
                return fwd(a, b, dims)

            f.defvjp(lambda a, b: (f(a, b), (a, b)), lambda res, g: (da_rule(*res, g), db_rule(*res, g)))
            return f

        one = _dot3_bf16
        self.mm = make(_NN, lambda a, b, g: one(g, b, _NT), lambda a, b, g: one(a, g, _TN))
        self.mm_nt = make(_NT, lambda a, b, g: one(g, b, _NN), lambda a, b, g: one(g, a, _TN))
        self.mm_tn = make(_TN, lambda a, b, g: one(b, g, _NT), lambda a, b, g: one(a, g, _NN))

        def powers(aab):
            ps = [aab]
            while 2 ** len(ps) < aab.shape[1]:
                ps.append(fwd(ps[-1], ps[-1], _NN))
            return ps

        def apply(ps, z, dims):
            for p in ps:
                z = z + fwd(p, z, dims)
            return z

        @jax.custom_vjp
        def solve(aab, z):
            return apply(powers(aab), z, _NN)

        def solve_fwd(aab, z):
            ps = powers(aab)
            x = apply(ps, z, _NN)
            return x, (ps, x)

        def solve_bwd(res, g):
            ps, x = res
            dz = apply(ps, g, _TN)
            return fwd(dz, x, _NT), dz

        solve.defvjp(solve_fwd, solve_bwd)
        self.solve = solve


_ONE_PASS = _Dots(_dot3_bf16)
_bmm, _bmm_tn = _ONE_PASS.mm, _ONE_PASS.mm_tn


def _chunk_fn(s0t, r, lw, k, v, kkraw, a, rk, lnw, lnb, first=False, d=_ONE_PASS):
    c = r.shape[1]
    at, rt, btc, ktc, gc, aab, arb, xv, arkv, ain, bin_ = _chunk_core(r, lw, k, v, kkraw, a, d)
    rs = d.mm(jnp.concatenate([at, rt], axis=1), s0t)
    u = d.solve(aab, rs[:, :c] + xv)
    y = rs[:, c:] + d.mm(arb, u) + arkv
    if first:
        y = _with_early_rows(y, r, lw, k, v, ain, bin_)
    gcol = jnp.sum(_diag(gc), axis=2, keepdims=True)
    sct = gcol * s0t + d.mm_tn(jnp.concatenate([btc, ktc], axis=1), jnp.concatenate([u, v], axis=1))
    return _post(y, r, k, v, rk, lnw, lnb), sct


def _diag(gc):
    return jnp.where(_masks(HEAD)[2], gc, 0.0)


def _with_early_rows(y, r, lw, k, v, ain, bin_):
    early = _early_rows(r[:2], lw[:2], k[:2], v[:2], ain[:2], bin_[:2])
    return jnp.concatenate([jnp.concatenate([early, y[:2, EARLY:]], axis=1), y[2:]], axis=0)


def _early_rows(r, lw, k, v, ain, bin_):
    cols = lambda x: _stack([jnp.transpose(x[h]) for h in range(2)])
    wc, bc, kc = cols(jnp.exp(lw)), cols(bin_), cols(k)
    st = jnp.zeros((2, HEAD, HEAD), F32)
    rows = []
    for t in range(EARLY):
        sa = _ONE_PASS.mm(ain[:, t:t + 1], st)
        st = st * wc[:, :, t:t + 1] + bc[:, :, t:t + 1] * sa + kc[:, :, t:t + 1] * v[:, t:t + 1]
        rows.append(_ONE_PASS.mm(r[:, t:t + 1], st))
    return jnp.concatenate(rows, axis=1)


def _chunk_rows(c):
    return pl.ds(c * CHUNK, CHUNK) if isinstance(c, int) else pl.ds(pl.multiple_of(c * CHUNK, CHUNK), CHUNK)


def _stack(xs):
    return jnp.concatenate([x[None] for x in xs], axis=0)


def _pairs(ref, chunks):
    tiles = [ref[0, _chunk_rows(c), :] for c in chunks]
    return _stack([t[:, HEAD * h:HEAD * h + HEAD] for t in tiles for h in range(2)])


def _unpair(vals, j):
    return jnp.concatenate([vals[2 * j], vals[2 * j + 1]], axis=1)


def _masks(c):
    ii = lax.broadcasted_iota(jnp.int32, (c, c), 0)
    jj = lax.broadcasted_iota(jnp.int32, (c, c), 1)
    return ii > jj, ii >= jj, ii == jj


@jax.custom_vjp
def _running_sum(lw):
    return _tri_dot(lw, _NN)


def _tri_dot(x, dims):
    g_, c, _ = x.shape
    tri = jnp.broadcast_to(_masks(c)[1].astype(BF16), (g_, c, c))
    head = x.astype(BF16)
    rest = (x - head.astype(F32)).astype(BF16)
    return lax.dot_general(tri, head, (dims, ((0,), (0,))), preferred_element_type=F32) + \
        lax.dot_general(tri, rest, (dims, ((0,), (0,))), preferred_element_type=F32)


_running_sum.defvjp(lambda lw: (_running_sum(lw), None), lambda _, ct: (_tri_dot(ct, _TN),))


def _chunk_core(r, lw, k, v, kkraw, a, d=_ONE_PASS):
    g_, c = r.shape[0], r.shape[1]
    nrm = jnp.sqrt(jnp.sum(kkraw * kkraw, axis=-1, keepdims=True))
    kkn = kkraw / jnp.maximum(nrm, 1e-12)
    ain, bin_ = -kkn, kkn * a
    strict, incl, _ = _masks(c)
    lg = _running_sum(lw)
    g, gp, gi = jnp.exp(lg), jnp.exp(lg - lw), jnp.exp(-lg)
    at, rt, bt, kt = ain * gp, r * g, bin_ * gi, k * gi
    aa = d.mm_nt(jnp.concatenate([at, rt], axis=1), jnp.concatenate([bt, kt], axis=1))
    aab = jnp.where(strict, aa[:, :c, :c], 0.0)
    aak = jnp.where(strict, aa[:, :c, c:], 0.0)
    arb = jnp.where(incl, aa[:, c:, :c], 0.0)
    ark = jnp.where(incl, aa[:, c:, c:], 0.0)
    akv = d.mm(jnp.concatenate([aak, ark], axis=1), v)
    gc = g[:, c - 1:c, :]
    return at, rt, bt * gc, kt * gc, gc, aab, arb, akv[:, :c], akv[:, c:], ain, bin_


def _post(y, r, k, v, rk, lnw, lnb):
    mu = jnp.mean(y, axis=-1, keepdims=True)
    var = jnp.mean(jnp.square(y - mu), axis=-1, keepdims=True)
    yn = (y - mu) * lax.rsqrt(var + GN_EPS) * lnw + lnb
    return yn + jnp.sum(r * k * rk, axis=-1, keepdims=True) * v


def _chunk_consts(r, lw, k, v, kkraw, a, first=False):
    d = _ONE_PASS
    at, rt, btc, ktc, gc, aab, arb, xv, arkv, ain, bin_ = _chunk_core(r, lw, k, v, kkraw, a, d)
    z = d.solve(aab, jnp.concatenate([at, xv], axis=2))
    ryv = jnp.concatenate([rt, arkv], axis=2) + d.mm(arb, z)
    if first:
        ryv = jnp.concatenate([ryv[:, :, :HEAD], _with_early_rows(ryv[:, :, HEAD:], r, lw, k, v, ain, bin_)], axis=2)
    mkv = d.mm_tn(btc, z) + jnp.concatenate([_diag(gc), d.mm_tn(ktc, v)], axis=2)
    return mkv, ryv


def _rwkv_scan(ins, rk, lnw, lnb):
    bsz, s, _ = ins[0].shape
    nch = s // CHUNK

    def consts_body(r_ref, lw_ref, k_ref, v_ref, kk_ref, a_ref, mkv_ref, ry_ref, yv_ref):
        def group(i, carry):
            chunks = [i * CHUNK_GROUP + j for j in range(CHUNK_GROUP)]
            mkv, ryv = _chunk_consts(*[_pairs(ref, chunks) for ref in (r_ref, lw_ref, k_ref, v_ref, kk_ref, a_ref)],
                                     first=isinstance(i, int) and i == 0)
            for j, c in enumerate(chunks):
                for h in range(2):
                    mkv_ref[0, 0, c, h] = mkv[2 * j + h]
                ry_ref[0, _chunk_rows(c), :] = jnp.concatenate([ryv[2 * j][:, :HEAD], ryv[2 * j + 1][:, :HEAD]], axis=1)
                yv_ref[0, _chunk_rows(c), :] = jnp.concatenate([ryv[2 * j][:, HEAD:], ryv[2 * j + 1][:, HEAD:]], axis=1)
            return carry

        group(0, 0)
        lax.fori_loop(1, nch // CHUNK_GROUP, group, 0)

    tile = pl.BlockSpec((1, s, LANE), lambda b, hp: (b, 0, hp))
    vec = pl.BlockSpec((1, LANE), lambda b, hp: (0, hp))
    mkv_spec = pl.BlockSpec((1, 1, nch, 2, HEAD, LANE), lambda b, hp: (b, hp, 0, 0, 0, 0))
    st_spec = pl.BlockSpec((1, 1, nch, 2, HEAD, HEAD), lambda b, hp: (b, hp, 0, 0, 0, 0))
    mkv, ry, yv = pl.pallas_call(
        consts_body, name="rwkv_consts", grid=(bsz, 4), in_specs=[tile] * 6, out_specs=[mkv_spec, tile, tile],
        out_shape=[SDS((bsz, 4, nch, 2, HEAD, LANE), F32), SDS((bsz, s, WIDTH), F32), SDS((bsz, s, WIDTH), F32)],
        compiler_params=_params(("parallel", "parallel")))(*ins)

    states = _chunk_recurrence(mkv, None, "rwkv_states")

    def out_body(ry_ref, yv_ref, r_ref, k_ref, v_ref, st_ref, rk_ref, lnw_ref, lnb_ref, o_ref):
        y, r, k, v, rk_, lnw_, lnb_ = _scan_rows(ry_ref, yv_ref, r_ref, k_ref, v_ref, st_ref, rk_ref, lnw_ref, lnb_ref)
        o = _post(y, r, k, v, rk_, lnw_, lnb_)
        for j in range(CHUNK_GROUP):
            o_ref[0, _chunk_rows(j), :] = _unpair(o, j)

    o = pl.pallas_call(
        out_body, name="rwkv_out", grid=(bsz, 4, nch // CHUNK_GROUP), in_specs=_group_specs(5), out_specs=_group_specs(1)[0],
        out_shape=SDS((bsz, s, WIDTH), F32),
        compiler_params=_params(("parallel", "parallel", "parallel")))(ry, yv, ins[0], ins[2], ins[3], states, rk, lnw, lnb)
    return o, states, (mkv, ry, yv)


def _group_specs(n_tiles):
    tile = pl.BlockSpec((1, CHUNK_GROUP * CHUNK, LANE), lambda b, hp, t: (b, t, hp))
    if n_tiles == 1:
        return [tile]
    st = pl.BlockSpec((1, 1, CHUNK_GROUP, 2, HEAD, HEAD), lambda b, hp, t: (b, hp, t, 0, 0, 0))
    vec = pl.BlockSpec((1, LANE), lambda b, hp, t: (0, hp))
    return [tile] * n_tiles + [st] + [vec] * 3


def _scan_rows(ry_ref, yv_ref, r_ref, k_ref, v_ref, st_ref, rk_ref, lnw_ref, lnb_ref):
    chunks = list(range(CHUNK_GROUP))
    ry, yv, r, k, v = (_pairs(ref, chunks) for ref in (ry_ref, yv_ref, r_ref, k_ref, v_ref))
    st = _stack([st_ref[0, 0, c, h] for c in chunks for h in range(2)])
    vecs = [_stack([ref[:, HEAD * h:HEAD * h + HEAD] for _ in chunks for h in range(2)]) for ref in (rk_ref, lnw_ref, lnb_ref)]
    return (_bmm(ry, st) + yv, r, k, v, *vecs)


def _chunk_recurrence(mkv, q, name):
    bsz, _, nch = mkv.shape[:3]
    pairs = [(hp, h) for hp in range(4) for h in range(2)]

    def body(*refs):
        mkv_ref, out_ref, acc = refs[0], refs[-2], refs[-1]
        acc[...] = jnp.zeros_like(acc)

        def step(i, carry):
            c = i if q is None else nch - 1 - i
            cur = acc[...]
            for j, (hp, h) in enumerate(pairs):
                out_ref[0, hp, c, h] = cur[j]
            m = _stack([mkv_ref[0, hp, c, h] for hp, h in pairs])
            if q is None:
                acc[...] = _bmm(m[:, :, :HEAD], cur) + m[:, :, HEAD:]
            else:
                acc[...] = _bmm_tn(m[:, :, :HEAD], cur) + _stack([refs[1][0, hp, c, h] for hp, h in pairs])
            return carry

        lax.fori_loop(0, nch, step, 0)

    spec = lambda w: pl.BlockSpec((1, 4, nch, 2, HEAD, w), lambda b: (b, 0, 0, 0, 0, 0))
    return pl.pallas_call(
        body, name=name, grid=(bsz,), in_specs=[spec(LANE)] + ([] if q is None else [spec(HEAD)]), out_specs=spec(HEAD),
        out_shape=SDS((bsz, 4, nch, 2, HEAD, HEAD), F32), scratch_shapes=[pltpu.VMEM((8, HEAD, HEAD), F32)],
        compiler_params=_params(("parallel",)))(*([mkv] if q is None else [mkv, q]))


def _rwkv_scan_bwd(ins, states, consts, do3, rk, lnw, lnb):
    bsz, s, _ = ins[0].shape
    nch = s // CHUNK

    mkv, ry, yv = consts

    def q_body(do_ref, ry_ref, yv_ref, r_ref, k_ref, v_ref, st_ref, rk_ref, lnw_ref, lnb_ref, q_ref):
        y, r, k, v, rk_, lnw_, lnb_ = _scan_rows(ry_ref, yv_ref, r_ref, k_ref, v_ref, st_ref, rk_ref, lnw_ref, lnb_ref)
        _, vjp = jax.vjp(lambda y_: _post(y_, r, k, v, rk_, lnw_, lnb_), y)
        (dy,) = vjp(_pairs(do_ref, list(range(CHUNK_GROUP))))
        q = _bmm_tn(_pairs(ry_ref, list(range(CHUNK_GROUP))), dy)
        for j in range(CHUNK_GROUP):
            for h in range(2):
                q_ref[0, 0, j, h] = q[2 * j + h]

    specs = _group_specs(6)
    q = pl.pallas_call(
        q_body, name="rwkv_q", grid=(bsz, 4, nch // CHUNK_GROUP), in_specs=specs, out_specs=specs[6],
        out_shape=SDS((bsz, 4, nch, 2, HEAD, HEAD), F32),
        compiler_params=_params(("parallel", "parallel", "parallel")))(do3, ry, yv, ins[0], ins[2], ins[3], states, rk, lnw, lnb)

    dstates = _chunk_recurrence(mkv, q, "rwkv_dstates")

    def body(r_ref, lw_ref, k_ref, v_ref, kk_ref, a_ref, st_ref, dst_ref, do_ref, rk_ref, lnw_ref, lnb_ref,
             dr_ref, dlw_ref, dk_ref, dv_ref, dkk_ref, da_ref, drk_ref, dlnw_ref, dlnb_ref):
        chunks = list(range(BWD_GROUP))
        par_refs = (drk_ref, dlnw_ref, dlnb_ref)

        @pl.when(jnp.logical_and(pl.program_id(1) == 0, pl.program_id(2) == 0))
        def _():
            for ref in par_refs:
                ref[...] = jnp.zeros_like(ref)

        def group(first):
            per_pair = lambda ref: _stack([ref[0, 0, c, h] for c in chunks for h in range(2)])
            vecs = [_stack([ref[:, HEAD * h:HEAD * h + HEAD] for _ in chunks for h in range(2)]) for ref in (rk_ref, lnw_ref, lnb_ref)]
            _, vjp = jax.vjp(functools.partial(_chunk_fn, first=first, d=_ONE_PASS), per_pair(st_ref),
                             *[_pairs(ref, chunks) for ref in (r_ref, lw_ref, k_ref, v_ref, kk_ref, a_ref)], *vecs)
            grads = vjp((_pairs(do_ref, chunks), per_pair(dst_ref)))
            for ref, cot in zip((dr_ref, dlw_ref, dk_ref, dv_ref, dkk_ref, da_ref), grads[1:7]):
                for j, c in enumerate(chunks):
                    ref[0, _chunk_rows(c), :] = _unpair(cot, j)
            for ref, g_ in zip(par_refs, grads[7:10]):
                ref[...] += jnp.concatenate([sum(g_[2 * j + h] for j in range(BWD_GROUP)) for h in range(2)], axis=1)

        pl.when(pl.program_id(2) == 0)(functools.partial(group, True))
        pl.when(pl.program_id(2) != 0)(functools.partial(group, False))

    tt = BWD_GROUP * CHUNK
    tile = pl.BlockSpec((1, tt, LANE), lambda hp, b, t: (b, t, hp))
    vec = pl.BlockSpec((1, LANE), lambda hp, b, t: (0, hp))
    st_spec = pl.BlockSpec((1, 1, BWD_GROUP, 2, HEAD, HEAD), lambda hp, b, t: (b, hp, t, 0, 0, 0))
    outs = pl.pallas_call(
        body, name="rwkv_scan_bwd", grid=(4, bsz, s // tt), in_specs=[tile] * 6 + [st_spec, st_spec, tile] + [vec] * 3,
        out_specs=[tile] * 6 + [vec] * 3,
        out_shape=[SDS((bsz, s, WIDTH), F32)] * 6 + [SDS((1, WIDTH), F32)] * 3,
        compiler_params=_params(("parallel", "arbitrary", "arbitrary")))(*ins, states, dstates, do3, rk, lnw, lnb)
    return outs[:6], outs[6:]


def _head(o_attn, o_rwkv, z_attn, z_rwkv, gm, x2, tgt, wua, wur, wout, g2):
    n = x2.shape[0]
    tm = 256
    nt = n // tm
    d = D_MODEL

    def body(oa_ref, or_ref, za_ref, zr_ref, gm_ref, x_ref, t_ref, wua_ref, wur_ref, wout_ref, g2_ref,
             dxo_ref, doa_ref, dor_ref, dza_ref, dzr_ref, dgm_ref, dwua_ref, dwur_ref, dwout_ref, dg2_ref, loss_ref, lacc):
        i = pl.program_id(0)
        oa, orw, za, zr = oa_ref[...], or_ref[...], za_ref[...], zr_ref[...]
        ga, gb = gm_ref[:, 0:d], gm_ref[:, d:2 * d]
        am = (oa * _silu(za)).astype(BF16)
        bm = (orw * _silu(zr)).astype(BF16)
        ya, yb = _dot(am, wua_ref[...]), _dot(bm, wur_ref[...])
        sa, sb = jax.nn.sigmoid(ga), jax.nn.sigmoid(gb)
        merged = (sa * ya + sb * yb).astype(BF16)
        out = _dot(merged, wout_ref[...])
        rs = lax.rsqrt(jnp.mean(out * out, axis=-1, keepdims=True) + RMS_EPS)
        g2 = g2_ref[...]
        err = x_ref[...] + out * rs * g2 - t_ref[...]
        lpart = jnp.sum(err * err, axis=0, keepdims=True)
        dxo = err * (1.0 / d)
        dxo_ref[...] = dxo
        dg2 = jnp.sum(dxo * out * rs, axis=0, keepdims=True)
        gd = dxo * g2
        dout = (rs * (gd - out * (rs * rs) * jnp.mean(gd * out, axis=-1, keepdims=True))).astype(BF16)
        dmerged = _dot_nt(dout, wout_ref[...])
        dwout = _dot_tn(merged, dout)
        dya, dyb = (dmerged * sa).astype(BF16), (dmerged * sb).astype(BF16)
        dgm_ref[:, 0:d] = (dmerged * ya * sa * (1.0 - sa)).astype(BF16)
        dgm_ref[:, d:2 * d] = (dmerged * yb * sb * (1.0 - sb)).astype(BF16)
        dam, dbm = _dot_nt(dya, wua_ref[...]), _dot_nt(dyb, wur_ref[...])
        dwua, dwur = _dot_tn(am, dya), _dot_tn(bm, dyb)
        doa_ref[...] = dam * _silu(za)
        dza_ref[...] = (dam * oa * _dsilu(za)).astype(BF16)
        dor_ref[...] = dbm * _silu(zr)
        dzr_ref[...] = (dbm * orw * _dsilu(zr)).astype(BF16)

        @pl.when(i == 0)
        def _():
            dwua_ref[...], dwur_ref[...], dwout_ref[...], dg2_ref[...], lacc[...] = dwua, dwur, dwout, dg2, lpart

        @pl.when(i != 0)
        def _():
            dwua_ref[...] += dwua
            dwur_ref[...] += dwur
            dwout_ref[...] += dwout
            dg2_ref[...] += dg2
            lacc[...] += lpart

        @pl.when(i == nt - 1)
        def _():
            loss_ref[...] = jnp.sum(lacc[...], axis=1, keepdims=True) * (0.5 / d)

    t512 = pl.BlockSpec((tm, WIDTH), lambda i: (i, 0))
    t1k = pl.BlockSpec((tm, d), lambda i: (i, 0))
    t2k = pl.BlockSpec((tm, 2 * d), lambda i: (i, 0))
    full = lambda r, c: pl.BlockSpec((r, c), lambda i: (0, 0))
    return pl.pallas_call(
        body, name="head_fwd_bwd", grid=(nt,),
        in_specs=[t512, t512, t512, t512, t2k, t1k, t1k, full(WIDTH, d), full(WIDTH, d), full(d, d), full(1, d)],
        out_specs=[t1k, t512, t512, t512, t512, t2k, full(WIDTH, d), full(WIDTH, d), full(d, d), full(1, d), full(1, 1)],
        out_shape=[SDS((n, d), F32), SDS((n, WIDTH), F32), SDS((n, WIDTH), F32), SDS((n, WIDTH), BF16), SDS((n, WIDTH), BF16),
                   SDS((n, 2 * d), BF16), SDS((WIDTH, d), F32), SDS((WIDTH, d), F32), SDS((d, d), F32), SDS((1, d), F32), SDS((1, 1), F32)],
        scratch_shapes=[pltpu.VMEM((1, d), F32)],
        compiler_params=_params(("arbitrary",)))(o_attn, o_rwkv, z_attn, z_rwkv, gm, x2, tgt, wua, wur, wout, g2)


def _prenorm_bwd(dh, x2, rs, g1, dxo):
    n, d = x2.shape
    tm = 1024

    def body(dh_ref, x_ref, rs_ref, g_ref, dxo_ref, gx_ref, dg_ref):
        x, r = x_ref[...], rs_ref[...]
        gd = dh_ref[...] * g_ref[...]
        gx_ref[...] = dxo_ref[...] + r * (gd - x * (r * r) * jnp.mean(gd * x, axis=-1, keepdims=True))
        dg = jnp.sum(dh_ref[...] * x * r, axis=0, keepdims=True)

        @pl.when(pl.program_id(0) == 0)
        def _():
            dg_ref[...] = dg

        @pl.when(pl.program_id(0) != 0)
        def _():
            dg_ref[...] += dg

    t = pl.BlockSpec((tm, d), lambda i: (i, 0))
    return pl.pallas_call(
        body, name="prenorm_bwd", grid=(n // tm,),
        in_specs=[t, t, pl.BlockSpec((tm, 1), lambda i: (i, 0)), pl.BlockSpec((1, d), lambda i: (0, 0)), t],
        out_specs=[t, pl.BlockSpec((1, d), lambda i: (0, 0))], out_shape=[SDS((n, d), F32), SDS((1, d), F32)],
        compiler_params=_params(("arbitrary",)))(dh, x2, rs, g1, dxo)


def _mesh_pos():
    x, y, c = lax.axis_index("x"), lax.axis_index("y"), lax.axis_index("c")
    return 4 * x + 2 * y + c


def _coords(idx):
    return (idx // 4, (idx // 2) % 2, idx % 2)


def _exchange(srcs, to_all, name):
    n = len(srcs)

    def body(*refs):
        src_refs, dst_refs = refs[:n], refs[n:2 * n]
        send_sems, recv_sems, local_sems = refs[2 * n:]
        me = _mesh_pos()

        def piece(i, j):
            return src_refs[i] if to_all[i] else src_refs[i].at[j]

        def remote(i, off, peer, block, slot):
            return pltpu.make_async_remote_copy(src_ref=piece(i, block), dst_ref=dst_refs[i].at[slot],
                                                send_sem=send_sems.at[i, off - 1], recv_sem=recv_sems.at[i, off - 1],
                                                device_id=_coords(peer), device_id_type=MESH)

        local = [pltpu.make_async_copy(piece(i, me), dst_refs[i].at[me], local_sems.at[i]) for i in range(n)]
        for cp in local:
            cp.start()
        sends = []
        for off in range(1, N_DEV):
            to = (me + off) % N_DEV
            for i in range(n):
                sends.append(remote(i, off, to, to, me))
                sends[-1].start()
        for off in range(1, N_DEV):
            frm = (me + N_DEV - off) % N_DEV
            for i in range(n):
                remote(i, off, frm, me, frm).wait_recv()
        for cp in sends:
            cp.wait_send()
        for cp in local:
            cp.wait()

    outs = pl.pallas_call(
        body, name=name, in_specs=[pl.BlockSpec(memory_space=pltpu.HBM)] * n, out_specs=[pl.BlockSpec(memory_space=pltpu.HBM)] * n,
        out_shape=[SDS((N_DEV,) + s.shape[-2:], s.dtype) for s in srcs],
        scratch_shapes=[pltpu.SemaphoreType.DMA((n, N_DEV - 1)), pltpu.SemaphoreType.DMA((n, N_DEV - 1)), pltpu.SemaphoreType.DMA((n,))],
        compiler_params=pltpu.CompilerParams())(*srcs)
    return outs


_HBM = pl.BlockSpec(memory_space=pltpu.HBM)
_SEM = pl.BlockSpec(memory_space=pltpu.SEMAPHORE)
_EFFECT = pltpu.SideEffectType.DATAFLOW_SIDE_EFFECTING


def _send_start(src):
    def body(src_ref, land_ref, send_sems, recv_sems, src_thru, land_thru, token):
        me = _mesh_pos()
        for off in range(1, N_DEV):
            to = (me + off) % N_DEV
            pltpu.make_async_remote_copy(src_ref=src_ref.at[to], dst_ref=land_ref.at[me], send_sem=send_sems.at[off - 1],
                                         recv_sem=recv_sems.at[off - 1], device_id=_coords(to), device_id_type=MESH).start()
        token[...] = jnp.zeros_like(token)

    hbm = pltpu.HBM(src.shape, src.dtype)
    return pl.pallas_call(
        body, name="grads_start",
        out_shape=(pltpu.SemaphoreType.DMA((N_DEV - 1,)), pltpu.SemaphoreType.DMA((N_DEV - 1,)), hbm, hbm, SDS((8, LANE), BF16)),
        in_specs=(_HBM, _HBM), out_specs=(_SEM, _SEM, _HBM, _HBM, pl.BlockSpec(memory_space=pltpu.VMEM)),
        input_output_aliases={0: 2, 1: 3}, compiler_params=pltpu.CompilerParams(has_side_effects=_EFFECT),
    )(pltpu.with_memory_space_constraint(src, pltpu.HBM), pltpu.with_memory_space_constraint(jnp.zeros(src.shape, src.dtype), pltpu.HBM))


def _send_wait(send_sems, recv_sems, src_thru, land_thru, after):
    def body(src_ref, land_ref, send_sems, recv_sems, after_ref, src_dead, got_ref):
        me = _mesh_pos()
        for off in range(1, N_DEV):
            to, frm = (me + off) % N_DEV, (me + N_DEV - off) % N_DEV
            pltpu.make_async_remote_copy(src_ref=src_ref.at[to], dst_ref=land_ref.at[me], send_sem=send_sems.at[off - 1],
                                         recv_sem=recv_sems.at[off - 1], device_id=_coords(to), device_id_type=MESH).wait_send()
            pltpu.make_async_remote_copy(src_ref=src_ref.at[me], dst_ref=land_ref.at[frm], send_sem=send_sems.at[off - 1],
                                         recv_sem=recv_sems.at[off - 1], device_id=_coords(frm), device_id_type=MESH).wait_recv()

    hbm = pltpu.HBM(src_thru.shape, src_thru.dtype)
    return pl.pallas_call(
        body, name="grads_wait", out_shape=(hbm, hbm), in_specs=(_HBM, _HBM, _SEM, _SEM, pl.BlockSpec(memory_space=pl.ANY)),
        out_specs=(_HBM, _HBM), input_output_aliases={0: 0, 1: 1}, compiler_params=pltpu.CompilerParams(has_side_effects=_EFFECT),
    )(src_thru, land_thru, send_sems, recv_sems, after)[1]


def _gather(srcs, name):
    n = len(srcs)

    def body(*refs):
        src_refs, dst_refs = refs[:n], refs[n:2 * n]
        send_sems, recv_sems, local_sems = refs[2 * n:]
        x, y, c = lax.axis_index("x"), lax.axis_index("y"), lax.axis_index("c")
        me, sibling = (x, y, c), (x, y, 1 - c)
        chips = [(1 - x, y), (x, 1 - y), (1 - x, 1 - y)]

        def slot(i, dev):
            return dst_refs[i].at[4 * dev[0] + 2 * dev[1] + dev[2]]

        def copy(i, k, block, to, own=False):
            return pltpu.make_async_remote_copy(src_ref=src_refs[i] if own else slot(i, block), dst_ref=slot(i, block),
                                                send_sem=send_sems.at[i, k], recv_sem=recv_sems.at[i, k],
                                                device_id=to, device_id_type=MESH)

        local = [pltpu.make_async_copy(src_refs[i], slot(i, me), local_sems.at[i]) for i in range(n)]
        for cp in local:
            cp.start()
        sends = []
        for i in range(n):
            sends.append(copy(i, 0, me, sibling, own=True))
            sends += [copy(i, 1 + j, me, (*chip, c), own=True) for j, chip in enumerate(chips)]
        for cp in sends:
            cp.start()
        for j, chip in enumerate(chips):
            for i in range(n):
                copy(i, 1 + j, (*chip, c), me).wait_recv()
                sends.append(copy(i, 4 + j, (*chip, c), sibling))
                sends[-1].start()
        for i in range(n):
            copy(i, 0, sibling, me).wait_recv()
            for j, chip in enumerate(chips):
                copy(i, 4 + j, (*chip, 1 - c), me).wait_recv()
        for cp in sends:
            cp.wait_send()
        for cp in local:
            cp.wait()

    return pl.pallas_call(
        body, name=name, in_specs=[pl.BlockSpec(memory_space=pltpu.HBM)] * n, out_specs=[pl.BlockSpec(memory_space=pltpu.HBM)] * n,
        out_shape=[SDS((N_DEV,) + s.shape, s.dtype) for s in srcs],
        scratch_shapes=[pltpu.SemaphoreType.DMA((n, N_DEV - 1)), pltpu.SemaphoreType.DMA((n, N_DEV - 1)), pltpu.SemaphoreType.DMA((n,))],
        compiler_params=pltpu.CompilerParams())(*srcs)


def _adamw(parts, w, m, v, tr, name, own=None):
    rows, cols = w.shape
    c1, c2 = 1.0 - ADAM_B1 ** ADAM_STEP, 1.0 - ADAM_B2 ** ADAM_STEP

    def body(p_ref, *refs):
        w_ref, m_ref, v_ref, g_ref, d_ref, nm_ref, nv_ref = refs[-7:]
        me = _mesh_pos()

        def part(j):
            return p_ref[j] if own is None else jnp.where(me == j, refs[0][...], p_ref[j])

        g = part(0).astype(F32)
        for j in range(1, N_DEV):
            g = g + part(j).astype(F32)
        nm = ADAM_B1 * m_ref[...] + (1.0 - ADAM_B1) * g
        nv = ADAM_B2 * v_ref[...] + (1.0 - ADAM_B2) * jnp.square(g)
        g_ref[...] = g
        nm_ref[...] = nm
        nv_ref[...] = nv
        d_ref[...] = -ADAM_LR * ((nm / c1) / (jnp.sqrt(nv / c2) + ADAM_EPS) + ADAM_WD * w_ref[...])

    t = pl.BlockSpec((tr, cols), lambda i: (i, 0))
    extra = [] if own is None else [own]
    return pl.pallas_call(
        body, name=name, grid=(rows // tr,), in_specs=[pl.BlockSpec((N_DEV, tr, cols), lambda i: (0, i, 0))] + [t] * (3 + len(extra)),
        out_specs=[t] * 4, out_shape=[SDS((rows, cols), F32)] * 4, compiler_params=_params(("parallel",)))(parts, *extra, w, m, v)


SHARDED = (("w_in", D_MODEL, IN_COLS // N_DEV, True, 128), ("w_up_attn", WIDTH, D_MODEL // N_DEV, True, WIDTH),
           ("w_up_rwkv", WIDTH, D_MODEL // N_DEV, True, WIDTH), ("w_out", D_MODEL // N_DEV, D_MODEL, False, D_MODEL // N_DEV),
           ("rwkv_w_up", LORA, WIDTH // N_DEV, True, LORA), ("rwkv_a_up", LORA, WIDTH // N_DEV, True, LORA))
LOSS_SLOT = sum(n for _, n in SMALL)


def _pack_small(small, extra=None):
    flat = [small[n].reshape(-1).astype(F32) for n, _ in SMALL]
    flat.append(jnp.zeros((1,), F32) if extra is None else extra.reshape(1))
    flat.append(jnp.zeros((SMALL_ROWS * LANE - LOSS_SLOT - 1,), F32))
    return jnp.concatenate(flat).reshape(SMALL_ROWS, LANE)


def _unpack_small(packed, shapes):
    flat = packed.reshape(-1)
    out, off = {}, 0
    for n, cnt in SMALL:
        out[n] = flat[off:off + cnt].reshape(shapes[n])
        off += cnt
    return out, flat[LOSS_SLOT]


def _whole(gathered, by_cols):
    if not by_cols:
        return gathered.reshape(-1, gathered.shape[-1])
    return gathered.transpose(1, 0, 2).reshape(gathered.shape[1], -1)


def _per_owner(full, by_cols):
    if not by_cols:
        return full.reshape(N_DEV, -1, full.shape[-1])
    return full.reshape(full.shape[0], N_DEV, -1).transpose(1, 0, 2)


def _local_step(x, loss_target, sm, wts):
    bsz, s, d = x.shape
    n = bsz * s
    x2, tgt = x.reshape(n, d), loss_target.reshape(n, d)
    bidx = jnp.asarray(_bucket_tables())
    w_in = wts["w_in"]
    segs = (("qkv", 0, QKV_COLS, 1536), ("za", OFF_ZA, WIDTH, 512), ("pr", OFF_PR, PR_COLS, PR_COLS), ("zr", OFF_ZR, WIDTH, 512),
            ("gm", OFF_GM, 2 * D_MODEL, 1024))

    h, rs = _prenorm(x2, sm["pre_norm_gain"])
    w_seg = {nm: w_in[:, off:off + cnt] for nm, off, cnt, _ in segs}
    proj = {nm: _mm(h, w_seg[nm], tn, "proj_" + nm) for nm, _, _, tn in segs}
    qkv3 = proj["qkv"].reshape(bsz, s, QKV_COLS)
    pr3 = proj["pr"].reshape(bsz, s, PR_COLS)

    o_attn, lse = _attn_fwd(qkv3, sm["rel_bias"], bidx)
    rk = sm["rwkv_r_k"].reshape(1, WIDTH)
    pre_args = (sm["rwkv_shift_mix"], sm["rwkv_w0"], wts["rwkv_w_up"], sm["rwkv_a0"], wts["rwkv_a_up"], sm["rwkv_k_k"], sm["rwkv_k_a"])
    scan_in = _rwkv_pre(pr3, *pre_args)
    o_rwkv, states, consts = _rwkv_scan(scan_in, rk, sm["rwkv_ln_w"], sm["rwkv_ln_b"])

    (dxo, do_attn, do_rwkv, dza, dzr, dgm, g_wua, g_wur, g_wout, g_post, loss) = _head(
        o_attn.reshape(n, WIDTH), o_rwkv.reshape(n, WIDTH), proj["za"], proj["zr"], proj["gm"], x2, tgt,
        wts["w_up_attn"], wts["w_up_rwkv"], wts["w_out"], sm["post_norm_gain"])

    dqkv, dbias = _attn_bwd(qkv3, o_attn, lse, do_attn.reshape(bsz, s, WIDTH), sm["rel_bias"], bidx)
    g_bias = _bias_grad(dbias, bidx)[:, :N_BUCKET].T

    scan_cots, (g_rk, g_lnw, g_lnb) = _rwkv_scan_bwd(scan_in, states, consts, do_rwkv.reshape(bsz, s, WIDTH), rk, sm["rwkv_ln_w"],
                                                     sm["rwkv_ln_b"])
    dprs, g_mix, g_w0, g_wup, g_a0, g_aup, g_kk, g_ka = _rwkv_pre_bwd(pr3, scan_cots, *pre_args)
    dpr = _shift_bwd(dprs, sm["rwkv_shift_mix"]).reshape(n, PR_COLS)

    dsegs = [(dqkv.reshape(9, n, WIDTH), 0, QKV_COLS, WIDTH), (dza, OFF_ZA, WIDTH, WIDTH), (dpr, OFF_PR, PR_COLS, PR_COLS),
             (dzr, OFF_ZR, WIDTH, WIDTH), (dgm, OFF_GM, 2 * D_MODEL, D_MODEL)]
    g_win = jnp.concatenate([_mm_tn(h, t, tn, "gw_in_%d" % j) for j, (t, _, _, tn) in enumerate(dsegs)], axis=1)
    blocks = _per_owner(g_win, True).astype(BF16)
    own = lax.dynamic_index_in_dim(blocks, 4 * lax.axis_index("x") + 2 * lax.axis_index("y") + lax.axis_index("c"), 0, keepdims=False)
    send_sems, recv_sems, blocks_thru, land_thru, token = _send_start(blocks)
    dh = _mm_nt(dsegs[0][0], w_seg["qkv"], token, "dh_qkv")
    dh = _mm_nt_multi([t for t, *_ in dsegs[1:]], [w_seg[nm] for nm in ("za", "pr", "zr", "gm")], dh, "dh_rest")
    grad_x, g_pre = _prenorm_bwd(dh, x2, rs, sm["pre_norm_gain"], dxo)
    landed = _send_wait(send_sems, recv_sems, blocks_thru, land_thru, g_pre)

    full = {"w_up_attn": g_wua, "w_up_rwkv": g_wur, "w_out": g_wout, "rwkv_w_up": g_wup, "rwkv_a_up": g_aup}
    small = {"pre_norm_gain": g_pre, "rel_bias": g_bias, "rwkv_shift_mix": g_mix, "rwkv_w0": g_w0, "rwkv_a0": g_a0, "rwkv_k_k": g_kk,
             "rwkv_k_a": g_ka, "rwkv_r_k": g_rk, "rwkv_ln_w": g_lnw, "rwkv_ln_b": g_lnb, "post_norm_gain": g_post}
    return loss[0, 0], grad_x.reshape(bsz, s, d), (landed, own), full, small


def kernel(x, pre_norm_gain, w_in, rel_bias, rwkv_shift_mix, rwkv_w0, rwkv_w_up, rwkv_a0, rwkv_a_up, rwkv_k_k, rwkv_k_a, rwkv_r_k, rwkv_ln_w, rwkv_ln_b, w_up_attn, w_up_rwkv, w_out, post_norm_gain, loss_target, m_pre_norm_gain, m_w_in, m_rel_bias, m_rwkv_shift_mix, m_rwkv_w0, m_rwkv_w_up, m_rwkv_a0, m_rwkv_a_up, m_rwkv_k_k, m_rwkv_k_a, m_rwkv_r_k, m_rwkv_ln_w, m_rwkv_ln_b, m_w_up_attn, m_w_up_rwkv, m_w_out, m_post_norm_gain, v_pre_norm_gain, v_w_in, v_rel_bias, v_rwkv_shift_mix, v_rwkv_w0, v_rwkv_w_up, v_rwkv_a0, v_rwkv_a_up, v_rwkv_k_k, v_rwkv_k_a, v_rwkv_r_k, v_rwkv_ln_w, v_rwkv_ln_b, v_w_up_attn, v_w_up_rwkv, v_w_out, v_post_norm_gain):
    names = [n for n, *_ in SHARDED] + [n for n, _ in SMALL]
    loc = dict(locals())
    w = {n: loc[n] for n in names}
    m = {n: loc["m_" + n] for n in names}
    v = {n: loc["v_" + n] for n in names}
    shapes = {n: w[n].shape for n in names}
    order = ["pre_norm_gain", "w_in", "rel_bias", "rwkv_shift_mix", "rwkv_w0", "rwkv_w_up", "rwkv_a0", "rwkv_a_up", "rwkv_k_k", "rwkv_k_a",
             "rwkv_r_k", "rwkv_ln_w", "rwkv_ln_b", "w_up_attn", "w_up_rwkv", "w_out", "post_norm_gain"]
    shard2d = lambda t, n, r, c: t[n].reshape(r, c)

    gathered = _gather([shard2d(w, n, r, c).astype(BF16) for n, r, c, _, _ in SHARDED], "gather_weights")
    wts = {n: _whole(g, by_cols) for (n, _, _, by_cols, _), g in zip(SHARDED, gathered)}

    loss, grad_x, (win_landed, win_own), full, small = _local_step(x, loss_target, w, wts)
    rest = SHARDED[1:]
    parts = _exchange([_per_owner(full[n], by_cols).astype(BF16) for n, _, _, by_cols, _ in rest] + [_pack_small(small, loss)],
                      [False] * len(rest) + [True], "exchange_grads")

    outs = [{}, {}, {}, {}]
    for (n, r, c, _, tr), p in zip(SHARDED, [win_landed] + list(parts)):
        res = _adamw(p, shard2d(w, n, r, c), shard2d(m, n, r, c), shard2d(v, n, r, c), tr, "adamw_" + n,
                     own=win_own if n == "w_in" else None)
        for o, t in zip(outs, res):
            o[n] = t.reshape(shapes[n])
    res = _adamw(parts[-1], _pack_small(w), _pack_small(m), _pack_small(v), SMALL_ROWS, "adamw_small")
    for o, t in zip(outs, res):
        o.update(_unpack_small(t, shapes)[0])
    loss = _unpack_small(res[0], shapes)[1]
    return (loss, grad_x, *[o[n] for o in outs for n in order])
```

```python
import functools
import math

import numpy as np
import jax
import jax.numpy as jnp
from jax import lax
from jax.experimental import pallas as pl
from jax.experimental.pallas import tpu as pltpu

F32, BF16 = jnp.float32, jnp.bfloat16
SDS = jax.ShapeDtypeStruct
MESH = pl.DeviceIdType.MESH

N_DEV = 8
D_MODEL = 1024
HEAD = 64
N_HEAD = 8
WIDTH = N_HEAD * HEAD
DILATIONS = (1, 4, 16)
QB = 128
N_BUCKET = 32
MAX_DIST = 2048
LORA = 64
QKV_COLS = 9 * WIDTH
PR_COLS = 3 * WIDTH + 2 * LORA
IN_COLS = QKV_COLS + WIDTH + PR_COLS + WIDTH + 2 * D_MODEL
OFF_ZA, OFF_PR, OFF_ZR, OFF_GM = QKV_COLS, QKV_COLS + WIDTH, QKV_COLS + WIDTH + PR_COLS, QKV_COLS + 2 * WIDTH + PR_COLS
RMS_EPS = 1e-6
GN_EPS = 64e-5
SCALE = 1.0 / math.sqrt(HEAD)
CHUNK = 64
CHUNK_GROUP = 8
BWD_GROUP = 16
EARLY = 8
NEG = -1e30
LANE = 128

ADAM_LR, ADAM_B1, ADAM_B2, ADAM_EPS, ADAM_WD, ADAM_STEP = 0.001, 0.9, 0.999, 1e-08, 0.01, 10

VMEM_LIMIT = 56 * 1024 * 1024

SMALL = (("pre_norm_gain", 1024), ("rel_bias", 768), ("rwkv_shift_mix", 1664), ("rwkv_w0", 512), ("rwkv_a0", 512),
         ("rwkv_k_k", 512), ("rwkv_k_a", 512), ("rwkv_r_k", 512), ("rwkv_ln_w", 512), ("rwkv_ln_b", 512),
         ("post_norm_gain", 1024))
SMALL_ROWS = 64


def _params(sem=None):
    return pltpu.CompilerParams(dimension_semantics=sem, vmem_limit_bytes=VMEM_LIMIT)


def _dot(a, b):
    return jnp.dot(a, b, preferred_element_type=F32)


def _dot_nt(a, b):
    return lax.dot_general(a, b, (((1,), (1,)), ((), ())), preferred_element_type=F32)


def _dot_tn(a, b):
    return lax.dot_general(a, b, (((0,), (0,)), ((), ())), preferred_element_type=F32)


@jax.custom_vjp
def _bdot(a, b):
    return _dot(a.astype(BF16), b.astype(BF16))


def _bdot_fwd(a, b):
    return _bdot(a, b), (a, b)


def _bdot_bwd(res, g):
    a, b = res
    gb = g.astype(BF16)
    return _dot_nt(gb, b.astype(BF16)), _dot_tn(a.astype(BF16), gb)


_bdot.defvjp(_bdot_fwd, _bdot_bwd)


def _silu(z):
    return z * jax.nn.sigmoid(z)


def _dsilu(z):
    s = jax.nn.sigmoid(z)
    return s * (1.0 + z * (1.0 - s))


def _softplus(x):
    return jnp.maximum(x, 0.0) + jnp.log(1.0 + jnp.exp(-jnp.abs(x)))


def _bucket_tables():
    qi = np.arange(QB)[:, None] + QB
    ki = np.arange(2 * QB)[None, :]
    rel = np.maximum(qi - ki, 0)
    out = []
    for d in DILATIONS:
        dist = rel * d
        max_exact = N_BUCKET // 2
        ratio = np.log(np.maximum(dist, 1).astype(np.float32) / max_exact) / np.float32(math.log(MAX_DIST / max_exact))
        large = max_exact + (ratio * (N_BUCKET - max_exact)).astype(np.int32)
        large = np.minimum(large, N_BUCKET - 1)
        out.append(np.where(dist < max_exact, dist, large).astype(np.int32))
    return np.stack(out)


def _prenorm(x2, g):
    n, d = x2.shape
    tm = 1024

    def body(x_ref, g_ref, h_ref, rs_ref):
        x = x_ref[...]
        rs = lax.rsqrt(jnp.mean(x * x, axis=-1, keepdims=True) + RMS_EPS)
        h_ref[...] = (x * rs * g_ref[...]).astype(BF16)
        rs_ref[...] = rs

    return pl.pallas_call(
        body, name="prenorm", grid=(n // tm,),
        in_specs=[pl.BlockSpec((tm, d), lambda i: (i, 0)), pl.BlockSpec((1, d), lambda i: (0, 0))],
        out_specs=[pl.BlockSpec((tm, d), lambda i: (i, 0)), pl.BlockSpec((tm, 1), lambda i: (i, 0))],
        out_shape=[SDS((n, d), BF16), SDS((n, 1), F32)], compiler_params=_params(("parallel",)))(x2, g)


def _mm(a, b, tn, name):
    m, k = a.shape
    n = b.shape[1]
    tm = 1024

    def body(a_ref, b_ref, o_ref):
        o_ref[...] = _dot(a_ref[...], b_ref[...])

    return pl.pallas_call(
        body, name=name, grid=(n // tn, m // tm),
        in_specs=[pl.BlockSpec((tm, k), lambda j, i: (i, 0)), pl.BlockSpec((k, tn), lambda j, i: (0, j))],
        out_specs=pl.BlockSpec((tm, tn), lambda j, i: (i, j)),
        out_shape=SDS((m, n), F32), compiler_params=_params(("parallel", "parallel")))(a, b)


def _mm_nt(a, b, after, name):
    m, seg = a.shape[1], a.shape[2]
    d, k = b.shape
    tm = 1024
    per = 3
    tk = per * seg

    def body(a_ref, b_ref, after_ref, o_ref):
        r = sum(_dot_nt(a_ref[j].astype(BF16), b_ref[:, seg * j:seg * (j + 1)]) for j in range(per))

        @pl.when(pl.program_id(1) == 0)
        def _():
            o_ref[...] = r

        @pl.when(pl.program_id(1) != 0)
        def _():
            o_ref[...] += r

    in_specs = [pl.BlockSpec((per, tm, seg), lambda i, j: (j, i, 0)), pl.BlockSpec((d, tk), lambda i, j: (0, j)),
                pl.BlockSpec(after.shape, lambda i, j: (0, 0))]
    return pl.pallas_call(
        body, name=name, grid=(m // tm, k // tk), in_specs=in_specs, out_specs=pl.BlockSpec((tm, d), lambda i, j: (i, 0)),
        out_shape=SDS((m, d), F32), compiler_params=_params(("parallel", "arbitrary")))(a, b, after)


def _mm_nt_multi(a_list, b_list, acc, name):
    m, d = acc.shape
    tm = 512
    n = len(a_list)

    def body(*refs):
        r = refs[2 * n][...]
        for a_ref, b_ref in zip(refs[:n], refs[n:2 * n]):
            r = r + _dot_nt(a_ref[...].astype(BF16), b_ref[...])
        refs[2 * n + 1][...] = r

    in_specs = [pl.BlockSpec((tm, a.shape[1]), lambda i: (i, 0)) for a in a_list]
    in_specs += [pl.BlockSpec(b.shape, lambda i: (0, 0)) for b in b_list]
    in_specs.append(pl.BlockSpec((tm, d), lambda i: (i, 0)))
    return pl.pallas_call(
        body, name=name, grid=(m // tm,), in_specs=in_specs, out_specs=pl.BlockSpec((tm, d), lambda i: (i, 0)),
        out_shape=SDS((m, d), F32), compiler_params=_params(("parallel",)))(*a_list, *b_list, acc)


def _mm_tn(a, b, tn, name):
    split = b.ndim == 3
    m, k1 = a.shape
    per = 3 if split else 1
    seg = b.shape[2] if split else tn
    tn = per * seg
    n2 = b.shape[0] * seg if split else b.shape[1]
    tm = 1024

    def body(a_ref, b_ref, o_ref):
        first = pl.program_id(1) == 0
        for j in range(per):
            r = _dot_tn(a_ref[...], (b_ref[j] if split else b_ref[...]).astype(BF16))
            cols = slice(seg * j, seg * (j + 1))

            @pl.when(first)
            def _(r=r, cols=cols):
                o_ref[:, cols] = r

            @pl.when(jnp.logical_not(first))
            def _(r=r, cols=cols):
                o_ref[:, cols] += r

    b_spec = pl.BlockSpec((per, tm, seg), lambda j, i: (j, i, 0)) if split else pl.BlockSpec((tm, tn), lambda j, i: (i, j))
    return pl.pallas_call(
        body, name=name, grid=(n2 // tn, m // tm),
        in_specs=[pl.BlockSpec((tm, k1), lambda j, i: (i, 0)), b_spec],
        out_specs=pl.BlockSpec((k1, tn), lambda j, i: (0, j)),
        out_shape=SDS((k1, n2), F32), compiler_params=_params(("parallel", "arbitrary")))(a, b)


def _ds(start, d):
    return pl.ds(start, QB) if d == 1 else pl.ds(start, QB, stride=d)


def _fill_bias(tab_ref, bidx_ref, bias_sc, hp):
    for g in range(3):
        bi = bidx_ref[g]
        for h in range(2):
            acc = jnp.zeros((QB, 2 * QB), F32)
            for j in range(N_BUCKET):
                acc = jnp.where(bi == j, tab_ref[j, g * N_HEAD + hp * 2 + h], acc)
            bias_sc[g * 2 + h] = acc


def _block_starts(it, d, nb):
    rho = it // nb
    n = it % nb
    st = rho + d * QB * n
    stp = rho + d * QB * jnp.maximum(n - 1, 0)
    if d == 1:
        st, stp = pl.multiple_of(QB * it, QB), pl.multiple_of(QB * jnp.maximum(it - 1, 0), QB)
    return st, stp, n > 0


ATTN_BLOCKS = 4


def _bdot3(a, b, dims):
    return lax.dot_general(a, b, (dims, ((0,), (0,))), preferred_element_type=F32)


def _attn_operands(q_ref, k_ref, v_ref, bias_sc, g, d, nb, it0):
    two = nb > 1
    nk = 2 * QB if two else QB
    ii = lax.broadcasted_iota(jnp.int32, (QB, nk), 0)
    cc = lax.broadcasted_iota(jnp.int32, (QB, nk), 1)
    qs, ks, vs, pens, starts = [], [], [], [], []
    for u in range(ATTN_BLOCKS):
        st, stp, hasprev = _block_starts(it0 + u, d, nb)
        qf = q_ref[0, _ds(st, d), :]
        if two:
            kf = jnp.concatenate([k_ref[0, _ds(stp, d), :], k_ref[0, _ds(st, d), :]], axis=0).astype(BF16)
            vf = jnp.concatenate([v_ref[0, _ds(stp, d), :], v_ref[0, _ds(st, d), :]], axis=0).astype(BF16)
            own = jnp.logical_and(cc >= QB, ii >= cc - QB)
            prev = jnp.logical_and(jnp.logical_and(cc < QB, cc >= ii), hasprev)
            pen = jnp.where(jnp.logical_or(own, prev), 0.0, NEG)
        else:
            kf, vf = k_ref[0, _ds(st, d), :].astype(BF16), v_ref[0, _ds(st, d), :].astype(BF16)
            pen = jnp.where(ii >= cc, 0.0, NEG)
        for h in range(2):
            qs.append(_one_head(qf, h).astype(BF16))
            ks.append(kf)
            vs.append(vf)
            pens.append(pen + (bias_sc[g * 2 + h] if two else bias_sc[g * 2 + h, :, QB:2 * QB]))
        starts.append((st, stp))
    return _stack(qs), _stack(ks), _stack(vs), _stack(pens), starts


def _one_head(x, h):
    lane = lax.broadcasted_iota(jnp.int32, x.shape, 1)
    return jnp.where(lane >= HEAD if h == 1 else lane < HEAD, x, 0.0)


def _pick_heads(x, u):
    lane = lax.broadcasted_iota(jnp.int32, x.shape[1:], 1)
    return jnp.where(lane < HEAD, x[2 * u], x[2 * u + 1])


def _add_heads(x, u):
    return x[2 * u] + x[2 * u + 1]


def _attn_fwd(qkv3, rel_bias, bidx):
    bsz, s, _ = qkv3.shape
    rt = 256

    def body(tab_ref, bidx_ref, *refs):
        q_refs, k_refs, v_refs = refs[0:3], refs[3:6], refs[6:9]
        o_ref, lse_ref = refs[9:11]
        bias_sc, num_sc, den_sc, m_sc = refs[11:]
        pl.when(pl.program_id(1) == 0)(lambda: _fill_bias(tab_ref, bidx_ref, bias_sc, pl.program_id(0)))
        for g, d in enumerate(DILATIONS):
            nb = s // (QB * d)

            def blk(it, c, g=g, d=d, nb=nb):
                q, k, v, bias, starts = _attn_operands(q_refs[g], k_refs[g], v_refs[g], bias_sc, g, d, nb, it * ATTN_BLOCKS)
                sc = _bdot3(q, k, ((2,), (2,))) * SCALE + bias
                m = jnp.max(sc, axis=-1, keepdims=True)
                p = jnp.exp(sc - m)
                den = jnp.sum(p, axis=-1, keepdims=True)
                num = _bdot3(p.astype(BF16), v, ((2,), (1,)))
                den, m = jnp.broadcast_to(den, num.shape), jnp.broadcast_to(m, num.shape)
                for u, (st, _) in enumerate(starts):
                    num_sc[g, _ds(st, d), :] = _pick_heads(num, u)
                    den_sc[g, _ds(st, d), :] = _pick_heads(den, u)
                    m_sc[g, _ds(st, d), :] = _pick_heads(m, u)
                return c

            lax.fori_loop(0, s // QB // ATTN_BLOCKS, blk, 0)

        def merge(i, c):
            rows = pl.ds(pl.multiple_of(i * rt, rt), rt)
            m0, m1, m2 = m_sc[0, rows, :], m_sc[1, rows, :], m_sc[2, rows, :]
            mall = jnp.maximum(jnp.maximum(m0, m1), m2)
            w0, w1, w2 = jnp.exp(m0 - mall), jnp.exp(m1 - mall), jnp.exp(m2 - mall)
            num = w0 * num_sc[0, rows, :] + w1 * num_sc[1, rows, :] + w2 * num_sc[2, rows, :]
            den = w0 * den_sc[0, rows, :] + w1 * den_sc[1, rows, :] + w2 * den_sc[2, rows, :]
            o_ref[0, rows, :] = num / den
            lse_ref[0, rows, :] = mall + jnp.log(den)
            return c

        lax.fori_loop(0, s // rt, merge, 0)

    col = lambda w, g: (lambda hp, b: (b, 0, (w * 3 + g) * 4 + hp))
    in_specs = [pl.BlockSpec(memory_space=pltpu.SMEM), pl.BlockSpec((3, QB, 2 * QB), lambda hp, b: (0, 0, 0))]
    in_specs += [pl.BlockSpec((1, s, LANE), col(w, g)) for w in range(3) for g in range(3)]
    out_spec = pl.BlockSpec((1, s, LANE), lambda hp, b: (b, 0, hp))
    return pl.pallas_call(
        body, name="attn_fwd", grid=(4, bsz), in_specs=in_specs, out_specs=[out_spec, out_spec],
        out_shape=[SDS((bsz, s, WIDTH), F32), SDS((bsz, s, WIDTH), F32)],
        scratch_shapes=[pltpu.VMEM((6, QB, 2 * QB), F32), pltpu.VMEM((3, s, LANE), F32), pltpu.VMEM((3, s, LANE), F32),
                        pltpu.VMEM((3, s, LANE), F32)],
        compiler_params=_params(("arbitrary", "arbitrary")))(rel_bias, bidx, *([qkv3] * 9))


def _attn_bwd(qkv3, o3, lse3, do3, rel_bias, bidx):
    bsz, s, _ = qkv3.shape
    rt = 256

    def body(tab_ref, bidx_ref, *refs):
        q_refs, k_refs, v_refs = refs[0:3], refs[3:6], refs[6:9]
        o_ref, lse_ref, do_ref, dqkv_ref, db_ref, bias_sc, delta_sc, acc_sc = refs[9:]
        dq_refs, dk_refs, dv_refs = ([acc_sc.at[w * 3 + g] for g in range(3)] for w in range(3))

        @pl.when(pl.program_id(1) == 0)
        def _():
            _fill_bias(tab_ref, bidx_ref, bias_sc, pl.program_id(0))
            db_ref[...] = jnp.zeros_like(db_ref)

        def prep(i, c):
            rows = pl.ds(pl.multiple_of(i * rt, rt), rt)
            prod = do_ref[0, rows, :] * o_ref[0, rows, :]
            d0 = jnp.sum(prod[:, :HEAD], axis=-1, keepdims=True)
            d1 = jnp.sum(prod[:, HEAD:], axis=-1, keepdims=True)
            delta_sc[rows, :] = jnp.concatenate([jnp.broadcast_to(d0, (rt, HEAD)), jnp.broadcast_to(d1, (rt, HEAD))], axis=1)
            z = jnp.zeros((rt, LANE), F32)
            for g in range(3):
                dk_refs[g][0, rows, :] = z
                dv_refs[g][0, rows, :] = z
            return c

        lax.fori_loop(0, s // rt, prep, 0)
        for g, d in enumerate(DILATIONS):
            nb = s // (QB * d)

            def blk(it, c, g=g, d=d, nb=nb):
                q, k, v, bias, starts = _attn_operands(q_refs[g], k_refs[g], v_refs[g], bias_sc, g, d, nb, it * ATTN_BLOCKS)
                dos, lses, deltas = [], [], []
                for st, _ in starts:
                    dof, lsef, delf = do_ref[0, _ds(st, d), :], lse_ref[0, _ds(st, d), :], delta_sc[_ds(st, d), :]
                    for h in range(2):
                        dos.append(_one_head(dof, h).astype(BF16))
                        lses.append(lsef[:, HEAD * h:HEAD * h + 1])
                        deltas.append(delf[:, HEAD * h:HEAD * h + 1])
                do, lse, delta = _stack(dos), _stack(lses), _stack(deltas)
                p = jnp.exp(_bdot3(q, k, ((2,), (2,))) * SCALE + bias - lse)
                dv = _bdot3(p.astype(BF16), do, ((1,), (1,)))
                ds = p * (_bdot3(do, v, ((2,), (2,))) - delta)
                dsb = ds.astype(BF16)
                dq = _bdot3(dsb, k, ((2,), (1,))) * SCALE
                dk = _bdot3(dsb, q, ((1,), (1,))) * SCALE
                two = nb > 1
                for h in range(2):
                    dsum = sum(ds[2 * u + h] for u in range(ATTN_BLOCKS))
                    if two:
                        db_ref[0, g * 2 + h] += dsum
                    else:
                        db_ref[0, g * 2 + h, :, QB:2 * QB] += dsum
                for u, (st, stp) in enumerate(starts):
                    dq_refs[g][0, _ds(st, d), :] = _pick_heads(dq, u)
                    if two:
                        dk_refs[g][0, _ds(stp, d), :] += _add_heads(dk[:, :QB], u)
                        dv_refs[g][0, _ds(stp, d), :] += _add_heads(dv[:, :QB], u)
                    dk_refs[g][0, _ds(st, d), :] += _add_heads(dk[:, QB:] if two else dk, u)
                    dv_refs[g][0, _ds(st, d), :] += _add_heads(dv[:, QB:] if two else dv, u)
                return c

            lax.fori_loop(0, s // QB // ATTN_BLOCKS, blk, 0)

        def flush(i, c):
            rows = pl.ds(pl.multiple_of(i * rt, rt), rt)
            for j in range(9):
                dqkv_ref[j, 0, rows, :] = acc_sc[j, 0, rows, :].astype(BF16)
            return c

        lax.fori_loop(0, s // rt, flush, 0)

    col = lambda w, g: (lambda hp, b: (b, 0, (w * 3 + g) * 4 + hp))
    blk_spec = pl.BlockSpec((1, s, LANE), lambda hp, b: (b, 0, hp))
    in_specs = [pl.BlockSpec(memory_space=pltpu.SMEM), pl.BlockSpec((3, QB, 2 * QB), lambda hp, b: (0, 0, 0))]
    in_specs += [pl.BlockSpec((1, s, LANE), col(w, g)) for w in range(3) for g in range(3)]
    in_specs += [blk_spec] * 3
    out_specs = [pl.BlockSpec((9, 1, s, LANE), lambda hp, b: (0, b, 0, hp)), pl.BlockSpec((1, 6, QB, 2 * QB), lambda hp, b: (hp, 0, 0, 0))]
    out_shape = [SDS((9, bsz, s, WIDTH), BF16), SDS((4, 6, QB, 2 * QB), F32)]
    return pl.pallas_call(
        body, name="attn_bwd", grid=(4, bsz), in_specs=in_specs, out_specs=out_specs, out_shape=out_shape,
        scratch_shapes=[pltpu.VMEM((6, QB, 2 * QB), F32), pltpu.VMEM((s, LANE), F32), pltpu.VMEM((9, 1, s, LANE), F32)],
        compiler_params=_params(("parallel", "arbitrary")))(rel_bias, bidx, *([qkv3] * 9), o3, lse3, do3)


def _bias_grad(dbias, bidx):
    def body(db_ref, bidx_ref, o_ref):
        lane = lax.broadcasted_iota(jnp.int32, (1, LANE), 1)
        for g in range(3):
            bi = bidx_ref[g]
            for hp in range(4):
                for h in range(2):
                    mat = db_ref[hp, g * 2 + h]
                    row = jnp.zeros((1, LANE), F32)
                    for j in range(N_BUCKET):
                        part = jnp.sum(jnp.where(bi == j, mat, 0.0), axis=0, keepdims=True)
                        row = jnp.where(lane == j, jnp.sum(part, axis=1, keepdims=True), row)
                    hd = g * N_HEAD + hp * 2 + h
                    o_ref[hd:hd + 1, :] = row

    return pl.pallas_call(body, name="bias_grad", out_shape=SDS((3 * N_HEAD, LANE), F32), compiler_params=_params())(dbias, bidx)


def _pre_fn(r, k0, v, wl, al, w0, wup, a0, aup, kk_, ka_):
    u = w0 + _bdot(jnp.tanh(wl), wup)
    lw = -jnp.exp(-_softplus(-u) - 0.5)
    a = jax.nn.sigmoid(a0 + _bdot(al, aup))
    kkraw = k0 * kk_
    k = k0 * (1.0 + (a - 1.0) * ka_)
    return r, lw, k, v, kkraw, a


PRE_SPLIT = (0, WIDTH, 2 * WIDTH, 3 * WIDTH, 3 * WIDTH + LORA, 3 * WIDTH + 2 * LORA)


def _pre_pieces(prs):
    return [prs[:, a:b] for a, b in zip(PRE_SPLIT[:-1], PRE_SPLIT[1:])]


PRE_TT = 512


def _shifted(pr_ref, edge_ref, first, back):
    pr = pr_ref[0]
    tt = pr.shape[0]
    row = lax.broadcasted_iota(jnp.int32, (tt, 1), 0)
    if back:
        edge = jnp.where(first, 0.0, edge_ref[0, 7:8, :])
        return jnp.where(row == 0, edge, pltpu.roll(pr, 1, axis=0))
    edge = jnp.where(first, 0.0, edge_ref[0, 0:1, :])
    return jnp.where(row == tt - 1, edge, pltpu.roll(pr, tt - 1, axis=0))


def _rwkv_pre(pr3, mix, w0, wup, a0, aup, kk_, ka_):
    bsz, s, _ = pr3.shape
    tt = PRE_TT

    def body(pr_ref, edge_ref, mix_ref, w0_ref, wup_ref, a0_ref, aup_ref, kk_ref, ka_ref, *outs):
        pr = pr_ref[0]
        prev = _shifted(pr_ref, edge_ref, pl.program_id(1) == 0, True)
        prs = pr + (prev - pr) * mix_ref[...]
        vals = _pre_fn(*_pre_pieces(prs), w0_ref[...], wup_ref[...].astype(F32), a0_ref[...], aup_ref[...].astype(F32), kk_ref[...],
                       ka_ref[...])
        for o, val in zip(outs, vals):
            o[0] = val

    vec = lambda n: pl.BlockSpec((1, n), lambda b, i: (0, 0))
    mat = pl.BlockSpec((LORA, WIDTH), lambda b, i: (0, 0))
    in_specs = [pl.BlockSpec((1, tt, PR_COLS), lambda b, i: (b, i, 0)),
                pl.BlockSpec((1, 8, PR_COLS), lambda b, i: (b, jnp.maximum(i * (tt // 8) - 1, 0), 0)),
                vec(PR_COLS), vec(WIDTH), mat, vec(WIDTH), mat, vec(WIDTH), vec(WIDTH)]
    out_spec = pl.BlockSpec((1, tt, WIDTH), lambda b, i: (b, i, 0))
    return pl.pallas_call(
        body, name="rwkv_pre", grid=(bsz, s // tt), in_specs=in_specs, out_specs=[out_spec] * 6,
        out_shape=[SDS((bsz, s, WIDTH), F32)] * 6, compiler_params=_params(("parallel", "parallel")))(
            pr3, pr3, mix, w0, wup, a0, aup, kk_, ka_)


def _rwkv_pre_bwd(pr3, cots, mix, w0, wup, a0, aup, kk_, ka_):
    bsz, s, _ = pr3.shape
    tt = PRE_TT

    def body(pr_ref, edge_ref, c0, c1, c2, c3, c4, c5, mix_ref, w0_ref, wup_ref, a0_ref, aup_ref, kk_ref, ka_ref,
             dprs_ref, dmix_ref, dw0_ref, dwup_ref, da0_ref, daup_ref, dkk_ref, dka_ref):
        pr = pr_ref[0]
        prev = _shifted(pr_ref, edge_ref, pl.program_id(1) == 0, True)
        prs = pr + (prev - pr) * mix_ref[...]
        _, vjp = jax.vjp(_pre_fn, *_pre_pieces(prs), w0_ref[...], wup_ref[...].astype(F32), a0_ref[...], aup_ref[...].astype(F32),
                         kk_ref[...], ka_ref[...])
        grads = vjp(tuple(c[0] for c in (c0, c1, c2, c3, c4, c5)))
        for piece, a, b in zip(grads[:5], PRE_SPLIT[:-1], PRE_SPLIT[1:]):
            dprs_ref[0, :, a:b] = piece
        dw0, dwup, da0, daup, dkk, dka = grads[5:]
        dprs = dprs_ref[0]
        grads = (jnp.sum(dprs * (prev - pr), axis=0, keepdims=True), dw0, dwup, da0, daup, dkk, dka)
        refs = (dmix_ref, dw0_ref, dwup_ref, da0_ref, daup_ref, dkk_ref, dka_ref)
        first = jnp.logical_and(pl.program_id(0) == 0, pl.program_id(1) == 0)

        @pl.when(first)
        def _():
            for r_, g_ in zip(refs, grads):
                r_[...] = g_

        @pl.when(jnp.logical_not(first))
        def _():
            for r_, g_ in zip(refs, grads):
                r_[...] += g_

    vec = lambda n: pl.BlockSpec((1, n), lambda b, i: (0, 0))
    mat = pl.BlockSpec((LORA, WIDTH), lambda b, i: (0, 0))
    tile = pl.BlockSpec((1, tt, WIDTH), lambda b, i: (b, i, 0))
    in_specs = [pl.BlockSpec((1, tt, PR_COLS), lambda b, i: (b, i, 0)),
                pl.BlockSpec((1, 8, PR_COLS), lambda b, i: (b, jnp.maximum(i * (tt // 8) - 1, 0), 0))]
    in_specs += [tile] * 6 + [vec(PR_COLS), vec(WIDTH), mat, vec(WIDTH), mat, vec(WIDTH), vec(WIDTH)]
    out_specs = [pl.BlockSpec((1, tt, PR_COLS), lambda b, i: (b, i, 0)), vec(PR_COLS), vec(WIDTH), mat, vec(WIDTH), mat,
                 vec(WIDTH), vec(WIDTH)]
    out_shape = [SDS((bsz, s, PR_COLS), F32), SDS((1, PR_COLS), F32), SDS((1, WIDTH), F32), SDS((LORA, WIDTH), F32),
                 SDS((1, WIDTH), F32), SDS((LORA, WIDTH), F32), SDS((1, WIDTH), F32), SDS((1, WIDTH), F32)]
    return pl.pallas_call(
        body, name="rwkv_pre_bwd", grid=(bsz, s // tt), in_specs=in_specs, out_specs=out_specs, out_shape=out_shape,
        compiler_params=_params(("arbitrary", "arbitrary")))(pr3, pr3, *cots, mix, w0, wup, a0, aup, kk_, ka_)


def _shift_bwd(dprs3, mix):
    bsz, s, _ = dprs3.shape
    tt = PRE_TT
    nt = s // tt

    def body(d_ref, edge_ref, mix_ref, o_ref):
        nxt = _shifted(d_ref, edge_ref, pl.program_id(1) == nt - 1, False)
        m = mix_ref[...]
        o_ref[0] = (d_ref[0] * (1.0 - m) + nxt * m).astype(BF16)

    in_specs = [pl.BlockSpec((1, tt, PR_COLS), lambda b, i: (b, i, 0)),
                pl.BlockSpec((1, 8, PR_COLS), lambda b, i: (b, jnp.minimum((i + 1) * (tt // 8), s // 8 - 1), 0)),
                pl.BlockSpec((1, PR_COLS), lambda b, i: (0, 0))]
    return pl.pallas_call(
        body, name="shift_bwd", grid=(bsz, nt), in_specs=in_specs, out_specs=pl.BlockSpec((1, tt, PR_COLS), lambda b, i: (b, i, 0)),
        out_shape=SDS((bsz, s, PR_COLS), BF16), compiler_params=_params(("parallel", "parallel")))(dprs3, dprs3, mix)


_NN, _NT, _TN = ((2,), (1,)), ((2,), (2,)), ((1,), (1,))


def _dot3_bf16(a, b, dims):
    return lax.dot_general(a.astype(BF16), b.astype(BF16), (dims, ((0,), (0,))), preferred_element_type=F32)


class _Dots:
    def __init__(self, fwd):
        def make(dims, da_rule, db_rule):
            @jax.custom_vjp
            def f(a, b):
                return fwd(a, b, dims)

            f.defvjp(lambda a, b: (f(a, b), (a, b)), lambda res, g: (da_rule(*res, g), db_rule(*res, g)))
            return f

        one = _dot3_bf16
        self.mm = make(_NN, lambda a, b, g: one(g, b, _NT), lambda a, b, g: one(a, g, _TN))
        self.mm_nt = make(_NT, lambda a, b, g: one(g, b, _NN), lambda a, b, g: one(g, a, _TN))
        self.mm_tn = make(_TN, lambda a, b, g: one(b, g, _NT), lambda a, b, g: one(a, g, _NN))

        def powers(aab):
            ps = [aab]
            while 2 ** len(ps) < aab.shape[1]:
                ps.append(fwd(ps[-1], ps[-1], _NN))
            return ps

        def apply(ps, z, dims):
            for p in ps:
                z = z + fwd(p, z, dims)
            return z

        @jax.custom_vjp
        def solve(aab, z):
            return apply(powers(aab), z, _NN)

        def solve_fwd(aab, z):
            ps = powers(aab)
            x = apply(ps, z, _NN)
            return x, (ps, x)

        def solve_bwd(res, g):
            ps, x = res
            dz = apply(ps, g, _TN)
            return fwd(dz, x, _NT), dz

        solve.defvjp(solve_fwd, solve_bwd)
        self.solve = solve


_ONE_PASS = _Dots(_dot3_bf16)
_bmm, _bmm_tn = _ONE_PASS.mm, _ONE_PASS.mm_tn


def _chunk_fn(s0t, r, lw, k, v, kkraw, a, rk, lnw, lnb, first=False, d=_ONE_PASS):
    c = r.shape[1]
    at, rt, btc, ktc, gc, aab, arb, xv, arkv, ain, bin_ = _chunk_core(r, lw, k, v, kkraw, a, d)
    rs = d.mm(jnp.concatenate([at, rt], axis=1), s0t)
    u = d.solve(aab, rs[:, :c] + xv)
    y = rs[:, c:] + d.mm(arb, u) + arkv
    if first:
        y = _with_early_rows(y, r, lw, k, v, ain, bin_)
    gcol = jnp.sum(_diag(gc), axis=2, keepdims=True)
    sct = gcol * s0t + d.mm_tn(jnp.concatenate([btc, ktc], axis=1), jnp.concatenate([u, v], axis=1))
    return _post(y, r, k, v, rk, lnw, lnb), sct


def _diag(gc):
    return jnp.where(_masks(HEAD)[2], gc, 0.0)


def _with_early_rows(y, r, lw, k, v, ain, bin_):
    early = _early_rows(r[:2], lw[:2], k[:2], v[:2], ain[:2], bin_[:2])
    return jnp.concatenate([jnp.concatenate([early, y[:2, EARLY:]], axis=1), y[2:]], axis=0)


def _early_rows(r, lw, k, v, ain, bin_):
    cols = lambda x: _stack([jnp.transpose(x[h]) for h in range(2)])
    wc, bc, kc = cols(jnp.exp(lw)), cols(bin_), cols(k)
    st = jnp.zeros((2, HEAD, HEAD), F32)
    rows = []
    for t in range(EARLY):
        sa = _ONE_PASS.mm(ain[:, t:t + 1], st)
        st = st * wc[:, :, t:t + 1] + bc[:, :, t:t + 1] * sa + kc[:, :, t:t + 1] * v[:, t:t + 1]
        rows.append(_ONE_PASS.mm(r[:, t:t + 1], st))
    return jnp.concatenate(rows, axis=1)


def _chunk_rows(c):
    return pl.ds(c * CHUNK, CHUNK) if isinstance(c, int) else pl.ds(pl.multiple_of(c * CHUNK, CHUNK), CHUNK)


def _stack(xs):
    return jnp.concatenate([x[None] for x in xs], axis=0)


def _pairs(ref, chunks):
    tiles = [ref[0, _chunk_rows(c), :] for c in chunks]
    return _stack([t[:, HEAD * h:HEAD * h + HEAD] for t in tiles for h in range(2)])


def _unpair(vals, j):
    return jnp.concatenate([vals[2 * j], vals[2 * j + 1]], axis=1)


def _masks(c):
    ii = lax.broadcasted_iota(jnp.int32, (c, c), 0)
    jj = lax.broadcasted_iota(jnp.int32, (c, c), 1)
    return ii > jj, ii >= jj, ii == jj


@jax.custom_vjp
def _running_sum(lw):
    return _tri_dot(lw, _NN)


def _tri_dot(x, dims):
    g_, c, _ = x.shape
    tri = jnp.broadcast_to(_masks(c)[1].astype(BF16), (g_, c, c))
    head = x.astype(BF16)
    rest = (x - head.astype(F32)).astype(BF16)
    return lax.dot_general(tri, head, (dims, ((0,), (0,))), preferred_element_type=F32) + \
        lax.dot_general(tri, rest, (dims, ((0,), (0,))), preferred_element_type=F32)


_running_sum.defvjp(lambda lw: (_running_sum(lw), None), lambda _, ct: (_tri_dot(ct, _TN),))


def _chunk_core(r, lw, k, v, kkraw, a, d=_ONE_PASS):
    g_, c = r.shape[0], r.shape[1]
    nrm = jnp.sqrt(jnp.sum(kkraw * kkraw, axis=-1, keepdims=True))
    kkn = kkraw / jnp.maximum(nrm, 1e-12)
    ain, bin_ = -kkn, kkn * a
    strict, incl, _ = _masks(c)
    lg = _running_sum(lw)
    g, gp, gi = jnp.exp(lg), jnp.exp(lg - lw), jnp.exp(-lg)
    at, rt, bt, kt = ain * gp, r * g, bin_ * gi, k * gi
    aa = d.mm_nt(jnp.concatenate([at, rt], axis=1), jnp.concatenate([bt, kt], axis=1))
    aab = jnp.where(strict, aa[:, :c, :c], 0.0)
    aak = jnp.where(strict, aa[:, :c, c:], 0.0)
    arb = jnp.where(incl, aa[:, c:, :c], 0.0)
    ark = jnp.where(incl, aa[:, c:, c:], 0.0)
    akv = d.mm(jnp.concatenate([aak, ark], axis=1), v)
    gc = g[:, c - 1:c, :]
    return at, rt, bt * gc, kt * gc, gc, aab, arb, akv[:, :c], akv[:, c:], ain, bin_


def _post(y, r, k, v, rk, lnw, lnb):
    mu = jnp.mean(y, axis=-1, keepdims=True)
    var = jnp.mean(jnp.square(y - mu), axis=-1, keepdims=True)
    yn = (y - mu) * lax.rsqrt(var + GN_EPS) * lnw + lnb
    return yn + jnp.sum(r * k * rk, axis=-1, keepdims=True) * v


def _chunk_consts(r, lw, k, v, kkraw, a, first=False):
    d = _ONE_PASS
    at, rt, btc, ktc, gc, aab, arb, xv, arkv, ain, bin_ = _chunk_core(r, lw, k, v, kkraw, a, d)
    z = d.solve(aab, jnp.concatenate([at, xv], axis=2))
    ryv = jnp.concatenate([rt, arkv], axis=2) + d.mm(arb, z)
    if first:
        ryv = jnp.concatenate([ryv[:, :, :HEAD], _with_early_rows(ryv[:, :, HEAD:], r, lw, k, v, ain, bin_)], axis=2)
    mkv = d.mm_tn(btc, z) + jnp.concatenate([_diag(gc), d.mm_tn(ktc, v)], axis=2)
    return mkv, ryv


def _rwkv_scan(ins, rk, lnw, lnb):
    bsz, s, _ = ins[0].shape
    nch = s // CHUNK

    def consts_body(r_ref, lw_ref, k_ref, v_ref, kk_ref, a_ref, mkv_ref, ry_ref, yv_ref):
        def group(i, carry):
            chunks = [i * CHUNK_GROUP + j for j in range(CHUNK_GROUP)]
            mkv, ryv = _chunk_consts(*[_pairs(ref, chunks) for ref in (r_ref, lw_ref, k_ref, v_ref, kk_ref, a_ref)],
                                     first=isinstance(i, int) and i == 0)
            for j, c in enumerate(chunks):
                for h in range(2):
                    mkv_ref[0, 0, c, h] = mkv[2 * j + h]
                ry_ref[0, _chunk_rows(c), :] = jnp.concatenate([ryv[2 * j][:, :HEAD], ryv[2 * j + 1][:, :HEAD]], axis=1)
                yv_ref[0, _chunk_rows(c), :] = jnp.concatenate([ryv[2 * j][:, HEAD:], ryv[2 * j + 1][:, HEAD:]], axis=1)
            return carry

        group(0, 0)
        lax.fori_loop(1, nch // CHUNK_GROUP, group, 0)

    tile = pl.BlockSpec((1, s, LANE), lambda b, hp: (b, 0, hp))
    vec = pl.BlockSpec((1, LANE), lambda b, hp: (0, hp))
    mkv_spec = pl.BlockSpec((1, 1, nch, 2, HEAD, LANE), lambda b, hp: (b, hp, 0, 0, 0, 0))
    st_spec = pl.BlockSpec((1, 1, nch, 2, HEAD, HEAD), lambda b, hp: (b, hp, 0, 0, 0, 0))
    mkv, ry, yv = pl.pallas_call(
        consts_body, name="rwkv_consts", grid=(bsz, 4), in_specs=[tile] * 6, out_specs=[mkv_spec, tile, tile],
        out_shape=[SDS((bsz, 4, nch, 2, HEAD, LANE), F32), SDS((bsz, s, WIDTH), F32), SDS((bsz, s, WIDTH), F32)],
        compiler_params=_params(("parallel", "parallel")))(*ins)

    states = _chunk_recurrence(mkv, None, "rwkv_states")

    def out_body(ry_ref, yv_ref, r_ref, k_ref, v_ref, st_ref, rk_ref, lnw_ref, lnb_ref, o_ref):
        y, r, k, v, rk_, lnw_, lnb_ = _scan_rows(ry_ref, yv_ref, r_ref, k_ref, v_ref, st_ref, rk_ref, lnw_ref, lnb_ref)
        o = _post(y, r, k, v, rk_, lnw_, lnb_)
        for j in range(CHUNK_GROUP):
            o_ref[0, _chunk_rows(j), :] = _unpair(o, j)

    o = pl.pallas_call(
        out_body, name="rwkv_out", grid=(bsz, 4, nch // CHUNK_GROUP), in_specs=_group_specs(5), out_specs=_group_specs(1)[0],
        out_shape=SDS((bsz, s, WIDTH), F32),
        compiler_params=_params(("parallel", "parallel", "parallel")))(ry, yv, ins[0], ins[2], ins[3], states, rk, lnw, lnb)
    return o, states, (mkv, ry, yv)


def _group_specs(n_tiles):
    tile = pl.BlockSpec((1, CHUNK_GROUP * CHUNK, LANE), lambda b, hp, t: (b, t, hp))
    if n_tiles == 1:
        return [tile]
    st = pl.BlockSpec((1, 1, CHUNK_GROUP, 2, HEAD, HEAD), lambda b, hp, t: (b, hp, t, 0, 0, 0))
    vec = pl.BlockSpec((1, LANE), lambda b, hp, t: (0, hp))
    return [tile] * n_tiles + [st] + [vec] * 3


def _scan_rows(ry_ref, yv_ref, r_ref, k_ref, v_ref, st_ref, rk_ref, lnw_ref, lnb_ref):
    chunks = list(range(CHUNK_GROUP))
    ry, yv, r, k, v = (_pairs(ref, chunks) for ref in (ry_ref, yv_ref, r_ref, k_ref, v_ref))
    st = _stack([st_ref[0, 0, c, h] for c in chunks for h in range(2)])
    vecs = [_stack([ref[:, HEAD * h:HEAD * h + HEAD] for _ in chunks for h in range(2)]) for ref in (rk_ref, lnw_ref, lnb_ref)]
    return (_bmm(ry, st) + yv, r, k, v, *vecs)


def _chunk_recurrence(mkv, q, name):
    bsz, _, nch = mkv.shape[:3]
    pairs = [(hp, h) for hp in range(4) for h in range(2)]

    def body(*refs):
        mkv_ref, out_ref, acc = refs[0], refs[-2], refs[-1]
        acc[...] = jnp.zeros_like(acc)

        def step(i, carry):
            c = i if q is None else nch - 1 - i
            cur = acc[...]
            for j, (hp, h) in enumerate(pairs):
                out_ref[0, hp, c, h] = cur[j]
            m = _stack([mkv_ref[0, hp, c, h] for hp, h in pairs])
            if q is None:
                acc[...] = _bmm(m[:, :, :HEAD], cur) + m[:, :, HEAD:]
            else:
                acc[...] = _bmm_tn(m[:, :, :HEAD], cur) + _stack([refs[1][0, hp, c, h] for hp, h in pairs])
            return carry

        lax.fori_loop(0, nch, step, 0)

    spec = lambda w: pl.BlockSpec((1, 4, nch, 2, HEAD, w), lambda b: (b, 0, 0, 0, 0, 0))
    return pl.pallas_call(
        body, name=name, grid=(bsz,), in_specs=[spec(LANE)] + ([] if q is None else [spec(HEAD)]), out_specs=spec(HEAD),
        out_shape=SDS((bsz, 4, nch, 2, HEAD, HEAD), F32), scratch_shapes=[pltpu.VMEM((8, HEAD, HEAD), F32)],
        compiler_params=_params(("parallel",)))(*([mkv] if q is None else [mkv, q]))


def _rwkv_scan_bwd(ins, states, consts, do3, rk, lnw, lnb):
    bsz, s, _ = ins[0].shape
    nch = s // CHUNK

    mkv, ry, yv = consts

    def q_body(do_ref, ry_ref, yv_ref, r_ref, k_ref, v_ref, st_ref, rk_ref, lnw_ref, lnb_ref, q_ref):
        y, r, k, v, rk_, lnw_, lnb_ = _scan_rows(ry_ref, yv_ref, r_ref, k_ref, v_ref, st_ref, rk_ref, lnw_ref, lnb_ref)
        _, vjp = jax.vjp(lambda y_: _post(y_, r, k, v, rk_, lnw_, lnb_), y)
        (dy,) = vjp(_pairs(do_ref, list(range(CHUNK_GROUP))))
        q = _bmm_tn(_pairs(ry_ref, list(range(CHUNK_GROUP))), dy)
        for j in range(CHUNK_GROUP):
            for h in range(2):
                q_ref[0, 0, j, h] = q[2 * j + h]

    specs = _group_specs(6)
    q = pl.pallas_call(
        q_body, name="rwkv_q", grid=(bsz, 4, nch // CHUNK_GROUP), in_specs=specs, out_specs=specs[6],
        out_shape=SDS((bsz, 4, nch, 2, HEAD, HEAD), F32),
        compiler_params=_params(("parallel", "parallel", "parallel")))(do3, ry, yv, ins[0], ins[2], ins[3], states, rk, lnw, lnb)

    dstates = _chunk_recurrence(mkv, q, "rwkv_dstates")

    def body(r_ref, lw_ref, k_ref, v_ref, kk_ref, a_ref, st_ref, dst_ref, do_ref, rk_ref, lnw_ref, lnb_ref,
             dr_ref, dlw_ref, dk_ref, dv_ref, dkk_ref, da_ref, drk_ref, dlnw_ref, dlnb_ref):
        chunks = list(range(BWD_GROUP))
        par_refs = (drk_ref, dlnw_ref, dlnb_ref)

        @pl.when(jnp.logical_and(pl.program_id(1) == 0, pl.program_id(2) == 0))
        def _():
            for ref in par_refs:
                ref[...] = jnp.zeros_like(ref)

        def group(first):
            per_pair = lambda ref: _stack([ref[0, 0, c, h] for c in chunks for h in range(2)])
            vecs = [_stack([ref[:, HEAD * h:HEAD * h + HEAD] for _ in chunks for h in range(2)]) for ref in (rk_ref, lnw_ref, lnb_ref)]
            _, vjp = jax.vjp(functools.partial(_chunk_fn, first=first, d=_ONE_PASS), per_pair(st_ref),
                             *[_pairs(ref, chunks) for ref in (r_ref, lw_ref, k_ref, v_ref, kk_ref, a_ref)], *vecs)
            grads = vjp((_pairs(do_ref, chunks), per_pair(dst_ref)))
            for ref, cot in zip((dr_ref, dlw_ref, dk_ref, dv_ref, dkk_ref, da_ref), grads[1:7]):
                for j, c in enumerate(chunks):
                    ref[0, _chunk_rows(c), :] = _unpair(cot, j)
            for ref, g_ in zip(par_refs, grads[7:10]):
                ref[...] += jnp.concatenate([sum(g_[2 * j + h] for j in range(BWD_GROUP)) for h in range(2)], axis=1)

        pl.when(pl.program_id(2) == 0)(functools.partial(group, True))
        pl.when(pl.program_id(2) != 0)(functools.partial(group, False))

    tt = BWD_GROUP * CHUNK
    tile = pl.BlockSpec((1, tt, LANE), lambda hp, b, t: (b, t, hp))
    vec = pl.BlockSpec((1, LANE), lambda hp, b, t: (0, hp))
    st_spec = pl.BlockSpec((1, 1, BWD_GROUP, 2, HEAD, HEAD), lambda hp, b, t: (b, hp, t, 0, 0, 0))
    outs = pl.pallas_call(
        body, name="rwkv_scan_bwd", grid=(4, bsz, s // tt), in_specs=[tile] * 6 + [st_spec, st_spec, tile] + [vec] * 3,
        out_specs=[tile] * 6 + [vec] * 3,
        out_shape=[SDS((bsz, s, WIDTH), F32)] * 6 + [SDS((1, WIDTH), F32)] * 3,
        compiler_params=_params(("parallel", "arbitrary", "arbitrary")))(*ins, states, dstates, do3, rk, lnw, lnb)
    return outs[:6], outs[6:]


def _head(o_attn, o_rwkv, z_attn, z_rwkv, gm, x2, tgt, wua, wur, wout, g2):
    n = x2.shape[0]
    tm = 256
    nt = n // tm
    d = D_MODEL

    def body(oa_ref, or_ref, za_ref, zr_ref, gm_ref, x_ref, t_ref, wua_ref, wur_ref, wout_ref, g2_ref,
             dxo_ref, doa_ref, dor_ref, dza_ref, dzr_ref, dgm_ref, dwua_ref, dwur_ref, dwout_ref, dg2_ref, loss_ref, lacc):
        i = pl.program_id(0)
        oa, orw, za, zr = oa_ref[...], or_ref[...], za_ref[...], zr_ref[...]
        ga, gb = gm_ref[:, 0:d], gm_ref[:, d:2 * d]
        am = (oa * _silu(za)).astype(BF16)
        bm = (orw * _silu(zr)).astype(BF16)
        ya, yb = _dot(am, wua_ref[...]), _dot(bm, wur_ref[...])
        sa, sb = jax.nn.sigmoid(ga), jax.nn.sigmoid(gb)
        merged = (sa * ya + sb * yb).astype(BF16)
        out = _dot(merged, wout_ref[...])
        rs = lax.rsqrt(jnp.mean(out * out, axis=-1, keepdims=True) + RMS_EPS)
        g2 = g2_ref[...]
        err = x_ref[...] + out * rs * g2 - t_ref[...]
        lpart = jnp.sum(err * err, axis=0, keepdims=True)
        dxo = err * (1.0 / d)
        dxo_ref[...] = dxo
        dg2 = jnp.sum(dxo * out * rs, axis=0, keepdims=True)
        gd = dxo * g2
        dout = (rs * (gd - out * (rs * rs) * jnp.mean(gd * out, axis=-1, keepdims=True))).astype(BF16)
        dmerged = _dot_nt(dout, wout_ref[...])
        dwout = _dot_tn(merged, dout)
        dya, dyb = (dmerged * sa).astype(BF16), (dmerged * sb).astype(BF16)
        dgm_ref[:, 0:d] = (dmerged * ya * sa * (1.0 - sa)).astype(BF16)
        dgm_ref[:, d:2 * d] = (dmerged * yb * sb * (1.0 - sb)).astype(BF16)
        dam, dbm = _dot_nt(dya, wua_ref[...]), _dot_nt(dyb, wur_ref[...])
        dwua, dwur = _dot_tn(am, dya), _dot_tn(bm, dyb)
        doa_ref[...] = dam * _silu(za)
        dza_ref[...] = (dam * oa * _dsilu(za)).astype(BF16)
        dor_ref[...] = dbm * _silu(zr)
        dzr_ref[...] = (dbm * orw * _dsilu(zr)).astype(BF16)

        @pl.when(i == 0)
        def _():
            dwua_ref[...], dwur_ref[...], dwout_ref[...], dg2_ref[...], lacc[...] = dwua, dwur, dwout, dg2, lpart

        @pl.when(i != 0)
        def _():
            dwua_ref[...] += dwua
            dwur_ref[...] += dwur
            dwout_ref[...] += dwout
            dg2_ref[...] += dg2
            lacc[...] += lpart

        @pl.when(i == nt - 1)
        def _():
            loss_ref[...] = jnp.sum(lacc[...], axis=1, keepdims=True) * (0.5 / d)

    t512 = pl.BlockSpec((tm, WIDTH), lambda i: (i, 0))
    t1k = pl.BlockSpec((tm, d), lambda i: (i, 0))
    t2k = pl.BlockSpec((tm, 2 * d), lambda i: (i, 0))
    full = lambda r, c: pl.BlockSpec((r, c), lambda i: (0, 0))
    return pl.pallas_call(
        body, name="head_fwd_bwd", grid=(nt,),
        in_specs=[t512, t512, t512, t512, t2k, t1k, t1k, full(WIDTH, d), full(WIDTH, d), full(d, d), full(1, d)],
        out_specs=[t1k, t512, t512, t512, t512, t2k, full(WIDTH, d), full(WIDTH, d), full(d, d), full(1, d), full(1, 1)],
        out_shape=[SDS((n, d), F32), SDS((n, WIDTH), F32), SDS((n, WIDTH), F32), SDS((n, WIDTH), BF16), SDS((n, WIDTH), BF16),
                   SDS((n, 2 * d), BF16), SDS((WIDTH, d), F32), SDS((WIDTH, d), F32), SDS((d, d), F32), SDS((1, d), F32), SDS((1, 1), F32)],
        scratch_shapes=[pltpu.VMEM((1, d), F32)],
        compiler_params=_params(("arbitrary",)))(o_attn, o_rwkv, z_attn, z_rwkv, gm, x2, tgt, wua, wur, wout, g2)


def _prenorm_bwd(dh, x2, rs, g1, dxo):
    n, d = x2.shape
    tm = 1024

    def body(dh_ref, x_ref, rs_ref, g_ref, dxo_ref, gx_ref, dg_ref):
        x, r = x_ref[...], rs_ref[...]
        gd = dh_ref[...] * g_ref[...]
        gx_ref[...] = dxo_ref[...] + r * (gd - x * (r * r) * jnp.mean(gd * x, axis=-1, keepdims=True))
        dg = jnp.sum(dh_ref[...] * x * r, axis=0, keepdims=True)

        @pl.when(pl.program_id(0) == 0)
        def _():
            dg_ref[...] = dg

        @pl.when(pl.program_id(0) != 0)
        def _():
            dg_ref[...] += dg

    t = pl.BlockSpec((tm, d), lambda i: (i, 0))
    return pl.pallas_call(
        body, name="prenorm_bwd", grid=(n // tm,),
        in_specs=[t, t, pl.BlockSpec((tm, 1), lambda i: (i, 0)), pl.BlockSpec((1, d), lambda i: (0, 0)), t],
        out_specs=[t, pl.BlockSpec((1, d), lambda i: (0, 0))], out_shape=[SDS((n, d), F32), SDS((1, d), F32)],
        compiler_params=_params(("arbitrary",)))(dh, x2, rs, g1, dxo)


def _mesh_pos():
    x, y, c = lax.axis_index("x"), lax.axis_index("y"), lax.axis_index("c")
    return 4 * x + 2 * y + c


def _coords(idx):
    return (idx // 4, (idx // 2) % 2, idx % 2)


def _exchange(srcs, to_all, name):
    n = len(srcs)

    def body(*refs):
        src_refs, dst_refs = refs[:n], refs[n:2 * n]
        send_sems, recv_sems, local_sems = refs[2 * n:]
        me = _mesh_pos()

        def piece(i, j):
            return src_refs[i] if to_all[i] else src_refs[i].at[j]

        def remote(i, off, peer, block, slot):
            return pltpu.make_async_remote_copy(src_ref=piece(i, block), dst_ref=dst_refs[i].at[slot],
                                                send_sem=send_sems.at[i, off - 1], recv_sem=recv_sems.at[i, off - 1],
                                                device_id=_coords(peer), device_id_type=MESH)

        local = [pltpu.make_async_copy(piece(i, me), dst_refs[i].at[me], local_sems.at[i]) for i in range(n)]
        for cp in local:
            cp.start()
        sends = []
        for off in range(1, N_DEV):
            to = (me + off) % N_DEV
            for i in range(n):
                sends.append(remote(i, off, to, to, me))
                sends[-1].start()
        for off in range(1, N_DEV):
            frm = (me + N_DEV - off) % N_DEV
            for i in range(n):
                remote(i, off, frm, me, frm).wait_recv()
        for cp in sends:
            cp.wait_send()
        for cp in local:
            cp.wait()

    outs = pl.pallas_call(
        body, name=name, in_specs=[pl.BlockSpec(memory_space=pltpu.HBM)] * n, out_specs=[pl.BlockSpec(memory_space=pltpu.HBM)] * n,
        out_shape=[SDS((N_DEV,) + s.shape[-2:], s.dtype) for s in srcs],
        scratch_shapes=[pltpu.SemaphoreType.DMA((n, N_DEV - 1)), pltpu.SemaphoreType.DMA((n, N_DEV - 1)), pltpu.SemaphoreType.DMA((n,))],
        compiler_params=pltpu.CompilerParams())(*srcs)
    return outs


_HBM = pl.BlockSpec(memory_space=pltpu.HBM)
_SEM = pl.BlockSpec(memory_space=pltpu.SEMAPHORE)
_EFFECT = pltpu.SideEffectType.DATAFLOW_SIDE_EFFECTING


def _send_copy(src_ref, land_ref, to_all, send_sems, recv_sems, i, off, block, slot, peer):
    k = i * (N_DEV - 1) + off - 1
    return pltpu.make_async_remote_copy(src_ref=src_ref if to_all else src_ref.at[block], dst_ref=land_ref.at[slot],
                                        send_sem=send_sems.at[k], recv_sem=recv_sems.at[k],
                                        device_id=_coords(peer), device_id_type=MESH)


def _send_start(srcs, to_all, name):
    n = len(srcs)

    def body(*refs):
        src_refs, land_refs = refs[:n], refs[n:2 * n]
        send_sems, recv_sems = refs[2 * n:2 * n + 2]
        me = _mesh_pos()
        for off in range(1, N_DEV):
            to = (me + off) % N_DEV
            for i in range(n):
                _send_copy(src_refs[i], land_refs[i], to_all, send_sems, recv_sems, i, off, to, me, to).start()
        refs[-1][...] = jnp.zeros_like(refs[-1])

    lands = [jnp.zeros((N_DEV,) + s.shape[-2:], s.dtype) for s in srcs]
    hbm = [pltpu.HBM(a.shape, a.dtype) for a in list(srcs) + lands]
    sems = pltpu.SemaphoreType.DMA((n * (N_DEV - 1),))
    outs = pl.pallas_call(
        body, name=name, out_shape=(sems, sems, *hbm, SDS((8, LANE), BF16)),
        in_specs=(_HBM,) * (2 * n), out_specs=(_SEM, _SEM) + (_HBM,) * (2 * n) + (pl.BlockSpec(memory_space=pltpu.VMEM),),
        input_output_aliases={i: 2 + i for i in range(2 * n)}, compiler_params=pltpu.CompilerParams(has_side_effects=_EFFECT),
    )(*[pltpu.with_memory_space_constraint(a, pltpu.HBM) for a in list(srcs) + lands])
    return outs[0], outs[1], outs[2:2 + n], outs[2 + n:2 + 2 * n], outs[-1]


def _send_wait(send_sems, recv_sems, srcs_thru, lands_thru, to_all, after, name):
    n = len(srcs_thru)

    def body(*refs):
        src_refs, land_refs = refs[:n], refs[n:2 * n]
        send_sems, recv_sems = refs[2 * n:2 * n + 2]
        me = _mesh_pos()
        for off in range(1, N_DEV):
            to, frm = (me + off) % N_DEV, (me + N_DEV - off) % N_DEV
            for i in range(n):
                _send_copy(src_refs[i], land_refs[i], to_all, send_sems, recv_sems, i, off, to, me, to).wait_send()
                _send_copy(src_refs[i], land_refs[i], to_all, send_sems, recv_sems, i, off, me, frm, frm).wait_recv()

    hbm = tuple(pltpu.HBM(a.shape, a.dtype) for a in list(srcs_thru) + list(lands_thru))
    outs = pl.pallas_call(
        body, name=name, out_shape=hbm, in_specs=(_HBM,) * (2 * n) + (_SEM, _SEM, pl.BlockSpec(memory_space=pl.ANY)),
        out_specs=(_HBM,) * (2 * n), input_output_aliases={i: i for i in range(2 * n)},
        compiler_params=pltpu.CompilerParams(has_side_effects=_EFFECT),
    )(*srcs_thru, *lands_thru, send_sems, recv_sems, after)
    return outs[n:]


def _gather(srcs, after, name):
    n = len(srcs)

    def body(*refs):
        src_refs, dst_refs = refs[:n], refs[n + 1:2 * n + 1]
        send_sems, recv_sems, local_sems = refs[2 * n + 1:]
        x, y, c = lax.axis_index("x"), lax.axis_index("y"), lax.axis_index("c")
        me, sibling = (x, y, c), (x, y, 1 - c)
        chips = [(1 - x, y), (x, 1 - y), (1 - x, 1 - y)]

        def slot(i, dev):
            return dst_refs[i].at[4 * dev[0] + 2 * dev[1] + dev[2]]

        def copy(i, k, block, to, own=False):
            return pltpu.make_async_remote_copy(src_ref=src_refs[i] if own else slot(i, block), dst_ref=slot(i, block),
                                                send_sem=send_sems.at[i, k], recv_sem=recv_sems.at[i, k],
                                                device_id=to, device_id_type=MESH)

        local = [pltpu.make_async_copy(src_refs[i], slot(i, me), local_sems.at[i]) for i in range(n)]
        for cp in local:
            cp.start()
        sends = []
        for i in range(n):
            sends.append(copy(i, 0, me, sibling, own=True))
            sends += [copy(i, 1 + j, me, (*chip, c), own=True) for j, chip in enumerate(chips)]
        for cp in sends:
            cp.start()
        for j, chip in enumerate(chips):
            for i in range(n):
                copy(i, 1 + j, (*chip, c), me).wait_recv()
                sends.append(copy(i, 4 + j, (*chip, c), sibling))
                sends[-1].start()
        for i in range(n):
            copy(i, 0, sibling, me).wait_recv()
            for j, chip in enumerate(chips):
                copy(i, 4 + j, (*chip, 1 - c), me).wait_recv()
        for cp in sends:
            cp.wait_send()
        for cp in local:
            cp.wait()

    return pl.pallas_call(
        body, name=name, in_specs=[pl.BlockSpec(memory_space=pltpu.HBM)] * n + [pl.BlockSpec(memory_space=pl.ANY)],
        out_specs=[pl.BlockSpec(memory_space=pltpu.HBM)] * n, out_shape=[SDS((N_DEV,) + s.shape, s.dtype) for s in srcs],
        scratch_shapes=[pltpu.SemaphoreType.DMA((n, N_DEV - 1)), pltpu.SemaphoreType.DMA((n, N_DEV - 1)), pltpu.SemaphoreType.DMA((n,))],
        compiler_params=pltpu.CompilerParams())(*srcs, after)


def _adamw(parts, w, m, v, tr, name, own=None):
    rows, cols = w.shape
    c1, c2 = 1.0 - ADAM_B1 ** ADAM_STEP, 1.0 - ADAM_B2 ** ADAM_STEP

    def body(p_ref, *refs):
        w_ref, m_ref, v_ref, g_ref, d_ref, nm_ref, nv_ref = refs[-7:]
        me = _mesh_pos()

        def part(j):
            return p_ref[j] if own is None else jnp.where(me == j, refs[0][...], p_ref[j])

        g = part(0).astype(F32)
        for j in range(1, N_DEV):
            g = g + part(j).astype(F32)
        nm = ADAM_B1 * m_ref[...] + (1.0 - ADAM_B1) * g
        nv = ADAM_B2 * v_ref[...] + (1.0 - ADAM_B2) * jnp.square(g)
        g_ref[...] = g
        nm_ref[...] = nm
        nv_ref[...] = nv
        d_ref[...] = -ADAM_LR * ((nm / c1) / (jnp.sqrt(nv / c2) + ADAM_EPS) + ADAM_WD * w_ref[...])

    t = pl.BlockSpec((tr, cols), lambda i: (i, 0))
    extra = [] if own is None else [own]
    return pl.pallas_call(
        body, name=name, grid=(rows // tr,), in_specs=[pl.BlockSpec((N_DEV, tr, cols), lambda i: (0, i, 0))] + [t] * (3 + len(extra)),
        out_specs=[t] * 4, out_shape=[SDS((rows, cols), F32)] * 4, compiler_params=_params(("parallel",)))(parts, *extra, w, m, v)


SHARDED = (("w_in", D_MODEL, IN_COLS // N_DEV, True, 128), ("w_up_attn", WIDTH, D_MODEL // N_DEV, True, WIDTH),
           ("w_up_rwkv", WIDTH, D_MODEL // N_DEV, True, WIDTH), ("w_out", D_MODEL // N_DEV, D_MODEL, False, D_MODEL // N_DEV),
           ("rwkv_w_up", LORA, WIDTH // N_DEV, True, LORA), ("rwkv_a_up", LORA, WIDTH // N_DEV, True, LORA))
LOSS_SLOT = sum(n for _, n in SMALL)


def _pack_small(small, extra=None):
    flat = [small[n].reshape(-1).astype(F32) for n, _ in SMALL]
    flat.append(jnp.zeros((1,), F32) if extra is None else extra.reshape(1))
    flat.append(jnp.zeros((SMALL_ROWS * LANE - LOSS_SLOT - 1,), F32))
    return jnp.concatenate(flat).reshape(SMALL_ROWS, LANE)


def _unpack_small(packed, shapes):
    flat = packed.reshape(-1)
    out, off = {}, 0
    for n, cnt in SMALL:
        out[n] = flat[off:off + cnt].reshape(shapes[n])
        off += cnt
    return out, flat[LOSS_SLOT]


def _whole(gathered, by_cols):
    if not by_cols:
        return gathered.reshape(-1, gathered.shape[-1])
    return gathered.transpose(1, 0, 2).reshape(gathered.shape[1], -1)


def _per_owner(full, by_cols):
    if not by_cols:
        return full.reshape(N_DEV, -1, full.shape[-1])
    return full.reshape(full.shape[0], N_DEV, -1).transpose(1, 0, 2)


def _local_step(x, loss_target, sm, wts):
    bsz, s, d = x.shape
    n = bsz * s
    x2, tgt = x.reshape(n, d), loss_target.reshape(n, d)
    bidx = jnp.asarray(_bucket_tables())
    w_in = wts["w_in"]
    segs = (("qkv", 0, QKV_COLS, 1536), ("za", OFF_ZA, WIDTH, 512), ("pr", OFF_PR, PR_COLS, PR_COLS), ("zr", OFF_ZR, WIDTH, 512),
            ("gm", OFF_GM, 2 * D_MODEL, 1024))

    h, rs = _prenorm(x2, sm["pre_norm_gain"])
    w_seg = {nm: w_in[:, off:off + cnt] for nm, off, cnt, _ in segs}
    proj = {nm: _mm(h, w_seg[nm], tn, "proj_" + nm) for nm, _, _, tn in segs}
    qkv3 = proj["qkv"].reshape(bsz, s, QKV_COLS)
    pr3 = proj["pr"].reshape(bsz, s, PR_COLS)

    o_attn, lse = _attn_fwd(qkv3, sm["rel_bias"], bidx)
    rk = sm["rwkv_r_k"].reshape(1, WIDTH)
    pre_args = (sm["rwkv_shift_mix"], sm["rwkv_w0"], wts["rwkv_w_up"], sm["rwkv_a0"], wts["rwkv_a_up"], sm["rwkv_k_k"], sm["rwkv_k_a"])
    scan_in = _rwkv_pre(pr3, *pre_args)
    o_rwkv, states, consts = _rwkv_scan(scan_in, rk, sm["rwkv_ln_w"], sm["rwkv_ln_b"])

    (dxo, do_attn, do_rwkv, dza, dzr, dgm, g_wua, g_wur, g_wout, g_post, loss) = _head(
        o_attn.reshape(n, WIDTH), o_rwkv.reshape(n, WIDTH), proj["za"], proj["zr"], proj["gm"], x2, tgt,
        wts["w_up_attn"], wts["w_up_rwkv"], wts["w_out"], sm["post_norm_gain"])

    dqkv, dbias = _attn_bwd(qkv3, o_attn, lse, do_attn.reshape(bsz, s, WIDTH), sm["rel_bias"], bidx)
    g_bias = _bias_grad(dbias, bidx)[:, :N_BUCKET].T

    scan_cots, (g_rk, g_lnw, g_lnb) = _rwkv_scan_bwd(scan_in, states, consts, do_rwkv.reshape(bsz, s, WIDTH), rk, sm["rwkv_ln_w"],
                                                     sm["rwkv_ln_b"])
    dprs, g_mix, g_w0, g_wup, g_a0, g_aup, g_kk, g_ka = _rwkv_pre_bwd(pr3, scan_cots, *pre_args)
    dpr = _shift_bwd(dprs, sm["rwkv_shift_mix"]).reshape(n, PR_COLS)

    dsegs = [(dqkv.reshape(9, n, WIDTH), 0, QKV_COLS, WIDTH), (dza, OFF_ZA, WIDTH, WIDTH), (dpr, OFF_PR, PR_COLS, PR_COLS),
             (dzr, OFF_ZR, WIDTH, WIDTH), (dgm, OFF_GM, 2 * D_MODEL, D_MODEL)]
    full = {"w_in": jnp.concatenate([_mm_tn(h, t, tn, "gw_in_%d" % j) for j, (t, _, _, tn) in enumerate(dsegs)], axis=1),
            "w_up_attn": g_wua, "w_up_rwkv": g_wur, "w_out": g_wout, "rwkv_w_up": g_wup, "rwkv_a_up": g_aup}
    blocks = [_per_owner(full[nm], by_cols).astype(BF16) for nm, _, _, by_cols, _ in SHARDED]
    me = 4 * lax.axis_index("x") + 2 * lax.axis_index("y") + lax.axis_index("c")
    own = [lax.dynamic_index_in_dim(b, me, 0, keepdims=False) for b in blocks]
    send_sems, recv_sems, blocks_thru, lands_thru, token = _send_start(blocks, False, "grads_start")
    dh = _mm_nt(dsegs[0][0], w_seg["qkv"], token, "dh_qkv")
    dh = _mm_nt_multi([t for t, *_ in dsegs[1:]], [w_seg[nm] for nm in ("za", "pr", "zr", "gm")], dh, "dh_rest")
    grad_x, g_pre = _prenorm_bwd(dh, x2, rs, sm["pre_norm_gain"], dxo)
    landed = _send_wait(send_sems, recv_sems, blocks_thru, lands_thru, False, g_pre, "grads_wait")

    small = {"pre_norm_gain": g_pre, "rel_bias": g_bias, "rwkv_shift_mix": g_mix, "rwkv_w0": g_w0, "rwkv_a0": g_a0, "rwkv_k_k": g_kk,
             "rwkv_k_a": g_ka, "rwkv_r_k": g_rk, "rwkv_ln_w": g_lnw, "rwkv_ln_b": g_lnb, "post_norm_gain": g_post}
    return loss[0, 0], grad_x.reshape(bsz, s, d), (landed, own), small


def kernel(x, pre_norm_gain, w_in, rel_bias, rwkv_shift_mix, rwkv_w0, rwkv_w_up, rwkv_a0, rwkv_a_up, rwkv_k_k, rwkv_k_a, rwkv_r_k, rwkv_ln_w, rwkv_ln_b, w_up_attn, w_up_rwkv, w_out, post_norm_gain, loss_target, m_pre_norm_gain, m_w_in, m_rel_bias, m_rwkv_shift_mix, m_rwkv_w0, m_rwkv_w_up, m_rwkv_a0, m_rwkv_a_up, m_rwkv_k_k, m_rwkv_k_a, m_rwkv_r_k, m_rwkv_ln_w, m_rwkv_ln_b, m_w_up_attn, m_w_up_rwkv, m_w_out, m_post_norm_gain, v_pre_norm_gain, v_w_in, v_rel_bias, v_rwkv_shift_mix, v_rwkv_w0, v_rwkv_w_up, v_rwkv_a0, v_rwkv_a_up, v_rwkv_k_k, v_rwkv_k_a, v_rwkv_r_k, v_rwkv_ln_w, v_rwkv_ln_b, v_w_up_attn, v_w_up_rwkv, v_w_out, v_post_norm_gain):
    names = [n for n, *_ in SHARDED] + [n for n, _ in SMALL]
    loc = dict(locals())
    w = {n: loc[n] for n in names}
    m = {n: loc["m_" + n] for n in names}
    v = {n: loc["v_" + n] for n in names}
    shapes = {n: w[n].shape for n in names}
    order = ["pre_norm_gain", "w_in", "rel_bias", "rwkv_shift_mix", "rwkv_w0", "rwkv_w_up", "rwkv_a0", "rwkv_a_up", "rwkv_k_k", "rwkv_k_a",
             "rwkv_r_k", "rwkv_ln_w", "rwkv_ln_b", "w_up_attn", "w_up_rwkv", "w_out", "post_norm_gain"]
    shard2d = lambda t, n, r, c: t[n].reshape(r, c)

    shards = [shard2d(w, n, r, c).astype(BF16) for n, r, c, _, _ in SHARDED]
    send_sems, recv_sems, srcs_thru, lands_thru, token = _send_start(shards[1:], True, "weights_start")
    gathered = list(_gather(shards[:1], token, "gather_weights"))
    landed = _send_wait(send_sems, recv_sems, srcs_thru, lands_thru, True, gathered[0], "weights_wait")
    me = 4 * lax.axis_index("x") + 2 * lax.axis_index("y") + lax.axis_index("c")
    gathered += [lax.dynamic_update_index_in_dim(g, sh, me, 0) for g, sh in zip(landed, shards[1:])]
    wts = {n: _whole(g, by_cols) for (n, _, _, by_cols, _), g in zip(SHARDED, gathered)}

    loss, grad_x, (landed, own), small = _local_step(x, loss_target, w, wts)
    (small_parts,) = _exchange([_pack_small(small, loss)], [True], "exchange_small")

    outs = [{}, {}, {}, {}]
    for (n, r, c, _, tr), p, o_ in zip(SHARDED, landed, own):
        res = _adamw(p, shard2d(w, n, r, c), shard2d(m, n, r, c), shard2d(v, n, r, c), tr, "adamw_" + n, own=o_)
        for o, t in zip(outs, res):
            o[n] = t.reshape(shapes[n])
    res = _adamw(small_parts, _pack_small(w), _pack_small(m), _pack_small(v), SMALL_ROWS, "adamw_small")
    for o, t in zip(outs, res):
        o.update(_unpack_small(t, shapes)[0])
    loss = _unpack_small(res[0], shapes)[1]
    return (loss, grad_x, *[o[n] for o in outs for n in order])
```

```python
import functools
import math

import numpy as np
import jax
import jax.numpy as jnp
from jax import lax
from jax.experimental import pallas as pl
from jax.experimental.pallas import tpu as pltpu

F32, BF16 = jnp.float32, jnp.bfloat16
SDS = jax.ShapeDtypeStruct
MESH = pl.DeviceIdType.MESH

N_DEV = 8
D_MODEL = 1024
HEAD = 64
N_HEAD = 8
WIDTH = N_HEAD * HEAD
DILATIONS = (1, 4, 16)
QB = 128
N_BUCKET = 32
MAX_DIST = 2048
LORA = 64
QKV_COLS = 9 * WIDTH
PR_COLS = 3 * WIDTH + 2 * LORA
IN_COLS = QKV_COLS + WIDTH + PR_COLS + WIDTH + 2 * D_MODEL
OFF_ZA, OFF_PR, OFF_ZR, OFF_GM = QKV_COLS, QKV_COLS + WIDTH, QKV_COLS + WIDTH + PR_COLS, QKV_COLS + 2 * WIDTH + PR_COLS
RMS_EPS = 1e-6
GN_EPS = 64e-5
SCALE = 1.0 / math.sqrt(HEAD)
CHUNK = 64
CHUNK_GROUP = 8
BWD_GROUP = 16
EARLY = 8
NEG = -1e30
LANE = 128

ADAM_LR, ADAM_B1, ADAM_B2, ADAM_EPS, ADAM_WD, ADAM_STEP = 0.001, 0.9, 0.999, 1e-08, 0.01, 10

VMEM_LIMIT = 56 * 1024 * 1024

SMALL = (("pre_norm_gain", 1024), ("rel_bias", 768), ("rwkv_shift_mix", 1664), ("rwkv_w0", 512), ("rwkv_a0", 512),
         ("rwkv_k_k", 512), ("rwkv_k_a", 512), ("rwkv_r_k", 512), ("rwkv_ln_w", 512), ("rwkv_ln_b", 512),
         ("post_norm_gain", 1024))
SMALL_ROWS = 64


def _params(sem=None):
    return pltpu.CompilerParams(dimension_semantics=sem, vmem_limit_bytes=VMEM_LIMIT)


def _dot(a, b):
    return jnp.dot(a, b, preferred_element_type=F32)


def _dot_nt(a, b):
    return lax.dot_general(a, b, (((1,), (1,)), ((), ())), preferred_element_type=F32)


def _dot_tn(a, b):
    return lax.dot_general(a, b, (((0,), (0,)), ((), ())), preferred_element_type=F32)


@jax.custom_vjp
def _bdot(a, b):
    return _dot(a.astype(BF16), b.astype(BF16))


def _bdot_fwd(a, b):
    return _bdot(a, b), (a, b)


def _bdot_bwd(res, g):
    a, b = res
    gb = g.astype(BF16)
    return _dot_nt(gb, b.astype(BF16)), _dot_tn(a.astype(BF16), gb)


_bdot.defvjp(_bdot_fwd, _bdot_bwd)


def _silu(z):
    return z * jax.nn.sigmoid(z)


def _dsilu(z):
    s = jax.nn.sigmoid(z)
    return s * (1.0 + z * (1.0 - s))


def _softplus(x):
    return jnp.maximum(x, 0.0) + jnp.log(1.0 + jnp.exp(-jnp.abs(x)))


def _bucket_tables():
    qi = np.arange(QB)[:, None] + QB
    ki = np.arange(2 * QB)[None, :]
    rel = np.maximum(qi - ki, 0)
    out = []
    for d in DILATIONS:
        dist = rel * d
        max_exact = N_BUCKET // 2
        ratio = np.log(np.maximum(dist, 1).astype(np.float32) / max_exact) / np.float32(math.log(MAX_DIST / max_exact))
        large = max_exact + (ratio * (N_BUCKET - max_exact)).astype(np.int32)
        large = np.minimum(large, N_BUCKET - 1)
        out.append(np.where(dist < max_exact, dist, large).astype(np.int32))
    return np.stack(out)


def _prenorm(x2, g):
    n, d = x2.shape
    tm = 1024

    def body(x_ref, g_ref, h_ref, rs_ref):
        x = x_ref[...]
        rs = lax.rsqrt(jnp.mean(x * x, axis=-1, keepdims=True) + RMS_EPS)
        h_ref[...] = (x * rs * g_ref[...]).astype(BF16)
        rs_ref[...] = rs

    return pl.pallas_call(
        body, name="prenorm", grid=(n // tm,),
        in_specs=[pl.BlockSpec((tm, d), lambda i: (i, 0)), pl.BlockSpec((1, d), lambda i: (0, 0))],
        out_specs=[pl.BlockSpec((tm, d), lambda i: (i, 0)), pl.BlockSpec((tm, 1), lambda i: (i, 0))],
        out_shape=[SDS((n, d), BF16), SDS((n, 1), F32)], compiler_params=_params(("parallel",)))(x2, g)


def _mm(a, b, tn, name):
    m, k = a.shape
    n = b.shape[1]
    tm = 1024

    def body(a_ref, b_ref, o_ref):
        o_ref[...] = _dot(a_ref[...], b_ref[...])

    return pl.pallas_call(
        body, name=name, grid=(n // tn, m // tm),
        in_specs=[pl.BlockSpec((tm, k), lambda j, i: (i, 0)), pl.BlockSpec((k, tn), lambda j, i: (0, j))],
        out_specs=pl.BlockSpec((tm, tn), lambda j, i: (i, j)),
        out_shape=SDS((m, n), F32), compiler_params=_params(("parallel", "parallel")))(a, b)


def _mm_nt(a, b, after, name):
    m, seg = a.shape[1], a.shape[2]
    d, k = b.shape
    tm = 1024
    per = 3
    tk = per * seg

    def body(a_ref, b_ref, after_ref, o_ref):
        r = sum(_dot_nt(a_ref[j].astype(BF16), b_ref[:, seg * j:seg * (j + 1)]) for j in range(per))

        @pl.when(pl.program_id(1) == 0)
        def _():
            o_ref[...] = r

        @pl.when(pl.program_id(1) != 0)
        def _():
            o_ref[...] += r

    in_specs = [pl.BlockSpec((per, tm, seg), lambda i, j: (j, i, 0)), pl.BlockSpec((d, tk), lambda i, j: (0, j)),
                pl.BlockSpec(after.shape, lambda i, j: (0, 0))]
    return pl.pallas_call(
        body, name=name, grid=(m // tm, k // tk), in_specs=in_specs, out_specs=pl.BlockSpec((tm, d), lambda i, j: (i, 0)),
        out_shape=SDS((m, d), F32), compiler_params=_params(("parallel", "arbitrary")))(a, b, after)


def _dh_rest_prenorm_bwd(a_list, b_list, acc, x2, rs, g1, dxo):
    m, d = acc.shape
    tm = 512
    n = len(a_list)

    def body(*refs):
        x_ref, rs_ref, g_ref, dxo_ref, gx_ref, dg_ref = refs[2 * n + 1:]
        dh = refs[2 * n][...]
        for a_ref, b_ref in zip(refs[:n], refs[n:2 * n]):
            dh = dh + _dot_nt(a_ref[...].astype(BF16), b_ref[...])
        x, r = x_ref[...], rs_ref[...]
        gd = dh * g_ref[...]
        gx_ref[...] = dxo_ref[...] + r * (gd - x * (r * r) * jnp.mean(gd * x, axis=-1, keepdims=True))
        dg = jnp.sum(dh * x * r, axis=0, keepdims=True)

        @pl.when(pl.program_id(0) == 0)
        def _():
            dg_ref[...] = dg

        @pl.when(pl.program_id(0) != 0)
        def _():
            dg_ref[...] += dg

    t = pl.BlockSpec((tm, d), lambda i: (i, 0))
    in_specs = [pl.BlockSpec((tm, a.shape[1]), lambda i: (i, 0)) for a in a_list]
    in_specs += [pl.BlockSpec(b.shape, lambda i: (0, 0)) for b in b_list]
    in_specs += [t, t, pl.BlockSpec((tm, 1), lambda i: (i, 0)), pl.BlockSpec((1, d), lambda i: (0, 0)), t]
    return pl.pallas_call(
        body, name="dh_rest_prenorm_bwd", grid=(m // tm,), in_specs=in_specs, out_specs=[t, pl.BlockSpec((1, d), lambda i: (0, 0))],
        out_shape=[SDS((m, d), F32), SDS((1, d), F32)], compiler_params=_params(("arbitrary",)))(*a_list, *b_list, acc, x2, rs, g1, dxo)


def _mm_tn(a, b, tn, name):
    split = b.ndim == 3
    m, k1 = a.shape
    per = 3 if split else 1
    seg = b.shape[2] if split else tn
    tn = per * seg
    n2 = b.shape[0] * seg if split else b.shape[1]
    tm = 1024

    def body(a_ref, b_ref, o_ref):
        first = pl.program_id(1) == 0
        for j in range(per):
            r = _dot_tn(a_ref[...], (b_ref[j] if split else b_ref[...]).astype(BF16))
            cols = slice(seg * j, seg * (j + 1))

            @pl.when(first)
            def _(r=r, cols=cols):
                o_ref[:, cols] = r

            @pl.when(jnp.logical_not(first))
            def _(r=r, cols=cols):
                o_ref[:, cols] += r

    b_spec = pl.BlockSpec((per, tm, seg), lambda j, i: (j, i, 0)) if split else pl.BlockSpec((tm, tn), lambda j, i: (i, j))
    return pl.pallas_call(
        body, name=name, grid=(n2 // tn, m // tm),
        in_specs=[pl.BlockSpec((tm, k1), lambda j, i: (i, 0)), b_spec],
        out_specs=pl.BlockSpec((k1, tn), lambda j, i: (0, j)),
        out_shape=SDS((k1, n2), F32), compiler_params=_params(("parallel", "arbitrary")))(a, b)


def _ds(start, d):
    return pl.ds(start, QB) if d == 1 else pl.ds(start, QB, stride=d)


def _fill_bias(tab_ref, bidx_ref, bias_sc, hp):
    for g in range(3):
        bi = bidx_ref[g]
        for h in range(2):
            acc = jnp.zeros((QB, 2 * QB), F32)
            for j in range(N_BUCKET):
                acc = jnp.where(bi == j, tab_ref[j, g * N_HEAD + hp * 2 + h], acc)
            bias_sc[g * 2 + h] = acc


def _block_starts(it, d, nb):
    rho = it // nb
    n = it % nb
    st = rho + d * QB * n
    stp = rho + d * QB * jnp.maximum(n - 1, 0)
    if d == 1:
        st, stp = pl.multiple_of(QB * it, QB), pl.multiple_of(QB * jnp.maximum(it - 1, 0), QB)
    return st, stp, n > 0


ATTN_BLOCKS = 4


def _bdot3(a, b, dims):
    return lax.dot_general(a, b, (dims, ((0,), (0,))), preferred_element_type=F32)


def _attn_operands(q_ref, k_ref, v_ref, bias_sc, g, d, nb, it0):
    two = nb > 1
    nk = 2 * QB if two else QB
    ii = lax.broadcasted_iota(jnp.int32, (QB, nk), 0)
    cc = lax.broadcasted_iota(jnp.int32, (QB, nk), 1)
    qs, ks, vs, pens, starts = [], [], [], [], []
    for u in range(ATTN_BLOCKS):
        st, stp, hasprev = _block_starts(it0 + u, d, nb)
        qf = q_ref[0, _ds(st, d), :]
        if two:
            kf = jnp.concatenate([k_ref[0, _ds(stp, d), :], k_ref[0, _ds(st, d), :]], axis=0).astype(BF16)
            vf = jnp.concatenate([v_ref[0, _ds(stp, d), :], v_ref[0, _ds(st, d), :]], axis=0).astype(BF16)
            own = jnp.logical_and(cc >= QB, ii >= cc - QB)
            prev = jnp.logical_and(jnp.logical_and(cc < QB, cc >= ii), hasprev)
            pen = jnp.where(jnp.logical_or(own, prev), 0.0, NEG)
        else:
            kf, vf = k_ref[0, _ds(st, d), :].astype(BF16), v_ref[0, _ds(st, d), :].astype(BF16)
            pen = jnp.where(ii >= cc, 0.0, NEG)
        for h in range(2):
            qs.append(_one_head(qf, h).astype(BF16))
            ks.append(kf)
            vs.append(vf)
            pens.append(pen + (bias_sc[g * 2 + h] if two else bias_sc[g * 2 + h, :, QB:2 * QB]))
        starts.append((st, stp))
    return _stack(qs), _stack(ks), _stack(vs), _stack(pens), starts


def _one_head(x, h):
    lane = lax.broadcasted_iota(jnp.int32, x.shape, 1)
    return jnp.where(lane >= HEAD if h == 1 else lane < HEAD, x, 0.0)


def _pick_heads(x, u):
    lane = lax.broadcasted_iota(jnp.int32, x.shape[1:], 1)
    return jnp.where(lane < HEAD, x[2 * u], x[2 * u + 1])


def _add_heads(x, u):
    return x[2 * u] + x[2 * u + 1]


def _attn_fwd(qkv3, rel_bias, bidx):
    bsz, s, _ = qkv3.shape
    rt = 256

    def body(tab_ref, bidx_ref, *refs):
        q_refs, k_refs, v_refs = refs[0:3], refs[3:6], refs[6:9]
        o_ref, lse_ref = refs[9:11]
        bias_sc, num_sc, den_sc, m_sc = refs[11:]
        pl.when(pl.program_id(1) == 0)(lambda: _fill_bias(tab_ref, bidx_ref, bias_sc, pl.program_id(0)))
        for g, d in enumerate(DILATIONS):
            nb = s // (QB * d)

            def blk(it, c, g=g, d=d, nb=nb):
                q, k, v, bias, starts = _attn_operands(q_refs[g], k_refs[g], v_refs[g], bias_sc, g, d, nb, it * ATTN_BLOCKS)
                sc = _bdot3(q, k, ((2,), (2,))) * SCALE + bias
                m = jnp.max(sc, axis=-1, keepdims=True)
                p = jnp.exp(sc - m)
                den = jnp.sum(p, axis=-1, keepdims=True)
                num = _bdot3(p.astype(BF16), v, ((2,), (1,)))
                den, m = jnp.broadcast_to(den, num.shape), jnp.broadcast_to(m, num.shape)
                for u, (st, _) in enumerate(starts):
                    num_sc[g, _ds(st, d), :] = _pick_heads(num, u)
                    den_sc[g, _ds(st, d), :] = _pick_heads(den, u)
                    m_sc[g, _ds(st, d), :] = _pick_heads(m, u)
                return c

            lax.fori_loop(0, s // QB // ATTN_BLOCKS, blk, 0)

        def merge(i, c):
            rows = pl.ds(pl.multiple_of(i * rt, rt), rt)
            m0, m1, m2 = m_sc[0, rows, :], m_sc[1, rows, :], m_sc[2, rows, :]
            mall = jnp.maximum(jnp.maximum(m0, m1), m2)
            w0, w1, w2 = jnp.exp(m0 - mall), jnp.exp(m1 - mall), jnp.exp(m2 - mall)
            num = w0 * num_sc[0, rows, :] + w1 * num_sc[1, rows, :] + w2 * num_sc[2, rows, :]
            den = w0 * den_sc[0, rows, :] + w1 * den_sc[1, rows, :] + w2 * den_sc[2, rows, :]
            o_ref[0, rows, :] = num / den
            lse_ref[0, rows, :] = mall + jnp.log(den)
            return c

        lax.fori_loop(0, s // rt, merge, 0)

    col = lambda w, g: (lambda hp, b: (b, 0, (w * 3 + g) * 4 + hp))
    in_specs = [pl.BlockSpec(memory_space=pltpu.SMEM), pl.BlockSpec((3, QB, 2 * QB), lambda hp, b: (0, 0, 0))]
    in_specs += [pl.BlockSpec((1, s, LANE), col(w, g)) for w in range(3) for g in range(3)]
    out_spec = pl.BlockSpec((1, s, LANE), lambda hp, b: (b, 0, hp))
    return pl.pallas_call(
        body, name="attn_fwd", grid=(4, bsz), in_specs=in_specs, out_specs=[out_spec, out_spec],
        out_shape=[SDS((bsz, s, WIDTH), F32), SDS((bsz, s, WIDTH), F32)],
        scratch_shapes=[pltpu.VMEM((6, QB, 2 * QB), F32), pltpu.VMEM((3, s, LANE), F32), pltpu.VMEM((3, s, LANE), F32),
                        pltpu.VMEM((3, s, LANE), F32)],
        compiler_params=_params(("arbitrary", "arbitrary")))(rel_bias, bidx, *([qkv3] * 9))


def _attn_bwd(qkv3, o3, lse3, do3, rel_bias, bidx):
    bsz, s, _ = qkv3.shape
    rt = 256

    def body(tab_ref, bidx_ref, *refs):
        q_refs, k_refs, v_refs = refs[0:3], refs[3:6], refs[6:9]
        o_ref, lse_ref, do_ref, dqkv_ref, db_ref, bias_sc, delta_sc, acc_sc = refs[9:]
        dq_refs, dk_refs, dv_refs = ([acc_sc.at[w * 3 + g] for g in range(3)] for w in range(3))

        @pl.when(pl.program_id(1) == 0)
        def _():
            _fill_bias(tab_ref, bidx_ref, bias_sc, pl.program_id(0))
            db_ref[...] = jnp.zeros_like(db_ref)

        def prep(i, c):
            rows = pl.ds(pl.multiple_of(i * rt, rt), rt)
            prod = do_ref[0, rows, :] * o_ref[0, rows, :]
            d0 = jnp.sum(prod[:, :HEAD], axis=-1, keepdims=True)
            d1 = jnp.sum(prod[:, HEAD:], axis=-1, keepdims=True)
            delta_sc[rows, :] = jnp.concatenate([jnp.broadcast_to(d0, (rt, HEAD)), jnp.broadcast_to(d1, (rt, HEAD))], axis=1)
            z = jnp.zeros((rt, LANE), F32)
            for g in range(3):
                dk_refs[g][0, rows, :] = z
                dv_refs[g][0, rows, :] = z
            return c

        lax.fori_loop(0, s // rt, prep, 0)
        for g, d in enumerate(DILATIONS):
            nb = s // (QB * d)

            def blk(it, c, g=g, d=d, nb=nb):
                q, k, v, bias, starts = _attn_operands(q_refs[g], k_refs[g], v_refs[g], bias_sc, g, d, nb, it * ATTN_BLOCKS)
                dos, lses, deltas = [], [], []
                for st, _ in starts:
                    dof, lsef, delf = do_ref[0, _ds(st, d), :], lse_ref[0, _ds(st, d), :], delta_sc[_ds(st, d), :]
                    for h in range(2):
                        dos.append(_one_head(dof, h).astype(BF16))
                        lses.append(lsef[:, HEAD * h:HEAD * h + 1])
                        deltas.append(delf[:, HEAD * h:HEAD * h + 1])
                do, lse, delta = _stack(dos), _stack(lses), _stack(deltas)
                p = jnp.exp(_bdot3(q, k, ((2,), (2,))) * SCALE + bias - lse)
                dv = _bdot3(p.astype(BF16), do, ((1,), (1,)))
                ds = p * (_bdot3(do, v, ((2,), (2,))) - delta)
                dsb = ds.astype(BF16)
                dq = _bdot3(dsb, k, ((2,), (1,))) * SCALE
                dk = _bdot3(dsb, q, ((1,), (1,))) * SCALE
                two = nb > 1
                for h in range(2):
                    dsum = sum(ds[2 * u + h] for u in range(ATTN_BLOCKS))
                    if two:
                        db_ref[0, g * 2 + h] += dsum
                    else:
                        db_ref[0, g * 2 + h, :, QB:2 * QB] += dsum
                for u, (st, stp) in enumerate(starts):
                    dq_refs[g][0, _ds(st, d), :] = _pick_heads(dq, u)
                    if two:
                        dk_refs[g][0, _ds(stp, d), :] += _add_heads(dk[:, :QB], u)
                        dv_refs[g][0, _ds(stp, d), :] += _add_heads(dv[:, :QB], u)
                    dk_refs[g][0, _ds(st, d), :] += _add_heads(dk[:, QB:] if two else dk, u)
                    dv_refs[g][0, _ds(st, d), :] += _add_heads(dv[:, QB:] if two else dv, u)
                return c

            lax.fori_loop(0, s // QB // ATTN_BLOCKS, blk, 0)

        def flush(i, c):
            rows = pl.ds(pl.multiple_of(i * rt, rt), rt)
            for j in range(9):
                dqkv_ref[j, 0, rows, :] = acc_sc[j, 0, rows, :].astype(BF16)
            return c

        lax.fori_loop(0, s // rt, flush, 0)

    col = lambda w, g: (lambda hp, b: (b, 0, (w * 3 + g) * 4 + hp))
    blk_spec = pl.BlockSpec((1, s, LANE), lambda hp, b: (b, 0, hp))
    in_specs = [pl.BlockSpec(memory_space=pltpu.SMEM), pl.BlockSpec((3, QB, 2 * QB), lambda hp, b: (0, 0, 0))]
    in_specs += [pl.BlockSpec((1, s, LANE), col(w, g)) for w in range(3) for g in range(3)]
    in_specs += [blk_spec] * 3
    out_specs = [pl.BlockSpec((9, 1, s, LANE), lambda hp, b: (0, b, 0, hp)), pl.BlockSpec((1, 6, QB, 2 * QB), lambda hp, b: (hp, 0, 0, 0))]
    out_shape = [SDS((9, bsz, s, WIDTH), BF16), SDS((4, 6, QB, 2 * QB), F32)]
    return pl.pallas_call(
        body, name="attn_bwd", grid=(4, bsz), in_specs=in_specs, out_specs=out_specs, out_shape=out_shape,
        scratch_shapes=[pltpu.VMEM((6, QB, 2 * QB), F32), pltpu.VMEM((s, LANE), F32), pltpu.VMEM((9, 1, s, LANE), F32)],
        compiler_params=_params(("parallel", "arbitrary")))(rel_bias, bidx, *([qkv3] * 9), o3, lse3, do3)


def _bias_grad(dbias, bidx):
    def body(db_ref, bidx_ref, o_ref):
        lane = lax.broadcasted_iota(jnp.int32, (1, LANE), 1)
        for g in range(3):
            bi = bidx_ref[g]
            for hp in range(4):
                for h in range(2):
                    mat = db_ref[hp, g * 2 + h]
                    row = jnp.zeros((1, LANE), F32)
                    for j in range(N_BUCKET):
                        part = jnp.sum(jnp.where(bi == j, mat, 0.0), axis=0, keepdims=True)
                        row = jnp.where(lane == j, jnp.sum(part, axis=1, keepdims=True), row)
                    hd = g * N_HEAD + hp * 2 + h
                    o_ref[hd:hd + 1, :] = row

    return pl.pallas_call(body, name="bias_grad", out_shape=SDS((3 * N_HEAD, LANE), F32), compiler_params=_params())(dbias, bidx)


def _pre_fn(r, k0, v, wl, al, w0, wup, a0, aup, kk_, ka_):
    u = w0 + _bdot(jnp.tanh(wl), wup)
    lw = -jnp.exp(-_softplus(-u) - 0.5)
    a = jax.nn.sigmoid(a0 + _bdot(al, aup))
    kkraw = k0 * kk_
    k = k0 * (1.0 + (a - 1.0) * ka_)
    return r, lw, k, v, kkraw, a


PRE_SPLIT = (0, WIDTH, 2 * WIDTH, 3 * WIDTH, 3 * WIDTH + LORA, 3 * WIDTH + 2 * LORA)


def _pre_pieces(prs):
    return [prs[:, a:b] for a, b in zip(PRE_SPLIT[:-1], PRE_SPLIT[1:])]


PRE_TT = 512


def _shifted(pr_ref, edge_ref, first, back):
    pr = pr_ref[0]
    tt = pr.shape[0]
    row = lax.broadcasted_iota(jnp.int32, (tt, 1), 0)
    if back:
        edge = jnp.where(first, 0.0, edge_ref[0, 7:8, :])
        return jnp.where(row == 0, edge, pltpu.roll(pr, 1, axis=0))
    edge = jnp.where(first, 0.0, edge_ref[0, 0:1, :])
    return jnp.where(row == tt - 1, edge, pltpu.roll(pr, tt - 1, axis=0))


def _rwkv_pre(pr3, mix, w0, wup, a0, aup, kk_, ka_):
    bsz, s, _ = pr3.shape
    tt = PRE_TT

    def body(pr_ref, edge_ref, mix_ref, w0_ref, wup_ref, a0_ref, aup_ref, kk_ref, ka_ref, *outs):
        pr = pr_ref[0]
        prev = _shifted(pr_ref, edge_ref, pl.program_id(1) == 0, True)
        prs = pr + (prev - pr) * mix_ref[...]
        vals = _pre_fn(*_pre_pieces(prs), w0_ref[...], wup_ref[...].astype(F32), a0_ref[...], aup_ref[...].astype(F32), kk_ref[...],
                       ka_ref[...])
        for o, val in zip(outs, vals):
            o[0] = val

    vec = lambda n: pl.BlockSpec((1, n), lambda b, i: (0, 0))
    mat = pl.BlockSpec((LORA, WIDTH), lambda b, i: (0, 0))
    in_specs = [pl.BlockSpec((1, tt, PR_COLS), lambda b, i: (b, i, 0)),
                pl.BlockSpec((1, 8, PR_COLS), lambda b, i: (b, jnp.maximum(i * (tt // 8) - 1, 0), 0)),
                vec(PR_COLS), vec(WIDTH), mat, vec(WIDTH), mat, vec(WIDTH), vec(WIDTH)]
    out_spec = pl.BlockSpec((1, tt, WIDTH), lambda b, i: (b, i, 0))
    return pl.pallas_call(
        body, name="rwkv_pre", grid=(bsz, s // tt), in_specs=in_specs, out_specs=[out_spec] * 6,
        out_shape=[SDS((bsz, s, WIDTH), F32)] * 6, compiler_params=_params(("parallel", "parallel")))(
            pr3, pr3, mix, w0, wup, a0, aup, kk_, ka_)


def _rwkv_pre_bwd(pr3, cots, mix, w0, wup, a0, aup, kk_, ka_):
    bsz, s, _ = pr3.shape
    tt = PRE_TT

    def body(pr_ref, edge_ref, c0, c1, c2, c3, c4, c5, mix_ref, w0_ref, wup_ref, a0_ref, aup_ref, kk_ref, ka_ref,
             dprs_ref, dmix_ref, dw0_ref, dwup_ref, da0_ref, daup_ref, dkk_ref, dka_ref):
        pr = pr_ref[0]
        prev = _shifted(pr_ref, edge_ref, pl.program_id(1) == 0, True)
        prs = pr + (prev - pr) * mix_ref[...]
        _, vjp = jax.vjp(_pre_fn, *_pre_pieces(prs), w0_ref[...], wup_ref[...].astype(F32), a0_ref[...], aup_ref[...].astype(F32),
                         kk_ref[...], ka_ref[...])
        grads = vjp(tuple(c[0] for c in (c0, c1, c2, c3, c4, c5)))
        for piece, a, b in zip(grads[:5], PRE_SPLIT[:-1], PRE_SPLIT[1:]):
            dprs_ref[0, :, a:b] = piece
        dw0, dwup, da0, daup, dkk, dka = grads[5:]
        dprs = dprs_ref[0]
        grads = (jnp.sum(dprs * (prev - pr), axis=0, keepdims=True), dw0, dwup, da0, daup, dkk, dka)
        refs = (dmix_ref, dw0_ref, dwup_ref, da0_ref, daup_ref, dkk_ref, dka_ref)
        first = jnp.logical_and(pl.program_id(0) == 0, pl.program_id(1) == 0)

        @pl.when(first)
        def _():
            for r_, g_ in zip(refs, grads):
                r_[...] = g_

        @pl.when(jnp.logical_not(first))
        def _():
            for r_, g_ in zip(refs, grads):
                r_[...] += g_

    vec = lambda n: pl.BlockSpec((1, n), lambda b, i: (0, 0))
    mat = pl.BlockSpec((LORA, WIDTH), lambda b, i: (0, 0))
    tile = pl.BlockSpec((1, tt, WIDTH), lambda b, i: (b, i, 0))
    in_specs = [pl.BlockSpec((1, tt, PR_COLS), lambda b, i: (b, i, 0)),
                pl.BlockSpec((1, 8, PR_COLS), lambda b, i: (b, jnp.maximum(i * (tt // 8) - 1, 0), 0))]
    in_specs += [tile] * 6 + [vec(PR_COLS), vec(WIDTH), mat, vec(WIDTH), mat, vec(WIDTH), vec(WIDTH)]
    out_specs = [pl.BlockSpec((1, tt, PR_COLS), lambda b, i: (b, i, 0)), vec(PR_COLS), vec(WIDTH), mat, vec(WIDTH), mat,
                 vec(WIDTH), vec(WIDTH)]
    out_shape = [SDS((bsz, s, PR_COLS), F32), SDS((1, PR_COLS), F32), SDS((1, WIDTH), F32), SDS((LORA, WIDTH), F32),
                 SDS((1, WIDTH), F32), SDS((LORA, WIDTH), F32), SDS((1, WIDTH), F32), SDS((1, WIDTH), F32)]
    return pl.pallas_call(
        body, name="rwkv_pre_bwd", grid=(bsz, s // tt), in_specs=in_specs, out_specs=out_specs, out_shape=out_shape,
        compiler_params=_params(("arbitrary", "arbitrary")))(pr3, pr3, *cots, mix, w0, wup, a0, aup, kk_, ka_)


def _shift_bwd(dprs3, mix):
    bsz, s, _ = dprs3.shape
    tt = PRE_TT
    nt = s // tt

    def body(d_ref, edge_ref, mix_ref, o_ref):
        nxt = _shifted(d_ref, edge_ref, pl.program_id(1) == nt - 1, False)
        m = mix_ref[...]
        o_ref[0] = (d_ref[0] * (1.0 - m) + nxt * m).astype(BF16)

    in_specs = [pl.BlockSpec((1, tt, PR_COLS), lambda b, i: (b, i, 0)),
                pl.BlockSpec((1, 8, PR_COLS), lambda b, i: (b, jnp.minimum((i + 1) * (tt // 8), s // 8 - 1), 0)),
                pl.BlockSpec((1, PR_COLS), lambda b, i: (0, 0))]
    return pl.pallas_call(
        body, name="shift_bwd", grid=(bsz, nt), in_specs=in_specs, out_specs=pl.BlockSpec((1, tt, PR_COLS), lambda b, i: (b, i, 0)),
        out_shape=SDS((bsz, s, PR_COLS), BF16), compiler_params=_params(("parallel", "parallel")))(dprs3, dprs3, mix)


_NN, _NT, _TN = ((2,), (1,)), ((2,), (2,)), ((1,), (1,))


def _dot3_bf16(a, b, dims):
    return lax.dot_general(a.astype(BF16), b.astype(BF16), (dims, ((0,), (0,))), preferred_element_type=F32)


class _Dots:
    def __init__(self, fwd):
        def make(dims, da_rule, db_rule):
            @jax.custom_vjp
            def f(a, b):
                return fwd(a, b, dims)

            f.defvjp(lambda a, b: (f(a, b), (a, b)), lambda res, g: (da_rule(*res, g), db_rule(*res, g)))
            return f

        one = _dot3_bf16
        self.mm = make(_NN, lambda a, b, g: one(g, b, _NT), lambda a, b, g: one(a, g, _TN))
        self.mm_nt = make(_NT, lambda a, b, g: one(g, b, _NN), lambda a, b, g: one(g, a, _TN))
        self.mm_tn = make(_TN, lambda a, b, g: one(b, g, _NT), lambda a, b, g: one(a, g, _NN))

        def powers(aab):
            ps = [aab]
            while 2 ** len(ps) < aab.shape[1]:
                ps.append(fwd(ps[-1], ps[-1], _NN))
            return ps

        def apply(ps, z, dims):
            for p in ps:
                z = z + fwd(p, z, dims)
            return z

        @jax.custom_vjp
        def solve(aab, z):
            return apply(powers(aab), z, _NN)

        def solve_fwd(aab, z):
            ps = powers(aab)
            x = apply(ps, z, _NN)
            return x, (ps, x)

        def solve_bwd(res, g):
            ps, x = res
            dz = apply(ps, g, _TN)
            return fwd(dz, x, _NT), dz

        solve.defvjp(solve_fwd, solve_bwd)
        self.solve = solve


_ONE_PASS = _Dots(_dot3_bf16)
_bmm, _bmm_tn = _ONE_PASS.mm, _ONE_PASS.mm_tn


def _chunk_fn(s0t, r, lw, k, v, kkraw, a, rk, lnw, lnb, first=False, d=_ONE_PASS):
    c = r.shape[1]
    at, rt, btc, ktc, gc, aab, arb, xv, arkv, ain, bin_ = _chunk_core(r, lw, k, v, kkraw, a, d)
    rs = d.mm(jnp.concatenate([at, rt], axis=1), s0t)
    u = d.solve(aab, rs[:, :c] + xv)
    y = rs[:, c:] + d.mm(arb, u) + arkv
    if first:
        y = _with_early_rows(y, r, lw, k, v, ain, bin_)
    gcol = jnp.sum(_diag(gc), axis=2, keepdims=True)
    sct = gcol * s0t + d.mm_tn(jnp.concatenate([btc, ktc], axis=1), jnp.concatenate([u, v], axis=1))
    return _post(y, r, k, v, rk, lnw, lnb), sct


def _diag(gc):
    return jnp.where(_masks(HEAD)[2], gc, 0.0)


def _with_early_rows(y, r, lw, k, v, ain, bin_):
    early = _early_rows(r[:2], lw[:2], k[:2], v[:2], ain[:2], bin_[:2])
    return jnp.concatenate([jnp.concatenate([early, y[:2, EARLY:]], axis=1), y[2:]], axis=0)


def _early_rows(r, lw, k, v, ain, bin_):
    cols = lambda x: _stack([jnp.transpose(x[h]) for h in range(2)])
    wc, bc, kc = cols(jnp.exp(lw)), cols(bin_), cols(k)
    st = jnp.zeros((2, HEAD, HEAD), F32)
    rows = []
    for t in range(EARLY):
        sa = _ONE_PASS.mm(ain[:, t:t + 1], st)
        st = st * wc[:, :, t:t + 1] + bc[:, :, t:t + 1] * sa + kc[:, :, t:t + 1] * v[:, t:t + 1]
        rows.append(_ONE_PASS.mm(r[:, t:t + 1], st))
    return jnp.concatenate(rows, axis=1)


def _chunk_rows(c):
    return pl.ds(c * CHUNK, CHUNK) if isinstance(c, int) else pl.ds(pl.multiple_of(c * CHUNK, CHUNK), CHUNK)


def _stack(xs):
    return jnp.concatenate([x[None] for x in xs], axis=0)


def _pairs(ref, chunks):
    tiles = [ref[0, _chunk_rows(c), :] for c in chunks]
    return _stack([t[:, HEAD * h:HEAD * h + HEAD] for t in tiles for h in range(2)])


def _unpair(vals, j):
    return jnp.concatenate([vals[2 * j], vals[2 * j + 1]], axis=1)


def _masks(c):
    ii = lax.broadcasted_iota(jnp.int32, (c, c), 0)
    jj = lax.broadcasted_iota(jnp.int32, (c, c), 1)
    return ii > jj, ii >= jj, ii == jj


@jax.custom_vjp
def _running_sum(lw):
    return _tri_dot(lw, _NN)


def _tri_dot(x, dims):
    g_, c, _ = x.shape
    tri = jnp.broadcast_to(_masks(c)[1].astype(BF16), (g_, c, c))
    head = x.astype(BF16)
    rest = (x - head.astype(F32)).astype(BF16)
    return lax.dot_general(tri, head, (dims, ((0,), (0,))), preferred_element_type=F32) + \
        lax.dot_general(tri, rest, (dims, ((0,), (0,))), preferred_element_type=F32)


_running_sum.defvjp(lambda lw: (_running_sum(lw), None), lambda _, ct: (_tri_dot(ct, _TN),))


def _chunk_core(r, lw, k, v, kkraw, a, d=_ONE_PASS):
    g_, c = r.shape[0], r.shape[1]
    nrm = jnp.sqrt(jnp.sum(kkraw * kkraw, axis=-1, keepdims=True))
    kkn = kkraw / jnp.maximum(nrm, 1e-12)
    ain, bin_ = -kkn, kkn * a
    strict, incl, _ = _masks(c)
    lg = _running_sum(lw)
    g, gp, gi = jnp.exp(lg), jnp.exp(lg - lw), jnp.exp(-lg)
    at, rt, bt, kt = ain * gp, r * g, bin_ * gi, k * gi
    aa = d.mm_nt(jnp.concatenate([at, rt], axis=1), jnp.concatenate([bt, kt], axis=1))
    aab = jnp.where(strict, aa[:, :c, :c], 0.0)
    aak = jnp.where(strict, aa[:, :c, c:], 0.0)
    arb = jnp.where(incl, aa[:, c:, :c], 0.0)
    ark = jnp.where(incl, aa[:, c:, c:], 0.0)
    akv = d.mm(jnp.concatenate([aak, ark], axis=1), v)
    gc = g[:, c - 1:c, :]
    return at, rt, bt * gc, kt * gc, gc, aab, arb, akv[:, :c], akv[:, c:], ain, bin_


def _post(y, r, k, v, rk, lnw, lnb):
    mu = jnp.mean(y, axis=-1, keepdims=True)
    var = jnp.mean(jnp.square(y - mu), axis=-1, keepdims=True)
    yn = (y - mu) * lax.rsqrt(var + GN_EPS) * lnw + lnb
    return yn + jnp.sum(r * k * rk, axis=-1, keepdims=True) * v


def _chunk_consts(r, lw, k, v, kkraw, a, first=False):
    d = _ONE_PASS
    at, rt, btc, ktc, gc, aab, arb, xv, arkv, ain, bin_ = _chunk_core(r, lw, k, v, kkraw, a, d)
    z = d.solve(aab, jnp.concatenate([at, xv], axis=2))
    ryv = jnp.concatenate([rt, arkv], axis=2) + d.mm(arb, z)
    if first:
        ryv = jnp.concatenate([ryv[:, :, :HEAD], _with_early_rows(ryv[:, :, HEAD:], r, lw, k, v, ain, bin_)], axis=2)
    mkv = d.mm_tn(btc, z) + jnp.concatenate([_diag(gc), d.mm_tn(ktc, v)], axis=2)
    return mkv, ryv


def _rwkv_scan(ins, rk, lnw, lnb):
    bsz, s, _ = ins[0].shape
    nch = s // CHUNK

    def consts_body(r_ref, lw_ref, k_ref, v_ref, kk_ref, a_ref, mkv_ref, ry_ref, yv_ref):
        def group(i, carry):
            chunks = [i * CHUNK_GROUP + j for j in range(CHUNK_GROUP)]
            mkv, ryv = _chunk_consts(*[_pairs(ref, chunks) for ref in (r_ref, lw_ref, k_ref, v_ref, kk_ref, a_ref)],
                                     first=isinstance(i, int) and i == 0)
            for j, c in enumerate(chunks):
                for h in range(2):
                    mkv_ref[0, 0, c, h] = mkv[2 * j + h]
                ry_ref[0, _chunk_rows(c), :] = jnp.concatenate([ryv[2 * j][:, :HEAD], ryv[2 * j + 1][:, :HEAD]], axis=1)
                yv_ref[0, _chunk_rows(c), :] = jnp.concatenate([ryv[2 * j][:, HEAD:], ryv[2 * j + 1][:, HEAD:]], axis=1)
            return carry

        group(0, 0)
        lax.fori_loop(1, nch // CHUNK_GROUP, group, 0)

    tile = pl.BlockSpec((1, s, LANE), lambda b, hp: (b, 0, hp))
    vec = pl.BlockSpec((1, LANE), lambda b, hp: (0, hp))
    mkv_spec = pl.BlockSpec((1, 1, nch, 2, HEAD, LANE), lambda b, hp: (b, hp, 0, 0, 0, 0))
    st_spec = pl.BlockSpec((1, 1, nch, 2, HEAD, HEAD), lambda b, hp: (b, hp, 0, 0, 0, 0))
    mkv, ry, yv = pl.pallas_call(
        consts_body, name="rwkv_consts", grid=(bsz, 4), in_specs=[tile] * 6, out_specs=[mkv_spec, tile, tile],
        out_shape=[SDS((bsz, 4, nch, 2, HEAD, LANE), F32), SDS((bsz, s, WIDTH), F32), SDS((bsz, s, WIDTH), F32)],
        compiler_params=_params(("parallel", "parallel")))(*ins)

    states = _chunk_recurrence(mkv, None, "rwkv_states")

    def out_body(ry_ref, yv_ref, r_ref, k_ref, v_ref, st_ref, rk_ref, lnw_ref, lnb_ref, o_ref):
        y, r, k, v, rk_, lnw_, lnb_ = _scan_rows(ry_ref, yv_ref, r_ref, k_ref, v_ref, st_ref, rk_ref, lnw_ref, lnb_ref)
        o = _post(y, r, k, v, rk_, lnw_, lnb_)
        for j in range(CHUNK_GROUP):
            o_ref[0, _chunk_rows(j), :] = _unpair(o, j)

    o = pl.pallas_call(
        out_body, name="rwkv_out", grid=(bsz, 4, nch // CHUNK_GROUP), in_specs=_group_specs(5), out_specs=_group_specs(1)[0],
        out_shape=SDS((bsz, s, WIDTH), F32),
        compiler_params=_params(("parallel", "parallel", "parallel")))(ry, yv, ins[0], ins[2], ins[3], states, rk, lnw, lnb)
    return o, states, (mkv, ry, yv)


def _group_specs(n_tiles):
    tile = pl.BlockSpec((1, CHUNK_GROUP * CHUNK, LANE), lambda b, hp, t: (b, t, hp))
    if n_tiles == 1:
        return [tile]
    st = pl.BlockSpec((1, 1, CHUNK_GROUP, 2, HEAD, HEAD), lambda b, hp, t: (b, hp, t, 0, 0, 0))
    vec = pl.BlockSpec((1, LANE), lambda b, hp, t: (0, hp))
    return [tile] * n_tiles + [st] + [vec] * 3


def _scan_rows(ry_ref, yv_ref, r_ref, k_ref, v_ref, st_ref, rk_ref, lnw_ref, lnb_ref):
    chunks = list(range(CHUNK_GROUP))
    ry, yv, r, k, v = (_pairs(ref, chunks) for ref in (ry_ref, yv_ref, r_ref, k_ref, v_ref))
    st = _stack([st_ref[0, 0, c, h] for c in chunks for h in range(2)])
    vecs = [_stack([ref[:, HEAD * h:HEAD * h + HEAD] for _ in chunks for h in range(2)]) for ref in (rk_ref, lnw_ref, lnb_ref)]
    return (_bmm(ry, st) + yv, r, k, v, *vecs)


def _chunk_recurrence(mkv, q, name):
    bsz, _, nch = mkv.shape[:3]
    pairs = [(hp, h) for hp in range(4) for h in range(2)]

    def body(*refs):
        mkv_ref, out_ref, acc = refs[0], refs[-2], refs[-1]
        acc[...] = jnp.zeros_like(acc)

        def step(i, carry):
            c = i if q is None else nch - 1 - i
            cur = acc[...]
            for j, (hp, h) in enumerate(pairs):
                out_ref[0, hp, c, h] = cur[j]
            m = _stack([mkv_ref[0, hp, c, h] for hp, h in pairs])
            if q is None:
                acc[...] = _bmm(m[:, :, :HEAD], cur) + m[:, :, HEAD:]
            else:
                acc[...] = _bmm_tn(m[:, :, :HEAD], cur) + _stack([refs[1][0, hp, c, h] for hp, h in pairs])
            return carry

        lax.fori_loop(0, nch, step, 0)

    spec = lambda w: pl.BlockSpec((1, 4, nch, 2, HEAD, w), lambda b: (b, 0, 0, 0, 0, 0))
    return pl.pallas_call(
        body, name=name, grid=(bsz,), in_specs=[spec(LANE)] + ([] if q is None else [spec(HEAD)]), out_specs=spec(HEAD),
        out_shape=SDS((bsz, 4, nch, 2, HEAD, HEAD), F32), scratch_shapes=[pltpu.VMEM((8, HEAD, HEAD), F32)],
        compiler_params=_params(("parallel",)))(*([mkv] if q is None else [mkv, q]))


def _rwkv_scan_bwd(ins, states, consts, do3, rk, lnw, lnb):
    bsz, s, _ = ins[0].shape
    nch = s // CHUNK

    mkv, ry, yv = consts

    def q_body(do_ref, ry_ref, yv_ref, r_ref, k_ref, v_ref, st_ref, rk_ref, lnw_ref, lnb_ref, q_ref):
        y, r, k, v, rk_, lnw_, lnb_ = _scan_rows(ry_ref, yv_ref, r_ref, k_ref, v_ref, st_ref, rk_ref, lnw_ref, lnb_ref)
        _, vjp = jax.vjp(lambda y_: _post(y_, r, k, v, rk_, lnw_, lnb_), y)
        (dy,) = vjp(_pairs(do_ref, list(range(CHUNK_GROUP))))
        q = _bmm_tn(_pairs(ry_ref, list(range(CHUNK_GROUP))), dy)
        for j in range(CHUNK_GROUP):
            for h in range(2):
                q_ref[0, 0, j, h] = q[2 * j + h]

    specs = _group_specs(6)
    q = pl.pallas_call(
        q_body, name="rwkv_q", grid=(bsz, 4, nch // CHUNK_GROUP), in_specs=specs, out_specs=specs[6],
        out_shape=SDS((bsz, 4, nch, 2, HEAD, HEAD), F32),
        compiler_params=_params(("parallel", "parallel", "parallel")))(do3, ry, yv, ins[0], ins[2], ins[3], states, rk, lnw, lnb)

    dstates = _chunk_recurrence(mkv, q, "rwkv_dstates")

    def body(r_ref, lw_ref, k_ref, v_ref, kk_ref, a_ref, st_ref, dst_ref, do_ref, rk_ref, lnw_ref, lnb_ref,
             dr_ref, dlw_ref, dk_ref, dv_ref, dkk_ref, da_ref, drk_ref, dlnw_ref, dlnb_ref):
        chunks = list(range(BWD_GROUP))
        par_refs = (drk_ref, dlnw_ref, dlnb_ref)

        @pl.when(jnp.logical_and(pl.program_id(1) == 0, pl.program_id(2) == 0))
        def _():
            for ref in par_refs:
                ref[...] = jnp.zeros_like(ref)

        def group(first):
            per_pair = lambda ref: _stack([ref[0, 0, c, h] for c in chunks for h in range(2)])
            vecs = [_stack([ref[:, HEAD * h:HEAD * h + HEAD] for _ in chunks for h in range(2)]) for ref in (rk_ref, lnw_ref, lnb_ref)]
            _, vjp = jax.vjp(functools.partial(_chunk_fn, first=first, d=_ONE_PASS), per_pair(st_ref),
                             *[_pairs(ref, chunks) for ref in (r_ref, lw_ref, k_ref, v_ref, kk_ref, a_ref)], *vecs)
            grads = vjp((_pairs(do_ref, chunks), per_pair(dst_ref)))
            for ref, cot in zip((dr_ref, dlw_ref, dk_ref, dv_ref, dkk_ref, da_ref), grads[1:7]):
                for j, c in enumerate(chunks):
                    ref[0, _chunk_rows(c), :] = _unpair(cot, j)
            for ref, g_ in zip(par_refs, grads[7:10]):
                ref[...] += jnp.concatenate([sum(g_[2 * j + h] for j in range(BWD_GROUP)) for h in range(2)], axis=1)

        pl.when(pl.program_id(2) == 0)(functools.partial(group, True))
        pl.when(pl.program_id(2) != 0)(functools.partial(group, False))

    tt = BWD_GROUP * CHUNK
    tile = pl.BlockSpec((1, tt, LANE), lambda hp, b, t: (b, t, hp))
    vec = pl.BlockSpec((1, LANE), lambda hp, b, t: (0, hp))
    st_spec = pl.BlockSpec((1, 1, BWD_GROUP, 2, HEAD, HEAD), lambda hp, b, t: (b, hp, t, 0, 0, 0))
    outs = pl.pallas_call(
        body, name="rwkv_scan_bwd", grid=(4, bsz, s // tt), in_specs=[tile] * 6 + [st_spec, st_spec, tile] + [vec] * 3,
        out_specs=[tile] * 6 + [vec] * 3,
        out_shape=[SDS((bsz, s, WIDTH), F32)] * 6 + [SDS((1, WIDTH), F32)] * 3,
        compiler_params=_params(("parallel", "arbitrary", "arbitrary")))(*ins, states, dstates, do3, rk, lnw, lnb)
    return outs[:6], outs[6:]


def _head(o_attn, o_rwkv, z_attn, z_rwkv, gm, x2, tgt, wua, wur, wout, g2):
    n = x2.shape[0]
    tm = 256
    nt = n // tm
    d = D_MODEL

    def body(oa_ref, or_ref, za_ref, zr_ref, gm_ref, x_ref, t_ref, wua_ref, wur_ref, wout_ref, g2_ref,
             dxo_ref, doa_ref, dor_ref, dza_ref, dzr_ref, dgm_ref, dwua_ref, dwur_ref, dwout_ref, dg2_ref, loss_ref, lacc):
        i = pl.program_id(0)
        oa, orw, za, zr = oa_ref[...], or_ref[...], za_ref[...], zr_ref[...]
        ga, gb = gm_ref[:, 0:d], gm_ref[:, d:2 * d]
        am = (oa * _silu(za)).astype(BF16)
        bm = (orw * _silu(zr)).astype(BF16)
        ya, yb = _dot(am, wua_ref[...]), _dot(bm, wur_ref[...])
        sa, sb = jax.nn.sigmoid(ga), jax.nn.sigmoid(gb)
        merged = (sa * ya + sb * yb).astype(BF16)
        out = _dot(merged, wout_ref[...])
        rs = lax.rsqrt(jnp.mean(out * out, axis=-1, keepdims=True) + RMS_EPS)
        g2 = g2_ref[...]
        err = x_ref[...] + out * rs * g2 - t_ref[...]
        lpart = jnp.sum(err * err, axis=0, keepdims=True)
        dxo = err * (1.0 / d)
        dxo_ref[...] = dxo
        dg2 = jnp.sum(dxo * out * rs, axis=0, keepdims=True)
        gd = dxo * g2
        dout = (rs * (gd - out * (rs * rs) * jnp.mean(gd * out, axis=-1, keepdims=True))).astype(BF16)
        dmerged = _dot_nt(dout, wout_ref[...])
        dwout = _dot_tn(merged, dout)
        dya, dyb = (dmerged * sa).astype(BF16), (dmerged * sb).astype(BF16)
        dgm_ref[:, 0:d] = (dmerged * ya * sa * (1.0 - sa)).astype(BF16)
        dgm_ref[:, d:2 * d] = (dmerged * yb * sb * (1.0 - sb)).astype(BF16)
        dam, dbm = _dot_nt(dya, wua_ref[...]), _dot_nt(dyb, wur_ref[...])
        dwua, dwur = _dot_tn(am, dya), _dot_tn(bm, dyb)
        doa_ref[...] = dam * _silu(za)
        dza_ref[...] = (dam * oa * _dsilu(za)).astype(BF16)
        dor_ref[...] = dbm * _silu(zr)
        dzr_ref[...] = (dbm * orw * _dsilu(zr)).astype(BF16)

        @pl.when(i == 0)
        def _():
            dwua_ref[...], dwur_ref[...], dwout_ref[...], dg2_ref[...], lacc[...] = dwua, dwur, dwout, dg2, lpart

        @pl.when(i != 0)
        def _():
            dwua_ref[...] += dwua
            dwur_ref[...] += dwur
            dwout_ref[...] += dwout
            dg2_ref[...] += dg2
            lacc[...] += lpart

        @pl.when(i == nt - 1)
        def _():
            loss_ref[...] = jnp.sum(lacc[...], axis=1, keepdims=True) * (0.5 / d)

    t512 = pl.BlockSpec((tm, WIDTH), lambda i: (i, 0))
    t1k = pl.BlockSpec((tm, d), lambda i: (i, 0))
    t2k = pl.BlockSpec((tm, 2 * d), lambda i: (i, 0))
    full = lambda r, c: pl.BlockSpec((r, c), lambda i: (0, 0))
    return pl.pallas_call(
        body, name="head_fwd_bwd", grid=(nt,),
        in_specs=[t512, t512, t512, t512, t2k, t1k, t1k, full(WIDTH, d), full(WIDTH, d), full(d, d), full(1, d)],
        out_specs=[t1k, t512, t512, t512, t512, t2k, full(WIDTH, d), full(WIDTH, d), full(d, d), full(1, d), full(1, 1)],
        out_shape=[SDS((n, d), F32), SDS((n, WIDTH), F32), SDS((n, WIDTH), F32), SDS((n, WIDTH), BF16), SDS((n, WIDTH), BF16),
                   SDS((n, 2 * d), BF16), SDS((WIDTH, d), F32), SDS((WIDTH, d), F32), SDS((d, d), F32), SDS((1, d), F32), SDS((1, 1), F32)],
        scratch_shapes=[pltpu.VMEM((1, d), F32)],
        compiler_params=_params(("arbitrary",)))(o_attn, o_rwkv, z_attn, z_rwkv, gm, x2, tgt, wua, wur, wout, g2)


def _mesh_pos():
    x, y, c = lax.axis_index("x"), lax.axis_index("y"), lax.axis_index("c")
    return 4 * x + 2 * y + c


def _coords(idx):
    return (idx // 4, (idx // 2) % 2, idx % 2)


def _exchange(srcs, to_all, name):
    n = len(srcs)

    def body(*refs):
        src_refs, dst_refs = refs[:n], refs[n:2 * n]
        send_sems, recv_sems, local_sems = refs[2 * n:]
        me = _mesh_pos()

        def piece(i, j):
            return src_refs[i] if to_all[i] else src_refs[i].at[j]

        def remote(i, off, peer, block, slot):
            return pltpu.make_async_remote_copy(src_ref=piece(i, block), dst_ref=dst_refs[i].at[slot],
                                                send_sem=send_sems.at[i, off - 1], recv_sem=recv_sems.at[i, off - 1],
                                                device_id=_coords(peer), device_id_type=MESH)

        local = [pltpu.make_async_copy(piece(i, me), dst_refs[i].at[me], local_sems.at[i]) for i in range(n)]
        for cp in local:
            cp.start()
        sends = []
        for off in range(1, N_DEV):
            to = (me + off) % N_DEV
            for i in range(n):
                sends.append(remote(i, off, to, to, me))
                sends[-1].start()
        for off in range(1, N_DEV):
            frm = (me + N_DEV - off) % N_DEV
            for i in range(n):
                remote(i, off, frm, me, frm).wait_recv()
        for cp in sends:
            cp.wait_send()
        for cp in local:
            cp.wait()

    outs = pl.pallas_call(
        body, name=name, in_specs=[pl.BlockSpec(memory_space=pltpu.HBM)] * n, out_specs=[pl.BlockSpec(memory_space=pltpu.HBM)] * n,
        out_shape=[SDS((N_DEV,) + s.shape[-2:], s.dtype) for s in srcs],
        scratch_shapes=[pltpu.SemaphoreType.DMA((n, N_DEV - 1)), pltpu.SemaphoreType.DMA((n, N_DEV - 1)), pltpu.SemaphoreType.DMA((n,))],
        compiler_params=pltpu.CompilerParams())(*srcs)
    return outs


_HBM = pl.BlockSpec(memory_space=pltpu.HBM)
_SEM = pl.BlockSpec(memory_space=pltpu.SEMAPHORE)
_EFFECT = pltpu.SideEffectType.DATAFLOW_SIDE_EFFECTING


def _send_copy(src_ref, land_ref, to_all, send_sems, recv_sems, i, off, block, slot, peer):
    k = i * (N_DEV - 1) + off - 1
    return pltpu.make_async_remote_copy(src_ref=src_ref if to_all else src_ref.at[block], dst_ref=land_ref.at[slot],
                                        send_sem=send_sems.at[k], recv_sem=recv_sems.at[k],
                                        device_id=_coords(peer), device_id_type=MESH)


def _send_start(srcs, to_all, name):
    n = len(srcs)

    def body(*refs):
        src_refs, land_refs = refs[:n], refs[n:2 * n]
        send_sems, recv_sems = refs[2 * n:2 * n + 2]
        me = _mesh_pos()
        for off in range(1, N_DEV):
            to = (me + off) % N_DEV
            for i in range(n):
                _send_copy(src_refs[i], land_refs[i], to_all, send_sems, recv_sems, i, off, to, me, to).start()
        refs[-1][...] = jnp.zeros_like(refs[-1])

    lands = [jnp.zeros((N_DEV,) + s.shape[-2:], s.dtype) for s in srcs]
    hbm = [pltpu.HBM(a.shape, a.dtype) for a in list(srcs) + lands]
    sems = pltpu.SemaphoreType.DMA((n * (N_DEV - 1),))
    outs = pl.pallas_call(
        body, name=name, out_shape=(sems, sems, *hbm, SDS((8, LANE), BF16)),
        in_specs=(_HBM,) * (2 * n), out_specs=(_SEM, _SEM) + (_HBM,) * (2 * n) + (pl.BlockSpec(memory_space=pltpu.VMEM),),
        input_output_aliases={i: 2 + i for i in range(2 * n)}, compiler_params=pltpu.CompilerParams(has_side_effects=_EFFECT),
    )(*[pltpu.with_memory_space_constraint(a, pltpu.HBM) for a in list(srcs) + lands])
    return outs[0], outs[1], outs[2:2 + n], outs[2 + n:2 + 2 * n], outs[-1]


def _send_wait(send_sems, recv_sems, srcs_thru, lands_thru, to_all, after, name):
    n = len(srcs_thru)

    def body(*refs):
        src_refs, land_refs = refs[:n], refs[n:2 * n]
        send_sems, recv_sems = refs[2 * n:2 * n + 2]
        me = _mesh_pos()
        for off in range(1, N_DEV):
            to, frm = (me + off) % N_DEV, (me + N_DEV - off) % N_DEV
            for i in range(n):
                _send_copy(src_refs[i], land_refs[i], to_all, send_sems, recv_sems, i, off, to, me, to).wait_send()
                _send_copy(src_refs[i], land_refs[i], to_all, send_sems, recv_sems, i, off, me, frm, frm).wait_recv()

    hbm = tuple(pltpu.HBM(a.shape, a.dtype) for a in list(srcs_thru) + list(lands_thru))
    outs = pl.pallas_call(
        body, name=name, out_shape=hbm, in_specs=(_HBM,) * (2 * n) + (_SEM, _SEM, pl.BlockSpec(memory_space=pl.ANY)),
        out_specs=(_HBM,) * (2 * n), input_output_aliases={i: i for i in range(2 * n)},
        compiler_params=pltpu.CompilerParams(has_side_effects=_EFFECT),
    )(*srcs_thru, *lands_thru, send_sems, recv_sems, after)
    return outs[n:]


def _gather(srcs, after, name):
    n = len(srcs)

    def body(*refs):
        src_refs, dst_refs = refs[:n], refs[n + 1:2 * n + 1]
        send_sems, recv_sems, local_sems = refs[2 * n + 1:]
        x, y, c = lax.axis_index("x"), lax.axis_index("y"), lax.axis_index("c")
        me, sibling = (x, y, c), (x, y, 1 - c)
        chips = [(1 - x, y), (x, 1 - y), (1 - x, 1 - y)]

        def slot(i, dev):
            return dst_refs[i].at[4 * dev[0] + 2 * dev[1] + dev[2]]

        def copy(i, k, block, to, own=False):
            return pltpu.make_async_remote_copy(src_ref=src_refs[i] if own else slot(i, block), dst_ref=slot(i, block),
                                                send_sem=send_sems.at[i, k], recv_sem=recv_sems.at[i, k],
                                                device_id=to, device_id_type=MESH)

        local = [pltpu.make_async_copy(src_refs[i], slot(i, me), local_sems.at[i]) for i in range(n)]
        for cp in local:
            cp.start()
        sends = []
        for i in range(n):
            sends.append(copy(i, 0, me, sibling, own=True))
            sends += [copy(i, 1 + j, me, (*chip, c), own=True) for j, chip in enumerate(chips)]
        for cp in sends:
            cp.start()
        for j, chip in enumerate(chips):
            for i in range(n):
                copy(i, 1 + j, (*chip, c), me).wait_recv()
                sends.append(copy(i, 4 + j, (*chip, c), sibling))
                sends[-1].start()
        for i in range(n):
            copy(i, 0, sibling, me).wait_recv()
            for j, chip in enumerate(chips):
                copy(i, 4 + j, (*chip, 1 - c), me).wait_recv()
        for cp in sends:
            cp.wait_send()
        for cp in local:
            cp.wait()

    return pl.pallas_call(
        body, name=name, in_specs=[pl.BlockSpec(memory_space=pltpu.HBM)] * n + [pl.BlockSpec(memory_space=pl.ANY)],
        out_specs=[pl.BlockSpec(memory_space=pltpu.HBM)] * n, out_shape=[SDS((N_DEV,) + s.shape, s.dtype) for s in srcs],
        scratch_shapes=[pltpu.SemaphoreType.DMA((n, N_DEV - 1)), pltpu.SemaphoreType.DMA((n, N_DEV - 1)), pltpu.SemaphoreType.DMA((n,))],
        compiler_params=pltpu.CompilerParams())(*srcs, after)


def _adamw(parts, w, m, v, tr, name, own=None):
    rows, cols = w.shape
    c1, c2 = 1.0 - ADAM_B1 ** ADAM_STEP, 1.0 - ADAM_B2 ** ADAM_STEP

    def body(p_ref, *refs):
        w_ref, m_ref, v_ref, g_ref, d_ref, nm_ref, nv_ref = refs[-7:]
        me = _mesh_pos()

        def part(j):
            return p_ref[j] if own is None else jnp.where(me == j, refs[0][...], p_ref[j])

        g = part(0).astype(F32)
        for j in range(1, N_DEV):
            g = g + part(j).astype(F32)
        nm = ADAM_B1 * m_ref[...] + (1.0 - ADAM_B1) * g
        nv = ADAM_B2 * v_ref[...] + (1.0 - ADAM_B2) * jnp.square(g)
        g_ref[...] = g
        nm_ref[...] = nm
        nv_ref[...] = nv
        d_ref[...] = -ADAM_LR * ((nm / c1) / (jnp.sqrt(nv / c2) + ADAM_EPS) + ADAM_WD * w_ref[...])

    t = pl.BlockSpec((tr, cols), lambda i: (i, 0))
    extra = [] if own is None else [own]
    return pl.pallas_call(
        body, name=name, grid=(rows // tr,), in_specs=[pl.BlockSpec((N_DEV, tr, cols), lambda i: (0, i, 0))] + [t] * (3 + len(extra)),
        out_specs=[t] * 4, out_shape=[SDS((rows, cols), F32)] * 4, compiler_params=_params(("parallel",)))(parts, *extra, w, m, v)


SHARDED = (("w_in", D_MODEL, IN_COLS // N_DEV, True, 128), ("w_up_attn", WIDTH, D_MODEL // N_DEV, True, WIDTH),
           ("w_up_rwkv", WIDTH, D_MODEL // N_DEV, True, WIDTH), ("w_out", D_MODEL // N_DEV, D_MODEL, False, D_MODEL // N_DEV),
           ("rwkv_w_up", LORA, WIDTH // N_DEV, True, LORA), ("rwkv_a_up", LORA, WIDTH // N_DEV, True, LORA))
LOSS_SLOT = sum(n for _, n in SMALL)


def _pack_small(small, extra=None):
    flat = [small[n].reshape(-1).astype(F32) for n, _ in SMALL]
    flat.append(jnp.zeros((1,), F32) if extra is None else extra.reshape(1))
    flat.append(jnp.zeros((SMALL_ROWS * LANE - LOSS_SLOT - 1,), F32))
    return jnp.concatenate(flat).reshape(SMALL_ROWS, LANE)


def _unpack_small(packed, shapes):
    flat = packed.reshape(-1)
    out, off = {}, 0
    for n, cnt in SMALL:
        out[n] = flat[off:off + cnt].reshape(shapes[n])
        off += cnt
    return out, flat[LOSS_SLOT]


def _whole(gathered, by_cols):
    if not by_cols:
        return gathered.reshape(-1, gathered.shape[-1])
    return gathered.transpose(1, 0, 2).reshape(gathered.shape[1], -1)


def _per_owner(full, by_cols):
    if not by_cols:
        return full.reshape(N_DEV, -1, full.shape[-1])
    return full.reshape(full.shape[0], N_DEV, -1).transpose(1, 0, 2)


def _local_step(x, loss_target, sm, wts):
    bsz, s, d = x.shape
    n = bsz * s
    x2, tgt = x.reshape(n, d), loss_target.reshape(n, d)
    bidx = jnp.asarray(_bucket_tables())
    w_in = wts["w_in"]
    segs = (("qkv", 0, QKV_COLS, 1536), ("za", OFF_ZA, WIDTH, 512), ("pr", OFF_PR, PR_COLS, PR_COLS), ("zr", OFF_ZR, WIDTH, 512),
            ("gm", OFF_GM, 2 * D_MODEL, 1024))

    h, rs = _prenorm(x2, sm["pre_norm_gain"])
    w_seg = {nm: w_in[:, off:off + cnt] for nm, off, cnt, _ in segs}
    proj = {nm: _mm(h, w_seg[nm], tn, "proj_" + nm) for nm, _, _, tn in segs}
    qkv3 = proj["qkv"].reshape(bsz, s, QKV_COLS)
    pr3 = proj["pr"].reshape(bsz, s, PR_COLS)

    o_attn, lse = _attn_fwd(qkv3, sm["rel_bias"], bidx)
    rk = sm["rwkv_r_k"].reshape(1, WIDTH)
    pre_args = (sm["rwkv_shift_mix"], sm["rwkv_w0"], wts["rwkv_w_up"], sm["rwkv_a0"], wts["rwkv_a_up"], sm["rwkv_k_k"], sm["rwkv_k_a"])
    scan_in = _rwkv_pre(pr3, *pre_args)
    o_rwkv, states, consts = _rwkv_scan(scan_in, rk, sm["rwkv_ln_w"], sm["rwkv_ln_b"])

    (dxo, do_attn, do_rwkv, dza, dzr, dgm, g_wua, g_wur, g_wout, g_post, loss) = _head(
        o_attn.reshape(n, WIDTH), o_rwkv.reshape(n, WIDTH), proj["za"], proj["zr"], proj["gm"], x2, tgt,
        wts["w_up_attn"], wts["w_up_rwkv"], wts["w_out"], sm["post_norm_gain"])

    dqkv, dbias = _attn_bwd(qkv3, o_attn, lse, do_attn.reshape(bsz, s, WIDTH), sm["rel_bias"], bidx)
    g_bias = _bias_grad(dbias, bidx)[:, :N_BUCKET].T

    scan_cots, (g_rk, g_lnw, g_lnb) = _rwkv_scan_bwd(scan_in, states, consts, do_rwkv.reshape(bsz, s, WIDTH), rk, sm["rwkv_ln_w"],
                                                     sm["rwkv_ln_b"])
    dprs, g_mix, g_w0, g_wup, g_a0, g_aup, g_kk, g_ka = _rwkv_pre_bwd(pr3, scan_cots, *pre_args)
    dpr = _shift_bwd(dprs, sm["rwkv_shift_mix"]).reshape(n, PR_COLS)

    dsegs = [(dqkv.reshape(9, n, WIDTH), 0, QKV_COLS, WIDTH), (dza, OFF_ZA, WIDTH, WIDTH), (dpr, OFF_PR, PR_COLS, PR_COLS),
             (dzr, OFF_ZR, WIDTH, WIDTH), (dgm, OFF_GM, 2 * D_MODEL, D_MODEL)]
    full = {"w_in": jnp.concatenate([_mm_tn(h, t, tn, "gw_in_%d" % j) for j, (t, _, _, tn) in enumerate(dsegs)], axis=1),
            "w_up_attn": g_wua, "w_up_rwkv": g_wur, "w_out": g_wout, "rwkv_w_up": g_wup, "rwkv_a_up": g_aup}
    blocks = [_per_owner(full[nm], by_cols).astype(BF16) for nm, _, _, by_cols, _ in SHARDED]
    me = 4 * lax.axis_index("x") + 2 * lax.axis_index("y") + lax.axis_index("c")
    own = [lax.dynamic_index_in_dim(b, me, 0, keepdims=False) for b in blocks]
    send_sems, recv_sems, blocks_thru, lands_thru, token = _send_start(blocks, False, "grads_start")
    dh = _mm_nt(dsegs[0][0], w_seg["qkv"], token, "dh_qkv")
    grad_x, g_pre = _dh_rest_prenorm_bwd([t for t, *_ in dsegs[1:]], [w_seg[nm] for nm in ("za", "pr", "zr", "gm")], dh, x2, rs,
                                         sm["pre_norm_gain"], dxo)
    landed = _send_wait(send_sems, recv_sems, blocks_thru, lands_thru, False, g_pre, "grads_wait")

    small = {"pre_norm_gain": g_pre, "rel_bias": g_bias, "rwkv_shift_mix": g_mix, "rwkv_w0": g_w0, "rwkv_a0": g_a0, "rwkv_k_k": g_kk,
             "rwkv_k_a": g_ka, "rwkv_r_k": g_rk, "rwkv_ln_w": g_lnw, "rwkv_ln_b": g_lnb, "post_norm_gain": g_post}
    return loss[0, 0], grad_x.reshape(bsz, s, d), (landed, own), small


def kernel(x, pre_norm_gain, w_in, rel_bias, rwkv_shift_mix, rwkv_w0, rwkv_w_up, rwkv_a0, rwkv_a_up, rwkv_k_k, rwkv_k_a, rwkv_r_k, rwkv_ln_w, rwkv_ln_b, w_up_attn, w_up_rwkv, w_out, post_norm_gain, loss_target, m_pre_norm_gain, m_w_in, m_rel_bias, m_rwkv_shift_mix, m_rwkv_w0, m_rwkv_w_up, m_rwkv_a0, m_rwkv_a_up, m_rwkv_k_k, m_rwkv_k_a, m_rwkv_r_k, m_rwkv_ln_w, m_rwkv_ln_b, m_w_up_attn, m_w_up_rwkv, m_w_out, m_post_norm_gain, v_pre_norm_gain, v_w_in, v_rel_bias, v_rwkv_shift_mix, v_rwkv_w0, v_rwkv_w_up, v_rwkv_a0, v_rwkv_a_up, v_rwkv_k_k, v_rwkv_k_a, v_rwkv_r_k, v_rwkv_ln_w, v_rwkv_ln_b, v_w_up_attn, v_w_up_rwkv, v_w_out, v_post_norm_gain):
    names = [n for n, *_ in SHARDED] + [n for n, _ in SMALL]
    loc = dict(locals())
    w = {n: loc[n] for n in names}
    m = {n: loc["m_" + n] for n in names}
    v = {n: loc["v_" + n] for n in names}
    shapes = {n: w[n].shape for n in names}
    order = ["pre_norm_gain", "w_in", "rel_bias", "rwkv_shift_mix", "rwkv_w0", "rwkv_w_up", "rwkv_a0", "rwkv_a_up", "rwkv_k_k", "rwkv_k_a",
             "rwkv_r_k", "rwkv_ln_w", "rwkv_ln_b", "w_up_attn", "w_up_rwkv", "w_out", "post_norm_gain"]
    shard2d = lambda t, n, r, c: t[n].reshape(r, c)

    shards = [shard2d(w, n, r, c).astype(BF16) for n, r, c, _, _ in SHARDED]
    send_sems, recv_sems, srcs_thru, lands_thru, token = _send_start(shards[1:], True, "weights_start")
    gathered = list(_gather(shards[:1], token, "gather_weights"))
    landed = _send_wait(send_sems, recv_sems, srcs_thru, lands_thru, True, gathered[0], "weights_wait")
    me = 4 * lax.axis_index("x") + 2 * lax.axis_index("y") + lax.axis_index("c")
    gathered += [lax.dynamic_update_index_in_dim(g, sh, me, 0) for g, sh in zip(landed, shards[1:])]
    wts = {n: _whole(g, by_cols) for (n, _, _, by_cols, _), g in zip(SHARDED, gathered)}

    loss, grad_x, (landed, own), small = _local_step(x, loss_target, w, wts)
    (small_parts,) = _exchange([_pack_small(small, loss)], [True], "exchange_small")

    outs = [{}, {}, {}, {}]
    for (n, r, c, _, tr), p, o_ in zip(SHARDED, landed, own):
        res = _adamw(p, shard2d(w, n, r, c), shard2d(m, n, r, c), shard2d(v, n, r, c), tr, "adamw_" + n, own=o_)
        for o, t in zip(outs, res):
            o[n] = t.reshape(shapes[n])
    res = _adamw(small_parts, _pack_small(w), _pack_small(m), _pack_small(v), SMALL_ROWS, "adamw_small")
    for o, t in zip(outs, res):
        o.update(_unpack_small(t, shapes)[0])
    loss = _unpack_small(res[0], shapes)[1]
    return (loss, grad_x, *[o[n] for o in outs for n in order])
```

```python
import functools
import math

import numpy as np
import jax
import jax.numpy as jnp
from jax import lax
from jax.experimental import pallas as pl
from jax.experimental.pallas import tpu as pltpu

F32, BF16 = jnp.float32, jnp.bfloat16
SDS = jax.ShapeDtypeStruct
MESH = pl.DeviceIdType.MESH

N_DEV = 8
D_MODEL = 1024
HEAD = 64
N_HEAD = 8
WIDTH = N_HEAD * HEAD
DILATIONS = (1, 4, 16)
QB = 128
N_BUCKET = 32
MAX_DIST = 2048
LORA = 64
QKV_COLS = 9 * WIDTH
PR_COLS = 3 * WIDTH + 2 * LORA
IN_COLS = QKV_COLS + WIDTH + PR_COLS + WIDTH + 2 * D_MODEL
OFF_ZA, OFF_PR, OFF_ZR, OFF_GM = QKV_COLS, QKV_COLS + WIDTH, QKV_COLS + WIDTH + PR_COLS, QKV_COLS + 2 * WIDTH + PR_COLS
RMS_EPS = 1e-6
GN_EPS = 64e-5
SCALE = 1.0 / math.sqrt(HEAD)
CHUNK = 64
CHUNK_GROUP = 16
BWD_GROUP = 16
EARLY = 8
NEG = -1e30
LANE = 128

ADAM_LR, ADAM_B1, ADAM_B2, ADAM_EPS, ADAM_WD, ADAM_STEP = 0.001, 0.9, 0.999, 1e-08, 0.01, 10

VMEM_LIMIT = 56 * 1024 * 1024

SMALL = (("pre_norm_gain", 1024), ("rel_bias", 768), ("rwkv_shift_mix", 1664), ("rwkv_w0", 512), ("rwkv_a0", 512),
         ("rwkv_k_k", 512), ("rwkv_k_a", 512), ("rwkv_r_k", 512), ("rwkv_ln_w", 512), ("rwkv_ln_b", 512),
         ("post_norm_gain", 1024))
SMALL_ROWS = 64


def _params(sem=None):
    return pltpu.CompilerParams(dimension_semantics=sem, vmem_limit_bytes=VMEM_LIMIT)


def _dot(a, b):
    return jnp.dot(a, b, preferred_element_type=F32)


def _dot_nt(a, b):
    return lax.dot_general(a, b, (((1,), (1,)), ((), ())), preferred_element_type=F32)


def _dot_tn(a, b):
    return lax.dot_general(a, b, (((0,), (0,)), ((), ())), preferred_element_type=F32)


@jax.custom_vjp
def _bdot(a, b):
    return _dot(a.astype(BF16), b.astype(BF16))


def _bdot_fwd(a, b):
    return _bdot(a, b), (a, b)


def _bdot_bwd(res, g):
    a, b = res
    gb = g.astype(BF16)
    return _dot_nt(gb, b.astype(BF16)), _dot_tn(a.astype(BF16), gb)


_bdot.defvjp(_bdot_fwd, _bdot_bwd)


def _silu(z):
    return z * jax.nn.sigmoid(z)


def _dsilu(z):
    s = jax.nn.sigmoid(z)
    return s * (1.0 + z * (1.0 - s))


def _softplus(x):
    return jnp.maximum(x, 0.0) + jnp.log(1.0 + jnp.exp(-jnp.abs(x)))


def _bucket_tables():
    qi = np.arange(QB)[:, None] + QB
    ki = np.arange(2 * QB)[None, :]
    rel = np.maximum(qi - ki, 0)
    out = []
    for d in DILATIONS:
        dist = rel * d
        max_exact = N_BUCKET // 2
        ratio = np.log(np.maximum(dist, 1).astype(np.float32) / max_exact) / np.float32(math.log(MAX_DIST / max_exact))
        large = max_exact + (ratio * (N_BUCKET - max_exact)).astype(np.int32)
        large = np.minimum(large, N_BUCKET - 1)
        out.append(np.where(dist < max_exact, dist, large).astype(np.int32))
    return np.stack(out)


def _prenorm(x2, g):
    n, d = x2.shape
    tm = 1024

    def body(x_ref, g_ref, h_ref, rs_ref):
        x = x_ref[...]
        rs = lax.rsqrt(jnp.mean(x * x, axis=-1, keepdims=True) + RMS_EPS)
        h_ref[...] = (x * rs * g_ref[...]).astype(BF16)
        rs_ref[...] = rs

    return pl.pallas_call(
        body, name="prenorm", grid=(n // tm,),
        in_specs=[pl.BlockSpec((tm, d), lambda i: (i, 0)), pl.BlockSpec((1, d), lambda i: (0, 0))],
        out_specs=[pl.BlockSpec((tm, d), lambda i: (i, 0)), pl.BlockSpec((tm, 1), lambda i: (i, 0))],
        out_shape=[SDS((n, d), BF16), SDS((n, 1), F32)], compiler_params=_params(("parallel",)))(x2, g)


def _mm(a, b, tn, name):
    m, k = a.shape
    n = b.shape[1]
    tm = 1024

    def body(a_ref, b_ref, o_ref):
        o_ref[...] = _dot(a_ref[...], b_ref[...])

    return pl.pallas_call(
        body, name=name, grid=(n // tn, m // tm),
        in_specs=[pl.BlockSpec((tm, k), lambda j, i: (i, 0)), pl.BlockSpec((k, tn), lambda j, i: (0, j))],
        out_specs=pl.BlockSpec((tm, tn), lambda j, i: (i, j)),
        out_shape=SDS((m, n), F32), compiler_params=_params(("parallel", "parallel")))(a, b)


def _mm_nt(a, b, after, name):
    m, seg = a.shape[1], a.shape[2]
    d, k = b.shape
    tm = 1024
    per = 3
    tk = per * seg

    def body(a_ref, b_ref, after_ref, o_ref):
        r = sum(_dot_nt(a_ref[j].astype(BF16), b_ref[:, seg * j:seg * (j + 1)]) for j in range(per))

        @pl.when(pl.program_id(1) == 0)
        def _():
            o_ref[...] = r

        @pl.when(pl.program_id(1) != 0)
        def _():
            o_ref[...] += r

    in_specs = [pl.BlockSpec((per, tm, seg), lambda i, j: (j, i, 0)), pl.BlockSpec((d, tk), lambda i, j: (0, j)),
                pl.BlockSpec(after.shape, lambda i, j: (0, 0))]
    return pl.pallas_call(
        body, name=name, grid=(m // tm, k // tk), in_specs=in_specs, out_specs=pl.BlockSpec((tm, d), lambda i, j: (i, 0)),
        out_shape=SDS((m, d), F32), compiler_params=_params(("parallel", "arbitrary")))(a, b, after)


def _dh_rest_prenorm_bwd(a_list, b_list, acc, x2, rs, g1, dxo):
    m, d = acc.shape
    tm = 512
    n = len(a_list)

    def body(*refs):
        x_ref, rs_ref, g_ref, dxo_ref, gx_ref, dg_ref = refs[2 * n + 1:]
        dh = refs[2 * n][...]
        for a_ref, b_ref in zip(refs[:n], refs[n:2 * n]):
            dh = dh + _dot_nt(a_ref[...].astype(BF16), b_ref[...])
        x, r = x_ref[...], rs_ref[...]
        gd = dh * g_ref[...]
        gx_ref[...] = dxo_ref[...] + r * (gd - x * (r * r) * jnp.mean(gd * x, axis=-1, keepdims=True))
        dg = jnp.sum(dh * x * r, axis=0, keepdims=True)

        @pl.when(pl.program_id(0) == 0)
        def _():
            dg_ref[...] = dg

        @pl.when(pl.program_id(0) != 0)
        def _():
            dg_ref[...] += dg

    t = pl.BlockSpec((tm, d), lambda i: (i, 0))
    in_specs = [pl.BlockSpec((tm, a.shape[1]), lambda i: (i, 0)) for a in a_list]
    in_specs += [pl.BlockSpec(b.shape, lambda i: (0, 0)) for b in b_list]
    in_specs += [t, t, pl.BlockSpec((tm, 1), lambda i: (i, 0)), pl.BlockSpec((1, d), lambda i: (0, 0)), t]
    return pl.pallas_call(
        body, name="dh_rest_prenorm_bwd", grid=(m // tm,), in_specs=in_specs, out_specs=[t, pl.BlockSpec((1, d), lambda i: (0, 0))],
        out_shape=[SDS((m, d), F32), SDS((1, d), F32)], compiler_params=_params(("arbitrary",)))(*a_list, *b_list, acc, x2, rs, g1, dxo)


def _mm_tn(a, b, tn, name):
    split = b.ndim == 3
    m, k1 = a.shape
    per = 3 if split else 1
    seg = b.shape[2] if split else tn
    tn = per * seg
    n2 = b.shape[0] * seg if split else b.shape[1]
    tm = 1024

    def body(a_ref, b_ref, o_ref):
        first = pl.program_id(1) == 0
        for j in range(per):
            r = _dot_tn(a_ref[...], (b_ref[j] if split else b_ref[...]).astype(BF16))
            cols = slice(seg * j, seg * (j + 1))

            @pl.when(first)
            def _(r=r, cols=cols):
                o_ref[:, cols] = r

            @pl.when(jnp.logical_not(first))
            def _(r=r, cols=cols):
                o_ref[:, cols] += r

    b_spec = pl.BlockSpec((per, tm, seg), lambda j, i: (j, i, 0)) if split else pl.BlockSpec((tm, tn), lambda j, i: (i, j))
    return pl.pallas_call(
        body, name=name, grid=(n2 // tn, m // tm),
        in_specs=[pl.BlockSpec((tm, k1), lambda j, i: (i, 0)), b_spec],
        out_specs=pl.BlockSpec((k1, tn), lambda j, i: (0, j)),
        out_shape=SDS((k1, n2), F32), compiler_params=_params(("parallel", "arbitrary")))(a, b)


def _ds(start, d):
    return pl.ds(start, QB) if d == 1 else pl.ds(start, QB, stride=d)


def _fill_bias(tab_ref, bidx_ref, bias_sc, hp):
    for g in range(3):
        bi = bidx_ref[g]
        for h in range(2):
            acc = jnp.zeros((QB, 2 * QB), F32)
            for j in range(N_BUCKET):
                acc = jnp.where(bi == j, tab_ref[j, g * N_HEAD + hp * 2 + h], acc)
            bias_sc[g * 2 + h] = acc


def _block_starts(it, d, nb):
    rho = it // nb
    n = it % nb
    st = rho + d * QB * n
    stp = rho + d * QB * jnp.maximum(n - 1, 0)
    if d == 1:
        st, stp = pl.multiple_of(QB * it, QB), pl.multiple_of(QB * jnp.maximum(it - 1, 0), QB)
    return st, stp, n > 0


ATTN_BLOCKS = 4


def _bdot3(a, b, dims):
    return lax.dot_general(a, b, (dims, ((0,), (0,))), preferred_element_type=F32)


def _attn_operands(q_ref, k_ref, v_ref, bias_sc, g, d, nb, it0):
    two = nb > 1
    nk = 2 * QB if two else QB
    ii = lax.broadcasted_iota(jnp.int32, (QB, nk), 0)
    cc = lax.broadcasted_iota(jnp.int32, (QB, nk), 1)
    qs, ks, vs, pens, starts = [], [], [], [], []
    for u in range(ATTN_BLOCKS):
        st, stp, hasprev = _block_starts(it0 + u, d, nb)
        qf = q_ref[0, _ds(st, d), :]
        if two:
            kf = jnp.concatenate([k_ref[0, _ds(stp, d), :], k_ref[0, _ds(st, d), :]], axis=0).astype(BF16)
            vf = jnp.concatenate([v_ref[0, _ds(stp, d), :], v_ref[0, _ds(st, d), :]], axis=0).astype(BF16)
            own = jnp.logical_and(cc >= QB, ii >= cc - QB)
            prev = jnp.logical_and(jnp.logical_and(cc < QB, cc >= ii), hasprev)
            pen = jnp.where(jnp.logical_or(own, prev), 0.0, NEG)
        else:
            kf, vf = k_ref[0, _ds(st, d), :].astype(BF16), v_ref[0, _ds(st, d), :].astype(BF16)
            pen = jnp.where(ii >= cc, 0.0, NEG)
        for h in range(2):
            qs.append(_one_head(qf, h).astype(BF16))
            ks.append(kf)
            vs.append(vf)
            pens.append(pen + (bias_sc[g * 2 + h] if two else bias_sc[g * 2 + h, :, QB:2 * QB]))
        starts.append((st, stp))
    return _stack(qs), _stack(ks), _stack(vs), _stack(pens), starts


def _one_head(x, h):
    lane = lax.broadcasted_iota(jnp.int32, x.shape, 1)
    return jnp.where(lane >= HEAD if h == 1 else lane < HEAD, x, 0.0)


def _pick_heads(x, u):
    lane = lax.broadcasted_iota(jnp.int32, x.shape[1:], 1)
    return jnp.where(lane < HEAD, x[2 * u], x[2 * u + 1])


def _add_heads(x, u):
    return x[2 * u] + x[2 * u + 1]


def _attn_fwd(qkv3, rel_bias, bidx):
    bsz, s, _ = qkv3.shape
    rt = 256

    def body(tab_ref, bidx_ref, *refs):
        q_refs, k_refs, v_refs = refs[0:3], refs[3:6], refs[6:9]
        o_ref, lse_ref = refs[9:11]
        bias_sc, num_sc, den_sc, m_sc = refs[11:]
        pl.when(pl.program_id(1) == 0)(lambda: _fill_bias(tab_ref, bidx_ref, bias_sc, pl.program_id(0)))
        for g, d in enumerate(DILATIONS):
            nb = s // (QB * d)

            def blk(it, c, g=g, d=d, nb=nb):
                q, k, v, bias, starts = _attn_operands(q_refs[g], k_refs[g], v_refs[g], bias_sc, g, d, nb, it * ATTN_BLOCKS)
                sc = _bdot3(q, k, ((2,), (2,))) * SCALE + bias
                m = jnp.max(sc, axis=-1, keepdims=True)
                p = jnp.exp(sc - m)
                den = jnp.sum(p, axis=-1, keepdims=True)
                num = _bdot3(p.astype(BF16), v, ((2,), (1,)))
                den, m = jnp.broadcast_to(den, num.shape), jnp.broadcast_to(m, num.shape)
                for u, (st, _) in enumerate(starts):
                    num_sc[g, _ds(st, d), :] = _pick_heads(num, u)
                    den_sc[g, _ds(st, d), :] = _pick_heads(den, u)
                    m_sc[g, _ds(st, d), :] = _pick_heads(m, u)
                return c

            lax.fori_loop(0, s // QB // ATTN_BLOCKS, blk, 0)

        def merge(i, c):
            rows = pl.ds(pl.multiple_of(i * rt, rt), rt)
            m0, m1, m2 = m_sc[0, rows, :], m_sc[1, rows, :], m_sc[2, rows, :]
            mall = jnp.maximum(jnp.maximum(m0, m1), m2)
            w0, w1, w2 = jnp.exp(m0 - mall), jnp.exp(m1 - mall), jnp.exp(m2 - mall)
            num = w0 * num_sc[0, rows, :] + w1 * num_sc[1, rows, :] + w2 * num_sc[2, rows, :]
            den = w0 * den_sc[0, rows, :] + w1 * den_sc[1, rows, :] + w2 * den_sc[2, rows, :]
            o_ref[0, rows, :] = num / den
            lse_ref[0, rows, :] = mall + jnp.log(den)
            return c

        lax.fori_loop(0, s // rt, merge, 0)

    col = lambda w, g: (lambda hp, b: (b, 0, (w * 3 + g) * 4 + hp))
    in_specs = [pl.BlockSpec(memory_space=pltpu.SMEM), pl.BlockSpec((3, QB, 2 * QB), lambda hp, b: (0, 0, 0))]
    in_specs += [pl.BlockSpec((1, s, LANE), col(w, g)) for w in range(3) for g in range(3)]
    out_spec = pl.BlockSpec((1, s, LANE), lambda hp, b: (b, 0, hp))
    return pl.pallas_call(
        body, name="attn_fwd", grid=(4, bsz), in_specs=in_specs, out_specs=[out_spec, out_spec],
        out_shape=[SDS((bsz, s, WIDTH), F32), SDS((bsz, s, WIDTH), F32)],
        scratch_shapes=[pltpu.VMEM((6, QB, 2 * QB), F32), pltpu.VMEM((3, s, LANE), F32), pltpu.VMEM((3, s, LANE), F32),
                        pltpu.VMEM((3, s, LANE), F32)],
        compiler_params=_params(("arbitrary", "arbitrary")))(rel_bias, bidx, *([qkv3] * 9))


def _attn_bwd(qkv3, o3, lse3, do3, rel_bias, bidx):
    bsz, s, _ = qkv3.shape
    rt = 256

    def body(tab_ref, bidx_ref, *refs):
        q_refs, k_refs, v_refs = refs[0:3], refs[3:6], refs[6:9]
        o_ref, lse_ref, do_ref, dqkv_ref, db_ref, bias_sc, delta_sc, acc_sc = refs[9:]
        dq_refs, dk_refs, dv_refs = ([acc_sc.at[w * 3 + g] for g in range(3)] for w in range(3))

        @pl.when(pl.program_id(1) == 0)
        def _():
            _fill_bias(tab_ref, bidx_ref, bias_sc, pl.program_id(0))
            db_ref[...] = jnp.zeros_like(db_ref)

        def prep(i, c):
            rows = pl.ds(pl.multiple_of(i * rt, rt), rt)
            prod = do_ref[0, rows, :] * o_ref[0, rows, :]
            d0 = jnp.sum(prod[:, :HEAD], axis=-1, keepdims=True)
            d1 = jnp.sum(prod[:, HEAD:], axis=-1, keepdims=True)
            delta_sc[rows, :] = jnp.concatenate([jnp.broadcast_to(d0, (rt, HEAD)), jnp.broadcast_to(d1, (rt, HEAD))], axis=1)
            z = jnp.zeros((rt, LANE), F32)
            for g in range(3):
                dk_refs[g][0, rows, :] = z
                dv_refs[g][0, rows, :] = z
            return c

        lax.fori_loop(0, s // rt, prep, 0)
        for g, d in enumerate(DILATIONS):
            nb = s // (QB * d)

            def blk(it, c, g=g, d=d, nb=nb):
                q, k, v, bias, starts = _attn_operands(q_refs[g], k_refs[g], v_refs[g], bias_sc, g, d, nb, it * ATTN_BLOCKS)
                dos, lses, deltas = [], [], []
                for st, _ in starts:
                    dof, lsef, delf = do_ref[0, _ds(st, d), :], lse_ref[0, _ds(st, d), :], delta_sc[_ds(st, d), :]
                    for h in range(2):
                        dos.append(_one_head(dof, h).astype(BF16))
                        lses.append(lsef[:, HEAD * h:HEAD * h + 1])
                        deltas.append(delf[:, HEAD * h:HEAD * h + 1])
                do, lse, delta = _stack(dos), _stack(lses), _stack(deltas)
                p = jnp.exp(_bdot3(q, k, ((2,), (2,))) * SCALE + bias - lse)
                dv = _bdot3(p.astype(BF16), do, ((1,), (1,)))
                ds = p * (_bdot3(do, v, ((2,), (2,))) - delta)
                dsb = ds.astype(BF16)
                dq = _bdot3(dsb, k, ((2,), (1,))) * SCALE
                dk = _bdot3(dsb, q, ((1,), (1,))) * SCALE
                two = nb > 1
                for h in range(2):
                    dsum = sum(ds[2 * u + h] for u in range(ATTN_BLOCKS))
                    if two:
                        db_ref[0, g * 2 + h] += dsum
                    else:
                        db_ref[0, g * 2 + h, :, QB:2 * QB] += dsum
                for u, (st, stp) in enumerate(starts):
                    dq_refs[g][0, _ds(st, d), :] = _pick_heads(dq, u)
                    if two:
                        dk_refs[g][0, _ds(stp, d), :] += _add_heads(dk[:, :QB], u)
                        dv_refs[g][0, _ds(stp, d), :] += _add_heads(dv[:, :QB], u)
                    dk_refs[g][0, _ds(st, d), :] += _add_heads(dk[:, QB:] if two else dk, u)
                    dv_refs[g][0, _ds(st, d), :] += _add_heads(dv[:, QB:] if two else dv, u)
                return c

            lax.fori_loop(0, s // QB // ATTN_BLOCKS, blk, 0)

        def flush(i, c):
            rows = pl.ds(pl.multiple_of(i * rt, rt), rt)
            for j in range(9):
                dqkv_ref[j, 0, rows, :] = acc_sc[j, 0, rows, :].astype(BF16)
            return c

        lax.fori_loop(0, s // rt, flush, 0)

    col = lambda w, g: (lambda hp, b: (b, 0, (w * 3 + g) * 4 + hp))
    blk_spec = pl.BlockSpec((1, s, LANE), lambda hp, b: (b, 0, hp))
    in_specs = [pl.BlockSpec(memory_space=pltpu.SMEM), pl.BlockSpec((3, QB, 2 * QB), lambda hp, b: (0, 0, 0))]
    in_specs += [pl.BlockSpec((1, s, LANE), col(w, g)) for w in range(3) for g in range(3)]
    in_specs += [blk_spec] * 3
    out_specs = [pl.BlockSpec((9, 1, s, LANE), lambda hp, b: (0, b, 0, hp)), pl.BlockSpec((1, 6, QB, 2 * QB), lambda hp, b: (hp, 0, 0, 0))]
    out_shape = [SDS((9, bsz, s, WIDTH), BF16), SDS((4, 6, QB, 2 * QB), F32)]
    return pl.pallas_call(
        body, name="attn_bwd", grid=(4, bsz), in_specs=in_specs, out_specs=out_specs, out_shape=out_shape,
        scratch_shapes=[pltpu.VMEM((6, QB, 2 * QB), F32), pltpu.VMEM((s, LANE), F32), pltpu.VMEM((9, 1, s, LANE), F32)],
        compiler_params=_params(("parallel", "arbitrary")))(rel_bias, bidx, *([qkv3] * 9), o3, lse3, do3)


def _bias_grad(dbias, bidx):
    def body(db_ref, bidx_ref, o_ref):
        lane = lax.broadcasted_iota(jnp.int32, (1, LANE), 1)
        for g in range(3):
            bi = bidx_ref[g]
            for hp in range(4):
                for h in range(2):
                    mat = db_ref[hp, g * 2 + h]
                    row = jnp.zeros((1, LANE), F32)
                    for j in range(N_BUCKET):
                        part = jnp.sum(jnp.where(bi == j, mat, 0.0), axis=0, keepdims=True)
                        row = jnp.where(lane == j, jnp.sum(part, axis=1, keepdims=True), row)
                    hd = g * N_HEAD + hp * 2 + h
                    o_ref[hd:hd + 1, :] = row

    return pl.pallas_call(body, name="bias_grad", out_shape=SDS((3 * N_HEAD, LANE), F32), compiler_params=_params())(dbias, bidx)


def _pre_fn(r, k0, v, wl, al, w0, wup, a0, aup, kk_, ka_):
    u = w0 + _bdot(jnp.tanh(wl), wup)
    lw = -jnp.exp(-_softplus(-u) - 0.5)
    a = jax.nn.sigmoid(a0 + _bdot(al, aup))
    kkraw = k0 * kk_
    k = k0 * (1.0 + (a - 1.0) * ka_)
    return r, lw, k, v, kkraw, a


PRE_SPLIT = (0, WIDTH, 2 * WIDTH, 3 * WIDTH, 3 * WIDTH + LORA, 3 * WIDTH + 2 * LORA)


def _pre_pieces(prs):
    return [prs[:, a:b] for a, b in zip(PRE_SPLIT[:-1], PRE_SPLIT[1:])]


PRE_TT = 512


def _shifted(pr_ref, edge_ref, first, back):
    pr = pr_ref[0]
    tt = pr.shape[0]
    row = lax.broadcasted_iota(jnp.int32, (tt, 1), 0)
    if back:
        edge = jnp.where(first, 0.0, edge_ref[0, 7:8, :])
        return jnp.where(row == 0, edge, pltpu.roll(pr, 1, axis=0))
    edge = jnp.where(first, 0.0, edge_ref[0, 0:1, :])
    return jnp.where(row == tt - 1, edge, pltpu.roll(pr, tt - 1, axis=0))


def _rwkv_pre(pr3, mix, w0, wup, a0, aup, kk_, ka_):
    bsz, s, _ = pr3.shape
    tt = PRE_TT

    def body(pr_ref, edge_ref, mix_ref, w0_ref, wup_ref, a0_ref, aup_ref, kk_ref, ka_ref, *outs):
        pr = pr_ref[0]
        prev = _shifted(pr_ref, edge_ref, pl.program_id(1) == 0, True)
        prs = pr + (prev - pr) * mix_ref[...]
        vals = _pre_fn(*_pre_pieces(prs), w0_ref[...], wup_ref[...].astype(F32), a0_ref[...], aup_ref[...].astype(F32), kk_ref[...],
                       ka_ref[...])
        for o, val in zip(outs, vals):
            o[0] = val

    vec = lambda n: pl.BlockSpec((1, n), lambda b, i: (0, 0))
    mat = pl.BlockSpec((LORA, WIDTH), lambda b, i: (0, 0))
    in_specs = [pl.BlockSpec((1, tt, PR_COLS), lambda b, i: (b, i, 0)),
                pl.BlockSpec((1, 8, PR_COLS), lambda b, i: (b, jnp.maximum(i * (tt // 8) - 1, 0), 0)),
                vec(PR_COLS), vec(WIDTH), mat, vec(WIDTH), mat, vec(WIDTH), vec(WIDTH)]
    out_spec = pl.BlockSpec((1, tt, WIDTH), lambda b, i: (b, i, 0))
    return pl.pallas_call(
        body, name="rwkv_pre", grid=(bsz, s // tt), in_specs=in_specs, out_specs=[out_spec] * 6,
        out_shape=[SDS((bsz, s, WIDTH), F32)] * 6, compiler_params=_params(("parallel", "parallel")))(
            pr3, pr3, mix, w0, wup, a0, aup, kk_, ka_)


def _rwkv_pre_bwd(pr3, cots, mix, w0, wup, a0, aup, kk_, ka_):
    bsz, s, _ = pr3.shape
    tt = PRE_TT

    def body(pr_ref, edge_ref, c0, c1, c2, c3, c4, c5, mix_ref, w0_ref, wup_ref, a0_ref, aup_ref, kk_ref, ka_ref,
             dprs_ref, dmix_ref, dw0_ref, dwup_ref, da0_ref, daup_ref, dkk_ref, dka_ref):
        pr = pr_ref[0]
        prev = _shifted(pr_ref, edge_ref, pl.program_id(1) == 0, True)
        prs = pr + (prev - pr) * mix_ref[...]
        _, vjp = jax.vjp(_pre_fn, *_pre_pieces(prs), w0_ref[...], wup_ref[...].astype(F32), a0_ref[...], aup_ref[...].astype(F32),
                         kk_ref[...], ka_ref[...])
        grads = vjp(tuple(c[0] for c in (c0, c1, c2, c3, c4, c5)))
        for piece, a, b in zip(grads[:5], PRE_SPLIT[:-1], PRE_SPLIT[1:]):
            dprs_ref[0, :, a:b] = piece
        dw0, dwup, da0, daup, dkk, dka = grads[5:]
        dprs = dprs_ref[0]
        grads = (jnp.sum(dprs * (prev - pr), axis=0, keepdims=True), dw0, dwup, da0, daup, dkk, dka)
        refs = (dmix_ref, dw0_ref, dwup_ref, da0_ref, daup_ref, dkk_ref, dka_ref)
        first = jnp.logical_and(pl.program_id(0) == 0, pl.program_id(1) == 0)

        @pl.when(first)
        def _():
            for r_, g_ in zip(refs, grads):
                r_[...] = g_

        @pl.when(jnp.logical_not(first))
        def _():
            for r_, g_ in zip(refs, grads):
                r_[...] += g_

    vec = lambda n: pl.BlockSpec((1, n), lambda b, i: (0, 0))
    mat = pl.BlockSpec((LORA, WIDTH), lambda b, i: (0, 0))
    tile = pl.BlockSpec((1, tt, WIDTH), lambda b, i: (b, i, 0))
    in_specs = [pl.BlockSpec((1, tt, PR_COLS), lambda b, i: (b, i, 0)),
                pl.BlockSpec((1, 8, PR_COLS), lambda b, i: (b, jnp.maximum(i * (tt // 8) - 1, 0), 0))]
    in_specs += [tile] * 6 + [vec(PR_COLS), vec(WIDTH), mat, vec(WIDTH), mat, vec(WIDTH), vec(WIDTH)]
    out_specs = [pl.BlockSpec((1, tt, PR_COLS), lambda b, i: (b, i, 0)), vec(PR_COLS), vec(WIDTH), mat, vec(WIDTH), mat,
                 vec(WIDTH), vec(WIDTH)]
    out_shape = [SDS((bsz, s, PR_COLS), F32), SDS((1, PR_COLS), F32), SDS((1, WIDTH), F32), SDS((LORA, WIDTH), F32),
                 SDS((1, WIDTH), F32), SDS((LORA, WIDTH), F32), SDS((1, WIDTH), F32), SDS((1, WIDTH), F32)]
    return pl.pallas_call(
        body, name="rwkv_pre_bwd", grid=(bsz, s // tt), in_specs=in_specs, out_specs=out_specs, out_shape=out_shape,
        compiler_params=_params(("arbitrary", "arbitrary")))(pr3, pr3, *cots, mix, w0, wup, a0, aup, kk_, ka_)


def _shift_bwd(dprs3, mix):
    bsz, s, _ = dprs3.shape
    tt = PRE_TT
    nt = s // tt

    def body(d_ref, edge_ref, mix_ref, o_ref):
        nxt = _shifted(d_ref, edge_ref, pl.program_id(1) == nt - 1, False)
        m = mix_ref[...]
        o_ref[0] = (d_ref[0] * (1.0 - m) + nxt * m).astype(BF16)

    in_specs = [pl.BlockSpec((1, tt, PR_COLS), lambda b, i: (b, i, 0)),
                pl.BlockSpec((1, 8, PR_COLS), lambda b, i: (b, jnp.minimum((i + 1) * (tt // 8), s // 8 - 1), 0)),
                pl.BlockSpec((1, PR_COLS), lambda b, i: (0, 0))]
    return pl.pallas_call(
        body, name="shift_bwd", grid=(bsz, nt), in_specs=in_specs, out_specs=pl.BlockSpec((1, tt, PR_COLS), lambda b, i: (b, i, 0)),
        out_shape=SDS((bsz, s, PR_COLS), BF16), compiler_params=_params(("parallel", "parallel")))(dprs3, dprs3, mix)


_NN, _NT, _TN = ((2,), (1,)), ((2,), (2,)), ((1,), (1,))


def _dot3_bf16(a, b, dims):
    return lax.dot_general(a.astype(BF16), b.astype(BF16), (dims, ((0,), (0,))), preferred_element_type=F32)


class _Dots:
    def __init__(self, fwd):
        def make(dims, da_rule, db_rule):
            @jax.custom_vjp
            def f(a, b):
                return fwd(a, b, dims)

            f.defvjp(lambda a, b: (f(a, b), (a, b)), lambda res, g: (da_rule(*res, g), db_rule(*res, g)))
            return f

        one = _dot3_bf16
        self.mm = make(_NN, lambda a, b, g: one(g, b, _NT), lambda a, b, g: one(a, g, _TN))
        self.mm_nt = make(_NT, lambda a, b, g: one(g, b, _NN), lambda a, b, g: one(g, a, _TN))
        self.mm_tn = make(_TN, lambda a, b, g: one(b, g, _NT), lambda a, b, g: one(a, g, _NN))

        def powers(aab):
            ps = [aab]
            while 2 ** len(ps) < aab.shape[1]:
                ps.append(fwd(ps[-1], ps[-1], _NN))
            return ps

        def apply(ps, z, dims):
            for p in ps:
                z = z + fwd(p, z, dims)
            return z

        @jax.custom_vjp
        def solve(aab, z):
            return apply(powers(aab), z, _NN)

        def solve_fwd(aab, z):
            ps = powers(aab)
            x = apply(ps, z, _NN)
            return x, (ps, x)

        def solve_bwd(res, g):
            ps, x = res
            dz = apply(ps, g, _TN)
            return fwd(dz, x, _NT), dz

        solve.defvjp(solve_fwd, solve_bwd)
        self.solve = solve


_ONE_PASS = _Dots(_dot3_bf16)
_bmm, _bmm_tn = _ONE_PASS.mm, _ONE_PASS.mm_tn


def _chunk_fn(s0t, r, lw, k, v, kkraw, a, rk, lnw, lnb, first=False, d=_ONE_PASS):
    c = r.shape[1]
    at, rt, btc, ktc, gc, aab, arb, xv, arkv, ain, bin_ = _chunk_core(r, lw, k, v, kkraw, a, d)
    rs = d.mm(jnp.concatenate([at, rt], axis=1), s0t)
    u = d.solve(aab, rs[:, :c] + xv)
    y = rs[:, c:] + d.mm(arb, u) + arkv
    if first:
        y = _with_early_rows(y, r, lw, k, v, ain, bin_)
    gcol = jnp.sum(_diag(gc), axis=2, keepdims=True)
    sct = gcol * s0t + d.mm_tn(jnp.concatenate([btc, ktc], axis=1), jnp.concatenate([u, v], axis=1))
    return _post(y, r, k, v, rk, lnw, lnb), sct


def _diag(gc):
    return jnp.where(_masks(HEAD)[2], gc, 0.0)


def _with_early_rows(y, r, lw, k, v, ain, bin_):
    early = _early_rows(r[:2], lw[:2], k[:2], v[:2], ain[:2], bin_[:2])
    return jnp.concatenate([jnp.concatenate([early, y[:2, EARLY:]], axis=1), y[2:]], axis=0)


def _early_rows(r, lw, k, v, ain, bin_):
    cols = lambda x: _stack([jnp.transpose(x[h]) for h in range(2)])
    wc, bc, kc = cols(jnp.exp(lw)), cols(bin_), cols(k)
    st = jnp.zeros((2, HEAD, HEAD), F32)
    rows = []
    for t in range(EARLY):
        sa = _ONE_PASS.mm(ain[:, t:t + 1], st)
        st = st * wc[:, :, t:t + 1] + bc[:, :, t:t + 1] * sa + kc[:, :, t:t + 1] * v[:, t:t + 1]
        rows.append(_ONE_PASS.mm(r[:, t:t + 1], st))
    return jnp.concatenate(rows, axis=1)


def _chunk_rows(c):
    return pl.ds(c * CHUNK, CHUNK) if isinstance(c, int) else pl.ds(pl.multiple_of(c * CHUNK, CHUNK), CHUNK)


def _stack(xs):
    return jnp.concatenate([x[None] for x in xs], axis=0)


def _pairs(ref, chunks):
    tiles = [ref[0, _chunk_rows(c), :] for c in chunks]
    return _stack([t[:, HEAD * h:HEAD * h + HEAD] for t in tiles for h in range(2)])


def _unpair(vals, j):
    return jnp.concatenate([vals[2 * j], vals[2 * j + 1]], axis=1)


def _masks(c):
    ii = lax.broadcasted_iota(jnp.int32, (c, c), 0)
    jj = lax.broadcasted_iota(jnp.int32, (c, c), 1)
    return ii > jj, ii >= jj, ii == jj


@jax.custom_vjp
def _running_sum(lw):
    return _tri_dot(lw, _NN)


def _tri_dot(x, dims):
    g_, c, _ = x.shape
    tri = jnp.broadcast_to(_masks(c)[1].astype(BF16), (g_, c, c))
    head = x.astype(BF16)
    rest = (x - head.astype(F32)).astype(BF16)
    return lax.dot_general(tri, head, (dims, ((0,), (0,))), preferred_element_type=F32) + \
        lax.dot_general(tri, rest, (dims, ((0,), (0,))), preferred_element_type=F32)


_running_sum.defvjp(lambda lw: (_running_sum(lw), None), lambda _, ct: (_tri_dot(ct, _TN),))


def _chunk_core(r, lw, k, v, kkraw, a, d=_ONE_PASS):
    g_, c = r.shape[0], r.shape[1]
    nrm = jnp.sqrt(jnp.sum(kkraw * kkraw, axis=-1, keepdims=True))
    kkn = kkraw / jnp.maximum(nrm, 1e-12)
    ain, bin_ = -kkn, kkn * a
    strict, incl, _ = _masks(c)
    lg = _running_sum(lw)
    g, gp, gi = jnp.exp(lg), jnp.exp(lg - lw), jnp.exp(-lg)
    at, rt, bt, kt = ain * gp, r * g, bin_ * gi, k * gi
    aa = d.mm_nt(jnp.concatenate([at, rt], axis=1), jnp.concatenate([bt, kt], axis=1))
    aab = jnp.where(strict, aa[:, :c, :c], 0.0)
    aak = jnp.where(strict, aa[:, :c, c:], 0.0)
    arb = jnp.where(incl, aa[:, c:, :c], 0.0)
    ark = jnp.where(incl, aa[:, c:, c:], 0.0)
    akv = d.mm(jnp.concatenate([aak, ark], axis=1), v)
    gc = g[:, c - 1:c, :]
    return at, rt, bt * gc, kt * gc, gc, aab, arb, akv[:, :c], akv[:, c:], ain, bin_


def _post(y, r, k, v, rk, lnw, lnb):
    mu = jnp.mean(y, axis=-1, keepdims=True)
    var = jnp.mean(jnp.square(y - mu), axis=-1, keepdims=True)
    yn = (y - mu) * lax.rsqrt(var + GN_EPS) * lnw + lnb
    return yn + jnp.sum(r * k * rk, axis=-1, keepdims=True) * v


def _chunk_consts(r, lw, k, v, kkraw, a, first=False):
    d = _ONE_PASS
    at, rt, btc, ktc, gc, aab, arb, xv, arkv, ain, bin_ = _chunk_core(r, lw, k, v, kkraw, a, d)
    z = d.solve(aab, jnp.concatenate([at, xv], axis=2))
    ryv = jnp.concatenate([rt, arkv], axis=2) + d.mm(arb, z)
    if first:
        ryv = jnp.concatenate([ryv[:, :, :HEAD], _with_early_rows(ryv[:, :, HEAD:], r, lw, k, v, ain, bin_)], axis=2)
    mkv = d.mm_tn(btc, z) + jnp.concatenate([_diag(gc), d.mm_tn(ktc, v)], axis=2)
    return mkv, ryv


def _rwkv_scan(ins, rk, lnw, lnb):
    bsz, s, _ = ins[0].shape
    nch = s // CHUNK

    def consts_body(r_ref, lw_ref, k_ref, v_ref, kk_ref, a_ref, mkv_ref, ry_ref, yv_ref):
        def group(i, carry):
            chunks = [i * CHUNK_GROUP + j for j in range(CHUNK_GROUP)]
            mkv, ryv = _chunk_consts(*[_pairs(ref, chunks) for ref in (r_ref, lw_ref, k_ref, v_ref, kk_ref, a_ref)],
                                     first=isinstance(i, int) and i == 0)
            for j, c in enumerate(chunks):
                for h in range(2):
                    mkv_ref[0, 0, c, h] = mkv[2 * j + h]
                ry_ref[0, _chunk_rows(c), :] = jnp.concatenate([ryv[2 * j][:, :HEAD], ryv[2 * j + 1][:, :HEAD]], axis=1)
                yv_ref[0, _chunk_rows(c), :] = jnp.concatenate([ryv[2 * j][:, HEAD:], ryv[2 * j + 1][:, HEAD:]], axis=1)
            return carry

        group(0, 0)
        lax.fori_loop(1, nch // CHUNK_GROUP, group, 0)

    tile = pl.BlockSpec((1, s, LANE), lambda b, hp: (b, 0, hp))
    vec = pl.BlockSpec((1, LANE), lambda b, hp: (0, hp))
    mkv_spec = pl.BlockSpec((1, 1, nch, 2, HEAD, LANE), lambda b, hp: (b, hp, 0, 0, 0, 0))
    st_spec = pl.BlockSpec((1, 1, nch, 2, HEAD, HEAD), lambda b, hp: (b, hp, 0, 0, 0, 0))
    mkv, ry, yv = pl.pallas_call(
        consts_body, name="rwkv_consts", grid=(bsz, 4), in_specs=[tile] * 6, out_specs=[mkv_spec, tile, tile],
        out_shape=[SDS((bsz, 4, nch, 2, HEAD, LANE), F32), SDS((bsz, s, WIDTH), F32), SDS((bsz, s, WIDTH), F32)],
        compiler_params=_params(("parallel", "parallel")))(*ins)

    states = _chunk_recurrence(mkv, None, "rwkv_states")

    def out_body(ry_ref, yv_ref, r_ref, k_ref, v_ref, st_ref, rk_ref, lnw_ref, lnb_ref, o_ref):
        y, r, k, v, rk_, lnw_, lnb_ = _scan_rows(ry_ref, yv_ref, r_ref, k_ref, v_ref, st_ref, rk_ref, lnw_ref, lnb_ref)
        o = _post(y, r, k, v, rk_, lnw_, lnb_)
        for j in range(CHUNK_GROUP):
            o_ref[0, _chunk_rows(j), :] = _unpair(o, j)

    o = pl.pallas_call(
        out_body, name="rwkv_out", grid=(bsz, 4, nch // CHUNK_GROUP), in_specs=_group_specs(5), out_specs=_group_specs(1)[0],
        out_shape=SDS((bsz, s, WIDTH), F32),
        compiler_params=_params(("parallel", "parallel", "parallel")))(ry, yv, ins[0], ins[2], ins[3], states, rk, lnw, lnb)
    return o, states, (mkv, ry, yv)


def _group_specs(n_tiles):
    tile = pl.BlockSpec((1, CHUNK_GROUP * CHUNK, LANE), lambda b, hp, t: (b, t, hp))
    if n_tiles == 1:
        return [tile]
    st = pl.BlockSpec((1, 1, CHUNK_GROUP, 2, HEAD, HEAD), lambda b, hp, t: (b, hp, t, 0, 0, 0))
    vec = pl.BlockSpec((1, LANE), lambda b, hp, t: (0, hp))
    return [tile] * n_tiles + [st] + [vec] * 3


def _scan_rows(ry_ref, yv_ref, r_ref, k_ref, v_ref, st_ref, rk_ref, lnw_ref, lnb_ref):
    chunks = list(range(CHUNK_GROUP))
    ry, yv, r, k, v = (_pairs(ref, chunks) for ref in (ry_ref, yv_ref, r_ref, k_ref, v_ref))
    st = _stack([st_ref[0, 0, c, h] for c in chunks for h in range(2)])
    vecs = [_stack([ref[:, HEAD * h:HEAD * h + HEAD] for _ in chunks for h in range(2)]) for ref in (rk_ref, lnw_ref, lnb_ref)]
    return (_bmm(ry, st) + yv, r, k, v, *vecs)


def _chunk_recurrence(mkv, q, name):
    bsz, _, nch = mkv.shape[:3]
    pairs = [(hp, h) for hp in range(4) for h in range(2)]

    def body(*refs):
        mkv_ref, out_ref, acc = refs[0], refs[-2], refs[-1]
        acc[...] = jnp.zeros_like(acc)

        def step(i, carry):
            c = i if q is None else nch - 1 - i
            cur = acc[...]
            for j, (hp, h) in enumerate(pairs):
                out_ref[0, hp, c, h] = cur[j]
            m = _stack([mkv_ref[0, hp, c, h] for hp, h in pairs])
            if q is None:
                acc[...] = _bmm(m[:, :, :HEAD], cur) + m[:, :, HEAD:]
            else:
                acc[...] = _bmm_tn(m[:, :, :HEAD], cur) + _stack([refs[1][0, hp, c, h] for hp, h in pairs])
            return carry

        lax.fori_loop(0, nch, step, 0)

    spec = lambda w: pl.BlockSpec((1, 4, nch, 2, HEAD, w), lambda b: (b, 0, 0, 0, 0, 0))
    return pl.pallas_call(
        body, name=name, grid=(bsz,), in_specs=[spec(LANE)] + ([] if q is None else [spec(HEAD)]), out_specs=spec(HEAD),
        out_shape=SDS((bsz, 4, nch, 2, HEAD, HEAD), F32), scratch_shapes=[pltpu.VMEM((8, HEAD, HEAD), F32)],
        compiler_params=_params(("parallel",)))(*([mkv] if q is None else [mkv, q]))


def _rwkv_scan_bwd(ins, states, consts, do3, rk, lnw, lnb):
    bsz, s, _ = ins[0].shape
    nch = s // CHUNK

    mkv, ry, yv = consts

    def q_body(do_ref, ry_ref, yv_ref, r_ref, k_ref, v_ref, st_ref, rk_ref, lnw_ref, lnb_ref, q_ref):
        y, r, k, v, rk_, lnw_, lnb_ = _scan_rows(ry_ref, yv_ref, r_ref, k_ref, v_ref, st_ref, rk_ref, lnw_ref, lnb_ref)
        _, vjp = jax.vjp(lambda y_: _post(y_, r, k, v, rk_, lnw_, lnb_), y)
        (dy,) = vjp(_pairs(do_ref, list(range(CHUNK_GROUP))))
        q = _bmm_tn(_pairs(ry_ref, list(range(CHUNK_GROUP))), dy)
        for j in range(CHUNK_GROUP):
            for h in range(2):
                q_ref[0, 0, j, h] = q[2 * j + h]

    specs = _group_specs(6)
    q = pl.pallas_call(
        q_body, name="rwkv_q", grid=(bsz, 4, nch // CHUNK_GROUP), in_specs=specs, out_specs=specs[6],
        out_shape=SDS((bsz, 4, nch, 2, HEAD, HEAD), F32),
        compiler_params=_params(("parallel", "parallel", "parallel")))(do3, ry, yv, ins[0], ins[2], ins[3], states, rk, lnw, lnb)

    dstates = _chunk_recurrence(mkv, q, "rwkv_dstates")

    def body(r_ref, lw_ref, k_ref, v_ref, kk_ref, a_ref, st_ref, dst_ref, do_ref, rk_ref, lnw_ref, lnb_ref,
             dr_ref, dlw_ref, dk_ref, dv_ref, dkk_ref, da_ref, drk_ref, dlnw_ref, dlnb_ref):
        chunks = list(range(BWD_GROUP))
        par_refs = (drk_ref, dlnw_ref, dlnb_ref)

        @pl.when(jnp.logical_and(pl.program_id(1) == 0, pl.program_id(2) == 0))
        def _():
            for ref in par_refs:
                ref[...] = jnp.zeros_like(ref)

        def group(first):
            per_pair = lambda ref: _stack([ref[0, 0, c, h] for c in chunks for h in range(2)])
            vecs = [_stack([ref[:, HEAD * h:HEAD * h + HEAD] for _ in chunks for h in range(2)]) for ref in (rk_ref, lnw_ref, lnb_ref)]
            _, vjp = jax.vjp(functools.partial(_chunk_fn, first=first, d=_ONE_PASS), per_pair(st_ref),
                             *[_pairs(ref, chunks) for ref in (r_ref, lw_ref, k_ref, v_ref, kk_ref, a_ref)], *vecs)
            grads = vjp((_pairs(do_ref, chunks), per_pair(dst_ref)))
            for ref, cot in zip((dr_ref, dlw_ref, dk_ref, dv_ref, dkk_ref, da_ref), grads[1:7]):
                for j, c in enumerate(chunks):
                    ref[0, _chunk_rows(c), :] = _unpair(cot, j)
            for ref, g_ in zip(par_refs, grads[7:10]):
                ref[...] += jnp.concatenate([sum(g_[2 * j + h] for j in range(BWD_GROUP)) for h in range(2)], axis=1)

        pl.when(pl.program_id(2) == 0)(functools.partial(group, True))
        pl.when(pl.program_id(2) != 0)(functools.partial(group, False))

    tt = BWD_GROUP * CHUNK
    tile = pl.BlockSpec((1, tt, LANE), lambda hp, b, t: (b, t, hp))
    vec = pl.BlockSpec((1, LANE), lambda hp, b, t: (0, hp))
    st_spec = pl.BlockSpec((1, 1, BWD_GROUP, 2, HEAD, HEAD), lambda hp, b, t: (b, hp, t, 0, 0, 0))
    outs = pl.pallas_call(
        body, name="rwkv_scan_bwd", grid=(4, bsz, s // tt), in_specs=[tile] * 6 + [st_spec, st_spec, tile] + [vec] * 3,
        out_specs=[tile] * 6 + [vec] * 3,
        out_shape=[SDS((bsz, s, WIDTH), F32)] * 6 + [SDS((1, WIDTH), F32)] * 3,
        compiler_params=_params(("parallel", "arbitrary", "arbitrary")))(*ins, states, dstates, do3, rk, lnw, lnb)
    return outs[:6], outs[6:]


def _head(o_attn, o_rwkv, z_attn, z_rwkv, gm, x2, tgt, wua, wur, wout, g2):
    n = x2.shape[0]
    tm = 256
    nt = n // tm
    d = D_MODEL

    def body(oa_ref, or_ref, za_ref, zr_ref, gm_ref, x_ref, t_ref, wua_ref, wur_ref, wout_ref, g2_ref,
             dxo_ref, doa_ref, dor_ref, dza_ref, dzr_ref, dgm_ref, dwua_ref, dwur_ref, dwout_ref, dg2_ref, loss_ref, lacc):
        i = pl.program_id(0)
        oa, orw, za, zr = oa_ref[...], or_ref[...], za_ref[...], zr_ref[...]
        ga, gb = gm_ref[:, 0:d], gm_ref[:, d:2 * d]
        am = (oa * _silu(za)).astype(BF16)
        bm = (orw * _silu(zr)).astype(BF16)
        ya, yb = _dot(am, wua_ref[...]), _dot(bm, wur_ref[...])
        sa, sb = jax.nn.sigmoid(ga), jax.nn.sigmoid(gb)
        merged = (sa * ya + sb * yb).astype(BF16)
        out = _dot(merged, wout_ref[...])
        rs = lax.rsqrt(jnp.mean(out * out, axis=-1, keepdims=True) + RMS_EPS)
        g2 = g2_ref[...]
        err = x_ref[...] + out * rs * g2 - t_ref[...]
        lpart = jnp.sum(err * err, axis=0, keepdims=True)
        dxo = err * (1.0 / d)
        dxo_ref[...] = dxo
        dg2 = jnp.sum(dxo * out * rs, axis=0, keepdims=True)
        gd = dxo * g2
        dout = (rs * (gd - out * (rs * rs) * jnp.mean(gd * out, axis=-1, keepdims=True))).astype(BF16)
        dmerged = _dot_nt(dout, wout_ref[...])
        dwout = _dot_tn(merged, dout)
        dya, dyb = (dmerged * sa).astype(BF16), (dmerged * sb).astype(BF16)
        dgm_ref[:, 0:d] = (dmerged * ya * sa * (1.0 - sa)).astype(BF16)
        dgm_ref[:, d:2 * d] = (dmerged * yb * sb * (1.0 - sb)).astype(BF16)
        dam, dbm = _dot_nt(dya, wua_ref[...]), _dot_nt(dyb, wur_ref[...])
        dwua, dwur = _dot_tn(am, dya), _dot_tn(bm, dyb)
        doa_ref[...] = dam * _silu(za)
        dza_ref[...] = (dam * oa * _dsilu(za)).astype(BF16)
        dor_ref[...] = dbm * _silu(zr)
        dzr_ref[...] = (dbm * orw * _dsilu(zr)).astype(BF16)

        @pl.when(i == 0)
        def _():
            dwua_ref[...], dwur_ref[...], dwout_ref[...], dg2_ref[...], lacc[...] = dwua, dwur, dwout, dg2, lpart

        @pl.when(i != 0)
        def _():
            dwua_ref[...] += dwua
            dwur_ref[...] += dwur
            dwout_ref[...] += dwout
            dg2_ref[...] += dg2
            lacc[...] += lpart

        @pl.when(i == nt - 1)
        def _():
            loss_ref[...] = jnp.sum(lacc[...], axis=1, keepdims=True) * (0.5 / d)

    t512 = pl.BlockSpec((tm, WIDTH), lambda i: (i, 0))
    t1k = pl.BlockSpec((tm, d), lambda i: (i, 0))
    t2k = pl.BlockSpec((tm, 2 * d), lambda i: (i, 0))
    full = lambda r, c: pl.BlockSpec((r, c), lambda i: (0, 0))
    return pl.pallas_call(
        body, name="head_fwd_bwd", grid=(nt,),
        in_specs=[t512, t512, t512, t512, t2k, t1k, t1k, full(WIDTH, d), full(WIDTH, d), full(d, d), full(1, d)],
        out_specs=[t1k, t512, t512, t512, t512, t2k, full(WIDTH, d), full(WIDTH, d), full(d, d), full(1, d), full(1, 1)],
        out_shape=[SDS((n, d), F32), SDS((n, WIDTH), F32), SDS((n, WIDTH), F32), SDS((n, WIDTH), BF16), SDS((n, WIDTH), BF16),
                   SDS((n, 2 * d), BF16), SDS((WIDTH, d), F32), SDS((WIDTH, d), F32), SDS((d, d), F32), SDS((1, d), F32), SDS((1, 1), F32)],
        scratch_shapes=[pltpu.VMEM((1, d), F32)],
        compiler_params=_params(("arbitrary",)))(o_attn, o_rwkv, z_attn, z_rwkv, gm, x2, tgt, wua, wur, wout, g2)


def _mesh_pos():
    x, y, c = lax.axis_index("x"), lax.axis_index("y"), lax.axis_index("c")
    return 4 * x + 2 * y + c


def _coords(idx):
    return (idx // 4, (idx // 2) % 2, idx % 2)


def _exchange(srcs, to_all, name):
    n = len(srcs)

    def body(*refs):
        src_refs, dst_refs = refs[:n], refs[n:2 * n]
        send_sems, recv_sems, local_sems = refs[2 * n:]
        me = _mesh_pos()

        def piece(i, j):
            return src_refs[i] if to_all[i] else src_refs[i].at[j]

        def remote(i, off, peer, block, slot):
            return pltpu.make_async_remote_copy(src_ref=piece(i, block), dst_ref=dst_refs[i].at[slot],
                                                send_sem=send_sems.at[i, off - 1], recv_sem=recv_sems.at[i, off - 1],
                                                device_id=_coords(peer), device_id_type=MESH)

        local = [pltpu.make_async_copy(piece(i, me), dst_refs[i].at[me], local_sems.at[i]) for i in range(n)]
        for cp in local:
            cp.start()
        sends = []
        for off in range(1, N_DEV):
            to = (me + off) % N_DEV
            for i in range(n):
                sends.append(remote(i, off, to, to, me))
                sends[-1].start()
        for off in range(1, N_DEV):
            frm = (me + N_DEV - off) % N_DEV
            for i in range(n):
                remote(i, off, frm, me, frm).wait_recv()
        for cp in sends:
            cp.wait_send()
        for cp in local:
            cp.wait()

    outs = pl.pallas_call(
        body, name=name, in_specs=[pl.BlockSpec(memory_space=pltpu.HBM)] * n, out_specs=[pl.BlockSpec(memory_space=pltpu.HBM)] * n,
        out_shape=[SDS((N_DEV,) + s.shape[-2:], s.dtype) for s in srcs],
        scratch_shapes=[pltpu.SemaphoreType.DMA((n, N_DEV - 1)), pltpu.SemaphoreType.DMA((n, N_DEV - 1)), pltpu.SemaphoreType.DMA((n,))],
        compiler_params=pltpu.CompilerParams())(*srcs)
    return outs


_HBM = pl.BlockSpec(memory_space=pltpu.HBM)
_SEM = pl.BlockSpec(memory_space=pltpu.SEMAPHORE)
_EFFECT = pltpu.SideEffectType.DATAFLOW_SIDE_EFFECTING


def _send_copy(src_ref, land_ref, to_all, send_sems, recv_sems, i, off, block, slot, peer):
    k = i * (N_DEV - 1) + off - 1
    return pltpu.make_async_remote_copy(src_ref=src_ref if to_all else src_ref.at[block], dst_ref=land_ref.at[slot],
                                        send_sem=send_sems.at[k], recv_sem=recv_sems.at[k],
                                        device_id=_coords(peer), device_id_type=MESH)


def _send_start(srcs, to_all, name):
    n = len(srcs)

    def body(*refs):
        src_refs, land_refs = refs[:n], refs[n:2 * n]
        send_sems, recv_sems = refs[2 * n:2 * n + 2]
        me = _mesh_pos()
        for off in range(1, N_DEV):
            to = (me + off) % N_DEV
            for i in range(n):
                _send_copy(src_refs[i], land_refs[i], to_all, send_sems, recv_sems, i, off, to, me, to).start()
        refs[-1][...] = jnp.zeros_like(refs[-1])

    lands = [jnp.zeros((N_DEV,) + s.shape[-2:], s.dtype) for s in srcs]
    hbm = [pltpu.HBM(a.shape, a.dtype) for a in list(srcs) + lands]
    sems = pltpu.SemaphoreType.DMA((n * (N_DEV - 1),))
    outs = pl.pallas_call(
        body, name=name, out_shape=(sems, sems, *hbm, SDS((8, LANE), BF16)),
        in_specs=(_HBM,) * (2 * n), out_specs=(_SEM, _SEM) + (_HBM,) * (2 * n) + (pl.BlockSpec(memory_space=pltpu.VMEM),),
        input_output_aliases={i: 2 + i for i in range(2 * n)}, compiler_params=pltpu.CompilerParams(has_side_effects=_EFFECT),
    )(*[pltpu.with_memory_space_constraint(a, pltpu.HBM) for a in list(srcs) + lands])
    return outs[0], outs[1], outs[2:2 + n], outs[2 + n:2 + 2 * n], outs[-1]


def _send_wait(send_sems, recv_sems, srcs_thru, lands_thru, to_all, after, name):
    n = len(srcs_thru)

    def body(*refs):
        src_refs, land_refs = refs[:n], refs[n:2 * n]
        send_sems, recv_sems = refs[2 * n:2 * n + 2]
        me = _mesh_pos()
        for off in range(1, N_DEV):
            to, frm = (me + off) % N_DEV, (me + N_DEV - off) % N_DEV
            for i in range(n):
                _send_copy(src_refs[i], land_refs[i], to_all, send_sems, recv_sems, i, off, to, me, to).wait_send()
                _send_copy(src_refs[i], land_refs[i], to_all, send_sems, recv_sems, i, off, me, frm, frm).wait_recv()

    hbm = tuple(pltpu.HBM(a.shape, a.dtype) for a in list(srcs_thru) + list(lands_thru))
    outs = pl.pallas_call(
        body, name=name, out_shape=hbm, in_specs=(_HBM,) * (2 * n) + (_SEM, _SEM, pl.BlockSpec(memory_space=pl.ANY)),
        out_specs=(_HBM,) * (2 * n), input_output_aliases={i: i for i in range(2 * n)},
        compiler_params=pltpu.CompilerParams(has_side_effects=_EFFECT),
    )(*srcs_thru, *lands_thru, send_sems, recv_sems, after)
    return outs[n:]


def _gather(srcs, after, name):
    n = len(srcs)

    def body(*refs):
        src_refs, dst_refs = refs[:n], refs[n + 1:2 * n + 1]
        send_sems, recv_sems, local_sems = refs[2 * n + 1:]
        x, y, c = lax.axis_index("x"), lax.axis_index("y"), lax.axis_index("c")
        me, sibling = (x, y, c), (x, y, 1 - c)
        chips = [(1 - x, y), (x, 1 - y), (1 - x, 1 - y)]

        def slot(i, dev):
            return dst_refs[i].at[4 * dev[0] + 2 * dev[1] + dev[2]]

        def copy(i, k, block, to, own=False):
            return pltpu.make_async_remote_copy(src_ref=src_refs[i] if own else slot(i, block), dst_ref=slot(i, block),
                                                send_sem=send_sems.at[i, k], recv_sem=recv_sems.at[i, k],
                                                device_id=to, device_id_type=MESH)

        local = [pltpu.make_async_copy(src_refs[i], slot(i, me), local_sems.at[i]) for i in range(n)]
        for cp in local:
            cp.start()
        sends = []
        for i in range(n):
            sends.append(copy(i, 0, me, sibling, own=True))
            sends += [copy(i, 1 + j, me, (*chip, c), own=True) for j, chip in enumerate(chips)]
        for cp in sends:
            cp.start()
        for j, chip in enumerate(chips):
            for i in range(n):
                copy(i, 1 + j, (*chip, c), me).wait_recv()
                sends.append(copy(i, 4 + j, (*chip, c), sibling))
                sends[-1].start()
        for i in range(n):
            copy(i, 0, sibling, me).wait_recv()
            for j, chip in enumerate(chips):
                copy(i, 4 + j, (*chip, 1 - c), me).wait_recv()
        for cp in sends:
            cp.wait_send()
        for cp in local:
            cp.wait()

    return pl.pallas_call(
        body, name=name, in_specs=[pl.BlockSpec(memory_space=pltpu.HBM)] * n + [pl.BlockSpec(memory_space=pl.ANY)],
        out_specs=[pl.BlockSpec(memory_space=pltpu.HBM)] * n, out_shape=[SDS((N_DEV,) + s.shape, s.dtype) for s in srcs],
        scratch_shapes=[pltpu.SemaphoreType.DMA((n, N_DEV - 1)), pltpu.SemaphoreType.DMA((n, N_DEV - 1)), pltpu.SemaphoreType.DMA((n,))],
        compiler_params=pltpu.CompilerParams())(*srcs, after)


def _adamw(parts, w, m, v, tr, name, own=None):
    rows, cols = w.shape
    c1, c2 = 1.0 - ADAM_B1 ** ADAM_STEP, 1.0 - ADAM_B2 ** ADAM_STEP

    def body(p_ref, *refs):
        w_ref, m_ref, v_ref, g_ref, d_ref, nm_ref, nv_ref = refs[-7:]
        me = _mesh_pos()

        def part(j):
            return p_ref[j] if own is None else jnp.where(me == j, refs[0][...], p_ref[j])

        g = part(0).astype(F32)
        for j in range(1, N_DEV):
            g = g + part(j).astype(F32)
        nm = ADAM_B1 * m_ref[...] + (1.0 - ADAM_B1) * g
        nv = ADAM_B2 * v_ref[...] + (1.0 - ADAM_B2) * jnp.square(g)
        g_ref[...] = g
        nm_ref[...] = nm
        nv_ref[...] = nv
        d_ref[...] = -ADAM_LR * ((nm / c1) / (jnp.sqrt(nv / c2) + ADAM_EPS) + ADAM_WD * w_ref[...])

    t = pl.BlockSpec((tr, cols), lambda i: (i, 0))
    extra = [] if own is None else [own]
    return pl.pallas_call(
        body, name=name, grid=(rows // tr,), in_specs=[pl.BlockSpec((N_DEV, tr, cols), lambda i: (0, i, 0))] + [t] * (3 + len(extra)),
        out_specs=[t] * 4, out_shape=[SDS((rows, cols), F32)] * 4, compiler_params=_params(("parallel",)))(parts, *extra, w, m, v)


SHARDED = (("w_in", D_MODEL, IN_COLS // N_DEV, True, 128), ("w_up_attn", WIDTH, D_MODEL // N_DEV, True, WIDTH),
           ("w_up_rwkv", WIDTH, D_MODEL // N_DEV, True, WIDTH), ("w_out", D_MODEL // N_DEV, D_MODEL, False, D_MODEL // N_DEV),
           ("rwkv_w_up", LORA, WIDTH // N_DEV, True, LORA), ("rwkv_a_up", LORA, WIDTH // N_DEV, True, LORA))
LOSS_SLOT = sum(n for _, n in SMALL)


def _pack_small(small, extra=None):
    flat = [small[n].reshape(-1).astype(F32) for n, _ in SMALL]
    flat.append(jnp.zeros((1,), F32) if extra is None else extra.reshape(1))
    flat.append(jnp.zeros((SMALL_ROWS * LANE - LOSS_SLOT - 1,), F32))
    return jnp.concatenate(flat).reshape(SMALL_ROWS, LANE)


def _unpack_small(packed, shapes):
    flat = packed.reshape(-1)
    out, off = {}, 0
    for n, cnt in SMALL:
        out[n] = flat[off:off + cnt].reshape(shapes[n])
        off += cnt
    return out, flat[LOSS_SLOT]


def _whole(gathered, by_cols):
    if not by_cols:
        return gathered.reshape(-1, gathered.shape[-1])
    return gathered.transpose(1, 0, 2).reshape(gathered.shape[1], -1)


def _per_owner(full, by_cols):
    if not by_cols:
        return full.reshape(N_DEV, -1, full.shape[-1])
    return full.reshape(full.shape[0], N_DEV, -1).transpose(1, 0, 2)


def _local_step(x, loss_target, sm, wts):
    bsz, s, d = x.shape
    n = bsz * s
    x2, tgt = x.reshape(n, d), loss_target.reshape(n, d)
    bidx = jnp.asarray(_bucket_tables())
    w_in = wts["w_in"]
    segs = (("qkv", 0, QKV_COLS, 1536), ("za", OFF_ZA, WIDTH, 512), ("pr", OFF_PR, PR_COLS, PR_COLS), ("zr", OFF_ZR, WIDTH, 512),
            ("gm", OFF_GM, 2 * D_MODEL, 1024))

    h, rs = _prenorm(x2, sm["pre_norm_gain"])
    w_seg = {nm: w_in[:, off:off + cnt] for nm, off, cnt, _ in segs}
    proj = {nm: _mm(h, w_seg[nm], tn, "proj_" + nm) for nm, _, _, tn in segs}
    qkv3 = proj["qkv"].reshape(bsz, s, QKV_COLS)
    pr3 = proj["pr"].reshape(bsz, s, PR_COLS)

    o_attn, lse = _attn_fwd(qkv3, sm["rel_bias"], bidx)
    rk = sm["rwkv_r_k"].reshape(1, WIDTH)
    pre_args = (sm["rwkv_shift_mix"], sm["rwkv_w0"], wts["rwkv_w_up"], sm["rwkv_a0"], wts["rwkv_a_up"], sm["rwkv_k_k"], sm["rwkv_k_a"])
    scan_in = _rwkv_pre(pr3, *pre_args)
    o_rwkv, states, consts = _rwkv_scan(scan_in, rk, sm["rwkv_ln_w"], sm["rwkv_ln_b"])

    (dxo, do_attn, do_rwkv, dza, dzr, dgm, g_wua, g_wur, g_wout, g_post, loss) = _head(
        o_attn.reshape(n, WIDTH), o_rwkv.reshape(n, WIDTH), proj["za"], proj["zr"], proj["gm"], x2, tgt,
        wts["w_up_attn"], wts["w_up_rwkv"], wts["w_out"], sm["post_norm_gain"])

    dqkv, dbias = _attn_bwd(qkv3, o_attn, lse, do_attn.reshape(bsz, s, WIDTH), sm["rel_bias"], bidx)
    g_bias = _bias_grad(dbias, bidx)[:, :N_BUCKET].T

    scan_cots, (g_rk, g_lnw, g_lnb) = _rwkv_scan_bwd(scan_in, states, consts, do_rwkv.reshape(bsz, s, WIDTH), rk, sm["rwkv_ln_w"],
                                                     sm["rwkv_ln_b"])
    dprs, g_mix, g_w0, g_wup, g_a0, g_aup, g_kk, g_ka = _rwkv_pre_bwd(pr3, scan_cots, *pre_args)
    dpr = _shift_bwd(dprs, sm["rwkv_shift_mix"]).reshape(n, PR_COLS)

    dsegs = [(dqkv.reshape(9, n, WIDTH), 0, QKV_COLS, WIDTH), (dza, OFF_ZA, WIDTH, WIDTH), (dpr, OFF_PR, PR_COLS, PR_COLS),
             (dzr, OFF_ZR, WIDTH, WIDTH), (dgm, OFF_GM, 2 * D_MODEL, D_MODEL)]
    full = {"w_in": jnp.concatenate([_mm_tn(h, t, tn, "gw_in_%d" % j) for j, (t, _, _, tn) in enumerate(dsegs)], axis=1),
            "w_up_attn": g_wua, "w_up_rwkv": g_wur, "w_out": g_wout, "rwkv_w_up": g_wup, "rwkv_a_up": g_aup}
    blocks = [_per_owner(full[nm], by_cols).astype(BF16) for nm, _, _, by_cols, _ in SHARDED]
    me = 4 * lax.axis_index("x") + 2 * lax.axis_index("y") + lax.axis_index("c")
    own = [lax.dynamic_index_in_dim(b, me, 0, keepdims=False) for b in blocks]
    send_sems, recv_sems, blocks_thru, lands_thru, token = _send_start(blocks, False, "grads_start")
    dh = _mm_nt(dsegs[0][0], w_seg["qkv"], token, "dh_qkv")
    grad_x, g_pre = _dh_rest_prenorm_bwd([t for t, *_ in dsegs[1:]], [w_seg[nm] for nm in ("za", "pr", "zr", "gm")], dh, x2, rs,
                                         sm["pre_norm_gain"], dxo)
    landed = _send_wait(send_sems, recv_sems, blocks_thru, lands_thru, False, g_pre, "grads_wait")

    small = {"pre_norm_gain": g_pre, "rel_bias": g_bias, "rwkv_shift_mix": g_mix, "rwkv_w0": g_w0, "rwkv_a0": g_a0, "rwkv_k_k": g_kk,
             "rwkv_k_a": g_ka, "rwkv_r_k": g_rk, "rwkv_ln_w": g_lnw, "rwkv_ln_b": g_lnb, "post_norm_gain": g_post}
    return loss[0, 0], grad_x.reshape(bsz, s, d), (landed, own), small


def kernel(x, pre_norm_gain, w_in, rel_bias, rwkv_shift_mix, rwkv_w0, rwkv_w_up, rwkv_a0, rwkv_a_up, rwkv_k_k, rwkv_k_a, rwkv_r_k, rwkv_ln_w, rwkv_ln_b, w_up_attn, w_up_rwkv, w_out, post_norm_gain, loss_target, m_pre_norm_gain, m_w_in, m_rel_bias, m_rwkv_shift_mix, m_rwkv_w0, m_rwkv_w_up, m_rwkv_a0, m_rwkv_a_up, m_rwkv_k_k, m_rwkv_k_a, m_rwkv_r_k, m_rwkv_ln_w, m_rwkv_ln_b, m_w_up_attn, m_w_up_rwkv, m_w_out, m_post_norm_gain, v_pre_norm_gain, v_w_in, v_rel_bias, v_rwkv_shift_mix, v_rwkv_w0, v_rwkv_w_up, v_rwkv_a0, v_rwkv_a_up, v_rwkv_k_k, v_rwkv_k_a, v_rwkv_r_k, v_rwkv_ln_w, v_rwkv_ln_b, v_w_up_attn, v_w_up_rwkv, v_w_out, v_post_norm_gain):
    names = [n for n, *_ in SHARDED] + [n for n, _ in SMALL]
    loc = dict(locals())
    w = {n: loc[n] for n in names}
    m = {n: loc["m_" + n] for n in names}
    v = {n: loc["v_" + n] for n in names}
    shapes = {n: w[n].shape for n in names}
    order = ["pre_norm_gain", "w_in", "rel_bias", "rwkv_shift_mix", "rwkv_w0", "rwkv_w_up", "rwkv_a0", "rwkv_a_up", "rwkv_k_k", "rwkv_k_a",
             "rwkv_r_k", "rwkv_ln_w", "rwkv_ln_b", "w_up_attn", "w_up_rwkv", "w_out", "post_norm_gain"]
    shard2d = lambda t, n, r, c: t[n].reshape(r, c)

    shards = [shard2d(w, n, r, c).astype(BF16) for n, r, c, _, _ in SHARDED]
    send_sems, recv_sems, srcs_thru, lands_thru, token = _send_start(shards[1:], True, "weights_start")
    gathered = list(_gather(shards[:1], token, "gather_weights"))
    landed = _send_wait(send_sems, recv_sems, srcs_thru, lands_thru, True, gathered[0], "weights_wait")
    me = 4 * lax.axis_index("x") + 2 * lax.axis_index("y") + lax.axis_index("c")
    gathered += [lax.dynamic_update_index_in_dim(g, sh, me, 0) for g, sh in zip(landed, shards[1:])]
    wts = {n: _whole(g, by_cols) for (n, _, _, by_cols, _), g in zip(SHARDED, gathered)}

    loss, grad_x, (landed, own), small = _local_step(x, loss_target, w, wts)
    (small_parts,) = _exchange([_pack_small(small, loss)], [True], "exchange_small")

    outs = [{}, {}, {}, {}]
    for (n, r, c, _, tr), p, o_ in zip(SHARDED, landed, own):
        res = _adamw(p, shard2d(w, n, r, c), shard2d(m, n, r, c), shard2d(v, n, r, c), tr, "adamw_" + n, own=o_)
        for o, t in zip(outs, res):
            o[n] = t.reshape(shapes[n])
    res = _adamw(small_parts, _pack_small(w), _pack_small(m), _pack_small(v), SMALL_ROWS, "adamw_small")
    for o, t in zip(outs, res):
        o.update(_unpack_small(t, shapes)[0])
    loss = _unpack_small(res[0], shapes)[1]
    return (loss, grad_x, *[o[n] for o in outs for n in order])
```

```python
import functools
import math

import numpy as np
import jax
import jax.numpy as jnp
from jax import lax
from jax.experimental import pallas as pl
from jax.experimental.pallas import tpu as pltpu

F32, BF16 = jnp.float32, jnp.bfloat16
SDS = jax.ShapeDtypeStruct
MESH = pl.DeviceIdType.MESH

N_DEV = 8
D_MODEL = 1024
HEAD = 64
N_HEAD = 8
WIDTH = N_HEAD * HEAD
DILATIONS = (1, 4, 16)
QB = 128
N_BUCKET = 32
MAX_DIST = 2048
LORA = 64
QKV_COLS = 9 * WIDTH
PR_COLS = 3 * WIDTH + 2 * LORA
IN_COLS = QKV_COLS + WIDTH + PR_COLS + WIDTH + 2 * D_MODEL
OFF_ZA, OFF_PR, OFF_ZR, OFF_GM = QKV_COLS, QKV_COLS + WIDTH, QKV_COLS + WIDTH + PR_COLS, QKV_COLS + 2 * WIDTH + PR_COLS
RMS_EPS = 1e-6
GN_EPS = 64e-5
SCALE = 1.0 / math.sqrt(HEAD)
CHUNK = 64
CHUNK_GROUP = 32
BWD_GROUP = 16
EARLY = 8
NEG = -1e30
LANE = 128

ADAM_LR, ADAM_B1, ADAM_B2, ADAM_EPS, ADAM_WD, ADAM_STEP = 0.001, 0.9, 0.999, 1e-08, 0.01, 10

VMEM_LIMIT = 56 * 1024 * 1024

SMALL = (("pre_norm_gain", 1024), ("rel_bias", 768), ("rwkv_shift_mix", 1664), ("rwkv_w0", 512), ("rwkv_a0", 512),
         ("rwkv_k_k", 512), ("rwkv_k_a", 512), ("rwkv_r_k", 512), ("rwkv_ln_w", 512), ("rwkv_ln_b", 512),
         ("post_norm_gain", 1024))
SMALL_ROWS = 64


def _params(sem=None):
    return pltpu.CompilerParams(dimension_semantics=sem, vmem_limit_bytes=VMEM_LIMIT)


def _dot(a, b):
    return jnp.dot(a, b, preferred_element_type=F32)


def _dot_nt(a, b):
    return lax.dot_general(a, b, (((1,), (1,)), ((), ())), preferred_element_type=F32)


def _dot_tn(a, b):
    return lax.dot_general(a, b, (((0,), (0,)), ((), ())), preferred_element_type=F32)


@jax.custom_vjp
def _bdot(a, b):
    return _dot(a.astype(BF16), b.astype(BF16))


def _bdot_fwd(a, b):
    return _bdot(a, b), (a, b)


def _bdot_bwd(res, g):
    a, b = res
    gb = g.astype(BF16)
    return _dot_nt(gb, b.astype(BF16)), _dot_tn(a.astype(BF16), gb)


_bdot.defvjp(_bdot_fwd, _bdot_bwd)


def _silu(z):
    return z * jax.nn.sigmoid(z)


def _dsilu(z):
    s = jax.nn.sigmoid(z)
    return s * (1.0 + z * (1.0 - s))


def _softplus(x):
    return jnp.maximum(x, 0.0) + jnp.log(1.0 + jnp.exp(-jnp.abs(x)))


def _bucket_tables():
    qi = np.arange(QB)[:, None] + QB
    ki = np.arange(2 * QB)[None, :]
    rel = np.maximum(qi - ki, 0)
    out = []
    for d in DILATIONS:
        dist = rel * d
        max_exact = N_BUCKET // 2
        ratio = np.log(np.maximum(dist, 1).astype(np.float32) / max_exact) / np.float32(math.log(MAX_DIST / max_exact))
        large = max_exact + (ratio * (N_BUCKET - max_exact)).astype(np.int32)
        large = np.minimum(large, N_BUCKET - 1)
        out.append(np.where(dist < max_exact, dist, large).astype(np.int32))
    return np.stack(out)


def _prenorm(x2, g):
    n, d = x2.shape
    tm = 1024

    def body(x_ref, g_ref, h_ref, rs_ref):
        x = x_ref[...]
        rs = lax.rsqrt(jnp.mean(x * x, axis=-1, keepdims=True) + RMS_EPS)
        h_ref[...] = (x * rs * g_ref[...]).astype(BF16)
        rs_ref[...] = rs

    return pl.pallas_call(
        body, name="prenorm", grid=(n // tm,),
        in_specs=[pl.BlockSpec((tm, d), lambda i: (i, 0)), pl.BlockSpec((1, d), lambda i: (0, 0))],
        out_specs=[pl.BlockSpec((tm, d), lambda i: (i, 0)), pl.BlockSpec((tm, 1), lambda i: (i, 0))],
        out_shape=[SDS((n, d), BF16), SDS((n, 1), F32)], compiler_params=_params(("parallel",)))(x2, g)


def _mm(a, b, tn, name):
    m, k = a.shape
    n = b.shape[1]
    tm = 1024

    def body(a_ref, b_ref, o_ref):
        o_ref[...] = _dot(a_ref[...], b_ref[...])

    return pl.pallas_call(
        body, name=name, grid=(n // tn, m // tm),
        in_specs=[pl.BlockSpec((tm, k), lambda j, i: (i, 0)), pl.BlockSpec((k, tn), lambda j, i: (0, j))],
        out_specs=pl.BlockSpec((tm, tn), lambda j, i: (i, j)),
        out_shape=SDS((m, n), F32), compiler_params=_params(("parallel", "parallel")))(a, b)


def _mm_nt(a, b, after, name):
    m, seg = a.shape[1], a.shape[2]
    d, k = b.shape
    tm = 1024
    per = 3
    tk = per * seg

    def body(a_ref, b_ref, after_ref, o_ref):
        r = sum(_dot_nt(a_ref[j].astype(BF16), b_ref[:, seg * j:seg * (j + 1)]) for j in range(per))

        @pl.when(pl.program_id(1) == 0)
        def _():
            o_ref[...] = r

        @pl.when(pl.program_id(1) != 0)
        def _():
            o_ref[...] += r

    in_specs = [pl.BlockSpec((per, tm, seg), lambda i, j: (j, i, 0)), pl.BlockSpec((d, tk), lambda i, j: (0, j)),
                pl.BlockSpec(after.shape, lambda i, j: (0, 0))]
    return pl.pallas_call(
        body, name=name, grid=(m // tm, k // tk), in_specs=in_specs, out_specs=pl.BlockSpec((tm, d), lambda i, j: (i, 0)),
        out_shape=SDS((m, d), F32), compiler_params=_params(("parallel", "arbitrary")))(a, b, after)


def _dh_rest_prenorm_bwd(a_list, b_list, acc, x2, rs, g1, dxo):
    m, d = acc.shape
    tm = 512
    n = len(a_list)

    def body(*refs):
        x_ref, rs_ref, g_ref, dxo_ref, gx_ref, dg_ref = refs[2 * n + 1:]
        dh = refs[2 * n][...]
        for a_ref, b_ref in zip(refs[:n], refs[n:2 * n]):
            dh = dh + _dot_nt(a_ref[...].astype(BF16), b_ref[...])
        x, r = x_ref[...], rs_ref[...]
        gd = dh * g_ref[...]
        gx_ref[...] = dxo_ref[...] + r * (gd - x * (r * r) * jnp.mean(gd * x, axis=-1, keepdims=True))
        dg = jnp.sum(dh * x * r, axis=0, keepdims=True)

        @pl.when(pl.program_id(0) == 0)
        def _():
            dg_ref[...] = dg

        @pl.when(pl.program_id(0) != 0)
        def _():
            dg_ref[...] += dg

    t = pl.BlockSpec((tm, d), lambda i: (i, 0))
    in_specs = [pl.BlockSpec((tm, a.shape[1]), lambda i: (i, 0)) for a in a_list]
    in_specs += [pl.BlockSpec(b.shape, lambda i: (0, 0)) for b in b_list]
    in_specs += [t, t, pl.BlockSpec((tm, 1), lambda i: (i, 0)), pl.BlockSpec((1, d), lambda i: (0, 0)), t]
    return pl.pallas_call(
        body, name="dh_rest_prenorm_bwd", grid=(m // tm,), in_specs=in_specs, out_specs=[t, pl.BlockSpec((1, d), lambda i: (0, 0))],
        out_shape=[SDS((m, d), F32), SDS((1, d), F32)], compiler_params=_params(("arbitrary",)))(*a_list, *b_list, acc, x2, rs, g1, dxo)


def _mm_tn(a, b, tn, name):
    split = b.ndim == 3
    m, k1 = a.shape
    per = 3 if split else 1
    seg = b.shape[2] if split else tn
    tn = per * seg
    n2 = b.shape[0] * seg if split else b.shape[1]
    tm = 1024

    def body(a_ref, b_ref, o_ref):
        first = pl.program_id(1) == 0
        for j in range(per):
            r = _dot_tn(a_ref[...], (b_ref[j] if split else b_ref[...]).astype(BF16))
            cols = slice(seg * j, seg * (j + 1))

            @pl.when(first)
            def _(r=r, cols=cols):
                o_ref[:, cols] = r

            @pl.when(jnp.logical_not(first))
            def _(r=r, cols=cols):
                o_ref[:, cols] += r

    b_spec = pl.BlockSpec((per, tm, seg), lambda j, i: (j, i, 0)) if split else pl.BlockSpec((tm, tn), lambda j, i: (i, j))
    return pl.pallas_call(
        body, name=name, grid=(n2 // tn, m // tm),
        in_specs=[pl.BlockSpec((tm, k1), lambda j, i: (i, 0)), b_spec],
        out_specs=pl.BlockSpec((k1, tn), lambda j, i: (0, j)),
        out_shape=SDS((k1, n2), F32), compiler_params=_params(("parallel", "arbitrary")))(a, b)


def _ds(start, d):
    return pl.ds(start, QB) if d == 1 else pl.ds(start, QB, stride=d)


def _fill_bias(tab_ref, bidx_ref, bias_sc, hp):
    for g in range(3):
        bi = bidx_ref[g]
        for h in range(2):
            acc = jnp.zeros((QB, 2 * QB), F32)
            for j in range(N_BUCKET):
                acc = jnp.where(bi == j, tab_ref[j, g * N_HEAD + hp * 2 + h], acc)
            bias_sc[g * 2 + h] = acc


def _block_starts(it, d, nb):
    rho = it // nb
    n = it % nb
    st = rho + d * QB * n
    stp = rho + d * QB * jnp.maximum(n - 1, 0)
    if d == 1:
        st, stp = pl.multiple_of(QB * it, QB), pl.multiple_of(QB * jnp.maximum(it - 1, 0), QB)
    return st, stp, n > 0


ATTN_BLOCKS_FWD = 8
ATTN_BLOCKS_BWD = 4


def _bdot3(a, b, dims):
    return lax.dot_general(a, b, (dims, ((0,), (0,))), preferred_element_type=F32)


def _attn_operands(q_ref, k_ref, v_ref, bias_sc, g, d, nb, it0, nblk):
    two = nb > 1
    nk = 2 * QB if two else QB
    ii = lax.broadcasted_iota(jnp.int32, (QB, nk), 0)
    cc = lax.broadcasted_iota(jnp.int32, (QB, nk), 1)
    qs, ks, vs, pens, starts = [], [], [], [], []
    for u in range(nblk):
        st, stp, hasprev = _block_starts(it0 + u, d, nb)
        qf = q_ref[0, _ds(st, d), :]
        if two:
            kf = jnp.concatenate([k_ref[0, _ds(stp, d), :], k_ref[0, _ds(st, d), :]], axis=0).astype(BF16)
            vf = jnp.concatenate([v_ref[0, _ds(stp, d), :], v_ref[0, _ds(st, d), :]], axis=0).astype(BF16)
            own = jnp.logical_and(cc >= QB, ii >= cc - QB)
            prev = jnp.logical_and(jnp.logical_and(cc < QB, cc >= ii), hasprev)
            pen = jnp.where(jnp.logical_or(own, prev), 0.0, NEG)
        else:
            kf, vf = k_ref[0, _ds(st, d), :].astype(BF16), v_ref[0, _ds(st, d), :].astype(BF16)
            pen = jnp.where(ii >= cc, 0.0, NEG)
        for h in range(2):
            qs.append(_one_head(qf, h).astype(BF16))
            ks.append(kf)
            vs.append(vf)
            pens.append(pen + (bias_sc[g * 2 + h] if two else bias_sc[g * 2 + h, :, QB:2 * QB]))
        starts.append((st, stp))
    return _stack(qs), _stack(ks), _stack(vs), _stack(pens), starts


def _one_head(x, h):
    lane = lax.broadcasted_iota(jnp.int32, x.shape, 1)
    return jnp.where(lane >= HEAD if h == 1 else lane < HEAD, x, 0.0)


def _pick_heads(x, u):
    lane = lax.broadcasted_iota(jnp.int32, x.shape[1:], 1)
    return jnp.where(lane < HEAD, x[2 * u], x[2 * u + 1])


def _add_heads(x, u):
    return x[2 * u] + x[2 * u + 1]


def _attn_fwd(qkv3, rel_bias, bidx):
    bsz, s, _ = qkv3.shape
    rt = 256

    def body(tab_ref, bidx_ref, *refs):
        q_refs, k_refs, v_refs = refs[0:3], refs[3:6], refs[6:9]
        o_ref, lse_ref = refs[9:11]
        bias_sc, num_sc, den_sc, m_sc = refs[11:]
        pl.when(pl.program_id(1) == 0)(lambda: _fill_bias(tab_ref, bidx_ref, bias_sc, pl.program_id(0)))
        for g, d in enumerate(DILATIONS):
            nb = s // (QB * d)

            def blk(it, c, g=g, d=d, nb=nb):
                q, k, v, bias, starts = _attn_operands(q_refs[g], k_refs[g], v_refs[g], bias_sc, g, d, nb, it * ATTN_BLOCKS_FWD,
                                                       ATTN_BLOCKS_FWD)
                sc = _bdot3(q, k, ((2,), (2,))) * SCALE + bias
                m = jnp.max(sc, axis=-1, keepdims=True)
                p = jnp.exp(sc - m)
                den = jnp.sum(p, axis=-1, keepdims=True)
                num = _bdot3(p.astype(BF16), v, ((2,), (1,)))
                den, m = jnp.broadcast_to(den, num.shape), jnp.broadcast_to(m, num.shape)
                for u, (st, _) in enumerate(starts):
                    num_sc[g, _ds(st, d), :] = _pick_heads(num, u)
                    den_sc[g, _ds(st, d), :] = _pick_heads(den, u)
                    m_sc[g, _ds(st, d), :] = _pick_heads(m, u)
                return c

            lax.fori_loop(0, s // QB // ATTN_BLOCKS_FWD, blk, 0)

        def merge(i, c):
            rows = pl.ds(pl.multiple_of(i * rt, rt), rt)
            m0, m1, m2 = m_sc[0, rows, :], m_sc[1, rows, :], m_sc[2, rows, :]
            mall = jnp.maximum(jnp.maximum(m0, m1), m2)
            w0, w1, w2 = jnp.exp(m0 - mall), jnp.exp(m1 - mall), jnp.exp(m2 - mall)
            num = w0 * num_sc[0, rows, :] + w1 * num_sc[1, rows, :] + w2 * num_sc[2, rows, :]
            den = w0 * den_sc[0, rows, :] + w1 * den_sc[1, rows, :] + w2 * den_sc[2, rows, :]
            o_ref[0, rows, :] = num / den
            lse_ref[0, rows, :] = mall + jnp.log(den)
            return c

        lax.fori_loop(0, s // rt, merge, 0)

    col = lambda w, g: (lambda hp, b: (b, 0, (w * 3 + g) * 4 + hp))
    in_specs = [pl.BlockSpec(memory_space=pltpu.SMEM), pl.BlockSpec((3, QB, 2 * QB), lambda hp, b: (0, 0, 0))]
    in_specs += [pl.BlockSpec((1, s, LANE), col(w, g)) for w in range(3) for g in range(3)]
    out_spec = pl.BlockSpec((1, s, LANE), lambda hp, b: (b, 0, hp))
    return pl.pallas_call(
        body, name="attn_fwd", grid=(4, bsz), in_specs=in_specs, out_specs=[out_spec, out_spec],
        out_shape=[SDS((bsz, s, WIDTH), F32), SDS((bsz, s, WIDTH), F32)],
        scratch_shapes=[pltpu.VMEM((6, QB, 2 * QB), F32), pltpu.VMEM((3, s, LANE), F32), pltpu.VMEM((3, s, LANE), F32),
                        pltpu.VMEM((3, s, LANE), F32)],
        compiler_params=_params(("arbitrary", "arbitrary")))(rel_bias, bidx, *([qkv3] * 9))


def _attn_bwd(qkv3, o3, lse3, do3, rel_bias, bidx):
    bsz, s, _ = qkv3.shape
    rt = 256

    def body(tab_ref, bidx_ref, *refs):
        q_refs, k_refs, v_refs = refs[0:3], refs[3:6], refs[6:9]
        o_ref, lse_ref, do_ref, dqkv_ref, db_ref, bias_sc, delta_sc, acc_sc = refs[9:]
        dq_refs, dk_refs, dv_refs = ([acc_sc.at[w * 3 + g] for g in range(3)] for w in range(3))

        @pl.when(pl.program_id(1) == 0)
        def _():
            _fill_bias(tab_ref, bidx_ref, bias_sc, pl.program_id(0))
            db_ref[...] = jnp.zeros_like(db_ref)

        def prep(i, c):
            rows = pl.ds(pl.multiple_of(i * rt, rt), rt)
            prod = do_ref[0, rows, :] * o_ref[0, rows, :]
            d0 = jnp.sum(prod[:, :HEAD], axis=-1, keepdims=True)
            d1 = jnp.sum(prod[:, HEAD:], axis=-1, keepdims=True)
            delta_sc[rows, :] = jnp.concatenate([jnp.broadcast_to(d0, (rt, HEAD)), jnp.broadcast_to(d1, (rt, HEAD))], axis=1)
            z = jnp.zeros((rt, LANE), F32)
            for g in range(3):
                dk_refs[g][0, rows, :] = z
                dv_refs[g][0, rows, :] = z
            return c

        lax.fori_loop(0, s // rt, prep, 0)
        for g, d in enumerate(DILATIONS):
            nb = s // (QB * d)

            def blk(it, c, g=g, d=d, nb=nb):
                q, k, v, bias, starts = _attn_operands(q_refs[g], k_refs[g], v_refs[g], bias_sc, g, d, nb, it * ATTN_BLOCKS_BWD,
                                                       ATTN_BLOCKS_BWD)
                dos, lses, deltas = [], [], []
                for st, _ in starts:
                    dof, lsef, delf = do_ref[0, _ds(st, d), :], lse_ref[0, _ds(st, d), :], delta_sc[_ds(st, d), :]
                    for h in range(2):
                        dos.append(_one_head(dof, h).astype(BF16))
                        lses.append(lsef[:, HEAD * h:HEAD * h + 1])
                        deltas.append(delf[:, HEAD * h:HEAD * h + 1])
                do, lse, delta = _stack(dos), _stack(lses), _stack(deltas)
                p = jnp.exp(_bdot3(q, k, ((2,), (2,))) * SCALE + bias - lse)
                dv = _bdot3(p.astype(BF16), do, ((1,), (1,)))
                ds = p * (_bdot3(do, v, ((2,), (2,))) - delta)
                dsb = ds.astype(BF16)
                dq = _bdot3(dsb, k, ((2,), (1,))) * SCALE
                dk = _bdot3(dsb, q, ((1,), (1,))) * SCALE
                two = nb > 1
                for h in range(2):
                    dsum = sum(ds[2 * u + h] for u in range(ATTN_BLOCKS_BWD))
                    if two:
                        db_ref[0, g * 2 + h] += dsum
                    else:
                        db_ref[0, g * 2 + h, :, QB:2 * QB] += dsum
                for u, (st, stp) in enumerate(starts):
                    dq_refs[g][0, _ds(st, d), :] = _pick_heads(dq, u)
                    if two:
                        dk_refs[g][0, _ds(stp, d), :] += _add_heads(dk[:, :QB], u)
                        dv_refs[g][0, _ds(stp, d), :] += _add_heads(dv[:, :QB], u)
                    dk_refs[g][0, _ds(st, d), :] += _add_heads(dk[:, QB:] if two else dk, u)
                    dv_refs[g][0, _ds(st, d), :] += _add_heads(dv[:, QB:] if two else dv, u)
                return c

            lax.fori_loop(0, s // QB // ATTN_BLOCKS_BWD, blk, 0)

        def flush(i, c):
            rows = pl.ds(pl.multiple_of(i * rt, rt), rt)
            for j in range(9):
                dqkv_ref[j, 0, rows, :] = acc_sc[j, 0, rows, :].astype(BF16)
            return c

        lax.fori_loop(0, s // rt, flush, 0)

    col = lambda w, g: (lambda hp, b: (b, 0, (w * 3 + g) * 4 + hp))
    blk_spec = pl.BlockSpec((1, s, LANE), lambda hp, b: (b, 0, hp))
    in_specs = [pl.BlockSpec(memory_space=pltpu.SMEM), pl.BlockSpec((3, QB, 2 * QB), lambda hp, b: (0, 0, 0))]
    in_specs += [pl.BlockSpec((1, s, LANE), col(w, g)) for w in range(3) for g in range(3)]
    in_specs += [blk_spec] * 3
    out_specs = [pl.BlockSpec((9, 1, s, LANE), lambda hp, b: (0, b, 0, hp)), pl.BlockSpec((1, 6, QB, 2 * QB), lambda hp, b: (hp, 0, 0, 0))]
    out_shape = [SDS((9, bsz, s, WIDTH), BF16), SDS((4, 6, QB, 2 * QB), F32)]
    return pl.pallas_call(
        body, name="attn_bwd", grid=(4, bsz), in_specs=in_specs, out_specs=out_specs, out_shape=out_shape,
        scratch_shapes=[pltpu.VMEM((6, QB, 2 * QB), F32), pltpu.VMEM((s, LANE), F32), pltpu.VMEM((9, 1, s, LANE), F32)],
        compiler_params=_params(("parallel", "arbitrary")))(rel_bias, bidx, *([qkv3] * 9), o3, lse3, do3)


def _bias_grad(dbias, bidx):
    def body(db_ref, bidx_ref, o_ref):
        lane = lax.broadcasted_iota(jnp.int32, (1, LANE), 1)
        for g in range(3):
            bi = bidx_ref[g]
            for hp in range(4):
                for h in range(2):
                    mat = db_ref[hp, g * 2 + h]
                    row = jnp.zeros((1, LANE), F32)
                    for j in range(N_BUCKET):
                        part = jnp.sum(jnp.where(bi == j, mat, 0.0), axis=0, keepdims=True)
                        row = jnp.where(lane == j, jnp.sum(part, axis=1, keepdims=True), row)
                    hd = g * N_HEAD + hp * 2 + h
                    o_ref[hd:hd + 1, :] = row

    return pl.pallas_call(body, name="bias_grad", out_shape=SDS((3 * N_HEAD, LANE), F32), compiler_params=_params())(dbias, bidx)


def _pre_fn(r, k0, v, wl, al, w0, wup, a0, aup, kk_, ka_):
    u = w0 + _bdot(jnp.tanh(wl), wup)
    lw = -jnp.exp(-_softplus(-u) - 0.5)
    a = jax.nn.sigmoid(a0 + _bdot(al, aup))
    kkraw = k0 * kk_
    k = k0 * (1.0 + (a - 1.0) * ka_)
    return r, lw, k, v, kkraw, a


PRE_SPLIT = (0, WIDTH, 2 * WIDTH, 3 * WIDTH, 3 * WIDTH + LORA, 3 * WIDTH + 2 * LORA)


def _pre_pieces(prs):
    return [prs[:, a:b] for a, b in zip(PRE_SPLIT[:-1], PRE_SPLIT[1:])]


PRE_TT = 512


def _shifted(pr_ref, edge_ref, first, back):
    pr = pr_ref[0]
    tt = pr.shape[0]
    row = lax.broadcasted_iota(jnp.int32, (tt, 1), 0)
    if back:
        edge = jnp.where(first, 0.0, edge_ref[0, 7:8, :])
        return jnp.where(row == 0, edge, pltpu.roll(pr, 1, axis=0))
    edge = jnp.where(first, 0.0, edge_ref[0, 0:1, :])
    return jnp.where(row == tt - 1, edge, pltpu.roll(pr, tt - 1, axis=0))


def _rwkv_pre(pr3, mix, w0, wup, a0, aup, kk_, ka_):
    bsz, s, _ = pr3.shape
    tt = PRE_TT

    def body(pr_ref, edge_ref, mix_ref, w0_ref, wup_ref, a0_ref, aup_ref, kk_ref, ka_ref, *outs):
        pr = pr_ref[0]
        prev = _shifted(pr_ref, edge_ref, pl.program_id(1) == 0, True)
        prs = pr + (prev - pr) * mix_ref[...]
        vals = _pre_fn(*_pre_pieces(prs), w0_ref[...], wup_ref[...].astype(F32), a0_ref[...], aup_ref[...].astype(F32), kk_ref[...],
                       ka_ref[...])
        for o, val in zip(outs, vals):
            o[0] = val

    vec = lambda n: pl.BlockSpec((1, n), lambda b, i: (0, 0))
    mat = pl.BlockSpec((LORA, WIDTH), lambda b, i: (0, 0))
    in_specs = [pl.BlockSpec((1, tt, PR_COLS), lambda b, i: (b, i, 0)),
                pl.BlockSpec((1, 8, PR_COLS), lambda b, i: (b, jnp.maximum(i * (tt // 8) - 1, 0), 0)),
                vec(PR_COLS), vec(WIDTH), mat, vec(WIDTH), mat, vec(WIDTH), vec(WIDTH)]
    out_spec = pl.BlockSpec((1, tt, WIDTH), lambda b, i: (b, i, 0))
    return pl.pallas_call(
        body, name="rwkv_pre", grid=(bsz, s // tt), in_specs=in_specs, out_specs=[out_spec] * 6,
        out_shape=[SDS((bsz, s, WIDTH), F32)] * 6, compiler_params=_params(("parallel", "parallel")))(
            pr3, pr3, mix, w0, wup, a0, aup, kk_, ka_)


def _rwkv_pre_bwd(pr3, cots, mix, w0, wup, a0, aup, kk_, ka_):
    bsz, s, _ = pr3.shape
    tt = PRE_TT

    def body(pr_ref, edge_ref, c0, c1, c2, c3, c4, c5, mix_ref, w0_ref, wup_ref, a0_ref, aup_ref, kk_ref, ka_ref,
             dprs_ref, dmix_ref, dw0_ref, dwup_ref, da0_ref, daup_ref, dkk_ref, dka_ref):
        pr = pr_ref[0]
        prev = _shifted(pr_ref, edge_ref, pl.program_id(1) == 0, True)
        prs = pr + (prev - pr) * mix_ref[...]
        _, vjp = jax.vjp(_pre_fn, *_pre_pieces(prs), w0_ref[...], wup_ref[...].astype(F32), a0_ref[...], aup_ref[...].astype(F32),
                         kk_ref[...], ka_ref[...])
        grads = vjp(tuple(c[0] for c in (c0, c1, c2, c3, c4, c5)))
        for piece, a, b in zip(grads[:5], PRE_SPLIT[:-1], PRE_SPLIT[1:]):
            dprs_ref[0, :, a:b] = piece
        dw0, dwup, da0, daup, dkk, dka = grads[5:]
        dprs = dprs_ref[0]
        grads = (jnp.sum(dprs * (prev - pr), axis=0, keepdims=True), dw0, dwup, da0, daup, dkk, dka)
        refs = (dmix_ref, dw0_ref, dwup_ref, da0_ref, daup_ref, dkk_ref, dka_ref)
        first = jnp.logical_and(pl.program_id(0) == 0, pl.program_id(1) == 0)

        @pl.when(first)
        def _():
            for r_, g_ in zip(refs, grads):
                r_[...] = g_

        @pl.when(jnp.logical_not(first))
        def _():
            for r_, g_ in zip(refs, grads):
                r_[...] += g_

    vec = lambda n: pl.BlockSpec((1, n), lambda b, i: (0, 0))
    mat = pl.BlockSpec((LORA, WIDTH), lambda b, i: (0, 0))
    tile = pl.BlockSpec((1, tt, WIDTH), lambda b, i: (b, i, 0))
    in_specs = [pl.BlockSpec((1, tt, PR_COLS), lambda b, i: (b, i, 0)),
                pl.BlockSpec((1, 8, PR_COLS), lambda b, i: (b, jnp.maximum(i * (tt // 8) - 1, 0), 0))]
    in_specs += [tile] * 6 + [vec(PR_COLS), vec(WIDTH), mat, vec(WIDTH), mat, vec(WIDTH), vec(WIDTH)]
    out_specs = [pl.BlockSpec((1, tt, PR_COLS), lambda b, i: (b, i, 0)), vec(PR_COLS), vec(WIDTH), mat, vec(WIDTH), mat,
                 vec(WIDTH), vec(WIDTH)]
    out_shape = [SDS((bsz, s, PR_COLS), F32), SDS((1, PR_COLS), F32), SDS((1, WIDTH), F32), SDS((LORA, WIDTH), F32),
                 SDS((1, WIDTH), F32), SDS((LORA, WIDTH), F32), SDS((1, WIDTH), F32), SDS((1, WIDTH), F32)]
    return pl.pallas_call(
        body, name="rwkv_pre_bwd", grid=(bsz, s // tt), in_specs=in_specs, out_specs=out_specs, out_shape=out_shape,
        compiler_params=_params(("arbitrary", "arbitrary")))(pr3, pr3, *cots, mix, w0, wup, a0, aup, kk_, ka_)


def _shift_bwd(dprs3, mix):
    bsz, s, _ = dprs3.shape
    tt = PRE_TT
    nt = s // tt

    def body(d_ref, edge_ref, mix_ref, o_ref):
        nxt = _shifted(d_ref, edge_ref, pl.program_id(1) == nt - 1, False)
        m = mix_ref[...]
        o_ref[0] = (d_ref[0] * (1.0 - m) + nxt * m).astype(BF16)

    in_specs = [pl.BlockSpec((1, tt, PR_COLS), lambda b, i: (b, i, 0)),
                pl.BlockSpec((1, 8, PR_COLS), lambda b, i: (b, jnp.minimum((i + 1) * (tt // 8), s // 8 - 1), 0)),
                pl.BlockSpec((1, PR_COLS), lambda b, i: (0, 0))]
    return pl.pallas_call(
        body, name="shift_bwd", grid=(bsz, nt), in_specs=in_specs, out_specs=pl.BlockSpec((1, tt, PR_COLS), lambda b, i: (b, i, 0)),
        out_shape=SDS((bsz, s, PR_COLS), BF16), compiler_params=_params(("parallel", "parallel")))(dprs3, dprs3, mix)


_NN, _NT, _TN = ((2,), (1,)), ((2,), (2,)), ((1,), (1,))


def _dot3_bf16(a, b, dims):
    return lax.dot_general(a.astype(BF16), b.astype(BF16), (dims, ((0,), (0,))), preferred_element_type=F32)


class _Dots:
    def __init__(self, fwd):
        def make(dims, da_rule, db_rule):
            @jax.custom_vjp
            def f(a, b):
                return fwd(a, b, dims)

            f.defvjp(lambda a, b: (f(a, b), (a, b)), lambda res, g: (da_rule(*res, g), db_rule(*res, g)))
            return f

        one = _dot3_bf16
        self.mm = make(_NN, lambda a, b, g: one(g, b, _NT), lambda a, b, g: one(a, g, _TN))
        self.mm_nt = make(_NT, lambda a, b, g: one(g, b, _NN), lambda a, b, g: one(g, a, _TN))
        self.mm_tn = make(_TN, lambda a, b, g: one(b, g, _NT), lambda a, b, g: one(a, g, _NN))

        def powers(aab):
            ps = [aab]
            while 2 ** len(ps) < aab.shape[1]:
                ps.append(fwd(ps[-1], ps[-1], _NN))
            return ps

        def apply(ps, z, dims):
            for p in ps:
                z = z + fwd(p, z, dims)
            return z

        @jax.custom_vjp
        def solve(aab, z):
            return apply(powers(aab), z, _NN)

        def solve_fwd(aab, z):
            ps = powers(aab)
            x = apply(ps, z, _NN)
            return x, (ps, x)

        def solve_bwd(res, g):
            ps, x = res
            dz = apply(ps, g, _TN)
            return fwd(dz, x, _NT), dz

        solve.defvjp(solve_fwd, solve_bwd)
        self.solve = solve


_ONE_PASS = _Dots(_dot3_bf16)
_bmm, _bmm_tn = _ONE_PASS.mm, _ONE_PASS.mm_tn


def _chunk_fn(s0t, r, lw, k, v, kkraw, a, rk, lnw, lnb, first=False, d=_ONE_PASS):
    c = r.shape[1]
    at, rt, btc, ktc, gc, aab, arb, xv, arkv, ain, bin_ = _chunk_core(r, lw, k, v, kkraw, a, d)
    rs = d.mm(jnp.concatenate([at, rt], axis=1), s0t)
    u = d.solve(aab, rs[:, :c] + xv)
    y = rs[:, c:] + d.mm(arb, u) + arkv
    if first:
        y = _with_early_rows(y, r, lw, k, v, ain, bin_)
    gcol = jnp.sum(_diag(gc), axis=2, keepdims=True)
    sct = gcol * s0t + d.mm_tn(jnp.concatenate([btc, ktc], axis=1), jnp.concatenate([u, v], axis=1))
    return _post(y, r, k, v, rk, lnw, lnb), sct


def _diag(gc):
    return jnp.where(_masks(HEAD)[2], gc, 0.0)


def _with_early_rows(y, r, lw, k, v, ain, bin_):
    early = _early_rows(r[:2], lw[:2], k[:2], v[:2], ain[:2], bin_[:2])
    return jnp.concatenate([jnp.concatenate([early, y[:2, EARLY:]], axis=1), y[2:]], axis=0)


def _early_rows(r, lw, k, v, ain, bin_):
    cols = lambda x: _stack([jnp.transpose(x[h]) for h in range(2)])
    wc, bc, kc = cols(jnp.exp(lw)), cols(bin_), cols(k)
    st = jnp.zeros((2, HEAD, HEAD), F32)
    rows = []
    for t in range(EARLY):
        sa = _ONE_PASS.mm(ain[:, t:t + 1], st)
        st = st * wc[:, :, t:t + 1] + bc[:, :, t:t + 1] * sa + kc[:, :, t:t + 1] * v[:, t:t + 1]
        rows.append(_ONE_PASS.mm(r[:, t:t + 1], st))
    return jnp.concatenate(rows, axis=1)


def _chunk_rows(c):
    return pl.ds(c * CHUNK, CHUNK) if isinstance(c, int) else pl.ds(pl.multiple_of(c * CHUNK, CHUNK), CHUNK)


def _stack(xs):
    return jnp.concatenate([x[None] for x in xs], axis=0)


def _pairs(ref, chunks):
    tiles = [ref[0, _chunk_rows(c), :] for c in chunks]
    return _stack([t[:, HEAD * h:HEAD * h + HEAD] for t in tiles for h in range(2)])


def _unpair(vals, j):
    return jnp.concatenate([vals[2 * j], vals[2 * j + 1]], axis=1)


def _masks(c):
    ii = lax.broadcasted_iota(jnp.int32, (c, c), 0)
    jj = lax.broadcasted_iota(jnp.int32, (c, c), 1)
    return ii > jj, ii >= jj, ii == jj


@jax.custom_vjp
def _running_sum(lw):
    return _tri_dot(lw, _NN)


def _tri_dot(x, dims):
    g_, c, _ = x.shape
    tri = jnp.broadcast_to(_masks(c)[1].astype(BF16), (g_, c, c))
    head = x.astype(BF16)
    rest = (x - head.astype(F32)).astype(BF16)
    return lax.dot_general(tri, head, (dims, ((0,), (0,))), preferred_element_type=F32) + \
        lax.dot_general(tri, rest, (dims, ((0,), (0,))), preferred_element_type=F32)


_running_sum.defvjp(lambda lw: (_running_sum(lw), None), lambda _, ct: (_tri_dot(ct, _TN),))


def _chunk_core(r, lw, k, v, kkraw, a, d=_ONE_PASS):
    g_, c = r.shape[0], r.shape[1]
    nrm = jnp.sqrt(jnp.sum(kkraw * kkraw, axis=-1, keepdims=True))
    kkn = kkraw / jnp.maximum(nrm, 1e-12)
    ain, bin_ = -kkn, kkn * a
    strict, incl, _ = _masks(c)
    lg = _running_sum(lw)
    g, gp, gi = jnp.exp(lg), jnp.exp(lg - lw), jnp.exp(-lg)
    at, rt, bt, kt = ain * gp, r * g, bin_ * gi, k * gi
    aa = d.mm_nt(jnp.concatenate([at, rt], axis=1), jnp.concatenate([bt, kt], axis=1))
    aab = jnp.where(strict, aa[:, :c, :c], 0.0)
    aak = jnp.where(strict, aa[:, :c, c:], 0.0)
    arb = jnp.where(incl, aa[:, c:, :c], 0.0)
    ark = jnp.where(incl, aa[:, c:, c:], 0.0)
    akv = d.mm(jnp.concatenate([aak, ark], axis=1), v)
    gc = g[:, c - 1:c, :]
    return at, rt, bt * gc, kt * gc, gc, aab, arb, akv[:, :c], akv[:, c:], ain, bin_


def _lane_sum(x):
    return jnp.sum(x, axis=-1, keepdims=True)


def _lane_sum_mxu(x):
    g, c, n = x.shape
    x2 = x.reshape(g * c, n)
    head = x2.astype(BF16)
    rest = (x2 - head.astype(F32)).astype(BF16)
    ones = jnp.ones((n, n), BF16)
    return (_dot(head, ones) + _dot(rest, ones)).reshape(g, c, n)


def _post(y, r, k, v, rk, lnw, lnb, lane_sum=_lane_sum):
    mu = lane_sum(y) * (1.0 / HEAD)
    var = lane_sum(jnp.square(y - mu)) * (1.0 / HEAD)
    yn = (y - mu) * lax.rsqrt(var + GN_EPS) * lnw + lnb
    return yn + lane_sum(r * k * rk) * v


def _chunk_consts(r, lw, k, v, kkraw, a, first=False):
    d = _ONE_PASS
    at, rt, btc, ktc, gc, aab, arb, xv, arkv, ain, bin_ = _chunk_core(r, lw, k, v, kkraw, a, d)
    z = d.solve(aab, jnp.concatenate([at, xv], axis=2))
    ryv = jnp.concatenate([rt, arkv], axis=2) + d.mm(arb, z)
    if first:
        ryv = jnp.concatenate([ryv[:, :, :HEAD], _with_early_rows(ryv[:, :, HEAD:], r, lw, k, v, ain, bin_)], axis=2)
    mkv = d.mm_tn(btc, z) + jnp.concatenate([_diag(gc), d.mm_tn(ktc, v)], axis=2)
    return mkv, ryv


def _rwkv_scan(ins, rk, lnw, lnb):
    bsz, s, _ = ins[0].shape
    nch = s // CHUNK

    def consts_body(r_ref, lw_ref, k_ref, v_ref, kk_ref, a_ref, mkv_ref, ry_ref, yv_ref):
        def group(i, carry):
            chunks = [i * CHUNK_GROUP + j for j in range(CHUNK_GROUP)]
            mkv, ryv = _chunk_consts(*[_pairs(ref, chunks) for ref in (r_ref, lw_ref, k_ref, v_ref, kk_ref, a_ref)],
                                     first=isinstance(i, int) and i == 0)
            for j, c in enumerate(chunks):
                for h in range(2):
                    mkv_ref[0, 0, c, h] = mkv[2 * j + h]
                ry_ref[0, _chunk_rows(c), :] = jnp.concatenate([ryv[2 * j][:, :HEAD], ryv[2 * j + 1][:, :HEAD]], axis=1)
                yv_ref[0, _chunk_rows(c), :] = jnp.concatenate([ryv[2 * j][:, HEAD:], ryv[2 * j + 1][:, HEAD:]], axis=1)
            return carry

        group(0, 0)
        lax.fori_loop(1, nch // CHUNK_GROUP, group, 0)

    tile = pl.BlockSpec((1, s, LANE), lambda b, hp: (b, 0, hp))
    vec = pl.BlockSpec((1, LANE), lambda b, hp: (0, hp))
    mkv_spec = pl.BlockSpec((1, 1, nch, 2, HEAD, LANE), lambda b, hp: (b, hp, 0, 0, 0, 0))
    st_spec = pl.BlockSpec((1, 1, nch, 2, HEAD, HEAD), lambda b, hp: (b, hp, 0, 0, 0, 0))
    mkv, ry, yv = pl.pallas_call(
        consts_body, name="rwkv_consts", grid=(bsz, 4), in_specs=[tile] * 6, out_specs=[mkv_spec, tile, tile],
        out_shape=[SDS((bsz, 4, nch, 2, HEAD, LANE), F32), SDS((bsz, s, WIDTH), F32), SDS((bsz, s, WIDTH), F32)],
        compiler_params=_params(("parallel", "parallel")))(*ins)

    states = _chunk_recurrence(mkv, None, "rwkv_states")

    def out_body(ry_ref, yv_ref, r_ref, k_ref, v_ref, st_ref, rk_ref, lnw_ref, lnb_ref, o_ref):
        y, r, k, v, rk_, lnw_, lnb_ = _scan_rows(ry_ref, yv_ref, r_ref, k_ref, v_ref, st_ref, rk_ref, lnw_ref, lnb_ref)
        o = _post(y, r, k, v, rk_, lnw_, lnb_, _lane_sum_mxu)
        for j in range(CHUNK_GROUP):
            o_ref[0, _chunk_rows(j), :] = _unpair(o, j)

    o = pl.pallas_call(
        out_body, name="rwkv_out", grid=(bsz, 4, nch // CHUNK_GROUP), in_specs=_group_specs(5), out_specs=_group_specs(1)[0],
        out_shape=SDS((bsz, s, WIDTH), F32),
        compiler_params=_params(("parallel", "parallel", "parallel")))(ry, yv, ins[0], ins[2], ins[3], states, rk, lnw, lnb)
    return o, states, (mkv, ry, yv)


def _group_specs(n_tiles):
    tile = pl.BlockSpec((1, CHUNK_GROUP * CHUNK, LANE), lambda b, hp, t: (b, t, hp))
    if n_tiles == 1:
        return [tile]
    st = pl.BlockSpec((1, 1, CHUNK_GROUP, 2, HEAD, HEAD), lambda b, hp, t: (b, hp, t, 0, 0, 0))
    vec = pl.BlockSpec((1, LANE), lambda b, hp, t: (0, hp))
    return [tile] * n_tiles + [st] + [vec] * 3


def _scan_rows(ry_ref, yv_ref, r_ref, k_ref, v_ref, st_ref, rk_ref, lnw_ref, lnb_ref):
    chunks = list(range(CHUNK_GROUP))
    ry, yv, r, k, v = (_pairs(ref, chunks) for ref in (ry_ref, yv_ref, r_ref, k_ref, v_ref))
    st = _stack([st_ref[0, 0, c, h] for c in chunks for h in range(2)])
    vecs = [_stack([ref[:, HEAD * h:HEAD * h + HEAD] for _ in chunks for h in range(2)]) for ref in (rk_ref, lnw_ref, lnb_ref)]
    return (_bmm(ry, st) + yv, r, k, v, *vecs)


def _chunk_recurrence(mkv, q, name):
    bsz, _, nch = mkv.shape[:3]
    pairs = [(hp, h) for hp in range(4) for h in range(2)]

    def body(*refs):
        mkv_ref, out_ref, acc = refs[0], refs[-2], refs[-1]
        acc[...] = jnp.zeros_like(acc)

        def step(i, carry):
            c = i if q is None else nch - 1 - i
            cur = acc[...]
            for j, (hp, h) in enumerate(pairs):
                out_ref[0, hp, c, h] = cur[j]
            m = _stack([mkv_ref[0, hp, c, h] for hp, h in pairs])
            if q is None:
                acc[...] = _bmm(m[:, :, :HEAD], cur) + m[:, :, HEAD:]
            else:
                acc[...] = _bmm_tn(m[:, :, :HEAD], cur) + _stack([refs[1][0, hp, c, h] for hp, h in pairs])
            return carry

        lax.fori_loop(0, nch, step, 0)

    spec = lambda w: pl.BlockSpec((1, 4, nch, 2, HEAD, w), lambda b: (b, 0, 0, 0, 0, 0))
    return pl.pallas_call(
        body, name=name, grid=(bsz,), in_specs=[spec(LANE)] + ([] if q is None else [spec(HEAD)]), out_specs=spec(HEAD),
        out_shape=SDS((bsz, 4, nch, 2, HEAD, HEAD), F32), scratch_shapes=[pltpu.VMEM((8, HEAD, HEAD), F32)],
        compiler_params=_params(("parallel",)))(*([mkv] if q is None else [mkv, q]))


def _rwkv_scan_bwd(ins, states, consts, do3, rk, lnw, lnb):
    bsz, s, _ = ins[0].shape
    nch = s // CHUNK

    mkv, ry, yv = consts

    def q_body(do_ref, ry_ref, yv_ref, r_ref, k_ref, v_ref, st_ref, rk_ref, lnw_ref, lnb_ref, q_ref):
        y, r, k, v, rk_, lnw_, lnb_ = _scan_rows(ry_ref, yv_ref, r_ref, k_ref, v_ref, st_ref, rk_ref, lnw_ref, lnb_ref)
        _, vjp = jax.vjp(lambda y_: _post(y_, r, k, v, rk_, lnw_, lnb_), y)
        (dy,) = vjp(_pairs(do_ref, list(range(CHUNK_GROUP))))
        q = _bmm_tn(_pairs(ry_ref, list(range(CHUNK_GROUP))), dy)
        for j in range(CHUNK_GROUP):
            for h in range(2):
                q_ref[0, 0, j, h] = q[2 * j + h]

    specs = _group_specs(6)
    q = pl.pallas_call(
        q_body, name="rwkv_q", grid=(bsz, 4, nch // CHUNK_GROUP), in_specs=specs, out_specs=specs[6],
        out_shape=SDS((bsz, 4, nch, 2, HEAD, HEAD), F32),
        compiler_params=_params(("parallel", "parallel", "parallel")))(do3, ry, yv, ins[0], ins[2], ins[3], states, rk, lnw, lnb)

    dstates = _chunk_recurrence(mkv, q, "rwkv_dstates")

    def body(r_ref, lw_ref, k_ref, v_ref, kk_ref, a_ref, st_ref, dst_ref, do_ref, rk_ref, lnw_ref, lnb_ref,
             dr_ref, dlw_ref, dk_ref, dv_ref, dkk_ref, da_ref, drk_ref, dlnw_ref, dlnb_ref):
        chunks = list(range(BWD_GROUP))
        par_refs = (drk_ref, dlnw_ref, dlnb_ref)

        @pl.when(jnp.logical_and(pl.program_id(1) == 0, pl.program_id(2) == 0))
        def _():
            for ref in par_refs:
                ref[...] = jnp.zeros_like(ref)

        def group(first):
            per_pair = lambda ref: _stack([ref[0, 0, c, h] for c in chunks for h in range(2)])
            vecs = [_stack([ref[:, HEAD * h:HEAD * h + HEAD] for _ in chunks for h in range(2)]) for ref in (rk_ref, lnw_ref, lnb_ref)]
            _, vjp = jax.vjp(functools.partial(_chunk_fn, first=first, d=_ONE_PASS), per_pair(st_ref),
                             *[_pairs(ref, chunks) for ref in (r_ref, lw_ref, k_ref, v_ref, kk_ref, a_ref)], *vecs)
            grads = vjp((_pairs(do_ref, chunks), per_pair(dst_ref)))
            for ref, cot in zip((dr_ref, dlw_ref, dk_ref, dv_ref, dkk_ref, da_ref), grads[1:7]):
                for j, c in enumerate(chunks):
                    ref[0, _chunk_rows(c), :] = _unpair(cot, j)
            for ref, g_ in zip(par_refs, grads[7:10]):
                ref[...] += jnp.concatenate([sum(g_[2 * j + h] for j in range(BWD_GROUP)) for h in range(2)], axis=1)

        pl.when(pl.program_id(2) == 0)(functools.partial(group, True))
        pl.when(pl.program_id(2) != 0)(functools.partial(group, False))

    tt = BWD_GROUP * CHUNK
    tile = pl.BlockSpec((1, tt, LANE), lambda hp, b, t: (b, t, hp))
    vec = pl.BlockSpec((1, LANE), lambda hp, b, t: (0, hp))
    st_spec = pl.BlockSpec((1, 1, BWD_GROUP, 2, HEAD, HEAD), lambda hp, b, t: (b, hp, t, 0, 0, 0))
    outs = pl.pallas_call(
        body, name="rwkv_scan_bwd", grid=(4, bsz, s // tt), in_specs=[tile] * 6 + [st_spec, st_spec, tile] + [vec] * 3,
        out_specs=[tile] * 6 + [vec] * 3,
        out_shape=[SDS((bsz, s, WIDTH), F32)] * 6 + [SDS((1, WIDTH), F32)] * 3,
        compiler_params=_params(("parallel", "arbitrary", "arbitrary")))(*ins, states, dstates, do3, rk, lnw, lnb)
    return outs[:6], outs[6:]


def _head(o_attn, o_rwkv, z_attn, z_rwkv, gm, x2, tgt, wua, wur, wout, g2):
    n = x2.shape[0]
    tm = 256
    nt = n // tm
    d = D_MODEL

    def body(oa_ref, or_ref, za_ref, zr_ref, gm_ref, x_ref, t_ref, wua_ref, wur_ref, wout_ref, g2_ref,
             dxo_ref, doa_ref, dor_ref, dza_ref, dzr_ref, dgm_ref, dwua_ref, dwur_ref, dwout_ref, dg2_ref, loss_ref, lacc):
        i = pl.program_id(0)
        oa, orw, za, zr = oa_ref[...], or_ref[...], za_ref[...], zr_ref[...]
        ga, gb = gm_ref[:, 0:d], gm_ref[:, d:2 * d]
        am = (oa * _silu(za)).astype(BF16)
        bm = (orw * _silu(zr)).astype(BF16)
        ya, yb = _dot(am, wua_ref[...]), _dot(bm, wur_ref[...])
        sa, sb = jax.nn.sigmoid(ga), jax.nn.sigmoid(gb)
        merged = (sa * ya + sb * yb).astype(BF16)
        out = _dot(merged, wout_ref[...])
        rs = lax.rsqrt(jnp.mean(out * out, axis=-1, keepdims=True) + RMS_EPS)
        g2 = g2_ref[...]
        err = x_ref[...] + out * rs * g2 - t_ref[...]
        lpart = jnp.sum(err * err, axis=0, keepdims=True)
        dxo = err * (1.0 / d)
        dxo_ref[...] = dxo
        dg2 = jnp.sum(dxo * out * rs, axis=0, keepdims=True)
        gd = dxo * g2
        dout = (rs * (gd - out * (rs * rs) * jnp.mean(gd * out, axis=-1, keepdims=True))).astype(BF16)
        dmerged = _dot_nt(dout, wout_ref[...])
        dwout = _dot_tn(merged, dout)
        dya, dyb = (dmerged * sa).astype(BF16), (dmerged * sb).astype(BF16)
        dgm_ref[:, 0:d] = (dmerged * ya * sa * (1.0 - sa)).astype(BF16)
        dgm_ref[:, d:2 * d] = (dmerged * yb * sb * (1.0 - sb)).astype(BF16)
        dam, dbm = _dot_nt(dya, wua_ref[...]), _dot_nt(dyb, wur_ref[...])
        dwua, dwur = _dot_tn(am, dya), _dot_tn(bm, dyb)
        doa_ref[...] = dam * _silu(za)
        dza_ref[...] = (dam * oa * _dsilu(za)).astype(BF16)
        dor_ref[...] = dbm * _silu(zr)
        dzr_ref[...] = (dbm * orw * _dsilu(zr)).astype(BF16)

        @pl.when(i == 0)
        def _():
            dwua_ref[...], dwur_ref[...], dwout_ref[...], dg2_ref[...], lacc[...] = dwua, dwur, dwout, dg2, lpart

        @pl.when(i != 0)
        def _():
            dwua_ref[...] += dwua
            dwur_ref[...] += dwur
            dwout_ref[...] += dwout
            dg2_ref[...] += dg2
            lacc[...] += lpart

        @pl.when(i == nt - 1)
        def _():
            loss_ref[...] = jnp.sum(lacc[...], axis=1, keepdims=True) * (0.5 / d)

    t512 = pl.BlockSpec((tm, WIDTH), lambda i: (i, 0))
    t1k = pl.BlockSpec((tm, d), lambda i: (i, 0))
    t2k = pl.BlockSpec((tm, 2 * d), lambda i: (i, 0))
    full = lambda r, c: pl.BlockSpec((r, c), lambda i: (0, 0))
    return pl.pallas_call(
        body, name="head_fwd_bwd", grid=(nt,),
        in_specs=[t512, t512, t512, t512, t2k, t1k, t1k, full(WIDTH, d), full(WIDTH, d), full(d, d), full(1, d)],
        out_specs=[t1k, t512, t512, t512, t512, t2k, full(WIDTH, d), full(WIDTH, d), full(d, d), full(1, d), full(1, 1)],
        out_shape=[SDS((n, d), F32), SDS((n, WIDTH), F32), SDS((n, WIDTH), F32), SDS((n, WIDTH), BF16), SDS((n, WIDTH), BF16),
                   SDS((n, 2 * d), BF16), SDS((WIDTH, d), F32), SDS((WIDTH, d), F32), SDS((d, d), F32), SDS((1, d), F32), SDS((1, 1), F32)],
        scratch_shapes=[pltpu.VMEM((1, d), F32)],
        compiler_params=_params(("arbitrary",)))(o_attn, o_rwkv, z_attn, z_rwkv, gm, x2, tgt, wua, wur, wout, g2)


def _mesh_pos():
    x, y, c = lax.axis_index("x"), lax.axis_index("y"), lax.axis_index("c")
    return 4 * x + 2 * y + c


def _coords(idx):
    return (idx // 4, (idx // 2) % 2, idx % 2)


def _exchange(srcs, to_all, name):
    n = len(srcs)

    def body(*refs):
        src_refs, dst_refs = refs[:n], refs[n:2 * n]
        send_sems, recv_sems, local_sems = refs[2 * n:]
        me = _mesh_pos()

        def piece(i, j):
            return src_refs[i] if to_all[i] else src_refs[i].at[j]

        def remote(i, off, peer, block, slot):
            return pltpu.make_async_remote_copy(src_ref=piece(i, block), dst_ref=dst_refs[i].at[slot],
                                                send_sem=send_sems.at[i, off - 1], recv_sem=recv_sems.at[i, off - 1],
                                                device_id=_coords(peer), device_id_type=MESH)

        local = [pltpu.make_async_copy(piece(i, me), dst_refs[i].at[me], local_sems.at[i]) for i in range(n)]
        for cp in local:
            cp.start()
        sends = []
        for off in range(1, N_DEV):
            to = (me + off) % N_DEV
            for i in range(n):
                sends.append(remote(i, off, to, to, me))
                sends[-1].start()
        for off in range(1, N_DEV):
            frm = (me + N_DEV - off) % N_DEV
            for i in range(n):
                remote(i, off, frm, me, frm).wait_recv()
        for cp in sends:
            cp.wait_send()
        for cp in local:
            cp.wait()

    outs = pl.pallas_call(
        body, name=name, in_specs=[pl.BlockSpec(memory_space=pltpu.HBM)] * n, out_specs=[pl.BlockSpec(memory_space=pltpu.HBM)] * n,
        out_shape=[SDS((N_DEV,) + s.shape[-2:], s.dtype) for s in srcs],
        scratch_shapes=[pltpu.SemaphoreType.DMA((n, N_DEV - 1)), pltpu.SemaphoreType.DMA((n, N_DEV - 1)), pltpu.SemaphoreType.DMA((n,))],
        compiler_params=pltpu.CompilerParams())(*srcs)
    return outs


_HBM = pl.BlockSpec(memory_space=pltpu.HBM)
_SEM = pl.BlockSpec(memory_space=pltpu.SEMAPHORE)
_EFFECT = pltpu.SideEffectType.DATAFLOW_SIDE_EFFECTING


def _send_copy(src_ref, land_ref, to_all, send_sems, recv_sems, i, off, block, slot, peer):
    k = i * (N_DEV - 1) + off - 1
    return pltpu.make_async_remote_copy(src_ref=src_ref if to_all else src_ref.at[block], dst_ref=land_ref.at[slot],
                                        send_sem=send_sems.at[k], recv_sem=recv_sems.at[k],
                                        device_id=_coords(peer), device_id_type=MESH)


def _send_start(srcs, to_all, name):
    n = len(srcs)

    def body(*refs):
        src_refs, land_refs = refs[:n], refs[n:2 * n]
        send_sems, recv_sems = refs[2 * n:2 * n + 2]
        me = _mesh_pos()
        for off in range(1, N_DEV):
            to = (me + off) % N_DEV
            for i in range(n):
                _send_copy(src_refs[i], land_refs[i], to_all, send_sems, recv_sems, i, off, to, me, to).start()
        refs[-1][...] = jnp.zeros_like(refs[-1])

    lands = [jnp.zeros((N_DEV,) + s.shape[-2:], s.dtype) for s in srcs]
    hbm = [pltpu.HBM(a.shape, a.dtype) for a in list(srcs) + lands]
    sems = pltpu.SemaphoreType.DMA((n * (N_DEV - 1),))
    outs = pl.pallas_call(
        body, name=name, out_shape=(sems, sems, *hbm, SDS((8, LANE), BF16)),
        in_specs=(_HBM,) * (2 * n), out_specs=(_SEM, _SEM) + (_HBM,) * (2 * n) + (pl.BlockSpec(memory_space=pltpu.VMEM),),
        input_output_aliases={i: 2 + i for i in range(2 * n)}, compiler_params=pltpu.CompilerParams(has_side_effects=_EFFECT),
    )(*[pltpu.with_memory_space_constraint(a, pltpu.HBM) for a in list(srcs) + lands])
    return outs[0], outs[1], outs[2:2 + n], outs[2 + n:2 + 2 * n], outs[-1]


def _send_wait(send_sems, recv_sems, srcs_thru, lands_thru, to_all, after, name):
    n = len(srcs_thru)

    def body(*refs):
        src_refs, land_refs = refs[:n], refs[n:2 * n]
        send_sems, recv_sems = refs[2 * n:2 * n + 2]
        me = _mesh_pos()
        for off in range(1, N_DEV):
            to, frm = (me + off) % N_DEV, (me + N_DEV - off) % N_DEV
            for i in range(n):
                _send_copy(src_refs[i], land_refs[i], to_all, send_sems, recv_sems, i, off, to, me, to).wait_send()
                _send_copy(src_refs[i], land_refs[i], to_all, send_sems, recv_sems, i, off, me, frm, frm).wait_recv()

    hbm = tuple(pltpu.HBM(a.shape, a.dtype) for a in list(srcs_thru) + list(lands_thru))
    outs = pl.pallas_call(
        body, name=name, out_shape=hbm, in_specs=(_HBM,) * (2 * n) + (_SEM, _SEM, pl.BlockSpec(memory_space=pl.ANY)),
        out_specs=(_HBM,) * (2 * n), input_output_aliases={i: i for i in range(2 * n)},
        compiler_params=pltpu.CompilerParams(has_side_effects=_EFFECT),
    )(*srcs_thru, *lands_thru, send_sems, recv_sems, after)
    return outs[n:]


def _gather(srcs, after, name):
    n = len(srcs)

    def body(*refs):
        src_refs, dst_refs = refs[:n], refs[n + 1:2 * n + 1]
        send_sems, recv_sems, local_sems = refs[2 * n + 1:]
        x, y, c = lax.axis_index("x"), lax.axis_index("y"), lax.axis_index("c")
        me, sibling = (x, y, c), (x, y, 1 - c)
        chips = [(1 - x, y), (x, 1 - y), (1 - x, 1 - y)]

        def slot(i, dev):
            return dst_refs[i].at[4 * dev[0] + 2 * dev[1] + dev[2]]

        def copy(i, k, block, to, own=False):
            return pltpu.make_async_remote_copy(src_ref=src_refs[i] if own else slot(i, block), dst_ref=slot(i, block),
                                                send_sem=send_sems.at[i, k], recv_sem=recv_sems.at[i, k],
                                                device_id=to, device_id_type=MESH)

        local = [pltpu.make_async_copy(src_refs[i], slot(i, me), local_sems.at[i]) for i in range(n)]
        for cp in local:
            cp.start()
        sends = []
        for i in range(n):
            sends.append(copy(i, 0, me, sibling, own=True))
            sends += [copy(i, 1 + j, me, (*chip, c), own=True) for j, chip in enumerate(chips)]
        for cp in sends:
            cp.start()
        for j, chip in enumerate(chips):
            for i in range(n):
                copy(i, 1 + j, (*chip, c), me).wait_recv()
                sends.append(copy(i, 4 + j, (*chip, c), sibling))
                sends[-1].start()
        for i in range(n):
            copy(i, 0, sibling, me).wait_recv()
            for j, chip in enumerate(chips):
                copy(i, 4 + j, (*chip, 1 - c), me).wait_recv()
        for cp in sends:
            cp.wait_send()
        for cp in local:
            cp.wait()

    return pl.pallas_call(
        body, name=name, in_specs=[pl.BlockSpec(memory_space=pltpu.HBM)] * n + [pl.BlockSpec(memory_space=pl.ANY)],
        out_specs=[pl.BlockSpec(memory_space=pltpu.HBM)] * n, out_shape=[SDS((N_DEV,) + s.shape, s.dtype) for s in srcs],
        scratch_shapes=[pltpu.SemaphoreType.DMA((n, N_DEV - 1)), pltpu.SemaphoreType.DMA((n, N_DEV - 1)), pltpu.SemaphoreType.DMA((n,))],
        compiler_params=pltpu.CompilerParams())(*srcs, after)


def _adamw(parts, w, m, v, tr, name, own=None):
    rows, cols = w.shape
    c1, c2 = 1.0 - ADAM_B1 ** ADAM_STEP, 1.0 - ADAM_B2 ** ADAM_STEP

    def body(p_ref, *refs):
        w_ref, m_ref, v_ref, g_ref, d_ref, nm_ref, nv_ref = refs[-7:]
        me = _mesh_pos()

        def part(j):
            return p_ref[j] if own is None else jnp.where(me == j, refs[0][...], p_ref[j])

        g = part(0).astype(F32)
        for j in range(1, N_DEV):
            g = g + part(j).astype(F32)
        nm = ADAM_B1 * m_ref[...] + (1.0 - ADAM_B1) * g
        nv = ADAM_B2 * v_ref[...] + (1.0 - ADAM_B2) * jnp.square(g)
        g_ref[...] = g
        nm_ref[...] = nm
        nv_ref[...] = nv
        d_ref[...] = -ADAM_LR * ((nm / c1) / (jnp.sqrt(nv / c2) + ADAM_EPS) + ADAM_WD * w_ref[...])

    t = pl.BlockSpec((tr, cols), lambda i: (i, 0))
    extra = [] if own is None else [own]
    return pl.pallas_call(
        body, name=name, grid=(rows // tr,), in_specs=[pl.BlockSpec((N_DEV, tr, cols), lambda i: (0, i, 0))] + [t] * (3 + len(extra)),
        out_specs=[t] * 4, out_shape=[SDS((rows, cols), F32)] * 4, compiler_params=_params(("parallel",)))(parts, *extra, w, m, v)


SHARDED = (("w_in", D_MODEL, IN_COLS // N_DEV, True, 128), ("w_up_attn", WIDTH, D_MODEL // N_DEV, True, WIDTH),
           ("w_up_rwkv", WIDTH, D_MODEL // N_DEV, True, WIDTH), ("w_out", D_MODEL // N_DEV, D_MODEL, False, D_MODEL // N_DEV),
           ("rwkv_w_up", LORA, WIDTH // N_DEV, True, LORA), ("rwkv_a_up", LORA, WIDTH // N_DEV, True, LORA))
LOSS_SLOT = sum(n for _, n in SMALL)


def _pack_small(small, extra=None):
    flat = [small[n].reshape(-1).astype(F32) for n, _ in SMALL]
    flat.append(jnp.zeros((1,), F32) if extra is None else extra.reshape(1))
    flat.append(jnp.zeros((SMALL_ROWS * LANE - LOSS_SLOT - 1,), F32))
    return jnp.concatenate(flat).reshape(SMALL_ROWS, LANE)


def _unpack_small(packed, shapes):
    flat = packed.reshape(-1)
    out, off = {}, 0
    for n, cnt in SMALL:
        out[n] = flat[off:off + cnt].reshape(shapes[n])
        off += cnt
    return out, flat[LOSS_SLOT]


def _whole(gathered, by_cols):
    if not by_cols:
        return gathered.reshape(-1, gathered.shape[-1])
    return gathered.transpose(1, 0, 2).reshape(gathered.shape[1], -1)


def _per_owner(full, by_cols):
    if not by_cols:
        return full.reshape(N_DEV, -1, full.shape[-1])
    return full.reshape(full.shape[0], N_DEV, -1).transpose(1, 0, 2)


def _local_step(x, loss_target, sm, wts):
    bsz, s, d = x.shape
    n = bsz * s
    x2, tgt = x.reshape(n, d), loss_target.reshape(n, d)
    bidx = jnp.asarray(_bucket_tables())
    w_in = wts["w_in"]
    segs = (("qkv", 0, QKV_COLS, 1536), ("za", OFF_ZA, WIDTH, 512), ("pr", OFF_PR, PR_COLS, PR_COLS), ("zr", OFF_ZR, WIDTH, 512),
            ("gm", OFF_GM, 2 * D_MODEL, 1024))

    h, rs = _prenorm(x2, sm["pre_norm_gain"])
    w_seg = {nm: w_in[:, off:off + cnt] for nm, off, cnt, _ in segs}
    proj = {nm: _mm(h, w_seg[nm], tn, "proj_" + nm) for nm, _, _, tn in segs}
    qkv3 = proj["qkv"].reshape(bsz, s, QKV_COLS)
    pr3 = proj["pr"].reshape(bsz, s, PR_COLS)

    o_attn, lse = _attn_fwd(qkv3, sm["rel_bias"], bidx)
    rk = sm["rwkv_r_k"].reshape(1, WIDTH)
    pre_args = (sm["rwkv_shift_mix"], sm["rwkv_w0"], wts["rwkv_w_up"], sm["rwkv_a0"], wts["rwkv_a_up"], sm["rwkv_k_k"], sm["rwkv_k_a"])
    scan_in = _rwkv_pre(pr3, *pre_args)
    o_rwkv, states, consts = _rwkv_scan(scan_in, rk, sm["rwkv_ln_w"], sm["rwkv_ln_b"])

    (dxo, do_attn, do_rwkv, dza, dzr, dgm, g_wua, g_wur, g_wout, g_post, loss) = _head(
        o_attn.reshape(n, WIDTH), o_rwkv.reshape(n, WIDTH), proj["za"], proj["zr"], proj["gm"], x2, tgt,
        wts["w_up_attn"], wts["w_up_rwkv"], wts["w_out"], sm["post_norm_gain"])

    dqkv, dbias = _attn_bwd(qkv3, o_attn, lse, do_attn.reshape(bsz, s, WIDTH), sm["rel_bias"], bidx)
    g_bias = _bias_grad(dbias, bidx)[:, :N_BUCKET].T

    scan_cots, (g_rk, g_lnw, g_lnb) = _rwkv_scan_bwd(scan_in, states, consts, do_rwkv.reshape(bsz, s, WIDTH), rk, sm["rwkv_ln_w"],
                                                     sm["rwkv_ln_b"])
    dprs, g_mix, g_w0, g_wup, g_a0, g_aup, g_kk, g_ka = _rwkv_pre_bwd(pr3, scan_cots, *pre_args)
    dpr = _shift_bwd(dprs, sm["rwkv_shift_mix"]).reshape(n, PR_COLS)

    dsegs = [(dqkv.reshape(9, n, WIDTH), 0, QKV_COLS, WIDTH), (dza, OFF_ZA, WIDTH, WIDTH), (dpr, OFF_PR, PR_COLS, PR_COLS),
             (dzr, OFF_ZR, WIDTH, WIDTH), (dgm, OFF_GM, 2 * D_MODEL, D_MODEL)]
    full = {"w_in": jnp.concatenate([_mm_tn(h, t, tn, "gw_in_%d" % j) for j, (t, _, _, tn) in enumerate(dsegs)], axis=1),
            "w_up_attn": g_wua, "w_up_rwkv": g_wur, "w_out": g_wout, "rwkv_w_up": g_wup, "rwkv_a_up": g_aup}
    blocks = [_per_owner(full[nm], by_cols).astype(BF16) for nm, _, _, by_cols, _ in SHARDED]
    me = 4 * lax.axis_index("x") + 2 * lax.axis_index("y") + lax.axis_index("c")
    own = [lax.dynamic_index_in_dim(b, me, 0, keepdims=False) for b in blocks]
    send_sems, recv_sems, blocks_thru, lands_thru, token = _send_start(blocks, False, "grads_start")
    dh = _mm_nt(dsegs[0][0], w_seg["qkv"], token, "dh_qkv")
    grad_x, g_pre = _dh_rest_prenorm_bwd([t for t, *_ in dsegs[1:]], [w_seg[nm] for nm in ("za", "pr", "zr", "gm")], dh, x2, rs,
                                         sm["pre_norm_gain"], dxo)
    landed = _send_wait(send_sems, recv_sems, blocks_thru, lands_thru, False, g_pre, "grads_wait")

    small = {"pre_norm_gain": g_pre, "rel_bias": g_bias, "rwkv_shift_mix": g_mix, "rwkv_w0": g_w0, "rwkv_a0": g_a0, "rwkv_k_k": g_kk,
             "rwkv_k_a": g_ka, "rwkv_r_k": g_rk, "rwkv_ln_w": g_lnw, "rwkv_ln_b": g_lnb, "post_norm_gain": g_post}
    return loss[0, 0], grad_x.reshape(bsz, s, d), (landed, own), small


def kernel(x, pre_norm_gain, w_in, rel_bias, rwkv_shift_mix, rwkv_w0, rwkv_w_up, rwkv_a0, rwkv_a_up, rwkv_k_k, rwkv_k_a, rwkv_r_k, rwkv_ln_w, rwkv_ln_b, w_up_attn, w_up_rwkv, w_out, post_norm_gain, loss_target, m_pre_norm_gain, m_w_in, m_rel_bias, m_rwkv_shift_mix, m_rwkv_w0, m_rwkv_w_up, m_rwkv_a0, m_rwkv_a_up, m_rwkv_k_k, m_rwkv_k_a, m_rwkv_r_k, m_rwkv_ln_w, m_rwkv_ln_b, m_w_up_attn, m_w_up_rwkv, m_w_out, m_post_norm_gain, v_pre_norm_gain, v_w_in, v_rel_bias, v_rwkv_shift_mix, v_rwkv_w0, v_rwkv_w_up, v_rwkv_a0, v_rwkv_a_up, v_rwkv_k_k, v_rwkv_k_a, v_rwkv_r_k, v_rwkv_ln_w, v_rwkv_ln_b, v_w_up_attn, v_w_up_rwkv, v_w_out, v_post_norm_gain):
    names = [n for n, *_ in SHARDED] + [n for n, _ in SMALL]
    loc = dict(locals())
    w = {n: loc[n] for n in names}
    m = {n: loc["m_" + n] for n in names}
    v = {n: loc["v_" + n] for n in names}
    shapes = {n: w[n].shape for n in names}
    order = ["pre_norm_gain", "w_in", "rel_bias", "rwkv_shift_mix", "rwkv_w0", "rwkv_w_up", "rwkv_a0", "rwkv_a_up", "rwkv_k_k", "rwkv_k_a",
             "rwkv_r_k", "rwkv_ln_w", "rwkv_ln_b", "w_up_attn", "w_up_rwkv", "w_out", "post_norm_gain"]
    shard2d = lambda t, n, r, c: t[n].reshape(r, c)

    shards = [shard2d(w, n, r, c).astype(BF16) for n, r, c, _, _ in SHARDED]
    send_sems, recv_sems, srcs_thru, lands_thru, token = _send_start(shards[1:], True, "weights_start")
    gathered = list(_gather(shards[:1], token, "gather_weights"))
    landed = _send_wait(send_sems, recv_sems, srcs_thru, lands_thru, True, gathered[0], "weights_wait")
    me = 4 * lax.axis_index("x") + 2 * lax.axis_index("y") + lax.axis_index("c")
    gathered += [lax.dynamic_update_index_in_dim(g, sh, me, 0) for g, sh in zip(landed, shards[1:])]
    wts = {n: _whole(g, by_cols) for (n, _, _, by_cols, _), g in zip(SHARDED, gathered)}

    loss, grad_x, (landed, own), small = _local_step(x, loss_target, w, wts)
    (small_parts,) = _exchange([_pack_small(small, loss)], [True], "exchange_small")

    outs = [{}, {}, {}, {}]
    for (n, r, c, _, tr), p, o_ in zip(SHARDED, landed, own):
        res = _adamw(p, shard2d(w, n, r, c), shard2d(m, n, r, c), shard2d(v, n, r, c), tr, "adamw_" + n, own=o_)
        for o, t in zip(outs, res):
            o[n] = t.reshape(shapes[n])
    res = _adamw(small_parts, _pack_small(w), _pack_small(m), _pack_small(v), SMALL_ROWS, "adamw_small")
    for o, t in zip(outs, res):
        o.update(_unpack_small(t, shapes)[0])
    loss = _unpack_small(res[0], shapes)[1]
    return (loss, grad_x, *[o[n] for o in outs for n in order])
```

```python
import functools
import math

import numpy as np
import jax
import jax.numpy as jnp
from jax import lax
from jax.experimental import pallas as pl
from jax.experimental.pallas import tpu as pltpu

F32, BF16 = jnp.float32, jnp.bfloat16
SDS = jax.ShapeDtypeStruct
MESH = pl.DeviceIdType.MESH

N_DEV = 8
D_MODEL = 1024
HEAD = 64
N_HEAD = 8
WIDTH = N_HEAD * HEAD
DILATIONS = (1, 4, 16)
QB = 128
N_BUCKET = 32
MAX_DIST = 2048
LORA = 64
QKV_COLS = 9 * WIDTH
PR_COLS = 3 * WIDTH + 2 * LORA
IN_COLS = QKV_COLS + WIDTH + PR_COLS + WIDTH + 2 * D_MODEL
OFF_ZA, OFF_PR, OFF_ZR, OFF_GM = QKV_COLS, QKV_COLS + WIDTH, QKV_COLS + WIDTH + PR_COLS, QKV_COLS + 2 * WIDTH + PR_COLS
RMS_EPS = 1e-6
GN_EPS = 64e-5
SCALE = 1.0 / math.sqrt(HEAD)
CHUNK = 64
CHUNK_GROUP = 32
BWD_GROUP = 16
EARLY = 8
NEG = -1e30
LANE = 128

ADAM_LR, ADAM_B1, ADAM_B2, ADAM_EPS, ADAM_WD, ADAM_STEP = 0.001, 0.9, 0.999, 1e-08, 0.01, 10

VMEM_LIMIT = 56 * 1024 * 1024

SMALL = (("pre_norm_gain", 1024), ("rel_bias", 768), ("rwkv_shift_mix", 1664), ("rwkv_w0", 512), ("rwkv_a0", 512),
         ("rwkv_k_k", 512), ("rwkv_k_a", 512), ("rwkv_r_k", 512), ("rwkv_ln_w", 512), ("rwkv_ln_b", 512),
         ("post_norm_gain", 1024))
SMALL_ROWS = 64


def _params(sem=None):
    return pltpu.CompilerParams(dimension_semantics=sem, vmem_limit_bytes=VMEM_LIMIT)


def _dot(a, b):
    return jnp.dot(a, b, preferred_element_type=F32)


def _dot_nt(a, b):
    return lax.dot_general(a, b, (((1,), (1,)), ((), ())), preferred_element_type=F32)


def _dot_tn(a, b):
    return lax.dot_general(a, b, (((0,), (0,)), ((), ())), preferred_element_type=F32)


@jax.custom_vjp
def _bdot(a, b):
    return _dot(a.astype(BF16), b.astype(BF16))


def _bdot_fwd(a, b):
    return _bdot(a, b), (a, b)


def _bdot_bwd(res, g):
    a, b = res
    gb = g.astype(BF16)
    return _dot_nt(gb, b.astype(BF16)), _dot_tn(a.astype(BF16), gb)


_bdot.defvjp(_bdot_fwd, _bdot_bwd)


def _silu(z):
    return z * jax.nn.sigmoid(z)


def _dsilu(z):
    s = jax.nn.sigmoid(z)
    return s * (1.0 + z * (1.0 - s))


def _softplus(x):
    return jnp.maximum(x, 0.0) + jnp.log(1.0 + jnp.exp(-jnp.abs(x)))


def _bucket_tables():
    qi = np.arange(QB)[:, None] + QB
    ki = np.arange(2 * QB)[None, :]
    rel = np.maximum(qi - ki, 0)
    out = []
    for d in DILATIONS:
        dist = rel * d
        max_exact = N_BUCKET // 2
        ratio = np.log(np.maximum(dist, 1).astype(np.float32) / max_exact) / np.float32(math.log(MAX_DIST / max_exact))
        large = max_exact + (ratio * (N_BUCKET - max_exact)).astype(np.int32)
        large = np.minimum(large, N_BUCKET - 1)
        out.append(np.where(dist < max_exact, dist, large).astype(np.int32))
    return np.stack(out)


def _prenorm(x2, g):
    n, d = x2.shape
    tm = 1024

    def body(x_ref, g_ref, h_ref, rs_ref):
        x = x_ref[...]
        rs = lax.rsqrt(jnp.mean(x * x, axis=-1, keepdims=True) + RMS_EPS)
        h_ref[...] = (x * rs * g_ref[...]).astype(BF16)
        rs_ref[...] = rs

    return pl.pallas_call(
        body, name="prenorm", grid=(n // tm,),
        in_specs=[pl.BlockSpec((tm, d), lambda i: (i, 0)), pl.BlockSpec((1, d), lambda i: (0, 0))],
        out_specs=[pl.BlockSpec((tm, d), lambda i: (i, 0)), pl.BlockSpec((tm, 1), lambda i: (i, 0))],
        out_shape=[SDS((n, d), BF16), SDS((n, 1), F32)], compiler_params=_params(("parallel",)))(x2, g)


def _mm(a, b, tn, name):
    m, k = a.shape
    n = b.shape[1]
    tm = 1024

    def body(a_ref, b_ref, o_ref):
        o_ref[...] = _dot(a_ref[...], b_ref[...])

    return pl.pallas_call(
        body, name=name, grid=(n // tn, m // tm),
        in_specs=[pl.BlockSpec((tm, k), lambda j, i: (i, 0)), pl.BlockSpec((k, tn), lambda j, i: (0, j))],
        out_specs=pl.BlockSpec((tm, tn), lambda j, i: (i, j)),
        out_shape=SDS((m, n), F32), compiler_params=_params(("parallel", "parallel")))(a, b)


def _mm_nt(a, b, after, name):
    m, seg = a.shape[1], a.shape[2]
    d, k = b.shape
    tm = 1024
    per = 3
    tk = per * seg

    def body(a_ref, b_ref, after_ref, o_ref):
        r = sum(_dot_nt(a_ref[j].astype(BF16), b_ref[:, seg * j:seg * (j + 1)]) for j in range(per))

        @pl.when(pl.program_id(1) == 0)
        def _():
            o_ref[...] = r

        @pl.when(pl.program_id(1) != 0)
        def _():
            o_ref[...] += r

    in_specs = [pl.BlockSpec((per, tm, seg), lambda i, j: (j, i, 0)), pl.BlockSpec((d, tk), lambda i, j: (0, j)),
                pl.BlockSpec(after.shape, lambda i, j: (0, 0))]
    return pl.pallas_call(
        body, name=name, grid=(m // tm, k // tk), in_specs=in_specs, out_specs=pl.BlockSpec((tm, d), lambda i, j: (i, 0)),
        out_shape=SDS((m, d), F32), compiler_params=_params(("parallel", "arbitrary")))(a, b, after)


def _dh_rest_prenorm_bwd(a_list, b_list, acc, x2, rs, g1, dxo):
    m, d = acc.shape
    tm = 512
    n = len(a_list)

    def body(*refs):
        x_ref, rs_ref, g_ref, dxo_ref, gx_ref, dg_ref = refs[2 * n + 1:]
        dh = refs[2 * n][...]
        for a_ref, b_ref in zip(refs[:n], refs[n:2 * n]):
            dh = dh + _dot_nt(a_ref[...].astype(BF16), b_ref[...])
        x, r = x_ref[...], rs_ref[...]
        gd = dh * g_ref[...]
        gx_ref[...] = dxo_ref[...] + r * (gd - x * (r * r) * jnp.mean(gd * x, axis=-1, keepdims=True))
        dg = jnp.sum(dh * x * r, axis=0, keepdims=True)

        @pl.when(pl.program_id(0) == 0)
        def _():
            dg_ref[...] = dg

        @pl.when(pl.program_id(0) != 0)
        def _():
            dg_ref[...] += dg

    t = pl.BlockSpec((tm, d), lambda i: (i, 0))
    in_specs = [pl.BlockSpec((tm, a.shape[1]), lambda i: (i, 0)) for a in a_list]
    in_specs += [pl.BlockSpec(b.shape, lambda i: (0, 0)) for b in b_list]
    in_specs += [t, t, pl.BlockSpec((tm, 1), lambda i: (i, 0)), pl.BlockSpec((1, d), lambda i: (0, 0)), t]
    return pl.pallas_call(
        body, name="dh_rest_prenorm_bwd", grid=(m // tm,), in_specs=in_specs, out_specs=[t, pl.BlockSpec((1, d), lambda i: (0, 0))],
        out_shape=[SDS((m, d), F32), SDS((1, d), F32)], compiler_params=_params(("arbitrary",)))(*a_list, *b_list, acc, x2, rs, g1, dxo)


def _mm_tn(a, b, tn, name):
    split = b.ndim == 3
    m, k1 = a.shape
    per = 3 if split else 1
    seg = b.shape[2] if split else tn
    tn = per * seg
    n2 = b.shape[0] * seg if split else b.shape[1]
    tm = 1024

    def body(a_ref, b_ref, o_ref):
        first = pl.program_id(1) == 0
        for j in range(per):
            r = _dot_tn(a_ref[...], (b_ref[j] if split else b_ref[...]).astype(BF16))
            cols = slice(seg * j, seg * (j + 1))

            @pl.when(first)
            def _(r=r, cols=cols):
                o_ref[:, cols] = r

            @pl.when(jnp.logical_not(first))
            def _(r=r, cols=cols):
                o_ref[:, cols] += r

    b_spec = pl.BlockSpec((per, tm, seg), lambda j, i: (j, i, 0)) if split else pl.BlockSpec((tm, tn), lambda j, i: (i, j))
    return pl.pallas_call(
        body, name=name, grid=(n2 // tn, m // tm),
        in_specs=[pl.BlockSpec((tm, k1), lambda j, i: (i, 0)), b_spec],
        out_specs=pl.BlockSpec((k1, tn), lambda j, i: (0, j)),
        out_shape=SDS((k1, n2), F32), compiler_params=_params(("parallel", "arbitrary")))(a, b)


def _ds(start, d):
    return pl.ds(start, QB) if d == 1 else pl.ds(start, QB, stride=d)


def _fill_bias(tab_ref, bidx_ref, bias_sc, hp):
    for g in range(3):
        bi = bidx_ref[g]
        for h in range(2):
            acc = jnp.zeros((QB, 2 * QB), F32)
            for j in range(N_BUCKET):
                acc = jnp.where(bi == j, tab_ref[j, g * N_HEAD + hp * 2 + h], acc)
            bias_sc[g * 2 + h] = acc


def _block_starts(it, d, nb):
    rho = it // nb
    n = it % nb
    st = rho + d * QB * n
    stp = rho + d * QB * jnp.maximum(n - 1, 0)
    if d == 1:
        st, stp = pl.multiple_of(QB * it, QB), pl.multiple_of(QB * jnp.maximum(it - 1, 0), QB)
    return st, stp, n > 0


ATTN_BLOCKS_FWD = 8
ATTN_BLOCKS_BWD = 4


def _bdot3(a, b, dims):
    return lax.dot_general(a, b, (dims, ((0,), (0,))), preferred_element_type=F32)


def _attn_operands(q_ref, k_ref, v_ref, bias_sc, g, d, nb, it0, nblk):
    two = nb > 1
    nk = 2 * QB if two else QB
    ii = lax.broadcasted_iota(jnp.int32, (QB, nk), 0)
    cc = lax.broadcasted_iota(jnp.int32, (QB, nk), 1)
    qs, ks, vs, pens, starts = [], [], [], [], []
    for u in range(nblk):
        st, stp, hasprev = _block_starts(it0 + u, d, nb)
        qf = q_ref[0, _ds(st, d), :]
        if two:
            kf = jnp.concatenate([k_ref[0, _ds(stp, d), :], k_ref[0, _ds(st, d), :]], axis=0).astype(BF16)
            vf = jnp.concatenate([v_ref[0, _ds(stp, d), :], v_ref[0, _ds(st, d), :]], axis=0).astype(BF16)
            own = jnp.logical_and(cc >= QB, ii >= cc - QB)
            prev = jnp.logical_and(jnp.logical_and(cc < QB, cc >= ii), hasprev)
            pen = jnp.where(jnp.logical_or(own, prev), 0.0, NEG)
        else:
            kf, vf = k_ref[0, _ds(st, d), :].astype(BF16), v_ref[0, _ds(st, d), :].astype(BF16)
            pen = jnp.where(ii >= cc, 0.0, NEG)
        for h in range(2):
            qs.append(_one_head(qf, h).astype(BF16))
            ks.append(kf)
            vs.append(vf)
            pens.append(pen + (bias_sc[g * 2 + h] if two else bias_sc[g * 2 + h, :, QB:2 * QB]))
        starts.append((st, stp))
    return _stack(qs), _stack(ks), _stack(vs), _stack(pens), starts


def _one_head(x, h):
    lane = lax.broadcasted_iota(jnp.int32, x.shape, 1)
    return jnp.where(lane >= HEAD if h == 1 else lane < HEAD, x, 0.0)


def _pick_heads(x, u):
    lane = lax.broadcasted_iota(jnp.int32, x.shape[1:], 1)
    return jnp.where(lane < HEAD, x[2 * u], x[2 * u + 1])


def _add_heads(x, u):
    return x[2 * u] + x[2 * u + 1]


def _attn_fwd(qkv3, rel_bias, bidx):
    bsz, s, _ = qkv3.shape
    rt = 256

    def body(tab_ref, bidx_ref, *refs):
        q_refs, k_refs, v_refs = refs[0:3], refs[3:6], refs[6:9]
        o_ref, lse_ref = refs[9:11]
        bias_sc, num_sc, den_sc, m_sc = refs[11:]
        pl.when(pl.program_id(1) == 0)(lambda: _fill_bias(tab_ref, bidx_ref, bias_sc, pl.program_id(0)))
        for g, d in enumerate(DILATIONS):
            nb = s // (QB * d)

            def blk(it, c, g=g, d=d, nb=nb):
                q, k, v, bias, starts = _attn_operands(q_refs[g], k_refs[g], v_refs[g], bias_sc, g, d, nb, it * ATTN_BLOCKS_FWD,
                                                       ATTN_BLOCKS_FWD)
                sc = _bdot3(q, k, ((2,), (2,))) * SCALE + bias
                m = jnp.max(sc, axis=-1, keepdims=True)
                p = jnp.exp(sc - m)
                den = jnp.sum(p, axis=-1, keepdims=True)
                num = _bdot3(p.astype(BF16), v, ((2,), (1,)))
                den, m = jnp.broadcast_to(den, num.shape), jnp.broadcast_to(m, num.shape)
                for u, (st, _) in enumerate(starts):
                    num_sc[g, _ds(st, d), :] = _pick_heads(num, u)
                    den_sc[g, _ds(st, d), :] = _pick_heads(den, u)
                    m_sc[g, _ds(st, d), :] = _pick_heads(m, u)
                return c

            lax.fori_loop(0, s // QB // ATTN_BLOCKS_FWD, blk, 0)

        def merge(i, c):
            rows = pl.ds(pl.multiple_of(i * rt, rt), rt)
            m0, m1, m2 = m_sc[0, rows, :], m_sc[1, rows, :], m_sc[2, rows, :]
            mall = jnp.maximum(jnp.maximum(m0, m1), m2)
            w0, w1, w2 = jnp.exp(m0 - mall), jnp.exp(m1 - mall), jnp.exp(m2 - mall)
            num = w0 * num_sc[0, rows, :] + w1 * num_sc[1, rows, :] + w2 * num_sc[2, rows, :]
            den = w0 * den_sc[0, rows, :] + w1 * den_sc[1, rows, :] + w2 * den_sc[2, rows, :]
            o_ref[0, rows, :] = num / den
            lse_ref[0, rows, :] = mall + jnp.log(den)
            return c

        lax.fori_loop(0, s // rt, merge, 0)

    col = lambda w, g: (lambda hp, b: (b, 0, (w * 3 + g) * 4 + hp))
    in_specs = [pl.BlockSpec(memory_space=pltpu.SMEM), pl.BlockSpec((3, QB, 2 * QB), lambda hp, b: (0, 0, 0))]
    in_specs += [pl.BlockSpec((1, s, LANE), col(w, g)) for w in range(3) for g in range(3)]
    out_spec = pl.BlockSpec((1, s, LANE), lambda hp, b: (b, 0, hp))
    return pl.pallas_call(
        body, name="attn_fwd", grid=(4, bsz), in_specs=in_specs, out_specs=[out_spec, out_spec],
        out_shape=[SDS((bsz, s, WIDTH), F32), SDS((bsz, s, WIDTH), F32)],
        scratch_shapes=[pltpu.VMEM((6, QB, 2 * QB), F32), pltpu.VMEM((3, s, LANE), F32), pltpu.VMEM((3, s, LANE), F32),
                        pltpu.VMEM((3, s, LANE), F32)],
        compiler_params=_params(("arbitrary", "arbitrary")))(rel_bias, bidx, *([qkv3] * 9))


def _attn_bwd(qkv3, o3, lse3, do3, rel_bias, bidx):
    bsz, s, _ = qkv3.shape
    rt = 256

    def body(tab_ref, bidx_ref, *refs):
        q_refs, k_refs, v_refs = refs[0:3], refs[3:6], refs[6:9]
        o_ref, lse_ref, do_ref, dqkv_ref, db_ref, bias_sc, delta_sc, acc_sc = refs[9:]
        dq_refs, dk_refs, dv_refs = ([acc_sc.at[w * 3 + g] for g in range(3)] for w in range(3))

        @pl.when(pl.program_id(1) == 0)
        def _():
            _fill_bias(tab_ref, bidx_ref, bias_sc, pl.program_id(0))
            db_ref[...] = jnp.zeros_like(db_ref)

        def prep(i, c):
            rows = pl.ds(pl.multiple_of(i * rt, rt), rt)
            prod = do_ref[0, rows, :] * o_ref[0, rows, :]
            d0 = jnp.sum(prod[:, :HEAD], axis=-1, keepdims=True)
            d1 = jnp.sum(prod[:, HEAD:], axis=-1, keepdims=True)
            delta_sc[rows, :] = jnp.concatenate([jnp.broadcast_to(d0, (rt, HEAD)), jnp.broadcast_to(d1, (rt, HEAD))], axis=1)
            z = jnp.zeros((rt, LANE), F32)
            for g in range(3):
                dk_refs[g][0, rows, :] = z
                dv_refs[g][0, rows, :] = z
            return c

        lax.fori_loop(0, s // rt, prep, 0)
        for g, d in enumerate(DILATIONS):
            nb = s // (QB * d)

            def blk(it, c, g=g, d=d, nb=nb):
                q, k, v, bias, starts = _attn_operands(q_refs[g], k_refs[g], v_refs[g], bias_sc, g, d, nb, it * ATTN_BLOCKS_BWD,
                                                       ATTN_BLOCKS_BWD)
                dos, lses, deltas = [], [], []
                for st, _ in starts:
                    dof, lsef, delf = do_ref[0, _ds(st, d), :], lse_ref[0, _ds(st, d), :], delta_sc[_ds(st, d), :]
                    for h in range(2):
                        dos.append(_one_head(dof, h).astype(BF16))
                        lses.append(lsef[:, HEAD * h:HEAD * h + 1])
                        deltas.append(delf[:, HEAD * h:HEAD * h + 1])
                do, lse, delta = _stack(dos), _stack(lses), _stack(deltas)
                p = jnp.exp(_bdot3(q, k, ((2,), (2,))) * SCALE + bias - lse)
                dv = _bdot3(p.astype(BF16), do, ((1,), (1,)))
                ds = p * (_bdot3(do, v, ((2,), (2,))) - delta)
                dsb = ds.astype(BF16)
                dq = _bdot3(dsb, k, ((2,), (1,))) * SCALE
                dk = _bdot3(dsb, q, ((1,), (1,))) * SCALE
                two = nb > 1
                for h in range(2):
                    dsum = sum(ds[2 * u + h] for u in range(ATTN_BLOCKS_BWD))
                    if two:
                        db_ref[0, g * 2 + h] += dsum
                    else:
                        db_ref[0, g * 2 + h, :, QB:2 * QB] += dsum
                for u, (st, stp) in enumerate(starts):
                    dq_refs[g][0, _ds(st, d), :] = _pick_heads(dq, u)
                    if two:
                        dk_refs[g][0, _ds(stp, d), :] += _add_heads(dk[:, :QB], u)
                        dv_refs[g][0, _ds(stp, d), :] += _add_heads(dv[:, :QB], u)
                    dk_refs[g][0, _ds(st, d), :] += _add_heads(dk[:, QB:] if two else dk, u)
                    dv_refs[g][0, _ds(st, d), :] += _add_heads(dv[:, QB:] if two else dv, u)
                return c

            lax.fori_loop(0, s // QB // ATTN_BLOCKS_BWD, blk, 0)

        def flush(i, c):
            rows = pl.ds(pl.multiple_of(i * rt, rt), rt)
            for j in range(9):
                dqkv_ref[j, 0, rows, :] = acc_sc[j, 0, rows, :].astype(BF16)
            return c

        lax.fori_loop(0, s // rt, flush, 0)

    col = lambda w, g: (lambda hp, b: (b, 0, (w * 3 + g) * 4 + hp))
    blk_spec = pl.BlockSpec((1, s, LANE), lambda hp, b: (b, 0, hp))
    in_specs = [pl.BlockSpec(memory_space=pltpu.SMEM), pl.BlockSpec((3, QB, 2 * QB), lambda hp, b: (0, 0, 0))]
    in_specs += [pl.BlockSpec((1, s, LANE), col(w, g)) for w in range(3) for g in range(3)]
    in_specs += [blk_spec] * 3
    out_specs = [pl.BlockSpec((9, 1, s, LANE), lambda hp, b: (0, b, 0, hp)), pl.BlockSpec((1, 6, QB, 2 * QB), lambda hp, b: (hp, 0, 0, 0))]
    out_shape = [SDS((9, bsz, s, WIDTH), BF16), SDS((4, 6, QB, 2 * QB), F32)]
    return pl.pallas_call(
        body, name="attn_bwd", grid=(4, bsz), in_specs=in_specs, out_specs=out_specs, out_shape=out_shape,
        scratch_shapes=[pltpu.VMEM((6, QB, 2 * QB), F32), pltpu.VMEM((s, LANE), F32), pltpu.VMEM((9, 1, s, LANE), F32)],
        compiler_params=_params(("parallel", "arbitrary")))(rel_bias, bidx, *([qkv3] * 9), o3, lse3, do3)


def _bias_grad(dbias, bidx):
    def body(db_ref, bidx_ref, o_ref):
        lane = lax.broadcasted_iota(jnp.int32, (1, LANE), 1)
        for g in range(3):
            bi = bidx_ref[g]
            for hp in range(4):
                for h in range(2):
                    mat = db_ref[hp, g * 2 + h]
                    row = jnp.zeros((1, LANE), F32)
                    for j in range(N_BUCKET):
                        part = jnp.sum(jnp.where(bi == j, mat, 0.0), axis=0, keepdims=True)
                        row = jnp.where(lane == j, jnp.sum(part, axis=1, keepdims=True), row)
                    hd = g * N_HEAD + hp * 2 + h
                    o_ref[hd:hd + 1, :] = row

    return pl.pallas_call(body, name="bias_grad", out_shape=SDS((3 * N_HEAD, LANE), F32), compiler_params=_params())(dbias, bidx)


def _pre_fn(r, k0, v, wl, al, w0, wup, a0, aup, kk_, ka_):
    u = w0 + _bdot(jnp.tanh(wl), wup)
    lw = -jnp.exp(-_softplus(-u) - 0.5)
    a = jax.nn.sigmoid(a0 + _bdot(al, aup))
    kkraw = k0 * kk_
    k = k0 * (1.0 + (a - 1.0) * ka_)
    return r, lw, k, v, kkraw, a


PRE_SPLIT = (0, WIDTH, 2 * WIDTH, 3 * WIDTH, 3 * WIDTH + LORA, 3 * WIDTH + 2 * LORA)


def _pre_pieces(prs):
    return [prs[:, a:b] for a, b in zip(PRE_SPLIT[:-1], PRE_SPLIT[1:])]


PRE_TT = 512


def _shifted(pr_ref, edge_ref, first, back):
    pr = pr_ref[0]
    tt = pr.shape[0]
    row = lax.broadcasted_iota(jnp.int32, (tt, 1), 0)
    if back:
        edge = jnp.where(first, 0.0, edge_ref[0, 7:8, :])
        return jnp.where(row == 0, edge, pltpu.roll(pr, 1, axis=0))
    edge = jnp.where(first, 0.0, edge_ref[0, 0:1, :])
    return jnp.where(row == tt - 1, edge, pltpu.roll(pr, tt - 1, axis=0))


def _rwkv_pre(pr3, mix, w0, wup, a0, aup, kk_, ka_):
    bsz, s, _ = pr3.shape
    tt = PRE_TT

    def body(pr_ref, edge_ref, mix_ref, w0_ref, wup_ref, a0_ref, aup_ref, kk_ref, ka_ref, *outs):
        pr = pr_ref[0]
        prev = _shifted(pr_ref, edge_ref, pl.program_id(1) == 0, True)
        prs = pr + (prev - pr) * mix_ref[...]
        vals = _pre_fn(*_pre_pieces(prs), w0_ref[...], wup_ref[...].astype(F32), a0_ref[...], aup_ref[...].astype(F32), kk_ref[...],
                       ka_ref[...])
        for o, val in zip(outs, vals):
            o[0] = val

    vec = lambda n: pl.BlockSpec((1, n), lambda b, i: (0, 0))
    mat = pl.BlockSpec((LORA, WIDTH), lambda b, i: (0, 0))
    in_specs = [pl.BlockSpec((1, tt, PR_COLS), lambda b, i: (b, i, 0)),
                pl.BlockSpec((1, 8, PR_COLS), lambda b, i: (b, jnp.maximum(i * (tt // 8) - 1, 0), 0)),
                vec(PR_COLS), vec(WIDTH), mat, vec(WIDTH), mat, vec(WIDTH), vec(WIDTH)]
    out_spec = pl.BlockSpec((1, tt, WIDTH), lambda b, i: (b, i, 0))
    return pl.pallas_call(
        body, name="rwkv_pre", grid=(bsz, s // tt), in_specs=in_specs, out_specs=[out_spec] * 6,
        out_shape=[SDS((bsz, s, WIDTH), F32)] * 6, compiler_params=_params(("parallel", "parallel")))(
            pr3, pr3, mix, w0, wup, a0, aup, kk_, ka_)


def _rwkv_pre_bwd(pr3, cots, mix, w0, wup, a0, aup, kk_, ka_):
    bsz, s, _ = pr3.shape
    tt = PRE_TT

    def body(pr_ref, edge_ref, c0, c1, c2, c3, c4, c5, mix_ref, w0_ref, wup_ref, a0_ref, aup_ref, kk_ref, ka_ref,
             dprs_ref, dmix_ref, dw0_ref, dwup_ref, da0_ref, daup_ref, dkk_ref, dka_ref):
        pr = pr_ref[0]
        prev = _shifted(pr_ref, edge_ref, pl.program_id(1) == 0, True)
        prs = pr + (prev - pr) * mix_ref[...]
        _, vjp = jax.vjp(_pre_fn, *_pre_pieces(prs), w0_ref[...], wup_ref[...].astype(F32), a0_ref[...], aup_ref[...].astype(F32),
                         kk_ref[...], ka_ref[...])
        grads = vjp(tuple(c[0] for c in (c0, c1, c2, c3, c4, c5)))
        for piece, a, b in zip(grads[:5], PRE_SPLIT[:-1], PRE_SPLIT[1:]):
            dprs_ref[0, :, a:b] = piece
        dw0, dwup, da0, daup, dkk, dka = grads[5:]
        dprs = dprs_ref[0]
        grads = (jnp.sum(dprs * (prev - pr), axis=0, keepdims=True), dw0, dwup, da0, daup, dkk, dka)
        refs = (dmix_ref, dw0_ref, dwup_ref, da0_ref, daup_ref, dkk_ref, dka_ref)
        first = jnp.logical_and(pl.program_id(0) == 0, pl.program_id(1) == 0)

        @pl.when(first)
        def _():
            for r_, g_ in zip(refs, grads):
                r_[...] = g_

        @pl.when(jnp.logical_not(first))
        def _():
            for r_, g_ in zip(refs, grads):
                r_[...] += g_

    vec = lambda n: pl.BlockSpec((1, n), lambda b, i: (0, 0))
    mat = pl.BlockSpec((LORA, WIDTH), lambda b, i: (0, 0))
    tile = pl.BlockSpec((1, tt, WIDTH), lambda b, i: (b, i, 0))
    in_specs = [pl.BlockSpec((1, tt, PR_COLS), lambda b, i: (b, i, 0)),
                pl.BlockSpec((1, 8, PR_COLS), lambda b, i: (b, jnp.maximum(i * (tt // 8) - 1, 0), 0))]
    in_specs += [tile] * 6 + [vec(PR_COLS), vec(WIDTH), mat, vec(WIDTH), mat, vec(WIDTH), vec(WIDTH)]
    out_specs = [pl.BlockSpec((1, tt, PR_COLS), lambda b, i: (b, i, 0)), vec(PR_COLS), vec(WIDTH), mat, vec(WIDTH), mat,
                 vec(WIDTH), vec(WIDTH)]
    out_shape = [SDS((bsz, s, PR_COLS), F32), SDS((1, PR_COLS), F32), SDS((1, WIDTH), F32), SDS((LORA, WIDTH), F32),
                 SDS((1, WIDTH), F32), SDS((LORA, WIDTH), F32), SDS((1, WIDTH), F32), SDS((1, WIDTH), F32)]
    return pl.pallas_call(
        body, name="rwkv_pre_bwd", grid=(bsz, s // tt), in_specs=in_specs, out_specs=out_specs, out_shape=out_shape,
        compiler_params=_params(("arbitrary", "arbitrary")))(pr3, pr3, *cots, mix, w0, wup, a0, aup, kk_, ka_)


def _shift_bwd(dprs3, mix):
    bsz, s, _ = dprs3.shape
    tt = PRE_TT
    nt = s // tt

    def body(d_ref, edge_ref, mix_ref, o_ref):
        nxt = _shifted(d_ref, edge_ref, pl.program_id(1) == nt - 1, False)
        m = mix_ref[...]
        o_ref[0] = (d_ref[0] * (1.0 - m) + nxt * m).astype(BF16)

    in_specs = [pl.BlockSpec((1, tt, PR_COLS), lambda b, i: (b, i, 0)),
                pl.BlockSpec((1, 8, PR_COLS), lambda b, i: (b, jnp.minimum((i + 1) * (tt // 8), s // 8 - 1), 0)),
                pl.BlockSpec((1, PR_COLS), lambda b, i: (0, 0))]
    return pl.pallas_call(
        body, name="shift_bwd", grid=(bsz, nt), in_specs=in_specs, out_specs=pl.BlockSpec((1, tt, PR_COLS), lambda b, i: (b, i, 0)),
        out_shape=SDS((bsz, s, PR_COLS), BF16), compiler_params=_params(("parallel", "parallel")))(dprs3, dprs3, mix)


_NN, _NT, _TN = ((2,), (1,)), ((2,), (2,)), ((1,), (1,))


def _dot3_bf16(a, b, dims):
    return lax.dot_general(a.astype(BF16), b.astype(BF16), (dims, ((0,), (0,))), preferred_element_type=F32)


class _Dots:
    def __init__(self, fwd):
        def make(dims, da_rule, db_rule):
            @jax.custom_vjp
            def f(a, b):
                return fwd(a, b, dims)

            f.defvjp(lambda a, b: (f(a, b), (a, b)), lambda res, g: (da_rule(*res, g), db_rule(*res, g)))
            return f

        one = _dot3_bf16
        self.mm = make(_NN, lambda a, b, g: one(g, b, _NT), lambda a, b, g: one(a, g, _TN))
        self.mm_nt = make(_NT, lambda a, b, g: one(g, b, _NN), lambda a, b, g: one(g, a, _TN))
        self.mm_tn = make(_TN, lambda a, b, g: one(b, g, _NT), lambda a, b, g: one(a, g, _NN))

        def powers(aab):
            ps = [aab]
            while 2 ** len(ps) < aab.shape[1]:
                ps.append(fwd(ps[-1], ps[-1], _NN))
            return ps

        def apply(ps, z, dims):
            for p in ps:
                z = z + fwd(p, z, dims)
            return z

        @jax.custom_vjp
        def solve(aab, z):
            return apply(powers(aab), z, _NN)

        def solve_fwd(aab, z):
            ps = powers(aab)
            x = apply(ps, z, _NN)
            return x, (ps, x)

        def solve_bwd(res, g):
            ps, x = res
            dz = apply(ps, g, _TN)
            return fwd(dz, x, _NT), dz

        solve.defvjp(solve_fwd, solve_bwd)
        self.solve = solve


_ONE_PASS = _Dots(_dot3_bf16)
_bmm, _bmm_tn = _ONE_PASS.mm, _ONE_PASS.mm_tn


def _chunk_fn(s0t, r, lw, k, v, kkraw, a, rk, lnw, lnb, first=False, d=_ONE_PASS):
    c = r.shape[1]
    at, rt, btc, ktc, gc, aab, arb, xv, arkv, ain, bin_ = _chunk_core(r, lw, k, v, kkraw, a, d)
    rs = d.mm(jnp.concatenate([at, rt], axis=1), s0t)
    u = d.solve(aab, rs[:, :c] + xv)
    y = rs[:, c:] + d.mm(arb, u) + arkv
    if first:
        y = _with_early_rows(y, r, lw, k, v, ain, bin_)
    gcol = jnp.sum(_diag(gc), axis=2, keepdims=True)
    sct = gcol * s0t + d.mm_tn(jnp.concatenate([btc, ktc], axis=1), jnp.concatenate([u, v], axis=1))
    return _post(y, r, k, v, rk, lnw, lnb), sct


def _diag(gc):
    return jnp.where(_masks(HEAD)[2], gc, 0.0)


def _with_early_rows(y, r, lw, k, v, ain, bin_):
    early = _early_rows(r[:2], lw[:2], k[:2], v[:2], ain[:2], bin_[:2])
    return jnp.concatenate([jnp.concatenate([early, y[:2, EARLY:]], axis=1), y[2:]], axis=0)


def _early_rows(r, lw, k, v, ain, bin_):
    cols = lambda x: _stack([jnp.transpose(x[h]) for h in range(2)])
    wc, bc, kc = cols(jnp.exp(lw)), cols(bin_), cols(k)
    st = jnp.zeros((2, HEAD, HEAD), F32)
    rows = []
    for t in range(EARLY):
        sa = _ONE_PASS.mm(ain[:, t:t + 1], st)
        st = st * wc[:, :, t:t + 1] + bc[:, :, t:t + 1] * sa + kc[:, :, t:t + 1] * v[:, t:t + 1]
        rows.append(_ONE_PASS.mm(r[:, t:t + 1], st))
    return jnp.concatenate(rows, axis=1)


def _chunk_rows(c):
    return pl.ds(c * CHUNK, CHUNK) if isinstance(c, int) else pl.ds(pl.multiple_of(c * CHUNK, CHUNK), CHUNK)


def _stack(xs):
    return jnp.concatenate([x[None] for x in xs], axis=0)


def _pairs(ref, chunks):
    tiles = [ref[0, _chunk_rows(c), :] for c in chunks]
    return _stack([t[:, HEAD * h:HEAD * h + HEAD] for t in tiles for h in range(2)])


def _unpair(vals, j):
    return jnp.concatenate([vals[2 * j], vals[2 * j + 1]], axis=1)


def _masks(c):
    ii = lax.broadcasted_iota(jnp.int32, (c, c), 0)
    jj = lax.broadcasted_iota(jnp.int32, (c, c), 1)
    return ii > jj, ii >= jj, ii == jj


@jax.custom_vjp
def _running_sum(lw):
    return _tri_dot(lw, _NN)


def _tri_dot(x, dims):
    g_, c, _ = x.shape
    tri = jnp.broadcast_to(_masks(c)[1].astype(BF16), (g_, c, c))
    head = x.astype(BF16)
    rest = (x - head.astype(F32)).astype(BF16)
    return lax.dot_general(tri, head, (dims, ((0,), (0,))), preferred_element_type=F32) + \
        lax.dot_general(tri, rest, (dims, ((0,), (0,))), preferred_element_type=F32)


_running_sum.defvjp(lambda lw: (_running_sum(lw), None), lambda _, ct: (_tri_dot(ct, _TN),))


def _chunk_core(r, lw, k, v, kkraw, a, d=_ONE_PASS):
    g_, c = r.shape[0], r.shape[1]
    nrm = jnp.sqrt(jnp.sum(kkraw * kkraw, axis=-1, keepdims=True))
    kkn = kkraw / jnp.maximum(nrm, 1e-12)
    ain, bin_ = -kkn, kkn * a
    strict, incl, _ = _masks(c)
    lg = _running_sum(lw)
    g, gp, gi = jnp.exp(lg), jnp.exp(lg - lw), jnp.exp(-lg)
    at, rt, bt, kt = ain * gp, r * g, bin_ * gi, k * gi
    aa = d.mm_nt(jnp.concatenate([at, rt], axis=1), jnp.concatenate([bt, kt], axis=1))
    aab = jnp.where(strict, aa[:, :c, :c], 0.0)
    aak = jnp.where(strict, aa[:, :c, c:], 0.0)
    arb = jnp.where(incl, aa[:, c:, :c], 0.0)
    ark = jnp.where(incl, aa[:, c:, c:], 0.0)
    akv = d.mm(jnp.concatenate([aak, ark], axis=1), v)
    gc = g[:, c - 1:c, :]
    return at, rt, bt * gc, kt * gc, gc, aab, arb, akv[:, :c], akv[:, c:], ain, bin_


def _lane_sum(x):
    return jnp.sum(x, axis=-1, keepdims=True)


def _lane_sum_mxu(x):
    g, c, n = x.shape
    x2 = x.reshape(g * c, n)
    head = x2.astype(BF16)
    rest = (x2 - head.astype(F32)).astype(BF16)
    ones = jnp.ones((n, n), BF16)
    return (_dot(head, ones) + _dot(rest, ones)).reshape(g, c, n)


def _post(y, r, k, v, rk, lnw, lnb, lane_sum=_lane_sum):
    mu = lane_sum(y) * (1.0 / HEAD)
    var = lane_sum(jnp.square(y - mu)) * (1.0 / HEAD)
    yn = (y - mu) * lax.rsqrt(var + GN_EPS) * lnw + lnb
    return yn + lane_sum(r * k * rk) * v


def _chunk_consts(r, lw, k, v, kkraw, a, first=False):
    d = _ONE_PASS
    at, rt, btc, ktc, gc, aab, arb, xv, arkv, ain, bin_ = _chunk_core(r, lw, k, v, kkraw, a, d)
    z = d.solve(aab, jnp.concatenate([at, xv], axis=2))
    ryv = jnp.concatenate([rt, arkv], axis=2) + d.mm(arb, z)
    if first:
        ryv = jnp.concatenate([ryv[:, :, :HEAD], _with_early_rows(ryv[:, :, HEAD:], r, lw, k, v, ain, bin_)], axis=2)
    mkv = d.mm_tn(btc, z) + jnp.concatenate([_diag(gc), d.mm_tn(ktc, v)], axis=2)
    return mkv, ryv


def _rwkv_scan(ins, rk, lnw, lnb):
    bsz, s, _ = ins[0].shape
    nch = s // CHUNK

    def consts_body(r_ref, lw_ref, k_ref, v_ref, kk_ref, a_ref, mkv_ref, ry_ref, yv_ref):
        def group(i, carry):
            chunks = [i * CHUNK_GROUP + j for j in range(CHUNK_GROUP)]
            mkv, ryv = _chunk_consts(*[_pairs(ref, chunks) for ref in (r_ref, lw_ref, k_ref, v_ref, kk_ref, a_ref)],
                                     first=isinstance(i, int) and i == 0)
            for j, c in enumerate(chunks):
                for h in range(2):
                    mkv_ref[0, 0, c, h] = mkv[2 * j + h]
                ry_ref[0, _chunk_rows(c), :] = jnp.concatenate([ryv[2 * j][:, :HEAD], ryv[2 * j + 1][:, :HEAD]], axis=1)
                yv_ref[0, _chunk_rows(c), :] = jnp.concatenate([ryv[2 * j][:, HEAD:], ryv[2 * j + 1][:, HEAD:]], axis=1)
            return carry

        group(0, 0)
        lax.fori_loop(1, nch // CHUNK_GROUP, group, 0)

    tile = pl.BlockSpec((1, s, LANE), lambda b, hp: (b, 0, hp))
    vec = pl.BlockSpec((1, LANE), lambda b, hp: (0, hp))
    mkv_spec = pl.BlockSpec((1, 1, nch, 2, HEAD, LANE), lambda b, hp: (b, hp, 0, 0, 0, 0))
    st_spec = pl.BlockSpec((1, 1, nch, 2, HEAD, HEAD), lambda b, hp: (b, hp, 0, 0, 0, 0))
    mkv, ry, yv = pl.pallas_call(
        consts_body, name="rwkv_consts", grid=(bsz, 4), in_specs=[tile] * 6, out_specs=[mkv_spec, tile, tile],
        out_shape=[SDS((bsz, 4, nch, 2, HEAD, LANE), F32), SDS((bsz, s, WIDTH), F32), SDS((bsz, s, WIDTH), F32)],
        compiler_params=_params(("parallel", "parallel")))(*ins)

    states = _chunk_recurrence(mkv, None, "rwkv_states")

    def out_body(ry_ref, yv_ref, r_ref, k_ref, v_ref, st_ref, rk_ref, lnw_ref, lnb_ref, o_ref):
        y, r, k, v, rk_, lnw_, lnb_ = _scan_rows(ry_ref, yv_ref, r_ref, k_ref, v_ref, st_ref, rk_ref, lnw_ref, lnb_ref)
        o = _post(y, r, k, v, rk_, lnw_, lnb_, _lane_sum_mxu)
        for j in range(CHUNK_GROUP):
            o_ref[0, _chunk_rows(j), :] = _unpair(o, j)

    o = pl.pallas_call(
        out_body, name="rwkv_out", grid=(bsz, 4, nch // CHUNK_GROUP), in_specs=_group_specs(5), out_specs=_group_specs(1)[0],
        out_shape=SDS((bsz, s, WIDTH), F32),
        compiler_params=_params(("parallel", "parallel", "parallel")))(ry, yv, ins[0], ins[2], ins[3], states, rk, lnw, lnb)
    return o, states, (mkv, ry, yv)


def _group_specs(n_tiles):
    tile = pl.BlockSpec((1, CHUNK_GROUP * CHUNK, LANE), lambda b, hp, t: (b, t, hp))
    if n_tiles == 1:
        return [tile]
    st = pl.BlockSpec((1, 1, CHUNK_GROUP, 2, HEAD, HEAD), lambda b, hp, t: (b, hp, t, 0, 0, 0))
    vec = pl.BlockSpec((1, LANE), lambda b, hp, t: (0, hp))
    return [tile] * n_tiles + [st] + [vec] * 3


def _scan_rows(ry_ref, yv_ref, r_ref, k_ref, v_ref, st_ref, rk_ref, lnw_ref, lnb_ref):
    chunks = list(range(CHUNK_GROUP))
    ry, yv, r, k, v = (_pairs(ref, chunks) for ref in (ry_ref, yv_ref, r_ref, k_ref, v_ref))
    st = _stack([st_ref[0, 0, c, h] for c in chunks for h in range(2)])
    vecs = [_stack([ref[:, HEAD * h:HEAD * h + HEAD] for _ in chunks for h in range(2)]) for ref in (rk_ref, lnw_ref, lnb_ref)]
    return (_bmm(ry, st) + yv, r, k, v, *vecs)


def _chunk_recurrence(mkv, q, name):
    bsz, _, nch = mkv.shape[:3]
    pairs = [(hp, h) for hp in range(4) for h in range(2)]

    def body(*refs):
        mkv_ref, out_ref, acc = refs[0], refs[-2], refs[-1]
        acc[...] = jnp.zeros_like(acc)

        def step(i, carry):
            c = i if q is None else nch - 1 - i
            cur = acc[...]
            for j, (hp, h) in enumerate(pairs):
                out_ref[0, hp, c, h] = cur[j]
            m = _stack([mkv_ref[0, hp, c, h] for hp, h in pairs])
            if q is None:
                acc[...] = _bmm(m[:, :, :HEAD], cur) + m[:, :, HEAD:]
            else:
                acc[...] = _bmm_tn(m[:, :, :HEAD], cur) + _stack([refs[1][0, hp, c, h] for hp, h in pairs])
            return carry

        lax.fori_loop(0, nch, step, 0)

    spec = lambda w: pl.BlockSpec((1, 4, nch, 2, HEAD, w), lambda b: (b, 0, 0, 0, 0, 0))
    return pl.pallas_call(
        body, name=name, grid=(bsz,), in_specs=[spec(LANE)] + ([] if q is None else [spec(HEAD)]), out_specs=spec(HEAD),
        out_shape=SDS((bsz, 4, nch, 2, HEAD, HEAD), F32), scratch_shapes=[pltpu.VMEM((8, HEAD, HEAD), F32)],
        compiler_params=_params(("parallel",)))(*([mkv] if q is None else [mkv, q]))


def _rwkv_scan_bwd(ins, states, consts, do3, rk, lnw, lnb):
    bsz, s, _ = ins[0].shape
    nch = s // CHUNK

    mkv, ry, yv = consts

    def q_body(do_ref, ry_ref, yv_ref, r_ref, k_ref, v_ref, st_ref, rk_ref, lnw_ref, lnb_ref, q_ref):
        y, r, k, v, rk_, lnw_, lnb_ = _scan_rows(ry_ref, yv_ref, r_ref, k_ref, v_ref, st_ref, rk_ref, lnw_ref, lnb_ref)
        _, vjp = jax.vjp(lambda y_: _post(y_, r, k, v, rk_, lnw_, lnb_), y)
        (dy,) = vjp(_pairs(do_ref, list(range(CHUNK_GROUP))))
        q = _bmm_tn(_pairs(ry_ref, list(range(CHUNK_GROUP))), dy)
        for j in range(CHUNK_GROUP):
            for h in range(2):
                q_ref[0, 0, j, h] = q[2 * j + h]

    specs = _group_specs(6)
    q = pl.pallas_call(
        q_body, name="rwkv_q", grid=(bsz, 4, nch // CHUNK_GROUP), in_specs=specs, out_specs=specs[6],
        out_shape=SDS((bsz, 4, nch, 2, HEAD, HEAD), F32),
        compiler_params=_params(("parallel", "parallel", "parallel")))(do3, ry, yv, ins[0], ins[2], ins[3], states, rk, lnw, lnb)

    dstates = _chunk_recurrence(mkv, q, "rwkv_dstates")

    def body(r_ref, lw_ref, k_ref, v_ref, kk_ref, a_ref, st_ref, dst_ref, do_ref, rk_ref, lnw_ref, lnb_ref,
             dr_ref, dlw_ref, dk_ref, dv_ref, dkk_ref, da_ref, drk_ref, dlnw_ref, dlnb_ref):
        chunks = list(range(BWD_GROUP))
        par_refs = (drk_ref, dlnw_ref, dlnb_ref)

        @pl.when(jnp.logical_and(pl.program_id(1) == 0, pl.program_id(2) == 0))
        def _():
            for ref in par_refs:
                ref[...] = jnp.zeros_like(ref)

        def group(first):
            per_pair = lambda ref: _stack([ref[0, 0, c, h] for c in chunks for h in range(2)])
            vecs = [_stack([ref[:, HEAD * h:HEAD * h + HEAD] for _ in chunks for h in range(2)]) for ref in (rk_ref, lnw_ref, lnb_ref)]
            _, vjp = jax.vjp(functools.partial(_chunk_fn, first=first, d=_ONE_PASS), per_pair(st_ref),
                             *[_pairs(ref, chunks) for ref in (r_ref, lw_ref, k_ref, v_ref, kk_ref, a_ref)], *vecs)
            grads = vjp((_pairs(do_ref, chunks), per_pair(dst_ref)))
            for ref, cot in zip((dr_ref, dlw_ref, dk_ref, dv_ref, dkk_ref, da_ref), grads[1:7]):
                for j, c in enumerate(chunks):
                    ref[0, _chunk_rows(c), :] = _unpair(cot, j)
            for ref, g_ in zip(par_refs, grads[7:10]):
                ref[...] += jnp.concatenate([sum(g_[2 * j + h] for j in range(BWD_GROUP)) for h in range(2)], axis=1)

        pl.when(pl.program_id(2) == 0)(functools.partial(group, True))
        pl.when(pl.program_id(2) != 0)(functools.partial(group, False))

    tt = BWD_GROUP * CHUNK
    tile = pl.BlockSpec((1, tt, LANE), lambda hp, b, t: (b, t, hp))
    vec = pl.BlockSpec((1, LANE), lambda hp, b, t: (0, hp))
    st_spec = pl.BlockSpec((1, 1, BWD_GROUP, 2, HEAD, HEAD), lambda hp, b, t: (b, hp, t, 0, 0, 0))
    outs = pl.pallas_call(
        body, name="rwkv_scan_bwd", grid=(4, bsz, s // tt), in_specs=[tile] * 6 + [st_spec, st_spec, tile] + [vec] * 3,
        out_specs=[tile] * 6 + [vec] * 3,
        out_shape=[SDS((bsz, s, WIDTH), F32)] * 6 + [SDS((1, WIDTH), F32)] * 3,
        compiler_params=_params(("parallel", "arbitrary", "arbitrary")))(*ins, states, dstates, do3, rk, lnw, lnb)
    return outs[:6], outs[6:]


def _head(o_attn, o_rwkv, z_attn, z_rwkv, gm, x2, tgt, wua, wur, wout, g2):
    n = x2.shape[0]
    tm = 256
    nt = n // tm
    d = D_MODEL

    def body(oa_ref, or_ref, za_ref, zr_ref, gm_ref, x_ref, t_ref, wua_ref, wur_ref, wout_ref, g2_ref,
             dxo_ref, doa_ref, dor_ref, dza_ref, dzr_ref, dgm_ref, dwua_ref, dwur_ref, dwout_ref, dg2_ref, loss_ref, lacc):
        i = pl.program_id(0)
        oa, orw, za, zr = oa_ref[...], or_ref[...], za_ref[...], zr_ref[...]
        ga, gb = gm_ref[:, 0:d], gm_ref[:, d:2 * d]
        am = (oa * _silu(za)).astype(BF16)
        bm = (orw * _silu(zr)).astype(BF16)
        ya, yb = _dot(am, wua_ref[...]), _dot(bm, wur_ref[...])
        sa, sb = jax.nn.sigmoid(ga), jax.nn.sigmoid(gb)
        merged = (sa * ya + sb * yb).astype(BF16)
        out = _dot(merged, wout_ref[...])
        rs = lax.rsqrt(jnp.mean(out * out, axis=-1, keepdims=True) + RMS_EPS)
        g2 = g2_ref[...]
        err = x_ref[...] + out * rs * g2 - t_ref[...]
        lpart = jnp.sum(err * err, axis=0, keepdims=True)
        dxo = err * (1.0 / d)
        dxo_ref[...] = dxo
        dg2 = jnp.sum(dxo * out * rs, axis=0, keepdims=True)
        gd = dxo * g2
        dout = (rs * (gd - out * (rs * rs) * jnp.mean(gd * out, axis=-1, keepdims=True))).astype(BF16)
        dmerged = _dot_nt(dout, wout_ref[...])
        dwout = _dot_tn(merged, dout)
        dya, dyb = (dmerged * sa).astype(BF16), (dmerged * sb).astype(BF16)
        dgm_ref[:, 0:d] = (dmerged * ya * sa * (1.0 - sa)).astype(BF16)
        dgm_ref[:, d:2 * d] = (dmerged * yb * sb * (1.0 - sb)).astype(BF16)
        dam, dbm = _dot_nt(dya, wua_ref[...]), _dot_nt(dyb, wur_ref[...])
        dwua, dwur = _dot_tn(am, dya), _dot_tn(bm, dyb)
        doa_ref[...] = dam * _silu(za)
        dza_ref[...] = (dam * oa * _dsilu(za)).astype(BF16)
        dor_ref[...] = dbm * _silu(zr)
        dzr_ref[...] = (dbm * orw * _dsilu(zr)).astype(BF16)

        @pl.when(i == 0)
        def _():
            dwua_ref[...], dwur_ref[...], dwout_ref[...], dg2_ref[...], lacc[...] = dwua, dwur, dwout, dg2, lpart

        @pl.when(i != 0)
        def _():
            dwua_ref[...] += dwua
            dwur_ref[...] += dwur
            dwout_ref[...] += dwout
            dg2_ref[...] += dg2
            lacc[...] += lpart

        @pl.when(i == nt - 1)
        def _():
            loss_ref[...] = jnp.sum(lacc[...], axis=1, keepdims=True) * (0.5 / d)

    t512 = pl.BlockSpec((tm, WIDTH), lambda i: (i, 0))
    t1k = pl.BlockSpec((tm, d), lambda i: (i, 0))
    t2k = pl.BlockSpec((tm, 2 * d), lambda i: (i, 0))
    full = lambda r, c: pl.BlockSpec((r, c), lambda i: (0, 0))
    return pl.pallas_call(
        body, name="head_fwd_bwd", grid=(nt,),
        in_specs=[t512, t512, t512, t512, t2k, t1k, t1k, full(WIDTH, d), full(WIDTH, d), full(d, d), full(1, d)],
        out_specs=[t1k, t512, t512, t512, t512, t2k, full(WIDTH, d), full(WIDTH, d), full(d, d), full(1, d), full(1, 1)],
        out_shape=[SDS((n, d), F32), SDS((n, WIDTH), F32), SDS((n, WIDTH), F32), SDS((n, WIDTH), BF16), SDS((n, WIDTH), BF16),
                   SDS((n, 2 * d), BF16), SDS((WIDTH, d), F32), SDS((WIDTH, d), F32), SDS((d, d), F32), SDS((1, d), F32), SDS((1, 1), F32)],
        scratch_shapes=[pltpu.VMEM((1, d), F32)],
        compiler_params=_params(("arbitrary",)))(o_attn, o_rwkv, z_attn, z_rwkv, gm, x2, tgt, wua, wur, wout, g2)


def _mesh_pos():
    x, y, c = lax.axis_index("x"), lax.axis_index("y"), lax.axis_index("c")
    return 4 * x + 2 * y + c


def _coords(idx):
    return (idx // 4, (idx // 2) % 2, idx % 2)


def _exchange(srcs, to_all, name):
    n = len(srcs)

    def body(*refs):
        src_refs, dst_refs = refs[:n], refs[n:2 * n]
        send_sems, recv_sems, local_sems = refs[2 * n:]
        me = _mesh_pos()

        def piece(i, j):
            return src_refs[i] if to_all[i] else src_refs[i].at[j]

        def remote(i, off, peer, block, slot):
            return pltpu.make_async_remote_copy(src_ref=piece(i, block), dst_ref=dst_refs[i].at[slot],
                                                send_sem=send_sems.at[i, off - 1], recv_sem=recv_sems.at[i, off - 1],
                                                device_id=_coords(peer), device_id_type=MESH)

        local = [pltpu.make_async_copy(piece(i, me), dst_refs[i].at[me], local_sems.at[i]) for i in range(n)]
        for cp in local:
            cp.start()
        sends = []
        for off in range(1, N_DEV):
            to = (me + off) % N_DEV
            for i in range(n):
                sends.append(remote(i, off, to, to, me))
                sends[-1].start()
        for off in range(1, N_DEV):
            frm = (me + N_DEV - off) % N_DEV
            for i in range(n):
                remote(i, off, frm, me, frm).wait_recv()
        for cp in sends:
            cp.wait_send()
        for cp in local:
            cp.wait()

    outs = pl.pallas_call(
        body, name=name, in_specs=[pl.BlockSpec(memory_space=pltpu.HBM)] * n, out_specs=[pl.BlockSpec(memory_space=pltpu.HBM)] * n,
        out_shape=[SDS((N_DEV,) + s.shape[-2:], s.dtype) for s in srcs],
        scratch_shapes=[pltpu.SemaphoreType.DMA((n, N_DEV - 1)), pltpu.SemaphoreType.DMA((n, N_DEV - 1)), pltpu.SemaphoreType.DMA((n,))],
        compiler_params=pltpu.CompilerParams())(*srcs)
    return outs


_HBM = pl.BlockSpec(memory_space=pltpu.HBM)
_SEM = pl.BlockSpec(memory_space=pltpu.SEMAPHORE)
_EFFECT = pltpu.SideEffectType.DATAFLOW_SIDE_EFFECTING


def _send_copy(src_ref, land_ref, to_all, send_sems, recv_sems, i, off, block, slot, peer):
    k = i * (N_DEV - 1) + off - 1
    return pltpu.make_async_remote_copy(src_ref=src_ref if to_all else src_ref.at[block], dst_ref=land_ref.at[slot],
                                        send_sem=send_sems.at[k], recv_sem=recv_sems.at[k],
                                        device_id=_coords(peer), device_id_type=MESH)


def _send_start(srcs, to_all, name):
    n = len(srcs)

    def body(*refs):
        src_refs, land_refs = refs[:n], refs[n:2 * n]
        send_sems, recv_sems = refs[2 * n:2 * n + 2]
        me = _mesh_pos()
        for off in range(1, N_DEV):
            to = (me + off) % N_DEV
            for i in range(n):
                _send_copy(src_refs[i], land_refs[i], to_all, send_sems, recv_sems, i, off, to, me, to).start()
        refs[-1][...] = jnp.zeros_like(refs[-1])

    lands = [jnp.zeros((N_DEV,) + s.shape[-2:], s.dtype) for s in srcs]
    hbm = [pltpu.HBM(a.shape, a.dtype) for a in list(srcs) + lands]
    sems = pltpu.SemaphoreType.DMA((n * (N_DEV - 1),))
    outs = pl.pallas_call(
        body, name=name, out_shape=(sems, sems, *hbm, SDS((8, LANE), BF16)),
        in_specs=(_HBM,) * (2 * n), out_specs=(_SEM, _SEM) + (_HBM,) * (2 * n) + (pl.BlockSpec(memory_space=pltpu.VMEM),),
        input_output_aliases={i: 2 + i for i in range(2 * n)}, compiler_params=pltpu.CompilerParams(has_side_effects=_EFFECT),
    )(*[pltpu.with_memory_space_constraint(a, pltpu.HBM) for a in list(srcs) + lands])
    return outs[0], outs[1], outs[2:2 + n], outs[2 + n:2 + 2 * n], outs[-1]


def _send_wait(send_sems, recv_sems, srcs_thru, lands_thru, to_all, after, name):
    n = len(srcs_thru)

    def body(*refs):
        src_refs, land_refs = refs[:n], refs[n:2 * n]
        send_sems, recv_sems = refs[2 * n:2 * n + 2]
        me = _mesh_pos()
        for off in range(1, N_DEV):
            to, frm = (me + off) % N_DEV, (me + N_DEV - off) % N_DEV
            for i in range(n):
                _send_copy(src_refs[i], land_refs[i], to_all, send_sems, recv_sems, i, off, to, me, to).wait_send()
                _send_copy(src_refs[i], land_refs[i], to_all, send_sems, recv_sems, i, off, me, frm, frm).wait_recv()

    hbm = tuple(pltpu.HBM(a.shape, a.dtype) for a in list(srcs_thru) + list(lands_thru))
    outs = pl.pallas_call(
        body, name=name, out_shape=hbm, in_specs=(_HBM,) * (2 * n) + (_SEM, _SEM, pl.BlockSpec(memory_space=pl.ANY)),
        out_specs=(_HBM,) * (2 * n), input_output_aliases={i: i for i in range(2 * n)},
        compiler_params=pltpu.CompilerParams(has_side_effects=_EFFECT),
    )(*srcs_thru, *lands_thru, send_sems, recv_sems, after)
    return outs[n:]


def _gather(srcs, after, name):
    n = len(srcs)

    def body(*refs):
        src_refs, dst_refs = refs[:n], refs[n + 1:2 * n + 1]
        send_sems, recv_sems, local_sems = refs[2 * n + 1:]
        x, y, c = lax.axis_index("x"), lax.axis_index("y"), lax.axis_index("c")
        me, sibling = (x, y, c), (x, y, 1 - c)
        chips = [(1 - x, y), (x, 1 - y), (1 - x, 1 - y)]

        def slot(i, dev):
            return dst_refs[i].at[4 * dev[0] + 2 * dev[1] + dev[2]]

        def copy(i, k, block, to, own=False):
            return pltpu.make_async_remote_copy(src_ref=src_refs[i] if own else slot(i, block), dst_ref=slot(i, block),
                                                send_sem=send_sems.at[i, k], recv_sem=recv_sems.at[i, k],
                                                device_id=to, device_id_type=MESH)

        local = [pltpu.make_async_copy(src_refs[i], slot(i, me), local_sems.at[i]) for i in range(n)]
        for cp in local:
            cp.start()
        sends = []
        for i in range(n):
            sends.append(copy(i, 0, me, sibling, own=True))
            sends += [copy(i, 1 + j, me, (*chip, c), own=True) for j, chip in enumerate(chips)]
        for cp in sends:
            cp.start()
        for j, chip in enumerate(chips):
            for i in range(n):
                copy(i, 1 + j, (*chip, c), me).wait_recv()
                sends.append(copy(i, 4 + j, (*chip, c), sibling))
                sends[-1].start()
        for i in range(n):
            copy(i, 0, sibling, me).wait_recv()
            for j, chip in enumerate(chips):
                copy(i, 4 + j, (*chip, 1 - c), me).wait_recv()
        for cp in sends:
            cp.wait_send()
        for cp in local:
            cp.wait()

    return pl.pallas_call(
        body, name=name, in_specs=[pl.BlockSpec(memory_space=pltpu.HBM)] * n + [pl.BlockSpec(memory_space=pl.ANY)],
        out_specs=[pl.BlockSpec(memory_space=pltpu.HBM)] * n, out_shape=[SDS((N_DEV,) + s.shape, s.dtype) for s in srcs],
        scratch_shapes=[pltpu.SemaphoreType.DMA((n, N_DEV - 1)), pltpu.SemaphoreType.DMA((n, N_DEV - 1)), pltpu.SemaphoreType.DMA((n,))],
        compiler_params=pltpu.CompilerParams())(*srcs, after)


def _adamw(parts, w, m, v, tr, name, own=None):
    rows, cols = w.shape
    c1, c2 = 1.0 - ADAM_B1 ** ADAM_STEP, 1.0 - ADAM_B2 ** ADAM_STEP

    def body(p_ref, *refs):
        w_ref, m_ref, v_ref, g_ref, d_ref, nm_ref, nv_ref = refs[-7:]
        me = _mesh_pos()

        def part(j):
            return p_ref[j] if own is None else jnp.where(me == j, refs[0][...], p_ref[j])

        g = part(0).astype(F32)
        for j in range(1, N_DEV):
            g = g + part(j).astype(F32)
        nm = ADAM_B1 * m_ref[...] + (1.0 - ADAM_B1) * g
        nv = ADAM_B2 * v_ref[...] + (1.0 - ADAM_B2) * jnp.square(g)
        g_ref[...] = g
        nm_ref[...] = nm
        nv_ref[...] = nv
        d_ref[...] = -ADAM_LR * ((nm / c1) / (jnp.sqrt(nv / c2) + ADAM_EPS) + ADAM_WD * w_ref[...])

    t = pl.BlockSpec((tr, cols), lambda i: (i, 0))
    extra = [] if own is None else [own]
    return pl.pallas_call(
        body, name=name, grid=(rows // tr,), in_specs=[pl.BlockSpec((N_DEV, tr, cols), lambda i: (0, i, 0))] + [t] * (3 + len(extra)),
        out_specs=[t] * 4, out_shape=[SDS((rows, cols), F32)] * 4, compiler_params=_params(("parallel",)))(parts, *extra, w, m, v)


SHARDED = (("w_in", D_MODEL, IN_COLS // N_DEV, True, 128), ("w_up_attn", WIDTH, D_MODEL // N_DEV, True, WIDTH),
           ("w_up_rwkv", WIDTH, D_MODEL // N_DEV, True, WIDTH), ("w_out", D_MODEL // N_DEV, D_MODEL, False, D_MODEL // N_DEV),
           ("rwkv_w_up", LORA, WIDTH // N_DEV, True, LORA), ("rwkv_a_up", LORA, WIDTH // N_DEV, True, LORA))
LOSS_SLOT = sum(n for _, n in SMALL)


def _pack_small(small, extra=None):
    flat = [small[n].reshape(-1).astype(F32) for n, _ in SMALL]
    flat.append(jnp.zeros((1,), F32) if extra is None else extra.reshape(1))
    flat.append(jnp.zeros((SMALL_ROWS * LANE - LOSS_SLOT - 1,), F32))
    return jnp.concatenate(flat).reshape(SMALL_ROWS, LANE)


def _unpack_small(packed, shapes):
    flat = packed.reshape(-1)
    out, off = {}, 0
    for n, cnt in SMALL:
        out[n] = flat[off:off + cnt].reshape(shapes[n])
        off += cnt
    return out, flat[LOSS_SLOT]


def _whole(gathered, by_cols):
    if not by_cols:
        return gathered.reshape(-1, gathered.shape[-1])
    return gathered.transpose(1, 0, 2).reshape(gathered.shape[1], -1)


def _per_owner(full, by_cols):
    if not by_cols:
        return full.reshape(N_DEV, -1, full.shape[-1])
    return full.reshape(full.shape[0], N_DEV, -1).transpose(1, 0, 2)


def _local_step(x, loss_target, sm, wts):
    bsz, s, d = x.shape
    n = bsz * s
    x2, tgt = x.reshape(n, d), loss_target.reshape(n, d)
    bidx = jnp.asarray(_bucket_tables())
    w_in = wts["w_in"]
    segs = (("qkv", 0, QKV_COLS, 1536), ("za", OFF_ZA, WIDTH, 512), ("pr", OFF_PR, PR_COLS, PR_COLS), ("zr", OFF_ZR, WIDTH, 512),
            ("gm", OFF_GM, 2 * D_MODEL, 1024))

    h, rs = _prenorm(x2, sm["pre_norm_gain"])
    w_seg = {nm: w_in[:, off:off + cnt] for nm, off, cnt, _ in segs}
    proj = {nm: _mm(h, w_seg[nm], tn, "proj_" + nm) for nm, _, _, tn in segs}
    qkv3 = proj["qkv"].reshape(bsz, s, QKV_COLS)
    pr3 = proj["pr"].reshape(bsz, s, PR_COLS)

    o_attn, lse = _attn_fwd(qkv3, sm["rel_bias"], bidx)
    rk = sm["rwkv_r_k"].reshape(1, WIDTH)
    pre_args = (sm["rwkv_shift_mix"], sm["rwkv_w0"], wts["rwkv_w_up"], sm["rwkv_a0"], wts["rwkv_a_up"], sm["rwkv_k_k"], sm["rwkv_k_a"])
    scan_in = _rwkv_pre(pr3, *pre_args)
    o_rwkv, states, consts = _rwkv_scan(scan_in, rk, sm["rwkv_ln_w"], sm["rwkv_ln_b"])

    (dxo, do_attn, do_rwkv, dza, dzr, dgm, g_wua, g_wur, g_wout, g_post, loss) = _head(
        o_attn.reshape(n, WIDTH), o_rwkv.reshape(n, WIDTH), proj["za"], proj["zr"], proj["gm"], x2, tgt,
        wts["w_up_attn"], wts["w_up_rwkv"], wts["w_out"], sm["post_norm_gain"])

    dqkv, dbias = _attn_bwd(qkv3, o_attn, lse, do_attn.reshape(bsz, s, WIDTH), sm["rel_bias"], bidx)
    g_bias = _bias_grad(dbias, bidx)[:, :N_BUCKET].T

    scan_cots, (g_rk, g_lnw, g_lnb) = _rwkv_scan_bwd(scan_in, states, consts, do_rwkv.reshape(bsz, s, WIDTH), rk, sm["rwkv_ln_w"],
                                                     sm["rwkv_ln_b"])
    dprs, g_mix, g_w0, g_wup, g_a0, g_aup, g_kk, g_ka = _rwkv_pre_bwd(pr3, scan_cots, *pre_args)
    dpr = _shift_bwd(dprs, sm["rwkv_shift_mix"]).reshape(n, PR_COLS)

    dsegs = [(dqkv.reshape(9, n, WIDTH), 0, QKV_COLS, WIDTH), (dza, OFF_ZA, WIDTH, WIDTH), (dpr, OFF_PR, PR_COLS, PR_COLS),
             (dzr, OFF_ZR, WIDTH, WIDTH), (dgm, OFF_GM, 2 * D_MODEL, D_MODEL)]
    full = {"w_in": jnp.concatenate([_mm_tn(h, t, tn, "gw_in_%d" % j) for j, (t, _, _, tn) in enumerate(dsegs)], axis=1),
            "w_up_attn": g_wua, "w_up_rwkv": g_wur, "w_out": g_wout, "rwkv_w_up": g_wup, "rwkv_a_up": g_aup}
    blocks = [_per_owner(full[nm], by_cols).astype(BF16) for nm, _, _, by_cols, _ in SHARDED]
    me = 4 * lax.axis_index("x") + 2 * lax.axis_index("y") + lax.axis_index("c")
    own = [lax.dynamic_index_in_dim(b, me, 0, keepdims=False) for b in blocks]
    send_sems, recv_sems, blocks_thru, lands_thru, token = _send_start(blocks, False, "grads_start")
    dh = _mm_nt(dsegs[0][0], w_seg["qkv"], token, "dh_qkv")
    grad_x, g_pre = _dh_rest_prenorm_bwd([t for t, *_ in dsegs[1:]], [w_seg[nm] for nm in ("za", "pr", "zr", "gm")], dh, x2, rs,
                                         sm["pre_norm_gain"], dxo)
    landed = _send_wait(send_sems, recv_sems, blocks_thru, lands_thru, False, g_pre, "grads_wait")

    small = {"pre_norm_gain": g_pre, "rel_bias": g_bias, "rwkv_shift_mix": g_mix, "rwkv_w0": g_w0, "rwkv_a0": g_a0, "rwkv_k_k": g_kk,
             "rwkv_k_a": g_ka, "rwkv_r_k": g_rk, "rwkv_ln_w": g_lnw, "rwkv_ln_b": g_lnb, "post_norm_gain": g_post}
    return loss[0, 0], grad_x.reshape(bsz, s, d), (landed, own), small


def kernel(x, pre_norm_gain, w_in, rel_bias, rwkv_shift_mix, rwkv_w0, rwkv_w_up, rwkv_a0, rwkv_a_up, rwkv_k_k, rwkv_k_a, rwkv_r_k, rwkv_ln_w, rwkv_ln_b, w_up_attn, w_up_rwkv, w_out, post_norm_gain, loss_target, m_pre_norm_gain, m_w_in, m_rel_bias, m_rwkv_shift_mix, m_rwkv_w0, m_rwkv_w_up, m_rwkv_a0, m_rwkv_a_up, m_rwkv_k_k, m_rwkv_k_a, m_rwkv_r_k, m_rwkv_ln_w, m_rwkv_ln_b, m_w_up_attn, m_w_up_rwkv, m_w_out, m_post_norm_gain, v_pre_norm_gain, v_w_in, v_rel_bias, v_rwkv_shift_mix, v_rwkv_w0, v_rwkv_w_up, v_rwkv_a0, v_rwkv_a_up, v_rwkv_k_k, v_rwkv_k_a, v_rwkv_r_k, v_rwkv_ln_w, v_rwkv_ln_b, v_w_up_attn, v_w_up_rwkv, v_w_out, v_post_norm_gain):
    names = [n for n, *_ in SHARDED] + [n for n, _ in SMALL]
    loc = dict(locals())
    w = {n: loc[n] for n in names}
    m = {n: loc["m_" + n] for n in names}
    v = {n: loc["v_" + n] for n in names}
    shapes = {n: w[n].shape for n in names}
    order = ["pre_norm_gain", "w_in", "rel_bias", "rwkv_shift_mix", "rwkv_w0", "rwkv_w_up", "rwkv_a0", "rwkv_a_up", "rwkv_k_k", "rwkv_k_a",
             "rwkv_r_k", "rwkv_ln_w", "rwkv_ln_b", "w_up_attn", "w_up_rwkv", "w_out", "post_norm_gain"]
    shard2d = lambda t, n, r, c: t[n].reshape(r, c)

    shards = [shard2d(w, n, r, c).astype(BF16) for n, r, c, _, _ in SHARDED]
    send_sems, recv_sems, srcs_thru, lands_thru, token = _send_start(shards[1:], True, "weights_start")
    gathered = list(_gather(shards[:1], token, "gather_weights"))
    landed = _send_wait(send_sems, recv_sems, srcs_thru, lands_thru, True, gathered[0], "weights_wait")
    me = 4 * lax.axis_index("x") + 2 * lax.axis_index("y") + lax.axis_index("c")
    gathered += [lax.dynamic_update_index_in_dim(g, sh, me, 0) for g, sh in zip(landed, shards[1:])]
    wts = {n: _whole(g, by_cols) for (n, _, _, by_cols, _), g in zip(SHARDED, gathered)}

    loss, grad_x, (landed, own), small = _local_step(x, loss_target, w, wts)
    small_block = _pack_small(small, loss)
    send_sems, recv_sems, srcs_thru, lands_thru, _ = _send_start([small_block], True, "small_start")

    outs = [{}, {}, {}, {}]
    for (n, r, c, _, tr), p, o_ in zip(SHARDED, landed, own):
        res = _adamw(p, shard2d(w, n, r, c), shard2d(m, n, r, c), shard2d(v, n, r, c), tr, "adamw_" + n, own=o_)
        for o, t in zip(outs, res):
            o[n] = t.reshape(shapes[n])
    (small_parts,) = _send_wait(send_sems, recv_sems, srcs_thru, lands_thru, True, res[0], "small_wait")
    res = _adamw(small_parts, _pack_small(w), _pack_small(m), _pack_small(v), SMALL_ROWS, "adamw_small", own=small_block)
    for o, t in zip(outs, res):
        o.update(_unpack_small(t, shapes)[0])
    loss = _unpack_small(res[0], shapes)[1]
    return (loss, grad_x, *[o[n] for o in outs for n in order])
```

```python
import functools
import math

import numpy as np
import jax
import jax.numpy as jnp
from jax import lax
from jax.experimental import pallas as pl
from jax.experimental.pallas import tpu as pltpu

F32, BF16 = jnp.float32, jnp.bfloat16
SDS = jax.ShapeDtypeStruct
MESH = pl.DeviceIdType.MESH

N_DEV = 8
D_MODEL = 1024
HEAD = 64
N_HEAD = 8
WIDTH = N_HEAD * HEAD
DILATIONS = (1, 4, 16)
QB = 128
N_BUCKET = 32
MAX_DIST = 2048
LORA = 64
QKV_COLS = 9 * WIDTH
PR_COLS = 3 * WIDTH + 2 * LORA
IN_COLS = QKV_COLS + WIDTH + PR_COLS + WIDTH + 2 * D_MODEL
OFF_ZA, OFF_PR, OFF_ZR, OFF_GM = QKV_COLS, QKV_COLS + WIDTH, QKV_COLS + WIDTH + PR_COLS, QKV_COLS + 2 * WIDTH + PR_COLS
RMS_EPS = 1e-6
GN_EPS = 64e-5
SCALE = 1.0 / math.sqrt(HEAD)
CHUNK = 64
CHUNK_GROUP = 32
BWD_GROUP = 16
EARLY = 8
NEG = -1e30
LANE = 128

ADAM_LR, ADAM_B1, ADAM_B2, ADAM_EPS, ADAM_WD, ADAM_STEP = 0.001, 0.9, 0.999, 1e-08, 0.01, 10

VMEM_LIMIT = 56 * 1024 * 1024
MM_ROWS = 2048

SMALL = (("pre_norm_gain", 1024), ("rel_bias", 768), ("rwkv_shift_mix", 1664), ("rwkv_w0", 512), ("rwkv_a0", 512),
         ("rwkv_k_k", 512), ("rwkv_k_a", 512), ("rwkv_r_k", 512), ("rwkv_ln_w", 512), ("rwkv_ln_b", 512),
         ("post_norm_gain", 1024))
SMALL_ROWS = 64


def _params(sem=None):
    return pltpu.CompilerParams(dimension_semantics=sem, vmem_limit_bytes=VMEM_LIMIT)


def _dot(a, b):
    return jnp.dot(a, b, preferred_element_type=F32)


def _dot_nt(a, b):
    return lax.dot_general(a, b, (((1,), (1,)), ((), ())), preferred_element_type=F32)


def _dot_tn(a, b):
    return lax.dot_general(a, b, (((0,), (0,)), ((), ())), preferred_element_type=F32)


@jax.custom_vjp
def _bdot(a, b):
    return _dot(a.astype(BF16), b.astype(BF16))


def _bdot_fwd(a, b):
    return _bdot(a, b), (a, b)


def _bdot_bwd(res, g):
    a, b = res
    gb = g.astype(BF16)
    return _dot_nt(gb, b.astype(BF16)), _dot_tn(a.astype(BF16), gb)


_bdot.defvjp(_bdot_fwd, _bdot_bwd)


def _silu(z):
    return z * jax.nn.sigmoid(z)


def _dsilu(z):
    s = jax.nn.sigmoid(z)
    return s * (1.0 + z * (1.0 - s))


def _softplus(x):
    return jnp.maximum(x, 0.0) + jnp.log(1.0 + jnp.exp(-jnp.abs(x)))


def _bucket_tables():
    qi = np.arange(QB)[:, None] + QB
    ki = np.arange(2 * QB)[None, :]
    rel = np.maximum(qi - ki, 0)
    out = []
    for d in DILATIONS:
        dist = rel * d
        max_exact = N_BUCKET // 2
        ratio = np.log(np.maximum(dist, 1).astype(np.float32) / max_exact) / np.float32(math.log(MAX_DIST / max_exact))
        large = max_exact + (ratio * (N_BUCKET - max_exact)).astype(np.int32)
        large = np.minimum(large, N_BUCKET - 1)
        out.append(np.where(dist < max_exact, dist, large).astype(np.int32))
    return np.stack(out)


def _prenorm(x2, g):
    n, d = x2.shape
    tm = 1024

    def body(x_ref, g_ref, h_ref, rs_ref):
        x = x_ref[...]
        rs = lax.rsqrt(jnp.mean(x * x, axis=-1, keepdims=True) + RMS_EPS)
        h_ref[...] = (x * rs * g_ref[...]).astype(BF16)
        rs_ref[...] = rs

    return pl.pallas_call(
        body, name="prenorm", grid=(n // tm,),
        in_specs=[pl.BlockSpec((tm, d), lambda i: (i, 0)), pl.BlockSpec((1, d), lambda i: (0, 0))],
        out_specs=[pl.BlockSpec((tm, d), lambda i: (i, 0)), pl.BlockSpec((tm, 1), lambda i: (i, 0))],
        out_shape=[SDS((n, d), BF16), SDS((n, 1), F32)], compiler_params=_params(("parallel",)))(x2, g)


def _mm(a, b, tn, name):
    m, k = a.shape
    n = b.shape[1]
    tm = MM_ROWS

    def body(a_ref, b_ref, o_ref):
        o_ref[...] = _dot(a_ref[...], b_ref[...])

    return pl.pallas_call(
        body, name=name, grid=(n // tn, m // tm),
        in_specs=[pl.BlockSpec((tm, k), lambda j, i: (i, 0)), pl.BlockSpec((k, tn), lambda j, i: (0, j))],
        out_specs=pl.BlockSpec((tm, tn), lambda j, i: (i, j)),
        out_shape=SDS((m, n), F32), compiler_params=_params(("parallel", "parallel")))(a, b)


def _mm_nt(a, b, after, name):
    m, seg = a.shape[1], a.shape[2]
    d, k = b.shape
    tm = MM_ROWS
    per = 3
    tk = per * seg

    def body(a_ref, b_ref, after_ref, o_ref):
        r = sum(_dot_nt(a_ref[j].astype(BF16), b_ref[:, seg * j:seg * (j + 1)]) for j in range(per))

        @pl.when(pl.program_id(1) == 0)
        def _():
            o_ref[...] = r

        @pl.when(pl.program_id(1) != 0)
        def _():
            o_ref[...] += r

    in_specs = [pl.BlockSpec((per, tm, seg), lambda i, j: (j, i, 0)), pl.BlockSpec((d, tk), lambda i, j: (0, j)),
                pl.BlockSpec(after.shape, lambda i, j: (0, 0))]
    return pl.pallas_call(
        body, name=name, grid=(m // tm, k // tk), in_specs=in_specs, out_specs=pl.BlockSpec((tm, d), lambda i, j: (i, 0)),
        out_shape=SDS((m, d), F32), compiler_params=_params(("parallel", "arbitrary")))(a, b, after)


def _dh_rest_prenorm_bwd(a_list, b_list, acc, x2, rs, g1, dxo):
    m, d = acc.shape
    tm = 512
    n = len(a_list)

    def body(*refs):
        x_ref, rs_ref, g_ref, dxo_ref, gx_ref, dg_ref = refs[2 * n + 1:]
        dh = refs[2 * n][...]
        for a_ref, b_ref in zip(refs[:n], refs[n:2 * n]):
            dh = dh + _dot_nt(a_ref[...].astype(BF16), b_ref[...])
        x, r = x_ref[...], rs_ref[...]
        gd = dh * g_ref[...]
        gx_ref[...] = dxo_ref[...] + r * (gd - x * (r * r) * jnp.mean(gd * x, axis=-1, keepdims=True))
        dg = jnp.sum(dh * x * r, axis=0, keepdims=True)

        @pl.when(pl.program_id(0) == 0)
        def _():
            dg_ref[...] = dg

        @pl.when(pl.program_id(0) != 0)
        def _():
            dg_ref[...] += dg

    t = pl.BlockSpec((tm, d), lambda i: (i, 0))
    in_specs = [pl.BlockSpec((tm, a.shape[1]), lambda i: (i, 0)) for a in a_list]
    in_specs += [pl.BlockSpec(b.shape, lambda i: (0, 0)) for b in b_list]
    in_specs += [t, t, pl.BlockSpec((tm, 1), lambda i: (i, 0)), pl.BlockSpec((1, d), lambda i: (0, 0)), t]
    return pl.pallas_call(
        body, name="dh_rest_prenorm_bwd", grid=(m // tm,), in_specs=in_specs, out_specs=[t, pl.BlockSpec((1, d), lambda i: (0, 0))],
        out_shape=[SDS((m, d), F32), SDS((1, d), F32)], compiler_params=_params(("arbitrary",)))(*a_list, *b_list, acc, x2, rs, g1, dxo)


def _mm_tn(a, b, tn, name):
    split = b.ndim == 3
    m, k1 = a.shape
    per = 3 if split else 1
    seg = b.shape[2] if split else tn
    tn = per * seg
    n2 = b.shape[0] * seg if split else b.shape[1]
    tm = MM_ROWS

    def body(a_ref, b_ref, o_ref):
        first = pl.program_id(1) == 0
        for j in range(per):
            r = _dot_tn(a_ref[...], (b_ref[j] if split else b_ref[...]).astype(BF16))
            cols = slice(seg * j, seg * (j + 1))

            @pl.when(first)
            def _(r=r, cols=cols):
                o_ref[:, cols] = r

            @pl.when(jnp.logical_not(first))
            def _(r=r, cols=cols):
                o_ref[:, cols] += r

    b_spec = pl.BlockSpec((per, tm, seg), lambda j, i: (j, i, 0)) if split else pl.BlockSpec((tm, tn), lambda j, i: (i, j))
    return pl.pallas_call(
        body, name=name, grid=(n2 // tn, m // tm),
        in_specs=[pl.BlockSpec((tm, k1), lambda j, i: (i, 0)), b_spec],
        out_specs=pl.BlockSpec((k1, tn), lambda j, i: (0, j)),
        out_shape=SDS((k1, n2), F32), compiler_params=_params(("parallel", "arbitrary")))(a, b)


def _ds(start, d):
    return pl.ds(start, QB) if d == 1 else pl.ds(start, QB, stride=d)


def _fill_bias(tab_ref, bidx_ref, bias_sc, hp):
    for g in range(3):
        bi = bidx_ref[g]
        for h in range(2):
            acc = jnp.zeros((QB, 2 * QB), F32)
            for j in range(N_BUCKET):
                acc = jnp.where(bi == j, tab_ref[j, g * N_HEAD + hp * 2 + h], acc)
            bias_sc[g * 2 + h] = acc


def _block_starts(it, d, nb):
    rho = it // nb
    n = it % nb
    st = rho + d * QB * n
    stp = rho + d * QB * jnp.maximum(n - 1, 0)
    if d == 1:
        st, stp = pl.multiple_of(QB * it, QB), pl.multiple_of(QB * jnp.maximum(it - 1, 0), QB)
    return st, stp, n > 0


ATTN_BLOCKS_FWD = 8
ATTN_BLOCKS_BWD = 4


def _bdot3(a, b, dims):
    return lax.dot_general(a, b, (dims, ((0,), (0,))), preferred_element_type=F32)


def _attn_operands(q_ref, k_ref, v_ref, bias_sc, g, d, nb, it0, nblk):
    two = nb > 1
    nk = 2 * QB if two else QB
    ii = lax.broadcasted_iota(jnp.int32, (QB, nk), 0)
    cc = lax.broadcasted_iota(jnp.int32, (QB, nk), 1)
    qs, ks, vs, pens, starts = [], [], [], [], []
    for u in range(nblk):
        st, stp, hasprev = _block_starts(it0 + u, d, nb)
        qf = q_ref[0, _ds(st, d), :]
        if two:
            kf = jnp.concatenate([k_ref[0, _ds(stp, d), :], k_ref[0, _ds(st, d), :]], axis=0).astype(BF16)
            vf = jnp.concatenate([v_ref[0, _ds(stp, d), :], v_ref[0, _ds(st, d), :]], axis=0).astype(BF16)
            own = jnp.logical_and(cc >= QB, ii >= cc - QB)
            prev = jnp.logical_and(jnp.logical_and(cc < QB, cc >= ii), hasprev)
            pen = jnp.where(jnp.logical_or(own, prev), 0.0, NEG)
        else:
            kf, vf = k_ref[0, _ds(st, d), :].astype(BF16), v_ref[0, _ds(st, d), :].astype(BF16)
            pen = jnp.where(ii >= cc, 0.0, NEG)
        for h in range(2):
            qs.append(_one_head(qf, h).astype(BF16))
            ks.append(kf)
            vs.append(vf)
            pens.append(pen + (bias_sc[g * 2 + h] if two else bias_sc[g * 2 + h, :, QB:2 * QB]))
        starts.append((st, stp))
    return _stack(qs), _stack(ks), _stack(vs), _stack(pens), starts


def _one_head(x, h):
    lane = lax.broadcasted_iota(jnp.int32, x.shape, 1)
    return jnp.where(lane >= HEAD if h == 1 else lane < HEAD, x, 0.0)


def _pick_heads(x, u):
    lane = lax.broadcasted_iota(jnp.int32, x.shape[1:], 1)
    return jnp.where(lane < HEAD, x[2 * u], x[2 * u + 1])


def _add_heads(x, u):
    return x[2 * u] + x[2 * u + 1]


def _attn_fwd(qkv3, rel_bias, bidx):
    bsz, s, _ = qkv3.shape
    rt = 256

    def body(tab_ref, bidx_ref, *refs):
        q_refs, k_refs, v_refs = refs[0:3], refs[3:6], refs[6:9]
        o_ref, lse_ref = refs[9:11]
        bias_sc, num_sc, den_sc, m_sc = refs[11:]
        pl.when(pl.program_id(1) == 0)(lambda: _fill_bias(tab_ref, bidx_ref, bias_sc, pl.program_id(0)))
        for g, d in enumerate(DILATIONS):
            nb = s // (QB * d)

            def blk(it, c, g=g, d=d, nb=nb):
                q, k, v, bias, starts = _attn_operands(q_refs[g], k_refs[g], v_refs[g], bias_sc, g, d, nb, it * ATTN_BLOCKS_FWD,
                                                       ATTN_BLOCKS_FWD)
                sc = _bdot3(q, k, ((2,), (2,))) * SCALE + bias
                m = jnp.max(sc, axis=-1, keepdims=True)
                p = jnp.exp(sc - m)
                den = jnp.sum(p, axis=-1, keepdims=True)
                num = _bdot3(p.astype(BF16), v, ((2,), (1,)))
                den, m = jnp.broadcast_to(den, num.shape), jnp.broadcast_to(m, num.shape)
                for u, (st, _) in enumerate(starts):
                    num_sc[g, _ds(st, d), :] = _pick_heads(num, u)
                    den_sc[g, _ds(st, d), :] = _pick_heads(den, u)
                    m_sc[g, _ds(st, d), :] = _pick_heads(m, u)
                return c

            lax.fori_loop(0, s // QB // ATTN_BLOCKS_FWD, blk, 0)

        def merge(i, c):
            rows = pl.ds(pl.multiple_of(i * rt, rt), rt)
            m0, m1, m2 = m_sc[0, rows, :], m_sc[1, rows, :], m_sc[2, rows, :]
            mall = jnp.maximum(jnp.maximum(m0, m1), m2)
            w0, w1, w2 = jnp.exp(m0 - mall), jnp.exp(m1 - mall), jnp.exp(m2 - mall)
            num = w0 * num_sc[0, rows, :] + w1 * num_sc[1, rows, :] + w2 * num_sc[2, rows, :]
            den = w0 * den_sc[0, rows, :] + w1 * den_sc[1, rows, :] + w2 * den_sc[2, rows, :]
            o_ref[0, rows, :] = num / den
            lse_ref[0, rows, :] = mall + jnp.log(den)
            return c

        lax.fori_loop(0, s // rt, merge, 0)

    col = lambda w, g: (lambda hp, b: (b, 0, (w * 3 + g) * 4 + hp))
    in_specs = [pl.BlockSpec(memory_space=pltpu.SMEM), pl.BlockSpec((3, QB, 2 * QB), lambda hp, b: (0, 0, 0))]
    in_specs += [pl.BlockSpec((1, s, LANE), col(w, g)) for w in range(3) for g in range(3)]
    out_spec = pl.BlockSpec((1, s, LANE), lambda hp, b: (b, 0, hp))
    return pl.pallas_call(
        body, name="attn_fwd", grid=(4, bsz), in_specs=in_specs, out_specs=[out_spec, out_spec],
        out_shape=[SDS((bsz, s, WIDTH), F32), SDS((bsz, s, WIDTH), F32)],
        scratch_shapes=[pltpu.VMEM((6, QB, 2 * QB), F32), pltpu.VMEM((3, s, LANE), F32), pltpu.VMEM((3, s, LANE), F32),
                        pltpu.VMEM((3, s, LANE), F32)],
        compiler_params=_params(("arbitrary", "arbitrary")))(rel_bias, bidx, *([qkv3] * 9))


def _attn_bwd(qkv3, o3, lse3, do3, rel_bias, bidx):
    bsz, s, _ = qkv3.shape
    rt = 256

    def body(tab_ref, bidx_ref, *refs):
        q_refs, k_refs, v_refs = refs[0:3], refs[3:6], refs[6:9]
        o_ref, lse_ref, do_ref, dqkv_ref, db_ref, bias_sc, delta_sc, acc_sc = refs[9:]
        dq_refs, dk_refs, dv_refs = ([acc_sc.at[w * 3 + g] for g in range(3)] for w in range(3))

        @pl.when(pl.program_id(1) == 0)
        def _():
            _fill_bias(tab_ref, bidx_ref, bias_sc, pl.program_id(0))
            db_ref[...] = jnp.zeros_like(db_ref)

        def prep(i, c):
            rows = pl.ds(pl.multiple_of(i * rt, rt), rt)
            prod = do_ref[0, rows, :] * o_ref[0, rows, :]
            d0 = jnp.sum(prod[:, :HEAD], axis=-1, keepdims=True)
            d1 = jnp.sum(prod[:, HEAD:], axis=-1, keepdims=True)
            delta_sc[rows, :] = jnp.concatenate([jnp.broadcast_to(d0, (rt, HEAD)), jnp.broadcast_to(d1, (rt, HEAD))], axis=1)
            z = jnp.zeros((rt, LANE), F32)
            for g in range(3):
                dk_refs[g][0, rows, :] = z
                dv_refs[g][0, rows, :] = z
            return c

        lax.fori_loop(0, s // rt, prep, 0)
        for g, d in enumerate(DILATIONS):
            nb = s // (QB * d)

            def blk(it, c, g=g, d=d, nb=nb):
                q, k, v, bias, starts = _attn_operands(q_refs[g], k_refs[g], v_refs[g], bias_sc, g, d, nb, it * ATTN_BLOCKS_BWD,
                                                       ATTN_BLOCKS_BWD)
                dos, lses, deltas = [], [], []
                for st, _ in starts:
                    dof, lsef, delf = do_ref[0, _ds(st, d), :], lse_ref[0, _ds(st, d), :], delta_sc[_ds(st, d), :]
                    for h in range(2):
                        dos.append(_one_head(dof, h).astype(BF16))
                        lses.append(lsef[:, HEAD * h:HEAD * h + 1])
                        deltas.append(delf[:, HEAD * h:HEAD * h + 1])
                do, lse, delta = _stack(dos), _stack(lses), _stack(deltas)
                p = jnp.exp(_bdot3(q, k, ((2,), (2,))) * SCALE + bias - lse)
                dv = _bdot3(p.astype(BF16), do, ((1,), (1,)))
                ds = p * (_bdot3(do, v, ((2,), (2,))) - delta)
                dsb = ds.astype(BF16)
                dq = _bdot3(dsb, k, ((2,), (1,))) * SCALE
                dk = _bdot3(dsb, q, ((1,), (1,))) * SCALE
                two = nb > 1
                for h in range(2):
                    dsum = sum(ds[2 * u + h] for u in range(ATTN_BLOCKS_BWD))
                    if two:
                        db_ref[0, g * 2 + h] += dsum
                    else:
                        db_ref[0, g * 2 + h, :, QB:2 * QB] += dsum
                for u, (st, stp) in enumerate(starts):
                    dq_refs[g][0, _ds(st, d), :] = _pick_heads(dq, u)
                    if two:
                        dk_refs[g][0, _ds(stp, d), :] += _add_heads(dk[:, :QB], u)
                        dv_refs[g][0, _ds(stp, d), :] += _add_heads(dv[:, :QB], u)
                    dk_refs[g][0, _ds(st, d), :] += _add_heads(dk[:, QB:] if two else dk, u)
                    dv_refs[g][0, _ds(st, d), :] += _add_heads(dv[:, QB:] if two else dv, u)
                return c

            lax.fori_loop(0, s // QB // ATTN_BLOCKS_BWD, blk, 0)

        def flush(i, c):
            rows = pl.ds(pl.multiple_of(i * rt, rt), rt)
            for j in range(9):
                dqkv_ref[j, 0, rows, :] = acc_sc[j, 0, rows, :].astype(BF16)
            return c

        lax.fori_loop(0, s // rt, flush, 0)

    col = lambda w, g: (lambda hp, b: (b, 0, (w * 3 + g) * 4 + hp))
    blk_spec = pl.BlockSpec((1, s, LANE), lambda hp, b: (b, 0, hp))
    in_specs = [pl.BlockSpec(memory_space=pltpu.SMEM), pl.BlockSpec((3, QB, 2 * QB), lambda hp, b: (0, 0, 0))]
    in_specs += [pl.BlockSpec((1, s, LANE), col(w, g)) for w in range(3) for g in range(3)]
    in_specs += [blk_spec] * 3
    out_specs = [pl.BlockSpec((9, 1, s, LANE), lambda hp, b: (0, b, 0, hp)), pl.BlockSpec((1, 6, QB, 2 * QB), lambda hp, b: (hp, 0, 0, 0))]
    out_shape = [SDS((9, bsz, s, WIDTH), BF16), SDS((4, 6, QB, 2 * QB), F32)]
    return pl.pallas_call(
        body, name="attn_bwd", grid=(4, bsz), in_specs=in_specs, out_specs=out_specs, out_shape=out_shape,
        scratch_shapes=[pltpu.VMEM((6, QB, 2 * QB), F32), pltpu.VMEM((s, LANE), F32), pltpu.VMEM((9, 1, s, LANE), F32)],
        compiler_params=_params(("parallel", "arbitrary")))(rel_bias, bidx, *([qkv3] * 9), o3, lse3, do3)


def _bias_grad(dbias, bidx):
    def body(db_ref, bidx_ref, o_ref):
        lane = lax.broadcasted_iota(jnp.int32, (1, LANE), 1)
        for g in range(3):
            bi = bidx_ref[g]
            for hp in range(4):
                for h in range(2):
                    mat = db_ref[hp, g * 2 + h]
                    row = jnp.zeros((1, LANE), F32)
                    for j in range(N_BUCKET):
                        part = jnp.sum(jnp.where(bi == j, mat, 0.0), axis=0, keepdims=True)
                        row = jnp.where(lane == j, jnp.sum(part, axis=1, keepdims=True), row)
                    hd = g * N_HEAD + hp * 2 + h
                    o_ref[hd:hd + 1, :] = row

    return pl.pallas_call(body, name="bias_grad", out_shape=SDS((3 * N_HEAD, LANE), F32), compiler_params=_params())(dbias, bidx)


def _pre_fn(r, k0, v, wl, al, w0, wup, a0, aup, kk_, ka_):
    u = w0 + _bdot(jnp.tanh(wl), wup)
    lw = -jnp.exp(-_softplus(-u) - 0.5)
    a = jax.nn.sigmoid(a0 + _bdot(al, aup))
    kkraw = k0 * kk_
    k = k0 * (1.0 + (a - 1.0) * ka_)
    return r, lw, k, v, kkraw, a


PRE_SPLIT = (0, WIDTH, 2 * WIDTH, 3 * WIDTH, 3 * WIDTH + LORA, 3 * WIDTH + 2 * LORA)


def _pre_pieces(prs):
    return [prs[:, a:b] for a, b in zip(PRE_SPLIT[:-1], PRE_SPLIT[1:])]


PRE_TT = 512


def _shifted(pr_ref, edge_ref, first, back):
    pr = pr_ref[0]
    tt = pr.shape[0]
    row = lax.broadcasted_iota(jnp.int32, (tt, 1), 0)
    if back:
        edge = jnp.where(first, 0.0, edge_ref[0, 7:8, :])
        return jnp.where(row == 0, edge, pltpu.roll(pr, 1, axis=0))
    edge = jnp.where(first, 0.0, edge_ref[0, 0:1, :])
    return jnp.where(row == tt - 1, edge, pltpu.roll(pr, tt - 1, axis=0))


def _rwkv_pre(pr3, mix, w0, wup, a0, aup, kk_, ka_):
    bsz, s, _ = pr3.shape
    tt = PRE_TT

    def body(pr_ref, edge_ref, mix_ref, w0_ref, wup_ref, a0_ref, aup_ref, kk_ref, ka_ref, *outs):
        pr = pr_ref[0]
        prev = _shifted(pr_ref, edge_ref, pl.program_id(1) == 0, True)
        prs = pr + (prev - pr) * mix_ref[...]
        vals = _pre_fn(*_pre_pieces(prs), w0_ref[...], wup_ref[...].astype(F32), a0_ref[...], aup_ref[...].astype(F32), kk_ref[...],
                       ka_ref[...])
        for o, val in zip(outs, vals):
            o[0] = val

    vec = lambda n: pl.BlockSpec((1, n), lambda b, i: (0, 0))
    mat = pl.BlockSpec((LORA, WIDTH), lambda b, i: (0, 0))
    in_specs = [pl.BlockSpec((1, tt, PR_COLS), lambda b, i: (b, i, 0)),
                pl.BlockSpec((1, 8, PR_COLS), lambda b, i: (b, jnp.maximum(i * (tt // 8) - 1, 0), 0)),
                vec(PR_COLS), vec(WIDTH), mat, vec(WIDTH), mat, vec(WIDTH), vec(WIDTH)]
    out_spec = pl.BlockSpec((1, tt, WIDTH), lambda b, i: (b, i, 0))
    return pl.pallas_call(
        body, name="rwkv_pre", grid=(bsz, s // tt), in_specs=in_specs, out_specs=[out_spec] * 6,
        out_shape=[SDS((bsz, s, WIDTH), F32)] * 6, compiler_params=_params(("parallel", "parallel")))(
            pr3, pr3, mix, w0, wup, a0, aup, kk_, ka_)


def _rwkv_pre_bwd(pr3, cots, mix, w0, wup, a0, aup, kk_, ka_):
    bsz, s, _ = pr3.shape
    tt = PRE_TT

    def body(pr_ref, edge_ref, c0, c1, c2, c3, c4, c5, mix_ref, w0_ref, wup_ref, a0_ref, aup_ref, kk_ref, ka_ref,
             dprs_ref, dmix_ref, dw0_ref, dwup_ref, da0_ref, daup_ref, dkk_ref, dka_ref):
        pr = pr_ref[0]
        prev = _shifted(pr_ref, edge_ref, pl.program_id(1) == 0, True)
        prs = pr + (prev - pr) * mix_ref[...]
        _, vjp = jax.vjp(_pre_fn, *_pre_pieces(prs), w0_ref[...], wup_ref[...].astype(F32), a0_ref[...], aup_ref[...].astype(F32),
                         kk_ref[...], ka_ref[...])
        grads = vjp(tuple(c[0] for c in (c0, c1, c2, c3, c4, c5)))
        for piece, a, b in zip(grads[:5], PRE_SPLIT[:-1], PRE_SPLIT[1:]):
            dprs_ref[0, :, a:b] = piece
        dw0, dwup, da0, daup, dkk, dka = grads[5:]
        dprs = dprs_ref[0]
        grads = (jnp.sum(dprs * (prev - pr), axis=0, keepdims=True), dw0, dwup, da0, daup, dkk, dka)
        refs = (dmix_ref, dw0_ref, dwup_ref, da0_ref, daup_ref, dkk_ref, dka_ref)
        first = jnp.logical_and(pl.program_id(0) == 0, pl.program_id(1) == 0)

        @pl.when(first)
        def _():
            for r_, g_ in zip(refs, grads):
                r_[...] = g_

        @pl.when(jnp.logical_not(first))
        def _():
            for r_, g_ in zip(refs, grads):
                r_[...] += g_

    vec = lambda n: pl.BlockSpec((1, n), lambda b, i: (0, 0))
    mat = pl.BlockSpec((LORA, WIDTH), lambda b, i: (0, 0))
    tile = pl.BlockSpec((1, tt, WIDTH), lambda b, i: (b, i, 0))
    in_specs = [pl.BlockSpec((1, tt, PR_COLS), lambda b, i: (b, i, 0)),
                pl.BlockSpec((1, 8, PR_COLS), lambda b, i: (b, jnp.maximum(i * (tt // 8) - 1, 0), 0))]
    in_specs += [tile] * 6 + [vec(PR_COLS), vec(WIDTH), mat, vec(WIDTH), mat, vec(WIDTH), vec(WIDTH)]
    out_specs = [pl.BlockSpec((1, tt, PR_COLS), lambda b, i: (b, i, 0)), vec(PR_COLS), vec(WIDTH), mat, vec(WIDTH), mat,
                 vec(WIDTH), vec(WIDTH)]
    out_shape = [SDS((bsz, s, PR_COLS), F32), SDS((1, PR_COLS), F32), SDS((1, WIDTH), F32), SDS((LORA, WIDTH), F32),
                 SDS((1, WIDTH), F32), SDS((LORA, WIDTH), F32), SDS((1, WIDTH), F32), SDS((1, WIDTH), F32)]
    return pl.pallas_call(
        body, name="rwkv_pre_bwd", grid=(bsz, s // tt), in_specs=in_specs, out_specs=out_specs, out_shape=out_shape,
        compiler_params=_params(("arbitrary", "arbitrary")))(pr3, pr3, *cots, mix, w0, wup, a0, aup, kk_, ka_)


def _shift_bwd(dprs3, mix):
    bsz, s, _ = dprs3.shape
    tt = PRE_TT
    nt = s // tt

    def body(d_ref, edge_ref, mix_ref, o_ref):
        nxt = _shifted(d_ref, edge_ref, pl.program_id(1) == nt - 1, False)
        m = mix_ref[...]
        o_ref[0] = (d_ref[0] * (1.0 - m) + nxt * m).astype(BF16)

    in_specs = [pl.BlockSpec((1, tt, PR_COLS), lambda b, i: (b, i, 0)),
                pl.BlockSpec((1, 8, PR_COLS), lambda b, i: (b, jnp.minimum((i + 1) * (tt // 8), s // 8 - 1), 0)),
                pl.BlockSpec((1, PR_COLS), lambda b, i: (0, 0))]
    return pl.pallas_call(
        body, name="shift_bwd", grid=(bsz, nt), in_specs=in_specs, out_specs=pl.BlockSpec((1, tt, PR_COLS), lambda b, i: (b, i, 0)),
        out_shape=SDS((bsz, s, PR_COLS), BF16), compiler_params=_params(("parallel", "parallel")))(dprs3, dprs3, mix)


_NN, _NT, _TN = ((2,), (1,)), ((2,), (2,)), ((1,), (1,))


def _dot3_bf16(a, b, dims):
    return lax.dot_general(a.astype(BF16), b.astype(BF16), (dims, ((0,), (0,))), preferred_element_type=F32)


class _Dots:
    def __init__(self, fwd):
        def make(dims, da_rule, db_rule):
            @jax.custom_vjp
            def f(a, b):
                return fwd(a, b, dims)

            f.defvjp(lambda a, b: (f(a, b), (a, b)), lambda res, g: (da_rule(*res, g), db_rule(*res, g)))
            return f

        one = _dot3_bf16
        self.mm = make(_NN, lambda a, b, g: one(g, b, _NT), lambda a, b, g: one(a, g, _TN))
        self.mm_nt = make(_NT, lambda a, b, g: one(g, b, _NN), lambda a, b, g: one(g, a, _TN))
        self.mm_tn = make(_TN, lambda a, b, g: one(b, g, _NT), lambda a, b, g: one(a, g, _NN))

        def powers(aab):
            ps = [aab]
            while 2 ** len(ps) < aab.shape[1]:
                ps.append(fwd(ps[-1], ps[-1], _NN))
            return ps

        def apply(ps, z, dims):
            for p in ps:
                z = z + fwd(p, z, dims)
            return z

        @jax.custom_vjp
        def solve(aab, z):
            return apply(powers(aab), z, _NN)

        def solve_fwd(aab, z):
            ps = powers(aab)
            x = apply(ps, z, _NN)
            return x, (ps, x)

        def solve_bwd(res, g):
            ps, x = res
            dz = apply(ps, g, _TN)
            return fwd(dz, x, _NT), dz

        solve.defvjp(solve_fwd, solve_bwd)
        self.solve = solve


_ONE_PASS = _Dots(_dot3_bf16)
_bmm, _bmm_tn = _ONE_PASS.mm, _ONE_PASS.mm_tn


def _chunk_fn(s0t, r, lw, k, v, kkraw, a, rk, lnw, lnb, first=False, d=_ONE_PASS):
    c = r.shape[1]
    at, rt, btc, ktc, gc, aab, arb, xv, arkv, ain, bin_ = _chunk_core(r, lw, k, v, kkraw, a, d)
    rs = d.mm(jnp.concatenate([at, rt], axis=1), s0t)
    u = d.solve(aab, rs[:, :c] + xv)
    y = rs[:, c:] + d.mm(arb, u) + arkv
    if first:
        y = _with_early_rows(y, r, lw, k, v, ain, bin_)
    gcol = jnp.sum(_diag(gc), axis=2, keepdims=True)
    sct = gcol * s0t + d.mm_tn(jnp.concatenate([btc, ktc], axis=1), jnp.concatenate([u, v], axis=1))
    return _post(y, r, k, v, rk, lnw, lnb), sct


def _diag(gc):
    return jnp.where(_masks(HEAD)[2], gc, 0.0)


def _with_early_rows(y, r, lw, k, v, ain, bin_):
    early = _early_rows(r[:2], lw[:2], k[:2], v[:2], ain[:2], bin_[:2])
    return jnp.concatenate([jnp.concatenate([early, y[:2, EARLY:]], axis=1), y[2:]], axis=0)


def _early_rows(r, lw, k, v, ain, bin_):
    cols = lambda x: _stack([jnp.transpose(x[h]) for h in range(2)])
    wc, bc, kc = cols(jnp.exp(lw)), cols(bin_), cols(k)
    st = jnp.zeros((2, HEAD, HEAD), F32)
    rows = []
    for t in range(EARLY):
        sa = _ONE_PASS.mm(ain[:, t:t + 1], st)
        st = st * wc[:, :, t:t + 1] + bc[:, :, t:t + 1] * sa + kc[:, :, t:t + 1] * v[:, t:t + 1]
        rows.append(_ONE_PASS.mm(r[:, t:t + 1], st))
    return jnp.concatenate(rows, axis=1)


def _chunk_rows(c):
    return pl.ds(c * CHUNK, CHUNK) if isinstance(c, int) else pl.ds(pl.multiple_of(c * CHUNK, CHUNK), CHUNK)


def _stack(xs):
    return jnp.concatenate([x[None] for x in xs], axis=0)


def _pairs(ref, chunks):
    tiles = [ref[0, _chunk_rows(c), :] for c in chunks]
    return _stack([t[:, HEAD * h:HEAD * h + HEAD] for t in tiles for h in range(2)])


def _unpair(vals, j):
    return jnp.concatenate([vals[2 * j], vals[2 * j + 1]], axis=1)


def _masks(c):
    ii = lax.broadcasted_iota(jnp.int32, (c, c), 0)
    jj = lax.broadcasted_iota(jnp.int32, (c, c), 1)
    return ii > jj, ii >= jj, ii == jj


@jax.custom_vjp
def _running_sum(lw):
    return _tri_dot(lw, _NN)


def _tri_dot(x, dims):
    g_, c, _ = x.shape
    tri = jnp.broadcast_to(_masks(c)[1].astype(BF16), (g_, c, c))
    head = x.astype(BF16)
    rest = (x - head.astype(F32)).astype(BF16)
    return lax.dot_general(tri, head, (dims, ((0,), (0,))), preferred_element_type=F32) + \
        lax.dot_general(tri, rest, (dims, ((0,), (0,))), preferred_element_type=F32)


_running_sum.defvjp(lambda lw: (_running_sum(lw), None), lambda _, ct: (_tri_dot(ct, _TN),))


def _chunk_core(r, lw, k, v, kkraw, a, d=_ONE_PASS):
    g_, c = r.shape[0], r.shape[1]
    nrm = jnp.sqrt(jnp.sum(kkraw * kkraw, axis=-1, keepdims=True))
    kkn = kkraw / jnp.maximum(nrm, 1e-12)
    ain, bin_ = -kkn, kkn * a
    strict, incl, _ = _masks(c)
    lg = _running_sum(lw)
    g, gp, gi = jnp.exp(lg), jnp.exp(lg - lw), jnp.exp(-lg)
    at, rt, bt, kt = ain * gp, r * g, bin_ * gi, k * gi
    aa = d.mm_nt(jnp.concatenate([at, rt], axis=1), jnp.concatenate([bt, kt], axis=1))
    aab = jnp.where(strict, aa[:, :c, :c], 0.0)
    aak = jnp.where(strict, aa[:, :c, c:], 0.0)
    arb = jnp.where(incl, aa[:, c:, :c], 0.0)
    ark = jnp.where(incl, aa[:, c:, c:], 0.0)
    akv = d.mm(jnp.concatenate([aak, ark], axis=1), v)
    gc = g[:, c - 1:c, :]
    return at, rt, bt * gc, kt * gc, gc, aab, arb, akv[:, :c], akv[:, c:], ain, bin_


def _lane_sum(x):
    return jnp.sum(x, axis=-1, keepdims=True)


def _lane_sum_mxu(x):
    g, c, n = x.shape
    x2 = x.reshape(g * c, n)
    head = x2.astype(BF16)
    rest = (x2 - head.astype(F32)).astype(BF16)
    ones = jnp.ones((n, n), BF16)
    return (_dot(head, ones) + _dot(rest, ones)).reshape(g, c, n)


def _post(y, r, k, v, rk, lnw, lnb, lane_sum=_lane_sum):
    mu = lane_sum(y) * (1.0 / HEAD)
    var = lane_sum(jnp.square(y - mu)) * (1.0 / HEAD)
    yn = (y - mu) * lax.rsqrt(var + GN_EPS) * lnw + lnb
    return yn + lane_sum(r * k * rk) * v


def _chunk_consts(r, lw, k, v, kkraw, a, first=False):
    d = _ONE_PASS
    at, rt, btc, ktc, gc, aab, arb, xv, arkv, ain, bin_ = _chunk_core(r, lw, k, v, kkraw, a, d)
    z = d.solve(aab, jnp.concatenate([at, xv], axis=2))
    ryv = jnp.concatenate([rt, arkv], axis=2) + d.mm(arb, z)
    if first:
        ryv = jnp.concatenate([ryv[:, :, :HEAD], _with_early_rows(ryv[:, :, HEAD:], r, lw, k, v, ain, bin_)], axis=2)
    mkv = d.mm_tn(btc, z) + jnp.concatenate([_diag(gc), d.mm_tn(ktc, v)], axis=2)
    return mkv, ryv


def _rwkv_scan(ins, rk, lnw, lnb):
    bsz, s, _ = ins[0].shape
    nch = s // CHUNK

    def consts_body(r_ref, lw_ref, k_ref, v_ref, kk_ref, a_ref, mkv_ref, ry_ref, yv_ref):
        def group(i, carry):
            chunks = [i * CHUNK_GROUP + j for j in range(CHUNK_GROUP)]
            mkv, ryv = _chunk_consts(*[_pairs(ref, chunks) for ref in (r_ref, lw_ref, k_ref, v_ref, kk_ref, a_ref)],
                                     first=isinstance(i, int) and i == 0)
            for j, c in enumerate(chunks):
                for h in range(2):
                    mkv_ref[0, 0, c, h] = mkv[2 * j + h]
                ry_ref[0, _chunk_rows(c), :] = jnp.concatenate([ryv[2 * j][:, :HEAD], ryv[2 * j + 1][:, :HEAD]], axis=1)
                yv_ref[0, _chunk_rows(c), :] = jnp.concatenate([ryv[2 * j][:, HEAD:], ryv[2 * j + 1][:, HEAD:]], axis=1)
            return carry

        group(0, 0)
        lax.fori_loop(1, nch // CHUNK_GROUP, group, 0)

    tile = pl.BlockSpec((1, s, LANE), lambda b, hp: (b, 0, hp))
    vec = pl.BlockSpec((1, LANE), lambda b, hp: (0, hp))
    mkv_spec = pl.BlockSpec((1, 1, nch, 2, HEAD, LANE), lambda b, hp: (b, hp, 0, 0, 0, 0))
    st_spec = pl.BlockSpec((1, 1, nch, 2, HEAD, HEAD), lambda b, hp: (b, hp, 0, 0, 0, 0))
    mkv, ry, yv = pl.pallas_call(
        consts_body, name="rwkv_consts", grid=(bsz, 4), in_specs=[tile] * 6, out_specs=[mkv_spec, tile, tile],
        out_shape=[SDS((bsz, 4, nch, 2, HEAD, LANE), F32), SDS((bsz, s, WIDTH), F32), SDS((bsz, s, WIDTH), F32)],
        compiler_params=_params(("parallel", "parallel")))(*ins)

    states = _chunk_recurrence(mkv, None, "rwkv_states")

    def out_body(ry_ref, yv_ref, r_ref, k_ref, v_ref, st_ref, rk_ref, lnw_ref, lnb_ref, o_ref):
        y, r, k, v, rk_, lnw_, lnb_ = _scan_rows(ry_ref, yv_ref, r_ref, k_ref, v_ref, st_ref, rk_ref, lnw_ref, lnb_ref)
        o = _post(y, r, k, v, rk_, lnw_, lnb_, _lane_sum_mxu)
        for j in range(CHUNK_GROUP):
            o_ref[0, _chunk_rows(j), :] = _unpair(o, j)

    o = pl.pallas_call(
        out_body, name="rwkv_out", grid=(bsz, 4, nch // CHUNK_GROUP), in_specs=_group_specs(5), out_specs=_group_specs(1)[0],
        out_shape=SDS((bsz, s, WIDTH), F32),
        compiler_params=_params(("parallel", "parallel", "parallel")))(ry, yv, ins[0], ins[2], ins[3], states, rk, lnw, lnb)
    return o, states, (mkv, ry, yv)


def _group_specs(n_tiles):
    tile = pl.BlockSpec((1, CHUNK_GROUP * CHUNK, LANE), lambda b, hp, t: (b, t, hp))
    if n_tiles == 1:
        return [tile]
    st = pl.BlockSpec((1, 1, CHUNK_GROUP, 2, HEAD, HEAD), lambda b, hp, t: (b, hp, t, 0, 0, 0))
    vec = pl.BlockSpec((1, LANE), lambda b, hp, t: (0, hp))
    return [tile] * n_tiles + [st] + [vec] * 3


def _scan_rows(ry_ref, yv_ref, r_ref, k_ref, v_ref, st_ref, rk_ref, lnw_ref, lnb_ref):
    chunks = list(range(CHUNK_GROUP))
    ry, yv, r, k, v = (_pairs(ref, chunks) for ref in (ry_ref, yv_ref, r_ref, k_ref, v_ref))
    st = _stack([st_ref[0, 0, c, h] for c in chunks for h in range(2)])
    vecs = [_stack([ref[:, HEAD * h:HEAD * h + HEAD] for _ in chunks for h in range(2)]) for ref in (rk_ref, lnw_ref, lnb_ref)]
    return (_bmm(ry, st) + yv, r, k, v, *vecs)


def _chunk_recurrence(mkv, q, name):
    bsz, _, nch = mkv.shape[:3]
    pairs = [(hp, h) for hp in range(4) for h in range(2)]

    def body(*refs):
        mkv_ref, out_ref, acc = refs[0], refs[-2], refs[-1]
        acc[...] = jnp.zeros_like(acc)

        def step(i, carry):
            c = i if q is None else nch - 1 - i
            cur = acc[...]
            for j, (hp, h) in enumerate(pairs):
                out_ref[0, hp, c, h] = cur[j]
            m = _stack([mkv_ref[0, hp, c, h] for hp, h in pairs])
            if q is None:
                acc[...] = _bmm(m[:, :, :HEAD], cur) + m[:, :, HEAD:]
            else:
                acc[...] = _bmm_tn(m[:, :, :HEAD], cur) + _stack([refs[1][0, hp, c, h] for hp, h in pairs])
            return carry

        lax.fori_loop(0, nch, step, 0)

    spec = lambda w: pl.BlockSpec((1, 4, nch, 2, HEAD, w), lambda b: (b, 0, 0, 0, 0, 0))
    return pl.pallas_call(
        body, name=name, grid=(bsz,), in_specs=[spec(LANE)] + ([] if q is None else [spec(HEAD)]), out_specs=spec(HEAD),
        out_shape=SDS((bsz, 4, nch, 2, HEAD, HEAD), F32), scratch_shapes=[pltpu.VMEM((8, HEAD, HEAD), F32)],
        compiler_params=_params(("parallel",)))(*([mkv] if q is None else [mkv, q]))


def _rwkv_scan_bwd(ins, states, consts, do3, rk, lnw, lnb):
    bsz, s, _ = ins[0].shape
    nch = s // CHUNK

    mkv, ry, yv = consts

    def q_body(do_ref, ry_ref, yv_ref, r_ref, k_ref, v_ref, st_ref, rk_ref, lnw_ref, lnb_ref, q_ref):
        y, r, k, v, rk_, lnw_, lnb_ = _scan_rows(ry_ref, yv_ref, r_ref, k_ref, v_ref, st_ref, rk_ref, lnw_ref, lnb_ref)
        _, vjp = jax.vjp(lambda y_: _post(y_, r, k, v, rk_, lnw_, lnb_), y)
        (dy,) = vjp(_pairs(do_ref, list(range(CHUNK_GROUP))))
        q = _bmm_tn(_pairs(ry_ref, list(range(CHUNK_GROUP))), dy)
        for j in range(CHUNK_GROUP):
            for h in range(2):
                q_ref[0, 0, j, h] = q[2 * j + h]

    specs = _group_specs(6)
    q = pl.pallas_call(
        q_body, name="rwkv_q", grid=(bsz, 4, nch // CHUNK_GROUP), in_specs=specs, out_specs=specs[6],
        out_shape=SDS((bsz, 4, nch, 2, HEAD, HEAD), F32),
        compiler_params=_params(("parallel", "parallel", "parallel")))(do3, ry, yv, ins[0], ins[2], ins[3], states, rk, lnw, lnb)

    dstates = _chunk_recurrence(mkv, q, "rwkv_dstates")

    def body(r_ref, lw_ref, k_ref, v_ref, kk_ref, a_ref, st_ref, dst_ref, do_ref, rk_ref, lnw_ref, lnb_ref,
             dr_ref, dlw_ref, dk_ref, dv_ref, dkk_ref, da_ref, drk_ref, dlnw_ref, dlnb_ref):
        chunks = list(range(BWD_GROUP))
        par_refs = (drk_ref, dlnw_ref, dlnb_ref)

        @pl.when(jnp.logical_and(pl.program_id(1) == 0, pl.program_id(2) == 0))
        def _():
            for ref in par_refs:
                ref[...] = jnp.zeros_like(ref)

        def group(first):
            per_pair = lambda ref: _stack([ref[0, 0, c, h] for c in chunks for h in range(2)])
            vecs = [_stack([ref[:, HEAD * h:HEAD * h + HEAD] for _ in chunks for h in range(2)]) for ref in (rk_ref, lnw_ref, lnb_ref)]
            _, vjp = jax.vjp(functools.partial(_chunk_fn, first=first, d=_ONE_PASS), per_pair(st_ref),
                             *[_pairs(ref, chunks) for ref in (r_ref, lw_ref, k_ref, v_ref, kk_ref, a_ref)], *vecs)
            grads = vjp((_pairs(do_ref, chunks), per_pair(dst_ref)))
            for ref, cot in zip((dr_ref, dlw_ref, dk_ref, dv_ref, dkk_ref, da_ref), grads[1:7]):
                for j, c in enumerate(chunks):
                    ref[0, _chunk_rows(c), :] = _unpair(cot, j)
            for ref, g_ in zip(par_refs, grads[7:10]):
                ref[...] += jnp.concatenate([sum(g_[2 * j + h] for j in range(BWD_GROUP)) for h in range(2)], axis=1)

        pl.when(pl.program_id(2) == 0)(functools.partial(group, True))
        pl.when(pl.program_id(2) != 0)(functools.partial(group, False))

    tt = BWD_GROUP * CHUNK
    tile = pl.BlockSpec((1, tt, LANE), lambda hp, b, t: (b, t, hp))
    vec = pl.BlockSpec((1, LANE), lambda hp, b, t: (0, hp))
    st_spec = pl.BlockSpec((1, 1, BWD_GROUP, 2, HEAD, HEAD), lambda hp, b, t: (b, hp, t, 0, 0, 0))
    outs = pl.pallas_call(
        body, name="rwkv_scan_bwd", grid=(4, bsz, s // tt), in_specs=[tile] * 6 + [st_spec, st_spec, tile] + [vec] * 3,
        out_specs=[tile] * 6 + [vec] * 3,
        out_shape=[SDS((bsz, s, WIDTH), F32)] * 6 + [SDS((1, WIDTH), F32)] * 3,
        compiler_params=_params(("parallel", "arbitrary", "arbitrary")))(*ins, states, dstates, do3, rk, lnw, lnb)
    return outs[:6], outs[6:]


def _head(o_attn, o_rwkv, z_attn, z_rwkv, gm, x2, tgt, wua, wur, wout, g2):
    n = x2.shape[0]
    tm = 256
    nt = n // tm
    d = D_MODEL

    def body(oa_ref, or_ref, za_ref, zr_ref, gm_ref, x_ref, t_ref, wua_ref, wur_ref, wout_ref, g2_ref,
             dxo_ref, doa_ref, dor_ref, dza_ref, dzr_ref, dgm_ref, dwua_ref, dwur_ref, dwout_ref, dg2_ref, loss_ref, lacc):
        i = pl.program_id(0)
        oa, orw, za, zr = oa_ref[...], or_ref[...], za_ref[...], zr_ref[...]
        ga, gb = gm_ref[:, 0:d], gm_ref[:, d:2 * d]
        am = (oa * _silu(za)).astype(BF16)
        bm = (orw * _silu(zr)).astype(BF16)
        ya, yb = _dot(am, wua_ref[...]), _dot(bm, wur_ref[...])
        sa, sb = jax.nn.sigmoid(ga), jax.nn.sigmoid(gb)
        merged = (sa * ya + sb * yb).astype(BF16)
        out = _dot(merged, wout_ref[...])
        rs = lax.rsqrt(jnp.mean(out * out, axis=-1, keepdims=True) + RMS_EPS)
        g2 = g2_ref[...]
        err = x_ref[...] + out * rs * g2 - t_ref[...]
        lpart = jnp.sum(err * err, axis=0, keepdims=True)
        dxo = err * (1.0 / d)
        dxo_ref[...] = dxo
        dg2 = jnp.sum(dxo * out * rs, axis=0, keepdims=True)
        gd = dxo * g2
        dout = (rs * (gd - out * (rs * rs) * jnp.mean(gd * out, axis=-1, keepdims=True))).astype(BF16)
        dmerged = _dot_nt(dout, wout_ref[...])
        dwout = _dot_tn(merged, dout)
        dya, dyb = (dmerged * sa).astype(BF16), (dmerged * sb).astype(BF16)
        dgm_ref[:, 0:d] = (dmerged * ya * sa * (1.0 - sa)).astype(BF16)
        dgm_ref[:, d:2 * d] = (dmerged * yb * sb * (1.0 - sb)).astype(BF16)
        dam, dbm = _dot_nt(dya, wua_ref[...]), _dot_nt(dyb, wur_ref[...])
        dwua, dwur = _dot_tn(am, dya), _dot_tn(bm, dyb)
        doa_ref[...] = dam * _silu(za)
        dza_ref[...] = (dam * oa * _dsilu(za)).astype(BF16)
        dor_ref[...] = dbm * _silu(zr)
        dzr_ref[...] = (dbm * orw * _dsilu(zr)).astype(BF16)

        @pl.when(i == 0)
        def _():
            dwua_ref[...], dwur_ref[...], dwout_ref[...], dg2_ref[...], lacc[...] = dwua, dwur, dwout, dg2, lpart

        @pl.when(i != 0)
        def _():
            dwua_ref[...] += dwua
            dwur_ref[...] += dwur
            dwout_ref[...] += dwout
            dg2_ref[...] += dg2
            lacc[...] += lpart

        @pl.when(i == nt - 1)
        def _():
            loss_ref[...] = jnp.sum(lacc[...], axis=1, keepdims=True) * (0.5 / d)

    t512 = pl.BlockSpec((tm, WIDTH), lambda i: (i, 0))
    t1k = pl.BlockSpec((tm, d), lambda i: (i, 0))
    t2k = pl.BlockSpec((tm, 2 * d), lambda i: (i, 0))
    full = lambda r, c: pl.BlockSpec((r, c), lambda i: (0, 0))
    return pl.pallas_call(
        body, name="head_fwd_bwd", grid=(nt,),
        in_specs=[t512, t512, t512, t512, t2k, t1k, t1k, full(WIDTH, d), full(WIDTH, d), full(d, d), full(1, d)],
        out_specs=[t1k, t512, t512, t512, t512, t2k, full(WIDTH, d), full(WIDTH, d), full(d, d), full(1, d), full(1, 1)],
        out_shape=[SDS((n, d), F32), SDS((n, WIDTH), F32), SDS((n, WIDTH), F32), SDS((n, WIDTH), BF16), SDS((n, WIDTH), BF16),
                   SDS((n, 2 * d), BF16), SDS((WIDTH, d), F32), SDS((WIDTH, d), F32), SDS((d, d), F32), SDS((1, d), F32), SDS((1, 1), F32)],
        scratch_shapes=[pltpu.VMEM((1, d), F32)],
        compiler_params=_params(("arbitrary",)))(o_attn, o_rwkv, z_attn, z_rwkv, gm, x2, tgt, wua, wur, wout, g2)


def _mesh_pos():
    x, y, c = lax.axis_index("x"), lax.axis_index("y"), lax.axis_index("c")
    return 4 * x + 2 * y + c


def _coords(idx):
    return (idx // 4, (idx // 2) % 2, idx % 2)


def _exchange(srcs, to_all, name):
    n = len(srcs)

    def body(*refs):
        src_refs, dst_refs = refs[:n], refs[n:2 * n]
        send_sems, recv_sems, local_sems = refs[2 * n:]
        me = _mesh_pos()

        def piece(i, j):
            return src_refs[i] if to_all[i] else src_refs[i].at[j]

        def remote(i, off, peer, block, slot):
            return pltpu.make_async_remote_copy(src_ref=piece(i, block), dst_ref=dst_refs[i].at[slot],
                                                send_sem=send_sems.at[i, off - 1], recv_sem=recv_sems.at[i, off - 1],
                                                device_id=_coords(peer), device_id_type=MESH)

        local = [pltpu.make_async_copy(piece(i, me), dst_refs[i].at[me], local_sems.at[i]) for i in range(n)]
        for cp in local:
            cp.start()
        sends = []
        for off in range(1, N_DEV):
            to = (me + off) % N_DEV
            for i in range(n):
                sends.append(remote(i, off, to, to, me))
                sends[-1].start()
        for off in range(1, N_DEV):
            frm = (me + N_DEV - off) % N_DEV
            for i in range(n):
                remote(i, off, frm, me, frm).wait_recv()
        for cp in sends:
            cp.wait_send()
        for cp in local:
            cp.wait()

    outs = pl.pallas_call(
        body, name=name, in_specs=[pl.BlockSpec(memory_space=pltpu.HBM)] * n, out_specs=[pl.BlockSpec(memory_space=pltpu.HBM)] * n,
        out_shape=[SDS((N_DEV,) + s.shape[-2:], s.dtype) for s in srcs],
        scratch_shapes=[pltpu.SemaphoreType.DMA((n, N_DEV - 1)), pltpu.SemaphoreType.DMA((n, N_DEV - 1)), pltpu.SemaphoreType.DMA((n,))],
        compiler_params=pltpu.CompilerParams())(*srcs)
    return outs


_HBM = pl.BlockSpec(memory_space=pltpu.HBM)
_SEM = pl.BlockSpec(memory_space=pltpu.SEMAPHORE)
_EFFECT = pltpu.SideEffectType.DATAFLOW_SIDE_EFFECTING


def _send_copy(src_ref, land_ref, to_all, send_sems, recv_sems, i, off, block, slot, peer):
    k = i * (N_DEV - 1) + off - 1
    return pltpu.make_async_remote_copy(src_ref=src_ref if to_all else src_ref.at[block], dst_ref=land_ref.at[slot],
                                        send_sem=send_sems.at[k], recv_sem=recv_sems.at[k],
                                        device_id=_coords(peer), device_id_type=MESH)


def _send_start(srcs, to_all, name):
    n = len(srcs)

    def body(*refs):
        src_refs, land_refs = refs[:n], refs[n:2 * n]
        send_sems, recv_sems = refs[2 * n:2 * n + 2]
        me = _mesh_pos()
        for off in range(1, N_DEV):
            to = (me + off) % N_DEV
            for i in range(n):
                _send_copy(src_refs[i], land_refs[i], to_all, send_sems, recv_sems, i, off, to, me, to).start()
        refs[-1][...] = jnp.zeros_like(refs[-1])

    lands = [jnp.zeros((N_DEV,) + s.shape[-2:], s.dtype) for s in srcs]
    hbm = [pltpu.HBM(a.shape, a.dtype) for a in list(srcs) + lands]
    sems = pltpu.SemaphoreType.DMA((n * (N_DEV - 1),))
    outs = pl.pallas_call(
        body, name=name, out_shape=(sems, sems, *hbm, SDS((8, LANE), BF16)),
        in_specs=(_HBM,) * (2 * n), out_specs=(_SEM, _SEM) + (_HBM,) * (2 * n) + (pl.BlockSpec(memory_space=pltpu.VMEM),),
        input_output_aliases={i: 2 + i for i in range(2 * n)}, compiler_params=pltpu.CompilerParams(has_side_effects=_EFFECT),
    )(*[pltpu.with_memory_space_constraint(a, pltpu.HBM) for a in list(srcs) + lands])
    return outs[0], outs[1], outs[2:2 + n], outs[2 + n:2 + 2 * n], outs[-1]


def _send_wait(send_sems, recv_sems, srcs_thru, lands_thru, to_all, after, name):
    n = len(srcs_thru)

    def body(*refs):
        src_refs, land_refs = refs[:n], refs[n:2 * n]
        send_sems, recv_sems = refs[2 * n:2 * n + 2]
        me = _mesh_pos()
        for off in range(1, N_DEV):
            to, frm = (me + off) % N_DEV, (me + N_DEV - off) % N_DEV
            for i in range(n):
                _send_copy(src_refs[i], land_refs[i], to_all, send_sems, recv_sems, i, off, to, me, to).wait_send()
                _send_copy(src_refs[i], land_refs[i], to_all, send_sems, recv_sems, i, off, me, frm, frm).wait_recv()

    hbm = tuple(pltpu.HBM(a.shape, a.dtype) for a in list(srcs_thru) + list(lands_thru))
    outs = pl.pallas_call(
        body, name=name, out_shape=hbm, in_specs=(_HBM,) * (2 * n) + (_SEM, _SEM, pl.BlockSpec(memory_space=pl.ANY)),
        out_specs=(_HBM,) * (2 * n), input_output_aliases={i: i for i in range(2 * n)},
        compiler_params=pltpu.CompilerParams(has_side_effects=_EFFECT),
    )(*srcs_thru, *lands_thru, send_sems, recv_sems, after)
    return outs[n:]


def _gather(srcs, after, name):
    n = len(srcs)

    def body(*refs):
        src_refs, dst_refs = refs[:n], refs[n + 1:2 * n + 1]
        send_sems, recv_sems, local_sems = refs[2 * n + 1:]
        x, y, c = lax.axis_index("x"), lax.axis_index("y"), lax.axis_index("c")
        me, sibling = (x, y, c), (x, y, 1 - c)
        chips = [(1 - x, y), (x, 1 - y), (1 - x, 1 - y)]

        def slot(i, dev):
            return dst_refs[i].at[4 * dev[0] + 2 * dev[1] + dev[2]]

        def copy(i, k, block, to, own=False):
            return pltpu.make_async_remote_copy(src_ref=src_refs[i] if own else slot(i, block), dst_ref=slot(i, block),
                                                send_sem=send_sems.at[i, k], recv_sem=recv_sems.at[i, k],
                                                device_id=to, device_id_type=MESH)

        local = [pltpu.make_async_copy(src_refs[i], slot(i, me), local_sems.at[i]) for i in range(n)]
        for cp in local:
            cp.start()
        sends = []
        for i in range(n):
            sends.append(copy(i, 0, me, sibling, own=True))
            sends += [copy(i, 1 + j, me, (*chip, c), own=True) for j, chip in enumerate(chips)]
        for cp in sends:
            cp.start()
        for j, chip in enumerate(chips):
            for i in range(n):
                copy(i, 1 + j, (*chip, c), me).wait_recv()
                sends.append(copy(i, 4 + j, (*chip, c), sibling))
                sends[-1].start()
        for i in range(n):
            copy(i, 0, sibling, me).wait_recv()
            for j, chip in enumerate(chips):
                copy(i, 4 + j, (*chip, 1 - c), me).wait_recv()
        for cp in sends:
            cp.wait_send()
        for cp in local:
            cp.wait()

    return pl.pallas_call(
        body, name=name, in_specs=[pl.BlockSpec(memory_space=pltpu.HBM)] * n + [pl.BlockSpec(memory_space=pl.ANY)],
        out_specs=[pl.BlockSpec(memory_space=pltpu.HBM)] * n, out_shape=[SDS((N_DEV,) + s.shape, s.dtype) for s in srcs],
        scratch_shapes=[pltpu.SemaphoreType.DMA((n, N_DEV - 1)), pltpu.SemaphoreType.DMA((n, N_DEV - 1)), pltpu.SemaphoreType.DMA((n,))],
        compiler_params=pltpu.CompilerParams())(*srcs, after)


def _adamw(parts, w, m, v, tr, name, own=None):
    rows, cols = w.shape
    c1, c2 = 1.0 - ADAM_B1 ** ADAM_STEP, 1.0 - ADAM_B2 ** ADAM_STEP

    def body(p_ref, *refs):
        w_ref, m_ref, v_ref, g_ref, d_ref, nm_ref, nv_ref = refs[-7:]
        me = _mesh_pos()

        def part(j):
            return p_ref[j] if own is None else jnp.where(me == j, refs[0][...], p_ref[j])

        g = part(0).astype(F32)
        for j in range(1, N_DEV):
            g = g + part(j).astype(F32)
        nm = ADAM_B1 * m_ref[...] + (1.0 - ADAM_B1) * g
        nv = ADAM_B2 * v_ref[...] + (1.0 - ADAM_B2) * jnp.square(g)
        g_ref[...] = g
        nm_ref[...] = nm
        nv_ref[...] = nv
        d_ref[...] = -ADAM_LR * ((nm / c1) / (jnp.sqrt(nv / c2) + ADAM_EPS) + ADAM_WD * w_ref[...])

    t = pl.BlockSpec((tr, cols), lambda i: (i, 0))
    extra = [] if own is None else [own]
    return pl.pallas_call(
        body, name=name, grid=(rows // tr,), in_specs=[pl.BlockSpec((N_DEV, tr, cols), lambda i: (0, i, 0))] + [t] * (3 + len(extra)),
        out_specs=[t] * 4, out_shape=[SDS((rows, cols), F32)] * 4, compiler_params=_params(("parallel",)))(parts, *extra, w, m, v)


SHARDED = (("w_in", D_MODEL, IN_COLS // N_DEV, True, 128), ("w_up_attn", WIDTH, D_MODEL // N_DEV, True, WIDTH),
           ("w_up_rwkv", WIDTH, D_MODEL // N_DEV, True, WIDTH), ("w_out", D_MODEL // N_DEV, D_MODEL, False, D_MODEL // N_DEV),
           ("rwkv_w_up", LORA, WIDTH // N_DEV, True, LORA), ("rwkv_a_up", LORA, WIDTH // N_DEV, True, LORA))
LOSS_SLOT = sum(n for _, n in SMALL)


def _pack_small(small, extra=None):
    flat = [small[n].reshape(-1).astype(F32) for n, _ in SMALL]
    flat.append(jnp.zeros((1,), F32) if extra is None else extra.reshape(1))
    flat.append(jnp.zeros((SMALL_ROWS * LANE - LOSS_SLOT - 1,), F32))
    return jnp.concatenate(flat).reshape(SMALL_ROWS, LANE)


def _unpack_small(packed, shapes):
    flat = packed.reshape(-1)
    out, off = {}, 0
    for n, cnt in SMALL:
        out[n] = flat[off:off + cnt].reshape(shapes[n])
        off += cnt
    return out, flat[LOSS_SLOT]


def _whole(gathered, by_cols):
    if not by_cols:
        return gathered.reshape(-1, gathered.shape[-1])
    return gathered.transpose(1, 0, 2).reshape(gathered.shape[1], -1)


def _per_owner(full, by_cols):
    if not by_cols:
        return full.reshape(N_DEV, -1, full.shape[-1])
    return full.reshape(full.shape[0], N_DEV, -1).transpose(1, 0, 2)


def _local_step(x, loss_target, sm, wts):
    bsz, s, d = x.shape
    n = bsz * s
    x2, tgt = x.reshape(n, d), loss_target.reshape(n, d)
    bidx = jnp.asarray(_bucket_tables())
    w_in = wts["w_in"]
    segs = (("qkv", 0, QKV_COLS, 1536), ("za", OFF_ZA, WIDTH, 512), ("pr", OFF_PR, PR_COLS, PR_COLS), ("zr", OFF_ZR, WIDTH, 512),
            ("gm", OFF_GM, 2 * D_MODEL, 1024))

    h, rs = _prenorm(x2, sm["pre_norm_gain"])
    w_seg = {nm: w_in[:, off:off + cnt] for nm, off, cnt, _ in segs}
    proj = {nm: _mm(h, w_seg[nm], tn, "proj_" + nm) for nm, _, _, tn in segs}
    qkv3 = proj["qkv"].reshape(bsz, s, QKV_COLS)
    pr3 = proj["pr"].reshape(bsz, s, PR_COLS)

    o_attn, lse = _attn_fwd(qkv3, sm["rel_bias"], bidx)
    rk = sm["rwkv_r_k"].reshape(1, WIDTH)
    pre_args = (sm["rwkv_shift_mix"], sm["rwkv_w0"], wts["rwkv_w_up"], sm["rwkv_a0"], wts["rwkv_a_up"], sm["rwkv_k_k"], sm["rwkv_k_a"])
    scan_in = _rwkv_pre(pr3, *pre_args)
    o_rwkv, states, consts = _rwkv_scan(scan_in, rk, sm["rwkv_ln_w"], sm["rwkv_ln_b"])

    (dxo, do_attn, do_rwkv, dza, dzr, dgm, g_wua, g_wur, g_wout, g_post, loss) = _head(
        o_attn.reshape(n, WIDTH), o_rwkv.reshape(n, WIDTH), proj["za"], proj["zr"], proj["gm"], x2, tgt,
        wts["w_up_attn"], wts["w_up_rwkv"], wts["w_out"], sm["post_norm_gain"])

    dqkv, dbias = _attn_bwd(qkv3, o_attn, lse, do_attn.reshape(bsz, s, WIDTH), sm["rel_bias"], bidx)
    g_bias = _bias_grad(dbias, bidx)[:, :N_BUCKET].T

    scan_cots, (g_rk, g_lnw, g_lnb) = _rwkv_scan_bwd(scan_in, states, consts, do_rwkv.reshape(bsz, s, WIDTH), rk, sm["rwkv_ln_w"],
                                                     sm["rwkv_ln_b"])
    dprs, g_mix, g_w0, g_wup, g_a0, g_aup, g_kk, g_ka = _rwkv_pre_bwd(pr3, scan_cots, *pre_args)
    dpr = _shift_bwd(dprs, sm["rwkv_shift_mix"]).reshape(n, PR_COLS)

    dsegs = [(dqkv.reshape(9, n, WIDTH), 0, QKV_COLS, WIDTH), (dza, OFF_ZA, WIDTH, WIDTH), (dpr, OFF_PR, PR_COLS, PR_COLS),
             (dzr, OFF_ZR, WIDTH, WIDTH), (dgm, OFF_GM, 2 * D_MODEL, D_MODEL)]
    full = {"w_in": jnp.concatenate([_mm_tn(h, t, tn, "gw_in_%d" % j) for j, (t, _, _, tn) in enumerate(dsegs)], axis=1),
            "w_up_attn": g_wua, "w_up_rwkv": g_wur, "w_out": g_wout, "rwkv_w_up": g_wup, "rwkv_a_up": g_aup}
    blocks = [_per_owner(full[nm], by_cols).astype(BF16) for nm, _, _, by_cols, _ in SHARDED]
    me = 4 * lax.axis_index("x") + 2 * lax.axis_index("y") + lax.axis_index("c")
    own = [lax.dynamic_index_in_dim(b, me, 0, keepdims=False) for b in blocks]
    send_sems, recv_sems, blocks_thru, lands_thru, token = _send_start(blocks, False, "grads_start")
    dh = _mm_nt(dsegs[0][0], w_seg["qkv"], token, "dh_qkv")
    grad_x, g_pre = _dh_rest_prenorm_bwd([t for t, *_ in dsegs[1:]], [w_seg[nm] for nm in ("za", "pr", "zr", "gm")], dh, x2, rs,
                                         sm["pre_norm_gain"], dxo)
    landed = _send_wait(send_sems, recv_sems, blocks_thru, lands_thru, False, g_pre, "grads_wait")

    small = {"pre_norm_gain": g_pre, "rel_bias": g_bias, "rwkv_shift_mix": g_mix, "rwkv_w0": g_w0, "rwkv_a0": g_a0, "rwkv_k_k": g_kk,
             "rwkv_k_a": g_ka, "rwkv_r_k": g_rk, "rwkv_ln_w": g_lnw, "rwkv_ln_b": g_lnb, "post_norm_gain": g_post}
    return loss[0, 0], grad_x.reshape(bsz, s, d), (landed, own), small


def kernel(x, pre_norm_gain, w_in, rel_bias, rwkv_shift_mix, rwkv_w0, rwkv_w_up, rwkv_a0, rwkv_a_up, rwkv_k_k, rwkv_k_a, rwkv_r_k, rwkv_ln_w, rwkv_ln_b, w_up_attn, w_up_rwkv, w_out, post_norm_gain, loss_target, m_pre_norm_gain, m_w_in, m_rel_bias, m_rwkv_shift_mix, m_rwkv_w0, m_rwkv_w_up, m_rwkv_a0, m_rwkv_a_up, m_rwkv_k_k, m_rwkv_k_a, m_rwkv_r_k, m_rwkv_ln_w, m_rwkv_ln_b, m_w_up_attn, m_w_up_rwkv, m_w_out, m_post_norm_gain, v_pre_norm_gain, v_w_in, v_rel_bias, v_rwkv_shift_mix, v_rwkv_w0, v_rwkv_w_up, v_rwkv_a0, v_rwkv_a_up, v_rwkv_k_k, v_rwkv_k_a, v_rwkv_r_k, v_rwkv_ln_w, v_rwkv_ln_b, v_w_up_attn, v_w_up_rwkv, v_w_out, v_post_norm_gain):
    names = [n for n, *_ in SHARDED] + [n for n, _ in SMALL]
    loc = dict(locals())
    w = {n: loc[n] for n in names}
    m = {n: loc["m_" + n] for n in names}
    v = {n: loc["v_" + n] for n in names}
    shapes = {n: w[n].shape for n in names}
    order = ["pre_norm_gain", "w_in", "rel_bias", "rwkv_shift_mix", "rwkv_w0", "rwkv_w_up", "rwkv_a0", "rwkv_a_up", "rwkv_k_k", "rwkv_k_a",
             "rwkv_r_k", "rwkv_ln_w", "rwkv_ln_b", "w_up_attn", "w_up_rwkv", "w_out", "post_norm_gain"]
    shard2d = lambda t, n, r, c: t[n].reshape(r, c)

    shards = [shard2d(w, n, r, c).astype(BF16) for n, r, c, _, _ in SHARDED]
    send_sems, recv_sems, srcs_thru, lands_thru, token = _send_start(shards[1:], True, "weights_start")
    gathered = list(_gather(shards[:1], token, "gather_weights"))
    landed = _send_wait(send_sems, recv_sems, srcs_thru, lands_thru, True, gathered[0], "weights_wait")
    me = 4 * lax.axis_index("x") + 2 * lax.axis_index("y") + lax.axis_index("c")
    gathered += [lax.dynamic_update_index_in_dim(g, sh, me, 0) for g, sh in zip(landed, shards[1:])]
    wts = {n: _whole(g, by_cols) for (n, _, _, by_cols, _), g in zip(SHARDED, gathered)}

    loss, grad_x, (landed, own), small = _local_step(x, loss_target, w, wts)
    (small_parts,) = _exchange([_pack_small(small, loss)], [True], "exchange_small")

    outs = [{}, {}, {}, {}]
    for (n, r, c, _, tr), p, o_ in zip(SHARDED, landed, own):
        res = _adamw(p, shard2d(w, n, r, c), shard2d(m, n, r, c), shard2d(v, n, r, c), tr, "adamw_" + n, own=o_)
        for o, t in zip(outs, res):
            o[n] = t.reshape(shapes[n])
    res = _adamw(small_parts, _pack_small(w), _pack_small(m), _pack_small(v), SMALL_ROWS, "adamw_small")
    for o, t in zip(outs, res):
        o.update(_unpack_small(t, shapes)[0])
    loss = _unpack_small(res[0], shapes)[1]
    return (loss, grad_x, *[o[n] for o in outs for n in order])
```

```python
import functools
import math

import numpy as np
import jax
import jax.numpy as jnp
from jax import lax
from jax.experimental import pallas as pl
from jax.experimental.pallas import tpu as pltpu

F32, BF16 = jnp.float32, jnp.bfloat16
SDS = jax.ShapeDtypeStruct
MESH = pl.DeviceIdType.MESH

N_DEV = 8
D_MODEL = 1024
HEAD = 64
N_HEAD = 8
WIDTH = N_HEAD * HEAD
DILATIONS = (1, 4, 16)
QB = 128
N_BUCKET = 32
MAX_DIST = 2048
LORA = 64
QKV_COLS = 9 * WIDTH
PR_COLS = 3 * WIDTH + 2 * LORA
IN_COLS = QKV_COLS + WIDTH + PR_COLS + WIDTH + 2 * D_MODEL
OFF_ZA, OFF_PR, OFF_ZR, OFF_GM = QKV_COLS, QKV_COLS + WIDTH, QKV_COLS + WIDTH + PR_COLS, QKV_COLS + 2 * WIDTH + PR_COLS
RMS_EPS = 1e-6
GN_EPS = 64e-5
SCALE = 1.0 / math.sqrt(HEAD)
CHUNK = 64
CHUNK_GROUP = 32
BWD_GROUP = 16
EARLY = 8
NEG = -1e30
LANE = 128

ADAM_LR, ADAM_B1, ADAM_B2, ADAM_EPS, ADAM_WD, ADAM_STEP = 0.001, 0.9, 0.999, 1e-08, 0.01, 10

VMEM_LIMIT = 56 * 1024 * 1024
MM_ROWS = 2048

SMALL = (("pre_norm_gain", 1024), ("rel_bias", 768), ("rwkv_shift_mix", 1664), ("rwkv_w0", 512), ("rwkv_a0", 512),
         ("rwkv_k_k", 512), ("rwkv_k_a", 512), ("rwkv_r_k", 512), ("rwkv_ln_w", 512), ("rwkv_ln_b", 512),
         ("post_norm_gain", 1024))
SMALL_ROWS = 64


def _params(sem=None):
    return pltpu.CompilerParams(dimension_semantics=sem, vmem_limit_bytes=VMEM_LIMIT)


def _dot(a, b):
    return jnp.dot(a, b, preferred_element_type=F32)


def _dot_nt(a, b):
    return lax.dot_general(a, b, (((1,), (1,)), ((), ())), preferred_element_type=F32)


def _dot_tn(a, b):
    return lax.dot_general(a, b, (((0,), (0,)), ((), ())), preferred_element_type=F32)


@jax.custom_vjp
def _bdot(a, b):
    return _dot(a.astype(BF16), b.astype(BF16))


def _bdot_fwd(a, b):
    return _bdot(a, b), (a, b)


def _bdot_bwd(res, g):
    a, b = res
    gb = g.astype(BF16)
    return _dot_nt(gb, b.astype(BF16)), _dot_tn(a.astype(BF16), gb)


_bdot.defvjp(_bdot_fwd, _bdot_bwd)


def _silu(z):
    return z * jax.nn.sigmoid(z)


def _dsilu(z):
    s = jax.nn.sigmoid(z)
    return s * (1.0 + z * (1.0 - s))


def _softplus(x):
    return jnp.maximum(x, 0.0) + jnp.log(1.0 + jnp.exp(-jnp.abs(x)))


def _bucket_tables():
    qi = np.arange(QB)[:, None] + QB
    ki = np.arange(2 * QB)[None, :]
    rel = np.maximum(qi - ki, 0)
    out = []
    for d in DILATIONS:
        dist = rel * d
        max_exact = N_BUCKET // 2
        ratio = np.log(np.maximum(dist, 1).astype(np.float32) / max_exact) / np.float32(math.log(MAX_DIST / max_exact))
        large = max_exact + (ratio * (N_BUCKET - max_exact)).astype(np.int32)
        large = np.minimum(large, N_BUCKET - 1)
        out.append(np.where(dist < max_exact, dist, large).astype(np.int32))
    return np.stack(out)


def _prenorm(x2, g):
    n, d = x2.shape
    tm = 1024

    def body(x_ref, g_ref, h_ref, rs_ref):
        x = x_ref[...]
        rs = lax.rsqrt(jnp.mean(x * x, axis=-1, keepdims=True) + RMS_EPS)
        h_ref[...] = (x * rs * g_ref[...]).astype(BF16)
        rs_ref[...] = rs

    return pl.pallas_call(
        body, name="prenorm", grid=(n // tm,),
        in_specs=[pl.BlockSpec((tm, d), lambda i: (i, 0)), pl.BlockSpec((1, d), lambda i: (0, 0))],
        out_specs=[pl.BlockSpec((tm, d), lambda i: (i, 0)), pl.BlockSpec((tm, 1), lambda i: (i, 0))],
        out_shape=[SDS((n, d), BF16), SDS((n, 1), F32)], compiler_params=_params(("parallel",)))(x2, g)


def _mm(a, b, tn, name):
    m, k = a.shape
    n = b.shape[1]
    tm = MM_ROWS

    def body(a_ref, b_ref, o_ref):
        o_ref[...] = _dot(a_ref[...], b_ref[...])

    return pl.pallas_call(
        body, name=name, grid=(n // tn, m // tm),
        in_specs=[pl.BlockSpec((tm, k), lambda j, i: (i, 0)), pl.BlockSpec((k, tn), lambda j, i: (0, j))],
        out_specs=pl.BlockSpec((tm, tn), lambda j, i: (i, j)),
        out_shape=SDS((m, n), F32), compiler_params=_params(("parallel", "parallel")))(a, b)


def _mm_nt(a, b, after, name):
    m, seg = a.shape[1], a.shape[2]
    d, k = b.shape
    tm = MM_ROWS
    per = 3
    tk = per * seg

    def body(a_ref, b_ref, after_ref, o_ref):
        r = sum(_dot_nt(a_ref[j].astype(BF16), b_ref[:, seg * j:seg * (j + 1)]) for j in range(per))

        @pl.when(pl.program_id(1) == 0)
        def _():
            o_ref[...] = r

        @pl.when(pl.program_id(1) != 0)
        def _():
            o_ref[...] += r

    in_specs = [pl.BlockSpec((per, tm, seg), lambda i, j: (j, i, 0)), pl.BlockSpec((d, tk), lambda i, j: (0, j)),
                pl.BlockSpec(after.shape, lambda i, j: (0, 0))]
    return pl.pallas_call(
        body, name=name, grid=(m // tm, k // tk), in_specs=in_specs, out_specs=pl.BlockSpec((tm, d), lambda i, j: (i, 0)),
        out_shape=SDS((m, d), F32), compiler_params=_params(("parallel", "arbitrary")))(a, b, after)


def _dh_rest_prenorm_bwd(a_list, b_list, acc, x2, rs, g1, dxo):
    m, d = acc.shape
    tm = 512
    n = len(a_list)

    def body(*refs):
        x_ref, rs_ref, g_ref, dxo_ref, gx_ref, dg_ref = refs[2 * n + 1:]
        dh = refs[2 * n][...]
        for a_ref, b_ref in zip(refs[:n], refs[n:2 * n]):
            dh = dh + _dot_nt(a_ref[...].astype(BF16), b_ref[...])
        x, r = x_ref[...], rs_ref[...]
        gd = dh * g_ref[...]
        gx_ref[...] = dxo_ref[...] + r * (gd - x * (r * r) * jnp.mean(gd * x, axis=-1, keepdims=True))
        dg = jnp.sum(dh * x * r, axis=0, keepdims=True)

        @pl.when(pl.program_id(0) == 0)
        def _():
            dg_ref[...] = dg

        @pl.when(pl.program_id(0) != 0)
        def _():
            dg_ref[...] += dg

    t = pl.BlockSpec((tm, d), lambda i: (i, 0))
    in_specs = [pl.BlockSpec((tm, a.shape[1]), lambda i: (i, 0)) for a in a_list]
    in_specs += [pl.BlockSpec(b.shape, lambda i: (0, 0)) for b in b_list]
    in_specs += [t, t, pl.BlockSpec((tm, 1), lambda i: (i, 0)), pl.BlockSpec((1, d), lambda i: (0, 0)), t]
    return pl.pallas_call(
        body, name="dh_rest_prenorm_bwd", grid=(m // tm,), in_specs=in_specs, out_specs=[t, pl.BlockSpec((1, d), lambda i: (0, 0))],
        out_shape=[SDS((m, d), F32), SDS((1, d), F32)], compiler_params=_params(("arbitrary",)))(*a_list, *b_list, acc, x2, rs, g1, dxo)


def _mm_tn(a, b, tn, name):
    split = b.ndim == 3
    m, k1 = a.shape
    per = 3 if split else 1
    seg = b.shape[2] if split else tn
    tn = per * seg
    n2 = b.shape[0] * seg if split else b.shape[1]
    tm = MM_ROWS

    def body(a_ref, b_ref, o_ref):
        first = pl.program_id(1) == 0
        for j in range(per):
            r = _dot_tn(a_ref[...], (b_ref[j] if split else b_ref[...]).astype(BF16))
            cols = slice(seg * j, seg * (j + 1))

            @pl.when(first)
            def _(r=r, cols=cols):
                o_ref[:, cols] = r

            @pl.when(jnp.logical_not(first))
            def _(r=r, cols=cols):
                o_ref[:, cols] += r

    b_spec = pl.BlockSpec((per, tm, seg), lambda j, i: (j, i, 0)) if split else pl.BlockSpec((tm, tn), lambda j, i: (i, j))
    return pl.pallas_call(
        body, name=name, grid=(n2 // tn, m // tm),
        in_specs=[pl.BlockSpec((tm, k1), lambda j, i: (i, 0)), b_spec],
        out_specs=pl.BlockSpec((k1, tn), lambda j, i: (0, j)),
        out_shape=SDS((k1, n2), F32), compiler_params=_params(("parallel", "arbitrary")))(a, b)


def _ds(start, d):
    return pl.ds(start, QB) if d == 1 else pl.ds(start, QB, stride=d)


def _fill_bias(tab_ref, bidx_ref, bias_sc, hp):
    for g in range(3):
        bi = bidx_ref[g]
        for h in range(2):
            acc = jnp.zeros((QB, 2 * QB), F32)
            for j in range(N_BUCKET):
                acc = jnp.where(bi == j, tab_ref[j, g * N_HEAD + hp * 2 + h], acc)
            bias_sc[g * 2 + h] = acc


def _block_starts(it, d, nb):
    rho = it // nb
    n = it % nb
    st = rho + d * QB * n
    stp = rho + d * QB * jnp.maximum(n - 1, 0)
    if d == 1:
        st, stp = pl.multiple_of(QB * it, QB), pl.multiple_of(QB * jnp.maximum(it - 1, 0), QB)
    return st, stp, n > 0


ATTN_BLOCKS_FWD = 16
ATTN_BLOCKS_BWD = 4


def _bdot3(a, b, dims):
    return lax.dot_general(a, b, (dims, ((0,), (0,))), preferred_element_type=F32)


def _attn_operands(q_ref, k_ref, v_ref, bias_sc, g, d, nb, it0, nblk):
    two = nb > 1
    nk = 2 * QB if two else QB
    ii = lax.broadcasted_iota(jnp.int32, (QB, nk), 0)
    cc = lax.broadcasted_iota(jnp.int32, (QB, nk), 1)
    qs, ks, vs, pens, starts = [], [], [], [], []
    for u in range(nblk):
        st, stp, hasprev = _block_starts(it0 + u, d, nb)
        qf = q_ref[0, _ds(st, d), :]
        if two:
            kf = jnp.concatenate([k_ref[0, _ds(stp, d), :], k_ref[0, _ds(st, d), :]], axis=0).astype(BF16)
            vf = jnp.concatenate([v_ref[0, _ds(stp, d), :], v_ref[0, _ds(st, d), :]], axis=0).astype(BF16)
            own = jnp.logical_and(cc >= QB, ii >= cc - QB)
            prev = jnp.logical_and(jnp.logical_and(cc < QB, cc >= ii), hasprev)
            pen = jnp.where(jnp.logical_or(own, prev), 0.0, NEG)
        else:
            kf, vf = k_ref[0, _ds(st, d), :].astype(BF16), v_ref[0, _ds(st, d), :].astype(BF16)
            pen = jnp.where(ii >= cc, 0.0, NEG)
        for h in range(2):
            qs.append(_one_head(qf, h).astype(BF16))
            ks.append(kf)
            vs.append(vf)
            pens.append(pen + (bias_sc[g * 2 + h] if two else bias_sc[g * 2 + h, :, QB:2 * QB]))
        starts.append((st, stp))
    return _stack(qs), _stack(ks), _stack(vs), _stack(pens), starts


def _one_head(x, h):
    lane = lax.broadcasted_iota(jnp.int32, x.shape, 1)
    return jnp.where(lane >= HEAD if h == 1 else lane < HEAD, x, 0.0)


def _pick_heads(x, u):
    lane = lax.broadcasted_iota(jnp.int32, x.shape[1:], 1)
    return jnp.where(lane < HEAD, x[2 * u], x[2 * u + 1])


def _add_heads(x, u):
    return x[2 * u] + x[2 * u + 1]


def _attn_fwd(qkv3, rel_bias, bidx):
    bsz, s, _ = qkv3.shape
    rt = 256

    def body(tab_ref, bidx_ref, *refs):
        q_refs, k_refs, v_refs = refs[0:3], refs[3:6], refs[6:9]
        o_ref, lse_ref = refs[9:11]
        bias_sc, num_sc, den_sc, m_sc = refs[11:]
        pl.when(pl.program_id(1) == 0)(lambda: _fill_bias(tab_ref, bidx_ref, bias_sc, pl.program_id(0)))
        for g, d in enumerate(DILATIONS):
            nb = s // (QB * d)

            def blk(it, c, g=g, d=d, nb=nb):
                q, k, v, bias, starts = _attn_operands(q_refs[g], k_refs[g], v_refs[g], bias_sc, g, d, nb, it * ATTN_BLOCKS_FWD,
                                                       ATTN_BLOCKS_FWD)
                sc = _bdot3(q, k, ((2,), (2,))) * SCALE + bias
                m = jnp.max(sc, axis=-1, keepdims=True)
                p = jnp.exp(sc - m)
                den = jnp.sum(p, axis=-1, keepdims=True)
                num = _bdot3(p.astype(BF16), v, ((2,), (1,)))
                den, m = jnp.broadcast_to(den, num.shape), jnp.broadcast_to(m, num.shape)
                for u, (st, _) in enumerate(starts):
                    num_sc[g, _ds(st, d), :] = _pick_heads(num, u)
                    den_sc[g, _ds(st, d), :] = _pick_heads(den, u)
                    m_sc[g, _ds(st, d), :] = _pick_heads(m, u)
                return c

            lax.fori_loop(0, s // QB // ATTN_BLOCKS_FWD, blk, 0)

        def merge(i, c):
            rows = pl.ds(pl.multiple_of(i * rt, rt), rt)
            m0, m1, m2 = m_sc[0, rows, :], m_sc[1, rows, :], m_sc[2, rows, :]
            mall = jnp.maximum(jnp.maximum(m0, m1), m2)
            w0, w1, w2 = jnp.exp(m0 - mall), jnp.exp(m1 - mall), jnp.exp(m2 - mall)
            num = w0 * num_sc[0, rows, :] + w1 * num_sc[1, rows, :] + w2 * num_sc[2, rows, :]
            den = w0 * den_sc[0, rows, :] + w1 * den_sc[1, rows, :] + w2 * den_sc[2, rows, :]
            o_ref[0, rows, :] = num / den
            lse_ref[0, rows, :] = mall + jnp.log(den)
            return c

        lax.fori_loop(0, s // rt, merge, 0)

    col = lambda w, g: (lambda hp, b: (b, 0, (w * 3 + g) * 4 + hp))
    in_specs = [pl.BlockSpec(memory_space=pltpu.SMEM), pl.BlockSpec((3, QB, 2 * QB), lambda hp, b: (0, 0, 0))]
    in_specs += [pl.BlockSpec((1, s, LANE), col(w, g)) for w in range(3) for g in range(3)]
    out_spec = pl.BlockSpec((1, s, LANE), lambda hp, b: (b, 0, hp))
    return pl.pallas_call(
        body, name="attn_fwd", grid=(4, bsz), in_specs=in_specs, out_specs=[out_spec, out_spec],
        out_shape=[SDS((bsz, s, WIDTH), F32), SDS((bsz, s, WIDTH), F32)],
        scratch_shapes=[pltpu.VMEM((6, QB, 2 * QB), F32), pltpu.VMEM((3, s, LANE), F32), pltpu.VMEM((3, s, LANE), F32),
                        pltpu.VMEM((3, s, LANE), F32)],
        compiler_params=_params(("arbitrary", "arbitrary")))(rel_bias, bidx, *([qkv3] * 9))


def _attn_bwd(qkv3, o3, lse3, do3, rel_bias, bidx):
    bsz, s, _ = qkv3.shape
    rt = 256

    def body(tab_ref, bidx_ref, *refs):
        q_refs, k_refs, v_refs = refs[0:3], refs[3:6], refs[6:9]
        o_ref, lse_ref, do_ref, dqkv_ref, db_ref, bias_sc, delta_sc, acc_sc = refs[9:]
        dq_refs, dk_refs, dv_refs = ([acc_sc.at[w * 3 + g] for g in range(3)] for w in range(3))

        @pl.when(pl.program_id(1) == 0)
        def _():
            _fill_bias(tab_ref, bidx_ref, bias_sc, pl.program_id(0))
            db_ref[...] = jnp.zeros_like(db_ref)

        def prep(i, c):
            rows = pl.ds(pl.multiple_of(i * rt, rt), rt)
            prod = do_ref[0, rows, :] * o_ref[0, rows, :]
            d0 = jnp.sum(prod[:, :HEAD], axis=-1, keepdims=True)
            d1 = jnp.sum(prod[:, HEAD:], axis=-1, keepdims=True)
            delta_sc[rows, :] = jnp.concatenate([jnp.broadcast_to(d0, (rt, HEAD)), jnp.broadcast_to(d1, (rt, HEAD))], axis=1)
            z = jnp.zeros((rt, LANE), F32)
            for g in range(3):
                dk_refs[g][0, rows, :] = z
                dv_refs[g][0, rows, :] = z
            return c

        lax.fori_loop(0, s // rt, prep, 0)
        for g, d in enumerate(DILATIONS):
            nb = s // (QB * d)

            def blk(it, c, g=g, d=d, nb=nb):
                q, k, v, bias, starts = _attn_operands(q_refs[g], k_refs[g], v_refs[g], bias_sc, g, d, nb, it * ATTN_BLOCKS_BWD,
                                                       ATTN_BLOCKS_BWD)
                dos, lses, deltas = [], [], []
                for st, _ in starts:
                    dof, lsef, delf = do_ref[0, _ds(st, d), :], lse_ref[0, _ds(st, d), :], delta_sc[_ds(st, d), :]
                    for h in range(2):
                        dos.append(_one_head(dof, h).astype(BF16))
                        lses.append(lsef[:, HEAD * h:HEAD * h + 1])
                        deltas.append(delf[:, HEAD * h:HEAD * h + 1])
                do, lse, delta = _stack(dos), _stack(lses), _stack(deltas)
                p = jnp.exp(_bdot3(q, k, ((2,), (2,))) * SCALE + bias - lse)
                dv = _bdot3(p.astype(BF16), do, ((1,), (1,)))
                ds = p * (_bdot3(do, v, ((2,), (2,))) - delta)
                dsb = ds.astype(BF16)
                dq = _bdot3(dsb, k, ((2,), (1,))) * SCALE
                dk = _bdot3(dsb, q, ((1,), (1,))) * SCALE
                two = nb > 1
                for h in range(2):
                    dsum = sum(ds[2 * u + h] for u in range(ATTN_BLOCKS_BWD))
                    if two:
                        db_ref[0, g * 2 + h] += dsum
                    else:
                        db_ref[0, g * 2 + h, :, QB:2 * QB] += dsum
                for u, (st, stp) in enumerate(starts):
                    dq_refs[g][0, _ds(st, d), :] = _pick_heads(dq, u)
                    if two:
                        dk_refs[g][0, _ds(stp, d), :] += _add_heads(dk[:, :QB], u)
                        dv_refs[g][0, _ds(stp, d), :] += _add_heads(dv[:, :QB], u)
                    dk_refs[g][0, _ds(st, d), :] += _add_heads(dk[:, QB:] if two else dk, u)
                    dv_refs[g][0, _ds(st, d), :] += _add_heads(dv[:, QB:] if two else dv, u)
                return c

            lax.fori_loop(0, s // QB // ATTN_BLOCKS_BWD, blk, 0)

        def flush(i, c):
            rows = pl.ds(pl.multiple_of(i * rt, rt), rt)
            for j in range(9):
                dqkv_ref[j, 0, rows, :] = acc_sc[j, 0, rows, :].astype(BF16)
            return c

        lax.fori_loop(0, s // rt, flush, 0)

    col = lambda w, g: (lambda hp, b: (b, 0, (w * 3 + g) * 4 + hp))
    blk_spec = pl.BlockSpec((1, s, LANE), lambda hp, b: (b, 0, hp))
    in_specs = [pl.BlockSpec(memory_space=pltpu.SMEM), pl.BlockSpec((3, QB, 2 * QB), lambda hp, b: (0, 0, 0))]
    in_specs += [pl.BlockSpec((1, s, LANE), col(w, g)) for w in range(3) for g in range(3)]
    in_specs += [blk_spec] * 3
    out_specs = [pl.BlockSpec((9, 1, s, LANE), lambda hp, b: (0, b, 0, hp)), pl.BlockSpec((1, 6, QB, 2 * QB), lambda hp, b: (hp, 0, 0, 0))]
    out_shape = [SDS((9, bsz, s, WIDTH), BF16), SDS((4, 6, QB, 2 * QB), F32)]
    return pl.pallas_call(
        body, name="attn_bwd", grid=(4, bsz), in_specs=in_specs, out_specs=out_specs, out_shape=out_shape,
        scratch_shapes=[pltpu.VMEM((6, QB, 2 * QB), F32), pltpu.VMEM((s, LANE), F32), pltpu.VMEM((9, 1, s, LANE), F32)],
        compiler_params=_params(("parallel", "arbitrary")))(rel_bias, bidx, *([qkv3] * 9), o3, lse3, do3)


def _bias_grad(dbias, bidx):
    def body(db_ref, bidx_ref, o_ref):
        lane = lax.broadcasted_iota(jnp.int32, (1, LANE), 1)
        for g in range(3):
            bi = bidx_ref[g]
            for hp in range(4):
                for h in range(2):
                    mat = db_ref[hp, g * 2 + h]
                    row = jnp.zeros((1, LANE), F32)
                    for j in range(N_BUCKET):
                        part = jnp.sum(jnp.where(bi == j, mat, 0.0), axis=0, keepdims=True)
                        row = jnp.where(lane == j, jnp.sum(part, axis=1, keepdims=True), row)
                    hd = g * N_HEAD + hp * 2 + h
                    o_ref[hd:hd + 1, :] = row

    return pl.pallas_call(body, name="bias_grad", out_shape=SDS((3 * N_HEAD, LANE), F32), compiler_params=_params())(dbias, bidx)


def _pre_fn(r, k0, v, wl, al, w0, wup, a0, aup, kk_, ka_):
    u = w0 + _bdot(jnp.tanh(wl), wup)
    lw = -jnp.exp(-_softplus(-u) - 0.5)
    a = jax.nn.sigmoid(a0 + _bdot(al, aup))
    kkraw = k0 * kk_
    k = k0 * (1.0 + (a - 1.0) * ka_)
    return r, lw, k, v, kkraw, a


PRE_SPLIT = (0, WIDTH, 2 * WIDTH, 3 * WIDTH, 3 * WIDTH + LORA, 3 * WIDTH + 2 * LORA)


def _pre_pieces(prs):
    return [prs[:, a:b] for a, b in zip(PRE_SPLIT[:-1], PRE_SPLIT[1:])]


PRE_TT = 512


def _shifted(pr_ref, edge_ref, first, back):
    pr = pr_ref[0]
    tt = pr.shape[0]
    row = lax.broadcasted_iota(jnp.int32, (tt, 1), 0)
    if back:
        edge = jnp.where(first, 0.0, edge_ref[0, 7:8, :])
        return jnp.where(row == 0, edge, pltpu.roll(pr, 1, axis=0))
    edge = jnp.where(first, 0.0, edge_ref[0, 0:1, :])
    return jnp.where(row == tt - 1, edge, pltpu.roll(pr, tt - 1, axis=0))


def _rwkv_pre(pr3, mix, w0, wup, a0, aup, kk_, ka_):
    bsz, s, _ = pr3.shape
    tt = PRE_TT

    def body(pr_ref, edge_ref, mix_ref, w0_ref, wup_ref, a0_ref, aup_ref, kk_ref, ka_ref, *outs):
        pr = pr_ref[0]
        prev = _shifted(pr_ref, edge_ref, pl.program_id(1) == 0, True)
        prs = pr + (prev - pr) * mix_ref[...]
        vals = _pre_fn(*_pre_pieces(prs), w0_ref[...], wup_ref[...].astype(F32), a0_ref[...], aup_ref[...].astype(F32), kk_ref[...],
                       ka_ref[...])
        for o, val in zip(outs, vals):
            o[0] = val

    vec = lambda n: pl.BlockSpec((1, n), lambda b, i: (0, 0))
    mat = pl.BlockSpec((LORA, WIDTH), lambda b, i: (0, 0))
    in_specs = [pl.BlockSpec((1, tt, PR_COLS), lambda b, i: (b, i, 0)),
                pl.BlockSpec((1, 8, PR_COLS), lambda b, i: (b, jnp.maximum(i * (tt // 8) - 1, 0), 0)),
                vec(PR_COLS), vec(WIDTH), mat, vec(WIDTH), mat, vec(WIDTH), vec(WIDTH)]
    out_spec = pl.BlockSpec((1, tt, WIDTH), lambda b, i: (b, i, 0))
    return pl.pallas_call(
        body, name="rwkv_pre", grid=(bsz, s // tt), in_specs=in_specs, out_specs=[out_spec] * 6,
        out_shape=[SDS((bsz, s, WIDTH), F32)] * 6, compiler_params=_params(("parallel", "parallel")))(
            pr3, pr3, mix, w0, wup, a0, aup, kk_, ka_)


def _rwkv_pre_bwd(pr3, cots, mix, w0, wup, a0, aup, kk_, ka_):
    bsz, s, _ = pr3.shape
    tt = PRE_TT

    def body(pr_ref, edge_ref, c0, c1, c2, c3, c4, c5, mix_ref, w0_ref, wup_ref, a0_ref, aup_ref, kk_ref, ka_ref,
             dprs_ref, dmix_ref, dw0_ref, dwup_ref, da0_ref, daup_ref, dkk_ref, dka_ref):
        pr = pr_ref[0]
        prev = _shifted(pr_ref, edge_ref, pl.program_id(1) == 0, True)
        prs = pr + (prev - pr) * mix_ref[...]
        _, vjp = jax.vjp(_pre_fn, *_pre_pieces(prs), w0_ref[...], wup_ref[...].astype(F32), a0_ref[...], aup_ref[...].astype(F32),
                         kk_ref[...], ka_ref[...])
        grads = vjp(tuple(c[0] for c in (c0, c1, c2, c3, c4, c5)))
        for piece, a, b in zip(grads[:5], PRE_SPLIT[:-1], PRE_SPLIT[1:]):
            dprs_ref[0, :, a:b] = piece
        dw0, dwup, da0, daup, dkk, dka = grads[5:]
        dprs = dprs_ref[0]
        grads = (jnp.sum(dprs * (prev - pr), axis=0, keepdims=True), dw0, dwup, da0, daup, dkk, dka)
        refs = (dmix_ref, dw0_ref, dwup_ref, da0_ref, daup_ref, dkk_ref, dka_ref)
        first = jnp.logical_and(pl.program_id(0) == 0, pl.program_id(1) == 0)

        @pl.when(first)
        def _():
            for r_, g_ in zip(refs, grads):
                r_[...] = g_

        @pl.when(jnp.logical_not(first))
        def _():
            for r_, g_ in zip(refs, grads):
                r_[...] += g_

    vec = lambda n: pl.BlockSpec((1, n), lambda b, i: (0, 0))
    mat = pl.BlockSpec((LORA, WIDTH), lambda b, i: (0, 0))
    tile = pl.BlockSpec((1, tt, WIDTH), lambda b, i: (b, i, 0))
    in_specs = [pl.BlockSpec((1, tt, PR_COLS), lambda b, i: (b, i, 0)),
                pl.BlockSpec((1, 8, PR_COLS), lambda b, i: (b, jnp.maximum(i * (tt // 8) - 1, 0), 0))]
    in_specs += [tile] * 6 + [vec(PR_COLS), vec(WIDTH), mat, vec(WIDTH), mat, vec(WIDTH), vec(WIDTH)]
    out_specs = [pl.BlockSpec((1, tt, PR_COLS), lambda b, i: (b, i, 0)), vec(PR_COLS), vec(WIDTH), mat, vec(WIDTH), mat,
                 vec(WIDTH), vec(WIDTH)]
    out_shape = [SDS((bsz, s, PR_COLS), F32), SDS((1, PR_COLS), F32), SDS((1, WIDTH), F32), SDS((LORA, WIDTH), F32),
                 SDS((1, WIDTH), F32), SDS((LORA, WIDTH), F32), SDS((1, WIDTH), F32), SDS((1, WIDTH), F32)]
    return pl.pallas_call(
        body, name="rwkv_pre_bwd", grid=(bsz, s // tt), in_specs=in_specs, out_specs=out_specs, out_shape=out_shape,
        compiler_params=_params(("arbitrary", "arbitrary")))(pr3, pr3, *cots, mix, w0, wup, a0, aup, kk_, ka_)


def _shift_bwd(dprs3, mix):
    bsz, s, _ = dprs3.shape
    tt = PRE_TT
    nt = s // tt

    def body(d_ref, edge_ref, mix_ref, o_ref):
        nxt = _shifted(d_ref, edge_ref, pl.program_id(1) == nt - 1, False)
        m = mix_ref[...]
        o_ref[0] = (d_ref[0] * (1.0 - m) + nxt * m).astype(BF16)

    in_specs = [pl.BlockSpec((1, tt, PR_COLS), lambda b, i: (b, i, 0)),
                pl.BlockSpec((1, 8, PR_COLS), lambda b, i: (b, jnp.minimum((i + 1) * (tt // 8), s // 8 - 1), 0)),
                pl.BlockSpec((1, PR_COLS), lambda b, i: (0, 0))]
    return pl.pallas_call(
        body, name="shift_bwd", grid=(bsz, nt), in_specs=in_specs, out_specs=pl.BlockSpec((1, tt, PR_COLS), lambda b, i: (b, i, 0)),
        out_shape=SDS((bsz, s, PR_COLS), BF16), compiler_params=_params(("parallel", "parallel")))(dprs3, dprs3, mix)


_NN, _NT, _TN = ((2,), (1,)), ((2,), (2,)), ((1,), (1,))


def _dot3_bf16(a, b, dims):
    return lax.dot_general(a.astype(BF16), b.astype(BF16), (dims, ((0,), (0,))), preferred_element_type=F32)


class _Dots:
    def __init__(self, fwd):
        def make(dims, da_rule, db_rule):
            @jax.custom_vjp
            def f(a, b):
                return fwd(a, b, dims)

            f.defvjp(lambda a, b: (f(a, b), (a, b)), lambda res, g: (da_rule(*res, g), db_rule(*res, g)))
            return f

        one = _dot3_bf16
        self.mm = make(_NN, lambda a, b, g: one(g, b, _NT), lambda a, b, g: one(a, g, _TN))
        self.mm_nt = make(_NT, lambda a, b, g: one(g, b, _NN), lambda a, b, g: one(g, a, _TN))
        self.mm_tn = make(_TN, lambda a, b, g: one(b, g, _NT), lambda a, b, g: one(a, g, _NN))

        def powers(aab):
            ps = [aab]
            while 2 ** len(ps) < aab.shape[1]:
                ps.append(fwd(ps[-1], ps[-1], _NN))
            return ps

        def apply(ps, z, dims):
            for p in ps:
                z = z + fwd(p, z, dims)
            return z

        @jax.custom_vjp
        def solve(aab, z):
            return apply(powers(aab), z, _NN)

        def solve_fwd(aab, z):
            ps = powers(aab)
            x = apply(ps, z, _NN)
            return x, (ps, x)

        def solve_bwd(res, g):
            ps, x = res
            dz = apply(ps, g, _TN)
            return fwd(dz, x, _NT), dz

        solve.defvjp(solve_fwd, solve_bwd)
        self.solve = solve


_ONE_PASS = _Dots(_dot3_bf16)
_bmm, _bmm_tn = _ONE_PASS.mm, _ONE_PASS.mm_tn


def _chunk_fn(s0t, r, lw, k, v, kkraw, a, rk, lnw, lnb, first=False, d=_ONE_PASS):
    c = r.shape[1]
    at, rt, btc, ktc, gc, aab, arb, xv, arkv, ain, bin_ = _chunk_core(r, lw, k, v, kkraw, a, d)
    rs = d.mm(jnp.concatenate([at, rt], axis=1), s0t)
    u = d.solve(aab, rs[:, :c] + xv)
    y = rs[:, c:] + d.mm(arb, u) + arkv
    if first:
        y = _with_early_rows(y, r, lw, k, v, ain, bin_)
    gcol = jnp.sum(_diag(gc), axis=2, keepdims=True)
    sct = gcol * s0t + d.mm_tn(jnp.concatenate([btc, ktc], axis=1), jnp.concatenate([u, v], axis=1))
    return _post(y, r, k, v, rk, lnw, lnb), sct


def _diag(gc):
    return jnp.where(_masks(HEAD)[2], gc, 0.0)


def _with_early_rows(y, r, lw, k, v, ain, bin_):
    early = _early_rows(r[:2], lw[:2], k[:2], v[:2], ain[:2], bin_[:2])
    return jnp.concatenate([jnp.concatenate([early, y[:2, EARLY:]], axis=1), y[2:]], axis=0)


def _early_rows(r, lw, k, v, ain, bin_):
    cols = lambda x: _stack([jnp.transpose(x[h]) for h in range(2)])
    wc, bc, kc = cols(jnp.exp(lw)), cols(bin_), cols(k)
    st = jnp.zeros((2, HEAD, HEAD), F32)
    rows = []
    for t in range(EARLY):
        sa = _ONE_PASS.mm(ain[:, t:t + 1], st)
        st = st * wc[:, :, t:t + 1] + bc[:, :, t:t + 1] * sa + kc[:, :, t:t + 1] * v[:, t:t + 1]
        rows.append(_ONE_PASS.mm(r[:, t:t + 1], st))
    return jnp.concatenate(rows, axis=1)


def _chunk_rows(c):
    return pl.ds(c * CHUNK, CHUNK) if isinstance(c, int) else pl.ds(pl.multiple_of(c * CHUNK, CHUNK), CHUNK)


def _stack(xs):
    return jnp.concatenate([x[None] for x in xs], axis=0)


def _pairs(ref, chunks):
    tiles = [ref[0, _chunk_rows(c), :] for c in chunks]
    return _stack([t[:, HEAD * h:HEAD * h + HEAD] for t in tiles for h in range(2)])


def _unpair(vals, j):
    return jnp.concatenate([vals[2 * j], vals[2 * j + 1]], axis=1)


def _masks(c):
    ii = lax.broadcasted_iota(jnp.int32, (c, c), 0)
    jj = lax.broadcasted_iota(jnp.int32, (c, c), 1)
    return ii > jj, ii >= jj, ii == jj


@jax.custom_vjp
def _running_sum(lw):
    return _tri_dot(lw, _NN)


def _tri_dot(x, dims):
    g_, c, _ = x.shape
    tri = jnp.broadcast_to(_masks(c)[1].astype(BF16), (g_, c, c))
    head = x.astype(BF16)
    rest = (x - head.astype(F32)).astype(BF16)
    return lax.dot_general(tri, head, (dims, ((0,), (0,))), preferred_element_type=F32) + \
        lax.dot_general(tri, rest, (dims, ((0,), (0,))), preferred_element_type=F32)


_running_sum.defvjp(lambda lw: (_running_sum(lw), None), lambda _, ct: (_tri_dot(ct, _TN),))


def _chunk_core(r, lw, k, v, kkraw, a, d=_ONE_PASS):
    g_, c = r.shape[0], r.shape[1]
    nrm = jnp.sqrt(jnp.sum(kkraw * kkraw, axis=-1, keepdims=True))
    kkn = kkraw / jnp.maximum(nrm, 1e-12)
    ain, bin_ = -kkn, kkn * a
    strict, incl, _ = _masks(c)
    lg = _running_sum(lw)
    g, gp, gi = jnp.exp(lg), jnp.exp(lg - lw), jnp.exp(-lg)
    at, rt, bt, kt = ain * gp, r * g, bin_ * gi, k * gi
    aa = d.mm_nt(jnp.concatenate([at, rt], axis=1), jnp.concatenate([bt, kt], axis=1))
    aab = jnp.where(strict, aa[:, :c, :c], 0.0)
    aak = jnp.where(strict, aa[:, :c, c:], 0.0)
    arb = jnp.where(incl, aa[:, c:, :c], 0.0)
    ark = jnp.where(incl, aa[:, c:, c:], 0.0)
    akv = d.mm(jnp.concatenate([aak, ark], axis=1), v)
    gc = g[:, c - 1:c, :]
    return at, rt, bt * gc, kt * gc, gc, aab, arb, akv[:, :c], akv[:, c:], ain, bin_


def _lane_sum(x):
    return jnp.sum(x, axis=-1, keepdims=True)


def _lane_sum_mxu(x):
    g, c, n = x.shape
    x2 = x.reshape(g * c, n)
    head = x2.astype(BF16)
    rest = (x2 - head.astype(F32)).astype(BF16)
    ones = jnp.ones((n, n), BF16)
    return (_dot(head, ones) + _dot(rest, ones)).reshape(g, c, n)


def _post(y, r, k, v, rk, lnw, lnb, lane_sum=_lane_sum):
    mu = lane_sum(y) * (1.0 / HEAD)
    var = lane_sum(jnp.square(y - mu)) * (1.0 / HEAD)
    yn = (y - mu) * lax.rsqrt(var + GN_EPS) * lnw + lnb
    return yn + lane_sum(r * k * rk) * v


def _chunk_consts(r, lw, k, v, kkraw, a, first=False):
    d = _ONE_PASS
    at, rt, btc, ktc, gc, aab, arb, xv, arkv, ain, bin_ = _chunk_core(r, lw, k, v, kkraw, a, d)
    z = d.solve(aab, jnp.concatenate([at, xv], axis=2))
    ryv = jnp.concatenate([rt, arkv], axis=2) + d.mm(arb, z)
    if first:
        ryv = jnp.concatenate([ryv[:, :, :HEAD], _with_early_rows(ryv[:, :, HEAD:], r, lw, k, v, ain, bin_)], axis=2)
    mkv = d.mm_tn(btc, z) + jnp.concatenate([_diag(gc), d.mm_tn(ktc, v)], axis=2)
    return mkv, ryv


def _rwkv_scan(ins, rk, lnw, lnb):
    bsz, s, _ = ins[0].shape
    nch = s // CHUNK

    def consts_body(r_ref, lw_ref, k_ref, v_ref, kk_ref, a_ref, mkv_ref, ry_ref, yv_ref):
        def group(i, carry):
            chunks = [i * CHUNK_GROUP + j for j in range(CHUNK_GROUP)]
            mkv, ryv = _chunk_consts(*[_pairs(ref, chunks) for ref in (r_ref, lw_ref, k_ref, v_ref, kk_ref, a_ref)],
                                     first=isinstance(i, int) and i == 0)
            for j, c in enumerate(chunks):
                for h in range(2):
                    mkv_ref[0, 0, c, h] = mkv[2 * j + h]
                ry_ref[0, _chunk_rows(c), :] = jnp.concatenate([ryv[2 * j][:, :HEAD], ryv[2 * j + 1][:, :HEAD]], axis=1)
                yv_ref[0, _chunk_rows(c), :] = jnp.concatenate([ryv[2 * j][:, HEAD:], ryv[2 * j + 1][:, HEAD:]], axis=1)
            return carry

        group(0, 0)
        lax.fori_loop(1, nch // CHUNK_GROUP, group, 0)

    tile = pl.BlockSpec((1, s, LANE), lambda b, hp: (b, 0, hp))
    vec = pl.BlockSpec((1, LANE), lambda b, hp: (0, hp))
    mkv_spec = pl.BlockSpec((1, 1, nch, 2, HEAD, LANE), lambda b, hp: (b, hp, 0, 0, 0, 0))
    st_spec = pl.BlockSpec((1, 1, nch, 2, HEAD, HEAD), lambda b, hp: (b, hp, 0, 0, 0, 0))
    mkv, ry, yv = pl.pallas_call(
        consts_body, name="rwkv_consts", grid=(bsz, 4), in_specs=[tile] * 6, out_specs=[mkv_spec, tile, tile],
        out_shape=[SDS((bsz, 4, nch, 2, HEAD, LANE), F32), SDS((bsz, s, WIDTH), F32), SDS((bsz, s, WIDTH), F32)],
        compiler_params=_params(("parallel", "parallel")))(*ins)

    states = _chunk_recurrence(mkv, None, "rwkv_states")

    def out_body(ry_ref, yv_ref, r_ref, k_ref, v_ref, st_ref, rk_ref, lnw_ref, lnb_ref, o_ref):
        y, r, k, v, rk_, lnw_, lnb_ = _scan_rows(ry_ref, yv_ref, r_ref, k_ref, v_ref, st_ref, rk_ref, lnw_ref, lnb_ref)
        o = _post(y, r, k, v, rk_, lnw_, lnb_, _lane_sum_mxu)
        for j in range(CHUNK_GROUP):
            o_ref[0, _chunk_rows(j), :] = _unpair(o, j)

    o = pl.pallas_call(
        out_body, name="rwkv_out", grid=(bsz, 4, nch // CHUNK_GROUP), in_specs=_group_specs(5), out_specs=_group_specs(1)[0],
        out_shape=SDS((bsz, s, WIDTH), F32),
        compiler_params=_params(("parallel", "parallel", "parallel")))(ry, yv, ins[0], ins[2], ins[3], states, rk, lnw, lnb)
    return o, states, (mkv, ry, yv)


def _group_specs(n_tiles):
    tile = pl.BlockSpec((1, CHUNK_GROUP * CHUNK, LANE), lambda b, hp, t: (b, t, hp))
    if n_tiles == 1:
        return [tile]
    st = pl.BlockSpec((1, 1, CHUNK_GROUP, 2, HEAD, HEAD), lambda b, hp, t: (b, hp, t, 0, 0, 0))
    vec = pl.BlockSpec((1, LANE), lambda b, hp, t: (0, hp))
    return [tile] * n_tiles + [st] + [vec] * 3


def _scan_rows(ry_ref, yv_ref, r_ref, k_ref, v_ref, st_ref, rk_ref, lnw_ref, lnb_ref):
    chunks = list(range(CHUNK_GROUP))
    ry, yv, r, k, v = (_pairs(ref, chunks) for ref in (ry_ref, yv_ref, r_ref, k_ref, v_ref))
    st = _stack([st_ref[0, 0, c, h] for c in chunks for h in range(2)])
    vecs = [_stack([ref[:, HEAD * h:HEAD * h + HEAD] for _ in chunks for h in range(2)]) for ref in (rk_ref, lnw_ref, lnb_ref)]
    return (_bmm(ry, st) + yv, r, k, v, *vecs)


def _chunk_recurrence(mkv, q, name):
    bsz, _, nch = mkv.shape[:3]
    pairs = [(hp, h) for hp in range(4) for h in range(2)]

    def body(*refs):
        mkv_ref, out_ref, acc = refs[0], refs[-2], refs[-1]
        acc[...] = jnp.zeros_like(acc)

        def step(i, carry):
            c = i if q is None else nch - 1 - i
            cur = acc[...]
            for j, (hp, h) in enumerate(pairs):
                out_ref[0, hp, c, h] = cur[j]
            m = _stack([mkv_ref[0, hp, c, h] for hp, h in pairs])
            if q is None:
                acc[...] = _bmm(m[:, :, :HEAD], cur) + m[:, :, HEAD:]
            else:
                acc[...] = _bmm_tn(m[:, :, :HEAD], cur) + _stack([refs[1][0, hp, c, h] for hp, h in pairs])
            return carry

        lax.fori_loop(0, nch, step, 0)

    spec = lambda w: pl.BlockSpec((1, 4, nch, 2, HEAD, w), lambda b: (b, 0, 0, 0, 0, 0))
    return pl.pallas_call(
        body, name=name, grid=(bsz,), in_specs=[spec(LANE)] + ([] if q is None else [spec(HEAD)]), out_specs=spec(HEAD),
        out_shape=SDS((bsz, 4, nch, 2, HEAD, HEAD), F32), scratch_shapes=[pltpu.VMEM((8, HEAD, HEAD), F32)],
        compiler_params=_params(("parallel",)))(*([mkv] if q is None else [mkv, q]))


def _rwkv_scan_bwd(ins, states, consts, do3, rk, lnw, lnb):
    bsz, s, _ = ins[0].shape
    nch = s // CHUNK

    mkv, ry, yv = consts

    def q_body(do_ref, ry_ref, yv_ref, r_ref, k_ref, v_ref, st_ref, rk_ref, lnw_ref, lnb_ref, q_ref):
        y, r, k, v, rk_, lnw_, lnb_ = _scan_rows(ry_ref, yv_ref, r_ref, k_ref, v_ref, st_ref, rk_ref, lnw_ref, lnb_ref)
        _, vjp = jax.vjp(lambda y_: _post(y_, r, k, v, rk_, lnw_, lnb_), y)
        (dy,) = vjp(_pairs(do_ref, list(range(CHUNK_GROUP))))
        q = _bmm_tn(_pairs(ry_ref, list(range(CHUNK_GROUP))), dy)
        for j in range(CHUNK_GROUP):
            for h in range(2):
                q_ref[0, 0, j, h] = q[2 * j + h]

    specs = _group_specs(6)
    q = pl.pallas_call(
        q_body, name="rwkv_q", grid=(bsz, 4, nch // CHUNK_GROUP), in_specs=specs, out_specs=specs[6],
        out_shape=SDS((bsz, 4, nch, 2, HEAD, HEAD), F32),
        compiler_params=_params(("parallel", "parallel", "parallel")))(do3, ry, yv, ins[0], ins[2], ins[3], states, rk, lnw, lnb)

    dstates = _chunk_recurrence(mkv, q, "rwkv_dstates")

    def body(r_ref, lw_ref, k_ref, v_ref, kk_ref, a_ref, st_ref, dst_ref, do_ref, rk_ref, lnw_ref, lnb_ref,
             dr_ref, dlw_ref, dk_ref, dv_ref, dkk_ref, da_ref, drk_ref, dlnw_ref, dlnb_ref):
        chunks = list(range(BWD_GROUP))
        par_refs = (drk_ref, dlnw_ref, dlnb_ref)

        @pl.when(jnp.logical_and(pl.program_id(1) == 0, pl.program_id(2) == 0))
        def _():
            for ref in par_refs:
                ref[...] = jnp.zeros_like(ref)

        def group(first):
            per_pair = lambda ref: _stack([ref[0, 0, c, h] for c in chunks for h in range(2)])
            vecs = [_stack([ref[:, HEAD * h:HEAD * h + HEAD] for _ in chunks for h in range(2)]) for ref in (rk_ref, lnw_ref, lnb_ref)]
            _, vjp = jax.vjp(functools.partial(_chunk_fn, first=first, d=_ONE_PASS), per_pair(st_ref),
                             *[_pairs(ref, chunks) for ref in (r_ref, lw_ref, k_ref, v_ref, kk_ref, a_ref)], *vecs)
            grads = vjp((_pairs(do_ref, chunks), per_pair(dst_ref)))
            for ref, cot in zip((dr_ref, dlw_ref, dk_ref, dv_ref, dkk_ref, da_ref), grads[1:7]):
                for j, c in enumerate(chunks):
                    ref[0, _chunk_rows(c), :] = _unpair(cot, j)
            for ref, g_ in zip(par_refs, grads[7:10]):
                ref[...] += jnp.concatenate([sum(g_[2 * j + h] for j in range(BWD_GROUP)) for h in range(2)], axis=1)

        pl.when(pl.program_id(2) == 0)(functools.partial(group, True))
        pl.when(pl.program_id(2) != 0)(functools.partial(group, False))

    tt = BWD_GROUP * CHUNK
    tile = pl.BlockSpec((1, tt, LANE), lambda hp, b, t: (b, t, hp))
    vec = pl.BlockSpec((1, LANE), lambda hp, b, t: (0, hp))
    st_spec = pl.BlockSpec((1, 1, BWD_GROUP, 2, HEAD, HEAD), lambda hp, b, t: (b, hp, t, 0, 0, 0))
    outs = pl.pallas_call(
        body, name="rwkv_scan_bwd", grid=(4, bsz, s // tt), in_specs=[tile] * 6 + [st_spec, st_spec, tile] + [vec] * 3,
        out_specs=[tile] * 6 + [vec] * 3,
        out_shape=[SDS((bsz, s, WIDTH), F32)] * 6 + [SDS((1, WIDTH), F32)] * 3,
        compiler_params=_params(("parallel", "arbitrary", "arbitrary")))(*ins, states, dstates, do3, rk, lnw, lnb)
    return outs[:6], outs[6:]


def _head(o_attn, o_rwkv, z_attn, z_rwkv, gm, x2, tgt, wua, wur, wout, g2):
    n = x2.shape[0]
    tm = 256
    nt = n // tm
    d = D_MODEL

    def body(oa_ref, or_ref, za_ref, zr_ref, gm_ref, x_ref, t_ref, wua_ref, wur_ref, wout_ref, g2_ref,
             dxo_ref, doa_ref, dor_ref, dza_ref, dzr_ref, dgm_ref, dwua_ref, dwur_ref, dwout_ref, dg2_ref, loss_ref, lacc):
        i = pl.program_id(0)
        oa, orw, za, zr = oa_ref[...], or_ref[...], za_ref[...], zr_ref[...]
        ga, gb = gm_ref[:, 0:d], gm_ref[:, d:2 * d]
        am = (oa * _silu(za)).astype(BF16)
        bm = (orw * _silu(zr)).astype(BF16)
        ya, yb = _dot(am, wua_ref[...]), _dot(bm, wur_ref[...])
        sa, sb = jax.nn.sigmoid(ga), jax.nn.sigmoid(gb)
        merged = (sa * ya + sb * yb).astype(BF16)
        out = _dot(merged, wout_ref[...])
        rs = lax.rsqrt(jnp.mean(out * out, axis=-1, keepdims=True) + RMS_EPS)
        g2 = g2_ref[...]
        err = x_ref[...] + out * rs * g2 - t_ref[...]
        lpart = jnp.sum(err * err, axis=0, keepdims=True)
        dxo = err * (1.0 / d)
        dxo_ref[...] = dxo
        dg2 = jnp.sum(dxo * out * rs, axis=0, keepdims=True)
        gd = dxo * g2
        dout = (rs * (gd - out * (rs * rs) * jnp.mean(gd * out, axis=-1, keepdims=True))).astype(BF16)
        dmerged = _dot_nt(dout, wout_ref[...])
        dwout = _dot_tn(merged, dout)
        dya, dyb = (dmerged * sa).astype(BF16), (dmerged * sb).astype(BF16)
        dgm_ref[:, 0:d] = (dmerged * ya * sa * (1.0 - sa)).astype(BF16)
        dgm_ref[:, d:2 * d] = (dmerged * yb * sb * (1.0 - sb)).astype(BF16)
        dam, dbm = _dot_nt(dya, wua_ref[...]), _dot_nt(dyb, wur_ref[...])
        dwua, dwur = _dot_tn(am, dya), _dot_tn(bm, dyb)
        doa_ref[...] = dam * _silu(za)
        dza_ref[...] = (dam * oa * _dsilu(za)).astype(BF16)
        dor_ref[...] = dbm * _silu(zr)
        dzr_ref[...] = (dbm * orw * _dsilu(zr)).astype(BF16)

        @pl.when(i == 0)
        def _():
            dwua_ref[...], dwur_ref[...], dwout_ref[...], dg2_ref[...], lacc[...] = dwua, dwur, dwout, dg2, lpart

        @pl.when(i != 0)
        def _():
            dwua_ref[...] += dwua
            dwur_ref[...] += dwur
            dwout_ref[...] += dwout
            dg2_ref[...] += dg2
            lacc[...] += lpart

        @pl.when(i == nt - 1)
        def _():
            loss_ref[...] = jnp.sum(lacc[...], axis=1, keepdims=True) * (0.5 / d)

    t512 = pl.BlockSpec((tm, WIDTH), lambda i: (i, 0))
    t1k = pl.BlockSpec((tm, d), lambda i: (i, 0))
    t2k = pl.BlockSpec((tm, 2 * d), lambda i: (i, 0))
    full = lambda r, c: pl.BlockSpec((r, c), lambda i: (0, 0))
    return pl.pallas_call(
        body, name="head_fwd_bwd", grid=(nt,),
        in_specs=[t512, t512, t512, t512, t2k, t1k, t1k, full(WIDTH, d), full(WIDTH, d), full(d, d), full(1, d)],
        out_specs=[t1k, t512, t512, t512, t512, t2k, full(WIDTH, d), full(WIDTH, d), full(d, d), full(1, d), full(1, 1)],
        out_shape=[SDS((n, d), F32), SDS((n, WIDTH), F32), SDS((n, WIDTH), F32), SDS((n, WIDTH), BF16), SDS((n, WIDTH), BF16),
                   SDS((n, 2 * d), BF16), SDS((WIDTH, d), F32), SDS((WIDTH, d), F32), SDS((d, d), F32), SDS((1, d), F32), SDS((1, 1), F32)],
        scratch_shapes=[pltpu.VMEM((1, d), F32)],
        compiler_params=_params(("arbitrary",)))(o_attn, o_rwkv, z_attn, z_rwkv, gm, x2, tgt, wua, wur, wout, g2)


def _mesh_pos():
    x, y, c = lax.axis_index("x"), lax.axis_index("y"), lax.axis_index("c")
    return 4 * x + 2 * y + c


def _coords(idx):
    return (idx // 4, (idx // 2) % 2, idx % 2)


def _exchange(srcs, to_all, name):
    n = len(srcs)

    def body(*refs):
        src_refs, dst_refs = refs[:n], refs[n:2 * n]
        send_sems, recv_sems, local_sems = refs[2 * n:]
        me = _mesh_pos()

        def piece(i, j):
            return src_refs[i] if to_all[i] else src_refs[i].at[j]

        def remote(i, off, peer, block, slot):
            return pltpu.make_async_remote_copy(src_ref=piece(i, block), dst_ref=dst_refs[i].at[slot],
                                                send_sem=send_sems.at[i, off - 1], recv_sem=recv_sems.at[i, off - 1],
                                                device_id=_coords(peer), device_id_type=MESH)

        local = [pltpu.make_async_copy(piece(i, me), dst_refs[i].at[me], local_sems.at[i]) for i in range(n)]
        for cp in local:
            cp.start()
        sends = []
        for off in range(1, N_DEV):
            to = (me + off) % N_DEV
            for i in range(n):
                sends.append(remote(i, off, to, to, me))
                sends[-1].start()
        for off in range(1, N_DEV):
            frm = (me + N_DEV - off) % N_DEV
            for i in range(n):
                remote(i, off, frm, me, frm).wait_recv()
        for cp in sends:
            cp.wait_send()
        for cp in local:
            cp.wait()

    outs = pl.pallas_call(
        body, name=name, in_specs=[pl.BlockSpec(memory_space=pltpu.HBM)] * n, out_specs=[pl.BlockSpec(memory_space=pltpu.HBM)] * n,
        out_shape=[SDS((N_DEV,) + s.shape[-2:], s.dtype) for s in srcs],
        scratch_shapes=[pltpu.SemaphoreType.DMA((n, N_DEV - 1)), pltpu.SemaphoreType.DMA((n, N_DEV - 1)), pltpu.SemaphoreType.DMA((n,))],
        compiler_params=pltpu.CompilerParams())(*srcs)
    return outs


_HBM = pl.BlockSpec(memory_space=pltpu.HBM)
_SEM = pl.BlockSpec(memory_space=pltpu.SEMAPHORE)
_EFFECT = pltpu.SideEffectType.DATAFLOW_SIDE_EFFECTING


def _send_copy(src_ref, land_ref, to_all, send_sems, recv_sems, i, off, block, slot, peer):
    k = i * (N_DEV - 1) + off - 1
    return pltpu.make_async_remote_copy(src_ref=src_ref if to_all else src_ref.at[block], dst_ref=land_ref.at[slot],
                                        send_sem=send_sems.at[k], recv_sem=recv_sems.at[k],
                                        device_id=_coords(peer), device_id_type=MESH)


def _send_start(srcs, to_all, name):
    n = len(srcs)

    def body(*refs):
        src_refs, land_refs = refs[:n], refs[n:2 * n]
        send_sems, recv_sems = refs[2 * n:2 * n + 2]
        me = _mesh_pos()
        for off in range(1, N_DEV):
            to = (me + off) % N_DEV
            for i in range(n):
                _send_copy(src_refs[i], land_refs[i], to_all, send_sems, recv_sems, i, off, to, me, to).start()
        refs[-1][...] = jnp.zeros_like(refs[-1])

    lands = [jnp.zeros((N_DEV,) + s.shape[-2:], s.dtype) for s in srcs]
    hbm = [pltpu.HBM(a.shape, a.dtype) for a in list(srcs) + lands]
    sems = pltpu.SemaphoreType.DMA((n * (N_DEV - 1),))
    outs = pl.pallas_call(
        body, name=name, out_shape=(sems, sems, *hbm, SDS((8, LANE), BF16)),
        in_specs=(_HBM,) * (2 * n), out_specs=(_SEM, _SEM) + (_HBM,) * (2 * n) + (pl.BlockSpec(memory_space=pltpu.VMEM),),
        input_output_aliases={i: 2 + i for i in range(2 * n)}, compiler_params=pltpu.CompilerParams(has_side_effects=_EFFECT),
    )(*[pltpu.with_memory_space_constraint(a, pltpu.HBM) for a in list(srcs) + lands])
    return outs[0], outs[1], outs[2:2 + n], outs[2 + n:2 + 2 * n], outs[-1]


def _send_wait(send_sems, recv_sems, srcs_thru, lands_thru, to_all, after, name):
    n = len(srcs_thru)

    def body(*refs):
        src_refs, land_refs = refs[:n], refs[n:2 * n]
        send_sems, recv_sems = refs[2 * n:2 * n + 2]
        me = _mesh_pos()
        for off in range(1, N_DEV):
            to, frm = (me + off) % N_DEV, (me + N_DEV - off) % N_DEV
            for i in range(n):
                _send_copy(src_refs[i], land_refs[i], to_all, send_sems, recv_sems, i, off, to, me, to).wait_send()
                _send_copy(src_refs[i], land_refs[i], to_all, send_sems, recv_sems, i, off, me, frm, frm).wait_recv()

    hbm = tuple(pltpu.HBM(a.shape, a.dtype) for a in list(srcs_thru) + list(lands_thru))
    outs = pl.pallas_call(
        body, name=name, out_shape=hbm, in_specs=(_HBM,) * (2 * n) + (_SEM, _SEM, pl.BlockSpec(memory_space=pl.ANY)),
        out_specs=(_HBM,) * (2 * n), input_output_aliases={i: i for i in range(2 * n)},
        compiler_params=pltpu.CompilerParams(has_side_effects=_EFFECT),
    )(*srcs_thru, *lands_thru, send_sems, recv_sems, after)
    return outs[n:]


def _gather(srcs, after, name):
    n = len(srcs)

    def body(*refs):
        src_refs, dst_refs = refs[:n], refs[n + 1:2 * n + 1]
        send_sems, recv_sems, local_sems = refs[2 * n + 1:]
        x, y, c = lax.axis_index("x"), lax.axis_index("y"), lax.axis_index("c")
        me, sibling = (x, y, c), (x, y, 1 - c)
        chips = [(1 - x, y), (x, 1 - y), (1 - x, 1 - y)]

        def slot(i, dev):
            return dst_refs[i].at[4 * dev[0] + 2 * dev[1] + dev[2]]

        def copy(i, k, block, to, own=False):
            return pltpu.make_async_remote_copy(src_ref=src_refs[i] if own else slot(i, block), dst_ref=slot(i, block),
                                                send_sem=send_sems.at[i, k], recv_sem=recv_sems.at[i, k],
                                                device_id=to, device_id_type=MESH)

        local = [pltpu.make_async_copy(src_refs[i], slot(i, me), local_sems.at[i]) for i in range(n)]
        for cp in local:
            cp.start()
        sends = []
        for i in range(n):
            sends.append(copy(i, 0, me, sibling, own=True))
            sends += [copy(i, 1 + j, me, (*chip, c), own=True) for j, chip in enumerate(chips)]
        for cp in sends:
            cp.start()
        for j, chip in enumerate(chips):
            for i in range(n):
                copy(i, 1 + j, (*chip, c), me).wait_recv()
                sends.append(copy(i, 4 + j, (*chip, c), sibling))
                sends[-1].start()
        for i in range(n):
            copy(i, 0, sibling, me).wait_recv()
            for j, chip in enumerate(chips):
                copy(i, 4 + j, (*chip, 1 - c), me).wait_recv()
        for cp in sends:
            cp.wait_send()
        for cp in local:
            cp.wait()

    return pl.pallas_call(
        body, name=name, in_specs=[pl.BlockSpec(memory_space=pltpu.HBM)] * n + [pl.BlockSpec(memory_space=pl.ANY)],
        out_specs=[pl.BlockSpec(memory_space=pltpu.HBM)] * n, out_shape=[SDS((N_DEV,) + s.shape, s.dtype) for s in srcs],
        scratch_shapes=[pltpu.SemaphoreType.DMA((n, N_DEV - 1)), pltpu.SemaphoreType.DMA((n, N_DEV - 1)), pltpu.SemaphoreType.DMA((n,))],
        compiler_params=pltpu.CompilerParams())(*srcs, after)


def _adamw(parts, w, m, v, tr, name, own=None):
    rows, cols = w.shape
    c1, c2 = 1.0 - ADAM_B1 ** ADAM_STEP, 1.0 - ADAM_B2 ** ADAM_STEP

    def body(p_ref, *refs):
        w_ref, m_ref, v_ref, g_ref, d_ref, nm_ref, nv_ref = refs[-7:]
        me = _mesh_pos()

        def part(j):
            return p_ref[j] if own is None else jnp.where(me == j, refs[0][...], p_ref[j])

        g = part(0).astype(F32)
        for j in range(1, N_DEV):
            g = g + part(j).astype(F32)
        nm = ADAM_B1 * m_ref[...] + (1.0 - ADAM_B1) * g
        nv = ADAM_B2 * v_ref[...] + (1.0 - ADAM_B2) * jnp.square(g)
        g_ref[...] = g
        nm_ref[...] = nm
        nv_ref[...] = nv
        d_ref[...] = -ADAM_LR * ((nm / c1) / (jnp.sqrt(nv / c2) + ADAM_EPS) + ADAM_WD * w_ref[...])

    t = pl.BlockSpec((tr, cols), lambda i: (i, 0))
    extra = [] if own is None else [own]
    return pl.pallas_call(
        body, name=name, grid=(rows // tr,), in_specs=[pl.BlockSpec((N_DEV, tr, cols), lambda i: (0, i, 0))] + [t] * (3 + len(extra)),
        out_specs=[t] * 4, out_shape=[SDS((rows, cols), F32)] * 4, compiler_params=_params(("parallel",)))(parts, *extra, w, m, v)


SHARDED = (("w_in", D_MODEL, IN_COLS // N_DEV, True, 128), ("w_up_attn", WIDTH, D_MODEL // N_DEV, True, WIDTH),
           ("w_up_rwkv", WIDTH, D_MODEL // N_DEV, True, WIDTH), ("w_out", D_MODEL // N_DEV, D_MODEL, False, D_MODEL // N_DEV),
           ("rwkv_w_up", LORA, WIDTH // N_DEV, True, LORA), ("rwkv_a_up", LORA, WIDTH // N_DEV, True, LORA))
LOSS_SLOT = sum(n for _, n in SMALL)


def _pack_small(small, extra=None):
    flat = [small[n].reshape(-1).astype(F32) for n, _ in SMALL]
    flat.append(jnp.zeros((1,), F32) if extra is None else extra.reshape(1))
    flat.append(jnp.zeros((SMALL_ROWS * LANE - LOSS_SLOT - 1,), F32))
    return jnp.concatenate(flat).reshape(SMALL_ROWS, LANE)


def _unpack_small(packed, shapes):
    flat = packed.reshape(-1)
    out, off = {}, 0
    for n, cnt in SMALL:
        out[n] = flat[off:off + cnt].reshape(shapes[n])
        off += cnt
    return out, flat[LOSS_SLOT]


def _whole(gathered, by_cols):
    if not by_cols:
        return gathered.reshape(-1, gathered.shape[-1])
    return gathered.transpose(1, 0, 2).reshape(gathered.shape[1], -1)


def _per_owner(full, by_cols):
    if not by_cols:
        return full.reshape(N_DEV, -1, full.shape[-1])
    return full.reshape(full.shape[0], N_DEV, -1).transpose(1, 0, 2)


def _local_step(x, loss_target, sm, wts):
    bsz, s, d = x.shape
    n = bsz * s
    x2, tgt = x.reshape(n, d), loss_target.reshape(n, d)
    bidx = jnp.asarray(_bucket_tables())
    w_in = wts["w_in"]
    segs = (("qkv", 0, QKV_COLS, 1536), ("za", OFF_ZA, WIDTH, 512), ("pr", OFF_PR, PR_COLS, PR_COLS), ("zr", OFF_ZR, WIDTH, 512),
            ("gm", OFF_GM, 2 * D_MODEL, 1024))

    h, rs = _prenorm(x2, sm["pre_norm_gain"])
    w_seg = {nm: w_in[:, off:off + cnt] for nm, off, cnt, _ in segs}
    proj = {nm: _mm(h, w_seg[nm], tn, "proj_" + nm) for nm, _, _, tn in segs}
    qkv3 = proj["qkv"].reshape(bsz, s, QKV_COLS)
    pr3 = proj["pr"].reshape(bsz, s, PR_COLS)

    o_attn, lse = _attn_fwd(qkv3, sm["rel_bias"], bidx)
    rk = sm["rwkv_r_k"].reshape(1, WIDTH)
    pre_args = (sm["rwkv_shift_mix"], sm["rwkv_w0"], wts["rwkv_w_up"], sm["rwkv_a0"], wts["rwkv_a_up"], sm["rwkv_k_k"], sm["rwkv_k_a"])
    scan_in = _rwkv_pre(pr3, *pre_args)
    o_rwkv, states, consts = _rwkv_scan(scan_in, rk, sm["rwkv_ln_w"], sm["rwkv_ln_b"])

    (dxo, do_attn, do_rwkv, dza, dzr, dgm, g_wua, g_wur, g_wout, g_post, loss) = _head(
        o_attn.reshape(n, WIDTH), o_rwkv.reshape(n, WIDTH), proj["za"], proj["zr"], proj["gm"], x2, tgt,
        wts["w_up_attn"], wts["w_up_rwkv"], wts["w_out"], sm["post_norm_gain"])

    dqkv, dbias = _attn_bwd(qkv3, o_attn, lse, do_attn.reshape(bsz, s, WIDTH), sm["rel_bias"], bidx)
    g_bias = _bias_grad(dbias, bidx)[:, :N_BUCKET].T

    scan_cots, (g_rk, g_lnw, g_lnb) = _rwkv_scan_bwd(scan_in, states, consts, do_rwkv.reshape(bsz, s, WIDTH), rk, sm["rwkv_ln_w"],
                                                     sm["rwkv_ln_b"])
    dprs, g_mix, g_w0, g_wup, g_a0, g_aup, g_kk, g_ka = _rwkv_pre_bwd(pr3, scan_cots, *pre_args)
    dpr = _shift_bwd(dprs, sm["rwkv_shift_mix"]).reshape(n, PR_COLS)

    dsegs = [(dqkv.reshape(9, n, WIDTH), 0, QKV_COLS, WIDTH), (dza, OFF_ZA, WIDTH, WIDTH), (dpr, OFF_PR, PR_COLS, PR_COLS),
             (dzr, OFF_ZR, WIDTH, WIDTH), (dgm, OFF_GM, 2 * D_MODEL, D_MODEL)]
    full = {"w_in": jnp.concatenate([_mm_tn(h, t, tn, "gw_in_%d" % j) for j, (t, _, _, tn) in enumerate(dsegs)], axis=1),
            "w_up_attn": g_wua, "w_up_rwkv": g_wur, "w_out": g_wout, "rwkv_w_up": g_wup, "rwkv_a_up": g_aup}
    blocks = [_per_owner(full[nm], by_cols).astype(BF16) for nm, _, _, by_cols, _ in SHARDED]
    me = 4 * lax.axis_index("x") + 2 * lax.axis_index("y") + lax.axis_index("c")
    own = [lax.dynamic_index_in_dim(b, me, 0, keepdims=False) for b in blocks]
    send_sems, recv_sems, blocks_thru, lands_thru, token = _send_start(blocks, False, "grads_start")
    dh = _mm_nt(dsegs[0][0], w_seg["qkv"], token, "dh_qkv")
    grad_x, g_pre = _dh_rest_prenorm_bwd([t for t, *_ in dsegs[1:]], [w_seg[nm] for nm in ("za", "pr", "zr", "gm")], dh, x2, rs,
                                         sm["pre_norm_gain"], dxo)
    landed = _send_wait(send_sems, recv_sems, blocks_thru, lands_thru, False, g_pre, "grads_wait")

    small = {"pre_norm_gain": g_pre, "rel_bias": g_bias, "rwkv_shift_mix": g_mix, "rwkv_w0": g_w0, "rwkv_a0": g_a0, "rwkv_k_k": g_kk,
             "rwkv_k_a": g_ka, "rwkv_r_k": g_rk, "rwkv_ln_w": g_lnw, "rwkv_ln_b": g_lnb, "post_norm_gain": g_post}
    return loss[0, 0], grad_x.reshape(bsz, s, d), (landed, own), small


def kernel(x, pre_norm_gain, w_in, rel_bias, rwkv_shift_mix, rwkv_w0, rwkv_w_up, rwkv_a0, rwkv_a_up, rwkv_k_k, rwkv_k_a, rwkv_r_k, rwkv_ln_w, rwkv_ln_b, w_up_attn, w_up_rwkv, w_out, post_norm_gain, loss_target, m_pre_norm_gain, m_w_in, m_rel_bias, m_rwkv_shift_mix, m_rwkv_w0, m_rwkv_w_up, m_rwkv_a0, m_rwkv_a_up, m_rwkv_k_k, m_rwkv_k_a, m_rwkv_r_k, m_rwkv_ln_w, m_rwkv_ln_b, m_w_up_attn, m_w_up_rwkv, m_w_out, m_post_norm_gain, v_pre_norm_gain, v_w_in, v_rel_bias, v_rwkv_shift_mix, v_rwkv_w0, v_rwkv_w_up, v_rwkv_a0, v_rwkv_a_up, v_rwkv_k_k, v_rwkv_k_a, v_rwkv_r_k, v_rwkv_ln_w, v_rwkv_ln_b, v_w_up_attn, v_w_up_rwkv, v_w_out, v_post_norm_gain):
    names = [n for n, *_ in SHARDED] + [n for n, _ in SMALL]
    loc = dict(locals())
    w = {n: loc[n] for n in names}
    m = {n: loc["m_" + n] for n in names}
    v = {n: loc["v_" + n] for n in names}
    shapes = {n: w[n].shape for n in names}
    order = ["pre_norm_gain", "w_in", "rel_bias", "rwkv_shift_mix", "rwkv_w0", "rwkv_w_up", "rwkv_a0", "rwkv_a_up", "rwkv_k_k", "rwkv_k_a",
             "rwkv_r_k", "rwkv_ln_w", "rwkv_ln_b", "w_up_attn", "w_up_rwkv", "w_out", "post_norm_gain"]
    shard2d = lambda t, n, r, c: t[n].reshape(r, c)

    shards = [shard2d(w, n, r, c).astype(BF16) for n, r, c, _, _ in SHARDED]
    send_sems, recv_sems, srcs_thru, lands_thru, token = _send_start(shards[1:], True, "weights_start")
    gathered = list(_gather(shards[:1], token, "gather_weights"))
    landed = _send_wait(send_sems, recv_sems, srcs_thru, lands_thru, True, gathered[0], "weights_wait")
    me = 4 * lax.axis_index("x") + 2 * lax.axis_index("y") + lax.axis_index("c")
    gathered += [lax.dynamic_update_index_in_dim(g, sh, me, 0) for g, sh in zip(landed, shards[1:])]
    wts = {n: _whole(g, by_cols) for (n, _, _, by_cols, _), g in zip(SHARDED, gathered)}

    loss, grad_x, (landed, own), small = _local_step(x, loss_target, w, wts)
    (small_parts,) = _exchange([_pack_small(small, loss)], [True], "exchange_small")

    outs = [{}, {}, {}, {}]
    for (n, r, c, _, tr), p, o_ in zip(SHARDED, landed, own):
        res = _adamw(p, shard2d(w, n, r, c), shard2d(m, n, r, c), shard2d(v, n, r, c), tr, "adamw_" + n, own=o_)
        for o, t in zip(outs, res):
            o[n] = t.reshape(shapes[n])
    res = _adamw(small_parts, _pack_small(w), _pack_small(m), _pack_small(v), SMALL_ROWS, "adamw_small")
    for o, t in zip(outs, res):
        o.update(_unpack_small(t, shapes)[0])
    loss = _unpack_small(res[0], shapes)[1]
    return (loss, grad_x, *[o[n] for o in outs for n in order])
```

```python
import functools
import math

import numpy as np
import jax
import jax.numpy as jnp
from jax import lax
from jax.experimental import pallas as pl
from jax.experimental.pallas import tpu as pltpu

F32, BF16 = jnp.float32, jnp.bfloat16
SDS = jax.ShapeDtypeStruct
MESH = pl.DeviceIdType.MESH

N_DEV = 8
D_MODEL = 1024
HEAD = 64
N_HEAD = 8
WIDTH = N_HEAD * HEAD
DILATIONS = (1, 4, 16)
QB = 128
N_BUCKET = 32
MAX_DIST = 2048
LORA = 64
QKV_COLS = 9 * WIDTH
PR_COLS = 3 * WIDTH + 2 * LORA
IN_COLS = QKV_COLS + WIDTH + PR_COLS + WIDTH + 2 * D_MODEL
OFF_ZA, OFF_PR, OFF_ZR, OFF_GM = QKV_COLS, QKV_COLS + WIDTH, QKV_COLS + WIDTH + PR_COLS, QKV_COLS + 2 * WIDTH + PR_COLS
RMS_EPS = 1e-6
GN_EPS = 64e-5
SCALE = 1.0 / math.sqrt(HEAD)
CHUNK = 64
CHUNK_GROUP = 32
BWD_GROUP = 16
EARLY = 8
NEG = -1e30
LANE = 128

ADAM_LR, ADAM_B1, ADAM_B2, ADAM_EPS, ADAM_WD, ADAM_STEP = 0.001, 0.9, 0.999, 1e-08, 0.01, 10

VMEM_LIMIT = 56 * 1024 * 1024
MM_ROWS = 2048

SMALL = (("pre_norm_gain", 1024), ("rel_bias", 768), ("rwkv_shift_mix", 1664), ("rwkv_w0", 512), ("rwkv_a0", 512),
         ("rwkv_k_k", 512), ("rwkv_k_a", 512), ("rwkv_r_k", 512), ("rwkv_ln_w", 512), ("rwkv_ln_b", 512),
         ("post_norm_gain", 1024))
SMALL_ROWS = 64


def _params(sem=None):
    return pltpu.CompilerParams(dimension_semantics=sem, vmem_limit_bytes=VMEM_LIMIT)


def _dot(a, b):
    return jnp.dot(a, b, preferred_element_type=F32)


def _dot_nt(a, b):
    return lax.dot_general(a, b, (((1,), (1,)), ((), ())), preferred_element_type=F32)


def _dot_tn(a, b):
    return lax.dot_general(a, b, (((0,), (0,)), ((), ())), preferred_element_type=F32)


@jax.custom_vjp
def _bdot(a, b):
    return _dot(a.astype(BF16), b.astype(BF16))


def _bdot_fwd(a, b):
    return _bdot(a, b), (a, b)


def _bdot_bwd(res, g):
    a, b = res
    gb = g.astype(BF16)
    return _dot_nt(gb, b.astype(BF16)), _dot_tn(a.astype(BF16), gb)


_bdot.defvjp(_bdot_fwd, _bdot_bwd)


def _silu(z):
    return z * jax.nn.sigmoid(z)


def _dsilu(z):
    s = jax.nn.sigmoid(z)
    return s * (1.0 + z * (1.0 - s))


def _softplus(x):
    return jnp.maximum(x, 0.0) + jnp.log(1.0 + jnp.exp(-jnp.abs(x)))


def _bucket_tables():
    qi = np.arange(QB)[:, None] + QB
    ki = np.arange(2 * QB)[None, :]
    rel = np.maximum(qi - ki, 0)
    out = []
    for d in DILATIONS:
        dist = rel * d
        max_exact = N_BUCKET // 2
        ratio = np.log(np.maximum(dist, 1).astype(np.float32) / max_exact) / np.float32(math.log(MAX_DIST / max_exact))
        large = max_exact + (ratio * (N_BUCKET - max_exact)).astype(np.int32)
        large = np.minimum(large, N_BUCKET - 1)
        out.append(np.where(dist < max_exact, dist, large).astype(np.int32))
    return np.stack(out)


def _prenorm(x2, g):
    n, d = x2.shape
    tm = 1024

    def body(x_ref, g_ref, h_ref, rs_ref):
        x = x_ref[...]
        rs = lax.rsqrt(jnp.mean(x * x, axis=-1, keepdims=True) + RMS_EPS)
        h_ref[...] = (x * rs * g_ref[...]).astype(BF16)
        rs_ref[...] = rs

    return pl.pallas_call(
        body, name="prenorm", grid=(n // tm,),
        in_specs=[pl.BlockSpec((tm, d), lambda i: (i, 0)), pl.BlockSpec((1, d), lambda i: (0, 0))],
        out_specs=[pl.BlockSpec((tm, d), lambda i: (i, 0)), pl.BlockSpec((tm, 1), lambda i: (i, 0))],
        out_shape=[SDS((n, d), BF16), SDS((n, 1), F32)], compiler_params=_params(("parallel",)))(x2, g)


def _mm(a, b, tn, name):
    m, k = a.shape
    n = b.shape[1]
    tm = MM_ROWS

    def body(a_ref, b_ref, o_ref):
        o_ref[...] = _dot(a_ref[...], b_ref[...])

    return pl.pallas_call(
        body, name=name, grid=(n // tn, m // tm),
        in_specs=[pl.BlockSpec((tm, k), lambda j, i: (i, 0)), pl.BlockSpec((k, tn), lambda j, i: (0, j))],
        out_specs=pl.BlockSpec((tm, tn), lambda j, i: (i, j)),
        out_shape=SDS((m, n), F32), compiler_params=_params(("parallel", "parallel")))(a, b)


def _mm_nt(a, b, after, name):
    m, seg = a.shape[1], a.shape[2]
    d, k = b.shape
    tm = MM_ROWS
    per = 3
    tk = per * seg

    def body(a_ref, b_ref, after_ref, o_ref):
        r = sum(_dot_nt(a_ref[j].astype(BF16), b_ref[:, seg * j:seg * (j + 1)]) for j in range(per))

        @pl.when(pl.program_id(1) == 0)
        def _():
            o_ref[...] = r

        @pl.when(pl.program_id(1) != 0)
        def _():
            o_ref[...] += r

    in_specs = [pl.BlockSpec((per, tm, seg), lambda i, j: (j, i, 0)), pl.BlockSpec((d, tk), lambda i, j: (0, j)),
                pl.BlockSpec(after.shape, lambda i, j: (0, 0))]
    return pl.pallas_call(
        body, name=name, grid=(m // tm, k // tk), in_specs=in_specs, out_specs=pl.BlockSpec((tm, d), lambda i, j: (i, 0)),
        out_shape=SDS((m, d), F32), compiler_params=_params(("parallel", "arbitrary")))(a, b, after)


def _dh_rest_prenorm_bwd(a_list, b_list, acc, x2, rs, g1, dxo):
    m, d = acc.shape
    tm = 512
    n = len(a_list)

    def body(*refs):
        x_ref, rs_ref, g_ref, dxo_ref, gx_ref, dg_ref = refs[2 * n + 1:]
        dh = refs[2 * n][...]
        for a_ref, b_ref in zip(refs[:n], refs[n:2 * n]):
            dh = dh + _dot_nt(a_ref[...].astype(BF16), b_ref[...])
        x, r = x_ref[...], rs_ref[...]
        gd = dh * g_ref[...]
        gx_ref[...] = dxo_ref[...] + r * (gd - x * (r * r) * jnp.mean(gd * x, axis=-1, keepdims=True))
        dg = jnp.sum(dh * x * r, axis=0, keepdims=True)

        @pl.when(pl.program_id(0) == 0)
        def _():
            dg_ref[...] = dg

        @pl.when(pl.program_id(0) != 0)
        def _():
            dg_ref[...] += dg

    t = pl.BlockSpec((tm, d), lambda i: (i, 0))
    in_specs = [pl.BlockSpec((tm, a.shape[1]), lambda i: (i, 0)) for a in a_list]
    in_specs += [pl.BlockSpec(b.shape, lambda i: (0, 0)) for b in b_list]
    in_specs += [t, t, pl.BlockSpec((tm, 1), lambda i: (i, 0)), pl.BlockSpec((1, d), lambda i: (0, 0)), t]
    return pl.pallas_call(
        body, name="dh_rest_prenorm_bwd", grid=(m // tm,), in_specs=in_specs, out_specs=[t, pl.BlockSpec((1, d), lambda i: (0, 0))],
        out_shape=[SDS((m, d), F32), SDS((1, d), F32)], compiler_params=_params(("arbitrary",)))(*a_list, *b_list, acc, x2, rs, g1, dxo)


def _mm_tn(a, b, tn, name):
    split = b.ndim == 3
    m, k1 = a.shape
    per = 3 if split else 1
    seg = b.shape[2] if split else tn
    tn = per * seg
    n2 = b.shape[0] * seg if split else b.shape[1]
    tm = MM_ROWS

    def body(a_ref, b_ref, o_ref):
        first = pl.program_id(1) == 0
        for j in range(per):
            r = _dot_tn(a_ref[...], (b_ref[j] if split else b_ref[...]).astype(BF16))
            cols = slice(seg * j, seg * (j + 1))

            @pl.when(first)
            def _(r=r, cols=cols):
                o_ref[:, cols] = r

            @pl.when(jnp.logical_not(first))
            def _(r=r, cols=cols):
                o_ref[:, cols] += r

    b_spec = pl.BlockSpec((per, tm, seg), lambda j, i: (j, i, 0)) if split else pl.BlockSpec((tm, tn), lambda j, i: (i, j))
    return pl.pallas_call(
        body, name=name, grid=(n2 // tn, m // tm),
        in_specs=[pl.BlockSpec((tm, k1), lambda j, i: (i, 0)), b_spec],
        out_specs=pl.BlockSpec((k1, tn), lambda j, i: (0, j)),
        out_shape=SDS((k1, n2), F32), compiler_params=_params(("parallel", "arbitrary")))(a, b)


def _ds(start, d):
    return pl.ds(start, QB) if d == 1 else pl.ds(start, QB, stride=d)


def _fill_bias(tab_ref, bidx_ref, bias_sc, hp):
    for g in range(3):
        bi = bidx_ref[g]
        for h in range(2):
            acc = jnp.zeros((QB, 2 * QB), F32)
            for j in range(N_BUCKET):
                acc = jnp.where(bi == j, tab_ref[j, g * N_HEAD + hp * 2 + h], acc)
            bias_sc[g * 2 + h] = acc


def _block_starts(it, d, nb):
    rho = it // nb
    n = it % nb
    st = rho + d * QB * n
    stp = rho + d * QB * jnp.maximum(n - 1, 0)
    if d == 1:
        st, stp = pl.multiple_of(QB * it, QB), pl.multiple_of(QB * jnp.maximum(it - 1, 0), QB)
    return st, stp, n > 0


ATTN_BLOCKS_FWD = 16
ATTN_BLOCKS_BWD = 4


def _bdot3(a, b, dims):
    return lax.dot_general(a, b, (dims, ((0,), (0,))), preferred_element_type=F32)


def _attn_operands(q_ref, k_ref, v_ref, bias_sc, g, d, nb, it0, nblk):
    two = nb > 1
    nk = 2 * QB if two else QB
    ii = lax.broadcasted_iota(jnp.int32, (QB, nk), 0)
    cc = lax.broadcasted_iota(jnp.int32, (QB, nk), 1)
    qs, ks, vs, pens, starts = [], [], [], [], []
    for u in range(nblk):
        st, stp, hasprev = _block_starts(it0 + u, d, nb)
        qf = q_ref[0, _ds(st, d), :]
        if two:
            kf = jnp.concatenate([k_ref[0, _ds(stp, d), :], k_ref[0, _ds(st, d), :]], axis=0).astype(BF16)
            vf = jnp.concatenate([v_ref[0, _ds(stp, d), :], v_ref[0, _ds(st, d), :]], axis=0).astype(BF16)
            own = jnp.logical_and(cc >= QB, ii >= cc - QB)
            prev = jnp.logical_and(jnp.logical_and(cc < QB, cc >= ii), hasprev)
            pen = jnp.where(jnp.logical_or(own, prev), 0.0, NEG)
        else:
            kf, vf = k_ref[0, _ds(st, d), :].astype(BF16), v_ref[0, _ds(st, d), :].astype(BF16)
            pen = jnp.where(ii >= cc, 0.0, NEG)
        for h in range(2):
            qs.append(_one_head(qf, h).astype(BF16))
            ks.append(kf)
            vs.append(vf)
            pens.append(pen + (bias_sc[g * 2 + h] if two else bias_sc[g * 2 + h, :, QB:2 * QB]))
        starts.append((st, stp))
    return _stack(qs), _stack(ks), _stack(vs), _stack(pens), starts


def _one_head(x, h):
    lane = lax.broadcasted_iota(jnp.int32, x.shape, 1)
    return jnp.where(lane >= HEAD if h == 1 else lane < HEAD, x, 0.0)


def _pick_heads(x, u):
    lane = lax.broadcasted_iota(jnp.int32, x.shape[1:], 1)
    return jnp.where(lane < HEAD, x[2 * u], x[2 * u + 1])


def _add_heads(x, u):
    return x[2 * u] + x[2 * u + 1]


def _attn_fwd(qkv3, rel_bias, bidx):
    bsz, s, _ = qkv3.shape
    rt = 256

    def body(tab_ref, bidx_ref, *refs):
        q_refs, k_refs, v_refs = refs[0:3], refs[3:6], refs[6:9]
        o_ref, lse_ref = refs[9:11]
        bias_sc, num_sc, den_sc, m_sc = refs[11:]
        pl.when(pl.program_id(1) == 0)(lambda: _fill_bias(tab_ref, bidx_ref, bias_sc, pl.program_id(0)))
        for g, d in enumerate(DILATIONS):
            nb = s // (QB * d)

            def blk(it, c, g=g, d=d, nb=nb):
                q, k, v, bias, starts = _attn_operands(q_refs[g], k_refs[g], v_refs[g], bias_sc, g, d, nb, it * ATTN_BLOCKS_FWD,
                                                       ATTN_BLOCKS_FWD)
                sc = _bdot3(q, k, ((2,), (2,))) * SCALE + bias
                m = jnp.max(sc, axis=-1, keepdims=True)
                p = jnp.exp(sc - m)
                den = jnp.sum(p, axis=-1, keepdims=True)
                num = _bdot3(p.astype(BF16), v, ((2,), (1,)))
                den, m = jnp.broadcast_to(den, num.shape), jnp.broadcast_to(m, num.shape)
                for u, (st, _) in enumerate(starts):
                    num_sc[g, _ds(st, d), :] = _pick_heads(num, u)
                    den_sc[g, _ds(st, d), :] = _pick_heads(den, u)
                    m_sc[g, _ds(st, d), :] = _pick_heads(m, u)
                return c

            lax.fori_loop(0, s // QB // ATTN_BLOCKS_FWD, blk, 0)

        def merge(i, c):
            rows = pl.ds(pl.multiple_of(i * rt, rt), rt)
            m0, m1, m2 = m_sc[0, rows, :], m_sc[1, rows, :], m_sc[2, rows, :]
            mall = jnp.maximum(jnp.maximum(m0, m1), m2)
            w0, w1, w2 = jnp.exp(m0 - mall), jnp.exp(m1 - mall), jnp.exp(m2 - mall)
            num = w0 * num_sc[0, rows, :] + w1 * num_sc[1, rows, :] + w2 * num_sc[2, rows, :]
            den = w0 * den_sc[0, rows, :] + w1 * den_sc[1, rows, :] + w2 * den_sc[2, rows, :]
            o_ref[0, rows, :] = num / den
            lse_ref[0, rows, :] = mall + jnp.log(den)
            return c

        lax.fori_loop(0, s // rt, merge, 0)

    col = lambda w, g: (lambda hp, b: (b, 0, (w * 3 + g) * 4 + hp))
    in_specs = [pl.BlockSpec(memory_space=pltpu.SMEM), pl.BlockSpec((3, QB, 2 * QB), lambda hp, b: (0, 0, 0))]
    in_specs += [pl.BlockSpec((1, s, LANE), col(w, g)) for w in range(3) for g in range(3)]
    out_spec = pl.BlockSpec((1, s, LANE), lambda hp, b: (b, 0, hp))
    return pl.pallas_call(
        body, name="attn_fwd", grid=(4, bsz), in_specs=in_specs, out_specs=[out_spec, out_spec],
        out_shape=[SDS((bsz, s, WIDTH), F32), SDS((bsz, s, WIDTH), F32)],
        scratch_shapes=[pltpu.VMEM((6, QB, 2 * QB), F32), pltpu.VMEM((3, s, LANE), F32), pltpu.VMEM((3, s, LANE), F32),
                        pltpu.VMEM((3, s, LANE), F32)],
        compiler_params=_params(("arbitrary", "arbitrary")))(rel_bias, bidx, *([qkv3] * 9))


def _attn_bwd(qkv3, o3, lse3, do3, rel_bias, bidx):
    bsz, s, _ = qkv3.shape
    rt = 256

    def body(tab_ref, bidx_ref, *refs):
        q_refs, k_refs, v_refs = refs[0:3], refs[3:6], refs[6:9]
        o_ref, lse_ref, do_ref, dqkv_ref, db_ref, bias_sc, delta_sc, acc_sc = refs[9:]
        dq_refs, dk_refs, dv_refs = ([acc_sc.at[w * 3 + g] for g in range(3)] for w in range(3))

        @pl.when(pl.program_id(1) == 0)
        def _():
            _fill_bias(tab_ref, bidx_ref, bias_sc, pl.program_id(0))
            db_ref[...] = jnp.zeros_like(db_ref)

        def prep(i, c):
            rows = pl.ds(pl.multiple_of(i * rt, rt), rt)
            prod = do_ref[0, rows, :] * o_ref[0, rows, :]
            d0 = jnp.sum(prod[:, :HEAD], axis=-1, keepdims=True)
            d1 = jnp.sum(prod[:, HEAD:], axis=-1, keepdims=True)
            delta_sc[rows, :] = jnp.concatenate([jnp.broadcast_to(d0, (rt, HEAD)), jnp.broadcast_to(d1, (rt, HEAD))], axis=1)
            z = jnp.zeros((rt, LANE), F32)
            for g in range(3):
                dk_refs[g][0, rows, :] = z
                dv_refs[g][0, rows, :] = z
            return c

        lax.fori_loop(0, s // rt, prep, 0)
        for g, d in enumerate(DILATIONS):
            nb = s // (QB * d)

            def blk(it, c, g=g, d=d, nb=nb):
                q, k, v, bias, starts = _attn_operands(q_refs[g], k_refs[g], v_refs[g], bias_sc, g, d, nb, it * ATTN_BLOCKS_BWD,
                                                       ATTN_BLOCKS_BWD)
                dos, lses, deltas = [], [], []
                for st, _ in starts:
                    dof, lsef, delf = do_ref[0, _ds(st, d), :], lse_ref[0, _ds(st, d), :], delta_sc[_ds(st, d), :]
                    for h in range(2):
                        dos.append(_one_head(dof, h).astype(BF16))
                        lses.append(lsef[:, HEAD * h:HEAD * h + 1])
                        deltas.append(delf[:, HEAD * h:HEAD * h + 1])
                do, lse, delta = _stack(dos), _stack(lses), _stack(deltas)
                p = jnp.exp(_bdot3(q, k, ((2,), (2,))) * SCALE + bias - lse)
                dv = _bdot3(p.astype(BF16), do, ((1,), (1,)))
                ds = p * (_bdot3(do, v, ((2,), (2,))) - delta)
                dsb = ds.astype(BF16)
                dq = _bdot3(dsb, k, ((2,), (1,))) * SCALE
                dk = _bdot3(dsb, q, ((1,), (1,))) * SCALE
                two = nb > 1
                for h in range(2):
                    dsum = sum(ds[2 * u + h] for u in range(ATTN_BLOCKS_BWD))
                    if two:
                        db_ref[0, g * 2 + h] += dsum
                    else:
                        db_ref[0, g * 2 + h, :, QB:2 * QB] += dsum
                if two:
                    stp0 = starts[0][1]
                    dk_refs[g][0, _ds(stp0, d), :] += _add_heads(dk[:, :QB], 0)
                    dv_refs[g][0, _ds(stp0, d), :] += _add_heads(dv[:, :QB], 0)
                for u, (st, _) in enumerate(starts):
                    dq_refs[g][0, _ds(st, d), :] = _pick_heads(dq, u)
                    dku, dvu = _add_heads(dk[:, QB:] if two else dk, u), _add_heads(dv[:, QB:] if two else dv, u)
                    if two and u + 1 < len(starts):
                        dku, dvu = dku + _add_heads(dk[:, :QB], u + 1), dvu + _add_heads(dv[:, :QB], u + 1)
                    dk_refs[g][0, _ds(st, d), :] += dku
                    dv_refs[g][0, _ds(st, d), :] += dvu
                return c

            lax.fori_loop(0, s // QB // ATTN_BLOCKS_BWD, blk, 0)

        def flush(i, c):
            rows = pl.ds(pl.multiple_of(i * rt, rt), rt)
            for j in range(9):
                dqkv_ref[j, 0, rows, :] = acc_sc[j, 0, rows, :].astype(BF16)
            return c

        lax.fori_loop(0, s // rt, flush, 0)

    col = lambda w, g: (lambda hp, b: (b, 0, (w * 3 + g) * 4 + hp))
    blk_spec = pl.BlockSpec((1, s, LANE), lambda hp, b: (b, 0, hp))
    in_specs = [pl.BlockSpec(memory_space=pltpu.SMEM), pl.BlockSpec((3, QB, 2 * QB), lambda hp, b: (0, 0, 0))]
    in_specs += [pl.BlockSpec((1, s, LANE), col(w, g)) for w in range(3) for g in range(3)]
    in_specs += [blk_spec] * 3
    out_specs = [pl.BlockSpec((9, 1, s, LANE), lambda hp, b: (0, b, 0, hp)), pl.BlockSpec((1, 6, QB, 2 * QB), lambda hp, b: (hp, 0, 0, 0))]
    out_shape = [SDS((9, bsz, s, WIDTH), BF16), SDS((4, 6, QB, 2 * QB), F32)]
    return pl.pallas_call(
        body, name="attn_bwd", grid=(4, bsz), in_specs=in_specs, out_specs=out_specs, out_shape=out_shape,
        scratch_shapes=[pltpu.VMEM((6, QB, 2 * QB), F32), pltpu.VMEM((s, LANE), F32), pltpu.VMEM((9, 1, s, LANE), F32)],
        compiler_params=_params(("parallel", "arbitrary")))(rel_bias, bidx, *([qkv3] * 9), o3, lse3, do3)


def _bias_grad(dbias, bidx):
    def body(db_ref, bidx_ref, o_ref):
        lane = lax.broadcasted_iota(jnp.int32, (1, LANE), 1)
        for g in range(3):
            bi = bidx_ref[g]
            for hp in range(4):
                for h in range(2):
                    mat = db_ref[hp, g * 2 + h]
                    row = jnp.zeros((1, LANE), F32)
                    for j in range(N_BUCKET):
                        part = jnp.sum(jnp.where(bi == j, mat, 0.0), axis=0, keepdims=True)
                        row = jnp.where(lane == j, jnp.sum(part, axis=1, keepdims=True), row)
                    hd = g * N_HEAD + hp * 2 + h
                    o_ref[hd:hd + 1, :] = row

    return pl.pallas_call(body, name="bias_grad", out_shape=SDS((3 * N_HEAD, LANE), F32), compiler_params=_params())(dbias, bidx)


def _pre_fn(r, k0, v, wl, al, w0, wup, a0, aup, kk_, ka_):
    u = w0 + _bdot(jnp.tanh(wl), wup)
    lw = -jnp.exp(-_softplus(-u) - 0.5)
    a = jax.nn.sigmoid(a0 + _bdot(al, aup))
    kkraw = k0 * kk_
    k = k0 * (1.0 + (a - 1.0) * ka_)
    return r, lw, k, v, kkraw, a


PRE_SPLIT = (0, WIDTH, 2 * WIDTH, 3 * WIDTH, 3 * WIDTH + LORA, 3 * WIDTH + 2 * LORA)


def _pre_pieces(prs):
    return [prs[:, a:b] for a, b in zip(PRE_SPLIT[:-1], PRE_SPLIT[1:])]


PRE_TT = 512


def _shifted(pr_ref, edge_ref, first, back):
    pr = pr_ref[0]
    tt = pr.shape[0]
    row = lax.broadcasted_iota(jnp.int32, (tt, 1), 0)
    if back:
        edge = jnp.where(first, 0.0, edge_ref[0, 7:8, :])
        return jnp.where(row == 0, edge, pltpu.roll(pr, 1, axis=0))
    edge = jnp.where(first, 0.0, edge_ref[0, 0:1, :])
    return jnp.where(row == tt - 1, edge, pltpu.roll(pr, tt - 1, axis=0))


def _rwkv_pre(pr3, mix, w0, wup, a0, aup, kk_, ka_):
    bsz, s, _ = pr3.shape
    tt = PRE_TT

    def body(pr_ref, edge_ref, mix_ref, w0_ref, wup_ref, a0_ref, aup_ref, kk_ref, ka_ref, *outs):
        pr = pr_ref[0]
        prev = _shifted(pr_ref, edge_ref, pl.program_id(1) == 0, True)
        prs = pr + (prev - pr) * mix_ref[...]
        vals = _pre_fn(*_pre_pieces(prs), w0_ref[...], wup_ref[...].astype(F32), a0_ref[...], aup_ref[...].astype(F32), kk_ref[...],
                       ka_ref[...])
        for o, val in zip(outs, vals):
            o[0] = val

    vec = lambda n: pl.BlockSpec((1, n), lambda b, i: (0, 0))
    mat = pl.BlockSpec((LORA, WIDTH), lambda b, i: (0, 0))
    in_specs = [pl.BlockSpec((1, tt, PR_COLS), lambda b, i: (b, i, 0)),
                pl.BlockSpec((1, 8, PR_COLS), lambda b, i: (b, jnp.maximum(i * (tt // 8) - 1, 0), 0)),
                vec(PR_COLS), vec(WIDTH), mat, vec(WIDTH), mat, vec(WIDTH), vec(WIDTH)]
    out_spec = pl.BlockSpec((1, tt, WIDTH), lambda b, i: (b, i, 0))
    return pl.pallas_call(
        body, name="rwkv_pre", grid=(bsz, s // tt), in_specs=in_specs, out_specs=[out_spec] * 6,
        out_shape=[SDS((bsz, s, WIDTH), F32)] * 6, compiler_params=_params(("parallel", "parallel")))(
            pr3, pr3, mix, w0, wup, a0, aup, kk_, ka_)


def _rwkv_pre_bwd(pr3, cots, mix, w0, wup, a0, aup, kk_, ka_):
    bsz, s, _ = pr3.shape
    tt = PRE_TT

    def body(pr_ref, edge_ref, c0, c1, c2, c3, c4, c5, mix_ref, w0_ref, wup_ref, a0_ref, aup_ref, kk_ref, ka_ref,
             dprs_ref, dmix_ref, dw0_ref, dwup_ref, da0_ref, daup_ref, dkk_ref, dka_ref):
        pr = pr_ref[0]
        prev = _shifted(pr_ref, edge_ref, pl.program_id(1) == 0, True)
        prs = pr + (prev - pr) * mix_ref[...]
        _, vjp = jax.vjp(_pre_fn, *_pre_pieces(prs), w0_ref[...], wup_ref[...].astype(F32), a0_ref[...], aup_ref[...].astype(F32),
                         kk_ref[...], ka_ref[...])
        grads = vjp(tuple(c[0] for c in (c0, c1, c2, c3, c4, c5)))
        for piece, a, b in zip(grads[:5], PRE_SPLIT[:-1], PRE_SPLIT[1:]):
            dprs_ref[0, :, a:b] = piece
        dw0, dwup, da0, daup, dkk, dka = grads[5:]
        dprs = dprs_ref[0]
        grads = (jnp.sum(dprs * (prev - pr), axis=0, keepdims=True), dw0, dwup, da0, daup, dkk, dka)
        refs = (dmix_ref, dw0_ref, dwup_ref, da0_ref, daup_ref, dkk_ref, dka_ref)
        first = jnp.logical_and(pl.program_id(0) == 0, pl.program_id(1) == 0)

        @pl.when(first)
        def _():
            for r_, g_ in zip(refs, grads):
                r_[...] = g_

        @pl.when(jnp.logical_not(first))
        def _():
            for r_, g_ in zip(refs, grads):
                r_[...] += g_

    vec = lambda n: pl.BlockSpec((1, n), lambda b, i: (0, 0))
    mat = pl.BlockSpec((LORA, WIDTH), lambda b, i: (0, 0))
    tile = pl.BlockSpec((1, tt, WIDTH), lambda b, i: (b, i, 0))
    in_specs = [pl.BlockSpec((1, tt, PR_COLS), lambda b, i: (b, i, 0)),
                pl.BlockSpec((1, 8, PR_COLS), lambda b, i: (b, jnp.maximum(i * (tt // 8) - 1, 0), 0))]
    in_specs += [tile] * 6 + [vec(PR_COLS), vec(WIDTH), mat, vec(WIDTH), mat, vec(WIDTH), vec(WIDTH)]
    out_specs = [pl.BlockSpec((1, tt, PR_COLS), lambda b, i: (b, i, 0)), vec(PR_COLS), vec(WIDTH), mat, vec(WIDTH), mat,
                 vec(WIDTH), vec(WIDTH)]
    out_shape = [SDS((bsz, s, PR_COLS), F32), SDS((1, PR_COLS), F32), SDS((1, WIDTH), F32), SDS((LORA, WIDTH), F32),
                 SDS((1, WIDTH), F32), SDS((LORA, WIDTH), F32), SDS((1, WIDTH), F32), SDS((1, WIDTH), F32)]
    return pl.pallas_call(
        body, name="rwkv_pre_bwd", grid=(bsz, s // tt), in_specs=in_specs, out_specs=out_specs, out_shape=out_shape,
        compiler_params=_params(("arbitrary", "arbitrary")))(pr3, pr3, *cots, mix, w0, wup, a0, aup, kk_, ka_)


def _shift_bwd(dprs3, mix):
    bsz, s, _ = dprs3.shape
    tt = PRE_TT
    nt = s // tt

    def body(d_ref, edge_ref, mix_ref, o_ref):
        nxt = _shifted(d_ref, edge_ref, pl.program_id(1) == nt - 1, False)
        m = mix_ref[...]
        o_ref[0] = (d_ref[0] * (1.0 - m) + nxt * m).astype(BF16)

    in_specs = [pl.BlockSpec((1, tt, PR_COLS), lambda b, i: (b, i, 0)),
                pl.BlockSpec((1, 8, PR_COLS), lambda b, i: (b, jnp.minimum((i + 1) * (tt // 8), s // 8 - 1), 0)),
                pl.BlockSpec((1, PR_COLS), lambda b, i: (0, 0))]
    return pl.pallas_call(
        body, name="shift_bwd", grid=(bsz, nt), in_specs=in_specs, out_specs=pl.BlockSpec((1, tt, PR_COLS), lambda b, i: (b, i, 0)),
        out_shape=SDS((bsz, s, PR_COLS), BF16), compiler_params=_params(("parallel", "parallel")))(dprs3, dprs3, mix)


_NN, _NT, _TN = ((2,), (1,)), ((2,), (2,)), ((1,), (1,))


def _dot3_bf16(a, b, dims):
    return lax.dot_general(a.astype(BF16), b.astype(BF16), (dims, ((0,), (0,))), preferred_element_type=F32)


class _Dots:
    def __init__(self, fwd):
        def make(dims, da_rule, db_rule):
            @jax.custom_vjp
            def f(a, b):
                return fwd(a, b, dims)

            f.defvjp(lambda a, b: (f(a, b), (a, b)), lambda res, g: (da_rule(*res, g), db_rule(*res, g)))
            return f

        one = _dot3_bf16
        self.mm = make(_NN, lambda a, b, g: one(g, b, _NT), lambda a, b, g: one(a, g, _TN))
        self.mm_nt = make(_NT, lambda a, b, g: one(g, b, _NN), lambda a, b, g: one(g, a, _TN))
        self.mm_tn = make(_TN, lambda a, b, g: one(b, g, _NT), lambda a, b, g: one(a, g, _NN))

        def powers(aab):
            ps = [aab]
            while 2 ** len(ps) < aab.shape[1]:
                ps.append(fwd(ps[-1], ps[-1], _NN))
            return ps

        def apply(ps, z, dims):
            for p in ps:
                z = z + fwd(p, z, dims)
            return z

        @jax.custom_vjp
        def solve(aab, z):
            return apply(powers(aab), z, _NN)

        def solve_fwd(aab, z):
            ps = powers(aab)
            x = apply(ps, z, _NN)
            return x, (ps, x)

        def solve_bwd(res, g):
            ps, x = res
            dz = apply(ps, g, _TN)
            return fwd(dz, x, _NT), dz

        solve.defvjp(solve_fwd, solve_bwd)
        self.solve = solve


_ONE_PASS = _Dots(_dot3_bf16)
_bmm, _bmm_tn = _ONE_PASS.mm, _ONE_PASS.mm_tn


def _chunk_fn(s0t, r, lw, k, v, kkraw, a, rk, lnw, lnb, first=False, d=_ONE_PASS):
    c = r.shape[1]
    at, rt, btc, ktc, gc, aab, arb, xv, arkv, ain, bin_ = _chunk_core(r, lw, k, v, kkraw, a, d)
    rs = d.mm(jnp.concatenate([at, rt], axis=1), s0t)
    u = d.solve(aab, rs[:, :c] + xv)
    y = rs[:, c:] + d.mm(arb, u) + arkv
    if first:
        y = _with_early_rows(y, r, lw, k, v, ain, bin_)
    gcol = jnp.sum(_diag(gc), axis=2, keepdims=True)
    sct = gcol * s0t + d.mm_tn(jnp.concatenate([btc, ktc], axis=1), jnp.concatenate([u, v], axis=1))
    return _post(y, r, k, v, rk, lnw, lnb), sct


def _diag(gc):
    return jnp.where(_masks(HEAD)[2], gc, 0.0)


def _with_early_rows(y, r, lw, k, v, ain, bin_):
    early = _early_rows(r[:2], lw[:2], k[:2], v[:2], ain[:2], bin_[:2])
    return jnp.concatenate([jnp.concatenate([early, y[:2, EARLY:]], axis=1), y[2:]], axis=0)


def _early_rows(r, lw, k, v, ain, bin_):
    cols = lambda x: _stack([jnp.transpose(x[h]) for h in range(2)])
    wc, bc, kc = cols(jnp.exp(lw)), cols(bin_), cols(k)
    st = jnp.zeros((2, HEAD, HEAD), F32)
    rows = []
    for t in range(EARLY):
        sa = _ONE_PASS.mm(ain[:, t:t + 1], st)
        st = st * wc[:, :, t:t + 1] + bc[:, :, t:t + 1] * sa + kc[:, :, t:t + 1] * v[:, t:t + 1]
        rows.append(_ONE_PASS.mm(r[:, t:t + 1], st))
    return jnp.concatenate(rows, axis=1)


def _chunk_rows(c):
    return pl.ds(c * CHUNK, CHUNK) if isinstance(c, int) else pl.ds(pl.multiple_of(c * CHUNK, CHUNK), CHUNK)


def _stack(xs):
    return jnp.concatenate([x[None] for x in xs], axis=0)


def _pairs(ref, chunks):
    tiles = [ref[0, _chunk_rows(c), :] for c in chunks]
    return _stack([t[:, HEAD * h:HEAD * h + HEAD] for t in tiles for h in range(2)])


def _unpair(vals, j):
    return jnp.concatenate([vals[2 * j], vals[2 * j + 1]], axis=1)


def _masks(c):
    ii = lax.broadcasted_iota(jnp.int32, (c, c), 0)
    jj = lax.broadcasted_iota(jnp.int32, (c, c), 1)
    return ii > jj, ii >= jj, ii == jj


@jax.custom_vjp
def _running_sum(lw):
    return _tri_dot(lw, _NN)


def _tri_dot(x, dims):
    g_, c, _ = x.shape
    tri = jnp.broadcast_to(_masks(c)[1].astype(BF16), (g_, c, c))
    head = x.astype(BF16)
    rest = (x - head.astype(F32)).astype(BF16)
    return lax.dot_general(tri, head, (dims, ((0,), (0,))), preferred_element_type=F32) + \
        lax.dot_general(tri, rest, (dims, ((0,), (0,))), preferred_element_type=F32)


_running_sum.defvjp(lambda lw: (_running_sum(lw), None), lambda _, ct: (_tri_dot(ct, _TN),))


def _chunk_core(r, lw, k, v, kkraw, a, d=_ONE_PASS):
    g_, c = r.shape[0], r.shape[1]
    nrm = jnp.sqrt(jnp.sum(kkraw * kkraw, axis=-1, keepdims=True))
    kkn = kkraw / jnp.maximum(nrm, 1e-12)
    ain, bin_ = -kkn, kkn * a
    strict, incl, _ = _masks(c)
    lg = _running_sum(lw)
    g, gp, gi = jnp.exp(lg), jnp.exp(lg - lw), jnp.exp(-lg)
    at, rt, bt, kt = ain * gp, r * g, bin_ * gi, k * gi
    aa = d.mm_nt(jnp.concatenate([at, rt], axis=1), jnp.concatenate([bt, kt], axis=1))
    aab = jnp.where(strict, aa[:, :c, :c], 0.0)
    aak = jnp.where(strict, aa[:, :c, c:], 0.0)
    arb = jnp.where(incl, aa[:, c:, :c], 0.0)
    ark = jnp.where(incl, aa[:, c:, c:], 0.0)
    akv = d.mm(jnp.concatenate([aak, ark], axis=1), v)
    gc = g[:, c - 1:c, :]
    return at, rt, bt * gc, kt * gc, gc, aab, arb, akv[:, :c], akv[:, c:], ain, bin_


def _lane_sum(x):
    return jnp.sum(x, axis=-1, keepdims=True)


def _lane_sum_mxu(x):
    g, c, n = x.shape
    x2 = x.reshape(g * c, n)
    head = x2.astype(BF16)
    rest = (x2 - head.astype(F32)).astype(BF16)
    ones = jnp.ones((n, n), BF16)
    return (_dot(head, ones) + _dot(rest, ones)).reshape(g, c, n)


def _post(y, r, k, v, rk, lnw, lnb, lane_sum=_lane_sum):
    mu = lane_sum(y) * (1.0 / HEAD)
    var = lane_sum(jnp.square(y - mu)) * (1.0 / HEAD)
    yn = (y - mu) * lax.rsqrt(var + GN_EPS) * lnw + lnb
    return yn + lane_sum(r * k * rk) * v


def _chunk_consts(r, lw, k, v, kkraw, a, first=False):
    d = _ONE_PASS
    at, rt, btc, ktc, gc, aab, arb, xv, arkv, ain, bin_ = _chunk_core(r, lw, k, v, kkraw, a, d)
    z = d.solve(aab, jnp.concatenate([at, xv], axis=2))
    ryv = jnp.concatenate([rt, arkv], axis=2) + d.mm(arb, z)
    if first:
        ryv = jnp.concatenate([ryv[:, :, :HEAD], _with_early_rows(ryv[:, :, HEAD:], r, lw, k, v, ain, bin_)], axis=2)
    mkv = d.mm_tn(btc, z) + jnp.concatenate([_diag(gc), d.mm_tn(ktc, v)], axis=2)
    return mkv, ryv


def _rwkv_scan(ins, rk, lnw, lnb):
    bsz, s, _ = ins[0].shape
    nch = s // CHUNK

    def consts_body(r_ref, lw_ref, k_ref, v_ref, kk_ref, a_ref, mkv_ref, ry_ref, yv_ref):
        def group(i, carry):
            chunks = [i * CHUNK_GROUP + j for j in range(CHUNK_GROUP)]
            mkv, ryv = _chunk_consts(*[_pairs(ref, chunks) for ref in (r_ref, lw_ref, k_ref, v_ref, kk_ref, a_ref)],
                                     first=isinstance(i, int) and i == 0)
            for j, c in enumerate(chunks):
                for h in range(2):
                    mkv_ref[0, 0, c, h] = mkv[2 * j + h]
                ry_ref[0, _chunk_rows(c), :] = jnp.concatenate([ryv[2 * j][:, :HEAD], ryv[2 * j + 1][:, :HEAD]], axis=1)
                yv_ref[0, _chunk_rows(c), :] = jnp.concatenate([ryv[2 * j][:, HEAD:], ryv[2 * j + 1][:, HEAD:]], axis=1)
            return carry

        group(0, 0)
        lax.fori_loop(1, nch // CHUNK_GROUP, group, 0)

    tile = pl.BlockSpec((1, s, LANE), lambda b, hp: (b, 0, hp))
    vec = pl.BlockSpec((1, LANE), lambda b, hp: (0, hp))
    mkv_spec = pl.BlockSpec((1, 1, nch, 2, HEAD, LANE), lambda b, hp: (b, hp, 0, 0, 0, 0))
    st_spec = pl.BlockSpec((1, 1, nch, 2, HEAD, HEAD), lambda b, hp: (b, hp, 0, 0, 0, 0))
    mkv, ry, yv = pl.pallas_call(
        consts_body, name="rwkv_consts", grid=(bsz, 4), in_specs=[tile] * 6, out_specs=[mkv_spec, tile, tile],
        out_shape=[SDS((bsz, 4, nch, 2, HEAD, LANE), F32), SDS((bsz, s, WIDTH), F32), SDS((bsz, s, WIDTH), F32)],
        compiler_params=_params(("parallel", "parallel")))(*ins)

    states = _chunk_recurrence(mkv, None, "rwkv_states")

    def out_body(ry_ref, yv_ref, r_ref, k_ref, v_ref, st_ref, rk_ref, lnw_ref, lnb_ref, o_ref):
        y, r, k, v, rk_, lnw_, lnb_ = _scan_rows(ry_ref, yv_ref, r_ref, k_ref, v_ref, st_ref, rk_ref, lnw_ref, lnb_ref)
        o = _post(y, r, k, v, rk_, lnw_, lnb_, _lane_sum_mxu)
        for j in range(CHUNK_GROUP):
            o_ref[0, _chunk_rows(j), :] = _unpair(o, j)

    o = pl.pallas_call(
        out_body, name="rwkv_out", grid=(bsz, 4, nch // CHUNK_GROUP), in_specs=_group_specs(5), out_specs=_group_specs(1)[0],
        out_shape=SDS((bsz, s, WIDTH), F32),
        compiler_params=_params(("parallel", "parallel", "parallel")))(ry, yv, ins[0], ins[2], ins[3], states, rk, lnw, lnb)
    return o, states, (mkv, ry, yv)


def _group_specs(n_tiles):
    tile = pl.BlockSpec((1, CHUNK_GROUP * CHUNK, LANE), lambda b, hp, t: (b, t, hp))
    if n_tiles == 1:
        return [tile]
    st = pl.BlockSpec((1, 1, CHUNK_GROUP, 2, HEAD, HEAD), lambda b, hp, t: (b, hp, t, 0, 0, 0))
    vec = pl.BlockSpec((1, LANE), lambda b, hp, t: (0, hp))
    return [tile] * n_tiles + [st] + [vec] * 3


def _scan_rows(ry_ref, yv_ref, r_ref, k_ref, v_ref, st_ref, rk_ref, lnw_ref, lnb_ref):
    chunks = list(range(CHUNK_GROUP))
    ry, yv, r, k, v = (_pairs(ref, chunks) for ref in (ry_ref, yv_ref, r_ref, k_ref, v_ref))
    st = _stack([st_ref[0, 0, c, h] for c in chunks for h in range(2)])
    vecs = [_stack([ref[:, HEAD * h:HEAD * h + HEAD] for _ in chunks for h in range(2)]) for ref in (rk_ref, lnw_ref, lnb_ref)]
    return (_bmm(ry, st) + yv, r, k, v, *vecs)


def _chunk_recurrence(mkv, q, name):
    bsz, _, nch = mkv.shape[:3]
    pairs = [(hp, h) for hp in range(4) for h in range(2)]

    def body(*refs):
        mkv_ref, out_ref, acc = refs[0], refs[-2], refs[-1]
        acc[...] = jnp.zeros_like(acc)

        def step(i, carry):
            c = i if q is None else nch - 1 - i
            cur = acc[...]
            for j, (hp, h) in enumerate(pairs):
                out_ref[0, hp, c, h] = cur[j]
            m = _stack([mkv_ref[0, hp, c, h] for hp, h in pairs])
            if q is None:
                acc[...] = _bmm(m[:, :, :HEAD], cur) + m[:, :, HEAD:]
            else:
                acc[...] = _bmm_tn(m[:, :, :HEAD], cur) + _stack([refs[1][0, hp, c, h] for hp, h in pairs])
            return carry

        lax.fori_loop(0, nch, step, 0)

    spec = lambda w: pl.BlockSpec((1, 4, nch, 2, HEAD, w), lambda b: (b, 0, 0, 0, 0, 0))
    return pl.pallas_call(
        body, name=name, grid=(bsz,), in_specs=[spec(LANE)] + ([] if q is None else [spec(HEAD)]), out_specs=spec(HEAD),
        out_shape=SDS((bsz, 4, nch, 2, HEAD, HEAD), F32), scratch_shapes=[pltpu.VMEM((8, HEAD, HEAD), F32)],
        compiler_params=_params(("parallel",)))(*([mkv] if q is None else [mkv, q]))


def _rwkv_scan_bwd(ins, states, consts, do3, rk, lnw, lnb):
    bsz, s, _ = ins[0].shape
    nch = s // CHUNK

    mkv, ry, yv = consts

    def q_body(do_ref, ry_ref, yv_ref, r_ref, k_ref, v_ref, st_ref, rk_ref, lnw_ref, lnb_ref, q_ref):
        y, r, k, v, rk_, lnw_, lnb_ = _scan_rows(ry_ref, yv_ref, r_ref, k_ref, v_ref, st_ref, rk_ref, lnw_ref, lnb_ref)
        _, vjp = jax.vjp(lambda y_: _post(y_, r, k, v, rk_, lnw_, lnb_), y)
        (dy,) = vjp(_pairs(do_ref, list(range(CHUNK_GROUP))))
        q = _bmm_tn(_pairs(ry_ref, list(range(CHUNK_GROUP))), dy)
        for j in range(CHUNK_GROUP):
            for h in range(2):
                q_ref[0, 0, j, h] = q[2 * j + h]

    specs = _group_specs(6)
    q = pl.pallas_call(
        q_body, name="rwkv_q", grid=(bsz, 4, nch // CHUNK_GROUP), in_specs=specs, out_specs=specs[6],
        out_shape=SDS((bsz, 4, nch, 2, HEAD, HEAD), F32),
        compiler_params=_params(("parallel", "parallel", "parallel")))(do3, ry, yv, ins[0], ins[2], ins[3], states, rk, lnw, lnb)

    dstates = _chunk_recurrence(mkv, q, "rwkv_dstates")

    def body(r_ref, lw_ref, k_ref, v_ref, kk_ref, a_ref, st_ref, dst_ref, do_ref, rk_ref, lnw_ref, lnb_ref,
             dr_ref, dlw_ref, dk_ref, dv_ref, dkk_ref, da_ref, drk_ref, dlnw_ref, dlnb_ref):
        chunks = list(range(BWD_GROUP))
        par_refs = (drk_ref, dlnw_ref, dlnb_ref)

        @pl.when(jnp.logical_and(pl.program_id(1) == 0, pl.program_id(2) == 0))
        def _():
            for ref in par_refs:
                ref[...] = jnp.zeros_like(ref)

        def group(first):
            per_pair = lambda ref: _stack([ref[0, 0, c, h] for c in chunks for h in range(2)])
            vecs = [_stack([ref[:, HEAD * h:HEAD * h + HEAD] for _ in chunks for h in range(2)]) for ref in (rk_ref, lnw_ref, lnb_ref)]
            _, vjp = jax.vjp(functools.partial(_chunk_fn, first=first, d=_ONE_PASS), per_pair(st_ref),
                             *[_pairs(ref, chunks) for ref in (r_ref, lw_ref, k_ref, v_ref, kk_ref, a_ref)], *vecs)
            grads = vjp((_pairs(do_ref, chunks), per_pair(dst_ref)))
            for ref, cot in zip((dr_ref, dlw_ref, dk_ref, dv_ref, dkk_ref, da_ref), grads[1:7]):
                for j, c in enumerate(chunks):
                    ref[0, _chunk_rows(c), :] = _unpair(cot, j)
            for ref, g_ in zip(par_refs, grads[7:10]):
                ref[...] += jnp.concatenate([sum(g_[2 * j + h] for j in range(BWD_GROUP)) for h in range(2)], axis=1)

        pl.when(pl.program_id(2) == 0)(functools.partial(group, True))
        pl.when(pl.program_id(2) != 0)(functools.partial(group, False))

    tt = BWD_GROUP * CHUNK
    tile = pl.BlockSpec((1, tt, LANE), lambda hp, b, t: (b, t, hp))
    vec = pl.BlockSpec((1, LANE), lambda hp, b, t: (0, hp))
    st_spec = pl.BlockSpec((1, 1, BWD_GROUP, 2, HEAD, HEAD), lambda hp, b, t: (b, hp, t, 0, 0, 0))
    outs = pl.pallas_call(
        body, name="rwkv_scan_bwd", grid=(4, bsz, s // tt), in_specs=[tile] * 6 + [st_spec, st_spec, tile] + [vec] * 3,
        out_specs=[tile] * 6 + [vec] * 3,
        out_shape=[SDS((bsz, s, WIDTH), F32)] * 6 + [SDS((1, WIDTH), F32)] * 3,
        compiler_params=_params(("parallel", "arbitrary", "arbitrary")))(*ins, states, dstates, do3, rk, lnw, lnb)
    return outs[:6], outs[6:]


def _head(o_attn, o_rwkv, z_attn, z_rwkv, gm, x2, tgt, wua, wur, wout, g2):
    n = x2.shape[0]
    tm = 256
    nt = n // tm
    d = D_MODEL

    def body(oa_ref, or_ref, za_ref, zr_ref, gm_ref, x_ref, t_ref, wua_ref, wur_ref, wout_ref, g2_ref,
             dxo_ref, doa_ref, dor_ref, dza_ref, dzr_ref, dgm_ref, dwua_ref, dwur_ref, dwout_ref, dg2_ref, loss_ref, lacc):
        i = pl.program_id(0)
        oa, orw, za, zr = oa_ref[...], or_ref[...], za_ref[...], zr_ref[...]
        ga, gb = gm_ref[:, 0:d], gm_ref[:, d:2 * d]
        am = (oa * _silu(za)).astype(BF16)
        bm = (orw * _silu(zr)).astype(BF16)
        ya, yb = _dot(am, wua_ref[...]), _dot(bm, wur_ref[...])
        sa, sb = jax.nn.sigmoid(ga), jax.nn.sigmoid(gb)
        merged = (sa * ya + sb * yb).astype(BF16)
        out = _dot(merged, wout_ref[...])
        rs = lax.rsqrt(jnp.mean(out * out, axis=-1, keepdims=True) + RMS_EPS)
        g2 = g2_ref[...]
        err = x_ref[...] + out * rs * g2 - t_ref[...]
        lpart = jnp.sum(err * err, axis=0, keepdims=True)
        dxo = err * (1.0 / d)
        dxo_ref[...] = dxo
        dg2 = jnp.sum(dxo * out * rs, axis=0, keepdims=True)
        gd = dxo * g2
        dout = (rs * (gd - out * (rs * rs) * jnp.mean(gd * out, axis=-1, keepdims=True))).astype(BF16)
        dmerged = _dot_nt(dout, wout_ref[...])
        dwout = _dot_tn(merged, dout)
        dya, dyb = (dmerged * sa).astype(BF16), (dmerged * sb).astype(BF16)
        dgm_ref[:, 0:d] = (dmerged * ya * sa * (1.0 - sa)).astype(BF16)
        dgm_ref[:, d:2 * d] = (dmerged * yb * sb * (1.0 - sb)).astype(BF16)
        dam, dbm = _dot_nt(dya, wua_ref[...]), _dot_nt(dyb, wur_ref[...])
        dwua, dwur = _dot_tn(am, dya), _dot_tn(bm, dyb)
        doa_ref[...] = dam * _silu(za)
        dza_ref[...] = (dam * oa * _dsilu(za)).astype(BF16)
        dor_ref[...] = dbm * _silu(zr)
        dzr_ref[...] = (dbm * orw * _dsilu(zr)).astype(BF16)

        @pl.when(i == 0)
        def _():
            dwua_ref[...], dwur_ref[...], dwout_ref[...], dg2_ref[...], lacc[...] = dwua, dwur, dwout, dg2, lpart

        @pl.when(i != 0)
        def _():
            dwua_ref[...] += dwua
            dwur_ref[...] += dwur
            dwout_ref[...] += dwout
            dg2_ref[...] += dg2
            lacc[...] += lpart

        @pl.when(i == nt - 1)
        def _():
            loss_ref[...] = jnp.sum(lacc[...], axis=1, keepdims=True) * (0.5 / d)

    t512 = pl.BlockSpec((tm, WIDTH), lambda i: (i, 0))
    t1k = pl.BlockSpec((tm, d), lambda i: (i, 0))
    t2k = pl.BlockSpec((tm, 2 * d), lambda i: (i, 0))
    full = lambda r, c: pl.BlockSpec((r, c), lambda i: (0, 0))
    return pl.pallas_call(
        body, name="head_fwd_bwd", grid=(nt,),
        in_specs=[t512, t512, t512, t512, t2k, t1k, t1k, full(WIDTH, d), full(WIDTH, d), full(d, d), full(1, d)],
        out_specs=[t1k, t512, t512, t512, t512, t2k, full(WIDTH, d), full(WIDTH, d), full(d, d), full(1, d), full(1, 1)],
        out_shape=[SDS((n, d), F32), SDS((n, WIDTH), F32), SDS((n, WIDTH), F32), SDS((n, WIDTH), BF16), SDS((n, WIDTH), BF16),
                   SDS((n, 2 * d), BF16), SDS((WIDTH, d), F32), SDS((WIDTH, d), F32), SDS((d, d), F32), SDS((1, d), F32), SDS((1, 1), F32)],
        scratch_shapes=[pltpu.VMEM((1, d), F32)],
        compiler_params=_params(("arbitrary",)))(o_attn, o_rwkv, z_attn, z_rwkv, gm, x2, tgt, wua, wur, wout, g2)


def _mesh_pos():
    x, y, c = lax.axis_index("x"), lax.axis_index("y"), lax.axis_index("c")
    return 4 * x + 2 * y + c


def _coords(idx):
    return (idx // 4, (idx // 2) % 2, idx % 2)


def _exchange(srcs, to_all, name):
    n = len(srcs)

    def body(*refs):
        src_refs, dst_refs = refs[:n], refs[n:2 * n]
        send_sems, recv_sems, local_sems = refs[2 * n:]
        me = _mesh_pos()

        def piece(i, j):
            return src_refs[i] if to_all[i] else src_refs[i].at[j]

        def remote(i, off, peer, block, slot):
            return pltpu.make_async_remote_copy(src_ref=piece(i, block), dst_ref=dst_refs[i].at[slot],
                                                send_sem=send_sems.at[i, off - 1], recv_sem=recv_sems.at[i, off - 1],
                                                device_id=_coords(peer), device_id_type=MESH)

        local = [pltpu.make_async_copy(piece(i, me), dst_refs[i].at[me], local_sems.at[i]) for i in range(n)]
        for cp in local:
            cp.start()
        sends = []
        for off in range(1, N_DEV):
            to = (me + off) % N_DEV
            for i in range(n):
                sends.append(remote(i, off, to, to, me))
                sends[-1].start()
        for off in range(1, N_DEV):
            frm = (me + N_DEV - off) % N_DEV
            for i in range(n):
                remote(i, off, frm, me, frm).wait_recv()
        for cp in sends:
            cp.wait_send()
        for cp in local:
            cp.wait()

    outs = pl.pallas_call(
        body, name=name, in_specs=[pl.BlockSpec(memory_space=pltpu.HBM)] * n, out_specs=[pl.BlockSpec(memory_space=pltpu.HBM)] * n,
        out_shape=[SDS((N_DEV,) + s.shape[-2:], s.dtype) for s in srcs],
        scratch_shapes=[pltpu.SemaphoreType.DMA((n, N_DEV - 1)), pltpu.SemaphoreType.DMA((n, N_DEV - 1)), pltpu.SemaphoreType.DMA((n,))],
        compiler_params=pltpu.CompilerParams())(*srcs)
    return outs


_HBM = pl.BlockSpec(memory_space=pltpu.HBM)
_SEM = pl.BlockSpec(memory_space=pltpu.SEMAPHORE)
_EFFECT = pltpu.SideEffectType.DATAFLOW_SIDE_EFFECTING


def _send_copy(src_ref, land_ref, to_all, send_sems, recv_sems, i, off, block, slot, peer):
    k = i * (N_DEV - 1) + off - 1
    return pltpu.make_async_remote_copy(src_ref=src_ref if to_all else src_ref.at[block], dst_ref=land_ref.at[slot],
                                        send_sem=send_sems.at[k], recv_sem=recv_sems.at[k],
                                        device_id=_coords(peer), device_id_type=MESH)


def _send_start(srcs, to_all, name):
    n = len(srcs)

    def body(*refs):
        src_refs, land_refs = refs[:n], refs[n:2 * n]
        send_sems, recv_sems = refs[2 * n:2 * n + 2]
        me = _mesh_pos()
        for off in range(1, N_DEV):
            to = (me + off) % N_DEV
            for i in range(n):
                _send_copy(src_refs[i], land_refs[i], to_all, send_sems, recv_sems, i, off, to, me, to).start()
        refs[-1][...] = jnp.zeros_like(refs[-1])

    lands = [jnp.zeros((N_DEV,) + s.shape[-2:], s.dtype) for s in srcs]
    hbm = [pltpu.HBM(a.shape, a.dtype) for a in list(srcs) + lands]
    sems = pltpu.SemaphoreType.DMA((n * (N_DEV - 1),))
    outs = pl.pallas_call(
        body, name=name, out_shape=(sems, sems, *hbm, SDS((8, LANE), BF16)),
        in_specs=(_HBM,) * (2 * n), out_specs=(_SEM, _SEM) + (_HBM,) * (2 * n) + (pl.BlockSpec(memory_space=pltpu.VMEM),),
        input_output_aliases={i: 2 + i for i in range(2 * n)}, compiler_params=pltpu.CompilerParams(has_side_effects=_EFFECT),
    )(*[pltpu.with_memory_space_constraint(a, pltpu.HBM) for a in list(srcs) + lands])
    return outs[0], outs[1], outs[2:2 + n], outs[2 + n:2 + 2 * n], outs[-1]


def _send_wait(send_sems, recv_sems, srcs_thru, lands_thru, to_all, after, name):
    n = len(srcs_thru)

    def body(*refs):
        src_refs, land_refs = refs[:n], refs[n:2 * n]
        send_sems, recv_sems = refs[2 * n:2 * n + 2]
        me = _mesh_pos()
        for off in range(1, N_DEV):
            to, frm = (me + off) % N_DEV, (me + N_DEV - off) % N_DEV
            for i in range(n):
                _send_copy(src_refs[i], land_refs[i], to_all, send_sems, recv_sems, i, off, to, me, to).wait_send()
                _send_copy(src_refs[i], land_refs[i], to_all, send_sems, recv_sems, i, off, me, frm, frm).wait_recv()

    hbm = tuple(pltpu.HBM(a.shape, a.dtype) for a in list(srcs_thru) + list(lands_thru))
    outs = pl.pallas_call(
        body, name=name, out_shape=hbm, in_specs=(_HBM,) * (2 * n) + (_SEM, _SEM, pl.BlockSpec(memory_space=pl.ANY)),
        out_specs=(_HBM,) * (2 * n), input_output_aliases={i: i for i in range(2 * n)},
        compiler_params=pltpu.CompilerParams(has_side_effects=_EFFECT),
    )(*srcs_thru, *lands_thru, send_sems, recv_sems, after)
    return outs[n:]


def _gather(srcs, after, name):
    n = len(srcs)

    def body(*refs):
        src_refs, dst_refs = refs[:n], refs[n + 1:2 * n + 1]
        send_sems, recv_sems, local_sems = refs[2 * n + 1:]
        x, y, c = lax.axis_index("x"), lax.axis_index("y"), lax.axis_index("c")
        me, sibling = (x, y, c), (x, y, 1 - c)
        chips = [(1 - x, y), (x, 1 - y), (1 - x, 1 - y)]

        def slot(i, dev):
            return dst_refs[i].at[4 * dev[0] + 2 * dev[1] + dev[2]]

        def copy(i, k, block, to, own=False):
            return pltpu.make_async_remote_copy(src_ref=src_refs[i] if own else slot(i, block), dst_ref=slot(i, block),
                                                send_sem=send_sems.at[i, k], recv_sem=recv_sems.at[i, k],
                                                device_id=to, device_id_type=MESH)

        local = [pltpu.make_async_copy(src_refs[i], slot(i, me), local_sems.at[i]) for i in range(n)]
        for cp in local:
            cp.start()
        sends = []
        for i in range(n):
            sends.append(copy(i, 0, me, sibling, own=True))
            sends += [copy(i, 1 + j, me, (*chip, c), own=True) for j, chip in enumerate(chips)]
        for cp in sends:
            cp.start()
        for j, chip in enumerate(chips):
            for i in range(n):
                copy(i, 1 + j, (*chip, c), me).wait_recv()
                sends.append(copy(i, 4 + j, (*chip, c), sibling))
                sends[-1].start()
        for i in range(n):
            copy(i, 0, sibling, me).wait_recv()
            for j, chip in enumerate(chips):
                copy(i, 4 + j, (*chip, 1 - c), me).wait_recv()
        for cp in sends:
            cp.wait_send()
        for cp in local:
            cp.wait()

    return pl.pallas_call(
        body, name=name, in_specs=[pl.BlockSpec(memory_space=pltpu.HBM)] * n + [pl.BlockSpec(memory_space=pl.ANY)],
        out_specs=[pl.BlockSpec(memory_space=pltpu.HBM)] * n, out_shape=[SDS((N_DEV,) + s.shape, s.dtype) for s in srcs],
        scratch_shapes=[pltpu.SemaphoreType.DMA((n, N_DEV - 1)), pltpu.SemaphoreType.DMA((n, N_DEV - 1)), pltpu.SemaphoreType.DMA((n,))],
        compiler_params=pltpu.CompilerParams())(*srcs, after)


def _adamw(parts, w, m, v, tr, name, own=None):
    rows, cols = w.shape
    c1, c2 = 1.0 - ADAM_B1 ** ADAM_STEP, 1.0 - ADAM_B2 ** ADAM_STEP

    def body(p_ref, *refs):
        w_ref, m_ref, v_ref, g_ref, d_ref, nm_ref, nv_ref = refs[-7:]
        me = _mesh_pos()

        def part(j):
            return p_ref[j] if own is None else jnp.where(me == j, refs[0][...], p_ref[j])

        g = part(0).astype(F32)
        for j in range(1, N_DEV):
            g = g + part(j).astype(F32)
        nm = ADAM_B1 * m_ref[...] + (1.0 - ADAM_B1) * g
        nv = ADAM_B2 * v_ref[...] + (1.0 - ADAM_B2) * jnp.square(g)
        g_ref[...] = g
        nm_ref[...] = nm
        nv_ref[...] = nv
        d_ref[...] = -ADAM_LR * ((nm / c1) / (jnp.sqrt(nv / c2) + ADAM_EPS) + ADAM_WD * w_ref[...])

    t = pl.BlockSpec((tr, cols), lambda i: (i, 0))
    extra = [] if own is None else [own]
    return pl.pallas_call(
        body, name=name, grid=(rows // tr,), in_specs=[pl.BlockSpec((N_DEV, tr, cols), lambda i: (0, i, 0))] + [t] * (3 + len(extra)),
        out_specs=[t] * 4, out_shape=[SDS((rows, cols), F32)] * 4, compiler_params=_params(("parallel",)))(parts, *extra, w, m, v)


SHARDED = (("w_in", D_MODEL, IN_COLS // N_DEV, True, 128), ("w_up_attn", WIDTH, D_MODEL // N_DEV, True, WIDTH),
           ("w_up_rwkv", WIDTH, D_MODEL // N_DEV, True, WIDTH), ("w_out", D_MODEL // N_DEV, D_MODEL, False, D_MODEL // N_DEV),
           ("rwkv_w_up", LORA, WIDTH // N_DEV, True, LORA), ("rwkv_a_up", LORA, WIDTH // N_DEV, True, LORA))
LOSS_SLOT = sum(n for _, n in SMALL)


def _pack_small(small, extra=None):
    flat = [small[n].reshape(-1).astype(F32) for n, _ in SMALL]
    flat.append(jnp.zeros((1,), F32) if extra is None else extra.reshape(1))
    flat.append(jnp.zeros((SMALL_ROWS * LANE - LOSS_SLOT - 1,), F32))
    return jnp.concatenate(flat).reshape(SMALL_ROWS, LANE)


def _unpack_small(packed, shapes):
    flat = packed.reshape(-1)
    out, off = {}, 0
    for n, cnt in SMALL:
        out[n] = flat[off:off + cnt].reshape(shapes[n])
        off += cnt
    return out, flat[LOSS_SLOT]


def _whole(gathered, by_cols):
    if not by_cols:
        return gathered.reshape(-1, gathered.shape[-1])
    return gathered.transpose(1, 0, 2).reshape(gathered.shape[1], -1)


def _per_owner(full, by_cols):
    if not by_cols:
        return full.reshape(N_DEV, -1, full.shape[-1])
    return full.reshape(full.shape[0], N_DEV, -1).transpose(1, 0, 2)


def _local_step(x, loss_target, sm, wts):
    bsz, s, d = x.shape
    n = bsz * s
    x2, tgt = x.reshape(n, d), loss_target.reshape(n, d)
    bidx = jnp.asarray(_bucket_tables())
    w_in = wts["w_in"]
    segs = (("qkv", 0, QKV_COLS, 1536), ("za", OFF_ZA, WIDTH, 512), ("pr", OFF_PR, PR_COLS, PR_COLS), ("zr", OFF_ZR, WIDTH, 512),
            ("gm", OFF_GM, 2 * D_MODEL, 1024))

    h, rs = _prenorm(x2, sm["pre_norm_gain"])
    w_seg = {nm: w_in[:, off:off + cnt] for nm, off, cnt, _ in segs}
    proj = {nm: _mm(h, w_seg[nm], tn, "proj_" + nm) for nm, _, _, tn in segs}
    qkv3 = proj["qkv"].reshape(bsz, s, QKV_COLS)
    pr3 = proj["pr"].reshape(bsz, s, PR_COLS)

    o_attn, lse = _attn_fwd(qkv3, sm["rel_bias"], bidx)
    rk = sm["rwkv_r_k"].reshape(1, WIDTH)
    pre_args = (sm["rwkv_shift_mix"], sm["rwkv_w0"], wts["rwkv_w_up"], sm["rwkv_a0"], wts["rwkv_a_up"], sm["rwkv_k_k"], sm["rwkv_k_a"])
    scan_in = _rwkv_pre(pr3, *pre_args)
    o_rwkv, states, consts = _rwkv_scan(scan_in, rk, sm["rwkv_ln_w"], sm["rwkv_ln_b"])

    (dxo, do_attn, do_rwkv, dza, dzr, dgm, g_wua, g_wur, g_wout, g_post, loss) = _head(
        o_attn.reshape(n, WIDTH), o_rwkv.reshape(n, WIDTH), proj["za"], proj["zr"], proj["gm"], x2, tgt,
        wts["w_up_attn"], wts["w_up_rwkv"], wts["w_out"], sm["post_norm_gain"])

    dqkv, dbias = _attn_bwd(qkv3, o_attn, lse, do_attn.reshape(bsz, s, WIDTH), sm["rel_bias"], bidx)
    g_bias = _bias_grad(dbias, bidx)[:, :N_BUCKET].T

    scan_cots, (g_rk, g_lnw, g_lnb) = _rwkv_scan_bwd(scan_in, states, consts, do_rwkv.reshape(bsz, s, WIDTH), rk, sm["rwkv_ln_w"],
                                                     sm["rwkv_ln_b"])
    dprs, g_mix, g_w0, g_wup, g_a0, g_aup, g_kk, g_ka = _rwkv_pre_bwd(pr3, scan_cots, *pre_args)
    dpr = _shift_bwd(dprs, sm["rwkv_shift_mix"]).reshape(n, PR_COLS)

    dsegs = [(dqkv.reshape(9, n, WIDTH), 0, QKV_COLS, WIDTH), (dza, OFF_ZA, WIDTH, WIDTH), (dpr, OFF_PR, PR_COLS, PR_COLS),
             (dzr, OFF_ZR, WIDTH, WIDTH), (dgm, OFF_GM, 2 * D_MODEL, D_MODEL)]
    full = {"w_in": jnp.concatenate([_mm_tn(h, t, tn, "gw_in_%d" % j) for j, (t, _, _, tn) in enumerate(dsegs)], axis=1),
            "w_up_attn": g_wua, "w_up_rwkv": g_wur, "w_out": g_wout, "rwkv_w_up": g_wup, "rwkv_a_up": g_aup}
    blocks = [_per_owner(full[nm], by_cols).astype(BF16) for nm, _, _, by_cols, _ in SHARDED]
    me = 4 * lax.axis_index("x") + 2 * lax.axis_index("y") + lax.axis_index("c")
    own = [lax.dynamic_index_in_dim(b, me, 0, keepdims=False) for b in blocks]
    send_sems, recv_sems, blocks_thru, lands_thru, token = _send_start(blocks, False, "grads_start")
    dh = _mm_nt(dsegs[0][0], w_seg["qkv"], token, "dh_qkv")
    grad_x, g_pre = _dh_rest_prenorm_bwd([t for t, *_ in dsegs[1:]], [w_seg[nm] for nm in ("za", "pr", "zr", "gm")], dh, x2, rs,
                                         sm["pre_norm_gain"], dxo)
    landed = _send_wait(send_sems, recv_sems, blocks_thru, lands_thru, False, g_pre, "grads_wait")

    small = {"pre_norm_gain": g_pre, "rel_bias": g_bias, "rwkv_shift_mix": g_mix, "rwkv_w0": g_w0, "rwkv_a0": g_a0, "rwkv_k_k": g_kk,
             "rwkv_k_a": g_ka, "rwkv_r_k": g_rk, "rwkv_ln_w": g_lnw, "rwkv_ln_b": g_lnb, "post_norm_gain": g_post}
    return loss[0, 0], grad_x.reshape(bsz, s, d), (landed, own), small


def kernel(x, pre_norm_gain, w_in, rel_bias, rwkv_shift_mix, rwkv_w0, rwkv_w_up, rwkv_a0, rwkv_a_up, rwkv_k_k, rwkv_k_a, rwkv_r_k, rwkv_ln_w, rwkv_ln_b, w_up_attn, w_up_rwkv, w_out, post_norm_gain, loss_target, m_pre_norm_gain, m_w_in, m_rel_bias, m_rwkv_shift_mix, m_rwkv_w0, m_rwkv_w_up, m_rwkv_a0, m_rwkv_a_up, m_rwkv_k_k, m_rwkv_k_a, m_rwkv_r_k, m_rwkv_ln_w, m_rwkv_ln_b, m_w_up_attn, m_w_up_rwkv, m_w_out, m_post_norm_gain, v_pre_norm_gain, v_w_in, v_rel_bias, v_rwkv_shift_mix, v_rwkv_w0, v_rwkv_w_up, v_rwkv_a0, v_rwkv_a_up, v_rwkv_k_k, v_rwkv_k_a, v_rwkv_r_k, v_rwkv_ln_w, v_rwkv_ln_b, v_w_up_attn, v_w_up_rwkv, v_w_out, v_post_norm_gain):
    names = [n for n, *_ in SHARDED] + [n for n, _ in SMALL]
    loc = dict(locals())
    w = {n: loc[n] for n in names}
    m = {n: loc["m_" + n] for n in names}
    v = {n: loc["v_" + n] for n in names}
    shapes = {n: w[n].shape for n in names}
    order = ["pre_norm_gain", "w_in", "rel_bias", "rwkv_shift_mix", "rwkv_w0", "rwkv_w_up", "rwkv_a0", "rwkv_a_up", "rwkv_k_k", "rwkv_k_a",
             "rwkv_r_k", "rwkv_ln_w", "rwkv_ln_b", "w_up_attn", "w_up_rwkv", "w_out", "post_norm_gain"]
    shard2d = lambda t, n, r, c: t[n].reshape(r, c)

    shards = [shard2d(w, n, r, c).astype(BF16) for n, r, c, _, _ in SHARDED]
    send_sems, recv_sems, srcs_thru, lands_thru, token = _send_start(shards[1:], True, "weights_start")
    gathered = list(_gather(shards[:1], token, "gather_weights"))
    landed = _send_wait(send_sems, recv_sems, srcs_thru, lands_thru, True, gathered[0], "weights_wait")
    me = 4 * lax.axis_index("x") + 2 * lax.axis_index("y") + lax.axis_index("c")
    gathered += [lax.dynamic_update_index_in_dim(g, sh, me, 0) for g, sh in zip(landed, shards[1:])]
    wts = {n: _whole(g, by_cols) for (n, _, _, by_cols, _), g in zip(SHARDED, gathered)}

    loss, grad_x, (landed, own), small = _local_step(x, loss_target, w, wts)
    (small_parts,) = _exchange([_pack_small(small, loss)], [True], "exchange_small")

    outs = [{}, {}, {}, {}]
    for (n, r, c, _, tr), p, o_ in zip(SHARDED, landed, own):
        res = _adamw(p, shard2d(w, n, r, c), shard2d(m, n, r, c), shard2d(v, n, r, c), tr, "adamw_" + n, own=o_)
        for o, t in zip(outs, res):
            o[n] = t.reshape(shapes[n])
    res = _adamw(small_parts, _pack_small(w), _pack_small(m), _pack_small(v), SMALL_ROWS, "adamw_small")
    for o, t in zip(outs, res):
        o.update(_unpack_small(t, shapes)[0])
    loss = _unpack_small(res[0], shapes)[1]
    return (loss, grad_x, *[o[n] for o in outs for n in order])
```
